```python
import jax
import jax.numpy as jnp
from jax import lax
import numpy as np

D_MODEL = 1024
BATCH = 8
SEQ = 4096
DEPTH = 1

ATTN_HEADS = 8
ATTN_HEAD_DIM = 64
ATTN_WIDTH = ATTN_HEADS * ATTN_HEAD_DIM
DILATED_PATTERNS = ((128, 1), (512, 4), (2048, 16))
ATTN_BLOCK = 128
DN_HEADS = 4
DN_HEAD_DIM = 128
DN_WIDTH = DN_HEADS * DN_HEAD_DIM
DN_CHUNK = 64
CONV_WIDTH = 4
MIX_WIDTH = ATTN_WIDTH + DN_WIDTH
IN_SECTIONS = (ATTN_WIDTH, ATTN_WIDTH, ATTN_WIDTH, DN_WIDTH, DN_WIDTH, DN_WIDTH, DN_HEADS, DN_HEADS, DN_WIDTH)
IN_COLS = sum(IN_SECTIONS)
D_FF = 2816
NORM_EPS = 1e-6
L2_EPS = 1e-6
INIT_NOISE = 0.02

kernel_name = "hybrid_dilated_attn_gated_deltanet_macaron"


def rms_norm(x, gain):
    xf = x.astype(jnp.float32)
    y = xf * lax.rsqrt(jnp.mean(xf * xf, axis=-1, keepdims=True) + NORM_EPS)
    return (y * gain.astype(jnp.float32)).astype(x.dtype)


def swiglu(h, w_gate, w_up, w_down):
    return (jax.nn.silu(h @ w_gate) * (h @ w_up)) @ w_down


def alibi_slopes(n_heads):
    return jnp.asarray(np.array([2.0 ** (-8.0 * (i + 1) / n_heads) for i in range(n_heads)], dtype=np.float32))


def dilated_window_attention(q, k, v, slopes, window, dilation):
    B, S, H, Dh = q.shape
    L = S // dilation
    W = window // dilation
    nb = -(-L // ATTN_BLOCK)
    Lp = nb * ATTN_BLOCK

    def residues(t):
        return t.reshape(B, L, dilation, H, Dh).transpose(0, 2, 3, 1, 4).reshape(B * dilation, H, L, Dh)

    qr = jnp.pad(residues(q), ((0, 0), (0, 0), (0, Lp - L), (0, 0))).reshape(B * dilation, H, nb, ATTN_BLOCK, Dh)

    def band(t):
        t = jnp.pad(residues(t), ((0, 0), (0, 0), (ATTN_BLOCK, Lp - L), (0, 0)))
        t = t.reshape(B * dilation, H, nb + 1, ATTN_BLOCK, Dh)
        return jnp.concatenate([t[:, :, :-1], t[:, :, 1:]], axis=3)

    kb, vb = band(k), band(v)
    s = jnp.einsum('zhnqd,zhnkd->zhnqk', qr, kb) * (Dh ** -0.5)
    qi = jnp.arange(ATTN_BLOCK)[:, None]
    kj = jnp.arange(2 * ATTN_BLOCK)[None, :]
    steps = qi + ATTN_BLOCK - kj
    key_idx = jnp.arange(nb)[:, None, None] * ATTN_BLOCK + kj - ATTN_BLOCK
    valid = (steps >= 0) & (steps <= W) & (key_idx >= 0)
    bias = -slopes[:, None, None, None] * (steps * dilation).astype(jnp.float32)
    s = jnp.where(valid, s + bias, -jnp.inf)
    m = jnp.max(s, axis=-1)
    p = jnp.exp(s - m[..., None])
    l = jnp.sum(p, axis=-1)
    o = jnp.einsum('zhnqk,zhnkd->zhnqd', p, vb) / l[..., None]

    def back(t):
        extra = t.shape[4:]
        t = t.reshape(B, dilation, H, Lp, *extra)[:, :, :, :L]
        perm = (0, 3, 1, 2) + tuple(range(4, 4 + len(extra)))
        return t.transpose(perm).reshape(B, S, H, *extra)

    return back(o), back(m), back(l)


def dilated_attention(q, k, v):
    slopes = alibi_slopes(q.shape[2])
    outs, maxes, denoms = [], [], []
    for window, dilation in DILATED_PATTERNS:
        o, m, l = dilated_window_attention(q, k, v, slopes, window, dilation)
        outs.append(o)
        maxes.append(m)
        denoms.append(l)
    m_all = jnp.stack(maxes)
    wts = jnp.stack(denoms) * jnp.exp(m_all - jnp.max(m_all, axis=0, keepdims=True))
    return jnp.einsum('pbsh,pbshd->bshd', wts, jnp.stack(outs)) / jnp.sum(wts, axis=0)[..., None]


def gated_delta_rule(q, k, v, beta, g):
    B, S, H, Dk = q.shape
    Dv = v.shape[-1]
    C = DN_CHUNK
    N = S // C

    def chunks(t):
        return jnp.moveaxis(t.reshape(B, N, C, H, *t.shape[3:]), 3, 1)

    q = chunks(q) * (Dk ** -0.5)
    k, v, beta = chunks(k), chunks(v), chunks(beta)
    g = jnp.cumsum(chunks(g), axis=-1)
    causal = jnp.tril(jnp.ones((C, C), dtype=bool))
    strict = jnp.tril(jnp.ones((C, C), dtype=bool), k=-1)
    decay = jnp.exp(jnp.where(causal, g[..., :, None] - g[..., None, :], -jnp.inf))
    k_beta = k * beta[..., None]
    a_mat = jnp.where(strict, jnp.einsum('bhnck,bhnjk->bhncj', k_beta, k) * decay, 0.0)
    eye = jnp.eye(C, dtype=q.dtype)
    rhs = jnp.concatenate([v * beta[..., None], k_beta * jnp.exp(g)[..., None]], axis=-1)
    sol = lax.linalg.triangular_solve(eye + a_mat, rhs, left_side=True, lower=True)
    u, w = sol[..., :Dv], sol[..., Dv:]
    attn_intra = jnp.where(causal, jnp.einsum('bhnck,bhnjk->bhncj', q, k) * decay, 0.0)

    def step(state, xs):
        q_n, k_n, u_n, w_n, g_n, a_n = xs
        v_new = u_n - jnp.einsum('bhck,bhkv->bhcv', w_n, state)
        o_n = (jnp.einsum('bhck,bhkv->bhcv', q_n * jnp.exp(g_n)[..., None], state)
               + jnp.einsum('bhcj,bhjv->bhcv', a_n, v_new))
        g_last = g_n[..., -1]
        k_dec = k_n * jnp.exp(g_last[..., None] - g_n)[..., None]
        state = state * jnp.exp(g_last)[..., None, None] + jnp.einsum('bhck,bhcv->bhkv', k_dec, v_new)
        return state, o_n

    xs = tuple(jnp.moveaxis(t, 2, 0) for t in (q, k, u, w, g, attn_intra))
    state0 = jnp.zeros((B, H, Dk, Dv), q.dtype)
    _, o = lax.scan(step, state0, xs)
    return o.transpose(1, 0, 3, 2, 4).reshape(B, S, H, Dv)


def hybrid_mixer(h, w_in, conv_w, a_log, dt_bias, dn_norm, w_out):
    B, S, _ = h.shape
    f32 = jnp.float32
    split_points = np.cumsum(IN_SECTIONS)[:-1].tolist()
    aq, ak, av, dq, dk, dv, beta_raw, decay_raw, gate = jnp.split(h @ w_in, split_points, axis=-1)

    def attn_heads(t):
        return t.reshape(B, S, ATTN_HEADS, ATTN_HEAD_DIM).astype(f32)

    attn = dilated_attention(attn_heads(aq), attn_heads(ak), attn_heads(av))
    attn = attn.reshape(B, S, ATTN_WIDTH).astype(h.dtype)

    qkv = jnp.concatenate([dq, dk, dv], axis=-1)
    qkv_pad = jnp.pad(qkv, ((0, 0), (CONV_WIDTH - 1, 0), (0, 0)))
    conv = qkv_pad[:, 0:S] * conv_w[0]
    for j in range(1, CONV_WIDTH):
        conv = conv + qkv_pad[:, j:j + S] * conv_w[j]
    dq, dk, dv = jnp.split(jax.nn.silu(conv).astype(f32), 3, axis=-1)

    def dn_heads(t):
        return t.reshape(B, S, DN_HEADS, DN_HEAD_DIM)

    def l2n(t):
        return t * lax.rsqrt(jnp.sum(t * t, axis=-1, keepdims=True) + L2_EPS)

    beta = jax.nn.sigmoid(beta_raw.astype(f32))
    g = -jnp.exp(a_log.astype(f32)) * jax.nn.softplus(decay_raw.astype(f32) + dt_bias.astype(f32))
    o = gated_delta_rule(l2n(dn_heads(dq)), l2n(dn_heads(dk)), dn_heads(dv), beta, g)
    o = (o * lax.rsqrt(jnp.mean(o * o, axis=-1, keepdims=True) + NORM_EPS) * dn_norm.astype(f32)
         * jax.nn.silu(dn_heads(gate.astype(f32))))
    dn = o.reshape(B, S, DN_WIDTH).astype(h.dtype)

    return jnp.concatenate([attn, dn], axis=-1) @ w_out


def _fwd_setup_inputs(seed: int = 0) -> dict:
    key = jax.random.key(seed)
    ks = jax.random.split(key, 17)
    f32 = jnp.float32
    L = DEPTH

    def normal(k, shape, scale):
        return jax.random.normal(k, shape, f32) * scale

    def gain(k, shape):
        return 1.0 + INIT_NOISE * jax.random.normal(k, shape, f32)

    dt = jnp.exp(jax.random.uniform(ks[9], (L, DN_HEADS), f32, float(np.log(1e-3)), float(np.log(1e-1))))
    return {
        "x": normal(ks[0], (BATCH, SEQ, D_MODEL), 1.0),
        "norm_ffn1": gain(ks[1], (L, D_MODEL)),
        "ffn1_gate": normal(ks[2], (L, D_MODEL, D_FF), D_MODEL ** -0.5),
        "ffn1_up": normal(ks[3], (L, D_MODEL, D_FF), D_MODEL ** -0.5),
        "ffn1_down": normal(ks[4], (L, D_FF, D_MODEL), D_FF ** -0.5),
        "norm_mix": gain(ks[5], (L, D_MODEL)),
        "w_in": normal(ks[6], (L, D_MODEL, IN_COLS), D_MODEL ** -0.5),
        "conv_w": normal(ks[7], (L, CONV_WIDTH, 3 * DN_WIDTH), CONV_WIDTH ** -0.5),
        "a_log": jnp.log(jax.random.uniform(ks[8], (L, DN_HEADS), f32, 1.0, 16.0)),
        "dt_bias": dt + jnp.log(-jnp.expm1(-dt)),
        "dn_norm": gain(ks[10], (L, DN_HEAD_DIM)),
        "w_out": normal(ks[11], (L, MIX_WIDTH, D_MODEL), MIX_WIDTH ** -0.5),
        "norm_ffn2": gain(ks[12], (L, D_MODEL)),
        "ffn2_gate": normal(ks[13], (L, D_MODEL, D_FF), D_MODEL ** -0.5),
        "ffn2_up": normal(ks[14], (L, D_MODEL, D_FF), D_MODEL ** -0.5),
        "ffn2_down": normal(ks[15], (L, D_FF, D_MODEL), D_FF ** -0.5),
        "norm_final": gain(ks[16], (D_MODEL,)),
    }


def _fwd_reference(x, norm_ffn1, ffn1_gate, ffn1_up, ffn1_down, norm_mix, w_in, conv_w, a_log, dt_bias,
              dn_norm, w_out, norm_ffn2, ffn2_gate, ffn2_up, ffn2_down, norm_final):
    for i in range(DEPTH):
        x = x + 0.5 * swiglu(rms_norm(x, norm_ffn1[i]), ffn1_gate[i], ffn1_up[i], ffn1_down[i])
        x = x + hybrid_mixer(rms_norm(x, norm_mix[i]), w_in[i], conv_w[i], a_log[i], dt_bias[i],
                             dn_norm[i], w_out[i])
        x = x + 0.5 * swiglu(rms_norm(x, norm_ffn2[i]), ffn2_gate[i], ffn2_up[i], ffn2_down[i])
    return rms_norm(x, norm_final)


import jax as _jax
import jax.numpy as _jnp

TWIN_FORMAT = 'train_step'
FWD_PARAMS = ['x', 'norm_ffn1', 'ffn1_gate', 'ffn1_up', 'ffn1_down', 'norm_mix', 'w_in', 'conv_w', 'a_log', 'dt_bias', 'dn_norm', 'w_out', 'norm_ffn2', 'ffn2_gate', 'ffn2_up', 'ffn2_down', 'norm_final']
TWIN_WEIGHTS = ['norm_ffn1', 'ffn1_gate', 'ffn1_up', 'ffn1_down', 'norm_mix', 'w_in', 'conv_w', 'a_log', 'dt_bias', 'dn_norm', 'w_out', 'norm_ffn2', 'ffn2_gate', 'ffn2_up', 'ffn2_down', 'norm_final']
TWIN_DIFF_INPUT = 'x'
TWIN_INPUTS = ['x', 'norm_ffn1', 'ffn1_gate', 'ffn1_up', 'ffn1_down', 'norm_mix', 'w_in', 'conv_w', 'a_log', 'dt_bias', 'dn_norm', 'w_out', 'norm_ffn2', 'ffn2_gate', 'ffn2_up', 'ffn2_down', 'norm_final', 'loss_target', 'm_norm_ffn1', 'm_ffn1_gate', 'm_ffn1_up', 'm_ffn1_down', 'm_norm_mix', 'm_w_in', 'm_conv_w', 'm_a_log', 'm_dt_bias', 'm_dn_norm', 'm_w_out', 'm_norm_ffn2', 'm_ffn2_gate', 'm_ffn2_up', 'm_ffn2_down', 'm_norm_final', 'v_norm_ffn1', 'v_ffn1_gate', 'v_ffn1_up', 'v_ffn1_down', 'v_norm_mix', 'v_w_in', 'v_conv_w', 'v_a_log', 'v_dt_bias', 'v_dn_norm', 'v_w_out', 'v_norm_ffn2', 'v_ffn2_gate', 'v_ffn2_up', 'v_ffn2_down', 'v_norm_final']
TWIN_OUTPUTS = ['loss', 'grad_x', 'grad_norm_ffn1', 'grad_ffn1_gate', 'grad_ffn1_up', 'grad_ffn1_down', 'grad_norm_mix', 'grad_w_in', 'grad_conv_w', 'grad_a_log', 'grad_dt_bias', 'grad_dn_norm', 'grad_w_out', 'grad_norm_ffn2', 'grad_ffn2_gate', 'grad_ffn2_up', 'grad_ffn2_down', 'grad_norm_final', 'delta_norm_ffn1', 'delta_ffn1_gate', 'delta_ffn1_up', 'delta_ffn1_down', 'delta_norm_mix', 'delta_w_in', 'delta_conv_w', 'delta_a_log', 'delta_dt_bias', 'delta_dn_norm', 'delta_w_out', 'delta_norm_ffn2', 'delta_ffn2_gate', 'delta_ffn2_up', 'delta_ffn2_down', 'delta_norm_final', 'new_m_norm_ffn1', 'new_m_ffn1_gate', 'new_m_ffn1_up', 'new_m_ffn1_down', 'new_m_norm_mix', 'new_m_w_in', 'new_m_conv_w', 'new_m_a_log', 'new_m_dt_bias', 'new_m_dn_norm', 'new_m_w_out', 'new_m_norm_ffn2', 'new_m_ffn2_gate', 'new_m_ffn2_up', 'new_m_ffn2_down', 'new_m_norm_final', 'new_v_norm_ffn1', 'new_v_ffn1_gate', 'new_v_ffn1_up', 'new_v_ffn1_down', 'new_v_norm_mix', 'new_v_w_in', 'new_v_conv_w', 'new_v_a_log', 'new_v_dt_bias', 'new_v_dn_norm', 'new_v_w_out', 'new_v_norm_ffn2', 'new_v_ffn2_gate', 'new_v_ffn2_up', 'new_v_ffn2_down', 'new_v_norm_final']
TWIN_LEAF_KINDS = {'loss': 'loss', 'grad_x': 'grad_x', 'grad_norm_ffn1': 'grad_w', 'grad_ffn1_gate': 'grad_w', 'grad_ffn1_up': 'grad_w', 'grad_ffn1_down': 'grad_w', 'grad_norm_mix': 'grad_w', 'grad_w_in': 'grad_w', 'grad_conv_w': 'grad_w', 'grad_a_log': 'grad_w', 'grad_dt_bias': 'grad_w', 'grad_dn_norm': 'grad_w', 'grad_w_out': 'grad_w', 'grad_norm_ffn2': 'grad_w', 'grad_ffn2_gate': 'grad_w', 'grad_ffn2_up': 'grad_w', 'grad_ffn2_down': 'grad_w', 'grad_norm_final': 'grad_w', 'delta_norm_ffn1': 'delta_w', 'delta_ffn1_gate': 'delta_w', 'delta_ffn1_up': 'delta_w', 'delta_ffn1_down': 'delta_w', 'delta_norm_mix': 'delta_w', 'delta_w_in': 'delta_w', 'delta_conv_w': 'delta_w', 'delta_a_log': 'delta_w', 'delta_dt_bias': 'delta_w', 'delta_dn_norm': 'delta_w', 'delta_w_out': 'delta_w', 'delta_norm_ffn2': 'delta_w', 'delta_ffn2_gate': 'delta_w', 'delta_ffn2_up': 'delta_w', 'delta_ffn2_down': 'delta_w', 'delta_norm_final': 'delta_w', 'new_m_norm_ffn1': 'new_m', 'new_m_ffn1_gate': 'new_m', 'new_m_ffn1_up': 'new_m', 'new_m_ffn1_down': 'new_m', 'new_m_norm_mix': 'new_m', 'new_m_w_in': 'new_m', 'new_m_conv_w': 'new_m', 'new_m_a_log': 'new_m', 'new_m_dt_bias': 'new_m', 'new_m_dn_norm': 'new_m', 'new_m_w_out': 'new_m', 'new_m_norm_ffn2': 'new_m', 'new_m_ffn2_gate': 'new_m', 'new_m_ffn2_up': 'new_m', 'new_m_ffn2_down': 'new_m', 'new_m_norm_final': 'new_m', 'new_v_norm_ffn1': 'new_v', 'new_v_ffn1_gate': 'new_v', 'new_v_ffn1_up': 'new_v', 'new_v_ffn1_down': 'new_v', 'new_v_norm_mix': 'new_v', 'new_v_w_in': 'new_v', 'new_v_conv_w': 'new_v', 'new_v_a_log': 'new_v', 'new_v_dt_bias': 'new_v', 'new_v_dn_norm': 'new_v', 'new_v_w_out': 'new_v', 'new_v_norm_ffn2': 'new_v', 'new_v_ffn2_gate': 'new_v', 'new_v_ffn2_up': 'new_v', 'new_v_ffn2_down': 'new_v', 'new_v_norm_final': 'new_v'}


def _forward(args):
    return _fwd_reference(*[args[k] for k in FWD_PARAMS])


def _output_shape():
    def fwd():
        inp = _fwd_setup_inputs(0)
        return _fwd_reference(*[inp[k] for k in FWD_PARAMS])
    out = _jax.eval_shape(fwd)
    return out.shape, out.dtype

N_MICROBATCH = 1
ADAM_LR = 0.001
ADAM_B1 = 0.9
ADAM_B2 = 0.999
ADAM_EPS = 1e-08
ADAM_WD = 0.01
ADAM_STEP = 10
PER_EXAMPLE_BATCH_AXIS = {'x': 0, 'loss_target': 0}
SHARED_INPUTS = []
_WEIGHT_DTYPES = {'norm_ffn1': _jnp.float32, 'ffn1_gate': _jnp.float32, 'ffn1_up': _jnp.float32, 'ffn1_down': _jnp.float32, 'norm_mix': _jnp.float32, 'w_in': _jnp.float32, 'conv_w': _jnp.float32, 'a_log': _jnp.float32, 'dt_bias': _jnp.float32, 'dn_norm': _jnp.float32, 'w_out': _jnp.float32, 'norm_ffn2': _jnp.float32, 'ffn2_gate': _jnp.float32, 'ffn2_up': _jnp.float32, 'ffn2_down': _jnp.float32, 'norm_final': _jnp.float32}
MOMENT_SCALE = {'norm_ffn1': 8.880018e-02, 'ffn1_gate': 3.699896e-02, 'ffn1_up': 3.581392e-02, 'ffn1_down': 5.937874e-02, 'norm_mix': 1.317632e-01, 'w_in': 6.745889e-02, 'conv_w': 7.826297e-02, 'a_log': 6.923206e-01, 'dt_bias': 6.834500e-01, 'dn_norm': 2.010950e-01, 'w_out': 7.903803e-02, 'norm_ffn2': 6.675150e-02, 'ffn2_gate': 2.788613e-02, 'ffn2_up': 2.703864e-02, 'ffn2_down': 4.480249e-02, 'norm_final': 3.204558e+01}


def _to_microbatches(a, axis):
    t = _jnp.moveaxis(a, axis, 0)
    t = t.reshape((N_MICROBATCH, t.shape[0] // N_MICROBATCH) + t.shape[1:])
    return _jnp.moveaxis(t, 1, axis + 1)


def setup_inputs(seed: int = 0) -> dict:
    inp = _fwd_setup_inputs(seed)
    key = _jax.random.fold_in(_jax.random.key(seed), 7919)
    shape, _ = _output_shape()
    out = dict(inp)
    out["loss_target"] = _jax.random.normal(_jax.random.fold_in(key, 0), shape, _jnp.float32)
    for i, name in enumerate(TWIN_WEIGHTS):
        w = inp[name].astype(_jnp.float32)
        if MOMENT_SCALE is None:
            s = _jnp.sqrt(_jnp.mean(_jnp.square(w)) + 1e-30)
        else:
            s = MOMENT_SCALE[name]
        km, kv = _jax.random.split(_jax.random.fold_in(key, i + 1))
        out[name] = w
        out["m_" + name] = s * _jax.random.normal(km, w.shape, _jnp.float32)
        out["v_" + name] = (s * s) * _jax.random.uniform(kv, w.shape, _jnp.float32, 0.5, 1.5)
    if N_MICROBATCH > 1:
        for name, axis in PER_EXAMPLE_BATCH_AXIS.items():
            out[name] = _to_microbatches(out[name], axis)
    return {'x': out['x'], 'norm_ffn1': out['norm_ffn1'], 'ffn1_gate': out['ffn1_gate'], 'ffn1_up': out['ffn1_up'], 'ffn1_down': out['ffn1_down'], 'norm_mix': out['norm_mix'], 'w_in': out['w_in'], 'conv_w': out['conv_w'], 'a_log': out['a_log'], 'dt_bias': out['dt_bias'], 'dn_norm': out['dn_norm'], 'w_out': out['w_out'], 'norm_ffn2': out['norm_ffn2'], 'ffn2_gate': out['ffn2_gate'], 'ffn2_up': out['ffn2_up'], 'ffn2_down': out['ffn2_down'], 'norm_final': out['norm_final'], 'loss_target': out['loss_target'], 'm_norm_ffn1': out['m_norm_ffn1'], 'm_ffn1_gate': out['m_ffn1_gate'], 'm_ffn1_up': out['m_ffn1_up'], 'm_ffn1_down': out['m_ffn1_down'], 'm_norm_mix': out['m_norm_mix'], 'm_w_in': out['m_w_in'], 'm_conv_w': out['m_conv_w'], 'm_a_log': out['m_a_log'], 'm_dt_bias': out['m_dt_bias'], 'm_dn_norm': out['m_dn_norm'], 'm_w_out': out['m_w_out'], 'm_norm_ffn2': out['m_norm_ffn2'], 'm_ffn2_gate': out['m_ffn2_gate'], 'm_ffn2_up': out['m_ffn2_up'], 'm_ffn2_down': out['m_ffn2_down'], 'm_norm_final': out['m_norm_final'], 'v_norm_ffn1': out['v_norm_ffn1'], 'v_ffn1_gate': out['v_ffn1_gate'], 'v_ffn1_up': out['v_ffn1_up'], 'v_ffn1_down': out['v_ffn1_down'], 'v_norm_mix': out['v_norm_mix'], 'v_w_in': out['v_w_in'], 'v_conv_w': out['v_conv_w'], 'v_a_log': out['v_a_log'], 'v_dt_bias': out['v_dt_bias'], 'v_dn_norm': out['v_dn_norm'], 'v_w_out': out['v_w_out'], 'v_norm_ffn2': out['v_norm_ffn2'], 'v_ffn2_gate': out['v_ffn2_gate'], 'v_ffn2_up': out['v_ffn2_up'], 'v_ffn2_down': out['v_ffn2_down'], 'v_norm_final': out['v_norm_final']}


def _loss(weights, diff, rest, loss_target):
    with _jax.named_scope("forward"):
        args = {**rest, TWIN_DIFF_INPUT: diff, **{k: w.astype(_WEIGHT_DTYPES[k]) for k, w in weights.items()}}
        y = _forward(args)
    with _jax.named_scope("loss_head"):
        err = _jnp.square(y.astype(_jnp.float32) - loss_target)
        return 0.5 * _jnp.sum(_jnp.mean(err, axis=-1)) if err.ndim else 0.5 * err


def _adamw(w, g, m, v):
    m = ADAM_B1 * m + (1.0 - ADAM_B1) * g
    v = ADAM_B2 * v + (1.0 - ADAM_B2) * _jnp.square(g)
    m_hat = m / (1.0 - ADAM_B1 ** ADAM_STEP)
    v_hat = v / (1.0 - ADAM_B2 ** ADAM_STEP)
    delta = -ADAM_LR * (m_hat / (_jnp.sqrt(v_hat) + ADAM_EPS) + ADAM_WD * w)
    return delta, m, v


def reference(x, norm_ffn1, ffn1_gate, ffn1_up, ffn1_down, norm_mix, w_in, conv_w, a_log, dt_bias, dn_norm, w_out, norm_ffn2, ffn2_gate, ffn2_up, ffn2_down, norm_final, loss_target, m_norm_ffn1, m_ffn1_gate, m_ffn1_up, m_ffn1_down, m_norm_mix, m_w_in, m_conv_w, m_a_log, m_dt_bias, m_dn_norm, m_w_out, m_norm_ffn2, m_ffn2_gate, m_ffn2_up, m_ffn2_down, m_norm_final, v_norm_ffn1, v_ffn1_gate, v_ffn1_up, v_ffn1_down, v_norm_mix, v_w_in, v_conv_w, v_a_log, v_dt_bias, v_dn_norm, v_w_out, v_norm_ffn2, v_ffn2_gate, v_ffn2_up, v_ffn2_down, v_norm_final):
    given = dict(x=x, norm_ffn1=norm_ffn1, ffn1_gate=ffn1_gate, ffn1_up=ffn1_up, ffn1_down=ffn1_down, norm_mix=norm_mix, w_in=w_in, conv_w=conv_w, a_log=a_log, dt_bias=dt_bias, dn_norm=dn_norm, w_out=w_out, norm_ffn2=norm_ffn2, ffn2_gate=ffn2_gate, ffn2_up=ffn2_up, ffn2_down=ffn2_down, norm_final=norm_final, loss_target=loss_target, m_norm_ffn1=m_norm_ffn1, m_ffn1_gate=m_ffn1_gate, m_ffn1_up=m_ffn1_up, m_ffn1_down=m_ffn1_down, m_norm_mix=m_norm_mix, m_w_in=m_w_in, m_conv_w=m_conv_w, m_a_log=m_a_log, m_dt_bias=m_dt_bias, m_dn_norm=m_dn_norm, m_w_out=m_w_out, m_norm_ffn2=m_norm_ffn2, m_ffn2_gate=m_ffn2_gate, m_ffn2_up=m_ffn2_up, m_ffn2_down=m_ffn2_down, m_norm_final=m_norm_final, v_norm_ffn1=v_norm_ffn1, v_ffn1_gate=v_ffn1_gate, v_ffn1_up=v_ffn1_up, v_ffn1_down=v_ffn1_down, v_norm_mix=v_norm_mix, v_w_in=v_w_in, v_conv_w=v_conv_w, v_a_log=v_a_log, v_dt_bias=v_dt_bias, v_dn_norm=v_dn_norm, v_w_out=v_w_out, v_norm_ffn2=v_norm_ffn2, v_ffn2_gate=v_ffn2_gate, v_ffn2_up=v_ffn2_up, v_ffn2_down=v_ffn2_down, v_norm_final=v_norm_final)
    weights = {n: given[n] for n in TWIN_WEIGHTS}
    shared = {n: given[n] for n in SHARED_INPUTS}
    per_example = {n: given[n] for n in ['x']}
    grad_fn = _jax.value_and_grad(_loss, argnums=(0, 1))

    def one_microbatch(ex, loss_target):
        ex = dict(ex)
        diff = ex.pop(TWIN_DIFF_INPUT)
        return grad_fn(weights, diff, {**shared, **ex}, loss_target)

    if N_MICROBATCH == 1:
        loss, (grad_w, grad_x) = one_microbatch(per_example, given["loss_target"])
    else:
        def body(carry, xs):
            loss_sum, grad_sum = carry
            l_k, (gw_k, gx_k) = one_microbatch(xs[0], xs[1])
            with _jax.named_scope("update"):
                return (loss_sum + l_k, _jax.tree.map(_jnp.add, grad_sum, gw_k)), gx_k

        init = (_jnp.zeros((), _jnp.float32), _jax.tree.map(_jnp.zeros_like, weights))
        (loss, grad_w), grad_x = _jax.lax.scan(body, init, (per_example, given["loss_target"]))
    with _jax.named_scope("update"):
        delta_w, new_m, new_v = {}, {}, {}
        for n in TWIN_WEIGHTS:
            delta_w[n], new_m[n], new_v[n] = _adamw(weights[n], grad_w[n], given["m_" + n], given["v_" + n])
    return (loss, grad_x, *[grad_w[n] for n in TWIN_WEIGHTS], *[delta_w[n] for n in TWIN_WEIGHTS],
            *[new_m[n] for n in TWIN_WEIGHTS], *[new_v[n] for n in TWIN_WEIGHTS])
```

```python
import functools

import numpy as np
import jax
import jax.numpy as jnp
from jax import lax
from jax.experimental import pallas as pl
from jax.experimental.pallas import tpu as pltpu

T = 4096
D = 1024
F = 2816
N_DEV = 8
A_HEADS = 8
A_HD = 64
AW = A_HEADS * A_HD
DN_H = 4
DN_HD = 128
DNW = DN_H * DN_HD
CH = 64
PAIR = 2 * CH
ZA = 3 * AW
ZD = 3 * DNW + DNW + 256
ZP = ZA + ZD
BD_BLK = (3 * DNW + DNW) // 128
IN_COLS = 3592
PATTERNS = ((128, 1), (512, 4), (2048, 16))
NORM_EPS = 1e-6
L2_EPS = 1e-6
ADAM_LR, ADAM_B1, ADAM_B2, ADAM_EPS, ADAM_WD, ADAM_STEP = 0.001, 0.9, 0.999, 1e-08, 0.01, 10
VMEM_LIMIT = 56 * 1024 * 1024
NEG = -1e30

BF = jnp.bfloat16
F32 = jnp.float32
NN = (((1,), (0,)), ((), ()))
NT = (((1,), (1,)), ((), ()))
TN = (((0,), (0,)), ((), ()))
HI = lax.Precision.HIGHEST
MESH_ID = pl.DeviceIdType.MESH
ANY = pl.BlockSpec(memory_space=pl.ANY)


def _cp():
    return pltpu.CompilerParams(vmem_limit_bytes=VMEM_LIMIT)


def _dg(a, b, dims):
    return lax.dot_general(a, b, dims, preferred_element_type=F32)


def _hdot(a, b):
    return lax.dot_general(a, b, NN, precision=HI, preferred_element_type=F32)


def _make_bdot(dims, da_dims, da_swap, db_dims, db_swap):
    @jax.custom_vjp
    def f(a, b):
        return _dg(a.astype(BF), b.astype(BF), dims)

    def fwd(a, b):
        return f(a, b), (a, b)

    def bwd(res, g):
        a, b = res
        gb, ab, bb = g.astype(BF), a.astype(BF), b.astype(BF)
        da = _dg(bb, gb, da_dims) if da_swap else _dg(gb, bb, da_dims)
        db = _dg(gb, ab, db_dims) if db_swap else _dg(ab, gb, db_dims)
        return da, db

    f.defvjp(fwd, bwd)
    return f


_bdot_nn = _make_bdot(NN, NT, False, TN, False)
_bdot_nt = _make_bdot(NT, NN, False, TN, True)
_bdot_tn = _make_bdot(TN, NT, True, NN, False)


def _iota(shape, dim):
    return lax.broadcasted_iota(jnp.int32, shape, dim)


def _col(x, idx):
    return jnp.sum(jnp.where(_iota(x.shape, 1) == idx, x, 0.0), axis=1, keepdims=True)


def _mm_nn(a, b, out_dtype, tm, tn, name):
    m, k = a.shape
    n = b.shape[1]

    def body(a_ref, b_ref, o_ref):
        o_ref[...] = _dg(a_ref[...], b_ref[...], NN).astype(out_dtype)

    return pl.pallas_call(
        body, grid=(m // tm, n // tn),
        in_specs=[pl.BlockSpec((tm, k), lambda i, j: (i, 0)), pl.BlockSpec((k, tn), lambda i, j: (0, j))],
        out_specs=pl.BlockSpec((tm, tn), lambda i, j: (i, j)),
        out_shape=jax.ShapeDtypeStruct((m, n), out_dtype), name=name, compiler_params=_cp())(a, b)


def _mm_nt(a, b, out_dtype, tm, tb, name):
    m, c = a.shape
    kb = b.shape[0]

    def body(a_ref, b_ref, o_ref):
        o_ref[...] = _dg(a_ref[...], b_ref[...], NT).astype(out_dtype)

    return pl.pallas_call(
        body, grid=(m // tm, kb // tb),
        in_specs=[pl.BlockSpec((tm, c), lambda i, j: (i, 0)), pl.BlockSpec((tb, c), lambda i, j: (j, 0))],
        out_specs=pl.BlockSpec((tm, tb), lambda i, j: (i, j)),
        out_shape=jax.ShapeDtypeStruct((m, kb), out_dtype), name=name, compiler_params=_cp())(a, b)


def _mm_tn(a, b, ta, tb, name):
    m, ka = a.shape
    nb = b.shape[1]

    def body(a_ref, b_ref, o_ref):
        o_ref[...] = _dg(a_ref[...], b_ref[...], TN)

    return pl.pallas_call(
        body, grid=(ka // ta, nb // tb),
        in_specs=[pl.BlockSpec((m, ta), lambda i, j: (0, i)), pl.BlockSpec((m, tb), lambda i, j: (0, j))],
        out_specs=pl.BlockSpec((ta, tb), lambda i, j: (i, j)),
        out_shape=jax.ShapeDtypeStruct((ka, nb), F32), name=name, compiler_params=_cp())(a, b)


def _rms_fwd(x, y, alpha, gain, name):
    tm = 512
    has_y = y is not None
    row = pl.BlockSpec((tm, D), lambda i: (i, 0))
    gspec = pl.BlockSpec((1, D), lambda i: (0, 0))

    def body(*refs):
        if has_y:
            x_ref, y_ref, g_ref, xo_ref, h_ref = refs
            xv = x_ref[...] + alpha * y_ref[...]
            xo_ref[...] = xv
        else:
            x_ref, g_ref, h_ref = refs
            xv = x_ref[...]
        r = lax.rsqrt(jnp.mean(xv * xv, axis=-1, keepdims=True) + NORM_EPS)
        h_ref[...] = (xv * r * g_ref[...]).astype(BF)

    if has_y:
        return pl.pallas_call(
            body, grid=(T // tm,), in_specs=[row, row, gspec], out_specs=[row, row],
            out_shape=[jax.ShapeDtypeStruct((T, D), F32), jax.ShapeDtypeStruct((T, D), BF)],
            name=name, compiler_params=_cp())(x, y, gain)
    h = pl.pallas_call(
        body, grid=(T // tm,), in_specs=[row, gspec], out_specs=row,
        out_shape=jax.ShapeDtypeStruct((T, D), BF), name=name, compiler_params=_cp())(x, gain)
    return x, h


def _rms_bwd(x, gain, dh, dres, alpha_out, name):
    tm = 512
    row = pl.BlockSpec((tm, D), lambda i: (i, 0))
    gspec = pl.BlockSpec((1, D), lambda i: (0, 0))

    def body(x_ref, g_ref, dh_ref, dres_ref, dx_ref, dxs_ref, dg_ref):
        i = pl.program_id(0)
        xv = x_ref[...]
        r = lax.rsqrt(jnp.mean(xv * xv, axis=-1, keepdims=True) + NORM_EPS)
        xh = xv * r
        dhv = dh_ref[...].astype(F32)
        part = jnp.sum(dhv * xh, axis=0, keepdims=True)

        @pl.when(i == 0)
        def _():
            dg_ref[...] = part

        @pl.when(i > 0)
        def _():
            dg_ref[...] += part

        dxh = dhv * g_ref[...]
        dx = r * (dxh - xh * jnp.mean(dxh * xh, axis=-1, keepdims=True)) + dres_ref[...]
        dx_ref[...] = dx
        dxs_ref[...] = (alpha_out * dx).astype(BF)

    return pl.pallas_call(
        body, grid=(T // tm,), in_specs=[row, gspec, row, row], out_specs=[row, row, gspec],
        out_shape=[jax.ShapeDtypeStruct((T, D), F32), jax.ShapeDtypeStruct((T, D), BF),
                   jax.ShapeDtypeStruct((1, D), F32)],
        name=name, compiler_params=_cp())(x, gain, dh, dres)


def _loss_bwd(x, gain, target, name):
    tm = 512
    row = pl.BlockSpec((tm, D), lambda i: (i, 0))
    gspec = pl.BlockSpec((1, D), lambda i: (0, 0))
    lspec = pl.BlockSpec((8, 128), lambda i: (0, 0))

    def body(x_ref, g_ref, t_ref, dx_ref, dxs_ref, dg_ref, loss_ref):
        i = pl.program_id(0)
        xv = x_ref[...]
        r = lax.rsqrt(jnp.mean(xv * xv, axis=-1, keepdims=True) + NORM_EPS)
        xh = xv * r
        diff = xh * g_ref[...] - t_ref[...]
        lpart = 0.5 * jnp.sum(jnp.mean(diff * diff, axis=-1, keepdims=True), axis=0, keepdims=True)
        dy = diff * (1.0 / D)
        part = jnp.sum(dy * xh, axis=0, keepdims=True)

        @pl.when(i == 0)
        def _():
            dg_ref[...] = part
            loss_ref[...] = jnp.broadcast_to(lpart, (8, 128))

        @pl.when(i > 0)
        def _():
            dg_ref[...] += part
            loss_ref[...] += jnp.broadcast_to(lpart, (8, 128))

        dxh = dy * g_ref[...]
        dx = r * (dxh - xh * jnp.mean(dxh * xh, axis=-1, keepdims=True))
        dx_ref[...] = dx
        dxs_ref[...] = (0.5 * dx).astype(BF)

    return pl.pallas_call(
        body, grid=(T // tm,), in_specs=[row, gspec, row], out_specs=[row, row, gspec, lspec],
        out_shape=[jax.ShapeDtypeStruct((T, D), F32), jax.ShapeDtypeStruct((T, D), BF),
                   jax.ShapeDtypeStruct((1, D), F32), jax.ShapeDtypeStruct((8, 128), F32)],
        name=name, compiler_params=_cp())(x, gain, target)


def _swiglu_fwd(gu, name):
    tm = 256

    def body(gu_ref, a_ref):
        g = gu_ref[:, :F].astype(F32)
        u = gu_ref[:, F:].astype(F32)
        a_ref[...] = (g * jax.nn.sigmoid(g) * u).astype(BF)

    return pl.pallas_call(
        body, grid=(T // tm,), in_specs=[pl.BlockSpec((tm, 2 * F), lambda i: (i, 0))],
        out_specs=pl.BlockSpec((tm, F), lambda i: (i, 0)),
        out_shape=jax.ShapeDtypeStruct((T, F), BF), name=name, compiler_params=_cp())(gu)


def _swiglu_bwd(gu, da, name):
    tm = 256

    def body(gu_ref, da_ref, o_ref):
        g = gu_ref[:, :F].astype(F32)
        u = gu_ref[:, F:].astype(F32)
        dav = da_ref[...].astype(F32)
        s = jax.nn.sigmoid(g)
        o_ref[:, :F] = (dav * u * (s * (1.0 + g * (1.0 - s)))).astype(BF)
        o_ref[:, F:] = (dav * (g * s)).astype(BF)

    return pl.pallas_call(
        body, grid=(T // tm,),
        in_specs=[pl.BlockSpec((tm, 2 * F), lambda i: (i, 0)), pl.BlockSpec((tm, F), lambda i: (i, 0))],
        out_specs=pl.BlockSpec((tm, 2 * F), lambda i: (i, 0)),
        out_shape=jax.ShapeDtypeStruct((T, 2 * F), BF), name=name, compiler_params=_cp())(gu, da)


def _ffn_fwd(x_prev, y_prev, alpha, gain, wgu, wd, tag):
    x_in, h = _rms_fwd(x_prev, y_prev, alpha, gain, tag + "_norm")
    gu = _mm_nn(h, wgu, BF, 1024, 1408, tag + "_gu")
    act = _swiglu_fwd(gu, tag + "_act")
    y = _mm_nn(act, wd, F32, 1024, 1024, tag + "_down")
    return x_in, h, gu, act, y


def _ffn_bwd(x_in, gain, h, gu, act, wgu, wd, dxo, dys, alpha_out, tag):
    dact = _mm_nt(dys, wd, BF, 1024, 1408, tag + "_dact")
    dgu = _swiglu_bwd(gu, dact, tag + "_dgu")
    dh = _mm_nt(dgu, wgu, F32, 512, 512, tag + "_dh")
    dwgu = _mm_tn(h, dgu, 512, 512, tag + "_dwgu")
    dwd = _mm_tn(act, dys, 256, 1024, tag + "_dwd")
    dx, dxs, dgain = _rms_bwd(x_in, gain, dh, dxo, alpha_out, tag + "_dnorm")
    return dx, dxs, dgain, dwgu, dwd


def _slope(h):
    return float(2.0 ** (-(h + 1)))


def _pair_masks():
    lane = _iota((128, 128), 1)
    return lane < A_HD, lane >= A_HD


def _attn_fwd(qkv_r, dil, name):
    L = T // dil
    nb = L // 128

    def body(q_ref, kp_ref, kc_ref, vp_ref, vc_ref, o_ref, m_ref, l_ref):
        n = pl.program_id(1)
        qi = _iota((128, 256), 0)
        kj = _iota((128, 256), 1)
        steps = qi + 128 - kj
        valid = (steps >= 0) & (steps <= 128) & ((n * 128 + kj - 128) >= 0)
        dist = (steps * dil).astype(F32)
        lane = _iota((128, 128), 1)
        lo, hi = _pair_masks()
        m_acc = jnp.zeros((128, 128), F32)
        l_acc = jnp.ones((128, 128), F32)
        for j in range(A_HEADS // 2):
            sl = slice(128 * j, 128 * j + 128)
            qp = q_ref[:, sl]
            kcat = jnp.concatenate([kp_ref[:, sl], kc_ref[:, sl]], axis=0)
            vcat = jnp.concatenate([vp_ref[:, sl], vc_ref[:, sl]], axis=0)
            outs = []
            for e in range(2):
                h = 2 * j + e
                qm = jnp.where(lo if e == 0 else hi, qp, jnp.zeros_like(qp))
                s = _dg(qm, kcat, NT) * (A_HD ** -0.5) - _slope(h) * dist
                s = jnp.where(valid, s, NEG)
                m = jnp.max(s, axis=1, keepdims=True)
                p = jnp.exp(s - m)
                l = jnp.sum(p, axis=1, keepdims=True)
                outs.append(_dg(p.astype(BF), vcat, NN) / l)
                m_acc = jnp.where(lane == h, m, m_acc)
                l_acc = jnp.where(lane == h, l, l_acc)
            o_ref[:, sl] = jnp.where(lo, outs[0], outs[1])
        m_ref[...] = m_acc
        l_ref[...] = l_acc

    blk = lambda col: pl.BlockSpec((128, AW), lambda r, n: (n, 3 * r + col))
    prv = lambda col: pl.BlockSpec((128, AW), lambda r, n: (jnp.maximum(n - 1, 0), 3 * r + col))
    return pl.pallas_call(
        body, grid=(dil, nb),
        in_specs=[blk(0), prv(1), blk(1), prv(2), blk(2)],
        out_specs=[pl.BlockSpec((128, AW), lambda r, n: (n, r)),
                   pl.BlockSpec((128, 128), lambda r, n: (n, r)),
                   pl.BlockSpec((128, 128), lambda r, n: (n, r))],
        out_shape=[jax.ShapeDtypeStruct((L, dil * AW), F32), jax.ShapeDtypeStruct((L, dil * 128), F32),
                   jax.ShapeDtypeStruct((L, dil * 128), F32)],
        name=name, compiler_params=_cp())(qkv_r, qkv_r, qkv_r, qkv_r, qkv_r)


def _expand_mat():
    r = _iota((128, AW), 0)
    c = _iota((128, AW), 1)
    return jnp.where((c // A_HD) == r, 1.0, 0.0).astype(F32)


def _attn_merge(os, ms, ls, name):
    tm = 256
    row = pl.BlockSpec((tm, AW), lambda i: (i, 0))
    st = pl.BlockSpec((tm, 128), lambda i: (i, 0))

    def body(o1, o2, o3, m1, m2, m3, l1, l2, l3, of_ref, ob_ref, lse_ref):
        mx = jnp.maximum(jnp.maximum(m1[...], m2[...]), m3[...])
        w = [l[...] * jnp.exp(m[...] - mx) for m, l in ((m1, l1), (m2, l2), (m3, l3))]
        den = w[0] + w[1] + w[2]
        ex = _expand_mat()
        acc = jnp.zeros((tm, AW), F32)
        for wp, op in zip(w, (o1, o2, o3)):
            acc = acc + _hdot(wp, ex) * op[...]
        out = acc / _hdot(den, ex)
        of_ref[...] = out
        ob_ref[...] = out.astype(BF)
        lse_ref[...] = mx + jnp.log(den)

    return pl.pallas_call(
        body, grid=(T // tm,), in_specs=[row] * 3 + [st] * 6, out_specs=[row, row, st],
        out_shape=[jax.ShapeDtypeStruct((T, AW), F32), jax.ShapeDtypeStruct((T, AW), BF),
                   jax.ShapeDtypeStruct((T, 128), F32)],
        name=name, compiler_params=_cp())(*os, *ms, *ls)


def _attn_delta(dout, out, name):
    tm = 256
    row = pl.BlockSpec((tm, AW), lambda i: (i, 0))

    def body(do_ref, o_ref, d_ref):
        d_ref[...] = lax.dot_general(do_ref[...] * o_ref[...], _expand_mat(), NT, precision=HI,
                                     preferred_element_type=F32)

    return pl.pallas_call(
        body, grid=(T // tm,), in_specs=[row, row], out_specs=pl.BlockSpec((tm, 128), lambda i: (i, 0)),
        out_shape=jax.ShapeDtypeStruct((T, 128), F32), name=name, compiler_params=_cp())(dout, out)


def _attn_bwd_dq(qkv_r, do_r, lse_r, dl_r, dil, name):
    L = T // dil
    nb = L // 128

    def body(q_ref, kp_ref, kc_ref, vp_ref, vc_ref, do_ref, lse_ref, dl_ref, dq_ref):
        n = pl.program_id(1)
        qi = _iota((128, 256), 0)
        kj = _iota((128, 256), 1)
        steps = qi + 128 - kj
        valid = (steps >= 0) & (steps <= 128) & ((n * 128 + kj - 128) >= 0)
        dist = (steps * dil).astype(F32)
        lo, hi = _pair_masks()
        lse = lse_ref[...]
        dl = dl_ref[...]
        for j in range(A_HEADS // 2):
            sl = slice(128 * j, 128 * j + 128)
            qp = q_ref[:, sl]
            dop = do_ref[:, sl].astype(BF)
            kcat = jnp.concatenate([kp_ref[:, sl], kc_ref[:, sl]], axis=0)
            vcat = jnp.concatenate([vp_ref[:, sl], vc_ref[:, sl]], axis=0)
            outs = []
            for e in range(2):
                h = 2 * j + e
                msk = lo if e == 0 else hi
                qm = jnp.where(msk, qp, jnp.zeros_like(qp))
                dom = jnp.where(msk, dop, jnp.zeros_like(dop))
                s = _dg(qm, kcat, NT) * (A_HD ** -0.5) - _slope(h) * dist
                p = jnp.where(valid, jnp.exp(jnp.where(valid, s, NEG) - _col(lse, h)), 0.0)
                dp = _dg(dom, vcat, NT)
                ds = p * (dp - _col(dl, h))
                outs.append(_dg(ds.astype(BF), kcat, NN) * (A_HD ** -0.5))
            dq_ref[:, sl] = jnp.where(lo, outs[0], outs[1])

    blk = lambda col: pl.BlockSpec((128, AW), lambda r, n: (n, 3 * r + col))
    prv = lambda col: pl.BlockSpec((128, AW), lambda r, n: (jnp.maximum(n - 1, 0), 3 * r + col))
    cur = pl.BlockSpec((128, AW), lambda r, n: (n, r))
    st = pl.BlockSpec((128, 128), lambda r, n: (n, r))
    return pl.pallas_call(
        body, grid=(dil, nb),
        in_specs=[blk(0), prv(1), blk(1), prv(2), blk(2), cur, st, st],
        out_specs=cur, out_shape=jax.ShapeDtypeStruct((L, dil * AW), F32),
        name=name, compiler_params=_cp())(qkv_r, qkv_r, qkv_r, qkv_r, qkv_r, do_r, lse_r, dl_r)


def _attn_bwd_dkv(qkv_r, do_r, lse_r, dl_r, dil, name):
    L = T // dil
    nb = L // 128

    def body(k_ref, v_ref, qc_ref, qn_ref, doc_ref, don_ref, lsec_ref, lsen_ref, dlc_ref, dln_ref,
             dk_ref, dv_ref):
        mblk = pl.program_id(1)
        qi = _iota((128, 128), 0)
        kj = _iota((128, 128), 1)
        steps_c = qi - kj
        steps_n = qi + 128 - kj
        valid_c = steps_c >= 0
        valid_n = (steps_n <= 128) & (mblk + 1 < nb)
        lo, hi = _pair_masks()
        for j in range(A_HEADS // 2):
            sl = slice(128 * j, 128 * j + 128)
            kp = k_ref[:, sl]
            vp = v_ref[:, sl]
            dk = jnp.zeros((128, 128), F32)
            dv = jnp.zeros((128, 128), F32)
            for q_ref, do_ref, lse_ref, dl_ref, steps, valid in (
                    (qc_ref, doc_ref, lsec_ref, dlc_ref, steps_c, valid_c),
                    (qn_ref, don_ref, lsen_ref, dln_ref, steps_n, valid_n)):
                qp = q_ref[:, sl]
                dop = do_ref[:, sl].astype(BF)
                dist = (steps * dil).astype(F32)
                lse = lse_ref[...]
                dl = dl_ref[...]
                for e in range(2):
                    h = 2 * j + e
                    msk = lo if e == 0 else hi
                    qm = jnp.where(msk, qp, jnp.zeros_like(qp))
                    dom = jnp.where(msk, dop, jnp.zeros_like(dop))
                    s = _dg(qm, kp, NT) * (A_HD ** -0.5) - _slope(h) * dist
                    p = jnp.where(valid, jnp.exp(jnp.where(valid, s, NEG) - _col(lse, h)), 0.0)
                    dp = _dg(dom, vp, NT)
                    ds = p * (dp - _col(dl, h))
                    dv = dv + _dg(p.astype(BF), dom, TN)
                    dk = dk + _dg(ds.astype(BF), qm, TN) * (A_HD ** -0.5)
            dk_ref[:, sl] = dk
            dv_ref[:, sl] = dv

    cur3 = lambda col: pl.BlockSpec((128, AW), lambda r, m: (m, 3 * r + col))
    nxt3 = lambda col: pl.BlockSpec((128, AW), lambda r, m: (jnp.minimum(m + 1, nb - 1), 3 * r + col))
    cur = pl.BlockSpec((128, AW), lambda r, m: (m, r))
    nxt = pl.BlockSpec((128, AW), lambda r, m: (jnp.minimum(m + 1, nb - 1), r))
    stc = pl.BlockSpec((128, 128), lambda r, m: (m, r))
    stn = pl.BlockSpec((128, 128), lambda r, m: (jnp.minimum(m + 1, nb - 1), r))
    return pl.pallas_call(
        body, grid=(dil, nb),
        in_specs=[cur3(1), cur3(2), cur3(0), nxt3(0), cur, nxt, stc, stn, stc, stn],
        out_specs=[cur, cur],
        out_shape=[jax.ShapeDtypeStruct((L, dil * AW), F32), jax.ShapeDtypeStruct((L, dil * AW), F32)],
        name=name, compiler_params=_cp())(qkv_r, qkv_r, qkv_r, qkv_r, do_r, do_r, lse_r, lse_r, dl_r, dl_r)


def _sum3_cast(parts, name):
    tm = 512
    row = pl.BlockSpec((tm, AW), lambda i: (i, 0))

    def body(*refs):
        o_ref = refs[-1]
        for c in range(3):
            acc = refs[c][...] + refs[3 + c][...] + refs[6 + c][...]
            o_ref[:, c * AW:(c + 1) * AW] = acc.astype(BF)

    return pl.pallas_call(
        body, grid=(T // tm,), in_specs=[row] * 9, out_specs=pl.BlockSpec((tm, ZA), lambda i: (i, 0)),
        out_shape=jax.ShapeDtypeStruct((T, ZA), BF), name=name, compiler_params=_cp())(*parts)


def _silu(x):
    return x * jax.nn.sigmoid(x)


def _qk_math(c):
    s = _silu(c)
    return s * lax.rsqrt(jnp.sum(s * s, axis=-1, keepdims=True) + L2_EPS)


def _softplus(x):
    return jnp.maximum(x, 0.0) + jnp.log(1.0 + jnp.exp(-jnp.abs(x)))


def _gate_math(bd, alog_row, dtb_row):
    lane = _iota(bd.shape, 1)
    beta = jax.nn.sigmoid(bd)
    g = -jnp.exp(alog_row) * _softplus(bd + dtb_row)
    return jnp.where(lane < DN_H, beta, jnp.where(lane < 2 * DN_H, g, 0.0))


def _shift_down(cur, halo, s):
    if s == 0:
        return cur
    rolled = pltpu.roll(cur, s, 0)
    hr = pltpu.roll(halo, s, 0)
    head = jnp.where(_iota(hr.shape, 0) < s, hr, rolled[:8])
    return jnp.concatenate([head, rolled[8:]], axis=0)


def _shift_up(cur, halo, s):
    if s == 0:
        return cur
    rows = cur.shape[0]
    rolled = pltpu.roll(cur, rows - s, 0)
    hr = pltpu.roll(halo, 8 - s, 0)
    tail = jnp.where(_iota(hr.shape, 0) >= 8 - s, hr, rolled[rows - 8:])
    return jnp.concatenate([rolled[:rows - 8], tail], axis=0)


def _dn_pre_fwd(z_dn, conv_w8, alog_row, dtb_row, name):
    tm = 256
    wq = 3 * DNW

    def body(raw_ref, halo_ref, bd_ref, w_ref, al_ref, dt_ref, conv_ref, qkv_ref, bg_ref):
        i = pl.program_id(0)
        cur = raw_ref[...]
        halo = jnp.where(i > 0, halo_ref[...], 0.0)
        w = w_ref[...]
        conv = jnp.zeros((tm, wq), F32)
        for j in range(4):
            conv = conv + _shift_down(cur, halo, 3 - j) * w[j:j + 1, :]
        conv_ref[...] = conv
        for blk in range(3 * DN_H):
            sl = slice(128 * blk, 128 * blk + 128)
            c = conv[:, sl]
            qkv_ref[:, sl] = _qk_math(c) if blk < 2 * DN_H else _silu(c)
        bg_ref[...] = _gate_math(bd_ref[...], al_ref[...], dt_ref[...])

    one = pl.BlockSpec((1, 128), lambda i: (0, 0))
    return pl.pallas_call(
        body, grid=(T // tm,),
        in_specs=[pl.BlockSpec((tm, wq), lambda i: (i, 0)),
                  pl.BlockSpec((8, wq), lambda i: (jnp.maximum(i * (tm // 8) - 1, 0), 0)),
                  pl.BlockSpec((tm, 128), lambda i: (i, BD_BLK)),
                  pl.BlockSpec((8, wq), lambda i: (0, 0)), one, one],
        out_specs=[pl.BlockSpec((tm, wq), lambda i: (i, 0)), pl.BlockSpec((tm, wq), lambda i: (i, 0)),
                   pl.BlockSpec((tm, 128), lambda i: (i, 0))],
        out_shape=[jax.ShapeDtypeStruct((T, wq), F32), jax.ShapeDtypeStruct((T, wq), F32),
                   jax.ShapeDtypeStruct((T, 128), F32)],
        name=name, compiler_params=_cp())(z_dn, z_dn, z_dn, conv_w8, alog_row, dtb_row)


def _dn_pre_bwd(conv, z_dn, alog_row, dtb_row, dqn, dkn, dvn, dbg, name):
    tm = 256
    wq = 3 * DNW

    def body(conv_ref, bd_ref, al_ref, dt_ref, dq_ref, dk_ref, dv_ref, dbg_ref,
             dconv_ref, dbd_ref, dal_ref, ddt_ref):
        i = pl.program_id(0)
        for blk in range(3 * DN_H):
            sl = slice(128 * blk, 128 * blk + 128)
            src = (dq_ref, dk_ref, dv_ref)[blk // DN_H]
            ct = src[:, 128 * (blk % DN_H):128 * (blk % DN_H) + 128]
            fn = _qk_math if blk < 2 * DN_H else _silu
            _, vjp = jax.vjp(fn, conv_ref[:, sl])
            dconv_ref[:, sl] = vjp(ct)[0]
        _, vjp = jax.vjp(_gate_math, bd_ref[...], al_ref[...], dt_ref[...])
        dbd, dal, ddt = vjp(dbg_ref[...])
        dbd_ref[...] = dbd

        @pl.when(i == 0)
        def _():
            dal_ref[...] = dal
            ddt_ref[...] = ddt

        @pl.when(i > 0)
        def _():
            dal_ref[...] += dal
            ddt_ref[...] += ddt

    one = pl.BlockSpec((1, 128), lambda i: (0, 0))
    row = pl.BlockSpec((tm, wq), lambda i: (i, 0))
    hd = pl.BlockSpec((tm, DNW), lambda i: (i, 0))
    st = pl.BlockSpec((tm, 128), lambda i: (i, 0))
    return pl.pallas_call(
        body, grid=(T // tm,),
        in_specs=[row, pl.BlockSpec((tm, 128), lambda i: (i, BD_BLK)), one, one, hd, hd, hd, st],
        out_specs=[row, st, one, one],
        out_shape=[jax.ShapeDtypeStruct((T, wq), F32), jax.ShapeDtypeStruct((T, 128), F32),
                   jax.ShapeDtypeStruct((1, 128), F32), jax.ShapeDtypeStruct((1, 128), F32)],
        name=name, compiler_params=_cp())(conv, z_dn, alog_row, dtb_row, dqn, dkn, dvn, dbg)


def _dn_conv_bwd(dconv, z_dn, conv_w8, name):
    tm = 256
    wq = 3 * DNW
    last = T // tm - 1

    def body(dc_ref, dcn_ref, raw_ref, halo_ref, w_ref, draw_ref, dw_ref):
        i = pl.program_id(0)
        dc = dc_ref[...]
        nxt = jnp.where(i < last, dcn_ref[...], 0.0)
        cur = raw_ref[...]
        halo = jnp.where(i > 0, halo_ref[...], 0.0)
        w = w_ref[...]
        draw = jnp.zeros((tm, wq), F32)
        rows = []
        for j in range(4):
            draw = draw + _shift_up(dc, nxt, 3 - j) * w[j:j + 1, :]
            rows.append(jnp.sum(dc * _shift_down(cur, halo, 3 - j), axis=0, keepdims=True))
        draw_ref[...] = draw
        part = jnp.concatenate(rows + [jnp.zeros((4, wq), F32)], axis=0)

        @pl.when(i == 0)
        def _():
            dw_ref[...] = part

        @pl.when(i > 0)
        def _():
            dw_ref[...] += part

    row = pl.BlockSpec((tm, wq), lambda i: (i, 0))
    return pl.pallas_call(
        body, grid=(T // tm,),
        in_specs=[row, pl.BlockSpec((8, wq), lambda i: (jnp.minimum((i + 1) * (tm // 8), T // 8 - 1), 0)),
                  row, pl.BlockSpec((8, wq), lambda i: (jnp.maximum(i * (tm // 8) - 1, 0), 0)),
                  pl.BlockSpec((8, wq), lambda i: (0, 0))],
        out_specs=[row, pl.BlockSpec((8, wq), lambda i: (0, 0))],
        out_shape=[jax.ShapeDtypeStruct((T, wq), F32), jax.ShapeDtypeStruct((8, wq), F32)],
        name=name, compiler_params=_cp())(dconv, dconv, z_dn, z_dn, conv_w8)


def _prep_math(q, k, v, bg, h):
    beta = _col(bg, h)
    gb = jnp.broadcast_to(_col(bg, DN_H + h), (PAIR, 128))
    ri = _iota((PAIR, PAIR), 0)
    ci = _iota((PAIR, PAIR), 1)
    same = (ri // CH) == (ci // CH)
    causal = same & (ci <= ri)
    strict = same & (ci < ri)
    eye = ri == ci
    gc = _hdot(jnp.where(causal, 1.0, 0.0), gb)
    glast = _hdot(jnp.where(same, 1.0, 0.0), gb)
    gc_cols = _hdot(jnp.ones((PAIR, PAIR), F32), jnp.where(eye, gc, 0.0))
    decay = jnp.exp(jnp.where(causal, gc - gc_cols, NEG))
    egc = jnp.exp(gc)
    kb = k * beta
    a_mat = jnp.where(strict, _bdot_nt(kb, k) * decay, 0.0)
    p = -a_mat
    tinv = jnp.where(eye, 1.0, 0.0) + p
    for _ in range(5):
        p = _hdot(p, p)
        tinv = tinv + _hdot(tinv, p)
    u = _hdot(tinv, v * beta)
    w = _hdot(tinv, kb * egc)
    qs = q * (DN_HD ** -0.5)
    attn = jnp.where(causal, _bdot_nt(qs, k) * decay, 0.0)
    return u, w, qs * egc, k * jnp.exp(glast - gc), attn, jnp.exp(glast)


def _dn_prep_fwd(qkv, bg, name):
    rows = 256
    hd = lambda off: pl.BlockSpec((rows, 128), lambda g, h: (g, off + h))
    out = pl.BlockSpec((rows, 128), lambda g, h: (g, h))

    def body(q_ref, k_ref, v_ref, bg_ref, *outs):
        h = pl.program_id(1)
        for pr in range(rows // PAIR):
            rs = slice(PAIR * pr, PAIR * pr + PAIR)
            res = _prep_math(q_ref[rs, :], k_ref[rs, :], v_ref[rs, :], bg_ref[rs, :], h)
            for o_ref, val in zip(outs, res):
                o_ref[rs, :] = val

    return pl.pallas_call(
        body, grid=(T // rows, DN_H),
        in_specs=[hd(0), hd(DN_H), hd(2 * DN_H), pl.BlockSpec((rows, 128), lambda g, h: (g, 0))],
        out_specs=[out] * 6, out_shape=[jax.ShapeDtypeStruct((T, DNW), F32)] * 6,
        name=name, compiler_params=_cp())(qkv, qkv, qkv, bg)


def _dn_prep_bwd(qkv, bg, cts, name):
    rows = 256
    hd = lambda off: pl.BlockSpec((rows, 128), lambda g, h: (g, off + h))
    out = pl.BlockSpec((rows, 128), lambda g, h: (g, h))
    st = pl.BlockSpec((rows, 128), lambda g, h: (g, 0))

    def body(q_ref, k_ref, v_ref, bg_ref, c0, c1, c2, c3, c4, c5, dq_ref, dk_ref, dv_ref, dbg_ref):
        h = pl.program_id(1)
        parts = []
        for pr in range(rows // PAIR):
            rs = slice(PAIR * pr, PAIR * pr + PAIR)
            _, vjp = jax.vjp(functools.partial(_prep_math, h=h), q_ref[rs, :], k_ref[rs, :], v_ref[rs, :],
                             bg_ref[rs, :])
            dq, dk, dv, dbg = vjp(tuple(c[rs, :] for c in (c0, c1, c2, c3, c4, c5)))
            dq_ref[rs, :] = dq
            dk_ref[rs, :] = dk
            dv_ref[rs, :] = dv
            parts.append(dbg)
        dbg_all = jnp.concatenate(parts, axis=0)

        @pl.when(h == 0)
        def _():
            dbg_ref[...] = dbg_all

        @pl.when(h > 0)
        def _():
            dbg_ref[...] += dbg_all

    return pl.pallas_call(
        body, grid=(T // rows, DN_H),
        in_specs=[hd(0), hd(DN_H), hd(2 * DN_H), st] + [out] * 6,
        out_specs=[out, out, out, st],
        out_shape=[jax.ShapeDtypeStruct((T, DNW), F32)] * 3 + [jax.ShapeDtypeStruct((T, 128), F32)],
        name=name, compiler_params=_cp())(qkv, qkv, qkv, bg, *cts)


def _step_math(s, u, w, qg, kdec, attn, decb, sub):
    vnew = u - _bdot_nn(w, s)
    z = jnp.zeros((CH, 128), F32)
    vfull = jnp.concatenate([vnew, z] if sub == 0 else [z, vnew], axis=0)
    o = _bdot_nn(qg, s) + _bdot_nn(attn, vfull)
    dec = jnp.sum(decb, axis=0, keepdims=True) * (1.0 / CH)
    return s * dec + _bdot_tn(kdec, vnew), o


def _dn_scan_fwd(prep, name):
    npair = T // PAIR
    row = pl.BlockSpec((PAIR, DNW), lambda p: (p, 0))

    def body(u_ref, w_ref, qg_ref, kd_ref, at_ref, db_ref, o_ref, ss_ref, s_ref):
        @pl.when(pl.program_id(0) == 0)
        def _():
            s_ref[...] = jnp.zeros_like(s_ref)

        for h in range(DN_H):
            ls = slice(128 * h, 128 * h + 128)
            s = s_ref[h]
            for sub in range(2):
                rs = slice(CH * sub, CH * sub + CH)
                ss_ref[sub, h] = s
                s, o = _step_math(s, u_ref[rs, ls], w_ref[rs, ls], qg_ref[rs, ls], kd_ref[rs, ls],
                                  at_ref[rs, ls], db_ref[rs, ls], sub)
                o_ref[rs, ls] = o
            s_ref[h] = s

    return pl.pallas_call(
        body, grid=(npair,), in_specs=[row] * 6,
        out_specs=[row, pl.BlockSpec((2, DN_H, 128, 128), lambda p: (p, 0, 0, 0))],
        out_shape=[jax.ShapeDtypeStruct((T, DNW), F32), jax.ShapeDtypeStruct((T // CH, DN_H, 128, 128), F32)],
        scratch_shapes=[pltpu.VMEM((DN_H, 128, 128), F32)],
        name=name, compiler_params=_cp())(*prep)


def _dn_scan_bwd(prep, states, do, name):
    npair = T // PAIR
    row = pl.BlockSpec((PAIR, DNW), lambda p: (npair - 1 - p, 0))

    def body(u_ref, w_ref, qg_ref, kd_ref, at_ref, db_ref, ss_ref, do_ref, *rest):
        outs, ds_ref = rest[:6], rest[6]

        @pl.when(pl.program_id(0) == 0)
        def _():
            ds_ref[...] = jnp.zeros_like(ds_ref)

        for h in range(DN_H):
            ls = slice(128 * h, 128 * h + 128)
            ds = ds_ref[h]
            for sub in (1, 0):
                rs = slice(CH * sub, CH * sub + CH)
                args = (ss_ref[sub, h],) + tuple(r[rs, ls] for r in (u_ref, w_ref, qg_ref, kd_ref, at_ref, db_ref))
                _, vjp = jax.vjp(functools.partial(_step_math, sub=sub), *args)
                cts = vjp((ds, do_ref[rs, ls]))
                ds = cts[0]
                for o_ref, val in zip(outs, cts[1:]):
                    o_ref[rs, ls] = val
            ds_ref[h] = ds

    return pl.pallas_call(
        body, grid=(npair,),
        in_specs=[row] * 6 + [pl.BlockSpec((2, DN_H, 128, 128), lambda p: (npair - 1 - p, 0, 0, 0)), row],
        out_specs=[row] * 6, out_shape=[jax.ShapeDtypeStruct((T, DNW), F32)] * 6,
        scratch_shapes=[pltpu.VMEM((DN_H, 128, 128), F32)],
        name=name, compiler_params=_cp())(*prep, states, do)


def _post_math(o, gate, wrow):
    return o * lax.rsqrt(jnp.mean(o * o, axis=-1, keepdims=True) + NORM_EPS) * wrow * _silu(gate)


def _dn_post_fwd(o, z_dn, dn_norm, name):
    tm = 512
    row = pl.BlockSpec((tm, DNW), lambda i: (i, 0))

    def body(o_ref, g_ref, w_ref, y_ref):
        for h in range(DN_H):
            ls = slice(128 * h, 128 * h + 128)
            y_ref[:, ls] = _post_math(o_ref[:, ls], g_ref[:, ls], w_ref[...]).astype(BF)

    return pl.pallas_call(
        body, grid=(T // tm,),
        in_specs=[row, pl.BlockSpec((tm, DNW), lambda i: (i, 3)), pl.BlockSpec((1, 128), lambda i: (0, 0))],
        out_specs=row, out_shape=jax.ShapeDtypeStruct((T, DNW), BF),
        name=name, compiler_params=_cp())(o, z_dn, dn_norm)


def _dn_post_bwd(o, z_dn, dn_norm, dy, name):
    tm = 512
    row = pl.BlockSpec((tm, DNW), lambda i: (i, 0))
    one = pl.BlockSpec((1, 128), lambda i: (0, 0))

    def body(o_ref, g_ref, w_ref, dy_ref, do_ref, dg_ref, dw_ref):
        i = pl.program_id(0)
        dw = jnp.zeros((1, 128), F32)
        for h in range(DN_H):
            ls = slice(128 * h, 128 * h + 128)
            _, vjp = jax.vjp(_post_math, o_ref[:, ls], g_ref[:, ls], w_ref[...])
            do, dg, dwh = vjp(dy_ref[:, ls].astype(F32))
            do_ref[:, ls] = do
            dg_ref[:, ls] = dg
            dw = dw + dwh

        @pl.when(i == 0)
        def _():
            dw_ref[...] = dw

        @pl.when(i > 0)
        def _():
            dw_ref[...] += dw

    return pl.pallas_call(
        body, grid=(T // tm,),
        in_specs=[row, pl.BlockSpec((tm, DNW), lambda i: (i, 3)), one, row],
        out_specs=[row, row, one],
        out_shape=[jax.ShapeDtypeStruct((T, DNW), F32), jax.ShapeDtypeStruct((T, DNW), F32),
                   jax.ShapeDtypeStruct((1, 128), F32)],
        name=name, compiler_params=_cp())(o, z_dn, dn_norm, dy)


def _dz_dn_assemble(draw, dgate, dbd, name):
    tm = 512

    def body(a_ref, b_ref, c_ref, o_ref):
        o_ref[:, :3 * DNW] = a_ref[...].astype(BF)
        o_ref[:, 3 * DNW:4 * DNW] = b_ref[...].astype(BF)
        o_ref[:, 4 * DNW:4 * DNW + 128] = c_ref[...].astype(BF)
        o_ref[:, 4 * DNW + 128:] = jnp.zeros((tm, 128), BF)

    return pl.pallas_call(
        body, grid=(T // tm,),
        in_specs=[pl.BlockSpec((tm, 3 * DNW), lambda i: (i, 0)), pl.BlockSpec((tm, DNW), lambda i: (i, 0)),
                  pl.BlockSpec((tm, 128), lambda i: (i, 0))],
        out_specs=pl.BlockSpec((tm, ZD), lambda i: (i, 0)),
        out_shape=jax.ShapeDtypeStruct((T, ZD), BF), name=name, compiler_params=_cp())(draw, dgate, dbd)


def _exchange(bufs, scatter, name):
    nt = len(bufs)
    shapes = [b.shape[1:] if scatter else b.shape for b in bufs]

    def body(*refs):
        ins, outs = refs[:nt], refs[nt:2 * nt]
        send_sems, recv_sems, loc_sems = refs[2 * nt:]
        x, y, c = lax.axis_index("x"), lax.axis_index("y"), lax.axis_index("c")
        me = 4 * x + 2 * y + c
        copies = []
        for t in range(nt):
            src = ins[t].at[me] if scatter else ins[t]
            cp = pltpu.make_async_copy(src, outs[t].at[me], loc_sems.at[t])
            cp.start()
            copies.append(cp)
        remote = []
        for k in (1, 2, 4, 3, 5, 6, 7):
            px, py, pc = x ^ (k >> 2), y ^ ((k >> 1) & 1), c ^ (k & 1)
            peer = 4 * px + 2 * py + pc
            for t in range(nt):
                src = ins[t].at[peer] if scatter else ins[t]
                cp = pltpu.make_async_remote_copy(
                    src_ref=src, dst_ref=outs[t].at[me], send_sem=send_sems.at[t, k], recv_sem=recv_sems.at[t, k],
                    device_id=(px, py, pc), device_id_type=MESH_ID)
                cp.start()
                remote.append(cp)
        for cp in remote:
            cp.wait()
        for cp in copies:
            cp.wait()

    return pl.pallas_call(
        body, in_specs=[ANY] * nt, out_specs=[ANY] * nt,
        out_shape=[jax.ShapeDtypeStruct((N_DEV,) + tuple(s), b.dtype) for s, b in zip(shapes, bufs)],
        scratch_shapes=[pltpu.SemaphoreType.DMA((nt, N_DEV)), pltpu.SemaphoreType.DMA((nt, N_DEV)),
                        pltpu.SemaphoreType.DMA((nt,))],
        name=name, compiler_params=pltpu.CompilerParams(has_side_effects=True))(*bufs)


def _adam(recv, w, m, v, tr, name):
    _, r, c = w.shape
    c1 = np.float32(1.0 - ADAM_B1 ** ADAM_STEP)
    c2 = np.float32(1.0 - ADAM_B2 ** ADAM_STEP)

    def body(r_ref, w_ref, m_ref, v_ref, g_ref, d_ref, mo_ref, vo_ref):
        g = r_ref[0].astype(F32)
        for s in range(1, N_DEV):
            g = g + r_ref[s].astype(F32)
        mn = ADAM_B1 * m_ref[0] + (1.0 - ADAM_B1) * g
        vn = ADAM_B2 * v_ref[0] + (1.0 - ADAM_B2) * (g * g)
        g_ref[0] = g
        mo_ref[0] = mn
        vo_ref[0] = vn
        d_ref[0] = -ADAM_LR * ((mn / c1) / (jnp.sqrt(vn / c2) + ADAM_EPS) + ADAM_WD * w_ref[0])

    one = pl.BlockSpec((1, tr, c), lambda i: (0, i, 0))
    return pl.pallas_call(
        body, grid=(r // tr,), in_specs=[pl.BlockSpec((N_DEV, tr, c), lambda i: (0, i, 0)), one, one, one],
        out_specs=[one] * 4, out_shape=[jax.ShapeDtypeStruct((1, r, c), F32)] * 4,
        name=name, compiler_params=_cp())(recv, w, m, v)


def _residues(a, dil, width):
    return a.reshape(T // dil, dil * width)


def _local_step(x, target, p):
    g = {}
    x0, h1, gu1, act1, y1 = _ffn_fwd(x, None, 0.0, p["norm_ffn1"], p["wgu1"], p["wd1"], "ffn1")
    x1, h2 = _rms_fwd(x0, y1, 0.5, p["norm_mix"], "mix_norm")
    z_at = _mm_nn(h2, p["win_a"], BF, 1024, 768, "mix_in_attn")
    z_dn = _mm_nn(h2, p["win_d"], F32, 1024, 768, "mix_in_dn")

    qkv_r, os_, ms_, ls_ = [], [], [], []
    for i, (_, dil) in enumerate(PATTERNS):
        qr = _residues(z_at, dil, ZA)
        o, m, l = _attn_fwd(qr, dil, "attn_fwd%d" % i)
        qkv_r.append(qr)
        os_.append(o.reshape(T, AW))
        ms_.append(m.reshape(T, 128))
        ls_.append(l.reshape(T, 128))
    attn_f, attn_b, lse = _attn_merge(os_, ms_, ls_, "attn_merge")

    conv, qkvn, bg = _dn_pre_fwd(z_dn, p["conv_w8"], p["alog_row"], p["dtb_row"], "dn_pre")
    prep = _dn_prep_fwd(qkvn, bg, "dn_prep")
    o_dn, states = _dn_scan_fwd(prep, "dn_scan")
    dn_b = _dn_post_fwd(o_dn, z_dn, p["dn_norm"], "dn_post")

    mix = jnp.concatenate([attn_b, dn_b], axis=1)
    y2 = _mm_nn(mix, p["wout"], F32, 1024, 1024, "mix_out")
    x2, h3, gu2, act2, y3 = _ffn_fwd(x1, y2, 1.0, p["norm_ffn2"], p["wgu2"], p["wd2"], "ffn2")
    x3, _ = _rms_fwd(x2, y3, 0.5, p["norm_final"], "final_add")

    dx3, dys3, g["norm_final"], loss8 = _loss_bwd(x3, p["norm_final"], target, "loss")
    dx2, dx2b, g["norm_ffn2"], g["wgu2"], g["wd2"] = _ffn_bwd(
        x2, p["norm_ffn2"], h3, gu2, act2, p["wgu2"], p["wd2"], dx3, dys3, 1.0, "ffn2b")

    dmix = _mm_nt(dx2b, p["wout"], F32, 1024, 1024, "mix_out_dx")
    g["wout"] = _mm_tn(mix, dx2b, 512, 1024, "mix_out_dw")
    d_attn = dmix[:, :AW]
    d_dn = dmix[:, AW:]

    dl = _attn_delta(d_attn, attn_f, "attn_delta")
    parts = []
    for i, (_, dil) in enumerate(PATTERNS):
        do_r = _residues(d_attn, dil, AW)
        lse_r = _residues(lse, dil, 128)
        dl_r = _residues(dl, dil, 128)
        dq = _attn_bwd_dq(qkv_r[i], do_r, lse_r, dl_r, dil, "attn_dq%d" % i)
        dk, dv = _attn_bwd_dkv(qkv_r[i], do_r, lse_r, dl_r, dil, "attn_dkv%d" % i)
        parts += [dq.reshape(T, AW), dk.reshape(T, AW), dv.reshape(T, AW)]
    dz_at = _sum3_cast(parts, "attn_dsum")

    do_dn, dgate, g["dn_norm"] = _dn_post_bwd(o_dn, z_dn, p["dn_norm"], d_dn, "dn_post_b")
    cts = _dn_scan_bwd(prep, states, do_dn, "dn_scan_b")
    dqn, dkn, dvn, dbg = _dn_prep_bwd(qkvn, bg, cts, "dn_prep_b")
    dconv, dbd, g["alog_row"], g["dtb_row"] = _dn_pre_bwd(
        conv, z_dn, p["alog_row"], p["dtb_row"], dqn, dkn, dvn, dbg, "dn_pre_b")
    draw, g["conv_w8"] = _dn_conv_bwd(dconv, z_dn, p["conv_w8"], "dn_conv_b")
    dz_dn = _dz_dn_assemble(draw, dgate, dbd, "dn_dz")

    dh2 = _mm_nt(dz_at, p["win_a"], F32, 1024, 1024, "mix_in_dx_a")
    dh2d = _mm_nt(dz_dn, p["win_d"], F32, 1024, 1024, "mix_in_dx_d")
    g["win_a"] = _mm_tn(h2, dz_at, 512, 768, "mix_in_dw_a")
    g["win_d"] = _mm_tn(h2, dz_dn, 512, 768, "mix_in_dw_d")
    dx1, _, g["norm_mix"] = _rms_bwd2(x1, p["norm_mix"], dh2, dh2d, dx2, "mix_dnorm")
    dys1 = None
    dx0, _, g["norm_ffn1"], g["wgu1"], g["wd1"] = _ffn_bwd(
        x0, p["norm_ffn1"], h1, gu1, act1, p["wgu1"], p["wd1"], dx1[0], dx1[1], 1.0, "ffn1b")
    return loss8[0, 0], dx0, g


def _rms_bwd2(x, gain, dh_a, dh_b, dres, name):
    tm = 512
    row = pl.BlockSpec((tm, D), lambda i: (i, 0))
    gspec = pl.BlockSpec((1, D), lambda i: (0, 0))

    def body(x_ref, g_ref, da_ref, db_ref, dres_ref, dx_ref, dxs_ref, dg_ref):
        i = pl.program_id(0)
        xv = x_ref[...]
        r = lax.rsqrt(jnp.mean(xv * xv, axis=-1, keepdims=True) + NORM_EPS)
        xh = xv * r
        dhv = da_ref[...] + db_ref[...]
        part = jnp.sum(dhv * xh, axis=0, keepdims=True)

        @pl.when(i == 0)
        def _():
            dg_ref[...] = part

        @pl.when(i > 0)
        def _():
            dg_ref[...] += part

        dxh = dhv * g_ref[...]
        dx = r * (dxh - xh * jnp.mean(dxh * xh, axis=-1, keepdims=True)) + dres_ref[...]
        dx_ref[...] = dx
        dxs_ref[...] = (0.5 * dx).astype(BF)

    dx, dxs, dg = pl.pallas_call(
        body, grid=(T // tm,), in_specs=[row, gspec, row, row, row], out_specs=[row, row, gspec],
        out_shape=[jax.ShapeDtypeStruct((T, D), F32), jax.ShapeDtypeStruct((T, D), BF),
                   jax.ShapeDtypeStruct((1, D), F32)],
        name=name, compiler_params=_cp())(x, gain, dh_a, dh_b, dres)
    return (dx, dxs), None, dg


def _cols_from_shards(gathered):
    n, r, c = gathered.shape
    return jnp.transpose(gathered, (1, 0, 2)).reshape(r, n * c)


def _shards_from_cols(full, dtype):
    r, nc = full.shape
    return jnp.transpose(full.reshape(r, N_DEV, nc // N_DEV), (1, 0, 2)).astype(dtype)


def _lane_row(vec4):
    return jnp.zeros((1, 128), F32).at[:, DN_H:2 * DN_H].set(vec4.astype(F32))


def _build_params(gath, norm_ffn1, norm_mix, norm_ffn2, norm_final, a_log, dt_bias, dn_norm):
    p = {}
    for i in ("1", "2"):
        p["wgu" + i] = jnp.concatenate(
            [_cols_from_shards(gath["gate" + i]), _cols_from_shards(gath["up" + i])], axis=1)
        p["wd" + i] = gath["down" + i].reshape(F, D)
    w_in = _cols_from_shards(gath["w_in"])
    c0 = 3 * AW + 3 * DNW
    wp = jnp.concatenate([w_in[:, :c0], w_in[:, c0 + 2 * DN_H:], w_in[:, c0:c0 + 2 * DN_H],
                          jnp.zeros((D, ZP - IN_COLS), w_in.dtype)], axis=1)
    p["win_a"] = wp[:, :ZA]
    p["win_d"] = wp[:, ZA:]
    p["wout"] = gath["w_out"].reshape(D, D)
    conv = _cols_from_shards(gath["conv_w"])
    p["conv_w8"] = jnp.concatenate([conv, jnp.zeros((4, 3 * DNW), F32)], axis=0)
    p["norm_ffn1"], p["norm_mix"], p["norm_ffn2"] = norm_ffn1, norm_mix, norm_ffn2
    p["norm_final"] = norm_final.reshape(1, D)
    p["alog_row"] = _lane_row(a_log)
    p["dtb_row"] = _lane_row(dt_bias)
    p["dn_norm"] = dn_norm
    return p


def _grad_slabs(g):
    s = {}
    for i in ("1", "2"):
        s["gate" + i] = _shards_from_cols(g["wgu" + i][:, :F], BF)
        s["up" + i] = _shards_from_cols(g["wgu" + i][:, F:], BF)
        s["down" + i] = g["wd" + i].reshape(N_DEV, F // N_DEV, D).astype(BF)
    gp = jnp.concatenate([g["win_a"], g["win_d"]], axis=1)
    c0 = 3 * AW + 3 * DNW
    g_in = jnp.concatenate([gp[:, :c0], gp[:, c0 + DNW:c0 + DNW + 2 * DN_H], gp[:, c0:c0 + DNW]], axis=1)
    s["w_in"] = _shards_from_cols(g_in, BF)
    s["w_out"] = g["wout"].reshape(N_DEV, D // N_DEV, D).astype(BF)
    s["conv_w"] = _shards_from_cols(g["conv_w8"][:4], F32)
    return s


SMALL_ROWS = 40


def _small_pack(norm_ffn1, norm_mix, norm_ffn2, norm_final, dn_norm, alog_row, dtb_row):
    rows = [a.reshape(8, 128) for a in (norm_ffn1, norm_mix, norm_ffn2, norm_final)]
    rows += [dn_norm.reshape(1, 128), alog_row, dtb_row, jnp.zeros((SMALL_ROWS - 35, 128), F32)]
    return jnp.concatenate(rows, axis=0)


def _small_unpack(pk):
    pk = pk[0]
    return (pk[0:8].reshape(1, D), pk[8:16].reshape(1, D), pk[16:24].reshape(1, D), pk[24:32].reshape(D),
            pk[32:33], pk[33:34, DN_H:2 * DN_H], pk[34:35, DN_H:2 * DN_H])


ADAM_TILE = {"gate1": 256, "up1": 256, "down1": 176, "gate2": 256, "up2": 256, "down2": 176,
             "w_in": 256, "w_out": 128, "conv_w": 4}
BIG = ("gate1", "up1", "down1", "w_in", "w_out", "gate2", "up2", "down2", "conv_w")


def kernel(x, norm_ffn1, ffn1_gate, ffn1_up, ffn1_down, norm_mix, w_in, conv_w, a_log, dt_bias, dn_norm, w_out, norm_ffn2, ffn2_gate, ffn2_up, ffn2_down, norm_final, loss_target, m_norm_ffn1, m_ffn1_gate, m_ffn1_up, m_ffn1_down, m_norm_mix, m_w_in, m_conv_w, m_a_log, m_dt_bias, m_dn_norm, m_w_out, m_norm_ffn2, m_ffn2_gate, m_ffn2_up, m_ffn2_down, m_norm_final, v_norm_ffn1, v_ffn1_gate, v_ffn1_up, v_ffn1_down, v_norm_mix, v_w_in, v_conv_w, v_a_log, v_dt_bias, v_dn_norm, v_w_out, v_norm_ffn2, v_ffn2_gate, v_ffn2_up, v_ffn2_down, v_norm_final):
    w = {"gate1": ffn1_gate, "up1": ffn1_up, "down1": ffn1_down, "w_in": w_in, "w_out": w_out,
         "gate2": ffn2_gate, "up2": ffn2_up, "down2": ffn2_down, "conv_w": conv_w}
    m = {"gate1": m_ffn1_gate, "up1": m_ffn1_up, "down1": m_ffn1_down, "w_in": m_w_in, "w_out": m_w_out,
         "gate2": m_ffn2_gate, "up2": m_ffn2_up, "down2": m_ffn2_down, "conv_w": m_conv_w}
    v = {"gate1": v_ffn1_gate, "up1": v_ffn1_up, "down1": v_ffn1_down, "w_in": v_w_in, "w_out": v_w_out,
         "gate2": v_ffn2_gate, "up2": v_ffn2_up, "down2": v_ffn2_down, "conv_w": v_conv_w}

    shards = [w[n][0].astype(F32 if n == "conv_w" else BF) for n in BIG]
    gath = dict(zip(BIG, _exchange(shards, False, "weights_allgather")))
    p = _build_params(gath, norm_ffn1, norm_mix, norm_ffn2, norm_final, a_log, dt_bias, dn_norm)

    loss_part, dx, g = _local_step(x[0], loss_target[0], p)

    slabs = _grad_slabs(g)
    small = _small_pack(g["norm_ffn1"], g["norm_mix"], g["norm_ffn2"], g["norm_final"], g["dn_norm"],
                        g["alog_row"], g["dtb_row"])
    sends = [slabs[n] for n in BIG] + [jnp.broadcast_to(small[None], (N_DEV, SMALL_ROWS, 128))]
    recv = _exchange(sends, True, "grads_exchange")
    res = {n: _adam(recv[i], w[n], m[n], v[n], ADAM_TILE[n], "adam_" + n) for i, n in enumerate(BIG)}
    pack = lambda a: _small_pack(*a)[None]
    res_s = _adam(
        recv[-1],
        pack((norm_ffn1, norm_mix, norm_ffn2, norm_final, dn_norm, _lane_row(a_log), _lane_row(dt_bias))),
        pack((m_norm_ffn1, m_norm_mix, m_norm_ffn2, m_norm_final, m_dn_norm, _lane_row(m_a_log), _lane_row(m_dt_bias))),
        pack((v_norm_ffn1, v_norm_mix, v_norm_ffn2, v_norm_final, v_dn_norm, _lane_row(v_a_log), _lane_row(v_dt_bias))),
        SMALL_ROWS, "adam_small")

    loss = lax.psum(loss_part, ("x", "y", "c"))
    outs = [loss, dx[None]]
    for k in range(4):
        n1, nm, n2, nf, dn, al, dt = _small_unpack(res_s[k])
        big = {n: res[n][k] for n in BIG}
        outs += [n1, big["gate1"], big["up1"], big["down1"], nm, big["w_in"], big["conv_w"], al, dt, dn,
                 big["w_out"], n2, big["gate2"], big["up2"], big["down2"], nf]
    return tuple(outs)
```

```python
import functools

import numpy as np
import jax
import jax.numpy as jnp
from jax import lax
from jax.experimental import pallas as pl
from jax.experimental.pallas import tpu as pltpu

T = 4096
D = 1024
F = 2816
N_DEV = 8
A_HEADS = 8
A_HD = 64
AW = A_HEADS * A_HD
DN_H = 4
DN_HD = 128
DNW = DN_H * DN_HD
CH = 64
PAIR = 2 * CH
ZA = 3 * AW
ZD = 3 * DNW + DNW + 256
ZP = ZA + ZD
BD_BLK = (3 * DNW + DNW) // 128
IN_COLS = 3592
PATTERNS = ((128, 1), (512, 4), (2048, 16))
NORM_EPS = 1e-6
L2_EPS = 1e-6
ADAM_LR, ADAM_B1, ADAM_B2, ADAM_EPS, ADAM_WD, ADAM_STEP = 0.001, 0.9, 0.999, 1e-08, 0.01, 10
VMEM_LIMIT = 56 * 1024 * 1024
NEG = -1e30

BF = jnp.bfloat16
F32 = jnp.float32
NN = (((1,), (0,)), ((), ()))
NT = (((1,), (1,)), ((), ()))
TN = (((0,), (0,)), ((), ()))
HI = lax.Precision.HIGHEST
MESH_ID = pl.DeviceIdType.MESH
ANY = pl.BlockSpec(memory_space=pl.ANY)


def _cp():
    return pltpu.CompilerParams(vmem_limit_bytes=VMEM_LIMIT)


def _dg(a, b, dims):
    return lax.dot_general(a, b, dims, preferred_element_type=F32)


def _hdot(a, b):
    return lax.dot_general(a, b, NN, precision=HI, preferred_element_type=F32)


def _make_bdot(dims, da_dims, da_swap, db_dims, db_swap):
    @jax.custom_vjp
    def f(a, b):
        return _dg(a.astype(BF), b.astype(BF), dims)

    def fwd(a, b):
        return f(a, b), (a, b)

    def bwd(res, g):
        a, b = res
        gb, ab, bb = g.astype(BF), a.astype(BF), b.astype(BF)
        da = _dg(bb, gb, da_dims) if da_swap else _dg(gb, bb, da_dims)
        db = _dg(gb, ab, db_dims) if db_swap else _dg(ab, gb, db_dims)
        return da, db

    f.defvjp(fwd, bwd)
    return f


_bdot_nn = _make_bdot(NN, NT, False, TN, False)
_bdot_nt = _make_bdot(NT, NN, False, TN, True)
_bdot_tn = _make_bdot(TN, NT, True, NN, False)


def _iota(shape, dim):
    return lax.broadcasted_iota(jnp.int32, shape, dim)


def _col(x, idx):
    return jnp.sum(jnp.where(_iota(x.shape, 1) == idx, x, 0.0), axis=1, keepdims=True)


def _mm_nn(a, b, out_dtype, tm, tn, name):
    m, k = a.shape
    n = b.shape[1]

    def body(a_ref, b_ref, o_ref):
        o_ref[...] = _dg(a_ref[...], b_ref[...], NN).astype(out_dtype)

    return pl.pallas_call(
        body, grid=(m // tm, n // tn),
        in_specs=[pl.BlockSpec((tm, k), lambda i, j: (i, 0)), pl.BlockSpec((k, tn), lambda i, j: (0, j))],
        out_specs=pl.BlockSpec((tm, tn), lambda i, j: (i, j)),
        out_shape=jax.ShapeDtypeStruct((m, n), out_dtype), name=name, compiler_params=_cp())(a, b)


def _mm_nt(a, b, out_dtype, tm, tb, name):
    m, c = a.shape
    kb = b.shape[0]

    def body(a_ref, b_ref, o_ref):
        o_ref[...] = _dg(a_ref[...], b_ref[...], NT).astype(out_dtype)

    return pl.pallas_call(
        body, grid=(m // tm, kb // tb),
        in_specs=[pl.BlockSpec((tm, c), lambda i, j: (i, 0)), pl.BlockSpec((tb, c), lambda i, j: (j, 0))],
        out_specs=pl.BlockSpec((tm, tb), lambda i, j: (i, j)),
        out_shape=jax.ShapeDtypeStruct((m, kb), out_dtype), name=name, compiler_params=_cp())(a, b)


def _mm_tn(a, b, ta, tb, name):
    m, ka = a.shape
    nb = b.shape[1]

    def body(a_ref, b_ref, o_ref):
        o_ref[...] = _dg(a_ref[...], b_ref[...], TN)

    return pl.pallas_call(
        body, grid=(ka // ta, nb // tb),
        in_specs=[pl.BlockSpec((m, ta), lambda i, j: (0, i)), pl.BlockSpec((m, tb), lambda i, j: (0, j))],
        out_specs=pl.BlockSpec((ta, tb), lambda i, j: (i, j)),
        out_shape=jax.ShapeDtypeStruct((ka, nb), F32), name=name, compiler_params=_cp())(a, b)


def _rms_fwd(x, y, alpha, gain, name):
    tm = 512
    has_y = y is not None
    row = pl.BlockSpec((tm, D), lambda i: (i, 0))
    gspec = pl.BlockSpec((1, D), lambda i: (0, 0))

    def body(*refs):
        if has_y:
            x_ref, y_ref, g_ref, xo_ref, h_ref = refs
            xv = x_ref[...] + alpha * y_ref[...]
            xo_ref[...] = xv
        else:
            x_ref, g_ref, h_ref = refs
            xv = x_ref[...]
        r = lax.rsqrt(jnp.mean(xv * xv, axis=-1, keepdims=True) + NORM_EPS)
        h_ref[...] = (xv * r * g_ref[...]).astype(BF)

    if has_y:
        return pl.pallas_call(
            body, grid=(T // tm,), in_specs=[row, row, gspec], out_specs=[row, row],
            out_shape=[jax.ShapeDtypeStruct((T, D), F32), jax.ShapeDtypeStruct((T, D), BF)],
            name=name, compiler_params=_cp())(x, y, gain)
    h = pl.pallas_call(
        body, grid=(T // tm,), in_specs=[row, gspec], out_specs=row,
        out_shape=jax.ShapeDtypeStruct((T, D), BF), name=name, compiler_params=_cp())(x, gain)
    return x, h


def _rms_bwd(x, gain, dh, dres, alpha_out, name):
    tm = 512
    row = pl.BlockSpec((tm, D), lambda i: (i, 0))
    gspec = pl.BlockSpec((1, D), lambda i: (0, 0))

    def body(x_ref, g_ref, dh_ref, dres_ref, dx_ref, dxs_ref, dg_ref):
        i = pl.program_id(0)
        xv = x_ref[...]
        r = lax.rsqrt(jnp.mean(xv * xv, axis=-1, keepdims=True) + NORM_EPS)
        xh = xv * r
        dhv = dh_ref[...].astype(F32)
        part = jnp.sum(dhv * xh, axis=0, keepdims=True)

        @pl.when(i == 0)
        def _():
            dg_ref[...] = part

        @pl.when(i > 0)
        def _():
            dg_ref[...] += part

        dxh = dhv * g_ref[...]
        dx = r * (dxh - xh * jnp.mean(dxh * xh, axis=-1, keepdims=True)) + dres_ref[...]
        dx_ref[...] = dx
        dxs_ref[...] = (alpha_out * dx).astype(BF)

    return pl.pallas_call(
        body, grid=(T // tm,), in_specs=[row, gspec, row, row], out_specs=[row, row, gspec],
        out_shape=[jax.ShapeDtypeStruct((T, D), F32), jax.ShapeDtypeStruct((T, D), BF),
                   jax.ShapeDtypeStruct((1, D), F32)],
        name=name, compiler_params=_cp())(x, gain, dh, dres)


def _loss_bwd(x, gain, target, name):
    tm = 512
    row = pl.BlockSpec((tm, D), lambda i: (i, 0))
    gspec = pl.BlockSpec((1, D), lambda i: (0, 0))
    lspec = pl.BlockSpec((8, 128), lambda i: (0, 0))

    def body(x_ref, g_ref, t_ref, dx_ref, dxs_ref, dg_ref, loss_ref):
        i = pl.program_id(0)
        xv = x_ref[...]
        r = lax.rsqrt(jnp.mean(xv * xv, axis=-1, keepdims=True) + NORM_EPS)
        xh = xv * r
        diff = xh * g_ref[...] - t_ref[...]
        lpart = 0.5 * jnp.sum(jnp.mean(diff * diff, axis=-1, keepdims=True), axis=0, keepdims=True)
        dy = diff * (1.0 / D)
        part = jnp.sum(dy * xh, axis=0, keepdims=True)

        @pl.when(i == 0)
        def _():
            dg_ref[...] = part
            loss_ref[...] = jnp.broadcast_to(lpart, (8, 128))

        @pl.when(i > 0)
        def _():
            dg_ref[...] += part
            loss_ref[...] += jnp.broadcast_to(lpart, (8, 128))

        dxh = dy * g_ref[...]
        dx = r * (dxh - xh * jnp.mean(dxh * xh, axis=-1, keepdims=True))
        dx_ref[...] = dx
        dxs_ref[...] = (0.5 * dx).astype(BF)

    return pl.pallas_call(
        body, grid=(T // tm,), in_specs=[row, gspec, row], out_specs=[row, row, gspec, lspec],
        out_shape=[jax.ShapeDtypeStruct((T, D), F32), jax.ShapeDtypeStruct((T, D), BF),
                   jax.ShapeDtypeStruct((1, D), F32), jax.ShapeDtypeStruct((8, 128), F32)],
        name=name, compiler_params=_cp())(x, gain, target)


def _swiglu_fwd(gu, name):
    tm = 256

    def body(gu_ref, a_ref):
        g = gu_ref[:, :F].astype(F32)
        u = gu_ref[:, F:].astype(F32)
        a_ref[...] = (g * jax.nn.sigmoid(g) * u).astype(BF)

    return pl.pallas_call(
        body, grid=(T // tm,), in_specs=[pl.BlockSpec((tm, 2 * F), lambda i: (i, 0))],
        out_specs=pl.BlockSpec((tm, F), lambda i: (i, 0)),
        out_shape=jax.ShapeDtypeStruct((T, F), BF), name=name, compiler_params=_cp())(gu)


def _swiglu_bwd(gu, da, name):
    tm = 256

    def body(gu_ref, da_ref, o_ref):
        g = gu_ref[:, :F].astype(F32)
        u = gu_ref[:, F:].astype(F32)
        dav = da_ref[...].astype(F32)
        s = jax.nn.sigmoid(g)
        o_ref[:, :F] = (dav * u * (s * (1.0 + g * (1.0 - s)))).astype(BF)
        o_ref[:, F:] = (dav * (g * s)).astype(BF)

    return pl.pallas_call(
        body, grid=(T // tm,),
        in_specs=[pl.BlockSpec((tm, 2 * F), lambda i: (i, 0)), pl.BlockSpec((tm, F), lambda i: (i, 0))],
        out_specs=pl.BlockSpec((tm, 2 * F), lambda i: (i, 0)),
        out_shape=jax.ShapeDtypeStruct((T, 2 * F), BF), name=name, compiler_params=_cp())(gu, da)


def _ffn_fwd(x_prev, y_prev, alpha, gain, wgu, wd, tag):
    x_in, h = _rms_fwd(x_prev, y_prev, alpha, gain, tag + "_norm")
    gu = _mm_nn(h, wgu, BF, 1024, 1408, tag + "_gu")
    act = _swiglu_fwd(gu, tag + "_act")
    y = _mm_nn(act, wd, F32, 1024, 1024, tag + "_down")
    return x_in, h, gu, act, y


def _ffn_bwd(x_in, gain, h, gu, act, wgu, wd, dxo, dys, alpha_out, tag):
    dact = _mm_nt(dys, wd, BF, 1024, 1408, tag + "_dact")
    dgu = _swiglu_bwd(gu, dact, tag + "_dgu")
    dh = _mm_nt(dgu, wgu, F32, 512, 512, tag + "_dh")
    dwgu = _mm_tn(h, dgu, 512, 512, tag + "_dwgu")
    dwd = _mm_tn(act, dys, 256, 1024, tag + "_dwd")
    dx, dxs, dgain = _rms_bwd(x_in, gain, dh, dxo, alpha_out, tag + "_dnorm")
    return dx, dxs, dgain, dwgu, dwd


def _slope(h):
    return float(2.0 ** (-(h + 1)))


def _pair_masks():
    lane = _iota((128, 128), 1)
    return lane < A_HD, lane >= A_HD


def _attn_fwd(qkv_r, dil, name):
    L = T // dil
    nb = L // 128

    def body(q_ref, kp_ref, kc_ref, vp_ref, vc_ref, o_ref, m_ref, l_ref):
        n = pl.program_id(1)
        qi = _iota((128, 256), 0)
        kj = _iota((128, 256), 1)
        steps = qi + 128 - kj
        valid = (steps >= 0) & (steps <= 128) & ((n * 128 + kj - 128) >= 0)
        dist = (steps * dil).astype(F32)
        lane = _iota((128, 128), 1)
        lo, hi = _pair_masks()
        m_acc = jnp.zeros((128, 128), F32)
        l_acc = jnp.ones((128, 128), F32)
        for j in range(A_HEADS // 2):
            sl = slice(128 * j, 128 * j + 128)
            qp = q_ref[:, sl]
            kcat = jnp.concatenate([kp_ref[:, sl], kc_ref[:, sl]], axis=0)
            vcat = jnp.concatenate([vp_ref[:, sl], vc_ref[:, sl]], axis=0)
            outs = []
            for e in range(2):
                h = 2 * j + e
                qm = jnp.where(lo if e == 0 else hi, qp, jnp.zeros_like(qp))
                s = _dg(qm, kcat, NT) * (A_HD ** -0.5) - _slope(h) * dist
                s = jnp.where(valid, s, NEG)
                m = jnp.max(s, axis=1, keepdims=True)
                p = jnp.exp(s - m)
                l = jnp.sum(p, axis=1, keepdims=True)
                outs.append(_dg(p.astype(BF), vcat, NN) / l)
                m_acc = jnp.where(lane == h, m, m_acc)
                l_acc = jnp.where(lane == h, l, l_acc)
            o_ref[:, sl] = jnp.where(lo, outs[0], outs[1])
        m_ref[...] = m_acc
        l_ref[...] = l_acc

    blk = lambda col: pl.BlockSpec((128, AW), lambda r, n: (n, 3 * r + col))
    prv = lambda col: pl.BlockSpec((128, AW), lambda r, n: (jnp.maximum(n - 1, 0), 3 * r + col))
    return pl.pallas_call(
        body, grid=(dil, nb),
        in_specs=[blk(0), prv(1), blk(1), prv(2), blk(2)],
        out_specs=[pl.BlockSpec((128, AW), lambda r, n: (n, r)),
                   pl.BlockSpec((128, 128), lambda r, n: (n, r)),
                   pl.BlockSpec((128, 128), lambda r, n: (n, r))],
        out_shape=[jax.ShapeDtypeStruct((L, dil * AW), F32), jax.ShapeDtypeStruct((L, dil * 128), F32),
                   jax.ShapeDtypeStruct((L, dil * 128), F32)],
        name=name, compiler_params=_cp())(qkv_r, qkv_r, qkv_r, qkv_r, qkv_r)


def _expand_mat():
    r = _iota((128, AW), 0)
    c = _iota((128, AW), 1)
    return jnp.where((c // A_HD) == r, 1.0, 0.0).astype(F32)


def _attn_merge(os, ms, ls, name):
    tm = 256
    row = pl.BlockSpec((tm, AW), lambda i: (i, 0))
    st = pl.BlockSpec((tm, 128), lambda i: (i, 0))

    def body(o1, o2, o3, m1, m2, m3, l1, l2, l3, of_ref, ob_ref, lse_ref):
        mx = jnp.maximum(jnp.maximum(m1[...], m2[...]), m3[...])
        w = [l[...] * jnp.exp(m[...] - mx) for m, l in ((m1, l1), (m2, l2), (m3, l3))]
        den = w[0] + w[1] + w[2]
        ex = _expand_mat()
        acc = jnp.zeros((tm, AW), F32)
        for wp, op in zip(w, (o1, o2, o3)):
            acc = acc + _hdot(wp, ex) * op[...]
        out = acc / _hdot(den, ex)
        of_ref[...] = out
        ob_ref[...] = out.astype(BF)
        lse_ref[...] = mx + jnp.log(den)

    return pl.pallas_call(
        body, grid=(T // tm,), in_specs=[row] * 3 + [st] * 6, out_specs=[row, row, st],
        out_shape=[jax.ShapeDtypeStruct((T, AW), F32), jax.ShapeDtypeStruct((T, AW), BF),
                   jax.ShapeDtypeStruct((T, 128), F32)],
        name=name, compiler_params=_cp())(*os, *ms, *ls)


def _attn_delta(dout, out, name):
    tm = 256
    row = pl.BlockSpec((tm, AW), lambda i: (i, 0))

    def body(do_ref, o_ref, d_ref):
        d_ref[...] = lax.dot_general(do_ref[...] * o_ref[...], _expand_mat(), NT, precision=HI,
                                     preferred_element_type=F32)

    return pl.pallas_call(
        body, grid=(T // tm,), in_specs=[row, row], out_specs=pl.BlockSpec((tm, 128), lambda i: (i, 0)),
        out_shape=jax.ShapeDtypeStruct((T, 128), F32), name=name, compiler_params=_cp())(dout, out)


def _attn_bwd_dq(qkv_r, do_r, lse_r, dl_r, dil, name):
    L = T // dil
    nb = L // 128

    def body(q_ref, kp_ref, kc_ref, vp_ref, vc_ref, do_ref, lse_ref, dl_ref, dq_ref):
        n = pl.program_id(1)
        qi = _iota((128, 256), 0)
        kj = _iota((128, 256), 1)
        steps = qi + 128 - kj
        valid = (steps >= 0) & (steps <= 128) & ((n * 128 + kj - 128) >= 0)
        dist = (steps * dil).astype(F32)
        lo, hi = _pair_masks()
        lse = lse_ref[...]
        dl = dl_ref[...]
        for j in range(A_HEADS // 2):
            sl = slice(128 * j, 128 * j + 128)
            qp = q_ref[:, sl]
            dop = do_ref[:, sl].astype(BF)
            kcat = jnp.concatenate([kp_ref[:, sl], kc_ref[:, sl]], axis=0)
            vcat = jnp.concatenate([vp_ref[:, sl], vc_ref[:, sl]], axis=0)
            outs = []
            for e in range(2):
                h = 2 * j + e
                msk = lo if e == 0 else hi
                qm = jnp.where(msk, qp, jnp.zeros_like(qp))
                dom = jnp.where(msk, dop, jnp.zeros_like(dop))
                s = _dg(qm, kcat, NT) * (A_HD ** -0.5) - _slope(h) * dist
                p = jnp.where(valid, jnp.exp(jnp.where(valid, s, NEG) - _col(lse, h)), 0.0)
                dp = _dg(dom, vcat, NT)
                ds = p * (dp - _col(dl, h))
                outs.append(_dg(ds.astype(BF), kcat, NN) * (A_HD ** -0.5))
            dq_ref[:, sl] = jnp.where(lo, outs[0], outs[1])

    blk = lambda col: pl.BlockSpec((128, AW), lambda r, n: (n, 3 * r + col))
    prv = lambda col: pl.BlockSpec((128, AW), lambda r, n: (jnp.maximum(n - 1, 0), 3 * r + col))
    cur = pl.BlockSpec((128, AW), lambda r, n: (n, r))
    st = pl.BlockSpec((128, 128), lambda r, n: (n, r))
    return pl.pallas_call(
        body, grid=(dil, nb),
        in_specs=[blk(0), prv(1), blk(1), prv(2), blk(2), cur, st, st],
        out_specs=cur, out_shape=jax.ShapeDtypeStruct((L, dil * AW), F32),
        name=name, compiler_params=_cp())(qkv_r, qkv_r, qkv_r, qkv_r, qkv_r, do_r, lse_r, dl_r)


def _attn_bwd_dkv(qkv_r, do_r, lse_r, dl_r, dil, name):
    L = T // dil
    nb = L // 128

    def body(k_ref, v_ref, qc_ref, qn_ref, doc_ref, don_ref, lsec_ref, lsen_ref, dlc_ref, dln_ref,
             dk_ref, dv_ref):
        mblk = pl.program_id(1)
        qi = _iota((128, 128), 0)
        kj = _iota((128, 128), 1)
        steps_c = qi - kj
        steps_n = qi + 128 - kj
        valid_c = steps_c >= 0
        valid_n = (steps_n <= 128) & (mblk + 1 < nb)
        lo, hi = _pair_masks()
        for j in range(A_HEADS // 2):
            sl = slice(128 * j, 128 * j + 128)
            kp = k_ref[:, sl]
            vp = v_ref[:, sl]
            dk = jnp.zeros((128, 128), F32)
            dv = jnp.zeros((128, 128), F32)
            for q_ref, do_ref, lse_ref, dl_ref, steps, valid in (
                    (qc_ref, doc_ref, lsec_ref, dlc_ref, steps_c, valid_c),
                    (qn_ref, don_ref, lsen_ref, dln_ref, steps_n, valid_n)):
                qp = q_ref[:, sl]
                dop = do_ref[:, sl].astype(BF)
                dist = (steps * dil).astype(F32)
                lse = lse_ref[...]
                dl = dl_ref[...]
                for e in range(2):
                    h = 2 * j + e
                    msk = lo if e == 0 else hi
                    qm = jnp.where(msk, qp, jnp.zeros_like(qp))
                    dom = jnp.where(msk, dop, jnp.zeros_like(dop))
                    s = _dg(qm, kp, NT) * (A_HD ** -0.5) - _slope(h) * dist
                    p = jnp.where(valid, jnp.exp(jnp.where(valid, s, NEG) - _col(lse, h)), 0.0)
                    dp = _dg(dom, vp, NT)
                    ds = p * (dp - _col(dl, h))
                    dv = dv + _dg(p.astype(BF), dom, TN)
                    dk = dk + _dg(ds.astype(BF), qm, TN) * (A_HD ** -0.5)
            dk_ref[:, sl] = dk
            dv_ref[:, sl] = dv

    cur3 = lambda col: pl.BlockSpec((128, AW), lambda r, m: (m, 3 * r + col))
    nxt3 = lambda col: pl.BlockSpec((128, AW), lambda r, m: (jnp.minimum(m + 1, nb - 1), 3 * r + col))
    cur = pl.BlockSpec((128, AW), lambda r, m: (m, r))
    nxt = pl.BlockSpec((128, AW), lambda r, m: (jnp.minimum(m + 1, nb - 1), r))
    stc = pl.BlockSpec((128, 128), lambda r, m: (m, r))
    stn = pl.BlockSpec((128, 128), lambda r, m: (jnp.minimum(m + 1, nb - 1), r))
    return pl.pallas_call(
        body, grid=(dil, nb),
        in_specs=[cur3(1), cur3(2), cur3(0), nxt3(0), cur, nxt, stc, stn, stc, stn],
        out_specs=[cur, cur],
        out_shape=[jax.ShapeDtypeStruct((L, dil * AW), F32), jax.ShapeDtypeStruct((L, dil * AW), F32)],
        name=name, compiler_params=_cp())(qkv_r, qkv_r, qkv_r, qkv_r, do_r, do_r, lse_r, lse_r, dl_r, dl_r)


def _sum3_cast(parts, name):
    tm = 512
    row = pl.BlockSpec((tm, AW), lambda i: (i, 0))

    def body(*refs):
        o_ref = refs[-1]
        for c in range(3):
            acc = refs[c][...] + refs[3 + c][...] + refs[6 + c][...]
            o_ref[:, c * AW:(c + 1) * AW] = acc.astype(BF)

    return pl.pallas_call(
        body, grid=(T // tm,), in_specs=[row] * 9, out_specs=pl.BlockSpec((tm, ZA), lambda i: (i, 0)),
        out_shape=jax.ShapeDtypeStruct((T, ZA), BF), name=name, compiler_params=_cp())(*parts)


def _silu(x):
    return x * jax.nn.sigmoid(x)


def _qk_math(c):
    s = _silu(c)
    return s * lax.rsqrt(jnp.sum(s * s, axis=-1, keepdims=True) + L2_EPS)


def _softplus(x):
    return jnp.maximum(x, 0.0) + jnp.log(1.0 + jnp.exp(-jnp.abs(x)))


def _gate_math(bd, alog_row, dtb_row):
    rows = bd.shape[0]
    lane = _iota(bd.shape, 1)
    beta = jax.nn.sigmoid(bd)
    g = jnp.where((lane >= DN_H) & (lane < 2 * DN_H), -jnp.exp(alog_row) * _softplus(bd + dtb_row), 0.0)
    ri = _iota((rows, rows), 0)
    ci = _iota((rows, rows), 1)
    same = (ri // CH) == (ci // CH)
    li = _iota((128, 128), 0)
    lj = _iota((128, 128), 1)
    to_next_group = jnp.where((lj == li + DN_H) & (li >= DN_H) & (li < 2 * DN_H), 1.0, 0.0)
    gc = _hdot(jnp.where(same & (ci <= ri), 1.0, 0.0), g)
    glast = _hdot(_hdot(jnp.where(same, 1.0, 0.0), g), to_next_group)
    return jnp.where(lane < DN_H, beta, 0.0) + gc + glast


def _shift_down(cur, halo, s):
    if s == 0:
        return cur
    rolled = pltpu.roll(cur, s, 0)
    hr = pltpu.roll(halo, s, 0)
    head = jnp.where(_iota(hr.shape, 0) < s, hr, rolled[:8])
    return jnp.concatenate([head, rolled[8:]], axis=0)


def _shift_up(cur, halo, s):
    if s == 0:
        return cur
    rows = cur.shape[0]
    rolled = pltpu.roll(cur, rows - s, 0)
    hr = pltpu.roll(halo, 8 - s, 0)
    tail = jnp.where(_iota(hr.shape, 0) >= 8 - s, hr, rolled[rows - 8:])
    return jnp.concatenate([rolled[:rows - 8], tail], axis=0)


def _dn_pre_fwd(z_dn, conv_w8, alog_row, dtb_row, name):
    tm = 256
    wq = 3 * DNW

    def body(raw_ref, halo_ref, bd_ref, w_ref, al_ref, dt_ref, conv_ref, qkv_ref, bg_ref):
        i = pl.program_id(0)
        cur = raw_ref[...]
        halo = jnp.where(i > 0, halo_ref[...], 0.0)
        w = w_ref[...]
        conv = jnp.zeros((tm, wq), F32)
        for j in range(4):
            conv = conv + _shift_down(cur, halo, 3 - j) * w[j:j + 1, :]
        conv_ref[...] = conv
        for blk in range(3 * DN_H):
            sl = slice(128 * blk, 128 * blk + 128)
            c = conv[:, sl]
            qkv_ref[:, sl] = _qk_math(c) if blk < 2 * DN_H else _silu(c)
        bg_ref[...] = _gate_math(bd_ref[...], al_ref[...], dt_ref[...])

    one = pl.BlockSpec((1, 128), lambda i: (0, 0))
    return pl.pallas_call(
        body, grid=(T // tm,),
        in_specs=[pl.BlockSpec((tm, wq), lambda i: (i, 0)),
                  pl.BlockSpec((8, wq), lambda i: (jnp.maximum(i * (tm // 8) - 1, 0), 0)),
                  pl.BlockSpec((tm, 128), lambda i: (i, BD_BLK)),
                  pl.BlockSpec((8, wq), lambda i: (0, 0)), one, one],
        out_specs=[pl.BlockSpec((tm, wq), lambda i: (i, 0)), pl.BlockSpec((tm, wq), lambda i: (i, 0)),
                   pl.BlockSpec((tm, 128), lambda i: (i, 0))],
        out_shape=[jax.ShapeDtypeStruct((T, wq), F32), jax.ShapeDtypeStruct((T, wq), F32),
                   jax.ShapeDtypeStruct((T, 128), F32)],
        name=name, compiler_params=_cp())(z_dn, z_dn, z_dn, conv_w8, alog_row, dtb_row)


def _dn_pre_bwd(conv, z_dn, alog_row, dtb_row, dqn, dkn, dvn, dbg, name):
    tm = 256
    wq = 3 * DNW

    def body(conv_ref, bd_ref, al_ref, dt_ref, dq_ref, dk_ref, dv_ref, dbg_ref,
             dconv_ref, dbd_ref, dal_ref, ddt_ref):
        i = pl.program_id(0)
        for blk in range(3 * DN_H):
            sl = slice(128 * blk, 128 * blk + 128)
            src = (dq_ref, dk_ref, dv_ref)[blk // DN_H]
            ct = src[:, 128 * (blk % DN_H):128 * (blk % DN_H) + 128]
            fn = _qk_math if blk < 2 * DN_H else _silu
            _, vjp = jax.vjp(fn, conv_ref[:, sl])
            dconv_ref[:, sl] = vjp(ct)[0]
        _, vjp = jax.vjp(_gate_math, bd_ref[...], al_ref[...], dt_ref[...])
        dbd, dal, ddt = vjp(dbg_ref[...])
        dbd_ref[...] = dbd

        @pl.when(i == 0)
        def _():
            dal_ref[...] = dal
            ddt_ref[...] = ddt

        @pl.when(i > 0)
        def _():
            dal_ref[...] += dal
            ddt_ref[...] += ddt

    one = pl.BlockSpec((1, 128), lambda i: (0, 0))
    row = pl.BlockSpec((tm, wq), lambda i: (i, 0))
    hd = pl.BlockSpec((tm, DNW), lambda i: (i, 0))
    st = pl.BlockSpec((tm, 128), lambda i: (i, 0))
    return pl.pallas_call(
        body, grid=(T // tm,),
        in_specs=[row, pl.BlockSpec((tm, 128), lambda i: (i, BD_BLK)), one, one, hd, hd, hd, st],
        out_specs=[row, st, one, one],
        out_shape=[jax.ShapeDtypeStruct((T, wq), F32), jax.ShapeDtypeStruct((T, 128), F32),
                   jax.ShapeDtypeStruct((1, 128), F32), jax.ShapeDtypeStruct((1, 128), F32)],
        name=name, compiler_params=_cp())(conv, z_dn, alog_row, dtb_row, dqn, dkn, dvn, dbg)


def _dn_conv_bwd(dconv, z_dn, conv_w8, name):
    tm = 256
    wq = 3 * DNW
    last = T // tm - 1

    def body(dc_ref, dcn_ref, raw_ref, halo_ref, w_ref, draw_ref, dw_ref):
        i = pl.program_id(0)
        dc = dc_ref[...]
        nxt = jnp.where(i < last, dcn_ref[...], 0.0)
        cur = raw_ref[...]
        halo = jnp.where(i > 0, halo_ref[...], 0.0)
        w = w_ref[...]
        draw = jnp.zeros((tm, wq), F32)
        rows = []
        for j in range(4):
            draw = draw + _shift_up(dc, nxt, 3 - j) * w[j:j + 1, :]
            rows.append(jnp.sum(dc * _shift_down(cur, halo, 3 - j), axis=0, keepdims=True))
        draw_ref[...] = draw
        part = jnp.concatenate(rows + [jnp.zeros((4, wq), F32)], axis=0)

        @pl.when(i == 0)
        def _():
            dw_ref[...] = part

        @pl.when(i > 0)
        def _():
            dw_ref[...] += part

    row = pl.BlockSpec((tm, wq), lambda i: (i, 0))
    return pl.pallas_call(
        body, grid=(T // tm,),
        in_specs=[row, pl.BlockSpec((8, wq), lambda i: (jnp.minimum((i + 1) * (tm // 8), T // 8 - 1), 0)),
                  row, pl.BlockSpec((8, wq), lambda i: (jnp.maximum(i * (tm // 8) - 1, 0), 0)),
                  pl.BlockSpec((8, wq), lambda i: (0, 0))],
        out_specs=[row, pl.BlockSpec((8, wq), lambda i: (0, 0))],
        out_shape=[jax.ShapeDtypeStruct((T, wq), F32), jax.ShapeDtypeStruct((8, wq), F32)],
        name=name, compiler_params=_cp())(dconv, dconv, z_dn, z_dn, conv_w8)


def _h3(a, b, dims=NN):
    return lax.dot_general(a, b, dims, precision=lax.Precision.HIGH, preferred_element_type=F32)


@jax.custom_vjp
def _inverse_given(a_mat, tinv):
    return tinv


def _inverse_given_fwd(a_mat, tinv):
    return tinv, tinv


def _inverse_given_bwd(tinv, g):
    return -_h3(tinv, _h3(g, tinv, NT), TN), jnp.zeros_like(tinv)


_inverse_given.defvjp(_inverse_given_fwd, _inverse_given_bwd)


def _prep_math(q, k, v, bgc, h, tinv_saved=None):
    beta = _col(bgc, h)
    gc = jnp.broadcast_to(_col(bgc, DN_H + h), (PAIR, 128))
    glast = jnp.broadcast_to(_col(bgc, 2 * DN_H + h), (PAIR, 128))
    ri = _iota((PAIR, PAIR), 0)
    ci = _iota((PAIR, PAIR), 1)
    same = (ri // CH) == (ci // CH)
    causal = same & (ci <= ri)
    strict = same & (ci < ri)
    eye = ri == ci
    gc_cols = _hdot(jnp.ones((PAIR, PAIR), F32), jnp.where(eye, gc, 0.0))
    decay = jnp.exp(jnp.where(causal, gc - gc_cols, NEG))
    egc = jnp.exp(gc)
    kb = k * beta
    a_mat = jnp.where(strict, _bdot_nt(kb, k) * decay, 0.0)
    if tinv_saved is None:
        p = -a_mat
        tinv = jnp.where(eye, 1.0, 0.0) + p
        for _ in range(5):
            p = _h3(p, p)
            tinv = tinv + _h3(tinv, p)
    else:
        tinv = _inverse_given(a_mat, tinv_saved)
    u = _h3(tinv, v * beta)
    w = _h3(tinv, kb * egc)
    qs = q * (DN_HD ** -0.5)
    attn = jnp.where(causal, _bdot_nt(qs, k) * decay, 0.0)
    return u, w, qs * egc, k * jnp.exp(glast - gc), attn, jnp.exp(glast), tinv


def _dn_prep_fwd(qkv, bg, name):
    rows = 512
    hd = lambda off: pl.BlockSpec((rows, 128), lambda g, h: (g, off + h))
    out = pl.BlockSpec((rows, 128), lambda g, h: (g, h))

    def body(q_ref, k_ref, v_ref, bg_ref, *outs):
        h = pl.program_id(1)
        for pr in range(rows // PAIR):
            rs = slice(PAIR * pr, PAIR * pr + PAIR)
            res = _prep_math(q_ref[rs, :], k_ref[rs, :], v_ref[rs, :], bg_ref[rs, :], h)
            for o_ref, val in zip(outs, res):
                o_ref[rs, :] = val

    return pl.pallas_call(
        body, grid=(T // rows, DN_H),
        in_specs=[hd(0), hd(DN_H), hd(2 * DN_H), pl.BlockSpec((rows, 128), lambda g, h: (g, 0))],
        out_specs=[out] * 7, out_shape=[jax.ShapeDtypeStruct((T, DNW), F32)] * 7,
        name=name, compiler_params=_cp())(qkv, qkv, qkv, bg)


def _dn_prep_bwd(qkv, bg, tinv, cts, name):
    rows = 512
    hd = lambda off: pl.BlockSpec((rows, 128), lambda g, h: (g, off + h))
    out = pl.BlockSpec((rows, 128), lambda g, h: (g, h))
    st = pl.BlockSpec((rows, 128), lambda g, h: (g, 0))

    def body(q_ref, k_ref, v_ref, bg_ref, ti_ref, c0, c1, c2, c3, c4, c5, dq_ref, dk_ref, dv_ref, dbg_ref):
        h = pl.program_id(1)
        parts = []
        for pr in range(rows // PAIR):
            rs = slice(PAIR * pr, PAIR * pr + PAIR)
            fn = lambda q, k, v, b, ti=ti_ref[rs, :]: _prep_math(q, k, v, b, h, ti)[:6]
            _, vjp = jax.vjp(fn, q_ref[rs, :], k_ref[rs, :], v_ref[rs, :], bg_ref[rs, :])
            dq, dk, dv, dbg = vjp(tuple(c[rs, :] for c in (c0, c1, c2, c3, c4, c5)))
            dq_ref[rs, :] = dq
            dk_ref[rs, :] = dk
            dv_ref[rs, :] = dv
            parts.append(dbg)
        dbg_all = jnp.concatenate(parts, axis=0)

        @pl.when(h == 0)
        def _():
            dbg_ref[...] = dbg_all

        @pl.when(h > 0)
        def _():
            dbg_ref[...] += dbg_all

    return pl.pallas_call(
        body, grid=(T // rows, DN_H),
        in_specs=[hd(0), hd(DN_H), hd(2 * DN_H), st] + [out] * 7,
        out_specs=[out, out, out, st],
        out_shape=[jax.ShapeDtypeStruct((T, DNW), F32)] * 3 + [jax.ShapeDtypeStruct((T, 128), F32)],
        name=name, compiler_params=_cp())(qkv, qkv, qkv, bg, tinv, *cts)


def _step_math(s, u, w, qg, kdec, attn, decb, sub):
    vnew = u - _bdot_nn(w, s)
    z = jnp.zeros((CH, 128), F32)
    vfull = jnp.concatenate([vnew, z] if sub == 0 else [z, vnew], axis=0)
    o = _bdot_nn(qg, s) + _bdot_nn(attn, vfull)
    dec = jnp.sum(decb, axis=0, keepdims=True) * (1.0 / CH)
    return s * dec + _bdot_tn(kdec, vnew), o


def _dn_scan_fwd(prep, name):
    npair = T // PAIR
    row = pl.BlockSpec((PAIR, DNW), lambda p: (p, 0))

    def body(u_ref, w_ref, qg_ref, kd_ref, at_ref, db_ref, o_ref, ss_ref, s_ref):
        @pl.when(pl.program_id(0) == 0)
        def _():
            s_ref[...] = jnp.zeros_like(s_ref)

        for h in range(DN_H):
            ls = slice(128 * h, 128 * h + 128)
            s = s_ref[h]
            for sub in range(2):
                rs = slice(CH * sub, CH * sub + CH)
                ss_ref[sub, h] = s
                s, o = _step_math(s, u_ref[rs, ls], w_ref[rs, ls], qg_ref[rs, ls], kd_ref[rs, ls],
                                  at_ref[rs, ls], db_ref[rs, ls], sub)
                o_ref[rs, ls] = o
            s_ref[h] = s

    return pl.pallas_call(
        body, grid=(npair,), in_specs=[row] * 6,
        out_specs=[row, pl.BlockSpec((2, DN_H, 128, 128), lambda p: (p, 0, 0, 0))],
        out_shape=[jax.ShapeDtypeStruct((T, DNW), F32), jax.ShapeDtypeStruct((T // CH, DN_H, 128, 128), F32)],
        scratch_shapes=[pltpu.VMEM((DN_H, 128, 128), F32)],
        name=name, compiler_params=_cp())(*prep)


def _dn_scan_bwd(prep, states, do, name):
    npair = T // PAIR
    row = pl.BlockSpec((PAIR, DNW), lambda p: (npair - 1 - p, 0))

    def body(u_ref, w_ref, qg_ref, kd_ref, at_ref, db_ref, ss_ref, do_ref, *rest):
        outs, ds_ref = rest[:6], rest[6]

        @pl.when(pl.program_id(0) == 0)
        def _():
            ds_ref[...] = jnp.zeros_like(ds_ref)

        for h in range(DN_H):
            ls = slice(128 * h, 128 * h + 128)
            ds = ds_ref[h]
            for sub in (1, 0):
                rs = slice(CH * sub, CH * sub + CH)
                args = (ss_ref[sub, h],) + tuple(r[rs, ls] for r in (u_ref, w_ref, qg_ref, kd_ref, at_ref, db_ref))
                _, vjp = jax.vjp(functools.partial(_step_math, sub=sub), *args)
                cts = vjp((ds, do_ref[rs, ls]))
                ds = cts[0]
                for o_ref, val in zip(outs, cts[1:]):
                    o_ref[rs, ls] = val
            ds_ref[h] = ds

    return pl.pallas_call(
        body, grid=(npair,),
        in_specs=[row] * 6 + [pl.BlockSpec((2, DN_H, 128, 128), lambda p: (npair - 1 - p, 0, 0, 0)), row],
        out_specs=[row] * 6, out_shape=[jax.ShapeDtypeStruct((T, DNW), F32)] * 6,
        scratch_shapes=[pltpu.VMEM((DN_H, 128, 128), F32)],
        name=name, compiler_params=_cp())(*prep, states, do)


def _post_math(o, gate, wrow):
    return o * lax.rsqrt(jnp.mean(o * o, axis=-1, keepdims=True) + NORM_EPS) * wrow * _silu(gate)


def _dn_post_fwd(o, z_dn, dn_norm, name):
    tm = 512
    row = pl.BlockSpec((tm, DNW), lambda i: (i, 0))

    def body(o_ref, g_ref, w_ref, y_ref):
        for h in range(DN_H):
            ls = slice(128 * h, 128 * h + 128)
            y_ref[:, ls] = _post_math(o_ref[:, ls], g_ref[:, ls], w_ref[...]).astype(BF)

    return pl.pallas_call(
        body, grid=(T // tm,),
        in_specs=[row, pl.BlockSpec((tm, DNW), lambda i: (i, 3)), pl.BlockSpec((1, 128), lambda i: (0, 0))],
        out_specs=row, out_shape=jax.ShapeDtypeStruct((T, DNW), BF),
        name=name, compiler_params=_cp())(o, z_dn, dn_norm)


def _dn_post_bwd(o, z_dn, dn_norm, dy, name):
    tm = 512
    row = pl.BlockSpec((tm, DNW), lambda i: (i, 0))
    one = pl.BlockSpec((1, 128), lambda i: (0, 0))

    def body(o_ref, g_ref, w_ref, dy_ref, do_ref, dg_ref, dw_ref):
        i = pl.program_id(0)
        dw = jnp.zeros((1, 128), F32)
        for h in range(DN_H):
            ls = slice(128 * h, 128 * h + 128)
            _, vjp = jax.vjp(_post_math, o_ref[:, ls], g_ref[:, ls], w_ref[...])
            do, dg, dwh = vjp(dy_ref[:, ls].astype(F32))
            do_ref[:, ls] = do
            dg_ref[:, ls] = dg
            dw = dw + dwh

        @pl.when(i == 0)
        def _():
            dw_ref[...] = dw

        @pl.when(i > 0)
        def _():
            dw_ref[...] += dw

    return pl.pallas_call(
        body, grid=(T // tm,),
        in_specs=[row, pl.BlockSpec((tm, DNW), lambda i: (i, 3)), one, row],
        out_specs=[row, row, one],
        out_shape=[jax.ShapeDtypeStruct((T, DNW), F32), jax.ShapeDtypeStruct((T, DNW), F32),
                   jax.ShapeDtypeStruct((1, 128), F32)],
        name=name, compiler_params=_cp())(o, z_dn, dn_norm, dy)


def _dz_dn_assemble(draw, dgate, dbd, name):
    tm = 512

    def body(a_ref, b_ref, c_ref, o_ref):
        o_ref[:, :3 * DNW] = a_ref[...].astype(BF)
        o_ref[:, 3 * DNW:4 * DNW] = b_ref[...].astype(BF)
        o_ref[:, 4 * DNW:4 * DNW + 128] = c_ref[...].astype(BF)
        o_ref[:, 4 * DNW + 128:] = jnp.zeros((tm, 128), BF)

    return pl.pallas_call(
        body, grid=(T // tm,),
        in_specs=[pl.BlockSpec((tm, 3 * DNW), lambda i: (i, 0)), pl.BlockSpec((tm, DNW), lambda i: (i, 0)),
                  pl.BlockSpec((tm, 128), lambda i: (i, 0))],
        out_specs=pl.BlockSpec((tm, ZD), lambda i: (i, 0)),
        out_shape=jax.ShapeDtypeStruct((T, ZD), BF), name=name, compiler_params=_cp())(draw, dgate, dbd)


def _exchange(bufs, scatter, name):
    nt = len(bufs)
    shapes = [b.shape[1:] if scatter else b.shape for b in bufs]

    def body(*refs):
        ins, outs = refs[:nt], refs[nt:2 * nt]
        send_sems, recv_sems, loc_sems = refs[2 * nt:]
        x, y, c = lax.axis_index("x"), lax.axis_index("y"), lax.axis_index("c")
        me = 4 * x + 2 * y + c
        copies = []
        for t in range(nt):
            src = ins[t].at[me] if scatter else ins[t]
            cp = pltpu.make_async_copy(src, outs[t].at[me], loc_sems.at[t])
            cp.start()
            copies.append(cp)
        remote = []
        for k in (1, 2, 4, 3, 5, 6, 7):
            px, py, pc = x ^ (k >> 2), y ^ ((k >> 1) & 1), c ^ (k & 1)
            peer = 4 * px + 2 * py + pc
            for t in range(nt):
                src = ins[t].at[peer] if scatter else ins[t]
                cp = pltpu.make_async_remote_copy(
                    src_ref=src, dst_ref=outs[t].at[me], send_sem=send_sems.at[t, k], recv_sem=recv_sems.at[t, k],
                    device_id=(px, py, pc), device_id_type=MESH_ID)
                cp.start()
                remote.append(cp)
        for cp in remote:
            cp.wait()
        for cp in copies:
            cp.wait()

    return pl.pallas_call(
        body, in_specs=[ANY] * nt, out_specs=[ANY] * nt,
        out_shape=[jax.ShapeDtypeStruct((N_DEV,) + tuple(s), b.dtype) for s, b in zip(shapes, bufs)],
        scratch_shapes=[pltpu.SemaphoreType.DMA((nt, N_DEV)), pltpu.SemaphoreType.DMA((nt, N_DEV)),
                        pltpu.SemaphoreType.DMA((nt,))],
        name=name, compiler_params=pltpu.CompilerParams(has_side_effects=True))(*bufs)


def _adam(recv, w, m, v, tr, name):
    _, r, c = w.shape
    c1 = np.float32(1.0 - ADAM_B1 ** ADAM_STEP)
    c2 = np.float32(1.0 - ADAM_B2 ** ADAM_STEP)

    def body(r_ref, w_ref, m_ref, v_ref, g_ref, d_ref, mo_ref, vo_ref):
        g = r_ref[0].astype(F32)
        for s in range(1, N_DEV):
            g = g + r_ref[s].astype(F32)
        mn = ADAM_B1 * m_ref[0] + (1.0 - ADAM_B1) * g
        vn = ADAM_B2 * v_ref[0] + (1.0 - ADAM_B2) * (g * g)
        g_ref[0] = g
        mo_ref[0] = mn
        vo_ref[0] = vn
        d_ref[0] = -ADAM_LR * ((mn / c1) / (jnp.sqrt(vn / c2) + ADAM_EPS) + ADAM_WD * w_ref[0])

    one = pl.BlockSpec((1, tr, c), lambda i: (0, i, 0))
    return pl.pallas_call(
        body, grid=(r // tr,), in_specs=[pl.BlockSpec((N_DEV, tr, c), lambda i: (0, i, 0)), one, one, one],
        out_specs=[one] * 4, out_shape=[jax.ShapeDtypeStruct((1, r, c), F32)] * 4,
        name=name, compiler_params=_cp())(recv, w, m, v)


def _residues(a, dil, width):
    return a.reshape(T // dil, dil * width)


def _local_step(x, target, p):
    g = {}
    x0, h1, gu1, act1, y1 = _ffn_fwd(x, None, 0.0, p["norm_ffn1"], p["wgu1"], p["wd1"], "ffn1")
    x1, h2 = _rms_fwd(x0, y1, 0.5, p["norm_mix"], "mix_norm")
    z_at = _mm_nn(h2, p["win_a"], BF, 1024, 768, "mix_in_attn")
    z_dn = _mm_nn(h2, p["win_d"], F32, 1024, 768, "mix_in_dn")

    qkv_r, os_, ms_, ls_ = [], [], [], []
    for i, (_, dil) in enumerate(PATTERNS):
        qr = _residues(z_at, dil, ZA)
        o, m, l = _attn_fwd(qr, dil, "attn_fwd%d" % i)
        qkv_r.append(qr)
        os_.append(o.reshape(T, AW))
        ms_.append(m.reshape(T, 128))
        ls_.append(l.reshape(T, 128))
    attn_f, attn_b, lse = _attn_merge(os_, ms_, ls_, "attn_merge")

    conv, qkvn, bg = _dn_pre_fwd(z_dn, p["conv_w8"], p["alog_row"], p["dtb_row"], "dn_pre")
    *prep, tinv = _dn_prep_fwd(qkvn, bg, "dn_prep")
    o_dn, states = _dn_scan_fwd(prep, "dn_scan")
    dn_b = _dn_post_fwd(o_dn, z_dn, p["dn_norm"], "dn_post")

    mix = jnp.concatenate([attn_b, dn_b], axis=1)
    y2 = _mm_nn(mix, p["wout"], F32, 1024, 1024, "mix_out")
    x2, h3, gu2, act2, y3 = _ffn_fwd(x1, y2, 1.0, p["norm_ffn2"], p["wgu2"], p["wd2"], "ffn2")
    x3, _ = _rms_fwd(x2, y3, 0.5, p["norm_final"], "final_add")

    dx3, dys3, g["norm_final"], loss8 = _loss_bwd(x3, p["norm_final"], target, "loss")
    dx2, dx2b, g["norm_ffn2"], g["wgu2"], g["wd2"] = _ffn_bwd(
        x2, p["norm_ffn2"], h3, gu2, act2, p["wgu2"], p["wd2"], dx3, dys3, 1.0, "ffn2b")

    dmix = _mm_nt(dx2b, p["wout"], F32, 1024, 1024, "mix_out_dx")
    g["wout"] = _mm_tn(mix, dx2b, 512, 1024, "mix_out_dw")
    d_attn = dmix[:, :AW]
    d_dn = dmix[:, AW:]

    dl = _attn_delta(d_attn, attn_f, "attn_delta")
    parts = []
    for i, (_, dil) in enumerate(PATTERNS):
        do_r = _residues(d_attn, dil, AW)
        lse_r = _residues(lse, dil, 128)
        dl_r = _residues(dl, dil, 128)
        dq = _attn_bwd_dq(qkv_r[i], do_r, lse_r, dl_r, dil, "attn_dq%d" % i)
        dk, dv = _attn_bwd_dkv(qkv_r[i], do_r, lse_r, dl_r, dil, "attn_dkv%d" % i)
        parts += [dq.reshape(T, AW), dk.reshape(T, AW), dv.reshape(T, AW)]
    dz_at = _sum3_cast(parts, "attn_dsum")

    do_dn, dgate, g["dn_norm"] = _dn_post_bwd(o_dn, z_dn, p["dn_norm"], d_dn, "dn_post_b")
    cts = _dn_scan_bwd(prep, states, do_dn, "dn_scan_b")
    dqn, dkn, dvn, dbg = _dn_prep_bwd(qkvn, bg, tinv, cts, "dn_prep_b")
    dconv, dbd, g["alog_row"], g["dtb_row"] = _dn_pre_bwd(
        conv, z_dn, p["alog_row"], p["dtb_row"], dqn, dkn, dvn, dbg, "dn_pre_b")
    draw, g["conv_w8"] = _dn_conv_bwd(dconv, z_dn, p["conv_w8"], "dn_conv_b")
    dz_dn = _dz_dn_assemble(draw, dgate, dbd, "dn_dz")

    dh2 = _mm_nt(dz_at, p["win_a"], F32, 1024, 1024, "mix_in_dx_a")
    dh2d = _mm_nt(dz_dn, p["win_d"], F32, 1024, 1024, "mix_in_dx_d")
    g["win_a"] = _mm_tn(h2, dz_at, 512, 768, "mix_in_dw_a")
    g["win_d"] = _mm_tn(h2, dz_dn, 512, 768, "mix_in_dw_d")
    dx1, _, g["norm_mix"] = _rms_bwd2(x1, p["norm_mix"], dh2, dh2d, dx2, "mix_dnorm")
    dys1 = None
    dx0, _, g["norm_ffn1"], g["wgu1"], g["wd1"] = _ffn_bwd(
        x0, p["norm_ffn1"], h1, gu1, act1, p["wgu1"], p["wd1"], dx1[0], dx1[1], 1.0, "ffn1b")
    return loss8[0, 0], dx0, g


def _rms_bwd2(x, gain, dh_a, dh_b, dres, name):
    tm = 512
    row = pl.BlockSpec((tm, D), lambda i: (i, 0))
    gspec = pl.BlockSpec((1, D), lambda i: (0, 0))

    def body(x_ref, g_ref, da_ref, db_ref, dres_ref, dx_ref, dxs_ref, dg_ref):
        i = pl.program_id(0)
        xv = x_ref[...]
        r = lax.rsqrt(jnp.mean(xv * xv, axis=-1, keepdims=True) + NORM_EPS)
        xh = xv * r
        dhv = da_ref[...] + db_ref[...]
        part = jnp.sum(dhv * xh, axis=0, keepdims=True)

        @pl.when(i == 0)
        def _():
            dg_ref[...] = part

        @pl.when(i > 0)
        def _():
            dg_ref[...] += part

        dxh = dhv * g_ref[...]
        dx = r * (dxh - xh * jnp.mean(dxh * xh, axis=-1, keepdims=True)) + dres_ref[...]
        dx_ref[...] = dx
        dxs_ref[...] = (0.5 * dx).astype(BF)

    dx, dxs, dg = pl.pallas_call(
        body, grid=(T // tm,), in_specs=[row, gspec, row, row, row], out_specs=[row, row, gspec],
        out_shape=[jax.ShapeDtypeStruct((T, D), F32), jax.ShapeDtypeStruct((T, D), BF),
                   jax.ShapeDtypeStruct((1, D), F32)],
        name=name, compiler_params=_cp())(x, gain, dh_a, dh_b, dres)
    return (dx, dxs), None, dg


def _cols_from_shards(gathered):
    n, r, c = gathered.shape
    return jnp.transpose(gathered, (1, 0, 2)).reshape(r, n * c)


def _shards_from_cols(full, dtype):
    r, nc = full.shape
    return jnp.transpose(full.reshape(r, N_DEV, nc // N_DEV), (1, 0, 2)).astype(dtype)


def _lane_row(vec4):
    return jnp.zeros((1, 128), F32).at[:, DN_H:2 * DN_H].set(vec4.astype(F32))


def _build_params(gath, norm_ffn1, norm_mix, norm_ffn2, norm_final, a_log, dt_bias, dn_norm):
    p = {}
    for i in ("1", "2"):
        p["wgu" + i] = jnp.concatenate(
            [_cols_from_shards(gath["gate" + i]), _cols_from_shards(gath["up" + i])], axis=1)
        p["wd" + i] = gath["down" + i].reshape(F, D)
    w_in = _cols_from_shards(gath["w_in"])
    c0 = 3 * AW + 3 * DNW
    wp = jnp.concatenate([w_in[:, :c0], w_in[:, c0 + 2 * DN_H:], w_in[:, c0:c0 + 2 * DN_H],
                          jnp.zeros((D, ZP - IN_COLS), w_in.dtype)], axis=1)
    p["win_a"] = wp[:, :ZA]
    p["win_d"] = wp[:, ZA:]
    p["wout"] = gath["w_out"].reshape(D, D)
    conv = _cols_from_shards(gath["conv_w"])
    p["conv_w8"] = jnp.concatenate([conv, jnp.zeros((4, 3 * DNW), F32)], axis=0)
    p["norm_ffn1"], p["norm_mix"], p["norm_ffn2"] = norm_ffn1, norm_mix, norm_ffn2
    p["norm_final"] = norm_final.reshape(1, D)
    p["alog_row"] = _lane_row(a_log)
    p["dtb_row"] = _lane_row(dt_bias)
    p["dn_norm"] = dn_norm
    return p


def _grad_slabs(g):
    s = {}
    for i in ("1", "2"):
        s["gate" + i] = _shards_from_cols(g["wgu" + i][:, :F], BF)
        s["up" + i] = _shards_from_cols(g["wgu" + i][:, F:], BF)
        s["down" + i] = g["wd" + i].reshape(N_DEV, F // N_DEV, D).astype(BF)
    gp = jnp.concatenate([g["win_a"], g["win_d"]], axis=1)
    c0 = 3 * AW + 3 * DNW
    g_in = jnp.concatenate([gp[:, :c0], gp[:, c0 + DNW:c0 + DNW + 2 * DN_H], gp[:, c0:c0 + DNW]], axis=1)
    s["w_in"] = _shards_from_cols(g_in, BF)
    s["w_out"] = g["wout"].reshape(N_DEV, D // N_DEV, D).astype(BF)
    s["conv_w"] = _shards_from_cols(g["conv_w8"][:4], F32)
    return s


SMALL_ROWS = 40


def _small_pack(norm_ffn1, norm_mix, norm_ffn2, norm_final, dn_norm, alog_row, dtb_row):
    rows = [a.reshape(8, 128) for a in (norm_ffn1, norm_mix, norm_ffn2, norm_final)]
    rows += [dn_norm.reshape(1, 128), alog_row, dtb_row, jnp.zeros((SMALL_ROWS - 35, 128), F32)]
    return jnp.concatenate(rows, axis=0)


def _small_unpack(pk):
    pk = pk[0]
    return (pk[0:8].reshape(1, D), pk[8:16].reshape(1, D), pk[16:24].reshape(1, D), pk[24:32].reshape(D),
            pk[32:33], pk[33:34, DN_H:2 * DN_H], pk[34:35, DN_H:2 * DN_H])


ADAM_TILE = {"gate1": 256, "up1": 256, "down1": 176, "gate2": 256, "up2": 256, "down2": 176,
             "w_in": 256, "w_out": 128, "conv_w": 4}
BIG = ("gate1", "up1", "down1", "w_in", "w_out", "gate2", "up2", "down2", "conv_w")


def kernel(x, norm_ffn1, ffn1_gate, ffn1_up, ffn1_down, norm_mix, w_in, conv_w, a_log, dt_bias, dn_norm, w_out, norm_ffn2, ffn2_gate, ffn2_up, ffn2_down, norm_final, loss_target, m_norm_ffn1, m_ffn1_gate, m_ffn1_up, m_ffn1_down, m_norm_mix, m_w_in, m_conv_w, m_a_log, m_dt_bias, m_dn_norm, m_w_out, m_norm_ffn2, m_ffn2_gate, m_ffn2_up, m_ffn2_down, m_norm_final, v_norm_ffn1, v_ffn1_gate, v_ffn1_up, v_ffn1_down, v_norm_mix, v_w_in, v_conv_w, v_a_log, v_dt_bias, v_dn_norm, v_w_out, v_norm_ffn2, v_ffn2_gate, v_ffn2_up, v_ffn2_down, v_norm_final):
    w = {"gate1": ffn1_gate, "up1": ffn1_up, "down1": ffn1_down, "w_in": w_in, "w_out": w_out,
         "gate2": ffn2_gate, "up2": ffn2_up, "down2": ffn2_down, "conv_w": conv_w}
    m = {"gate1": m_ffn1_gate, "up1": m_ffn1_up, "down1": m_ffn1_down, "w_in": m_w_in, "w_out": m_w_out,
         "gate2": m_ffn2_gate, "up2": m_ffn2_up, "down2": m_ffn2_down, "conv_w": m_conv_w}
    v = {"gate1": v_ffn1_gate, "up1": v_ffn1_up, "down1": v_ffn1_down, "w_in": v_w_in, "w_out": v_w_out,
         "gate2": v_ffn2_gate, "up2": v_ffn2_up, "down2": v_ffn2_down, "conv_w": v_conv_w}

    shards = [w[n][0].astype(F32 if n == "conv_w" else BF) for n in BIG]
    gath = dict(zip(BIG, _exchange(shards, False, "weights_allgather")))
    p = _build_params(gath, norm_ffn1, norm_mix, norm_ffn2, norm_final, a_log, dt_bias, dn_norm)

    loss_part, dx, g = _local_step(x[0], loss_target[0], p)

    slabs = _grad_slabs(g)
    small = _small_pack(g["norm_ffn1"], g["norm_mix"], g["norm_ffn2"], g["norm_final"], g["dn_norm"],
                        g["alog_row"], g["dtb_row"])
    sends = [slabs[n] for n in BIG] + [jnp.broadcast_to(small[None], (N_DEV, SMALL_ROWS, 128))]
    recv = _exchange(sends, True, "grads_exchange")
    res = {n: _adam(recv[i], w[n], m[n], v[n], ADAM_TILE[n], "adam_" + n) for i, n in enumerate(BIG)}
    pack = lambda a: _small_pack(*a)[None]
    res_s = _adam(
        recv[-1],
        pack((norm_ffn1, norm_mix, norm_ffn2, norm_final, dn_norm, _lane_row(a_log), _lane_row(dt_bias))),
        pack((m_norm_ffn1, m_norm_mix, m_norm_ffn2, m_norm_final, m_dn_norm, _lane_row(m_a_log), _lane_row(m_dt_bias))),
        pack((v_norm_ffn1, v_norm_mix, v_norm_ffn2, v_norm_final, v_dn_norm, _lane_row(v_a_log), _lane_row(v_dt_bias))),
        SMALL_ROWS, "adam_small")

    loss = lax.psum(loss_part, ("x", "y", "c"))
    outs = [loss, dx[None]]
    for k in range(4):
        n1, nm, n2, nf, dn, al, dt = _small_unpack(res_s[k])
        big = {n: res[n][k] for n in BIG}
        outs += [n1, big["gate1"], big["up1"], big["down1"], nm, big["w_in"], big["conv_w"], al, dt, dn,
                 big["w_out"], n2, big["gate2"], big["up2"], big["down2"], nf]
    return tuple(outs)
```

```python
import functools

import numpy as np
import jax
import jax.numpy as jnp
from jax import lax
from jax.experimental import pallas as pl
from jax.experimental.pallas import tpu as pltpu

T = 4096
D = 1024
F = 2816
N_DEV = 8
A_HEADS = 8
A_HD = 64
AW = A_HEADS * A_HD
DN_H = 4
DN_HD = 128
DNW = DN_H * DN_HD
CH = 64
PAIR = 2 * CH
ZA = 3 * AW
ZD = 3 * DNW + DNW + 256
ZP = ZA + ZD
BD_BLK = (3 * DNW + DNW) // 128
IN_COLS = 3592
PATTERNS = ((128, 1), (512, 4), (2048, 16))
NORM_EPS = 1e-6
L2_EPS = 1e-6
ADAM_LR, ADAM_B1, ADAM_B2, ADAM_EPS, ADAM_WD, ADAM_STEP = 0.001, 0.9, 0.999, 1e-08, 0.01, 10
VMEM_LIMIT = 56 * 1024 * 1024
NEG = -1e30

BF = jnp.bfloat16
F32 = jnp.float32
NN = (((1,), (0,)), ((), ()))
NT = (((1,), (1,)), ((), ()))
TN = (((0,), (0,)), ((), ()))
HI = lax.Precision.HIGHEST
MESH_ID = pl.DeviceIdType.MESH
ANY = pl.BlockSpec(memory_space=pl.ANY)


def _cp():
    return pltpu.CompilerParams(vmem_limit_bytes=VMEM_LIMIT)


def _dg(a, b, dims):
    return lax.dot_general(a, b, dims, preferred_element_type=F32)


def _hdot(a, b):
    return lax.dot_general(a, b, NN, precision=HI, preferred_element_type=F32)


def _make_bdot(dims, da_dims, da_swap, db_dims, db_swap):
    @jax.custom_vjp
    def f(a, b):
        return _dg(a.astype(BF), b.astype(BF), dims)

    def fwd(a, b):
        return f(a, b), (a, b)

    def bwd(res, g):
        a, b = res
        gb, ab, bb = g.astype(BF), a.astype(BF), b.astype(BF)
        da = _dg(bb, gb, da_dims) if da_swap else _dg(gb, bb, da_dims)
        db = _dg(gb, ab, db_dims) if db_swap else _dg(ab, gb, db_dims)
        return da, db

    f.defvjp(fwd, bwd)
    return f


_bdot_nn = _make_bdot(NN, NT, False, TN, False)
_bdot_nt = _make_bdot(NT, NN, False, TN, True)
_bdot_tn = _make_bdot(TN, NT, True, NN, False)


def _iota(shape, dim):
    return lax.broadcasted_iota(jnp.int32, shape, dim)


def _col(x, idx):
    return jnp.sum(jnp.where(_iota(x.shape, 1) == idx, x, 0.0), axis=1, keepdims=True)


def _mm_nn(a, b, out_dtype, tm, tn, name):
    m, k = a.shape
    n = b.shape[1]

    def body(a_ref, b_ref, o_ref):
        o_ref[...] = _dg(a_ref[...], b_ref[...], NN).astype(out_dtype)

    return pl.pallas_call(
        body, grid=(m // tm, n // tn),
        in_specs=[pl.BlockSpec((tm, k), lambda i, j: (i, 0)), pl.BlockSpec((k, tn), lambda i, j: (0, j))],
        out_specs=pl.BlockSpec((tm, tn), lambda i, j: (i, j)),
        out_shape=jax.ShapeDtypeStruct((m, n), out_dtype), name=name, compiler_params=_cp())(a, b)


def _mm_nt(a, b, out_dtype, tm, tb, name):
    m, c = a.shape
    kb = b.shape[0]

    def body(a_ref, b_ref, o_ref):
        o_ref[...] = _dg(a_ref[...], b_ref[...], NT).astype(out_dtype)

    return pl.pallas_call(
        body, grid=(m // tm, kb // tb),
        in_specs=[pl.BlockSpec((tm, c), lambda i, j: (i, 0)), pl.BlockSpec((tb, c), lambda i, j: (j, 0))],
        out_specs=pl.BlockSpec((tm, tb), lambda i, j: (i, j)),
        out_shape=jax.ShapeDtypeStruct((m, kb), out_dtype), name=name, compiler_params=_cp())(a, b)


def _mm_tn(a, b, ta, tb, name):
    m, ka = a.shape
    nb = b.shape[1]

    def body(a_ref, b_ref, o_ref):
        o_ref[...] = _dg(a_ref[...], b_ref[...], TN)

    return pl.pallas_call(
        body, grid=(ka // ta, nb // tb),
        in_specs=[pl.BlockSpec((m, ta), lambda i, j: (0, i)), pl.BlockSpec((m, tb), lambda i, j: (0, j))],
        out_specs=pl.BlockSpec((ta, tb), lambda i, j: (i, j)),
        out_shape=jax.ShapeDtypeStruct((ka, nb), F32), name=name, compiler_params=_cp())(a, b)


def _rms_fwd(x, y, alpha, gain, name):
    tm = 512
    has_y = y is not None
    row = pl.BlockSpec((tm, D), lambda i: (i, 0))
    gspec = pl.BlockSpec((1, D), lambda i: (0, 0))

    def body(*refs):
        if has_y:
            x_ref, y_ref, g_ref, xo_ref, h_ref = refs
            xv = x_ref[...] + alpha * y_ref[...]
            xo_ref[...] = xv
        else:
            x_ref, g_ref, h_ref = refs
            xv = x_ref[...]
        r = lax.rsqrt(jnp.mean(xv * xv, axis=-1, keepdims=True) + NORM_EPS)
        h_ref[...] = (xv * r * g_ref[...]).astype(BF)

    if has_y:
        return pl.pallas_call(
            body, grid=(T // tm,), in_specs=[row, row, gspec], out_specs=[row, row],
            out_shape=[jax.ShapeDtypeStruct((T, D), F32), jax.ShapeDtypeStruct((T, D), BF)],
            name=name, compiler_params=_cp())(x, y, gain)
    h = pl.pallas_call(
        body, grid=(T // tm,), in_specs=[row, gspec], out_specs=row,
        out_shape=jax.ShapeDtypeStruct((T, D), BF), name=name, compiler_params=_cp())(x, gain)
    return x, h


def _rms_bwd(x, gain, dh, dres, alpha_out, name):
    tm = 512
    row = pl.BlockSpec((tm, D), lambda i: (i, 0))
    gspec = pl.BlockSpec((1, D), lambda i: (0, 0))

    def body(x_ref, g_ref, dh_ref, dres_ref, dx_ref, dxs_ref, dg_ref):
        i = pl.program_id(0)
        xv = x_ref[...]
        r = lax.rsqrt(jnp.mean(xv * xv, axis=-1, keepdims=True) + NORM_EPS)
        xh = xv * r
        dhv = dh_ref[...].astype(F32)
        part = jnp.sum(dhv * xh, axis=0, keepdims=True)

        @pl.when(i == 0)
        def _():
            dg_ref[...] = part

        @pl.when(i > 0)
        def _():
            dg_ref[...] += part

        dxh = dhv * g_ref[...]
        dx = r * (dxh - xh * jnp.mean(dxh * xh, axis=-1, keepdims=True)) + dres_ref[...]
        dx_ref[...] = dx
        dxs_ref[...] = (alpha_out * dx).astype(BF)

    return pl.pallas_call(
        body, grid=(T // tm,), in_specs=[row, gspec, row, row], out_specs=[row, row, gspec],
        out_shape=[jax.ShapeDtypeStruct((T, D), F32), jax.ShapeDtypeStruct((T, D), BF),
                   jax.ShapeDtypeStruct((1, D), F32)],
        name=name, compiler_params=_cp())(x, gain, dh, dres)


def _loss_bwd(x, gain, target, name):
    tm = 512
    row = pl.BlockSpec((tm, D), lambda i: (i, 0))
    gspec = pl.BlockSpec((1, D), lambda i: (0, 0))
    lspec = pl.BlockSpec((8, 128), lambda i: (0, 0))

    def body(x_ref, g_ref, t_ref, dx_ref, dxs_ref, dg_ref, loss_ref):
        i = pl.program_id(0)
        xv = x_ref[...]
        r = lax.rsqrt(jnp.mean(xv * xv, axis=-1, keepdims=True) + NORM_EPS)
        xh = xv * r
        diff = xh * g_ref[...] - t_ref[...]
        lpart = 0.5 * jnp.sum(jnp.mean(diff * diff, axis=-1, keepdims=True), axis=0, keepdims=True)
        dy = diff * (1.0 / D)
        part = jnp.sum(dy * xh, axis=0, keepdims=True)

        @pl.when(i == 0)
        def _():
            dg_ref[...] = part
            loss_ref[...] = jnp.broadcast_to(lpart, (8, 128))

        @pl.when(i > 0)
        def _():
            dg_ref[...] += part
            loss_ref[...] += jnp.broadcast_to(lpart, (8, 128))

        dxh = dy * g_ref[...]
        dx = r * (dxh - xh * jnp.mean(dxh * xh, axis=-1, keepdims=True))
        dx_ref[...] = dx
        dxs_ref[...] = (0.5 * dx).astype(BF)

    return pl.pallas_call(
        body, grid=(T // tm,), in_specs=[row, gspec, row], out_specs=[row, row, gspec, lspec],
        out_shape=[jax.ShapeDtypeStruct((T, D), F32), jax.ShapeDtypeStruct((T, D), BF),
                   jax.ShapeDtypeStruct((1, D), F32), jax.ShapeDtypeStruct((8, 128), F32)],
        name=name, compiler_params=_cp())(x, gain, target)


def _swiglu_fwd(gu, name):
    tm = 256

    def body(gu_ref, a_ref):
        g = gu_ref[:, :F].astype(F32)
        u = gu_ref[:, F:].astype(F32)
        a_ref[...] = (g * jax.nn.sigmoid(g) * u).astype(BF)

    return pl.pallas_call(
        body, grid=(T // tm,), in_specs=[pl.BlockSpec((tm, 2 * F), lambda i: (i, 0))],
        out_specs=pl.BlockSpec((tm, F), lambda i: (i, 0)),
        out_shape=jax.ShapeDtypeStruct((T, F), BF), name=name, compiler_params=_cp())(gu)


def _swiglu_bwd(gu, da, name):
    tm = 256

    def body(gu_ref, da_ref, o_ref):
        g = gu_ref[:, :F].astype(F32)
        u = gu_ref[:, F:].astype(F32)
        dav = da_ref[...].astype(F32)
        s = jax.nn.sigmoid(g)
        o_ref[:, :F] = (dav * u * (s * (1.0 + g * (1.0 - s)))).astype(BF)
        o_ref[:, F:] = (dav * (g * s)).astype(BF)

    return pl.pallas_call(
        body, grid=(T // tm,),
        in_specs=[pl.BlockSpec((tm, 2 * F), lambda i: (i, 0)), pl.BlockSpec((tm, F), lambda i: (i, 0))],
        out_specs=pl.BlockSpec((tm, 2 * F), lambda i: (i, 0)),
        out_shape=jax.ShapeDtypeStruct((T, 2 * F), BF), name=name, compiler_params=_cp())(gu, da)


def _ffn_fwd(x_prev, y_prev, alpha, gain, need_gu, need_down, tag):
    x_in, h = _rms_fwd(x_prev, y_prev, alpha, gain, tag + "_norm")
    wgu = need_gu(h)
    gu = _mm_nn(h, wgu, BF, 1024, 1408, tag + "_gu")
    wd = need_down(gu)
    act = _swiglu_fwd(gu, tag + "_act")
    y = _mm_nn(act, wd, F32, 1024, 1024, tag + "_down")
    return x_in, h, gu, act, y, wgu, wd


def _ffn_bwd(x_in, gain, h, gu, act, wgu, wd, dxo, dys, alpha_out, emit, tag):
    dact = _mm_nt(dys, wd, BF, 1024, 1408, tag + "_dact")
    dgu = _swiglu_bwd(gu, dact, tag + "_dgu")
    dwgu = _mm_tn(h, dgu, 512, 512, tag + "_dwgu")
    dwd = _mm_tn(act, dys, 256, 1024, tag + "_dwd")
    zero = emit(dwgu, dwd)
    dh = _mm_nt(dgu, wgu, F32, 512, 512, tag + "_dh")
    dx, dxs, dgain = _rms_bwd(x_in, gain + zero, dh, dxo, alpha_out, tag + "_dnorm")
    return dx, dxs, dgain


def _slope(h):
    return float(2.0 ** (-(h + 1)))


def _pair_masks():
    lane = _iota((128, 128), 1)
    return lane < A_HD, lane >= A_HD


def _attn_fwd(qkv_r, dil, name):
    L = T // dil
    nb = L // 128

    def body(q_ref, kp_ref, kc_ref, vp_ref, vc_ref, o_ref, m_ref, l_ref):
        n = pl.program_id(1)
        qi = _iota((128, 256), 0)
        kj = _iota((128, 256), 1)
        steps = qi + 128 - kj
        valid = (steps >= 0) & (steps <= 128) & ((n * 128 + kj - 128) >= 0)
        dist = (steps * dil).astype(F32)
        lane = _iota((128, 128), 1)
        lo, hi = _pair_masks()
        m_acc = jnp.zeros((128, 128), F32)
        l_acc = jnp.ones((128, 128), F32)
        for j in range(A_HEADS // 2):
            sl = slice(128 * j, 128 * j + 128)
            qp = q_ref[:, sl]
            kcat = jnp.concatenate([kp_ref[:, sl], kc_ref[:, sl]], axis=0)
            vcat = jnp.concatenate([vp_ref[:, sl], vc_ref[:, sl]], axis=0)
            outs = []
            for e in range(2):
                h = 2 * j + e
                qm = jnp.where(lo if e == 0 else hi, qp, jnp.zeros_like(qp))
                s = _dg(qm, kcat, NT) * (A_HD ** -0.5) - _slope(h) * dist
                s = jnp.where(valid, s, NEG)
                m = jnp.max(s, axis=1, keepdims=True)
                p = jnp.exp(s - m)
                l = jnp.sum(p, axis=1, keepdims=True)
                outs.append(_dg(p.astype(BF), vcat, NN) / l)
                m_acc = jnp.where(lane == h, m, m_acc)
                l_acc = jnp.where(lane == h, l, l_acc)
            o_ref[:, sl] = jnp.where(lo, outs[0], outs[1])
        m_ref[...] = m_acc
        l_ref[...] = l_acc

    blk = lambda col: pl.BlockSpec((128, AW), lambda r, n: (n, 3 * r + col))
    prv = lambda col: pl.BlockSpec((128, AW), lambda r, n: (jnp.maximum(n - 1, 0), 3 * r + col))
    return pl.pallas_call(
        body, grid=(dil, nb),
        in_specs=[blk(0), prv(1), blk(1), prv(2), blk(2)],
        out_specs=[pl.BlockSpec((128, AW), lambda r, n: (n, r)),
                   pl.BlockSpec((128, 128), lambda r, n: (n, r)),
                   pl.BlockSpec((128, 128), lambda r, n: (n, r))],
        out_shape=[jax.ShapeDtypeStruct((L, dil * AW), F32), jax.ShapeDtypeStruct((L, dil * 128), F32),
                   jax.ShapeDtypeStruct((L, dil * 128), F32)],
        name=name, compiler_params=_cp())(qkv_r, qkv_r, qkv_r, qkv_r, qkv_r)


def _expand_mat():
    r = _iota((128, AW), 0)
    c = _iota((128, AW), 1)
    return jnp.where((c // A_HD) == r, 1.0, 0.0).astype(F32)


def _attn_merge(os, ms, ls, name):
    tm = 256
    row = pl.BlockSpec((tm, AW), lambda i: (i, 0))
    st = pl.BlockSpec((tm, 128), lambda i: (i, 0))

    def body(o1, o2, o3, m1, m2, m3, l1, l2, l3, of_ref, ob_ref, lse_ref):
        mx = jnp.maximum(jnp.maximum(m1[...], m2[...]), m3[...])
        w = [l[...] * jnp.exp(m[...] - mx) for m, l in ((m1, l1), (m2, l2), (m3, l3))]
        den = w[0] + w[1] + w[2]
        ex = _expand_mat()
        acc = jnp.zeros((tm, AW), F32)
        for wp, op in zip(w, (o1, o2, o3)):
            acc = acc + _hdot(wp, ex) * op[...]
        out = acc / _hdot(den, ex)
        of_ref[...] = out
        ob_ref[...] = out.astype(BF)
        lse_ref[...] = mx + jnp.log(den)

    return pl.pallas_call(
        body, grid=(T // tm,), in_specs=[row] * 3 + [st] * 6, out_specs=[row, row, st],
        out_shape=[jax.ShapeDtypeStruct((T, AW), F32), jax.ShapeDtypeStruct((T, AW), BF),
                   jax.ShapeDtypeStruct((T, 128), F32)],
        name=name, compiler_params=_cp())(*os, *ms, *ls)


def _attn_delta(dout, out, name):
    tm = 256
    row = pl.BlockSpec((tm, AW), lambda i: (i, 0))

    def body(do_ref, o_ref, d_ref):
        d_ref[...] = lax.dot_general(do_ref[...] * o_ref[...], _expand_mat(), NT, precision=HI,
                                     preferred_element_type=F32)

    return pl.pallas_call(
        body, grid=(T // tm,), in_specs=[row, row], out_specs=pl.BlockSpec((tm, 128), lambda i: (i, 0)),
        out_shape=jax.ShapeDtypeStruct((T, 128), F32), name=name, compiler_params=_cp())(dout, out)


def _attn_bwd_dq(qkv_r, do_r, lse_r, dl_r, dil, name):
    L = T // dil
    nb = L // 128

    def body(q_ref, kp_ref, kc_ref, vp_ref, vc_ref, do_ref, lse_ref, dl_ref, dq_ref):
        n = pl.program_id(1)
        qi = _iota((128, 256), 0)
        kj = _iota((128, 256), 1)
        steps = qi + 128 - kj
        valid = (steps >= 0) & (steps <= 128) & ((n * 128 + kj - 128) >= 0)
        dist = (steps * dil).astype(F32)
        lo, hi = _pair_masks()
        lse = lse_ref[...]
        dl = dl_ref[...]
        for j in range(A_HEADS // 2):
            sl = slice(128 * j, 128 * j + 128)
            qp = q_ref[:, sl]
            dop = do_ref[:, sl].astype(BF)
            kcat = jnp.concatenate([kp_ref[:, sl], kc_ref[:, sl]], axis=0)
            vcat = jnp.concatenate([vp_ref[:, sl], vc_ref[:, sl]], axis=0)
            outs = []
            for e in range(2):
                h = 2 * j + e
                msk = lo if e == 0 else hi
                qm = jnp.where(msk, qp, jnp.zeros_like(qp))
                dom = jnp.where(msk, dop, jnp.zeros_like(dop))
                s = _dg(qm, kcat, NT) * (A_HD ** -0.5) - _slope(h) * dist
                p = jnp.where(valid, jnp.exp(jnp.where(valid, s, NEG) - _col(lse, h)), 0.0)
                dp = _dg(dom, vcat, NT)
                ds = p * (dp - _col(dl, h))
                outs.append(_dg(ds.astype(BF), kcat, NN) * (A_HD ** -0.5))
            dq_ref[:, sl] = jnp.where(lo, outs[0], outs[1])

    blk = lambda col: pl.BlockSpec((128, AW), lambda r, n: (n, 3 * r + col))
    prv = lambda col: pl.BlockSpec((128, AW), lambda r, n: (jnp.maximum(n - 1, 0), 3 * r + col))
    cur = pl.BlockSpec((128, AW), lambda r, n: (n, r))
    st = pl.BlockSpec((128, 128), lambda r, n: (n, r))
    return pl.pallas_call(
        body, grid=(dil, nb),
        in_specs=[blk(0), prv(1), blk(1), prv(2), blk(2), cur, st, st],
        out_specs=cur, out_shape=jax.ShapeDtypeStruct((L, dil * AW), F32),
        name=name, compiler_params=_cp())(qkv_r, qkv_r, qkv_r, qkv_r, qkv_r, do_r, lse_r, dl_r)


def _attn_bwd_dkv(qkv_r, do_r, lse_r, dl_r, dil, name):
    L = T // dil
    nb = L // 128

    def body(k_ref, v_ref, qc_ref, qn_ref, doc_ref, don_ref, lsec_ref, lsen_ref, dlc_ref, dln_ref,
             dk_ref, dv_ref):
        mblk = pl.program_id(1)
        qi = _iota((128, 128), 0)
        kj = _iota((128, 128), 1)
        steps_c = qi - kj
        steps_n = qi + 128 - kj
        valid_c = steps_c >= 0
        valid_n = (steps_n <= 128) & (mblk + 1 < nb)
        lo, hi = _pair_masks()
        for j in range(A_HEADS // 2):
            sl = slice(128 * j, 128 * j + 128)
            kp = k_ref[:, sl]
            vp = v_ref[:, sl]
            dk = jnp.zeros((128, 128), F32)
            dv = jnp.zeros((128, 128), F32)
            for q_ref, do_ref, lse_ref, dl_ref, steps, valid in (
                    (qc_ref, doc_ref, lsec_ref, dlc_ref, steps_c, valid_c),
                    (qn_ref, don_ref, lsen_ref, dln_ref, steps_n, valid_n)):
                qp = q_ref[:, sl]
                dop = do_ref[:, sl].astype(BF)
                dist = (steps * dil).astype(F32)
                lse = lse_ref[...]
                dl = dl_ref[...]
                for e in range(2):
                    h = 2 * j + e
                    msk = lo if e == 0 else hi
                    qm = jnp.where(msk, qp, jnp.zeros_like(qp))
                    dom = jnp.where(msk, dop, jnp.zeros_like(dop))
                    s = _dg(qm, kp, NT) * (A_HD ** -0.5) - _slope(h) * dist
                    p = jnp.where(valid, jnp.exp(jnp.where(valid, s, NEG) - _col(lse, h)), 0.0)
                    dp = _dg(dom, vp, NT)
                    ds = p * (dp - _col(dl, h))
                    dv = dv + _dg(p.astype(BF), dom, TN)
                    dk = dk + _dg(ds.astype(BF), qm, TN) * (A_HD ** -0.5)
            dk_ref[:, sl] = dk
            dv_ref[:, sl] = dv

    cur3 = lambda col: pl.BlockSpec((128, AW), lambda r, m: (m, 3 * r + col))
    nxt3 = lambda col: pl.BlockSpec((128, AW), lambda r, m: (jnp.minimum(m + 1, nb - 1), 3 * r + col))
    cur = pl.BlockSpec((128, AW), lambda r, m: (m, r))
    nxt = pl.BlockSpec((128, AW), lambda r, m: (jnp.minimum(m + 1, nb - 1), r))
    stc = pl.BlockSpec((128, 128), lambda r, m: (m, r))
    stn = pl.BlockSpec((128, 128), lambda r, m: (jnp.minimum(m + 1, nb - 1), r))
    return pl.pallas_call(
        body, grid=(dil, nb),
        in_specs=[cur3(1), cur3(2), cur3(0), nxt3(0), cur, nxt, stc, stn, stc, stn],
        out_specs=[cur, cur],
        out_shape=[jax.ShapeDtypeStruct((L, dil * AW), F32), jax.ShapeDtypeStruct((L, dil * AW), F32)],
        name=name, compiler_params=_cp())(qkv_r, qkv_r, qkv_r, qkv_r, do_r, do_r, lse_r, lse_r, dl_r, dl_r)


def _sum3_cast(parts, name):
    tm = 512
    row = pl.BlockSpec((tm, AW), lambda i: (i, 0))

    def body(*refs):
        o_ref = refs[-1]
        for c in range(3):
            acc = refs[c][...] + refs[3 + c][...] + refs[6 + c][...]
            o_ref[:, c * AW:(c + 1) * AW] = acc.astype(BF)

    return pl.pallas_call(
        body, grid=(T // tm,), in_specs=[row] * 9, out_specs=pl.BlockSpec((tm, ZA), lambda i: (i, 0)),
        out_shape=jax.ShapeDtypeStruct((T, ZA), BF), name=name, compiler_params=_cp())(*parts)


def _silu(x):
    return x * jax.nn.sigmoid(x)


def _qk_math(c):
    s = _silu(c)
    return s * lax.rsqrt(jnp.sum(s * s, axis=-1, keepdims=True) + L2_EPS)


def _softplus(x):
    return jnp.maximum(x, 0.0) + jnp.log(1.0 + jnp.exp(-jnp.abs(x)))


def _gate_math(bd, alog_row, dtb_row):
    rows = bd.shape[0]
    lane = _iota(bd.shape, 1)
    beta = jax.nn.sigmoid(bd)
    g = jnp.where((lane >= DN_H) & (lane < 2 * DN_H), -jnp.exp(alog_row) * _softplus(bd + dtb_row), 0.0)
    ri = _iota((rows, rows), 0)
    ci = _iota((rows, rows), 1)
    same = (ri // CH) == (ci // CH)
    li = _iota((128, 128), 0)
    lj = _iota((128, 128), 1)
    to_next_group = jnp.where((lj == li + DN_H) & (li >= DN_H) & (li < 2 * DN_H), 1.0, 0.0)
    gc = _hdot(jnp.where(same & (ci <= ri), 1.0, 0.0), g)
    glast = _hdot(_hdot(jnp.where(same, 1.0, 0.0), g), to_next_group)
    return jnp.where(lane < DN_H, beta, 0.0) + gc + glast


def _shift_down(cur, halo, s):
    if s == 0:
        return cur
    rolled = pltpu.roll(cur, s, 0)
    hr = pltpu.roll(halo, s, 0)
    head = jnp.where(_iota(hr.shape, 0) < s, hr, rolled[:8])
    return jnp.concatenate([head, rolled[8:]], axis=0)


def _shift_up(cur, halo, s):
    if s == 0:
        return cur
    rows = cur.shape[0]
    rolled = pltpu.roll(cur, rows - s, 0)
    hr = pltpu.roll(halo, 8 - s, 0)
    tail = jnp.where(_iota(hr.shape, 0) >= 8 - s, hr, rolled[rows - 8:])
    return jnp.concatenate([rolled[:rows - 8], tail], axis=0)


def _dn_pre_fwd(z_dn, conv_w8, alog_row, dtb_row, name):
    tm = 256
    wq = 3 * DNW

    def body(raw_ref, halo_ref, bd_ref, w_ref, al_ref, dt_ref, conv_ref, qkv_ref, bg_ref):
        i = pl.program_id(0)
        cur = raw_ref[...]
        halo = jnp.where(i > 0, halo_ref[...], 0.0)
        w = w_ref[...]
        conv = jnp.zeros((tm, wq), F32)
        for j in range(4):
            conv = conv + _shift_down(cur, halo, 3 - j) * w[j:j + 1, :]
        conv_ref[...] = conv
        for blk in range(3 * DN_H):
            sl = slice(128 * blk, 128 * blk + 128)
            c = conv[:, sl]
            qkv_ref[:, sl] = _qk_math(c) if blk < 2 * DN_H else _silu(c)
        bg_ref[...] = _gate_math(bd_ref[...], al_ref[...], dt_ref[...])

    one = pl.BlockSpec((1, 128), lambda i: (0, 0))
    return pl.pallas_call(
        body, grid=(T // tm,),
        in_specs=[pl.BlockSpec((tm, wq), lambda i: (i, 0)),
                  pl.BlockSpec((8, wq), lambda i: (jnp.maximum(i * (tm // 8) - 1, 0), 0)),
                  pl.BlockSpec((tm, 128), lambda i: (i, BD_BLK)),
                  pl.BlockSpec((8, wq), lambda i: (0, 0)), one, one],
        out_specs=[pl.BlockSpec((tm, wq), lambda i: (i, 0)), pl.BlockSpec((tm, wq), lambda i: (i, 0)),
                   pl.BlockSpec((tm, 128), lambda i: (i, 0))],
        out_shape=[jax.ShapeDtypeStruct((T, wq), F32), jax.ShapeDtypeStruct((T, wq), F32),
                   jax.ShapeDtypeStruct((T, 128), F32)],
        name=name, compiler_params=_cp())(z_dn, z_dn, z_dn, conv_w8, alog_row, dtb_row)


def _dn_pre_bwd(conv, z_dn, alog_row, dtb_row, dqn, dkn, dvn, dbg, name):
    tm = 256
    wq = 3 * DNW

    def body(conv_ref, bd_ref, al_ref, dt_ref, dq_ref, dk_ref, dv_ref, dbg_ref,
             dconv_ref, dbd_ref, dal_ref, ddt_ref):
        i = pl.program_id(0)
        for blk in range(3 * DN_H):
            sl = slice(128 * blk, 128 * blk + 128)
            src = (dq_ref, dk_ref, dv_ref)[blk // DN_H]
            ct = src[:, 128 * (blk % DN_H):128 * (blk % DN_H) + 128]
            fn = _qk_math if blk < 2 * DN_H else _silu
            _, vjp = jax.vjp(fn, conv_ref[:, sl])
            dconv_ref[:, sl] = vjp(ct)[0]
        _, vjp = jax.vjp(_gate_math, bd_ref[...], al_ref[...], dt_ref[...])
        dbd, dal, ddt = vjp(dbg_ref[...])
        dbd_ref[...] = dbd

        @pl.when(i == 0)
        def _():
            dal_ref[...] = dal
            ddt_ref[...] = ddt

        @pl.when(i > 0)
        def _():
            dal_ref[...] += dal
            ddt_ref[...] += ddt

    one = pl.BlockSpec((1, 128), lambda i: (0, 0))
    row = pl.BlockSpec((tm, wq), lambda i: (i, 0))
    hd = pl.BlockSpec((tm, DNW), lambda i: (i, 0))
    st = pl.BlockSpec((tm, 128), lambda i: (i, 0))
    return pl.pallas_call(
        body, grid=(T // tm,),
        in_specs=[row, pl.BlockSpec((tm, 128), lambda i: (i, BD_BLK)), one, one, hd, hd, hd, st],
        out_specs=[row, st, one, one],
        out_shape=[jax.ShapeDtypeStruct((T, wq), F32), jax.ShapeDtypeStruct((T, 128), F32),
                   jax.ShapeDtypeStruct((1, 128), F32), jax.ShapeDtypeStruct((1, 128), F32)],
        name=name, compiler_params=_cp())(conv, z_dn, alog_row, dtb_row, dqn, dkn, dvn, dbg)


def _dn_conv_bwd(dconv, z_dn, conv_w8, name):
    tm = 256
    wq = 3 * DNW
    last = T // tm - 1

    def body(dc_ref, dcn_ref, raw_ref, halo_ref, w_ref, draw_ref, dw_ref):
        i = pl.program_id(0)
        dc = dc_ref[...]
        nxt = jnp.where(i < last, dcn_ref[...], 0.0)
        cur = raw_ref[...]
        halo = jnp.where(i > 0, halo_ref[...], 0.0)
        w = w_ref[...]
        draw = jnp.zeros((tm, wq), F32)
        rows = []
        for j in range(4):
            draw = draw + _shift_up(dc, nxt, 3 - j) * w[j:j + 1, :]
            rows.append(jnp.sum(dc * _shift_down(cur, halo, 3 - j), axis=0, keepdims=True))
        draw_ref[...] = draw
        part = jnp.concatenate(rows + [jnp.zeros((4, wq), F32)], axis=0)

        @pl.when(i == 0)
        def _():
            dw_ref[...] = part

        @pl.when(i > 0)
        def _():
            dw_ref[...] += part

    row = pl.BlockSpec((tm, wq), lambda i: (i, 0))
    return pl.pallas_call(
        body, grid=(T // tm,),
        in_specs=[row, pl.BlockSpec((8, wq), lambda i: (jnp.minimum((i + 1) * (tm // 8), T // 8 - 1), 0)),
                  row, pl.BlockSpec((8, wq), lambda i: (jnp.maximum(i * (tm // 8) - 1, 0), 0)),
                  pl.BlockSpec((8, wq), lambda i: (0, 0))],
        out_specs=[row, pl.BlockSpec((8, wq), lambda i: (0, 0))],
        out_shape=[jax.ShapeDtypeStruct((T, wq), F32), jax.ShapeDtypeStruct((8, wq), F32)],
        name=name, compiler_params=_cp())(dconv, dconv, z_dn, z_dn, conv_w8)


def _h3(a, b, dims=NN):
    return lax.dot_general(a, b, dims, precision=lax.Precision.HIGH, preferred_element_type=F32)


@jax.custom_vjp
def _inverse_given(a_mat, tinv):
    return tinv


def _inverse_given_fwd(a_mat, tinv):
    return tinv, tinv


def _inverse_given_bwd(tinv, g):
    return -_h3(tinv, _h3(g, tinv, NT), TN), jnp.zeros_like(tinv)


_inverse_given.defvjp(_inverse_given_fwd, _inverse_given_bwd)


def _prep_math(q, k, v, bgc, h, tinv_saved=None):
    beta = _col(bgc, h)
    gc = jnp.broadcast_to(_col(bgc, DN_H + h), (PAIR, 128))
    glast = jnp.broadcast_to(_col(bgc, 2 * DN_H + h), (PAIR, 128))
    ri = _iota((PAIR, PAIR), 0)
    ci = _iota((PAIR, PAIR), 1)
    same = (ri // CH) == (ci // CH)
    causal = same & (ci <= ri)
    strict = same & (ci < ri)
    eye = ri == ci
    gc_cols = _hdot(jnp.ones((PAIR, PAIR), F32), jnp.where(eye, gc, 0.0))
    decay = jnp.exp(jnp.where(causal, gc - gc_cols, NEG))
    egc = jnp.exp(gc)
    kb = k * beta
    a_mat = jnp.where(strict, _bdot_nt(kb, k) * decay, 0.0)
    if tinv_saved is None:
        p = -a_mat
        tinv = jnp.where(eye, 1.0, 0.0) + p
        for _ in range(5):
            p = _h3(p, p)
            tinv = tinv + _h3(tinv, p)
    else:
        tinv = _inverse_given(a_mat, tinv_saved)
    u = _h3(tinv, v * beta)
    w = _h3(tinv, kb * egc)
    qs = q * (DN_HD ** -0.5)
    attn = jnp.where(causal, _bdot_nt(qs, k) * decay, 0.0)
    return u, w, qs * egc, k * jnp.exp(glast - gc), attn, jnp.exp(glast), tinv


def _dn_prep_fwd(qkv, bg, name):
    rows = 512
    hd = lambda off: pl.BlockSpec((rows, 128), lambda g, h: (g, off + h))
    out = pl.BlockSpec((rows, 128), lambda g, h: (g, h))

    def body(q_ref, k_ref, v_ref, bg_ref, *outs):
        h = pl.program_id(1)
        for pr in range(rows // PAIR):
            rs = slice(PAIR * pr, PAIR * pr + PAIR)
            res = _prep_math(q_ref[rs, :], k_ref[rs, :], v_ref[rs, :], bg_ref[rs, :], h)
            for o_ref, val in zip(outs, res):
                o_ref[rs, :] = val

    return pl.pallas_call(
        body, grid=(T // rows, DN_H),
        in_specs=[hd(0), hd(DN_H), hd(2 * DN_H), pl.BlockSpec((rows, 128), lambda g, h: (g, 0))],
        out_specs=[out] * 7, out_shape=[jax.ShapeDtypeStruct((T, DNW), F32)] * 7,
        name=name, compiler_params=_cp())(qkv, qkv, qkv, bg)


def _dn_prep_bwd(qkv, bg, tinv, cts, name):
    rows = 512
    hd = lambda off: pl.BlockSpec((rows, 128), lambda g, h: (g, off + h))
    out = pl.BlockSpec((rows, 128), lambda g, h: (g, h))
    st = pl.BlockSpec((rows, 128), lambda g, h: (g, 0))

    def body(q_ref, k_ref, v_ref, bg_ref, ti_ref, c0, c1, c2, c3, c4, c5, dq_ref, dk_ref, dv_ref, dbg_ref):
        h = pl.program_id(1)
        parts = []
        for pr in range(rows // PAIR):
            rs = slice(PAIR * pr, PAIR * pr + PAIR)
            fn = lambda q, k, v, b, ti=ti_ref[rs, :]: _prep_math(q, k, v, b, h, ti)[:6]
            _, vjp = jax.vjp(fn, q_ref[rs, :], k_ref[rs, :], v_ref[rs, :], bg_ref[rs, :])
            dq, dk, dv, dbg = vjp(tuple(c[rs, :] for c in (c0, c1, c2, c3, c4, c5)))
            dq_ref[rs, :] = dq
            dk_ref[rs, :] = dk
            dv_ref[rs, :] = dv
            parts.append(dbg)
        dbg_all = jnp.concatenate(parts, axis=0)

        @pl.when(h == 0)
        def _():
            dbg_ref[...] = dbg_all

        @pl.when(h > 0)
        def _():
            dbg_ref[...] += dbg_all

    return pl.pallas_call(
        body, grid=(T // rows, DN_H),
        in_specs=[hd(0), hd(DN_H), hd(2 * DN_H), st] + [out] * 7,
        out_specs=[out, out, out, st],
        out_shape=[jax.ShapeDtypeStruct((T, DNW), F32)] * 3 + [jax.ShapeDtypeStruct((T, 128), F32)],
        name=name, compiler_params=_cp())(qkv, qkv, qkv, bg, tinv, *cts)


def _step_math(s, u, w, qg, kdec, attn, decb, sub):
    vnew = u - _bdot_nn(w, s)
    z = jnp.zeros((CH, 128), F32)
    vfull = jnp.concatenate([vnew, z] if sub == 0 else [z, vnew], axis=0)
    o = _bdot_nn(qg, s) + _bdot_nn(attn, vfull)
    dec = jnp.sum(decb, axis=0, keepdims=True) * (1.0 / CH)
    return s * dec + _bdot_tn(kdec, vnew), o


def _dn_scan_fwd(prep, name):
    npair = T // PAIR
    row = pl.BlockSpec((PAIR, DNW), lambda p: (p, 0))

    def body(u_ref, w_ref, qg_ref, kd_ref, at_ref, db_ref, o_ref, ss_ref, s_ref):
        @pl.when(pl.program_id(0) == 0)
        def _():
            s_ref[...] = jnp.zeros_like(s_ref)

        for h in range(DN_H):
            ls = slice(128 * h, 128 * h + 128)
            s = s_ref[h]
            for sub in range(2):
                rs = slice(CH * sub, CH * sub + CH)
                ss_ref[sub, h] = s
                s, o = _step_math(s, u_ref[rs, ls], w_ref[rs, ls], qg_ref[rs, ls], kd_ref[rs, ls],
                                  at_ref[rs, ls], db_ref[rs, ls], sub)
                o_ref[rs, ls] = o
            s_ref[h] = s

    return pl.pallas_call(
        body, grid=(npair,), in_specs=[row] * 6,
        out_specs=[row, pl.BlockSpec((2, DN_H, 128, 128), lambda p: (p, 0, 0, 0))],
        out_shape=[jax.ShapeDtypeStruct((T, DNW), F32), jax.ShapeDtypeStruct((T // CH, DN_H, 128, 128), F32)],
        scratch_shapes=[pltpu.VMEM((DN_H, 128, 128), F32)],
        name=name, compiler_params=_cp())(*prep)


def _dn_scan_bwd(prep, states, do, name):
    npair = T // PAIR
    row = pl.BlockSpec((PAIR, DNW), lambda p: (npair - 1 - p, 0))

    def body(u_ref, w_ref, qg_ref, kd_ref, at_ref, db_ref, ss_ref, do_ref, *rest):
        outs, ds_ref = rest[:6], rest[6]

        @pl.when(pl.program_id(0) == 0)
        def _():
            ds_ref[...] = jnp.zeros_like(ds_ref)

        for h in range(DN_H):
            ls = slice(128 * h, 128 * h + 128)
            ds = ds_ref[h]
            for sub in (1, 0):
                rs = slice(CH * sub, CH * sub + CH)
                args = (ss_ref[sub, h],) + tuple(r[rs, ls] for r in (u_ref, w_ref, qg_ref, kd_ref, at_ref, db_ref))
                _, vjp = jax.vjp(functools.partial(_step_math, sub=sub), *args)
                cts = vjp((ds, do_ref[rs, ls]))
                ds = cts[0]
                for o_ref, val in zip(outs, cts[1:]):
                    o_ref[rs, ls] = val
            ds_ref[h] = ds

    return pl.pallas_call(
        body, grid=(npair,),
        in_specs=[row] * 6 + [pl.BlockSpec((2, DN_H, 128, 128), lambda p: (npair - 1 - p, 0, 0, 0)), row],
        out_specs=[row] * 6, out_shape=[jax.ShapeDtypeStruct((T, DNW), F32)] * 6,
        scratch_shapes=[pltpu.VMEM((DN_H, 128, 128), F32)],
        name=name, compiler_params=_cp())(*prep, states, do)


def _post_math(o, gate, wrow):
    return o * lax.rsqrt(jnp.mean(o * o, axis=-1, keepdims=True) + NORM_EPS) * wrow * _silu(gate)


def _dn_post_fwd(o, z_dn, dn_norm, name):
    tm = 512
    row = pl.BlockSpec((tm, DNW), lambda i: (i, 0))

    def body(o_ref, g_ref, w_ref, y_ref):
        for h in range(DN_H):
            ls = slice(128 * h, 128 * h + 128)
            y_ref[:, ls] = _post_math(o_ref[:, ls], g_ref[:, ls], w_ref[...]).astype(BF)

    return pl.pallas_call(
        body, grid=(T // tm,),
        in_specs=[row, pl.BlockSpec((tm, DNW), lambda i: (i, 3)), pl.BlockSpec((1, 128), lambda i: (0, 0))],
        out_specs=row, out_shape=jax.ShapeDtypeStruct((T, DNW), BF),
        name=name, compiler_params=_cp())(o, z_dn, dn_norm)


def _dn_post_bwd(o, z_dn, dn_norm, dy, name):
    tm = 512
    row = pl.BlockSpec((tm, DNW), lambda i: (i, 0))
    one = pl.BlockSpec((1, 128), lambda i: (0, 0))

    def body(o_ref, g_ref, w_ref, dy_ref, do_ref, dg_ref, dw_ref):
        i = pl.program_id(0)
        dw = jnp.zeros((1, 128), F32)
        for h in range(DN_H):
            ls = slice(128 * h, 128 * h + 128)
            _, vjp = jax.vjp(_post_math, o_ref[:, ls], g_ref[:, ls], w_ref[...])
            do, dg, dwh = vjp(dy_ref[:, ls].astype(F32))
            do_ref[:, ls] = do
            dg_ref[:, ls] = dg
            dw = dw + dwh

        @pl.when(i == 0)
        def _():
            dw_ref[...] = dw

        @pl.when(i > 0)
        def _():
            dw_ref[...] += dw

    return pl.pallas_call(
        body, grid=(T // tm,),
        in_specs=[row, pl.BlockSpec((tm, DNW), lambda i: (i, 3)), one, row],
        out_specs=[row, row, one],
        out_shape=[jax.ShapeDtypeStruct((T, DNW), F32), jax.ShapeDtypeStruct((T, DNW), F32),
                   jax.ShapeDtypeStruct((1, 128), F32)],
        name=name, compiler_params=_cp())(o, z_dn, dn_norm, dy)


def _dz_dn_assemble(draw, dgate, dbd, name):
    tm = 512

    def body(a_ref, b_ref, c_ref, o_ref):
        o_ref[:, :3 * DNW] = a_ref[...].astype(BF)
        o_ref[:, 3 * DNW:4 * DNW] = b_ref[...].astype(BF)
        o_ref[:, 4 * DNW:4 * DNW + 128] = c_ref[...].astype(BF)
        o_ref[:, 4 * DNW + 128:] = jnp.zeros((tm, 128), BF)

    return pl.pallas_call(
        body, grid=(T // tm,),
        in_specs=[pl.BlockSpec((tm, 3 * DNW), lambda i: (i, 0)), pl.BlockSpec((tm, DNW), lambda i: (i, 0)),
                  pl.BlockSpec((tm, 128), lambda i: (i, 0))],
        out_specs=pl.BlockSpec((tm, ZD), lambda i: (i, 0)),
        out_shape=jax.ShapeDtypeStruct((T, ZD), BF), name=name, compiler_params=_cp())(draw, dgate, dbd)


HBM = pl.BlockSpec(memory_space=pltpu.HBM)
SEM = pl.BlockSpec(memory_space=pltpu.SEMAPHORE)
EFFECT = pltpu.SideEffectType.DATAFLOW_SIDE_EFFECTING
N_PEER = N_DEV - 1


def _peers(x, y, c):
    return [(k, (x ^ (k >> 2), y ^ ((k >> 1) & 1), c ^ (k & 1))) for k in (1, 2, 4, 3, 5, 6, 7)]


def _exchange_copy(ins, lands, ssems, rsems, scatter, t, k, pos, me):
    px, py, pc = pos
    src = ins[t].at[4 * px + 2 * py + pc] if scatter else ins[t]
    return pltpu.make_async_remote_copy(
        src_ref=src, dst_ref=lands[t].at[me], send_sem=ssems[t].at[k - 1], recv_sem=rsems[t].at[k - 1],
        device_id=pos, device_id_type=MESH_ID)


def _xstart(bufs, scatter, name):
    nt = len(bufs)
    lands = [lax.empty((N_DEV,) + tuple(b.shape[1:] if scatter else b.shape), b.dtype) for b in bufs]

    def body(*refs):
        ins, lnd = refs[:nt], refs[nt:2 * nt]
        ssems, rsems = refs[2 * nt:3 * nt], refs[3 * nt:4 * nt]
        token = refs[-1]
        x, y, c = lax.axis_index("x"), lax.axis_index("y"), lax.axis_index("c")
        me = 4 * x + 2 * y + c
        for t in range(nt):
            for k, pos in _peers(x, y, c):
                _exchange_copy(ins, lnd, ssems, rsems, scatter, t, k, pos, me).start()
        token[...] = jnp.zeros_like(token)

    both = list(bufs) + lands
    res = pl.pallas_call(
        body, name=name,
        out_shape=[pltpu.SemaphoreType.DMA((N_PEER,))] * (2 * nt)
        + [pltpu.HBM(b.shape, b.dtype) for b in both] + [jax.ShapeDtypeStruct((8, 128), F32)],
        in_specs=[HBM] * (2 * nt),
        out_specs=[SEM] * (2 * nt) + [HBM] * (2 * nt) + [pl.BlockSpec(memory_space=pltpu.VMEM)],
        input_output_aliases={i: 2 * nt + i for i in range(2 * nt)},
        compiler_params=pltpu.CompilerParams(has_side_effects=EFFECT),
    )(*[pltpu.with_memory_space_constraint(b, pltpu.HBM) for b in both])
    return res[:nt], res[nt:2 * nt], res[2 * nt:3 * nt], res[3 * nt:4 * nt], res[-1][0, 0]


def _xwait(ssems, rsems, thrus, lands, scatter, after, name):
    nt = len(lands)

    def body(*refs):
        ins, lnd = refs[:nt], refs[nt:2 * nt]
        ss, rs = refs[2 * nt:3 * nt], refs[3 * nt:4 * nt]
        x, y, c = lax.axis_index("x"), lax.axis_index("y"), lax.axis_index("c")
        me = 4 * x + 2 * y + c
        for t in range(nt):
            for k, pos in _peers(x, y, c):
                cp = _exchange_copy(ins, lnd, ss, rs, scatter, t, k, pos, me)
                cp.wait_send()
                cp.wait_recv()

    both = list(thrus) + list(lands)
    res = pl.pallas_call(
        body, name=name, out_shape=[pltpu.HBM(b.shape, b.dtype) for b in both],
        in_specs=[HBM] * (2 * nt) + [SEM] * (2 * nt) + [ANY], out_specs=[HBM] * (2 * nt),
        input_output_aliases={i: i for i in range(2 * nt)},
        compiler_params=pltpu.CompilerParams(has_side_effects=EFFECT),
    )(*both, *ssems, *rsems, after)
    return res[:nt], res[nt:]


def _adam(recv, w, m, v, tr, name):
    _, r, c = w.shape
    c1 = np.float32(1.0 - ADAM_B1 ** ADAM_STEP)
    c2 = np.float32(1.0 - ADAM_B2 ** ADAM_STEP)

    def body(r_ref, w_ref, m_ref, v_ref, g_ref, d_ref, mo_ref, vo_ref):
        g = r_ref[0].astype(F32)
        for s in range(1, N_DEV):
            g = g + r_ref[s].astype(F32)
        mn = ADAM_B1 * m_ref[0] + (1.0 - ADAM_B1) * g
        vn = ADAM_B2 * v_ref[0] + (1.0 - ADAM_B2) * (g * g)
        g_ref[0] = g
        mo_ref[0] = mn
        vo_ref[0] = vn
        d_ref[0] = -ADAM_LR * ((mn / c1) / (jnp.sqrt(vn / c2) + ADAM_EPS) + ADAM_WD * w_ref[0])

    one = pl.BlockSpec((1, tr, c), lambda i: (0, i, 0))
    return pl.pallas_call(
        body, grid=(r // tr,), in_specs=[pl.BlockSpec((N_DEV, tr, c), lambda i: (0, i, 0)), one, one, one],
        out_specs=[one] * 4, out_shape=[jax.ShapeDtypeStruct((1, r, c), F32)] * 4,
        name=name, compiler_params=_cp())(recv, w, m, v)


def _residues(a, dil, width):
    return a.reshape(T // dil, dil * width)


def _local_step(x, target, sp, need, emit):
    g = {}
    x0, h1, gu1, act1, y1, wgu1, wd1 = _ffn_fwd(
        x, None, 0.0, sp["norm_ffn1"], lambda a: need("wgu1", a), lambda a: need("wd1", a), "ffn1")
    x1, h2 = _rms_fwd(x0, y1, 0.5, sp["norm_mix"], "mix_norm")
    win_a, win_d = need("win_a", h2), need("win_d", h2)
    conv_w8, wout = need("conv_w8", h2), need("wout", h2)
    z_at = _mm_nn(h2, win_a, BF, 1024, 768, "mix_in_attn")
    z_dn = _mm_nn(h2, win_d, F32, 1024, 768, "mix_in_dn")

    qkv_r, os_, ms_, ls_ = [], [], [], []
    for i, (_, dil) in enumerate(PATTERNS):
        qr = _residues(z_at, dil, ZA)
        o, m, l = _attn_fwd(qr, dil, "attn_fwd%d" % i)
        qkv_r.append(qr)
        os_.append(o.reshape(T, AW))
        ms_.append(m.reshape(T, 128))
        ls_.append(l.reshape(T, 128))
    attn_f, attn_b, lse = _attn_merge(os_, ms_, ls_, "attn_merge")

    conv, qkvn, bg = _dn_pre_fwd(z_dn, conv_w8, sp["alog_row"], sp["dtb_row"], "dn_pre")
    *prep, tinv = _dn_prep_fwd(qkvn, bg, "dn_prep")
    o_dn, states = _dn_scan_fwd(prep, "dn_scan")
    dn_b = _dn_post_fwd(o_dn, z_dn, sp["dn_norm"], "dn_post")

    mix = jnp.concatenate([attn_b, dn_b], axis=1)
    y2 = _mm_nn(mix, wout, F32, 1024, 1024, "mix_out")
    x2, h3, gu2, act2, y3, wgu2, wd2 = _ffn_fwd(
        x1, y2, 1.0, sp["norm_ffn2"], lambda a: need("wgu2", a), lambda a: need("wd2", a), "ffn2")
    x3, _ = _rms_fwd(x2, y3, 0.5, sp["norm_final"], "final_add")

    dx3, dys3, g["norm_final"], loss8 = _loss_bwd(x3, sp["norm_final"], target, "loss")
    dx2, dx2b, g["norm_ffn2"] = _ffn_bwd(
        x2, sp["norm_ffn2"], h3, gu2, act2, wgu2, wd2, dx3, dys3, 1.0,
        lambda dwgu, dwd: emit("ffn2", {"wgu2": dwgu, "wd2": dwd}), "ffn2b")

    zero = emit("wout", {"wout": _mm_tn(mix, dx2b, 512, 1024, "mix_out_dw")})
    dmix = _mm_nt(dx2b, wout, F32, 1024, 1024, "mix_out_dx")
    d_attn = dmix[:, :AW]
    d_dn = dmix[:, AW:]

    dl = _attn_delta(d_attn, attn_f, "attn_delta")
    parts = []
    for i, (_, dil) in enumerate(PATTERNS):
        do_r = _residues(d_attn, dil, AW)
        lse_r = _residues(lse, dil, 128)
        dl_r = _residues(dl, dil, 128)
        dq = _attn_bwd_dq(qkv_r[i], do_r, lse_r, dl_r, dil, "attn_dq%d" % i)
        dk, dv = _attn_bwd_dkv(qkv_r[i], do_r, lse_r, dl_r, dil, "attn_dkv%d" % i)
        parts += [dq.reshape(T, AW), dk.reshape(T, AW), dv.reshape(T, AW)]
    dz_at = _sum3_cast(parts, "attn_dsum")

    do_dn, dgate, g["dn_norm"] = _dn_post_bwd(o_dn, z_dn, sp["dn_norm"] + zero, d_dn, "dn_post_b")
    cts = _dn_scan_bwd(prep, states, do_dn, "dn_scan_b")
    dqn, dkn, dvn, dbg = _dn_prep_bwd(qkvn, bg, tinv, cts, "dn_prep_b")
    dconv, dbd, g["alog_row"], g["dtb_row"] = _dn_pre_bwd(
        conv, z_dn, sp["alog_row"], sp["dtb_row"], dqn, dkn, dvn, dbg, "dn_pre_b")
    draw, dconv_w8 = _dn_conv_bwd(dconv, z_dn, conv_w8, "dn_conv_b")
    dz_dn = _dz_dn_assemble(draw, dgate, dbd, "dn_dz")

    zero = emit("win", {"win_a": _mm_tn(h2, dz_at, 512, 768, "mix_in_dw_a"),
                        "win_d": _mm_tn(h2, dz_dn, 512, 768, "mix_in_dw_d"), "conv_w8": dconv_w8})
    dh2 = _mm_nt(dz_at, win_a, F32, 1024, 1024, "mix_in_dx_a")
    dh2d = _mm_nt(dz_dn, win_d, F32, 1024, 1024, "mix_in_dx_d")
    (dx1, dys1), _, g["norm_mix"] = _rms_bwd2(x1, sp["norm_mix"] + zero, dh2, dh2d, dx2, "mix_dnorm")
    dx0, _, g["norm_ffn1"] = _ffn_bwd(
        x0, sp["norm_ffn1"], h1, gu1, act1, wgu1, wd1, dx1, dys1, 1.0,
        lambda dwgu, dwd: emit("ffn1", {"wgu1": dwgu, "wd1": dwd}), "ffn1b")
    return loss8[0, 0], dx0, g


def _rms_bwd2(x, gain, dh_a, dh_b, dres, name):
    tm = 512
    row = pl.BlockSpec((tm, D), lambda i: (i, 0))
    gspec = pl.BlockSpec((1, D), lambda i: (0, 0))

    def body(x_ref, g_ref, da_ref, db_ref, dres_ref, dx_ref, dxs_ref, dg_ref):
        i = pl.program_id(0)
        xv = x_ref[...]
        r = lax.rsqrt(jnp.mean(xv * xv, axis=-1, keepdims=True) + NORM_EPS)
        xh = xv * r
        dhv = da_ref[...] + db_ref[...]
        part = jnp.sum(dhv * xh, axis=0, keepdims=True)

        @pl.when(i == 0)
        def _():
            dg_ref[...] = part

        @pl.when(i > 0)
        def _():
            dg_ref[...] += part

        dxh = dhv * g_ref[...]
        dx = r * (dxh - xh * jnp.mean(dxh * xh, axis=-1, keepdims=True)) + dres_ref[...]
        dx_ref[...] = dx
        dxs_ref[...] = (0.5 * dx).astype(BF)

    dx, dxs, dg = pl.pallas_call(
        body, grid=(T // tm,), in_specs=[row, gspec, row, row, row], out_specs=[row, row, gspec],
        out_shape=[jax.ShapeDtypeStruct((T, D), F32), jax.ShapeDtypeStruct((T, D), BF),
                   jax.ShapeDtypeStruct((1, D), F32)],
        name=name, compiler_params=_cp())(x, gain, dh_a, dh_b, dres)
    return (dx, dxs), None, dg


def _cols_from_shards(gathered):
    n, r, c = gathered.shape
    return jnp.transpose(gathered, (1, 0, 2)).reshape(r, n * c)


def _shards_from_cols(full, dtype):
    r, nc = full.shape
    return jnp.transpose(full.reshape(r, N_DEV, nc // N_DEV), (1, 0, 2)).astype(dtype)


def _lane_row(vec4):
    return jnp.zeros((1, 128), F32).at[:, DN_H:2 * DN_H].set(vec4.astype(F32))


WEIGHT_SOURCES = {"wgu1": ("gate1", "up1"), "wd1": ("down1",), "win_a": ("w_in",), "win_d": ("w_in",),
                  "conv_w8": ("conv_w",), "wout": ("w_out",), "wgu2": ("gate2", "up2"), "wd2": ("down2",)}


def _build_weights(name, gath):
    if name in ("wgu1", "wgu2"):
        i = name[-1]
        return {name: jnp.concatenate(
            [_cols_from_shards(gath["gate" + i]), _cols_from_shards(gath["up" + i])], axis=1)}
    if name in ("wd1", "wd2"):
        return {name: gath["down" + name[-1]].reshape(F, D)}
    if name in ("win_a", "win_d"):
        w_in = _cols_from_shards(gath["w_in"])
        c0 = 3 * AW + 3 * DNW
        wp = jnp.concatenate([w_in[:, :c0], w_in[:, c0 + 2 * DN_H:], w_in[:, c0:c0 + 2 * DN_H],
                              jnp.zeros((D, ZP - IN_COLS), w_in.dtype)], axis=1)
        return {"win_a": wp[:, :ZA], "win_d": wp[:, ZA:]}
    if name == "wout":
        return {name: gath["w_out"].reshape(D, D)}
    conv = _cols_from_shards(gath["conv_w"])
    return {"conv_w8": jnp.concatenate([conv, jnp.zeros((4, 3 * DNW), F32)], axis=0)}


def _small_params(norm_ffn1, norm_mix, norm_ffn2, norm_final, a_log, dt_bias, dn_norm):
    return {"norm_ffn1": norm_ffn1, "norm_mix": norm_mix, "norm_ffn2": norm_ffn2,
            "norm_final": norm_final.reshape(1, D), "alog_row": _lane_row(a_log), "dtb_row": _lane_row(dt_bias),
            "dn_norm": dn_norm}


def _grad_slabs(group, g):
    if group in ("ffn1", "ffn2"):
        i = group[-1]
        return {"gate" + i: _shards_from_cols(g["wgu" + i][:, :F], BF),
                "up" + i: _shards_from_cols(g["wgu" + i][:, F:], BF),
                "down" + i: g["wd" + i].reshape(N_DEV, F // N_DEV, D).astype(BF)}
    if group == "wout":
        return {"w_out": g["wout"].reshape(N_DEV, D // N_DEV, D).astype(BF)}
    gp = jnp.concatenate([g["win_a"], g["win_d"]], axis=1)
    c0 = 3 * AW + 3 * DNW
    g_in = jnp.concatenate([gp[:, :c0], gp[:, c0 + DNW:c0 + DNW + 2 * DN_H], gp[:, c0:c0 + DNW]], axis=1)
    return {"w_in": _shards_from_cols(g_in, BF), "conv_w": _shards_from_cols(g["conv_w8"][:4], F32)}


SMALL_ROWS = 40


def _small_pack(norm_ffn1, norm_mix, norm_ffn2, norm_final, dn_norm, alog_row, dtb_row):
    rows = [a.reshape(8, 128) for a in (norm_ffn1, norm_mix, norm_ffn2, norm_final)]
    rows += [dn_norm.reshape(1, 128), alog_row, dtb_row, jnp.zeros((SMALL_ROWS - 35, 128), F32)]
    return jnp.concatenate(rows, axis=0)


def _small_unpack(pk):
    pk = pk[0]
    return (pk[0:8].reshape(1, D), pk[8:16].reshape(1, D), pk[16:24].reshape(1, D), pk[24:32].reshape(D),
            pk[32:33], pk[33:34, DN_H:2 * DN_H], pk[34:35, DN_H:2 * DN_H])


ADAM_TILE = {"gate1": 256, "up1": 256, "down1": 176, "gate2": 256, "up2": 256, "down2": 176,
             "w_in": 256, "w_out": 128, "conv_w": 4}
BIG = ("gate1", "up1", "down1", "w_in", "w_out", "gate2", "up2", "down2", "conv_w")


def kernel(x, norm_ffn1, ffn1_gate, ffn1_up, ffn1_down, norm_mix, w_in, conv_w, a_log, dt_bias, dn_norm, w_out, norm_ffn2, ffn2_gate, ffn2_up, ffn2_down, norm_final, loss_target, m_norm_ffn1, m_ffn1_gate, m_ffn1_up, m_ffn1_down, m_norm_mix, m_w_in, m_conv_w, m_a_log, m_dt_bias, m_dn_norm, m_w_out, m_norm_ffn2, m_ffn2_gate, m_ffn2_up, m_ffn2_down, m_norm_final, v_norm_ffn1, v_ffn1_gate, v_ffn1_up, v_ffn1_down, v_norm_mix, v_w_in, v_conv_w, v_a_log, v_dt_bias, v_dn_norm, v_w_out, v_norm_ffn2, v_ffn2_gate, v_ffn2_up, v_ffn2_down, v_norm_final):
    w = {"gate1": ffn1_gate, "up1": ffn1_up, "down1": ffn1_down, "w_in": w_in, "w_out": w_out,
         "gate2": ffn2_gate, "up2": ffn2_up, "down2": ffn2_down, "conv_w": conv_w}
    m = {"gate1": m_ffn1_gate, "up1": m_ffn1_up, "down1": m_ffn1_down, "w_in": m_w_in, "w_out": m_w_out,
         "gate2": m_ffn2_gate, "up2": m_ffn2_up, "down2": m_ffn2_down, "conv_w": m_conv_w}
    v = {"gate1": v_ffn1_gate, "up1": v_ffn1_up, "down1": v_ffn1_down, "w_in": v_w_in, "w_out": v_w_out,
         "gate2": v_ffn2_gate, "up2": v_ffn2_up, "down2": v_ffn2_down, "conv_w": v_conv_w}

    me = 4 * lax.axis_index("x") + 2 * lax.axis_index("y") + lax.axis_index("c")
    own_slot = lambda land, mine: lax.dynamic_update_index_in_dim(land, mine, me, 0)

    ag_order = ("gate1", "up1", "down1", "w_in", "conv_w", "w_out", "gate2", "up2", "down2")
    ag_groups = (("gate1", "up1"), ("down1",), ("w_in", "conv_w", "w_out"), ("gate2", "up2", "down2"))
    pos = {n: i for i, n in enumerate(ag_order)}
    ss, rs, thru, land, zero = _xstart(
        [w[n][0].astype(F32 if n == "conv_w" else BF) for n in ag_order], False, "weights_start")
    gath, built = {}, {}

    def need(name, after):
        if name not in built:
            for src in WEIGHT_SOURCES[name]:
                if src not in gath:
                    gi = [i for i, grp in enumerate(ag_groups) if src in grp][0]
                    ids = [pos[n] for n in ag_groups[gi]]
                    thrus, lands = _xwait([ss[i] for i in ids], [rs[i] for i in ids], [thru[i] for i in ids],
                                          [land[i] for i in ids], False, after, "weights_wait%d" % gi)
                    for n, t, l in zip(ag_groups[gi], thrus, lands):
                        gath[n] = own_slot(l, t)
            built.update(_build_weights(name, gath))
        return built[name]

    pending = []

    def emit(group, grads):
        slabs = grads if group == "small" else _grad_slabs(group, grads)
        names = list(slabs)
        started = _xstart([slabs[n] for n in names], True, "grads_start_" + group)
        pending.append((group, names) + started[:4])
        return started[4]

    sp = _small_params(norm_ffn1 + zero, norm_mix, norm_ffn2, norm_final, a_log, dt_bias, dn_norm)
    loss_part, dx, g = _local_step(x[0], loss_target[0], sp, need, emit)
    small = _small_pack(g["norm_ffn1"], g["norm_mix"], g["norm_ffn2"], g["norm_final"], g["dn_norm"],
                        g["alog_row"], g["dtb_row"])
    emit("small", {"small": jnp.broadcast_to(small[None], (N_DEV, SMALL_ROWS, 128))})

    pack = lambda a: _small_pack(*a)[None]
    res, after = {}, dx
    for group, names, gss, grs, gthru, gland in pending:
        thrus, lands = _xwait(gss, grs, gthru, gland, True, after, "grads_wait_" + group)
        for n, t, l in zip(names, thrus, lands):
            recv = own_slot(l, lax.dynamic_index_in_dim(t, me, 0, keepdims=False))
            if n == "small":
                res[n] = _adam(
                    recv,
                    pack((norm_ffn1, norm_mix, norm_ffn2, norm_final, dn_norm, _lane_row(a_log), _lane_row(dt_bias))),
                    pack((m_norm_ffn1, m_norm_mix, m_norm_ffn2, m_norm_final, m_dn_norm, _lane_row(m_a_log),
                          _lane_row(m_dt_bias))),
                    pack((v_norm_ffn1, v_norm_mix, v_norm_ffn2, v_norm_final, v_dn_norm, _lane_row(v_a_log),
                          _lane_row(v_dt_bias))),
                    SMALL_ROWS, "adam_small")
            else:
                res[n] = _adam(recv, w[n], m[n], v[n], ADAM_TILE[n], "adam_" + n)
            after = res[n][0]
    res_s = res["small"]

    loss = lax.psum(loss_part, ("x", "y", "c"))
    outs = [loss, dx[None]]
    for k in range(4):
        n1, nm, n2, nf, dn, al, dt = _small_unpack(res_s[k])
        big = {n: res[n][k] for n in BIG}
        outs += [n1, big["gate1"], big["up1"], big["down1"], nm, big["w_in"], big["conv_w"], al, dt, dn,
                 big["w_out"], n2, big["gate2"], big["up2"], big["down2"], nf]
    return tuple(outs)
```

```python
import functools

import numpy as np
import jax
import jax.numpy as jnp
from jax import lax
from jax.experimental import pallas as pl
from jax.experimental.pallas import tpu as pltpu

T = 4096
D = 1024
F = 2816
N_DEV = 8
A_HEADS = 8
A_HD = 64
AW = A_HEADS * A_HD
DN_H = 4
DN_HD = 128
DNW = DN_H * DN_HD
CH = 64
PAIR = 2 * CH
ZA = 3 * AW
ZD = 3 * DNW + DNW + 256
ZP = ZA + ZD
BD_BLK = (3 * DNW + DNW) // 128
IN_COLS = 3592
PATTERNS = ((128, 1), (512, 4), (2048, 16))
NORM_EPS = 1e-6
L2_EPS = 1e-6
ADAM_LR, ADAM_B1, ADAM_B2, ADAM_EPS, ADAM_WD, ADAM_STEP = 0.001, 0.9, 0.999, 1e-08, 0.01, 10
VMEM_LIMIT = 56 * 1024 * 1024
NEG = -1e30

BF = jnp.bfloat16
F32 = jnp.float32
NN = (((1,), (0,)), ((), ()))
NT = (((1,), (1,)), ((), ()))
TN = (((0,), (0,)), ((), ()))
HI = lax.Precision.HIGHEST
MESH_ID = pl.DeviceIdType.MESH
ANY = pl.BlockSpec(memory_space=pl.ANY)


def _cp():
    return pltpu.CompilerParams(vmem_limit_bytes=VMEM_LIMIT)


def _dg(a, b, dims):
    return lax.dot_general(a, b, dims, preferred_element_type=F32)


def _hdot(a, b):
    return lax.dot_general(a, b, NN, precision=HI, preferred_element_type=F32)


def _make_bdot(dims, da_dims, da_swap, db_dims, db_swap):
    @jax.custom_vjp
    def f(a, b):
        return _dg(a.astype(BF), b.astype(BF), dims)

    def fwd(a, b):
        return f(a, b), (a, b)

    def bwd(res, g):
        a, b = res
        gb, ab, bb = g.astype(BF), a.astype(BF), b.astype(BF)
        da = _dg(bb, gb, da_dims) if da_swap else _dg(gb, bb, da_dims)
        db = _dg(gb, ab, db_dims) if db_swap else _dg(ab, gb, db_dims)
        return da, db

    f.defvjp(fwd, bwd)
    return f


_bdot_nn = _make_bdot(NN, NT, False, TN, False)
_bdot_nt = _make_bdot(NT, NN, False, TN, True)
_bdot_tn = _make_bdot(TN, NT, True, NN, False)


def _iota(shape, dim):
    return lax.broadcasted_iota(jnp.int32, shape, dim)


def _col(x, idx):
    return jnp.sum(jnp.where(_iota(x.shape, 1) == idx, x, 0.0), axis=1, keepdims=True)


def _mm_nn(a, b, out_dtype, tm, tn, name):
    m, k = a.shape
    n = b.shape[1]

    def body(a_ref, b_ref, o_ref):
        o_ref[...] = _dg(a_ref[...], b_ref[...], NN).astype(out_dtype)

    return pl.pallas_call(
        body, grid=(m // tm, n // tn),
        in_specs=[pl.BlockSpec((tm, k), lambda i, j: (i, 0)), pl.BlockSpec((k, tn), lambda i, j: (0, j))],
        out_specs=pl.BlockSpec((tm, tn), lambda i, j: (i, j)),
        out_shape=jax.ShapeDtypeStruct((m, n), out_dtype), name=name, compiler_params=_cp())(a, b)


def _mm_nt(a, b, out_dtype, tm, tb, name):
    m, c = a.shape
    kb = b.shape[0]

    def body(a_ref, b_ref, o_ref):
        o_ref[...] = _dg(a_ref[...], b_ref[...], NT).astype(out_dtype)

    return pl.pallas_call(
        body, grid=(m // tm, kb // tb),
        in_specs=[pl.BlockSpec((tm, c), lambda i, j: (i, 0)), pl.BlockSpec((tb, c), lambda i, j: (j, 0))],
        out_specs=pl.BlockSpec((tm, tb), lambda i, j: (i, j)),
        out_shape=jax.ShapeDtypeStruct((m, kb), out_dtype), name=name, compiler_params=_cp())(a, b)


def _mm_tn(a, b, ta, tb, name):
    m, ka = a.shape
    nb = b.shape[1]

    def body(a_ref, b_ref, o_ref):
        o_ref[...] = _dg(a_ref[...], b_ref[...], TN)

    return pl.pallas_call(
        body, grid=(ka // ta, nb // tb),
        in_specs=[pl.BlockSpec((m, ta), lambda i, j: (0, i)), pl.BlockSpec((m, tb), lambda i, j: (0, j))],
        out_specs=pl.BlockSpec((ta, tb), lambda i, j: (i, j)),
        out_shape=jax.ShapeDtypeStruct((ka, nb), F32), name=name, compiler_params=_cp())(a, b)


def _rms_fwd(x, y, alpha, gain, name):
    tm = 512
    has_y = y is not None
    row = pl.BlockSpec((tm, D), lambda i: (i, 0))
    gspec = pl.BlockSpec((1, D), lambda i: (0, 0))

    def body(*refs):
        if has_y:
            x_ref, y_ref, g_ref, xo_ref, h_ref = refs
            xv = x_ref[...] + alpha * y_ref[...]
            xo_ref[...] = xv
        else:
            x_ref, g_ref, h_ref = refs
            xv = x_ref[...]
        r = lax.rsqrt(jnp.mean(xv * xv, axis=-1, keepdims=True) + NORM_EPS)
        h_ref[...] = (xv * r * g_ref[...]).astype(BF)

    if has_y:
        return pl.pallas_call(
            body, grid=(T // tm,), in_specs=[row, row, gspec], out_specs=[row, row],
            out_shape=[jax.ShapeDtypeStruct((T, D), F32), jax.ShapeDtypeStruct((T, D), BF)],
            name=name, compiler_params=_cp())(x, y, gain)
    h = pl.pallas_call(
        body, grid=(T // tm,), in_specs=[row, gspec], out_specs=row,
        out_shape=jax.ShapeDtypeStruct((T, D), BF), name=name, compiler_params=_cp())(x, gain)
    return x, h


def _rms_bwd(x, gain, dh, dres, alpha_out, name):
    tm = 512
    row = pl.BlockSpec((tm, D), lambda i: (i, 0))
    gspec = pl.BlockSpec((1, D), lambda i: (0, 0))

    def body(x_ref, g_ref, dh_ref, dres_ref, dx_ref, dxs_ref, dg_ref):
        i = pl.program_id(0)
        xv = x_ref[...]
        r = lax.rsqrt(jnp.mean(xv * xv, axis=-1, keepdims=True) + NORM_EPS)
        xh = xv * r
        dhv = dh_ref[...].astype(F32)
        part = jnp.sum(dhv * xh, axis=0, keepdims=True)

        @pl.when(i == 0)
        def _():
            dg_ref[...] = part

        @pl.when(i > 0)
        def _():
            dg_ref[...] += part

        dxh = dhv * g_ref[...]
        dx = r * (dxh - xh * jnp.mean(dxh * xh, axis=-1, keepdims=True)) + dres_ref[...]
        dx_ref[...] = dx
        dxs_ref[...] = (alpha_out * dx).astype(BF)

    return pl.pallas_call(
        body, grid=(T // tm,), in_specs=[row, gspec, row, row], out_specs=[row, row, gspec],
        out_shape=[jax.ShapeDtypeStruct((T, D), F32), jax.ShapeDtypeStruct((T, D), BF),
                   jax.ShapeDtypeStruct((1, D), F32)],
        name=name, compiler_params=_cp())(x, gain, dh, dres)


def _loss_bwd(x, gain, target, name):
    tm = 512
    row = pl.BlockSpec((tm, D), lambda i: (i, 0))
    gspec = pl.BlockSpec((1, D), lambda i: (0, 0))
    lspec = pl.BlockSpec((8, 128), lambda i: (0, 0))

    def body(x_ref, g_ref, t_ref, dx_ref, dxs_ref, dg_ref, loss_ref):
        i = pl.program_id(0)
        xv = x_ref[...]
        r = lax.rsqrt(jnp.mean(xv * xv, axis=-1, keepdims=True) + NORM_EPS)
        xh = xv * r
        diff = xh * g_ref[...] - t_ref[...]
        lpart = 0.5 * jnp.sum(jnp.mean(diff * diff, axis=-1, keepdims=True), axis=0, keepdims=True)
        dy = diff * (1.0 / D)
        part = jnp.sum(dy * xh, axis=0, keepdims=True)

        @pl.when(i == 0)
        def _():
            dg_ref[...] = part
            loss_ref[...] = jnp.broadcast_to(lpart, (8, 128))

        @pl.when(i > 0)
        def _():
            dg_ref[...] += part
            loss_ref[...] += jnp.broadcast_to(lpart, (8, 128))

        dxh = dy * g_ref[...]
        dx = r * (dxh - xh * jnp.mean(dxh * xh, axis=-1, keepdims=True))
        dx_ref[...] = dx
        dxs_ref[...] = (0.5 * dx).astype(BF)

    return pl.pallas_call(
        body, grid=(T // tm,), in_specs=[row, gspec, row], out_specs=[row, row, gspec, lspec],
        out_shape=[jax.ShapeDtypeStruct((T, D), F32), jax.ShapeDtypeStruct((T, D), BF),
                   jax.ShapeDtypeStruct((1, D), F32), jax.ShapeDtypeStruct((8, 128), F32)],
        name=name, compiler_params=_cp())(x, gain, target)


def _swiglu_fwd(gu, name):
    tm = 256

    def body(gu_ref, a_ref):
        g = gu_ref[:, :F].astype(F32)
        u = gu_ref[:, F:].astype(F32)
        a_ref[...] = (g * jax.nn.sigmoid(g) * u).astype(BF)

    return pl.pallas_call(
        body, grid=(T // tm,), in_specs=[pl.BlockSpec((tm, 2 * F), lambda i: (i, 0))],
        out_specs=pl.BlockSpec((tm, F), lambda i: (i, 0)),
        out_shape=jax.ShapeDtypeStruct((T, F), BF), name=name, compiler_params=_cp())(gu)


def _swiglu_bwd(gu, da, name):
    tm = 256

    def body(gu_ref, da_ref, o_ref):
        g = gu_ref[:, :F].astype(F32)
        u = gu_ref[:, F:].astype(F32)
        dav = da_ref[...].astype(F32)
        s = jax.nn.sigmoid(g)
        o_ref[:, :F] = (dav * u * (s * (1.0 + g * (1.0 - s)))).astype(BF)
        o_ref[:, F:] = (dav * (g * s)).astype(BF)

    return pl.pallas_call(
        body, grid=(T // tm,),
        in_specs=[pl.BlockSpec((tm, 2 * F), lambda i: (i, 0)), pl.BlockSpec((tm, F), lambda i: (i, 0))],
        out_specs=pl.BlockSpec((tm, 2 * F), lambda i: (i, 0)),
        out_shape=jax.ShapeDtypeStruct((T, 2 * F), BF), name=name, compiler_params=_cp())(gu, da)


def _ffn_fwd(x_prev, y_prev, alpha, gain, need_gu, need_down, tag):
    x_in, h = _rms_fwd(x_prev, y_prev, alpha, gain, tag + "_norm")
    wgu = need_gu(h)
    gu = _mm_nn(h, wgu, BF, 1024, 1408, tag + "_gu")
    wd = need_down(gu)
    act = _swiglu_fwd(gu, tag + "_act")
    y = _mm_nn(act, wd, F32, 1024, 1024, tag + "_down")
    return x_in, h, gu, act, y, wgu, wd


def _ffn_bwd(x_in, gain, h, gu, act, wgu, wd, dxo, dys, alpha_out, emit, tag):
    zero = emit("d", _mm_tn(act, dys, 256, 1024, tag + "_dwd"))
    dact = _mm_nt(dys, wd, BF, 1024, 1408, tag + "_dact")
    dgu = _swiglu_bwd(gu, dact, tag + "_dgu")
    zero = zero + emit("gu", _mm_tn(h, dgu, 512, 512, tag + "_dwgu"))
    dh = _mm_nt(dgu, wgu, F32, 512, 512, tag + "_dh")
    dx, dxs, dgain = _rms_bwd(x_in, gain + zero, dh, dxo, alpha_out, tag + "_dnorm")
    return dx, dxs, dgain


SLAB = 2048
N_SLAB = T // SLAB
N_PAIR = A_HEADS // 2


def _pair_masks():
    lane = _iota((128, 128), 1)
    return lane < A_HD, lane >= A_HD


def _slope_table():
    h = 2 * jnp.arange(N_PAIR)[:, None] + jnp.minimum(jnp.arange(8), 1)[None, :]
    return jnp.broadcast_to((2.0 ** (-(h + 1).astype(F32)))[:, :, None], (N_PAIR, 8, 128))


def _rows(ref, start, d):
    if d == 1:
        return ref[pl.ds(start, 128), :]
    return ref[pl.ds(start, 128, stride=d), :]


def _put_rows(ref, start, d, val):
    if d == 1:
        ref[pl.ds(start, 128), :] = val
    else:
        ref[pl.ds(start, 128, stride=d), :] = val


def _units(d):
    return [(r, b, r + 128 * d * b) for r in range(d) for b in range(SLAB // (128 * d))]


def _band(d, prev_valid):
    qi = _iota((128, 256), 0)
    kj = _iota((128, 256), 1)
    steps = qi + 128 - kj
    valid = (steps >= 0) & (steps <= 128) & (prev_valid | (kj >= 128))
    return valid, (steps * d).astype(F32)


def _attn_fwd(z_at, name):
    def body(sl_ref, q_ref, kc_ref, kp_ref, vc_ref, vp_ref, of_ref, ob_ref, lse_ref, m_s, l_s, a_s):
        n = pl.program_id(1)
        lo, hi = _pair_masks()
        slopes = (sl_ref[0, 0:1, 0:1], sl_ref[0, 1:2, 0:1])

        def unit(d, start, b, first):
            q = _rows(q_ref, start, d).astype(BF)
            if b > 0:
                kprev, vprev, prev_valid = _rows(kc_ref, start - 128 * d, d), _rows(vc_ref, start - 128 * d, d), True
            else:
                pstart = start + SLAB - 128 * d
                kprev, vprev, prev_valid = _rows(kp_ref, pstart, d), _rows(vp_ref, pstart, d), n > 0
            kcat = jnp.concatenate([kprev, _rows(kc_ref, start, d)], axis=0).astype(BF)
            vcat = jnp.concatenate([vprev, _rows(vc_ref, start, d)], axis=0).astype(BF)
            valid, dist = _band(d, prev_valid)
            ms, ls, pvs = [], [], []
            for e in range(2):
                qm = jnp.where(lo if e == 0 else hi, q, jnp.zeros_like(q))
                s = _dg(qm, kcat, NT) * (A_HD ** -0.5) - slopes[e] * dist
                s = jnp.where(valid, s, NEG)
                m = jnp.max(s, axis=1, keepdims=True)
                p = jnp.exp(s - m)
                ms.append(m)
                ls.append(jnp.sum(p, axis=1, keepdims=True))
                pvs.append(_dg(p.astype(BF), vcat, NN))
            m_u = jnp.where(lo, ms[0], ms[1])
            l_u = jnp.where(lo, ls[0], ls[1])
            a_u = jnp.where(lo, pvs[0], pvs[1])
            if first:
                m_n, l_n, a_n = m_u, l_u, a_u
            else:
                m_o = _rows(m_s, start, d)
                m_n = jnp.maximum(m_o, m_u)
                c_o = jnp.exp(m_o - m_n)
                c_u = jnp.exp(m_u - m_n)
                l_n = _rows(l_s, start, d) * c_o + l_u * c_u
                a_n = _rows(a_s, start, d) * c_o + a_u * c_u
            _put_rows(m_s, start, d, m_n)
            _put_rows(l_s, start, d, l_n)
            _put_rows(a_s, start, d, a_n)

        for pi, (_, d) in enumerate(PATTERNS):
            for r, b, start in _units(d):
                unit(d, start, b, pi == 0)
        l = l_s[...]
        out = a_s[...] / l
        of_ref[...] = out
        ob_ref[...] = out.astype(BF)
        lse_ref[...] = m_s[...] + jnp.log(l)

    cur = lambda c: pl.BlockSpec((SLAB, 128), lambda j, n: (n, c * N_PAIR + j))
    prv = lambda c: pl.BlockSpec((SLAB, 128), lambda j, n: (jnp.maximum(n - 1, 0), c * N_PAIR + j))
    out = pl.BlockSpec((SLAB, 128), lambda j, n: (n, j))
    return pl.pallas_call(
        body, grid=(N_PAIR, N_SLAB),
        in_specs=[pl.BlockSpec((1, 8, 128), lambda j, n: (j, 0, 0)), cur(0), cur(1), prv(1), cur(2), prv(2)],
        out_specs=[out, out, out],
        out_shape=[jax.ShapeDtypeStruct((T, AW), F32), jax.ShapeDtypeStruct((T, AW), BF),
                   jax.ShapeDtypeStruct((T, AW), F32)],
        scratch_shapes=[pltpu.VMEM((SLAB, 128), F32)] * 3,
        name=name, compiler_params=_cp())(_slope_table(), z_at, z_at, z_at, z_at, z_at)


def _attn_bwd(z_at, dout, out, lse, name):
    def body(sl_ref, q_ref, kc_ref, kp_ref, vc_ref, vp_ref, do_ref, o_ref, lse_ref, dq_ref, dk_ref, dv_ref,
             dq_s, dk_s, dv_s, ck_s, cv_s):
        step = pl.program_id(1)
        n = N_SLAB - 1 - step
        lo, hi = _pair_masks()
        slopes = (sl_ref[0, 0:1, 0:1], sl_ref[0, 1:2, 0:1])

        @pl.when(step == 0)
        def _():
            ck_s[...] = jnp.zeros_like(ck_s)
            cv_s[...] = jnp.zeros_like(cv_s)

        dk_s[...] = ck_s[...]
        dv_s[...] = cv_s[...]
        ck_s[...] = jnp.zeros_like(ck_s)
        cv_s[...] = jnp.zeros_like(cv_s)

        def add_rows(ref, start, d, val):
            _put_rows(ref, start, d, _rows(ref, start, d) + val)

        def unit(d, start, b, first):
            q = _rows(q_ref, start, d).astype(BF)
            do_f = _rows(do_ref, start, d)
            do = do_f.astype(BF)
            prod = do_f * _rows(o_ref, start, d)
            lse_u = _rows(lse_ref, start, d)
            if b > 0:
                kprev, vprev, prev_valid = _rows(kc_ref, start - 128 * d, d), _rows(vc_ref, start - 128 * d, d), True
            else:
                pstart = start + SLAB - 128 * d
                kprev, vprev, prev_valid = _rows(kp_ref, pstart, d), _rows(vp_ref, pstart, d), n > 0
            kcat = jnp.concatenate([kprev, _rows(kc_ref, start, d)], axis=0).astype(BF)
            vcat = jnp.concatenate([vprev, _rows(vc_ref, start, d)], axis=0).astype(BF)
            valid, dist = _band(d, prev_valid)
            dqs = []
            dkc = jnp.zeros((256, 128), F32)
            dvc = jnp.zeros((256, 128), F32)
            for e in range(2):
                msk = lo if e == 0 else hi
                qm = jnp.where(msk, q, jnp.zeros_like(q))
                dom = jnp.where(msk, do, jnp.zeros_like(do))
                delta = jnp.sum(jnp.where(msk, prod, 0.0), axis=1, keepdims=True)
                lse_e = lse_u[:, 64 * e:64 * e + 1]
                s = _dg(qm, kcat, NT) * (A_HD ** -0.5) - slopes[e] * dist
                p = jnp.where(valid, jnp.exp(jnp.where(valid, s, NEG) - lse_e), 0.0)
                dp = _dg(dom, vcat, NT)
                ds = (p * (dp - delta)).astype(BF)
                dqs.append(_dg(ds, kcat, NN))
                dkc = dkc + _dg(ds, qm, TN)
                dvc = dvc + _dg(p.astype(BF), dom, TN)
            dq_u = jnp.where(lo, dqs[0], dqs[1]) * (A_HD ** -0.5)
            dkc = dkc * (A_HD ** -0.5)
            if first:
                _put_rows(dq_s, start, d, dq_u)
            else:
                add_rows(dq_s, start, d, dq_u)
            add_rows(dk_s, start, d, dkc[128:])
            add_rows(dv_s, start, d, dvc[128:])
            if b > 0:
                add_rows(dk_s, start - 128 * d, d, dkc[:128])
                add_rows(dv_s, start - 128 * d, d, dvc[:128])
            else:
                pstart = start + SLAB - 128 * d
                add_rows(ck_s, pstart, d, dkc[:128])
                add_rows(cv_s, pstart, d, dvc[:128])

        for pi, (_, d) in enumerate(PATTERNS):
            for r, b, start in _units(d):
                unit(d, start, b, pi == 0)
        dq_ref[...] = dq_s[...].astype(BF)
        dk_ref[...] = dk_s[...].astype(BF)
        dv_ref[...] = dv_s[...].astype(BF)

    rev = lambda n: N_SLAB - 1 - n
    cur = lambda c: pl.BlockSpec((SLAB, 128), lambda j, n: (rev(n), c * N_PAIR + j))
    prv = lambda c: pl.BlockSpec((SLAB, 128), lambda j, n: (jnp.maximum(rev(n) - 1, 0), c * N_PAIR + j))
    one = pl.BlockSpec((SLAB, 128), lambda j, n: (rev(n), j))
    return pl.pallas_call(
        body, grid=(N_PAIR, N_SLAB),
        in_specs=[pl.BlockSpec((1, 8, 128), lambda j, n: (j, 0, 0)), cur(0), cur(1), prv(1), cur(2), prv(2),
                  one, one, one],
        out_specs=[one, one, one], out_shape=[jax.ShapeDtypeStruct((T, AW), BF)] * 3,
        scratch_shapes=[pltpu.VMEM((SLAB, 128), F32)] * 5,
        name=name, compiler_params=_cp())(_slope_table(), z_at, z_at, z_at, z_at, z_at, dout, out, lse)


def _silu(x):
    return x * jax.nn.sigmoid(x)


def _qk_math(c):
    s = _silu(c)
    return s * lax.rsqrt(jnp.sum(s * s, axis=-1, keepdims=True) + L2_EPS)


def _softplus(x):
    return jnp.maximum(x, 0.0) + jnp.log(1.0 + jnp.exp(-jnp.abs(x)))


def _gate_math(bd, alog_row, dtb_row):
    rows = bd.shape[0]
    lane = _iota(bd.shape, 1)
    beta = jax.nn.sigmoid(bd)
    g = jnp.where((lane >= DN_H) & (lane < 2 * DN_H), -jnp.exp(alog_row) * _softplus(bd + dtb_row), 0.0)
    ri = _iota((rows, rows), 0)
    ci = _iota((rows, rows), 1)
    same = (ri // CH) == (ci // CH)
    li = _iota((128, 128), 0)
    lj = _iota((128, 128), 1)
    to_next_group = jnp.where((lj == li + DN_H) & (li >= DN_H) & (li < 2 * DN_H), 1.0, 0.0)
    gc = _hdot(jnp.where(same & (ci <= ri), 1.0, 0.0), g)
    glast = _hdot(_hdot(jnp.where(same, 1.0, 0.0), g), to_next_group)
    return jnp.where(lane < DN_H, beta, 0.0) + gc + glast


def _shift_down(cur, halo, s):
    if s == 0:
        return cur
    rolled = pltpu.roll(cur, s, 0)
    hr = pltpu.roll(halo, s, 0)
    head = jnp.where(_iota(hr.shape, 0) < s, hr, rolled[:8])
    return jnp.concatenate([head, rolled[8:]], axis=0)


def _shift_up(cur, halo, s):
    if s == 0:
        return cur
    rows = cur.shape[0]
    rolled = pltpu.roll(cur, rows - s, 0)
    hr = pltpu.roll(halo, 8 - s, 0)
    tail = jnp.where(_iota(hr.shape, 0) >= 8 - s, hr, rolled[rows - 8:])
    return jnp.concatenate([rolled[:rows - 8], tail], axis=0)


def _dn_pre_fwd(z_dn, conv_w8, alog_row, dtb_row, name):
    tm = 256
    wq = 3 * DNW

    def body(raw_ref, halo_ref, bd_ref, w_ref, al_ref, dt_ref, conv_ref, qkv_ref, bg_ref):
        i = pl.program_id(0)
        cur = raw_ref[...]
        halo = jnp.where(i > 0, halo_ref[...], 0.0)
        w = w_ref[...]
        conv = jnp.zeros((tm, wq), F32)
        for j in range(4):
            conv = conv + _shift_down(cur, halo, 3 - j) * w[j:j + 1, :]
        conv_ref[...] = conv
        for blk in range(3 * DN_H):
            sl = slice(128 * blk, 128 * blk + 128)
            c = conv[:, sl]
            qkv_ref[:, sl] = _qk_math(c) if blk < 2 * DN_H else _silu(c)
        bg_ref[...] = _gate_math(bd_ref[...], al_ref[...], dt_ref[...])

    one = pl.BlockSpec((1, 128), lambda i: (0, 0))
    return pl.pallas_call(
        body, grid=(T // tm,),
        in_specs=[pl.BlockSpec((tm, wq), lambda i: (i, 0)),
                  pl.BlockSpec((8, wq), lambda i: (jnp.maximum(i * (tm // 8) - 1, 0), 0)),
                  pl.BlockSpec((tm, 128), lambda i: (i, BD_BLK)),
                  pl.BlockSpec((8, wq), lambda i: (0, 0)), one, one],
        out_specs=[pl.BlockSpec((tm, wq), lambda i: (i, 0)), pl.BlockSpec((tm, wq), lambda i: (i, 0)),
                   pl.BlockSpec((tm, 128), lambda i: (i, 0))],
        out_shape=[jax.ShapeDtypeStruct((T, wq), F32), jax.ShapeDtypeStruct((T, wq), F32),
                   jax.ShapeDtypeStruct((T, 128), F32)],
        name=name, compiler_params=_cp())(z_dn, z_dn, z_dn, conv_w8, alog_row, dtb_row)


def _dn_pre_bwd(conv, z_dn, alog_row, dtb_row, dqn, dkn, dvn, dbg, name):
    tm = 256
    wq = 3 * DNW

    def body(conv_ref, bd_ref, al_ref, dt_ref, dq_ref, dk_ref, dv_ref, dbg_ref,
             dconv_ref, dbd_ref, dal_ref, ddt_ref):
        i = pl.program_id(0)
        for blk in range(3 * DN_H):
            sl = slice(128 * blk, 128 * blk + 128)
            src = (dq_ref, dk_ref, dv_ref)[blk // DN_H]
            ct = src[:, 128 * (blk % DN_H):128 * (blk % DN_H) + 128]
            fn = _qk_math if blk < 2 * DN_H else _silu
            _, vjp = jax.vjp(fn, conv_ref[:, sl])
            dconv_ref[:, sl] = vjp(ct)[0]
        _, vjp = jax.vjp(_gate_math, bd_ref[...], al_ref[...], dt_ref[...])
        dbd, dal, ddt = vjp(dbg_ref[...])
        dbd_ref[...] = dbd

        @pl.when(i == 0)
        def _():
            dal_ref[...] = dal
            ddt_ref[...] = ddt

        @pl.when(i > 0)
        def _():
            dal_ref[...] += dal
            ddt_ref[...] += ddt

    one = pl.BlockSpec((1, 128), lambda i: (0, 0))
    row = pl.BlockSpec((tm, wq), lambda i: (i, 0))
    hd = pl.BlockSpec((tm, DNW), lambda i: (i, 0))
    st = pl.BlockSpec((tm, 128), lambda i: (i, 0))
    return pl.pallas_call(
        body, grid=(T // tm,),
        in_specs=[row, pl.BlockSpec((tm, 128), lambda i: (i, BD_BLK)), one, one, hd, hd, hd, st],
        out_specs=[row, st, one, one],
        out_shape=[jax.ShapeDtypeStruct((T, wq), F32), jax.ShapeDtypeStruct((T, 128), F32),
                   jax.ShapeDtypeStruct((1, 128), F32), jax.ShapeDtypeStruct((1, 128), F32)],
        name=name, compiler_params=_cp())(conv, z_dn, alog_row, dtb_row, dqn, dkn, dvn, dbg)


def _dn_conv_bwd(dconv, z_dn, conv_w8, name):
    tm = 256
    wq = 3 * DNW
    last = T // tm - 1

    def body(dc_ref, dcn_ref, raw_ref, halo_ref, w_ref, draw_ref, dw_ref):
        i = pl.program_id(0)
        dc = dc_ref[...]
        nxt = jnp.where(i < last, dcn_ref[...], 0.0)
        cur = raw_ref[...]
        halo = jnp.where(i > 0, halo_ref[...], 0.0)
        w = w_ref[...]
        draw = jnp.zeros((tm, wq), F32)
        rows = []
        for j in range(4):
            draw = draw + _shift_up(dc, nxt, 3 - j) * w[j:j + 1, :]
            rows.append(jnp.sum(dc * _shift_down(cur, halo, 3 - j), axis=0, keepdims=True))
        draw_ref[...] = draw
        part = jnp.concatenate(rows + [jnp.zeros((4, wq), F32)], axis=0)

        @pl.when(i == 0)
        def _():
            dw_ref[...] = part

        @pl.when(i > 0)
        def _():
            dw_ref[...] += part

    row = pl.BlockSpec((tm, wq), lambda i: (i, 0))
    return pl.pallas_call(
        body, grid=(T // tm,),
        in_specs=[row, pl.BlockSpec((8, wq), lambda i: (jnp.minimum((i + 1) * (tm // 8), T // 8 - 1), 0)),
                  row, pl.BlockSpec((8, wq), lambda i: (jnp.maximum(i * (tm // 8) - 1, 0), 0)),
                  pl.BlockSpec((8, wq), lambda i: (0, 0))],
        out_specs=[row, pl.BlockSpec((8, wq), lambda i: (0, 0))],
        out_shape=[jax.ShapeDtypeStruct((T, wq), F32), jax.ShapeDtypeStruct((8, wq), F32)],
        name=name, compiler_params=_cp())(dconv, dconv, z_dn, z_dn, conv_w8)


def _h3(a, b, dims=NN):
    return lax.dot_general(a, b, dims, precision=lax.Precision.HIGH, preferred_element_type=F32)


@jax.custom_vjp
def _inverse_given(a_mat, tinv):
    return tinv


def _inverse_given_fwd(a_mat, tinv):
    return tinv, tinv


def _inverse_given_bwd(tinv, g):
    return -_h3(tinv, _h3(g, tinv, NT), TN), jnp.zeros_like(tinv)


_inverse_given.defvjp(_inverse_given_fwd, _inverse_given_bwd)


def _prep_math(q, k, v, bgc, h, tinv_saved=None):
    beta = _col(bgc, h)
    gc = jnp.broadcast_to(_col(bgc, DN_H + h), (PAIR, 128))
    glast = jnp.broadcast_to(_col(bgc, 2 * DN_H + h), (PAIR, 128))
    ri = _iota((PAIR, PAIR), 0)
    ci = _iota((PAIR, PAIR), 1)
    same = (ri // CH) == (ci // CH)
    causal = same & (ci <= ri)
    strict = same & (ci < ri)
    eye = ri == ci
    gc_cols = _hdot(jnp.ones((PAIR, PAIR), F32), jnp.where(eye, gc, 0.0))
    decay = jnp.exp(jnp.where(causal, gc - gc_cols, NEG))
    egc = jnp.exp(gc)
    kb = k * beta
    a_mat = jnp.where(strict, _bdot_nt(kb, k) * decay, 0.0)
    if tinv_saved is None:
        p = -a_mat
        tinv = jnp.where(eye, 1.0, 0.0) + p
        for _ in range(5):
            p = _h3(p, p)
            tinv = tinv + _h3(tinv, p)
    else:
        tinv = _inverse_given(a_mat, tinv_saved)
    u = _h3(tinv, v * beta)
    w = _h3(tinv, kb * egc)
    qs = q * (DN_HD ** -0.5)
    attn = jnp.where(causal, _bdot_nt(qs, k) * decay, 0.0)
    return u, w, qs * egc, k * jnp.exp(glast - gc), attn, jnp.exp(glast), tinv


def _dn_prep_fwd(qkv, bg, name):
    rows = 512
    hd = lambda off: pl.BlockSpec((rows, 128), lambda g, h: (g, off + h))
    out = pl.BlockSpec((rows, 128), lambda g, h: (g, h))

    def body(q_ref, k_ref, v_ref, bg_ref, *outs):
        h = pl.program_id(1)
        for pr in range(rows // PAIR):
            rs = slice(PAIR * pr, PAIR * pr + PAIR)
            res = _prep_math(q_ref[rs, :], k_ref[rs, :], v_ref[rs, :], bg_ref[rs, :], h)
            for o_ref, val in zip(outs, res):
                o_ref[rs, :] = val

    return pl.pallas_call(
        body, grid=(T // rows, DN_H),
        in_specs=[hd(0), hd(DN_H), hd(2 * DN_H), pl.BlockSpec((rows, 128), lambda g, h: (g, 0))],
        out_specs=[out] * 7, out_shape=[jax.ShapeDtypeStruct((T, DNW), F32)] * 7,
        name=name, compiler_params=_cp())(qkv, qkv, qkv, bg)


def _dn_prep_bwd(qkv, bg, tinv, cts, name):
    rows = 512
    hd = lambda off: pl.BlockSpec((rows, 128), lambda g, h: (g, off + h))
    out = pl.BlockSpec((rows, 128), lambda g, h: (g, h))
    st = pl.BlockSpec((rows, 128), lambda g, h: (g, 0))

    def body(q_ref, k_ref, v_ref, bg_ref, ti_ref, c0, c1, c2, c3, c4, c5, dq_ref, dk_ref, dv_ref, dbg_ref):
        h = pl.program_id(1)
        parts = []
        for pr in range(rows // PAIR):
            rs = slice(PAIR * pr, PAIR * pr + PAIR)
            fn = lambda q, k, v, b, ti=ti_ref[rs, :]: _prep_math(q, k, v, b, h, ti)[:6]
            _, vjp = jax.vjp(fn, q_ref[rs, :], k_ref[rs, :], v_ref[rs, :], bg_ref[rs, :])
            dq, dk, dv, dbg = vjp(tuple(c[rs, :] for c in (c0, c1, c2, c3, c4, c5)))
            dq_ref[rs, :] = dq
            dk_ref[rs, :] = dk
            dv_ref[rs, :] = dv
            parts.append(dbg)
        dbg_all = jnp.concatenate(parts, axis=0)

        @pl.when(h == 0)
        def _():
            dbg_ref[...] = dbg_all

        @pl.when(h > 0)
        def _():
            dbg_ref[...] += dbg_all

    return pl.pallas_call(
        body, grid=(T // rows, DN_H),
        in_specs=[hd(0), hd(DN_H), hd(2 * DN_H), st] + [out] * 7,
        out_specs=[out, out, out, st],
        out_shape=[jax.ShapeDtypeStruct((T, DNW), F32)] * 3 + [jax.ShapeDtypeStruct((T, 128), F32)],
        name=name, compiler_params=_cp())(qkv, qkv, qkv, bg, tinv, *cts)


def _step_math(s, u, w, qg, kdec, attn, decb, sub):
    vnew = u - _bdot_nn(w, s)
    z = jnp.zeros((CH, 128), F32)
    vfull = jnp.concatenate([vnew, z] if sub == 0 else [z, vnew], axis=0)
    o = _bdot_nn(qg, s) + _bdot_nn(attn, vfull)
    dec = jnp.sum(decb, axis=0, keepdims=True) * (1.0 / CH)
    return s * dec + _bdot_tn(kdec, vnew), o


def _dn_scan_fwd(prep, name):
    npair = T // PAIR
    row = pl.BlockSpec((PAIR, DNW), lambda p: (p, 0))

    def body(u_ref, w_ref, qg_ref, kd_ref, at_ref, db_ref, o_ref, ss_ref, s_ref):
        @pl.when(pl.program_id(0) == 0)
        def _():
            s_ref[...] = jnp.zeros_like(s_ref)

        for h in range(DN_H):
            ls = slice(128 * h, 128 * h + 128)
            s = s_ref[h]
            for sub in range(2):
                rs = slice(CH * sub, CH * sub + CH)
                ss_ref[sub, h] = s
                s, o = _step_math(s, u_ref[rs, ls], w_ref[rs, ls], qg_ref[rs, ls], kd_ref[rs, ls],
                                  at_ref[rs, ls], db_ref[rs, ls], sub)
                o_ref[rs, ls] = o
            s_ref[h] = s

    return pl.pallas_call(
        body, grid=(npair,), in_specs=[row] * 6,
        out_specs=[row, pl.BlockSpec((2, DN_H, 128, 128), lambda p: (p, 0, 0, 0))],
        out_shape=[jax.ShapeDtypeStruct((T, DNW), F32), jax.ShapeDtypeStruct((T // CH, DN_H, 128, 128), F32)],
        scratch_shapes=[pltpu.VMEM((DN_H, 128, 128), F32)],
        name=name, compiler_params=_cp())(*prep)


def _dn_scan_bwd(prep, states, do, name):
    npair = T // PAIR
    row = pl.BlockSpec((PAIR, DNW), lambda p: (npair - 1 - p, 0))

    def body(u_ref, w_ref, qg_ref, kd_ref, at_ref, db_ref, ss_ref, do_ref, *rest):
        outs, ds_ref = rest[:6], rest[6]

        @pl.when(pl.program_id(0) == 0)
        def _():
            ds_ref[...] = jnp.zeros_like(ds_ref)

        for h in range(DN_H):
            ls = slice(128 * h, 128 * h + 128)
            ds = ds_ref[h]
            for sub in (1, 0):
                rs = slice(CH * sub, CH * sub + CH)
                args = (ss_ref[sub, h],) + tuple(r[rs, ls] for r in (u_ref, w_ref, qg_ref, kd_ref, at_ref, db_ref))
                _, vjp = jax.vjp(functools.partial(_step_math, sub=sub), *args)
                cts = vjp((ds, do_ref[rs, ls]))
                ds = cts[0]
                for o_ref, val in zip(outs, cts[1:]):
                    o_ref[rs, ls] = val
            ds_ref[h] = ds

    return pl.pallas_call(
        body, grid=(npair,),
        in_specs=[row] * 6 + [pl.BlockSpec((2, DN_H, 128, 128), lambda p: (npair - 1 - p, 0, 0, 0)), row],
        out_specs=[row] * 6, out_shape=[jax.ShapeDtypeStruct((T, DNW), F32)] * 6,
        scratch_shapes=[pltpu.VMEM((DN_H, 128, 128), F32)],
        name=name, compiler_params=_cp())(*prep, states, do)


def _post_math(o, gate, wrow):
    return o * lax.rsqrt(jnp.mean(o * o, axis=-1, keepdims=True) + NORM_EPS) * wrow * _silu(gate)


def _dn_post_fwd(o, z_dn, dn_norm, name):
    tm = 512
    row = pl.BlockSpec((tm, DNW), lambda i: (i, 0))

    def body(o_ref, g_ref, w_ref, y_ref):
        for h in range(DN_H):
            ls = slice(128 * h, 128 * h + 128)
            y_ref[:, ls] = _post_math(o_ref[:, ls], g_ref[:, ls], w_ref[...]).astype(BF)

    return pl.pallas_call(
        body, grid=(T // tm,),
        in_specs=[row, pl.BlockSpec((tm, DNW), lambda i: (i, 3)), pl.BlockSpec((1, 128), lambda i: (0, 0))],
        out_specs=row, out_shape=jax.ShapeDtypeStruct((T, DNW), BF),
        name=name, compiler_params=_cp())(o, z_dn, dn_norm)


def _dn_post_bwd(o, z_dn, dn_norm, dy, name):
    tm = 512
    row = pl.BlockSpec((tm, DNW), lambda i: (i, 0))
    one = pl.BlockSpec((1, 128), lambda i: (0, 0))

    def body(o_ref, g_ref, w_ref, dy_ref, do_ref, dg_ref, dw_ref):
        i = pl.program_id(0)
        dw = jnp.zeros((1, 128), F32)
        for h in range(DN_H):
            ls = slice(128 * h, 128 * h + 128)
            _, vjp = jax.vjp(_post_math, o_ref[:, ls], g_ref[:, ls], w_ref[...])
            do, dg, dwh = vjp(dy_ref[:, ls].astype(F32))
            do_ref[:, ls] = do
            dg_ref[:, ls] = dg
            dw = dw + dwh

        @pl.when(i == 0)
        def _():
            dw_ref[...] = dw

        @pl.when(i > 0)
        def _():
            dw_ref[...] += dw

    return pl.pallas_call(
        body, grid=(T // tm,),
        in_specs=[row, pl.BlockSpec((tm, DNW), lambda i: (i, 3)), one, pl.BlockSpec((tm, DNW), lambda i: (i, 1))],
        out_specs=[row, row, one],
        out_shape=[jax.ShapeDtypeStruct((T, DNW), F32), jax.ShapeDtypeStruct((T, DNW), F32),
                   jax.ShapeDtypeStruct((1, 128), F32)],
        name=name, compiler_params=_cp())(o, z_dn, dn_norm, dy)


def _dz_dn_assemble(draw, dgate, dbd, name):
    tm = 512

    def body(a_ref, b_ref, c_ref, o_ref):
        o_ref[:, :3 * DNW] = a_ref[...].astype(BF)
        o_ref[:, 3 * DNW:4 * DNW] = b_ref[...].astype(BF)
        o_ref[:, 4 * DNW:4 * DNW + 128] = c_ref[...].astype(BF)
        o_ref[:, 4 * DNW + 128:] = jnp.zeros((tm, 128), BF)

    return pl.pallas_call(
        body, grid=(T // tm,),
        in_specs=[pl.BlockSpec((tm, 3 * DNW), lambda i: (i, 0)), pl.BlockSpec((tm, DNW), lambda i: (i, 0)),
                  pl.BlockSpec((tm, 128), lambda i: (i, 0))],
        out_specs=pl.BlockSpec((tm, ZD), lambda i: (i, 0)),
        out_shape=jax.ShapeDtypeStruct((T, ZD), BF), name=name, compiler_params=_cp())(draw, dgate, dbd)


HBM = pl.BlockSpec(memory_space=pltpu.HBM)
SEM = pl.BlockSpec(memory_space=pltpu.SEMAPHORE)
EFFECT = pltpu.SideEffectType.DATAFLOW_SIDE_EFFECTING
N_PEER = N_DEV - 1


def _peers(x, y, c):
    return [(k, (x ^ (k >> 2), y ^ ((k >> 1) & 1), c ^ (k & 1))) for k in (1, 2, 4, 3, 5, 6, 7)]


def _exchange_copy(ins, lands, ssems, rsems, scatter, t, k, pos, me):
    px, py, pc = pos
    src = ins[t].at[4 * px + 2 * py + pc] if scatter else ins[t]
    return pltpu.make_async_remote_copy(
        src_ref=src, dst_ref=lands[t].at[me], send_sem=ssems[t].at[k - 1], recv_sem=rsems[t].at[k - 1],
        device_id=pos, device_id_type=MESH_ID)


def _xstart(bufs, scatter, name):
    nt = len(bufs)
    lands = [lax.empty((N_DEV,) + tuple(b.shape[1:] if scatter else b.shape), b.dtype) for b in bufs]

    def body(*refs):
        ins, lnd = refs[:nt], refs[nt:2 * nt]
        ssems, rsems = refs[2 * nt:3 * nt], refs[3 * nt:4 * nt]
        token = refs[-1]
        x, y, c = lax.axis_index("x"), lax.axis_index("y"), lax.axis_index("c")
        me = 4 * x + 2 * y + c
        for t in range(nt):
            for k, pos in _peers(x, y, c):
                _exchange_copy(ins, lnd, ssems, rsems, scatter, t, k, pos, me).start()
        token[...] = jnp.zeros_like(token)

    both = list(bufs) + lands
    res = pl.pallas_call(
        body, name=name,
        out_shape=[pltpu.SemaphoreType.DMA((N_PEER,))] * (2 * nt)
        + [pltpu.HBM(b.shape, b.dtype) for b in both] + [jax.ShapeDtypeStruct((8, 128), F32)],
        in_specs=[HBM] * (2 * nt),
        out_specs=[SEM] * (2 * nt) + [HBM] * (2 * nt) + [pl.BlockSpec(memory_space=pltpu.VMEM)],
        input_output_aliases={i: 2 * nt + i for i in range(2 * nt)},
        compiler_params=pltpu.CompilerParams(has_side_effects=EFFECT),
    )(*[pltpu.with_memory_space_constraint(b, pltpu.HBM) for b in both])
    return res[:nt], res[nt:2 * nt], res[2 * nt:3 * nt], res[3 * nt:4 * nt], res[-1][0, 0]


def _xwait(ssems, rsems, thrus, lands, scatter, after, name):
    nt = len(lands)

    def body(*refs):
        ins, lnd = refs[:nt], refs[nt:2 * nt]
        ss, rs = refs[2 * nt:3 * nt], refs[3 * nt:4 * nt]
        x, y, c = lax.axis_index("x"), lax.axis_index("y"), lax.axis_index("c")
        me = 4 * x + 2 * y + c
        for t in range(nt):
            for k, pos in _peers(x, y, c):
                cp = _exchange_copy(ins, lnd, ss, rs, scatter, t, k, pos, me)
                cp.wait_send()
                cp.wait_recv()

    both = list(thrus) + list(lands)
    res = pl.pallas_call(
        body, name=name, out_shape=[pltpu.HBM(b.shape, b.dtype) for b in both],
        in_specs=[HBM] * (2 * nt) + [SEM] * (2 * nt) + [ANY], out_specs=[HBM] * (2 * nt),
        input_output_aliases={i: i for i in range(2 * nt)},
        compiler_params=pltpu.CompilerParams(has_side_effects=EFFECT),
    )(*both, *ssems, *rsems, after)
    return res[:nt], res[nt:]


def _adam(recv, w, m, v, tr, name):
    _, r, c = w.shape
    c1 = np.float32(1.0 - ADAM_B1 ** ADAM_STEP)
    c2 = np.float32(1.0 - ADAM_B2 ** ADAM_STEP)

    def body(r_ref, w_ref, m_ref, v_ref, g_ref, d_ref, mo_ref, vo_ref):
        g = r_ref[0].astype(F32)
        for s in range(1, N_DEV):
            g = g + r_ref[s].astype(F32)
        mn = ADAM_B1 * m_ref[0] + (1.0 - ADAM_B1) * g
        vn = ADAM_B2 * v_ref[0] + (1.0 - ADAM_B2) * (g * g)
        g_ref[0] = g
        mo_ref[0] = mn
        vo_ref[0] = vn
        d_ref[0] = -ADAM_LR * ((mn / c1) / (jnp.sqrt(vn / c2) + ADAM_EPS) + ADAM_WD * w_ref[0])

    one = pl.BlockSpec((1, tr, c), lambda i: (0, i, 0))
    return pl.pallas_call(
        body, grid=(r // tr,), in_specs=[pl.BlockSpec((N_DEV, tr, c), lambda i: (0, i, 0)), one, one, one],
        out_specs=[one] * 4, out_shape=[jax.ShapeDtypeStruct((1, r, c), F32)] * 4,
        name=name, compiler_params=_cp())(recv, w, m, v)


def _local_step(x, target, sp, need, emit):
    g = {}
    x0, h1, gu1, act1, y1, wgu1, wd1 = _ffn_fwd(
        x, None, 0.0, sp["norm_ffn1"], lambda a: need("wgu1", a), lambda a: need("wd1", a), "ffn1")
    x1, h2 = _rms_fwd(x0, y1, 0.5, sp["norm_mix"], "mix_norm")
    win_a, win_d = need("win_a", h2), need("win_d", h2)
    conv_w8, wout = need("conv_w8", h2), need("wout", h2)
    z_at = _mm_nn(h2, win_a, F32, 1024, 768, "mix_in_attn")
    z_dn = _mm_nn(h2, win_d, F32, 1024, 768, "mix_in_dn")
    attn_f, attn_b, lse = _attn_fwd(z_at, "attn_fwd")

    conv, qkvn, bg = _dn_pre_fwd(z_dn, conv_w8, sp["alog_row"], sp["dtb_row"], "dn_pre")
    *prep, tinv = _dn_prep_fwd(qkvn, bg, "dn_prep")
    o_dn, states = _dn_scan_fwd(prep, "dn_scan")
    dn_b = _dn_post_fwd(o_dn, z_dn, sp["dn_norm"], "dn_post")

    mix = jnp.concatenate([attn_b, dn_b], axis=1)
    y2 = _mm_nn(mix, wout, F32, 1024, 1024, "mix_out")
    x2, h3, gu2, act2, y3, wgu2, wd2 = _ffn_fwd(
        x1, y2, 1.0, sp["norm_ffn2"], lambda a: need("wgu2", a), lambda a: need("wd2", a), "ffn2")
    x3, _ = _rms_fwd(x2, y3, 0.5, sp["norm_final"], "final_add")

    dx3, dys3, g["norm_final"], loss8 = _loss_bwd(x3, sp["norm_final"], target, "loss")
    dx2, dx2b, g["norm_ffn2"] = _ffn_bwd(
        x2, sp["norm_ffn2"], h3, gu2, act2, wgu2, wd2, dx3, dys3, 1.0,
        lambda kind, dw: emit(kind + "2", {"w" + kind + "2": dw}), "ffn2b")

    zero = emit("wout", {"wout": _mm_tn(mix, dx2b, 512, 1024, "mix_out_dw")})
    dmix = _mm_nt(dx2b, wout, F32, 1024, 1024, "mix_out_dx")

    dz_at = jnp.concatenate(_attn_bwd(z_at, dmix, attn_f, lse, "attn_bwd"), axis=1)

    do_dn, dgate, g["dn_norm"] = _dn_post_bwd(o_dn, z_dn, sp["dn_norm"] + zero, dmix, "dn_post_b")
    cts = _dn_scan_bwd(prep, states, do_dn, "dn_scan_b")
    dqn, dkn, dvn, dbg = _dn_prep_bwd(qkvn, bg, tinv, cts, "dn_prep_b")
    dconv, dbd, g["alog_row"], g["dtb_row"] = _dn_pre_bwd(
        conv, z_dn, sp["alog_row"], sp["dtb_row"], dqn, dkn, dvn, dbg, "dn_pre_b")
    draw, dconv_w8 = _dn_conv_bwd(dconv, z_dn, conv_w8, "dn_conv_b")
    dz_dn = _dz_dn_assemble(draw, dgate, dbd, "dn_dz")

    zero = emit("win", {"win_a": _mm_tn(h2, dz_at, 512, 768, "mix_in_dw_a"),
                        "win_d": _mm_tn(h2, dz_dn, 512, 768, "mix_in_dw_d"), "conv_w8": dconv_w8})
    dh2 = _mm_nt(dz_at, win_a, F32, 1024, 1024, "mix_in_dx_a")
    dh2d = _mm_nt(dz_dn, win_d, F32, 1024, 1024, "mix_in_dx_d")
    (dx1, dys1), _, g["norm_mix"] = _rms_bwd2(x1, sp["norm_mix"] + zero, dh2, dh2d, dx2, "mix_dnorm")
    dx0, _, g["norm_ffn1"] = _ffn_bwd(
        x0, sp["norm_ffn1"], h1, gu1, act1, wgu1, wd1, dx1, dys1, 1.0,
        lambda kind, dw: emit(kind + "1", {"w" + kind + "1": dw}), "ffn1b")
    return loss8[0, 0], dx0, g


def _rms_bwd2(x, gain, dh_a, dh_b, dres, name):
    tm = 512
    row = pl.BlockSpec((tm, D), lambda i: (i, 0))
    gspec = pl.BlockSpec((1, D), lambda i: (0, 0))

    def body(x_ref, g_ref, da_ref, db_ref, dres_ref, dx_ref, dxs_ref, dg_ref):
        i = pl.program_id(0)
        xv = x_ref[...]
        r = lax.rsqrt(jnp.mean(xv * xv, axis=-1, keepdims=True) + NORM_EPS)
        xh = xv * r
        dhv = da_ref[...] + db_ref[...]
        part = jnp.sum(dhv * xh, axis=0, keepdims=True)

        @pl.when(i == 0)
        def _():
            dg_ref[...] = part

        @pl.when(i > 0)
        def _():
            dg_ref[...] += part

        dxh = dhv * g_ref[...]
        dx = r * (dxh - xh * jnp.mean(dxh * xh, axis=-1, keepdims=True)) + dres_ref[...]
        dx_ref[...] = dx
        dxs_ref[...] = (0.5 * dx).astype(BF)

    dx, dxs, dg = pl.pallas_call(
        body, grid=(T // tm,), in_specs=[row, gspec, row, row, row], out_specs=[row, row, gspec],
        out_shape=[jax.ShapeDtypeStruct((T, D), F32), jax.ShapeDtypeStruct((T, D), BF),
                   jax.ShapeDtypeStruct((1, D), F32)],
        name=name, compiler_params=_cp())(x, gain, dh_a, dh_b, dres)
    return (dx, dxs), None, dg


def _cols_from_shards(gathered):
    n, r, c = gathered.shape
    return jnp.transpose(gathered, (1, 0, 2)).reshape(r, n * c)


def _shards_from_cols(full, dtype):
    r, nc = full.shape
    return jnp.transpose(full.reshape(r, N_DEV, nc // N_DEV), (1, 0, 2)).astype(dtype)


def _lane_row(vec4):
    return jnp.zeros((1, 128), F32).at[:, DN_H:2 * DN_H].set(vec4.astype(F32))


WEIGHT_SOURCES = {"wgu1": ("gate1", "up1"), "wd1": ("down1",), "win_a": ("w_in",), "win_d": ("w_in",),
                  "conv_w8": ("conv_w",), "wout": ("w_out",), "wgu2": ("gate2", "up2"), "wd2": ("down2",)}


def _build_weights(name, gath):
    if name in ("wgu1", "wgu2"):
        i = name[-1]
        return {name: jnp.concatenate(
            [_cols_from_shards(gath["gate" + i]), _cols_from_shards(gath["up" + i])], axis=1)}
    if name in ("wd1", "wd2"):
        return {name: gath["down" + name[-1]].reshape(F, D)}
    if name in ("win_a", "win_d"):
        w_in = _cols_from_shards(gath["w_in"])
        c0 = 3 * AW + 3 * DNW
        wp = jnp.concatenate([w_in[:, :c0], w_in[:, c0 + 2 * DN_H:], w_in[:, c0:c0 + 2 * DN_H],
                              jnp.zeros((D, ZP - IN_COLS), w_in.dtype)], axis=1)
        return {"win_a": wp[:, :ZA], "win_d": wp[:, ZA:]}
    if name == "wout":
        return {name: gath["w_out"].reshape(D, D)}
    conv = _cols_from_shards(gath["conv_w"])
    return {"conv_w8": jnp.concatenate([conv, jnp.zeros((4, 3 * DNW), F32)], axis=0)}


def _small_params(norm_ffn1, norm_mix, norm_ffn2, norm_final, a_log, dt_bias, dn_norm):
    return {"norm_ffn1": norm_ffn1, "norm_mix": norm_mix, "norm_ffn2": norm_ffn2,
            "norm_final": norm_final.reshape(1, D), "alog_row": _lane_row(a_log), "dtb_row": _lane_row(dt_bias),
            "dn_norm": dn_norm}


def _grad_slabs(group, g):
    if group in ("gu1", "gu2"):
        i = group[-1]
        return {"gate" + i: _shards_from_cols(g["wgu" + i][:, :F], BF),
                "up" + i: _shards_from_cols(g["wgu" + i][:, F:], BF)}
    if group in ("d1", "d2"):
        return {"down" + group[-1]: g["wd" + group[-1]].reshape(N_DEV, F // N_DEV, D).astype(BF)}
    if group == "wout":
        return {"w_out": g["wout"].reshape(N_DEV, D // N_DEV, D).astype(BF)}
    gp = jnp.concatenate([g["win_a"], g["win_d"]], axis=1)
    c0 = 3 * AW + 3 * DNW
    g_in = jnp.concatenate([gp[:, :c0], gp[:, c0 + DNW:c0 + DNW + 2 * DN_H], gp[:, c0:c0 + DNW]], axis=1)
    return {"w_in": _shards_from_cols(g_in, BF), "conv_w": _shards_from_cols(g["conv_w8"][:4], F32)}


SMALL_ROWS = 40


def _small_pack(norm_ffn1, norm_mix, norm_ffn2, norm_final, dn_norm, alog_row, dtb_row):
    rows = [a.reshape(8, 128) for a in (norm_ffn1, norm_mix, norm_ffn2, norm_final)]
    rows += [dn_norm.reshape(1, 128), alog_row, dtb_row, jnp.zeros((SMALL_ROWS - 35, 128), F32)]
    return jnp.concatenate(rows, axis=0)


def _small_unpack(pk):
    pk = pk[0]
    return (pk[0:8].reshape(1, D), pk[8:16].reshape(1, D), pk[16:24].reshape(1, D), pk[24:32].reshape(D),
            pk[32:33], pk[33:34, DN_H:2 * DN_H], pk[34:35, DN_H:2 * DN_H])


ADAM_TILE = {"gate1": 256, "up1": 256, "down1": 176, "gate2": 256, "up2": 256, "down2": 176,
             "w_in": 256, "w_out": 128, "conv_w": 4}
BIG = ("gate1", "up1", "down1", "w_in", "w_out", "gate2", "up2", "down2", "conv_w")


def kernel(x, norm_ffn1, ffn1_gate, ffn1_up, ffn1_down, norm_mix, w_in, conv_w, a_log, dt_bias, dn_norm, w_out, norm_ffn2, ffn2_gate, ffn2_up, ffn2_down, norm_final, loss_target, m_norm_ffn1, m_ffn1_gate, m_ffn1_up, m_ffn1_down, m_norm_mix, m_w_in, m_conv_w, m_a_log, m_dt_bias, m_dn_norm, m_w_out, m_norm_ffn2, m_ffn2_gate, m_ffn2_up, m_ffn2_down, m_norm_final, v_norm_ffn1, v_ffn1_gate, v_ffn1_up, v_ffn1_down, v_norm_mix, v_w_in, v_conv_w, v_a_log, v_dt_bias, v_dn_norm, v_w_out, v_norm_ffn2, v_ffn2_gate, v_ffn2_up, v_ffn2_down, v_norm_final):
    w = {"gate1": ffn1_gate, "up1": ffn1_up, "down1": ffn1_down, "w_in": w_in, "w_out": w_out,
         "gate2": ffn2_gate, "up2": ffn2_up, "down2": ffn2_down, "conv_w": conv_w}
    m = {"gate1": m_ffn1_gate, "up1": m_ffn1_up, "down1": m_ffn1_down, "w_in": m_w_in, "w_out": m_w_out,
         "gate2": m_ffn2_gate, "up2": m_ffn2_up, "down2": m_ffn2_down, "conv_w": m_conv_w}
    v = {"gate1": v_ffn1_gate, "up1": v_ffn1_up, "down1": v_ffn1_down, "w_in": v_w_in, "w_out": v_w_out,
         "gate2": v_ffn2_gate, "up2": v_ffn2_up, "down2": v_ffn2_down, "conv_w": v_conv_w}

    me = 4 * lax.axis_index("x") + 2 * lax.axis_index("y") + lax.axis_index("c")
    own_slot = lambda land, mine: lax.dynamic_update_index_in_dim(land, mine, me, 0)

    ag_order = ("gate1", "up1", "down1", "w_in", "conv_w", "w_out", "gate2", "up2", "down2")
    ag_groups = (("gate1", "up1"), ("down1",), ("w_in", "conv_w", "w_out"), ("gate2", "up2", "down2"))
    pos = {n: i for i, n in enumerate(ag_order)}
    ss, rs, thru, land, zero = _xstart(
        [w[n][0].astype(F32 if n == "conv_w" else BF) for n in ag_order], False, "weights_start")
    gath, built = {}, {}

    def need(name, after):
        if name not in built:
            for src in WEIGHT_SOURCES[name]:
                if src not in gath:
                    gi = [i for i, grp in enumerate(ag_groups) if src in grp][0]
                    ids = [pos[n] for n in ag_groups[gi]]
                    thrus, lands = _xwait([ss[i] for i in ids], [rs[i] for i in ids], [thru[i] for i in ids],
                                          [land[i] for i in ids], False, after, "weights_wait%d" % gi)
                    for n, t, l in zip(ag_groups[gi], thrus, lands):
                        gath[n] = own_slot(l, t)
            built.update(_build_weights(name, gath))
        return built[name]

    pending = []

    def emit(group, grads):
        slabs = grads if group == "small" else _grad_slabs(group, grads)
        names = list(slabs)
        started = _xstart([slabs[n] for n in names], True, "grads_start_" + group)
        pending.append((group, names) + started[:4])
        return started[4]

    sp = _small_params(norm_ffn1 + zero, norm_mix, norm_ffn2, norm_final, a_log, dt_bias, dn_norm)
    loss_part, dx, g = _local_step(x[0], loss_target[0], sp, need, emit)
    small = _small_pack(g["norm_ffn1"], g["norm_mix"], g["norm_ffn2"], g["norm_final"], g["dn_norm"],
                        g["alog_row"], g["dtb_row"])
    emit("small", {"small": jnp.broadcast_to(small[None], (N_DEV, SMALL_ROWS, 128))})

    pack = lambda a: _small_pack(*a)[None]
    res, after = {}, dx
    for group, names, gss, grs, gthru, gland in pending:
        thrus, lands = _xwait(gss, grs, gthru, gland, True, after, "grads_wait_" + group)
        for n, t, l in zip(names, thrus, lands):
            recv = own_slot(l, lax.dynamic_index_in_dim(t, me, 0, keepdims=False))
            if n == "small":
                res[n] = _adam(
                    recv,
                    pack((norm_ffn1, norm_mix, norm_ffn2, norm_final, dn_norm, _lane_row(a_log), _lane_row(dt_bias))),
                    pack((m_norm_ffn1, m_norm_mix, m_norm_ffn2, m_norm_final, m_dn_norm, _lane_row(m_a_log),
                          _lane_row(m_dt_bias))),
                    pack((v_norm_ffn1, v_norm_mix, v_norm_ffn2, v_norm_final, v_dn_norm, _lane_row(v_a_log),
                          _lane_row(v_dt_bias))),
                    SMALL_ROWS, "adam_small")
            else:
                res[n] = _adam(recv, w[n], m[n], v[n], ADAM_TILE[n], "adam_" + n)
            after = res[n][0]
    res_s = res["small"]

    loss = lax.psum(loss_part, ("x", "y", "c"))
    outs = [loss, dx[None]]
    for k in range(4):
        n1, nm, n2, nf, dn, al, dt = _small_unpack(res_s[k])
        big = {n: res[n][k] for n in BIG}
        outs += [n1, big["gate1"], big["up1"], big["down1"], nm, big["w_in"], big["conv_w"], al, dt, dn,
                 big["w_out"], n2, big["gate2"], big["up2"], big["down2"], nf]
    return tuple(outs)
```

```python
import functools

import numpy as np
import jax
import jax.numpy as jnp
from jax import lax
from jax.experimental import pallas as pl
from jax.experimental.pallas import tpu as pltpu

T = 4096
D = 1024
F = 2816
N_DEV = 8
A_HEADS = 8
A_HD = 64
AW = A_HEADS * A_HD
DN_H = 4
DN_HD = 128
DNW = DN_H * DN_HD
CH = 64
PAIR = 2 * CH
ZA = 3 * AW
ZD = 3 * DNW + DNW + 256
ZP = ZA + ZD
BD_BLK = (3 * DNW + DNW) // 128
IN_COLS = 3592
PATTERNS = ((128, 1), (512, 4), (2048, 16))
NORM_EPS = 1e-6
L2_EPS = 1e-6
ADAM_LR, ADAM_B1, ADAM_B2, ADAM_EPS, ADAM_WD, ADAM_STEP = 0.001, 0.9, 0.999, 1e-08, 0.01, 10
VMEM_LIMIT = 56 * 1024 * 1024
NEG = -1e30

BF = jnp.bfloat16
F32 = jnp.float32
NN = (((1,), (0,)), ((), ()))
NT = (((1,), (1,)), ((), ()))
TN = (((0,), (0,)), ((), ()))
HI = lax.Precision.HIGHEST
MESH_ID = pl.DeviceIdType.MESH
ANY = pl.BlockSpec(memory_space=pl.ANY)


def _cp():
    return pltpu.CompilerParams(vmem_limit_bytes=VMEM_LIMIT)


def _dg(a, b, dims):
    return lax.dot_general(a, b, dims, preferred_element_type=F32)


def _hdot(a, b):
    return lax.dot_general(a, b, NN, precision=HI, preferred_element_type=F32)


def _make_bdot(dims, da_dims, da_swap, db_dims, db_swap):
    @jax.custom_vjp
    def f(a, b):
        return _dg(a.astype(BF), b.astype(BF), dims)

    def fwd(a, b):
        return f(a, b), (a, b)

    def bwd(res, g):
        a, b = res
        gb, ab, bb = g.astype(BF), a.astype(BF), b.astype(BF)
        da = _dg(bb, gb, da_dims) if da_swap else _dg(gb, bb, da_dims)
        db = _dg(gb, ab, db_dims) if db_swap else _dg(ab, gb, db_dims)
        return da, db

    f.defvjp(fwd, bwd)
    return f


_bdot_nn = _make_bdot(NN, NT, False, TN, False)
_bdot_nt = _make_bdot(NT, NN, False, TN, True)
_bdot_tn = _make_bdot(TN, NT, True, NN, False)


def _iota(shape, dim):
    return lax.broadcasted_iota(jnp.int32, shape, dim)


def _col(x, idx):
    return jnp.sum(jnp.where(_iota(x.shape, 1) == idx, x, 0.0), axis=1, keepdims=True)


def _mm_nn(a, b, out_dtype, tm, tn, name):
    m, k = a.shape
    n = b.shape[1]

    def body(a_ref, b_ref, o_ref):
        o_ref[...] = _dg(a_ref[...], b_ref[...], NN).astype(out_dtype)

    return pl.pallas_call(
        body, grid=(m // tm, n // tn),
        in_specs=[pl.BlockSpec((tm, k), lambda i, j: (i, 0)), pl.BlockSpec((k, tn), lambda i, j: (0, j))],
        out_specs=pl.BlockSpec((tm, tn), lambda i, j: (i, j)),
        out_shape=jax.ShapeDtypeStruct((m, n), out_dtype), name=name, compiler_params=_cp())(a, b)


def _mm_nt(a, b, out_dtype, tm, tb, name):
    m, c = a.shape
    kb = b.shape[0]

    def body(a_ref, b_ref, o_ref):
        o_ref[...] = _dg(a_ref[...], b_ref[...], NT).astype(out_dtype)

    return pl.pallas_call(
        body, grid=(m // tm, kb // tb),
        in_specs=[pl.BlockSpec((tm, c), lambda i, j: (i, 0)), pl.BlockSpec((tb, c), lambda i, j: (j, 0))],
        out_specs=pl.BlockSpec((tm, tb), lambda i, j: (i, j)),
        out_shape=jax.ShapeDtypeStruct((m, kb), out_dtype), name=name, compiler_params=_cp())(a, b)


def _mm_tn(a, b, out_dtype, ta, tb, name):
    m, ka = a.shape
    nb = b.shape[1]

    def body(a_ref, b_ref, o_ref):
        o_ref[...] = _dg(a_ref[...], b_ref[...], TN).astype(out_dtype)

    return pl.pallas_call(
        body, grid=(ka // ta, nb // tb),
        in_specs=[pl.BlockSpec((m, ta), lambda i, j: (0, i)), pl.BlockSpec((m, tb), lambda i, j: (0, j))],
        out_specs=pl.BlockSpec((ta, tb), lambda i, j: (i, j)),
        out_shape=jax.ShapeDtypeStruct((ka, nb), out_dtype), name=name, compiler_params=_cp())(a, b)


def _mm_nn2(a1, a2, b1, b2, tm, tn, name):
    m, k = a1.shape
    n = b1.shape[1]

    def body(a1_ref, a2_ref, b1_ref, b2_ref, o_ref):
        o_ref[...] = _dg(a1_ref[...], b1_ref[...], NN) + _dg(a2_ref[...], b2_ref[...], NN)

    arow = pl.BlockSpec((tm, k), lambda i, j: (i, 0))
    bcol = pl.BlockSpec((k, tn), lambda i, j: (0, j))
    return pl.pallas_call(
        body, grid=(m // tm, n // tn), in_specs=[arow, arow, bcol, bcol],
        out_specs=pl.BlockSpec((tm, tn), lambda i, j: (i, j)),
        out_shape=jax.ShapeDtypeStruct((m, n), F32), name=name, compiler_params=_cp())(a1, a2, b1, b2)


def _mm_nt_swiglu(h, wu_t, gate, tm, tb, name):
    m, c = h.shape
    kb = wu_t.shape[0]

    def body(h_ref, w_ref, g_ref, u_ref, a_ref):
        u = _dg(h_ref[...], w_ref[...], NT)
        g = g_ref[...].astype(F32)
        u_ref[...] = u.astype(BF)
        a_ref[...] = (g * jax.nn.sigmoid(g) * u).astype(BF)

    tile = pl.BlockSpec((tm, tb), lambda i, j: (i, j))
    return pl.pallas_call(
        body, grid=(m // tm, kb // tb),
        in_specs=[pl.BlockSpec((tm, c), lambda i, j: (i, 0)), pl.BlockSpec((tb, c), lambda i, j: (j, 0)), tile],
        out_specs=[tile, tile], out_shape=[jax.ShapeDtypeStruct((m, kb), BF)] * 2,
        name=name, compiler_params=_cp())(h, wu_t, gate)


def _mm_nt_dswiglu(dys, wd, gate, up, tm, tb, name):
    m, c = dys.shape
    kb = wd.shape[0]

    def body(d_ref, w_ref, g_ref, u_ref, dg_ref, du_ref):
        da = _dg(d_ref[...], w_ref[...], NT)
        g = g_ref[...].astype(F32)
        u = u_ref[...].astype(F32)
        s = jax.nn.sigmoid(g)
        dg_ref[...] = (da * u * (s * (1.0 + g * (1.0 - s)))).astype(BF)
        du_ref[...] = (da * (g * s)).astype(BF)

    tile = pl.BlockSpec((tm, tb), lambda i, j: (i, j))
    return pl.pallas_call(
        body, grid=(m // tm, kb // tb),
        in_specs=[pl.BlockSpec((tm, c), lambda i, j: (i, 0)), pl.BlockSpec((tb, c), lambda i, j: (j, 0)),
                  tile, tile],
        out_specs=[tile, tile], out_shape=[jax.ShapeDtypeStruct((m, kb), BF)] * 2,
        name=name, compiler_params=_cp())(dys, wd, gate, up)


def _rms_fwd(x, y, alpha, gain, name):
    tm = 512
    has_y = y is not None
    row = pl.BlockSpec((tm, D), lambda i: (i, 0))
    gspec = pl.BlockSpec((1, D), lambda i: (0, 0))

    def body(*refs):
        if has_y:
            x_ref, y_ref, g_ref, xo_ref, h_ref = refs
            xv = x_ref[...] + alpha * y_ref[...]
            xo_ref[...] = xv
        else:
            x_ref, g_ref, h_ref = refs
            xv = x_ref[...]
        r = lax.rsqrt(jnp.mean(xv * xv, axis=-1, keepdims=True) + NORM_EPS)
        h_ref[...] = (xv * r * g_ref[...]).astype(BF)

    if has_y:
        return pl.pallas_call(
            body, grid=(T // tm,), in_specs=[row, row, gspec], out_specs=[row, row],
            out_shape=[jax.ShapeDtypeStruct((T, D), F32), jax.ShapeDtypeStruct((T, D), BF)],
            name=name, compiler_params=_cp())(x, y, gain)
    h = pl.pallas_call(
        body, grid=(T // tm,), in_specs=[row, gspec], out_specs=row,
        out_shape=jax.ShapeDtypeStruct((T, D), BF), name=name, compiler_params=_cp())(x, gain)
    return x, h


def _rms_bwd(x, gain, dh, dres, alpha_out, name):
    tm = 512
    row = pl.BlockSpec((tm, D), lambda i: (i, 0))
    gspec = pl.BlockSpec((1, D), lambda i: (0, 0))

    def body(x_ref, g_ref, dh_ref, dres_ref, dx_ref, dxs_ref, dg_ref):
        i = pl.program_id(0)
        xv = x_ref[...]
        r = lax.rsqrt(jnp.mean(xv * xv, axis=-1, keepdims=True) + NORM_EPS)
        xh = xv * r
        dhv = dh_ref[...].astype(F32)
        part = jnp.sum(dhv * xh, axis=0, keepdims=True)

        @pl.when(i == 0)
        def _():
            dg_ref[...] = part

        @pl.when(i > 0)
        def _():
            dg_ref[...] += part

        dxh = dhv * g_ref[...]
        dx = r * (dxh - xh * jnp.mean(dxh * xh, axis=-1, keepdims=True)) + dres_ref[...]
        dx_ref[...] = dx
        dxs_ref[...] = (alpha_out * dx).astype(BF)

    return pl.pallas_call(
        body, grid=(T // tm,), in_specs=[row, gspec, row, row], out_specs=[row, row, gspec],
        out_shape=[jax.ShapeDtypeStruct((T, D), F32), jax.ShapeDtypeStruct((T, D), BF),
                   jax.ShapeDtypeStruct((1, D), F32)],
        name=name, compiler_params=_cp())(x, gain, dh, dres)


def _loss_bwd(x, gain, target, name):
    tm = 512
    row = pl.BlockSpec((tm, D), lambda i: (i, 0))
    gspec = pl.BlockSpec((1, D), lambda i: (0, 0))
    lspec = pl.BlockSpec((8, 128), lambda i: (0, 0))

    def body(x_ref, g_ref, t_ref, dx_ref, dxs_ref, dg_ref, loss_ref):
        i = pl.program_id(0)
        xv = x_ref[...]
        r = lax.rsqrt(jnp.mean(xv * xv, axis=-1, keepdims=True) + NORM_EPS)
        xh = xv * r
        diff = xh * g_ref[...] - t_ref[...]
        lpart = 0.5 * jnp.sum(jnp.mean(diff * diff, axis=-1, keepdims=True), axis=0, keepdims=True)
        dy = diff * (1.0 / D)
        part = jnp.sum(dy * xh, axis=0, keepdims=True)

        @pl.when(i == 0)
        def _():
            dg_ref[...] = part
            loss_ref[...] = jnp.broadcast_to(lpart, (8, 128))

        @pl.when(i > 0)
        def _():
            dg_ref[...] += part
            loss_ref[...] += jnp.broadcast_to(lpart, (8, 128))

        dxh = dy * g_ref[...]
        dx = r * (dxh - xh * jnp.mean(dxh * xh, axis=-1, keepdims=True))
        dx_ref[...] = dx
        dxs_ref[...] = (0.5 * dx).astype(BF)

    return pl.pallas_call(
        body, grid=(T // tm,), in_specs=[row, gspec, row], out_specs=[row, row, gspec, lspec],
        out_shape=[jax.ShapeDtypeStruct((T, D), F32), jax.ShapeDtypeStruct((T, D), BF),
                   jax.ShapeDtypeStruct((1, D), F32), jax.ShapeDtypeStruct((8, 128), F32)],
        name=name, compiler_params=_cp())(x, gain, target)


def _ffn_fwd(x_prev, y_prev, alpha, gain, need, tag):
    x_in, h = _rms_fwd(x_prev, y_prev, alpha, gain, tag + "_norm")
    wg = need("g", h)
    gate = _mm_nt(h, wg, BF, 1024, 1408, tag + "_gate")
    wu = need("u", gate)
    up, act = _mm_nt_swiglu(h, wu, gate, 1024, 1408, tag + "_up_act")
    wd = need("d", up)
    y = _mm_nn(act, wd, F32, 1024, 1024, tag + "_down")
    return x_in, y, (h, gate, up, act, wg, wu, wd)


def _ffn_bwd(x_in, gain, saved, dxo, dys, alpha_out, emit, tag):
    h, gate, up, act, wg, wu, wd = saved
    zero = emit("d", _mm_tn(act, dys, BF, 256, 1024, tag + "_dwd"))
    dgate, dup = _mm_nt_dswiglu(dys, wd, gate, up, 1024, 1408, tag + "_dact")
    zero = zero + emit("g", _mm_tn(dgate, h, BF, 256, 1024, tag + "_dwg"))
    zero = zero + emit("u", _mm_tn(dup, h, BF, 256, 1024, tag + "_dwu"))
    dh = _mm_nn2(dgate, dup, wg, wu, 512, 512, tag + "_dh")
    dx, dxs, dgain = _rms_bwd(x_in, gain + zero, dh, dxo, alpha_out, tag + "_dnorm")
    return dx, dxs, dgain


SLAB = 2048
N_SLAB = T // SLAB
N_PAIR = A_HEADS // 2


def _pair_masks():
    lane = _iota((128, 128), 1)
    return lane < A_HD, lane >= A_HD


def _slope_table():
    h = 2 * jnp.arange(N_PAIR)[:, None] + jnp.minimum(jnp.arange(8), 1)[None, :]
    return jnp.broadcast_to((2.0 ** (-(h + 1).astype(F32)))[:, :, None], (N_PAIR, 8, 128))


def _rows(ref, start, d):
    if d == 1:
        return ref[pl.ds(start, 128), :]
    return ref[pl.ds(start, 128, stride=d), :]


def _put_rows(ref, start, d, val):
    if d == 1:
        ref[pl.ds(start, 128), :] = val
    else:
        ref[pl.ds(start, 128, stride=d), :] = val


def _units(d):
    return [(r, b, r + 128 * d * b) for r in range(d) for b in range(SLAB // (128 * d))]


def _band(d, prev_valid):
    qi = _iota((128, 256), 0)
    kj = _iota((128, 256), 1)
    steps = qi + 128 - kj
    valid = (steps >= 0) & (steps <= 128) & (prev_valid | (kj >= 128))
    return valid, (steps * d).astype(F32)


def _attn_fwd(z_at, name):
    def body(sl_ref, q_ref, kc_ref, kp_ref, vc_ref, vp_ref, of_ref, ob_ref, lse_ref, m_s, l_s, a_s):
        n = pl.program_id(1)
        lo, hi = _pair_masks()
        slopes = (sl_ref[0, 0:1, 0:1], sl_ref[0, 1:2, 0:1])

        def unit(d, start, b, first):
            q = _rows(q_ref, start, d).astype(BF)
            if b > 0:
                kprev, vprev, prev_valid = _rows(kc_ref, start - 128 * d, d), _rows(vc_ref, start - 128 * d, d), True
            else:
                pstart = start + SLAB - 128 * d
                kprev, vprev, prev_valid = _rows(kp_ref, pstart, d), _rows(vp_ref, pstart, d), n > 0
            kcat = jnp.concatenate([kprev, _rows(kc_ref, start, d)], axis=0).astype(BF)
            vcat = jnp.concatenate([vprev, _rows(vc_ref, start, d)], axis=0).astype(BF)
            valid, dist = _band(d, prev_valid)
            ms, ls, pvs = [], [], []
            for e in range(2):
                qm = jnp.where(lo if e == 0 else hi, q, jnp.zeros_like(q))
                s = _dg(qm, kcat, NT) * (A_HD ** -0.5) - slopes[e] * dist
                s = jnp.where(valid, s, NEG)
                m = jnp.max(s, axis=1, keepdims=True)
                p = jnp.exp(s - m)
                ms.append(m)
                ls.append(jnp.sum(p, axis=1, keepdims=True))
                pvs.append(_dg(p.astype(BF), vcat, NN))
            m_u = jnp.where(lo, ms[0], ms[1])
            l_u = jnp.where(lo, ls[0], ls[1])
            a_u = jnp.where(lo, pvs[0], pvs[1])
            if first:
                m_n, l_n, a_n = m_u, l_u, a_u
            else:
                m_o = _rows(m_s, start, d)
                m_n = jnp.maximum(m_o, m_u)
                c_o = jnp.exp(m_o - m_n)
                c_u = jnp.exp(m_u - m_n)
                l_n = _rows(l_s, start, d) * c_o + l_u * c_u
                a_n = _rows(a_s, start, d) * c_o + a_u * c_u
            _put_rows(m_s, start, d, m_n)
            _put_rows(l_s, start, d, l_n)
            _put_rows(a_s, start, d, a_n)

        for pi, (_, d) in enumerate(PATTERNS):
            for r, b, start in _units(d):
                unit(d, start, b, pi == 0)
        l = l_s[...]
        out = a_s[...] / l
        of_ref[...] = out
        ob_ref[...] = out.astype(BF)
        lse_ref[...] = m_s[...] + jnp.log(l)

    cur = lambda c: pl.BlockSpec((SLAB, 128), lambda j, n: (n, c * N_PAIR + j))
    prv = lambda c: pl.BlockSpec((SLAB, 128), lambda j, n: (jnp.maximum(n - 1, 0), c * N_PAIR + j))
    out = pl.BlockSpec((SLAB, 128), lambda j, n: (n, j))
    return pl.pallas_call(
        body, grid=(N_PAIR, N_SLAB),
        in_specs=[pl.BlockSpec((1, 8, 128), lambda j, n: (j, 0, 0)), cur(0), cur(1), prv(1), cur(2), prv(2)],
        out_specs=[out, out, out],
        out_shape=[jax.ShapeDtypeStruct((T, AW), F32), jax.ShapeDtypeStruct((T, AW), BF),
                   jax.ShapeDtypeStruct((T, AW), F32)],
        scratch_shapes=[pltpu.VMEM((SLAB, 128), F32)] * 3,
        name=name, compiler_params=_cp())(_slope_table(), z_at, z_at, z_at, z_at, z_at)


def _attn_bwd(z_at, dout, out, lse, name):
    def body(sl_ref, q_ref, kc_ref, kp_ref, vc_ref, vp_ref, do_ref, o_ref, lse_ref, dq_ref, dk_ref, dv_ref,
             dq_s, dk_s, dv_s, ck_s, cv_s):
        step = pl.program_id(1)
        n = N_SLAB - 1 - step
        lo, hi = _pair_masks()
        slopes = (sl_ref[0, 0:1, 0:1], sl_ref[0, 1:2, 0:1])

        @pl.when(step == 0)
        def _():
            ck_s[...] = jnp.zeros_like(ck_s)
            cv_s[...] = jnp.zeros_like(cv_s)

        dk_s[...] = ck_s[...]
        dv_s[...] = cv_s[...]
        ck_s[...] = jnp.zeros_like(ck_s)
        cv_s[...] = jnp.zeros_like(cv_s)

        def add_rows(ref, start, d, val):
            _put_rows(ref, start, d, _rows(ref, start, d) + val)

        def unit(d, start, b, first):
            q = _rows(q_ref, start, d).astype(BF)
            do_f = _rows(do_ref, start, d)
            do = do_f.astype(BF)
            prod = do_f * _rows(o_ref, start, d)
            lse_u = _rows(lse_ref, start, d)
            if b > 0:
                kprev, vprev, prev_valid = _rows(kc_ref, start - 128 * d, d), _rows(vc_ref, start - 128 * d, d), True
            else:
                pstart = start + SLAB - 128 * d
                kprev, vprev, prev_valid = _rows(kp_ref, pstart, d), _rows(vp_ref, pstart, d), n > 0
            kcat = jnp.concatenate([kprev, _rows(kc_ref, start, d)], axis=0).astype(BF)
            vcat = jnp.concatenate([vprev, _rows(vc_ref, start, d)], axis=0).astype(BF)
            valid, dist = _band(d, prev_valid)
            dqs = []
            dkc = jnp.zeros((256, 128), F32)
            dvc = jnp.zeros((256, 128), F32)
            for e in range(2):
                msk = lo if e == 0 else hi
                qm = jnp.where(msk, q, jnp.zeros_like(q))
                dom = jnp.where(msk, do, jnp.zeros_like(do))
                delta = jnp.sum(jnp.where(msk, prod, 0.0), axis=1, keepdims=True)
                lse_e = lse_u[:, 64 * e:64 * e + 1]
                s = _dg(qm, kcat, NT) * (A_HD ** -0.5) - slopes[e] * dist
                p = jnp.where(valid, jnp.exp(jnp.where(valid, s, NEG) - lse_e), 0.0)
                dp = _dg(dom, vcat, NT)
                ds = (p * (dp - delta)).astype(BF)
                dqs.append(_dg(ds, kcat, NN))
                dkc = dkc + _dg(ds, qm, TN)
                dvc = dvc + _dg(p.astype(BF), dom, TN)
            dq_u = jnp.where(lo, dqs[0], dqs[1]) * (A_HD ** -0.5)
            dkc = dkc * (A_HD ** -0.5)
            if first:
                _put_rows(dq_s, start, d, dq_u)
            else:
                add_rows(dq_s, start, d, dq_u)
            add_rows(dk_s, start, d, dkc[128:])
            add_rows(dv_s, start, d, dvc[128:])
            if b > 0:
                add_rows(dk_s, start - 128 * d, d, dkc[:128])
                add_rows(dv_s, start - 128 * d, d, dvc[:128])
            else:
                pstart = start + SLAB - 128 * d
                add_rows(ck_s, pstart, d, dkc[:128])
                add_rows(cv_s, pstart, d, dvc[:128])

        for pi, (_, d) in enumerate(PATTERNS):
            for r, b, start in _units(d):
                unit(d, start, b, pi == 0)
        dq_ref[...] = dq_s[...].astype(BF)
        dk_ref[...] = dk_s[...].astype(BF)
        dv_ref[...] = dv_s[...].astype(BF)

    rev = lambda n: N_SLAB - 1 - n
    cur = lambda c: pl.BlockSpec((SLAB, 128), lambda j, n: (rev(n), c * N_PAIR + j))
    prv = lambda c: pl.BlockSpec((SLAB, 128), lambda j, n: (jnp.maximum(rev(n) - 1, 0), c * N_PAIR + j))
    one = pl.BlockSpec((SLAB, 128), lambda j, n: (rev(n), j))
    return pl.pallas_call(
        body, grid=(N_PAIR, N_SLAB),
        in_specs=[pl.BlockSpec((1, 8, 128), lambda j, n: (j, 0, 0)), cur(0), cur(1), prv(1), cur(2), prv(2),
                  one, one, one],
        out_specs=[one, one, one], out_shape=[jax.ShapeDtypeStruct((T, AW), BF)] * 3,
        scratch_shapes=[pltpu.VMEM((SLAB, 128), F32)] * 5,
        name=name, compiler_params=_cp())(_slope_table(), z_at, z_at, z_at, z_at, z_at, dout, out, lse)


def _silu(x):
    return x * jax.nn.sigmoid(x)


def _qk_math(c):
    s = _silu(c)
    return s * lax.rsqrt(jnp.sum(s * s, axis=-1, keepdims=True) + L2_EPS)


def _softplus(x):
    return jnp.maximum(x, 0.0) + jnp.log(1.0 + jnp.exp(-jnp.abs(x)))


def _gate_math(bd, alog_row, dtb_row):
    rows = bd.shape[0]
    lane = _iota(bd.shape, 1)
    beta = jax.nn.sigmoid(bd)
    g = jnp.where((lane >= DN_H) & (lane < 2 * DN_H), -jnp.exp(alog_row) * _softplus(bd + dtb_row), 0.0)
    ri = _iota((rows, rows), 0)
    ci = _iota((rows, rows), 1)
    same = (ri // CH) == (ci // CH)
    li = _iota((128, 128), 0)
    lj = _iota((128, 128), 1)
    to_next_group = jnp.where((lj == li + DN_H) & (li >= DN_H) & (li < 2 * DN_H), 1.0, 0.0)
    gc = _hdot(jnp.where(same & (ci <= ri), 1.0, 0.0), g)
    glast = _hdot(_hdot(jnp.where(same, 1.0, 0.0), g), to_next_group)
    return jnp.where(lane < DN_H, beta, 0.0) + gc + glast


def _shift_down(cur, halo, s):
    if s == 0:
        return cur
    rolled = pltpu.roll(cur, s, 0)
    hr = pltpu.roll(halo, s, 0)
    head = jnp.where(_iota(hr.shape, 0) < s, hr, rolled[:8])
    return jnp.concatenate([head, rolled[8:]], axis=0)


def _shift_up(cur, halo, s):
    if s == 0:
        return cur
    rows = cur.shape[0]
    rolled = pltpu.roll(cur, rows - s, 0)
    hr = pltpu.roll(halo, 8 - s, 0)
    tail = jnp.where(_iota(hr.shape, 0) >= 8 - s, hr, rolled[rows - 8:])
    return jnp.concatenate([rolled[:rows - 8], tail], axis=0)


def _dn_pre_fwd(z_dn, conv_w8, alog_row, dtb_row, name):
    tm = 256
    wq = 3 * DNW

    def body(raw_ref, halo_ref, bd_ref, w_ref, al_ref, dt_ref, conv_ref, qkv_ref, bg_ref):
        i = pl.program_id(0)
        cur = raw_ref[...]
        halo = jnp.where(i > 0, halo_ref[...], 0.0)
        w = w_ref[...]
        conv = jnp.zeros((tm, wq), F32)
        for j in range(4):
            conv = conv + _shift_down(cur, halo, 3 - j) * w[j:j + 1, :]
        conv_ref[...] = conv
        for blk in range(3 * DN_H):
            sl = slice(128 * blk, 128 * blk + 128)
            c = conv[:, sl]
            qkv_ref[:, sl] = _qk_math(c) if blk < 2 * DN_H else _silu(c)
        bg_ref[...] = _gate_math(bd_ref[...], al_ref[...], dt_ref[...])

    one = pl.BlockSpec((1, 128), lambda i: (0, 0))
    return pl.pallas_call(
        body, grid=(T // tm,),
        in_specs=[pl.BlockSpec((tm, wq), lambda i: (i, 0)),
                  pl.BlockSpec((8, wq), lambda i: (jnp.maximum(i * (tm // 8) - 1, 0), 0)),
                  pl.BlockSpec((tm, 128), lambda i: (i, BD_BLK)),
                  pl.BlockSpec((8, wq), lambda i: (0, 0)), one, one],
        out_specs=[pl.BlockSpec((tm, wq), lambda i: (i, 0)), pl.BlockSpec((tm, wq), lambda i: (i, 0)),
                   pl.BlockSpec((tm, 128), lambda i: (i, 0))],
        out_shape=[jax.ShapeDtypeStruct((T, wq), F32), jax.ShapeDtypeStruct((T, wq), F32),
                   jax.ShapeDtypeStruct((T, 128), F32)],
        name=name, compiler_params=_cp())(z_dn, z_dn, z_dn, conv_w8, alog_row, dtb_row)


def _dn_pre_bwd(conv, z_dn, alog_row, dtb_row, dqn, dkn, dvn, dbg, name):
    tm = 256
    wq = 3 * DNW

    def body(conv_ref, bd_ref, al_ref, dt_ref, dq_ref, dk_ref, dv_ref, dbg_ref,
             dconv_ref, dbd_ref, dal_ref, ddt_ref):
        i = pl.program_id(0)
        for blk in range(3 * DN_H):
            sl = slice(128 * blk, 128 * blk + 128)
            src = (dq_ref, dk_ref, dv_ref)[blk // DN_H]
            ct = src[:, 128 * (blk % DN_H):128 * (blk % DN_H) + 128]
            fn = _qk_math if blk < 2 * DN_H else _silu
            _, vjp = jax.vjp(fn, conv_ref[:, sl])
            dconv_ref[:, sl] = vjp(ct)[0]
        _, vjp = jax.vjp(_gate_math, bd_ref[...], al_ref[...], dt_ref[...])
        dbd, dal, ddt = vjp(dbg_ref[...])
        dbd_ref[...] = dbd

        @pl.when(i == 0)
        def _():
            dal_ref[...] = dal
            ddt_ref[...] = ddt

        @pl.when(i > 0)
        def _():
            dal_ref[...] += dal
            ddt_ref[...] += ddt

    one = pl.BlockSpec((1, 128), lambda i: (0, 0))
    row = pl.BlockSpec((tm, wq), lambda i: (i, 0))
    hd = pl.BlockSpec((tm, DNW), lambda i: (i, 0))
    st = pl.BlockSpec((tm, 128), lambda i: (i, 0))
    return pl.pallas_call(
        body, grid=(T // tm,),
        in_specs=[row, pl.BlockSpec((tm, 128), lambda i: (i, BD_BLK)), one, one, hd, hd, hd, st],
        out_specs=[row, st, one, one],
        out_shape=[jax.ShapeDtypeStruct((T, wq), F32), jax.ShapeDtypeStruct((T, 128), F32),
                   jax.ShapeDtypeStruct((1, 128), F32), jax.ShapeDtypeStruct((1, 128), F32)],
        name=name, compiler_params=_cp())(conv, z_dn, alog_row, dtb_row, dqn, dkn, dvn, dbg)


def _dn_conv_bwd(dconv, z_dn, conv_w8, name):
    tm = 256
    wq = 3 * DNW
    last = T // tm - 1

    def body(dc_ref, dcn_ref, raw_ref, halo_ref, w_ref, draw_ref, dw_ref):
        i = pl.program_id(0)
        dc = dc_ref[...]
        nxt = jnp.where(i < last, dcn_ref[...], 0.0)
        cur = raw_ref[...]
        halo = jnp.where(i > 0, halo_ref[...], 0.0)
        w = w_ref[...]
        draw = jnp.zeros((tm, wq), F32)
        rows = []
        for j in range(4):
            draw = draw + _shift_up(dc, nxt, 3 - j) * w[j:j + 1, :]
            rows.append(jnp.sum(dc * _shift_down(cur, halo, 3 - j), axis=0, keepdims=True))
        draw_ref[...] = draw
        part = jnp.concatenate(rows + [jnp.zeros((4, wq), F32)], axis=0)

        @pl.when(i == 0)
        def _():
            dw_ref[...] = part

        @pl.when(i > 0)
        def _():
            dw_ref[...] += part

    row = pl.BlockSpec((tm, wq), lambda i: (i, 0))
    return pl.pallas_call(
        body, grid=(T // tm,),
        in_specs=[row, pl.BlockSpec((8, wq), lambda i: (jnp.minimum((i + 1) * (tm // 8), T // 8 - 1), 0)),
                  row, pl.BlockSpec((8, wq), lambda i: (jnp.maximum(i * (tm // 8) - 1, 0), 0)),
                  pl.BlockSpec((8, wq), lambda i: (0, 0))],
        out_specs=[row, pl.BlockSpec((8, wq), lambda i: (0, 0))],
        out_shape=[jax.ShapeDtypeStruct((T, wq), F32), jax.ShapeDtypeStruct((8, wq), F32)],
        name=name, compiler_params=_cp())(dconv, dconv, z_dn, z_dn, conv_w8)


def _h3(a, b, dims=NN):
    return lax.dot_general(a, b, dims, precision=lax.Precision.HIGH, preferred_element_type=F32)


@jax.custom_vjp
def _inverse_given(a_mat, tinv):
    return tinv


def _inverse_given_fwd(a_mat, tinv):
    return tinv, tinv


def _inverse_given_bwd(tinv, g):
    return -_h3(tinv, _h3(g, tinv, NT), TN), jnp.zeros_like(tinv)


_inverse_given.defvjp(_inverse_given_fwd, _inverse_given_bwd)


def _prep_math(q, k, v, bgc, h, tinv_saved=None):
    beta = _col(bgc, h)
    gc = jnp.broadcast_to(_col(bgc, DN_H + h), (PAIR, 128))
    glast = jnp.broadcast_to(_col(bgc, 2 * DN_H + h), (PAIR, 128))
    ri = _iota((PAIR, PAIR), 0)
    ci = _iota((PAIR, PAIR), 1)
    same = (ri // CH) == (ci // CH)
    causal = same & (ci <= ri)
    strict = same & (ci < ri)
    eye = ri == ci
    gc_cols = _hdot(jnp.ones((PAIR, PAIR), F32), jnp.where(eye, gc, 0.0))
    decay = jnp.exp(jnp.where(causal, gc - gc_cols, NEG))
    egc = jnp.exp(gc)
    kb = k * beta
    a_mat = jnp.where(strict, _bdot_nt(kb, k) * decay, 0.0)
    if tinv_saved is None:
        p = -a_mat
        tinv = jnp.where(eye, 1.0, 0.0) + p
        for _ in range(5):
            p = _h3(p, p)
            tinv = tinv + _h3(tinv, p)
    else:
        tinv = _inverse_given(a_mat, tinv_saved)
    u = _h3(tinv, v * beta)
    w = _h3(tinv, kb * egc)
    qs = q * (DN_HD ** -0.5)
    attn = jnp.where(causal, _bdot_nt(qs, k) * decay, 0.0)
    return u, w, qs * egc, k * jnp.exp(glast - gc), attn, jnp.exp(glast), tinv


def _dn_prep_fwd(qkv, bg, name):
    rows = 512
    hd = lambda off: pl.BlockSpec((rows, 128), lambda g, h: (g, off + h))
    out = pl.BlockSpec((rows, 128), lambda g, h: (g, h))

    def body(q_ref, k_ref, v_ref, bg_ref, *outs):
        h = pl.program_id(1)
        for pr in range(rows // PAIR):
            rs = slice(PAIR * pr, PAIR * pr + PAIR)
            res = _prep_math(q_ref[rs, :], k_ref[rs, :], v_ref[rs, :], bg_ref[rs, :], h)
            for o_ref, val in zip(outs, res):
                o_ref[rs, :] = val

    return pl.pallas_call(
        body, grid=(T // rows, DN_H),
        in_specs=[hd(0), hd(DN_H), hd(2 * DN_H), pl.BlockSpec((rows, 128), lambda g, h: (g, 0))],
        out_specs=[out] * 7, out_shape=[jax.ShapeDtypeStruct((T, DNW), F32)] * 7,
        name=name, compiler_params=_cp())(qkv, qkv, qkv, bg)


def _dn_prep_bwd(qkv, bg, tinv, cts, name):
    rows = 512
    hd = lambda off: pl.BlockSpec((rows, 128), lambda g, h: (g, off + h))
    out = pl.BlockSpec((rows, 128), lambda g, h: (g, h))
    st = pl.BlockSpec((rows, 128), lambda g, h: (g, 0))

    def body(q_ref, k_ref, v_ref, bg_ref, ti_ref, c0, c1, c2, c3, c4, c5, dq_ref, dk_ref, dv_ref, dbg_ref):
        h = pl.program_id(1)
        parts = []
        for pr in range(rows // PAIR):
            rs = slice(PAIR * pr, PAIR * pr + PAIR)
            fn = lambda q, k, v, b, ti=ti_ref[rs, :]: _prep_math(q, k, v, b, h, ti)[:6]
            _, vjp = jax.vjp(fn, q_ref[rs, :], k_ref[rs, :], v_ref[rs, :], bg_ref[rs, :])
            dq, dk, dv, dbg = vjp(tuple(c[rs, :] for c in (c0, c1, c2, c3, c4, c5)))
            dq_ref[rs, :] = dq
            dk_ref[rs, :] = dk
            dv_ref[rs, :] = dv
            parts.append(dbg)
        dbg_all = jnp.concatenate(parts, axis=0)

        @pl.when(h == 0)
        def _():
            dbg_ref[...] = dbg_all

        @pl.when(h > 0)
        def _():
            dbg_ref[...] += dbg_all

    return pl.pallas_call(
        body, grid=(T // rows, DN_H),
        in_specs=[hd(0), hd(DN_H), hd(2 * DN_H), st] + [out] * 7,
        out_specs=[out, out, out, st],
        out_shape=[jax.ShapeDtypeStruct((T, DNW), F32)] * 3 + [jax.ShapeDtypeStruct((T, 128), F32)],
        name=name, compiler_params=_cp())(qkv, qkv, qkv, bg, tinv, *cts)


def _step_math(s, u, w, qg, kdec, attn, decb, sub):
    vnew = u - _bdot_nn(w, s)
    z = jnp.zeros((CH, 128), F32)
    vfull = jnp.concatenate([vnew, z] if sub == 0 else [z, vnew], axis=0)
    o = _bdot_nn(qg, s) + _bdot_nn(attn, vfull)
    dec = jnp.sum(decb, axis=0, keepdims=True) * (1.0 / CH)
    return s * dec + _bdot_tn(kdec, vnew), o


def _dn_scan_fwd(prep, name):
    npair = T // PAIR
    row = pl.BlockSpec((PAIR, DNW), lambda p: (p, 0))

    def body(u_ref, w_ref, qg_ref, kd_ref, at_ref, db_ref, o_ref, ss_ref, s_ref):
        @pl.when(pl.program_id(0) == 0)
        def _():
            s_ref[...] = jnp.zeros_like(s_ref)

        for h in range(DN_H):
            ls = slice(128 * h, 128 * h + 128)
            s = s_ref[h]
            for sub in range(2):
                rs = slice(CH * sub, CH * sub + CH)
                ss_ref[sub, h] = s
                s, o = _step_math(s, u_ref[rs, ls], w_ref[rs, ls], qg_ref[rs, ls], kd_ref[rs, ls],
                                  at_ref[rs, ls], db_ref[rs, ls], sub)
                o_ref[rs, ls] = o
            s_ref[h] = s

    return pl.pallas_call(
        body, grid=(npair,), in_specs=[row] * 6,
        out_specs=[row, pl.BlockSpec((2, DN_H, 128, 128), lambda p: (p, 0, 0, 0))],
        out_shape=[jax.ShapeDtypeStruct((T, DNW), F32), jax.ShapeDtypeStruct((T // CH, DN_H, 128, 128), F32)],
        scratch_shapes=[pltpu.VMEM((DN_H, 128, 128), F32)],
        name=name, compiler_params=_cp())(*prep)


def _dn_scan_bwd(prep, states, do, name):
    npair = T // PAIR
    row = pl.BlockSpec((PAIR, DNW), lambda p: (npair - 1 - p, 0))

    def body(u_ref, w_ref, qg_ref, kd_ref, at_ref, db_ref, ss_ref, do_ref, *rest):
        outs, ds_ref = rest[:6], rest[6]

        @pl.when(pl.program_id(0) == 0)
        def _():
            ds_ref[...] = jnp.zeros_like(ds_ref)

        for h in range(DN_H):
            ls = slice(128 * h, 128 * h + 128)
            ds = ds_ref[h]
            for sub in (1, 0):
                rs = slice(CH * sub, CH * sub + CH)
                args = (ss_ref[sub, h],) + tuple(r[rs, ls] for r in (u_ref, w_ref, qg_ref, kd_ref, at_ref, db_ref))
                _, vjp = jax.vjp(functools.partial(_step_math, sub=sub), *args)
                cts = vjp((ds, do_ref[rs, ls]))
                ds = cts[0]
                for o_ref, val in zip(outs, cts[1:]):
                    o_ref[rs, ls] = val
            ds_ref[h] = ds

    return pl.pallas_call(
        body, grid=(npair,),
        in_specs=[row] * 6 + [pl.BlockSpec((2, DN_H, 128, 128), lambda p: (npair - 1 - p, 0, 0, 0)), row],
        out_specs=[row] * 6, out_shape=[jax.ShapeDtypeStruct((T, DNW), F32)] * 6,
        scratch_shapes=[pltpu.VMEM((DN_H, 128, 128), F32)],
        name=name, compiler_params=_cp())(*prep, states, do)


def _post_math(o, gate, wrow):
    return o * lax.rsqrt(jnp.mean(o * o, axis=-1, keepdims=True) + NORM_EPS) * wrow * _silu(gate)


def _dn_post_fwd(o, z_dn, dn_norm, name):
    tm = 512
    row = pl.BlockSpec((tm, DNW), lambda i: (i, 0))

    def body(o_ref, g_ref, w_ref, y_ref):
        for h in range(DN_H):
            ls = slice(128 * h, 128 * h + 128)
            y_ref[:, ls] = _post_math(o_ref[:, ls], g_ref[:, ls], w_ref[...]).astype(BF)

    return pl.pallas_call(
        body, grid=(T // tm,),
        in_specs=[row, pl.BlockSpec((tm, DNW), lambda i: (i, 3)), pl.BlockSpec((1, 128), lambda i: (0, 0))],
        out_specs=row, out_shape=jax.ShapeDtypeStruct((T, DNW), BF),
        name=name, compiler_params=_cp())(o, z_dn, dn_norm)


def _dn_post_bwd(o, z_dn, dn_norm, dy, name):
    tm = 512
    row = pl.BlockSpec((tm, DNW), lambda i: (i, 0))
    one = pl.BlockSpec((1, 128), lambda i: (0, 0))

    def body(o_ref, g_ref, w_ref, dy_ref, do_ref, dg_ref, dw_ref):
        i = pl.program_id(0)
        dw = jnp.zeros((1, 128), F32)
        for h in range(DN_H):
            ls = slice(128 * h, 128 * h + 128)
            _, vjp = jax.vjp(_post_math, o_ref[:, ls], g_ref[:, ls], w_ref[...])
            do, dg, dwh = vjp(dy_ref[:, ls].astype(F32))
            do_ref[:, ls] = do
            dg_ref[:, ls] = dg
            dw = dw + dwh

        @pl.when(i == 0)
        def _():
            dw_ref[...] = dw

        @pl.when(i > 0)
        def _():
            dw_ref[...] += dw

    return pl.pallas_call(
        body, grid=(T // tm,),
        in_specs=[row, pl.BlockSpec((tm, DNW), lambda i: (i, 3)), one, pl.BlockSpec((tm, DNW), lambda i: (i, 1))],
        out_specs=[row, row, one],
        out_shape=[jax.ShapeDtypeStruct((T, DNW), F32), jax.ShapeDtypeStruct((T, DNW), F32),
                   jax.ShapeDtypeStruct((1, 128), F32)],
        name=name, compiler_params=_cp())(o, z_dn, dn_norm, dy)


def _dz_dn_assemble(draw, dgate, dbd, name):
    tm = 512

    def body(a_ref, b_ref, c_ref, o_ref):
        o_ref[:, :3 * DNW] = a_ref[...].astype(BF)
        o_ref[:, 3 * DNW:4 * DNW] = b_ref[...].astype(BF)
        o_ref[:, 4 * DNW:4 * DNW + 128] = c_ref[...].astype(BF)
        o_ref[:, 4 * DNW + 128:] = jnp.zeros((tm, 128), BF)

    return pl.pallas_call(
        body, grid=(T // tm,),
        in_specs=[pl.BlockSpec((tm, 3 * DNW), lambda i: (i, 0)), pl.BlockSpec((tm, DNW), lambda i: (i, 0)),
                  pl.BlockSpec((tm, 128), lambda i: (i, 0))],
        out_specs=pl.BlockSpec((tm, ZD), lambda i: (i, 0)),
        out_shape=jax.ShapeDtypeStruct((T, ZD), BF), name=name, compiler_params=_cp())(draw, dgate, dbd)


HBM = pl.BlockSpec(memory_space=pltpu.HBM)
SEM = pl.BlockSpec(memory_space=pltpu.SEMAPHORE)
EFFECT = pltpu.SideEffectType.DATAFLOW_SIDE_EFFECTING
N_PEER = N_DEV - 1


def _peers(x, y, c):
    return [(k, (x ^ (k >> 2), y ^ ((k >> 1) & 1), c ^ (k & 1))) for k in (1, 2, 4, 3, 5, 6, 7)]


def _exchange_copy(ins, lands, ssems, rsems, scatter, t, k, pos, me):
    px, py, pc = pos
    src = ins[t].at[4 * px + 2 * py + pc] if scatter else ins[t]
    return pltpu.make_async_remote_copy(
        src_ref=src, dst_ref=lands[t].at[me], send_sem=ssems[t].at[k - 1], recv_sem=rsems[t].at[k - 1],
        device_id=pos, device_id_type=MESH_ID)


def _xstart(bufs, scatter, name):
    nt = len(bufs)
    lands = [lax.empty((N_DEV,) + tuple(b.shape[1:] if scatter else b.shape), b.dtype) for b in bufs]

    def body(*refs):
        ins, lnd = refs[:nt], refs[nt:2 * nt]
        ssems, rsems = refs[2 * nt:3 * nt], refs[3 * nt:4 * nt]
        token = refs[-1]
        x, y, c = lax.axis_index("x"), lax.axis_index("y"), lax.axis_index("c")
        me = 4 * x + 2 * y + c
        for t in range(nt):
            for k, pos in _peers(x, y, c):
                _exchange_copy(ins, lnd, ssems, rsems, scatter, t, k, pos, me).start()
        token[...] = jnp.zeros_like(token)

    both = list(bufs) + lands
    res = pl.pallas_call(
        body, name=name,
        out_shape=[pltpu.SemaphoreType.DMA((N_PEER,))] * (2 * nt)
        + [pltpu.HBM(b.shape, b.dtype) for b in both] + [jax.ShapeDtypeStruct((8, 128), F32)],
        in_specs=[HBM] * (2 * nt),
        out_specs=[SEM] * (2 * nt) + [HBM] * (2 * nt) + [pl.BlockSpec(memory_space=pltpu.VMEM)],
        input_output_aliases={i: 2 * nt + i for i in range(2 * nt)},
        compiler_params=pltpu.CompilerParams(has_side_effects=EFFECT),
    )(*[pltpu.with_memory_space_constraint(b, pltpu.HBM) for b in both])
    return res[:nt], res[nt:2 * nt], res[2 * nt:3 * nt], res[3 * nt:4 * nt], res[-1][0, 0]


def _xwait(ssems, rsems, thrus, lands, scatter, after, name):
    nt = len(lands)

    def body(*refs):
        ins, lnd = refs[:nt], refs[nt:2 * nt]
        ss, rs = refs[2 * nt:3 * nt], refs[3 * nt:4 * nt]
        x, y, c = lax.axis_index("x"), lax.axis_index("y"), lax.axis_index("c")
        me = 4 * x + 2 * y + c
        for t in range(nt):
            for k, pos in _peers(x, y, c):
                cp = _exchange_copy(ins, lnd, ss, rs, scatter, t, k, pos, me)
                cp.wait_send()
                cp.wait_recv()

    both = list(thrus) + list(lands)
    res = pl.pallas_call(
        body, name=name, out_shape=[pltpu.HBM(b.shape, b.dtype) for b in both],
        in_specs=[HBM] * (2 * nt) + [SEM] * (2 * nt) + [ANY], out_specs=[HBM] * (2 * nt),
        input_output_aliases={i: i for i in range(2 * nt)},
        compiler_params=pltpu.CompilerParams(has_side_effects=EFFECT),
    )(*both, *ssems, *rsems, after)
    return res[:nt], res[nt:]


def _sum_slabs(recv, name):
    n, r, c = recv.shape
    tr = r // 2

    def body(r_ref, o_ref):
        g = r_ref[0].astype(F32)
        for s in range(1, n):
            g = g + r_ref[s].astype(F32)
        o_ref[...] = g

    return pl.pallas_call(
        body, grid=(r // tr,), in_specs=[pl.BlockSpec((n, tr, c), lambda i: (0, i, 0))],
        out_specs=pl.BlockSpec((tr, c), lambda i: (i, 0)),
        out_shape=jax.ShapeDtypeStruct((r, c), F32), name=name, compiler_params=_cp())(recv)


def _adam(recv, w, m, v, tr, name):
    _, r, c = w.shape
    n_part = recv.shape[0]
    c1 = np.float32(1.0 - ADAM_B1 ** ADAM_STEP)
    c2 = np.float32(1.0 - ADAM_B2 ** ADAM_STEP)

    def body(r_ref, w_ref, m_ref, v_ref, g_ref, d_ref, mo_ref, vo_ref):
        g = r_ref[0].astype(F32)
        for s in range(1, n_part):
            g = g + r_ref[s].astype(F32)
        mn = ADAM_B1 * m_ref[0] + (1.0 - ADAM_B1) * g
        vn = ADAM_B2 * v_ref[0] + (1.0 - ADAM_B2) * (g * g)
        g_ref[0] = g
        mo_ref[0] = mn
        vo_ref[0] = vn
        d_ref[0] = -ADAM_LR * ((mn / c1) / (jnp.sqrt(vn / c2) + ADAM_EPS) + ADAM_WD * w_ref[0])

    one = pl.BlockSpec((1, tr, c), lambda i: (0, i, 0))
    return pl.pallas_call(
        body, grid=(r // tr,), in_specs=[pl.BlockSpec((n_part, tr, c), lambda i: (0, i, 0)), one, one, one],
        out_specs=[one] * 4, out_shape=[jax.ShapeDtypeStruct((1, r, c), F32)] * 4,
        name=name, compiler_params=_cp())(recv, w, m, v)


def _local_step(x, target, sp, need, emit):
    g = {}
    x0, y1, saved1 = _ffn_fwd(x, None, 0.0, sp["norm_ffn1"], lambda kind, a: need("w" + kind + "1", a), "ffn1")
    x1, h2 = _rms_fwd(x0, y1, 0.5, sp["norm_mix"], "mix_norm")
    win_a, win_d = need("win_a", h2), need("win_d", h2)
    conv_w8, wout = need("conv_w8", h2), need("wout", h2)
    z_at = _mm_nn(h2, win_a, F32, 1024, 768, "mix_in_attn")
    z_dn = _mm_nn(h2, win_d, F32, 1024, 768, "mix_in_dn")
    attn_f, attn_b, lse = _attn_fwd(z_at, "attn_fwd")

    conv, qkvn, bg = _dn_pre_fwd(z_dn, conv_w8, sp["alog_row"], sp["dtb_row"], "dn_pre")
    *prep, tinv = _dn_prep_fwd(qkvn, bg, "dn_prep")
    o_dn, states = _dn_scan_fwd(prep, "dn_scan")
    dn_b = _dn_post_fwd(o_dn, z_dn, sp["dn_norm"], "dn_post")

    mix = jnp.concatenate([attn_b, dn_b], axis=1)
    y2 = _mm_nn(mix, wout, F32, 1024, 1024, "mix_out")
    x2, y3, saved2 = _ffn_fwd(x1, y2, 1.0, sp["norm_ffn2"], lambda kind, a: need("w" + kind + "2", a), "ffn2")
    x3, _ = _rms_fwd(x2, y3, 0.5, sp["norm_final"], "final_add")

    dx3, dys3, g["norm_final"], loss8 = _loss_bwd(x3, sp["norm_final"], target, "loss")
    dx2, dx2b, g["norm_ffn2"] = _ffn_bwd(
        x2, sp["norm_ffn2"], saved2, dx3, dys3, 1.0,
        lambda kind, dw: emit(kind + "2", {"w" + kind + "2": dw}), "ffn2b")

    zero = emit("wout", {"wout": _mm_tn(mix, dx2b, BF, 512, 1024, "mix_out_dw")})
    dmix = _mm_nt(dx2b, wout, F32, 1024, 1024, "mix_out_dx")

    dz_at = jnp.concatenate(_attn_bwd(z_at, dmix, attn_f, lse, "attn_bwd"), axis=1)

    do_dn, dgate, g["dn_norm"] = _dn_post_bwd(o_dn, z_dn, sp["dn_norm"] + zero, dmix, "dn_post_b")
    cts = _dn_scan_bwd(prep, states, do_dn, "dn_scan_b")
    dqn, dkn, dvn, dbg = _dn_prep_bwd(qkvn, bg, tinv, cts, "dn_prep_b")
    dconv, dbd, g["alog_row"], g["dtb_row"] = _dn_pre_bwd(
        conv, z_dn, sp["alog_row"], sp["dtb_row"], dqn, dkn, dvn, dbg, "dn_pre_b")
    draw, dconv_w8 = _dn_conv_bwd(dconv, z_dn, conv_w8, "dn_conv_b")
    dz_dn = _dz_dn_assemble(draw, dgate, dbd, "dn_dz")

    zero = emit("win", {"win_a": _mm_tn(h2, dz_at, BF, 512, 768, "mix_in_dw_a"),
                        "win_d": _mm_tn(h2, dz_dn, BF, 512, 768, "mix_in_dw_d"), "conv_w8": dconv_w8})
    dh2 = _mm_nt(dz_at, win_a, F32, 1024, 1024, "mix_in_dx_a")
    dh2d = _mm_nt(dz_dn, win_d, F32, 1024, 1024, "mix_in_dx_d")
    (dx1, dys1), _, g["norm_mix"] = _rms_bwd2(x1, sp["norm_mix"] + zero, dh2, dh2d, dx2, "mix_dnorm")
    dx0, _, g["norm_ffn1"] = _ffn_bwd(
        x0, sp["norm_ffn1"], saved1, dx1, dys1, 1.0,
        lambda kind, dw: emit(kind + "1", {"w" + kind + "1": dw}), "ffn1b")
    return loss8[0, 0], dx0, g


def _rms_bwd2(x, gain, dh_a, dh_b, dres, name):
    tm = 512
    row = pl.BlockSpec((tm, D), lambda i: (i, 0))
    gspec = pl.BlockSpec((1, D), lambda i: (0, 0))

    def body(x_ref, g_ref, da_ref, db_ref, dres_ref, dx_ref, dxs_ref, dg_ref):
        i = pl.program_id(0)
        xv = x_ref[...]
        r = lax.rsqrt(jnp.mean(xv * xv, axis=-1, keepdims=True) + NORM_EPS)
        xh = xv * r
        dhv = da_ref[...] + db_ref[...]
        part = jnp.sum(dhv * xh, axis=0, keepdims=True)

        @pl.when(i == 0)
        def _():
            dg_ref[...] = part

        @pl.when(i > 0)
        def _():
            dg_ref[...] += part

        dxh = dhv * g_ref[...]
        dx = r * (dxh - xh * jnp.mean(dxh * xh, axis=-1, keepdims=True)) + dres_ref[...]
        dx_ref[...] = dx
        dxs_ref[...] = (0.5 * dx).astype(BF)

    dx, dxs, dg = pl.pallas_call(
        body, grid=(T // tm,), in_specs=[row, gspec, row, row, row], out_specs=[row, row, gspec],
        out_shape=[jax.ShapeDtypeStruct((T, D), F32), jax.ShapeDtypeStruct((T, D), BF),
                   jax.ShapeDtypeStruct((1, D), F32)],
        name=name, compiler_params=_cp())(x, gain, dh_a, dh_b, dres)
    return (dx, dxs), None, dg


def _cols_from_shards(gathered):
    n, r, c = gathered.shape
    return jnp.transpose(gathered, (1, 0, 2)).reshape(r, n * c)


def _shards_from_cols(full, dtype):
    r, nc = full.shape
    return jnp.transpose(full.reshape(r, N_DEV, nc // N_DEV), (1, 0, 2)).astype(dtype)


def _lane_row(vec4):
    return jnp.zeros((1, 128), F32).at[:, DN_H:2 * DN_H].set(vec4.astype(F32))


WEIGHT_SOURCES = {"wg1": "gate1", "wu1": "up1", "wd1": "down1", "win_a": "w_in", "win_d": "w_in",
                  "conv_w8": "conv_w", "wout": "w_out", "wg2": "gate2", "wu2": "up2", "wd2": "down2"}
TRANSPOSED = ("gate1", "up1", "gate2", "up2")


def _build_weights(name, gath):
    if name in ("wg1", "wu1", "wd1", "wg2", "wu2", "wd2"):
        return {name: gath[WEIGHT_SOURCES[name]].reshape(F, D)}
    if name in ("win_a", "win_d"):
        w_in = _cols_from_shards(gath["w_in"])
        c0 = 3 * AW + 3 * DNW
        wp = jnp.concatenate([w_in[:, :c0], w_in[:, c0 + 2 * DN_H:], w_in[:, c0:c0 + 2 * DN_H],
                              jnp.zeros((D, ZP - IN_COLS), w_in.dtype)], axis=1)
        return {"win_a": wp[:, :ZA], "win_d": wp[:, ZA:]}
    if name == "wout":
        return {name: gath["w_out"].reshape(D, D)}
    conv = _cols_from_shards(gath["conv_w"])
    return {"conv_w8": jnp.concatenate([conv, jnp.zeros((4, 3 * DNW), F32)], axis=0)}


def _small_params(norm_ffn1, norm_mix, norm_ffn2, norm_final, a_log, dt_bias, dn_norm):
    return {"norm_ffn1": norm_ffn1, "norm_mix": norm_mix, "norm_ffn2": norm_ffn2,
            "norm_final": norm_final.reshape(1, D), "alog_row": _lane_row(a_log), "dtb_row": _lane_row(dt_bias),
            "dn_norm": dn_norm}


def _grad_slabs(group, g):
    if group[0] in "gud":
        return {WEIGHT_SOURCES["w" + group]: g["w" + group].reshape(N_DEV, F // N_DEV, D)}
    if group == "wout":
        return {"w_out": g["wout"].reshape(N_DEV, D // N_DEV, D)}
    gp = jnp.concatenate([g["win_a"], g["win_d"]], axis=1)
    c0 = 3 * AW + 3 * DNW
    g_in = jnp.concatenate([gp[:, :c0], gp[:, c0 + DNW:c0 + DNW + 2 * DN_H], gp[:, c0:c0 + DNW]], axis=1)
    return {"w_in": _shards_from_cols(g_in, BF), "conv_w": _shards_from_cols(g["conv_w8"][:4], F32)}


SMALL_ROWS = 40


def _small_pack(norm_ffn1, norm_mix, norm_ffn2, norm_final, dn_norm, alog_row, dtb_row):
    rows = [a.reshape(8, 128) for a in (norm_ffn1, norm_mix, norm_ffn2, norm_final)]
    rows += [dn_norm.reshape(1, 128), alog_row, dtb_row, jnp.zeros((SMALL_ROWS - 35, 128), F32)]
    return jnp.concatenate(rows, axis=0)


def _small_unpack(pk):
    pk = pk[0]
    return (pk[0:8].reshape(1, D), pk[8:16].reshape(1, D), pk[16:24].reshape(1, D), pk[24:32].reshape(D),
            pk[32:33], pk[33:34, DN_H:2 * DN_H], pk[34:35, DN_H:2 * DN_H])


ADAM_TILE = {"gate1": 256, "up1": 256, "down1": 176, "gate2": 256, "up2": 256, "down2": 176,
             "w_in": 256, "w_out": 128, "conv_w": 4}
BIG = ("gate1", "up1", "down1", "w_in", "w_out", "gate2", "up2", "down2", "conv_w")


def kernel(x, norm_ffn1, ffn1_gate, ffn1_up, ffn1_down, norm_mix, w_in, conv_w, a_log, dt_bias, dn_norm, w_out, norm_ffn2, ffn2_gate, ffn2_up, ffn2_down, norm_final, loss_target, m_norm_ffn1, m_ffn1_gate, m_ffn1_up, m_ffn1_down, m_norm_mix, m_w_in, m_conv_w, m_a_log, m_dt_bias, m_dn_norm, m_w_out, m_norm_ffn2, m_ffn2_gate, m_ffn2_up, m_ffn2_down, m_norm_final, v_norm_ffn1, v_ffn1_gate, v_ffn1_up, v_ffn1_down, v_norm_mix, v_w_in, v_conv_w, v_a_log, v_dt_bias, v_dn_norm, v_w_out, v_norm_ffn2, v_ffn2_gate, v_ffn2_up, v_ffn2_down, v_norm_final):
    w = {"gate1": ffn1_gate, "up1": ffn1_up, "down1": ffn1_down, "w_in": w_in, "w_out": w_out,
         "gate2": ffn2_gate, "up2": ffn2_up, "down2": ffn2_down, "conv_w": conv_w}
    m = {"gate1": m_ffn1_gate, "up1": m_ffn1_up, "down1": m_ffn1_down, "w_in": m_w_in, "w_out": m_w_out,
         "gate2": m_ffn2_gate, "up2": m_ffn2_up, "down2": m_ffn2_down, "conv_w": m_conv_w}
    v = {"gate1": v_ffn1_gate, "up1": v_ffn1_up, "down1": v_ffn1_down, "w_in": v_w_in, "w_out": v_w_out,
         "gate2": v_ffn2_gate, "up2": v_ffn2_up, "down2": v_ffn2_down, "conv_w": v_conv_w}

    me = 4 * lax.axis_index("x") + 2 * lax.axis_index("y") + lax.axis_index("c")
    own_slot = lambda land, mine: lax.dynamic_update_index_in_dim(land, mine, me, 0)

    ag_order = ("gate1", "up1", "down1", "w_in", "conv_w", "w_out", "gate2", "up2", "down2")
    ag_groups = (("gate1",), ("up1",), ("down1",), ("w_in", "conv_w", "w_out"), ("gate2",), ("up2",), ("down2",))
    pos = {n: i for i, n in enumerate(ag_order)}

    def shard(n):
        if n == "conv_w":
            return w[n][0]
        return (w[n][0].T if n in TRANSPOSED else w[n][0]).astype(BF)

    ss, rs, thru, land, zero = _xstart([shard(n) for n in ag_order], False, "weights_start")
    gath, built = {}, {}

    def need(name, after):
        if name not in built:
            src = WEIGHT_SOURCES[name]
            if src not in gath:
                gi = [i for i, grp in enumerate(ag_groups) if src in grp][0]
                ids = [pos[n] for n in ag_groups[gi]]
                thrus, lands = _xwait([ss[i] for i in ids], [rs[i] for i in ids], [thru[i] for i in ids],
                                      [land[i] for i in ids], False, after, "weights_wait%d" % gi)
                for n, t, l in zip(ag_groups[gi], thrus, lands):
                    gath[n] = own_slot(l, t)
            built.update(_build_weights(name, gath))
        return built[name]

    pending = []

    def emit(group, grads):
        slabs = grads if group == "small" else _grad_slabs(group, grads)
        names = list(slabs)
        started = _xstart([slabs[n] for n in names], True, "grads_start_" + group)
        pending.append((group, names) + started[:4])
        return started[4]

    sp = _small_params(norm_ffn1 + zero, norm_mix, norm_ffn2, norm_final, a_log, dt_bias, dn_norm)
    loss_part, dx, g = _local_step(x[0], loss_target[0], sp, need, emit)
    small = _small_pack(g["norm_ffn1"], g["norm_mix"], g["norm_ffn2"], g["norm_final"], g["dn_norm"],
                        g["alog_row"], g["dtb_row"])
    emit("small", {"small": jnp.broadcast_to(small[None], (N_DEV, SMALL_ROWS, 128))})

    pack = lambda a: _small_pack(*a)[None]
    res, after = {}, dx
    for group, names, gss, grs, gthru, gland in pending:
        thrus, lands = _xwait(gss, grs, gthru, gland, True, after, "grads_wait_" + group)
        for n, t, l in zip(names, thrus, lands):
            recv = own_slot(l, lax.dynamic_index_in_dim(t, me, 0, keepdims=False))
            if n == "small":
                res[n] = _adam(
                    recv,
                    pack((norm_ffn1, norm_mix, norm_ffn2, norm_final, dn_norm, _lane_row(a_log), _lane_row(dt_bias))),
                    pack((m_norm_ffn1, m_norm_mix, m_norm_ffn2, m_norm_final, m_dn_norm, _lane_row(m_a_log),
                          _lane_row(m_dt_bias))),
                    pack((v_norm_ffn1, v_norm_mix, v_norm_ffn2, v_norm_final, v_dn_norm, _lane_row(v_a_log),
                          _lane_row(v_dt_bias))),
                    SMALL_ROWS, "adam_small")
            elif n in TRANSPOSED:
                total = _sum_slabs(recv, "gsum_" + n)
                res[n] = _adam(total.T[None], w[n], m[n], v[n], ADAM_TILE[n], "adam_" + n)
            else:
                res[n] = _adam(recv, w[n], m[n], v[n], ADAM_TILE[n], "adam_" + n)
            after = res[n][0]
    res_s = res["small"]

    loss = lax.psum(loss_part, ("x", "y", "c"))
    outs = [loss, dx[None]]
    for k in range(4):
        n1, nm, n2, nf, dn, al, dt = _small_unpack(res_s[k])
        big = {n: res[n][k] for n in BIG}
        outs += [n1, big["gate1"], big["up1"], big["down1"], nm, big["w_in"], big["conv_w"], al, dt, dn,
                 big["w_out"], n2, big["gate2"], big["up2"], big["down2"], nf]
    return tuple(outs)
```

```python
import functools

import numpy as np
import jax
import jax.numpy as jnp
from jax import lax
from jax.experimental import pallas as pl
from jax.experimental.pallas import tpu as pltpu

T = 4096
D = 1024
F = 2816
N_DEV = 8
A_HEADS = 8
A_HD = 64
AW = A_HEADS * A_HD
DN_H = 4
DN_HD = 128
DNW = DN_H * DN_HD
CH = 64
PAIR = 2 * CH
ZA = 3 * AW
ZD = 3 * DNW + DNW + 256
ZP = ZA + ZD
BD_BLK = (3 * DNW + DNW) // 128
IN_COLS = 3592
PATTERNS = ((128, 1), (512, 4), (2048, 16))
NORM_EPS = 1e-6
L2_EPS = 1e-6
ADAM_LR, ADAM_B1, ADAM_B2, ADAM_EPS, ADAM_WD, ADAM_STEP = 0.001, 0.9, 0.999, 1e-08, 0.01, 10
VMEM_LIMIT = 56 * 1024 * 1024
NEG = -1e30

BF = jnp.bfloat16
F32 = jnp.float32
NN = (((1,), (0,)), ((), ()))
NT = (((1,), (1,)), ((), ()))
TN = (((0,), (0,)), ((), ()))
HI = lax.Precision.HIGHEST
MESH_ID = pl.DeviceIdType.MESH
ANY = pl.BlockSpec(memory_space=pl.ANY)


def _cp():
    return pltpu.CompilerParams(vmem_limit_bytes=VMEM_LIMIT)


def _dg(a, b, dims):
    return lax.dot_general(a, b, dims, preferred_element_type=F32)


def _hdot(a, b):
    return lax.dot_general(a, b, NN, precision=HI, preferred_element_type=F32)


def _make_bdot(dims, da_dims, da_swap, db_dims, db_swap):
    @jax.custom_vjp
    def f(a, b):
        return _dg(a.astype(BF), b.astype(BF), dims)

    def fwd(a, b):
        return f(a, b), (a, b)

    def bwd(res, g):
        a, b = res
        gb, ab, bb = g.astype(BF), a.astype(BF), b.astype(BF)
        da = _dg(bb, gb, da_dims) if da_swap else _dg(gb, bb, da_dims)
        db = _dg(gb, ab, db_dims) if db_swap else _dg(ab, gb, db_dims)
        return da, db

    f.defvjp(fwd, bwd)
    return f


_bdot_nn = _make_bdot(NN, NT, False, TN, False)
_bdot_nt = _make_bdot(NT, NN, False, TN, True)
_bdot_tn = _make_bdot(TN, NT, True, NN, False)


def _iota(shape, dim):
    return lax.broadcasted_iota(jnp.int32, shape, dim)


def _col(x, idx):
    return jnp.sum(jnp.where(_iota(x.shape, 1) == idx, x, 0.0), axis=1, keepdims=True)


def _mm_nn(a, b, out_dtype, tm, tn, name):
    m, k = a.shape
    n = b.shape[1]

    def body(a_ref, b_ref, o_ref):
        o_ref[...] = _dg(a_ref[...], b_ref[...], NN).astype(out_dtype)

    return pl.pallas_call(
        body, grid=(m // tm, n // tn),
        in_specs=[pl.BlockSpec((tm, k), lambda i, j: (i, 0)), pl.BlockSpec((k, tn), lambda i, j: (0, j))],
        out_specs=pl.BlockSpec((tm, tn), lambda i, j: (i, j)),
        out_shape=jax.ShapeDtypeStruct((m, n), out_dtype), name=name, compiler_params=_cp())(a, b)


def _mm_nt(a, b, out_dtype, tm, tb, name):
    m, c = a.shape
    kb = b.shape[0]

    def body(a_ref, b_ref, o_ref):
        o_ref[...] = _dg(a_ref[...], b_ref[...], NT).astype(out_dtype)

    return pl.pallas_call(
        body, grid=(m // tm, kb // tb),
        in_specs=[pl.BlockSpec((tm, c), lambda i, j: (i, 0)), pl.BlockSpec((tb, c), lambda i, j: (j, 0))],
        out_specs=pl.BlockSpec((tm, tb), lambda i, j: (i, j)),
        out_shape=jax.ShapeDtypeStruct((m, kb), out_dtype), name=name, compiler_params=_cp())(a, b)


def _mm_tn(a, b, out_dtype, ta, tb, name):
    m, ka = a.shape
    nb = b.shape[1]

    def body(a_ref, b_ref, o_ref):
        o_ref[...] = _dg(a_ref[...], b_ref[...], TN).astype(out_dtype)

    return pl.pallas_call(
        body, grid=(ka // ta, nb // tb),
        in_specs=[pl.BlockSpec((m, ta), lambda i, j: (0, i)), pl.BlockSpec((m, tb), lambda i, j: (0, j))],
        out_specs=pl.BlockSpec((ta, tb), lambda i, j: (i, j)),
        out_shape=jax.ShapeDtypeStruct((ka, nb), out_dtype), name=name, compiler_params=_cp())(a, b)


def _mm_nn2(a1, a2, b1, b2, tm, tn, name):
    m, k = a1.shape
    n = b1.shape[1]

    def body(a1_ref, a2_ref, b1_ref, b2_ref, o_ref):
        o_ref[...] = _dg(a1_ref[...], b1_ref[...], NN) + _dg(a2_ref[...], b2_ref[...], NN)

    arow = pl.BlockSpec((tm, k), lambda i, j: (i, 0))
    bcol = pl.BlockSpec((k, tn), lambda i, j: (0, j))
    return pl.pallas_call(
        body, grid=(m // tm, n // tn), in_specs=[arow, arow, bcol, bcol],
        out_specs=pl.BlockSpec((tm, tn), lambda i, j: (i, j)),
        out_shape=jax.ShapeDtypeStruct((m, n), F32), name=name, compiler_params=_cp())(a1, a2, b1, b2)


def _mm_nt_swiglu(h, wu_t, gate, tm, tb, name):
    m, c = h.shape
    kb = wu_t.shape[0]

    def body(h_ref, w_ref, g_ref, u_ref, a_ref):
        u = _dg(h_ref[...], w_ref[...], NT)
        g = g_ref[...].astype(F32)
        u_ref[...] = u.astype(BF)
        a_ref[...] = (g * jax.nn.sigmoid(g) * u).astype(BF)

    tile = pl.BlockSpec((tm, tb), lambda i, j: (i, j))
    return pl.pallas_call(
        body, grid=(m // tm, kb // tb),
        in_specs=[pl.BlockSpec((tm, c), lambda i, j: (i, 0)), pl.BlockSpec((tb, c), lambda i, j: (j, 0)), tile],
        out_specs=[tile, tile], out_shape=[jax.ShapeDtypeStruct((m, kb), BF)] * 2,
        name=name, compiler_params=_cp())(h, wu_t, gate)


def _mm_nt_dswiglu(dys, wd, gate, up, tm, tb, name):
    m, c = dys.shape
    kb = wd.shape[0]

    def body(d_ref, w_ref, g_ref, u_ref, dg_ref, du_ref):
        da = _dg(d_ref[...], w_ref[...], NT)
        g = g_ref[...].astype(F32)
        u = u_ref[...].astype(F32)
        s = jax.nn.sigmoid(g)
        dg_ref[...] = (da * u * (s * (1.0 + g * (1.0 - s)))).astype(BF)
        du_ref[...] = (da * (g * s)).astype(BF)

    tile = pl.BlockSpec((tm, tb), lambda i, j: (i, j))
    return pl.pallas_call(
        body, grid=(m // tm, kb // tb),
        in_specs=[pl.BlockSpec((tm, c), lambda i, j: (i, 0)), pl.BlockSpec((tb, c), lambda i, j: (j, 0)),
                  tile, tile],
        out_specs=[tile, tile], out_shape=[jax.ShapeDtypeStruct((m, kb), BF)] * 2,
        name=name, compiler_params=_cp())(dys, wd, gate, up)


def _rms_fwd(x, y, alpha, gain, name):
    tm = 512
    has_y = y is not None
    row = pl.BlockSpec((tm, D), lambda i: (i, 0))
    gspec = pl.BlockSpec((1, D), lambda i: (0, 0))

    def body(*refs):
        if has_y:
            x_ref, y_ref, g_ref, xo_ref, h_ref = refs
            xv = x_ref[...] + alpha * y_ref[...]
            xo_ref[...] = xv
        else:
            x_ref, g_ref, h_ref = refs
            xv = x_ref[...]
        r = lax.rsqrt(jnp.mean(xv * xv, axis=-1, keepdims=True) + NORM_EPS)
        h_ref[...] = (xv * r * g_ref[...]).astype(BF)

    if has_y:
        return pl.pallas_call(
            body, grid=(T // tm,), in_specs=[row, row, gspec], out_specs=[row, row],
            out_shape=[jax.ShapeDtypeStruct((T, D), F32), jax.ShapeDtypeStruct((T, D), BF)],
            name=name, compiler_params=_cp())(x, y, gain)
    h = pl.pallas_call(
        body, grid=(T // tm,), in_specs=[row, gspec], out_specs=row,
        out_shape=jax.ShapeDtypeStruct((T, D), BF), name=name, compiler_params=_cp())(x, gain)
    return x, h


def _rms_bwd(x, gain, dh, dres, alpha_out, name):
    tm = 512
    row = pl.BlockSpec((tm, D), lambda i: (i, 0))
    gspec = pl.BlockSpec((1, D), lambda i: (0, 0))

    def body(x_ref, g_ref, dh_ref, dres_ref, dx_ref, dxs_ref, dg_ref):
        i = pl.program_id(0)
        xv = x_ref[...]
        r = lax.rsqrt(jnp.mean(xv * xv, axis=-1, keepdims=True) + NORM_EPS)
        xh = xv * r
        dhv = dh_ref[...].astype(F32)
        part = jnp.sum(dhv * xh, axis=0, keepdims=True)

        @pl.when(i == 0)
        def _():
            dg_ref[...] = part

        @pl.when(i > 0)
        def _():
            dg_ref[...] += part

        dxh = dhv * g_ref[...]
        dx = r * (dxh - xh * jnp.mean(dxh * xh, axis=-1, keepdims=True)) + dres_ref[...]
        dx_ref[...] = dx
        dxs_ref[...] = (alpha_out * dx).astype(BF)

    return pl.pallas_call(
        body, grid=(T // tm,), in_specs=[row, gspec, row, row], out_specs=[row, row, gspec],
        out_shape=[jax.ShapeDtypeStruct((T, D), F32), jax.ShapeDtypeStruct((T, D), BF),
                   jax.ShapeDtypeStruct((1, D), F32)],
        name=name, compiler_params=_cp())(x, gain, dh, dres)


def _loss_bwd(x_prev, y, gain, target, name):
    tm = 512
    row = pl.BlockSpec((tm, D), lambda i: (i, 0))
    gspec = pl.BlockSpec((1, D), lambda i: (0, 0))
    lspec = pl.BlockSpec((8, 128), lambda i: (0, 0))

    def body(x_ref, y_ref, g_ref, t_ref, dx_ref, dxs_ref, dg_ref, loss_ref):
        i = pl.program_id(0)
        xv = x_ref[...] + 0.5 * y_ref[...]
        r = lax.rsqrt(jnp.mean(xv * xv, axis=-1, keepdims=True) + NORM_EPS)
        xh = xv * r
        diff = xh * g_ref[...] - t_ref[...]
        lpart = 0.5 * jnp.sum(jnp.mean(diff * diff, axis=-1, keepdims=True), axis=0, keepdims=True)
        dy = diff * (1.0 / D)
        part = jnp.sum(dy * xh, axis=0, keepdims=True)

        @pl.when(i == 0)
        def _():
            dg_ref[...] = part
            loss_ref[...] = jnp.broadcast_to(lpart, (8, 128))

        @pl.when(i > 0)
        def _():
            dg_ref[...] += part
            loss_ref[...] += jnp.broadcast_to(lpart, (8, 128))

        dxh = dy * g_ref[...]
        dx = r * (dxh - xh * jnp.mean(dxh * xh, axis=-1, keepdims=True))
        dx_ref[...] = dx
        dxs_ref[...] = (0.5 * dx).astype(BF)

    return pl.pallas_call(
        body, grid=(T // tm,), in_specs=[row, row, gspec, row], out_specs=[row, row, gspec, lspec],
        out_shape=[jax.ShapeDtypeStruct((T, D), F32), jax.ShapeDtypeStruct((T, D), BF),
                   jax.ShapeDtypeStruct((1, D), F32), jax.ShapeDtypeStruct((8, 128), F32)],
        name=name, compiler_params=_cp())(x_prev, y, gain, target)


def _ffn_fwd(x_prev, y_prev, alpha, gain, need, tag):
    x_in, h = _rms_fwd(x_prev, y_prev, alpha, gain, tag + "_norm")
    wg = need("g", h)
    gate = _mm_nt(h, wg, BF, 1024, 1408, tag + "_gate")
    wu = need("u", gate)
    up, act = _mm_nt_swiglu(h, wu, gate, 1024, 1408, tag + "_up_act")
    wd = need("d", up)
    y = _mm_nn(act, wd, F32, 1024, 1024, tag + "_down")
    return x_in, y, (h, gate, up, act, wg, wu, wd)


def _ffn_bwd(x_in, gain, saved, dxo, dys, alpha_out, emit, tag):
    h, gate, up, act, wg, wu, wd = saved
    zero = emit("d", _mm_tn(act, dys, BF, 256, 1024, tag + "_dwd"))
    dgate, dup = _mm_nt_dswiglu(dys, wd, gate, up, 1024, 1408, tag + "_dact")
    zero = zero + emit("g", _mm_tn(dgate, h, BF, 256, 1024, tag + "_dwg"))
    zero = zero + emit("u", _mm_tn(dup, h, BF, 256, 1024, tag + "_dwu"))
    dh = _mm_nn2(dgate, dup, wg, wu, 512, 512, tag + "_dh")
    dx, dxs, dgain = _rms_bwd(x_in, gain + zero, dh, dxo, alpha_out, tag + "_dnorm")
    return dx, dxs, dgain


SLAB = 2048
N_SLAB = T // SLAB
N_PAIR = A_HEADS // 2


def _pair_masks():
    lane = _iota((128, 128), 1)
    return lane < A_HD, lane >= A_HD


def _slope_table():
    h = 2 * jnp.arange(N_PAIR)[:, None] + jnp.minimum(jnp.arange(8), 1)[None, :]
    return jnp.broadcast_to((2.0 ** (-(h + 1).astype(F32)))[:, :, None], (N_PAIR, 8, 128))


def _rows(ref, start, d):
    if d == 1:
        return ref[pl.ds(start, 128), :]
    return ref[pl.ds(start, 128, stride=d), :]


def _put_rows(ref, start, d, val):
    if d == 1:
        ref[pl.ds(start, 128), :] = val
    else:
        ref[pl.ds(start, 128, stride=d), :] = val


def _units(d):
    return [(r, b, r + 128 * d * b) for r in range(d) for b in range(SLAB // (128 * d))]


def _band(d, prev_valid):
    qi = _iota((128, 256), 0)
    kj = _iota((128, 256), 1)
    steps = qi + 128 - kj
    valid = (steps >= 0) & (steps <= 128) & (prev_valid | (kj >= 128))
    return valid, (steps * d).astype(F32)


def _attn_fwd(z_at, name):
    def body(sl_ref, q_ref, kc_ref, kp_ref, vc_ref, vp_ref, of_ref, ob_ref, lse_ref, m_s, l_s, a_s):
        n = pl.program_id(1)
        lo, hi = _pair_masks()
        slopes = (sl_ref[0, 0:1, 0:1], sl_ref[0, 1:2, 0:1])

        def unit(d, start, b, first):
            q = _rows(q_ref, start, d).astype(BF)
            if b > 0:
                kprev, vprev, prev_valid = _rows(kc_ref, start - 128 * d, d), _rows(vc_ref, start - 128 * d, d), True
            else:
                pstart = start + SLAB - 128 * d
                kprev, vprev, prev_valid = _rows(kp_ref, pstart, d), _rows(vp_ref, pstart, d), n > 0
            kcat = jnp.concatenate([kprev, _rows(kc_ref, start, d)], axis=0).astype(BF)
            vcat = jnp.concatenate([vprev, _rows(vc_ref, start, d)], axis=0).astype(BF)
            valid, dist = _band(d, prev_valid)
            ms, ls, pvs = [], [], []
            for e in range(2):
                qm = jnp.where(lo if e == 0 else hi, q, jnp.zeros_like(q))
                s = _dg(qm, kcat, NT) * (A_HD ** -0.5) - slopes[e] * dist
                s = jnp.where(valid, s, NEG)
                m = jnp.max(s, axis=1, keepdims=True)
                p = jnp.exp(s - m)
                ms.append(m)
                ls.append(jnp.sum(p, axis=1, keepdims=True))
                pvs.append(_dg(p.astype(BF), vcat, NN))
            m_u = jnp.where(lo, ms[0], ms[1])
            l_u = jnp.where(lo, ls[0], ls[1])
            a_u = jnp.where(lo, pvs[0], pvs[1])
            if first:
                m_n, l_n, a_n = m_u, l_u, a_u
            else:
                m_o = _rows(m_s, start, d)
                m_n = jnp.maximum(m_o, m_u)
                c_o = jnp.exp(m_o - m_n)
                c_u = jnp.exp(m_u - m_n)
                l_n = _rows(l_s, start, d) * c_o + l_u * c_u
                a_n = _rows(a_s, start, d) * c_o + a_u * c_u
            _put_rows(m_s, start, d, m_n)
            _put_rows(l_s, start, d, l_n)
            _put_rows(a_s, start, d, a_n)

        for pi, (_, d) in enumerate(PATTERNS):
            for r, b, start in _units(d):
                unit(d, start, b, pi == 0)
        l = l_s[...]
        out = a_s[...] / l
        of_ref[...] = out
        ob_ref[...] = out.astype(BF)
        lse_ref[...] = m_s[...] + jnp.log(l)

    cur = lambda c: pl.BlockSpec((SLAB, 128), lambda j, n: (n, c * N_PAIR + j))
    prv = lambda c: pl.BlockSpec((SLAB, 128), lambda j, n: (jnp.maximum(n - 1, 0), c * N_PAIR + j))
    out = pl.BlockSpec((SLAB, 128), lambda j, n: (n, j))
    return pl.pallas_call(
        body, grid=(N_PAIR, N_SLAB),
        in_specs=[pl.BlockSpec((1, 8, 128), lambda j, n: (j, 0, 0)), cur(0), cur(1), prv(1), cur(2), prv(2)],
        out_specs=[out, out, out],
        out_shape=[jax.ShapeDtypeStruct((T, AW), F32), jax.ShapeDtypeStruct((T, AW), BF),
                   jax.ShapeDtypeStruct((T, AW), F32)],
        scratch_shapes=[pltpu.VMEM((SLAB, 128), F32)] * 3,
        name=name, compiler_params=_cp())(_slope_table(), z_at, z_at, z_at, z_at, z_at)


def _attn_bwd(z_at, dout, out, lse, name):
    def body(sl_ref, q_ref, kc_ref, kp_ref, vc_ref, vp_ref, do_ref, o_ref, lse_ref, dq_ref, dk_ref, dv_ref,
             dq_s, dk_s, dv_s, ck_s, cv_s):
        step = pl.program_id(1)
        n = N_SLAB - 1 - step
        lo, hi = _pair_masks()
        slopes = (sl_ref[0, 0:1, 0:1], sl_ref[0, 1:2, 0:1])

        @pl.when(step == 0)
        def _():
            ck_s[...] = jnp.zeros_like(ck_s)
            cv_s[...] = jnp.zeros_like(cv_s)

        dk_s[...] = ck_s[...]
        dv_s[...] = cv_s[...]
        ck_s[...] = jnp.zeros_like(ck_s)
        cv_s[...] = jnp.zeros_like(cv_s)

        def add_rows(ref, start, d, val):
            _put_rows(ref, start, d, _rows(ref, start, d) + val)

        def unit(d, start, b, first):
            q = _rows(q_ref, start, d).astype(BF)
            do_f = _rows(do_ref, start, d)
            do = do_f.astype(BF)
            prod = do_f * _rows(o_ref, start, d)
            lse_u = _rows(lse_ref, start, d)
            if b > 0:
                kprev, vprev, prev_valid = _rows(kc_ref, start - 128 * d, d), _rows(vc_ref, start - 128 * d, d), True
            else:
                pstart = start + SLAB - 128 * d
                kprev, vprev, prev_valid = _rows(kp_ref, pstart, d), _rows(vp_ref, pstart, d), n > 0
            kcat = jnp.concatenate([kprev, _rows(kc_ref, start, d)], axis=0).astype(BF)
            vcat = jnp.concatenate([vprev, _rows(vc_ref, start, d)], axis=0).astype(BF)
            valid, dist = _band(d, prev_valid)
            dqs = []
            dkc = jnp.zeros((256, 128), F32)
            dvc = jnp.zeros((256, 128), F32)
            for e in range(2):
                msk = lo if e == 0 else hi
                qm = jnp.where(msk, q, jnp.zeros_like(q))
                dom = jnp.where(msk, do, jnp.zeros_like(do))
                delta = jnp.sum(jnp.where(msk, prod, 0.0), axis=1, keepdims=True)
                lse_e = lse_u[:, 64 * e:64 * e + 1]
                s = _dg(qm, kcat, NT) * (A_HD ** -0.5) - slopes[e] * dist
                p = jnp.where(valid, jnp.exp(jnp.where(valid, s, NEG) - lse_e), 0.0)
                dp = _dg(dom, vcat, NT)
                ds = (p * (dp - delta)).astype(BF)
                dqs.append(_dg(ds, kcat, NN))
                dkc = dkc + _dg(ds, qm, TN)
                dvc = dvc + _dg(p.astype(BF), dom, TN)
            dq_u = jnp.where(lo, dqs[0], dqs[1]) * (A_HD ** -0.5)
            dkc = dkc * (A_HD ** -0.5)
            if first:
                _put_rows(dq_s, start, d, dq_u)
            else:
                add_rows(dq_s, start, d, dq_u)
            add_rows(dk_s, start, d, dkc[128:])
            add_rows(dv_s, start, d, dvc[128:])
            if b > 0:
                add_rows(dk_s, start - 128 * d, d, dkc[:128])
                add_rows(dv_s, start - 128 * d, d, dvc[:128])
            else:
                pstart = start + SLAB - 128 * d
                add_rows(ck_s, pstart, d, dkc[:128])
                add_rows(cv_s, pstart, d, dvc[:128])

        for pi, (_, d) in enumerate(PATTERNS):
            for r, b, start in _units(d):
                unit(d, start, b, pi == 0)
        dq_ref[...] = dq_s[...].astype(BF)
        dk_ref[...] = dk_s[...].astype(BF)
        dv_ref[...] = dv_s[...].astype(BF)

    rev = lambda n: N_SLAB - 1 - n
    cur = lambda c: pl.BlockSpec((SLAB, 128), lambda j, n: (rev(n), c * N_PAIR + j))
    prv = lambda c: pl.BlockSpec((SLAB, 128), lambda j, n: (jnp.maximum(rev(n) - 1, 0), c * N_PAIR + j))
    one = pl.BlockSpec((SLAB, 128), lambda j, n: (rev(n), j))
    return pl.pallas_call(
        body, grid=(N_PAIR, N_SLAB),
        in_specs=[pl.BlockSpec((1, 8, 128), lambda j, n: (j, 0, 0)), cur(0), cur(1), prv(1), cur(2), prv(2),
                  one, one, one],
        out_specs=[one, one, one], out_shape=[jax.ShapeDtypeStruct((T, AW), BF)] * 3,
        scratch_shapes=[pltpu.VMEM((SLAB, 128), F32)] * 5,
        name=name, compiler_params=_cp())(_slope_table(), z_at, z_at, z_at, z_at, z_at, dout, out, lse)


def _silu(x):
    return x * jax.nn.sigmoid(x)


def _qk_math(c):
    s = _silu(c)
    return s * lax.rsqrt(jnp.sum(s * s, axis=-1, keepdims=True) + L2_EPS)


def _softplus(x):
    return jnp.maximum(x, 0.0) + jnp.log(1.0 + jnp.exp(-jnp.abs(x)))


def _gate_math(bd, alog_row, dtb_row):
    rows = bd.shape[0]
    lane = _iota(bd.shape, 1)
    beta = jax.nn.sigmoid(bd)
    g = jnp.where((lane >= DN_H) & (lane < 2 * DN_H), -jnp.exp(alog_row) * _softplus(bd + dtb_row), 0.0)
    ri = _iota((rows, rows), 0)
    ci = _iota((rows, rows), 1)
    same = (ri // CH) == (ci // CH)
    li = _iota((128, 128), 0)
    lj = _iota((128, 128), 1)
    to_next_group = jnp.where((lj == li + DN_H) & (li >= DN_H) & (li < 2 * DN_H), 1.0, 0.0)
    gc = _hdot(jnp.where(same & (ci <= ri), 1.0, 0.0), g)
    glast = _hdot(_hdot(jnp.where(same, 1.0, 0.0), g), to_next_group)
    return jnp.where(lane < DN_H, beta, 0.0) + gc + glast


def _shift_down(cur, halo, s):
    if s == 0:
        return cur
    rolled = pltpu.roll(cur, s, 0)
    hr = pltpu.roll(halo, s, 0)
    head = jnp.where(_iota(hr.shape, 0) < s, hr, rolled[:8])
    return jnp.concatenate([head, rolled[8:]], axis=0)


def _shift_up(cur, halo, s):
    if s == 0:
        return cur
    rows = cur.shape[0]
    rolled = pltpu.roll(cur, rows - s, 0)
    hr = pltpu.roll(halo, 8 - s, 0)
    tail = jnp.where(_iota(hr.shape, 0) >= 8 - s, hr, rolled[rows - 8:])
    return jnp.concatenate([rolled[:rows - 8], tail], axis=0)


def _dn_pre_fwd(z_dn, conv_w8, alog_row, dtb_row, name):
    tm = 256
    wq = 3 * DNW

    def body(raw_ref, halo_ref, bd_ref, w_ref, al_ref, dt_ref, conv_ref, qkv_ref, bg_ref):
        i = pl.program_id(0)
        cur = raw_ref[...]
        halo = jnp.where(i > 0, halo_ref[...], 0.0)
        w = w_ref[...]
        conv = jnp.zeros((tm, wq), F32)
        for j in range(4):
            conv = conv + _shift_down(cur, halo, 3 - j) * w[j:j + 1, :]
        conv_ref[...] = conv
        for blk in range(3 * DN_H):
            sl = slice(128 * blk, 128 * blk + 128)
            c = conv[:, sl]
            qkv_ref[:, sl] = _qk_math(c) if blk < 2 * DN_H else _silu(c)
        bg_ref[...] = _gate_math(bd_ref[...], al_ref[...], dt_ref[...])

    one = pl.BlockSpec((1, 128), lambda i: (0, 0))
    return pl.pallas_call(
        body, grid=(T // tm,),
        in_specs=[pl.BlockSpec((tm, wq), lambda i: (i, 0)),
                  pl.BlockSpec((8, wq), lambda i: (jnp.maximum(i * (tm // 8) - 1, 0), 0)),
                  pl.BlockSpec((tm, 128), lambda i: (i, BD_BLK)),
                  pl.BlockSpec((8, wq), lambda i: (0, 0)), one, one],
        out_specs=[pl.BlockSpec((tm, wq), lambda i: (i, 0)), pl.BlockSpec((tm, wq), lambda i: (i, 0)),
                   pl.BlockSpec((tm, 128), lambda i: (i, 0))],
        out_shape=[jax.ShapeDtypeStruct((T, wq), F32), jax.ShapeDtypeStruct((T, wq), F32),
                   jax.ShapeDtypeStruct((T, 128), F32)],
        name=name, compiler_params=_cp())(z_dn, z_dn, z_dn, conv_w8, alog_row, dtb_row)


def _dn_pre_bwd(conv, z_dn, alog_row, dtb_row, dqn, dkn, dvn, dbg, name):
    tm = 256
    wq = 3 * DNW

    def body(conv_ref, bd_ref, al_ref, dt_ref, dq_ref, dk_ref, dv_ref, dbg_ref,
             dconv_ref, dbd_ref, dal_ref, ddt_ref):
        i = pl.program_id(0)
        for blk in range(3 * DN_H):
            sl = slice(128 * blk, 128 * blk + 128)
            src = (dq_ref, dk_ref, dv_ref)[blk // DN_H]
            ct = src[:, 128 * (blk % DN_H):128 * (blk % DN_H) + 128]
            fn = _qk_math if blk < 2 * DN_H else _silu
            _, vjp = jax.vjp(fn, conv_ref[:, sl])
            dconv_ref[:, sl] = vjp(ct)[0]
        _, vjp = jax.vjp(_gate_math, bd_ref[...], al_ref[...], dt_ref[...])
        dbd, dal, ddt = vjp(dbg_ref[...])
        dbd_ref[...] = dbd

        @pl.when(i == 0)
        def _():
            dal_ref[...] = dal
            ddt_ref[...] = ddt

        @pl.when(i > 0)
        def _():
            dal_ref[...] += dal
            ddt_ref[...] += ddt

    one = pl.BlockSpec((1, 128), lambda i: (0, 0))
    row = pl.BlockSpec((tm, wq), lambda i: (i, 0))
    hd = pl.BlockSpec((tm, DNW), lambda i: (i, 0))
    st = pl.BlockSpec((tm, 128), lambda i: (i, 0))
    return pl.pallas_call(
        body, grid=(T // tm,),
        in_specs=[row, pl.BlockSpec((tm, 128), lambda i: (i, BD_BLK)), one, one, hd, hd, hd, st],
        out_specs=[row, st, one, one],
        out_shape=[jax.ShapeDtypeStruct((T, wq), F32), jax.ShapeDtypeStruct((T, 128), F32),
                   jax.ShapeDtypeStruct((1, 128), F32), jax.ShapeDtypeStruct((1, 128), F32)],
        name=name, compiler_params=_cp())(conv, z_dn, alog_row, dtb_row, dqn, dkn, dvn, dbg)


def _dn_conv_bwd(dconv, z_dn, conv_w8, name):
    tm = 256
    wq = 3 * DNW
    last = T // tm - 1

    def body(dc_ref, dcn_ref, raw_ref, halo_ref, w_ref, draw_ref, dw_ref):
        i = pl.program_id(0)
        dc = dc_ref[...]
        nxt = jnp.where(i < last, dcn_ref[...], 0.0)
        cur = raw_ref[...]
        halo = jnp.where(i > 0, halo_ref[...], 0.0)
        w = w_ref[...]
        draw = jnp.zeros((tm, wq), F32)
        rows = []
        for j in range(4):
            draw = draw + _shift_up(dc, nxt, 3 - j) * w[j:j + 1, :]
            rows.append(jnp.sum(dc * _shift_down(cur, halo, 3 - j), axis=0, keepdims=True))
        draw_ref[...] = draw
        part = jnp.concatenate(rows + [jnp.zeros((4, wq), F32)], axis=0)

        @pl.when(i == 0)
        def _():
            dw_ref[...] = part

        @pl.when(i > 0)
        def _():
            dw_ref[...] += part

    row = pl.BlockSpec((tm, wq), lambda i: (i, 0))
    return pl.pallas_call(
        body, grid=(T // tm,),
        in_specs=[row, pl.BlockSpec((8, wq), lambda i: (jnp.minimum((i + 1) * (tm // 8), T // 8 - 1), 0)),
                  row, pl.BlockSpec((8, wq), lambda i: (jnp.maximum(i * (tm // 8) - 1, 0), 0)),
                  pl.BlockSpec((8, wq), lambda i: (0, 0))],
        out_specs=[row, pl.BlockSpec((8, wq), lambda i: (0, 0))],
        out_shape=[jax.ShapeDtypeStruct((T, wq), F32), jax.ShapeDtypeStruct((8, wq), F32)],
        name=name, compiler_params=_cp())(dconv, dconv, z_dn, z_dn, conv_w8)


def _h3(a, b, dims=NN):
    return lax.dot_general(a, b, dims, precision=lax.Precision.HIGH, preferred_element_type=F32)


@jax.custom_vjp
def _inverse_given(a_mat, tinv):
    return tinv


def _inverse_given_fwd(a_mat, tinv):
    return tinv, tinv


def _inverse_given_bwd(tinv, g):
    return -_h3(tinv, _h3(g, tinv, NT), TN), jnp.zeros_like(tinv)


_inverse_given.defvjp(_inverse_given_fwd, _inverse_given_bwd)


def _prep_head(q, k, v, bgc, h):
    beta = _col(bgc, h)
    gc = jnp.broadcast_to(_col(bgc, DN_H + h), (PAIR, 128))
    glast = jnp.broadcast_to(_col(bgc, 2 * DN_H + h), (PAIR, 128))
    ri = _iota((PAIR, PAIR), 0)
    ci = _iota((PAIR, PAIR), 1)
    same = (ri // CH) == (ci // CH)
    causal = same & (ci <= ri)
    strict = same & (ci < ri)
    eye = ri == ci
    gc_cols = _hdot(jnp.ones((PAIR, PAIR), F32), jnp.where(eye, gc, 0.0))
    decay = jnp.exp(jnp.where(causal, gc - gc_cols, NEG))
    egc = jnp.exp(gc)
    kb = k * beta
    a_mat = jnp.where(strict, _bdot_nt(kb, k) * decay, 0.0)
    qs = q * (DN_HD ** -0.5)
    attn = jnp.where(causal, _bdot_nt(qs, k) * decay, 0.0)
    return a_mat, (v * beta, kb * egc, qs * egc, k * jnp.exp(glast - gc), attn, jnp.exp(glast))


def _prep_tail(tinv, ctx):
    vb, kbe, qg, kdec, attn, decb = ctx
    return _h3(tinv, vb), _h3(tinv, kbe), qg, kdec, attn, decb


def _inverses(a_mats):
    eye = jnp.where(_iota((PAIR, PAIR), 0) == _iota((PAIR, PAIR), 1), 1.0, 0.0)
    ps = [-a for a in a_mats]
    tinvs = [eye + p for p in ps]
    for _ in range(5):
        ps = [_h3(p, p) for p in ps]
        tinvs = [t + _h3(t, p) for t, p in zip(tinvs, ps)]
    return tinvs


def _prep_math(q, k, v, bgc, h, tinv_saved):
    a_mat, ctx = _prep_head(q, k, v, bgc, h)
    return _prep_tail(_inverse_given(a_mat, tinv_saved), ctx)


def _dn_prep_fwd(qkv, bg, name):
    rows = 512
    hd = lambda off: pl.BlockSpec((rows, 128), lambda g, h: (g, off + h))
    out = pl.BlockSpec((rows, 128), lambda g, h: (g, h))

    def body(q_ref, k_ref, v_ref, bg_ref, *outs):
        h = pl.program_id(1)
        spans = [slice(PAIR * pr, PAIR * pr + PAIR) for pr in range(rows // PAIR)]
        heads = [_prep_head(q_ref[rs, :], k_ref[rs, :], v_ref[rs, :], bg_ref[rs, :], h) for rs in spans]
        tinvs = _inverses([a for a, _ in heads])
        for rs, tinv, (_, ctx) in zip(spans, tinvs, heads):
            for o_ref, val in zip(outs, _prep_tail(tinv, ctx) + (tinv,)):
                o_ref[rs, :] = val

    return pl.pallas_call(
        body, grid=(T // rows, DN_H),
        in_specs=[hd(0), hd(DN_H), hd(2 * DN_H), pl.BlockSpec((rows, 128), lambda g, h: (g, 0))],
        out_specs=[out] * 7, out_shape=[jax.ShapeDtypeStruct((T, DNW), F32)] * 7,
        name=name, compiler_params=_cp())(qkv, qkv, qkv, bg)


def _dn_prep_bwd(qkv, bg, tinv, cts, name):
    rows = 512
    hd = lambda off: pl.BlockSpec((rows, 128), lambda g, h: (g, off + h))
    out = pl.BlockSpec((rows, 128), lambda g, h: (g, h))
    st = pl.BlockSpec((rows, 128), lambda g, h: (g, 0))

    def body(q_ref, k_ref, v_ref, bg_ref, ti_ref, c0, c1, c2, c3, c4, c5, dq_ref, dk_ref, dv_ref, dbg_ref):
        h = pl.program_id(1)
        spans = [slice(PAIR * pr, PAIR * pr + PAIR) for pr in range(rows // PAIR)]
        tis = [ti_ref[rs, :] for rs in spans]

        def joint(qs, ks, vs, bs):
            heads = [_prep_head(q, k, v, b, h) for q, k, v, b in zip(qs, ks, vs, bs)]
            return [_prep_tail(_inverse_given(a, ti), ctx) for (a, ctx), ti in zip(heads, tis)]

        _, vjp = jax.vjp(joint, *[[r[rs, :] for rs in spans] for r in (q_ref, k_ref, v_ref, bg_ref)])
        dqs, dks, dvs, dbs = vjp([tuple(c[rs, :] for c in (c0, c1, c2, c3, c4, c5)) for rs in spans])
        for rs, dq, dk, dv in zip(spans, dqs, dks, dvs):
            dq_ref[rs, :] = dq
            dk_ref[rs, :] = dk
            dv_ref[rs, :] = dv
        dbg_all = jnp.concatenate(dbs, axis=0)

        @pl.when(h == 0)
        def _():
            dbg_ref[...] = dbg_all

        @pl.when(h > 0)
        def _():
            dbg_ref[...] += dbg_all

    return pl.pallas_call(
        body, grid=(T // rows, DN_H),
        in_specs=[hd(0), hd(DN_H), hd(2 * DN_H), st] + [out] * 7,
        out_specs=[out, out, out, st],
        out_shape=[jax.ShapeDtypeStruct((T, DNW), F32)] * 3 + [jax.ShapeDtypeStruct((T, 128), F32)],
        name=name, compiler_params=_cp())(qkv, qkv, qkv, bg, tinv, *cts)


def _step_math(s, u, w, qg, kdec, attn, decb, sub):
    vnew = u - _bdot_nn(w, s)
    z = jnp.zeros((CH, 128), F32)
    vfull = jnp.concatenate([vnew, z] if sub == 0 else [z, vnew], axis=0)
    o = _bdot_nn(qg, s) + _bdot_nn(attn, vfull)
    dec = jnp.sum(decb, axis=0, keepdims=True) * (1.0 / CH)
    return s * dec + _bdot_tn(kdec, vnew), o


def _dn_scan_fwd(prep, name):
    npair = T // PAIR
    row = pl.BlockSpec((PAIR, DNW), lambda p: (p, 0))

    def body(u_ref, w_ref, qg_ref, kd_ref, at_ref, db_ref, o_ref, ss_ref, s_ref):
        @pl.when(pl.program_id(0) == 0)
        def _():
            s_ref[...] = jnp.zeros_like(s_ref)

        states = [s_ref[h] for h in range(DN_H)]
        for sub in range(2):
            rs = slice(CH * sub, CH * sub + CH)
            for h in range(DN_H):
                ls = slice(128 * h, 128 * h + 128)
                ss_ref[sub, h] = states[h]
                states[h], o = _step_math(states[h], u_ref[rs, ls], w_ref[rs, ls], qg_ref[rs, ls],
                                          kd_ref[rs, ls], at_ref[rs, ls], db_ref[rs, ls], sub)
                o_ref[rs, ls] = o
        for h in range(DN_H):
            s_ref[h] = states[h]

    return pl.pallas_call(
        body, grid=(npair,), in_specs=[row] * 6,
        out_specs=[row, pl.BlockSpec((2, DN_H, 128, 128), lambda p: (p, 0, 0, 0))],
        out_shape=[jax.ShapeDtypeStruct((T, DNW), F32), jax.ShapeDtypeStruct((T // CH, DN_H, 128, 128), F32)],
        scratch_shapes=[pltpu.VMEM((DN_H, 128, 128), F32)],
        name=name, compiler_params=_cp())(*prep)


def _dn_scan_bwd(prep, states, do, name):
    npair = T // PAIR
    row = pl.BlockSpec((PAIR, DNW), lambda p: (npair - 1 - p, 0))

    def body(u_ref, w_ref, qg_ref, kd_ref, at_ref, db_ref, ss_ref, do_ref, *rest):
        outs, ds_ref = rest[:6], rest[6]

        @pl.when(pl.program_id(0) == 0)
        def _():
            ds_ref[...] = jnp.zeros_like(ds_ref)

        dss = [ds_ref[h] for h in range(DN_H)]
        for sub in (1, 0):
            rs = slice(CH * sub, CH * sub + CH)
            for h in range(DN_H):
                ls = slice(128 * h, 128 * h + 128)
                args = (ss_ref[sub, h],) + tuple(r[rs, ls] for r in (u_ref, w_ref, qg_ref, kd_ref, at_ref, db_ref))
                _, vjp = jax.vjp(functools.partial(_step_math, sub=sub), *args)
                cts = vjp((dss[h], do_ref[rs, ls]))
                dss[h] = cts[0]
                for o_ref, val in zip(outs, cts[1:]):
                    o_ref[rs, ls] = val
        for h in range(DN_H):
            ds_ref[h] = dss[h]

    return pl.pallas_call(
        body, grid=(npair,),
        in_specs=[row] * 6 + [pl.BlockSpec((2, DN_H, 128, 128), lambda p: (npair - 1 - p, 0, 0, 0)), row],
        out_specs=[row] * 6, out_shape=[jax.ShapeDtypeStruct((T, DNW), F32)] * 6,
        scratch_shapes=[pltpu.VMEM((DN_H, 128, 128), F32)],
        name=name, compiler_params=_cp())(*prep, states, do)


def _post_math(o, gate, wrow):
    return o * lax.rsqrt(jnp.mean(o * o, axis=-1, keepdims=True) + NORM_EPS) * wrow * _silu(gate)


def _dn_post_fwd(o, z_dn, dn_norm, name):
    tm = 512
    row = pl.BlockSpec((tm, DNW), lambda i: (i, 0))

    def body(o_ref, g_ref, w_ref, y_ref):
        for h in range(DN_H):
            ls = slice(128 * h, 128 * h + 128)
            y_ref[:, ls] = _post_math(o_ref[:, ls], g_ref[:, ls], w_ref[...]).astype(BF)

    return pl.pallas_call(
        body, grid=(T // tm,),
        in_specs=[row, pl.BlockSpec((tm, DNW), lambda i: (i, 3)), pl.BlockSpec((1, 128), lambda i: (0, 0))],
        out_specs=row, out_shape=jax.ShapeDtypeStruct((T, DNW), BF),
        name=name, compiler_params=_cp())(o, z_dn, dn_norm)


def _dn_post_bwd(o, z_dn, dn_norm, dy, name):
    tm = 512
    row = pl.BlockSpec((tm, DNW), lambda i: (i, 0))
    one = pl.BlockSpec((1, 128), lambda i: (0, 0))

    def body(o_ref, g_ref, w_ref, dy_ref, do_ref, dg_ref, dw_ref):
        i = pl.program_id(0)
        dw = jnp.zeros((1, 128), F32)
        for h in range(DN_H):
            ls = slice(128 * h, 128 * h + 128)
            _, vjp = jax.vjp(_post_math, o_ref[:, ls], g_ref[:, ls], w_ref[...])
            do, dg, dwh = vjp(dy_ref[:, ls].astype(F32))
            do_ref[:, ls] = do
            dg_ref[:, ls] = dg
            dw = dw + dwh

        @pl.when(i == 0)
        def _():
            dw_ref[...] = dw

        @pl.when(i > 0)
        def _():
            dw_ref[...] += dw

    return pl.pallas_call(
        body, grid=(T // tm,),
        in_specs=[row, pl.BlockSpec((tm, DNW), lambda i: (i, 3)), one, pl.BlockSpec((tm, DNW), lambda i: (i, 1))],
        out_specs=[row, row, one],
        out_shape=[jax.ShapeDtypeStruct((T, DNW), F32), jax.ShapeDtypeStruct((T, DNW), F32),
                   jax.ShapeDtypeStruct((1, 128), F32)],
        name=name, compiler_params=_cp())(o, z_dn, dn_norm, dy)


def _dz_dn_assemble(draw, dgate, dbd, name):
    tm = 512

    def body(a_ref, b_ref, c_ref, o_ref):
        o_ref[:, :3 * DNW] = a_ref[...].astype(BF)
        o_ref[:, 3 * DNW:4 * DNW] = b_ref[...].astype(BF)
        o_ref[:, 4 * DNW:4 * DNW + 128] = c_ref[...].astype(BF)
        o_ref[:, 4 * DNW + 128:] = jnp.zeros((tm, 128), BF)

    return pl.pallas_call(
        body, grid=(T // tm,),
        in_specs=[pl.BlockSpec((tm, 3 * DNW), lambda i: (i, 0)), pl.BlockSpec((tm, DNW), lambda i: (i, 0)),
                  pl.BlockSpec((tm, 128), lambda i: (i, 0))],
        out_specs=pl.BlockSpec((tm, ZD), lambda i: (i, 0)),
        out_shape=jax.ShapeDtypeStruct((T, ZD), BF), name=name, compiler_params=_cp())(draw, dgate, dbd)


HBM = pl.BlockSpec(memory_space=pltpu.HBM)
SEM = pl.BlockSpec(memory_space=pltpu.SEMAPHORE)
EFFECT = pltpu.SideEffectType.DATAFLOW_SIDE_EFFECTING
N_PEER = N_DEV - 1


def _peers(x, y, c):
    return [(k, (x ^ (k >> 2), y ^ ((k >> 1) & 1), c ^ (k & 1))) for k in (1, 2, 4, 3, 5, 6, 7)]


def _exchange_copy(ins, lands, ssems, rsems, scatter, t, k, pos, me):
    px, py, pc = pos
    src = ins[t].at[4 * px + 2 * py + pc] if scatter else ins[t]
    return pltpu.make_async_remote_copy(
        src_ref=src, dst_ref=lands[t].at[me], send_sem=ssems[t].at[k - 1], recv_sem=rsems[t].at[k - 1],
        device_id=pos, device_id_type=MESH_ID)


def _xstart(bufs, scatter, name):
    nt = len(bufs)
    lands = [lax.empty((N_DEV,) + tuple(b.shape[1:] if scatter else b.shape), b.dtype) for b in bufs]

    def body(*refs):
        ins, lnd = refs[:nt], refs[nt:2 * nt]
        ssems, rsems = refs[2 * nt:3 * nt], refs[3 * nt:4 * nt]
        token = refs[-1]
        x, y, c = lax.axis_index("x"), lax.axis_index("y"), lax.axis_index("c")
        me = 4 * x + 2 * y + c
        for t in range(nt):
            for k, pos in _peers(x, y, c):
                _exchange_copy(ins, lnd, ssems, rsems, scatter, t, k, pos, me).start()
        token[...] = jnp.zeros_like(token)

    both = list(bufs) + lands
    res = pl.pallas_call(
        body, name=name,
        out_shape=[pltpu.SemaphoreType.DMA((N_PEER,))] * (2 * nt)
        + [pltpu.HBM(b.shape, b.dtype) for b in both] + [jax.ShapeDtypeStruct((8, 128), F32)],
        in_specs=[HBM] * (2 * nt),
        out_specs=[SEM] * (2 * nt) + [HBM] * (2 * nt) + [pl.BlockSpec(memory_space=pltpu.VMEM)],
        input_output_aliases={i: 2 * nt + i for i in range(2 * nt)},
        compiler_params=pltpu.CompilerParams(has_side_effects=EFFECT),
    )(*[pltpu.with_memory_space_constraint(b, pltpu.HBM) for b in both])
    return res[:nt], res[nt:2 * nt], res[2 * nt:3 * nt], res[3 * nt:4 * nt], res[-1][0, 0]


def _xwait(ssems, rsems, thrus, lands, scatter, after, name):
    nt = len(lands)

    def body(*refs):
        ins, lnd = refs[:nt], refs[nt:2 * nt]
        ss, rs = refs[2 * nt:3 * nt], refs[3 * nt:4 * nt]
        x, y, c = lax.axis_index("x"), lax.axis_index("y"), lax.axis_index("c")
        me = 4 * x + 2 * y + c
        for t in range(nt):
            for k, pos in _peers(x, y, c):
                cp = _exchange_copy(ins, lnd, ss, rs, scatter, t, k, pos, me)
                cp.wait_send()
                cp.wait_recv()

    both = list(thrus) + list(lands)
    res = pl.pallas_call(
        body, name=name, out_shape=[pltpu.HBM(b.shape, b.dtype) for b in both],
        in_specs=[HBM] * (2 * nt) + [SEM] * (2 * nt) + [ANY], out_specs=[HBM] * (2 * nt),
        input_output_aliases={i: i for i in range(2 * nt)},
        compiler_params=pltpu.CompilerParams(has_side_effects=EFFECT),
    )(*both, *ssems, *rsems, after)
    return res[:nt], res[nt:]


def _sum_slabs(recv, name):
    n, r, c = recv.shape
    tr = r // 2

    def body(r_ref, o_ref):
        g = r_ref[0].astype(F32)
        for s in range(1, n):
            g = g + r_ref[s].astype(F32)
        o_ref[...] = g

    return pl.pallas_call(
        body, grid=(r // tr,), in_specs=[pl.BlockSpec((n, tr, c), lambda i: (0, i, 0))],
        out_specs=pl.BlockSpec((tr, c), lambda i: (i, 0)),
        out_shape=jax.ShapeDtypeStruct((r, c), F32), name=name, compiler_params=_cp())(recv)


def _adam(recv, w, m, v, tr, name):
    _, r, c = w.shape
    n_part = recv.shape[0]
    c1 = np.float32(1.0 - ADAM_B1 ** ADAM_STEP)
    c2 = np.float32(1.0 - ADAM_B2 ** ADAM_STEP)

    def body(r_ref, w_ref, m_ref, v_ref, g_ref, d_ref, mo_ref, vo_ref):
        g = r_ref[0].astype(F32)
        for s in range(1, n_part):
            g = g + r_ref[s].astype(F32)
        mn = ADAM_B1 * m_ref[0] + (1.0 - ADAM_B1) * g
        vn = ADAM_B2 * v_ref[0] + (1.0 - ADAM_B2) * (g * g)
        g_ref[0] = g
        mo_ref[0] = mn
        vo_ref[0] = vn
        d_ref[0] = -ADAM_LR * ((mn / c1) / (jnp.sqrt(vn / c2) + ADAM_EPS) + ADAM_WD * w_ref[0])

    one = pl.BlockSpec((1, tr, c), lambda i: (0, i, 0))
    return pl.pallas_call(
        body, grid=(r // tr,), in_specs=[pl.BlockSpec((n_part, tr, c), lambda i: (0, i, 0)), one, one, one],
        out_specs=[one] * 4, out_shape=[jax.ShapeDtypeStruct((1, r, c), F32)] * 4,
        name=name, compiler_params=_cp())(recv, w, m, v)


def _local_step(x, target, sp, need, emit):
    g = {}
    x0, y1, saved1 = _ffn_fwd(x, None, 0.0, sp["norm_ffn1"], lambda kind, a: need("w" + kind + "1", a), "ffn1")
    x1, h2 = _rms_fwd(x0, y1, 0.5, sp["norm_mix"], "mix_norm")
    win_a, win_d = need("win_a", h2), need("win_d", h2)
    conv_w8, wout = need("conv_w8", h2), need("wout", h2)
    z_at = _mm_nn(h2, win_a, F32, 1024, 768, "mix_in_attn")
    z_dn = _mm_nn(h2, win_d, F32, 1024, 768, "mix_in_dn")
    attn_f, attn_b, lse = _attn_fwd(z_at, "attn_fwd")

    conv, qkvn, bg = _dn_pre_fwd(z_dn, conv_w8, sp["alog_row"], sp["dtb_row"], "dn_pre")
    *prep, tinv = _dn_prep_fwd(qkvn, bg, "dn_prep")
    o_dn, states = _dn_scan_fwd(prep, "dn_scan")
    dn_b = _dn_post_fwd(o_dn, z_dn, sp["dn_norm"], "dn_post")

    y2 = _mm_nn2(attn_b, dn_b, wout[:AW], wout[AW:], 1024, 1024, "mix_out")
    x2, y3, saved2 = _ffn_fwd(x1, y2, 1.0, sp["norm_ffn2"], lambda kind, a: need("w" + kind + "2", a), "ffn2")

    dx3, dys3, g["norm_final"], loss8 = _loss_bwd(x2, y3, sp["norm_final"], target, "loss")
    dx2, dx2b, g["norm_ffn2"] = _ffn_bwd(
        x2, sp["norm_ffn2"], saved2, dx3, dys3, 1.0,
        lambda kind, dw: emit(kind + "2", {"w" + kind + "2": dw}), "ffn2b")

    zero = emit("wout", {"wout": jnp.concatenate([_mm_tn(attn_b, dx2b, BF, 512, 1024, "mix_out_dw_a"),
                                                  _mm_tn(dn_b, dx2b, BF, 512, 1024, "mix_out_dw_d")], axis=0)})
    dmix = _mm_nt(dx2b, wout, F32, 1024, 1024, "mix_out_dx")

    dz_at = jnp.concatenate(_attn_bwd(z_at, dmix, attn_f, lse, "attn_bwd"), axis=1)

    do_dn, dgate, g["dn_norm"] = _dn_post_bwd(o_dn, z_dn, sp["dn_norm"] + zero, dmix, "dn_post_b")
    cts = _dn_scan_bwd(prep, states, do_dn, "dn_scan_b")
    dqn, dkn, dvn, dbg = _dn_prep_bwd(qkvn, bg, tinv, cts, "dn_prep_b")
    dconv, dbd, g["alog_row"], g["dtb_row"] = _dn_pre_bwd(
        conv, z_dn, sp["alog_row"], sp["dtb_row"], dqn, dkn, dvn, dbg, "dn_pre_b")
    draw, dconv_w8 = _dn_conv_bwd(dconv, z_dn, conv_w8, "dn_conv_b")
    dz_dn = _dz_dn_assemble(draw, dgate, dbd, "dn_dz")

    zero = emit("win", {"win_a": _mm_tn(h2, dz_at, BF, 512, 768, "mix_in_dw_a"),
                        "win_d": _mm_tn(h2, dz_dn, BF, 512, 768, "mix_in_dw_d"), "conv_w8": dconv_w8})
    dh2 = _mm_nt(dz_at, win_a, F32, 1024, 1024, "mix_in_dx_a")
    dh2d = _mm_nt(dz_dn, win_d, F32, 1024, 1024, "mix_in_dx_d")
    (dx1, dys1), _, g["norm_mix"] = _rms_bwd2(x1, sp["norm_mix"] + zero, dh2, dh2d, dx2, "mix_dnorm")
    dx0, _, g["norm_ffn1"] = _ffn_bwd(
        x0, sp["norm_ffn1"], saved1, dx1, dys1, 1.0,
        lambda kind, dw: emit(kind + "1", {"w" + kind + "1": dw}), "ffn1b")
    return loss8[0, 0], dx0, g


def _rms_bwd2(x, gain, dh_a, dh_b, dres, name):
    tm = 512
    row = pl.BlockSpec((tm, D), lambda i: (i, 0))
    gspec = pl.BlockSpec((1, D), lambda i: (0, 0))

    def body(x_ref, g_ref, da_ref, db_ref, dres_ref, dx_ref, dxs_ref, dg_ref):
        i = pl.program_id(0)
        xv = x_ref[...]
        r = lax.rsqrt(jnp.mean(xv * xv, axis=-1, keepdims=True) + NORM_EPS)
        xh = xv * r
        dhv = da_ref[...] + db_ref[...]
        part = jnp.sum(dhv * xh, axis=0, keepdims=True)

        @pl.when(i == 0)
        def _():
            dg_ref[...] = part

        @pl.when(i > 0)
        def _():
            dg_ref[...] += part

        dxh = dhv * g_ref[...]
        dx = r * (dxh - xh * jnp.mean(dxh * xh, axis=-1, keepdims=True)) + dres_ref[...]
        dx_ref[...] = dx
        dxs_ref[...] = (0.5 * dx).astype(BF)

    dx, dxs, dg = pl.pallas_call(
        body, grid=(T // tm,), in_specs=[row, gspec, row, row, row], out_specs=[row, row, gspec],
        out_shape=[jax.ShapeDtypeStruct((T, D), F32), jax.ShapeDtypeStruct((T, D), BF),
                   jax.ShapeDtypeStruct((1, D), F32)],
        name=name, compiler_params=_cp())(x, gain, dh_a, dh_b, dres)
    return (dx, dxs), None, dg


def _cols_from_shards(gathered):
    n, r, c = gathered.shape
    return jnp.transpose(gathered, (1, 0, 2)).reshape(r, n * c)


def _shards_from_cols(full, dtype):
    r, nc = full.shape
    return jnp.transpose(full.reshape(r, N_DEV, nc // N_DEV), (1, 0, 2)).astype(dtype)


def _lane_row(vec4):
    return jnp.zeros((1, 128), F32).at[:, DN_H:2 * DN_H].set(vec4.astype(F32))


WEIGHT_SOURCES = {"wg1": "gate1", "wu1": "up1", "wd1": "down1", "win_a": "w_in", "win_d": "w_in",
                  "conv_w8": "conv_w", "wout": "w_out", "wg2": "gate2", "wu2": "up2", "wd2": "down2"}
TRANSPOSED = ("gate1", "up1", "gate2", "up2")


def _build_weights(name, gath):
    if name in ("wg1", "wu1", "wd1", "wg2", "wu2", "wd2"):
        return {name: gath[WEIGHT_SOURCES[name]].reshape(F, D)}
    if name in ("win_a", "win_d"):
        w_in = _cols_from_shards(gath["w_in"])
        c0 = 3 * AW + 3 * DNW
        wp = jnp.concatenate([w_in[:, :c0], w_in[:, c0 + 2 * DN_H:], w_in[:, c0:c0 + 2 * DN_H],
                              jnp.zeros((D, ZP - IN_COLS), w_in.dtype)], axis=1)
        return {"win_a": wp[:, :ZA], "win_d": wp[:, ZA:]}
    if name == "wout":
        return {name: gath["w_out"].reshape(D, D)}
    conv = _cols_from_shards(gath["conv_w"])
    return {"conv_w8": jnp.concatenate([conv, jnp.zeros((4, 3 * DNW), F32)], axis=0)}


def _small_params(norm_ffn1, norm_mix, norm_ffn2, norm_final, a_log, dt_bias, dn_norm):
    return {"norm_ffn1": norm_ffn1, "norm_mix": norm_mix, "norm_ffn2": norm_ffn2,
            "norm_final": norm_final.reshape(1, D), "alog_row": _lane_row(a_log), "dtb_row": _lane_row(dt_bias),
            "dn_norm": dn_norm}


def _grad_slabs(group, g):
    if group[0] in "gud":
        return {WEIGHT_SOURCES["w" + group]: g["w" + group].reshape(N_DEV, F // N_DEV, D)}
    if group == "wout":
        return {"w_out": g["wout"].reshape(N_DEV, D // N_DEV, D)}
    gp = jnp.concatenate([g["win_a"], g["win_d"]], axis=1)
    c0 = 3 * AW + 3 * DNW
    g_in = jnp.concatenate([gp[:, :c0], gp[:, c0 + DNW:c0 + DNW + 2 * DN_H], gp[:, c0:c0 + DNW]], axis=1)
    return {"w_in": _shards_from_cols(g_in, BF), "conv_w": _shards_from_cols(g["conv_w8"][:4], F32)}


SMALL_ROWS = 40


def _small_pack(norm_ffn1, norm_mix, norm_ffn2, norm_final, dn_norm, alog_row, dtb_row):
    rows = [a.reshape(8, 128) for a in (norm_ffn1, norm_mix, norm_ffn2, norm_final)]
    rows += [dn_norm.reshape(1, 128), alog_row, dtb_row, jnp.zeros((SMALL_ROWS - 35, 128), F32)]
    return jnp.concatenate(rows, axis=0)


def _small_unpack(pk):
    pk = pk[0]
    return (pk[0:8].reshape(1, D), pk[8:16].reshape(1, D), pk[16:24].reshape(1, D), pk[24:32].reshape(D),
            pk[32:33], pk[33:34, DN_H:2 * DN_H], pk[34:35, DN_H:2 * DN_H])


ADAM_TILE = {"gate1": 256, "up1": 256, "down1": 176, "gate2": 256, "up2": 256, "down2": 176,
             "w_in": 256, "w_out": 128, "conv_w": 4}
BIG = ("gate1", "up1", "down1", "w_in", "w_out", "gate2", "up2", "down2", "conv_w")


def kernel(x, norm_ffn1, ffn1_gate, ffn1_up, ffn1_down, norm_mix, w_in, conv_w, a_log, dt_bias, dn_norm, w_out, norm_ffn2, ffn2_gate, ffn2_up, ffn2_down, norm_final, loss_target, m_norm_ffn1, m_ffn1_gate, m_ffn1_up, m_ffn1_down, m_norm_mix, m_w_in, m_conv_w, m_a_log, m_dt_bias, m_dn_norm, m_w_out, m_norm_ffn2, m_ffn2_gate, m_ffn2_up, m_ffn2_down, m_norm_final, v_norm_ffn1, v_ffn1_gate, v_ffn1_up, v_ffn1_down, v_norm_mix, v_w_in, v_conv_w, v_a_log, v_dt_bias, v_dn_norm, v_w_out, v_norm_ffn2, v_ffn2_gate, v_ffn2_up, v_ffn2_down, v_norm_final):
    w = {"gate1": ffn1_gate, "up1": ffn1_up, "down1": ffn1_down, "w_in": w_in, "w_out": w_out,
         "gate2": ffn2_gate, "up2": ffn2_up, "down2": ffn2_down, "conv_w": conv_w}
    m = {"gate1": m_ffn1_gate, "up1": m_ffn1_up, "down1": m_ffn1_down, "w_in": m_w_in, "w_out": m_w_out,
         "gate2": m_ffn2_gate, "up2": m_ffn2_up, "down2": m_ffn2_down, "conv_w": m_conv_w}
    v = {"gate1": v_ffn1_gate, "up1": v_ffn1_up, "down1": v_ffn1_down, "w_in": v_w_in, "w_out": v_w_out,
         "gate2": v_ffn2_gate, "up2": v_ffn2_up, "down2": v_ffn2_down, "conv_w": v_conv_w}

    me = 4 * lax.axis_index("x") + 2 * lax.axis_index("y") + lax.axis_index("c")
    own_slot = lambda land, mine: lax.dynamic_update_index_in_dim(land, mine, me, 0)

    ag_order = ("gate1", "up1", "down1", "w_in", "conv_w", "w_out", "gate2", "up2", "down2")
    ag_groups = (("gate1",), ("up1",), ("down1",), ("w_in", "conv_w", "w_out"), ("gate2",), ("up2",), ("down2",))
    pos = {n: i for i, n in enumerate(ag_order)}

    def shard(n):
        if n == "conv_w":
            return w[n][0]
        return (w[n][0].T if n in TRANSPOSED else w[n][0]).astype(BF)

    ss, rs, thru, land, zero = _xstart([shard(n) for n in ag_order], False, "weights_start")
    gath, built = {}, {}

    def need(name, after):
        if name not in built:
            src = WEIGHT_SOURCES[name]
            if src not in gath:
                gi = [i for i, grp in enumerate(ag_groups) if src in grp][0]
                ids = [pos[n] for n in ag_groups[gi]]
                thrus, lands = _xwait([ss[i] for i in ids], [rs[i] for i in ids], [thru[i] for i in ids],
                                      [land[i] for i in ids], False, after, "weights_wait%d" % gi)
                for n, t, l in zip(ag_groups[gi], thrus, lands):
                    gath[n] = own_slot(l, t)
            built.update(_build_weights(name, gath))
        return built[name]

    pending = []

    def emit(group, grads):
        slabs = grads if group == "small" else _grad_slabs(group, grads)
        names = list(slabs)
        started = _xstart([slabs[n] for n in names], True, "grads_start_" + group)
        pending.append((group, names) + started[:4])
        return started[4]

    sp = _small_params(norm_ffn1 + zero, norm_mix, norm_ffn2, norm_final, a_log, dt_bias, dn_norm)
    loss_part, dx, g = _local_step(x[0], loss_target[0], sp, need, emit)
    small = _small_pack(g["norm_ffn1"], g["norm_mix"], g["norm_ffn2"], g["norm_final"], g["dn_norm"],
                        g["alog_row"], g["dtb_row"])
    emit("small", {"small": jnp.broadcast_to(small[None], (N_DEV, SMALL_ROWS, 128))})

    pack = lambda a: _small_pack(*a)[None]
    res, after = {}, dx
    for group, names, gss, grs, gthru, gland in pending:
        thrus, lands = _xwait(gss, grs, gthru, gland, True, after, "grads_wait_" + group)
        for n, t, l in zip(names, thrus, lands):
            recv = own_slot(l, lax.dynamic_index_in_dim(t, me, 0, keepdims=False))
            if n == "small":
                res[n] = _adam(
                    recv,
                    pack((norm_ffn1, norm_mix, norm_ffn2, norm_final, dn_norm, _lane_row(a_log), _lane_row(dt_bias))),
                    pack((m_norm_ffn1, m_norm_mix, m_norm_ffn2, m_norm_final, m_dn_norm, _lane_row(m_a_log),
                          _lane_row(m_dt_bias))),
                    pack((v_norm_ffn1, v_norm_mix, v_norm_ffn2, v_norm_final, v_dn_norm, _lane_row(v_a_log),
                          _lane_row(v_dt_bias))),
                    SMALL_ROWS, "adam_small")
            elif n in TRANSPOSED:
                total = _sum_slabs(recv, "gsum_" + n)
                res[n] = _adam(total.T[None], w[n], m[n], v[n], ADAM_TILE[n], "adam_" + n)
            else:
                res[n] = _adam(recv, w[n], m[n], v[n], ADAM_TILE[n], "adam_" + n)
            after = res[n][0]
    res_s = res["small"]

    loss = lax.psum(loss_part, ("x", "y", "c"))
    outs = [loss, dx[None]]
    for k in range(4):
        n1, nm, n2, nf, dn, al, dt = _small_unpack(res_s[k])
        big = {n: res[n][k] for n in BIG}
        outs += [n1, big["gate1"], big["up1"], big["down1"], nm, big["w_in"], big["conv_w"], al, dt, dn,
                 big["w_out"], n2, big["gate2"], big["up2"], big["down2"], nf]
    return tuple(outs)
```

```python
import functools

import numpy as np
import jax
import jax.numpy as jnp
from jax import lax
from jax.experimental import pallas as pl
from jax.experimental.pallas import tpu as pltpu

T = 4096
D = 1024
F = 2816
N_DEV = 8
A_HEADS = 8
A_HD = 64
AW = A_HEADS * A_HD
DN_H = 4
DN_HD = 128
DNW = DN_H * DN_HD
CH = 64
PAIR = 2 * CH
ZA = 3 * AW
ZD = 3 * DNW + DNW + 256
ZP = ZA + ZD
BD_BLK = (3 * DNW + DNW) // 128
IN_COLS = 3592
PATTERNS = ((128, 1), (512, 4), (2048, 16))
NORM_EPS = 1e-6
L2_EPS = 1e-6
ADAM_LR, ADAM_B1, ADAM_B2, ADAM_EPS, ADAM_WD, ADAM_STEP = 0.001, 0.9, 0.999, 1e-08, 0.01, 10
VMEM_LIMIT = 56 * 1024 * 1024
NEG = -1e30

BF = jnp.bfloat16
F32 = jnp.float32
NN = (((1,), (0,)), ((), ()))
NT = (((1,), (1,)), ((), ()))
TN = (((0,), (0,)), ((), ()))
HI = lax.Precision.HIGHEST
MESH_ID = pl.DeviceIdType.MESH
ANY = pl.BlockSpec(memory_space=pl.ANY)


def _cp():
    return pltpu.CompilerParams(vmem_limit_bytes=VMEM_LIMIT)


def _dg(a, b, dims):
    return lax.dot_general(a, b, dims, preferred_element_type=F32)


def _hdot(a, b):
    return lax.dot_general(a, b, NN, precision=HI, preferred_element_type=F32)


def _make_bdot(dims, da_dims, da_swap, db_dims, db_swap):
    @jax.custom_vjp
    def f(a, b):
        return _dg(a.astype(BF), b.astype(BF), dims)

    def fwd(a, b):
        return f(a, b), (a, b)

    def bwd(res, g):
        a, b = res
        gb, ab, bb = g.astype(BF), a.astype(BF), b.astype(BF)
        da = _dg(bb, gb, da_dims) if da_swap else _dg(gb, bb, da_dims)
        db = _dg(gb, ab, db_dims) if db_swap else _dg(ab, gb, db_dims)
        return da, db

    f.defvjp(fwd, bwd)
    return f


_bdot_nn = _make_bdot(NN, NT, False, TN, False)
_bdot_nt = _make_bdot(NT, NN, False, TN, True)
_bdot_tn = _make_bdot(TN, NT, True, NN, False)


def _iota(shape, dim):
    return lax.broadcasted_iota(jnp.int32, shape, dim)


def _col(x, idx):
    return jnp.sum(jnp.where(_iota(x.shape, 1) == idx, x, 0.0), axis=1, keepdims=True)


def _mm_nn(a, b, out_dtype, tm, tn, name):
    m, k = a.shape
    n = b.shape[1]

    def body(a_ref, b_ref, o_ref):
        o_ref[...] = _dg(a_ref[...], b_ref[...], NN).astype(out_dtype)

    return pl.pallas_call(
        body, grid=(m // tm, n // tn),
        in_specs=[pl.BlockSpec((tm, k), lambda i, j: (i, 0)), pl.BlockSpec((k, tn), lambda i, j: (0, j))],
        out_specs=pl.BlockSpec((tm, tn), lambda i, j: (i, j)),
        out_shape=jax.ShapeDtypeStruct((m, n), out_dtype), name=name, compiler_params=_cp())(a, b)


def _mm_nt(a, b, out_dtype, tm, tb, name):
    m, c = a.shape
    kb = b.shape[0]

    def body(a_ref, b_ref, o_ref):
        o_ref[...] = _dg(a_ref[...], b_ref[...], NT).astype(out_dtype)

    return pl.pallas_call(
        body, grid=(m // tm, kb // tb),
        in_specs=[pl.BlockSpec((tm, c), lambda i, j: (i, 0)), pl.BlockSpec((tb, c), lambda i, j: (j, 0))],
        out_specs=pl.BlockSpec((tm, tb), lambda i, j: (i, j)),
        out_shape=jax.ShapeDtypeStruct((m, kb), out_dtype), name=name, compiler_params=_cp())(a, b)


def _mm_tn(a, b, out_dtype, ta, tb, name):
    m, ka = a.shape
    nb = b.shape[1]

    def body(a_ref, b_ref, o_ref):
        o_ref[...] = _dg(a_ref[...], b_ref[...], TN).astype(out_dtype)

    return pl.pallas_call(
        body, grid=(ka // ta, nb // tb),
        in_specs=[pl.BlockSpec((m, ta), lambda i, j: (0, i)), pl.BlockSpec((m, tb), lambda i, j: (0, j))],
        out_specs=pl.BlockSpec((ta, tb), lambda i, j: (i, j)),
        out_shape=jax.ShapeDtypeStruct((ka, nb), out_dtype), name=name, compiler_params=_cp())(a, b)


def _mm_nn2(a1, a2, b1, b2, tm, tn, name):
    m, k = a1.shape
    n = b1.shape[1]

    def body(a1_ref, a2_ref, b1_ref, b2_ref, o_ref):
        o_ref[...] = _dg(a1_ref[...], b1_ref[...], NN) + _dg(a2_ref[...], b2_ref[...], NN)

    arow = pl.BlockSpec((tm, k), lambda i, j: (i, 0))
    bcol = pl.BlockSpec((k, tn), lambda i, j: (0, j))
    return pl.pallas_call(
        body, grid=(m // tm, n // tn), in_specs=[arow, arow, bcol, bcol],
        out_specs=pl.BlockSpec((tm, tn), lambda i, j: (i, j)),
        out_shape=jax.ShapeDtypeStruct((m, n), F32), name=name, compiler_params=_cp())(a1, a2, b1, b2)


def _mm_nt_swiglu(h, wu_t, gate, tm, tb, name):
    m, c = h.shape
    kb = wu_t.shape[0]

    def body(h_ref, w_ref, g_ref, u_ref, a_ref):
        hv = h_ref[...]
        for c0 in range(0, tb, 128):
            cs = slice(c0, c0 + 128)
            u = _dg(hv, w_ref[cs, :], NT)
            g = g_ref[:, cs].astype(F32)
            u_ref[:, cs] = u.astype(BF)
            a_ref[:, cs] = (g * jax.nn.sigmoid(g) * u).astype(BF)

    tile = pl.BlockSpec((tm, tb), lambda i, j: (i, j))
    return pl.pallas_call(
        body, grid=(m // tm, kb // tb),
        in_specs=[pl.BlockSpec((tm, c), lambda i, j: (i, 0)), pl.BlockSpec((tb, c), lambda i, j: (j, 0)), tile],
        out_specs=[tile, tile], out_shape=[jax.ShapeDtypeStruct((m, kb), BF)] * 2,
        name=name, compiler_params=_cp())(h, wu_t, gate)


def _mm_nt_dswiglu(dys, wd, gate, up, tm, tb, name):
    m, c = dys.shape
    kb = wd.shape[0]

    def body(d_ref, w_ref, g_ref, u_ref, dg_ref, du_ref):
        dv = d_ref[...]
        for c0 in range(0, tb, 128):
            cs = slice(c0, c0 + 128)
            da = _dg(dv, w_ref[cs, :], NT)
            g = g_ref[:, cs].astype(F32)
            u = u_ref[:, cs].astype(F32)
            s = jax.nn.sigmoid(g)
            dg_ref[:, cs] = (da * u * (s * (1.0 + g * (1.0 - s)))).astype(BF)
            du_ref[:, cs] = (da * (g * s)).astype(BF)

    tile = pl.BlockSpec((tm, tb), lambda i, j: (i, j))
    return pl.pallas_call(
        body, grid=(m // tm, kb // tb),
        in_specs=[pl.BlockSpec((tm, c), lambda i, j: (i, 0)), pl.BlockSpec((tb, c), lambda i, j: (j, 0)),
                  tile, tile],
        out_specs=[tile, tile], out_shape=[jax.ShapeDtypeStruct((m, kb), BF)] * 2,
        name=name, compiler_params=_cp())(dys, wd, gate, up)


def _rms_fwd(x, y, alpha, gain, name):
    tm = 512
    has_y = y is not None
    row = pl.BlockSpec((tm, D), lambda i: (i, 0))
    gspec = pl.BlockSpec((1, D), lambda i: (0, 0))

    def body(*refs):
        if has_y:
            x_ref, y_ref, g_ref, xo_ref, h_ref = refs
            xv = x_ref[...] + alpha * y_ref[...]
            xo_ref[...] = xv
        else:
            x_ref, g_ref, h_ref = refs
            xv = x_ref[...]
        r = lax.rsqrt(jnp.mean(xv * xv, axis=-1, keepdims=True) + NORM_EPS)
        h_ref[...] = (xv * r * g_ref[...]).astype(BF)

    if has_y:
        return pl.pallas_call(
            body, grid=(T // tm,), in_specs=[row, row, gspec], out_specs=[row, row],
            out_shape=[jax.ShapeDtypeStruct((T, D), F32), jax.ShapeDtypeStruct((T, D), BF)],
            name=name, compiler_params=_cp())(x, y, gain)
    h = pl.pallas_call(
        body, grid=(T // tm,), in_specs=[row, gspec], out_specs=row,
        out_shape=jax.ShapeDtypeStruct((T, D), BF), name=name, compiler_params=_cp())(x, gain)
    return x, h


def _rms_bwd(x, gain, dh, dres, alpha_out, name):
    tm = 512
    row = pl.BlockSpec((tm, D), lambda i: (i, 0))
    gspec = pl.BlockSpec((1, D), lambda i: (0, 0))

    def body(x_ref, g_ref, dh_ref, dres_ref, dx_ref, dxs_ref, dg_ref):
        i = pl.program_id(0)
        xv = x_ref[...]
        r = lax.rsqrt(jnp.mean(xv * xv, axis=-1, keepdims=True) + NORM_EPS)
        xh = xv * r
        dhv = dh_ref[...].astype(F32)
        part = jnp.sum(dhv * xh, axis=0, keepdims=True)

        @pl.when(i == 0)
        def _():
            dg_ref[...] = part

        @pl.when(i > 0)
        def _():
            dg_ref[...] += part

        dxh = dhv * g_ref[...]
        dx = r * (dxh - xh * jnp.mean(dxh * xh, axis=-1, keepdims=True)) + dres_ref[...]
        dx_ref[...] = dx
        dxs_ref[...] = (alpha_out * dx).astype(BF)

    return pl.pallas_call(
        body, grid=(T // tm,), in_specs=[row, gspec, row, row], out_specs=[row, row, gspec],
        out_shape=[jax.ShapeDtypeStruct((T, D), F32), jax.ShapeDtypeStruct((T, D), BF),
                   jax.ShapeDtypeStruct((1, D), F32)],
        name=name, compiler_params=_cp())(x, gain, dh, dres)


def _loss_bwd(x_prev, y, gain, target, name):
    tm = 512
    row = pl.BlockSpec((tm, D), lambda i: (i, 0))
    gspec = pl.BlockSpec((1, D), lambda i: (0, 0))
    lspec = pl.BlockSpec((8, 128), lambda i: (0, 0))

    def body(x_ref, y_ref, g_ref, t_ref, dx_ref, dxs_ref, dg_ref, loss_ref):
        i = pl.program_id(0)
        xv = x_ref[...] + 0.5 * y_ref[...]
        r = lax.rsqrt(jnp.mean(xv * xv, axis=-1, keepdims=True) + NORM_EPS)
        xh = xv * r
        diff = xh * g_ref[...] - t_ref[...]
        lpart = 0.5 * jnp.sum(jnp.mean(diff * diff, axis=-1, keepdims=True), axis=0, keepdims=True)
        dy = diff * (1.0 / D)
        part = jnp.sum(dy * xh, axis=0, keepdims=True)

        @pl.when(i == 0)
        def _():
            dg_ref[...] = part
            loss_ref[...] = jnp.broadcast_to(lpart, (8, 128))

        @pl.when(i > 0)
        def _():
            dg_ref[...] += part
            loss_ref[...] += jnp.broadcast_to(lpart, (8, 128))

        dxh = dy * g_ref[...]
        dx = r * (dxh - xh * jnp.mean(dxh * xh, axis=-1, keepdims=True))
        dx_ref[...] = dx
        dxs_ref[...] = (0.5 * dx).astype(BF)

    return pl.pallas_call(
        body, grid=(T // tm,), in_specs=[row, row, gspec, row], out_specs=[row, row, gspec, lspec],
        out_shape=[jax.ShapeDtypeStruct((T, D), F32), jax.ShapeDtypeStruct((T, D), BF),
                   jax.ShapeDtypeStruct((1, D), F32), jax.ShapeDtypeStruct((8, 128), F32)],
        name=name, compiler_params=_cp())(x_prev, y, gain, target)


def _ffn_fwd(x_prev, y_prev, alpha, gain, need, tag):
    x_in, h = _rms_fwd(x_prev, y_prev, alpha, gain, tag + "_norm")
    wg = need("g", h)
    gate = _mm_nt(h, wg, BF, 1024, 1408, tag + "_gate")
    wu = need("u", gate)
    up, act = _mm_nt_swiglu(h, wu, gate, 1024, 1408, tag + "_up_act")
    wd = need("d", up)
    y = _mm_nn(act, wd, F32, 1024, 1024, tag + "_down")
    return x_in, y, (h, gate, up, act, wg, wu, wd)


def _ffn_bwd(x_in, gain, saved, dxo, dys, alpha_out, emit, tag):
    h, gate, up, act, wg, wu, wd = saved
    zero = emit("d", _mm_tn(act, dys, BF, 256, 1024, tag + "_dwd"))
    dgate, dup = _mm_nt_dswiglu(dys, wd, gate, up, 1024, 1408, tag + "_dact")
    zero = zero + emit("g", _mm_tn(dgate, h, BF, 256, 1024, tag + "_dwg"))
    zero = zero + emit("u", _mm_tn(dup, h, BF, 256, 1024, tag + "_dwu"))
    dh = _mm_nn2(dgate, dup, wg, wu, 512, 512, tag + "_dh")
    dx, dxs, dgain = _rms_bwd(x_in, gain + zero, dh, dxo, alpha_out, tag + "_dnorm")
    return dx, dxs, dgain


SLAB = 2048
N_SLAB = T // SLAB
N_PAIR = A_HEADS // 2


def _pair_masks():
    lane = _iota((128, 128), 1)
    return lane < A_HD, lane >= A_HD


def _slope_table():
    h = 2 * jnp.arange(N_PAIR)[:, None] + jnp.minimum(jnp.arange(8), 1)[None, :]
    return jnp.broadcast_to((2.0 ** (-(h + 1).astype(F32)))[:, :, None], (N_PAIR, 8, 128))


def _rows(ref, start, d):
    if d == 1:
        return ref[pl.ds(start, 128), :]
    return ref[pl.ds(start, 128, stride=d), :]


def _put_rows(ref, start, d, val):
    if d == 1:
        ref[pl.ds(start, 128), :] = val
    else:
        ref[pl.ds(start, 128, stride=d), :] = val


def _units(d):
    return [(r, b, r + 128 * d * b) for r in range(d) for b in range(SLAB // (128 * d))]


def _band(d, prev_valid):
    qi = _iota((128, 256), 0)
    kj = _iota((128, 256), 1)
    steps = qi + 128 - kj
    valid = (steps >= 0) & (steps <= 128) & (prev_valid | (kj >= 128))
    return valid, (steps * d).astype(F32)


def _attn_fwd(z_at, name):
    def body(sl_ref, q_ref, kc_ref, kp_ref, vc_ref, vp_ref, of_ref, ob_ref, lse_ref, m_s, l_s, a_s):
        n = pl.program_id(1)
        lo, hi = _pair_masks()
        slopes = (sl_ref[0, 0:1, 0:1], sl_ref[0, 1:2, 0:1])

        def unit(d, start, b, first, carry):
            q = _rows(q_ref, start, d).astype(BF)
            kcur, vcur = _rows(kc_ref, start, d).astype(BF), _rows(vc_ref, start, d).astype(BF)
            if b > 0:
                (kprev, vprev), prev_valid = carry, True
            else:
                pstart = start + SLAB - 128 * d
                kprev, vprev = _rows(kp_ref, pstart, d).astype(BF), _rows(vp_ref, pstart, d).astype(BF)
                prev_valid = n > 0
            kcat = jnp.concatenate([kprev, kcur], axis=0)
            vcat = jnp.concatenate([vprev, vcur], axis=0)
            valid, dist = _band(d, prev_valid)
            ms, ls, pvs = [], [], []
            for e in range(2):
                qm = jnp.where(lo if e == 0 else hi, q, jnp.zeros_like(q))
                s = _dg(qm, kcat, NT) * (A_HD ** -0.5) - slopes[e] * dist
                s = jnp.where(valid, s, NEG)
                m = jnp.max(s, axis=1, keepdims=True)
                p = jnp.exp(s - m)
                ms.append(m)
                ls.append(jnp.sum(p, axis=1, keepdims=True))
                pvs.append(_dg(p.astype(BF), vcat, NN))
            m_u = jnp.where(lo, ms[0], ms[1])
            l_u = jnp.where(lo, ls[0], ls[1])
            a_u = jnp.where(lo, pvs[0], pvs[1])
            if first:
                m_n, l_n, a_n = m_u, l_u, a_u
            else:
                m_o = _rows(m_s, start, d)
                m_n = jnp.maximum(m_o, m_u)
                c_o = jnp.exp(m_o - m_n)
                c_u = jnp.exp(m_u - m_n)
                l_n = _rows(l_s, start, d) * c_o + l_u * c_u
                a_n = _rows(a_s, start, d) * c_o + a_u * c_u
            _put_rows(m_s, start, d, m_n)
            _put_rows(l_s, start, d, l_n)
            _put_rows(a_s, start, d, a_n)
            return kcur, vcur

        for pi, (_, d) in enumerate(PATTERNS):
            carry = None
            for r, b, start in _units(d):
                carry = unit(d, start, b, pi == 0, carry)
        l = l_s[...]
        out = a_s[...] / l
        of_ref[...] = out
        ob_ref[...] = out.astype(BF)
        lse_ref[...] = m_s[...] + jnp.log(l)

    cur = lambda c: pl.BlockSpec((SLAB, 128), lambda j, n: (n, c * N_PAIR + j))
    prv = lambda c: pl.BlockSpec((SLAB, 128), lambda j, n: (jnp.maximum(n - 1, 0), c * N_PAIR + j))
    out = pl.BlockSpec((SLAB, 128), lambda j, n: (n, j))
    return pl.pallas_call(
        body, grid=(N_PAIR, N_SLAB),
        in_specs=[pl.BlockSpec((1, 8, 128), lambda j, n: (j, 0, 0)), cur(0), cur(1), prv(1), cur(2), prv(2)],
        out_specs=[out, out, out],
        out_shape=[jax.ShapeDtypeStruct((T, AW), F32), jax.ShapeDtypeStruct((T, AW), BF),
                   jax.ShapeDtypeStruct((T, AW), F32)],
        scratch_shapes=[pltpu.VMEM((SLAB, 128), F32)] * 3,
        name=name, compiler_params=_cp())(_slope_table(), z_at, z_at, z_at, z_at, z_at)


def _attn_bwd(z_at, dout, out, lse, name):
    def body(sl_ref, q_ref, kc_ref, kp_ref, vc_ref, vp_ref, do_ref, o_ref, lse_ref, dq_ref, dk_ref, dv_ref,
             dq_s, dk_s, dv_s, ck_s, cv_s):
        step = pl.program_id(1)
        n = N_SLAB - 1 - step
        lo, hi = _pair_masks()
        slopes = (sl_ref[0, 0:1, 0:1], sl_ref[0, 1:2, 0:1])

        @pl.when(step == 0)
        def _():
            ck_s[...] = jnp.zeros_like(ck_s)
            cv_s[...] = jnp.zeros_like(cv_s)

        dk_s[...] = ck_s[...]
        dv_s[...] = cv_s[...]
        ck_s[...] = jnp.zeros_like(ck_s)
        cv_s[...] = jnp.zeros_like(cv_s)

        def add_rows(ref, start, d, val):
            _put_rows(ref, start, d, _rows(ref, start, d) + val)

        def unit(d, start, b, first, carry):
            q = _rows(q_ref, start, d).astype(BF)
            do_f = _rows(do_ref, start, d)
            do = do_f.astype(BF)
            prod = do_f * _rows(o_ref, start, d)
            lse_u = _rows(lse_ref, start, d)
            kcur, vcur = _rows(kc_ref, start, d).astype(BF), _rows(vc_ref, start, d).astype(BF)
            if b > 0:
                (kprev, vprev), prev_valid = carry, True
            else:
                pstart = start + SLAB - 128 * d
                kprev, vprev = _rows(kp_ref, pstart, d).astype(BF), _rows(vp_ref, pstart, d).astype(BF)
                prev_valid = n > 0
            kcat = jnp.concatenate([kprev, kcur], axis=0)
            vcat = jnp.concatenate([vprev, vcur], axis=0)
            valid, dist = _band(d, prev_valid)
            masks = (lo, hi)
            qms = [jnp.where(msk, q, jnp.zeros_like(q)) for msk in masks]
            doms = [jnp.where(msk, do, jnp.zeros_like(do)) for msk in masks]
            deltas = [jnp.sum(jnp.where(msk, prod, 0.0), axis=1, keepdims=True) for msk in masks]
            ss = [_dg(qm, kcat, NT) * (A_HD ** -0.5) - sl * dist for qm, sl in zip(qms, slopes)]
            dps = [_dg(dom, vcat, NT) for dom in doms]
            ps = [jnp.where(valid, jnp.exp(jnp.where(valid, s, NEG) - lse_u[:, 64 * e:64 * e + 1]), 0.0)
                  for e, s in enumerate(ss)]
            dss = [(p * (dp - delta)).astype(BF) for p, dp, delta in zip(ps, dps, deltas)]
            pbs = [p.astype(BF) for p in ps]
            dqs = [_dg(ds, kcat, NN) for ds in dss]
            dkc = (_dg(dss[0], qms[0], TN) + _dg(dss[1], qms[1], TN)) * (A_HD ** -0.5)
            dvc = _dg(pbs[0], doms[0], TN) + _dg(pbs[1], doms[1], TN)
            dq_u = jnp.where(lo, dqs[0], dqs[1]) * (A_HD ** -0.5)
            if first:
                _put_rows(dq_s, start, d, dq_u)
            else:
                add_rows(dq_s, start, d, dq_u)
            add_rows(dk_s, start, d, dkc[128:])
            add_rows(dv_s, start, d, dvc[128:])
            if b > 0:
                add_rows(dk_s, start - 128 * d, d, dkc[:128])
                add_rows(dv_s, start - 128 * d, d, dvc[:128])
            else:
                pstart = start + SLAB - 128 * d
                add_rows(ck_s, pstart, d, dkc[:128])
                add_rows(cv_s, pstart, d, dvc[:128])
            return kcur, vcur

        for pi, (_, d) in enumerate(PATTERNS):
            carry = None
            for r, b, start in _units(d):
                carry = unit(d, start, b, pi == 0, carry)
        dq_ref[...] = dq_s[...].astype(BF)
        dk_ref[...] = dk_s[...].astype(BF)
        dv_ref[...] = dv_s[...].astype(BF)

    rev = lambda n: N_SLAB - 1 - n
    cur = lambda c: pl.BlockSpec((SLAB, 128), lambda j, n: (rev(n), c * N_PAIR + j))
    prv = lambda c: pl.BlockSpec((SLAB, 128), lambda j, n: (jnp.maximum(rev(n) - 1, 0), c * N_PAIR + j))
    one = pl.BlockSpec((SLAB, 128), lambda j, n: (rev(n), j))
    return pl.pallas_call(
        body, grid=(N_PAIR, N_SLAB),
        in_specs=[pl.BlockSpec((1, 8, 128), lambda j, n: (j, 0, 0)), cur(0), cur(1), prv(1), cur(2), prv(2),
                  one, one, one],
        out_specs=[one, one, one], out_shape=[jax.ShapeDtypeStruct((T, AW), BF)] * 3,
        scratch_shapes=[pltpu.VMEM((SLAB, 128), F32)] * 5,
        name=name, compiler_params=_cp())(_slope_table(), z_at, z_at, z_at, z_at, z_at, dout, out, lse)


def _silu(x):
    return x * jax.nn.sigmoid(x)


def _qk_math(c):
    s = _silu(c)
    return s * lax.rsqrt(jnp.sum(s * s, axis=-1, keepdims=True) + L2_EPS)


def _softplus(x):
    return jnp.maximum(x, 0.0) + jnp.log(1.0 + jnp.exp(-jnp.abs(x)))


def _gate_math(bd, alog_row, dtb_row):
    rows = bd.shape[0]
    lane = _iota(bd.shape, 1)
    beta = jax.nn.sigmoid(bd)
    g = jnp.where((lane >= DN_H) & (lane < 2 * DN_H), -jnp.exp(alog_row) * _softplus(bd + dtb_row), 0.0)
    ri = _iota((rows, rows), 0)
    ci = _iota((rows, rows), 1)
    same = (ri // CH) == (ci // CH)
    li = _iota((128, 128), 0)
    lj = _iota((128, 128), 1)
    to_next_group = jnp.where((lj == li + DN_H) & (li >= DN_H) & (li < 2 * DN_H), 1.0, 0.0)
    gc = _hdot(jnp.where(same & (ci <= ri), 1.0, 0.0), g)
    glast = _hdot(_hdot(jnp.where(same, 1.0, 0.0), g), to_next_group)
    return jnp.where(lane < DN_H, beta, 0.0) + gc + glast


def _shift_down(cur, halo, s):
    if s == 0:
        return cur
    rolled = pltpu.roll(cur, s, 0)
    hr = pltpu.roll(halo, s, 0)
    head = jnp.where(_iota(hr.shape, 0) < s, hr, rolled[:8])
    return jnp.concatenate([head, rolled[8:]], axis=0)


def _shift_up(cur, halo, s):
    if s == 0:
        return cur
    rows = cur.shape[0]
    rolled = pltpu.roll(cur, rows - s, 0)
    hr = pltpu.roll(halo, 8 - s, 0)
    tail = jnp.where(_iota(hr.shape, 0) >= 8 - s, hr, rolled[rows - 8:])
    return jnp.concatenate([rolled[:rows - 8], tail], axis=0)


def _dn_pre_fwd(z_dn, conv_w8, alog_row, dtb_row, name):
    tm = 256
    wq = 3 * DNW

    def body(raw_ref, halo_ref, bd_ref, w_ref, al_ref, dt_ref, conv_ref, qkv_ref, bg_ref):
        i = pl.program_id(0)
        cur = raw_ref[...]
        halo = jnp.where(i > 0, halo_ref[...], 0.0)
        w = w_ref[...]
        conv = jnp.zeros((tm, wq), F32)
        for j in range(4):
            conv = conv + _shift_down(cur, halo, 3 - j) * w[j:j + 1, :]
        conv_ref[...] = conv
        for blk in range(3 * DN_H):
            sl = slice(128 * blk, 128 * blk + 128)
            c = conv[:, sl]
            qkv_ref[:, sl] = _qk_math(c) if blk < 2 * DN_H else _silu(c)
        bg_ref[...] = _gate_math(bd_ref[...], al_ref[...], dt_ref[...])

    one = pl.BlockSpec((1, 128), lambda i: (0, 0))
    return pl.pallas_call(
        body, grid=(T // tm,),
        in_specs=[pl.BlockSpec((tm, wq), lambda i: (i, 0)),
                  pl.BlockSpec((8, wq), lambda i: (jnp.maximum(i * (tm // 8) - 1, 0), 0)),
                  pl.BlockSpec((tm, 128), lambda i: (i, BD_BLK)),
                  pl.BlockSpec((8, wq), lambda i: (0, 0)), one, one],
        out_specs=[pl.BlockSpec((tm, wq), lambda i: (i, 0)), pl.BlockSpec((tm, wq), lambda i: (i, 0)),
                   pl.BlockSpec((tm, 128), lambda i: (i, 0))],
        out_shape=[jax.ShapeDtypeStruct((T, wq), F32), jax.ShapeDtypeStruct((T, wq), F32),
                   jax.ShapeDtypeStruct((T, 128), F32)],
        name=name, compiler_params=_cp())(z_dn, z_dn, z_dn, conv_w8, alog_row, dtb_row)


def _dn_pre_bwd(conv, z_dn, alog_row, dtb_row, dqn, dkn, dvn, dbg, name):
    tm = 256
    wq = 3 * DNW

    def body(conv_ref, bd_ref, al_ref, dt_ref, dq_ref, dk_ref, dv_ref, dbg_ref,
             dconv_ref, dbd_ref, dal_ref, ddt_ref):
        i = pl.program_id(0)
        for blk in range(3 * DN_H):
            sl = slice(128 * blk, 128 * blk + 128)
            src = (dq_ref, dk_ref, dv_ref)[blk // DN_H]
            ct = src[:, 128 * (blk % DN_H):128 * (blk % DN_H) + 128]
            fn = _qk_math if blk < 2 * DN_H else _silu
            _, vjp = jax.vjp(fn, conv_ref[:, sl])
            dconv_ref[:, sl] = vjp(ct)[0]
        _, vjp = jax.vjp(_gate_math, bd_ref[...], al_ref[...], dt_ref[...])
        dbd, dal, ddt = vjp(dbg_ref[...])
        dbd_ref[...] = dbd

        @pl.when(i == 0)
        def _():
            dal_ref[...] = dal
            ddt_ref[...] = ddt

        @pl.when(i > 0)
        def _():
            dal_ref[...] += dal
            ddt_ref[...] += ddt

    one = pl.BlockSpec((1, 128), lambda i: (0, 0))
    row = pl.BlockSpec((tm, wq), lambda i: (i, 0))
    hd = pl.BlockSpec((tm, DNW), lambda i: (i, 0))
    st = pl.BlockSpec((tm, 128), lambda i: (i, 0))
    return pl.pallas_call(
        body, grid=(T // tm,),
        in_specs=[row, pl.BlockSpec((tm, 128), lambda i: (i, BD_BLK)), one, one, hd, hd, hd, st],
        out_specs=[row, st, one, one],
        out_shape=[jax.ShapeDtypeStruct((T, wq), F32), jax.ShapeDtypeStruct((T, 128), F32),
                   jax.ShapeDtypeStruct((1, 128), F32), jax.ShapeDtypeStruct((1, 128), F32)],
        name=name, compiler_params=_cp())(conv, z_dn, alog_row, dtb_row, dqn, dkn, dvn, dbg)


def _dn_conv_bwd(dconv, z_dn, conv_w8, name):
    tm = 256
    wq = 3 * DNW
    last = T // tm - 1

    def body(dc_ref, dcn_ref, raw_ref, halo_ref, w_ref, draw_ref, dw_ref):
        i = pl.program_id(0)
        dc = dc_ref[...]
        nxt = jnp.where(i < last, dcn_ref[...], 0.0)
        cur = raw_ref[...]
        halo = jnp.where(i > 0, halo_ref[...], 0.0)
        w = w_ref[...]
        draw = jnp.zeros((tm, wq), F32)
        rows = []
        for j in range(4):
            draw = draw + _shift_up(dc, nxt, 3 - j) * w[j:j + 1, :]
            rows.append(jnp.sum(dc * _shift_down(cur, halo, 3 - j), axis=0, keepdims=True))
        draw_ref[...] = draw
        part = jnp.concatenate(rows + [jnp.zeros((4, wq), F32)], axis=0)

        @pl.when(i == 0)
        def _():
            dw_ref[...] = part

        @pl.when(i > 0)
        def _():
            dw_ref[...] += part

    row = pl.BlockSpec((tm, wq), lambda i: (i, 0))
    return pl.pallas_call(
        body, grid=(T // tm,),
        in_specs=[row, pl.BlockSpec((8, wq), lambda i: (jnp.minimum((i + 1) * (tm // 8), T // 8 - 1), 0)),
                  row, pl.BlockSpec((8, wq), lambda i: (jnp.maximum(i * (tm // 8) - 1, 0), 0)),
                  pl.BlockSpec((8, wq), lambda i: (0, 0))],
        out_specs=[row, pl.BlockSpec((8, wq), lambda i: (0, 0))],
        out_shape=[jax.ShapeDtypeStruct((T, wq), F32), jax.ShapeDtypeStruct((8, wq), F32)],
        name=name, compiler_params=_cp())(dconv, dconv, z_dn, z_dn, conv_w8)


def _h3(a, b, dims=NN):
    return lax.dot_general(a, b, dims, precision=lax.Precision.HIGH, preferred_element_type=F32)


@jax.custom_vjp
def _inverse_given(a_mat, tinv):
    return tinv


def _inverse_given_fwd(a_mat, tinv):
    return tinv, tinv


def _inverse_given_bwd(tinv, g):
    return -_h3(tinv, _h3(g, tinv, NT), TN), jnp.zeros_like(tinv)


_inverse_given.defvjp(_inverse_given_fwd, _inverse_given_bwd)


def _prep_head(q, k, v, bgc, h):
    beta = _col(bgc, h)
    gc = jnp.broadcast_to(_col(bgc, DN_H + h), (PAIR, 128))
    glast = jnp.broadcast_to(_col(bgc, 2 * DN_H + h), (PAIR, 128))
    ri = _iota((PAIR, PAIR), 0)
    ci = _iota((PAIR, PAIR), 1)
    same = (ri // CH) == (ci // CH)
    causal = same & (ci <= ri)
    strict = same & (ci < ri)
    eye = ri == ci
    gc_cols = _hdot(jnp.ones((PAIR, PAIR), F32), jnp.where(eye, gc, 0.0))
    decay = jnp.exp(jnp.where(causal, gc - gc_cols, NEG))
    egc = jnp.exp(gc)
    kb = k * beta
    a_mat = jnp.where(strict, _bdot_nt(kb, k) * decay, 0.0)
    qs = q * (DN_HD ** -0.5)
    attn = jnp.where(causal, _bdot_nt(qs, k) * decay, 0.0)
    return a_mat, (v * beta, kb * egc, qs * egc, k * jnp.exp(glast - gc), attn, jnp.exp(glast))


def _prep_tail(tinv, ctx):
    vb, kbe, qg, kdec, attn, decb = ctx
    return _h3(tinv, vb), _h3(tinv, kbe), qg, kdec, attn, decb


def _inverses(a_mats):
    eye = jnp.where(_iota((PAIR, PAIR), 0) == _iota((PAIR, PAIR), 1), 1.0, 0.0)
    ps = [-a for a in a_mats]
    tinvs = [eye + p for p in ps]
    for _ in range(5):
        ps = [_h3(p, p) for p in ps]
        tinvs = [t + _h3(t, p) for t, p in zip(tinvs, ps)]
    return tinvs


def _prep_math(q, k, v, bgc, h, tinv_saved):
    a_mat, ctx = _prep_head(q, k, v, bgc, h)
    return _prep_tail(_inverse_given(a_mat, tinv_saved), ctx)


def _dn_prep_fwd(qkv, bg, name):
    rows = 512
    hd = lambda off: pl.BlockSpec((rows, 128), lambda g, h: (g, off + h))
    out = pl.BlockSpec((rows, 128), lambda g, h: (g, h))

    def body(q_ref, k_ref, v_ref, bg_ref, *outs):
        h = pl.program_id(1)
        spans = [slice(PAIR * pr, PAIR * pr + PAIR) for pr in range(rows // PAIR)]
        heads = [_prep_head(q_ref[rs, :], k_ref[rs, :], v_ref[rs, :], bg_ref[rs, :], h) for rs in spans]
        tinvs = _inverses([a for a, _ in heads])
        for rs, tinv, (_, ctx) in zip(spans, tinvs, heads):
            for o_ref, val in zip(outs, _prep_tail(tinv, ctx) + (tinv,)):
                o_ref[rs, :] = val

    return pl.pallas_call(
        body, grid=(T // rows, DN_H),
        in_specs=[hd(0), hd(DN_H), hd(2 * DN_H), pl.BlockSpec((rows, 128), lambda g, h: (g, 0))],
        out_specs=[out] * 7, out_shape=[jax.ShapeDtypeStruct((T, DNW), F32)] * 7,
        name=name, compiler_params=_cp())(qkv, qkv, qkv, bg)


def _dn_prep_bwd(qkv, bg, tinv, cts, name):
    rows = 512
    hd = lambda off: pl.BlockSpec((rows, 128), lambda g, h: (g, off + h))
    out = pl.BlockSpec((rows, 128), lambda g, h: (g, h))
    st = pl.BlockSpec((rows, 128), lambda g, h: (g, 0))

    def body(q_ref, k_ref, v_ref, bg_ref, ti_ref, c0, c1, c2, c3, c4, c5, dq_ref, dk_ref, dv_ref, dbg_ref):
        h = pl.program_id(1)
        spans = [slice(PAIR * pr, PAIR * pr + PAIR) for pr in range(rows // PAIR)]
        tis = [ti_ref[rs, :] for rs in spans]

        def joint(qs, ks, vs, bs):
            heads = [_prep_head(q, k, v, b, h) for q, k, v, b in zip(qs, ks, vs, bs)]
            return [_prep_tail(_inverse_given(a, ti), ctx) for (a, ctx), ti in zip(heads, tis)]

        _, vjp = jax.vjp(joint, *[[r[rs, :] for rs in spans] for r in (q_ref, k_ref, v_ref, bg_ref)])
        dqs, dks, dvs, dbs = vjp([tuple(c[rs, :] for c in (c0, c1, c2, c3, c4, c5)) for rs in spans])
        for rs, dq, dk, dv in zip(spans, dqs, dks, dvs):
            dq_ref[rs, :] = dq
            dk_ref[rs, :] = dk
            dv_ref[rs, :] = dv
        dbg_all = jnp.concatenate(dbs, axis=0)

        @pl.when(h == 0)
        def _():
            dbg_ref[...] = dbg_all

        @pl.when(h > 0)
        def _():
            dbg_ref[...] += dbg_all

    return pl.pallas_call(
        body, grid=(T // rows, DN_H),
        in_specs=[hd(0), hd(DN_H), hd(2 * DN_H), st] + [out] * 7,
        out_specs=[out, out, out, st],
        out_shape=[jax.ShapeDtypeStruct((T, DNW), F32)] * 3 + [jax.ShapeDtypeStruct((T, 128), F32)],
        name=name, compiler_params=_cp())(qkv, qkv, qkv, bg, tinv, *cts)


def _step_math(s, u, w, qg, kdec, attn, decb, sub):
    vnew = u - _bdot_nn(w, s)
    z = jnp.zeros((CH, 128), F32)
    vfull = jnp.concatenate([vnew, z] if sub == 0 else [z, vnew], axis=0)
    o = _bdot_nn(qg, s) + _bdot_nn(attn, vfull)
    dec = jnp.sum(decb, axis=0, keepdims=True) * (1.0 / CH)
    return s * dec + _bdot_tn(kdec, vnew), o


def _dn_scan_fwd(prep, name):
    npair = T // PAIR
    row = pl.BlockSpec((PAIR, DNW), lambda p: (p, 0))

    def body(u_ref, w_ref, qg_ref, kd_ref, at_ref, db_ref, o_ref, ss_ref, s_ref):
        @pl.when(pl.program_id(0) == 0)
        def _():
            s_ref[...] = jnp.zeros_like(s_ref)

        states = [s_ref[h] for h in range(DN_H)]
        for sub in range(2):
            rs = slice(CH * sub, CH * sub + CH)
            for h in range(DN_H):
                ls = slice(128 * h, 128 * h + 128)
                ss_ref[sub, h] = states[h]
                states[h], o = _step_math(states[h], u_ref[rs, ls], w_ref[rs, ls], qg_ref[rs, ls],
                                          kd_ref[rs, ls], at_ref[rs, ls], db_ref[rs, ls], sub)
                o_ref[rs, ls] = o
        for h in range(DN_H):
            s_ref[h] = states[h]

    return pl.pallas_call(
        body, grid=(npair,), in_specs=[row] * 6,
        out_specs=[row, pl.BlockSpec((2, DN_H, 128, 128), lambda p: (p, 0, 0, 0))],
        out_shape=[jax.ShapeDtypeStruct((T, DNW), F32), jax.ShapeDtypeStruct((T // CH, DN_H, 128, 128), F32)],
        scratch_shapes=[pltpu.VMEM((DN_H, 128, 128), F32)],
        name=name, compiler_params=_cp())(*prep)


def _dn_scan_bwd(prep, states, do, name):
    npair = T // PAIR
    row = pl.BlockSpec((PAIR, DNW), lambda p: (npair - 1 - p, 0))

    def body(u_ref, w_ref, qg_ref, kd_ref, at_ref, db_ref, ss_ref, do_ref, *rest):
        outs, ds_ref = rest[:6], rest[6]

        @pl.when(pl.program_id(0) == 0)
        def _():
            ds_ref[...] = jnp.zeros_like(ds_ref)

        dss = [ds_ref[h] for h in range(DN_H)]
        for sub in (1, 0):
            rs = slice(CH * sub, CH * sub + CH)
            for h in range(DN_H):
                ls = slice(128 * h, 128 * h + 128)
                args = (ss_ref[sub, h],) + tuple(r[rs, ls] for r in (u_ref, w_ref, qg_ref, kd_ref, at_ref, db_ref))
                _, vjp = jax.vjp(functools.partial(_step_math, sub=sub), *args)
                cts = vjp((dss[h], do_ref[rs, ls]))
                dss[h] = cts[0]
                for o_ref, val in zip(outs, cts[1:]):
                    o_ref[rs, ls] = val
        for h in range(DN_H):
            ds_ref[h] = dss[h]

    return pl.pallas_call(
        body, grid=(npair,),
        in_specs=[row] * 6 + [pl.BlockSpec((2, DN_H, 128, 128), lambda p: (npair - 1 - p, 0, 0, 0)), row],
        out_specs=[row] * 6, out_shape=[jax.ShapeDtypeStruct((T, DNW), F32)] * 6,
        scratch_shapes=[pltpu.VMEM((DN_H, 128, 128), F32)],
        name=name, compiler_params=_cp())(*prep, states, do)


def _post_math(o, gate, wrow):
    return o * lax.rsqrt(jnp.mean(o * o, axis=-1, keepdims=True) + NORM_EPS) * wrow * _silu(gate)


def _dn_post_fwd(o, z_dn, dn_norm, name):
    tm = 512
    row = pl.BlockSpec((tm, DNW), lambda i: (i, 0))

    def body(o_ref, g_ref, w_ref, y_ref):
        for h in range(DN_H):
            ls = slice(128 * h, 128 * h + 128)
            y_ref[:, ls] = _post_math(o_ref[:, ls], g_ref[:, ls], w_ref[...]).astype(BF)

    return pl.pallas_call(
        body, grid=(T // tm,),
        in_specs=[row, pl.BlockSpec((tm, DNW), lambda i: (i, 3)), pl.BlockSpec((1, 128), lambda i: (0, 0))],
        out_specs=row, out_shape=jax.ShapeDtypeStruct((T, DNW), BF),
        name=name, compiler_params=_cp())(o, z_dn, dn_norm)


def _dn_post_bwd(o, z_dn, dn_norm, dy, name):
    tm = 512
    row = pl.BlockSpec((tm, DNW), lambda i: (i, 0))
    one = pl.BlockSpec((1, 128), lambda i: (0, 0))

    def body(o_ref, g_ref, w_ref, dy_ref, do_ref, dg_ref, dw_ref):
        i = pl.program_id(0)
        dw = jnp.zeros((1, 128), F32)
        for h in range(DN_H):
            ls = slice(128 * h, 128 * h + 128)
            _, vjp = jax.vjp(_post_math, o_ref[:, ls], g_ref[:, ls], w_ref[...])
            do, dg, dwh = vjp(dy_ref[:, ls].astype(F32))
            do_ref[:, ls] = do
            dg_ref[:, ls] = dg
            dw = dw + dwh

        @pl.when(i == 0)
        def _():
            dw_ref[...] = dw

        @pl.when(i > 0)
        def _():
            dw_ref[...] += dw

    return pl.pallas_call(
        body, grid=(T // tm,),
        in_specs=[row, pl.BlockSpec((tm, DNW), lambda i: (i, 3)), one, pl.BlockSpec((tm, DNW), lambda i: (i, 1))],
        out_specs=[row, row, one],
        out_shape=[jax.ShapeDtypeStruct((T, DNW), F32), jax.ShapeDtypeStruct((T, DNW), F32),
                   jax.ShapeDtypeStruct((1, 128), F32)],
        name=name, compiler_params=_cp())(o, z_dn, dn_norm, dy)


def _dz_dn_assemble(draw, dgate, dbd, name):
    tm = 512

    def body(a_ref, b_ref, c_ref, o_ref):
        o_ref[:, :3 * DNW] = a_ref[...].astype(BF)
        o_ref[:, 3 * DNW:4 * DNW] = b_ref[...].astype(BF)
        o_ref[:, 4 * DNW:4 * DNW + 128] = c_ref[...].astype(BF)
        o_ref[:, 4 * DNW + 128:] = jnp.zeros((tm, 128), BF)

    return pl.pallas_call(
        body, grid=(T // tm,),
        in_specs=[pl.BlockSpec((tm, 3 * DNW), lambda i: (i, 0)), pl.BlockSpec((tm, DNW), lambda i: (i, 0)),
                  pl.BlockSpec((tm, 128), lambda i: (i, 0))],
        out_specs=pl.BlockSpec((tm, ZD), lambda i: (i, 0)),
        out_shape=jax.ShapeDtypeStruct((T, ZD), BF), name=name, compiler_params=_cp())(draw, dgate, dbd)


HBM = pl.BlockSpec(memory_space=pltpu.HBM)
SEM = pl.BlockSpec(memory_space=pltpu.SEMAPHORE)
EFFECT = pltpu.SideEffectType.DATAFLOW_SIDE_EFFECTING
N_PEER = N_DEV - 1


def _peers(x, y, c):
    return [(k, (x ^ (k >> 2), y ^ ((k >> 1) & 1), c ^ (k & 1))) for k in (1, 2, 4, 3, 5, 6, 7)]


def _exchange_copy(ins, lands, ssems, rsems, scatter, t, k, pos, me):
    px, py, pc = pos
    src = ins[t].at[4 * px + 2 * py + pc] if scatter else ins[t]
    return pltpu.make_async_remote_copy(
        src_ref=src, dst_ref=lands[t].at[me], send_sem=ssems[t].at[k - 1], recv_sem=rsems[t].at[k - 1],
        device_id=pos, device_id_type=MESH_ID)


def _xstart(bufs, scatter, name):
    nt = len(bufs)
    lands = [lax.empty((N_DEV,) + tuple(b.shape[1:] if scatter else b.shape), b.dtype) for b in bufs]

    def body(*refs):
        ins, lnd = refs[:nt], refs[nt:2 * nt]
        ssems, rsems = refs[2 * nt:3 * nt], refs[3 * nt:4 * nt]
        token = refs[-1]
        x, y, c = lax.axis_index("x"), lax.axis_index("y"), lax.axis_index("c")
        me = 4 * x + 2 * y + c
        for t in range(nt):
            for k, pos in _peers(x, y, c):
                _exchange_copy(ins, lnd, ssems, rsems, scatter, t, k, pos, me).start()
        token[...] = jnp.zeros_like(token)

    both = list(bufs) + lands
    res = pl.pallas_call(
        body, name=name,
        out_shape=[pltpu.SemaphoreType.DMA((N_PEER,))] * (2 * nt)
        + [pltpu.HBM(b.shape, b.dtype) for b in both] + [jax.ShapeDtypeStruct((8, 128), F32)],
        in_specs=[HBM] * (2 * nt),
        out_specs=[SEM] * (2 * nt) + [HBM] * (2 * nt) + [pl.BlockSpec(memory_space=pltpu.VMEM)],
        input_output_aliases={i: 2 * nt + i for i in range(2 * nt)},
        compiler_params=pltpu.CompilerParams(has_side_effects=EFFECT),
    )(*[pltpu.with_memory_space_constraint(b, pltpu.HBM) for b in both])
    return res[:nt], res[nt:2 * nt], res[2 * nt:3 * nt], res[3 * nt:4 * nt], res[-1][0, 0]


def _xwait(ssems, rsems, thrus, lands, scatter, after, name):
    nt = len(lands)

    def body(*refs):
        ins, lnd = refs[:nt], refs[nt:2 * nt]
        ss, rs = refs[2 * nt:3 * nt], refs[3 * nt:4 * nt]
        x, y, c = lax.axis_index("x"), lax.axis_index("y"), lax.axis_index("c")
        me = 4 * x + 2 * y + c
        for t in range(nt):
            for k, pos in _peers(x, y, c):
                cp = _exchange_copy(ins, lnd, ss, rs, scatter, t, k, pos, me)
                cp.wait_send()
                cp.wait_recv()

    both = list(thrus) + list(lands)
    res = pl.pallas_call(
        body, name=name, out_shape=[pltpu.HBM(b.shape, b.dtype) for b in both],
        in_specs=[HBM] * (2 * nt) + [SEM] * (2 * nt) + [ANY], out_specs=[HBM] * (2 * nt),
        input_output_aliases={i: i for i in range(2 * nt)},
        compiler_params=pltpu.CompilerParams(has_side_effects=EFFECT),
    )(*both, *ssems, *rsems, after)
    return res[:nt], res[nt:]


def _adam(recv, w, m, v, tr, name):
    _, r, c = w.shape
    n_part = recv.shape[0]
    c1 = np.float32(1.0 - ADAM_B1 ** ADAM_STEP)
    c2 = np.float32(1.0 - ADAM_B2 ** ADAM_STEP)

    def body(r_ref, w_ref, m_ref, v_ref, g_ref, d_ref, mo_ref, vo_ref):
        g = r_ref[0].astype(F32)
        for s in range(1, n_part):
            g = g + r_ref[s].astype(F32)
        mn = ADAM_B1 * m_ref[0] + (1.0 - ADAM_B1) * g
        vn = ADAM_B2 * v_ref[0] + (1.0 - ADAM_B2) * (g * g)
        g_ref[0] = g
        mo_ref[0] = mn
        vo_ref[0] = vn
        d_ref[0] = -ADAM_LR * ((mn / c1) / (jnp.sqrt(vn / c2) + ADAM_EPS) + ADAM_WD * w_ref[0])

    one = pl.BlockSpec((1, tr, c), lambda i: (0, i, 0))
    return pl.pallas_call(
        body, grid=(r // tr,), in_specs=[pl.BlockSpec((n_part, tr, c), lambda i: (0, i, 0)), one, one, one],
        out_specs=[one] * 4, out_shape=[jax.ShapeDtypeStruct((1, r, c), F32)] * 4,
        name=name, compiler_params=_cp())(recv, w, m, v)


def _local_step(x, target, sp, need, emit):
    g = {}
    x0, y1, saved1 = _ffn_fwd(x, None, 0.0, sp["norm_ffn1"], lambda kind, a: need("w" + kind + "1", a), "ffn1")
    x1, h2 = _rms_fwd(x0, y1, 0.5, sp["norm_mix"], "mix_norm")
    win_a, win_d = need("win_a", h2), need("win_d", h2)
    conv_w8, wout = need("conv_w8", h2), need("wout", h2)
    z_at = _mm_nn(h2, win_a, F32, 1024, 768, "mix_in_attn")
    z_dn = _mm_nn(h2, win_d, F32, 1024, 768, "mix_in_dn")
    attn_f, attn_b, lse = _attn_fwd(z_at, "attn_fwd")

    conv, qkvn, bg = _dn_pre_fwd(z_dn, conv_w8, sp["alog_row"], sp["dtb_row"], "dn_pre")
    *prep, tinv = _dn_prep_fwd(qkvn, bg, "dn_prep")
    o_dn, states = _dn_scan_fwd(prep, "dn_scan")
    dn_b = _dn_post_fwd(o_dn, z_dn, sp["dn_norm"], "dn_post")

    y2 = _mm_nn2(attn_b, dn_b, wout[:AW], wout[AW:], 1024, 1024, "mix_out")
    x2, y3, saved2 = _ffn_fwd(x1, y2, 1.0, sp["norm_ffn2"], lambda kind, a: need("w" + kind + "2", a), "ffn2")

    dx3, dys3, g["norm_final"], loss8 = _loss_bwd(x2, y3, sp["norm_final"], target, "loss")
    dx2, dx2b, g["norm_ffn2"] = _ffn_bwd(
        x2, sp["norm_ffn2"], saved2, dx3, dys3, 1.0,
        lambda kind, dw: emit(kind + "2", {"w" + kind + "2": dw}), "ffn2b")

    zero = emit("wout", {"wout": jnp.concatenate([_mm_tn(attn_b, dx2b, BF, 512, 1024, "mix_out_dw_a"),
                                                  _mm_tn(dn_b, dx2b, BF, 512, 1024, "mix_out_dw_d")], axis=0)})
    dmix = _mm_nt(dx2b, wout, F32, 1024, 1024, "mix_out_dx")

    dz_at = jnp.concatenate(_attn_bwd(z_at, dmix, attn_f, lse, "attn_bwd"), axis=1)

    do_dn, dgate, g["dn_norm"] = _dn_post_bwd(o_dn, z_dn, sp["dn_norm"] + zero, dmix, "dn_post_b")
    cts = _dn_scan_bwd(prep, states, do_dn, "dn_scan_b")
    dqn, dkn, dvn, dbg = _dn_prep_bwd(qkvn, bg, tinv, cts, "dn_prep_b")
    dconv, dbd, g["alog_row"], g["dtb_row"] = _dn_pre_bwd(
        conv, z_dn, sp["alog_row"], sp["dtb_row"], dqn, dkn, dvn, dbg, "dn_pre_b")
    draw, dconv_w8 = _dn_conv_bwd(dconv, z_dn, conv_w8, "dn_conv_b")
    dz_dn = _dz_dn_assemble(draw, dgate, dbd, "dn_dz")

    zero = emit("win", {"win_a": _mm_tn(h2, dz_at, BF, 512, 768, "mix_in_dw_a"),
                        "win_d": _mm_tn(h2, dz_dn, BF, 512, 768, "mix_in_dw_d"), "conv_w8": dconv_w8})
    dh2 = _mm_nt(dz_at, win_a, F32, 1024, 1024, "mix_in_dx_a")
    dh2d = _mm_nt(dz_dn, win_d, F32, 1024, 1024, "mix_in_dx_d")
    (dx1, dys1), _, g["norm_mix"] = _rms_bwd2(x1, sp["norm_mix"] + zero, dh2, dh2d, dx2, "mix_dnorm")
    dx0, _, g["norm_ffn1"] = _ffn_bwd(
        x0, sp["norm_ffn1"], saved1, dx1, dys1, 1.0,
        lambda kind, dw: emit(kind + "1", {"w" + kind + "1": dw}), "ffn1b")
    return loss8[0, 0], dx0, g


def _rms_bwd2(x, gain, dh_a, dh_b, dres, name):
    tm = 512
    row = pl.BlockSpec((tm, D), lambda i: (i, 0))
    gspec = pl.BlockSpec((1, D), lambda i: (0, 0))

    def body(x_ref, g_ref, da_ref, db_ref, dres_ref, dx_ref, dxs_ref, dg_ref):
        i = pl.program_id(0)
        xv = x_ref[...]
        r = lax.rsqrt(jnp.mean(xv * xv, axis=-1, keepdims=True) + NORM_EPS)
        xh = xv * r
        dhv = da_ref[...] + db_ref[...]
        part = jnp.sum(dhv * xh, axis=0, keepdims=True)

        @pl.when(i == 0)
        def _():
            dg_ref[...] = part

        @pl.when(i > 0)
        def _():
            dg_ref[...] += part

        dxh = dhv * g_ref[...]
        dx = r * (dxh - xh * jnp.mean(dxh * xh, axis=-1, keepdims=True)) + dres_ref[...]
        dx_ref[...] = dx
        dxs_ref[...] = (0.5 * dx).astype(BF)

    dx, dxs, dg = pl.pallas_call(
        body, grid=(T // tm,), in_specs=[row, gspec, row, row, row], out_specs=[row, row, gspec],
        out_shape=[jax.ShapeDtypeStruct((T, D), F32), jax.ShapeDtypeStruct((T, D), BF),
                   jax.ShapeDtypeStruct((1, D), F32)],
        name=name, compiler_params=_cp())(x, gain, dh_a, dh_b, dres)
    return (dx, dxs), None, dg


def _cols_from_shards(gathered):
    n, r, c = gathered.shape
    return jnp.transpose(gathered, (1, 0, 2)).reshape(r, n * c)


def _shards_from_cols(full, dtype):
    r, nc = full.shape
    return jnp.transpose(full.reshape(r, N_DEV, nc // N_DEV), (1, 0, 2)).astype(dtype)


def _lane_row(vec4):
    return jnp.zeros((1, 128), F32).at[:, DN_H:2 * DN_H].set(vec4.astype(F32))


WEIGHT_SOURCES = {"wg1": "gate1", "wu1": "up1", "wd1": "down1", "win_a": "w_in", "win_d": "w_in",
                  "conv_w8": "conv_w", "wout": "w_out", "wg2": "gate2", "wu2": "up2", "wd2": "down2"}
TRANSPOSED = ("gate1", "up1", "gate2", "up2")


def _build_weights(name, gath):
    if name in ("wg1", "wu1", "wd1", "wg2", "wu2", "wd2"):
        return {name: gath[WEIGHT_SOURCES[name]].reshape(F, D)}
    if name in ("win_a", "win_d"):
        w_in = _cols_from_shards(gath["w_in"])
        c0 = 3 * AW + 3 * DNW
        wp = jnp.concatenate([w_in[:, :c0], w_in[:, c0 + 2 * DN_H:], w_in[:, c0:c0 + 2 * DN_H],
                              jnp.zeros((D, ZP - IN_COLS), w_in.dtype)], axis=1)
        return {"win_a": wp[:, :ZA], "win_d": wp[:, ZA:]}
    if name == "wout":
        return {name: gath["w_out"].reshape(D, D)}
    conv = _cols_from_shards(gath["conv_w"])
    return {"conv_w8": jnp.concatenate([conv, jnp.zeros((4, 3 * DNW), F32)], axis=0)}


def _small_params(norm_ffn1, norm_mix, norm_ffn2, norm_final, a_log, dt_bias, dn_norm):
    return {"norm_ffn1": norm_ffn1, "norm_mix": norm_mix, "norm_ffn2": norm_ffn2,
            "norm_final": norm_final.reshape(1, D), "alog_row": _lane_row(a_log), "dtb_row": _lane_row(dt_bias),
            "dn_norm": dn_norm}


def _grad_slabs(group, g):
    if group[0] in "gud":
        return {WEIGHT_SOURCES["w" + group]: g["w" + group].reshape(N_DEV, F // N_DEV, D)}
    if group == "wout":
        return {"w_out": g["wout"].reshape(N_DEV, D // N_DEV, D)}
    gp = jnp.concatenate([g["win_a"], g["win_d"]], axis=1)
    c0 = 3 * AW + 3 * DNW
    g_in = jnp.concatenate([gp[:, :c0], gp[:, c0 + DNW:c0 + DNW + 2 * DN_H], gp[:, c0:c0 + DNW]], axis=1)
    return {"w_in": _shards_from_cols(g_in, BF), "conv_w": _shards_from_cols(g["conv_w8"][:4], F32)}


SMALL_ROWS = 40


def _small_pack(norm_ffn1, norm_mix, norm_ffn2, norm_final, dn_norm, alog_row, dtb_row):
    rows = [a.reshape(8, 128) for a in (norm_ffn1, norm_mix, norm_ffn2, norm_final)]
    rows += [dn_norm.reshape(1, 128), alog_row, dtb_row, jnp.zeros((SMALL_ROWS - 35, 128), F32)]
    return jnp.concatenate(rows, axis=0)


def _small_unpack(pk):
    pk = pk[0]
    return (pk[0:8].reshape(1, D), pk[8:16].reshape(1, D), pk[16:24].reshape(1, D), pk[24:32].reshape(D),
            pk[32:33], pk[33:34, DN_H:2 * DN_H], pk[34:35, DN_H:2 * DN_H])


ADAM_TILE = {"gate1": 256, "up1": 256, "down1": 176, "gate2": 256, "up2": 256, "down2": 176,
             "w_in": 256, "w_out": 128, "conv_w": 4}
BIG = ("gate1", "up1", "down1", "w_in", "w_out", "gate2", "up2", "down2", "conv_w")


def kernel(x, norm_ffn1, ffn1_gate, ffn1_up, ffn1_down, norm_mix, w_in, conv_w, a_log, dt_bias, dn_norm, w_out, norm_ffn2, ffn2_gate, ffn2_up, ffn2_down, norm_final, loss_target, m_norm_ffn1, m_ffn1_gate, m_ffn1_up, m_ffn1_down, m_norm_mix, m_w_in, m_conv_w, m_a_log, m_dt_bias, m_dn_norm, m_w_out, m_norm_ffn2, m_ffn2_gate, m_ffn2_up, m_ffn2_down, m_norm_final, v_norm_ffn1, v_ffn1_gate, v_ffn1_up, v_ffn1_down, v_norm_mix, v_w_in, v_conv_w, v_a_log, v_dt_bias, v_dn_norm, v_w_out, v_norm_ffn2, v_ffn2_gate, v_ffn2_up, v_ffn2_down, v_norm_final):
    w = {"gate1": ffn1_gate, "up1": ffn1_up, "down1": ffn1_down, "w_in": w_in, "w_out": w_out,
         "gate2": ffn2_gate, "up2": ffn2_up, "down2": ffn2_down, "conv_w": conv_w}
    m = {"gate1": m_ffn1_gate, "up1": m_ffn1_up, "down1": m_ffn1_down, "w_in": m_w_in, "w_out": m_w_out,
         "gate2": m_ffn2_gate, "up2": m_ffn2_up, "down2": m_ffn2_down, "conv_w": m_conv_w}
    v = {"gate1": v_ffn1_gate, "up1": v_ffn1_up, "down1": v_ffn1_down, "w_in": v_w_in, "w_out": v_w_out,
         "gate2": v_ffn2_gate, "up2": v_ffn2_up, "down2": v_ffn2_down, "conv_w": v_conv_w}

    me = 4 * lax.axis_index("x") + 2 * lax.axis_index("y") + lax.axis_index("c")
    own_slot = lambda land, mine: lax.dynamic_update_index_in_dim(land, mine, me, 0)

    ag_order = ("gate1", "up1", "down1", "w_in", "conv_w", "w_out", "gate2", "up2", "down2")
    ag_groups = (("gate1",), ("up1",), ("down1",), ("w_in", "conv_w", "w_out"), ("gate2",), ("up2",), ("down2",))
    pos = {n: i for i, n in enumerate(ag_order)}

    def shard(n):
        if n == "conv_w":
            return w[n][0]
        return (w[n][0].T if n in TRANSPOSED else w[n][0]).astype(BF)

    ss, rs, thru, land, zero = _xstart([shard(n) for n in ag_order], False, "weights_start")
    gath, built = {}, {}

    def need(name, after):
        if name not in built:
            src = WEIGHT_SOURCES[name]
            if src not in gath:
                gi = [i for i, grp in enumerate(ag_groups) if src in grp][0]
                ids = [pos[n] for n in ag_groups[gi]]
                thrus, lands = _xwait([ss[i] for i in ids], [rs[i] for i in ids], [thru[i] for i in ids],
                                      [land[i] for i in ids], False, after, "weights_wait%d" % gi)
                for n, t, l in zip(ag_groups[gi], thrus, lands):
                    gath[n] = own_slot(l, t)
            built.update(_build_weights(name, gath))
        return built[name]

    pending = []

    def emit(group, grads):
        slabs = grads if group == "small" else _grad_slabs(group, grads)
        names = list(slabs)
        started = _xstart([slabs[n] for n in names], True, "grads_start_" + group)
        pending.append((group, names) + started[:4])
        return started[4]

    sp = _small_params(norm_ffn1 + zero, norm_mix, norm_ffn2, norm_final, a_log, dt_bias, dn_norm)
    loss_part, dx, g = _local_step(x[0], loss_target[0], sp, need, emit)
    small = _small_pack(g["norm_ffn1"], g["norm_mix"], g["norm_ffn2"], g["norm_final"], g["dn_norm"],
                        g["alog_row"], g["dtb_row"])
    emit("small", {"small": jnp.broadcast_to(small[None], (N_DEV, SMALL_ROWS, 128))})

    pack = lambda a: _small_pack(*a)[None]
    res, after = {}, dx
    for group, names, gss, grs, gthru, gland in pending:
        thrus, lands = _xwait(gss, grs, gthru, gland, True, after, "grads_wait_" + group)
        for n, t, l in zip(names, thrus, lands):
            recv = own_slot(l, lax.dynamic_index_in_dim(t, me, 0, keepdims=False))
            if n == "small":
                res[n] = _adam(
                    recv,
                    pack((norm_ffn1, norm_mix, norm_ffn2, norm_final, dn_norm, _lane_row(a_log), _lane_row(dt_bias))),
                    pack((m_norm_ffn1, m_norm_mix, m_norm_ffn2, m_norm_final, m_dn_norm, _lane_row(m_a_log),
                          _lane_row(m_dt_bias))),
                    pack((v_norm_ffn1, v_norm_mix, v_norm_ffn2, v_norm_final, v_dn_norm, _lane_row(v_a_log),
                          _lane_row(v_dt_bias))),
                    SMALL_ROWS, "adam_small")
            elif n in TRANSPOSED:
                flip = lambda a: jnp.swapaxes(a, 1, 2)
                res[n] = [flip(o) for o in _adam(recv, flip(w[n]), flip(m[n]), flip(v[n]), F // N_DEV // 2,
                                                 "adam_" + n)]
            else:
                res[n] = _adam(recv, w[n], m[n], v[n], ADAM_TILE[n], "adam_" + n)
            after = res[n][0]
    res_s = res["small"]

    loss = lax.psum(loss_part, ("x", "y", "c"))
    outs = [loss, dx[None]]
    for k in range(4):
        n1, nm, n2, nf, dn, al, dt = _small_unpack(res_s[k])
        big = {n: res[n][k] for n in BIG}
        outs += [n1, big["gate1"], big["up1"], big["down1"], nm, big["w_in"], big["conv_w"], al, dt, dn,
                 big["w_out"], n2, big["gate2"], big["up2"], big["down2"], nf]
    return tuple(outs)
```

```python
import functools

import numpy as np
import jax
import jax.numpy as jnp
from jax import lax
from jax.experimental import pallas as pl
from jax.experimental.pallas import tpu as pltpu

T = 4096
D = 1024
F = 2816
N_DEV = 8
A_HEADS = 8
A_HD = 64
AW = A_HEADS * A_HD
DN_H = 4
DN_HD = 128
DNW = DN_H * DN_HD
CH = 64
PAIR = 2 * CH
ZA = 3 * AW
ZD = 3 * DNW + DNW + 256
ZP = ZA + ZD
BD_BLK = (3 * DNW + DNW) // 128
IN_COLS = 3592
PATTERNS = ((128, 1), (512, 4), (2048, 16))
NORM_EPS = 1e-6
L2_EPS = 1e-6
ADAM_LR, ADAM_B1, ADAM_B2, ADAM_EPS, ADAM_WD, ADAM_STEP = 0.001, 0.9, 0.999, 1e-08, 0.01, 10
VMEM_LIMIT = 56 * 1024 * 1024
NEG = -1e30

BF = jnp.bfloat16
F32 = jnp.float32
NN = (((1,), (0,)), ((), ()))
NT = (((1,), (1,)), ((), ()))
TN = (((0,), (0,)), ((), ()))
HI = lax.Precision.HIGHEST
MESH_ID = pl.DeviceIdType.MESH
ANY = pl.BlockSpec(memory_space=pl.ANY)


def _cp():
    return pltpu.CompilerParams(vmem_limit_bytes=VMEM_LIMIT)


def _dg(a, b, dims):
    return lax.dot_general(a, b, dims, preferred_element_type=F32)


def _hdot(a, b):
    return lax.dot_general(a, b, NN, precision=HI, preferred_element_type=F32)


def _make_bdot(dims, da_dims, da_swap, db_dims, db_swap):
    @jax.custom_vjp
    def f(a, b):
        return _dg(a.astype(BF), b.astype(BF), dims)

    def fwd(a, b):
        return f(a, b), (a, b)

    def bwd(res, g):
        a, b = res
        gb, ab, bb = g.astype(BF), a.astype(BF), b.astype(BF)
        da = _dg(bb, gb, da_dims) if da_swap else _dg(gb, bb, da_dims)
        db = _dg(gb, ab, db_dims) if db_swap else _dg(ab, gb, db_dims)
        return da, db

    f.defvjp(fwd, bwd)
    return f


_bdot_nn = _make_bdot(NN, NT, False, TN, False)
_bdot_nt = _make_bdot(NT, NN, False, TN, True)
_bdot_tn = _make_bdot(TN, NT, True, NN, False)


def _iota(shape, dim):
    return lax.broadcasted_iota(jnp.int32, shape, dim)


def _col(x, idx):
    return jnp.sum(jnp.where(_iota(x.shape, 1) == idx, x, 0.0), axis=1, keepdims=True)


def _mm_nn(a, b, out_dtype, tm, tn, name):
    m, k = a.shape
    n = b.shape[1]

    def body(a_ref, b_ref, o_ref):
        o_ref[...] = _dg(a_ref[...], b_ref[...], NN).astype(out_dtype)

    return pl.pallas_call(
        body, grid=(m // tm, n // tn),
        in_specs=[pl.BlockSpec((tm, k), lambda i, j: (i, 0)), pl.BlockSpec((k, tn), lambda i, j: (0, j))],
        out_specs=pl.BlockSpec((tm, tn), lambda i, j: (i, j)),
        out_shape=jax.ShapeDtypeStruct((m, n), out_dtype), name=name, compiler_params=_cp())(a, b)


def _mm_nt(a, b, out_dtype, tm, tb, name):
    m, c = a.shape
    kb = b.shape[0]

    def body(a_ref, b_ref, o_ref):
        o_ref[...] = _dg(a_ref[...], b_ref[...], NT).astype(out_dtype)

    return pl.pallas_call(
        body, grid=(m // tm, kb // tb),
        in_specs=[pl.BlockSpec((tm, c), lambda i, j: (i, 0)), pl.BlockSpec((tb, c), lambda i, j: (j, 0))],
        out_specs=pl.BlockSpec((tm, tb), lambda i, j: (i, j)),
        out_shape=jax.ShapeDtypeStruct((m, kb), out_dtype), name=name, compiler_params=_cp())(a, b)


def _mm_tn(a, b, out_dtype, ta, tb, name):
    m, ka = a.shape
    nb = b.shape[1]

    def body(a_ref, b_ref, o_ref):
        o_ref[...] = _dg(a_ref[...], b_ref[...], TN).astype(out_dtype)

    return pl.pallas_call(
        body, grid=(ka // ta, nb // tb),
        in_specs=[pl.BlockSpec((m, ta), lambda i, j: (0, i)), pl.BlockSpec((m, tb), lambda i, j: (0, j))],
        out_specs=pl.BlockSpec((ta, tb), lambda i, j: (i, j)),
        out_shape=jax.ShapeDtypeStruct((ka, nb), out_dtype), name=name, compiler_params=_cp())(a, b)


def _mm_nn2(a1, a2, b1, b2, tm, tn, name):
    m, k = a1.shape
    n = b1.shape[1]

    def body(a1_ref, a2_ref, b1_ref, b2_ref, o_ref):
        o_ref[...] = _dg(a1_ref[...], b1_ref[...], NN) + _dg(a2_ref[...], b2_ref[...], NN)

    arow = pl.BlockSpec((tm, k), lambda i, j: (i, 0))
    bcol = pl.BlockSpec((k, tn), lambda i, j: (0, j))
    return pl.pallas_call(
        body, grid=(m // tm, n // tn), in_specs=[arow, arow, bcol, bcol],
        out_specs=pl.BlockSpec((tm, tn), lambda i, j: (i, j)),
        out_shape=jax.ShapeDtypeStruct((m, n), F32), name=name, compiler_params=_cp())(a1, a2, b1, b2)


def _mm_nt_swiglu(h, wu_t, gate, tm, tb, name):
    m, c = h.shape
    kb = wu_t.shape[0]

    def body(h_ref, w_ref, g_ref, u_ref, a_ref):
        u = _dg(h_ref[...], w_ref[...], NT)
        g = g_ref[...].astype(F32)
        u_ref[...] = u.astype(BF)
        a_ref[...] = (g * jax.nn.sigmoid(g) * u).astype(BF)

    tile = pl.BlockSpec((tm, tb), lambda i, j: (i, j))
    return pl.pallas_call(
        body, grid=(m // tm, kb // tb),
        in_specs=[pl.BlockSpec((tm, c), lambda i, j: (i, 0)), pl.BlockSpec((tb, c), lambda i, j: (j, 0)), tile],
        out_specs=[tile, tile], out_shape=[jax.ShapeDtypeStruct((m, kb), BF)] * 2,
        name=name, compiler_params=_cp())(h, wu_t, gate)


def _mm_nt_dswiglu(dys, wd, gate, up, tm, tb, name):
    m, c = dys.shape
    kb = wd.shape[0]

    def body(d_ref, w_ref, g_ref, u_ref, dg_ref, du_ref):
        da = _dg(d_ref[...], w_ref[...], NT)
        g = g_ref[...].astype(F32)
        u = u_ref[...].astype(F32)
        s = jax.nn.sigmoid(g)
        dg_ref[...] = (da * u * (s * (1.0 + g * (1.0 - s)))).astype(BF)
        du_ref[...] = (da * (g * s)).astype(BF)

    tile = pl.BlockSpec((tm, tb), lambda i, j: (i, j))
    return pl.pallas_call(
        body, grid=(m // tm, kb // tb),
        in_specs=[pl.BlockSpec((tm, c), lambda i, j: (i, 0)), pl.BlockSpec((tb, c), lambda i, j: (j, 0)),
                  tile, tile],
        out_specs=[tile, tile], out_shape=[jax.ShapeDtypeStruct((m, kb), BF)] * 2,
        name=name, compiler_params=_cp())(dys, wd, gate, up)


def _rms_fwd(x, y, alpha, gain, name):
    tm = 512
    has_y = y is not None
    row = pl.BlockSpec((tm, D), lambda i: (i, 0))
    gspec = pl.BlockSpec((1, D), lambda i: (0, 0))

    def body(*refs):
        if has_y:
            x_ref, y_ref, g_ref, xo_ref, h_ref = refs
            xv = x_ref[...] + alpha * y_ref[...]
            xo_ref[...] = xv
        else:
            x_ref, g_ref, h_ref = refs
            xv = x_ref[...]
        r = lax.rsqrt(jnp.mean(xv * xv, axis=-1, keepdims=True) + NORM_EPS)
        h_ref[...] = (xv * r * g_ref[...]).astype(BF)

    if has_y:
        return pl.pallas_call(
            body, grid=(T // tm,), in_specs=[row, row, gspec], out_specs=[row, row],
            out_shape=[jax.ShapeDtypeStruct((T, D), F32), jax.ShapeDtypeStruct((T, D), BF)],
            name=name, compiler_params=_cp())(x, y, gain)
    h = pl.pallas_call(
        body, grid=(T // tm,), in_specs=[row, gspec], out_specs=row,
        out_shape=jax.ShapeDtypeStruct((T, D), BF), name=name, compiler_params=_cp())(x, gain)
    return x, h


def _rms_bwd(x, gain, dh, dres, alpha_out, name):
    tm = 512
    row = pl.BlockSpec((tm, D), lambda i: (i, 0))
    gspec = pl.BlockSpec((1, D), lambda i: (0, 0))

    def body(x_ref, g_ref, dh_ref, dres_ref, dx_ref, dxs_ref, dg_ref):
        i = pl.program_id(0)
        xv = x_ref[...]
        r = lax.rsqrt(jnp.mean(xv * xv, axis=-1, keepdims=True) + NORM_EPS)
        xh = xv * r
        dhv = dh_ref[...].astype(F32)
        part = jnp.sum(dhv * xh, axis=0, keepdims=True)

        @pl.when(i == 0)
        def _():
            dg_ref[...] = part

        @pl.when(i > 0)
        def _():
            dg_ref[...] += part

        dxh = dhv * g_ref[...]
        dx = r * (dxh - xh * jnp.mean(dxh * xh, axis=-1, keepdims=True)) + dres_ref[...]
        dx_ref[...] = dx
        dxs_ref[...] = (alpha_out * dx).astype(BF)

    return pl.pallas_call(
        body, grid=(T // tm,), in_specs=[row, gspec, row, row], out_specs=[row, row, gspec],
        out_shape=[jax.ShapeDtypeStruct((T, D), F32), jax.ShapeDtypeStruct((T, D), BF),
                   jax.ShapeDtypeStruct((1, D), F32)],
        name=name, compiler_params=_cp())(x, gain, dh, dres)


def _loss_bwd(x_prev, y, gain, target, name):
    tm = 512
    row = pl.BlockSpec((tm, D), lambda i: (i, 0))
    gspec = pl.BlockSpec((1, D), lambda i: (0, 0))
    lspec = pl.BlockSpec((8, 128), lambda i: (0, 0))

    def body(x_ref, y_ref, g_ref, t_ref, dx_ref, dxs_ref, dg_ref, loss_ref):
        i = pl.program_id(0)
        xv = x_ref[...] + 0.5 * y_ref[...]
        r = lax.rsqrt(jnp.mean(xv * xv, axis=-1, keepdims=True) + NORM_EPS)
        xh = xv * r
        diff = xh * g_ref[...] - t_ref[...]
        lpart = 0.5 * jnp.sum(jnp.mean(diff * diff, axis=-1, keepdims=True), axis=0, keepdims=True)
        dy = diff * (1.0 / D)
        part = jnp.sum(dy * xh, axis=0, keepdims=True)

        @pl.when(i == 0)
        def _():
            dg_ref[...] = part
            loss_ref[...] = jnp.broadcast_to(lpart, (8, 128))

        @pl.when(i > 0)
        def _():
            dg_ref[...] += part
            loss_ref[...] += jnp.broadcast_to(lpart, (8, 128))

        dxh = dy * g_ref[...]
        dx = r * (dxh - xh * jnp.mean(dxh * xh, axis=-1, keepdims=True))
        dx_ref[...] = dx
        dxs_ref[...] = (0.5 * dx).astype(BF)

    return pl.pallas_call(
        body, grid=(T // tm,), in_specs=[row, row, gspec, row], out_specs=[row, row, gspec, lspec],
        out_shape=[jax.ShapeDtypeStruct((T, D), F32), jax.ShapeDtypeStruct((T, D), BF),
                   jax.ShapeDtypeStruct((1, D), F32), jax.ShapeDtypeStruct((8, 128), F32)],
        name=name, compiler_params=_cp())(x_prev, y, gain, target)


def _after(x, token):
    return lax.optimization_barrier((x, token))[0]


def _ffn_fwd(x_prev, y_prev, alpha, gain, need, tag):
    x_in, h = _rms_fwd(x_prev, y_prev, alpha, gain, tag + "_norm")
    wg = need("g", h)
    gate = _mm_nt(h, wg, BF, 1024, 1408, tag + "_gate")
    wu = need("u", gate)
    up, act = _mm_nt_swiglu(h, wu, gate, 1024, 1408, tag + "_up_act")
    wd = need("d", up)
    y = _mm_nn(act, wd, F32, 1024, 1024, tag + "_down")
    return x_in, y, (h, gate, up, act, wg, wu, wd)


def _ffn_bwd(x_in, gain, saved, dxo, dys, alpha_out, emit, tag):
    h, gate, up, act, wg, wu, wd = saved
    dys = _after(dys, emit("d", _mm_tn(act, dys, BF, 256, 1024, tag + "_dwd")))
    dgate, dup = _mm_nt_dswiglu(dys, wd, gate, up, 1024, 1408, tag + "_dact")
    dup = _after(dup, emit("g", _mm_tn(dgate, h, BF, 256, 1024, tag + "_dwg")))
    dgate = _after(dgate, emit("u", _mm_tn(dup, h, BF, 256, 1024, tag + "_dwu")))
    dh = _mm_nn2(dgate, dup, wg, wu, 512, 512, tag + "_dh")
    dx, dxs, dgain = _rms_bwd(x_in, gain, dh, dxo, alpha_out, tag + "_dnorm")
    return dx, dxs, dgain


SLAB = 2048
N_SLAB = T // SLAB
N_PAIR = A_HEADS // 2


def _pair_masks():
    lane = _iota((128, 128), 1)
    return lane < A_HD, lane >= A_HD


def _slope_table():
    h = 2 * jnp.arange(N_PAIR)[:, None] + jnp.minimum(jnp.arange(8), 1)[None, :]
    return jnp.broadcast_to((2.0 ** (-(h + 1).astype(F32)))[:, :, None], (N_PAIR, 8, 128))


def _rows(ref, start, d):
    if d == 1:
        return ref[pl.ds(start, 128), :]
    return ref[pl.ds(start, 128, stride=d), :]


def _put_rows(ref, start, d, val):
    if d == 1:
        ref[pl.ds(start, 128), :] = val
    else:
        ref[pl.ds(start, 128, stride=d), :] = val


def _units(d):
    return [(r, b, r + 128 * d * b) for r in range(d) for b in range(SLAB // (128 * d))]


def _band(d, prev_valid):
    qi = _iota((128, 256), 0)
    kj = _iota((128, 256), 1)
    steps = qi + 128 - kj
    valid = (steps >= 0) & (steps <= 128) & (prev_valid | (kj >= 128))
    return valid, (steps * d).astype(F32)


def _attn_fwd(z_at, name):
    def body(sl_ref, q_ref, kc_ref, kp_ref, vc_ref, vp_ref, of_ref, ob_ref, lse_ref, m_s, l_s, a_s):
        n = pl.program_id(1)
        lo, hi = _pair_masks()
        slopes = (sl_ref[0, 0:1, 0:1], sl_ref[0, 1:2, 0:1])

        def unit(d, start, b, first, carry):
            q = _rows(q_ref, start, d).astype(BF)
            kcur, vcur = _rows(kc_ref, start, d).astype(BF), _rows(vc_ref, start, d).astype(BF)
            if b > 0:
                (kprev, vprev), prev_valid = carry, True
            else:
                pstart = start + SLAB - 128 * d
                kprev, vprev = _rows(kp_ref, pstart, d).astype(BF), _rows(vp_ref, pstart, d).astype(BF)
                prev_valid = n > 0
            kcat = jnp.concatenate([kprev, kcur], axis=0)
            vcat = jnp.concatenate([vprev, vcur], axis=0)
            valid, dist = _band(d, prev_valid)
            ms, ls, pvs = [], [], []
            for e in range(2):
                qm = jnp.where(lo if e == 0 else hi, q, jnp.zeros_like(q))
                s = _dg(qm, kcat, NT) * (A_HD ** -0.5) - slopes[e] * dist
                s = jnp.where(valid, s, NEG)
                m = jnp.max(s, axis=1, keepdims=True)
                p = jnp.exp(s - m)
                ms.append(m)
                ls.append(jnp.sum(p, axis=1, keepdims=True))
                pvs.append(_dg(p.astype(BF), vcat, NN))
            m_u = jnp.where(lo, ms[0], ms[1])
            l_u = jnp.where(lo, ls[0], ls[1])
            a_u = jnp.where(lo, pvs[0], pvs[1])
            if first:
                m_n, l_n, a_n = m_u, l_u, a_u
            else:
                m_o = _rows(m_s, start, d)
                m_n = jnp.maximum(m_o, m_u)
                c_o = jnp.exp(m_o - m_n)
                c_u = jnp.exp(m_u - m_n)
                l_n = _rows(l_s, start, d) * c_o + l_u * c_u
                a_n = _rows(a_s, start, d) * c_o + a_u * c_u
            _put_rows(m_s, start, d, m_n)
            _put_rows(l_s, start, d, l_n)
            _put_rows(a_s, start, d, a_n)
            return kcur, vcur

        for pi, (_, d) in enumerate(PATTERNS):
            carry = None
            for r, b, start in _units(d):
                carry = unit(d, start, b, pi == 0, carry)
        l = l_s[...]
        out = a_s[...] / l
        of_ref[...] = out
        ob_ref[...] = out.astype(BF)
        lse_ref[...] = m_s[...] + jnp.log(l)

    cur = lambda c: pl.BlockSpec((SLAB, 128), lambda j, n: (n, c * N_PAIR + j))
    prv = lambda c: pl.BlockSpec((SLAB, 128), lambda j, n: (jnp.maximum(n - 1, 0), c * N_PAIR + j))
    out = pl.BlockSpec((SLAB, 128), lambda j, n: (n, j))
    return pl.pallas_call(
        body, grid=(N_PAIR, N_SLAB),
        in_specs=[pl.BlockSpec((1, 8, 128), lambda j, n: (j, 0, 0)), cur(0), cur(1), prv(1), cur(2), prv(2)],
        out_specs=[out, out, out],
        out_shape=[jax.ShapeDtypeStruct((T, AW), F32), jax.ShapeDtypeStruct((T, AW), BF),
                   jax.ShapeDtypeStruct((T, AW), F32)],
        scratch_shapes=[pltpu.VMEM((SLAB, 128), F32)] * 3,
        name=name, compiler_params=_cp())(_slope_table(), z_at, z_at, z_at, z_at, z_at)


def _attn_bwd(z_at, dout, out, lse, name):
    def body(sl_ref, q_ref, kc_ref, kp_ref, vc_ref, vp_ref, do_ref, o_ref, lse_ref, dq_ref, dk_ref, dv_ref,
             dq_s, dk_s, dv_s, ck_s, cv_s):
        step = pl.program_id(1)
        n = N_SLAB - 1 - step
        lo, hi = _pair_masks()
        slopes = (sl_ref[0, 0:1, 0:1], sl_ref[0, 1:2, 0:1])

        @pl.when(step == 0)
        def _():
            ck_s[...] = jnp.zeros_like(ck_s)
            cv_s[...] = jnp.zeros_like(cv_s)

        dk_s[...] = ck_s[...]
        dv_s[...] = cv_s[...]
        ck_s[...] = jnp.zeros_like(ck_s)
        cv_s[...] = jnp.zeros_like(cv_s)

        def add_rows(ref, start, d, val):
            _put_rows(ref, start, d, _rows(ref, start, d) + val)

        def unit(d, start, b, first, carry):
            q = _rows(q_ref, start, d).astype(BF)
            do_f = _rows(do_ref, start, d)
            do = do_f.astype(BF)
            prod = do_f * _rows(o_ref, start, d)
            lse_u = _rows(lse_ref, start, d)
            kcur, vcur = _rows(kc_ref, start, d).astype(BF), _rows(vc_ref, start, d).astype(BF)
            if b > 0:
                (kprev, vprev), prev_valid = carry, True
            else:
                pstart = start + SLAB - 128 * d
                kprev, vprev = _rows(kp_ref, pstart, d).astype(BF), _rows(vp_ref, pstart, d).astype(BF)
                prev_valid = n > 0
            kcat = jnp.concatenate([kprev, kcur], axis=0)
            vcat = jnp.concatenate([vprev, vcur], axis=0)
            valid, dist = _band(d, prev_valid)
            masks = (lo, hi)
            qms = [jnp.where(msk, q, jnp.zeros_like(q)) for msk in masks]
            doms = [jnp.where(msk, do, jnp.zeros_like(do)) for msk in masks]
            deltas = [jnp.sum(jnp.where(msk, prod, 0.0), axis=1, keepdims=True) for msk in masks]
            ss = [_dg(qm, kcat, NT) * (A_HD ** -0.5) - sl * dist for qm, sl in zip(qms, slopes)]
            dps = [_dg(dom, vcat, NT) for dom in doms]
            ps = [jnp.where(valid, jnp.exp(jnp.where(valid, s, NEG) - lse_u[:, 64 * e:64 * e + 1]), 0.0)
                  for e, s in enumerate(ss)]
            dss = [(p * (dp - delta)).astype(BF) for p, dp, delta in zip(ps, dps, deltas)]
            pbs = [p.astype(BF) for p in ps]
            dqs = [_dg(ds, kcat, NN) for ds in dss]
            dkc = (_dg(dss[0], qms[0], TN) + _dg(dss[1], qms[1], TN)) * (A_HD ** -0.5)
            dvc = _dg(pbs[0], doms[0], TN) + _dg(pbs[1], doms[1], TN)
            dq_u = jnp.where(lo, dqs[0], dqs[1]) * (A_HD ** -0.5)
            if first:
                _put_rows(dq_s, start, d, dq_u)
            else:
                add_rows(dq_s, start, d, dq_u)
            add_rows(dk_s, start, d, dkc[128:])
            add_rows(dv_s, start, d, dvc[128:])
            if b > 0:
                add_rows(dk_s, start - 128 * d, d, dkc[:128])
                add_rows(dv_s, start - 128 * d, d, dvc[:128])
            else:
                pstart = start + SLAB - 128 * d
                add_rows(ck_s, pstart, d, dkc[:128])
                add_rows(cv_s, pstart, d, dvc[:128])
            return kcur, vcur

        for pi, (_, d) in enumerate(PATTERNS):
            carry = None
            for r, b, start in _units(d):
                carry = unit(d, start, b, pi == 0, carry)
        dq_ref[...] = dq_s[...].astype(BF)
        dk_ref[...] = dk_s[...].astype(BF)
        dv_ref[...] = dv_s[...].astype(BF)

    rev = lambda n: N_SLAB - 1 - n
    cur = lambda c: pl.BlockSpec((SLAB, 128), lambda j, n: (rev(n), c * N_PAIR + j))
    prv = lambda c: pl.BlockSpec((SLAB, 128), lambda j, n: (jnp.maximum(rev(n) - 1, 0), c * N_PAIR + j))
    one = pl.BlockSpec((SLAB, 128), lambda j, n: (rev(n), j))
    return pl.pallas_call(
        body, grid=(N_PAIR, N_SLAB),
        in_specs=[pl.BlockSpec((1, 8, 128), lambda j, n: (j, 0, 0)), cur(0), cur(1), prv(1), cur(2), prv(2),
                  one, one, one],
        out_specs=[one, one, one], out_shape=[jax.ShapeDtypeStruct((T, AW), BF)] * 3,
        scratch_shapes=[pltpu.VMEM((SLAB, 128), F32)] * 5,
        name=name, compiler_params=_cp())(_slope_table(), z_at, z_at, z_at, z_at, z_at, dout, out, lse)


def _silu(x):
    return x * jax.nn.sigmoid(x)


def _qk_math(c):
    s = _silu(c)
    return s * lax.rsqrt(jnp.sum(s * s, axis=-1, keepdims=True) + L2_EPS)


def _softplus(x):
    return jnp.maximum(x, 0.0) + jnp.log(1.0 + jnp.exp(-jnp.abs(x)))


def _gate_math(bd, alog_row, dtb_row):
    rows = bd.shape[0]
    lane = _iota(bd.shape, 1)
    beta = jax.nn.sigmoid(bd)
    g = jnp.where((lane >= DN_H) & (lane < 2 * DN_H), -jnp.exp(alog_row) * _softplus(bd + dtb_row), 0.0)
    ri = _iota((rows, rows), 0)
    ci = _iota((rows, rows), 1)
    same = (ri // CH) == (ci // CH)
    li = _iota((128, 128), 0)
    lj = _iota((128, 128), 1)
    to_next_group = jnp.where((lj == li + DN_H) & (li >= DN_H) & (li < 2 * DN_H), 1.0, 0.0)
    gc = _hdot(jnp.where(same & (ci <= ri), 1.0, 0.0), g)
    glast = _hdot(_hdot(jnp.where(same, 1.0, 0.0), g), to_next_group)
    return jnp.where(lane < DN_H, beta, 0.0) + gc + glast


def _shift_down(cur, halo, s):
    if s == 0:
        return cur
    rolled = pltpu.roll(cur, s, 0)
    hr = pltpu.roll(halo, s, 0)
    head = jnp.where(_iota(hr.shape, 0) < s, hr, rolled[:8])
    return jnp.concatenate([head, rolled[8:]], axis=0)


def _shift_up(cur, halo, s):
    if s == 0:
        return cur
    rows = cur.shape[0]
    rolled = pltpu.roll(cur, rows - s, 0)
    hr = pltpu.roll(halo, 8 - s, 0)
    tail = jnp.where(_iota(hr.shape, 0) >= 8 - s, hr, rolled[rows - 8:])
    return jnp.concatenate([rolled[:rows - 8], tail], axis=0)


def _dn_pre_fwd(z_dn, conv_w8, alog_row, dtb_row, name):
    tm = 256
    wq = 3 * DNW

    def body(raw_ref, halo_ref, bd_ref, w_ref, al_ref, dt_ref, conv_ref, qkv_ref, bg_ref):
        i = pl.program_id(0)
        cur = raw_ref[...]
        halo = jnp.where(i > 0, halo_ref[...], 0.0)
        w = w_ref[...]
        conv = jnp.zeros((tm, wq), F32)
        for j in range(4):
            conv = conv + _shift_down(cur, halo, 3 - j) * w[j:j + 1, :]
        conv_ref[...] = conv
        for blk in range(3 * DN_H):
            sl = slice(128 * blk, 128 * blk + 128)
            c = conv[:, sl]
            qkv_ref[:, sl] = _qk_math(c) if blk < 2 * DN_H else _silu(c)
        bg_ref[...] = _gate_math(bd_ref[...], al_ref[...], dt_ref[...])

    one = pl.BlockSpec((1, 128), lambda i: (0, 0))
    return pl.pallas_call(
        body, grid=(T // tm,),
        in_specs=[pl.BlockSpec((tm, wq), lambda i: (i, 0)),
                  pl.BlockSpec((8, wq), lambda i: (jnp.maximum(i * (tm // 8) - 1, 0), 0)),
                  pl.BlockSpec((tm, 128), lambda i: (i, BD_BLK)),
                  pl.BlockSpec((8, wq), lambda i: (0, 0)), one, one],
        out_specs=[pl.BlockSpec((tm, wq), lambda i: (i, 0)), pl.BlockSpec((tm, wq), lambda i: (i, 0)),
                   pl.BlockSpec((tm, 128), lambda i: (i, 0))],
        out_shape=[jax.ShapeDtypeStruct((T, wq), F32), jax.ShapeDtypeStruct((T, wq), F32),
                   jax.ShapeDtypeStruct((T, 128), F32)],
        name=name, compiler_params=_cp())(z_dn, z_dn, z_dn, conv_w8, alog_row, dtb_row)


def _dn_pre_bwd(conv, z_dn, alog_row, dtb_row, dqn, dkn, dvn, dbg, name):
    tm = 256
    wq = 3 * DNW

    def body(conv_ref, bd_ref, al_ref, dt_ref, dq_ref, dk_ref, dv_ref, dbg_ref,
             dconv_ref, dbd_ref, dal_ref, ddt_ref):
        i = pl.program_id(0)
        for blk in range(3 * DN_H):
            sl = slice(128 * blk, 128 * blk + 128)
            src = (dq_ref, dk_ref, dv_ref)[blk // DN_H]
            ct = src[:, 128 * (blk % DN_H):128 * (blk % DN_H) + 128]
            fn = _qk_math if blk < 2 * DN_H else _silu
            _, vjp = jax.vjp(fn, conv_ref[:, sl])
            dconv_ref[:, sl] = vjp(ct)[0]
        _, vjp = jax.vjp(_gate_math, bd_ref[...], al_ref[...], dt_ref[...])
        dbd, dal, ddt = vjp(dbg_ref[...])
        dbd_ref[...] = dbd

        @pl.when(i == 0)
        def _():
            dal_ref[...] = dal
            ddt_ref[...] = ddt

        @pl.when(i > 0)
        def _():
            dal_ref[...] += dal
            ddt_ref[...] += ddt

    one = pl.BlockSpec((1, 128), lambda i: (0, 0))
    row = pl.BlockSpec((tm, wq), lambda i: (i, 0))
    hd = pl.BlockSpec((tm, DNW), lambda i: (i, 0))
    st = pl.BlockSpec((tm, 128), lambda i: (i, 0))
    return pl.pallas_call(
        body, grid=(T // tm,),
        in_specs=[row, pl.BlockSpec((tm, 128), lambda i: (i, BD_BLK)), one, one, hd, hd, hd, st],
        out_specs=[row, st, one, one],
        out_shape=[jax.ShapeDtypeStruct((T, wq), F32), jax.ShapeDtypeStruct((T, 128), F32),
                   jax.ShapeDtypeStruct((1, 128), F32), jax.ShapeDtypeStruct((1, 128), F32)],
        name=name, compiler_params=_cp())(conv, z_dn, alog_row, dtb_row, dqn, dkn, dvn, dbg)


def _dn_conv_bwd(dconv, z_dn, conv_w8, name):
    tm = 256
    wq = 3 * DNW
    last = T // tm - 1

    def body(dc_ref, dcn_ref, raw_ref, halo_ref, w_ref, draw_ref, dw_ref):
        i = pl.program_id(0)
        dc = dc_ref[...]
        nxt = jnp.where(i < last, dcn_ref[...], 0.0)
        cur = raw_ref[...]
        halo = jnp.where(i > 0, halo_ref[...], 0.0)
        w = w_ref[...]
        draw = jnp.zeros((tm, wq), F32)
        rows = []
        for j in range(4):
            draw = draw + _shift_up(dc, nxt, 3 - j) * w[j:j + 1, :]
            rows.append(jnp.sum(dc * _shift_down(cur, halo, 3 - j), axis=0, keepdims=True))
        draw_ref[...] = draw
        part = jnp.concatenate(rows + [jnp.zeros((4, wq), F32)], axis=0)

        @pl.when(i == 0)
        def _():
            dw_ref[...] = part

        @pl.when(i > 0)
        def _():
            dw_ref[...] += part

    row = pl.BlockSpec((tm, wq), lambda i: (i, 0))
    return pl.pallas_call(
        body, grid=(T // tm,),
        in_specs=[row, pl.BlockSpec((8, wq), lambda i: (jnp.minimum((i + 1) * (tm // 8), T // 8 - 1), 0)),
                  row, pl.BlockSpec((8, wq), lambda i: (jnp.maximum(i * (tm // 8) - 1, 0), 0)),
                  pl.BlockSpec((8, wq), lambda i: (0, 0))],
        out_specs=[row, pl.BlockSpec((8, wq), lambda i: (0, 0))],
        out_shape=[jax.ShapeDtypeStruct((T, wq), F32), jax.ShapeDtypeStruct((8, wq), F32)],
        name=name, compiler_params=_cp())(dconv, dconv, z_dn, z_dn, conv_w8)


def _h3(a, b, dims=NN):
    return lax.dot_general(a, b, dims, precision=lax.Precision.HIGH, preferred_element_type=F32)


@jax.custom_vjp
def _inverse_given(a_mat, tinv):
    return tinv


def _inverse_given_fwd(a_mat, tinv):
    return tinv, tinv


def _inverse_given_bwd(tinv, g):
    return -_h3(tinv, _h3(g, tinv, NT), TN), jnp.zeros_like(tinv)


_inverse_given.defvjp(_inverse_given_fwd, _inverse_given_bwd)


def _prep_head(q, k, v, bgc, h):
    beta = _col(bgc, h)
    gc = jnp.broadcast_to(_col(bgc, DN_H + h), (PAIR, 128))
    glast = jnp.broadcast_to(_col(bgc, 2 * DN_H + h), (PAIR, 128))
    ri = _iota((PAIR, PAIR), 0)
    ci = _iota((PAIR, PAIR), 1)
    same = (ri // CH) == (ci // CH)
    causal = same & (ci <= ri)
    strict = same & (ci < ri)
    eye = ri == ci
    gc_cols = _hdot(jnp.ones((PAIR, PAIR), F32), jnp.where(eye, gc, 0.0))
    decay = jnp.exp(jnp.where(causal, gc - gc_cols, NEG))
    egc = jnp.exp(gc)
    kb = k * beta
    a_mat = jnp.where(strict, _bdot_nt(kb, k) * decay, 0.0)
    qs = q * (DN_HD ** -0.5)
    attn = jnp.where(causal, _bdot_nt(qs, k) * decay, 0.0)
    return a_mat, (v * beta, kb * egc, qs * egc, k * jnp.exp(glast - gc), attn, jnp.exp(glast))


def _prep_tail(tinv, ctx):
    vb, kbe, qg, kdec, attn, decb = ctx
    return _h3(tinv, vb), _h3(tinv, kbe), qg, kdec, attn, decb


def _inverses(a_mats):
    eye = jnp.where(_iota((PAIR, PAIR), 0) == _iota((PAIR, PAIR), 1), 1.0, 0.0)
    ps = [-a for a in a_mats]
    tinvs = [eye + p for p in ps]
    for _ in range(5):
        ps = [_h3(p, p) for p in ps]
        tinvs = [t + _h3(t, p) for t, p in zip(tinvs, ps)]
    return tinvs


def _prep_math(q, k, v, bgc, h, tinv_saved):
    a_mat, ctx = _prep_head(q, k, v, bgc, h)
    return _prep_tail(_inverse_given(a_mat, tinv_saved), ctx)


def _dn_prep_fwd(qkv, bg, name):
    rows = 512
    hd = lambda off: pl.BlockSpec((rows, 128), lambda g, h: (g, off + h))
    out = pl.BlockSpec((rows, 128), lambda g, h: (g, h))

    def body(q_ref, k_ref, v_ref, bg_ref, *outs):
        h = pl.program_id(1)
        spans = [slice(PAIR * pr, PAIR * pr + PAIR) for pr in range(rows // PAIR)]
        heads = [_prep_head(q_ref[rs, :], k_ref[rs, :], v_ref[rs, :], bg_ref[rs, :], h) for rs in spans]
        tinvs = _inverses([a for a, _ in heads])
        for rs, tinv, (_, ctx) in zip(spans, tinvs, heads):
            for o_ref, val in zip(outs, _prep_tail(tinv, ctx) + (tinv,)):
                o_ref[rs, :] = val

    return pl.pallas_call(
        body, grid=(T // rows, DN_H),
        in_specs=[hd(0), hd(DN_H), hd(2 * DN_H), pl.BlockSpec((rows, 128), lambda g, h: (g, 0))],
        out_specs=[out] * 7, out_shape=[jax.ShapeDtypeStruct((T, DNW), F32)] * 7,
        name=name, compiler_params=_cp())(qkv, qkv, qkv, bg)


def _dn_prep_bwd(qkv, bg, tinv, cts, name):
    rows = 512
    hd = lambda off: pl.BlockSpec((rows, 128), lambda g, h: (g, off + h))
    out = pl.BlockSpec((rows, 128), lambda g, h: (g, h))
    st = pl.BlockSpec((rows, 128), lambda g, h: (g, 0))

    def body(q_ref, k_ref, v_ref, bg_ref, ti_ref, c0, c1, c2, c3, c4, c5, dq_ref, dk_ref, dv_ref, dbg_ref):
        h = pl.program_id(1)
        spans = [slice(PAIR * pr, PAIR * pr + PAIR) for pr in range(rows // PAIR)]
        tis = [ti_ref[rs, :] for rs in spans]

        def joint(qs, ks, vs, bs):
            heads = [_prep_head(q, k, v, b, h) for q, k, v, b in zip(qs, ks, vs, bs)]
            return [_prep_tail(_inverse_given(a, ti), ctx) for (a, ctx), ti in zip(heads, tis)]

        _, vjp = jax.vjp(joint, *[[r[rs, :] for rs in spans] for r in (q_ref, k_ref, v_ref, bg_ref)])
        dqs, dks, dvs, dbs = vjp([tuple(c[rs, :] for c in (c0, c1, c2, c3, c4, c5)) for rs in spans])
        for rs, dq, dk, dv in zip(spans, dqs, dks, dvs):
            dq_ref[rs, :] = dq
            dk_ref[rs, :] = dk
            dv_ref[rs, :] = dv
        dbg_all = jnp.concatenate(dbs, axis=0)

        @pl.when(h == 0)
        def _():
            dbg_ref[...] = dbg_all

        @pl.when(h > 0)
        def _():
            dbg_ref[...] += dbg_all

    return pl.pallas_call(
        body, grid=(T // rows, DN_H),
        in_specs=[hd(0), hd(DN_H), hd(2 * DN_H), st] + [out] * 7,
        out_specs=[out, out, out, st],
        out_shape=[jax.ShapeDtypeStruct((T, DNW), F32)] * 3 + [jax.ShapeDtypeStruct((T, 128), F32)],
        name=name, compiler_params=_cp())(qkv, qkv, qkv, bg, tinv, *cts)


def _step_math(s, u, w, qg, kdec, attn, decb, sub):
    vnew = u - _bdot_nn(w, s)
    z = jnp.zeros((CH, 128), F32)
    vfull = jnp.concatenate([vnew, z] if sub == 0 else [z, vnew], axis=0)
    o = _bdot_nn(qg, s) + _bdot_nn(attn, vfull)
    dec = jnp.sum(decb, axis=0, keepdims=True) * (1.0 / CH)
    return s * dec + _bdot_tn(kdec, vnew), o


def _dn_scan_fwd(prep, name):
    npair = T // PAIR
    row = pl.BlockSpec((PAIR, DNW), lambda p: (p, 0))

    def body(u_ref, w_ref, qg_ref, kd_ref, at_ref, db_ref, o_ref, ss_ref, s_ref):
        @pl.when(pl.program_id(0) == 0)
        def _():
            s_ref[...] = jnp.zeros_like(s_ref)

        states = [s_ref[h] for h in range(DN_H)]
        for sub in range(2):
            rs = slice(CH * sub, CH * sub + CH)
            for h in range(DN_H):
                ls = slice(128 * h, 128 * h + 128)
                ss_ref[sub, h] = states[h]
                states[h], o = _step_math(states[h], u_ref[rs, ls], w_ref[rs, ls], qg_ref[rs, ls],
                                          kd_ref[rs, ls], at_ref[rs, ls], db_ref[rs, ls], sub)
                o_ref[rs, ls] = o
        for h in range(DN_H):
            s_ref[h] = states[h]

    return pl.pallas_call(
        body, grid=(npair,), in_specs=[row] * 6,
        out_specs=[row, pl.BlockSpec((2, DN_H, 128, 128), lambda p: (p, 0, 0, 0))],
        out_shape=[jax.ShapeDtypeStruct((T, DNW), F32), jax.ShapeDtypeStruct((T // CH, DN_H, 128, 128), F32)],
        scratch_shapes=[pltpu.VMEM((DN_H, 128, 128), F32)],
        name=name, compiler_params=_cp())(*prep)


def _dn_scan_bwd(prep, states, do, name):
    npair = T // PAIR
    row = pl.BlockSpec((PAIR, DNW), lambda p: (npair - 1 - p, 0))

    def body(u_ref, w_ref, qg_ref, kd_ref, at_ref, db_ref, ss_ref, do_ref, *rest):
        outs, ds_ref = rest[:6], rest[6]

        @pl.when(pl.program_id(0) == 0)
        def _():
            ds_ref[...] = jnp.zeros_like(ds_ref)

        dss = [ds_ref[h] for h in range(DN_H)]
        for sub in (1, 0):
            rs = slice(CH * sub, CH * sub + CH)
            for h in range(DN_H):
                ls = slice(128 * h, 128 * h + 128)
                args = (ss_ref[sub, h],) + tuple(r[rs, ls] for r in (u_ref, w_ref, qg_ref, kd_ref, at_ref, db_ref))
                _, vjp = jax.vjp(functools.partial(_step_math, sub=sub), *args)
                cts = vjp((dss[h], do_ref[rs, ls]))
                dss[h] = cts[0]
                for o_ref, val in zip(outs, cts[1:]):
                    o_ref[rs, ls] = val
        for h in range(DN_H):
            ds_ref[h] = dss[h]

    return pl.pallas_call(
        body, grid=(npair,),
        in_specs=[row] * 6 + [pl.BlockSpec((2, DN_H, 128, 128), lambda p: (npair - 1 - p, 0, 0, 0)), row],
        out_specs=[row] * 6, out_shape=[jax.ShapeDtypeStruct((T, DNW), F32)] * 6,
        scratch_shapes=[pltpu.VMEM((DN_H, 128, 128), F32)],
        name=name, compiler_params=_cp())(*prep, states, do)


def _post_math(o, gate, wrow):
    return o * lax.rsqrt(jnp.mean(o * o, axis=-1, keepdims=True) + NORM_EPS) * wrow * _silu(gate)


def _dn_post_fwd(o, z_dn, dn_norm, name):
    tm = 512
    row = pl.BlockSpec((tm, DNW), lambda i: (i, 0))

    def body(o_ref, g_ref, w_ref, y_ref):
        for h in range(DN_H):
            ls = slice(128 * h, 128 * h + 128)
            y_ref[:, ls] = _post_math(o_ref[:, ls], g_ref[:, ls], w_ref[...]).astype(BF)

    return pl.pallas_call(
        body, grid=(T // tm,),
        in_specs=[row, pl.BlockSpec((tm, DNW), lambda i: (i, 3)), pl.BlockSpec((1, 128), lambda i: (0, 0))],
        out_specs=row, out_shape=jax.ShapeDtypeStruct((T, DNW), BF),
        name=name, compiler_params=_cp())(o, z_dn, dn_norm)


def _dn_post_bwd(o, z_dn, dn_norm, dy, name):
    tm = 512
    row = pl.BlockSpec((tm, DNW), lambda i: (i, 0))
    one = pl.BlockSpec((1, 128), lambda i: (0, 0))

    def body(o_ref, g_ref, w_ref, dy_ref, do_ref, dg_ref, dw_ref):
        i = pl.program_id(0)
        dw = jnp.zeros((1, 128), F32)
        for h in range(DN_H):
            ls = slice(128 * h, 128 * h + 128)
            _, vjp = jax.vjp(_post_math, o_ref[:, ls], g_ref[:, ls], w_ref[...])
            do, dg, dwh = vjp(dy_ref[:, ls].astype(F32))
            do_ref[:, ls] = do
            dg_ref[:, ls] = dg
            dw = dw + dwh

        @pl.when(i == 0)
        def _():
            dw_ref[...] = dw

        @pl.when(i > 0)
        def _():
            dw_ref[...] += dw

    return pl.pallas_call(
        body, grid=(T // tm,),
        in_specs=[row, pl.BlockSpec((tm, DNW), lambda i: (i, 3)), one, pl.BlockSpec((tm, DNW), lambda i: (i, 1))],
        out_specs=[row, row, one],
        out_shape=[jax.ShapeDtypeStruct((T, DNW), F32), jax.ShapeDtypeStruct((T, DNW), F32),
                   jax.ShapeDtypeStruct((1, 128), F32)],
        name=name, compiler_params=_cp())(o, z_dn, dn_norm, dy)


def _dz_dn_assemble(draw, dgate, dbd, name):
    tm = 512

    def body(a_ref, b_ref, c_ref, o_ref):
        o_ref[:, :3 * DNW] = a_ref[...].astype(BF)
        o_ref[:, 3 * DNW:4 * DNW] = b_ref[...].astype(BF)
        o_ref[:, 4 * DNW:4 * DNW + 128] = c_ref[...].astype(BF)
        o_ref[:, 4 * DNW + 128:] = jnp.zeros((tm, 128), BF)

    return pl.pallas_call(
        body, grid=(T // tm,),
        in_specs=[pl.BlockSpec((tm, 3 * DNW), lambda i: (i, 0)), pl.BlockSpec((tm, DNW), lambda i: (i, 0)),
                  pl.BlockSpec((tm, 128), lambda i: (i, 0))],
        out_specs=pl.BlockSpec((tm, ZD), lambda i: (i, 0)),
        out_shape=jax.ShapeDtypeStruct((T, ZD), BF), name=name, compiler_params=_cp())(draw, dgate, dbd)


HBM = pl.BlockSpec(memory_space=pltpu.HBM)
SEM = pl.BlockSpec(memory_space=pltpu.SEMAPHORE)
EFFECT = pltpu.SideEffectType.DATAFLOW_SIDE_EFFECTING
N_PEER = N_DEV - 1


def _peers(x, y, c):
    return [(k, (x ^ (k >> 2), y ^ ((k >> 1) & 1), c ^ (k & 1))) for k in (1, 2, 4, 3, 5, 6, 7)]


def _exchange_copy(ins, lands, ssems, rsems, scatter, t, k, pos, me):
    px, py, pc = pos
    src = ins[t].at[4 * px + 2 * py + pc] if scatter else ins[t]
    return pltpu.make_async_remote_copy(
        src_ref=src, dst_ref=lands[t].at[me], send_sem=ssems[t].at[k - 1], recv_sem=rsems[t].at[k - 1],
        device_id=pos, device_id_type=MESH_ID)


def _xstart(bufs, scatter, name):
    nt = len(bufs)
    lands = [lax.empty((N_DEV,) + tuple(b.shape[1:] if scatter else b.shape), b.dtype) for b in bufs]

    def body(*refs):
        ins, lnd = refs[:nt], refs[nt:2 * nt]
        ssems, rsems = refs[2 * nt:3 * nt], refs[3 * nt:4 * nt]
        token = refs[-1]
        x, y, c = lax.axis_index("x"), lax.axis_index("y"), lax.axis_index("c")
        me = 4 * x + 2 * y + c
        for t in range(nt):
            for k, pos in _peers(x, y, c):
                _exchange_copy(ins, lnd, ssems, rsems, scatter, t, k, pos, me).start()
        token[...] = jnp.zeros_like(token)

    both = list(bufs) + lands
    res = pl.pallas_call(
        body, name=name,
        out_shape=[pltpu.SemaphoreType.DMA((N_PEER,))] * (2 * nt)
        + [pltpu.HBM(b.shape, b.dtype) for b in both] + [jax.ShapeDtypeStruct((8, 128), F32)],
        in_specs=[HBM] * (2 * nt),
        out_specs=[SEM] * (2 * nt) + [HBM] * (2 * nt) + [pl.BlockSpec(memory_space=pltpu.VMEM)],
        input_output_aliases={i: 2 * nt + i for i in range(2 * nt)},
        compiler_params=pltpu.CompilerParams(has_side_effects=EFFECT),
    )(*[pltpu.with_memory_space_constraint(b, pltpu.HBM) for b in both])
    return res[:nt], res[nt:2 * nt], res[2 * nt:3 * nt], res[3 * nt:4 * nt], res[-1][0, 0]


def _xwait(ssems, rsems, thrus, lands, scatter, after, name):
    nt = len(lands)

    def body(*refs):
        ins, lnd = refs[:nt], refs[nt:2 * nt]
        ss, rs = refs[2 * nt:3 * nt], refs[3 * nt:4 * nt]
        x, y, c = lax.axis_index("x"), lax.axis_index("y"), lax.axis_index("c")
        me = 4 * x + 2 * y + c
        for t in range(nt):
            for k, pos in _peers(x, y, c):
                cp = _exchange_copy(ins, lnd, ss, rs, scatter, t, k, pos, me)
                cp.wait_send()
                cp.wait_recv()

    both = list(thrus) + list(lands)
    res = pl.pallas_call(
        body, name=name, out_shape=[pltpu.HBM(b.shape, b.dtype) for b in both],
        in_specs=[HBM] * (2 * nt) + [SEM] * (2 * nt) + [ANY], out_specs=[HBM] * (2 * nt),
        input_output_aliases={i: i for i in range(2 * nt)},
        compiler_params=pltpu.CompilerParams(has_side_effects=EFFECT),
    )(*both, *ssems, *rsems, after)
    return res[:nt], res[nt:]


def _adam(recv, w, m, v, tr, name):
    _, r, c = w.shape
    n_part = recv.shape[0]
    c1 = np.float32(1.0 - ADAM_B1 ** ADAM_STEP)
    c2 = np.float32(1.0 - ADAM_B2 ** ADAM_STEP)

    def body(r_ref, w_ref, m_ref, v_ref, g_ref, d_ref, mo_ref, vo_ref):
        g = r_ref[0].astype(F32)
        for s in range(1, n_part):
            g = g + r_ref[s].astype(F32)
        mn = ADAM_B1 * m_ref[0] + (1.0 - ADAM_B1) * g
        vn = ADAM_B2 * v_ref[0] + (1.0 - ADAM_B2) * (g * g)
        g_ref[0] = g
        mo_ref[0] = mn
        vo_ref[0] = vn
        d_ref[0] = -ADAM_LR * ((mn / c1) / (jnp.sqrt(vn / c2) + ADAM_EPS) + ADAM_WD * w_ref[0])

    one = pl.BlockSpec((1, tr, c), lambda i: (0, i, 0))
    return pl.pallas_call(
        body, grid=(r // tr,), in_specs=[pl.BlockSpec((n_part, tr, c), lambda i: (0, i, 0)), one, one, one],
        out_specs=[one] * 4, out_shape=[jax.ShapeDtypeStruct((1, r, c), F32)] * 4,
        name=name, compiler_params=_cp())(recv, w, m, v)


def _local_step(x, target, sp, need, emit):
    g = {}
    x0, y1, saved1 = _ffn_fwd(x, None, 0.0, sp["norm_ffn1"], lambda kind, a: need("w" + kind + "1", a), "ffn1")
    x1, h2 = _rms_fwd(x0, y1, 0.5, sp["norm_mix"], "mix_norm")
    win_a, win_d = need("win_a", h2), need("win_d", h2)
    conv_w8, wout = need("conv_w8", h2), need("wout", h2)
    z_at = _mm_nn(h2, win_a, F32, 1024, 768, "mix_in_attn")
    z_dn = _mm_nn(h2, win_d, F32, 1024, 768, "mix_in_dn")
    attn_f, attn_b, lse = _attn_fwd(z_at, "attn_fwd")

    conv, qkvn, bg = _dn_pre_fwd(z_dn, conv_w8, sp["alog_row"], sp["dtb_row"], "dn_pre")
    *prep, tinv = _dn_prep_fwd(qkvn, bg, "dn_prep")
    o_dn, states = _dn_scan_fwd(prep, "dn_scan")
    dn_b = _dn_post_fwd(o_dn, z_dn, sp["dn_norm"], "dn_post")

    y2 = _mm_nn2(attn_b, dn_b, wout[:AW], wout[AW:], 1024, 1024, "mix_out")
    x2, y3, saved2 = _ffn_fwd(x1, y2, 1.0, sp["norm_ffn2"], lambda kind, a: need("w" + kind + "2", a), "ffn2")

    dx3, dys3, g["norm_final"], loss8 = _loss_bwd(x2, y3, sp["norm_final"], target, "loss")
    dx2, dx2b, g["norm_ffn2"] = _ffn_bwd(
        x2, sp["norm_ffn2"], saved2, dx3, dys3, 1.0,
        lambda kind, dw: emit(kind + "2", {"w" + kind + "2": dw}), "ffn2b")

    zero = emit("wout", {"wout": jnp.concatenate([_mm_tn(attn_b, dx2b, BF, 512, 1024, "mix_out_dw_a"),
                                                  _mm_tn(dn_b, dx2b, BF, 512, 1024, "mix_out_dw_d")], axis=0)})
    dmix = _mm_nt(_after(dx2b, zero), wout, F32, 1024, 1024, "mix_out_dx")

    dz_at = jnp.concatenate(_attn_bwd(z_at, dmix, attn_f, lse, "attn_bwd"), axis=1)

    do_dn, dgate, g["dn_norm"] = _dn_post_bwd(o_dn, z_dn, sp["dn_norm"], dmix, "dn_post_b")
    cts = _dn_scan_bwd(prep, states, do_dn, "dn_scan_b")
    dqn, dkn, dvn, dbg = _dn_prep_bwd(qkvn, bg, tinv, cts, "dn_prep_b")
    dconv, dbd, g["alog_row"], g["dtb_row"] = _dn_pre_bwd(
        conv, z_dn, sp["alog_row"], sp["dtb_row"], dqn, dkn, dvn, dbg, "dn_pre_b")
    draw, dconv_w8 = _dn_conv_bwd(dconv, z_dn, conv_w8, "dn_conv_b")
    dz_dn = _dz_dn_assemble(draw, dgate, dbd, "dn_dz")

    zero = emit("win", {"win_a": _mm_tn(h2, dz_at, BF, 512, 768, "mix_in_dw_a"),
                        "win_d": _mm_tn(h2, dz_dn, BF, 512, 768, "mix_in_dw_d"), "conv_w8": dconv_w8})
    dh2 = _mm_nt(_after(dz_at, zero), win_a, F32, 1024, 1024, "mix_in_dx_a")
    dh2d = _mm_nt(dz_dn, win_d, F32, 1024, 1024, "mix_in_dx_d")
    (dx1, dys1), _, g["norm_mix"] = _rms_bwd2(x1, sp["norm_mix"], dh2, dh2d, dx2, "mix_dnorm")
    dx0, _, g["norm_ffn1"] = _ffn_bwd(
        x0, sp["norm_ffn1"], saved1, dx1, dys1, 1.0,
        lambda kind, dw: emit(kind + "1", {"w" + kind + "1": dw}), "ffn1b")
    return loss8[0, 0], dx0, g


def _rms_bwd2(x, gain, dh_a, dh_b, dres, name):
    tm = 512
    row = pl.BlockSpec((tm, D), lambda i: (i, 0))
    gspec = pl.BlockSpec((1, D), lambda i: (0, 0))

    def body(x_ref, g_ref, da_ref, db_ref, dres_ref, dx_ref, dxs_ref, dg_ref):
        i = pl.program_id(0)
        xv = x_ref[...]
        r = lax.rsqrt(jnp.mean(xv * xv, axis=-1, keepdims=True) + NORM_EPS)
        xh = xv * r
        dhv = da_ref[...] + db_ref[...]
        part = jnp.sum(dhv * xh, axis=0, keepdims=True)

        @pl.when(i == 0)
        def _():
            dg_ref[...] = part

        @pl.when(i > 0)
        def _():
            dg_ref[...] += part

        dxh = dhv * g_ref[...]
        dx = r * (dxh - xh * jnp.mean(dxh * xh, axis=-1, keepdims=True)) + dres_ref[...]
        dx_ref[...] = dx
        dxs_ref[...] = (0.5 * dx).astype(BF)

    dx, dxs, dg = pl.pallas_call(
        body, grid=(T // tm,), in_specs=[row, gspec, row, row, row], out_specs=[row, row, gspec],
        out_shape=[jax.ShapeDtypeStruct((T, D), F32), jax.ShapeDtypeStruct((T, D), BF),
                   jax.ShapeDtypeStruct((1, D), F32)],
        name=name, compiler_params=_cp())(x, gain, dh_a, dh_b, dres)
    return (dx, dxs), None, dg


def _cols_from_shards(gathered):
    n, r, c = gathered.shape
    return jnp.transpose(gathered, (1, 0, 2)).reshape(r, n * c)


def _shards_from_cols(full, dtype):
    r, nc = full.shape
    return jnp.transpose(full.reshape(r, N_DEV, nc // N_DEV), (1, 0, 2)).astype(dtype)


def _lane_row(vec4):
    return jnp.zeros((1, 128), F32).at[:, DN_H:2 * DN_H].set(vec4.astype(F32))


WEIGHT_SOURCES = {"wg1": "gate1", "wu1": "up1", "wd1": "down1", "win_a": "w_in", "win_d": "w_in",
                  "conv_w8": "conv_w", "wout": "w_out", "wg2": "gate2", "wu2": "up2", "wd2": "down2"}
TRANSPOSED = ("gate1", "up1", "gate2", "up2")


def _build_weights(name, gath):
    if name in ("wg1", "wu1", "wd1", "wg2", "wu2", "wd2"):
        return {name: gath[WEIGHT_SOURCES[name]].reshape(F, D)}
    if name in ("win_a", "win_d"):
        w_in = _cols_from_shards(gath["w_in"])
        c0 = 3 * AW + 3 * DNW
        wp = jnp.concatenate([w_in[:, :c0], w_in[:, c0 + 2 * DN_H:], w_in[:, c0:c0 + 2 * DN_H],
                              jnp.zeros((D, ZP - IN_COLS), w_in.dtype)], axis=1)
        return {"win_a": wp[:, :ZA], "win_d": wp[:, ZA:]}
    if name == "wout":
        return {name: gath["w_out"].reshape(D, D)}
    conv = _cols_from_shards(gath["conv_w"])
    return {"conv_w8": jnp.concatenate([conv, jnp.zeros((4, 3 * DNW), F32)], axis=0)}


def _small_params(norm_ffn1, norm_mix, norm_ffn2, norm_final, a_log, dt_bias, dn_norm):
    return {"norm_ffn1": norm_ffn1, "norm_mix": norm_mix, "norm_ffn2": norm_ffn2,
            "norm_final": norm_final.reshape(1, D), "alog_row": _lane_row(a_log), "dtb_row": _lane_row(dt_bias),
            "dn_norm": dn_norm}


def _grad_slabs(group, g):
    if group[0] in "gud":
        return {WEIGHT_SOURCES["w" + group]: g["w" + group].reshape(N_DEV, F // N_DEV, D)}
    if group == "wout":
        return {"w_out": g["wout"].reshape(N_DEV, D // N_DEV, D)}
    gp = jnp.concatenate([g["win_a"], g["win_d"]], axis=1)
    c0 = 3 * AW + 3 * DNW
    g_in = jnp.concatenate([gp[:, :c0], gp[:, c0 + DNW:c0 + DNW + 2 * DN_H], gp[:, c0:c0 + DNW]], axis=1)
    return {"w_in": _shards_from_cols(g_in, BF), "conv_w": _shards_from_cols(g["conv_w8"][:4], F32)}


SMALL_ROWS = 40


def _small_pack(norm_ffn1, norm_mix, norm_ffn2, norm_final, dn_norm, alog_row, dtb_row):
    rows = [a.reshape(8, 128) for a in (norm_ffn1, norm_mix, norm_ffn2, norm_final)]
    rows += [dn_norm.reshape(1, 128), alog_row, dtb_row, jnp.zeros((SMALL_ROWS - 35, 128), F32)]
    return jnp.concatenate(rows, axis=0)


def _small_unpack(pk):
    pk = pk[0]
    return (pk[0:8].reshape(1, D), pk[8:16].reshape(1, D), pk[16:24].reshape(1, D), pk[24:32].reshape(D),
            pk[32:33], pk[33:34, DN_H:2 * DN_H], pk[34:35, DN_H:2 * DN_H])


ADAM_TILE = {"gate1": 256, "up1": 256, "down1": 176, "gate2": 256, "up2": 256, "down2": 176,
             "w_in": 256, "w_out": 128, "conv_w": 4}
BIG = ("gate1", "up1", "down1", "w_in", "w_out", "gate2", "up2", "down2", "conv_w")


def kernel(x, norm_ffn1, ffn1_gate, ffn1_up, ffn1_down, norm_mix, w_in, conv_w, a_log, dt_bias, dn_norm, w_out, norm_ffn2, ffn2_gate, ffn2_up, ffn2_down, norm_final, loss_target, m_norm_ffn1, m_ffn1_gate, m_ffn1_up, m_ffn1_down, m_norm_mix, m_w_in, m_conv_w, m_a_log, m_dt_bias, m_dn_norm, m_w_out, m_norm_ffn2, m_ffn2_gate, m_ffn2_up, m_ffn2_down, m_norm_final, v_norm_ffn1, v_ffn1_gate, v_ffn1_up, v_ffn1_down, v_norm_mix, v_w_in, v_conv_w, v_a_log, v_dt_bias, v_dn_norm, v_w_out, v_norm_ffn2, v_ffn2_gate, v_ffn2_up, v_ffn2_down, v_norm_final):
    w = {"gate1": ffn1_gate, "up1": ffn1_up, "down1": ffn1_down, "w_in": w_in, "w_out": w_out,
         "gate2": ffn2_gate, "up2": ffn2_up, "down2": ffn2_down, "conv_w": conv_w}
    m = {"gate1": m_ffn1_gate, "up1": m_ffn1_up, "down1": m_ffn1_down, "w_in": m_w_in, "w_out": m_w_out,
         "gate2": m_ffn2_gate, "up2": m_ffn2_up, "down2": m_ffn2_down, "conv_w": m_conv_w}
    v = {"gate1": v_ffn1_gate, "up1": v_ffn1_up, "down1": v_ffn1_down, "w_in": v_w_in, "w_out": v_w_out,
         "gate2": v_ffn2_gate, "up2": v_ffn2_up, "down2": v_ffn2_down, "conv_w": v_conv_w}

    me = 4 * lax.axis_index("x") + 2 * lax.axis_index("y") + lax.axis_index("c")
    own_slot = lambda land, mine: lax.dynamic_update_index_in_dim(land, mine, me, 0)

    ag_order = ("gate1", "up1", "down1", "w_in", "conv_w", "w_out", "gate2", "up2", "down2")
    ag_groups = (("gate1",), ("up1",), ("down1",), ("w_in", "conv_w", "w_out"), ("gate2",), ("up2",), ("down2",))
    pos = {n: i for i, n in enumerate(ag_order)}

    def shard(n):
        if n == "conv_w":
            return w[n][0]
        return (w[n][0].T if n in TRANSPOSED else w[n][0]).astype(BF)

    ss, rs, thru, land, zero = _xstart([shard(n) for n in ag_order], False, "weights_start")
    gath, built = {}, {}

    def need(name, after):
        if name not in built:
            src = WEIGHT_SOURCES[name]
            if src not in gath:
                gi = [i for i, grp in enumerate(ag_groups) if src in grp][0]
                ids = [pos[n] for n in ag_groups[gi]]
                thrus, lands = _xwait([ss[i] for i in ids], [rs[i] for i in ids], [thru[i] for i in ids],
                                      [land[i] for i in ids], False, after, "weights_wait%d" % gi)
                for n, t, l in zip(ag_groups[gi], thrus, lands):
                    gath[n] = own_slot(l, t)
            built.update(_build_weights(name, gath))
        return built[name]

    pending = []

    def emit(group, grads):
        slabs = grads if group == "small" else _grad_slabs(group, grads)
        names = list(slabs)
        started = _xstart([slabs[n] for n in names], True, "grads_start_" + group)
        pending.append((group, names) + started[:4])
        return started[4]

    sp = _small_params(norm_ffn1 + zero, norm_mix, norm_ffn2, norm_final, a_log, dt_bias, dn_norm)
    loss_part, dx, g = _local_step(x[0], loss_target[0], sp, need, emit)
    small = _small_pack(g["norm_ffn1"], g["norm_mix"], g["norm_ffn2"], g["norm_final"], g["dn_norm"],
                        g["alog_row"], g["dtb_row"])
    emit("small", {"small": jnp.broadcast_to(small[None], (N_DEV, SMALL_ROWS, 128))})

    pack = lambda a: _small_pack(*a)[None]
    res, after = {}, dx
    for group, names, gss, grs, gthru, gland in pending:
        thrus, lands = _xwait(gss, grs, gthru, gland, True, after, "grads_wait_" + group)
        for n, t, l in zip(names, thrus, lands):
            recv = own_slot(l, lax.dynamic_index_in_dim(t, me, 0, keepdims=False))
            if n == "small":
                res[n] = _adam(
                    recv,
                    pack((norm_ffn1, norm_mix, norm_ffn2, norm_final, dn_norm, _lane_row(a_log), _lane_row(dt_bias))),
                    pack((m_norm_ffn1, m_norm_mix, m_norm_ffn2, m_norm_final, m_dn_norm, _lane_row(m_a_log),
                          _lane_row(m_dt_bias))),
                    pack((v_norm_ffn1, v_norm_mix, v_norm_ffn2, v_norm_final, v_dn_norm, _lane_row(v_a_log),
                          _lane_row(v_dt_bias))),
                    SMALL_ROWS, "adam_small")
            elif n in TRANSPOSED:
                flip = lambda a: jnp.swapaxes(a, 1, 2)
                res[n] = [flip(o) for o in _adam(recv, flip(w[n]), flip(m[n]), flip(v[n]), F // N_DEV // 2,
                                                 "adam_" + n)]
            else:
                res[n] = _adam(recv, w[n], m[n], v[n], ADAM_TILE[n], "adam_" + n)
            after = res[n][0]
    res_s = res["small"]

    loss = lax.psum(loss_part, ("x", "y", "c"))
    outs = [loss, dx[None]]
    for k in range(4):
        n1, nm, n2, nf, dn, al, dt = _small_unpack(res_s[k])
        big = {n: res[n][k] for n in BIG}
        outs += [n1, big["gate1"], big["up1"], big["down1"], nm, big["w_in"], big["conv_w"], al, dt, dn,
                 big["w_out"], n2, big["gate2"], big["up2"], big["down2"], nf]
    return tuple(outs)
```

```python
import functools

import numpy as np
import jax
import jax.numpy as jnp
from jax import lax
from jax.experimental import pallas as pl
from jax.experimental.pallas import tpu as pltpu

T = 4096
D = 1024
F = 2816
N_DEV = 8
A_HEADS = 8
A_HD = 64
AW = A_HEADS * A_HD
DN_H = 4
DN_HD = 128
DNW = DN_H * DN_HD
CH = 64
PAIR = 2 * CH
ZA = 3 * AW
ZD = 3 * DNW + DNW + 256
ZP = ZA + ZD
BD_BLK = (3 * DNW + DNW) // 128
IN_COLS = 3592
PATTERNS = ((128, 1), (512, 4), (2048, 16))
NORM_EPS = 1e-6
L2_EPS = 1e-6
ADAM_LR, ADAM_B1, ADAM_B2, ADAM_EPS, ADAM_WD, ADAM_STEP = 0.001, 0.9, 0.999, 1e-08, 0.01, 10
VMEM_LIMIT = 56 * 1024 * 1024
NEG = -1e30

BF = jnp.bfloat16
F32 = jnp.float32
NN = (((1,), (0,)), ((), ()))
NT = (((1,), (1,)), ((), ()))
TN = (((0,), (0,)), ((), ()))
HI = lax.Precision.HIGHEST
MESH_ID = pl.DeviceIdType.MESH
ANY = pl.BlockSpec(memory_space=pl.ANY)


def _cp():
    return pltpu.CompilerParams(vmem_limit_bytes=VMEM_LIMIT)


def _dg(a, b, dims):
    return lax.dot_general(a, b, dims, preferred_element_type=F32)


def _hdot(a, b):
    return lax.dot_general(a, b, NN, precision=HI, preferred_element_type=F32)


def _make_bdot(dims, da_dims, da_swap, db_dims, db_swap):
    @jax.custom_vjp
    def f(a, b):
        return _dg(a.astype(BF), b.astype(BF), dims)

    def fwd(a, b):
        return f(a, b), (a, b)

    def bwd(res, g):
        a, b = res
        gb, ab, bb = g.astype(BF), a.astype(BF), b.astype(BF)
        da = _dg(bb, gb, da_dims) if da_swap else _dg(gb, bb, da_dims)
        db = _dg(gb, ab, db_dims) if db_swap else _dg(ab, gb, db_dims)
        return da, db

    f.defvjp(fwd, bwd)
    return f


_bdot_nn = _make_bdot(NN, NT, False, TN, False)
_bdot_nt = _make_bdot(NT, NN, False, TN, True)
_bdot_tn = _make_bdot(TN, NT, True, NN, False)


def _iota(shape, dim):
    return lax.broadcasted_iota(jnp.int32, shape, dim)


def _col(x, idx):
    return jnp.sum(jnp.where(_iota(x.shape, 1) == idx, x, 0.0), axis=1, keepdims=True)


def _mm_nn(a, b, out_dtype, tm, tn, name):
    m, k = a.shape
    n = b.shape[1]

    def body(a_ref, b_ref, o_ref):
        o_ref[...] = _dg(a_ref[...], b_ref[...], NN).astype(out_dtype)

    return pl.pallas_call(
        body, grid=(m // tm, n // tn),
        in_specs=[pl.BlockSpec((tm, k), lambda i, j: (i, 0)), pl.BlockSpec((k, tn), lambda i, j: (0, j))],
        out_specs=pl.BlockSpec((tm, tn), lambda i, j: (i, j)),
        out_shape=jax.ShapeDtypeStruct((m, n), out_dtype), name=name, compiler_params=_cp())(a, b)


def _tie(body, after):
    if after is None:
        return body, [], []
    return (lambda tok_ref, *refs: body(*refs)), [ANY], [after.reshape(1, 1)]


def _mm_nt(a, b, out_dtype, tm, tb, name, after=None):
    m, c = a.shape
    kb = b.shape[0]

    def body(a_ref, b_ref, o_ref):
        o_ref[...] = _dg(a_ref[...], b_ref[...], NT).astype(out_dtype)

    body, tspec, tok = _tie(body, after)
    return pl.pallas_call(
        body, grid=(m // tm, kb // tb),
        in_specs=tspec + [pl.BlockSpec((tm, c), lambda i, j: (i, 0)), pl.BlockSpec((tb, c), lambda i, j: (j, 0))],
        out_specs=pl.BlockSpec((tm, tb), lambda i, j: (i, j)),
        out_shape=jax.ShapeDtypeStruct((m, kb), out_dtype), name=name, compiler_params=_cp())(*tok, a, b)


def _mm_tn(a, b, out_dtype, ta, tb, name, after=None):
    m, ka = a.shape
    nb = b.shape[1]

    def body(a_ref, b_ref, o_ref):
        o_ref[...] = _dg(a_ref[...], b_ref[...], TN).astype(out_dtype)

    body, tspec, tok = _tie(body, after)
    return pl.pallas_call(
        body, grid=(ka // ta, nb // tb),
        in_specs=tspec + [pl.BlockSpec((m, ta), lambda i, j: (0, i)), pl.BlockSpec((m, tb), lambda i, j: (0, j))],
        out_specs=pl.BlockSpec((ta, tb), lambda i, j: (i, j)),
        out_shape=jax.ShapeDtypeStruct((ka, nb), out_dtype), name=name, compiler_params=_cp())(*tok, a, b)


def _mm_nn2(a1, a2, b1, b2, tm, tn, name, after=None):
    m, k = a1.shape
    n = b1.shape[1]

    def body(a1_ref, a2_ref, b1_ref, b2_ref, o_ref):
        o_ref[...] = _dg(a1_ref[...], b1_ref[...], NN) + _dg(a2_ref[...], b2_ref[...], NN)

    body, tspec, tok = _tie(body, after)
    arow = pl.BlockSpec((tm, k), lambda i, j: (i, 0))
    bcol = pl.BlockSpec((k, tn), lambda i, j: (0, j))
    return pl.pallas_call(
        body, grid=(m // tm, n // tn), in_specs=tspec + [arow, arow, bcol, bcol],
        out_specs=pl.BlockSpec((tm, tn), lambda i, j: (i, j)),
        out_shape=jax.ShapeDtypeStruct((m, n), F32), name=name, compiler_params=_cp())(*tok, a1, a2, b1, b2)


def _mm_nt_swiglu(h, wu_t, gate, tm, tb, name):
    m, c = h.shape
    kb = wu_t.shape[0]

    def body(h_ref, w_ref, g_ref, u_ref, a_ref):
        u = _dg(h_ref[...], w_ref[...], NT)
        g = g_ref[...].astype(F32)
        u_ref[...] = u.astype(BF)
        a_ref[...] = (g * jax.nn.sigmoid(g) * u).astype(BF)

    tile = pl.BlockSpec((tm, tb), lambda i, j: (i, j))
    return pl.pallas_call(
        body, grid=(m // tm, kb // tb),
        in_specs=[pl.BlockSpec((tm, c), lambda i, j: (i, 0)), pl.BlockSpec((tb, c), lambda i, j: (j, 0)), tile],
        out_specs=[tile, tile], out_shape=[jax.ShapeDtypeStruct((m, kb), BF)] * 2,
        name=name, compiler_params=_cp())(h, wu_t, gate)


def _mm_nt_dswiglu(dys, wd, gate, up, tm, tb, name, after=None):
    m, c = dys.shape
    kb = wd.shape[0]

    def body(d_ref, w_ref, g_ref, u_ref, dg_ref, du_ref):
        da = _dg(d_ref[...], w_ref[...], NT)
        g = g_ref[...].astype(F32)
        u = u_ref[...].astype(F32)
        s = jax.nn.sigmoid(g)
        dg_ref[...] = (da * u * (s * (1.0 + g * (1.0 - s)))).astype(BF)
        du_ref[...] = (da * (g * s)).astype(BF)

    body, tspec, tok = _tie(body, after)
    tile = pl.BlockSpec((tm, tb), lambda i, j: (i, j))
    return pl.pallas_call(
        body, grid=(m // tm, kb // tb),
        in_specs=tspec + [pl.BlockSpec((tm, c), lambda i, j: (i, 0)), pl.BlockSpec((tb, c), lambda i, j: (j, 0)),
                          tile, tile],
        out_specs=[tile, tile], out_shape=[jax.ShapeDtypeStruct((m, kb), BF)] * 2,
        name=name, compiler_params=_cp())(*tok, dys, wd, gate, up)


def _rms_fwd(x, y, alpha, gain, name):
    tm = 512
    has_y = y is not None
    row = pl.BlockSpec((tm, D), lambda i: (i, 0))
    gspec = pl.BlockSpec((1, D), lambda i: (0, 0))

    def body(*refs):
        if has_y:
            x_ref, y_ref, g_ref, xo_ref, h_ref = refs
            xv = x_ref[...] + alpha * y_ref[...]
            xo_ref[...] = xv
        else:
            x_ref, g_ref, h_ref = refs
            xv = x_ref[...]
        r = lax.rsqrt(jnp.mean(xv * xv, axis=-1, keepdims=True) + NORM_EPS)
        h_ref[...] = (xv * r * g_ref[...]).astype(BF)

    if has_y:
        return pl.pallas_call(
            body, grid=(T // tm,), in_specs=[row, row, gspec], out_specs=[row, row],
            out_shape=[jax.ShapeDtypeStruct((T, D), F32), jax.ShapeDtypeStruct((T, D), BF)],
            name=name, compiler_params=_cp())(x, y, gain)
    h = pl.pallas_call(
        body, grid=(T // tm,), in_specs=[row, gspec], out_specs=row,
        out_shape=jax.ShapeDtypeStruct((T, D), BF), name=name, compiler_params=_cp())(x, gain)
    return x, h


def _rms_bwd(x, gain, dh, dres, alpha_out, name):
    tm = 512
    row = pl.BlockSpec((tm, D), lambda i: (i, 0))
    gspec = pl.BlockSpec((1, D), lambda i: (0, 0))

    def body(x_ref, g_ref, dh_ref, dres_ref, dx_ref, dxs_ref, dg_ref):
        i = pl.program_id(0)
        xv = x_ref[...]
        r = lax.rsqrt(jnp.mean(xv * xv, axis=-1, keepdims=True) + NORM_EPS)
        xh = xv * r
        dhv = dh_ref[...].astype(F32)
        part = jnp.sum(dhv * xh, axis=0, keepdims=True)

        @pl.when(i == 0)
        def _():
            dg_ref[...] = part

        @pl.when(i > 0)
        def _():
            dg_ref[...] += part

        dxh = dhv * g_ref[...]
        dx = r * (dxh - xh * jnp.mean(dxh * xh, axis=-1, keepdims=True)) + dres_ref[...]
        dx_ref[...] = dx
        dxs_ref[...] = (alpha_out * dx).astype(BF)

    return pl.pallas_call(
        body, grid=(T // tm,), in_specs=[row, gspec, row, row], out_specs=[row, row, gspec],
        out_shape=[jax.ShapeDtypeStruct((T, D), F32), jax.ShapeDtypeStruct((T, D), BF),
                   jax.ShapeDtypeStruct((1, D), F32)],
        name=name, compiler_params=_cp())(x, gain, dh, dres)


def _loss_bwd(x_prev, y, gain, target, name):
    tm = 512
    row = pl.BlockSpec((tm, D), lambda i: (i, 0))
    gspec = pl.BlockSpec((1, D), lambda i: (0, 0))
    lspec = pl.BlockSpec((8, 128), lambda i: (0, 0))

    def body(x_ref, y_ref, g_ref, t_ref, dx_ref, dxs_ref, dg_ref, loss_ref):
        i = pl.program_id(0)
        xv = x_ref[...] + 0.5 * y_ref[...]
        r = lax.rsqrt(jnp.mean(xv * xv, axis=-1, keepdims=True) + NORM_EPS)
        xh = xv * r
        diff = xh * g_ref[...] - t_ref[...]
        lpart = 0.5 * jnp.sum(jnp.mean(diff * diff, axis=-1, keepdims=True), axis=0, keepdims=True)
        dy = diff * (1.0 / D)
        part = jnp.sum(dy * xh, axis=0, keepdims=True)

        @pl.when(i == 0)
        def _():
            dg_ref[...] = part
            loss_ref[...] = jnp.broadcast_to(lpart, (8, 128))

        @pl.when(i > 0)
        def _():
            dg_ref[...] += part
            loss_ref[...] += jnp.broadcast_to(lpart, (8, 128))

        dxh = dy * g_ref[...]
        dx = r * (dxh - xh * jnp.mean(dxh * xh, axis=-1, keepdims=True))
        dx_ref[...] = dx
        dxs_ref[...] = (0.5 * dx).astype(BF)

    return pl.pallas_call(
        body, grid=(T // tm,), in_specs=[row, row, gspec, row], out_specs=[row, row, gspec, lspec],
        out_shape=[jax.ShapeDtypeStruct((T, D), F32), jax.ShapeDtypeStruct((T, D), BF),
                   jax.ShapeDtypeStruct((1, D), F32), jax.ShapeDtypeStruct((8, 128), F32)],
        name=name, compiler_params=_cp())(x_prev, y, gain, target)


def _ffn_fwd(x_prev, y_prev, alpha, gain, need, tag):
    x_in, h = _rms_fwd(x_prev, y_prev, alpha, gain, tag + "_norm")
    wg = need("g", h)
    gate = _mm_nt(h, wg, BF, 1024, 1408, tag + "_gate")
    wu = need("u", gate)
    up, act = _mm_nt_swiglu(h, wu, gate, 1024, 1408, tag + "_up_act")
    wd = need("d", up)
    y = _mm_nn(act, wd, F32, 1024, 1024, tag + "_down")
    return x_in, y, (h, gate, up, act, wg, wu, wd)


def _ffn_bwd(x_in, gain, saved, dxo, dys, alpha_out, emit, tag):
    h, gate, up, act, wg, wu, wd = saved
    sent = emit("d", _mm_tn(act, dys, BF, 256, 1024, tag + "_dwd"))
    dgate, dup = _mm_nt_dswiglu(dys, wd, gate, up, 1024, 1408, tag + "_dact", after=sent)
    sent = emit("g", _mm_tn(dgate, h, BF, 256, 1024, tag + "_dwg"))
    sent = emit("u", _mm_tn(dup, h, BF, 256, 1024, tag + "_dwu", after=sent))
    dh = _mm_nn2(dgate, dup, wg, wu, 512, 512, tag + "_dh", after=sent)
    dx, dxs, dgain = _rms_bwd(x_in, gain, dh, dxo, alpha_out, tag + "_dnorm")
    return dx, dxs, dgain


SLAB = 2048
N_SLAB = T // SLAB
N_PAIR = A_HEADS // 2


def _pair_masks():
    lane = _iota((128, 128), 1)
    return lane < A_HD, lane >= A_HD


def _slope_table():
    h = 2 * jnp.arange(N_PAIR)[:, None] + jnp.minimum(jnp.arange(8), 1)[None, :]
    return jnp.broadcast_to((2.0 ** (-(h + 1).astype(F32)))[:, :, None], (N_PAIR, 8, 128))


def _rows(ref, start, d):
    if d == 1:
        return ref[pl.ds(start, 128), :]
    return ref[pl.ds(start, 128, stride=d), :]


def _put_rows(ref, start, d, val):
    if d == 1:
        ref[pl.ds(start, 128), :] = val
    else:
        ref[pl.ds(start, 128, stride=d), :] = val


def _units(d):
    return [(r, b, r + 128 * d * b) for r in range(d) for b in range(SLAB // (128 * d))]


def _band(d, prev_valid):
    qi = _iota((128, 256), 0)
    kj = _iota((128, 256), 1)
    steps = qi + 128 - kj
    valid = (steps >= 0) & (steps <= 128) & (prev_valid | (kj >= 128))
    return valid, (steps * d).astype(F32)


def _attn_fwd(z_at, name):
    def body(sl_ref, q_ref, kc_ref, kp_ref, vc_ref, vp_ref, of_ref, ob_ref, lse_ref, m_s, l_s, a_s):
        n = pl.program_id(1)
        lo, hi = _pair_masks()
        slopes = (sl_ref[0, 0:1, 0:1], sl_ref[0, 1:2, 0:1])

        def unit(d, start, b, first, carry):
            q = _rows(q_ref, start, d).astype(BF)
            kcur, vcur = _rows(kc_ref, start, d).astype(BF), _rows(vc_ref, start, d).astype(BF)
            if b > 0:
                (kprev, vprev), prev_valid = carry, True
            else:
                pstart = start + SLAB - 128 * d
                kprev, vprev = _rows(kp_ref, pstart, d).astype(BF), _rows(vp_ref, pstart, d).astype(BF)
                prev_valid = n > 0
            kcat = jnp.concatenate([kprev, kcur], axis=0)
            vcat = jnp.concatenate([vprev, vcur], axis=0)
            valid, dist = _band(d, prev_valid)
            ms, ls, pvs = [], [], []
            for e in range(2):
                qm = jnp.where(lo if e == 0 else hi, q, jnp.zeros_like(q))
                s = _dg(qm, kcat, NT) * (A_HD ** -0.5) - slopes[e] * dist
                s = jnp.where(valid, s, NEG)
                m = jnp.max(s, axis=1, keepdims=True)
                p = jnp.exp(s - m)
                ms.append(m)
                ls.append(jnp.sum(p, axis=1, keepdims=True))
                pvs.append(_dg(p.astype(BF), vcat, NN))
            m_u = jnp.where(lo, ms[0], ms[1])
            l_u = jnp.where(lo, ls[0], ls[1])
            a_u = jnp.where(lo, pvs[0], pvs[1])
            if first:
                m_n, l_n, a_n = m_u, l_u, a_u
            else:
                m_o = _rows(m_s, start, d)
                m_n = jnp.maximum(m_o, m_u)
                c_o = jnp.exp(m_o - m_n)
                c_u = jnp.exp(m_u - m_n)
                l_n = _rows(l_s, start, d) * c_o + l_u * c_u
                a_n = _rows(a_s, start, d) * c_o + a_u * c_u
            _put_rows(m_s, start, d, m_n)
            _put_rows(l_s, start, d, l_n)
            _put_rows(a_s, start, d, a_n)
            return kcur, vcur

        for pi, (_, d) in enumerate(PATTERNS):
            carry = None
            for r, b, start in _units(d):
                carry = unit(d, start, b, pi == 0, carry)
        l = l_s[...]
        out = a_s[...] / l
        of_ref[...] = out
        ob_ref[...] = out.astype(BF)
        lse_ref[...] = m_s[...] + jnp.log(l)

    cur = lambda c: pl.BlockSpec((SLAB, 128), lambda j, n: (n, c * N_PAIR + j))
    prv = lambda c: pl.BlockSpec((SLAB, 128), lambda j, n: (jnp.maximum(n - 1, 0), c * N_PAIR + j))
    out = pl.BlockSpec((SLAB, 128), lambda j, n: (n, j))
    return pl.pallas_call(
        body, grid=(N_PAIR, N_SLAB),
        in_specs=[pl.BlockSpec((1, 8, 128), lambda j, n: (j, 0, 0)), cur(0), cur(1), prv(1), cur(2), prv(2)],
        out_specs=[out, out, out],
        out_shape=[jax.ShapeDtypeStruct((T, AW), F32), jax.ShapeDtypeStruct((T, AW), BF),
                   jax.ShapeDtypeStruct((T, AW), F32)],
        scratch_shapes=[pltpu.VMEM((SLAB, 128), F32)] * 3,
        name=name, compiler_params=_cp())(_slope_table(), z_at, z_at, z_at, z_at, z_at)


def _attn_bwd(z_at, dout, out, lse, name):
    def body(sl_ref, q_ref, kc_ref, kp_ref, vc_ref, vp_ref, do_ref, o_ref, lse_ref, dq_ref, dk_ref, dv_ref,
             dq_s, dk_s, dv_s, ck_s, cv_s):
        step = pl.program_id(1)
        n = N_SLAB - 1 - step
        lo, hi = _pair_masks()
        slopes = (sl_ref[0, 0:1, 0:1], sl_ref[0, 1:2, 0:1])

        @pl.when(step == 0)
        def _():
            ck_s[...] = jnp.zeros_like(ck_s)
            cv_s[...] = jnp.zeros_like(cv_s)

        dk_s[...] = ck_s[...]
        dv_s[...] = cv_s[...]
        ck_s[...] = jnp.zeros_like(ck_s)
        cv_s[...] = jnp.zeros_like(cv_s)

        def add_rows(ref, start, d, val):
            _put_rows(ref, start, d, _rows(ref, start, d) + val)

        def unit(d, start, b, first, carry):
            q = _rows(q_ref, start, d).astype(BF)
            do_f = _rows(do_ref, start, d)
            do = do_f.astype(BF)
            prod = do_f * _rows(o_ref, start, d)
            lse_u = _rows(lse_ref, start, d)
            kcur, vcur = _rows(kc_ref, start, d).astype(BF), _rows(vc_ref, start, d).astype(BF)
            if b > 0:
                (kprev, vprev), prev_valid = carry, True
            else:
                pstart = start + SLAB - 128 * d
                kprev, vprev = _rows(kp_ref, pstart, d).astype(BF), _rows(vp_ref, pstart, d).astype(BF)
                prev_valid = n > 0
            kcat = jnp.concatenate([kprev, kcur], axis=0)
            vcat = jnp.concatenate([vprev, vcur], axis=0)
            valid, dist = _band(d, prev_valid)
            masks = (lo, hi)
            qms = [jnp.where(msk, q, jnp.zeros_like(q)) for msk in masks]
            doms = [jnp.where(msk, do, jnp.zeros_like(do)) for msk in masks]
            deltas = [jnp.sum(jnp.where(msk, prod, 0.0), axis=1, keepdims=True) for msk in masks]
            ss = [_dg(qm, kcat, NT) * (A_HD ** -0.5) - sl * dist for qm, sl in zip(qms, slopes)]
            dps = [_dg(dom, vcat, NT) for dom in doms]
            ps = [jnp.where(valid, jnp.exp(jnp.where(valid, s, NEG) - lse_u[:, 64 * e:64 * e + 1]), 0.0)
                  for e, s in enumerate(ss)]
            dss = [(p * (dp - delta)).astype(BF) for p, dp, delta in zip(ps, dps, deltas)]
            pbs = [p.astype(BF) for p in ps]
            dqs = [_dg(ds, kcat, NN) for ds in dss]
            dkc = (_dg(dss[0], qms[0], TN) + _dg(dss[1], qms[1], TN)) * (A_HD ** -0.5)
            dvc = _dg(pbs[0], doms[0], TN) + _dg(pbs[1], doms[1], TN)
            dq_u = jnp.where(lo, dqs[0], dqs[1]) * (A_HD ** -0.5)
            if first:
                _put_rows(dq_s, start, d, dq_u)
            else:
                add_rows(dq_s, start, d, dq_u)
            add_rows(dk_s, start, d, dkc[128:])
            add_rows(dv_s, start, d, dvc[128:])
            if b > 0:
                add_rows(dk_s, start - 128 * d, d, dkc[:128])
                add_rows(dv_s, start - 128 * d, d, dvc[:128])
            else:
                pstart = start + SLAB - 128 * d
                add_rows(ck_s, pstart, d, dkc[:128])
                add_rows(cv_s, pstart, d, dvc[:128])
            return kcur, vcur

        for pi, (_, d) in enumerate(PATTERNS):
            carry = None
            for r, b, start in _units(d):
                carry = unit(d, start, b, pi == 0, carry)
        dq_ref[...] = dq_s[...].astype(BF)
        dk_ref[...] = dk_s[...].astype(BF)
        dv_ref[...] = dv_s[...].astype(BF)

    rev = lambda n: N_SLAB - 1 - n
    cur = lambda c: pl.BlockSpec((SLAB, 128), lambda j, n: (rev(n), c * N_PAIR + j))
    prv = lambda c: pl.BlockSpec((SLAB, 128), lambda j, n: (jnp.maximum(rev(n) - 1, 0), c * N_PAIR + j))
    one = pl.BlockSpec((SLAB, 128), lambda j, n: (rev(n), j))
    return pl.pallas_call(
        body, grid=(N_PAIR, N_SLAB),
        in_specs=[pl.BlockSpec((1, 8, 128), lambda j, n: (j, 0, 0)), cur(0), cur(1), prv(1), cur(2), prv(2),
                  one, one, one],
        out_specs=[one, one, one], out_shape=[jax.ShapeDtypeStruct((T, AW), BF)] * 3,
        scratch_shapes=[pltpu.VMEM((SLAB, 128), F32)] * 5,
        name=name, compiler_params=_cp())(_slope_table(), z_at, z_at, z_at, z_at, z_at, dout, out, lse)


def _silu(x):
    return x * jax.nn.sigmoid(x)


def _qk_math(c):
    s = _silu(c)
    return s * lax.rsqrt(jnp.sum(s * s, axis=-1, keepdims=True) + L2_EPS)


def _softplus(x):
    return jnp.maximum(x, 0.0) + jnp.log(1.0 + jnp.exp(-jnp.abs(x)))


def _gate_math(bd, alog_row, dtb_row):
    rows = bd.shape[0]
    lane = _iota(bd.shape, 1)
    beta = jax.nn.sigmoid(bd)
    g = jnp.where((lane >= DN_H) & (lane < 2 * DN_H), -jnp.exp(alog_row) * _softplus(bd + dtb_row), 0.0)
    ri = _iota((rows, rows), 0)
    ci = _iota((rows, rows), 1)
    same = (ri // CH) == (ci // CH)
    li = _iota((128, 128), 0)
    lj = _iota((128, 128), 1)
    to_next_group = jnp.where((lj == li + DN_H) & (li >= DN_H) & (li < 2 * DN_H), 1.0, 0.0)
    gc = _hdot(jnp.where(same & (ci <= ri), 1.0, 0.0), g)
    glast = _hdot(_hdot(jnp.where(same, 1.0, 0.0), g), to_next_group)
    return jnp.where(lane < DN_H, beta, 0.0) + gc + glast


def _shift_down(cur, halo, s):
    if s == 0:
        return cur
    rolled = pltpu.roll(cur, s, 0)
    hr = pltpu.roll(halo, s, 0)
    head = jnp.where(_iota(hr.shape, 0) < s, hr, rolled[:8])
    return jnp.concatenate([head, rolled[8:]], axis=0)


def _shift_up(cur, halo, s):
    if s == 0:
        return cur
    rows = cur.shape[0]
    rolled = pltpu.roll(cur, rows - s, 0)
    hr = pltpu.roll(halo, 8 - s, 0)
    tail = jnp.where(_iota(hr.shape, 0) >= 8 - s, hr, rolled[rows - 8:])
    return jnp.concatenate([rolled[:rows - 8], tail], axis=0)


def _dn_pre_fwd(z_dn, conv_w8, alog_row, dtb_row, name):
    tm = 256
    wq = 3 * DNW

    def body(raw_ref, halo_ref, bd_ref, w_ref, al_ref, dt_ref, conv_ref, qkv_ref, bg_ref):
        i = pl.program_id(0)
        cur = raw_ref[...]
        halo = jnp.where(i > 0, halo_ref[...], 0.0)
        w = w_ref[...]
        conv = jnp.zeros((tm, wq), F32)
        for j in range(4):
            conv = conv + _shift_down(cur, halo, 3 - j) * w[j:j + 1, :]
        conv_ref[...] = conv
        for blk in range(3 * DN_H):
            sl = slice(128 * blk, 128 * blk + 128)
            c = conv[:, sl]
            qkv_ref[:, sl] = _qk_math(c) if blk < 2 * DN_H else _silu(c)
        bg_ref[...] = _gate_math(bd_ref[...], al_ref[...], dt_ref[...])

    one = pl.BlockSpec((1, 128), lambda i: (0, 0))
    return pl.pallas_call(
        body, grid=(T // tm,),
        in_specs=[pl.BlockSpec((tm, wq), lambda i: (i, 0)),
                  pl.BlockSpec((8, wq), lambda i: (jnp.maximum(i * (tm // 8) - 1, 0), 0)),
                  pl.BlockSpec((tm, 128), lambda i: (i, BD_BLK)),
                  pl.BlockSpec((8, wq), lambda i: (0, 0)), one, one],
        out_specs=[pl.BlockSpec((tm, wq), lambda i: (i, 0)), pl.BlockSpec((tm, wq), lambda i: (i, 0)),
                   pl.BlockSpec((tm, 128), lambda i: (i, 0))],
        out_shape=[jax.ShapeDtypeStruct((T, wq), F32), jax.ShapeDtypeStruct((T, wq), F32),
                   jax.ShapeDtypeStruct((T, 128), F32)],
        name=name, compiler_params=_cp())(z_dn, z_dn, z_dn, conv_w8, alog_row, dtb_row)


def _dn_pre_bwd(conv, z_dn, alog_row, dtb_row, dqn, dkn, dvn, dbg, name):
    tm = 256
    wq = 3 * DNW

    def body(conv_ref, bd_ref, al_ref, dt_ref, dq_ref, dk_ref, dv_ref, dbg_ref,
             dconv_ref, dbd_ref, dal_ref, ddt_ref):
        i = pl.program_id(0)
        for blk in range(3 * DN_H):
            sl = slice(128 * blk, 128 * blk + 128)
            src = (dq_ref, dk_ref, dv_ref)[blk // DN_H]
            ct = src[:, 128 * (blk % DN_H):128 * (blk % DN_H) + 128]
            fn = _qk_math if blk < 2 * DN_H else _silu
            _, vjp = jax.vjp(fn, conv_ref[:, sl])
            dconv_ref[:, sl] = vjp(ct)[0]
        _, vjp = jax.vjp(_gate_math, bd_ref[...], al_ref[...], dt_ref[...])
        dbd, dal, ddt = vjp(dbg_ref[...])
        dbd_ref[...] = dbd

        @pl.when(i == 0)
        def _():
            dal_ref[...] = dal
            ddt_ref[...] = ddt

        @pl.when(i > 0)
        def _():
            dal_ref[...] += dal
            ddt_ref[...] += ddt

    one = pl.BlockSpec((1, 128), lambda i: (0, 0))
    row = pl.BlockSpec((tm, wq), lambda i: (i, 0))
    hd = pl.BlockSpec((tm, DNW), lambda i: (i, 0))
    st = pl.BlockSpec((tm, 128), lambda i: (i, 0))
    return pl.pallas_call(
        body, grid=(T // tm,),
        in_specs=[row, pl.BlockSpec((tm, 128), lambda i: (i, BD_BLK)), one, one, hd, hd, hd, st],
        out_specs=[row, st, one, one],
        out_shape=[jax.ShapeDtypeStruct((T, wq), F32), jax.ShapeDtypeStruct((T, 128), F32),
                   jax.ShapeDtypeStruct((1, 128), F32), jax.ShapeDtypeStruct((1, 128), F32)],
        name=name, compiler_params=_cp())(conv, z_dn, alog_row, dtb_row, dqn, dkn, dvn, dbg)


def _dn_conv_bwd(dconv, z_dn, conv_w8, name):
    tm = 256
    wq = 3 * DNW
    last = T // tm - 1

    def body(dc_ref, dcn_ref, raw_ref, halo_ref, w_ref, draw_ref, dw_ref):
        i = pl.program_id(0)
        dc = dc_ref[...]
        nxt = jnp.where(i < last, dcn_ref[...], 0.0)
        cur = raw_ref[...]
        halo = jnp.where(i > 0, halo_ref[...], 0.0)
        w = w_ref[...]
        draw = jnp.zeros((tm, wq), F32)
        rows = []
        for j in range(4):
            draw = draw + _shift_up(dc, nxt, 3 - j) * w[j:j + 1, :]
            rows.append(jnp.sum(dc * _shift_down(cur, halo, 3 - j), axis=0, keepdims=True))
        draw_ref[...] = draw
        part = jnp.concatenate(rows + [jnp.zeros((4, wq), F32)], axis=0)

        @pl.when(i == 0)
        def _():
            dw_ref[...] = part

        @pl.when(i > 0)
        def _():
            dw_ref[...] += part

    row = pl.BlockSpec((tm, wq), lambda i: (i, 0))
    return pl.pallas_call(
        body, grid=(T // tm,),
        in_specs=[row, pl.BlockSpec((8, wq), lambda i: (jnp.minimum((i + 1) * (tm // 8), T // 8 - 1), 0)),
                  row, pl.BlockSpec((8, wq), lambda i: (jnp.maximum(i * (tm // 8) - 1, 0), 0)),
                  pl.BlockSpec((8, wq), lambda i: (0, 0))],
        out_specs=[row, pl.BlockSpec((8, wq), lambda i: (0, 0))],
        out_shape=[jax.ShapeDtypeStruct((T, wq), F32), jax.ShapeDtypeStruct((8, wq), F32)],
        name=name, compiler_params=_cp())(dconv, dconv, z_dn, z_dn, conv_w8)


def _h3(a, b, dims=NN):
    return lax.dot_general(a, b, dims, precision=lax.Precision.HIGH, preferred_element_type=F32)


@jax.custom_vjp
def _inverse_given(a_mat, tinv):
    return tinv


def _inverse_given_fwd(a_mat, tinv):
    return tinv, tinv


def _inverse_given_bwd(tinv, g):
    return -_h3(tinv, _h3(g, tinv, NT), TN), jnp.zeros_like(tinv)


_inverse_given.defvjp(_inverse_given_fwd, _inverse_given_bwd)


def _prep_head(q, k, v, bgc, h):
    beta = _col(bgc, h)
    gc = jnp.broadcast_to(_col(bgc, DN_H + h), (PAIR, 128))
    glast = jnp.broadcast_to(_col(bgc, 2 * DN_H + h), (PAIR, 128))
    ri = _iota((PAIR, PAIR), 0)
    ci = _iota((PAIR, PAIR), 1)
    same = (ri // CH) == (ci // CH)
    causal = same & (ci <= ri)
    strict = same & (ci < ri)
    eye = ri == ci
    gc_cols = _hdot(jnp.ones((PAIR, PAIR), F32), jnp.where(eye, gc, 0.0))
    decay = jnp.exp(jnp.where(causal, gc - gc_cols, NEG))
    egc = jnp.exp(gc)
    kb = k * beta
    a_mat = jnp.where(strict, _bdot_nt(kb, k) * decay, 0.0)
    qs = q * (DN_HD ** -0.5)
    attn = jnp.where(causal, _bdot_nt(qs, k) * decay, 0.0)
    return a_mat, (v * beta, kb * egc, qs * egc, k * jnp.exp(glast - gc), attn, jnp.exp(glast))


def _prep_tail(tinv, ctx):
    vb, kbe, qg, kdec, attn, decb = ctx
    return _h3(tinv, vb), _h3(tinv, kbe), qg, kdec, attn, decb


def _inverses(a_mats):
    eye = jnp.where(_iota((PAIR, PAIR), 0) == _iota((PAIR, PAIR), 1), 1.0, 0.0)
    ps = [-a for a in a_mats]
    tinvs = [eye + p for p in ps]
    for _ in range(5):
        ps = [_h3(p, p) for p in ps]
        tinvs = [t + _h3(t, p) for t, p in zip(tinvs, ps)]
    return tinvs


def _prep_math(q, k, v, bgc, h, tinv_saved):
    a_mat, ctx = _prep_head(q, k, v, bgc, h)
    return _prep_tail(_inverse_given(a_mat, tinv_saved), ctx)


def _dn_prep_fwd(qkv, bg, name):
    rows = 512
    hd = lambda off: pl.BlockSpec((rows, 128), lambda g, h: (g, off + h))
    out = pl.BlockSpec((rows, 128), lambda g, h: (g, h))

    def body(q_ref, k_ref, v_ref, bg_ref, *outs):
        h = pl.program_id(1)
        spans = [slice(PAIR * pr, PAIR * pr + PAIR) for pr in range(rows // PAIR)]
        heads = [_prep_head(q_ref[rs, :], k_ref[rs, :], v_ref[rs, :], bg_ref[rs, :], h) for rs in spans]
        tinvs = _inverses([a for a, _ in heads])
        for rs, tinv, (_, ctx) in zip(spans, tinvs, heads):
            for o_ref, val in zip(outs, _prep_tail(tinv, ctx) + (tinv,)):
                o_ref[rs, :] = val

    return pl.pallas_call(
        body, grid=(T // rows, DN_H),
        in_specs=[hd(0), hd(DN_H), hd(2 * DN_H), pl.BlockSpec((rows, 128), lambda g, h: (g, 0))],
        out_specs=[out] * 7, out_shape=[jax.ShapeDtypeStruct((T, DNW), F32)] * 7,
        name=name, compiler_params=_cp())(qkv, qkv, qkv, bg)


def _dn_prep_bwd(qkv, bg, tinv, cts, name):
    rows = 512
    hd = lambda off: pl.BlockSpec((rows, 128), lambda g, h: (g, off + h))
    out = pl.BlockSpec((rows, 128), lambda g, h: (g, h))
    st = pl.BlockSpec((rows, 128), lambda g, h: (g, 0))

    def body(q_ref, k_ref, v_ref, bg_ref, ti_ref, c0, c1, c2, c3, c4, c5, dq_ref, dk_ref, dv_ref, dbg_ref):
        h = pl.program_id(1)
        spans = [slice(PAIR * pr, PAIR * pr + PAIR) for pr in range(rows // PAIR)]
        tis = [ti_ref[rs, :] for rs in spans]

        def joint(qs, ks, vs, bs):
            heads = [_prep_head(q, k, v, b, h) for q, k, v, b in zip(qs, ks, vs, bs)]
            return [_prep_tail(_inverse_given(a, ti), ctx) for (a, ctx), ti in zip(heads, tis)]

        _, vjp = jax.vjp(joint, *[[r[rs, :] for rs in spans] for r in (q_ref, k_ref, v_ref, bg_ref)])
        dqs, dks, dvs, dbs = vjp([tuple(c[rs, :] for c in (c0, c1, c2, c3, c4, c5)) for rs in spans])
        for rs, dq, dk, dv in zip(spans, dqs, dks, dvs):
            dq_ref[rs, :] = dq
            dk_ref[rs, :] = dk
            dv_ref[rs, :] = dv
        dbg_all = jnp.concatenate(dbs, axis=0)

        @pl.when(h == 0)
        def _():
            dbg_ref[...] = dbg_all

        @pl.when(h > 0)
        def _():
            dbg_ref[...] += dbg_all

    return pl.pallas_call(
        body, grid=(T // rows, DN_H),
        in_specs=[hd(0), hd(DN_H), hd(2 * DN_H), st] + [out] * 7,
        out_specs=[out, out, out, st],
        out_shape=[jax.ShapeDtypeStruct((T, DNW), F32)] * 3 + [jax.ShapeDtypeStruct((T, 128), F32)],
        name=name, compiler_params=_cp())(qkv, qkv, qkv, bg, tinv, *cts)


def _step_math(s, u, w, qg, kdec, attn, decb, sub):
    vnew = u - _bdot_nn(w, s)
    z = jnp.zeros((CH, 128), F32)
    vfull = jnp.concatenate([vnew, z] if sub == 0 else [z, vnew], axis=0)
    o = _bdot_nn(qg, s) + _bdot_nn(attn, vfull)
    dec = jnp.sum(decb, axis=0, keepdims=True) * (1.0 / CH)
    return s * dec + _bdot_tn(kdec, vnew), o


def _dn_scan_fwd(prep, name):
    npair = T // PAIR
    row = pl.BlockSpec((PAIR, DNW), lambda p: (p, 0))

    def body(u_ref, w_ref, qg_ref, kd_ref, at_ref, db_ref, o_ref, ss_ref, s_ref):
        @pl.when(pl.program_id(0) == 0)
        def _():
            s_ref[...] = jnp.zeros_like(s_ref)

        states = [s_ref[h] for h in range(DN_H)]
        for sub in range(2):
            rs = slice(CH * sub, CH * sub + CH)
            for h in range(DN_H):
                ls = slice(128 * h, 128 * h + 128)
                ss_ref[sub, h] = states[h]
                states[h], o = _step_math(states[h], u_ref[rs, ls], w_ref[rs, ls], qg_ref[rs, ls],
                                          kd_ref[rs, ls], at_ref[rs, ls], db_ref[rs, ls], sub)
                o_ref[rs, ls] = o
        for h in range(DN_H):
            s_ref[h] = states[h]

    return pl.pallas_call(
        body, grid=(npair,), in_specs=[row] * 6,
        out_specs=[row, pl.BlockSpec((2, DN_H, 128, 128), lambda p: (p, 0, 0, 0))],
        out_shape=[jax.ShapeDtypeStruct((T, DNW), F32), jax.ShapeDtypeStruct((T // CH, DN_H, 128, 128), F32)],
        scratch_shapes=[pltpu.VMEM((DN_H, 128, 128), F32)],
        name=name, compiler_params=_cp())(*prep)


def _dn_scan_bwd(prep, states, do, name):
    npair = T // PAIR
    row = pl.BlockSpec((PAIR, DNW), lambda p: (npair - 1 - p, 0))

    def body(u_ref, w_ref, qg_ref, kd_ref, at_ref, db_ref, ss_ref, do_ref, *rest):
        outs, ds_ref = rest[:6], rest[6]

        @pl.when(pl.program_id(0) == 0)
        def _():
            ds_ref[...] = jnp.zeros_like(ds_ref)

        dss = [ds_ref[h] for h in range(DN_H)]
        for sub in (1, 0):
            rs = slice(CH * sub, CH * sub + CH)
            for h in range(DN_H):
                ls = slice(128 * h, 128 * h + 128)
                args = (ss_ref[sub, h],) + tuple(r[rs, ls] for r in (u_ref, w_ref, qg_ref, kd_ref, at_ref, db_ref))
                _, vjp = jax.vjp(functools.partial(_step_math, sub=sub), *args)
                cts = vjp((dss[h], do_ref[rs, ls]))
                dss[h] = cts[0]
                for o_ref, val in zip(outs, cts[1:]):
                    o_ref[rs, ls] = val
        for h in range(DN_H):
            ds_ref[h] = dss[h]

    return pl.pallas_call(
        body, grid=(npair,),
        in_specs=[row] * 6 + [pl.BlockSpec((2, DN_H, 128, 128), lambda p: (npair - 1 - p, 0, 0, 0)), row],
        out_specs=[row] * 6, out_shape=[jax.ShapeDtypeStruct((T, DNW), F32)] * 6,
        scratch_shapes=[pltpu.VMEM((DN_H, 128, 128), F32)],
        name=name, compiler_params=_cp())(*prep, states, do)


def _post_math(o, gate, wrow):
    return o * lax.rsqrt(jnp.mean(o * o, axis=-1, keepdims=True) + NORM_EPS) * wrow * _silu(gate)


def _dn_post_fwd(o, z_dn, dn_norm, name):
    tm = 512
    row = pl.BlockSpec((tm, DNW), lambda i: (i, 0))

    def body(o_ref, g_ref, w_ref, y_ref):
        for h in range(DN_H):
            ls = slice(128 * h, 128 * h + 128)
            y_ref[:, ls] = _post_math(o_ref[:, ls], g_ref[:, ls], w_ref[...]).astype(BF)

    return pl.pallas_call(
        body, grid=(T // tm,),
        in_specs=[row, pl.BlockSpec((tm, DNW), lambda i: (i, 3)), pl.BlockSpec((1, 128), lambda i: (0, 0))],
        out_specs=row, out_shape=jax.ShapeDtypeStruct((T, DNW), BF),
        name=name, compiler_params=_cp())(o, z_dn, dn_norm)


def _dn_post_bwd(o, z_dn, dn_norm, dy, name):
    tm = 512
    row = pl.BlockSpec((tm, DNW), lambda i: (i, 0))
    one = pl.BlockSpec((1, 128), lambda i: (0, 0))

    def body(o_ref, g_ref, w_ref, dy_ref, do_ref, dg_ref, dw_ref):
        i = pl.program_id(0)
        dw = jnp.zeros((1, 128), F32)
        for h in range(DN_H):
            ls = slice(128 * h, 128 * h + 128)
            _, vjp = jax.vjp(_post_math, o_ref[:, ls], g_ref[:, ls], w_ref[...])
            do, dg, dwh = vjp(dy_ref[:, ls].astype(F32))
            do_ref[:, ls] = do
            dg_ref[:, ls] = dg
            dw = dw + dwh

        @pl.when(i == 0)
        def _():
            dw_ref[...] = dw

        @pl.when(i > 0)
        def _():
            dw_ref[...] += dw

    return pl.pallas_call(
        body, grid=(T // tm,),
        in_specs=[row, pl.BlockSpec((tm, DNW), lambda i: (i, 3)), one, pl.BlockSpec((tm, DNW), lambda i: (i, 1))],
        out_specs=[row, row, one],
        out_shape=[jax.ShapeDtypeStruct((T, DNW), F32), jax.ShapeDtypeStruct((T, DNW), F32),
                   jax.ShapeDtypeStruct((1, 128), F32)],
        name=name, compiler_params=_cp())(o, z_dn, dn_norm, dy)


def _dz_dn_assemble(draw, dgate, dbd, name):
    tm = 512

    def body(a_ref, b_ref, c_ref, o_ref):
        o_ref[:, :3 * DNW] = a_ref[...].astype(BF)
        o_ref[:, 3 * DNW:4 * DNW] = b_ref[...].astype(BF)
        o_ref[:, 4 * DNW:4 * DNW + 128] = c_ref[...].astype(BF)
        o_ref[:, 4 * DNW + 128:] = jnp.zeros((tm, 128), BF)

    return pl.pallas_call(
        body, grid=(T // tm,),
        in_specs=[pl.BlockSpec((tm, 3 * DNW), lambda i: (i, 0)), pl.BlockSpec((tm, DNW), lambda i: (i, 0)),
                  pl.BlockSpec((tm, 128), lambda i: (i, 0))],
        out_specs=pl.BlockSpec((tm, ZD), lambda i: (i, 0)),
        out_shape=jax.ShapeDtypeStruct((T, ZD), BF), name=name, compiler_params=_cp())(draw, dgate, dbd)


HBM = pl.BlockSpec(memory_space=pltpu.HBM)
SEM = pl.BlockSpec(memory_space=pltpu.SEMAPHORE)
EFFECT = pltpu.SideEffectType.DATAFLOW_SIDE_EFFECTING
N_PEER = N_DEV - 1


def _peers(x, y, c):
    return [(k, (x ^ (k >> 2), y ^ ((k >> 1) & 1), c ^ (k & 1))) for k in (1, 2, 4, 3, 5, 6, 7)]


def _exchange_copy(ins, lands, ssems, rsems, scatter, t, k, pos, me):
    px, py, pc = pos
    src = ins[t].at[4 * px + 2 * py + pc] if scatter else ins[t]
    return pltpu.make_async_remote_copy(
        src_ref=src, dst_ref=lands[t].at[me], send_sem=ssems[t].at[k - 1], recv_sem=rsems[t].at[k - 1],
        device_id=pos, device_id_type=MESH_ID)


def _xstart(bufs, scatter, name):
    nt = len(bufs)
    lands = [lax.empty((N_DEV,) + tuple(b.shape[1:] if scatter else b.shape), b.dtype) for b in bufs]

    def body(*refs):
        ins, lnd = refs[:nt], refs[nt:2 * nt]
        ssems, rsems = refs[2 * nt:3 * nt], refs[3 * nt:4 * nt]
        token = refs[-1]
        x, y, c = lax.axis_index("x"), lax.axis_index("y"), lax.axis_index("c")
        me = 4 * x + 2 * y + c
        for t in range(nt):
            for k, pos in _peers(x, y, c):
                _exchange_copy(ins, lnd, ssems, rsems, scatter, t, k, pos, me).start()
        token[...] = jnp.zeros_like(token)

    both = list(bufs) + lands
    res = pl.pallas_call(
        body, name=name,
        out_shape=[pltpu.SemaphoreType.DMA((N_PEER,))] * (2 * nt)
        + [pltpu.HBM(b.shape, b.dtype) for b in both] + [jax.ShapeDtypeStruct((8, 128), F32)],
        in_specs=[HBM] * (2 * nt),
        out_specs=[SEM] * (2 * nt) + [HBM] * (2 * nt) + [pl.BlockSpec(memory_space=pltpu.VMEM)],
        input_output_aliases={i: 2 * nt + i for i in range(2 * nt)},
        compiler_params=pltpu.CompilerParams(has_side_effects=EFFECT),
    )(*[pltpu.with_memory_space_constraint(b, pltpu.HBM) for b in both])
    return res[:nt], res[nt:2 * nt], res[2 * nt:3 * nt], res[3 * nt:4 * nt], res[-1][0, 0]


def _xwait(ssems, rsems, thrus, lands, scatter, after, name):
    nt = len(lands)

    def body(*refs):
        ins, lnd = refs[:nt], refs[nt:2 * nt]
        ss, rs = refs[2 * nt:3 * nt], refs[3 * nt:4 * nt]
        x, y, c = lax.axis_index("x"), lax.axis_index("y"), lax.axis_index("c")
        me = 4 * x + 2 * y + c
        for t in range(nt):
            for k, pos in _peers(x, y, c):
                cp = _exchange_copy(ins, lnd, ss, rs, scatter, t, k, pos, me)
                cp.wait_send()
                cp.wait_recv()

    both = list(thrus) + list(lands)
    res = pl.pallas_call(
        body, name=name, out_shape=[pltpu.HBM(b.shape, b.dtype) for b in both],
        in_specs=[HBM] * (2 * nt) + [SEM] * (2 * nt) + [ANY], out_specs=[HBM] * (2 * nt),
        input_output_aliases={i: i for i in range(2 * nt)},
        compiler_params=pltpu.CompilerParams(has_side_effects=EFFECT),
    )(*both, *ssems, *rsems, after)
    return res[:nt], res[nt:]


def _adam(recv, w, m, v, tr, name):
    _, r, c = w.shape
    n_part = recv.shape[0]
    c1 = np.float32(1.0 - ADAM_B1 ** ADAM_STEP)
    c2 = np.float32(1.0 - ADAM_B2 ** ADAM_STEP)

    def body(r_ref, w_ref, m_ref, v_ref, g_ref, d_ref, mo_ref, vo_ref):
        g = r_ref[0].astype(F32)
        for s in range(1, n_part):
            g = g + r_ref[s].astype(F32)
        mn = ADAM_B1 * m_ref[0] + (1.0 - ADAM_B1) * g
        vn = ADAM_B2 * v_ref[0] + (1.0 - ADAM_B2) * (g * g)
        g_ref[0] = g
        mo_ref[0] = mn
        vo_ref[0] = vn
        d_ref[0] = -ADAM_LR * ((mn / c1) / (jnp.sqrt(vn / c2) + ADAM_EPS) + ADAM_WD * w_ref[0])

    one = pl.BlockSpec((1, tr, c), lambda i: (0, i, 0))
    return pl.pallas_call(
        body, grid=(r // tr,), in_specs=[pl.BlockSpec((n_part, tr, c), lambda i: (0, i, 0)), one, one, one],
        out_specs=[one] * 4, out_shape=[jax.ShapeDtypeStruct((1, r, c), F32)] * 4,
        name=name, compiler_params=_cp())(recv, w, m, v)


def _local_step(x, target, sp, need, emit):
    g = {}
    x0, y1, saved1 = _ffn_fwd(x, None, 0.0, sp["norm_ffn1"], lambda kind, a: need("w" + kind + "1", a), "ffn1")
    x1, h2 = _rms_fwd(x0, y1, 0.5, sp["norm_mix"], "mix_norm")
    win_a, win_d = need("win_a", h2), need("win_d", h2)
    conv_w8, wout = need("conv_w8", h2), need("wout", h2)
    z_at = _mm_nn(h2, win_a, F32, 1024, 768, "mix_in_attn")
    z_dn = _mm_nn(h2, win_d, F32, 1024, 768, "mix_in_dn")
    attn_f, attn_b, lse = _attn_fwd(z_at, "attn_fwd")

    conv, qkvn, bg = _dn_pre_fwd(z_dn, conv_w8, sp["alog_row"], sp["dtb_row"], "dn_pre")
    *prep, tinv = _dn_prep_fwd(qkvn, bg, "dn_prep")
    o_dn, states = _dn_scan_fwd(prep, "dn_scan")
    dn_b = _dn_post_fwd(o_dn, z_dn, sp["dn_norm"], "dn_post")

    y2 = _mm_nn2(attn_b, dn_b, wout[:AW], wout[AW:], 1024, 1024, "mix_out")
    x2, y3, saved2 = _ffn_fwd(x1, y2, 1.0, sp["norm_ffn2"], lambda kind, a: need("w" + kind + "2", a), "ffn2")

    dx3, dys3, g["norm_final"], loss8 = _loss_bwd(x2, y3, sp["norm_final"], target, "loss")
    dx2, dx2b, g["norm_ffn2"] = _ffn_bwd(
        x2, sp["norm_ffn2"], saved2, dx3, dys3, 1.0,
        lambda kind, dw: emit(kind + "2", {"w" + kind + "2": dw}), "ffn2b")

    zero = emit("wout", {"wout": jnp.concatenate([_mm_tn(attn_b, dx2b, BF, 512, 1024, "mix_out_dw_a"),
                                                  _mm_tn(dn_b, dx2b, BF, 512, 1024, "mix_out_dw_d")], axis=0)})
    dmix = _mm_nt(dx2b, wout, F32, 1024, 1024, "mix_out_dx", after=zero)

    dz_at = jnp.concatenate(_attn_bwd(z_at, dmix, attn_f, lse, "attn_bwd"), axis=1)

    do_dn, dgate, g["dn_norm"] = _dn_post_bwd(o_dn, z_dn, sp["dn_norm"], dmix, "dn_post_b")
    cts = _dn_scan_bwd(prep, states, do_dn, "dn_scan_b")
    dqn, dkn, dvn, dbg = _dn_prep_bwd(qkvn, bg, tinv, cts, "dn_prep_b")
    dconv, dbd, g["alog_row"], g["dtb_row"] = _dn_pre_bwd(
        conv, z_dn, sp["alog_row"], sp["dtb_row"], dqn, dkn, dvn, dbg, "dn_pre_b")
    draw, dconv_w8 = _dn_conv_bwd(dconv, z_dn, conv_w8, "dn_conv_b")
    dz_dn = _dz_dn_assemble(draw, dgate, dbd, "dn_dz")

    zero = emit("win", {"win_a": _mm_tn(h2, dz_at, BF, 512, 768, "mix_in_dw_a"),
                        "win_d": _mm_tn(h2, dz_dn, BF, 512, 768, "mix_in_dw_d"), "conv_w8": dconv_w8})
    dh2 = _mm_nt(dz_at, win_a, F32, 1024, 1024, "mix_in_dx_a", after=zero)
    dh2d = _mm_nt(dz_dn, win_d, F32, 1024, 1024, "mix_in_dx_d")
    (dx1, dys1), _, g["norm_mix"] = _rms_bwd2(x1, sp["norm_mix"], dh2, dh2d, dx2, "mix_dnorm")
    dx0, _, g["norm_ffn1"] = _ffn_bwd(
        x0, sp["norm_ffn1"], saved1, dx1, dys1, 1.0,
        lambda kind, dw: emit(kind + "1", {"w" + kind + "1": dw}), "ffn1b")
    return loss8[0, 0], dx0, g


def _rms_bwd2(x, gain, dh_a, dh_b, dres, name):
    tm = 512
    row = pl.BlockSpec((tm, D), lambda i: (i, 0))
    gspec = pl.BlockSpec((1, D), lambda i: (0, 0))

    def body(x_ref, g_ref, da_ref, db_ref, dres_ref, dx_ref, dxs_ref, dg_ref):
        i = pl.program_id(0)
        xv = x_ref[...]
        r = lax.rsqrt(jnp.mean(xv * xv, axis=-1, keepdims=True) + NORM_EPS)
        xh = xv * r
        dhv = da_ref[...] + db_ref[...]
        part = jnp.sum(dhv * xh, axis=0, keepdims=True)

        @pl.when(i == 0)
        def _():
            dg_ref[...] = part

        @pl.when(i > 0)
        def _():
            dg_ref[...] += part

        dxh = dhv * g_ref[...]
        dx = r * (dxh - xh * jnp.mean(dxh * xh, axis=-1, keepdims=True)) + dres_ref[...]
        dx_ref[...] = dx
        dxs_ref[...] = (0.5 * dx).astype(BF)

    dx, dxs, dg = pl.pallas_call(
        body, grid=(T // tm,), in_specs=[row, gspec, row, row, row], out_specs=[row, row, gspec],
        out_shape=[jax.ShapeDtypeStruct((T, D), F32), jax.ShapeDtypeStruct((T, D), BF),
                   jax.ShapeDtypeStruct((1, D), F32)],
        name=name, compiler_params=_cp())(x, gain, dh_a, dh_b, dres)
    return (dx, dxs), None, dg


def _cols_from_shards(gathered):
    n, r, c = gathered.shape
    return jnp.transpose(gathered, (1, 0, 2)).reshape(r, n * c)


def _shards_from_cols(full, dtype):
    r, nc = full.shape
    return jnp.transpose(full.reshape(r, N_DEV, nc // N_DEV), (1, 0, 2)).astype(dtype)


def _lane_row(vec4):
    return jnp.zeros((1, 128), F32).at[:, DN_H:2 * DN_H].set(vec4.astype(F32))


WEIGHT_SOURCES = {"wg1": "gate1", "wu1": "up1", "wd1": "down1", "win_a": "w_in", "win_d": "w_in",
                  "conv_w8": "conv_w", "wout": "w_out", "wg2": "gate2", "wu2": "up2", "wd2": "down2"}
TRANSPOSED = ("gate1", "up1", "gate2", "up2")


def _build_weights(name, gath):
    if name in ("wg1", "wu1", "wd1", "wg2", "wu2", "wd2"):
        return {name: gath[WEIGHT_SOURCES[name]].reshape(F, D)}
    if name in ("win_a", "win_d"):
        w_in = _cols_from_shards(gath["w_in"])
        c0 = 3 * AW + 3 * DNW
        wp = jnp.concatenate([w_in[:, :c0], w_in[:, c0 + 2 * DN_H:], w_in[:, c0:c0 + 2 * DN_H],
                              jnp.zeros((D, ZP - IN_COLS), w_in.dtype)], axis=1)
        return {"win_a": wp[:, :ZA], "win_d": wp[:, ZA:]}
    if name == "wout":
        return {name: gath["w_out"].reshape(D, D)}
    conv = _cols_from_shards(gath["conv_w"])
    return {"conv_w8": jnp.concatenate([conv, jnp.zeros((4, 3 * DNW), F32)], axis=0)}


def _small_params(norm_ffn1, norm_mix, norm_ffn2, norm_final, a_log, dt_bias, dn_norm):
    return {"norm_ffn1": norm_ffn1, "norm_mix": norm_mix, "norm_ffn2": norm_ffn2,
            "norm_final": norm_final.reshape(1, D), "alog_row": _lane_row(a_log), "dtb_row": _lane_row(dt_bias),
            "dn_norm": dn_norm}


def _grad_slabs(group, g):
    if group[0] in "gud":
        return {WEIGHT_SOURCES["w" + group]: g["w" + group].reshape(N_DEV, F // N_DEV, D)}
    if group == "wout":
        return {"w_out": g["wout"].reshape(N_DEV, D // N_DEV, D)}
    gp = jnp.concatenate([g["win_a"], g["win_d"]], axis=1)
    c0 = 3 * AW + 3 * DNW
    g_in = jnp.concatenate([gp[:, :c0], gp[:, c0 + DNW:c0 + DNW + 2 * DN_H], gp[:, c0:c0 + DNW]], axis=1)
    return {"w_in": _shards_from_cols(g_in, BF), "conv_w": _shards_from_cols(g["conv_w8"][:4], F32)}


SMALL_ROWS = 40


def _small_pack(norm_ffn1, norm_mix, norm_ffn2, norm_final, dn_norm, alog_row, dtb_row):
    rows = [a.reshape(8, 128) for a in (norm_ffn1, norm_mix, norm_ffn2, norm_final)]
    rows += [dn_norm.reshape(1, 128), alog_row, dtb_row, jnp.zeros((SMALL_ROWS - 35, 128), F32)]
    return jnp.concatenate(rows, axis=0)


def _small_unpack(pk):
    pk = pk[0]
    return (pk[0:8].reshape(1, D), pk[8:16].reshape(1, D), pk[16:24].reshape(1, D), pk[24:32].reshape(D),
            pk[32:33], pk[33:34, DN_H:2 * DN_H], pk[34:35, DN_H:2 * DN_H])


ADAM_TILE = {"gate1": 256, "up1": 256, "down1": 176, "gate2": 256, "up2": 256, "down2": 176,
             "w_in": 256, "w_out": 128, "conv_w": 4}
BIG = ("gate1", "up1", "down1", "w_in", "w_out", "gate2", "up2", "down2", "conv_w")


def kernel(x, norm_ffn1, ffn1_gate, ffn1_up, ffn1_down, norm_mix, w_in, conv_w, a_log, dt_bias, dn_norm, w_out, norm_ffn2, ffn2_gate, ffn2_up, ffn2_down, norm_final, loss_target, m_norm_ffn1, m_ffn1_gate, m_ffn1_up, m_ffn1_down, m_norm_mix, m_w_in, m_conv_w, m_a_log, m_dt_bias, m_dn_norm, m_w_out, m_norm_ffn2, m_ffn2_gate, m_ffn2_up, m_ffn2_down, m_norm_final, v_norm_ffn1, v_ffn1_gate, v_ffn1_up, v_ffn1_down, v_norm_mix, v_w_in, v_conv_w, v_a_log, v_dt_bias, v_dn_norm, v_w_out, v_norm_ffn2, v_ffn2_gate, v_ffn2_up, v_ffn2_down, v_norm_final):
    w = {"gate1": ffn1_gate, "up1": ffn1_up, "down1": ffn1_down, "w_in": w_in, "w_out": w_out,
         "gate2": ffn2_gate, "up2": ffn2_up, "down2": ffn2_down, "conv_w": conv_w}
    m = {"gate1": m_ffn1_gate, "up1": m_ffn1_up, "down1": m_ffn1_down, "w_in": m_w_in, "w_out": m_w_out,
         "gate2": m_ffn2_gate, "up2": m_ffn2_up, "down2": m_ffn2_down, "conv_w": m_conv_w}
    v = {"gate1": v_ffn1_gate, "up1": v_ffn1_up, "down1": v_ffn1_down, "w_in": v_w_in, "w_out": v_w_out,
         "gate2": v_ffn2_gate, "up2": v_ffn2_up, "down2": v_ffn2_down, "conv_w": v_conv_w}

    me = 4 * lax.axis_index("x") + 2 * lax.axis_index("y") + lax.axis_index("c")
    own_slot = lambda land, mine: lax.dynamic_update_index_in_dim(land, mine, me, 0)

    ag_order = ("gate1", "up1", "down1", "w_in", "conv_w", "w_out", "gate2", "up2", "down2")
    ag_groups = (("gate1",), ("up1",), ("down1",), ("w_in", "conv_w", "w_out"), ("gate2",), ("up2",), ("down2",))
    pos = {n: i for i, n in enumerate(ag_order)}

    def shard(n):
        if n == "conv_w":
            return w[n][0]
        return (w[n][0].T if n in TRANSPOSED else w[n][0]).astype(BF)

    ss, rs, thru, land, zero = _xstart([shard(n) for n in ag_order], False, "weights_start")
    gath, built = {}, {}

    def need(name, after):
        if name not in built:
            src = WEIGHT_SOURCES[name]
            if src not in gath:
                gi = [i for i, grp in enumerate(ag_groups) if src in grp][0]
                ids = [pos[n] for n in ag_groups[gi]]
                thrus, lands = _xwait([ss[i] for i in ids], [rs[i] for i in ids], [thru[i] for i in ids],
                                      [land[i] for i in ids], False, after, "weights_wait%d" % gi)
                for n, t, l in zip(ag_groups[gi], thrus, lands):
                    gath[n] = own_slot(l, t)
            built.update(_build_weights(name, gath))
        return built[name]

    pending = []

    def emit(group, grads):
        slabs = grads if group == "small" else _grad_slabs(group, grads)
        names = list(slabs)
        started = _xstart([slabs[n] for n in names], True, "grads_start_" + group)
        pending.append((group, names) + started[:4])
        return started[4]

    sp = _small_params(norm_ffn1 + zero, norm_mix, norm_ffn2, norm_final, a_log, dt_bias, dn_norm)
    loss_part, dx, g = _local_step(x[0], loss_target[0], sp, need, emit)
    small = _small_pack(g["norm_ffn1"], g["norm_mix"], g["norm_ffn2"], g["norm_final"], g["dn_norm"],
                        g["alog_row"], g["dtb_row"])
    emit("small", {"small": jnp.broadcast_to(small[None], (N_DEV, SMALL_ROWS, 128))})

    pack = lambda a: _small_pack(*a)[None]
    res, after = {}, dx
    for group, names, gss, grs, gthru, gland in pending:
        thrus, lands = _xwait(gss, grs, gthru, gland, True, after, "grads_wait_" + group)
        for n, t, l in zip(names, thrus, lands):
            recv = own_slot(l, lax.dynamic_index_in_dim(t, me, 0, keepdims=False))
            if n == "small":
                res[n] = _adam(
                    recv,
                    pack((norm_ffn1, norm_mix, norm_ffn2, norm_final, dn_norm, _lane_row(a_log), _lane_row(dt_bias))),
                    pack((m_norm_ffn1, m_norm_mix, m_norm_ffn2, m_norm_final, m_dn_norm, _lane_row(m_a_log),
                          _lane_row(m_dt_bias))),
                    pack((v_norm_ffn1, v_norm_mix, v_norm_ffn2, v_norm_final, v_dn_norm, _lane_row(v_a_log),
                          _lane_row(v_dt_bias))),
                    SMALL_ROWS, "adam_small")
            elif n in TRANSPOSED:
                flip = lambda a: jnp.swapaxes(a, 1, 2)
                res[n] = [flip(o) for o in _adam(recv, flip(w[n]), flip(m[n]), flip(v[n]), F // N_DEV // 2,
                                                 "adam_" + n)]
            else:
                res[n] = _adam(recv, w[n], m[n], v[n], ADAM_TILE[n], "adam_" + n)
            after = res[n][0]
    res_s = res["small"]

    loss = lax.psum(loss_part, ("x", "y", "c"))
    outs = [loss, dx[None]]
    for k in range(4):
        n1, nm, n2, nf, dn, al, dt = _small_unpack(res_s[k])
        big = {n: res[n][k] for n in BIG}
        outs += [n1, big["gate1"], big["up1"], big["down1"], nm, big["w_in"], big["conv_w"], al, dt, dn,
                 big["w_out"], n2, big["gate2"], big["up2"], big["down2"], nf]
    return tuple(outs)
```

```python
import functools

import numpy as np
import jax
import jax.numpy as jnp
from jax import lax
from jax.experimental import pallas as pl
from jax.experimental.pallas import tpu as pltpu

T = 4096
D = 1024
F = 2816
N_DEV = 8
A_HEADS = 8
A_HD = 64
AW = A_HEADS * A_HD
DN_H = 4
DN_HD = 128
DNW = DN_H * DN_HD
CH = 64
PAIR = 2 * CH
ZA = 3 * AW
ZD = 3 * DNW + DNW + 256
ZP = ZA + ZD
BD_BLK = (3 * DNW + DNW) // 128
IN_COLS = 3592
PATTERNS = ((128, 1), (512, 4), (2048, 16))
NORM_EPS = 1e-6
L2_EPS = 1e-6
ADAM_LR, ADAM_B1, ADAM_B2, ADAM_EPS, ADAM_WD, ADAM_STEP = 0.001, 0.9, 0.999, 1e-08, 0.01, 10
VMEM_LIMIT = 56 * 1024 * 1024
NEG = -1e30

BF = jnp.bfloat16
F32 = jnp.float32
NN = (((1,), (0,)), ((), ()))
NT = (((1,), (1,)), ((), ()))
TN = (((0,), (0,)), ((), ()))
HI = lax.Precision.HIGHEST
MESH_ID = pl.DeviceIdType.MESH
ANY = pl.BlockSpec(memory_space=pl.ANY)


def _cp():
    return pltpu.CompilerParams(vmem_limit_bytes=VMEM_LIMIT)


def _dg(a, b, dims):
    return lax.dot_general(a, b, dims, preferred_element_type=F32)


def _hdot(a, b):
    return lax.dot_general(a, b, NN, precision=HI, preferred_element_type=F32)


def _make_bdot(dims, da_dims, da_swap, db_dims, db_swap):
    @jax.custom_vjp
    def f(a, b):
        return _dg(a.astype(BF), b.astype(BF), dims)

    def fwd(a, b):
        return f(a, b), (a, b)

    def bwd(res, g):
        a, b = res
        gb, ab, bb = g.astype(BF), a.astype(BF), b.astype(BF)
        da = _dg(bb, gb, da_dims) if da_swap else _dg(gb, bb, da_dims)
        db = _dg(gb, ab, db_dims) if db_swap else _dg(ab, gb, db_dims)
        return da, db

    f.defvjp(fwd, bwd)
    return f


_bdot_nn = _make_bdot(NN, NT, False, TN, False)
_bdot_nt = _make_bdot(NT, NN, False, TN, True)
_bdot_tn = _make_bdot(TN, NT, True, NN, False)


def _iota(shape, dim):
    return lax.broadcasted_iota(jnp.int32, shape, dim)


def _col(x, idx):
    return jnp.sum(jnp.where(_iota(x.shape, 1) == idx, x, 0.0), axis=1, keepdims=True)


def _mm_nn(a, b, out_dtype, tm, tn, name):
    m, k = a.shape
    n = b.shape[1]

    def body(a_ref, b_ref, o_ref):
        o_ref[...] = _dg(a_ref[...], b_ref[...], NN).astype(out_dtype)

    return pl.pallas_call(
        body, grid=(m // tm, n // tn),
        in_specs=[pl.BlockSpec((tm, k), lambda i, j: (i, 0)), pl.BlockSpec((k, tn), lambda i, j: (0, j))],
        out_specs=pl.BlockSpec((tm, tn), lambda i, j: (i, j)),
        out_shape=jax.ShapeDtypeStruct((m, n), out_dtype), name=name, compiler_params=_cp())(a, b)


def _tie(body, after):
    if after is None:
        return body, [], []
    return (lambda tok_ref, *refs: body(*refs)), [ANY], [after.reshape(1, 1)]


def _mm_nt(a, b, out_dtype, tm, tb, name, after=None):
    m, c = a.shape
    kb = b.shape[0]

    def body(a_ref, b_ref, o_ref):
        o_ref[...] = _dg(a_ref[...], b_ref[...], NT).astype(out_dtype)

    body, tspec, tok = _tie(body, after)
    return pl.pallas_call(
        body, grid=(m // tm, kb // tb),
        in_specs=tspec + [pl.BlockSpec((tm, c), lambda i, j: (i, 0)), pl.BlockSpec((tb, c), lambda i, j: (j, 0))],
        out_specs=pl.BlockSpec((tm, tb), lambda i, j: (i, j)),
        out_shape=jax.ShapeDtypeStruct((m, kb), out_dtype), name=name, compiler_params=_cp())(*tok, a, b)


def _mm_tn(a, b, out_dtype, ta, tb, name, after=None):
    m, ka = a.shape
    nb = b.shape[1]

    def body(a_ref, b_ref, o_ref):
        o_ref[...] = _dg(a_ref[...], b_ref[...], TN).astype(out_dtype)

    body, tspec, tok = _tie(body, after)
    return pl.pallas_call(
        body, grid=(ka // ta, nb // tb),
        in_specs=tspec + [pl.BlockSpec((m, ta), lambda i, j: (0, i)), pl.BlockSpec((m, tb), lambda i, j: (0, j))],
        out_specs=pl.BlockSpec((ta, tb), lambda i, j: (i, j)),
        out_shape=jax.ShapeDtypeStruct((ka, nb), out_dtype), name=name, compiler_params=_cp())(*tok, a, b)


def _mm_nn2(a1, a2, b1, b2, tm, tn, name, after=None):
    m, k = a1.shape
    n = b1.shape[1]

    def body(a1_ref, a2_ref, b1_ref, b2_ref, o_ref):
        o_ref[...] = _dg(a1_ref[...], b1_ref[...], NN) + _dg(a2_ref[...], b2_ref[...], NN)

    body, tspec, tok = _tie(body, after)
    arow = pl.BlockSpec((tm, k), lambda i, j: (i, 0))
    bcol = pl.BlockSpec((k, tn), lambda i, j: (0, j))
    return pl.pallas_call(
        body, grid=(m // tm, n // tn), in_specs=tspec + [arow, arow, bcol, bcol],
        out_specs=pl.BlockSpec((tm, tn), lambda i, j: (i, j)),
        out_shape=jax.ShapeDtypeStruct((m, n), F32), name=name, compiler_params=_cp())(*tok, a1, a2, b1, b2)


def _mm_nt_swiglu(h, wu_t, gate, tm, tb, name):
    m, c = h.shape
    kb = wu_t.shape[0]

    def body(h_ref, w_ref, g_ref, u_ref, a_ref):
        wv = w_ref[...]
        for r0 in range(0, tm, tm // 2):
            rs = slice(r0, r0 + tm // 2)
            u = _dg(h_ref[rs, :], wv, NT)
            g = g_ref[rs, :].astype(F32)
            u_ref[rs, :] = u.astype(BF)
            a_ref[rs, :] = (g * jax.nn.sigmoid(g) * u).astype(BF)

    tile = pl.BlockSpec((tm, tb), lambda i, j: (i, j))
    return pl.pallas_call(
        body, grid=(m // tm, kb // tb),
        in_specs=[pl.BlockSpec((tm, c), lambda i, j: (i, 0)), pl.BlockSpec((tb, c), lambda i, j: (j, 0)), tile],
        out_specs=[tile, tile], out_shape=[jax.ShapeDtypeStruct((m, kb), BF)] * 2,
        name=name, compiler_params=_cp())(h, wu_t, gate)


def _mm_nt_dswiglu(dys, wd, gate, up, tm, tb, name, after=None):
    m, c = dys.shape
    kb = wd.shape[0]

    def body(d_ref, w_ref, g_ref, u_ref, dg_ref, du_ref):
        wv = w_ref[...]
        for r0 in range(0, tm, tm // 2):
            rs = slice(r0, r0 + tm // 2)
            da = _dg(d_ref[rs, :], wv, NT)
            g = g_ref[rs, :].astype(F32)
            u = u_ref[rs, :].astype(F32)
            s = jax.nn.sigmoid(g)
            dg_ref[rs, :] = (da * u * (s * (1.0 + g * (1.0 - s)))).astype(BF)
            du_ref[rs, :] = (da * (g * s)).astype(BF)

    body, tspec, tok = _tie(body, after)
    tile = pl.BlockSpec((tm, tb), lambda i, j: (i, j))
    return pl.pallas_call(
        body, grid=(m // tm, kb // tb),
        in_specs=tspec + [pl.BlockSpec((tm, c), lambda i, j: (i, 0)), pl.BlockSpec((tb, c), lambda i, j: (j, 0)),
                          tile, tile],
        out_specs=[tile, tile], out_shape=[jax.ShapeDtypeStruct((m, kb), BF)] * 2,
        name=name, compiler_params=_cp())(*tok, dys, wd, gate, up)


def _rms_fwd(x, gain, name):
    tm = 512
    row = pl.BlockSpec((tm, D), lambda i: (i, 0))

    def body(x_ref, g_ref, h_ref):
        xv = x_ref[...]
        r = lax.rsqrt(jnp.mean(xv * xv, axis=-1, keepdims=True) + NORM_EPS)
        h_ref[...] = (xv * r * g_ref[...]).astype(BF)

    return pl.pallas_call(
        body, grid=(T // tm,), in_specs=[row, pl.BlockSpec((1, D), lambda i: (0, 0))], out_specs=row,
        out_shape=jax.ShapeDtypeStruct((T, D), BF), name=name, compiler_params=_cp())(x, gain)


def _rms_bwd(x, gain, dh, dres, alpha_out, name):
    tm = 512
    row = pl.BlockSpec((tm, D), lambda i: (i, 0))
    gspec = pl.BlockSpec((1, D), lambda i: (0, 0))

    def body(x_ref, g_ref, dh_ref, dres_ref, dx_ref, dxs_ref, dg_ref):
        i = pl.program_id(0)
        xv = x_ref[...]
        r = lax.rsqrt(jnp.mean(xv * xv, axis=-1, keepdims=True) + NORM_EPS)
        xh = xv * r
        dhv = dh_ref[...].astype(F32)
        part = jnp.sum(dhv * xh, axis=0, keepdims=True)

        @pl.when(i == 0)
        def _():
            dg_ref[...] = part

        @pl.when(i > 0)
        def _():
            dg_ref[...] += part

        dxh = dhv * g_ref[...]
        dx = r * (dxh - xh * jnp.mean(dxh * xh, axis=-1, keepdims=True)) + dres_ref[...]
        dx_ref[...] = dx
        dxs_ref[...] = (alpha_out * dx).astype(BF)

    return pl.pallas_call(
        body, grid=(T // tm,), in_specs=[row, gspec, row, row], out_specs=[row, row, gspec],
        out_shape=[jax.ShapeDtypeStruct((T, D), F32), jax.ShapeDtypeStruct((T, D), BF),
                   jax.ShapeDtypeStruct((1, D), F32)],
        name=name, compiler_params=_cp())(x, gain, dh, dres)


def _loss_bwd(x_prev, y, gain, target, name):
    tm = 512
    row = pl.BlockSpec((tm, D), lambda i: (i, 0))
    gspec = pl.BlockSpec((1, D), lambda i: (0, 0))
    lspec = pl.BlockSpec((8, 128), lambda i: (0, 0))

    def body(x_ref, y_ref, g_ref, t_ref, dx_ref, dxs_ref, dg_ref, loss_ref):
        i = pl.program_id(0)
        xv = x_ref[...] + 0.5 * y_ref[...]
        r = lax.rsqrt(jnp.mean(xv * xv, axis=-1, keepdims=True) + NORM_EPS)
        xh = xv * r
        diff = xh * g_ref[...] - t_ref[...]
        lpart = 0.5 * jnp.sum(jnp.mean(diff * diff, axis=-1, keepdims=True), axis=0, keepdims=True)
        dy = diff * (1.0 / D)
        part = jnp.sum(dy * xh, axis=0, keepdims=True)

        @pl.when(i == 0)
        def _():
            dg_ref[...] = part
            loss_ref[...] = jnp.broadcast_to(lpart, (8, 128))

        @pl.when(i > 0)
        def _():
            dg_ref[...] += part
            loss_ref[...] += jnp.broadcast_to(lpart, (8, 128))

        dxh = dy * g_ref[...]
        dx = r * (dxh - xh * jnp.mean(dxh * xh, axis=-1, keepdims=True))
        dx_ref[...] = dx
        dxs_ref[...] = (0.5 * dx).astype(BF)

    return pl.pallas_call(
        body, grid=(T // tm,), in_specs=[row, row, gspec, row], out_specs=[row, row, gspec, lspec],
        out_shape=[jax.ShapeDtypeStruct((T, D), F32), jax.ShapeDtypeStruct((T, D), BF),
                   jax.ShapeDtypeStruct((1, D), F32), jax.ShapeDtypeStruct((8, 128), F32)],
        name=name, compiler_params=_cp())(x_prev, y, gain, target)


def _mm_nn_resnorm(pairs, x_prev, alpha, gain, tm, name):
    m = x_prev.shape[0]
    n = len(pairs)

    def body(*refs):
        x_ref, g_ref, xo_ref, h_ref = refs[2 * n:]
        y = _dg(refs[0][...], refs[n][...], NN)
        for i in range(1, n):
            y = y + _dg(refs[i][...], refs[n + i][...], NN)
        xv = x_ref[...] + alpha * y
        xo_ref[...] = xv
        r = lax.rsqrt(jnp.mean(xv * xv, axis=-1, keepdims=True) + NORM_EPS)
        h_ref[...] = (xv * r * g_ref[...]).astype(BF)

    row = pl.BlockSpec((tm, D), lambda i: (i, 0))
    return pl.pallas_call(
        body, grid=(m // tm,),
        in_specs=[pl.BlockSpec((tm, a.shape[1]), lambda i: (i, 0)) for a, _ in pairs]
        + [pl.BlockSpec(b.shape, lambda i: (0, 0)) for _, b in pairs] + [row, pl.BlockSpec((1, D), lambda i: (0, 0))],
        out_specs=[row, row],
        out_shape=[jax.ShapeDtypeStruct((m, D), F32), jax.ShapeDtypeStruct((m, D), BF)],
        name=name, compiler_params=_cp())(*[a for a, _ in pairs], *[b for _, b in pairs], x_prev, gain)


def _ffn_up(h, need, tag):
    wg = need("g", h)
    gate = _mm_nt(h, wg, BF, 1024, 1408, tag + "_gate")
    wu = need("u", gate)
    up, act = _mm_nt_swiglu(h, wu, gate, 1024, 1408, tag + "_up_act")
    wd = need("d", up)
    return act, (h, gate, up, act, wg, wu, wd)


def _ffn_bwd(x_in, gain, saved, dxo, dys, alpha_out, emit, tag):
    h, gate, up, act, wg, wu, wd = saved
    sent = emit("d", _mm_tn(act, dys, BF, 256, 1024, tag + "_dwd"))
    dgate, dup = _mm_nt_dswiglu(dys, wd, gate, up, 1024, 1408, tag + "_dact", after=sent)
    sent = emit("g", _mm_tn(dgate, h, BF, 256, 1024, tag + "_dwg"))
    sent = emit("u", _mm_tn(dup, h, BF, 256, 1024, tag + "_dwu", after=sent))
    dh = _mm_nn2(dgate, dup, wg, wu, 1024, 512, tag + "_dh", after=sent)
    dx, dxs, dgain = _rms_bwd(x_in, gain, dh, dxo, alpha_out, tag + "_dnorm")
    return dx, dxs, dgain


SLAB = 2048
N_SLAB = T // SLAB
N_PAIR = A_HEADS // 2


def _pair_masks():
    lane = _iota((128, 128), 1)
    return lane < A_HD, lane >= A_HD


def _slope_table():
    h = 2 * jnp.arange(N_PAIR)[:, None] + jnp.minimum(jnp.arange(8), 1)[None, :]
    return jnp.broadcast_to((2.0 ** (-(h + 1).astype(F32)))[:, :, None], (N_PAIR, 8, 128))


def _rows(ref, start, d):
    if d == 1:
        return ref[pl.ds(start, 128), :]
    return ref[pl.ds(start, 128, stride=d), :]


def _put_rows(ref, start, d, val):
    if d == 1:
        ref[pl.ds(start, 128), :] = val
    else:
        ref[pl.ds(start, 128, stride=d), :] = val


def _units(d):
    return [(r, b, r + 128 * d * b) for r in range(d) for b in range(SLAB // (128 * d))]


def _band(d, prev_valid):
    qi = _iota((128, 256), 0)
    kj = _iota((128, 256), 1)
    steps = qi + 128 - kj
    valid = (steps >= 0) & (steps <= 128) & (prev_valid | (kj >= 128))
    return valid, (steps * d).astype(F32)


def _attn_fwd(z_at, name):
    def body(sl_ref, q_ref, kc_ref, kp_ref, vc_ref, vp_ref, of_ref, ob_ref, lse_ref, m_s, l_s, a_s):
        n = pl.program_id(1)
        lo, hi = _pair_masks()
        slopes = (sl_ref[0, 0:1, 0:1], sl_ref[0, 1:2, 0:1])

        def unit(d, start, b, first, carry):
            q = _rows(q_ref, start, d).astype(BF)
            kcur, vcur = _rows(kc_ref, start, d).astype(BF), _rows(vc_ref, start, d).astype(BF)
            if b > 0:
                (kprev, vprev), prev_valid = carry, True
            else:
                pstart = start + SLAB - 128 * d
                kprev, vprev = _rows(kp_ref, pstart, d).astype(BF), _rows(vp_ref, pstart, d).astype(BF)
                prev_valid = n > 0
            kcat = jnp.concatenate([kprev, kcur], axis=0)
            vcat = jnp.concatenate([vprev, vcur], axis=0)
            valid, dist = _band(d, prev_valid)
            ms, ls, pvs = [], [], []
            for e in range(2):
                qm = jnp.where(lo if e == 0 else hi, q, jnp.zeros_like(q))
                s = _dg(qm, kcat, NT) * (A_HD ** -0.5) - slopes[e] * dist
                s = jnp.where(valid, s, NEG)
                m = jnp.max(s, axis=1, keepdims=True)
                p = jnp.exp(s - m)
                ms.append(m)
                ls.append(jnp.sum(p, axis=1, keepdims=True))
                pvs.append(_dg(p.astype(BF), vcat, NN))
            m_u = jnp.where(lo, ms[0], ms[1])
            l_u = jnp.where(lo, ls[0], ls[1])
            a_u = jnp.where(lo, pvs[0], pvs[1])
            if first:
                m_n, l_n, a_n = m_u, l_u, a_u
            else:
                m_o = _rows(m_s, start, d)
                m_n = jnp.maximum(m_o, m_u)
                c_o = jnp.exp(m_o - m_n)
                c_u = jnp.exp(m_u - m_n)
                l_n = _rows(l_s, start, d) * c_o + l_u * c_u
                a_n = _rows(a_s, start, d) * c_o + a_u * c_u
            _put_rows(m_s, start, d, m_n)
            _put_rows(l_s, start, d, l_n)
            _put_rows(a_s, start, d, a_n)
            return kcur, vcur

        for pi, (_, d) in enumerate(PATTERNS):
            carry = None
            for r, b, start in _units(d):
                carry = unit(d, start, b, pi == 0, carry)
        l = l_s[...]
        out = a_s[...] / l
        of_ref[...] = out
        ob_ref[...] = out.astype(BF)
        lse_ref[...] = m_s[...] + jnp.log(l)

    cur = lambda c: pl.BlockSpec((SLAB, 128), lambda j, n: (n, c * N_PAIR + j))
    prv = lambda c: pl.BlockSpec((SLAB, 128), lambda j, n: (jnp.maximum(n - 1, 0), c * N_PAIR + j))
    out = pl.BlockSpec((SLAB, 128), lambda j, n: (n, j))
    return pl.pallas_call(
        body, grid=(N_PAIR, N_SLAB),
        in_specs=[pl.BlockSpec((1, 8, 128), lambda j, n: (j, 0, 0)), cur(0), cur(1), prv(1), cur(2), prv(2)],
        out_specs=[out, out, out],
        out_shape=[jax.ShapeDtypeStruct((T, AW), F32), jax.ShapeDtypeStruct((T, AW), BF),
                   jax.ShapeDtypeStruct((T, AW), F32)],
        scratch_shapes=[pltpu.VMEM((SLAB, 128), F32)] * 3,
        name=name, compiler_params=_cp())(_slope_table(), z_at, z_at, z_at, z_at, z_at)


def _attn_bwd(z_at, dout, out, lse, name):
    def body(sl_ref, q_ref, kc_ref, kp_ref, vc_ref, vp_ref, do_ref, o_ref, lse_ref, dq_ref, dk_ref, dv_ref,
             dq_s, dk_s, dv_s, ck_s, cv_s):
        step = pl.program_id(1)
        n = N_SLAB - 1 - step
        lo, hi = _pair_masks()
        slopes = (sl_ref[0, 0:1, 0:1], sl_ref[0, 1:2, 0:1])

        @pl.when(step == 0)
        def _():
            ck_s[...] = jnp.zeros_like(ck_s)
            cv_s[...] = jnp.zeros_like(cv_s)

        dk_s[...] = ck_s[...]
        dv_s[...] = cv_s[...]
        ck_s[...] = jnp.zeros_like(ck_s)
        cv_s[...] = jnp.zeros_like(cv_s)

        def add_rows(ref, start, d, val):
            _put_rows(ref, start, d, _rows(ref, start, d) + val)

        def unit(d, start, b, first, carry):
            q = _rows(q_ref, start, d).astype(BF)
            do_f = _rows(do_ref, start, d)
            do = do_f.astype(BF)
            prod = do_f * _rows(o_ref, start, d)
            lse_u = _rows(lse_ref, start, d)
            kcur, vcur = _rows(kc_ref, start, d).astype(BF), _rows(vc_ref, start, d).astype(BF)
            if b > 0:
                (kprev, vprev), prev_valid = carry, True
            else:
                pstart = start + SLAB - 128 * d
                kprev, vprev = _rows(kp_ref, pstart, d).astype(BF), _rows(vp_ref, pstart, d).astype(BF)
                prev_valid = n > 0
            kcat = jnp.concatenate([kprev, kcur], axis=0)
            vcat = jnp.concatenate([vprev, vcur], axis=0)
            valid, dist = _band(d, prev_valid)
            masks = (lo, hi)
            qms = [jnp.where(msk, q, jnp.zeros_like(q)) for msk in masks]
            doms = [jnp.where(msk, do, jnp.zeros_like(do)) for msk in masks]
            deltas = [jnp.sum(jnp.where(msk, prod, 0.0), axis=1, keepdims=True) for msk in masks]
            ss = [_dg(qm, kcat, NT) * (A_HD ** -0.5) - sl * dist for qm, sl in zip(qms, slopes)]
            dps = [_dg(dom, vcat, NT) for dom in doms]
            ps = [jnp.where(valid, jnp.exp(jnp.where(valid, s, NEG) - lse_u[:, 64 * e:64 * e + 1]), 0.0)
                  for e, s in enumerate(ss)]
            dss = [(p * (dp - delta)).astype(BF) for p, dp, delta in zip(ps, dps, deltas)]
            pbs = [p.astype(BF) for p in ps]
            dqs = [_dg(ds, kcat, NN) for ds in dss]
            dkc = (_dg(dss[0], qms[0], TN) + _dg(dss[1], qms[1], TN)) * (A_HD ** -0.5)
            dvc = _dg(pbs[0], doms[0], TN) + _dg(pbs[1], doms[1], TN)
            dq_u = jnp.where(lo, dqs[0], dqs[1]) * (A_HD ** -0.5)
            if first:
                _put_rows(dq_s, start, d, dq_u)
            else:
                add_rows(dq_s, start, d, dq_u)
            add_rows(dk_s, start, d, dkc[128:])
            add_rows(dv_s, start, d, dvc[128:])
            if b > 0:
                add_rows(dk_s, start - 128 * d, d, dkc[:128])
                add_rows(dv_s, start - 128 * d, d, dvc[:128])
            else:
                pstart = start + SLAB - 128 * d
                add_rows(ck_s, pstart, d, dkc[:128])
                add_rows(cv_s, pstart, d, dvc[:128])
            return kcur, vcur

        for pi, (_, d) in enumerate(PATTERNS):
            carry = None
            for r, b, start in _units(d):
                carry = unit(d, start, b, pi == 0, carry)
        dq_ref[...] = dq_s[...].astype(BF)
        dk_ref[...] = dk_s[...].astype(BF)
        dv_ref[...] = dv_s[...].astype(BF)

    rev = lambda n: N_SLAB - 1 - n
    cur = lambda c: pl.BlockSpec((SLAB, 128), lambda j, n: (rev(n), c * N_PAIR + j))
    prv = lambda c: pl.BlockSpec((SLAB, 128), lambda j, n: (jnp.maximum(rev(n) - 1, 0), c * N_PAIR + j))
    one = pl.BlockSpec((SLAB, 128), lambda j, n: (rev(n), j))
    return pl.pallas_call(
        body, grid=(N_PAIR, N_SLAB),
        in_specs=[pl.BlockSpec((1, 8, 128), lambda j, n: (j, 0, 0)), cur(0), cur(1), prv(1), cur(2), prv(2),
                  one, one, one],
        out_specs=[one, one, one], out_shape=[jax.ShapeDtypeStruct((T, AW), BF)] * 3,
        scratch_shapes=[pltpu.VMEM((SLAB, 128), F32)] * 5,
        name=name, compiler_params=_cp())(_slope_table(), z_at, z_at, z_at, z_at, z_at, dout, out, lse)


def _silu(x):
    return x * jax.nn.sigmoid(x)


def _qk_math(c):
    s = _silu(c)
    return s * lax.rsqrt(jnp.sum(s * s, axis=-1, keepdims=True) + L2_EPS)


def _softplus(x):
    return jnp.maximum(x, 0.0) + jnp.log(1.0 + jnp.exp(-jnp.abs(x)))


def _gate_math(bd, alog_row, dtb_row):
    rows = bd.shape[0]
    lane = _iota(bd.shape, 1)
    beta = jax.nn.sigmoid(bd)
    g = jnp.where((lane >= DN_H) & (lane < 2 * DN_H), -jnp.exp(alog_row) * _softplus(bd + dtb_row), 0.0)
    ri = _iota((rows, rows), 0)
    ci = _iota((rows, rows), 1)
    same = (ri // CH) == (ci // CH)
    li = _iota((128, 128), 0)
    lj = _iota((128, 128), 1)
    to_next_group = jnp.where((lj == li + DN_H) & (li >= DN_H) & (li < 2 * DN_H), 1.0, 0.0)
    gc = _hdot(jnp.where(same & (ci <= ri), 1.0, 0.0), g)
    glast = _hdot(_hdot(jnp.where(same, 1.0, 0.0), g), to_next_group)
    return jnp.where(lane < DN_H, beta, 0.0) + gc + glast


def _shift_down(cur, halo, s):
    if s == 0:
        return cur
    rolled = pltpu.roll(cur, s, 0)
    hr = pltpu.roll(halo, s, 0)
    head = jnp.where(_iota(hr.shape, 0) < s, hr, rolled[:8])
    return jnp.concatenate([head, rolled[8:]], axis=0)


def _shift_up(cur, halo, s):
    if s == 0:
        return cur
    rows = cur.shape[0]
    rolled = pltpu.roll(cur, rows - s, 0)
    hr = pltpu.roll(halo, 8 - s, 0)
    tail = jnp.where(_iota(hr.shape, 0) >= 8 - s, hr, rolled[rows - 8:])
    return jnp.concatenate([rolled[:rows - 8], tail], axis=0)


def _dn_pre_fwd(z_dn, conv_w8, alog_row, dtb_row, name):
    tm = 256
    wq = 3 * DNW

    def body(raw_ref, halo_ref, bd_ref, w_ref, al_ref, dt_ref, conv_ref, qkv_ref, bg_ref):
        i = pl.program_id(0)
        cur = raw_ref[...]
        halo = jnp.where(i > 0, halo_ref[...], 0.0)
        w = w_ref[...]
        conv = jnp.zeros((tm, wq), F32)
        for j in range(4):
            conv = conv + _shift_down(cur, halo, 3 - j) * w[j:j + 1, :]
        conv_ref[...] = conv
        for blk in range(3 * DN_H):
            sl = slice(128 * blk, 128 * blk + 128)
            c = conv[:, sl]
            qkv_ref[:, sl] = _qk_math(c) if blk < 2 * DN_H else _silu(c)
        bg_ref[...] = _gate_math(bd_ref[...], al_ref[...], dt_ref[...])

    one = pl.BlockSpec((1, 128), lambda i: (0, 0))
    return pl.pallas_call(
        body, grid=(T // tm,),
        in_specs=[pl.BlockSpec((tm, wq), lambda i: (i, 0)),
                  pl.BlockSpec((8, wq), lambda i: (jnp.maximum(i * (tm // 8) - 1, 0), 0)),
                  pl.BlockSpec((tm, 128), lambda i: (i, BD_BLK)),
                  pl.BlockSpec((8, wq), lambda i: (0, 0)), one, one],
        out_specs=[pl.BlockSpec((tm, wq), lambda i: (i, 0)), pl.BlockSpec((tm, wq), lambda i: (i, 0)),
                   pl.BlockSpec((tm, 128), lambda i: (i, 0))],
        out_shape=[jax.ShapeDtypeStruct((T, wq), F32), jax.ShapeDtypeStruct((T, wq), F32),
                   jax.ShapeDtypeStruct((T, 128), F32)],
        name=name, compiler_params=_cp())(z_dn, z_dn, z_dn, conv_w8, alog_row, dtb_row)


def _dn_pre_bwd(conv, z_dn, alog_row, dtb_row, dqn, dkn, dvn, dbg, name):
    tm = 256
    wq = 3 * DNW

    def body(conv_ref, bd_ref, al_ref, dt_ref, dq_ref, dk_ref, dv_ref, dbg_ref,
             dconv_ref, dbd_ref, dal_ref, ddt_ref):
        i = pl.program_id(0)
        for blk in range(3 * DN_H):
            sl = slice(128 * blk, 128 * blk + 128)
            src = (dq_ref, dk_ref, dv_ref)[blk // DN_H]
            ct = src[:, 128 * (blk % DN_H):128 * (blk % DN_H) + 128]
            fn = _qk_math if blk < 2 * DN_H else _silu
            _, vjp = jax.vjp(fn, conv_ref[:, sl])
            dconv_ref[:, sl] = vjp(ct)[0]
        _, vjp = jax.vjp(_gate_math, bd_ref[...], al_ref[...], dt_ref[...])
        dbd, dal, ddt = vjp(dbg_ref[...])
        dbd_ref[...] = dbd

        @pl.when(i == 0)
        def _():
            dal_ref[...] = dal
            ddt_ref[...] = ddt

        @pl.when(i > 0)
        def _():
            dal_ref[...] += dal
            ddt_ref[...] += ddt

    one = pl.BlockSpec((1, 128), lambda i: (0, 0))
    row = pl.BlockSpec((tm, wq), lambda i: (i, 0))
    hd = pl.BlockSpec((tm, DNW), lambda i: (i, 0))
    st = pl.BlockSpec((tm, 128), lambda i: (i, 0))
    return pl.pallas_call(
        body, grid=(T // tm,),
        in_specs=[row, pl.BlockSpec((tm, 128), lambda i: (i, BD_BLK)), one, one, hd, hd, hd, st],
        out_specs=[row, st, one, one],
        out_shape=[jax.ShapeDtypeStruct((T, wq), F32), jax.ShapeDtypeStruct((T, 128), F32),
                   jax.ShapeDtypeStruct((1, 128), F32), jax.ShapeDtypeStruct((1, 128), F32)],
        name=name, compiler_params=_cp())(conv, z_dn, alog_row, dtb_row, dqn, dkn, dvn, dbg)


def _dn_conv_bwd(dconv, z_dn, conv_w8, name):
    tm = 256
    wq = 3 * DNW
    last = T // tm - 1

    def body(dc_ref, dcn_ref, raw_ref, halo_ref, w_ref, draw_ref, dw_ref):
        i = pl.program_id(0)
        dc = dc_ref[...]
        nxt = jnp.where(i < last, dcn_ref[...], 0.0)
        cur = raw_ref[...]
        halo = jnp.where(i > 0, halo_ref[...], 0.0)
        w = w_ref[...]
        draw = jnp.zeros((tm, wq), F32)
        rows = []
        for j in range(4):
            draw = draw + _shift_up(dc, nxt, 3 - j) * w[j:j + 1, :]
            rows.append(jnp.sum(dc * _shift_down(cur, halo, 3 - j), axis=0, keepdims=True))
        draw_ref[...] = draw
        part = jnp.concatenate(rows + [jnp.zeros((4, wq), F32)], axis=0)

        @pl.when(i == 0)
        def _():
            dw_ref[...] = part

        @pl.when(i > 0)
        def _():
            dw_ref[...] += part

    row = pl.BlockSpec((tm, wq), lambda i: (i, 0))
    return pl.pallas_call(
        body, grid=(T // tm,),
        in_specs=[row, pl.BlockSpec((8, wq), lambda i: (jnp.minimum((i + 1) * (tm // 8), T // 8 - 1), 0)),
                  row, pl.BlockSpec((8, wq), lambda i: (jnp.maximum(i * (tm // 8) - 1, 0), 0)),
                  pl.BlockSpec((8, wq), lambda i: (0, 0))],
        out_specs=[row, pl.BlockSpec((8, wq), lambda i: (0, 0))],
        out_shape=[jax.ShapeDtypeStruct((T, wq), F32), jax.ShapeDtypeStruct((8, wq), F32)],
        name=name, compiler_params=_cp())(dconv, dconv, z_dn, z_dn, conv_w8)


def _h3(a, b, dims=NN):
    return lax.dot_general(a, b, dims, precision=lax.Precision.HIGH, preferred_element_type=F32)


@jax.custom_vjp
def _inverse_given(a_mat, tinv):
    return tinv


def _inverse_given_fwd(a_mat, tinv):
    return tinv, tinv


def _inverse_given_bwd(tinv, g):
    return -_h3(tinv, _h3(g, tinv, NT), TN), jnp.zeros_like(tinv)


_inverse_given.defvjp(_inverse_given_fwd, _inverse_given_bwd)


def _prep_head(q, k, v, bgc, h):
    beta = _col(bgc, h)
    gc = jnp.broadcast_to(_col(bgc, DN_H + h), (PAIR, 128))
    glast = jnp.broadcast_to(_col(bgc, 2 * DN_H + h), (PAIR, 128))
    ri = _iota((PAIR, PAIR), 0)
    ci = _iota((PAIR, PAIR), 1)
    same = (ri // CH) == (ci // CH)
    causal = same & (ci <= ri)
    strict = same & (ci < ri)
    eye = ri == ci
    gc_cols = _hdot(jnp.ones((PAIR, PAIR), F32), jnp.where(eye, gc, 0.0))
    decay = jnp.exp(jnp.where(causal, gc - gc_cols, NEG))
    egc = jnp.exp(gc)
    kb = k * beta
    a_mat = jnp.where(strict, _bdot_nt(kb, k) * decay, 0.0)
    qs = q * (DN_HD ** -0.5)
    attn = jnp.where(causal, _bdot_nt(qs, k) * decay, 0.0)
    return a_mat, (v * beta, kb * egc, qs * egc, k * jnp.exp(glast - gc), attn, jnp.exp(glast))


def _prep_tail(tinv, ctx):
    vb, kbe, qg, kdec, attn, decb = ctx
    return _h3(tinv, vb), _h3(tinv, kbe), qg, kdec, attn, decb


def _inverses(a_mats):
    eye = jnp.where(_iota((PAIR, PAIR), 0) == _iota((PAIR, PAIR), 1), 1.0, 0.0)
    ps = [-a for a in a_mats]
    tinvs = [eye + p for p in ps]
    for _ in range(5):
        ps = [_h3(p, p) for p in ps]
        tinvs = [t + _h3(t, p) for t, p in zip(tinvs, ps)]
    return tinvs


def _prep_math(q, k, v, bgc, h, tinv_saved):
    a_mat, ctx = _prep_head(q, k, v, bgc, h)
    return _prep_tail(_inverse_given(a_mat, tinv_saved), ctx)


def _dn_prep_fwd(qkv, bg, name):
    rows = 512
    hd = lambda off: pl.BlockSpec((rows, 128), lambda g, h: (g, off + h))
    out = pl.BlockSpec((rows, 128), lambda g, h: (g, h))

    def body(q_ref, k_ref, v_ref, bg_ref, *outs):
        h = pl.program_id(1)
        spans = [slice(PAIR * pr, PAIR * pr + PAIR) for pr in range(rows // PAIR)]
        heads = [_prep_head(q_ref[rs, :], k_ref[rs, :], v_ref[rs, :], bg_ref[rs, :], h) for rs in spans]
        tinvs = _inverses([a for a, _ in heads])
        for rs, tinv, (_, ctx) in zip(spans, tinvs, heads):
            for o_ref, val in zip(outs, _prep_tail(tinv, ctx) + (tinv,)):
                o_ref[rs, :] = val

    return pl.pallas_call(
        body, grid=(T // rows, DN_H),
        in_specs=[hd(0), hd(DN_H), hd(2 * DN_H), pl.BlockSpec((rows, 128), lambda g, h: (g, 0))],
        out_specs=[out] * 7, out_shape=[jax.ShapeDtypeStruct((T, DNW), F32)] * 7,
        name=name, compiler_params=_cp())(qkv, qkv, qkv, bg)


def _dn_prep_bwd(qkv, bg, tinv, cts, name):
    rows = 512
    hd = lambda off: pl.BlockSpec((rows, 128), lambda g, h: (g, off + h))
    out = pl.BlockSpec((rows, 128), lambda g, h: (g, h))
    st = pl.BlockSpec((rows, 128), lambda g, h: (g, 0))

    def body(q_ref, k_ref, v_ref, bg_ref, ti_ref, c0, c1, c2, c3, c4, c5, dq_ref, dk_ref, dv_ref, dbg_ref):
        h = pl.program_id(1)
        spans = [slice(PAIR * pr, PAIR * pr + PAIR) for pr in range(rows // PAIR)]
        tis = [ti_ref[rs, :] for rs in spans]

        def joint(qs, ks, vs, bs):
            heads = [_prep_head(q, k, v, b, h) for q, k, v, b in zip(qs, ks, vs, bs)]
            return [_prep_tail(_inverse_given(a, ti), ctx) for (a, ctx), ti in zip(heads, tis)]

        _, vjp = jax.vjp(joint, *[[r[rs, :] for rs in spans] for r in (q_ref, k_ref, v_ref, bg_ref)])
        dqs, dks, dvs, dbs = vjp([tuple(c[rs, :] for c in (c0, c1, c2, c3, c4, c5)) for rs in spans])
        for rs, dq, dk, dv in zip(spans, dqs, dks, dvs):
            dq_ref[rs, :] = dq
            dk_ref[rs, :] = dk
            dv_ref[rs, :] = dv
        dbg_all = jnp.concatenate(dbs, axis=0)

        @pl.when(h == 0)
        def _():
            dbg_ref[...] = dbg_all

        @pl.when(h > 0)
        def _():
            dbg_ref[...] += dbg_all

    return pl.pallas_call(
        body, grid=(T // rows, DN_H),
        in_specs=[hd(0), hd(DN_H), hd(2 * DN_H), st] + [out] * 7,
        out_specs=[out, out, out, st],
        out_shape=[jax.ShapeDtypeStruct((T, DNW), F32)] * 3 + [jax.ShapeDtypeStruct((T, 128), F32)],
        name=name, compiler_params=_cp())(qkv, qkv, qkv, bg, tinv, *cts)


def _step_math(s, u, w, qg, kdec, attn, decb, sub):
    vnew = u - _bdot_nn(w, s)
    z = jnp.zeros((CH, 128), F32)
    vfull = jnp.concatenate([vnew, z] if sub == 0 else [z, vnew], axis=0)
    o = _bdot_nn(qg, s) + _bdot_nn(attn, vfull)
    dec = jnp.sum(decb, axis=0, keepdims=True) * (1.0 / CH)
    return s * dec + _bdot_tn(kdec, vnew), o


def _dn_scan_fwd(prep, name):
    npair = T // PAIR
    row = pl.BlockSpec((PAIR, DNW), lambda p: (p, 0))

    def body(u_ref, w_ref, qg_ref, kd_ref, at_ref, db_ref, o_ref, ss_ref, s_ref):
        @pl.when(pl.program_id(0) == 0)
        def _():
            s_ref[...] = jnp.zeros_like(s_ref)

        states = [s_ref[h] for h in range(DN_H)]
        for sub in range(2):
            rs = slice(CH * sub, CH * sub + CH)
            for h in range(DN_H):
                ls = slice(128 * h, 128 * h + 128)
                ss_ref[sub, h] = states[h]
                states[h], o = _step_math(states[h], u_ref[rs, ls], w_ref[rs, ls], qg_ref[rs, ls],
                                          kd_ref[rs, ls], at_ref[rs, ls], db_ref[rs, ls], sub)
                o_ref[rs, ls] = o
        for h in range(DN_H):
            s_ref[h] = states[h]

    return pl.pallas_call(
        body, grid=(npair,), in_specs=[row] * 6,
        out_specs=[row, pl.BlockSpec((2, DN_H, 128, 128), lambda p: (p, 0, 0, 0))],
        out_shape=[jax.ShapeDtypeStruct((T, DNW), F32), jax.ShapeDtypeStruct((T // CH, DN_H, 128, 128), F32)],
        scratch_shapes=[pltpu.VMEM((DN_H, 128, 128), F32)],
        name=name, compiler_params=_cp())(*prep)


def _dn_scan_bwd(prep, states, do, name):
    npair = T // PAIR
    row = pl.BlockSpec((PAIR, DNW), lambda p: (npair - 1 - p, 0))

    def body(u_ref, w_ref, qg_ref, kd_ref, at_ref, db_ref, ss_ref, do_ref, *rest):
        outs, ds_ref = rest[:6], rest[6]

        @pl.when(pl.program_id(0) == 0)
        def _():
            ds_ref[...] = jnp.zeros_like(ds_ref)

        dss = [ds_ref[h] for h in range(DN_H)]
        for sub in (1, 0):
            rs = slice(CH * sub, CH * sub + CH)
            for h in range(DN_H):
                ls = slice(128 * h, 128 * h + 128)
                args = (ss_ref[sub, h],) + tuple(r[rs, ls] for r in (u_ref, w_ref, qg_ref, kd_ref, at_ref, db_ref))
                _, vjp = jax.vjp(functools.partial(_step_math, sub=sub), *args)
                cts = vjp((dss[h], do_ref[rs, ls]))
                dss[h] = cts[0]
                for o_ref, val in zip(outs, cts[1:]):
                    o_ref[rs, ls] = val
        for h in range(DN_H):
            ds_ref[h] = dss[h]

    return pl.pallas_call(
        body, grid=(npair,),
        in_specs=[row] * 6 + [pl.BlockSpec((2, DN_H, 128, 128), lambda p: (npair - 1 - p, 0, 0, 0)), row],
        out_specs=[row] * 6, out_shape=[jax.ShapeDtypeStruct((T, DNW), F32)] * 6,
        scratch_shapes=[pltpu.VMEM((DN_H, 128, 128), F32)],
        name=name, compiler_params=_cp())(*prep, states, do)


def _post_math(o, gate, wrow):
    return o * lax.rsqrt(jnp.mean(o * o, axis=-1, keepdims=True) + NORM_EPS) * wrow * _silu(gate)


def _dn_post_fwd(o, z_dn, dn_norm, name):
    tm = 512
    row = pl.BlockSpec((tm, DNW), lambda i: (i, 0))

    def body(o_ref, g_ref, w_ref, y_ref):
        for h in range(DN_H):
            ls = slice(128 * h, 128 * h + 128)
            y_ref[:, ls] = _post_math(o_ref[:, ls], g_ref[:, ls], w_ref[...]).astype(BF)

    return pl.pallas_call(
        body, grid=(T // tm,),
        in_specs=[row, pl.BlockSpec((tm, DNW), lambda i: (i, 3)), pl.BlockSpec((1, 128), lambda i: (0, 0))],
        out_specs=row, out_shape=jax.ShapeDtypeStruct((T, DNW), BF),
        name=name, compiler_params=_cp())(o, z_dn, dn_norm)


def _dn_post_bwd(o, z_dn, dn_norm, dy, name):
    tm = 512
    row = pl.BlockSpec((tm, DNW), lambda i: (i, 0))
    one = pl.BlockSpec((1, 128), lambda i: (0, 0))

    def body(o_ref, g_ref, w_ref, dy_ref, do_ref, dg_ref, dw_ref):
        i = pl.program_id(0)
        dw = jnp.zeros((1, 128), F32)
        for h in range(DN_H):
            ls = slice(128 * h, 128 * h + 128)
            _, vjp = jax.vjp(_post_math, o_ref[:, ls], g_ref[:, ls], w_ref[...])
            do, dg, dwh = vjp(dy_ref[:, ls].astype(F32))
            do_ref[:, ls] = do
            dg_ref[:, ls] = dg
            dw = dw + dwh

        @pl.when(i == 0)
        def _():
            dw_ref[...] = dw

        @pl.when(i > 0)
        def _():
            dw_ref[...] += dw

    return pl.pallas_call(
        body, grid=(T // tm,),
        in_specs=[row, pl.BlockSpec((tm, DNW), lambda i: (i, 3)), one, pl.BlockSpec((tm, DNW), lambda i: (i, 1))],
        out_specs=[row, row, one],
        out_shape=[jax.ShapeDtypeStruct((T, DNW), F32), jax.ShapeDtypeStruct((T, DNW), F32),
                   jax.ShapeDtypeStruct((1, 128), F32)],
        name=name, compiler_params=_cp())(o, z_dn, dn_norm, dy)


def _dz_dn_assemble(draw, dgate, dbd, name):
    tm = 512

    def body(a_ref, b_ref, c_ref, o_ref):
        o_ref[:, :3 * DNW] = a_ref[...].astype(BF)
        o_ref[:, 3 * DNW:4 * DNW] = b_ref[...].astype(BF)
        o_ref[:, 4 * DNW:4 * DNW + 128] = c_ref[...].astype(BF)
        o_ref[:, 4 * DNW + 128:] = jnp.zeros((tm, 128), BF)

    return pl.pallas_call(
        body, grid=(T // tm,),
        in_specs=[pl.BlockSpec((tm, 3 * DNW), lambda i: (i, 0)), pl.BlockSpec((tm, DNW), lambda i: (i, 0)),
                  pl.BlockSpec((tm, 128), lambda i: (i, 0))],
        out_specs=pl.BlockSpec((tm, ZD), lambda i: (i, 0)),
        out_shape=jax.ShapeDtypeStruct((T, ZD), BF), name=name, compiler_params=_cp())(draw, dgate, dbd)


HBM = pl.BlockSpec(memory_space=pltpu.HBM)
SEM = pl.BlockSpec(memory_space=pltpu.SEMAPHORE)
EFFECT = pltpu.SideEffectType.DATAFLOW_SIDE_EFFECTING
N_PEER = N_DEV - 1


def _peers(x, y, c):
    return [(k, (x ^ (k >> 2), y ^ ((k >> 1) & 1), c ^ (k & 1))) for k in (1, 2, 4, 3, 5, 6, 7)]


def _exchange_copy(ins, lands, ssems, rsems, scatter, t, k, pos, me):
    px, py, pc = pos
    src = ins[t].at[4 * px + 2 * py + pc] if scatter else ins[t]
    return pltpu.make_async_remote_copy(
        src_ref=src, dst_ref=lands[t].at[me], send_sem=ssems[t].at[k - 1], recv_sem=rsems[t].at[k - 1],
        device_id=pos, device_id_type=MESH_ID)


def _xstart(bufs, scatter, name):
    nt = len(bufs)
    lands = [lax.empty((N_DEV,) + tuple(b.shape[1:] if scatter else b.shape), b.dtype) for b in bufs]

    def body(*refs):
        ins, lnd = refs[:nt], refs[nt:2 * nt]
        ssems, rsems = refs[2 * nt:3 * nt], refs[3 * nt:4 * nt]
        token = refs[-1]
        x, y, c = lax.axis_index("x"), lax.axis_index("y"), lax.axis_index("c")
        me = 4 * x + 2 * y + c
        for t in range(nt):
            for k, pos in _peers(x, y, c):
                _exchange_copy(ins, lnd, ssems, rsems, scatter, t, k, pos, me).start()
        token[...] = jnp.zeros_like(token)

    both = list(bufs) + lands
    res = pl.pallas_call(
        body, name=name,
        out_shape=[pltpu.SemaphoreType.DMA((N_PEER,))] * (2 * nt)
        + [pltpu.HBM(b.shape, b.dtype) for b in both] + [jax.ShapeDtypeStruct((8, 128), F32)],
        in_specs=[HBM] * (2 * nt),
        out_specs=[SEM] * (2 * nt) + [HBM] * (2 * nt) + [pl.BlockSpec(memory_space=pltpu.VMEM)],
        input_output_aliases={i: 2 * nt + i for i in range(2 * nt)},
        compiler_params=pltpu.CompilerParams(has_side_effects=EFFECT),
    )(*[pltpu.with_memory_space_constraint(b, pltpu.HBM) for b in both])
    return res[:nt], res[nt:2 * nt], res[2 * nt:3 * nt], res[3 * nt:4 * nt], res[-1][0, 0]


def _xwait(ssems, rsems, thrus, lands, scatter, after, name):
    nt = len(lands)

    def body(*refs):
        ins, lnd = refs[:nt], refs[nt:2 * nt]
        ss, rs = refs[2 * nt:3 * nt], refs[3 * nt:4 * nt]
        x, y, c = lax.axis_index("x"), lax.axis_index("y"), lax.axis_index("c")
        me = 4 * x + 2 * y + c
        for t in range(nt):
            for k, pos in _peers(x, y, c):
                cp = _exchange_copy(ins, lnd, ss, rs, scatter, t, k, pos, me)
                cp.wait_send()
                cp.wait_recv()

    both = list(thrus) + list(lands)
    res = pl.pallas_call(
        body, name=name, out_shape=[pltpu.HBM(b.shape, b.dtype) for b in both],
        in_specs=[HBM] * (2 * nt) + [SEM] * (2 * nt) + [ANY], out_specs=[HBM] * (2 * nt),
        input_output_aliases={i: i for i in range(2 * nt)},
        compiler_params=pltpu.CompilerParams(has_side_effects=EFFECT),
    )(*both, *ssems, *rsems, after)
    return res[:nt], res[nt:]


def _adam(recv, w, m, v, tr, name):
    _, r, c = w.shape
    n_part = recv.shape[0]
    c1 = np.float32(1.0 - ADAM_B1 ** ADAM_STEP)
    c2 = np.float32(1.0 - ADAM_B2 ** ADAM_STEP)

    def body(r_ref, w_ref, m_ref, v_ref, g_ref, d_ref, mo_ref, vo_ref):
        g = r_ref[0].astype(F32)
        for s in range(1, n_part):
            g = g + r_ref[s].astype(F32)
        mn = ADAM_B1 * m_ref[0] + (1.0 - ADAM_B1) * g
        vn = ADAM_B2 * v_ref[0] + (1.0 - ADAM_B2) * (g * g)
        g_ref[0] = g
        mo_ref[0] = mn
        vo_ref[0] = vn
        d_ref[0] = -ADAM_LR * ((mn / c1) / (jnp.sqrt(vn / c2) + ADAM_EPS) + ADAM_WD * w_ref[0])

    one = pl.BlockSpec((1, tr, c), lambda i: (0, i, 0))
    return pl.pallas_call(
        body, grid=(r // tr,), in_specs=[pl.BlockSpec((n_part, tr, c), lambda i: (0, i, 0)), one, one, one],
        out_specs=[one] * 4, out_shape=[jax.ShapeDtypeStruct((1, r, c), F32)] * 4,
        name=name, compiler_params=_cp())(recv, w, m, v)


def _local_step(x, target, sp, need, emit):
    g = {}
    x0, h1 = x, _rms_fwd(x, sp["norm_ffn1"], "ffn1_norm")
    act1, saved1 = _ffn_up(h1, lambda kind, a: need("w" + kind + "1", a), "ffn1")
    x1, h2 = _mm_nn_resnorm([(act1, saved1[-1])], x0, 0.5, sp["norm_mix"], 512, "ffn1_down_norm")
    win_a, win_d = need("win_a", h2), need("win_d", h2)
    conv_w8, wout = need("conv_w8", h2), need("wout", h2)
    z_at = _mm_nn(h2, win_a, F32, 1024, 768, "mix_in_attn")
    z_dn = _mm_nn(h2, win_d, F32, 1024, 768, "mix_in_dn")
    attn_f, attn_b, lse = _attn_fwd(z_at, "attn_fwd")

    conv, qkvn, bg = _dn_pre_fwd(z_dn, conv_w8, sp["alog_row"], sp["dtb_row"], "dn_pre")
    *prep, tinv = _dn_prep_fwd(qkvn, bg, "dn_prep")
    o_dn, states = _dn_scan_fwd(prep, "dn_scan")
    dn_b = _dn_post_fwd(o_dn, z_dn, sp["dn_norm"], "dn_post")

    x2, h3 = _mm_nn_resnorm([(attn_b, wout[:AW]), (dn_b, wout[AW:])], x1, 1.0, sp["norm_ffn2"], 512,
                            "mix_out_norm")
    act2, saved2 = _ffn_up(h3, lambda kind, a: need("w" + kind + "2", a), "ffn2")
    y3 = _mm_nn(act2, saved2[-1], F32, 1024, 1024, "ffn2_down")

    dx3, dys3, g["norm_final"], loss8 = _loss_bwd(x2, y3, sp["norm_final"], target, "loss")
    dx2, dx2b, g["norm_ffn2"] = _ffn_bwd(
        x2, sp["norm_ffn2"], saved2, dx3, dys3, 1.0,
        lambda kind, dw: emit(kind + "2", {"w" + kind + "2": dw}), "ffn2b")

    zero = emit("wout", {"wout": jnp.concatenate([_mm_tn(attn_b, dx2b, BF, 512, 1024, "mix_out_dw_a"),
                                                  _mm_tn(dn_b, dx2b, BF, 512, 1024, "mix_out_dw_d")], axis=0)})
    dmix = _mm_nt(dx2b, wout, F32, 1024, 1024, "mix_out_dx", after=zero)

    dz_at = jnp.concatenate(_attn_bwd(z_at, dmix, attn_f, lse, "attn_bwd"), axis=1)

    do_dn, dgate, g["dn_norm"] = _dn_post_bwd(o_dn, z_dn, sp["dn_norm"], dmix, "dn_post_b")
    cts = _dn_scan_bwd(prep, states, do_dn, "dn_scan_b")
    dqn, dkn, dvn, dbg = _dn_prep_bwd(qkvn, bg, tinv, cts, "dn_prep_b")
    dconv, dbd, g["alog_row"], g["dtb_row"] = _dn_pre_bwd(
        conv, z_dn, sp["alog_row"], sp["dtb_row"], dqn, dkn, dvn, dbg, "dn_pre_b")
    draw, dconv_w8 = _dn_conv_bwd(dconv, z_dn, conv_w8, "dn_conv_b")
    dz_dn = _dz_dn_assemble(draw, dgate, dbd, "dn_dz")

    zero = emit("win", {"win_a": _mm_tn(h2, dz_at, BF, 512, 768, "mix_in_dw_a"),
                        "win_d": _mm_tn(h2, dz_dn, BF, 512, 768, "mix_in_dw_d"), "conv_w8": dconv_w8})
    dh2 = _mm_nt(dz_at, win_a, F32, 1024, 1024, "mix_in_dx_a", after=zero)
    dh2d = _mm_nt(dz_dn, win_d, F32, 1024, 1024, "mix_in_dx_d")
    (dx1, dys1), _, g["norm_mix"] = _rms_bwd2(x1, sp["norm_mix"], dh2, dh2d, dx2, "mix_dnorm")
    dx0, _, g["norm_ffn1"] = _ffn_bwd(
        x0, sp["norm_ffn1"], saved1, dx1, dys1, 1.0,
        lambda kind, dw: emit(kind + "1", {"w" + kind + "1": dw}), "ffn1b")
    return loss8[0, 0], dx0, g


def _rms_bwd2(x, gain, dh_a, dh_b, dres, name):
    tm = 512
    row = pl.BlockSpec((tm, D), lambda i: (i, 0))
    gspec = pl.BlockSpec((1, D), lambda i: (0, 0))

    def body(x_ref, g_ref, da_ref, db_ref, dres_ref, dx_ref, dxs_ref, dg_ref):
        i = pl.program_id(0)
        xv = x_ref[...]
        r = lax.rsqrt(jnp.mean(xv * xv, axis=-1, keepdims=True) + NORM_EPS)
        xh = xv * r
        dhv = da_ref[...] + db_ref[...]
        part = jnp.sum(dhv * xh, axis=0, keepdims=True)

        @pl.when(i == 0)
        def _():
            dg_ref[...] = part

        @pl.when(i > 0)
        def _():
            dg_ref[...] += part

        dxh = dhv * g_ref[...]
        dx = r * (dxh - xh * jnp.mean(dxh * xh, axis=-1, keepdims=True)) + dres_ref[...]
        dx_ref[...] = dx
        dxs_ref[...] = (0.5 * dx).astype(BF)

    dx, dxs, dg = pl.pallas_call(
        body, grid=(T // tm,), in_specs=[row, gspec, row, row, row], out_specs=[row, row, gspec],
        out_shape=[jax.ShapeDtypeStruct((T, D), F32), jax.ShapeDtypeStruct((T, D), BF),
                   jax.ShapeDtypeStruct((1, D), F32)],
        name=name, compiler_params=_cp())(x, gain, dh_a, dh_b, dres)
    return (dx, dxs), None, dg


def _cols_from_shards(gathered):
    n, r, c = gathered.shape
    return jnp.transpose(gathered, (1, 0, 2)).reshape(r, n * c)


def _shards_from_cols(full, dtype):
    r, nc = full.shape
    return jnp.transpose(full.reshape(r, N_DEV, nc // N_DEV), (1, 0, 2)).astype(dtype)


def _lane_row(vec4):
    return jnp.zeros((1, 128), F32).at[:, DN_H:2 * DN_H].set(vec4.astype(F32))


WEIGHT_SOURCES = {"wg1": "gate1", "wu1": "up1", "wd1": "down1", "win_a": "w_in", "win_d": "w_in",
                  "conv_w8": "conv_w", "wout": "w_out", "wg2": "gate2", "wu2": "up2", "wd2": "down2"}
TRANSPOSED = ("gate1", "up1", "gate2", "up2")


def _build_weights(name, gath):
    if name in ("wg1", "wu1", "wd1", "wg2", "wu2", "wd2"):
        return {name: gath[WEIGHT_SOURCES[name]].reshape(F, D)}
    if name in ("win_a", "win_d"):
        w_in = _cols_from_shards(gath["w_in"])
        c0 = 3 * AW + 3 * DNW
        wp = jnp.concatenate([w_in[:, :c0], w_in[:, c0 + 2 * DN_H:], w_in[:, c0:c0 + 2 * DN_H],
                              jnp.zeros((D, ZP - IN_COLS), w_in.dtype)], axis=1)
        return {"win_a": wp[:, :ZA], "win_d": wp[:, ZA:]}
    if name == "wout":
        return {name: gath["w_out"].reshape(D, D)}
    conv = _cols_from_shards(gath["conv_w"])
    return {"conv_w8": jnp.concatenate([conv, jnp.zeros((4, 3 * DNW), F32)], axis=0)}


def _small_params(norm_ffn1, norm_mix, norm_ffn2, norm_final, a_log, dt_bias, dn_norm):
    return {"norm_ffn1": norm_ffn1, "norm_mix": norm_mix, "norm_ffn2": norm_ffn2,
            "norm_final": norm_final.reshape(1, D), "alog_row": _lane_row(a_log), "dtb_row": _lane_row(dt_bias),
            "dn_norm": dn_norm}


def _grad_slabs(group, g):
    if group[0] in "gud":
        return {WEIGHT_SOURCES["w" + group]: g["w" + group].reshape(N_DEV, F // N_DEV, D)}
    if group == "wout":
        return {"w_out": g["wout"].reshape(N_DEV, D // N_DEV, D)}
    gp = jnp.concatenate([g["win_a"], g["win_d"]], axis=1)
    c0 = 3 * AW + 3 * DNW
    g_in = jnp.concatenate([gp[:, :c0], gp[:, c0 + DNW:c0 + DNW + 2 * DN_H], gp[:, c0:c0 + DNW]], axis=1)
    return {"w_in": _shards_from_cols(g_in, BF), "conv_w": _shards_from_cols(g["conv_w8"][:4], F32)}


SMALL_ROWS = 40


def _small_pack(norm_ffn1, norm_mix, norm_ffn2, norm_final, dn_norm, alog_row, dtb_row, loss=None):
    rows = [a.reshape(8, 128) for a in (norm_ffn1, norm_mix, norm_ffn2, norm_final)]
    loss_row = jnp.zeros((1, 128), F32) if loss is None else jnp.broadcast_to(loss.reshape(1, 1), (1, 128))
    rows += [dn_norm.reshape(1, 128), alog_row, dtb_row, loss_row, jnp.zeros((SMALL_ROWS - 36, 128), F32)]
    return jnp.concatenate(rows, axis=0)


def _small_unpack(pk):
    pk = pk[0]
    return (pk[0:8].reshape(1, D), pk[8:16].reshape(1, D), pk[16:24].reshape(1, D), pk[24:32].reshape(D),
            pk[32:33], pk[33:34, DN_H:2 * DN_H], pk[34:35, DN_H:2 * DN_H])


ADAM_TILE = {"gate1": 256, "up1": 256, "down1": 176, "gate2": 256, "up2": 256, "down2": 176,
             "w_in": 256, "w_out": 128, "conv_w": 4}
BIG = ("gate1", "up1", "down1", "w_in", "w_out", "gate2", "up2", "down2", "conv_w")


def kernel(x, norm_ffn1, ffn1_gate, ffn1_up, ffn1_down, norm_mix, w_in, conv_w, a_log, dt_bias, dn_norm, w_out, norm_ffn2, ffn2_gate, ffn2_up, ffn2_down, norm_final, loss_target, m_norm_ffn1, m_ffn1_gate, m_ffn1_up, m_ffn1_down, m_norm_mix, m_w_in, m_conv_w, m_a_log, m_dt_bias, m_dn_norm, m_w_out, m_norm_ffn2, m_ffn2_gate, m_ffn2_up, m_ffn2_down, m_norm_final, v_norm_ffn1, v_ffn1_gate, v_ffn1_up, v_ffn1_down, v_norm_mix, v_w_in, v_conv_w, v_a_log, v_dt_bias, v_dn_norm, v_w_out, v_norm_ffn2, v_ffn2_gate, v_ffn2_up, v_ffn2_down, v_norm_final):
    w = {"gate1": ffn1_gate, "up1": ffn1_up, "down1": ffn1_down, "w_in": w_in, "w_out": w_out,
         "gate2": ffn2_gate, "up2": ffn2_up, "down2": ffn2_down, "conv_w": conv_w}
    m = {"gate1": m_ffn1_gate, "up1": m_ffn1_up, "down1": m_ffn1_down, "w_in": m_w_in, "w_out": m_w_out,
         "gate2": m_ffn2_gate, "up2": m_ffn2_up, "down2": m_ffn2_down, "conv_w": m_conv_w}
    v = {"gate1": v_ffn1_gate, "up1": v_ffn1_up, "down1": v_ffn1_down, "w_in": v_w_in, "w_out": v_w_out,
         "gate2": v_ffn2_gate, "up2": v_ffn2_up, "down2": v_ffn2_down, "conv_w": v_conv_w}

    me = 4 * lax.axis_index("x") + 2 * lax.axis_index("y") + lax.axis_index("c")
    own_slot = lambda land, mine: lax.dynamic_update_index_in_dim(land, mine, me, 0)

    ag_order = ("gate1", "up1", "down1", "w_in", "conv_w", "w_out", "gate2", "up2", "down2")
    ag_groups = (("gate1",), ("up1",), ("down1",), ("w_in", "conv_w", "w_out"), ("gate2",), ("up2",), ("down2",))
    pos = {n: i for i, n in enumerate(ag_order)}

    def shard(n):
        if n == "conv_w":
            return w[n][0]
        return (w[n][0].T if n in TRANSPOSED else w[n][0]).astype(BF)

    ss, rs, thru, land, zero = _xstart([shard(n) for n in ag_order], False, "weights_start")
    gath, built = {}, {}

    def need(name, after):
        if name not in built:
            src = WEIGHT_SOURCES[name]
            if src not in gath:
                gi = [i for i, grp in enumerate(ag_groups) if src in grp][0]
                ids = [pos[n] for n in ag_groups[gi]]
                thrus, lands = _xwait([ss[i] for i in ids], [rs[i] for i in ids], [thru[i] for i in ids],
                                      [land[i] for i in ids], False, after, "weights_wait%d" % gi)
                for n, t, l in zip(ag_groups[gi], thrus, lands):
                    gath[n] = own_slot(l, t)
            built.update(_build_weights(name, gath))
        return built[name]

    pending = []

    def emit(group, grads):
        slabs = grads if group == "small" else _grad_slabs(group, grads)
        names = list(slabs)
        started = _xstart([slabs[n] for n in names], True, "grads_start_" + group)
        pending.append((group, names) + started[:4])
        return started[4]

    sp = _small_params(norm_ffn1 + zero, norm_mix, norm_ffn2, norm_final, a_log, dt_bias, dn_norm)
    loss_part, dx, g = _local_step(x[0], loss_target[0], sp, need, emit)
    small = _small_pack(g["norm_ffn1"], g["norm_mix"], g["norm_ffn2"], g["norm_final"], g["dn_norm"],
                        g["alog_row"], g["dtb_row"], loss_part)
    emit("small", {"small": jnp.broadcast_to(small[None], (N_DEV, SMALL_ROWS, 128))})

    pack = lambda a: _small_pack(*a)[None]
    res, after = {}, dx
    for group, names, gss, grs, gthru, gland in pending:
        thrus, lands = _xwait(gss, grs, gthru, gland, True, after, "grads_wait_" + group)
        for n, t, l in zip(names, thrus, lands):
            recv = own_slot(l, lax.dynamic_index_in_dim(t, me, 0, keepdims=False))
            if n == "small":
                res[n] = _adam(
                    recv,
                    pack((norm_ffn1, norm_mix, norm_ffn2, norm_final, dn_norm, _lane_row(a_log), _lane_row(dt_bias))),
                    pack((m_norm_ffn1, m_norm_mix, m_norm_ffn2, m_norm_final, m_dn_norm, _lane_row(m_a_log),
                          _lane_row(m_dt_bias))),
                    pack((v_norm_ffn1, v_norm_mix, v_norm_ffn2, v_norm_final, v_dn_norm, _lane_row(v_a_log),
                          _lane_row(v_dt_bias))),
                    SMALL_ROWS, "adam_small")
            elif n in TRANSPOSED:
                flip = lambda a: jnp.swapaxes(a, 1, 2)
                res[n] = [flip(o) for o in _adam(recv, flip(w[n]), flip(m[n]), flip(v[n]), F // N_DEV // 2,
                                                 "adam_" + n)]
            else:
                res[n] = _adam(recv, w[n], m[n], v[n], ADAM_TILE[n], "adam_" + n)
            after = res[n][0]
    res_s = res["small"]

    loss = res_s[0][0, 35, 0]
    outs = [loss, dx[None]]
    for k in range(4):
        n1, nm, n2, nf, dn, al, dt = _small_unpack(res_s[k])
        big = {n: res[n][k] for n in BIG}
        outs += [n1, big["gate1"], big["up1"], big["down1"], nm, big["w_in"], big["conv_w"], al, dt, dn,
                 big["w_out"], n2, big["gate2"], big["up2"], big["down2"], nf]
    return tuple(outs)
```

```python
import functools

import numpy as np
import jax
import jax.numpy as jnp
from jax import lax
from jax.experimental import pallas as pl
from jax.experimental.pallas import tpu as pltpu

T = 4096
D = 1024
F = 2816
N_DEV = 8
A_HEADS = 8
A_HD = 64
AW = A_HEADS * A_HD
DN_H = 4
DN_HD = 128
DNW = DN_H * DN_HD
CH = 64
PAIR = 2 * CH
ZA = 3 * AW
ZD = 3 * DNW + DNW + 256
ZP = ZA + ZD
BD_BLK = (3 * DNW + DNW) // 128
IN_COLS = 3592
PATTERNS = ((128, 1), (512, 4), (2048, 16))
NORM_EPS = 1e-6
L2_EPS = 1e-6
ADAM_LR, ADAM_B1, ADAM_B2, ADAM_EPS, ADAM_WD, ADAM_STEP = 0.001, 0.9, 0.999, 1e-08, 0.01, 10
VMEM_LIMIT = 56 * 1024 * 1024
NEG = -1e30

BF = jnp.bfloat16
F32 = jnp.float32
NN = (((1,), (0,)), ((), ()))
NT = (((1,), (1,)), ((), ()))
TN = (((0,), (0,)), ((), ()))
HI = lax.Precision.HIGHEST
MESH_ID = pl.DeviceIdType.MESH
ANY = pl.BlockSpec(memory_space=pl.ANY)


def _cp():
    return pltpu.CompilerParams(vmem_limit_bytes=VMEM_LIMIT)


def _dg(a, b, dims):
    return lax.dot_general(a, b, dims, preferred_element_type=F32)


def _hdot(a, b):
    return lax.dot_general(a, b, NN, precision=HI, preferred_element_type=F32)


def _make_bdot(dims, da_dims, da_swap, db_dims, db_swap):
    @jax.custom_vjp
    def f(a, b):
        return _dg(a.astype(BF), b.astype(BF), dims)

    def fwd(a, b):
        return f(a, b), (a, b)

    def bwd(res, g):
        a, b = res
        gb, ab, bb = g.astype(BF), a.astype(BF), b.astype(BF)
        da = _dg(bb, gb, da_dims) if da_swap else _dg(gb, bb, da_dims)
        db = _dg(gb, ab, db_dims) if db_swap else _dg(ab, gb, db_dims)
        return da, db

    f.defvjp(fwd, bwd)
    return f


_bdot_nn = _make_bdot(NN, NT, False, TN, False)
_bdot_nt = _make_bdot(NT, NN, False, TN, True)
_bdot_tn = _make_bdot(TN, NT, True, NN, False)


def _iota(shape, dim):
    return lax.broadcasted_iota(jnp.int32, shape, dim)


def _col(x, idx):
    return jnp.sum(jnp.where(_iota(x.shape, 1) == idx, x, 0.0), axis=1, keepdims=True)


def _mm_nn(a, b, out_dtype, tm, tn, name):
    m, k = a.shape
    n = b.shape[1]

    def body(a_ref, b_ref, o_ref):
        o_ref[...] = _dg(a_ref[...], b_ref[...], NN).astype(out_dtype)

    return pl.pallas_call(
        body, grid=(m // tm, n // tn),
        in_specs=[pl.BlockSpec((tm, k), lambda i, j: (i, 0)), pl.BlockSpec((k, tn), lambda i, j: (0, j))],
        out_specs=pl.BlockSpec((tm, tn), lambda i, j: (i, j)),
        out_shape=jax.ShapeDtypeStruct((m, n), out_dtype), name=name, compiler_params=_cp())(a, b)


def _tie(body, after):
    if after is None:
        return body, [], []
    return (lambda tok_ref, *refs: body(*refs)), [ANY], [after.reshape(1, 1)]


def _mm_nt(a, b, out_dtype, tm, tb, name, after=None):
    m, c = a.shape
    kb = b.shape[0]

    def body(a_ref, b_ref, o_ref):
        o_ref[...] = _dg(a_ref[...], b_ref[...], NT).astype(out_dtype)

    body, tspec, tok = _tie(body, after)
    return pl.pallas_call(
        body, grid=(m // tm, kb // tb),
        in_specs=tspec + [pl.BlockSpec((tm, c), lambda i, j: (i, 0)), pl.BlockSpec((tb, c), lambda i, j: (j, 0))],
        out_specs=pl.BlockSpec((tm, tb), lambda i, j: (i, j)),
        out_shape=jax.ShapeDtypeStruct((m, kb), out_dtype), name=name, compiler_params=_cp())(*tok, a, b)


def _mm_tn(a, b, out_dtype, ta, tb, name, after=None):
    m, ka = a.shape
    nb = b.shape[1]

    def body(a_ref, b_ref, o_ref):
        o_ref[...] = _dg(a_ref[...], b_ref[...], TN).astype(out_dtype)

    body, tspec, tok = _tie(body, after)
    return pl.pallas_call(
        body, grid=(ka // ta, nb // tb),
        in_specs=tspec + [pl.BlockSpec((m, ta), lambda i, j: (0, i)), pl.BlockSpec((m, tb), lambda i, j: (0, j))],
        out_specs=pl.BlockSpec((ta, tb), lambda i, j: (i, j)),
        out_shape=jax.ShapeDtypeStruct((ka, nb), out_dtype), name=name, compiler_params=_cp())(*tok, a, b)


def _mm_nn2(a1, a2, b1, b2, tm, tn, name, after=None):
    m, k = a1.shape
    n = b1.shape[1]

    def body(a1_ref, a2_ref, b1_ref, b2_ref, o_ref):
        o_ref[...] = _dg(a1_ref[...], b1_ref[...], NN) + _dg(a2_ref[...], b2_ref[...], NN)

    body, tspec, tok = _tie(body, after)
    arow = pl.BlockSpec((tm, k), lambda i, j: (i, 0))
    bcol = pl.BlockSpec((k, tn), lambda i, j: (0, j))
    return pl.pallas_call(
        body, grid=(m // tm, n // tn), in_specs=tspec + [arow, arow, bcol, bcol],
        out_specs=pl.BlockSpec((tm, tn), lambda i, j: (i, j)),
        out_shape=jax.ShapeDtypeStruct((m, n), F32), name=name, compiler_params=_cp())(*tok, a1, a2, b1, b2)


def _mm_nt_swiglu(h, wu_t, gate, tm, tb, name):
    m, c = h.shape
    kb = wu_t.shape[0]

    def body(h_ref, w_ref, g_ref, u_ref, a_ref):
        wv = w_ref[...]
        for r0 in range(0, tm, tm // 2):
            rs = slice(r0, r0 + tm // 2)
            u = _dg(h_ref[rs, :], wv, NT)
            g = g_ref[rs, :].astype(F32)
            u_ref[rs, :] = u.astype(BF)
            a_ref[rs, :] = (g * jax.nn.sigmoid(g) * u).astype(BF)

    tile = pl.BlockSpec((tm, tb), lambda i, j: (i, j))
    return pl.pallas_call(
        body, grid=(m // tm, kb // tb),
        in_specs=[pl.BlockSpec((tm, c), lambda i, j: (i, 0)), pl.BlockSpec((tb, c), lambda i, j: (j, 0)), tile],
        out_specs=[tile, tile], out_shape=[jax.ShapeDtypeStruct((m, kb), BF)] * 2,
        name=name, compiler_params=_cp())(h, wu_t, gate)


def _mm_nt_dswiglu(dys, wd, gate, up, tm, tb, name, after=None):
    m, c = dys.shape
    kb = wd.shape[0]

    def body(d_ref, w_ref, g_ref, u_ref, dg_ref, du_ref):
        wv = w_ref[...]
        for r0 in range(0, tm, tm // 2):
            rs = slice(r0, r0 + tm // 2)
            da = _dg(d_ref[rs, :], wv, NT)
            g = g_ref[rs, :].astype(F32)
            u = u_ref[rs, :].astype(F32)
            s = jax.nn.sigmoid(g)
            dg_ref[rs, :] = (da * u * (s * (1.0 + g * (1.0 - s)))).astype(BF)
            du_ref[rs, :] = (da * (g * s)).astype(BF)

    body, tspec, tok = _tie(body, after)
    tile = pl.BlockSpec((tm, tb), lambda i, j: (i, j))
    return pl.pallas_call(
        body, grid=(m // tm, kb // tb),
        in_specs=tspec + [pl.BlockSpec((tm, c), lambda i, j: (i, 0)), pl.BlockSpec((tb, c), lambda i, j: (j, 0)),
                          tile, tile],
        out_specs=[tile, tile], out_shape=[jax.ShapeDtypeStruct((m, kb), BF)] * 2,
        name=name, compiler_params=_cp())(*tok, dys, wd, gate, up)


def _rms_fwd(x, gain, name):
    tm = 512
    row = pl.BlockSpec((tm, D), lambda i: (i, 0))

    def body(x_ref, g_ref, h_ref):
        xv = x_ref[...]
        r = lax.rsqrt(jnp.mean(xv * xv, axis=-1, keepdims=True) + NORM_EPS)
        h_ref[...] = (xv * r * g_ref[...]).astype(BF)

    return pl.pallas_call(
        body, grid=(T // tm,), in_specs=[row, pl.BlockSpec((1, D), lambda i: (0, 0))], out_specs=row,
        out_shape=jax.ShapeDtypeStruct((T, D), BF), name=name, compiler_params=_cp())(x, gain)


def _rms_bwd(x, gain, dh, dres, alpha_out, name):
    tm = 512
    row = pl.BlockSpec((tm, D), lambda i: (i, 0))
    gspec = pl.BlockSpec((1, D), lambda i: (0, 0))

    def body(x_ref, g_ref, dh_ref, dres_ref, dx_ref, dxs_ref, dg_ref):
        i = pl.program_id(0)
        xv = x_ref[...]
        r = lax.rsqrt(jnp.mean(xv * xv, axis=-1, keepdims=True) + NORM_EPS)
        xh = xv * r
        dhv = dh_ref[...].astype(F32)
        part = jnp.sum(dhv * xh, axis=0, keepdims=True)

        @pl.when(i == 0)
        def _():
            dg_ref[...] = part

        @pl.when(i > 0)
        def _():
            dg_ref[...] += part

        dxh = dhv * g_ref[...]
        dx = r * (dxh - xh * jnp.mean(dxh * xh, axis=-1, keepdims=True)) + dres_ref[...]
        dx_ref[...] = dx
        dxs_ref[...] = (alpha_out * dx).astype(BF)

    return pl.pallas_call(
        body, grid=(T // tm,), in_specs=[row, gspec, row, row], out_specs=[row, row, gspec],
        out_shape=[jax.ShapeDtypeStruct((T, D), F32), jax.ShapeDtypeStruct((T, D), BF),
                   jax.ShapeDtypeStruct((1, D), F32)],
        name=name, compiler_params=_cp())(x, gain, dh, dres)


def _loss_bwd(x_prev, y, gain, target, name):
    tm = 512
    row = pl.BlockSpec((tm, D), lambda i: (i, 0))
    gspec = pl.BlockSpec((1, D), lambda i: (0, 0))
    lspec = pl.BlockSpec((8, 128), lambda i: (0, 0))

    def body(x_ref, y_ref, g_ref, t_ref, dx_ref, dxs_ref, dg_ref, loss_ref):
        i = pl.program_id(0)
        xv = x_ref[...] + 0.5 * y_ref[...]
        r = lax.rsqrt(jnp.mean(xv * xv, axis=-1, keepdims=True) + NORM_EPS)
        xh = xv * r
        diff = xh * g_ref[...] - t_ref[...]
        lpart = 0.5 * jnp.sum(jnp.mean(diff * diff, axis=-1, keepdims=True), axis=0, keepdims=True)
        dy = diff * (1.0 / D)
        part = jnp.sum(dy * xh, axis=0, keepdims=True)

        @pl.when(i == 0)
        def _():
            dg_ref[...] = part
            loss_ref[...] = jnp.broadcast_to(lpart, (8, 128))

        @pl.when(i > 0)
        def _():
            dg_ref[...] += part
            loss_ref[...] += jnp.broadcast_to(lpart, (8, 128))

        dxh = dy * g_ref[...]
        dx = r * (dxh - xh * jnp.mean(dxh * xh, axis=-1, keepdims=True))
        dx_ref[...] = dx
        dxs_ref[...] = (0.5 * dx).astype(BF)

    return pl.pallas_call(
        body, grid=(T // tm,), in_specs=[row, row, gspec, row], out_specs=[row, row, gspec, lspec],
        out_shape=[jax.ShapeDtypeStruct((T, D), F32), jax.ShapeDtypeStruct((T, D), BF),
                   jax.ShapeDtypeStruct((1, D), F32), jax.ShapeDtypeStruct((8, 128), F32)],
        name=name, compiler_params=_cp())(x_prev, y, gain, target)


def _mm_nn_resnorm(pairs, x_prev, alpha, gain, tm, name):
    m = x_prev.shape[0]
    n = len(pairs)

    def body(*refs):
        x_ref, g_ref, xo_ref, h_ref = refs[2 * n:]
        y = _dg(refs[0][...], refs[n][...], NN)
        for i in range(1, n):
            y = y + _dg(refs[i][...], refs[n + i][...], NN)
        xv = x_ref[...] + alpha * y
        xo_ref[...] = xv
        r = lax.rsqrt(jnp.mean(xv * xv, axis=-1, keepdims=True) + NORM_EPS)
        h_ref[...] = (xv * r * g_ref[...]).astype(BF)

    row = pl.BlockSpec((tm, D), lambda i: (i, 0))
    return pl.pallas_call(
        body, grid=(m // tm,),
        in_specs=[pl.BlockSpec((tm, a.shape[1]), lambda i: (i, 0)) for a, _ in pairs]
        + [pl.BlockSpec(b.shape, lambda i: (0, 0)) for _, b in pairs] + [row, pl.BlockSpec((1, D), lambda i: (0, 0))],
        out_specs=[row, row],
        out_shape=[jax.ShapeDtypeStruct((m, D), F32), jax.ShapeDtypeStruct((m, D), BF)],
        name=name, compiler_params=_cp())(*[a for a, _ in pairs], *[b for _, b in pairs], x_prev, gain)


def _ffn_up(h, need, tag):
    wg = need("g", h)
    gate = _mm_nt(h, wg, BF, 1024, 1408, tag + "_gate")
    wu = need("u", gate)
    up, act = _mm_nt_swiglu(h, wu, gate, 1024, 1408, tag + "_up_act")
    wd = need("d", up)
    return act, (h, gate, up, act, wg, wu, wd)


def _ffn_bwd(x_in, gain, saved, dxo, dys, alpha_out, emit, tag):
    h, gate, up, act, wg, wu, wd = saved
    sent = emit("d", _mm_tn(act, dys, BF, 256, 1024, tag + "_dwd"))
    dgate, dup = _mm_nt_dswiglu(dys, wd, gate, up, 1024, 1408, tag + "_dact", after=sent)
    sent = emit("g", _mm_tn(dgate, h, BF, 256, 1024, tag + "_dwg"))
    sent = emit("u", _mm_tn(dup, h, BF, 256, 1024, tag + "_dwu", after=sent))
    dh = _mm_nn2(dgate, dup, wg, wu, 1024, 512, tag + "_dh", after=sent)
    dx, dxs, dgain = _rms_bwd(x_in, gain, dh, dxo, alpha_out, tag + "_dnorm")
    return dx, dxs, dgain


SLAB = 2048
N_SLAB = T // SLAB
N_PAIR = A_HEADS // 2


def _pair_masks():
    lane = _iota((128, 128), 1)
    return lane < A_HD, lane >= A_HD


def _slope_table():
    h = 2 * jnp.arange(N_PAIR)[:, None] + jnp.minimum(jnp.arange(8), 1)[None, :]
    return jnp.broadcast_to((2.0 ** (-(h + 1).astype(F32)))[:, :, None], (N_PAIR, 8, 128))


def _rows(ref, start, d):
    if d == 1:
        return ref[pl.ds(start, 128), :]
    return ref[pl.ds(start, 128, stride=d), :]


def _put_rows(ref, start, d, val):
    if d == 1:
        ref[pl.ds(start, 128), :] = val
    else:
        ref[pl.ds(start, 128, stride=d), :] = val


def _units(d):
    return [(r, b, r + 128 * d * b) for r in range(d) for b in range(SLAB // (128 * d))]


def _band(d, prev_valid):
    qi = _iota((128, 256), 0)
    kj = _iota((128, 256), 1)
    steps = qi + 128 - kj
    valid = (steps >= 0) & (steps <= 128) & (prev_valid | (kj >= 128))
    return valid, (steps * d).astype(F32)


def _attn_fwd(z_at, name):
    def body(sl_ref, q_ref, kc_ref, kp_ref, vc_ref, vp_ref, of_ref, ob_ref, lse_ref, m_s, l_s, a_s):
        n = pl.program_id(1)
        lo, hi = _pair_masks()
        slopes = (sl_ref[0, 0:1, 0:1], sl_ref[0, 1:2, 0:1])

        def unit(d, start, b, first, carry):
            q = _rows(q_ref, start, d).astype(BF)
            kcur, vcur = _rows(kc_ref, start, d).astype(BF), _rows(vc_ref, start, d).astype(BF)
            if b > 0:
                (kprev, vprev), prev_valid = carry, True
            else:
                pstart = start + SLAB - 128 * d
                kprev, vprev = _rows(kp_ref, pstart, d).astype(BF), _rows(vp_ref, pstart, d).astype(BF)
                prev_valid = n > 0
            kcat = jnp.concatenate([kprev, kcur], axis=0)
            vcat = jnp.concatenate([vprev, vcur], axis=0)
            valid, dist = _band(d, prev_valid)
            ms, ls, pvs = [], [], []
            for e in range(2):
                qm = jnp.where(lo if e == 0 else hi, q, jnp.zeros_like(q))
                s = _dg(qm, kcat, NT) * (A_HD ** -0.5) - slopes[e] * dist
                s = jnp.where(valid, s, NEG)
                m = jnp.max(s, axis=1, keepdims=True)
                p = jnp.exp(s - m)
                ms.append(m)
                ls.append(jnp.sum(p, axis=1, keepdims=True))
                pvs.append(_dg(p.astype(BF), vcat, NN))
            m_u = jnp.where(lo, ms[0], ms[1])
            l_u = jnp.where(lo, ls[0], ls[1])
            a_u = jnp.where(lo, pvs[0], pvs[1])
            if first:
                m_n, l_n, a_n = m_u, l_u, a_u
            else:
                m_o = _rows(m_s, start, d)
                m_n = jnp.maximum(m_o, m_u)
                c_o = jnp.exp(m_o - m_n)
                c_u = jnp.exp(m_u - m_n)
                l_n = _rows(l_s, start, d) * c_o + l_u * c_u
                a_n = _rows(a_s, start, d) * c_o + a_u * c_u
            _put_rows(m_s, start, d, m_n)
            _put_rows(l_s, start, d, l_n)
            _put_rows(a_s, start, d, a_n)
            return kcur, vcur

        for pi, (_, d) in enumerate(PATTERNS):
            carry = None
            for r, b, start in _units(d):
                carry = unit(d, start, b, pi == 0, carry)
        l = l_s[...]
        out = a_s[...] / l
        of_ref[...] = out
        ob_ref[...] = out.astype(BF)
        lse_ref[...] = m_s[...] + jnp.log(l)

    cur = lambda c: pl.BlockSpec((SLAB, 128), lambda j, n: (n, c * N_PAIR + j))
    prv = lambda c: pl.BlockSpec((SLAB, 128), lambda j, n: (jnp.maximum(n - 1, 0), c * N_PAIR + j))
    out = pl.BlockSpec((SLAB, 128), lambda j, n: (n, j))
    return pl.pallas_call(
        body, grid=(N_PAIR, N_SLAB),
        in_specs=[pl.BlockSpec((1, 8, 128), lambda j, n: (j, 0, 0)), cur(0), cur(1), prv(1), cur(2), prv(2)],
        out_specs=[out, out, out],
        out_shape=[jax.ShapeDtypeStruct((T, AW), F32), jax.ShapeDtypeStruct((T, AW), BF),
                   jax.ShapeDtypeStruct((T, AW), F32)],
        scratch_shapes=[pltpu.VMEM((SLAB, 128), F32)] * 3,
        name=name, compiler_params=_cp())(_slope_table(), z_at, z_at, z_at, z_at, z_at)


def _attn_bwd(z_at, dout, out, lse, name):
    def body(sl_ref, q_ref, kc_ref, kp_ref, vc_ref, vp_ref, do_ref, o_ref, lse_ref, dq_ref, dk_ref, dv_ref,
             dq_s, dk_s, dv_s, ck_s, cv_s):
        step = pl.program_id(1)
        n = N_SLAB - 1 - step
        lo, hi = _pair_masks()
        slopes = (sl_ref[0, 0:1, 0:1], sl_ref[0, 1:2, 0:1])

        @pl.when(step == 0)
        def _():
            ck_s[...] = jnp.zeros_like(ck_s)
            cv_s[...] = jnp.zeros_like(cv_s)

        dk_s[...] = ck_s[...]
        dv_s[...] = cv_s[...]
        ck_s[...] = jnp.zeros_like(ck_s)
        cv_s[...] = jnp.zeros_like(cv_s)

        def add_rows(ref, start, d, val):
            _put_rows(ref, start, d, _rows(ref, start, d) + val)

        def unit(d, start, b, first, carry):
            q = _rows(q_ref, start, d).astype(BF)
            do_f = _rows(do_ref, start, d)
            do = do_f.astype(BF)
            prod = do_f * _rows(o_ref, start, d)
            lse_u = _rows(lse_ref, start, d)
            kcur, vcur = _rows(kc_ref, start, d).astype(BF), _rows(vc_ref, start, d).astype(BF)
            if b > 0:
                (kprev, vprev), prev_valid = carry, True
            else:
                pstart = start + SLAB - 128 * d
                kprev, vprev = _rows(kp_ref, pstart, d).astype(BF), _rows(vp_ref, pstart, d).astype(BF)
                prev_valid = n > 0
            kcat = jnp.concatenate([kprev, kcur], axis=0)
            vcat = jnp.concatenate([vprev, vcur], axis=0)
            valid, dist = _band(d, prev_valid)
            masks = (lo, hi)
            qms = [jnp.where(msk, q, jnp.zeros_like(q)) for msk in masks]
            doms = [jnp.where(msk, do, jnp.zeros_like(do)) for msk in masks]
            deltas = [jnp.sum(jnp.where(msk, prod, 0.0), axis=1, keepdims=True) for msk in masks]
            ss = [_dg(qm, kcat, NT) * (A_HD ** -0.5) - sl * dist for qm, sl in zip(qms, slopes)]
            dps = [_dg(dom, vcat, NT) for dom in doms]
            ps = [jnp.where(valid, jnp.exp(jnp.where(valid, s, NEG) - lse_u[:, 64 * e:64 * e + 1]), 0.0)
                  for e, s in enumerate(ss)]
            dss = [(p * (dp - delta)).astype(BF) for p, dp, delta in zip(ps, dps, deltas)]
            pbs = [p.astype(BF) for p in ps]
            dqs = [_dg(ds, kcat, NN) for ds in dss]
            dkc = (_dg(dss[0], qms[0], TN) + _dg(dss[1], qms[1], TN)) * (A_HD ** -0.5)
            dvc = _dg(pbs[0], doms[0], TN) + _dg(pbs[1], doms[1], TN)
            dq_u = jnp.where(lo, dqs[0], dqs[1]) * (A_HD ** -0.5)
            if first:
                _put_rows(dq_s, start, d, dq_u)
            else:
                add_rows(dq_s, start, d, dq_u)
            add_rows(dk_s, start, d, dkc[128:])
            add_rows(dv_s, start, d, dvc[128:])
            if b > 0:
                add_rows(dk_s, start - 128 * d, d, dkc[:128])
                add_rows(dv_s, start - 128 * d, d, dvc[:128])
            else:
                pstart = start + SLAB - 128 * d
                add_rows(ck_s, pstart, d, dkc[:128])
                add_rows(cv_s, pstart, d, dvc[:128])
            return kcur, vcur

        for pi, (_, d) in enumerate(PATTERNS):
            carry = None
            for r, b, start in _units(d):
                carry = unit(d, start, b, pi == 0, carry)
        dq_ref[...] = dq_s[...].astype(BF)
        dk_ref[...] = dk_s[...].astype(BF)
        dv_ref[...] = dv_s[...].astype(BF)

    rev = lambda n: N_SLAB - 1 - n
    cur = lambda c: pl.BlockSpec((SLAB, 128), lambda j, n: (rev(n), c * N_PAIR + j))
    prv = lambda c: pl.BlockSpec((SLAB, 128), lambda j, n: (jnp.maximum(rev(n) - 1, 0), c * N_PAIR + j))
    one = pl.BlockSpec((SLAB, 128), lambda j, n: (rev(n), j))
    return pl.pallas_call(
        body, grid=(N_PAIR, N_SLAB),
        in_specs=[pl.BlockSpec((1, 8, 128), lambda j, n: (j, 0, 0)), cur(0), cur(1), prv(1), cur(2), prv(2),
                  one, one, one],
        out_specs=[one, one, one], out_shape=[jax.ShapeDtypeStruct((T, AW), BF)] * 3,
        scratch_shapes=[pltpu.VMEM((SLAB, 128), F32)] * 5,
        name=name, compiler_params=_cp())(_slope_table(), z_at, z_at, z_at, z_at, z_at, dout, out, lse)


def _silu(x):
    return x * jax.nn.sigmoid(x)


def _qk_math(c):
    s = _silu(c)
    return s * lax.rsqrt(jnp.sum(s * s, axis=-1, keepdims=True) + L2_EPS)


def _softplus(x):
    return jnp.maximum(x, 0.0) + jnp.log(1.0 + jnp.exp(-jnp.abs(x)))


def _gate_math(bd, alog_row, dtb_row):
    rows = bd.shape[0]
    lane = _iota(bd.shape, 1)
    beta = jax.nn.sigmoid(bd)
    g = jnp.where((lane >= DN_H) & (lane < 2 * DN_H), -jnp.exp(alog_row) * _softplus(bd + dtb_row), 0.0)
    ri = _iota((rows, rows), 0)
    ci = _iota((rows, rows), 1)
    same = (ri // CH) == (ci // CH)
    li = _iota((128, 128), 0)
    lj = _iota((128, 128), 1)
    to_next_group = jnp.where((lj == li + DN_H) & (li >= DN_H) & (li < 2 * DN_H), 1.0, 0.0)
    gc = _hdot(jnp.where(same & (ci <= ri), 1.0, 0.0), g)
    glast = _hdot(_hdot(jnp.where(same, 1.0, 0.0), g), to_next_group)
    return jnp.where(lane < DN_H, beta, 0.0) + gc + glast


def _shift_down(cur, halo, s):
    if s == 0:
        return cur
    rolled = pltpu.roll(cur, s, 0)
    hr = pltpu.roll(halo, s, 0)
    head = jnp.where(_iota(hr.shape, 0) < s, hr, rolled[:8])
    return jnp.concatenate([head, rolled[8:]], axis=0)


def _shift_up(cur, halo, s):
    if s == 0:
        return cur
    rows = cur.shape[0]
    rolled = pltpu.roll(cur, rows - s, 0)
    hr = pltpu.roll(halo, 8 - s, 0)
    tail = jnp.where(_iota(hr.shape, 0) >= 8 - s, hr, rolled[rows - 8:])
    return jnp.concatenate([rolled[:rows - 8], tail], axis=0)


def _dn_pre_fwd(z_dn, conv_w8, alog_row, dtb_row, name):
    tm = 256
    wq = 3 * DNW

    def body(raw_ref, halo_ref, bd_ref, w_ref, al_ref, dt_ref, conv_ref, qkv_ref, bg_ref):
        i = pl.program_id(0)
        cur = raw_ref[...]
        halo = jnp.where(i > 0, halo_ref[...], 0.0)
        w = w_ref[...]
        conv = jnp.zeros((tm, wq), F32)
        for j in range(4):
            conv = conv + _shift_down(cur, halo, 3 - j) * w[j:j + 1, :]
        conv_ref[...] = conv
        for blk in range(3 * DN_H):
            sl = slice(128 * blk, 128 * blk + 128)
            c = conv[:, sl]
            qkv_ref[:, sl] = _qk_math(c) if blk < 2 * DN_H else _silu(c)
        bg_ref[...] = _gate_math(bd_ref[...], al_ref[...], dt_ref[...])

    one = pl.BlockSpec((1, 128), lambda i: (0, 0))
    return pl.pallas_call(
        body, grid=(T // tm,),
        in_specs=[pl.BlockSpec((tm, wq), lambda i: (i, 0)),
                  pl.BlockSpec((8, wq), lambda i: (jnp.maximum(i * (tm // 8) - 1, 0), 0)),
                  pl.BlockSpec((tm, 128), lambda i: (i, BD_BLK)),
                  pl.BlockSpec((8, wq), lambda i: (0, 0)), one, one],
        out_specs=[pl.BlockSpec((tm, wq), lambda i: (i, 0)), pl.BlockSpec((tm, wq), lambda i: (i, 0)),
                   pl.BlockSpec((tm, 128), lambda i: (i, 0))],
        out_shape=[jax.ShapeDtypeStruct((T, wq), F32), jax.ShapeDtypeStruct((T, wq), F32),
                   jax.ShapeDtypeStruct((T, 128), F32)],
        name=name, compiler_params=_cp())(z_dn, z_dn, z_dn, conv_w8, alog_row, dtb_row)


def _dn_pre_bwd(conv, z_dn, alog_row, dtb_row, dqn, dkn, dvn, dbg, name):
    tm = 256
    wq = 3 * DNW

    def body(conv_ref, bd_ref, al_ref, dt_ref, dq_ref, dk_ref, dv_ref, dbg_ref,
             dconv_ref, dbd_ref, dal_ref, ddt_ref):
        i = pl.program_id(0)
        for blk in range(3 * DN_H):
            sl = slice(128 * blk, 128 * blk + 128)
            src = (dq_ref, dk_ref, dv_ref)[blk // DN_H]
            ct = src[:, 128 * (blk % DN_H):128 * (blk % DN_H) + 128]
            fn = _qk_math if blk < 2 * DN_H else _silu
            _, vjp = jax.vjp(fn, conv_ref[:, sl])
            dconv_ref[:, sl] = vjp(ct)[0]
        _, vjp = jax.vjp(_gate_math, bd_ref[...], al_ref[...], dt_ref[...])
        dbd, dal, ddt = vjp(dbg_ref[...])
        dbd_ref[...] = dbd

        @pl.when(i == 0)
        def _():
            dal_ref[...] = dal
            ddt_ref[...] = ddt

        @pl.when(i > 0)
        def _():
            dal_ref[...] += dal
            ddt_ref[...] += ddt

    one = pl.BlockSpec((1, 128), lambda i: (0, 0))
    row = pl.BlockSpec((tm, wq), lambda i: (i, 0))
    hd = pl.BlockSpec((tm, DNW), lambda i: (i, 0))
    st = pl.BlockSpec((tm, 128), lambda i: (i, 0))
    return pl.pallas_call(
        body, grid=(T // tm,),
        in_specs=[row, pl.BlockSpec((tm, 128), lambda i: (i, BD_BLK)), one, one, hd, hd, hd, st],
        out_specs=[row, st, one, one],
        out_shape=[jax.ShapeDtypeStruct((T, wq), F32), jax.ShapeDtypeStruct((T, 128), F32),
                   jax.ShapeDtypeStruct((1, 128), F32), jax.ShapeDtypeStruct((1, 128), F32)],
        name=name, compiler_params=_cp())(conv, z_dn, alog_row, dtb_row, dqn, dkn, dvn, dbg)


def _dn_conv_bwd(dconv, z_dn, conv_w8, name):
    tm = 256
    wq = 3 * DNW
    last = T // tm - 1

    def body(dc_ref, dcn_ref, raw_ref, halo_ref, w_ref, draw_ref, dw_ref):
        i = pl.program_id(0)
        dc = dc_ref[...]
        nxt = jnp.where(i < last, dcn_ref[...], 0.0)
        cur = raw_ref[...]
        halo = jnp.where(i > 0, halo_ref[...], 0.0)
        w = w_ref[...]
        draw = jnp.zeros((tm, wq), F32)
        rows = []
        for j in range(4):
            draw = draw + _shift_up(dc, nxt, 3 - j) * w[j:j + 1, :]
            rows.append(jnp.sum(dc * _shift_down(cur, halo, 3 - j), axis=0, keepdims=True))
        draw_ref[...] = draw
        part = jnp.concatenate(rows + [jnp.zeros((4, wq), F32)], axis=0)

        @pl.when(i == 0)
        def _():
            dw_ref[...] = part

        @pl.when(i > 0)
        def _():
            dw_ref[...] += part

    row = pl.BlockSpec((tm, wq), lambda i: (i, 0))
    return pl.pallas_call(
        body, grid=(T // tm,),
        in_specs=[row, pl.BlockSpec((8, wq), lambda i: (jnp.minimum((i + 1) * (tm // 8), T // 8 - 1), 0)),
                  row, pl.BlockSpec((8, wq), lambda i: (jnp.maximum(i * (tm // 8) - 1, 0), 0)),
                  pl.BlockSpec((8, wq), lambda i: (0, 0))],
        out_specs=[row, pl.BlockSpec((8, wq), lambda i: (0, 0))],
        out_shape=[jax.ShapeDtypeStruct((T, wq), F32), jax.ShapeDtypeStruct((8, wq), F32)],
        name=name, compiler_params=_cp())(dconv, dconv, z_dn, z_dn, conv_w8)


def _h3(a, b, dims=NN):
    return lax.dot_general(a, b, dims, precision=lax.Precision.HIGH, preferred_element_type=F32)


@jax.custom_vjp
def _inverse_given(a_mat, tinv):
    return tinv


def _inverse_given_fwd(a_mat, tinv):
    return tinv, tinv


def _inverse_given_bwd(tinv, g):
    return -_h3(tinv, _h3(g, tinv, NT), TN), jnp.zeros_like(tinv)


_inverse_given.defvjp(_inverse_given_fwd, _inverse_given_bwd)


def _prep_head(q, k, v, bgc, h):
    beta = _col(bgc, h)
    gc = jnp.broadcast_to(_col(bgc, DN_H + h), (PAIR, 128))
    glast = jnp.broadcast_to(_col(bgc, 2 * DN_H + h), (PAIR, 128))
    ri = _iota((PAIR, PAIR), 0)
    ci = _iota((PAIR, PAIR), 1)
    same = (ri // CH) == (ci // CH)
    causal = same & (ci <= ri)
    strict = same & (ci < ri)
    eye = ri == ci
    gc_cols = _hdot(jnp.ones((PAIR, PAIR), F32), jnp.where(eye, gc, 0.0))
    decay = jnp.exp(jnp.where(causal, gc - gc_cols, NEG))
    egc = jnp.exp(gc)
    kb = k * beta
    a_mat = jnp.where(strict, _bdot_nt(kb, k) * decay, 0.0)
    qs = q * (DN_HD ** -0.5)
    attn = jnp.where(causal, _bdot_nt(qs, k) * decay, 0.0)
    return a_mat, (v * beta, kb * egc, qs * egc, k * jnp.exp(glast - gc), attn, jnp.exp(glast))


def _prep_tail(tinv, ctx):
    vb, kbe, qg, kdec, attn, decb = ctx
    return _h3(tinv, vb), _h3(tinv, kbe), qg, kdec, attn, decb


def _inverses(a_mats):
    eye = jnp.where(_iota((PAIR, PAIR), 0) == _iota((PAIR, PAIR), 1), 1.0, 0.0)
    ps = [-a for a in a_mats]
    tinvs = [eye + p for p in ps]
    for _ in range(5):
        ps = [_h3(p, p) for p in ps]
        tinvs = [t + _h3(t, p) for t, p in zip(tinvs, ps)]
    return tinvs


def _prep_math(q, k, v, bgc, h, tinv_saved):
    a_mat, ctx = _prep_head(q, k, v, bgc, h)
    return _prep_tail(_inverse_given(a_mat, tinv_saved), ctx)


def _dn_prep_fwd(qkv, bg, name):
    rows = 512
    hd = lambda off: pl.BlockSpec((rows, 128), lambda g, h: (g, off + h))
    out = pl.BlockSpec((rows, 128), lambda g, h: (g, h))

    def body(q_ref, k_ref, v_ref, bg_ref, *outs):
        h = pl.program_id(1)
        spans = [slice(PAIR * pr, PAIR * pr + PAIR) for pr in range(rows // PAIR)]
        heads = [_prep_head(q_ref[rs, :], k_ref[rs, :], v_ref[rs, :], bg_ref[rs, :], h) for rs in spans]
        tinvs = _inverses([a for a, _ in heads])
        for rs, tinv, (_, ctx) in zip(spans, tinvs, heads):
            for o_ref, val in zip(outs, _prep_tail(tinv, ctx) + (tinv,)):
                o_ref[rs, :] = val

    return pl.pallas_call(
        body, grid=(T // rows, DN_H),
        in_specs=[hd(0), hd(DN_H), hd(2 * DN_H), pl.BlockSpec((rows, 128), lambda g, h: (g, 0))],
        out_specs=[out] * 7, out_shape=[jax.ShapeDtypeStruct((T, DNW), F32)] * 7,
        name=name, compiler_params=_cp())(qkv, qkv, qkv, bg)


def _dn_prep_bwd(qkv, bg, tinv, cts, name):
    rows = 512
    hd = lambda off: pl.BlockSpec((rows, 128), lambda g, h: (g, off + h))
    out = pl.BlockSpec((rows, 128), lambda g, h: (g, h))
    st = pl.BlockSpec((rows, 128), lambda g, h: (g, 0))

    def body(q_ref, k_ref, v_ref, bg_ref, ti_ref, c0, c1, c2, c3, c4, c5, dq_ref, dk_ref, dv_ref, dbg_ref):
        h = pl.program_id(1)
        spans = [slice(PAIR * pr, PAIR * pr + PAIR) for pr in range(rows // PAIR)]
        tis = [ti_ref[rs, :] for rs in spans]

        def joint(qs, ks, vs, bs):
            heads = [_prep_head(q, k, v, b, h) for q, k, v, b in zip(qs, ks, vs, bs)]
            return [_prep_tail(_inverse_given(a, ti), ctx) for (a, ctx), ti in zip(heads, tis)]

        _, vjp = jax.vjp(joint, *[[r[rs, :] for rs in spans] for r in (q_ref, k_ref, v_ref, bg_ref)])
        dqs, dks, dvs, dbs = vjp([tuple(c[rs, :] for c in (c0, c1, c2, c3, c4, c5)) for rs in spans])
        for rs, dq, dk, dv in zip(spans, dqs, dks, dvs):
            dq_ref[rs, :] = dq
            dk_ref[rs, :] = dk
            dv_ref[rs, :] = dv
        dbg_all = jnp.concatenate(dbs, axis=0)

        @pl.when(h == 0)
        def _():
            dbg_ref[...] = dbg_all

        @pl.when(h > 0)
        def _():
            dbg_ref[...] += dbg_all

    return pl.pallas_call(
        body, grid=(T // rows, DN_H),
        in_specs=[hd(0), hd(DN_H), hd(2 * DN_H), st] + [out] * 7,
        out_specs=[out, out, out, st],
        out_shape=[jax.ShapeDtypeStruct((T, DNW), F32)] * 3 + [jax.ShapeDtypeStruct((T, 128), F32)],
        name=name, compiler_params=_cp())(qkv, qkv, qkv, bg, tinv, *cts)


def _step_math(s, u, w, qg, kdec, attn, decb, sub):
    vnew = u - _bdot_nn(w, s)
    z = jnp.zeros((CH, 128), F32)
    vfull = jnp.concatenate([vnew, z] if sub == 0 else [z, vnew], axis=0)
    o = _bdot_nn(qg, s) + _bdot_nn(attn, vfull)
    dec = jnp.sum(decb, axis=0, keepdims=True) * (1.0 / CH)
    return s * dec + _bdot_tn(kdec, vnew), o


def _dn_scan_fwd(prep, name):
    npair = T // PAIR
    row = pl.BlockSpec((PAIR, DNW), lambda p: (p, 0))

    def body(u_ref, w_ref, qg_ref, kd_ref, at_ref, db_ref, o_ref, ss_ref, s_ref):
        @pl.when(pl.program_id(0) == 0)
        def _():
            s_ref[...] = jnp.zeros_like(s_ref)

        states = [s_ref[h] for h in range(DN_H)]
        for sub in range(2):
            rs = slice(CH * sub, CH * sub + CH)
            for h in range(DN_H):
                ls = slice(128 * h, 128 * h + 128)
                ss_ref[sub, h] = states[h]
                states[h], o = _step_math(states[h], u_ref[rs, ls], w_ref[rs, ls], qg_ref[rs, ls],
                                          kd_ref[rs, ls], at_ref[rs, ls], db_ref[rs, ls], sub)
                o_ref[rs, ls] = o
        for h in range(DN_H):
            s_ref[h] = states[h]

    return pl.pallas_call(
        body, grid=(npair,), in_specs=[row] * 6,
        out_specs=[row, pl.BlockSpec((2, DN_H, 128, 128), lambda p: (p, 0, 0, 0))],
        out_shape=[jax.ShapeDtypeStruct((T, DNW), F32), jax.ShapeDtypeStruct((T // CH, DN_H, 128, 128), F32)],
        scratch_shapes=[pltpu.VMEM((DN_H, 128, 128), F32)],
        name=name, compiler_params=_cp())(*prep)


def _dn_scan_bwd(prep, states, do, name):
    npair = T // PAIR
    row = pl.BlockSpec((PAIR, DNW), lambda p: (npair - 1 - p, 0))

    def body(u_ref, w_ref, qg_ref, kd_ref, at_ref, db_ref, ss_ref, do_ref, *rest):
        outs, ds_ref = rest[:6], rest[6]

        @pl.when(pl.program_id(0) == 0)
        def _():
            ds_ref[...] = jnp.zeros_like(ds_ref)

        dss = [ds_ref[h] for h in range(DN_H)]
        for sub in (1, 0):
            rs = slice(CH * sub, CH * sub + CH)
            for h in range(DN_H):
                ls = slice(128 * h, 128 * h + 128)
                args = (ss_ref[sub, h],) + tuple(r[rs, ls] for r in (u_ref, w_ref, qg_ref, kd_ref, at_ref, db_ref))
                _, vjp = jax.vjp(functools.partial(_step_math, sub=sub), *args)
                cts = vjp((dss[h], do_ref[rs, ls]))
                dss[h] = cts[0]
                for o_ref, val in zip(outs, cts[1:]):
                    o_ref[rs, ls] = val
        for h in range(DN_H):
            ds_ref[h] = dss[h]

    return pl.pallas_call(
        body, grid=(npair,),
        in_specs=[row] * 6 + [pl.BlockSpec((2, DN_H, 128, 128), lambda p: (npair - 1 - p, 0, 0, 0)), row],
        out_specs=[row] * 6, out_shape=[jax.ShapeDtypeStruct((T, DNW), F32)] * 6,
        scratch_shapes=[pltpu.VMEM((DN_H, 128, 128), F32)],
        name=name, compiler_params=_cp())(*prep, states, do)


def _post_math(o, gate, wrow):
    return o * lax.rsqrt(jnp.mean(o * o, axis=-1, keepdims=True) + NORM_EPS) * wrow * _silu(gate)


def _dn_post_fwd(o, z_dn, dn_norm, name):
    tm = 512
    row = pl.BlockSpec((tm, DNW), lambda i: (i, 0))

    def body(o_ref, g_ref, w_ref, y_ref):
        for h in range(DN_H):
            ls = slice(128 * h, 128 * h + 128)
            y_ref[:, ls] = _post_math(o_ref[:, ls], g_ref[:, ls], w_ref[...]).astype(BF)

    return pl.pallas_call(
        body, grid=(T // tm,),
        in_specs=[row, pl.BlockSpec((tm, DNW), lambda i: (i, 3)), pl.BlockSpec((1, 128), lambda i: (0, 0))],
        out_specs=row, out_shape=jax.ShapeDtypeStruct((T, DNW), BF),
        name=name, compiler_params=_cp())(o, z_dn, dn_norm)


def _dn_post_bwd(o, z_dn, dn_norm, dy, name):
    tm = 512
    row = pl.BlockSpec((tm, DNW), lambda i: (i, 0))
    one = pl.BlockSpec((1, 128), lambda i: (0, 0))

    def body(o_ref, g_ref, w_ref, dy_ref, do_ref, dg_ref, dw_ref):
        i = pl.program_id(0)
        dw = jnp.zeros((1, 128), F32)
        for h in range(DN_H):
            ls = slice(128 * h, 128 * h + 128)
            _, vjp = jax.vjp(_post_math, o_ref[:, ls], g_ref[:, ls], w_ref[...])
            do, dg, dwh = vjp(dy_ref[:, ls].astype(F32))
            do_ref[:, ls] = do
            dg_ref[:, ls] = dg
            dw = dw + dwh

        @pl.when(i == 0)
        def _():
            dw_ref[...] = dw

        @pl.when(i > 0)
        def _():
            dw_ref[...] += dw

    return pl.pallas_call(
        body, grid=(T // tm,),
        in_specs=[row, pl.BlockSpec((tm, DNW), lambda i: (i, 3)), one, pl.BlockSpec((tm, DNW), lambda i: (i, 1))],
        out_specs=[row, row, one],
        out_shape=[jax.ShapeDtypeStruct((T, DNW), F32), jax.ShapeDtypeStruct((T, DNW), F32),
                   jax.ShapeDtypeStruct((1, 128), F32)],
        name=name, compiler_params=_cp())(o, z_dn, dn_norm, dy)


def _dz_dn_assemble(draw, dgate, dbd, name):
    tm = 512

    def body(a_ref, b_ref, c_ref, o_ref):
        o_ref[:, :3 * DNW] = a_ref[...].astype(BF)
        o_ref[:, 3 * DNW:4 * DNW] = b_ref[...].astype(BF)
        o_ref[:, 4 * DNW:4 * DNW + 128] = c_ref[...].astype(BF)
        o_ref[:, 4 * DNW + 128:] = jnp.zeros((tm, 128), BF)

    return pl.pallas_call(
        body, grid=(T // tm,),
        in_specs=[pl.BlockSpec((tm, 3 * DNW), lambda i: (i, 0)), pl.BlockSpec((tm, DNW), lambda i: (i, 0)),
                  pl.BlockSpec((tm, 128), lambda i: (i, 0))],
        out_specs=pl.BlockSpec((tm, ZD), lambda i: (i, 0)),
        out_shape=jax.ShapeDtypeStruct((T, ZD), BF), name=name, compiler_params=_cp())(draw, dgate, dbd)


HBM = pl.BlockSpec(memory_space=pltpu.HBM)
SEM = pl.BlockSpec(memory_space=pltpu.SEMAPHORE)
EFFECT = pltpu.SideEffectType.DATAFLOW_SIDE_EFFECTING
N_PEER = N_DEV - 1


ALL_PEERS = (1, 2, 4, 3, 5, 6, 7)
FIRST_HOP = (1, 2, 4, 6)
FORWARDED = (2, 4, 6)


def _peers(x, y, c, ks=ALL_PEERS):
    return [(k, (x ^ (k >> 2), y ^ ((k >> 1) & 1), c ^ (k & 1))) for k in ks]


def _exchange_copy(ins, lands, ssems, rsems, scatter, t, k, pos, me):
    px, py, pc = pos
    src = ins[t].at[4 * px + 2 * py + pc] if scatter else ins[t]
    return pltpu.make_async_remote_copy(
        src_ref=src, dst_ref=lands[t].at[me], send_sem=ssems[t].at[k - 1], recv_sem=rsems[t].at[k - 1],
        device_id=pos, device_id_type=MESH_ID)


def _xstart(bufs, scatter, name, ks=ALL_PEERS):
    nt = len(bufs)
    lands = [lax.empty((N_DEV,) + tuple(b.shape[1:] if scatter else b.shape), b.dtype) for b in bufs]

    def body(*refs):
        ins, lnd = refs[:nt], refs[nt:2 * nt]
        ssems, rsems = refs[2 * nt:3 * nt], refs[3 * nt:4 * nt]
        token = refs[-1]
        x, y, c = lax.axis_index("x"), lax.axis_index("y"), lax.axis_index("c")
        me = 4 * x + 2 * y + c
        for t in range(nt):
            for k, pos in _peers(x, y, c, ks):
                _exchange_copy(ins, lnd, ssems, rsems, scatter, t, k, pos, me).start()
        token[...] = jnp.zeros_like(token)

    both = list(bufs) + lands
    res = pl.pallas_call(
        body, name=name,
        out_shape=[pltpu.SemaphoreType.DMA((N_PEER,))] * (2 * nt)
        + [pltpu.HBM(b.shape, b.dtype) for b in both] + [jax.ShapeDtypeStruct((8, 128), F32)],
        in_specs=[HBM] * (2 * nt),
        out_specs=[SEM] * (2 * nt) + [HBM] * (2 * nt) + [pl.BlockSpec(memory_space=pltpu.VMEM)],
        input_output_aliases={i: 2 * nt + i for i in range(2 * nt)},
        compiler_params=pltpu.CompilerParams(has_side_effects=EFFECT),
    )(*[pltpu.with_memory_space_constraint(b, pltpu.HBM) for b in both])
    return res[:nt], res[nt:2 * nt], res[2 * nt:3 * nt], res[3 * nt:4 * nt], res[-1][0, 0]


def _xwait(ssems, rsems, thrus, lands, scatter, after, name, ks=ALL_PEERS):
    nt = len(lands)

    def body(*refs):
        ins, lnd = refs[:nt], refs[nt:2 * nt]
        ss, rs = refs[2 * nt:3 * nt], refs[3 * nt:4 * nt]
        x, y, c = lax.axis_index("x"), lax.axis_index("y"), lax.axis_index("c")
        me = 4 * x + 2 * y + c
        for t in range(nt):
            for k, pos in _peers(x, y, c, ks):
                cp = _exchange_copy(ins, lnd, ss, rs, scatter, t, k, pos, me)
                cp.wait_send()
                cp.wait_recv()

    both = list(thrus) + list(lands)
    res = pl.pallas_call(
        body, name=name, out_shape=[pltpu.HBM(b.shape, b.dtype) for b in both],
        in_specs=[HBM] * (2 * nt) + [SEM] * (2 * nt) + [ANY], out_specs=[HBM] * (2 * nt),
        input_output_aliases={i: i for i in range(2 * nt)},
        compiler_params=pltpu.CompilerParams(has_side_effects=EFFECT),
    )(*both, *ssems, *rsems, after)
    return res[:nt], res[nt:]


def _forward_copy(lands, ssems, rsems, t, k, pos, sibling):
    px, py, pc = pos
    slot = lands[t].at[4 * px + 2 * py + pc]
    return pltpu.make_async_remote_copy(
        src_ref=slot, dst_ref=slot, send_sem=ssems[t].at[k - 1], recv_sem=rsems[t].at[k - 1],
        device_id=sibling, device_id_type=MESH_ID)


def _fstart(lands, name):
    nt = len(lands)

    def body(*refs):
        lnd = refs[:nt]
        ssems, rsems = refs[nt:2 * nt], refs[2 * nt:3 * nt]
        token = refs[-1]
        x, y, c = lax.axis_index("x"), lax.axis_index("y"), lax.axis_index("c")
        for t in range(nt):
            for k, pos in _peers(x, y, c, FORWARDED):
                _forward_copy(lnd, ssems, rsems, t, k, pos, (x, y, c ^ 1)).start()
        token[...] = jnp.zeros_like(token)

    res = pl.pallas_call(
        body, name=name,
        out_shape=[pltpu.SemaphoreType.DMA((N_PEER,))] * (2 * nt)
        + [pltpu.HBM(b.shape, b.dtype) for b in lands] + [jax.ShapeDtypeStruct((8, 128), F32)],
        in_specs=[HBM] * nt,
        out_specs=[SEM] * (2 * nt) + [HBM] * nt + [pl.BlockSpec(memory_space=pltpu.VMEM)],
        input_output_aliases={i: 2 * nt + i for i in range(nt)},
        compiler_params=pltpu.CompilerParams(has_side_effects=EFFECT),
    )(*[pltpu.with_memory_space_constraint(b, pltpu.HBM) for b in lands])
    return res[:nt], res[nt:2 * nt], res[2 * nt:3 * nt], res[-1][0, 0]


def _fwait(ssems, rsems, lands, after, name):
    nt = len(lands)

    def body(*refs):
        lnd = refs[:nt]
        ss, rs = refs[nt:2 * nt], refs[2 * nt:3 * nt]
        x, y, c = lax.axis_index("x"), lax.axis_index("y"), lax.axis_index("c")
        for t in range(nt):
            for k, pos in _peers(x, y, c, FORWARDED):
                cp = _forward_copy(lnd, ss, rs, t, k, pos, (x, y, c ^ 1))
                cp.wait_send()
                cp.wait_recv()

    return pl.pallas_call(
        body, name=name, out_shape=[pltpu.HBM(b.shape, b.dtype) for b in lands],
        in_specs=[HBM] * nt + [SEM] * (2 * nt) + [ANY], out_specs=[HBM] * nt,
        input_output_aliases={i: i for i in range(nt)},
        compiler_params=pltpu.CompilerParams(has_side_effects=EFFECT),
    )(*lands, *ssems, *rsems, after)


def _adam(recv, w, m, v, tr, name):
    _, r, c = w.shape
    n_part = recv.shape[0]
    c1 = np.float32(1.0 - ADAM_B1 ** ADAM_STEP)
    c2 = np.float32(1.0 - ADAM_B2 ** ADAM_STEP)

    def body(r_ref, w_ref, m_ref, v_ref, g_ref, d_ref, mo_ref, vo_ref):
        g = r_ref[0].astype(F32)
        for s in range(1, n_part):
            g = g + r_ref[s].astype(F32)
        mn = ADAM_B1 * m_ref[0] + (1.0 - ADAM_B1) * g
        vn = ADAM_B2 * v_ref[0] + (1.0 - ADAM_B2) * (g * g)
        g_ref[0] = g
        mo_ref[0] = mn
        vo_ref[0] = vn
        d_ref[0] = -ADAM_LR * ((mn / c1) / (jnp.sqrt(vn / c2) + ADAM_EPS) + ADAM_WD * w_ref[0])

    one = pl.BlockSpec((1, tr, c), lambda i: (0, i, 0))
    return pl.pallas_call(
        body, grid=(r // tr,), in_specs=[pl.BlockSpec((n_part, tr, c), lambda i: (0, i, 0)), one, one, one],
        out_specs=[one] * 4, out_shape=[jax.ShapeDtypeStruct((1, r, c), F32)] * 4,
        name=name, compiler_params=_cp())(recv, w, m, v)


def _local_step(x, target, sp, need, emit):
    g = {}
    x0, h1 = x, _rms_fwd(x, sp["norm_ffn1"], "ffn1_norm")
    act1, saved1 = _ffn_up(h1, lambda kind, a: need("w" + kind + "1", a), "ffn1")
    x1, h2 = _mm_nn_resnorm([(act1, saved1[-1])], x0, 0.5, sp["norm_mix"], 512, "ffn1_down_norm")
    win_a, win_d = need("win_a", h2), need("win_d", h2)
    conv_w8, wout = need("conv_w8", h2), need("wout", h2)
    z_at = _mm_nn(h2, win_a, F32, 1024, 768, "mix_in_attn")
    z_dn = _mm_nn(h2, win_d, F32, 1024, 768, "mix_in_dn")
    attn_f, attn_b, lse = _attn_fwd(z_at, "attn_fwd")

    conv, qkvn, bg = _dn_pre_fwd(z_dn, conv_w8, sp["alog_row"], sp["dtb_row"], "dn_pre")
    *prep, tinv = _dn_prep_fwd(qkvn, bg, "dn_prep")
    o_dn, states = _dn_scan_fwd(prep, "dn_scan")
    dn_b = _dn_post_fwd(o_dn, z_dn, sp["dn_norm"], "dn_post")

    x2, h3 = _mm_nn_resnorm([(attn_b, wout[:AW]), (dn_b, wout[AW:])], x1, 1.0, sp["norm_ffn2"], 512,
                            "mix_out_norm")
    act2, saved2 = _ffn_up(h3, lambda kind, a: need("w" + kind + "2", a), "ffn2")
    y3 = _mm_nn(act2, saved2[-1], F32, 1024, 1024, "ffn2_down")

    dx3, dys3, g["norm_final"], loss8 = _loss_bwd(x2, y3, sp["norm_final"], target, "loss")
    dx2, dx2b, g["norm_ffn2"] = _ffn_bwd(
        x2, sp["norm_ffn2"], saved2, dx3, dys3, 1.0,
        lambda kind, dw: emit(kind + "2", {"w" + kind + "2": dw}), "ffn2b")

    zero = emit("wout", {"wout": jnp.concatenate([_mm_tn(attn_b, dx2b, BF, 512, 1024, "mix_out_dw_a"),
                                                  _mm_tn(dn_b, dx2b, BF, 512, 1024, "mix_out_dw_d")], axis=0)})
    dmix = _mm_nt(dx2b, wout, F32, 1024, 1024, "mix_out_dx", after=zero)

    dz_at = jnp.concatenate(_attn_bwd(z_at, dmix, attn_f, lse, "attn_bwd"), axis=1)

    do_dn, dgate, g["dn_norm"] = _dn_post_bwd(o_dn, z_dn, sp["dn_norm"], dmix, "dn_post_b")
    cts = _dn_scan_bwd(prep, states, do_dn, "dn_scan_b")
    dqn, dkn, dvn, dbg = _dn_prep_bwd(qkvn, bg, tinv, cts, "dn_prep_b")
    dconv, dbd, g["alog_row"], g["dtb_row"] = _dn_pre_bwd(
        conv, z_dn, sp["alog_row"], sp["dtb_row"], dqn, dkn, dvn, dbg, "dn_pre_b")
    draw, dconv_w8 = _dn_conv_bwd(dconv, z_dn, conv_w8, "dn_conv_b")
    dz_dn = _dz_dn_assemble(draw, dgate, dbd, "dn_dz")

    zero = emit("win", {"win_a": _mm_tn(h2, dz_at, BF, 512, 768, "mix_in_dw_a"),
                        "win_d": _mm_tn(h2, dz_dn, BF, 512, 768, "mix_in_dw_d"), "conv_w8": dconv_w8})
    dh2 = _mm_nt(dz_at, win_a, F32, 1024, 1024, "mix_in_dx_a", after=zero)
    dh2d = _mm_nt(dz_dn, win_d, F32, 1024, 1024, "mix_in_dx_d")
    (dx1, dys1), _, g["norm_mix"] = _rms_bwd2(x1, sp["norm_mix"], dh2, dh2d, dx2, "mix_dnorm")
    dx0, _, g["norm_ffn1"] = _ffn_bwd(
        x0, sp["norm_ffn1"], saved1, dx1, dys1, 1.0,
        lambda kind, dw: emit(kind + "1", {"w" + kind + "1": dw}), "ffn1b")
    return loss8[0, 0], dx0, g


def _rms_bwd2(x, gain, dh_a, dh_b, dres, name):
    tm = 512
    row = pl.BlockSpec((tm, D), lambda i: (i, 0))
    gspec = pl.BlockSpec((1, D), lambda i: (0, 0))

    def body(x_ref, g_ref, da_ref, db_ref, dres_ref, dx_ref, dxs_ref, dg_ref):
        i = pl.program_id(0)
        xv = x_ref[...]
        r = lax.rsqrt(jnp.mean(xv * xv, axis=-1, keepdims=True) + NORM_EPS)
        xh = xv * r
        dhv = da_ref[...] + db_ref[...]
        part = jnp.sum(dhv * xh, axis=0, keepdims=True)

        @pl.when(i == 0)
        def _():
            dg_ref[...] = part

        @pl.when(i > 0)
        def _():
            dg_ref[...] += part

        dxh = dhv * g_ref[...]
        dx = r * (dxh - xh * jnp.mean(dxh * xh, axis=-1, keepdims=True)) + dres_ref[...]
        dx_ref[...] = dx
        dxs_ref[...] = (0.5 * dx).astype(BF)

    dx, dxs, dg = pl.pallas_call(
        body, grid=(T // tm,), in_specs=[row, gspec, row, row, row], out_specs=[row, row, gspec],
        out_shape=[jax.ShapeDtypeStruct((T, D), F32), jax.ShapeDtypeStruct((T, D), BF),
                   jax.ShapeDtypeStruct((1, D), F32)],
        name=name, compiler_params=_cp())(x, gain, dh_a, dh_b, dres)
    return (dx, dxs), None, dg


def _cols_from_shards(gathered):
    n, r, c = gathered.shape
    return jnp.transpose(gathered, (1, 0, 2)).reshape(r, n * c)


def _shards_from_cols(full, dtype):
    r, nc = full.shape
    return jnp.transpose(full.reshape(r, N_DEV, nc // N_DEV), (1, 0, 2)).astype(dtype)


def _lane_row(vec4):
    return jnp.zeros((1, 128), F32).at[:, DN_H:2 * DN_H].set(vec4.astype(F32))


WEIGHT_SOURCES = {"wg1": "gate1", "wu1": "up1", "wd1": "down1", "win_a": "w_in", "win_d": "w_in",
                  "conv_w8": "conv_w", "wout": "w_out", "wg2": "gate2", "wu2": "up2", "wd2": "down2"}
TRANSPOSED = ("gate1", "up1", "gate2", "up2")


def _build_weights(name, gath):
    if name in ("wg1", "wu1", "wd1", "wg2", "wu2", "wd2"):
        return {name: gath[WEIGHT_SOURCES[name]].reshape(F, D)}
    if name in ("win_a", "win_d"):
        w_in = _cols_from_shards(gath["w_in"])
        c0 = 3 * AW + 3 * DNW
        wp = jnp.concatenate([w_in[:, :c0], w_in[:, c0 + 2 * DN_H:], w_in[:, c0:c0 + 2 * DN_H],
                              jnp.zeros((D, ZP - IN_COLS), w_in.dtype)], axis=1)
        return {"win_a": wp[:, :ZA], "win_d": wp[:, ZA:]}
    if name == "wout":
        return {name: gath["w_out"].reshape(D, D)}
    conv = _cols_from_shards(gath["conv_w"])
    return {"conv_w8": jnp.concatenate([conv, jnp.zeros((4, 3 * DNW), F32)], axis=0)}


def _small_params(norm_ffn1, norm_mix, norm_ffn2, norm_final, a_log, dt_bias, dn_norm):
    return {"norm_ffn1": norm_ffn1, "norm_mix": norm_mix, "norm_ffn2": norm_ffn2,
            "norm_final": norm_final.reshape(1, D), "alog_row": _lane_row(a_log), "dtb_row": _lane_row(dt_bias),
            "dn_norm": dn_norm}


def _grad_slabs(group, g):
    if group[0] in "gud":
        return {WEIGHT_SOURCES["w" + group]: g["w" + group].reshape(N_DEV, F // N_DEV, D)}
    if group == "wout":
        return {"w_out": g["wout"].reshape(N_DEV, D // N_DEV, D)}
    gp = jnp.concatenate([g["win_a"], g["win_d"]], axis=1)
    c0 = 3 * AW + 3 * DNW
    g_in = jnp.concatenate([gp[:, :c0], gp[:, c0 + DNW:c0 + DNW + 2 * DN_H], gp[:, c0:c0 + DNW]], axis=1)
    return {"w_in": _shards_from_cols(g_in, BF), "conv_w": _shards_from_cols(g["conv_w8"][:4], F32)}


SMALL_ROWS = 40


def _small_pack(norm_ffn1, norm_mix, norm_ffn2, norm_final, dn_norm, alog_row, dtb_row, loss=None):
    rows = [a.reshape(8, 128) for a in (norm_ffn1, norm_mix, norm_ffn2, norm_final)]
    loss_row = jnp.zeros((1, 128), F32) if loss is None else jnp.broadcast_to(loss.reshape(1, 1), (1, 128))
    rows += [dn_norm.reshape(1, 128), alog_row, dtb_row, loss_row, jnp.zeros((SMALL_ROWS - 36, 128), F32)]
    return jnp.concatenate(rows, axis=0)


def _small_unpack(pk):
    pk = pk[0]
    return (pk[0:8].reshape(1, D), pk[8:16].reshape(1, D), pk[16:24].reshape(1, D), pk[24:32].reshape(D),
            pk[32:33], pk[33:34, DN_H:2 * DN_H], pk[34:35, DN_H:2 * DN_H])


ADAM_TILE = {"gate1": 256, "up1": 256, "down1": 176, "gate2": 256, "up2": 256, "down2": 176,
             "w_in": 256, "w_out": 128, "conv_w": 4}
BIG = ("gate1", "up1", "down1", "w_in", "w_out", "gate2", "up2", "down2", "conv_w")


def kernel(x, norm_ffn1, ffn1_gate, ffn1_up, ffn1_down, norm_mix, w_in, conv_w, a_log, dt_bias, dn_norm, w_out, norm_ffn2, ffn2_gate, ffn2_up, ffn2_down, norm_final, loss_target, m_norm_ffn1, m_ffn1_gate, m_ffn1_up, m_ffn1_down, m_norm_mix, m_w_in, m_conv_w, m_a_log, m_dt_bias, m_dn_norm, m_w_out, m_norm_ffn2, m_ffn2_gate, m_ffn2_up, m_ffn2_down, m_norm_final, v_norm_ffn1, v_ffn1_gate, v_ffn1_up, v_ffn1_down, v_norm_mix, v_w_in, v_conv_w, v_a_log, v_dt_bias, v_dn_norm, v_w_out, v_norm_ffn2, v_ffn2_gate, v_ffn2_up, v_ffn2_down, v_norm_final):
    w = {"gate1": ffn1_gate, "up1": ffn1_up, "down1": ffn1_down, "w_in": w_in, "w_out": w_out,
         "gate2": ffn2_gate, "up2": ffn2_up, "down2": ffn2_down, "conv_w": conv_w}
    m = {"gate1": m_ffn1_gate, "up1": m_ffn1_up, "down1": m_ffn1_down, "w_in": m_w_in, "w_out": m_w_out,
         "gate2": m_ffn2_gate, "up2": m_ffn2_up, "down2": m_ffn2_down, "conv_w": m_conv_w}
    v = {"gate1": v_ffn1_gate, "up1": v_ffn1_up, "down1": v_ffn1_down, "w_in": v_w_in, "w_out": v_w_out,
         "gate2": v_ffn2_gate, "up2": v_ffn2_up, "down2": v_ffn2_down, "conv_w": v_conv_w}

    me = 4 * lax.axis_index("x") + 2 * lax.axis_index("y") + lax.axis_index("c")
    own_slot = lambda land, mine: lax.dynamic_update_index_in_dim(land, mine, me, 0)

    ag_order = ("gate1", "up1", "down1", "w_in", "conv_w", "w_out", "gate2", "up2", "down2")
    ag_groups = (("gate1",), ("up1",), ("down1",), ("w_in", "conv_w", "w_out"), ("gate2", "up2", "down2"))
    pos = {n: i for i, n in enumerate(ag_order)}

    def shard(n):
        if n == "conv_w":
            return w[n][0]
        return (w[n][0].T if n in TRANSPOSED else w[n][0]).astype(BF)

    ss, rs, thru, land, zero = _xstart([shard(n) for n in ag_order], False, "weights_start", FIRST_HOP)
    gath, built = {}, {}

    def need(name, after):
        if name not in built:
            src = WEIGHT_SOURCES[name]
            if src not in gath:
                gi = [i for i, grp in enumerate(ag_groups) if src in grp][0]
                ids = [pos[n] for n in ag_groups[gi]]
                thrus, lands = _xwait([ss[i] for i in ids], [rs[i] for i in ids], [thru[i] for i in ids],
                                      [land[i] for i in ids], False, after, "weights_wait%d" % gi, FIRST_HOP)
                fss, frs, lands, _ = _fstart(lands, "weights_forward%d" % gi)
                lands = _fwait(fss, frs, lands, after, "weights_forward_wait%d" % gi)
                for n, t, l in zip(ag_groups[gi], thrus, lands):
                    gath[n] = own_slot(l, t)
            built.update(_build_weights(name, gath))
        return built[name]

    pending = []

    def emit(group, grads):
        slabs = grads if group == "small" else _grad_slabs(group, grads)
        names = list(slabs)
        started = _xstart([slabs[n] for n in names], True, "grads_start_" + group)
        pending.append((group, names) + started[:4])
        return started[4]

    sp = _small_params(norm_ffn1 + zero, norm_mix, norm_ffn2, norm_final, a_log, dt_bias, dn_norm)
    loss_part, dx, g = _local_step(x[0], loss_target[0], sp, need, emit)
    small = _small_pack(g["norm_ffn1"], g["norm_mix"], g["norm_ffn2"], g["norm_final"], g["dn_norm"],
                        g["alog_row"], g["dtb_row"], loss_part)
    emit("small", {"small": jnp.broadcast_to(small[None], (N_DEV, SMALL_ROWS, 128))})

    pack = lambda a: _small_pack(*a)[None]
    res, after = {}, dx
    for group, names, gss, grs, gthru, gland in pending:
        thrus, lands = _xwait(gss, grs, gthru, gland, True, after, "grads_wait_" + group)
        for n, t, l in zip(names, thrus, lands):
            recv = own_slot(l, lax.dynamic_index_in_dim(t, me, 0, keepdims=False))
            if n == "small":
                res[n] = _adam(
                    recv,
                    pack((norm_ffn1, norm_mix, norm_ffn2, norm_final, dn_norm, _lane_row(a_log), _lane_row(dt_bias))),
                    pack((m_norm_ffn1, m_norm_mix, m_norm_ffn2, m_norm_final, m_dn_norm, _lane_row(m_a_log),
                          _lane_row(m_dt_bias))),
                    pack((v_norm_ffn1, v_norm_mix, v_norm_ffn2, v_norm_final, v_dn_norm, _lane_row(v_a_log),
                          _lane_row(v_dt_bias))),
                    SMALL_ROWS, "adam_small")
            elif n in TRANSPOSED:
                flip = lambda a: jnp.swapaxes(a, 1, 2)
                res[n] = [flip(o) for o in _adam(recv, flip(w[n]), flip(m[n]), flip(v[n]), F // N_DEV // 2,
                                                 "adam_" + n)]
            else:
                res[n] = _adam(recv, w[n], m[n], v[n], ADAM_TILE[n], "adam_" + n)
            after = res[n][0]
    res_s = res["small"]

    loss = res_s[0][0, 35, 0]
    outs = [loss, dx[None]]
    for k in range(4):
        n1, nm, n2, nf, dn, al, dt = _small_unpack(res_s[k])
        big = {n: res[n][k] for n in BIG}
        outs += [n1, big["gate1"], big["up1"], big["down1"], nm, big["w_in"], big["conv_w"], al, dt, dn,
                 big["w_out"], n2, big["gate2"], big["up2"], big["down2"], nf]
    return tuple(outs)
```

```python
import functools

import numpy as np
import jax
import jax.numpy as jnp
from jax import lax
from jax.experimental import pallas as pl
from jax.experimental.pallas import tpu as pltpu

T = 4096
D = 1024
F = 2816
N_DEV = 8
A_HEADS = 8
A_HD = 64
AW = A_HEADS * A_HD
DN_H = 4
DN_HD = 128
DNW = DN_H * DN_HD
CH = 64
PAIR = 2 * CH
ZA = 3 * AW
ZD = 3 * DNW + DNW + 256
ZP = ZA + ZD
BD_BLK = (3 * DNW + DNW) // 128
IN_COLS = 3592
PATTERNS = ((128, 1), (512, 4), (2048, 16))
NORM_EPS = 1e-6
L2_EPS = 1e-6
ADAM_LR, ADAM_B1, ADAM_B2, ADAM_EPS, ADAM_WD, ADAM_STEP = 0.001, 0.9, 0.999, 1e-08, 0.01, 10
VMEM_LIMIT = 56 * 1024 * 1024
NEG = -1e30

BF = jnp.bfloat16
F32 = jnp.float32
NN = (((1,), (0,)), ((), ()))
NT = (((1,), (1,)), ((), ()))
TN = (((0,), (0,)), ((), ()))
HI = lax.Precision.HIGHEST
MESH_ID = pl.DeviceIdType.MESH
ANY = pl.BlockSpec(memory_space=pl.ANY)


def _cp():
    return pltpu.CompilerParams(vmem_limit_bytes=VMEM_LIMIT)


def _dg(a, b, dims):
    return lax.dot_general(a, b, dims, preferred_element_type=F32)


def _hdot(a, b):
    return lax.dot_general(a, b, NN, precision=HI, preferred_element_type=F32)


def _make_bdot(dims, da_dims, da_swap, db_dims, db_swap):
    @jax.custom_vjp
    def f(a, b):
        return _dg(a.astype(BF), b.astype(BF), dims)

    def fwd(a, b):
        return f(a, b), (a, b)

    def bwd(res, g):
        a, b = res
        gb, ab, bb = g.astype(BF), a.astype(BF), b.astype(BF)
        da = _dg(bb, gb, da_dims) if da_swap else _dg(gb, bb, da_dims)
        db = _dg(gb, ab, db_dims) if db_swap else _dg(ab, gb, db_dims)
        return da, db

    f.defvjp(fwd, bwd)
    return f


_bdot_nn = _make_bdot(NN, NT, False, TN, False)
_bdot_nt = _make_bdot(NT, NN, False, TN, True)
_bdot_tn = _make_bdot(TN, NT, True, NN, False)


def _iota(shape, dim):
    return lax.broadcasted_iota(jnp.int32, shape, dim)


def _col(x, idx):
    return jnp.sum(jnp.where(_iota(x.shape, 1) == idx, x, 0.0), axis=1, keepdims=True)


def _mm_nn(a, b, out_dtype, tm, tn, name):
    m, k = a.shape
    n = b.shape[1]

    def body(a_ref, b_ref, o_ref):
        o_ref[...] = _dg(a_ref[...], b_ref[...], NN).astype(out_dtype)

    return pl.pallas_call(
        body, grid=(m // tm, n // tn),
        in_specs=[pl.BlockSpec((tm, k), lambda i, j: (i, 0)), pl.BlockSpec((k, tn), lambda i, j: (0, j))],
        out_specs=pl.BlockSpec((tm, tn), lambda i, j: (i, j)),
        out_shape=jax.ShapeDtypeStruct((m, n), out_dtype), name=name, compiler_params=_cp())(a, b)


def _tie(body, after):
    if after is None:
        return body, [], []
    return (lambda tok_ref, *refs: body(*refs)), [ANY], [after.reshape(1, 1)]


def _mm_nt(a, b, out_dtype, tm, tb, name, after=None):
    m, c = a.shape
    kb = b.shape[0]

    def body(a_ref, b_ref, o_ref):
        o_ref[...] = _dg(a_ref[...], b_ref[...], NT).astype(out_dtype)

    body, tspec, tok = _tie(body, after)
    return pl.pallas_call(
        body, grid=(m // tm, kb // tb),
        in_specs=tspec + [pl.BlockSpec((tm, c), lambda i, j: (i, 0)), pl.BlockSpec((tb, c), lambda i, j: (j, 0))],
        out_specs=pl.BlockSpec((tm, tb), lambda i, j: (i, j)),
        out_shape=jax.ShapeDtypeStruct((m, kb), out_dtype), name=name, compiler_params=_cp())(*tok, a, b)


def _mm_tn(a, b, out_dtype, ta, tb, name, after=None):
    m, ka = a.shape
    nb = b.shape[1]

    def body(a_ref, b_ref, o_ref):
        o_ref[...] = _dg(a_ref[...], b_ref[...], TN).astype(out_dtype)

    body, tspec, tok = _tie(body, after)
    return pl.pallas_call(
        body, grid=(ka // ta, nb // tb),
        in_specs=tspec + [pl.BlockSpec((m, ta), lambda i, j: (0, i)), pl.BlockSpec((m, tb), lambda i, j: (0, j))],
        out_specs=pl.BlockSpec((ta, tb), lambda i, j: (i, j)),
        out_shape=jax.ShapeDtypeStruct((ka, nb), out_dtype), name=name, compiler_params=_cp())(*tok, a, b)


def _mm_nn2(a1, a2, b1, b2, tm, tn, name, after=None):
    m, k = a1.shape
    n = b1.shape[1]

    def body(a1_ref, a2_ref, b1_ref, b2_ref, o_ref):
        o_ref[...] = _dg(a1_ref[...], b1_ref[...], NN) + _dg(a2_ref[...], b2_ref[...], NN)

    body, tspec, tok = _tie(body, after)
    arow = pl.BlockSpec((tm, k), lambda i, j: (i, 0))
    bcol = pl.BlockSpec((k, tn), lambda i, j: (0, j))
    return pl.pallas_call(
        body, grid=(m // tm, n // tn), in_specs=tspec + [arow, arow, bcol, bcol],
        out_specs=pl.BlockSpec((tm, tn), lambda i, j: (i, j)),
        out_shape=jax.ShapeDtypeStruct((m, n), F32), name=name, compiler_params=_cp())(*tok, a1, a2, b1, b2)


def _mm_nt_swiglu(h, wu_t, gate, tm, tb, name, after=None):
    m, c = h.shape
    kb = wu_t.shape[0]

    def body(h_ref, w_ref, g_ref, u_ref, a_ref):
        u = _dg(h_ref[...], w_ref[...], NT)
        g = g_ref[...].astype(F32)
        u_ref[...] = u.astype(BF)
        a_ref[...] = (g * jax.nn.sigmoid(g) * u).astype(BF)

    body, tspec, tok = _tie(body, after)
    tile = pl.BlockSpec((tm, tb), lambda i, j: (i, j))
    return pl.pallas_call(
        body, grid=(m // tm, kb // tb),
        in_specs=tspec + [pl.BlockSpec((tm, c), lambda i, j: (i, 0)), pl.BlockSpec((tb, c), lambda i, j: (j, 0)),
                          tile],
        out_specs=[tile, tile], out_shape=[jax.ShapeDtypeStruct((m, kb), BF)] * 2,
        name=name, compiler_params=_cp())(*tok, h, wu_t, gate)


def _mm_nt_dswiglu(dys, wd, gate, up, tm, tb, name, after=None):
    m, c = dys.shape
    kb = wd.shape[0]

    def body(d_ref, w_ref, g_ref, u_ref, dg_ref, du_ref):
        da = _dg(d_ref[...], w_ref[...], NT)
        g = g_ref[...].astype(F32)
        u = u_ref[...].astype(F32)
        s = jax.nn.sigmoid(g)
        dg_ref[...] = (da * u * (s * (1.0 + g * (1.0 - s)))).astype(BF)
        du_ref[...] = (da * (g * s)).astype(BF)

    body, tspec, tok = _tie(body, after)
    tile = pl.BlockSpec((tm, tb), lambda i, j: (i, j))
    return pl.pallas_call(
        body, grid=(m // tm, kb // tb),
        in_specs=tspec + [pl.BlockSpec((tm, c), lambda i, j: (i, 0)), pl.BlockSpec((tb, c), lambda i, j: (j, 0)),
                          tile, tile],
        out_specs=[tile, tile], out_shape=[jax.ShapeDtypeStruct((m, kb), BF)] * 2,
        name=name, compiler_params=_cp())(*tok, dys, wd, gate, up)


def _rms_fwd(x, gain, name):
    tm = 512
    row = pl.BlockSpec((tm, D), lambda i: (i, 0))

    def body(x_ref, g_ref, h_ref):
        xv = x_ref[...]
        r = lax.rsqrt(jnp.mean(xv * xv, axis=-1, keepdims=True) + NORM_EPS)
        h_ref[...] = (xv * r * g_ref[...]).astype(BF)

    return pl.pallas_call(
        body, grid=(T // tm,), in_specs=[row, pl.BlockSpec((1, D), lambda i: (0, 0))], out_specs=row,
        out_shape=jax.ShapeDtypeStruct((T, D), BF), name=name, compiler_params=_cp())(x, gain)


def _rms_bwd(x, gain, dh, dres, alpha_out, name):
    tm = 512
    row = pl.BlockSpec((tm, D), lambda i: (i, 0))
    gspec = pl.BlockSpec((1, D), lambda i: (0, 0))

    def body(x_ref, g_ref, dh_ref, dres_ref, dx_ref, dxs_ref, dg_ref):
        i = pl.program_id(0)
        xv = x_ref[...]
        r = lax.rsqrt(jnp.mean(xv * xv, axis=-1, keepdims=True) + NORM_EPS)
        xh = xv * r
        dhv = dh_ref[...].astype(F32)
        part = jnp.sum(dhv * xh, axis=0, keepdims=True)

        @pl.when(i == 0)
        def _():
            dg_ref[...] = part

        @pl.when(i > 0)
        def _():
            dg_ref[...] += part

        dxh = dhv * g_ref[...]
        dx = r * (dxh - xh * jnp.mean(dxh * xh, axis=-1, keepdims=True)) + dres_ref[...]
        dx_ref[...] = dx
        dxs_ref[...] = (alpha_out * dx).astype(BF)

    return pl.pallas_call(
        body, grid=(T // tm,), in_specs=[row, gspec, row, row], out_specs=[row, row, gspec],
        out_shape=[jax.ShapeDtypeStruct((T, D), F32), jax.ShapeDtypeStruct((T, D), BF),
                   jax.ShapeDtypeStruct((1, D), F32)],
        name=name, compiler_params=_cp())(x, gain, dh, dres)


def _loss_bwd(x_prev, y, gain, target, name):
    tm = 512
    row = pl.BlockSpec((tm, D), lambda i: (i, 0))
    gspec = pl.BlockSpec((1, D), lambda i: (0, 0))
    lspec = pl.BlockSpec((8, 128), lambda i: (0, 0))

    def body(x_ref, y_ref, g_ref, t_ref, dx_ref, dxs_ref, dg_ref, loss_ref):
        i = pl.program_id(0)
        xv = x_ref[...] + 0.5 * y_ref[...]
        r = lax.rsqrt(jnp.mean(xv * xv, axis=-1, keepdims=True) + NORM_EPS)
        xh = xv * r
        diff = xh * g_ref[...] - t_ref[...]
        lpart = 0.5 * jnp.sum(jnp.mean(diff * diff, axis=-1, keepdims=True), axis=0, keepdims=True)
        dy = diff * (1.0 / D)
        part = jnp.sum(dy * xh, axis=0, keepdims=True)

        @pl.when(i == 0)
        def _():
            dg_ref[...] = part
            loss_ref[...] = jnp.broadcast_to(lpart, (8, 128))

        @pl.when(i > 0)
        def _():
            dg_ref[...] += part
            loss_ref[...] += jnp.broadcast_to(lpart, (8, 128))

        dxh = dy * g_ref[...]
        dx = r * (dxh - xh * jnp.mean(dxh * xh, axis=-1, keepdims=True))
        dx_ref[...] = dx
        dxs_ref[...] = (0.5 * dx).astype(BF)

    return pl.pallas_call(
        body, grid=(T // tm,), in_specs=[row, row, gspec, row], out_specs=[row, row, gspec, lspec],
        out_shape=[jax.ShapeDtypeStruct((T, D), F32), jax.ShapeDtypeStruct((T, D), BF),
                   jax.ShapeDtypeStruct((1, D), F32), jax.ShapeDtypeStruct((8, 128), F32)],
        name=name, compiler_params=_cp())(x_prev, y, gain, target)


def _mm_nn_resnorm(pairs, x_prev, alpha, gain, tm, name, after=None):
    m = x_prev.shape[0]
    n = len(pairs)

    def body(*refs):
        x_ref, g_ref, xo_ref, h_ref = refs[2 * n:]
        y = _dg(refs[0][...], refs[n][...], NN)
        for i in range(1, n):
            y = y + _dg(refs[i][...], refs[n + i][...], NN)
        xv = x_ref[...] + alpha * y
        xo_ref[...] = xv
        r = lax.rsqrt(jnp.mean(xv * xv, axis=-1, keepdims=True) + NORM_EPS)
        h_ref[...] = (xv * r * g_ref[...]).astype(BF)

    body, tspec, tok = _tie(body, after)
    row = pl.BlockSpec((tm, D), lambda i: (i, 0))
    return pl.pallas_call(
        body, grid=(m // tm,),
        in_specs=tspec + [pl.BlockSpec((tm, a.shape[1]), lambda i: (i, 0)) for a, _ in pairs]
        + [pl.BlockSpec(b.shape, lambda i: (0, 0)) for _, b in pairs] + [row, pl.BlockSpec((1, D), lambda i: (0, 0))],
        out_specs=[row, row],
        out_shape=[jax.ShapeDtypeStruct((m, D), F32), jax.ShapeDtypeStruct((m, D), BF)],
        name=name, compiler_params=_cp())(*tok, *[a for a, _ in pairs], *[b for _, b in pairs], x_prev, gain)


def _ffn_up(h, need, ahead, tag):
    wg = need("g", h)
    gate = _mm_nt(h, wg, BF, 1024, 1408, tag + "_gate")
    wu = need("u", gate)
    up, act = _mm_nt_swiglu(h, wu, gate, 1024, 1408, tag + "_up_act", after=ahead("d", gate))
    wd = need("d", up)
    return act, (h, gate, up, act, wg, wu, wd)


def _ffn_bwd(x_in, gain, saved, dxo, dys, alpha_out, emit, tag):
    h, gate, up, act, wg, wu, wd = saved
    sent = emit("d", _mm_tn(act, dys, BF, 256, 1024, tag + "_dwd"))
    dgate, dup = _mm_nt_dswiglu(dys, wd, gate, up, 1024, 1408, tag + "_dact", after=sent)
    sent = emit("g", _mm_tn(dgate, h, BF, 256, 1024, tag + "_dwg"))
    sent = emit("u", _mm_tn(dup, h, BF, 256, 1024, tag + "_dwu", after=sent))
    dh = _mm_nn2(dgate, dup, wg, wu, 1024, 512, tag + "_dh", after=sent)
    dx, dxs, dgain = _rms_bwd(x_in, gain, dh, dxo, alpha_out, tag + "_dnorm")
    return dx, dxs, dgain


SLAB = 2048
N_SLAB = T // SLAB
N_PAIR = A_HEADS // 2


def _pair_masks():
    lane = _iota((128, 128), 1)
    return lane < A_HD, lane >= A_HD


def _slope_table():
    h = 2 * jnp.arange(N_PAIR)[:, None] + jnp.minimum(jnp.arange(8), 1)[None, :]
    return jnp.broadcast_to((2.0 ** (-(h + 1).astype(F32)))[:, :, None], (N_PAIR, 8, 128))


def _rows(ref, start, d):
    if d == 1:
        return ref[pl.ds(start, 128), :]
    return ref[pl.ds(start, 128, stride=d), :]


def _put_rows(ref, start, d, val):
    if d == 1:
        ref[pl.ds(start, 128), :] = val
    else:
        ref[pl.ds(start, 128, stride=d), :] = val


def _units(d):
    return [(r, b, r + 128 * d * b) for r in range(d) for b in range(SLAB // (128 * d))]


def _band(d, prev_valid):
    qi = _iota((128, 256), 0)
    kj = _iota((128, 256), 1)
    steps = qi + 128 - kj
    valid = (steps >= 0) & (steps <= 128) & (prev_valid | (kj >= 128))
    return valid, (steps * d).astype(F32)


def _attn_fwd(z_at, name, after=None):
    def body(sl_ref, q_ref, kc_ref, kp_ref, vc_ref, vp_ref, of_ref, ob_ref, lse_ref, m_s, l_s, a_s):
        n = pl.program_id(1)
        lo, hi = _pair_masks()
        slopes = (sl_ref[0, 0:1, 0:1], sl_ref[0, 1:2, 0:1])

        def unit(d, start, b, first, carry):
            q = _rows(q_ref, start, d).astype(BF)
            kcur, vcur = _rows(kc_ref, start, d).astype(BF), _rows(vc_ref, start, d).astype(BF)
            if b > 0:
                (kprev, vprev), prev_valid = carry, True
            else:
                pstart = start + SLAB - 128 * d
                kprev, vprev = _rows(kp_ref, pstart, d).astype(BF), _rows(vp_ref, pstart, d).astype(BF)
                prev_valid = n > 0
            kcat = jnp.concatenate([kprev, kcur], axis=0)
            vcat = jnp.concatenate([vprev, vcur], axis=0)
            valid, dist = _band(d, prev_valid)
            ms, ls, pvs = [], [], []
            for e in range(2):
                qm = jnp.where(lo if e == 0 else hi, q, jnp.zeros_like(q))
                s = _dg(qm, kcat, NT) * (A_HD ** -0.5) - slopes[e] * dist
                s = jnp.where(valid, s, NEG)
                m = jnp.max(s, axis=1, keepdims=True)
                p = jnp.exp(s - m)
                ms.append(m)
                ls.append(jnp.sum(p, axis=1, keepdims=True))
                pvs.append(_dg(p.astype(BF), vcat, NN))
            m_u = jnp.where(lo, ms[0], ms[1])
            l_u = jnp.where(lo, ls[0], ls[1])
            a_u = jnp.where(lo, pvs[0], pvs[1])
            if first:
                m_n, l_n, a_n = m_u, l_u, a_u
            else:
                m_o = _rows(m_s, start, d)
                m_n = jnp.maximum(m_o, m_u)
                c_o = jnp.exp(m_o - m_n)
                c_u = jnp.exp(m_u - m_n)
                l_n = _rows(l_s, start, d) * c_o + l_u * c_u
                a_n = _rows(a_s, start, d) * c_o + a_u * c_u
            _put_rows(m_s, start, d, m_n)
            _put_rows(l_s, start, d, l_n)
            _put_rows(a_s, start, d, a_n)
            return kcur, vcur

        for pi, (_, d) in enumerate(PATTERNS):
            carry = None
            for r, b, start in _units(d):
                carry = unit(d, start, b, pi == 0, carry)
        l = l_s[...]
        out = a_s[...] / l
        of_ref[...] = out
        ob_ref[...] = out.astype(BF)
        lse_ref[...] = m_s[...] + jnp.log(l)

    body, tspec, tok = _tie(body, after)
    cur = lambda c: pl.BlockSpec((SLAB, 128), lambda j, n: (n, c * N_PAIR + j))
    prv = lambda c: pl.BlockSpec((SLAB, 128), lambda j, n: (jnp.maximum(n - 1, 0), c * N_PAIR + j))
    out = pl.BlockSpec((SLAB, 128), lambda j, n: (n, j))
    return pl.pallas_call(
        body, grid=(N_PAIR, N_SLAB),
        in_specs=tspec + [pl.BlockSpec((1, 8, 128), lambda j, n: (j, 0, 0)), cur(0), cur(1), prv(1), cur(2), prv(2)],
        out_specs=[out, out, out],
        out_shape=[jax.ShapeDtypeStruct((T, AW), F32), jax.ShapeDtypeStruct((T, AW), BF),
                   jax.ShapeDtypeStruct((T, AW), F32)],
        scratch_shapes=[pltpu.VMEM((SLAB, 128), F32)] * 3,
        name=name, compiler_params=_cp())(*tok, _slope_table(), z_at, z_at, z_at, z_at, z_at)


def _attn_bwd(z_at, dout, out, lse, name):
    def body(sl_ref, q_ref, kc_ref, kp_ref, vc_ref, vp_ref, do_ref, o_ref, lse_ref, dq_ref, dk_ref, dv_ref,
             dq_s, dk_s, dv_s, ck_s, cv_s):
        step = pl.program_id(1)
        n = N_SLAB - 1 - step
        lo, hi = _pair_masks()
        slopes = (sl_ref[0, 0:1, 0:1], sl_ref[0, 1:2, 0:1])

        @pl.when(step == 0)
        def _():
            ck_s[...] = jnp.zeros_like(ck_s)
            cv_s[...] = jnp.zeros_like(cv_s)

        dk_s[...] = ck_s[...]
        dv_s[...] = cv_s[...]
        ck_s[...] = jnp.zeros_like(ck_s)
        cv_s[...] = jnp.zeros_like(cv_s)

        def add_rows(ref, start, d, val):
            _put_rows(ref, start, d, _rows(ref, start, d) + val)

        def unit(d, start, b, first, carry):
            q = _rows(q_ref, start, d).astype(BF)
            do_f = _rows(do_ref, start, d)
            do = do_f.astype(BF)
            prod = do_f * _rows(o_ref, start, d)
            lse_u = _rows(lse_ref, start, d)
            kcur, vcur = _rows(kc_ref, start, d).astype(BF), _rows(vc_ref, start, d).astype(BF)
            if b > 0:
                (kprev, vprev), prev_valid = carry, True
            else:
                pstart = start + SLAB - 128 * d
                kprev, vprev = _rows(kp_ref, pstart, d).astype(BF), _rows(vp_ref, pstart, d).astype(BF)
                prev_valid = n > 0
            kcat = jnp.concatenate([kprev, kcur], axis=0)
            vcat = jnp.concatenate([vprev, vcur], axis=0)
            valid, dist = _band(d, prev_valid)
            masks = (lo, hi)
            qms = [jnp.where(msk, q, jnp.zeros_like(q)) for msk in masks]
            doms = [jnp.where(msk, do, jnp.zeros_like(do)) for msk in masks]
            deltas = [jnp.sum(jnp.where(msk, prod, 0.0), axis=1, keepdims=True) for msk in masks]
            ss = [_dg(qm, kcat, NT) * (A_HD ** -0.5) - sl * dist for qm, sl in zip(qms, slopes)]
            dps = [_dg(dom, vcat, NT) for dom in doms]
            ps = [jnp.where(valid, jnp.exp(jnp.where(valid, s, NEG) - lse_u[:, 64 * e:64 * e + 1]), 0.0)
                  for e, s in enumerate(ss)]
            dss = [(p * (dp - delta)).astype(BF) for p, dp, delta in zip(ps, dps, deltas)]
            pbs = [p.astype(BF) for p in ps]
            dqs = [_dg(ds, kcat, NN) for ds in dss]
            dkc = (_dg(dss[0], qms[0], TN) + _dg(dss[1], qms[1], TN)) * (A_HD ** -0.5)
            dvc = _dg(pbs[0], doms[0], TN) + _dg(pbs[1], doms[1], TN)
            dq_u = jnp.where(lo, dqs[0], dqs[1]) * (A_HD ** -0.5)
            if first:
                _put_rows(dq_s, start, d, dq_u)
            else:
                add_rows(dq_s, start, d, dq_u)
            add_rows(dk_s, start, d, dkc[128:])
            add_rows(dv_s, start, d, dvc[128:])
            if b > 0:
                add_rows(dk_s, start - 128 * d, d, dkc[:128])
                add_rows(dv_s, start - 128 * d, d, dvc[:128])
            else:
                pstart = start + SLAB - 128 * d
                add_rows(ck_s, pstart, d, dkc[:128])
                add_rows(cv_s, pstart, d, dvc[:128])
            return kcur, vcur

        for pi, (_, d) in enumerate(PATTERNS):
            carry = None
            for r, b, start in _units(d):
                carry = unit(d, start, b, pi == 0, carry)
        dq_ref[...] = dq_s[...].astype(BF)
        dk_ref[...] = dk_s[...].astype(BF)
        dv_ref[...] = dv_s[...].astype(BF)

    rev = lambda n: N_SLAB - 1 - n
    cur = lambda c: pl.BlockSpec((SLAB, 128), lambda j, n: (rev(n), c * N_PAIR + j))
    prv = lambda c: pl.BlockSpec((SLAB, 128), lambda j, n: (jnp.maximum(rev(n) - 1, 0), c * N_PAIR + j))
    one = pl.BlockSpec((SLAB, 128), lambda j, n: (rev(n), j))
    return pl.pallas_call(
        body, grid=(N_PAIR, N_SLAB),
        in_specs=[pl.BlockSpec((1, 8, 128), lambda j, n: (j, 0, 0)), cur(0), cur(1), prv(1), cur(2), prv(2),
                  one, one, one],
        out_specs=[one, one, one], out_shape=[jax.ShapeDtypeStruct((T, AW), BF)] * 3,
        scratch_shapes=[pltpu.VMEM((SLAB, 128), F32)] * 5,
        name=name, compiler_params=_cp())(_slope_table(), z_at, z_at, z_at, z_at, z_at, dout, out, lse)


def _silu(x):
    return x * jax.nn.sigmoid(x)


def _qk_math(c):
    s = _silu(c)
    return s * lax.rsqrt(jnp.sum(s * s, axis=-1, keepdims=True) + L2_EPS)


def _softplus(x):
    return jnp.maximum(x, 0.0) + jnp.log(1.0 + jnp.exp(-jnp.abs(x)))


def _gate_math(bd, alog_row, dtb_row):
    rows = bd.shape[0]
    lane = _iota(bd.shape, 1)
    beta = jax.nn.sigmoid(bd)
    g = jnp.where((lane >= DN_H) & (lane < 2 * DN_H), -jnp.exp(alog_row) * _softplus(bd + dtb_row), 0.0)
    ri = _iota((rows, rows), 0)
    ci = _iota((rows, rows), 1)
    same = (ri // CH) == (ci // CH)
    li = _iota((128, 128), 0)
    lj = _iota((128, 128), 1)
    to_next_group = jnp.where((lj == li + DN_H) & (li >= DN_H) & (li < 2 * DN_H), 1.0, 0.0)
    gc = _hdot(jnp.where(same & (ci <= ri), 1.0, 0.0), g)
    glast = _hdot(_hdot(jnp.where(same, 1.0, 0.0), g), to_next_group)
    return jnp.where(lane < DN_H, beta, 0.0) + gc + glast


def _shift_down(cur, halo, s):
    if s == 0:
        return cur
    rolled = pltpu.roll(cur, s, 0)
    hr = pltpu.roll(halo, s, 0)
    head = jnp.where(_iota(hr.shape, 0) < s, hr, rolled[:8])
    return jnp.concatenate([head, rolled[8:]], axis=0)


def _shift_up(cur, halo, s):
    if s == 0:
        return cur
    rows = cur.shape[0]
    rolled = pltpu.roll(cur, rows - s, 0)
    hr = pltpu.roll(halo, 8 - s, 0)
    tail = jnp.where(_iota(hr.shape, 0) >= 8 - s, hr, rolled[rows - 8:])
    return jnp.concatenate([rolled[:rows - 8], tail], axis=0)


def _dn_pre_fwd(z_dn, conv_w8, alog_row, dtb_row, name):
    tm = 256
    wq = 3 * DNW

    def body(raw_ref, halo_ref, bd_ref, w_ref, al_ref, dt_ref, conv_ref, qkv_ref, bg_ref):
        i = pl.program_id(0)
        cur = raw_ref[...]
        halo = jnp.where(i > 0, halo_ref[...], 0.0)
        w = w_ref[...]
        conv = jnp.zeros((tm, wq), F32)
        for j in range(4):
            conv = conv + _shift_down(cur, halo, 3 - j) * w[j:j + 1, :]
        conv_ref[...] = conv
        for blk in range(3 * DN_H):
            sl = slice(128 * blk, 128 * blk + 128)
            c = conv[:, sl]
            qkv_ref[:, sl] = _qk_math(c) if blk < 2 * DN_H else _silu(c)
        bg_ref[...] = _gate_math(bd_ref[...], al_ref[...], dt_ref[...])

    one = pl.BlockSpec((1, 128), lambda i: (0, 0))
    return pl.pallas_call(
        body, grid=(T // tm,),
        in_specs=[pl.BlockSpec((tm, wq), lambda i: (i, 0)),
                  pl.BlockSpec((8, wq), lambda i: (jnp.maximum(i * (tm // 8) - 1, 0), 0)),
                  pl.BlockSpec((tm, 128), lambda i: (i, BD_BLK)),
                  pl.BlockSpec((8, wq), lambda i: (0, 0)), one, one],
        out_specs=[pl.BlockSpec((tm, wq), lambda i: (i, 0)), pl.BlockSpec((tm, wq), lambda i: (i, 0)),
                   pl.BlockSpec((tm, 128), lambda i: (i, 0))],
        out_shape=[jax.ShapeDtypeStruct((T, wq), F32), jax.ShapeDtypeStruct((T, wq), F32),
                   jax.ShapeDtypeStruct((T, 128), F32)],
        name=name, compiler_params=_cp())(z_dn, z_dn, z_dn, conv_w8, alog_row, dtb_row)


def _dn_pre_bwd(conv, z_dn, alog_row, dtb_row, dqn, dkn, dvn, dbg, name):
    tm = 256
    wq = 3 * DNW

    def body(conv_ref, bd_ref, al_ref, dt_ref, dq_ref, dk_ref, dv_ref, dbg_ref,
             dconv_ref, dbd_ref, dal_ref, ddt_ref):
        i = pl.program_id(0)
        for blk in range(3 * DN_H):
            sl = slice(128 * blk, 128 * blk + 128)
            src = (dq_ref, dk_ref, dv_ref)[blk // DN_H]
            ct = src[:, 128 * (blk % DN_H):128 * (blk % DN_H) + 128]
            fn = _qk_math if blk < 2 * DN_H else _silu
            _, vjp = jax.vjp(fn, conv_ref[:, sl])
            dconv_ref[:, sl] = vjp(ct)[0]
        _, vjp = jax.vjp(_gate_math, bd_ref[...], al_ref[...], dt_ref[...])
        dbd, dal, ddt = vjp(dbg_ref[...])
        dbd_ref[...] = dbd

        @pl.when(i == 0)
        def _():
            dal_ref[...] = dal
            ddt_ref[...] = ddt

        @pl.when(i > 0)
        def _():
            dal_ref[...] += dal
            ddt_ref[...] += ddt

    one = pl.BlockSpec((1, 128), lambda i: (0, 0))
    row = pl.BlockSpec((tm, wq), lambda i: (i, 0))
    hd = pl.BlockSpec((tm, DNW), lambda i: (i, 0))
    st = pl.BlockSpec((tm, 128), lambda i: (i, 0))
    return pl.pallas_call(
        body, grid=(T // tm,),
        in_specs=[row, pl.BlockSpec((tm, 128), lambda i: (i, BD_BLK)), one, one, hd, hd, hd, st],
        out_specs=[row, st, one, one],
        out_shape=[jax.ShapeDtypeStruct((T, wq), F32), jax.ShapeDtypeStruct((T, 128), F32),
                   jax.ShapeDtypeStruct((1, 128), F32), jax.ShapeDtypeStruct((1, 128), F32)],
        name=name, compiler_params=_cp())(conv, z_dn, alog_row, dtb_row, dqn, dkn, dvn, dbg)


def _dn_conv_bwd(dconv, z_dn, conv_w8, name):
    tm = 256
    wq = 3 * DNW
    last = T // tm - 1

    def body(dc_ref, dcn_ref, raw_ref, halo_ref, w_ref, draw_ref, dw_ref):
        i = pl.program_id(0)
        dc = dc_ref[...]
        nxt = jnp.where(i < last, dcn_ref[...], 0.0)
        cur = raw_ref[...]
        halo = jnp.where(i > 0, halo_ref[...], 0.0)
        w = w_ref[...]
        draw = jnp.zeros((tm, wq), F32)
        rows = []
        for j in range(4):
            draw = draw + _shift_up(dc, nxt, 3 - j) * w[j:j + 1, :]
            rows.append(jnp.sum(dc * _shift_down(cur, halo, 3 - j), axis=0, keepdims=True))
        draw_ref[...] = draw
        part = jnp.concatenate(rows + [jnp.zeros((4, wq), F32)], axis=0)

        @pl.when(i == 0)
        def _():
            dw_ref[...] = part

        @pl.when(i > 0)
        def _():
            dw_ref[...] += part

    row = pl.BlockSpec((tm, wq), lambda i: (i, 0))
    return pl.pallas_call(
        body, grid=(T // tm,),
        in_specs=[row, pl.BlockSpec((8, wq), lambda i: (jnp.minimum((i + 1) * (tm // 8), T // 8 - 1), 0)),
                  row, pl.BlockSpec((8, wq), lambda i: (jnp.maximum(i * (tm // 8) - 1, 0), 0)),
                  pl.BlockSpec((8, wq), lambda i: (0, 0))],
        out_specs=[row, pl.BlockSpec((8, wq), lambda i: (0, 0))],
        out_shape=[jax.ShapeDtypeStruct((T, wq), F32), jax.ShapeDtypeStruct((8, wq), F32)],
        name=name, compiler_params=_cp())(dconv, dconv, z_dn, z_dn, conv_w8)


def _h3(a, b, dims=NN):
    return lax.dot_general(a, b, dims, precision=lax.Precision.HIGH, preferred_element_type=F32)


@jax.custom_vjp
def _inverse_given(a_mat, tinv):
    return tinv


def _inverse_given_fwd(a_mat, tinv):
    return tinv, tinv


def _inverse_given_bwd(tinv, g):
    return -_h3(tinv, _h3(g, tinv, NT), TN), jnp.zeros_like(tinv)


_inverse_given.defvjp(_inverse_given_fwd, _inverse_given_bwd)


def _prep_head(q, k, v, bgc, h):
    beta = _col(bgc, h)
    gc = jnp.broadcast_to(_col(bgc, DN_H + h), (PAIR, 128))
    glast = jnp.broadcast_to(_col(bgc, 2 * DN_H + h), (PAIR, 128))
    ri = _iota((PAIR, PAIR), 0)
    ci = _iota((PAIR, PAIR), 1)
    same = (ri // CH) == (ci // CH)
    causal = same & (ci <= ri)
    strict = same & (ci < ri)
    eye = ri == ci
    gc_cols = _hdot(jnp.ones((PAIR, PAIR), F32), jnp.where(eye, gc, 0.0))
    decay = jnp.exp(jnp.where(causal, gc - gc_cols, NEG))
    egc = jnp.exp(gc)
    kb = k * beta
    a_mat = jnp.where(strict, _bdot_nt(kb, k) * decay, 0.0)
    qs = q * (DN_HD ** -0.5)
    attn = jnp.where(causal, _bdot_nt(qs, k) * decay, 0.0)
    return a_mat, (v * beta, kb * egc, qs * egc, k * jnp.exp(glast - gc), attn, jnp.exp(glast))


def _prep_tail(tinv, ctx):
    vb, kbe, qg, kdec, attn, decb = ctx
    return _h3(tinv, vb), _h3(tinv, kbe), qg, kdec, attn, decb


def _inverses(a_mats):
    eye = jnp.where(_iota((PAIR, PAIR), 0) == _iota((PAIR, PAIR), 1), 1.0, 0.0)
    ps = [-a for a in a_mats]
    tinvs = [eye + p for p in ps]
    for _ in range(5):
        ps = [_h3(p, p) for p in ps]
        tinvs = [t + _h3(t, p) for t, p in zip(tinvs, ps)]
    return tinvs


def _prep_math(q, k, v, bgc, h, tinv_saved):
    a_mat, ctx = _prep_head(q, k, v, bgc, h)
    return _prep_tail(_inverse_given(a_mat, tinv_saved), ctx)


def _dn_prep_fwd(qkv, bg, name):
    rows = 512
    hd = lambda off: pl.BlockSpec((rows, 128), lambda g, h: (g, off + h))
    out = pl.BlockSpec((rows, 128), lambda g, h: (g, h))

    def body(q_ref, k_ref, v_ref, bg_ref, *outs):
        h = pl.program_id(1)
        spans = [slice(PAIR * pr, PAIR * pr + PAIR) for pr in range(rows // PAIR)]
        heads = [_prep_head(q_ref[rs, :], k_ref[rs, :], v_ref[rs, :], bg_ref[rs, :], h) for rs in spans]
        tinvs = _inverses([a for a, _ in heads])
        for rs, tinv, (_, ctx) in zip(spans, tinvs, heads):
            for o_ref, val in zip(outs, _prep_tail(tinv, ctx) + (tinv,)):
                o_ref[rs, :] = val

    return pl.pallas_call(
        body, grid=(T // rows, DN_H),
        in_specs=[hd(0), hd(DN_H), hd(2 * DN_H), pl.BlockSpec((rows, 128), lambda g, h: (g, 0))],
        out_specs=[out] * 7, out_shape=[jax.ShapeDtypeStruct((T, DNW), F32)] * 7,
        name=name, compiler_params=_cp())(qkv, qkv, qkv, bg)


def _dn_prep_bwd(qkv, bg, tinv, cts, name):
    rows = 512
    hd = lambda off: pl.BlockSpec((rows, 128), lambda g, h: (g, off + h))
    out = pl.BlockSpec((rows, 128), lambda g, h: (g, h))
    st = pl.BlockSpec((rows, 128), lambda g, h: (g, 0))

    def body(q_ref, k_ref, v_ref, bg_ref, ti_ref, c0, c1, c2, c3, c4, c5, dq_ref, dk_ref, dv_ref, dbg_ref):
        h = pl.program_id(1)
        spans = [slice(PAIR * pr, PAIR * pr + PAIR) for pr in range(rows // PAIR)]
        tis = [ti_ref[rs, :] for rs in spans]

        def joint(qs, ks, vs, bs):
            heads = [_prep_head(q, k, v, b, h) for q, k, v, b in zip(qs, ks, vs, bs)]
            return [_prep_tail(_inverse_given(a, ti), ctx) for (a, ctx), ti in zip(heads, tis)]

        _, vjp = jax.vjp(joint, *[[r[rs, :] for rs in spans] for r in (q_ref, k_ref, v_ref, bg_ref)])
        dqs, dks, dvs, dbs = vjp([tuple(c[rs, :] for c in (c0, c1, c2, c3, c4, c5)) for rs in spans])
        for rs, dq, dk, dv in zip(spans, dqs, dks, dvs):
            dq_ref[rs, :] = dq
            dk_ref[rs, :] = dk
            dv_ref[rs, :] = dv
        dbg_all = jnp.concatenate(dbs, axis=0)

        @pl.when(h == 0)
        def _():
            dbg_ref[...] = dbg_all

        @pl.when(h > 0)
        def _():
            dbg_ref[...] += dbg_all

    return pl.pallas_call(
        body, grid=(T // rows, DN_H),
        in_specs=[hd(0), hd(DN_H), hd(2 * DN_H), st] + [out] * 7,
        out_specs=[out, out, out, st],
        out_shape=[jax.ShapeDtypeStruct((T, DNW), F32)] * 3 + [jax.ShapeDtypeStruct((T, 128), F32)],
        name=name, compiler_params=_cp())(qkv, qkv, qkv, bg, tinv, *cts)


def _step_math(s, u, w, qg, kdec, attn, decb, sub):
    vnew = u - _bdot_nn(w, s)
    z = jnp.zeros((CH, 128), F32)
    vfull = jnp.concatenate([vnew, z] if sub == 0 else [z, vnew], axis=0)
    o = _bdot_nn(qg, s) + _bdot_nn(attn, vfull)
    dec = jnp.sum(decb, axis=0, keepdims=True) * (1.0 / CH)
    return s * dec + _bdot_tn(kdec, vnew), o


def _dn_scan_fwd(prep, name):
    npair = T // PAIR
    row = pl.BlockSpec((PAIR, DNW), lambda p: (p, 0))

    def body(u_ref, w_ref, qg_ref, kd_ref, at_ref, db_ref, o_ref, ss_ref, s_ref):
        @pl.when(pl.program_id(0) == 0)
        def _():
            s_ref[...] = jnp.zeros_like(s_ref)

        states = [s_ref[h] for h in range(DN_H)]
        for sub in range(2):
            rs = slice(CH * sub, CH * sub + CH)
            for h in range(DN_H):
                ls = slice(128 * h, 128 * h + 128)
                ss_ref[sub, h] = states[h]
                states[h], o = _step_math(states[h], u_ref[rs, ls], w_ref[rs, ls], qg_ref[rs, ls],
                                          kd_ref[rs, ls], at_ref[rs, ls], db_ref[rs, ls], sub)
                o_ref[rs, ls] = o
        for h in range(DN_H):
            s_ref[h] = states[h]

    return pl.pallas_call(
        body, grid=(npair,), in_specs=[row] * 6,
        out_specs=[row, pl.BlockSpec((2, DN_H, 128, 128), lambda p: (p, 0, 0, 0))],
        out_shape=[jax.ShapeDtypeStruct((T, DNW), F32), jax.ShapeDtypeStruct((T // CH, DN_H, 128, 128), F32)],
        scratch_shapes=[pltpu.VMEM((DN_H, 128, 128), F32)],
        name=name, compiler_params=_cp())(*prep)


def _dn_scan_bwd(prep, states, do, name):
    npair = T // PAIR
    row = pl.BlockSpec((PAIR, DNW), lambda p: (npair - 1 - p, 0))

    def body(u_ref, w_ref, qg_ref, kd_ref, at_ref, db_ref, ss_ref, do_ref, *rest):
        outs, ds_ref = rest[:6], rest[6]

        @pl.when(pl.program_id(0) == 0)
        def _():
            ds_ref[...] = jnp.zeros_like(ds_ref)

        dss = [ds_ref[h] for h in range(DN_H)]
        for sub in (1, 0):
            rs = slice(CH * sub, CH * sub + CH)
            for h in range(DN_H):
                ls = slice(128 * h, 128 * h + 128)
                args = (ss_ref[sub, h],) + tuple(r[rs, ls] for r in (u_ref, w_ref, qg_ref, kd_ref, at_ref, db_ref))
                _, vjp = jax.vjp(functools.partial(_step_math, sub=sub), *args)
                cts = vjp((dss[h], do_ref[rs, ls]))
                dss[h] = cts[0]
                for o_ref, val in zip(outs, cts[1:]):
                    o_ref[rs, ls] = val
        for h in range(DN_H):
            ds_ref[h] = dss[h]

    return pl.pallas_call(
        body, grid=(npair,),
        in_specs=[row] * 6 + [pl.BlockSpec((2, DN_H, 128, 128), lambda p: (npair - 1 - p, 0, 0, 0)), row],
        out_specs=[row] * 6, out_shape=[jax.ShapeDtypeStruct((T, DNW), F32)] * 6,
        scratch_shapes=[pltpu.VMEM((DN_H, 128, 128), F32)],
        name=name, compiler_params=_cp())(*prep, states, do)


def _post_math(o, gate, wrow):
    return o * lax.rsqrt(jnp.mean(o * o, axis=-1, keepdims=True) + NORM_EPS) * wrow * _silu(gate)


def _dn_post_fwd(o, z_dn, dn_norm, name):
    tm = 512
    row = pl.BlockSpec((tm, DNW), lambda i: (i, 0))

    def body(o_ref, g_ref, w_ref, y_ref):
        for h in range(DN_H):
            ls = slice(128 * h, 128 * h + 128)
            y_ref[:, ls] = _post_math(o_ref[:, ls], g_ref[:, ls], w_ref[...]).astype(BF)

    return pl.pallas_call(
        body, grid=(T // tm,),
        in_specs=[row, pl.BlockSpec((tm, DNW), lambda i: (i, 3)), pl.BlockSpec((1, 128), lambda i: (0, 0))],
        out_specs=row, out_shape=jax.ShapeDtypeStruct((T, DNW), BF),
        name=name, compiler_params=_cp())(o, z_dn, dn_norm)


def _dn_post_bwd(o, z_dn, dn_norm, dy, name):
    tm = 512
    row = pl.BlockSpec((tm, DNW), lambda i: (i, 0))
    one = pl.BlockSpec((1, 128), lambda i: (0, 0))

    def body(o_ref, g_ref, w_ref, dy_ref, do_ref, dg_ref, dw_ref):
        i = pl.program_id(0)
        dw = jnp.zeros((1, 128), F32)
        for h in range(DN_H):
            ls = slice(128 * h, 128 * h + 128)
            _, vjp = jax.vjp(_post_math, o_ref[:, ls], g_ref[:, ls], w_ref[...])
            do, dg, dwh = vjp(dy_ref[:, ls].astype(F32))
            do_ref[:, ls] = do
            dg_ref[:, ls] = dg
            dw = dw + dwh

        @pl.when(i == 0)
        def _():
            dw_ref[...] = dw

        @pl.when(i > 0)
        def _():
            dw_ref[...] += dw

    return pl.pallas_call(
        body, grid=(T // tm,),
        in_specs=[row, pl.BlockSpec((tm, DNW), lambda i: (i, 3)), one, pl.BlockSpec((tm, DNW), lambda i: (i, 1))],
        out_specs=[row, row, one],
        out_shape=[jax.ShapeDtypeStruct((T, DNW), F32), jax.ShapeDtypeStruct((T, DNW), F32),
                   jax.ShapeDtypeStruct((1, 128), F32)],
        name=name, compiler_params=_cp())(o, z_dn, dn_norm, dy)


def _dz_dn_assemble(draw, dgate, dbd, name):
    tm = 512

    def body(a_ref, b_ref, c_ref, o_ref):
        o_ref[:, :3 * DNW] = a_ref[...].astype(BF)
        o_ref[:, 3 * DNW:4 * DNW] = b_ref[...].astype(BF)
        o_ref[:, 4 * DNW:4 * DNW + 128] = c_ref[...].astype(BF)
        o_ref[:, 4 * DNW + 128:] = jnp.zeros((tm, 128), BF)

    return pl.pallas_call(
        body, grid=(T // tm,),
        in_specs=[pl.BlockSpec((tm, 3 * DNW), lambda i: (i, 0)), pl.BlockSpec((tm, DNW), lambda i: (i, 0)),
                  pl.BlockSpec((tm, 128), lambda i: (i, 0))],
        out_specs=pl.BlockSpec((tm, ZD), lambda i: (i, 0)),
        out_shape=jax.ShapeDtypeStruct((T, ZD), BF), name=name, compiler_params=_cp())(draw, dgate, dbd)


HBM = pl.BlockSpec(memory_space=pltpu.HBM)
SEM = pl.BlockSpec(memory_space=pltpu.SEMAPHORE)
EFFECT = pltpu.SideEffectType.DATAFLOW_SIDE_EFFECTING
N_PEER = N_DEV - 1


ALL_PEERS = (1, 2, 4, 3, 5, 6, 7)
FIRST_HOP = (1, 2, 4, 6)
FORWARDED = (2, 4, 6)


def _peers(x, y, c, ks=ALL_PEERS):
    return [(k, (x ^ (k >> 2), y ^ ((k >> 1) & 1), c ^ (k & 1))) for k in ks]


def _exchange_copy(ins, lands, ssems, rsems, scatter, t, k, pos, me):
    px, py, pc = pos
    src = ins[t].at[4 * px + 2 * py + pc] if scatter else ins[t]
    return pltpu.make_async_remote_copy(
        src_ref=src, dst_ref=lands[t].at[me], send_sem=ssems[t].at[k - 1], recv_sem=rsems[t].at[k - 1],
        device_id=pos, device_id_type=MESH_ID)


def _xstart(bufs, scatter, name, ks=ALL_PEERS):
    nt = len(bufs)
    lands = [lax.empty((N_DEV,) + tuple(b.shape[1:] if scatter else b.shape), b.dtype) for b in bufs]

    def body(*refs):
        ins, lnd = refs[:nt], refs[nt:2 * nt]
        ssems, rsems = refs[2 * nt:3 * nt], refs[3 * nt:4 * nt]
        token = refs[-1]
        x, y, c = lax.axis_index("x"), lax.axis_index("y"), lax.axis_index("c")
        me = 4 * x + 2 * y + c
        for t in range(nt):
            for k, pos in _peers(x, y, c, ks):
                _exchange_copy(ins, lnd, ssems, rsems, scatter, t, k, pos, me).start()
        token[...] = jnp.zeros_like(token)

    both = list(bufs) + lands
    res = pl.pallas_call(
        body, name=name,
        out_shape=[pltpu.SemaphoreType.DMA((N_PEER,))] * (2 * nt)
        + [pltpu.HBM(b.shape, b.dtype) for b in both] + [jax.ShapeDtypeStruct((8, 128), F32)],
        in_specs=[HBM] * (2 * nt),
        out_specs=[SEM] * (2 * nt) + [HBM] * (2 * nt) + [pl.BlockSpec(memory_space=pltpu.VMEM)],
        input_output_aliases={i: 2 * nt + i for i in range(2 * nt)},
        compiler_params=pltpu.CompilerParams(has_side_effects=EFFECT),
    )(*[pltpu.with_memory_space_constraint(b, pltpu.HBM) for b in both])
    return res[:nt], res[nt:2 * nt], res[2 * nt:3 * nt], res[3 * nt:4 * nt], res[-1][0, 0]


def _xwait(ssems, rsems, thrus, lands, scatter, after, name, ks=ALL_PEERS):
    nt = len(lands)

    def body(*refs):
        ins, lnd = refs[:nt], refs[nt:2 * nt]
        ss, rs = refs[2 * nt:3 * nt], refs[3 * nt:4 * nt]
        x, y, c = lax.axis_index("x"), lax.axis_index("y"), lax.axis_index("c")
        me = 4 * x + 2 * y + c
        for t in range(nt):
            for k, pos in _peers(x, y, c, ks):
                cp = _exchange_copy(ins, lnd, ss, rs, scatter, t, k, pos, me)
                cp.wait_send()
                cp.wait_recv()

    both = list(thrus) + list(lands)
    res = pl.pallas_call(
        body, name=name, out_shape=[pltpu.HBM(b.shape, b.dtype) for b in both],
        in_specs=[HBM] * (2 * nt) + [SEM] * (2 * nt) + [ANY], out_specs=[HBM] * (2 * nt),
        input_output_aliases={i: i for i in range(2 * nt)},
        compiler_params=pltpu.CompilerParams(has_side_effects=EFFECT),
    )(*both, *ssems, *rsems, after)
    return res[:nt], res[nt:]


def _forward_copy(lands, ssems, rsems, t, k, pos, sibling):
    px, py, pc = pos
    slot = lands[t].at[4 * px + 2 * py + pc]
    return pltpu.make_async_remote_copy(
        src_ref=slot, dst_ref=slot, send_sem=ssems[t].at[k - 1], recv_sem=rsems[t].at[k - 1],
        device_id=sibling, device_id_type=MESH_ID)


def _fstart(lands, name):
    nt = len(lands)

    def body(*refs):
        lnd = refs[:nt]
        ssems, rsems = refs[nt:2 * nt], refs[2 * nt:3 * nt]
        token = refs[-1]
        x, y, c = lax.axis_index("x"), lax.axis_index("y"), lax.axis_index("c")
        for t in range(nt):
            for k, pos in _peers(x, y, c, FORWARDED):
                _forward_copy(lnd, ssems, rsems, t, k, pos, (x, y, c ^ 1)).start()
        token[...] = jnp.zeros_like(token)

    res = pl.pallas_call(
        body, name=name,
        out_shape=[pltpu.SemaphoreType.DMA((N_PEER,))] * (2 * nt)
        + [pltpu.HBM(b.shape, b.dtype) for b in lands] + [jax.ShapeDtypeStruct((8, 128), F32)],
        in_specs=[HBM] * nt,
        out_specs=[SEM] * (2 * nt) + [HBM] * nt + [pl.BlockSpec(memory_space=pltpu.VMEM)],
        input_output_aliases={i: 2 * nt + i for i in range(nt)},
        compiler_params=pltpu.CompilerParams(has_side_effects=EFFECT),
    )(*[pltpu.with_memory_space_constraint(b, pltpu.HBM) for b in lands])
    return res[:nt], res[nt:2 * nt], res[2 * nt:3 * nt], res[-1][0, 0]


def _fwait(ssems, rsems, lands, after, name):
    nt = len(lands)

    def body(*refs):
        lnd = refs[:nt]
        ss, rs = refs[nt:2 * nt], refs[2 * nt:3 * nt]
        x, y, c = lax.axis_index("x"), lax.axis_index("y"), lax.axis_index("c")
        for t in range(nt):
            for k, pos in _peers(x, y, c, FORWARDED):
                cp = _forward_copy(lnd, ss, rs, t, k, pos, (x, y, c ^ 1))
                cp.wait_send()
                cp.wait_recv()

    return pl.pallas_call(
        body, name=name, out_shape=[pltpu.HBM(b.shape, b.dtype) for b in lands],
        in_specs=[HBM] * nt + [SEM] * (2 * nt) + [ANY], out_specs=[HBM] * nt,
        input_output_aliases={i: i for i in range(nt)},
        compiler_params=pltpu.CompilerParams(has_side_effects=EFFECT),
    )(*lands, *ssems, *rsems, after)


def _adam(recv, w, m, v, tr, name):
    _, r, c = w.shape
    n_part = recv.shape[0]
    c1 = np.float32(1.0 - ADAM_B1 ** ADAM_STEP)
    c2 = np.float32(1.0 - ADAM_B2 ** ADAM_STEP)

    def body(r_ref, w_ref, m_ref, v_ref, g_ref, d_ref, mo_ref, vo_ref):
        g = r_ref[0].astype(F32)
        for s in range(1, n_part):
            g = g + r_ref[s].astype(F32)
        mn = ADAM_B1 * m_ref[0] + (1.0 - ADAM_B1) * g
        vn = ADAM_B2 * v_ref[0] + (1.0 - ADAM_B2) * (g * g)
        g_ref[0] = g
        mo_ref[0] = mn
        vo_ref[0] = vn
        d_ref[0] = -ADAM_LR * ((mn / c1) / (jnp.sqrt(vn / c2) + ADAM_EPS) + ADAM_WD * w_ref[0])

    one = pl.BlockSpec((1, tr, c), lambda i: (0, i, 0))
    return pl.pallas_call(
        body, grid=(r // tr,), in_specs=[pl.BlockSpec((n_part, tr, c), lambda i: (0, i, 0)), one, one, one],
        out_specs=[one] * 4, out_shape=[jax.ShapeDtypeStruct((1, r, c), F32)] * 4,
        name=name, compiler_params=_cp())(recv, w, m, v)


def _local_step(x, target, sp, need, ahead, emit):
    g = {}
    x0, h1 = x, _rms_fwd(x, sp["norm_ffn1"], "ffn1_norm")
    act1, saved1 = _ffn_up(h1, lambda kind, a: need("w" + kind + "1", a),
                           lambda kind, a: ahead("w" + kind + "1", a), "ffn1")
    x1, h2 = _mm_nn_resnorm([(act1, saved1[-1])], x0, 0.5, sp["norm_mix"], 512, "ffn1_down_norm",
                            after=ahead("win_a", act1))
    win_a, win_d = need("win_a", h2), need("win_d", h2)
    conv_w8, wout = need("conv_w8", h2), need("wout", h2)
    z_at = _mm_nn(h2, win_a, F32, 1024, 768, "mix_in_attn")
    z_dn = _mm_nn(h2, win_d, F32, 1024, 768, "mix_in_dn")

    conv, qkvn, bg = _dn_pre_fwd(z_dn, conv_w8, sp["alog_row"], sp["dtb_row"], "dn_pre")
    *prep, tinv = _dn_prep_fwd(qkvn, bg, "dn_prep")
    o_dn, states = _dn_scan_fwd(prep, "dn_scan")
    dn_b = _dn_post_fwd(o_dn, z_dn, sp["dn_norm"], "dn_post")
    attn_f, attn_b, lse = _attn_fwd(z_at, "attn_fwd", after=ahead("wg2", dn_b))

    x2, h3 = _mm_nn_resnorm([(attn_b, wout[:AW]), (dn_b, wout[AW:])], x1, 1.0, sp["norm_ffn2"], 512,
                            "mix_out_norm")
    act2, saved2 = _ffn_up(h3, lambda kind, a: need("w" + kind + "2", a),
                           lambda kind, a: ahead("w" + kind + "2", a), "ffn2")
    y3 = _mm_nn(act2, saved2[-1], F32, 1024, 1024, "ffn2_down")

    dx3, dys3, g["norm_final"], loss8 = _loss_bwd(x2, y3, sp["norm_final"], target, "loss")
    dx2, dx2b, g["norm_ffn2"] = _ffn_bwd(
        x2, sp["norm_ffn2"], saved2, dx3, dys3, 1.0,
        lambda kind, dw: emit(kind + "2", {"w" + kind + "2": dw}), "ffn2b")

    zero = emit("wout", {"wout": jnp.concatenate([_mm_tn(attn_b, dx2b, BF, 512, 1024, "mix_out_dw_a"),
                                                  _mm_tn(dn_b, dx2b, BF, 512, 1024, "mix_out_dw_d")], axis=0)})
    dmix = _mm_nt(dx2b, wout, F32, 1024, 1024, "mix_out_dx", after=zero)

    dz_at = jnp.concatenate(_attn_bwd(z_at, dmix, attn_f, lse, "attn_bwd"), axis=1)

    do_dn, dgate, g["dn_norm"] = _dn_post_bwd(o_dn, z_dn, sp["dn_norm"], dmix, "dn_post_b")
    cts = _dn_scan_bwd(prep, states, do_dn, "dn_scan_b")
    dqn, dkn, dvn, dbg = _dn_prep_bwd(qkvn, bg, tinv, cts, "dn_prep_b")
    dconv, dbd, g["alog_row"], g["dtb_row"] = _dn_pre_bwd(
        conv, z_dn, sp["alog_row"], sp["dtb_row"], dqn, dkn, dvn, dbg, "dn_pre_b")
    draw, dconv_w8 = _dn_conv_bwd(dconv, z_dn, conv_w8, "dn_conv_b")
    dz_dn = _dz_dn_assemble(draw, dgate, dbd, "dn_dz")

    zero = emit("win", {"win_a": _mm_tn(h2, dz_at, BF, 512, 768, "mix_in_dw_a"),
                        "win_d": _mm_tn(h2, dz_dn, BF, 512, 768, "mix_in_dw_d"), "conv_w8": dconv_w8})
    dh2 = _mm_nt(dz_at, win_a, F32, 1024, 1024, "mix_in_dx_a", after=zero)
    dh2d = _mm_nt(dz_dn, win_d, F32, 1024, 1024, "mix_in_dx_d")
    (dx1, dys1), _, g["norm_mix"] = _rms_bwd2(x1, sp["norm_mix"], dh2, dh2d, dx2, "mix_dnorm")
    dx0, _, g["norm_ffn1"] = _ffn_bwd(
        x0, sp["norm_ffn1"], saved1, dx1, dys1, 1.0,
        lambda kind, dw: emit(kind + "1", {"w" + kind + "1": dw}), "ffn1b")
    return loss8[0, 0], dx0, g


def _rms_bwd2(x, gain, dh_a, dh_b, dres, name):
    tm = 512
    row = pl.BlockSpec((tm, D), lambda i: (i, 0))
    gspec = pl.BlockSpec((1, D), lambda i: (0, 0))

    def body(x_ref, g_ref, da_ref, db_ref, dres_ref, dx_ref, dxs_ref, dg_ref):
        i = pl.program_id(0)
        xv = x_ref[...]
        r = lax.rsqrt(jnp.mean(xv * xv, axis=-1, keepdims=True) + NORM_EPS)
        xh = xv * r
        dhv = da_ref[...] + db_ref[...]
        part = jnp.sum(dhv * xh, axis=0, keepdims=True)

        @pl.when(i == 0)
        def _():
            dg_ref[...] = part

        @pl.when(i > 0)
        def _():
            dg_ref[...] += part

        dxh = dhv * g_ref[...]
        dx = r * (dxh - xh * jnp.mean(dxh * xh, axis=-1, keepdims=True)) + dres_ref[...]
        dx_ref[...] = dx
        dxs_ref[...] = (0.5 * dx).astype(BF)

    dx, dxs, dg = pl.pallas_call(
        body, grid=(T // tm,), in_specs=[row, gspec, row, row, row], out_specs=[row, row, gspec],
        out_shape=[jax.ShapeDtypeStruct((T, D), F32), jax.ShapeDtypeStruct((T, D), BF),
                   jax.ShapeDtypeStruct((1, D), F32)],
        name=name, compiler_params=_cp())(x, gain, dh_a, dh_b, dres)
    return (dx, dxs), None, dg


def _cols_from_shards(gathered):
    n, r, c = gathered.shape
    return jnp.transpose(gathered, (1, 0, 2)).reshape(r, n * c)


def _shards_from_cols(full, dtype):
    r, nc = full.shape
    return jnp.transpose(full.reshape(r, N_DEV, nc // N_DEV), (1, 0, 2)).astype(dtype)


def _lane_row(vec4):
    return jnp.zeros((1, 128), F32).at[:, DN_H:2 * DN_H].set(vec4.astype(F32))


WEIGHT_SOURCES = {"wg1": "gate1", "wu1": "up1", "wd1": "down1", "win_a": "w_in", "win_d": "w_in",
                  "conv_w8": "conv_w", "wout": "w_out", "wg2": "gate2", "wu2": "up2", "wd2": "down2"}
TRANSPOSED = ("gate1", "up1", "gate2", "up2")


def _build_weights(name, gath):
    if name in ("wg1", "wu1", "wd1", "wg2", "wu2", "wd2"):
        return {name: gath[WEIGHT_SOURCES[name]].reshape(F, D)}
    if name in ("win_a", "win_d"):
        w_in = _cols_from_shards(gath["w_in"])
        c0 = 3 * AW + 3 * DNW
        wp = jnp.concatenate([w_in[:, :c0], w_in[:, c0 + 2 * DN_H:], w_in[:, c0:c0 + 2 * DN_H],
                              jnp.zeros((D, ZP - IN_COLS), w_in.dtype)], axis=1)
        return {"win_a": wp[:, :ZA], "win_d": wp[:, ZA:]}
    if name == "wout":
        return {name: gath["w_out"].reshape(D, D)}
    conv = _cols_from_shards(gath["conv_w"])
    return {"conv_w8": jnp.concatenate([conv, jnp.zeros((4, 3 * DNW), F32)], axis=0)}


def _small_params(norm_ffn1, norm_mix, norm_ffn2, norm_final, a_log, dt_bias, dn_norm):
    return {"norm_ffn1": norm_ffn1, "norm_mix": norm_mix, "norm_ffn2": norm_ffn2,
            "norm_final": norm_final.reshape(1, D), "alog_row": _lane_row(a_log), "dtb_row": _lane_row(dt_bias),
            "dn_norm": dn_norm}


def _grad_slabs(group, g):
    if group[0] in "gud":
        return {WEIGHT_SOURCES["w" + group]: g["w" + group].reshape(N_DEV, F // N_DEV, D)}
    if group == "wout":
        return {"w_out": g["wout"].reshape(N_DEV, D // N_DEV, D)}
    gp = jnp.concatenate([g["win_a"], g["win_d"]], axis=1)
    c0 = 3 * AW + 3 * DNW
    g_in = jnp.concatenate([gp[:, :c0], gp[:, c0 + DNW:c0 + DNW + 2 * DN_H], gp[:, c0:c0 + DNW]], axis=1)
    return {"w_in": _shards_from_cols(g_in, BF), "conv_w": _shards_from_cols(g["conv_w8"][:4], F32)}


SMALL_ROWS = 40


def _small_pack(norm_ffn1, norm_mix, norm_ffn2, norm_final, dn_norm, alog_row, dtb_row, loss=None):
    rows = [a.reshape(8, 128) for a in (norm_ffn1, norm_mix, norm_ffn2, norm_final)]
    loss_row = jnp.zeros((1, 128), F32) if loss is None else jnp.broadcast_to(loss.reshape(1, 1), (1, 128))
    rows += [dn_norm.reshape(1, 128), alog_row, dtb_row, loss_row, jnp.zeros((SMALL_ROWS - 36, 128), F32)]
    return jnp.concatenate(rows, axis=0)


def _small_unpack(pk):
    pk = pk[0]
    return (pk[0:8].reshape(1, D), pk[8:16].reshape(1, D), pk[16:24].reshape(1, D), pk[24:32].reshape(D),
            pk[32:33], pk[33:34, DN_H:2 * DN_H], pk[34:35, DN_H:2 * DN_H])


ADAM_TILE = {"gate1": 256, "up1": 256, "down1": 176, "gate2": 256, "up2": 256, "down2": 176,
             "w_in": 256, "w_out": 128, "conv_w": 4}
BIG = ("gate1", "up1", "down1", "w_in", "w_out", "gate2", "up2", "down2", "conv_w")


def kernel(x, norm_ffn1, ffn1_gate, ffn1_up, ffn1_down, norm_mix, w_in, conv_w, a_log, dt_bias, dn_norm, w_out, norm_ffn2, ffn2_gate, ffn2_up, ffn2_down, norm_final, loss_target, m_norm_ffn1, m_ffn1_gate, m_ffn1_up, m_ffn1_down, m_norm_mix, m_w_in, m_conv_w, m_a_log, m_dt_bias, m_dn_norm, m_w_out, m_norm_ffn2, m_ffn2_gate, m_ffn2_up, m_ffn2_down, m_norm_final, v_norm_ffn1, v_ffn1_gate, v_ffn1_up, v_ffn1_down, v_norm_mix, v_w_in, v_conv_w, v_a_log, v_dt_bias, v_dn_norm, v_w_out, v_norm_ffn2, v_ffn2_gate, v_ffn2_up, v_ffn2_down, v_norm_final):
    w = {"gate1": ffn1_gate, "up1": ffn1_up, "down1": ffn1_down, "w_in": w_in, "w_out": w_out,
         "gate2": ffn2_gate, "up2": ffn2_up, "down2": ffn2_down, "conv_w": conv_w}
    m = {"gate1": m_ffn1_gate, "up1": m_ffn1_up, "down1": m_ffn1_down, "w_in": m_w_in, "w_out": m_w_out,
         "gate2": m_ffn2_gate, "up2": m_ffn2_up, "down2": m_ffn2_down, "conv_w": m_conv_w}
    v = {"gate1": v_ffn1_gate, "up1": v_ffn1_up, "down1": v_ffn1_down, "w_in": v_w_in, "w_out": v_w_out,
         "gate2": v_ffn2_gate, "up2": v_ffn2_up, "down2": v_ffn2_down, "conv_w": v_conv_w}

    me = 4 * lax.axis_index("x") + 2 * lax.axis_index("y") + lax.axis_index("c")
    own_slot = lambda land, mine: lax.dynamic_update_index_in_dim(land, mine, me, 0)

    ag_order = ("gate1", "up1", "down1", "w_in", "conv_w", "w_out", "gate2", "up2", "down2")
    ag_groups = (("gate1",), ("up1",), ("down1",), ("w_in", "conv_w", "w_out"), ("gate2", "up2", "down2"))
    pos = {n: i for i, n in enumerate(ag_order)}

    def shard(n):
        if n == "conv_w":
            return w[n][0]
        return (w[n][0].T if n in TRANSPOSED else w[n][0]).astype(BF)

    ss, rs, thru, land, zero = _xstart([shard(n) for n in ag_order], False, "weights_start", FIRST_HOP)
    gath, built, on_its_way = {}, {}, {}
    group_of = lambda name: [i for i, grp in enumerate(ag_groups) if WEIGHT_SOURCES[name] in grp][0]

    def ahead(name, after):
        gi = group_of(name)
        if WEIGHT_SOURCES[name] in gath or gi in on_its_way:
            return None
        ids = [pos[n] for n in ag_groups[gi]]
        thrus, lands = _xwait([ss[i] for i in ids], [rs[i] for i in ids], [thru[i] for i in ids],
                              [land[i] for i in ids], False, after, "weights_wait%d" % gi, FIRST_HOP)
        fss, frs, lands, token = _fstart(lands, "weights_forward%d" % gi)
        on_its_way[gi] = (thrus, fss, frs, lands)
        return token

    def need(name, after):
        if name not in built:
            if WEIGHT_SOURCES[name] not in gath:
                gi = group_of(name)
                ahead(name, after)
                thrus, fss, frs, lands = on_its_way.pop(gi)
                lands = _fwait(fss, frs, lands, after, "weights_forward_wait%d" % gi)
                for n, t, l in zip(ag_groups[gi], thrus, lands):
                    gath[n] = own_slot(l, t)
            built.update(_build_weights(name, gath))
        return built[name]

    pending = []

    def emit(group, grads):
        slabs = grads if group == "small" else _grad_slabs(group, grads)
        names = list(slabs)
        started = _xstart([slabs[n] for n in names], True, "grads_start_" + group)
        pending.append((group, names) + started[:4])
        return started[4]

    sp = _small_params(norm_ffn1 + zero, norm_mix, norm_ffn2, norm_final, a_log, dt_bias, dn_norm)
    loss_part, dx, g = _local_step(x[0], loss_target[0], sp, need, ahead, emit)
    small = _small_pack(g["norm_ffn1"], g["norm_mix"], g["norm_ffn2"], g["norm_final"], g["dn_norm"],
                        g["alog_row"], g["dtb_row"], loss_part)
    emit("small", {"small": jnp.broadcast_to(small[None], (N_DEV, SMALL_ROWS, 128))})

    pack = lambda a: _small_pack(*a)[None]
    res, after = {}, dx
    for group, names, gss, grs, gthru, gland in pending:
        thrus, lands = _xwait(gss, grs, gthru, gland, True, after, "grads_wait_" + group)
        for n, t, l in zip(names, thrus, lands):
            recv = own_slot(l, lax.dynamic_index_in_dim(t, me, 0, keepdims=False))
            if n == "small":
                res[n] = _adam(
                    recv,
                    pack((norm_ffn1, norm_mix, norm_ffn2, norm_final, dn_norm, _lane_row(a_log), _lane_row(dt_bias))),
                    pack((m_norm_ffn1, m_norm_mix, m_norm_ffn2, m_norm_final, m_dn_norm, _lane_row(m_a_log),
                          _lane_row(m_dt_bias))),
                    pack((v_norm_ffn1, v_norm_mix, v_norm_ffn2, v_norm_final, v_dn_norm, _lane_row(v_a_log),
                          _lane_row(v_dt_bias))),
                    SMALL_ROWS, "adam_small")
            elif n in TRANSPOSED:
                flip = lambda a: jnp.swapaxes(a, 1, 2)
                res[n] = [flip(o) for o in _adam(recv, flip(w[n]), flip(m[n]), flip(v[n]), F // N_DEV // 2,
                                                 "adam_" + n)]
            else:
                res[n] = _adam(recv, w[n], m[n], v[n], ADAM_TILE[n], "adam_" + n)
            after = res[n][0]
    res_s = res["small"]

    loss = res_s[0][0, 35, 0]
    outs = [loss, dx[None]]
    for k in range(4):
        n1, nm, n2, nf, dn, al, dt = _small_unpack(res_s[k])
        big = {n: res[n][k] for n in BIG}
        outs += [n1, big["gate1"], big["up1"], big["down1"], nm, big["w_in"], big["conv_w"], al, dt, dn,
                 big["w_out"], n2, big["gate2"], big["up2"], big["down2"], nf]
    return tuple(outs)
```

```python
import functools

import numpy as np
import jax
import jax.numpy as jnp
from jax import lax
from jax.experimental import pallas as pl
from jax.experimental.pallas import tpu as pltpu

T = 4096
D = 1024
F = 2816
N_DEV = 8
A_HEADS = 8
A_HD = 64
AW = A_HEADS * A_HD
DN_H = 4
DN_HD = 128
DNW = DN_H * DN_HD
CH = 64
PAIR = 2 * CH
ZA = 3 * AW
ZD = 3 * DNW + DNW + 256
ZP = ZA + ZD
BD_BLK = (3 * DNW + DNW) // 128
IN_COLS = 3592
PATTERNS = ((128, 1), (512, 4), (2048, 16))
NORM_EPS = 1e-6
L2_EPS = 1e-6
ADAM_LR, ADAM_B1, ADAM_B2, ADAM_EPS, ADAM_WD, ADAM_STEP = 0.001, 0.9, 0.999, 1e-08, 0.01, 10
VMEM_LIMIT = 56 * 1024 * 1024
NEG = -1e30

BF = jnp.bfloat16
F32 = jnp.float32
NN = (((1,), (0,)), ((), ()))
NT = (((1,), (1,)), ((), ()))
TN = (((0,), (0,)), ((), ()))
HI = lax.Precision.HIGHEST
MESH_ID = pl.DeviceIdType.MESH
ANY = pl.BlockSpec(memory_space=pl.ANY)


def _cp():
    return pltpu.CompilerParams(vmem_limit_bytes=VMEM_LIMIT)


def _dg(a, b, dims):
    return lax.dot_general(a, b, dims, preferred_element_type=F32)


def _hdot(a, b):
    return lax.dot_general(a, b, NN, precision=HI, preferred_element_type=F32)


def _make_bdot(dims, da_dims, da_swap, db_dims, db_swap):
    @jax.custom_vjp
    def f(a, b):
        return _dg(a.astype(BF), b.astype(BF), dims)

    def fwd(a, b):
        return f(a, b), (a, b)

    def bwd(res, g):
        a, b = res
        gb, ab, bb = g.astype(BF), a.astype(BF), b.astype(BF)
        da = _dg(bb, gb, da_dims) if da_swap else _dg(gb, bb, da_dims)
        db = _dg(gb, ab, db_dims) if db_swap else _dg(ab, gb, db_dims)
        return da, db

    f.defvjp(fwd, bwd)
    return f


_bdot_nn = _make_bdot(NN, NT, False, TN, False)
_bdot_nt = _make_bdot(NT, NN, False, TN, True)
_bdot_tn = _make_bdot(TN, NT, True, NN, False)


def _iota(shape, dim):
    return lax.broadcasted_iota(jnp.int32, shape, dim)


def _col(x, idx):
    return jnp.sum(jnp.where(_iota(x.shape, 1) == idx, x, 0.0), axis=1, keepdims=True)


def _mm_nn(a, b, out_dtype, tm, tn, name):
    m, k = a.shape
    n = b.shape[1]

    def body(a_ref, b_ref, o_ref):
        o_ref[...] = _dg(a_ref[...], b_ref[...], NN).astype(out_dtype)

    return pl.pallas_call(
        body, grid=(m // tm, n // tn),
        in_specs=[pl.BlockSpec((tm, k), lambda i, j: (i, 0)), pl.BlockSpec((k, tn), lambda i, j: (0, j))],
        out_specs=pl.BlockSpec((tm, tn), lambda i, j: (i, j)),
        out_shape=jax.ShapeDtypeStruct((m, n), out_dtype), name=name, compiler_params=_cp())(a, b)


def _tie(body, after):
    if after is None:
        return body, [], []
    return (lambda tok_ref, *refs: body(*refs)), [ANY], [after.reshape(1, 1)]


def _mm_nt(a, b, out_dtype, tm, tb, name, after=None):
    m, c = a.shape
    kb = b.shape[0]

    def body(a_ref, b_ref, o_ref):
        o_ref[...] = _dg(a_ref[...], b_ref[...], NT).astype(out_dtype)

    body, tspec, tok = _tie(body, after)
    return pl.pallas_call(
        body, grid=(m // tm, kb // tb),
        in_specs=tspec + [pl.BlockSpec((tm, c), lambda i, j: (i, 0)), pl.BlockSpec((tb, c), lambda i, j: (j, 0))],
        out_specs=pl.BlockSpec((tm, tb), lambda i, j: (i, j)),
        out_shape=jax.ShapeDtypeStruct((m, kb), out_dtype), name=name, compiler_params=_cp())(*tok, a, b)


def _mm_tn(a, b, out_dtype, ta, tb, name, after=None):
    m, ka = a.shape
    nb = b.shape[1]

    def body(a_ref, b_ref, o_ref):
        o_ref[...] = _dg(a_ref[...], b_ref[...], TN).astype(out_dtype)

    body, tspec, tok = _tie(body, after)
    return pl.pallas_call(
        body, grid=(ka // ta, nb // tb),
        in_specs=tspec + [pl.BlockSpec((m, ta), lambda i, j: (0, i)), pl.BlockSpec((m, tb), lambda i, j: (0, j))],
        out_specs=pl.BlockSpec((ta, tb), lambda i, j: (i, j)),
        out_shape=jax.ShapeDtypeStruct((ka, nb), out_dtype), name=name, compiler_params=_cp())(*tok, a, b)


def _mm_nn2(a1, a2, b1, b2, tm, tn, name, after=None):
    m, k = a1.shape
    n = b1.shape[1]

    def body(a1_ref, a2_ref, b1_ref, b2_ref, o_ref):
        o_ref[...] = _dg(a1_ref[...], b1_ref[...], NN) + _dg(a2_ref[...], b2_ref[...], NN)

    body, tspec, tok = _tie(body, after)
    arow = pl.BlockSpec((tm, k), lambda i, j: (i, 0))
    bcol = pl.BlockSpec((k, tn), lambda i, j: (0, j))
    return pl.pallas_call(
        body, grid=(m // tm, n // tn), in_specs=tspec + [arow, arow, bcol, bcol],
        out_specs=pl.BlockSpec((tm, tn), lambda i, j: (i, j)),
        out_shape=jax.ShapeDtypeStruct((m, n), F32), name=name, compiler_params=_cp())(*tok, a1, a2, b1, b2)


def _mm_nt_swiglu(h, wu_t, gate, tm, tb, name, after=None):
    m, c = h.shape
    kb = wu_t.shape[0]

    def body(h_ref, w_ref, g_ref, u_ref, a_ref):
        u = _dg(h_ref[...], w_ref[...], NT)
        g = g_ref[...].astype(F32)
        u_ref[...] = u.astype(BF)
        a_ref[...] = (g * jax.nn.sigmoid(g) * u).astype(BF)

    body, tspec, tok = _tie(body, after)
    tile = pl.BlockSpec((tm, tb), lambda i, j: (i, j))
    return pl.pallas_call(
        body, grid=(m // tm, kb // tb),
        in_specs=tspec + [pl.BlockSpec((tm, c), lambda i, j: (i, 0)), pl.BlockSpec((tb, c), lambda i, j: (j, 0)),
                          tile],
        out_specs=[tile, tile], out_shape=[jax.ShapeDtypeStruct((m, kb), BF)] * 2,
        name=name, compiler_params=_cp())(*tok, h, wu_t, gate)


def _mm_nt_dswiglu(dys, wd, gate, up, tm, tb, name, after=None):
    m, c = dys.shape
    kb = wd.shape[0]

    def body(d_ref, w_ref, g_ref, u_ref, dg_ref, du_ref):
        da = _dg(d_ref[...], w_ref[...], NT)
        g = g_ref[...].astype(F32)
        u = u_ref[...].astype(F32)
        s = jax.nn.sigmoid(g)
        dg_ref[...] = (da * u * (s * (1.0 + g * (1.0 - s)))).astype(BF)
        du_ref[...] = (da * (g * s)).astype(BF)

    body, tspec, tok = _tie(body, after)
    tile = pl.BlockSpec((tm, tb), lambda i, j: (i, j))
    return pl.pallas_call(
        body, grid=(m // tm, kb // tb),
        in_specs=tspec + [pl.BlockSpec((tm, c), lambda i, j: (i, 0)), pl.BlockSpec((tb, c), lambda i, j: (j, 0)),
                          tile, tile],
        out_specs=[tile, tile], out_shape=[jax.ShapeDtypeStruct((m, kb), BF)] * 2,
        name=name, compiler_params=_cp())(*tok, dys, wd, gate, up)


def _rms_fwd(x, gain, name):
    tm = 512
    row = pl.BlockSpec((tm, D), lambda i: (i, 0))

    def body(x_ref, g_ref, h_ref):
        xv = x_ref[...]
        r = lax.rsqrt(jnp.mean(xv * xv, axis=-1, keepdims=True) + NORM_EPS)
        h_ref[...] = (xv * r * g_ref[...]).astype(BF)

    return pl.pallas_call(
        body, grid=(T // tm,), in_specs=[row, pl.BlockSpec((1, D), lambda i: (0, 0))], out_specs=row,
        out_shape=jax.ShapeDtypeStruct((T, D), BF), name=name, compiler_params=_cp())(x, gain)


def _rms_bwd(x, gain, dh, dres, alpha_out, name):
    tm = 512
    row = pl.BlockSpec((tm, D), lambda i: (i, 0))
    gspec = pl.BlockSpec((1, D), lambda i: (0, 0))

    def body(x_ref, g_ref, dh_ref, dres_ref, dx_ref, dxs_ref, dg_ref):
        i = pl.program_id(0)
        xv = x_ref[...]
        r = lax.rsqrt(jnp.mean(xv * xv, axis=-1, keepdims=True) + NORM_EPS)
        xh = xv * r
        dhv = dh_ref[...].astype(F32)
        part = jnp.sum(dhv * xh, axis=0, keepdims=True)

        @pl.when(i == 0)
        def _():
            dg_ref[...] = part

        @pl.when(i > 0)
        def _():
            dg_ref[...] += part

        dxh = dhv * g_ref[...]
        dx = r * (dxh - xh * jnp.mean(dxh * xh, axis=-1, keepdims=True)) + dres_ref[...]
        dx_ref[...] = dx
        dxs_ref[...] = (alpha_out * dx).astype(BF)

    return pl.pallas_call(
        body, grid=(T // tm,), in_specs=[row, gspec, row, row], out_specs=[row, row, gspec],
        out_shape=[jax.ShapeDtypeStruct((T, D), F32), jax.ShapeDtypeStruct((T, D), BF),
                   jax.ShapeDtypeStruct((1, D), F32)],
        name=name, compiler_params=_cp())(x, gain, dh, dres)


def _down_loss_bwd(x_prev, act, wd, gain, target, name):
    tm = 512
    row = pl.BlockSpec((tm, D), lambda i: (i, 0))
    gspec = pl.BlockSpec((1, D), lambda i: (0, 0))
    lspec = pl.BlockSpec((8, 128), lambda i: (0, 0))

    def body(x_ref, a_ref, w_ref, g_ref, t_ref, dx_ref, dxs_ref, dg_ref, loss_ref):
        i = pl.program_id(0)
        xv = x_ref[...] + 0.5 * _dg(a_ref[...], w_ref[...], NN)
        r = lax.rsqrt(jnp.mean(xv * xv, axis=-1, keepdims=True) + NORM_EPS)
        xh = xv * r
        diff = xh * g_ref[...] - t_ref[...]
        lpart = 0.5 * jnp.sum(jnp.mean(diff * diff, axis=-1, keepdims=True), axis=0, keepdims=True)
        dy = diff * (1.0 / D)
        part = jnp.sum(dy * xh, axis=0, keepdims=True)

        @pl.when(i == 0)
        def _():
            dg_ref[...] = part
            loss_ref[...] = jnp.broadcast_to(lpart, (8, 128))

        @pl.when(i > 0)
        def _():
            dg_ref[...] += part
            loss_ref[...] += jnp.broadcast_to(lpart, (8, 128))

        dxh = dy * g_ref[...]
        dx = r * (dxh - xh * jnp.mean(dxh * xh, axis=-1, keepdims=True))
        dx_ref[...] = dx
        dxs_ref[...] = (0.5 * dx).astype(BF)

    return pl.pallas_call(
        body, grid=(T // tm,),
        in_specs=[row, pl.BlockSpec((tm, F), lambda i: (i, 0)), pl.BlockSpec((F, D), lambda i: (0, 0)), gspec, row],
        out_specs=[row, row, gspec, lspec],
        out_shape=[jax.ShapeDtypeStruct((T, D), F32), jax.ShapeDtypeStruct((T, D), BF),
                   jax.ShapeDtypeStruct((1, D), F32), jax.ShapeDtypeStruct((8, 128), F32)],
        name=name, compiler_params=_cp())(x_prev, act, wd, gain, target)


def _mm_nn_resnorm(pairs, x_prev, alpha, gain, tm, name, after=None):
    m = x_prev.shape[0]
    n = len(pairs)

    def body(*refs):
        x_ref, g_ref, xo_ref, h_ref = refs[2 * n:]
        y = _dg(refs[0][...], refs[n][...], NN)
        for i in range(1, n):
            y = y + _dg(refs[i][...], refs[n + i][...], NN)
        xv = x_ref[...] + alpha * y
        xo_ref[...] = xv
        r = lax.rsqrt(jnp.mean(xv * xv, axis=-1, keepdims=True) + NORM_EPS)
        h_ref[...] = (xv * r * g_ref[...]).astype(BF)

    body, tspec, tok = _tie(body, after)
    row = pl.BlockSpec((tm, D), lambda i: (i, 0))
    return pl.pallas_call(
        body, grid=(m // tm,),
        in_specs=tspec + [pl.BlockSpec((tm, a.shape[1]), lambda i: (i, 0)) for a, _ in pairs]
        + [pl.BlockSpec(b.shape, lambda i: (0, 0)) for _, b in pairs] + [row, pl.BlockSpec((1, D), lambda i: (0, 0))],
        out_specs=[row, row],
        out_shape=[jax.ShapeDtypeStruct((m, D), F32), jax.ShapeDtypeStruct((m, D), BF)],
        name=name, compiler_params=_cp())(*tok, *[a for a, _ in pairs], *[b for _, b in pairs], x_prev, gain)


def _ffn_up(h, need, ahead, tag):
    wg = need("g", h)
    gate = _mm_nt(h, wg, BF, 1024, 1408, tag + "_gate")
    wu = need("u", gate)
    up, act = _mm_nt_swiglu(h, wu, gate, 1024, 1408, tag + "_up_act", after=ahead("d", wu))
    wd = need("d", up)
    return act, (h, gate, up, act, wg, wu, wd)


def _ffn_bwd(x_in, gain, saved, dxo, dys, alpha_out, emit, tag):
    h, gate, up, act, wg, wu, wd = saved
    sent = emit("d", _mm_tn(act, dys, BF, 256, 1024, tag + "_dwd"))
    dgate, dup = _mm_nt_dswiglu(dys, wd, gate, up, 1024, 1408, tag + "_dact", after=sent)
    sent = emit("g", _mm_tn(dgate, h, BF, 256, 1024, tag + "_dwg"))
    sent = emit("u", _mm_tn(dup, h, BF, 256, 1024, tag + "_dwu", after=sent))
    dh = _mm_nn2(dgate, dup, wg, wu, 1024, 512, tag + "_dh", after=sent)
    dx, dxs, dgain = _rms_bwd(x_in, gain, dh, dxo, alpha_out, tag + "_dnorm")
    return dx, dxs, dgain


SLAB = 2048
N_SLAB = T // SLAB
N_PAIR = A_HEADS // 2


def _pair_masks():
    lane = _iota((128, 128), 1)
    return lane < A_HD, lane >= A_HD


def _slope_table():
    h = 2 * jnp.arange(N_PAIR)[:, None] + jnp.minimum(jnp.arange(8), 1)[None, :]
    return jnp.broadcast_to((2.0 ** (-(h + 1).astype(F32)))[:, :, None], (N_PAIR, 8, 128))


def _rows(ref, start, d):
    if d == 1:
        return ref[pl.ds(start, 128), :]
    return ref[pl.ds(start, 128, stride=d), :]


def _put_rows(ref, start, d, val):
    if d == 1:
        ref[pl.ds(start, 128), :] = val
    else:
        ref[pl.ds(start, 128, stride=d), :] = val


def _units(d):
    return [(r, b, r + 128 * d * b) for r in range(d) for b in range(SLAB // (128 * d))]


def _band(d, prev_valid):
    qi = _iota((128, 256), 0)
    kj = _iota((128, 256), 1)
    steps = qi + 128 - kj
    valid = (steps >= 0) & (steps <= 128) & (prev_valid | (kj >= 128))
    return valid, (steps * d).astype(F32)


def _attn_fwd(z_at, name, after=None):
    def body(sl_ref, q_ref, kc_ref, kp_ref, vc_ref, vp_ref, of_ref, ob_ref, lse_ref, m_s, l_s, a_s):
        n = pl.program_id(1)
        lo, hi = _pair_masks()
        slopes = (sl_ref[0, 0:1, 0:1], sl_ref[0, 1:2, 0:1])

        def unit(d, start, b, first, carry):
            q = _rows(q_ref, start, d).astype(BF)
            kcur, vcur = _rows(kc_ref, start, d).astype(BF), _rows(vc_ref, start, d).astype(BF)
            if b > 0:
                (kprev, vprev), prev_valid = carry, True
            else:
                pstart = start + SLAB - 128 * d
                kprev, vprev = _rows(kp_ref, pstart, d).astype(BF), _rows(vp_ref, pstart, d).astype(BF)
                prev_valid = n > 0
            kcat = jnp.concatenate([kprev, kcur], axis=0)
            vcat = jnp.concatenate([vprev, vcur], axis=0)
            valid, dist = _band(d, prev_valid)
            ms, ls, pvs = [], [], []
            for e in range(2):
                qm = jnp.where(lo if e == 0 else hi, q, jnp.zeros_like(q))
                s = _dg(qm, kcat, NT) * (A_HD ** -0.5) - slopes[e] * dist
                s = jnp.where(valid, s, NEG)
                m = jnp.max(s, axis=1, keepdims=True)
                p = jnp.exp(s - m)
                ms.append(m)
                ls.append(jnp.sum(p, axis=1, keepdims=True))
                pvs.append(_dg(p.astype(BF), vcat, NN))
            m_u = jnp.where(lo, ms[0], ms[1])
            l_u = jnp.where(lo, ls[0], ls[1])
            a_u = jnp.where(lo, pvs[0], pvs[1])
            if first:
                m_n, l_n, a_n = m_u, l_u, a_u
            else:
                m_o = _rows(m_s, start, d)
                m_n = jnp.maximum(m_o, m_u)
                c_o = jnp.exp(m_o - m_n)
                c_u = jnp.exp(m_u - m_n)
                l_n = _rows(l_s, start, d) * c_o + l_u * c_u
                a_n = _rows(a_s, start, d) * c_o + a_u * c_u
            _put_rows(m_s, start, d, m_n)
            _put_rows(l_s, start, d, l_n)
            _put_rows(a_s, start, d, a_n)
            return kcur, vcur

        for pi, (_, d) in enumerate(PATTERNS):
            carry = None
            for r, b, start in _units(d):
                carry = unit(d, start, b, pi == 0, carry)
        l = l_s[...]
        out = a_s[...] / l
        of_ref[...] = out
        ob_ref[...] = out.astype(BF)
        lse_ref[...] = m_s[...] + jnp.log(l)

    body, tspec, tok = _tie(body, after)
    cur = lambda c: pl.BlockSpec((SLAB, 128), lambda j, n: (n, c * N_PAIR + j))
    prv = lambda c: pl.BlockSpec((SLAB, 128), lambda j, n: (jnp.maximum(n - 1, 0), c * N_PAIR + j))
    out = pl.BlockSpec((SLAB, 128), lambda j, n: (n, j))
    return pl.pallas_call(
        body, grid=(N_PAIR, N_SLAB),
        in_specs=tspec + [pl.BlockSpec((1, 8, 128), lambda j, n: (j, 0, 0)), cur(0), cur(1), prv(1), cur(2), prv(2)],
        out_specs=[out, out, out],
        out_shape=[jax.ShapeDtypeStruct((T, AW), F32), jax.ShapeDtypeStruct((T, AW), BF),
                   jax.ShapeDtypeStruct((T, AW), F32)],
        scratch_shapes=[pltpu.VMEM((SLAB, 128), F32)] * 3,
        name=name, compiler_params=_cp())(*tok, _slope_table(), z_at, z_at, z_at, z_at, z_at)


def _attn_bwd(z_at, dout, out, lse, name):
    def body(sl_ref, q_ref, kc_ref, kp_ref, vc_ref, vp_ref, do_ref, o_ref, lse_ref, dq_ref, dk_ref, dv_ref,
             dq_s, dk_s, dv_s, ck_s, cv_s):
        step = pl.program_id(1)
        n = N_SLAB - 1 - step
        lo, hi = _pair_masks()
        slopes = (sl_ref[0, 0:1, 0:1], sl_ref[0, 1:2, 0:1])

        @pl.when(step == 0)
        def _():
            ck_s[...] = jnp.zeros_like(ck_s)
            cv_s[...] = jnp.zeros_like(cv_s)

        dk_s[...] = ck_s[...]
        dv_s[...] = cv_s[...]
        ck_s[...] = jnp.zeros_like(ck_s)
        cv_s[...] = jnp.zeros_like(cv_s)

        def add_rows(ref, start, d, val):
            _put_rows(ref, start, d, _rows(ref, start, d) + val)

        def unit(d, start, b, first, carry):
            q = _rows(q_ref, start, d).astype(BF)
            do_f = _rows(do_ref, start, d)
            do = do_f.astype(BF)
            prod = do_f * _rows(o_ref, start, d)
            lse_u = _rows(lse_ref, start, d)
            kcur, vcur = _rows(kc_ref, start, d).astype(BF), _rows(vc_ref, start, d).astype(BF)
            if b > 0:
                (kprev, vprev), prev_valid = carry, True
            else:
                pstart = start + SLAB - 128 * d
                kprev, vprev = _rows(kp_ref, pstart, d).astype(BF), _rows(vp_ref, pstart, d).astype(BF)
                prev_valid = n > 0
            kcat = jnp.concatenate([kprev, kcur], axis=0)
            vcat = jnp.concatenate([vprev, vcur], axis=0)
            valid, dist = _band(d, prev_valid)
            masks = (lo, hi)
            qms = [jnp.where(msk, q, jnp.zeros_like(q)) for msk in masks]
            doms = [jnp.where(msk, do, jnp.zeros_like(do)) for msk in masks]
            deltas = [jnp.sum(jnp.where(msk, prod, 0.0), axis=1, keepdims=True) for msk in masks]
            ss = [_dg(qm, kcat, NT) * (A_HD ** -0.5) - sl * dist for qm, sl in zip(qms, slopes)]
            dps = [_dg(dom, vcat, NT) for dom in doms]
            ps = [jnp.where(valid, jnp.exp(jnp.where(valid, s, NEG) - lse_u[:, 64 * e:64 * e + 1]), 0.0)
                  for e, s in enumerate(ss)]
            dss = [(p * (dp - delta)).astype(BF) for p, dp, delta in zip(ps, dps, deltas)]
            pbs = [p.astype(BF) for p in ps]
            dqs = [_dg(ds, kcat, NN) for ds in dss]
            dkc = (_dg(dss[0], qms[0], TN) + _dg(dss[1], qms[1], TN)) * (A_HD ** -0.5)
            dvc = _dg(pbs[0], doms[0], TN) + _dg(pbs[1], doms[1], TN)
            dq_u = jnp.where(lo, dqs[0], dqs[1]) * (A_HD ** -0.5)
            if first:
                _put_rows(dq_s, start, d, dq_u)
            else:
                add_rows(dq_s, start, d, dq_u)
            add_rows(dk_s, start, d, dkc[128:])
            add_rows(dv_s, start, d, dvc[128:])
            if b > 0:
                add_rows(dk_s, start - 128 * d, d, dkc[:128])
                add_rows(dv_s, start - 128 * d, d, dvc[:128])
            else:
                pstart = start + SLAB - 128 * d
                add_rows(ck_s, pstart, d, dkc[:128])
                add_rows(cv_s, pstart, d, dvc[:128])
            return kcur, vcur

        for pi, (_, d) in enumerate(PATTERNS):
            carry = None
            for r, b, start in _units(d):
                carry = unit(d, start, b, pi == 0, carry)
        dq_ref[...] = dq_s[...].astype(BF)
        dk_ref[...] = dk_s[...].astype(BF)
        dv_ref[...] = dv_s[...].astype(BF)

    rev = lambda n: N_SLAB - 1 - n
    cur = lambda c: pl.BlockSpec((SLAB, 128), lambda j, n: (rev(n), c * N_PAIR + j))
    prv = lambda c: pl.BlockSpec((SLAB, 128), lambda j, n: (jnp.maximum(rev(n) - 1, 0), c * N_PAIR + j))
    one = pl.BlockSpec((SLAB, 128), lambda j, n: (rev(n), j))
    return pl.pallas_call(
        body, grid=(N_PAIR, N_SLAB),
        in_specs=[pl.BlockSpec((1, 8, 128), lambda j, n: (j, 0, 0)), cur(0), cur(1), prv(1), cur(2), prv(2),
                  one, one, one],
        out_specs=[one, one, one], out_shape=[jax.ShapeDtypeStruct((T, AW), BF)] * 3,
        scratch_shapes=[pltpu.VMEM((SLAB, 128), F32)] * 5,
        name=name, compiler_params=_cp())(_slope_table(), z_at, z_at, z_at, z_at, z_at, dout, out, lse)


def _silu(x):
    return x * jax.nn.sigmoid(x)


def _qk_math(c):
    s = _silu(c)
    return s * lax.rsqrt(jnp.sum(s * s, axis=-1, keepdims=True) + L2_EPS)


def _softplus(x):
    return jnp.maximum(x, 0.0) + jnp.log(1.0 + jnp.exp(-jnp.abs(x)))


def _gate_math(bd, alog_row, dtb_row):
    rows = bd.shape[0]
    lane = _iota(bd.shape, 1)
    beta = jax.nn.sigmoid(bd)
    g = jnp.where((lane >= DN_H) & (lane < 2 * DN_H), -jnp.exp(alog_row) * _softplus(bd + dtb_row), 0.0)
    ri = _iota((rows, rows), 0)
    ci = _iota((rows, rows), 1)
    same = (ri // CH) == (ci // CH)
    li = _iota((128, 128), 0)
    lj = _iota((128, 128), 1)
    to_next_group = jnp.where((lj == li + DN_H) & (li >= DN_H) & (li < 2 * DN_H), 1.0, 0.0)
    gc = _hdot(jnp.where(same & (ci <= ri), 1.0, 0.0), g)
    glast = _hdot(_hdot(jnp.where(same, 1.0, 0.0), g), to_next_group)
    return jnp.where(lane < DN_H, beta, 0.0) + gc + glast


def _shift_down(cur, halo, s):
    if s == 0:
        return cur
    rolled = pltpu.roll(cur, s, 0)
    hr = pltpu.roll(halo, s, 0)
    head = jnp.where(_iota(hr.shape, 0) < s, hr, rolled[:8])
    return jnp.concatenate([head, rolled[8:]], axis=0)


def _shift_up(cur, halo, s):
    if s == 0:
        return cur
    rows = cur.shape[0]
    rolled = pltpu.roll(cur, rows - s, 0)
    hr = pltpu.roll(halo, 8 - s, 0)
    tail = jnp.where(_iota(hr.shape, 0) >= 8 - s, hr, rolled[rows - 8:])
    return jnp.concatenate([rolled[:rows - 8], tail], axis=0)


def _dn_pre_fwd(z_dn, conv_w8, alog_row, dtb_row, name):
    tm = 256
    wq = 3 * DNW

    def body(raw_ref, halo_ref, bd_ref, w_ref, al_ref, dt_ref, conv_ref, qkv_ref, bg_ref):
        i = pl.program_id(0)
        cur = raw_ref[...]
        halo = jnp.where(i > 0, halo_ref[...], 0.0)
        w = w_ref[...]
        conv = jnp.zeros((tm, wq), F32)
        for j in range(4):
            conv = conv + _shift_down(cur, halo, 3 - j) * w[j:j + 1, :]
        conv_ref[...] = conv
        for blk in range(3 * DN_H):
            sl = slice(128 * blk, 128 * blk + 128)
            c = conv[:, sl]
            qkv_ref[:, sl] = _qk_math(c) if blk < 2 * DN_H else _silu(c)
        bg_ref[...] = _gate_math(bd_ref[...], al_ref[...], dt_ref[...])

    one = pl.BlockSpec((1, 128), lambda i: (0, 0))
    return pl.pallas_call(
        body, grid=(T // tm,),
        in_specs=[pl.BlockSpec((tm, wq), lambda i: (i, 0)),
                  pl.BlockSpec((8, wq), lambda i: (jnp.maximum(i * (tm // 8) - 1, 0), 0)),
                  pl.BlockSpec((tm, 128), lambda i: (i, BD_BLK)),
                  pl.BlockSpec((8, wq), lambda i: (0, 0)), one, one],
        out_specs=[pl.BlockSpec((tm, wq), lambda i: (i, 0)), pl.BlockSpec((tm, wq), lambda i: (i, 0)),
                   pl.BlockSpec((tm, 128), lambda i: (i, 0))],
        out_shape=[jax.ShapeDtypeStruct((T, wq), F32), jax.ShapeDtypeStruct((T, wq), F32),
                   jax.ShapeDtypeStruct((T, 128), F32)],
        name=name, compiler_params=_cp())(z_dn, z_dn, z_dn, conv_w8, alog_row, dtb_row)


def _dn_pre_bwd(conv, z_dn, alog_row, dtb_row, dqn, dkn, dvn, dbg, name):
    tm = 256
    wq = 3 * DNW

    def body(conv_ref, bd_ref, al_ref, dt_ref, dq_ref, dk_ref, dv_ref, dbg_ref,
             dconv_ref, dbd_ref, dal_ref, ddt_ref):
        i = pl.program_id(0)
        for blk in range(3 * DN_H):
            sl = slice(128 * blk, 128 * blk + 128)
            src = (dq_ref, dk_ref, dv_ref)[blk // DN_H]
            ct = src[:, 128 * (blk % DN_H):128 * (blk % DN_H) + 128]
            fn = _qk_math if blk < 2 * DN_H else _silu
            _, vjp = jax.vjp(fn, conv_ref[:, sl])
            dconv_ref[:, sl] = vjp(ct)[0]
        _, vjp = jax.vjp(_gate_math, bd_ref[...], al_ref[...], dt_ref[...])
        dbd, dal, ddt = vjp(dbg_ref[...])
        dbd_ref[...] = dbd

        @pl.when(i == 0)
        def _():
            dal_ref[...] = dal
            ddt_ref[...] = ddt

        @pl.when(i > 0)
        def _():
            dal_ref[...] += dal
            ddt_ref[...] += ddt

    one = pl.BlockSpec((1, 128), lambda i: (0, 0))
    row = pl.BlockSpec((tm, wq), lambda i: (i, 0))
    hd = pl.BlockSpec((tm, DNW), lambda i: (i, 0))
    st = pl.BlockSpec((tm, 128), lambda i: (i, 0))
    return pl.pallas_call(
        body, grid=(T // tm,),
        in_specs=[row, pl.BlockSpec((tm, 128), lambda i: (i, BD_BLK)), one, one, hd, hd, hd, st],
        out_specs=[row, st, one, one],
        out_shape=[jax.ShapeDtypeStruct((T, wq), F32), jax.ShapeDtypeStruct((T, 128), F32),
                   jax.ShapeDtypeStruct((1, 128), F32), jax.ShapeDtypeStruct((1, 128), F32)],
        name=name, compiler_params=_cp())(conv, z_dn, alog_row, dtb_row, dqn, dkn, dvn, dbg)


def _dn_conv_bwd(dconv, z_dn, conv_w8, name):
    tm = 256
    wq = 3 * DNW
    last = T // tm - 1

    def body(dc_ref, dcn_ref, raw_ref, halo_ref, w_ref, draw_ref, dw_ref):
        i = pl.program_id(0)
        dc = dc_ref[...]
        nxt = jnp.where(i < last, dcn_ref[...], 0.0)
        cur = raw_ref[...]
        halo = jnp.where(i > 0, halo_ref[...], 0.0)
        w = w_ref[...]
        draw = jnp.zeros((tm, wq), F32)
        rows = []
        for j in range(4):
            draw = draw + _shift_up(dc, nxt, 3 - j) * w[j:j + 1, :]
            rows.append(jnp.sum(dc * _shift_down(cur, halo, 3 - j), axis=0, keepdims=True))
        draw_ref[...] = draw
        part = jnp.concatenate(rows + [jnp.zeros((4, wq), F32)], axis=0)

        @pl.when(i == 0)
        def _():
            dw_ref[...] = part

        @pl.when(i > 0)
        def _():
            dw_ref[...] += part

    row = pl.BlockSpec((tm, wq), lambda i: (i, 0))
    return pl.pallas_call(
        body, grid=(T // tm,),
        in_specs=[row, pl.BlockSpec((8, wq), lambda i: (jnp.minimum((i + 1) * (tm // 8), T // 8 - 1), 0)),
                  row, pl.BlockSpec((8, wq), lambda i: (jnp.maximum(i * (tm // 8) - 1, 0), 0)),
                  pl.BlockSpec((8, wq), lambda i: (0, 0))],
        out_specs=[row, pl.BlockSpec((8, wq), lambda i: (0, 0))],
        out_shape=[jax.ShapeDtypeStruct((T, wq), F32), jax.ShapeDtypeStruct((8, wq), F32)],
        name=name, compiler_params=_cp())(dconv, dconv, z_dn, z_dn, conv_w8)


def _h3(a, b, dims=NN):
    return lax.dot_general(a, b, dims, precision=lax.Precision.HIGH, preferred_element_type=F32)


@jax.custom_vjp
def _inverse_given(a_mat, tinv):
    return tinv


def _inverse_given_fwd(a_mat, tinv):
    return tinv, tinv


def _inverse_given_bwd(tinv, g):
    return -_h3(tinv, _h3(g, tinv, NT), TN), jnp.zeros_like(tinv)


_inverse_given.defvjp(_inverse_given_fwd, _inverse_given_bwd)


def _prep_head(q, k, v, bgc, h):
    beta = _col(bgc, h)
    gc = jnp.broadcast_to(_col(bgc, DN_H + h), (PAIR, 128))
    glast = jnp.broadcast_to(_col(bgc, 2 * DN_H + h), (PAIR, 128))
    ri = _iota((PAIR, PAIR), 0)
    ci = _iota((PAIR, PAIR), 1)
    same = (ri // CH) == (ci // CH)
    causal = same & (ci <= ri)
    strict = same & (ci < ri)
    eye = ri == ci
    gc_cols = _hdot(jnp.ones((PAIR, PAIR), F32), jnp.where(eye, gc, 0.0))
    decay = jnp.exp(jnp.where(causal, gc - gc_cols, NEG))
    egc = jnp.exp(gc)
    kb = k * beta
    a_mat = jnp.where(strict, _bdot_nt(kb, k) * decay, 0.0)
    qs = q * (DN_HD ** -0.5)
    attn = jnp.where(causal, _bdot_nt(qs, k) * decay, 0.0)
    return a_mat, (v * beta, kb * egc, qs * egc, k * jnp.exp(glast - gc), attn, jnp.exp(glast))


def _prep_tail(tinv, ctx):
    vb, kbe, qg, kdec, attn, decb = ctx
    return _h3(tinv, vb), _h3(tinv, kbe), qg, kdec, attn, decb


def _inverses(a_mats):
    eye = jnp.where(_iota((PAIR, PAIR), 0) == _iota((PAIR, PAIR), 1), 1.0, 0.0)
    ps = [-a for a in a_mats]
    tinvs = [eye + p for p in ps]
    for _ in range(5):
        ps = [_h3(p, p) for p in ps]
        tinvs = [t + _h3(t, p) for t, p in zip(tinvs, ps)]
    return tinvs


def _prep_math(q, k, v, bgc, h, tinv_saved):
    a_mat, ctx = _prep_head(q, k, v, bgc, h)
    return _prep_tail(_inverse_given(a_mat, tinv_saved), ctx)


def _dn_prep_fwd(qkv, bg, name):
    rows = 512
    hd = lambda off: pl.BlockSpec((rows, 128), lambda g, h: (g, off + h))
    out = pl.BlockSpec((rows, 128), lambda g, h: (g, h))

    def body(q_ref, k_ref, v_ref, bg_ref, *outs):
        h = pl.program_id(1)
        spans = [slice(PAIR * pr, PAIR * pr + PAIR) for pr in range(rows // PAIR)]
        heads = [_prep_head(q_ref[rs, :], k_ref[rs, :], v_ref[rs, :], bg_ref[rs, :], h) for rs in spans]
        tinvs = _inverses([a for a, _ in heads])
        for rs, tinv, (_, ctx) in zip(spans, tinvs, heads):
            for o_ref, val in zip(outs, _prep_tail(tinv, ctx) + (tinv,)):
                o_ref[rs, :] = val

    return pl.pallas_call(
        body, grid=(T // rows, DN_H),
        in_specs=[hd(0), hd(DN_H), hd(2 * DN_H), pl.BlockSpec((rows, 128), lambda g, h: (g, 0))],
        out_specs=[out] * 7, out_shape=[jax.ShapeDtypeStruct((T, DNW), F32)] * 7,
        name=name, compiler_params=_cp())(qkv, qkv, qkv, bg)


def _dn_prep_bwd(qkv, bg, tinv, cts, name):
    rows = 512
    hd = lambda off: pl.BlockSpec((rows, 128), lambda g, h: (g, off + h))
    out = pl.BlockSpec((rows, 128), lambda g, h: (g, h))
    st = pl.BlockSpec((rows, 128), lambda g, h: (g, 0))

    def body(q_ref, k_ref, v_ref, bg_ref, ti_ref, c0, c1, c2, c3, c4, c5, dq_ref, dk_ref, dv_ref, dbg_ref):
        h = pl.program_id(1)
        spans = [slice(PAIR * pr, PAIR * pr + PAIR) for pr in range(rows // PAIR)]
        tis = [ti_ref[rs, :] for rs in spans]

        def joint(qs, ks, vs, bs):
            heads = [_prep_head(q, k, v, b, h) for q, k, v, b in zip(qs, ks, vs, bs)]
            return [_prep_tail(_inverse_given(a, ti), ctx) for (a, ctx), ti in zip(heads, tis)]

        _, vjp = jax.vjp(joint, *[[r[rs, :] for rs in spans] for r in (q_ref, k_ref, v_ref, bg_ref)])
        dqs, dks, dvs, dbs = vjp([tuple(c[rs, :] for c in (c0, c1, c2, c3, c4, c5)) for rs in spans])
        for rs, dq, dk, dv in zip(spans, dqs, dks, dvs):
            dq_ref[rs, :] = dq
            dk_ref[rs, :] = dk
            dv_ref[rs, :] = dv
        dbg_all = jnp.concatenate(dbs, axis=0)

        @pl.when(h == 0)
        def _():
            dbg_ref[...] = dbg_all

        @pl.when(h > 0)
        def _():
            dbg_ref[...] += dbg_all

    return pl.pallas_call(
        body, grid=(T // rows, DN_H),
        in_specs=[hd(0), hd(DN_H), hd(2 * DN_H), st] + [out] * 7,
        out_specs=[out, out, out, st],
        out_shape=[jax.ShapeDtypeStruct((T, DNW), F32)] * 3 + [jax.ShapeDtypeStruct((T, 128), F32)],
        name=name, compiler_params=_cp())(qkv, qkv, qkv, bg, tinv, *cts)


def _step_math(s, u, w, qg, kdec, attn, decb, sub):
    vnew = u - _bdot_nn(w, s)
    z = jnp.zeros((CH, 128), F32)
    vfull = jnp.concatenate([vnew, z] if sub == 0 else [z, vnew], axis=0)
    o = _bdot_nn(qg, s) + _bdot_nn(attn, vfull)
    dec = jnp.sum(decb, axis=0, keepdims=True) * (1.0 / CH)
    return s * dec + _bdot_tn(kdec, vnew), o


def _dn_scan_fwd(prep, name):
    npair = T // PAIR
    row = pl.BlockSpec((PAIR, DNW), lambda p: (p, 0))

    def body(u_ref, w_ref, qg_ref, kd_ref, at_ref, db_ref, o_ref, ss_ref, s_ref):
        @pl.when(pl.program_id(0) == 0)
        def _():
            s_ref[...] = jnp.zeros_like(s_ref)

        states = [s_ref[h] for h in range(DN_H)]
        for sub in range(2):
            rs = slice(CH * sub, CH * sub + CH)
            for h in range(DN_H):
                ls = slice(128 * h, 128 * h + 128)
                ss_ref[sub, h] = states[h]
                states[h], o = _step_math(states[h], u_ref[rs, ls], w_ref[rs, ls], qg_ref[rs, ls],
                                          kd_ref[rs, ls], at_ref[rs, ls], db_ref[rs, ls], sub)
                o_ref[rs, ls] = o
        for h in range(DN_H):
            s_ref[h] = states[h]

    return pl.pallas_call(
        body, grid=(npair,), in_specs=[row] * 6,
        out_specs=[row, pl.BlockSpec((2, DN_H, 128, 128), lambda p: (p, 0, 0, 0))],
        out_shape=[jax.ShapeDtypeStruct((T, DNW), F32), jax.ShapeDtypeStruct((T // CH, DN_H, 128, 128), F32)],
        scratch_shapes=[pltpu.VMEM((DN_H, 128, 128), F32)],
        name=name, compiler_params=_cp())(*prep)


def _dn_scan_bwd(prep, states, do, name):
    npair = T // PAIR
    row = pl.BlockSpec((PAIR, DNW), lambda p: (npair - 1 - p, 0))

    def body(u_ref, w_ref, qg_ref, kd_ref, at_ref, db_ref, ss_ref, do_ref, *rest):
        outs, ds_ref = rest[:6], rest[6]

        @pl.when(pl.program_id(0) == 0)
        def _():
            ds_ref[...] = jnp.zeros_like(ds_ref)

        dss = [ds_ref[h] for h in range(DN_H)]
        for sub in (1, 0):
            rs = slice(CH * sub, CH * sub + CH)
            for h in range(DN_H):
                ls = slice(128 * h, 128 * h + 128)
                args = (ss_ref[sub, h],) + tuple(r[rs, ls] for r in (u_ref, w_ref, qg_ref, kd_ref, at_ref, db_ref))
                _, vjp = jax.vjp(functools.partial(_step_math, sub=sub), *args)
                cts = vjp((dss[h], do_ref[rs, ls]))
                dss[h] = cts[0]
                for o_ref, val in zip(outs, cts[1:]):
                    o_ref[rs, ls] = val
        for h in range(DN_H):
            ds_ref[h] = dss[h]

    return pl.pallas_call(
        body, grid=(npair,),
        in_specs=[row] * 6 + [pl.BlockSpec((2, DN_H, 128, 128), lambda p: (npair - 1 - p, 0, 0, 0)), row],
        out_specs=[row] * 6, out_shape=[jax.ShapeDtypeStruct((T, DNW), F32)] * 6,
        scratch_shapes=[pltpu.VMEM((DN_H, 128, 128), F32)],
        name=name, compiler_params=_cp())(*prep, states, do)


def _post_math(o, gate, wrow):
    return o * lax.rsqrt(jnp.mean(o * o, axis=-1, keepdims=True) + NORM_EPS) * wrow * _silu(gate)


def _dn_post_fwd(o, z_dn, dn_norm, name):
    tm = 512
    row = pl.BlockSpec((tm, DNW), lambda i: (i, 0))

    def body(o_ref, g_ref, w_ref, y_ref):
        for h in range(DN_H):
            ls = slice(128 * h, 128 * h + 128)
            y_ref[:, ls] = _post_math(o_ref[:, ls], g_ref[:, ls], w_ref[...]).astype(BF)

    return pl.pallas_call(
        body, grid=(T // tm,),
        in_specs=[row, pl.BlockSpec((tm, DNW), lambda i: (i, 3)), pl.BlockSpec((1, 128), lambda i: (0, 0))],
        out_specs=row, out_shape=jax.ShapeDtypeStruct((T, DNW), BF),
        name=name, compiler_params=_cp())(o, z_dn, dn_norm)


def _dn_post_bwd(o, z_dn, dn_norm, dy, name):
    tm = 512
    row = pl.BlockSpec((tm, DNW), lambda i: (i, 0))
    one = pl.BlockSpec((1, 128), lambda i: (0, 0))

    def body(o_ref, g_ref, w_ref, dy_ref, do_ref, dg_ref, dw_ref):
        i = pl.program_id(0)
        dw = jnp.zeros((1, 128), F32)
        for h in range(DN_H):
            ls = slice(128 * h, 128 * h + 128)
            _, vjp = jax.vjp(_post_math, o_ref[:, ls], g_ref[:, ls], w_ref[...])
            do, dg, dwh = vjp(dy_ref[:, ls].astype(F32))
            do_ref[:, ls] = do
            dg_ref[:, ls] = dg
            dw = dw + dwh

        @pl.when(i == 0)
        def _():
            dw_ref[...] = dw

        @pl.when(i > 0)
        def _():
            dw_ref[...] += dw

    return pl.pallas_call(
        body, grid=(T // tm,),
        in_specs=[row, pl.BlockSpec((tm, DNW), lambda i: (i, 3)), one, pl.BlockSpec((tm, DNW), lambda i: (i, 1))],
        out_specs=[row, row, one],
        out_shape=[jax.ShapeDtypeStruct((T, DNW), F32), jax.ShapeDtypeStruct((T, DNW), F32),
                   jax.ShapeDtypeStruct((1, 128), F32)],
        name=name, compiler_params=_cp())(o, z_dn, dn_norm, dy)


def _dz_dn_assemble(draw, dgate, dbd, name):
    tm = 512

    def body(a_ref, b_ref, c_ref, o_ref):
        o_ref[:, :3 * DNW] = a_ref[...].astype(BF)
        o_ref[:, 3 * DNW:4 * DNW] = b_ref[...].astype(BF)
        o_ref[:, 4 * DNW:4 * DNW + 128] = c_ref[...].astype(BF)
        o_ref[:, 4 * DNW + 128:] = jnp.zeros((tm, 128), BF)

    return pl.pallas_call(
        body, grid=(T // tm,),
        in_specs=[pl.BlockSpec((tm, 3 * DNW), lambda i: (i, 0)), pl.BlockSpec((tm, DNW), lambda i: (i, 0)),
                  pl.BlockSpec((tm, 128), lambda i: (i, 0))],
        out_specs=pl.BlockSpec((tm, ZD), lambda i: (i, 0)),
        out_shape=jax.ShapeDtypeStruct((T, ZD), BF), name=name, compiler_params=_cp())(draw, dgate, dbd)


HBM = pl.BlockSpec(memory_space=pltpu.HBM)
SEM = pl.BlockSpec(memory_space=pltpu.SEMAPHORE)
EFFECT = pltpu.SideEffectType.DATAFLOW_SIDE_EFFECTING
N_PEER = N_DEV - 1


ALL_PEERS = (1, 2, 4, 3, 5, 6, 7)
FIRST_HOP = (1, 2, 4, 6)
FORWARDED = (2, 4, 6)


def _peers(x, y, c, ks=ALL_PEERS):
    return [(k, (x ^ (k >> 2), y ^ ((k >> 1) & 1), c ^ (k & 1))) for k in ks]


def _exchange_copy(ins, lands, ssems, rsems, scatter, t, k, pos, me):
    px, py, pc = pos
    src = ins[t].at[4 * px + 2 * py + pc] if scatter else ins[t]
    return pltpu.make_async_remote_copy(
        src_ref=src, dst_ref=lands[t].at[me], send_sem=ssems[t].at[k - 1], recv_sem=rsems[t].at[k - 1],
        device_id=pos, device_id_type=MESH_ID)


def _xstart(bufs, scatter, name, ks=ALL_PEERS):
    nt = len(bufs)
    lands = [lax.empty((N_DEV,) + tuple(b.shape[1:] if scatter else b.shape), b.dtype) for b in bufs]

    def body(*refs):
        ins, lnd = refs[:nt], refs[nt:2 * nt]
        ssems, rsems = refs[2 * nt:3 * nt], refs[3 * nt:4 * nt]
        token = refs[-1]
        x, y, c = lax.axis_index("x"), lax.axis_index("y"), lax.axis_index("c")
        me = 4 * x + 2 * y + c
        for t in range(nt):
            for k, pos in _peers(x, y, c, ks):
                _exchange_copy(ins, lnd, ssems, rsems, scatter, t, k, pos, me).start()
        token[...] = jnp.zeros_like(token)

    both = list(bufs) + lands
    res = pl.pallas_call(
        body, name=name,
        out_shape=[pltpu.SemaphoreType.DMA((N_PEER,))] * (2 * nt)
        + [pltpu.HBM(b.shape, b.dtype) for b in both] + [jax.ShapeDtypeStruct((8, 128), F32)],
        in_specs=[HBM] * (2 * nt),
        out_specs=[SEM] * (2 * nt) + [HBM] * (2 * nt) + [pl.BlockSpec(memory_space=pltpu.VMEM)],
        input_output_aliases={i: 2 * nt + i for i in range(2 * nt)},
        compiler_params=pltpu.CompilerParams(has_side_effects=EFFECT),
    )(*[pltpu.with_memory_space_constraint(b, pltpu.HBM) for b in both])
    return res[:nt], res[nt:2 * nt], res[2 * nt:3 * nt], res[3 * nt:4 * nt], res[-1][0, 0]


def _xwait(ssems, rsems, thrus, lands, scatter, after, name, ks=ALL_PEERS):
    nt = len(lands)

    def body(*refs):
        ins, lnd = refs[:nt], refs[nt:2 * nt]
        ss, rs = refs[2 * nt:3 * nt], refs[3 * nt:4 * nt]
        x, y, c = lax.axis_index("x"), lax.axis_index("y"), lax.axis_index("c")
        me = 4 * x + 2 * y + c
        for t in range(nt):
            for k, pos in _peers(x, y, c, ks):
                cp = _exchange_copy(ins, lnd, ss, rs, scatter, t, k, pos, me)
                cp.wait_send()
                cp.wait_recv()

    both = list(thrus) + list(lands)
    res = pl.pallas_call(
        body, name=name, out_shape=[pltpu.HBM(b.shape, b.dtype) for b in both],
        in_specs=[HBM] * (2 * nt) + [SEM] * (2 * nt) + [ANY], out_specs=[HBM] * (2 * nt),
        input_output_aliases={i: i for i in range(2 * nt)},
        compiler_params=pltpu.CompilerParams(has_side_effects=EFFECT),
    )(*both, *ssems, *rsems, after)
    return res[:nt], res[nt:]


def _forward_copy(lands, ssems, rsems, t, k, pos, sibling):
    px, py, pc = pos
    slot = lands[t].at[4 * px + 2 * py + pc]
    return pltpu.make_async_remote_copy(
        src_ref=slot, dst_ref=slot, send_sem=ssems[t].at[k - 1], recv_sem=rsems[t].at[k - 1],
        device_id=sibling, device_id_type=MESH_ID)


def _fstart(lands, name):
    nt = len(lands)

    def body(*refs):
        lnd = refs[:nt]
        ssems, rsems = refs[nt:2 * nt], refs[2 * nt:3 * nt]
        token = refs[-1]
        x, y, c = lax.axis_index("x"), lax.axis_index("y"), lax.axis_index("c")
        for t in range(nt):
            for k, pos in _peers(x, y, c, FORWARDED):
                _forward_copy(lnd, ssems, rsems, t, k, pos, (x, y, c ^ 1)).start()
        token[...] = jnp.zeros_like(token)

    res = pl.pallas_call(
        body, name=name,
        out_shape=[pltpu.SemaphoreType.DMA((N_PEER,))] * (2 * nt)
        + [pltpu.HBM(b.shape, b.dtype) for b in lands] + [jax.ShapeDtypeStruct((8, 128), F32)],
        in_specs=[HBM] * nt,
        out_specs=[SEM] * (2 * nt) + [HBM] * nt + [pl.BlockSpec(memory_space=pltpu.VMEM)],
        input_output_aliases={i: 2 * nt + i for i in range(nt)},
        compiler_params=pltpu.CompilerParams(has_side_effects=EFFECT),
    )(*[pltpu.with_memory_space_constraint(b, pltpu.HBM) for b in lands])
    return res[:nt], res[nt:2 * nt], res[2 * nt:3 * nt], res[-1][0, 0]


def _fwait(ssems, rsems, lands, after, name):
    nt = len(lands)

    def body(*refs):
        lnd = refs[:nt]
        ss, rs = refs[nt:2 * nt], refs[2 * nt:3 * nt]
        x, y, c = lax.axis_index("x"), lax.axis_index("y"), lax.axis_index("c")
        for t in range(nt):
            for k, pos in _peers(x, y, c, FORWARDED):
                cp = _forward_copy(lnd, ss, rs, t, k, pos, (x, y, c ^ 1))
                cp.wait_send()
                cp.wait_recv()

    return pl.pallas_call(
        body, name=name, out_shape=[pltpu.HBM(b.shape, b.dtype) for b in lands],
        in_specs=[HBM] * nt + [SEM] * (2 * nt) + [ANY], out_specs=[HBM] * nt,
        input_output_aliases={i: i for i in range(nt)},
        compiler_params=pltpu.CompilerParams(has_side_effects=EFFECT),
    )(*lands, *ssems, *rsems, after)


def _adam(recv, w, m, v, tr, name):
    _, r, c = w.shape
    n_part = recv.shape[0]
    c1 = np.float32(1.0 - ADAM_B1 ** ADAM_STEP)
    c2 = np.float32(1.0 - ADAM_B2 ** ADAM_STEP)

    def body(r_ref, w_ref, m_ref, v_ref, g_ref, d_ref, mo_ref, vo_ref):
        g = r_ref[0].astype(F32)
        for s in range(1, n_part):
            g = g + r_ref[s].astype(F32)
        mn = ADAM_B1 * m_ref[0] + (1.0 - ADAM_B1) * g
        vn = ADAM_B2 * v_ref[0] + (1.0 - ADAM_B2) * (g * g)
        g_ref[0] = g
        mo_ref[0] = mn
        vo_ref[0] = vn
        d_ref[0] = -ADAM_LR * ((mn / c1) / (jnp.sqrt(vn / c2) + ADAM_EPS) + ADAM_WD * w_ref[0])

    one = pl.BlockSpec((1, tr, c), lambda i: (0, i, 0))
    return pl.pallas_call(
        body, grid=(r // tr,), in_specs=[pl.BlockSpec((n_part, tr, c), lambda i: (0, i, 0)), one, one, one],
        out_specs=[one] * 4, out_shape=[jax.ShapeDtypeStruct((1, r, c), F32)] * 4,
        name=name, compiler_params=_cp())(recv, w, m, v)


def _local_step(x, target, sp, need, ahead, emit):
    g = {}
    x0, h1 = x, _rms_fwd(x, sp["norm_ffn1"], "ffn1_norm")
    act1, saved1 = _ffn_up(h1, lambda kind, a: need("w" + kind + "1", a),
                           lambda kind, a: ahead("w" + kind + "1", a), "ffn1")
    x1, h2 = _mm_nn_resnorm([(act1, saved1[-1])], x0, 0.5, sp["norm_mix"], 512, "ffn1_down_norm",
                            after=ahead("win_a", act1))
    win_a, win_d = need("win_a", h2), need("win_d", h2)
    conv_w8, wout = need("conv_w8", h2), need("wout", h2)
    z_at = _mm_nn(h2, win_a, F32, 1024, 768, "mix_in_attn")
    z_dn = _mm_nn(h2, win_d, F32, 1024, 768, "mix_in_dn")

    conv, qkvn, bg = _dn_pre_fwd(z_dn, conv_w8, sp["alog_row"], sp["dtb_row"], "dn_pre")
    *prep, tinv = _dn_prep_fwd(qkvn, bg, "dn_prep")
    o_dn, states = _dn_scan_fwd(prep, "dn_scan")
    dn_b = _dn_post_fwd(o_dn, z_dn, sp["dn_norm"], "dn_post")
    attn_f, attn_b, lse = _attn_fwd(z_at, "attn_fwd", after=ahead("wg2", dn_b))

    x2, h3 = _mm_nn_resnorm([(attn_b, wout[:AW]), (dn_b, wout[AW:])], x1, 1.0, sp["norm_ffn2"], 512,
                            "mix_out_norm")
    act2, saved2 = _ffn_up(h3, lambda kind, a: need("w" + kind + "2", a),
                           lambda kind, a: ahead("w" + kind + "2", a), "ffn2")

    dx3, dys3, g["norm_final"], loss8 = _down_loss_bwd(x2, act2, saved2[-1], sp["norm_final"], target,
                                                       "ffn2_down_loss")
    dx2, dx2b, g["norm_ffn2"] = _ffn_bwd(
        x2, sp["norm_ffn2"], saved2, dx3, dys3, 1.0,
        lambda kind, dw: emit(kind + "2", {"w" + kind + "2": dw}), "ffn2b")

    zero = emit("wout", {"wout": jnp.concatenate([_mm_tn(attn_b, dx2b, BF, 512, 1024, "mix_out_dw_a"),
                                                  _mm_tn(dn_b, dx2b, BF, 512, 1024, "mix_out_dw_d")], axis=0)})
    dmix = _mm_nt(dx2b, wout, F32, 1024, 1024, "mix_out_dx", after=zero)

    dz_at = jnp.concatenate(_attn_bwd(z_at, dmix, attn_f, lse, "attn_bwd"), axis=1)

    do_dn, dgate, g["dn_norm"] = _dn_post_bwd(o_dn, z_dn, sp["dn_norm"], dmix, "dn_post_b")
    cts = _dn_scan_bwd(prep, states, do_dn, "dn_scan_b")
    dqn, dkn, dvn, dbg = _dn_prep_bwd(qkvn, bg, tinv, cts, "dn_prep_b")
    dconv, dbd, g["alog_row"], g["dtb_row"] = _dn_pre_bwd(
        conv, z_dn, sp["alog_row"], sp["dtb_row"], dqn, dkn, dvn, dbg, "dn_pre_b")
    draw, dconv_w8 = _dn_conv_bwd(dconv, z_dn, conv_w8, "dn_conv_b")
    dz_dn = _dz_dn_assemble(draw, dgate, dbd, "dn_dz")

    zero = emit("win", {"win_a": _mm_tn(h2, dz_at, BF, 512, 768, "mix_in_dw_a"),
                        "win_d": _mm_tn(h2, dz_dn, BF, 512, 768, "mix_in_dw_d"), "conv_w8": dconv_w8})
    dh2 = _mm_nt(dz_at, win_a, F32, 1024, 1024, "mix_in_dx_a", after=zero)
    dh2d = _mm_nt(dz_dn, win_d, F32, 1024, 1024, "mix_in_dx_d")
    (dx1, dys1), _, g["norm_mix"] = _rms_bwd2(x1, sp["norm_mix"], dh2, dh2d, dx2, "mix_dnorm")
    dx0, _, g["norm_ffn1"] = _ffn_bwd(
        x0, sp["norm_ffn1"], saved1, dx1, dys1, 1.0,
        lambda kind, dw: emit(kind + "1", {"w" + kind + "1": dw}), "ffn1b")
    return loss8[0, 0], dx0, g


def _rms_bwd2(x, gain, dh_a, dh_b, dres, name):
    tm = 512
    row = pl.BlockSpec((tm, D), lambda i: (i, 0))
    gspec = pl.BlockSpec((1, D), lambda i: (0, 0))

    def body(x_ref, g_ref, da_ref, db_ref, dres_ref, dx_ref, dxs_ref, dg_ref):
        i = pl.program_id(0)
        xv = x_ref[...]
        r = lax.rsqrt(jnp.mean(xv * xv, axis=-1, keepdims=True) + NORM_EPS)
        xh = xv * r
        dhv = da_ref[...] + db_ref[...]
        part = jnp.sum(dhv * xh, axis=0, keepdims=True)

        @pl.when(i == 0)
        def _():
            dg_ref[...] = part

        @pl.when(i > 0)
        def _():
            dg_ref[...] += part

        dxh = dhv * g_ref[...]
        dx = r * (dxh - xh * jnp.mean(dxh * xh, axis=-1, keepdims=True)) + dres_ref[...]
        dx_ref[...] = dx
        dxs_ref[...] = (0.5 * dx).astype(BF)

    dx, dxs, dg = pl.pallas_call(
        body, grid=(T // tm,), in_specs=[row, gspec, row, row, row], out_specs=[row, row, gspec],
        out_shape=[jax.ShapeDtypeStruct((T, D), F32), jax.ShapeDtypeStruct((T, D), BF),
                   jax.ShapeDtypeStruct((1, D), F32)],
        name=name, compiler_params=_cp())(x, gain, dh_a, dh_b, dres)
    return (dx, dxs), None, dg


def _cols_from_shards(gathered):
    n, r, c = gathered.shape
    return jnp.transpose(gathered, (1, 0, 2)).reshape(r, n * c)


def _shards_from_cols(full, dtype):
    r, nc = full.shape
    return jnp.transpose(full.reshape(r, N_DEV, nc // N_DEV), (1, 0, 2)).astype(dtype)


def _lane_row(vec4):
    return jnp.zeros((1, 128), F32).at[:, DN_H:2 * DN_H].set(vec4.astype(F32))


WEIGHT_SOURCES = {"wg1": "gate1", "wu1": "up1", "wd1": "down1", "win_a": "w_in", "win_d": "w_in",
                  "conv_w8": "conv_w", "wout": "w_out", "wg2": "gate2", "wu2": "up2", "wd2": "down2"}
TRANSPOSED = ("gate1", "up1", "gate2", "up2")


def _build_weights(name, gath):
    if name in ("wg1", "wu1", "wd1", "wg2", "wu2", "wd2"):
        return {name: gath[WEIGHT_SOURCES[name]].reshape(F, D)}
    if name in ("win_a", "win_d"):
        w_in = _cols_from_shards(gath["w_in"])
        c0 = 3 * AW + 3 * DNW
        win_d = jnp.concatenate([w_in[:, ZA:c0], w_in[:, c0 + 2 * DN_H:], w_in[:, c0:c0 + 2 * DN_H],
                                 jnp.zeros((D, ZP - IN_COLS), w_in.dtype)], axis=1)
        return {"win_a": w_in[:, :ZA], "win_d": win_d}
    if name == "wout":
        return {name: gath["w_out"].reshape(D, D)}
    conv = _cols_from_shards(gath["conv_w"])
    return {"conv_w8": jnp.concatenate([conv, jnp.zeros((4, 3 * DNW), F32)], axis=0)}


def _small_params(norm_ffn1, norm_mix, norm_ffn2, norm_final, a_log, dt_bias, dn_norm):
    return {"norm_ffn1": norm_ffn1, "norm_mix": norm_mix, "norm_ffn2": norm_ffn2,
            "norm_final": norm_final.reshape(1, D), "alog_row": _lane_row(a_log), "dtb_row": _lane_row(dt_bias),
            "dn_norm": dn_norm}


def _grad_slabs(group, g):
    if group[0] in "gud":
        return {WEIGHT_SOURCES["w" + group]: g["w" + group].reshape(N_DEV, F // N_DEV, D)}
    if group == "wout":
        return {"w_out": g["wout"].reshape(N_DEV, D // N_DEV, D)}
    gp = jnp.concatenate([g["win_a"], g["win_d"]], axis=1)
    c0 = 3 * AW + 3 * DNW
    g_in = jnp.concatenate([gp[:, :c0], gp[:, c0 + DNW:c0 + DNW + 2 * DN_H], gp[:, c0:c0 + DNW]], axis=1)
    return {"w_in": _shards_from_cols(g_in, BF), "conv_w": _shards_from_cols(g["conv_w8"][:4], F32)}


SMALL_ROWS = 40


def _small_pack(norm_ffn1, norm_mix, norm_ffn2, norm_final, dn_norm, alog_row, dtb_row, loss=None):
    rows = [a.reshape(8, 128) for a in (norm_ffn1, norm_mix, norm_ffn2, norm_final)]
    loss_row = jnp.zeros((1, 128), F32) if loss is None else jnp.broadcast_to(loss.reshape(1, 1), (1, 128))
    rows += [dn_norm.reshape(1, 128), alog_row, dtb_row, loss_row, jnp.zeros((SMALL_ROWS - 36, 128), F32)]
    return jnp.concatenate(rows, axis=0)


def _small_unpack(pk):
    pk = pk[0]
    return (pk[0:8].reshape(1, D), pk[8:16].reshape(1, D), pk[16:24].reshape(1, D), pk[24:32].reshape(D),
            pk[32:33], pk[33:34, DN_H:2 * DN_H], pk[34:35, DN_H:2 * DN_H])


ADAM_TILE = {"gate1": 256, "up1": 256, "down1": 176, "gate2": 256, "up2": 256, "down2": 176,
             "w_in": 256, "w_out": 128, "conv_w": 4}
BIG = ("gate1", "up1", "down1", "w_in", "w_out", "gate2", "up2", "down2", "conv_w")


def kernel(x, norm_ffn1, ffn1_gate, ffn1_up, ffn1_down, norm_mix, w_in, conv_w, a_log, dt_bias, dn_norm, w_out, norm_ffn2, ffn2_gate, ffn2_up, ffn2_down, norm_final, loss_target, m_norm_ffn1, m_ffn1_gate, m_ffn1_up, m_ffn1_down, m_norm_mix, m_w_in, m_conv_w, m_a_log, m_dt_bias, m_dn_norm, m_w_out, m_norm_ffn2, m_ffn2_gate, m_ffn2_up, m_ffn2_down, m_norm_final, v_norm_ffn1, v_ffn1_gate, v_ffn1_up, v_ffn1_down, v_norm_mix, v_w_in, v_conv_w, v_a_log, v_dt_bias, v_dn_norm, v_w_out, v_norm_ffn2, v_ffn2_gate, v_ffn2_up, v_ffn2_down, v_norm_final):
    w = {"gate1": ffn1_gate, "up1": ffn1_up, "down1": ffn1_down, "w_in": w_in, "w_out": w_out,
         "gate2": ffn2_gate, "up2": ffn2_up, "down2": ffn2_down, "conv_w": conv_w}
    m = {"gate1": m_ffn1_gate, "up1": m_ffn1_up, "down1": m_ffn1_down, "w_in": m_w_in, "w_out": m_w_out,
         "gate2": m_ffn2_gate, "up2": m_ffn2_up, "down2": m_ffn2_down, "conv_w": m_conv_w}
    v = {"gate1": v_ffn1_gate, "up1": v_ffn1_up, "down1": v_ffn1_down, "w_in": v_w_in, "w_out": v_w_out,
         "gate2": v_ffn2_gate, "up2": v_ffn2_up, "down2": v_ffn2_down, "conv_w": v_conv_w}

    me = 4 * lax.axis_index("x") + 2 * lax.axis_index("y") + lax.axis_index("c")
    own_slot = lambda land, mine: lax.dynamic_update_index_in_dim(land, mine, me, 0)

    ag_order = ("gate1", "up1", "down1", "w_in", "conv_w", "w_out", "gate2", "up2", "down2")
    ag_groups = (("gate1",), ("up1",), ("down1",), ("w_in", "conv_w", "w_out"), ("gate2", "up2", "down2"))
    pos = {n: i for i, n in enumerate(ag_order)}

    def shard(n):
        if n == "conv_w":
            return w[n][0]
        return (w[n][0].T if n in TRANSPOSED else w[n][0]).astype(BF)

    ss, rs, thru, land, zero = _xstart([shard(n) for n in ag_order], False, "weights_start", FIRST_HOP)
    gath, built, on_its_way = {}, {}, {}
    group_of = lambda name: [i for i, grp in enumerate(ag_groups) if WEIGHT_SOURCES[name] in grp][0]

    def ahead(name, after):
        gi = group_of(name)
        if WEIGHT_SOURCES[name] in gath or gi in on_its_way:
            return None
        ids = [pos[n] for n in ag_groups[gi]]
        thrus, lands = _xwait([ss[i] for i in ids], [rs[i] for i in ids], [thru[i] for i in ids],
                              [land[i] for i in ids], False, after, "weights_wait%d" % gi, FIRST_HOP)
        fss, frs, lands, token = _fstart(lands, "weights_forward%d" % gi)
        on_its_way[gi] = (thrus, fss, frs, lands)
        return token

    def need(name, after):
        if name not in built:
            if WEIGHT_SOURCES[name] not in gath:
                gi = group_of(name)
                ahead(name, after)
                thrus, fss, frs, lands = on_its_way.pop(gi)
                lands = _fwait(fss, frs, lands, after, "weights_forward_wait%d" % gi)
                for n, t, l in zip(ag_groups[gi], thrus, lands):
                    gath[n] = own_slot(l, t)
            built.update(_build_weights(name, gath))
        return built[name]

    pending = []

    def emit(group, grads):
        slabs = grads if group == "small" else _grad_slabs(group, grads)
        names = list(slabs)
        started = _xstart([slabs[n] for n in names], True, "grads_start_" + group)
        pending.append((group, names) + started[:4])
        return started[4]

    sp = _small_params(norm_ffn1 + zero, norm_mix, norm_ffn2, norm_final, a_log, dt_bias, dn_norm)
    loss_part, dx, g = _local_step(x[0], loss_target[0], sp, need, ahead, emit)
    small = _small_pack(g["norm_ffn1"], g["norm_mix"], g["norm_ffn2"], g["norm_final"], g["dn_norm"],
                        g["alog_row"], g["dtb_row"], loss_part)
    emit("small", {"small": jnp.broadcast_to(small[None], (N_DEV, SMALL_ROWS, 128))})

    pack = lambda a: _small_pack(*a)[None]
    res, after = {}, dx
    for group, names, gss, grs, gthru, gland in pending:
        thrus, lands = _xwait(gss, grs, gthru, gland, True, after, "grads_wait_" + group)
        for n, t, l in zip(names, thrus, lands):
            recv = own_slot(l, lax.dynamic_index_in_dim(t, me, 0, keepdims=False))
            if n == "small":
                res[n] = _adam(
                    recv,
                    pack((norm_ffn1, norm_mix, norm_ffn2, norm_final, dn_norm, _lane_row(a_log), _lane_row(dt_bias))),
                    pack((m_norm_ffn1, m_norm_mix, m_norm_ffn2, m_norm_final, m_dn_norm, _lane_row(m_a_log),
                          _lane_row(m_dt_bias))),
                    pack((v_norm_ffn1, v_norm_mix, v_norm_ffn2, v_norm_final, v_dn_norm, _lane_row(v_a_log),
                          _lane_row(v_dt_bias))),
                    SMALL_ROWS, "adam_small")
            elif n in TRANSPOSED:
                flip = lambda a: jnp.swapaxes(a, 1, 2)
                res[n] = [flip(o) for o in _adam(recv, flip(w[n]), flip(m[n]), flip(v[n]), F // N_DEV // 2,
                                                 "adam_" + n)]
            else:
                res[n] = _adam(recv, w[n], m[n], v[n], ADAM_TILE[n], "adam_" + n)
            after = res[n][0]
    res_s = res["small"]

    loss = res_s[0][0, 35, 0]
    outs = [loss, dx[None]]
    for k in range(4):
        n1, nm, n2, nf, dn, al, dt = _small_unpack(res_s[k])
        big = {n: res[n][k] for n in BIG}
        outs += [n1, big["gate1"], big["up1"], big["down1"], nm, big["w_in"], big["conv_w"], al, dt, dn,
                 big["w_out"], n2, big["gate2"], big["up2"], big["down2"], nf]
    return tuple(outs)
```

```python
import functools

import numpy as np
import jax
import jax.numpy as jnp
from jax import lax
from jax.experimental import pallas as pl
from jax.experimental.pallas import tpu as pltpu

T = 4096
D = 1024
F = 2816
N_DEV = 8
A_HEADS = 8
A_HD = 64
AW = A_HEADS * A_HD
DN_H = 4
DN_HD = 128
DNW = DN_H * DN_HD
CH = 64
PAIR = 2 * CH
ZA = 3 * AW
ZD = 3 * DNW + DNW + 256
ZP = ZA + ZD
BD_BLK = (3 * DNW + DNW) // 128
IN_COLS = 3592
PATTERNS = ((128, 1), (512, 4), (2048, 16))
NORM_EPS = 1e-6
L2_EPS = 1e-6
ADAM_LR, ADAM_B1, ADAM_B2, ADAM_EPS, ADAM_WD, ADAM_STEP = 0.001, 0.9, 0.999, 1e-08, 0.01, 10
VMEM_LIMIT = 56 * 1024 * 1024
NEG = -1e30

BF = jnp.bfloat16
F32 = jnp.float32
NN = (((1,), (0,)), ((), ()))
NT = (((1,), (1,)), ((), ()))
TN = (((0,), (0,)), ((), ()))
HI = lax.Precision.HIGHEST
MESH_ID = pl.DeviceIdType.MESH
ANY = pl.BlockSpec(memory_space=pl.ANY)


def _cp():
    return pltpu.CompilerParams(vmem_limit_bytes=VMEM_LIMIT)


def _dg(a, b, dims):
    return lax.dot_general(a, b, dims, preferred_element_type=F32)


def _hdot(a, b):
    return lax.dot_general(a, b, NN, precision=HI, preferred_element_type=F32)


def _make_bdot(dims, da_dims, da_swap, db_dims, db_swap):
    @jax.custom_vjp
    def f(a, b):
        return _dg(a.astype(BF), b.astype(BF), dims)

    def fwd(a, b):
        return f(a, b), (a, b)

    def bwd(res, g):
        a, b = res
        gb, ab, bb = g.astype(BF), a.astype(BF), b.astype(BF)
        da = _dg(bb, gb, da_dims) if da_swap else _dg(gb, bb, da_dims)
        db = _dg(gb, ab, db_dims) if db_swap else _dg(ab, gb, db_dims)
        return da, db

    f.defvjp(fwd, bwd)
    return f


_bdot_nn = _make_bdot(NN, NT, False, TN, False)
_bdot_nt = _make_bdot(NT, NN, False, TN, True)
_bdot_tn = _make_bdot(TN, NT, True, NN, False)


def _iota(shape, dim):
    return lax.broadcasted_iota(jnp.int32, shape, dim)


def _col(x, idx):
    return jnp.sum(jnp.where(_iota(x.shape, 1) == idx, x, 0.0), axis=1, keepdims=True)


def _mm_nn(a, b, out_dtype, tm, tn, name):
    m, k = a.shape
    n = b.shape[1]

    def body(a_ref, b_ref, o_ref):
        o_ref[...] = _dg(a_ref[...], b_ref[...], NN).astype(out_dtype)

    return pl.pallas_call(
        body, grid=(m // tm, n // tn),
        in_specs=[pl.BlockSpec((tm, k), lambda i, j: (i, 0)), pl.BlockSpec((k, tn), lambda i, j: (0, j))],
        out_specs=pl.BlockSpec((tm, tn), lambda i, j: (i, j)),
        out_shape=jax.ShapeDtypeStruct((m, n), out_dtype), name=name, compiler_params=_cp())(a, b)


def _tie(body, after):
    if after is None:
        return body, [], []
    return (lambda tok_ref, *refs: body(*refs)), [ANY], [after.reshape(1, 1)]


def _mm_nt(a, b, out_dtype, tm, tb, name, after=None):
    m, c = a.shape
    kb = b.shape[0]

    def body(a_ref, b_ref, o_ref):
        o_ref[...] = _dg(a_ref[...], b_ref[...], NT).astype(out_dtype)

    body, tspec, tok = _tie(body, after)
    return pl.pallas_call(
        body, grid=(m // tm, kb // tb),
        in_specs=tspec + [pl.BlockSpec((tm, c), lambda i, j: (i, 0)), pl.BlockSpec((tb, c), lambda i, j: (j, 0))],
        out_specs=pl.BlockSpec((tm, tb), lambda i, j: (i, j)),
        out_shape=jax.ShapeDtypeStruct((m, kb), out_dtype), name=name, compiler_params=_cp())(*tok, a, b)


def _mm_tn(a, b, out_dtype, ta, tb, name, after=None):
    m, ka = a.shape
    nb = b.shape[1]

    def body(a_ref, b_ref, o_ref):
        o_ref[...] = _dg(a_ref[...], b_ref[...], TN).astype(out_dtype)

    body, tspec, tok = _tie(body, after)
    return pl.pallas_call(
        body, grid=(ka // ta, nb // tb),
        in_specs=tspec + [pl.BlockSpec((m, ta), lambda i, j: (0, i)), pl.BlockSpec((m, tb), lambda i, j: (0, j))],
        out_specs=pl.BlockSpec((ta, tb), lambda i, j: (i, j)),
        out_shape=jax.ShapeDtypeStruct((ka, nb), out_dtype), name=name, compiler_params=_cp())(*tok, a, b)


def _mm_nn2(a1, a2, b1, b2, tm, tn, name, after=None):
    m, k = a1.shape
    n = b1.shape[1]

    def body(a1_ref, a2_ref, b1_ref, b2_ref, o_ref):
        o_ref[...] = _dg(a1_ref[...], b1_ref[...], NN) + _dg(a2_ref[...], b2_ref[...], NN)

    body, tspec, tok = _tie(body, after)
    arow = pl.BlockSpec((tm, k), lambda i, j: (i, 0))
    bcol = pl.BlockSpec((k, tn), lambda i, j: (0, j))
    return pl.pallas_call(
        body, grid=(m // tm, n // tn), in_specs=tspec + [arow, arow, bcol, bcol],
        out_specs=pl.BlockSpec((tm, tn), lambda i, j: (i, j)),
        out_shape=jax.ShapeDtypeStruct((m, n), F32), name=name, compiler_params=_cp())(*tok, a1, a2, b1, b2)


def _mm_nt_swiglu(h, wu_t, gate, tm, tb, name, after=None):
    m, c = h.shape
    kb = wu_t.shape[0]

    def body(h_ref, w_ref, g_ref, u_ref, a_ref):
        u = _dg(h_ref[...], w_ref[...], NT)
        g = g_ref[...].astype(F32)
        u_ref[...] = u.astype(BF)
        a_ref[...] = (g * jax.nn.sigmoid(g) * u).astype(BF)

    body, tspec, tok = _tie(body, after)
    tile = pl.BlockSpec((tm, tb), lambda i, j: (i, j))
    return pl.pallas_call(
        body, grid=(m // tm, kb // tb),
        in_specs=tspec + [pl.BlockSpec((tm, c), lambda i, j: (i, 0)), pl.BlockSpec((tb, c), lambda i, j: (j, 0)),
                          tile],
        out_specs=[tile, tile], out_shape=[jax.ShapeDtypeStruct((m, kb), BF)] * 2,
        name=name, compiler_params=_cp())(*tok, h, wu_t, gate)


def _mm_nt_dswiglu(dys, wd, gate, up, tm, tb, name, after=None):
    m, c = dys.shape
    kb = wd.shape[0]

    def body(d_ref, w_ref, g_ref, u_ref, dg_ref, du_ref):
        da = _dg(d_ref[...], w_ref[...], NT)
        g = g_ref[...].astype(F32)
        u = u_ref[...].astype(F32)
        s = jax.nn.sigmoid(g)
        dg_ref[...] = (da * u * (s * (1.0 + g * (1.0 - s)))).astype(BF)
        du_ref[...] = (da * (g * s)).astype(BF)

    body, tspec, tok = _tie(body, after)
    tile = pl.BlockSpec((tm, tb), lambda i, j: (i, j))
    return pl.pallas_call(
        body, grid=(m // tm, kb // tb),
        in_specs=tspec + [pl.BlockSpec((tm, c), lambda i, j: (i, 0)), pl.BlockSpec((tb, c), lambda i, j: (j, 0)),
                          tile, tile],
        out_specs=[tile, tile], out_shape=[jax.ShapeDtypeStruct((m, kb), BF)] * 2,
        name=name, compiler_params=_cp())(*tok, dys, wd, gate, up)


def _rms_fwd(x, gain, name):
    tm = 512
    row = pl.BlockSpec((tm, D), lambda i: (i, 0))

    def body(x_ref, g_ref, h_ref):
        xv = x_ref[...]
        r = lax.rsqrt(jnp.mean(xv * xv, axis=-1, keepdims=True) + NORM_EPS)
        h_ref[...] = (xv * r * g_ref[...]).astype(BF)

    return pl.pallas_call(
        body, grid=(T // tm,), in_specs=[row, pl.BlockSpec((1, D), lambda i: (0, 0))], out_specs=row,
        out_shape=jax.ShapeDtypeStruct((T, D), BF), name=name, compiler_params=_cp())(x, gain)


def _rms_bwd(x, gain, dh, dres, alpha_out, name):
    tm = 512
    row = pl.BlockSpec((tm, D), lambda i: (i, 0))
    gspec = pl.BlockSpec((1, D), lambda i: (0, 0))

    def body(x_ref, g_ref, dh_ref, dres_ref, dx_ref, dxs_ref, dg_ref):
        i = pl.program_id(0)
        xv = x_ref[...]
        r = lax.rsqrt(jnp.mean(xv * xv, axis=-1, keepdims=True) + NORM_EPS)
        xh = xv * r
        dhv = dh_ref[...].astype(F32)
        part = jnp.sum(dhv * xh, axis=0, keepdims=True)

        @pl.when(i == 0)
        def _():
            dg_ref[...] = part

        @pl.when(i > 0)
        def _():
            dg_ref[...] += part

        dxh = dhv * g_ref[...]
        dx = r * (dxh - xh * jnp.mean(dxh * xh, axis=-1, keepdims=True)) + dres_ref[...]
        dx_ref[...] = dx
        dxs_ref[...] = (alpha_out * dx).astype(BF)

    return pl.pallas_call(
        body, grid=(T // tm,), in_specs=[row, gspec, row, row], out_specs=[row, row, gspec],
        out_shape=[jax.ShapeDtypeStruct((T, D), F32), jax.ShapeDtypeStruct((T, D), BF),
                   jax.ShapeDtypeStruct((1, D), F32)],
        name=name, compiler_params=_cp())(x, gain, dh, dres)


def _down_loss_bwd(x_prev, act, wd, gain, target, name):
    tm = 512
    row = pl.BlockSpec((tm, D), lambda i: (i, 0))
    gspec = pl.BlockSpec((1, D), lambda i: (0, 0))
    lspec = pl.BlockSpec((8, 128), lambda i: (0, 0))

    def body(x_ref, a_ref, w_ref, g_ref, t_ref, dx_ref, dxs_ref, dg_ref, loss_ref):
        i = pl.program_id(0)
        xv = x_ref[...] + 0.5 * _dg(a_ref[...], w_ref[...], NN)
        r = lax.rsqrt(jnp.mean(xv * xv, axis=-1, keepdims=True) + NORM_EPS)
        xh = xv * r
        diff = xh * g_ref[...] - t_ref[...]
        lpart = 0.5 * jnp.sum(jnp.mean(diff * diff, axis=-1, keepdims=True), axis=0, keepdims=True)
        dy = diff * (1.0 / D)
        part = jnp.sum(dy * xh, axis=0, keepdims=True)

        @pl.when(i == 0)
        def _():
            dg_ref[...] = part
            loss_ref[...] = jnp.broadcast_to(lpart, (8, 128))

        @pl.when(i > 0)
        def _():
            dg_ref[...] += part
            loss_ref[...] += jnp.broadcast_to(lpart, (8, 128))

        dxh = dy * g_ref[...]
        dx = r * (dxh - xh * jnp.mean(dxh * xh, axis=-1, keepdims=True))
        dx_ref[...] = dx
        dxs_ref[...] = (0.5 * dx).astype(BF)

    return pl.pallas_call(
        body, grid=(T // tm,),
        in_specs=[row, pl.BlockSpec((tm, F), lambda i: (i, 0)), pl.BlockSpec((F, D), lambda i: (0, 0)), gspec, row],
        out_specs=[row, row, gspec, lspec],
        out_shape=[jax.ShapeDtypeStruct((T, D), F32), jax.ShapeDtypeStruct((T, D), BF),
                   jax.ShapeDtypeStruct((1, D), F32), jax.ShapeDtypeStruct((8, 128), F32)],
        name=name, compiler_params=_cp())(x_prev, act, wd, gain, target)


def _mm_nn_resnorm(pairs, x_prev, alpha, gain, tm, name, after=None):
    m = x_prev.shape[0]
    n = len(pairs)

    def body(*refs):
        x_ref, g_ref, xo_ref, h_ref = refs[2 * n:]
        y = _dg(refs[0][...], refs[n][...], NN)
        for i in range(1, n):
            y = y + _dg(refs[i][...], refs[n + i][...], NN)
        xv = x_ref[...] + alpha * y
        xo_ref[...] = xv
        r = lax.rsqrt(jnp.mean(xv * xv, axis=-1, keepdims=True) + NORM_EPS)
        h_ref[...] = (xv * r * g_ref[...]).astype(BF)

    body, tspec, tok = _tie(body, after)
    row = pl.BlockSpec((tm, D), lambda i: (i, 0))
    return pl.pallas_call(
        body, grid=(m // tm,),
        in_specs=tspec + [pl.BlockSpec((tm, a.shape[1]), lambda i: (i, 0)) for a, _ in pairs]
        + [pl.BlockSpec(b.shape, lambda i: (0, 0)) for _, b in pairs] + [row, pl.BlockSpec((1, D), lambda i: (0, 0))],
        out_specs=[row, row],
        out_shape=[jax.ShapeDtypeStruct((m, D), F32), jax.ShapeDtypeStruct((m, D), BF)],
        name=name, compiler_params=_cp())(*tok, *[a for a, _ in pairs], *[b for _, b in pairs], x_prev, gain)


def _ffn_up(h, need, ahead, tag):
    wg = need("g", h)
    gate = _mm_nt(h, wg, BF, 1024, 1408, tag + "_gate")
    wu = need("u", gate)
    up, act = _mm_nt_swiglu(h, wu, gate, 1024, 1408, tag + "_up_act", after=ahead("d", wu))
    wd = need("d", up)
    return act, (h, gate, up, act, wg, wu, wd)


def _ffn_bwd(x_in, gain, saved, dxo, dys, alpha_out, emit, tag):
    h, gate, up, act, wg, wu, wd = saved
    sent = emit("d", _mm_tn(act, dys, BF, 256, 1024, tag + "_dwd"))
    dgate, dup = _mm_nt_dswiglu(dys, wd, gate, up, 1024, 1408, tag + "_dact", after=sent)
    sent = emit("g", _mm_tn(dgate, h, BF, 256, 1024, tag + "_dwg"))
    sent = emit("u", _mm_tn(dup, h, BF, 256, 1024, tag + "_dwu", after=sent))
    dh = _mm_nn2(dgate, dup, wg, wu, 1024, 512, tag + "_dh", after=sent)
    dx, dxs, dgain = _rms_bwd(x_in, gain, dh, dxo, alpha_out, tag + "_dnorm")
    return dx, dxs, dgain


SLAB = 2048
N_SLAB = T // SLAB
N_PAIR = A_HEADS // 2


def _pair_masks():
    lane = _iota((128, 128), 1)
    return lane < A_HD, lane >= A_HD


def _slope_table():
    h = 2 * jnp.arange(N_PAIR)[:, None] + jnp.minimum(jnp.arange(8), 1)[None, :]
    return jnp.broadcast_to((2.0 ** (-(h + 1).astype(F32)))[:, :, None], (N_PAIR, 8, 128))


def _rows(ref, start, d):
    if d == 1:
        return ref[pl.ds(start, 128), :]
    return ref[pl.ds(start, 128, stride=d), :]


def _put_rows(ref, start, d, val):
    if d == 1:
        ref[pl.ds(start, 128), :] = val
    else:
        ref[pl.ds(start, 128, stride=d), :] = val


def _units(d):
    return [(r, b, r + 128 * d * b) for r in range(d) for b in range(SLAB // (128 * d))]


def _biases(d, slopes, has_prev):
    qi = _iota((128, 256), 0)
    kj = _iota((128, 256), 1)
    steps = qi + 128 - kj
    in_band = (steps >= 0) & (steps <= 128)
    dist = (steps * d).astype(F32)
    base = [jnp.where(in_band, -(sl * dist), NEG) for sl in slopes]
    edge = [jnp.where(has_prev | (kj >= 128), b, NEG) for b in base]
    return base, edge


def _attn_fwd(z_at, name, after=None):
    def body(sl_ref, q_ref, kc_ref, kp_ref, vc_ref, vp_ref, of_ref, ob_ref, lse_ref, m_s, l_s, a_s):
        n = pl.program_id(1)
        lo, hi = _pair_masks()
        slopes = (sl_ref[0, 0:1, 0:1], sl_ref[0, 1:2, 0:1])

        def unit(d, start, b, first, carry, bias):
            q = _rows(q_ref, start, d).astype(BF)
            kcur, vcur = _rows(kc_ref, start, d).astype(BF), _rows(vc_ref, start, d).astype(BF)
            if b > 0:
                kprev, vprev = carry
            else:
                pstart = start + SLAB - 128 * d
                kprev, vprev = _rows(kp_ref, pstart, d).astype(BF), _rows(vp_ref, pstart, d).astype(BF)
            kcat = jnp.concatenate([kprev, kcur], axis=0)
            vcat = jnp.concatenate([vprev, vcur], axis=0)
            ms, ls, pvs = [], [], []
            for e in range(2):
                qm = jnp.where(lo if e == 0 else hi, q, jnp.zeros_like(q))
                s = _dg(qm, kcat, NT) * (A_HD ** -0.5) + bias[e]
                m = jnp.max(s, axis=1, keepdims=True)
                p = jnp.exp(s - m)
                ms.append(m)
                ls.append(jnp.sum(p, axis=1, keepdims=True))
                pvs.append(_dg(p.astype(BF), vcat, NN))
            m_u = jnp.where(lo, ms[0], ms[1])
            l_u = jnp.where(lo, ls[0], ls[1])
            a_u = jnp.where(lo, pvs[0], pvs[1])
            if first:
                m_n, l_n, a_n = m_u, l_u, a_u
            else:
                m_o = _rows(m_s, start, d)
                m_n = jnp.maximum(m_o, m_u)
                c_o = jnp.exp(m_o - m_n)
                c_u = jnp.exp(m_u - m_n)
                l_n = _rows(l_s, start, d) * c_o + l_u * c_u
                a_n = _rows(a_s, start, d) * c_o + a_u * c_u
            _put_rows(m_s, start, d, m_n)
            _put_rows(l_s, start, d, l_n)
            _put_rows(a_s, start, d, a_n)
            return kcur, vcur

        for pi, (_, d) in enumerate(PATTERNS):
            base, edge = _biases(d, slopes, n > 0)
            carry = None
            for r, b, start in _units(d):
                carry = unit(d, start, b, pi == 0, carry, edge if b == 0 else base)
        l = l_s[...]
        out = a_s[...] / l
        of_ref[...] = out
        ob_ref[...] = out.astype(BF)
        lse_ref[...] = m_s[...] + jnp.log(l)

    body, tspec, tok = _tie(body, after)
    cur = lambda c: pl.BlockSpec((SLAB, 128), lambda j, n: (n, c * N_PAIR + j))
    prv = lambda c: pl.BlockSpec((SLAB, 128), lambda j, n: (jnp.maximum(n - 1, 0), c * N_PAIR + j))
    out = pl.BlockSpec((SLAB, 128), lambda j, n: (n, j))
    return pl.pallas_call(
        body, grid=(N_PAIR, N_SLAB),
        in_specs=tspec + [pl.BlockSpec((1, 8, 128), lambda j, n: (j, 0, 0)), cur(0), cur(1), prv(1), cur(2), prv(2)],
        out_specs=[out, out, out],
        out_shape=[jax.ShapeDtypeStruct((T, AW), F32), jax.ShapeDtypeStruct((T, AW), BF),
                   jax.ShapeDtypeStruct((T, AW), F32)],
        scratch_shapes=[pltpu.VMEM((SLAB, 128), F32)] * 3,
        name=name, compiler_params=_cp())(*tok, _slope_table(), z_at, z_at, z_at, z_at, z_at)


def _attn_bwd(z_at, dout, out, lse, name):
    def body(sl_ref, q_ref, kc_ref, kp_ref, vc_ref, vp_ref, do_ref, o_ref, lse_ref, dq_ref, dk_ref, dv_ref,
             dq_s, dk_s, dv_s, ck_s, cv_s):
        step = pl.program_id(1)
        n = N_SLAB - 1 - step
        lo, hi = _pair_masks()
        slopes = (sl_ref[0, 0:1, 0:1], sl_ref[0, 1:2, 0:1])

        @pl.when(step == 0)
        def _():
            ck_s[...] = jnp.zeros_like(ck_s)
            cv_s[...] = jnp.zeros_like(cv_s)

        dk_s[...] = ck_s[...]
        dv_s[...] = cv_s[...]
        ck_s[...] = jnp.zeros_like(ck_s)
        cv_s[...] = jnp.zeros_like(cv_s)

        def add_rows(ref, start, d, val):
            _put_rows(ref, start, d, _rows(ref, start, d) + val)

        def unit(d, start, b, first, carry, bias):
            q = _rows(q_ref, start, d).astype(BF)
            do_f = _rows(do_ref, start, d)
            do = do_f.astype(BF)
            prod = do_f * _rows(o_ref, start, d)
            lse_u = _rows(lse_ref, start, d)
            kcur, vcur = _rows(kc_ref, start, d).astype(BF), _rows(vc_ref, start, d).astype(BF)
            if b > 0:
                kprev, vprev = carry
            else:
                pstart = start + SLAB - 128 * d
                kprev, vprev = _rows(kp_ref, pstart, d).astype(BF), _rows(vp_ref, pstart, d).astype(BF)
            kcat = jnp.concatenate([kprev, kcur], axis=0)
            vcat = jnp.concatenate([vprev, vcur], axis=0)
            masks = (lo, hi)
            qms = [jnp.where(msk, q, jnp.zeros_like(q)) for msk in masks]
            doms = [jnp.where(msk, do, jnp.zeros_like(do)) for msk in masks]
            deltas = [jnp.sum(jnp.where(msk, prod, 0.0), axis=1, keepdims=True) for msk in masks]
            ss = [_dg(qm, kcat, NT) * (A_HD ** -0.5) + bs for qm, bs in zip(qms, bias)]
            dps = [_dg(dom, vcat, NT) for dom in doms]
            ps = [jnp.exp(s - lse_u[:, 64 * e:64 * e + 1]) for e, s in enumerate(ss)]
            dss = [(p * (dp - delta)).astype(BF) for p, dp, delta in zip(ps, dps, deltas)]
            pbs = [p.astype(BF) for p in ps]
            dqs = [_dg(ds, kcat, NN) for ds in dss]
            dkc = (_dg(dss[0], qms[0], TN) + _dg(dss[1], qms[1], TN)) * (A_HD ** -0.5)
            dvc = _dg(pbs[0], doms[0], TN) + _dg(pbs[1], doms[1], TN)
            dq_u = jnp.where(lo, dqs[0], dqs[1]) * (A_HD ** -0.5)
            if first:
                _put_rows(dq_s, start, d, dq_u)
            else:
                add_rows(dq_s, start, d, dq_u)
            add_rows(dk_s, start, d, dkc[128:])
            add_rows(dv_s, start, d, dvc[128:])
            if b > 0:
                add_rows(dk_s, start - 128 * d, d, dkc[:128])
                add_rows(dv_s, start - 128 * d, d, dvc[:128])
            else:
                pstart = start + SLAB - 128 * d
                add_rows(ck_s, pstart, d, dkc[:128])
                add_rows(cv_s, pstart, d, dvc[:128])
            return kcur, vcur

        for pi, (_, d) in enumerate(PATTERNS):
            base, edge = _biases(d, slopes, n > 0)
            carry = None
            for r, b, start in _units(d):
                carry = unit(d, start, b, pi == 0, carry, edge if b == 0 else base)
        dq_ref[...] = dq_s[...].astype(BF)
        dk_ref[...] = dk_s[...].astype(BF)
        dv_ref[...] = dv_s[...].astype(BF)

    rev = lambda n: N_SLAB - 1 - n
    cur = lambda c: pl.BlockSpec((SLAB, 128), lambda j, n: (rev(n), c * N_PAIR + j))
    prv = lambda c: pl.BlockSpec((SLAB, 128), lambda j, n: (jnp.maximum(rev(n) - 1, 0), c * N_PAIR + j))
    one = pl.BlockSpec((SLAB, 128), lambda j, n: (rev(n), j))
    return pl.pallas_call(
        body, grid=(N_PAIR, N_SLAB),
        in_specs=[pl.BlockSpec((1, 8, 128), lambda j, n: (j, 0, 0)), cur(0), cur(1), prv(1), cur(2), prv(2),
                  one, one, one],
        out_specs=[one, one, one], out_shape=[jax.ShapeDtypeStruct((T, AW), BF)] * 3,
        scratch_shapes=[pltpu.VMEM((SLAB, 128), F32)] * 5,
        name=name, compiler_params=_cp())(_slope_table(), z_at, z_at, z_at, z_at, z_at, dout, out, lse)


def _silu(x):
    return x * jax.nn.sigmoid(x)


def _qk_math(c):
    s = _silu(c)
    return s * lax.rsqrt(jnp.sum(s * s, axis=-1, keepdims=True) + L2_EPS)


def _softplus(x):
    return jnp.maximum(x, 0.0) + jnp.log(1.0 + jnp.exp(-jnp.abs(x)))


def _gate_math(bd, alog_row, dtb_row):
    rows = bd.shape[0]
    lane = _iota(bd.shape, 1)
    beta = jax.nn.sigmoid(bd)
    g = jnp.where((lane >= DN_H) & (lane < 2 * DN_H), -jnp.exp(alog_row) * _softplus(bd + dtb_row), 0.0)
    ri = _iota((rows, rows), 0)
    ci = _iota((rows, rows), 1)
    same = (ri // CH) == (ci // CH)
    li = _iota((128, 128), 0)
    lj = _iota((128, 128), 1)
    to_next_group = jnp.where((lj == li + DN_H) & (li >= DN_H) & (li < 2 * DN_H), 1.0, 0.0)
    gc = _hdot(jnp.where(same & (ci <= ri), 1.0, 0.0), g)
    glast = _hdot(_hdot(jnp.where(same, 1.0, 0.0), g), to_next_group)
    return jnp.where(lane < DN_H, beta, 0.0) + gc + glast


def _shift_down(cur, halo, s):
    if s == 0:
        return cur
    rolled = pltpu.roll(cur, s, 0)
    hr = pltpu.roll(halo, s, 0)
    head = jnp.where(_iota(hr.shape, 0) < s, hr, rolled[:8])
    return jnp.concatenate([head, rolled[8:]], axis=0)


def _shift_up(cur, halo, s):
    if s == 0:
        return cur
    rows = cur.shape[0]
    rolled = pltpu.roll(cur, rows - s, 0)
    hr = pltpu.roll(halo, 8 - s, 0)
    tail = jnp.where(_iota(hr.shape, 0) >= 8 - s, hr, rolled[rows - 8:])
    return jnp.concatenate([rolled[:rows - 8], tail], axis=0)


def _dn_pre_fwd(z_dn, conv_w8, alog_row, dtb_row, name):
    tm = 256
    wq = 3 * DNW

    def body(raw_ref, halo_ref, bd_ref, w_ref, al_ref, dt_ref, conv_ref, qkv_ref, bg_ref):
        i = pl.program_id(0)
        cur = raw_ref[...]
        halo = jnp.where(i > 0, halo_ref[...], 0.0)
        w = w_ref[...]
        conv = jnp.zeros((tm, wq), F32)
        for j in range(4):
            conv = conv + _shift_down(cur, halo, 3 - j) * w[j:j + 1, :]
        conv_ref[...] = conv
        for blk in range(3 * DN_H):
            sl = slice(128 * blk, 128 * blk + 128)
            c = conv[:, sl]
            qkv_ref[:, sl] = _qk_math(c) if blk < 2 * DN_H else _silu(c)
        bg_ref[...] = _gate_math(bd_ref[...], al_ref[...], dt_ref[...])

    one = pl.BlockSpec((1, 128), lambda i: (0, 0))
    return pl.pallas_call(
        body, grid=(T // tm,),
        in_specs=[pl.BlockSpec((tm, wq), lambda i: (i, 0)),
                  pl.BlockSpec((8, wq), lambda i: (jnp.maximum(i * (tm // 8) - 1, 0), 0)),
                  pl.BlockSpec((tm, 128), lambda i: (i, BD_BLK)),
                  pl.BlockSpec((8, wq), lambda i: (0, 0)), one, one],
        out_specs=[pl.BlockSpec((tm, wq), lambda i: (i, 0)), pl.BlockSpec((tm, wq), lambda i: (i, 0)),
                   pl.BlockSpec((tm, 128), lambda i: (i, 0))],
        out_shape=[jax.ShapeDtypeStruct((T, wq), F32), jax.ShapeDtypeStruct((T, wq), F32),
                   jax.ShapeDtypeStruct((T, 128), F32)],
        name=name, compiler_params=_cp())(z_dn, z_dn, z_dn, conv_w8, alog_row, dtb_row)


def _dn_pre_bwd(conv, z_dn, alog_row, dtb_row, dqn, dkn, dvn, dbg, name):
    tm = 256
    wq = 3 * DNW

    def body(conv_ref, bd_ref, al_ref, dt_ref, dq_ref, dk_ref, dv_ref, dbg_ref,
             dconv_ref, dbd_ref, dal_ref, ddt_ref):
        i = pl.program_id(0)
        for blk in range(3 * DN_H):
            sl = slice(128 * blk, 128 * blk + 128)
            src = (dq_ref, dk_ref, dv_ref)[blk // DN_H]
            ct = src[:, 128 * (blk % DN_H):128 * (blk % DN_H) + 128]
            fn = _qk_math if blk < 2 * DN_H else _silu
            _, vjp = jax.vjp(fn, conv_ref[:, sl])
            dconv_ref[:, sl] = vjp(ct)[0]
        _, vjp = jax.vjp(_gate_math, bd_ref[...], al_ref[...], dt_ref[...])
        dbd, dal, ddt = vjp(dbg_ref[...])
        dbd_ref[...] = dbd

        @pl.when(i == 0)
        def _():
            dal_ref[...] = dal
            ddt_ref[...] = ddt

        @pl.when(i > 0)
        def _():
            dal_ref[...] += dal
            ddt_ref[...] += ddt

    one = pl.BlockSpec((1, 128), lambda i: (0, 0))
    row = pl.BlockSpec((tm, wq), lambda i: (i, 0))
    hd = pl.BlockSpec((tm, DNW), lambda i: (i, 0))
    st = pl.BlockSpec((tm, 128), lambda i: (i, 0))
    return pl.pallas_call(
        body, grid=(T // tm,),
        in_specs=[row, pl.BlockSpec((tm, 128), lambda i: (i, BD_BLK)), one, one, hd, hd, hd, st],
        out_specs=[row, st, one, one],
        out_shape=[jax.ShapeDtypeStruct((T, wq), F32), jax.ShapeDtypeStruct((T, 128), F32),
                   jax.ShapeDtypeStruct((1, 128), F32), jax.ShapeDtypeStruct((1, 128), F32)],
        name=name, compiler_params=_cp())(conv, z_dn, alog_row, dtb_row, dqn, dkn, dvn, dbg)


def _dn_conv_bwd(dconv, z_dn, conv_w8, name):
    tm = 256
    wq = 3 * DNW
    last = T // tm - 1

    def body(dc_ref, dcn_ref, raw_ref, halo_ref, w_ref, draw_ref, dw_ref):
        i = pl.program_id(0)
        dc = dc_ref[...]
        nxt = jnp.where(i < last, dcn_ref[...], 0.0)
        cur = raw_ref[...]
        halo = jnp.where(i > 0, halo_ref[...], 0.0)
        w = w_ref[...]
        draw = jnp.zeros((tm, wq), F32)
        rows = []
        for j in range(4):
            draw = draw + _shift_up(dc, nxt, 3 - j) * w[j:j + 1, :]
            rows.append(jnp.sum(dc * _shift_down(cur, halo, 3 - j), axis=0, keepdims=True))
        draw_ref[...] = draw
        part = jnp.concatenate(rows + [jnp.zeros((4, wq), F32)], axis=0)

        @pl.when(i == 0)
        def _():
            dw_ref[...] = part

        @pl.when(i > 0)
        def _():
            dw_ref[...] += part

    row = pl.BlockSpec((tm, wq), lambda i: (i, 0))
    return pl.pallas_call(
        body, grid=(T // tm,),
        in_specs=[row, pl.BlockSpec((8, wq), lambda i: (jnp.minimum((i + 1) * (tm // 8), T // 8 - 1), 0)),
                  row, pl.BlockSpec((8, wq), lambda i: (jnp.maximum(i * (tm // 8) - 1, 0), 0)),
                  pl.BlockSpec((8, wq), lambda i: (0, 0))],
        out_specs=[row, pl.BlockSpec((8, wq), lambda i: (0, 0))],
        out_shape=[jax.ShapeDtypeStruct((T, wq), F32), jax.ShapeDtypeStruct((8, wq), F32)],
        name=name, compiler_params=_cp())(dconv, dconv, z_dn, z_dn, conv_w8)


def _h3(a, b, dims=NN):
    return lax.dot_general(a, b, dims, precision=lax.Precision.HIGH, preferred_element_type=F32)


@jax.custom_vjp
def _inverse_given(a_mat, tinv):
    return tinv


def _inverse_given_fwd(a_mat, tinv):
    return tinv, tinv


def _inverse_given_bwd(tinv, g):
    return -_bdot_tn(tinv, _bdot_nt(g, tinv)), jnp.zeros_like(tinv)


_inverse_given.defvjp(_inverse_given_fwd, _inverse_given_bwd)


@jax.custom_vjp
def _h3_lo(a, b):
    return _h3(a, b)


def _h3_lo_fwd(a, b):
    return _h3(a, b), (a, b)


def _h3_lo_bwd(res, g):
    a, b = res
    gb = g.astype(BF)
    return _dg(gb, b.astype(BF), NT), _dg(a.astype(BF), gb, TN)


_h3_lo.defvjp(_h3_lo_fwd, _h3_lo_bwd)


def _prep_head(q, k, v, bgc, h):
    beta = _col(bgc, h)
    gc = jnp.broadcast_to(_col(bgc, DN_H + h), (PAIR, 128))
    glast = jnp.broadcast_to(_col(bgc, 2 * DN_H + h), (PAIR, 128))
    ri = _iota((PAIR, PAIR), 0)
    ci = _iota((PAIR, PAIR), 1)
    same = (ri // CH) == (ci // CH)
    causal = same & (ci <= ri)
    strict = same & (ci < ri)
    eye = ri == ci
    gc_cols = _hdot(jnp.ones((PAIR, PAIR), F32), jnp.where(eye, gc, 0.0))
    decay = jnp.exp(jnp.where(causal, gc - gc_cols, NEG))
    egc = jnp.exp(gc)
    kb = k * beta
    a_mat = jnp.where(strict, _bdot_nt(kb, k) * decay, 0.0)
    qs = q * (DN_HD ** -0.5)
    attn = jnp.where(causal, _bdot_nt(qs, k) * decay, 0.0)
    return a_mat, (v * beta, kb * egc, qs * egc, k * jnp.exp(glast - gc), attn, jnp.exp(glast))


def _prep_tail(tinv, ctx):
    vb, kbe, qg, kdec, attn, decb = ctx
    return _h3_lo(tinv, vb), _h3_lo(tinv, kbe), qg, kdec, attn, decb


def _inverses(a_mats):
    eye = jnp.where(_iota((PAIR, PAIR), 0) == _iota((PAIR, PAIR), 1), 1.0, 0.0)
    ps = [-a for a in a_mats]
    tinvs = [eye + p for p in ps]
    for _ in range(5):
        ps = [_h3(p, p) for p in ps]
        tinvs = [t + _h3(t, p) for t, p in zip(tinvs, ps)]
    return tinvs


def _prep_math(q, k, v, bgc, h, tinv_saved):
    a_mat, ctx = _prep_head(q, k, v, bgc, h)
    return _prep_tail(_inverse_given(a_mat, tinv_saved), ctx)


def _dn_prep_fwd(qkv, bg, name):
    rows = 512
    hd = lambda off: pl.BlockSpec((rows, 128), lambda g, h: (g, off + h))
    out = pl.BlockSpec((rows, 128), lambda g, h: (g, h))

    def body(q_ref, k_ref, v_ref, bg_ref, *outs):
        h = pl.program_id(1)
        spans = [slice(PAIR * pr, PAIR * pr + PAIR) for pr in range(rows // PAIR)]
        heads = [_prep_head(q_ref[rs, :], k_ref[rs, :], v_ref[rs, :], bg_ref[rs, :], h) for rs in spans]
        tinvs = _inverses([a for a, _ in heads])
        for rs, tinv, (_, ctx) in zip(spans, tinvs, heads):
            for o_ref, val in zip(outs, _prep_tail(tinv, ctx) + (tinv,)):
                o_ref[rs, :] = val

    return pl.pallas_call(
        body, grid=(T // rows, DN_H),
        in_specs=[hd(0), hd(DN_H), hd(2 * DN_H), pl.BlockSpec((rows, 128), lambda g, h: (g, 0))],
        out_specs=[out] * 7, out_shape=[jax.ShapeDtypeStruct((T, DNW), F32)] * 7,
        name=name, compiler_params=_cp())(qkv, qkv, qkv, bg)


def _dn_prep_bwd(qkv, bg, tinv, cts, name):
    rows = 512
    hd = lambda off: pl.BlockSpec((rows, 128), lambda g, h: (g, off + h))
    out = pl.BlockSpec((rows, 128), lambda g, h: (g, h))
    st = pl.BlockSpec((rows, 128), lambda g, h: (g, 0))

    def body(q_ref, k_ref, v_ref, bg_ref, ti_ref, c0, c1, c2, c3, c4, c5, dq_ref, dk_ref, dv_ref, dbg_ref):
        h = pl.program_id(1)
        spans = [slice(PAIR * pr, PAIR * pr + PAIR) for pr in range(rows // PAIR)]
        tis = [ti_ref[rs, :] for rs in spans]

        def joint(qs, ks, vs, bs):
            heads = [_prep_head(q, k, v, b, h) for q, k, v, b in zip(qs, ks, vs, bs)]
            return [_prep_tail(_inverse_given(a, ti), ctx) for (a, ctx), ti in zip(heads, tis)]

        _, vjp = jax.vjp(joint, *[[r[rs, :] for rs in spans] for r in (q_ref, k_ref, v_ref, bg_ref)])
        dqs, dks, dvs, dbs = vjp([tuple(c[rs, :] for c in (c0, c1, c2, c3, c4, c5)) for rs in spans])
        for rs, dq, dk, dv in zip(spans, dqs, dks, dvs):
            dq_ref[rs, :] = dq
            dk_ref[rs, :] = dk
            dv_ref[rs, :] = dv
        dbg_all = jnp.concatenate(dbs, axis=0)

        @pl.when(h == 0)
        def _():
            dbg_ref[...] = dbg_all

        @pl.when(h > 0)
        def _():
            dbg_ref[...] += dbg_all

    return pl.pallas_call(
        body, grid=(T // rows, DN_H),
        in_specs=[hd(0), hd(DN_H), hd(2 * DN_H), st] + [out] * 7,
        out_specs=[out, out, out, st],
        out_shape=[jax.ShapeDtypeStruct((T, DNW), F32)] * 3 + [jax.ShapeDtypeStruct((T, 128), F32)],
        name=name, compiler_params=_cp())(qkv, qkv, qkv, bg, tinv, *cts)


def _step_math(s, u, w, qg, kdec, attn, decb, sub):
    vnew = u - _bdot_nn(w, s)
    z = jnp.zeros((CH, 128), F32)
    vfull = jnp.concatenate([vnew, z] if sub == 0 else [z, vnew], axis=0)
    o = _bdot_nn(qg, s) + _bdot_nn(attn, vfull)
    dec = jnp.sum(decb, axis=0, keepdims=True) * (1.0 / CH)
    return s * dec + _bdot_tn(kdec, vnew), o


def _dn_scan_fwd(prep, name):
    npair = T // PAIR
    row = pl.BlockSpec((PAIR, DNW), lambda p: (p, 0))

    def body(u_ref, w_ref, qg_ref, kd_ref, at_ref, db_ref, o_ref, ss_ref, s_ref):
        @pl.when(pl.program_id(0) == 0)
        def _():
            s_ref[...] = jnp.zeros_like(s_ref)

        states = [s_ref[h] for h in range(DN_H)]
        for sub in range(2):
            rs = slice(CH * sub, CH * sub + CH)
            for h in range(DN_H):
                ls = slice(128 * h, 128 * h + 128)
                ss_ref[sub, h] = states[h]
                states[h], o = _step_math(states[h], u_ref[rs, ls], w_ref[rs, ls], qg_ref[rs, ls],
                                          kd_ref[rs, ls], at_ref[rs, ls], db_ref[rs, ls], sub)
                o_ref[rs, ls] = o
        for h in range(DN_H):
            s_ref[h] = states[h]

    return pl.pallas_call(
        body, grid=(npair,), in_specs=[row] * 6,
        out_specs=[row, pl.BlockSpec((2, DN_H, 128, 128), lambda p: (p, 0, 0, 0))],
        out_shape=[jax.ShapeDtypeStruct((T, DNW), F32), jax.ShapeDtypeStruct((T // CH, DN_H, 128, 128), F32)],
        scratch_shapes=[pltpu.VMEM((DN_H, 128, 128), F32)],
        name=name, compiler_params=_cp())(*prep)


def _dn_scan_bwd(prep, states, do, name):
    npair = T // PAIR
    row = pl.BlockSpec((PAIR, DNW), lambda p: (npair - 1 - p, 0))

    def body(u_ref, w_ref, qg_ref, kd_ref, at_ref, db_ref, ss_ref, do_ref, *rest):
        outs, ds_ref = rest[:6], rest[6]

        @pl.when(pl.program_id(0) == 0)
        def _():
            ds_ref[...] = jnp.zeros_like(ds_ref)

        dss = [ds_ref[h] for h in range(DN_H)]
        for sub in (1, 0):
            rs = slice(CH * sub, CH * sub + CH)
            for h in range(DN_H):
                ls = slice(128 * h, 128 * h + 128)
                args = (ss_ref[sub, h],) + tuple(r[rs, ls] for r in (u_ref, w_ref, qg_ref, kd_ref, at_ref, db_ref))
                _, vjp = jax.vjp(functools.partial(_step_math, sub=sub), *args)
                cts = vjp((dss[h], do_ref[rs, ls]))
                dss[h] = cts[0]
                for o_ref, val in zip(outs, cts[1:]):
                    o_ref[rs, ls] = val
        for h in range(DN_H):
            ds_ref[h] = dss[h]

    return pl.pallas_call(
        body, grid=(npair,),
        in_specs=[row] * 6 + [pl.BlockSpec((2, DN_H, 128, 128), lambda p: (npair - 1 - p, 0, 0, 0)), row],
        out_specs=[row] * 6, out_shape=[jax.ShapeDtypeStruct((T, DNW), F32)] * 6,
        scratch_shapes=[pltpu.VMEM((DN_H, 128, 128), F32)],
        name=name, compiler_params=_cp())(*prep, states, do)


def _post_math(o, gate, wrow):
    return o * lax.rsqrt(jnp.mean(o * o, axis=-1, keepdims=True) + NORM_EPS) * wrow * _silu(gate)


def _dn_post_fwd(o, z_dn, dn_norm, name):
    tm = 512
    row = pl.BlockSpec((tm, DNW), lambda i: (i, 0))

    def body(o_ref, g_ref, w_ref, y_ref):
        for h in range(DN_H):
            ls = slice(128 * h, 128 * h + 128)
            y_ref[:, ls] = _post_math(o_ref[:, ls], g_ref[:, ls], w_ref[...]).astype(BF)

    return pl.pallas_call(
        body, grid=(T // tm,),
        in_specs=[row, pl.BlockSpec((tm, DNW), lambda i: (i, 3)), pl.BlockSpec((1, 128), lambda i: (0, 0))],
        out_specs=row, out_shape=jax.ShapeDtypeStruct((T, DNW), BF),
        name=name, compiler_params=_cp())(o, z_dn, dn_norm)


def _dn_post_bwd(o, z_dn, dn_norm, dy, name):
    tm = 512
    row = pl.BlockSpec((tm, DNW), lambda i: (i, 0))
    one = pl.BlockSpec((1, 128), lambda i: (0, 0))

    def body(o_ref, g_ref, w_ref, dy_ref, do_ref, dg_ref, dw_ref):
        i = pl.program_id(0)
        dw = jnp.zeros((1, 128), F32)
        for h in range(DN_H):
            ls = slice(128 * h, 128 * h + 128)
            _, vjp = jax.vjp(_post_math, o_ref[:, ls], g_ref[:, ls], w_ref[...])
            do, dg, dwh = vjp(dy_ref[:, ls].astype(F32))
            do_ref[:, ls] = do
            dg_ref[:, ls] = dg
            dw = dw + dwh

        @pl.when(i == 0)
        def _():
            dw_ref[...] = dw

        @pl.when(i > 0)
        def _():
            dw_ref[...] += dw

    return pl.pallas_call(
        body, grid=(T // tm,),
        in_specs=[row, pl.BlockSpec((tm, DNW), lambda i: (i, 3)), one, pl.BlockSpec((tm, DNW), lambda i: (i, 1))],
        out_specs=[row, row, one],
        out_shape=[jax.ShapeDtypeStruct((T, DNW), F32), jax.ShapeDtypeStruct((T, DNW), F32),
                   jax.ShapeDtypeStruct((1, 128), F32)],
        name=name, compiler_params=_cp())(o, z_dn, dn_norm, dy)


def _dz_dn_assemble(draw, dgate, dbd, name):
    tm = 512

    def body(a_ref, b_ref, c_ref, o_ref):
        o_ref[:, :3 * DNW] = a_ref[...].astype(BF)
        o_ref[:, 3 * DNW:4 * DNW] = b_ref[...].astype(BF)
        o_ref[:, 4 * DNW:4 * DNW + 128] = c_ref[...].astype(BF)
        o_ref[:, 4 * DNW + 128:] = jnp.zeros((tm, 128), BF)

    return pl.pallas_call(
        body, grid=(T // tm,),
        in_specs=[pl.BlockSpec((tm, 3 * DNW), lambda i: (i, 0)), pl.BlockSpec((tm, DNW), lambda i: (i, 0)),
                  pl.BlockSpec((tm, 128), lambda i: (i, 0))],
        out_specs=pl.BlockSpec((tm, ZD), lambda i: (i, 0)),
        out_shape=jax.ShapeDtypeStruct((T, ZD), BF), name=name, compiler_params=_cp())(draw, dgate, dbd)


HBM = pl.BlockSpec(memory_space=pltpu.HBM)
SEM = pl.BlockSpec(memory_space=pltpu.SEMAPHORE)
EFFECT = pltpu.SideEffectType.DATAFLOW_SIDE_EFFECTING
N_PEER = N_DEV - 1


ALL_PEERS = (1, 2, 4, 3, 5, 6, 7)
FIRST_HOP = (1, 2, 4, 6)
FORWARDED = (2, 4, 6)


def _peers(x, y, c, ks=ALL_PEERS):
    return [(k, (x ^ (k >> 2), y ^ ((k >> 1) & 1), c ^ (k & 1))) for k in ks]


def _exchange_copy(ins, lands, ssems, rsems, scatter, t, k, pos, me):
    px, py, pc = pos
    src = ins[t].at[4 * px + 2 * py + pc] if scatter else ins[t]
    return pltpu.make_async_remote_copy(
        src_ref=src, dst_ref=lands[t].at[me], send_sem=ssems[t].at[k - 1], recv_sem=rsems[t].at[k - 1],
        device_id=pos, device_id_type=MESH_ID)


def _xstart(bufs, scatter, name, ks=ALL_PEERS):
    nt = len(bufs)
    lands = [lax.empty((N_DEV,) + tuple(b.shape[1:] if scatter else b.shape), b.dtype) for b in bufs]

    def body(*refs):
        ins, lnd = refs[:nt], refs[nt:2 * nt]
        ssems, rsems = refs[2 * nt:3 * nt], refs[3 * nt:4 * nt]
        token = refs[-1]
        x, y, c = lax.axis_index("x"), lax.axis_index("y"), lax.axis_index("c")
        me = 4 * x + 2 * y + c
        for t in range(nt):
            for k, pos in _peers(x, y, c, ks):
                _exchange_copy(ins, lnd, ssems, rsems, scatter, t, k, pos, me).start()
        token[...] = jnp.zeros_like(token)

    both = list(bufs) + lands
    res = pl.pallas_call(
        body, name=name,
        out_shape=[pltpu.SemaphoreType.DMA((N_PEER,))] * (2 * nt)
        + [pltpu.HBM(b.shape, b.dtype) for b in both] + [jax.ShapeDtypeStruct((8, 128), F32)],
        in_specs=[HBM] * (2 * nt),
        out_specs=[SEM] * (2 * nt) + [HBM] * (2 * nt) + [pl.BlockSpec(memory_space=pltpu.VMEM)],
        input_output_aliases={i: 2 * nt + i for i in range(2 * nt)},
        compiler_params=pltpu.CompilerParams(has_side_effects=EFFECT),
    )(*[pltpu.with_memory_space_constraint(b, pltpu.HBM) for b in both])
    return res[:nt], res[nt:2 * nt], res[2 * nt:3 * nt], res[3 * nt:4 * nt], res[-1][0, 0]


def _xwait(ssems, rsems, thrus, lands, scatter, after, name, ks=ALL_PEERS):
    nt = len(lands)

    def body(*refs):
        ins, lnd = refs[:nt], refs[nt:2 * nt]
        ss, rs = refs[2 * nt:3 * nt], refs[3 * nt:4 * nt]
        x, y, c = lax.axis_index("x"), lax.axis_index("y"), lax.axis_index("c")
        me = 4 * x + 2 * y + c
        for t in range(nt):
            for k, pos in _peers(x, y, c, ks):
                cp = _exchange_copy(ins, lnd, ss, rs, scatter, t, k, pos, me)
                cp.wait_send()
                cp.wait_recv()

    both = list(thrus) + list(lands)
    res = pl.pallas_call(
        body, name=name, out_shape=[pltpu.HBM(b.shape, b.dtype) for b in both],
        in_specs=[HBM] * (2 * nt) + [SEM] * (2 * nt) + [ANY], out_specs=[HBM] * (2 * nt),
        input_output_aliases={i: i for i in range(2 * nt)},
        compiler_params=pltpu.CompilerParams(has_side_effects=EFFECT),
    )(*both, *ssems, *rsems, after)
    return res[:nt], res[nt:]


def _forward_copy(lands, ssems, rsems, t, k, pos, sibling):
    px, py, pc = pos
    slot = lands[t].at[4 * px + 2 * py + pc]
    return pltpu.make_async_remote_copy(
        src_ref=slot, dst_ref=slot, send_sem=ssems[t].at[k - 1], recv_sem=rsems[t].at[k - 1],
        device_id=sibling, device_id_type=MESH_ID)


def _fstart(lands, name):
    nt = len(lands)

    def body(*refs):
        lnd = refs[:nt]
        ssems, rsems = refs[nt:2 * nt], refs[2 * nt:3 * nt]
        token = refs[-1]
        x, y, c = lax.axis_index("x"), lax.axis_index("y"), lax.axis_index("c")
        for t in range(nt):
            for k, pos in _peers(x, y, c, FORWARDED):
                _forward_copy(lnd, ssems, rsems, t, k, pos, (x, y, c ^ 1)).start()
        token[...] = jnp.zeros_like(token)

    res = pl.pallas_call(
        body, name=name,
        out_shape=[pltpu.SemaphoreType.DMA((N_PEER,))] * (2 * nt)
        + [pltpu.HBM(b.shape, b.dtype) for b in lands] + [jax.ShapeDtypeStruct((8, 128), F32)],
        in_specs=[HBM] * nt,
        out_specs=[SEM] * (2 * nt) + [HBM] * nt + [pl.BlockSpec(memory_space=pltpu.VMEM)],
        input_output_aliases={i: 2 * nt + i for i in range(nt)},
        compiler_params=pltpu.CompilerParams(has_side_effects=EFFECT),
    )(*[pltpu.with_memory_space_constraint(b, pltpu.HBM) for b in lands])
    return res[:nt], res[nt:2 * nt], res[2 * nt:3 * nt], res[-1][0, 0]


def _fwait(ssems, rsems, lands, after, name):
    nt = len(lands)

    def body(*refs):
        lnd = refs[:nt]
        ss, rs = refs[nt:2 * nt], refs[2 * nt:3 * nt]
        x, y, c = lax.axis_index("x"), lax.axis_index("y"), lax.axis_index("c")
        for t in range(nt):
            for k, pos in _peers(x, y, c, FORWARDED):
                cp = _forward_copy(lnd, ss, rs, t, k, pos, (x, y, c ^ 1))
                cp.wait_send()
                cp.wait_recv()

    return pl.pallas_call(
        body, name=name, out_shape=[pltpu.HBM(b.shape, b.dtype) for b in lands],
        in_specs=[HBM] * nt + [SEM] * (2 * nt) + [ANY], out_specs=[HBM] * nt,
        input_output_aliases={i: i for i in range(nt)},
        compiler_params=pltpu.CompilerParams(has_side_effects=EFFECT),
    )(*lands, *ssems, *rsems, after)


def _adam(recv, w, m, v, tr, name):
    _, r, c = w.shape
    n_part = recv.shape[0]
    c1 = np.float32(1.0 - ADAM_B1 ** ADAM_STEP)
    c2 = np.float32(1.0 - ADAM_B2 ** ADAM_STEP)

    def body(r_ref, w_ref, m_ref, v_ref, g_ref, d_ref, mo_ref, vo_ref):
        g = r_ref[0].astype(F32)
        for s in range(1, n_part):
            g = g + r_ref[s].astype(F32)
        mn = ADAM_B1 * m_ref[0] + (1.0 - ADAM_B1) * g
        vn = ADAM_B2 * v_ref[0] + (1.0 - ADAM_B2) * (g * g)
        g_ref[0] = g
        mo_ref[0] = mn
        vo_ref[0] = vn
        d_ref[0] = -ADAM_LR * ((mn / c1) / (jnp.sqrt(vn / c2) + ADAM_EPS) + ADAM_WD * w_ref[0])

    one = pl.BlockSpec((1, tr, c), lambda i: (0, i, 0))
    return pl.pallas_call(
        body, grid=(r // tr,), in_specs=[pl.BlockSpec((n_part, tr, c), lambda i: (0, i, 0)), one, one, one],
        out_specs=[one] * 4, out_shape=[jax.ShapeDtypeStruct((1, r, c), F32)] * 4,
        name=name, compiler_params=_cp())(recv, w, m, v)


def _local_step(x, target, sp, need, ahead, emit):
    g = {}
    x0, h1 = x, _rms_fwd(x, sp["norm_ffn1"], "ffn1_norm")
    act1, saved1 = _ffn_up(h1, lambda kind, a: need("w" + kind + "1", a),
                           lambda kind, a: ahead("w" + kind + "1", a), "ffn1")
    x1, h2 = _mm_nn_resnorm([(act1, saved1[-1])], x0, 0.5, sp["norm_mix"], 512, "ffn1_down_norm",
                            after=ahead("win_a", act1))
    win_a, win_d = need("win_a", h2), need("win_d", h2)
    conv_w8, wout = need("conv_w8", h2), need("wout", h2)
    z_at = _mm_nn(h2, win_a, F32, 1024, 768, "mix_in_attn")
    z_dn = _mm_nn(h2, win_d, F32, 1024, 768, "mix_in_dn")

    conv, qkvn, bg = _dn_pre_fwd(z_dn, conv_w8, sp["alog_row"], sp["dtb_row"], "dn_pre")
    *prep, tinv = _dn_prep_fwd(qkvn, bg, "dn_prep")
    o_dn, states = _dn_scan_fwd(prep, "dn_scan")
    dn_b = _dn_post_fwd(o_dn, z_dn, sp["dn_norm"], "dn_post")
    attn_f, attn_b, lse = _attn_fwd(z_at, "attn_fwd", after=ahead("wg2", dn_b))

    x2, h3 = _mm_nn_resnorm([(attn_b, wout[:AW]), (dn_b, wout[AW:])], x1, 1.0, sp["norm_ffn2"], 512,
                            "mix_out_norm")
    act2, saved2 = _ffn_up(h3, lambda kind, a: need("w" + kind + "2", a),
                           lambda kind, a: ahead("w" + kind + "2", a), "ffn2")

    dx3, dys3, g["norm_final"], loss8 = _down_loss_bwd(x2, act2, saved2[-1], sp["norm_final"], target,
                                                       "ffn2_down_loss")
    dx2, dx2b, g["norm_ffn2"] = _ffn_bwd(
        x2, sp["norm_ffn2"], saved2, dx3, dys3, 1.0,
        lambda kind, dw: emit(kind + "2", {"w" + kind + "2": dw}), "ffn2b")

    zero = emit("wout", {"wout": jnp.concatenate([_mm_tn(attn_b, dx2b, BF, 512, 1024, "mix_out_dw_a"),
                                                  _mm_tn(dn_b, dx2b, BF, 512, 1024, "mix_out_dw_d")], axis=0)})
    dmix = _mm_nt(dx2b, wout, F32, 1024, 1024, "mix_out_dx", after=zero)

    dz_at = jnp.concatenate(_attn_bwd(z_at, dmix, attn_f, lse, "attn_bwd"), axis=1)

    do_dn, dgate, g["dn_norm"] = _dn_post_bwd(o_dn, z_dn, sp["dn_norm"], dmix, "dn_post_b")
    cts = _dn_scan_bwd(prep, states, do_dn, "dn_scan_b")
    dqn, dkn, dvn, dbg = _dn_prep_bwd(qkvn, bg, tinv, cts, "dn_prep_b")
    dconv, dbd, g["alog_row"], g["dtb_row"] = _dn_pre_bwd(
        conv, z_dn, sp["alog_row"], sp["dtb_row"], dqn, dkn, dvn, dbg, "dn_pre_b")
    draw, dconv_w8 = _dn_conv_bwd(dconv, z_dn, conv_w8, "dn_conv_b")
    dz_dn = _dz_dn_assemble(draw, dgate, dbd, "dn_dz")

    zero = emit("win", {"win_a": _mm_tn(h2, dz_at, BF, 512, 768, "mix_in_dw_a"),
                        "win_d": _mm_tn(h2, dz_dn, BF, 512, 768, "mix_in_dw_d"), "conv_w8": dconv_w8})
    dh2 = _mm_nt(dz_at, win_a, F32, 1024, 1024, "mix_in_dx_a", after=zero)
    dh2d = _mm_nt(dz_dn, win_d, F32, 1024, 1024, "mix_in_dx_d")
    (dx1, dys1), _, g["norm_mix"] = _rms_bwd2(x1, sp["norm_mix"], dh2, dh2d, dx2, "mix_dnorm")
    dx0, _, g["norm_ffn1"] = _ffn_bwd(
        x0, sp["norm_ffn1"], saved1, dx1, dys1, 1.0,
        lambda kind, dw: emit(kind + "1", {"w" + kind + "1": dw}), "ffn1b")
    return loss8[0, 0], dx0, g


def _rms_bwd2(x, gain, dh_a, dh_b, dres, name):
    tm = 512
    row = pl.BlockSpec((tm, D), lambda i: (i, 0))
    gspec = pl.BlockSpec((1, D), lambda i: (0, 0))

    def body(x_ref, g_ref, da_ref, db_ref, dres_ref, dx_ref, dxs_ref, dg_ref):
        i = pl.program_id(0)
        xv = x_ref[...]
        r = lax.rsqrt(jnp.mean(xv * xv, axis=-1, keepdims=True) + NORM_EPS)
        xh = xv * r
        dhv = da_ref[...] + db_ref[...]
        part = jnp.sum(dhv * xh, axis=0, keepdims=True)

        @pl.when(i == 0)
        def _():
            dg_ref[...] = part

        @pl.when(i > 0)
        def _():
            dg_ref[...] += part

        dxh = dhv * g_ref[...]
        dx = r * (dxh - xh * jnp.mean(dxh * xh, axis=-1, keepdims=True)) + dres_ref[...]
        dx_ref[...] = dx
        dxs_ref[...] = (0.5 * dx).astype(BF)

    dx, dxs, dg = pl.pallas_call(
        body, grid=(T // tm,), in_specs=[row, gspec, row, row, row], out_specs=[row, row, gspec],
        out_shape=[jax.ShapeDtypeStruct((T, D), F32), jax.ShapeDtypeStruct((T, D), BF),
                   jax.ShapeDtypeStruct((1, D), F32)],
        name=name, compiler_params=_cp())(x, gain, dh_a, dh_b, dres)
    return (dx, dxs), None, dg


def _cols_from_shards(gathered):
    n, r, c = gathered.shape
    return jnp.transpose(gathered, (1, 0, 2)).reshape(r, n * c)


def _shards_from_cols(full, dtype):
    r, nc = full.shape
    return jnp.transpose(full.reshape(r, N_DEV, nc // N_DEV), (1, 0, 2)).astype(dtype)


def _lane_row(vec4):
    return jnp.zeros((1, 128), F32).at[:, DN_H:2 * DN_H].set(vec4.astype(F32))


WEIGHT_SOURCES = {"wg1": "gate1", "wu1": "up1", "wd1": "down1", "win_a": "w_in", "win_d": "w_in",
                  "conv_w8": "conv_w", "wout": "w_out", "wg2": "gate2", "wu2": "up2", "wd2": "down2"}
TRANSPOSED = ("gate1", "up1", "gate2", "up2")


def _build_weights(name, gath):
    if name in ("wg1", "wu1", "wd1", "wg2", "wu2", "wd2"):
        return {name: gath[WEIGHT_SOURCES[name]].reshape(F, D)}
    if name in ("win_a", "win_d"):
        w_in = _cols_from_shards(gath["w_in"])
        c0 = 3 * AW + 3 * DNW
        win_d = jnp.concatenate([w_in[:, ZA:c0], w_in[:, c0 + 2 * DN_H:], w_in[:, c0:c0 + 2 * DN_H],
                                 jnp.zeros((D, ZP - IN_COLS), w_in.dtype)], axis=1)
        return {"win_a": w_in[:, :ZA], "win_d": win_d}
    if name == "wout":
        return {name: gath["w_out"].reshape(D, D)}
    conv = _cols_from_shards(gath["conv_w"])
    return {"conv_w8": jnp.concatenate([conv, jnp.zeros((4, 3 * DNW), F32)], axis=0)}


def _small_params(norm_ffn1, norm_mix, norm_ffn2, norm_final, a_log, dt_bias, dn_norm):
    return {"norm_ffn1": norm_ffn1, "norm_mix": norm_mix, "norm_ffn2": norm_ffn2,
            "norm_final": norm_final.reshape(1, D), "alog_row": _lane_row(a_log), "dtb_row": _lane_row(dt_bias),
            "dn_norm": dn_norm}


def _grad_slabs(group, g):
    if group[0] in "gud":
        return {WEIGHT_SOURCES["w" + group]: g["w" + group].reshape(N_DEV, F // N_DEV, D)}
    if group == "wout":
        return {"w_out": g["wout"].reshape(N_DEV, D // N_DEV, D)}
    gp = jnp.concatenate([g["win_a"], g["win_d"]], axis=1)
    c0 = 3 * AW + 3 * DNW
    g_in = jnp.concatenate([gp[:, :c0], gp[:, c0 + DNW:c0 + DNW + 2 * DN_H], gp[:, c0:c0 + DNW]], axis=1)
    return {"w_in": _shards_from_cols(g_in, BF), "conv_w": _shards_from_cols(g["conv_w8"][:4], F32)}


SMALL_ROWS = 40


def _small_pack(norm_ffn1, norm_mix, norm_ffn2, norm_final, dn_norm, alog_row, dtb_row, loss=None):
    rows = [a.reshape(8, 128) for a in (norm_ffn1, norm_mix, norm_ffn2, norm_final)]
    loss_row = jnp.zeros((1, 128), F32) if loss is None else jnp.broadcast_to(loss.reshape(1, 1), (1, 128))
    rows += [dn_norm.reshape(1, 128), alog_row, dtb_row, loss_row, jnp.zeros((SMALL_ROWS - 36, 128), F32)]
    return jnp.concatenate(rows, axis=0)


def _small_unpack(pk):
    pk = pk[0]
    return (pk[0:8].reshape(1, D), pk[8:16].reshape(1, D), pk[16:24].reshape(1, D), pk[24:32].reshape(D),
            pk[32:33], pk[33:34, DN_H:2 * DN_H], pk[34:35, DN_H:2 * DN_H])


ADAM_TILE = {"gate1": 256, "up1": 256, "down1": 176, "gate2": 256, "up2": 256, "down2": 176,
             "w_in": 256, "w_out": 128, "conv_w": 4}
BIG = ("gate1", "up1", "down1", "w_in", "w_out", "gate2", "up2", "down2", "conv_w")


def kernel(x, norm_ffn1, ffn1_gate, ffn1_up, ffn1_down, norm_mix, w_in, conv_w, a_log, dt_bias, dn_norm, w_out, norm_ffn2, ffn2_gate, ffn2_up, ffn2_down, norm_final, loss_target, m_norm_ffn1, m_ffn1_gate, m_ffn1_up, m_ffn1_down, m_norm_mix, m_w_in, m_conv_w, m_a_log, m_dt_bias, m_dn_norm, m_w_out, m_norm_ffn2, m_ffn2_gate, m_ffn2_up, m_ffn2_down, m_norm_final, v_norm_ffn1, v_ffn1_gate, v_ffn1_up, v_ffn1_down, v_norm_mix, v_w_in, v_conv_w, v_a_log, v_dt_bias, v_dn_norm, v_w_out, v_norm_ffn2, v_ffn2_gate, v_ffn2_up, v_ffn2_down, v_norm_final):
    w = {"gate1": ffn1_gate, "up1": ffn1_up, "down1": ffn1_down, "w_in": w_in, "w_out": w_out,
         "gate2": ffn2_gate, "up2": ffn2_up, "down2": ffn2_down, "conv_w": conv_w}
    m = {"gate1": m_ffn1_gate, "up1": m_ffn1_up, "down1": m_ffn1_down, "w_in": m_w_in, "w_out": m_w_out,
         "gate2": m_ffn2_gate, "up2": m_ffn2_up, "down2": m_ffn2_down, "conv_w": m_conv_w}
    v = {"gate1": v_ffn1_gate, "up1": v_ffn1_up, "down1": v_ffn1_down, "w_in": v_w_in, "w_out": v_w_out,
         "gate2": v_ffn2_gate, "up2": v_ffn2_up, "down2": v_ffn2_down, "conv_w": v_conv_w}

    me = 4 * lax.axis_index("x") + 2 * lax.axis_index("y") + lax.axis_index("c")
    own_slot = lambda land, mine: lax.dynamic_update_index_in_dim(land, mine, me, 0)

    ag_order = ("gate1", "up1", "down1", "w_in", "conv_w", "w_out", "gate2", "up2", "down2")
    ag_groups = (("gate1",), ("up1",), ("down1",), ("w_in", "conv_w", "w_out"), ("gate2", "up2", "down2"))
    pos = {n: i for i, n in enumerate(ag_order)}

    def shard(n):
        if n == "conv_w":
            return w[n][0]
        return (w[n][0].T if n in TRANSPOSED else w[n][0]).astype(BF)

    ss, rs, thru, land, zero = _xstart([shard(n) for n in ag_order], False, "weights_start", FIRST_HOP)
    gath, built, on_its_way = {}, {}, {}
    group_of = lambda name: [i for i, grp in enumerate(ag_groups) if WEIGHT_SOURCES[name] in grp][0]

    def ahead(name, after):
        gi = group_of(name)
        if WEIGHT_SOURCES[name] in gath or gi in on_its_way:
            return None
        ids = [pos[n] for n in ag_groups[gi]]
        thrus, lands = _xwait([ss[i] for i in ids], [rs[i] for i in ids], [thru[i] for i in ids],
                              [land[i] for i in ids], False, after, "weights_wait%d" % gi, FIRST_HOP)
        fss, frs, lands, token = _fstart(lands, "weights_forward%d" % gi)
        on_its_way[gi] = (thrus, fss, frs, lands)
        return token

    def need(name, after):
        if name not in built:
            if WEIGHT_SOURCES[name] not in gath:
                gi = group_of(name)
                ahead(name, after)
                thrus, fss, frs, lands = on_its_way.pop(gi)
                lands = _fwait(fss, frs, lands, after, "weights_forward_wait%d" % gi)
                for n, t, l in zip(ag_groups[gi], thrus, lands):
                    gath[n] = own_slot(l, t)
            built.update(_build_weights(name, gath))
        return built[name]

    pending = []

    def emit(group, grads):
        slabs = grads if group == "small" else _grad_slabs(group, grads)
        names = list(slabs)
        started = _xstart([slabs[n] for n in names], True, "grads_start_" + group)
        pending.append((group, names) + started[:4])
        return started[4]

    sp = _small_params(norm_ffn1 + zero, norm_mix, norm_ffn2, norm_final, a_log, dt_bias, dn_norm)
    loss_part, dx, g = _local_step(x[0], loss_target[0], sp, need, ahead, emit)
    small = _small_pack(g["norm_ffn1"], g["norm_mix"], g["norm_ffn2"], g["norm_final"], g["dn_norm"],
                        g["alog_row"], g["dtb_row"], loss_part)
    emit("small", {"small": jnp.broadcast_to(small[None], (N_DEV, SMALL_ROWS, 128))})

    pack = lambda a: _small_pack(*a)[None]
    res, after = {}, dx
    for group, names, gss, grs, gthru, gland in pending:
        thrus, lands = _xwait(gss, grs, gthru, gland, True, after, "grads_wait_" + group)
        for n, t, l in zip(names, thrus, lands):
            recv = own_slot(l, lax.dynamic_index_in_dim(t, me, 0, keepdims=False))
            if n == "small":
                res[n] = _adam(
                    recv,
                    pack((norm_ffn1, norm_mix, norm_ffn2, norm_final, dn_norm, _lane_row(a_log), _lane_row(dt_bias))),
                    pack((m_norm_ffn1, m_norm_mix, m_norm_ffn2, m_norm_final, m_dn_norm, _lane_row(m_a_log),
                          _lane_row(m_dt_bias))),
                    pack((v_norm_ffn1, v_norm_mix, v_norm_ffn2, v_norm_final, v_dn_norm, _lane_row(v_a_log),
                          _lane_row(v_dt_bias))),
                    SMALL_ROWS, "adam_small")
            elif n in TRANSPOSED:
                flip = lambda a: jnp.swapaxes(a, 1, 2)
                res[n] = [flip(o) for o in _adam(recv, flip(w[n]), flip(m[n]), flip(v[n]), F // N_DEV // 2,
                                                 "adam_" + n)]
            else:
                res[n] = _adam(recv, w[n], m[n], v[n], ADAM_TILE[n], "adam_" + n)
            after = res[n][0]
    res_s = res["small"]

    loss = res_s[0][0, 35, 0]
    outs = [loss, dx[None]]
    for k in range(4):
        n1, nm, n2, nf, dn, al, dt = _small_unpack(res_s[k])
        big = {n: res[n][k] for n in BIG}
        outs += [n1, big["gate1"], big["up1"], big["down1"], nm, big["w_in"], big["conv_w"], al, dt, dn,
                 big["w_out"], n2, big["gate2"], big["up2"], big["down2"], nf]
    return tuple(outs)
```

```python
import functools

import numpy as np
import jax
import jax.numpy as jnp
from jax import lax
from jax.experimental import pallas as pl
from jax.experimental.pallas import tpu as pltpu

T = 4096
D = 1024
F = 2816
N_DEV = 8
A_HEADS = 8
A_HD = 64
AW = A_HEADS * A_HD
DN_H = 4
DN_HD = 128
DNW = DN_H * DN_HD
CH = 64
PAIR = 2 * CH
ZA = 3 * AW
ZD = 3 * DNW + DNW + 256
ZP = ZA + ZD
BD_BLK = (3 * DNW + DNW) // 128
IN_COLS = 3592
PATTERNS = ((128, 1), (512, 4), (2048, 16))
NORM_EPS = 1e-6
L2_EPS = 1e-6
ADAM_LR, ADAM_B1, ADAM_B2, ADAM_EPS, ADAM_WD, ADAM_STEP = 0.001, 0.9, 0.999, 1e-08, 0.01, 10
VMEM_LIMIT = 56 * 1024 * 1024
NEG = -1e30

BF = jnp.bfloat16
F32 = jnp.float32
NN = (((1,), (0,)), ((), ()))
NT = (((1,), (1,)), ((), ()))
TN = (((0,), (0,)), ((), ()))
HI = lax.Precision.HIGHEST
MESH_ID = pl.DeviceIdType.MESH
ANY = pl.BlockSpec(memory_space=pl.ANY)


def _cp():
    return pltpu.CompilerParams(vmem_limit_bytes=VMEM_LIMIT)


def _dg(a, b, dims):
    return lax.dot_general(a, b, dims, preferred_element_type=F32)


def _hdot(a, b):
    return lax.dot_general(a, b, NN, precision=HI, preferred_element_type=F32)


def _make_bdot(dims, da_dims, da_swap, db_dims, db_swap):
    @jax.custom_vjp
    def f(a, b):
        return _dg(a.astype(BF), b.astype(BF), dims)

    def fwd(a, b):
        return f(a, b), (a, b)

    def bwd(res, g):
        a, b = res
        gb, ab, bb = g.astype(BF), a.astype(BF), b.astype(BF)
        da = _dg(bb, gb, da_dims) if da_swap else _dg(gb, bb, da_dims)
        db = _dg(gb, ab, db_dims) if db_swap else _dg(ab, gb, db_dims)
        return da, db

    f.defvjp(fwd, bwd)
    return f


_bdot_nn = _make_bdot(NN, NT, False, TN, False)
_bdot_nt = _make_bdot(NT, NN, False, TN, True)
_bdot_tn = _make_bdot(TN, NT, True, NN, False)


def _iota(shape, dim):
    return lax.broadcasted_iota(jnp.int32, shape, dim)


def _col(x, idx):
    return jnp.sum(jnp.where(_iota(x.shape, 1) == idx, x, 0.0), axis=1, keepdims=True)


def _mm_nn(a, b, out_dtype, tm, tn, name):
    m, k = a.shape
    n = b.shape[1]

    def body(a_ref, b_ref, o_ref):
        o_ref[...] = _dg(a_ref[...], b_ref[...], NN).astype(out_dtype)

    return pl.pallas_call(
        body, grid=(m // tm, n // tn),
        in_specs=[pl.BlockSpec((tm, k), lambda i, j: (i, 0)), pl.BlockSpec((k, tn), lambda i, j: (0, j))],
        out_specs=pl.BlockSpec((tm, tn), lambda i, j: (i, j)),
        out_shape=jax.ShapeDtypeStruct((m, n), out_dtype), name=name, compiler_params=_cp())(a, b)


def _tie(body, after):
    if after is None:
        return body, [], []
    return (lambda tok_ref, *refs: body(*refs)), [ANY], [after.reshape(1, 1)]


def _mm_nt(a, b, out_dtype, tm, tb, name, after=None):
    m, c = a.shape
    kb = b.shape[0]

    def body(a_ref, b_ref, o_ref):
        o_ref[...] = _dg(a_ref[...], b_ref[...], NT).astype(out_dtype)

    body, tspec, tok = _tie(body, after)
    return pl.pallas_call(
        body, grid=(m // tm, kb // tb),
        in_specs=tspec + [pl.BlockSpec((tm, c), lambda i, j: (i, 0)), pl.BlockSpec((tb, c), lambda i, j: (j, 0))],
        out_specs=pl.BlockSpec((tm, tb), lambda i, j: (i, j)),
        out_shape=jax.ShapeDtypeStruct((m, kb), out_dtype), name=name, compiler_params=_cp())(*tok, a, b)


def _mm_tn(a, b, out_dtype, ta, tb, name, after=None):
    m, ka = a.shape
    nb = b.shape[1]

    def body(a_ref, b_ref, o_ref):
        o_ref[...] = _dg(a_ref[...], b_ref[...], TN).astype(out_dtype)

    body, tspec, tok = _tie(body, after)
    return pl.pallas_call(
        body, grid=(ka // ta, nb // tb),
        in_specs=tspec + [pl.BlockSpec((m, ta), lambda i, j: (0, i)), pl.BlockSpec((m, tb), lambda i, j: (0, j))],
        out_specs=pl.BlockSpec((ta, tb), lambda i, j: (i, j)),
        out_shape=jax.ShapeDtypeStruct((ka, nb), out_dtype), name=name, compiler_params=_cp())(*tok, a, b)


def _mm_nt_swiglu(h, wu_t, gate, tm, tb, name, after=None):
    m, c = h.shape
    kb = wu_t.shape[0]

    def body(h_ref, w_ref, g_ref, u_ref, a_ref):
        u = _dg(h_ref[...], w_ref[...], NT)
        g = g_ref[...].astype(F32)
        u_ref[...] = u.astype(BF)
        a_ref[...] = (g * jax.nn.sigmoid(g) * u).astype(BF)

    body, tspec, tok = _tie(body, after)
    tile = pl.BlockSpec((tm, tb), lambda i, j: (i, j))
    return pl.pallas_call(
        body, grid=(m // tm, kb // tb),
        in_specs=tspec + [pl.BlockSpec((tm, c), lambda i, j: (i, 0)), pl.BlockSpec((tb, c), lambda i, j: (j, 0)),
                          tile],
        out_specs=[tile, tile], out_shape=[jax.ShapeDtypeStruct((m, kb), BF)] * 2,
        name=name, compiler_params=_cp())(*tok, h, wu_t, gate)


def _mm_nt_dswiglu(dys, wd, gate, up, tm, tb, name, after=None):
    m, c = dys.shape
    kb = wd.shape[0]

    def body(d_ref, w_ref, g_ref, u_ref, dg_ref, du_ref):
        da = _dg(d_ref[...], w_ref[...], NT)
        g = g_ref[...].astype(F32)
        u = u_ref[...].astype(F32)
        s = jax.nn.sigmoid(g)
        dg_ref[...] = (da * u * (s * (1.0 + g * (1.0 - s)))).astype(BF)
        du_ref[...] = (da * (g * s)).astype(BF)

    body, tspec, tok = _tie(body, after)
    tile = pl.BlockSpec((tm, tb), lambda i, j: (i, j))
    return pl.pallas_call(
        body, grid=(m // tm, kb // tb),
        in_specs=tspec + [pl.BlockSpec((tm, c), lambda i, j: (i, 0)), pl.BlockSpec((tb, c), lambda i, j: (j, 0)),
                          tile, tile],
        out_specs=[tile, tile], out_shape=[jax.ShapeDtypeStruct((m, kb), BF)] * 2,
        name=name, compiler_params=_cp())(*tok, dys, wd, gate, up)


def _rms_fwd(x, gain, name):
    tm = 512
    row = pl.BlockSpec((tm, D), lambda i: (i, 0))

    def body(x_ref, g_ref, h_ref):
        xv = x_ref[...]
        r = lax.rsqrt(jnp.mean(xv * xv, axis=-1, keepdims=True) + NORM_EPS)
        h_ref[...] = (xv * r * g_ref[...]).astype(BF)

    return pl.pallas_call(
        body, grid=(T // tm,), in_specs=[row, pl.BlockSpec((1, D), lambda i: (0, 0))], out_specs=row,
        out_shape=jax.ShapeDtypeStruct((T, D), BF), name=name, compiler_params=_cp())(x, gain)


def _mm_rms_bwd(pairs, dims, x, gain, dres, alpha_out, tm, name, after=None):
    n = len(pairs)
    row = pl.BlockSpec((tm, D), lambda i: (i, 0))
    gspec = pl.BlockSpec((1, D), lambda i: (0, 0))

    def body(*refs):
        x_ref, g_ref, dres_ref, dx_ref, dxs_ref, dg_ref = refs[2 * n:]
        i = pl.program_id(0)
        dhv = _dg(refs[0][...], refs[n][...], dims)
        for p in range(1, n):
            dhv = dhv + _dg(refs[p][...], refs[n + p][...], dims)
        xv = x_ref[...]
        r = lax.rsqrt(jnp.mean(xv * xv, axis=-1, keepdims=True) + NORM_EPS)
        xh = xv * r
        part = jnp.sum(dhv * xh, axis=0, keepdims=True)

        @pl.when(i == 0)
        def _():
            dg_ref[...] = part

        @pl.when(i > 0)
        def _():
            dg_ref[...] += part

        dxh = dhv * g_ref[...]
        dx = r * (dxh - xh * jnp.mean(dxh * xh, axis=-1, keepdims=True)) + dres_ref[...]
        dx_ref[...] = dx
        dxs_ref[...] = (alpha_out * dx).astype(BF)

    body, tspec, tok = _tie(body, after)
    return pl.pallas_call(
        body, grid=(T // tm,),
        in_specs=tspec + [pl.BlockSpec((tm, a.shape[1]), lambda i: (i, 0)) for a, _ in pairs]
        + [pl.BlockSpec(b.shape, lambda i: (0, 0)) for _, b in pairs] + [row, gspec, row],
        out_specs=[row, row, gspec],
        out_shape=[jax.ShapeDtypeStruct((T, D), F32), jax.ShapeDtypeStruct((T, D), BF),
                   jax.ShapeDtypeStruct((1, D), F32)],
        name=name, compiler_params=_cp())(*tok, *[a for a, _ in pairs], *[b for _, b in pairs], x, gain, dres)


def _down_loss_bwd(x_prev, act, wd, gain, target, name):
    tm = 512
    row = pl.BlockSpec((tm, D), lambda i: (i, 0))
    gspec = pl.BlockSpec((1, D), lambda i: (0, 0))
    lspec = pl.BlockSpec((8, 128), lambda i: (0, 0))

    def body(x_ref, a_ref, w_ref, g_ref, t_ref, dx_ref, dxs_ref, dg_ref, loss_ref):
        i = pl.program_id(0)
        xv = x_ref[...] + 0.5 * _dg(a_ref[...], w_ref[...], NN)
        r = lax.rsqrt(jnp.mean(xv * xv, axis=-1, keepdims=True) + NORM_EPS)
        xh = xv * r
        diff = xh * g_ref[...] - t_ref[...]
        lpart = 0.5 * jnp.sum(jnp.mean(diff * diff, axis=-1, keepdims=True), axis=0, keepdims=True)
        dy = diff * (1.0 / D)
        part = jnp.sum(dy * xh, axis=0, keepdims=True)

        @pl.when(i == 0)
        def _():
            dg_ref[...] = part
            loss_ref[...] = jnp.broadcast_to(lpart, (8, 128))

        @pl.when(i > 0)
        def _():
            dg_ref[...] += part
            loss_ref[...] += jnp.broadcast_to(lpart, (8, 128))

        dxh = dy * g_ref[...]
        dx = r * (dxh - xh * jnp.mean(dxh * xh, axis=-1, keepdims=True))
        dx_ref[...] = dx
        dxs_ref[...] = (0.5 * dx).astype(BF)

    return pl.pallas_call(
        body, grid=(T // tm,),
        in_specs=[row, pl.BlockSpec((tm, F), lambda i: (i, 0)), pl.BlockSpec((F, D), lambda i: (0, 0)), gspec, row],
        out_specs=[row, row, gspec, lspec],
        out_shape=[jax.ShapeDtypeStruct((T, D), F32), jax.ShapeDtypeStruct((T, D), BF),
                   jax.ShapeDtypeStruct((1, D), F32), jax.ShapeDtypeStruct((8, 128), F32)],
        name=name, compiler_params=_cp())(x_prev, act, wd, gain, target)


def _mm_nn_resnorm(pairs, x_prev, alpha, gain, tm, name, after=None):
    m = x_prev.shape[0]
    n = len(pairs)

    def body(*refs):
        x_ref, g_ref, xo_ref, h_ref = refs[2 * n:]
        y = _dg(refs[0][...], refs[n][...], NN)
        for i in range(1, n):
            y = y + _dg(refs[i][...], refs[n + i][...], NN)
        xv = x_ref[...] + alpha * y
        xo_ref[...] = xv
        r = lax.rsqrt(jnp.mean(xv * xv, axis=-1, keepdims=True) + NORM_EPS)
        h_ref[...] = (xv * r * g_ref[...]).astype(BF)

    body, tspec, tok = _tie(body, after)
    row = pl.BlockSpec((tm, D), lambda i: (i, 0))
    return pl.pallas_call(
        body, grid=(m // tm,),
        in_specs=tspec + [pl.BlockSpec((tm, a.shape[1]), lambda i: (i, 0)) for a, _ in pairs]
        + [pl.BlockSpec(b.shape, lambda i: (0, 0)) for _, b in pairs] + [row, pl.BlockSpec((1, D), lambda i: (0, 0))],
        out_specs=[row, row],
        out_shape=[jax.ShapeDtypeStruct((m, D), F32), jax.ShapeDtypeStruct((m, D), BF)],
        name=name, compiler_params=_cp())(*tok, *[a for a, _ in pairs], *[b for _, b in pairs], x_prev, gain)


def _ffn_up(h, need, ahead, tag):
    wg = need("g", h)
    gate = _mm_nt(h, wg, BF, 1024, 1408, tag + "_gate")
    wu = need("u", gate)
    up, act = _mm_nt_swiglu(h, wu, gate, 1024, 1408, tag + "_up_act", after=ahead("d", wu))
    wd = need("d", up)
    return act, (h, gate, up, act, wg, wu, wd)


def _ffn_bwd(x_in, gain, saved, dxo, dys, alpha_out, emit, tag):
    h, gate, up, act, wg, wu, wd = saved
    sent = emit("d", _mm_tn(act, dys, BF, 256, 1024, tag + "_dwd"))
    dgate, dup = _mm_nt_dswiglu(dys, wd, gate, up, 1024, 1408, tag + "_dact", after=sent)
    sent = emit("g", _mm_tn(dgate, h, BF, 256, 1024, tag + "_dwg"))
    sent = emit("u", _mm_tn(dup, h, BF, 256, 1024, tag + "_dwu", after=sent))
    return _mm_rms_bwd([(dgate, wg), (dup, wu)], NN, x_in, gain, dxo, alpha_out, 512, tag + "_dh_dnorm",
                       after=sent)


SLAB = 2048
N_SLAB = T // SLAB
N_PAIR = A_HEADS // 2


def _pair_masks():
    lane = _iota((128, 128), 1)
    return lane < A_HD, lane >= A_HD


def _slope_table():
    h = 2 * jnp.arange(N_PAIR)[:, None] + jnp.minimum(jnp.arange(8), 1)[None, :]
    return jnp.broadcast_to((2.0 ** (-(h + 1).astype(F32)))[:, :, None], (N_PAIR, 8, 128))


def _rows(ref, start, d):
    if d == 1:
        return ref[pl.ds(start, 128), :]
    return ref[pl.ds(start, 128, stride=d), :]


def _put_rows(ref, start, d, val):
    if d == 1:
        ref[pl.ds(start, 128), :] = val
    else:
        ref[pl.ds(start, 128, stride=d), :] = val


def _units(d):
    return [(r, b, r + 128 * d * b) for r in range(d) for b in range(SLAB // (128 * d))]


def _biases(d, slopes, has_prev):
    qi = _iota((128, 256), 0)
    kj = _iota((128, 256), 1)
    steps = qi + 128 - kj
    in_band = (steps >= 0) & (steps <= 128)
    dist = (steps * d).astype(F32)
    base = [jnp.where(in_band, -(sl * dist), NEG) for sl in slopes]
    edge = [jnp.where(has_prev | (kj >= 128), b, NEG) for b in base]
    return base, edge


def _attn_fwd(z_at, name, after=None):
    def body(sl_ref, q_ref, kc_ref, kp_ref, vc_ref, vp_ref, of_ref, ob_ref, lse_ref, m_s, l_s, a_s):
        n = pl.program_id(1)
        lo, hi = _pair_masks()
        slopes = (sl_ref[0, 0:1, 0:1], sl_ref[0, 1:2, 0:1])

        def unit(d, start, b, first, carry, bias):
            q = _rows(q_ref, start, d).astype(BF)
            kcur, vcur = _rows(kc_ref, start, d).astype(BF), _rows(vc_ref, start, d).astype(BF)
            if b > 0:
                kprev, vprev = carry
            else:
                pstart = start + SLAB - 128 * d
                kprev, vprev = _rows(kp_ref, pstart, d).astype(BF), _rows(vp_ref, pstart, d).astype(BF)
            kcat = jnp.concatenate([kprev, kcur], axis=0)
            vcat = jnp.concatenate([vprev, vcur], axis=0)
            ms, ls, pvs = [], [], []
            for e in range(2):
                qm = jnp.where(lo if e == 0 else hi, q, jnp.zeros_like(q))
                s = _dg(qm, kcat, NT) * (A_HD ** -0.5) + bias[e]
                m = jnp.max(s, axis=1, keepdims=True)
                p = jnp.exp(s - m)
                ms.append(m)
                ls.append(jnp.sum(p, axis=1, keepdims=True))
                pvs.append(_dg(p.astype(BF), vcat, NN))
            m_u = jnp.where(lo, ms[0], ms[1])
            l_u = jnp.where(lo, ls[0], ls[1])
            a_u = jnp.where(lo, pvs[0], pvs[1])
            if first:
                m_n, l_n, a_n = m_u, l_u, a_u
            else:
                m_o = _rows(m_s, start, d)
                m_n = jnp.maximum(m_o, m_u)
                c_o = jnp.exp(m_o - m_n)
                c_u = jnp.exp(m_u - m_n)
                l_n = _rows(l_s, start, d) * c_o + l_u * c_u
                a_n = _rows(a_s, start, d) * c_o + a_u * c_u
            _put_rows(m_s, start, d, m_n)
            _put_rows(l_s, start, d, l_n)
            _put_rows(a_s, start, d, a_n)
            return kcur, vcur

        for pi, (_, d) in enumerate(PATTERNS):
            base, edge = _biases(d, slopes, n > 0)
            carry = None
            for r, b, start in _units(d):
                carry = unit(d, start, b, pi == 0, carry, edge if b == 0 else base)
        l = l_s[...]
        out = a_s[...] / l
        of_ref[...] = out
        ob_ref[...] = out.astype(BF)
        lse_ref[...] = m_s[...] + jnp.log(l)

    body, tspec, tok = _tie(body, after)
    cur = lambda c: pl.BlockSpec((SLAB, 128), lambda j, n: (n, c * N_PAIR + j))
    prv = lambda c: pl.BlockSpec((SLAB, 128), lambda j, n: (jnp.maximum(n - 1, 0), c * N_PAIR + j))
    out = pl.BlockSpec((SLAB, 128), lambda j, n: (n, j))
    return pl.pallas_call(
        body, grid=(N_PAIR, N_SLAB),
        in_specs=tspec + [pl.BlockSpec((1, 8, 128), lambda j, n: (j, 0, 0)), cur(0), cur(1), prv(1), cur(2), prv(2)],
        out_specs=[out, out, out],
        out_shape=[jax.ShapeDtypeStruct((T, AW), F32), jax.ShapeDtypeStruct((T, AW), BF),
                   jax.ShapeDtypeStruct((T, AW), F32)],
        scratch_shapes=[pltpu.VMEM((SLAB, 128), F32)] * 3,
        name=name, compiler_params=_cp())(*tok, _slope_table(), z_at, z_at, z_at, z_at, z_at)


def _attn_bwd(z_at, dout, out, lse, name):
    def body(sl_ref, q_ref, kc_ref, kp_ref, vc_ref, vp_ref, do_ref, o_ref, lse_ref, dq_ref, dk_ref, dv_ref,
             dq_s, dk_s, dv_s, ck_s, cv_s):
        step = pl.program_id(1)
        n = N_SLAB - 1 - step
        lo, hi = _pair_masks()
        slopes = (sl_ref[0, 0:1, 0:1], sl_ref[0, 1:2, 0:1])

        @pl.when(step == 0)
        def _():
            ck_s[...] = jnp.zeros_like(ck_s)
            cv_s[...] = jnp.zeros_like(cv_s)

        dk_s[...] = ck_s[...]
        dv_s[...] = cv_s[...]
        ck_s[...] = jnp.zeros_like(ck_s)
        cv_s[...] = jnp.zeros_like(cv_s)

        def add_rows(ref, start, d, val):
            _put_rows(ref, start, d, _rows(ref, start, d) + val)

        def unit(d, start, b, first, carry, bias):
            q = _rows(q_ref, start, d).astype(BF)
            do_f = _rows(do_ref, start, d)
            do = do_f.astype(BF)
            prod = do_f * _rows(o_ref, start, d)
            lse_u = _rows(lse_ref, start, d)
            kcur, vcur = _rows(kc_ref, start, d).astype(BF), _rows(vc_ref, start, d).astype(BF)
            if b > 0:
                kprev, vprev = carry
            else:
                pstart = start + SLAB - 128 * d
                kprev, vprev = _rows(kp_ref, pstart, d).astype(BF), _rows(vp_ref, pstart, d).astype(BF)
            kcat = jnp.concatenate([kprev, kcur], axis=0)
            vcat = jnp.concatenate([vprev, vcur], axis=0)
            masks = (lo, hi)
            qms = [jnp.where(msk, q, jnp.zeros_like(q)) for msk in masks]
            doms = [jnp.where(msk, do, jnp.zeros_like(do)) for msk in masks]
            deltas = [jnp.sum(jnp.where(msk, prod, 0.0), axis=1, keepdims=True) for msk in masks]
            ss = [_dg(qm, kcat, NT) * (A_HD ** -0.5) + bs for qm, bs in zip(qms, bias)]
            dps = [_dg(dom, vcat, NT) for dom in doms]
            ps = [jnp.exp(s - lse_u[:, 64 * e:64 * e + 1]) for e, s in enumerate(ss)]
            dss = [(p * (dp - delta)).astype(BF) for p, dp, delta in zip(ps, dps, deltas)]
            pbs = [p.astype(BF) for p in ps]
            dqs = [_dg(ds, kcat, NN) for ds in dss]
            dkc = (_dg(dss[0], qms[0], TN) + _dg(dss[1], qms[1], TN)) * (A_HD ** -0.5)
            dvc = _dg(pbs[0], doms[0], TN) + _dg(pbs[1], doms[1], TN)
            dq_u = jnp.where(lo, dqs[0], dqs[1]) * (A_HD ** -0.5)
            if first:
                _put_rows(dq_s, start, d, dq_u)
            else:
                add_rows(dq_s, start, d, dq_u)
            add_rows(dk_s, start, d, dkc[128:])
            add_rows(dv_s, start, d, dvc[128:])
            if b > 0:
                add_rows(dk_s, start - 128 * d, d, dkc[:128])
                add_rows(dv_s, start - 128 * d, d, dvc[:128])
            else:
                pstart = start + SLAB - 128 * d
                add_rows(ck_s, pstart, d, dkc[:128])
                add_rows(cv_s, pstart, d, dvc[:128])
            return kcur, vcur

        for pi, (_, d) in enumerate(PATTERNS):
            base, edge = _biases(d, slopes, n > 0)
            carry = None
            for r, b, start in _units(d):
                carry = unit(d, start, b, pi == 0, carry, edge if b == 0 else base)
        dq_ref[...] = dq_s[...].astype(BF)
        dk_ref[...] = dk_s[...].astype(BF)
        dv_ref[...] = dv_s[...].astype(BF)

    rev = lambda n: N_SLAB - 1 - n
    cur = lambda c: pl.BlockSpec((SLAB, 128), lambda j, n: (rev(n), c * N_PAIR + j))
    prv = lambda c: pl.BlockSpec((SLAB, 128), lambda j, n: (jnp.maximum(rev(n) - 1, 0), c * N_PAIR + j))
    one = pl.BlockSpec((SLAB, 128), lambda j, n: (rev(n), j))
    return pl.pallas_call(
        body, grid=(N_PAIR, N_SLAB),
        in_specs=[pl.BlockSpec((1, 8, 128), lambda j, n: (j, 0, 0)), cur(0), cur(1), prv(1), cur(2), prv(2),
                  one, one, one],
        out_specs=[one, one, one], out_shape=[jax.ShapeDtypeStruct((T, AW), BF)] * 3,
        scratch_shapes=[pltpu.VMEM((SLAB, 128), F32)] * 5,
        name=name, compiler_params=_cp())(_slope_table(), z_at, z_at, z_at, z_at, z_at, dout, out, lse)


def _silu(x):
    return x * jax.nn.sigmoid(x)


def _qk_math(c):
    s = _silu(c)
    return s * lax.rsqrt(jnp.sum(s * s, axis=-1, keepdims=True) + L2_EPS)


def _softplus(x):
    return jnp.maximum(x, 0.0) + jnp.log(1.0 + jnp.exp(-jnp.abs(x)))


def _gate_math(bd, alog_row, dtb_row):
    rows = bd.shape[0]
    lane = _iota(bd.shape, 1)
    beta = jax.nn.sigmoid(bd)
    g = jnp.where((lane >= DN_H) & (lane < 2 * DN_H), -jnp.exp(alog_row) * _softplus(bd + dtb_row), 0.0)
    ri = _iota((rows, rows), 0)
    ci = _iota((rows, rows), 1)
    same = (ri // CH) == (ci // CH)
    li = _iota((128, 128), 0)
    lj = _iota((128, 128), 1)
    to_next_group = jnp.where((lj == li + DN_H) & (li >= DN_H) & (li < 2 * DN_H), 1.0, 0.0)
    gc = _hdot(jnp.where(same & (ci <= ri), 1.0, 0.0), g)
    glast = _hdot(_hdot(jnp.where(same, 1.0, 0.0), g), to_next_group)
    return jnp.where(lane < DN_H, beta, 0.0) + gc + glast


def _shift_down(cur, halo, s):
    if s == 0:
        return cur
    rolled = pltpu.roll(cur, s, 0)
    hr = pltpu.roll(halo, s, 0)
    head = jnp.where(_iota(hr.shape, 0) < s, hr, rolled[:8])
    return jnp.concatenate([head, rolled[8:]], axis=0)


def _shift_up(cur, halo, s):
    if s == 0:
        return cur
    rows = cur.shape[0]
    rolled = pltpu.roll(cur, rows - s, 0)
    hr = pltpu.roll(halo, 8 - s, 0)
    tail = jnp.where(_iota(hr.shape, 0) >= 8 - s, hr, rolled[rows - 8:])
    return jnp.concatenate([rolled[:rows - 8], tail], axis=0)


def _dn_pre_fwd(z_dn, conv_w8, alog_row, dtb_row, name):
    tm = 256
    wq = 3 * DNW

    def body(raw_ref, halo_ref, bd_ref, w_ref, al_ref, dt_ref, conv_ref, qkv_ref, bg_ref):
        i = pl.program_id(0)
        cur = raw_ref[...]
        halo = jnp.where(i > 0, halo_ref[...], 0.0)
        w = w_ref[...]
        conv = jnp.zeros((tm, wq), F32)
        for j in range(4):
            conv = conv + _shift_down(cur, halo, 3 - j) * w[j:j + 1, :]
        conv_ref[...] = conv
        for blk in range(3 * DN_H):
            sl = slice(128 * blk, 128 * blk + 128)
            c = conv[:, sl]
            qkv_ref[:, sl] = _qk_math(c) if blk < 2 * DN_H else _silu(c)
        bg_ref[...] = _gate_math(bd_ref[...], al_ref[...], dt_ref[...])

    one = pl.BlockSpec((1, 128), lambda i: (0, 0))
    return pl.pallas_call(
        body, grid=(T // tm,),
        in_specs=[pl.BlockSpec((tm, wq), lambda i: (i, 0)),
                  pl.BlockSpec((8, wq), lambda i: (jnp.maximum(i * (tm // 8) - 1, 0), 0)),
                  pl.BlockSpec((tm, 128), lambda i: (i, BD_BLK)),
                  pl.BlockSpec((8, wq), lambda i: (0, 0)), one, one],
        out_specs=[pl.BlockSpec((tm, wq), lambda i: (i, 0)), pl.BlockSpec((tm, wq), lambda i: (i, 0)),
                   pl.BlockSpec((tm, 128), lambda i: (i, 0))],
        out_shape=[jax.ShapeDtypeStruct((T, wq), F32), jax.ShapeDtypeStruct((T, wq), F32),
                   jax.ShapeDtypeStruct((T, 128), F32)],
        name=name, compiler_params=_cp())(z_dn, z_dn, z_dn, conv_w8, alog_row, dtb_row)


def _dn_pre_bwd(conv, z_dn, alog_row, dtb_row, dqn, dkn, dvn, dbg, name):
    tm = 256
    wq = 3 * DNW

    def body(conv_ref, bd_ref, al_ref, dt_ref, dq_ref, dk_ref, dv_ref, dbg_ref,
             dconv_ref, dbd_ref, dal_ref, ddt_ref):
        i = pl.program_id(0)
        for blk in range(3 * DN_H):
            sl = slice(128 * blk, 128 * blk + 128)
            src = (dq_ref, dk_ref, dv_ref)[blk // DN_H]
            ct = src[:, 128 * (blk % DN_H):128 * (blk % DN_H) + 128]
            fn = _qk_math if blk < 2 * DN_H else _silu
            _, vjp = jax.vjp(fn, conv_ref[:, sl])
            dconv_ref[:, sl] = vjp(ct)[0]
        _, vjp = jax.vjp(_gate_math, bd_ref[...], al_ref[...], dt_ref[...])
        dbd, dal, ddt = vjp(dbg_ref[...])
        dbd_ref[...] = dbd

        @pl.when(i == 0)
        def _():
            dal_ref[...] = dal
            ddt_ref[...] = ddt

        @pl.when(i > 0)
        def _():
            dal_ref[...] += dal
            ddt_ref[...] += ddt

    one = pl.BlockSpec((1, 128), lambda i: (0, 0))
    row = pl.BlockSpec((tm, wq), lambda i: (i, 0))
    hd = pl.BlockSpec((tm, DNW), lambda i: (i, 0))
    st = pl.BlockSpec((tm, 128), lambda i: (i, 0))
    return pl.pallas_call(
        body, grid=(T // tm,),
        in_specs=[row, pl.BlockSpec((tm, 128), lambda i: (i, BD_BLK)), one, one, hd, hd, hd, st],
        out_specs=[row, st, one, one],
        out_shape=[jax.ShapeDtypeStruct((T, wq), F32), jax.ShapeDtypeStruct((T, 128), F32),
                   jax.ShapeDtypeStruct((1, 128), F32), jax.ShapeDtypeStruct((1, 128), F32)],
        name=name, compiler_params=_cp())(conv, z_dn, alog_row, dtb_row, dqn, dkn, dvn, dbg)


def _dn_conv_bwd(dconv, z_dn, conv_w8, name):
    tm = 256
    wq = 3 * DNW
    last = T // tm - 1

    def body(dc_ref, dcn_ref, raw_ref, halo_ref, w_ref, draw_ref, dw_ref):
        i = pl.program_id(0)
        dc = dc_ref[...]
        nxt = jnp.where(i < last, dcn_ref[...], 0.0)
        cur = raw_ref[...]
        halo = jnp.where(i > 0, halo_ref[...], 0.0)
        w = w_ref[...]
        draw = jnp.zeros((tm, wq), F32)
        rows = []
        for j in range(4):
            draw = draw + _shift_up(dc, nxt, 3 - j) * w[j:j + 1, :]
            rows.append(jnp.sum(dc * _shift_down(cur, halo, 3 - j), axis=0, keepdims=True))
        draw_ref[...] = draw
        part = jnp.concatenate(rows + [jnp.zeros((4, wq), F32)], axis=0)

        @pl.when(i == 0)
        def _():
            dw_ref[...] = part

        @pl.when(i > 0)
        def _():
            dw_ref[...] += part

    row = pl.BlockSpec((tm, wq), lambda i: (i, 0))
    return pl.pallas_call(
        body, grid=(T // tm,),
        in_specs=[row, pl.BlockSpec((8, wq), lambda i: (jnp.minimum((i + 1) * (tm // 8), T // 8 - 1), 0)),
                  row, pl.BlockSpec((8, wq), lambda i: (jnp.maximum(i * (tm // 8) - 1, 0), 0)),
                  pl.BlockSpec((8, wq), lambda i: (0, 0))],
        out_specs=[row, pl.BlockSpec((8, wq), lambda i: (0, 0))],
        out_shape=[jax.ShapeDtypeStruct((T, wq), F32), jax.ShapeDtypeStruct((8, wq), F32)],
        name=name, compiler_params=_cp())(dconv, dconv, z_dn, z_dn, conv_w8)


def _h3(a, b, dims=NN):
    return lax.dot_general(a, b, dims, precision=lax.Precision.HIGH, preferred_element_type=F32)


@jax.custom_vjp
def _inverse_given(a_mat, tinv):
    return tinv


def _inverse_given_fwd(a_mat, tinv):
    return tinv, tinv


def _inverse_given_bwd(tinv, g):
    return -_bdot_tn(tinv, _bdot_nt(g, tinv)), jnp.zeros_like(tinv)


_inverse_given.defvjp(_inverse_given_fwd, _inverse_given_bwd)


@jax.custom_vjp
def _h3_lo(a, b):
    return _h3(a, b)


def _h3_lo_fwd(a, b):
    return _h3(a, b), (a, b)


def _h3_lo_bwd(res, g):
    a, b = res
    gb = g.astype(BF)
    return _dg(gb, b.astype(BF), NT), _dg(a.astype(BF), gb, TN)


_h3_lo.defvjp(_h3_lo_fwd, _h3_lo_bwd)


def _prep_head(q, k, v, bgc, h):
    beta = _col(bgc, h)
    gc = jnp.broadcast_to(_col(bgc, DN_H + h), (PAIR, 128))
    glast = jnp.broadcast_to(_col(bgc, 2 * DN_H + h), (PAIR, 128))
    ri = _iota((PAIR, PAIR), 0)
    ci = _iota((PAIR, PAIR), 1)
    same = (ri // CH) == (ci // CH)
    causal = same & (ci <= ri)
    strict = same & (ci < ri)
    eye = ri == ci
    gc_cols = _hdot(jnp.ones((PAIR, PAIR), F32), jnp.where(eye, gc, 0.0))
    decay = jnp.exp(jnp.where(causal, gc - gc_cols, NEG))
    egc = jnp.exp(gc)
    kb = k * beta
    a_mat = jnp.where(strict, _bdot_nt(kb, k) * decay, 0.0)
    qs = q * (DN_HD ** -0.5)
    attn = jnp.where(causal, _bdot_nt(qs, k) * decay, 0.0)
    return a_mat, (v * beta, kb * egc, qs * egc, k * jnp.exp(glast - gc), attn, jnp.exp(glast))


def _prep_tail(tinv, ctx):
    vb, kbe, qg, kdec, attn, decb = ctx
    return _h3_lo(tinv, vb), _h3_lo(tinv, kbe), qg, kdec, attn, decb


def _inverses(a_mats):
    eye = jnp.where(_iota((PAIR, PAIR), 0) == _iota((PAIR, PAIR), 1), 1.0, 0.0)
    ps = [-a for a in a_mats]
    tinvs = [eye + p for p in ps]
    for _ in range(5):
        ps = [_h3(p, p) for p in ps]
        tinvs = [t + _h3(t, p) for t, p in zip(tinvs, ps)]
    return tinvs


def _dn_prep_fwd(qkv, bg, name):
    rows = 512
    hd = lambda off: pl.BlockSpec((rows, 128), lambda g, h: (g, off + h))
    out = pl.BlockSpec((rows, 128), lambda g, h: (g, h))

    def body(q_ref, k_ref, v_ref, bg_ref, *outs):
        h = pl.program_id(1)
        spans = [slice(PAIR * pr, PAIR * pr + PAIR) for pr in range(rows // PAIR)]
        heads = [_prep_head(q_ref[rs, :], k_ref[rs, :], v_ref[rs, :], bg_ref[rs, :], h) for rs in spans]
        tinvs = _inverses([a for a, _ in heads])
        for rs, tinv, (_, ctx) in zip(spans, tinvs, heads):
            for o_ref, val in zip(outs, _prep_tail(tinv, ctx) + (tinv,)):
                o_ref[rs, :] = val

    return pl.pallas_call(
        body, grid=(T // rows, DN_H),
        in_specs=[hd(0), hd(DN_H), hd(2 * DN_H), pl.BlockSpec((rows, 128), lambda g, h: (g, 0))],
        out_specs=[out] * 7, out_shape=[jax.ShapeDtypeStruct((T, DNW), F32)] * 7,
        name=name, compiler_params=_cp())(qkv, qkv, qkv, bg)


def _dn_prep_bwd(qkv, bg, tinv, cts, name):
    rows = 512
    hd = lambda off: pl.BlockSpec((rows, 128), lambda g, h: (g, off + h))
    out = pl.BlockSpec((rows, 128), lambda g, h: (g, h))
    st = pl.BlockSpec((rows, 128), lambda g, h: (g, 0))

    def body(q_ref, k_ref, v_ref, bg_ref, ti_ref, c0, c1, c2, c3, c4, c5, dq_ref, dk_ref, dv_ref, dbg_ref):
        h = pl.program_id(1)
        spans = [slice(PAIR * pr, PAIR * pr + PAIR) for pr in range(rows // PAIR)]
        tis = [ti_ref[rs, :] for rs in spans]

        def joint(qs, ks, vs, bs):
            heads = [_prep_head(q, k, v, b, h) for q, k, v, b in zip(qs, ks, vs, bs)]
            return [_prep_tail(_inverse_given(a, ti), ctx) for (a, ctx), ti in zip(heads, tis)]

        _, vjp = jax.vjp(joint, *[[r[rs, :] for rs in spans] for r in (q_ref, k_ref, v_ref, bg_ref)])
        dqs, dks, dvs, dbs = vjp([tuple(c[rs, :] for c in (c0, c1, c2, c3, c4, c5)) for rs in spans])
        for rs, dq, dk, dv in zip(spans, dqs, dks, dvs):
            dq_ref[rs, :] = dq
            dk_ref[rs, :] = dk
            dv_ref[rs, :] = dv
        dbg_all = jnp.concatenate(dbs, axis=0)

        @pl.when(h == 0)
        def _():
            dbg_ref[...] = dbg_all

        @pl.when(h > 0)
        def _():
            dbg_ref[...] += dbg_all

    return pl.pallas_call(
        body, grid=(T // rows, DN_H),
        in_specs=[hd(0), hd(DN_H), hd(2 * DN_H), st] + [out] * 7,
        out_specs=[out, out, out, st],
        out_shape=[jax.ShapeDtypeStruct((T, DNW), F32)] * 3 + [jax.ShapeDtypeStruct((T, 128), F32)],
        name=name, compiler_params=_cp())(qkv, qkv, qkv, bg, tinv, *cts)


def _step_math(s, u, w, qg, kdec, attn, decb, sub):
    vnew = u - _bdot_nn(w, s)
    z = jnp.zeros((CH, 128), F32)
    vfull = jnp.concatenate([vnew, z] if sub == 0 else [z, vnew], axis=0)
    o = _bdot_nn(qg, s) + _bdot_nn(attn, vfull)
    dec = jnp.sum(decb, axis=0, keepdims=True) * (1.0 / CH)
    return s * dec + _bdot_tn(kdec, vnew), o


def _dn_scan_fwd(prep, name):
    npair = T // PAIR
    row = pl.BlockSpec((PAIR, DNW), lambda p: (p, 0))

    def body(u_ref, w_ref, qg_ref, kd_ref, at_ref, db_ref, o_ref, ss_ref, s_ref):
        @pl.when(pl.program_id(0) == 0)
        def _():
            s_ref[...] = jnp.zeros_like(s_ref)

        states = [s_ref[h] for h in range(DN_H)]
        for sub in range(2):
            rs = slice(CH * sub, CH * sub + CH)
            for h in range(DN_H):
                ls = slice(128 * h, 128 * h + 128)
                ss_ref[sub, h] = states[h]
                states[h], o = _step_math(states[h], u_ref[rs, ls], w_ref[rs, ls], qg_ref[rs, ls],
                                          kd_ref[rs, ls], at_ref[rs, ls], db_ref[rs, ls], sub)
                o_ref[rs, ls] = o
        for h in range(DN_H):
            s_ref[h] = states[h]

    return pl.pallas_call(
        body, grid=(npair,), in_specs=[row] * 6,
        out_specs=[row, pl.BlockSpec((2, DN_H, 128, 128), lambda p: (p, 0, 0, 0))],
        out_shape=[jax.ShapeDtypeStruct((T, DNW), F32), jax.ShapeDtypeStruct((T // CH, DN_H, 128, 128), F32)],
        scratch_shapes=[pltpu.VMEM((DN_H, 128, 128), F32)],
        name=name, compiler_params=_cp())(*prep)


def _dn_scan_bwd(prep, states, do, name):
    npair = T // PAIR
    row = pl.BlockSpec((PAIR, DNW), lambda p: (npair - 1 - p, 0))

    def body(u_ref, w_ref, qg_ref, kd_ref, at_ref, db_ref, ss_ref, do_ref, *rest):
        outs, ds_ref = rest[:6], rest[6]

        @pl.when(pl.program_id(0) == 0)
        def _():
            ds_ref[...] = jnp.zeros_like(ds_ref)

        dss = [ds_ref[h] for h in range(DN_H)]
        for sub in (1, 0):
            rs = slice(CH * sub, CH * sub + CH)
            for h in range(DN_H):
                ls = slice(128 * h, 128 * h + 128)
                args = (ss_ref[sub, h],) + tuple(r[rs, ls] for r in (u_ref, w_ref, qg_ref, kd_ref, at_ref, db_ref))
                _, vjp = jax.vjp(functools.partial(_step_math, sub=sub), *args)
                cts = vjp((dss[h], do_ref[rs, ls]))
                dss[h] = cts[0]
                for o_ref, val in zip(outs, cts[1:]):
                    o_ref[rs, ls] = val
        for h in range(DN_H):
            ds_ref[h] = dss[h]

    return pl.pallas_call(
        body, grid=(npair,),
        in_specs=[row] * 6 + [pl.BlockSpec((2, DN_H, 128, 128), lambda p: (npair - 1 - p, 0, 0, 0)), row],
        out_specs=[row] * 6, out_shape=[jax.ShapeDtypeStruct((T, DNW), F32)] * 6,
        scratch_shapes=[pltpu.VMEM((DN_H, 128, 128), F32)],
        name=name, compiler_params=_cp())(*prep, states, do)


def _post_math(o, gate, wrow):
    return o * lax.rsqrt(jnp.mean(o * o, axis=-1, keepdims=True) + NORM_EPS) * wrow * _silu(gate)


def _dn_post_fwd(o, z_dn, dn_norm, name):
    tm = 512
    row = pl.BlockSpec((tm, DNW), lambda i: (i, 0))

    def body(o_ref, g_ref, w_ref, y_ref):
        for h in range(DN_H):
            ls = slice(128 * h, 128 * h + 128)
            y_ref[:, ls] = _post_math(o_ref[:, ls], g_ref[:, ls], w_ref[...]).astype(BF)

    return pl.pallas_call(
        body, grid=(T // tm,),
        in_specs=[row, pl.BlockSpec((tm, DNW), lambda i: (i, 3)), pl.BlockSpec((1, 128), lambda i: (0, 0))],
        out_specs=row, out_shape=jax.ShapeDtypeStruct((T, DNW), BF),
        name=name, compiler_params=_cp())(o, z_dn, dn_norm)


def _dn_post_bwd(o, z_dn, dn_norm, dy, name):
    tm = 512
    row = pl.BlockSpec((tm, DNW), lambda i: (i, 0))
    one = pl.BlockSpec((1, 128), lambda i: (0, 0))

    def body(o_ref, g_ref, w_ref, dy_ref, do_ref, dg_ref, dw_ref):
        i = pl.program_id(0)
        dw = jnp.zeros((1, 128), F32)
        for h in range(DN_H):
            ls = slice(128 * h, 128 * h + 128)
            _, vjp = jax.vjp(_post_math, o_ref[:, ls], g_ref[:, ls], w_ref[...])
            do, dg, dwh = vjp(dy_ref[:, ls].astype(F32))
            do_ref[:, ls] = do
            dg_ref[:, ls] = dg
            dw = dw + dwh

        @pl.when(i == 0)
        def _():
            dw_ref[...] = dw

        @pl.when(i > 0)
        def _():
            dw_ref[...] += dw

    return pl.pallas_call(
        body, grid=(T // tm,),
        in_specs=[row, pl.BlockSpec((tm, DNW), lambda i: (i, 3)), one, pl.BlockSpec((tm, DNW), lambda i: (i, 1))],
        out_specs=[row, row, one],
        out_shape=[jax.ShapeDtypeStruct((T, DNW), F32), jax.ShapeDtypeStruct((T, DNW), F32),
                   jax.ShapeDtypeStruct((1, 128), F32)],
        name=name, compiler_params=_cp())(o, z_dn, dn_norm, dy)


def _dz_dn_assemble(draw, dgate, dbd, name):
    tm = 512

    def body(a_ref, b_ref, c_ref, o_ref):
        o_ref[:, :3 * DNW] = a_ref[...].astype(BF)
        o_ref[:, 3 * DNW:4 * DNW] = b_ref[...].astype(BF)
        o_ref[:, 4 * DNW:4 * DNW + 128] = c_ref[...].astype(BF)
        o_ref[:, 4 * DNW + 128:] = jnp.zeros((tm, 128), BF)

    return pl.pallas_call(
        body, grid=(T // tm,),
        in_specs=[pl.BlockSpec((tm, 3 * DNW), lambda i: (i, 0)), pl.BlockSpec((tm, DNW), lambda i: (i, 0)),
                  pl.BlockSpec((tm, 128), lambda i: (i, 0))],
        out_specs=pl.BlockSpec((tm, ZD), lambda i: (i, 0)),
        out_shape=jax.ShapeDtypeStruct((T, ZD), BF), name=name, compiler_params=_cp())(draw, dgate, dbd)


HBM = pl.BlockSpec(memory_space=pltpu.HBM)
SEM = pl.BlockSpec(memory_space=pltpu.SEMAPHORE)
EFFECT = pltpu.SideEffectType.DATAFLOW_SIDE_EFFECTING
N_PEER = N_DEV - 1


ALL_PEERS = (1, 2, 4, 3, 5, 6, 7)
FIRST_HOP = (1, 2, 4, 6)
FORWARDED = (2, 4, 6)


def _peers(x, y, c, ks=ALL_PEERS):
    return [(k, (x ^ (k >> 2), y ^ ((k >> 1) & 1), c ^ (k & 1))) for k in ks]


def _exchange_copy(ins, lands, ssems, rsems, scatter, t, k, pos, me):
    px, py, pc = pos
    src = ins[t].at[4 * px + 2 * py + pc] if scatter else ins[t]
    return pltpu.make_async_remote_copy(
        src_ref=src, dst_ref=lands[t].at[me], send_sem=ssems[t].at[k - 1], recv_sem=rsems[t].at[k - 1],
        device_id=pos, device_id_type=MESH_ID)


def _xstart(bufs, scatter, name, ks=ALL_PEERS):
    nt = len(bufs)
    lands = [lax.empty((N_DEV,) + tuple(b.shape[1:] if scatter else b.shape), b.dtype) for b in bufs]

    def body(*refs):
        ins, lnd = refs[:nt], refs[nt:2 * nt]
        ssems, rsems = refs[2 * nt:3 * nt], refs[3 * nt:4 * nt]
        token = refs[-1]
        x, y, c = lax.axis_index("x"), lax.axis_index("y"), lax.axis_index("c")
        me = 4 * x + 2 * y + c
        for t in range(nt):
            for k, pos in _peers(x, y, c, ks):
                _exchange_copy(ins, lnd, ssems, rsems, scatter, t, k, pos, me).start()
        token[...] = jnp.zeros_like(token)

    both = list(bufs) + lands
    res = pl.pallas_call(
        body, name=name,
        out_shape=[pltpu.SemaphoreType.DMA((N_PEER,))] * (2 * nt)
        + [pltpu.HBM(b.shape, b.dtype) for b in both] + [jax.ShapeDtypeStruct((8, 128), F32)],
        in_specs=[HBM] * (2 * nt),
        out_specs=[SEM] * (2 * nt) + [HBM] * (2 * nt) + [pl.BlockSpec(memory_space=pltpu.VMEM)],
        input_output_aliases={i: 2 * nt + i for i in range(2 * nt)},
        compiler_params=pltpu.CompilerParams(has_side_effects=EFFECT),
    )(*[pltpu.with_memory_space_constraint(b, pltpu.HBM) for b in both])
    return res[:nt], res[nt:2 * nt], res[2 * nt:3 * nt], res[3 * nt:4 * nt], res[-1][0, 0]


def _xwait(ssems, rsems, thrus, lands, scatter, after, name, ks=ALL_PEERS):
    nt = len(lands)

    def body(*refs):
        ins, lnd = refs[:nt], refs[nt:2 * nt]
        ss, rs = refs[2 * nt:3 * nt], refs[3 * nt:4 * nt]
        x, y, c = lax.axis_index("x"), lax.axis_index("y"), lax.axis_index("c")
        me = 4 * x + 2 * y + c
        for t in range(nt):
            for k, pos in _peers(x, y, c, ks):
                cp = _exchange_copy(ins, lnd, ss, rs, scatter, t, k, pos, me)
                cp.wait_send()
                cp.wait_recv()

    both = list(thrus) + list(lands)
    res = pl.pallas_call(
        body, name=name, out_shape=[pltpu.HBM(b.shape, b.dtype) for b in both],
        in_specs=[HBM] * (2 * nt) + [SEM] * (2 * nt) + [ANY], out_specs=[HBM] * (2 * nt),
        input_output_aliases={i: i for i in range(2 * nt)},
        compiler_params=pltpu.CompilerParams(has_side_effects=EFFECT),
    )(*both, *ssems, *rsems, after)
    return res[:nt], res[nt:]


def _forward_copy(lands, ssems, rsems, t, k, pos, sibling):
    px, py, pc = pos
    slot = lands[t].at[4 * px + 2 * py + pc]
    return pltpu.make_async_remote_copy(
        src_ref=slot, dst_ref=slot, send_sem=ssems[t].at[k - 1], recv_sem=rsems[t].at[k - 1],
        device_id=sibling, device_id_type=MESH_ID)


def _fstart(lands, name):
    nt = len(lands)

    def body(*refs):
        lnd = refs[:nt]
        ssems, rsems = refs[nt:2 * nt], refs[2 * nt:3 * nt]
        token = refs[-1]
        x, y, c = lax.axis_index("x"), lax.axis_index("y"), lax.axis_index("c")
        for t in range(nt):
            for k, pos in _peers(x, y, c, FORWARDED):
                _forward_copy(lnd, ssems, rsems, t, k, pos, (x, y, c ^ 1)).start()
        token[...] = jnp.zeros_like(token)

    res = pl.pallas_call(
        body, name=name,
        out_shape=[pltpu.SemaphoreType.DMA((N_PEER,))] * (2 * nt)
        + [pltpu.HBM(b.shape, b.dtype) for b in lands] + [jax.ShapeDtypeStruct((8, 128), F32)],
        in_specs=[HBM] * nt,
        out_specs=[SEM] * (2 * nt) + [HBM] * nt + [pl.BlockSpec(memory_space=pltpu.VMEM)],
        input_output_aliases={i: 2 * nt + i for i in range(nt)},
        compiler_params=pltpu.CompilerParams(has_side_effects=EFFECT),
    )(*[pltpu.with_memory_space_constraint(b, pltpu.HBM) for b in lands])
    return res[:nt], res[nt:2 * nt], res[2 * nt:3 * nt], res[-1][0, 0]


def _fwait(ssems, rsems, lands, after, name):
    nt = len(lands)

    def body(*refs):
        lnd = refs[:nt]
        ss, rs = refs[nt:2 * nt], refs[2 * nt:3 * nt]
        x, y, c = lax.axis_index("x"), lax.axis_index("y"), lax.axis_index("c")
        for t in range(nt):
            for k, pos in _peers(x, y, c, FORWARDED):
                cp = _forward_copy(lnd, ss, rs, t, k, pos, (x, y, c ^ 1))
                cp.wait_send()
                cp.wait_recv()

    return pl.pallas_call(
        body, name=name, out_shape=[pltpu.HBM(b.shape, b.dtype) for b in lands],
        in_specs=[HBM] * nt + [SEM] * (2 * nt) + [ANY], out_specs=[HBM] * nt,
        input_output_aliases={i: i for i in range(nt)},
        compiler_params=pltpu.CompilerParams(has_side_effects=EFFECT),
    )(*lands, *ssems, *rsems, after)


def _adam(recv, w, m, v, tr, name):
    _, r, c = w.shape
    n_part = recv.shape[0]
    c1 = np.float32(1.0 - ADAM_B1 ** ADAM_STEP)
    c2 = np.float32(1.0 - ADAM_B2 ** ADAM_STEP)

    def body(r_ref, w_ref, m_ref, v_ref, g_ref, d_ref, mo_ref, vo_ref):
        g = r_ref[0].astype(F32)
        for s in range(1, n_part):
            g = g + r_ref[s].astype(F32)
        mn = ADAM_B1 * m_ref[0] + (1.0 - ADAM_B1) * g
        vn = ADAM_B2 * v_ref[0] + (1.0 - ADAM_B2) * (g * g)
        g_ref[0] = g
        mo_ref[0] = mn
        vo_ref[0] = vn
        d_ref[0] = -ADAM_LR * ((mn / c1) / (jnp.sqrt(vn / c2) + ADAM_EPS) + ADAM_WD * w_ref[0])

    one = pl.BlockSpec((1, tr, c), lambda i: (0, i, 0))
    return pl.pallas_call(
        body, grid=(r // tr,), in_specs=[pl.BlockSpec((n_part, tr, c), lambda i: (0, i, 0)), one, one, one],
        out_specs=[one] * 4, out_shape=[jax.ShapeDtypeStruct((1, r, c), F32)] * 4,
        name=name, compiler_params=_cp())(recv, w, m, v)


def _local_step(x, target, sp, need, ahead, emit):
    g = {}
    x0, h1 = x, _rms_fwd(x, sp["norm_ffn1"], "ffn1_norm")
    act1, saved1 = _ffn_up(h1, lambda kind, a: need("w" + kind + "1", a),
                           lambda kind, a: ahead("w" + kind + "1", a), "ffn1")
    x1, h2 = _mm_nn_resnorm([(act1, saved1[-1])], x0, 0.5, sp["norm_mix"], 512, "ffn1_down_norm",
                            after=ahead("win_a", act1))
    win_a, win_d = need("win_a", h2), need("win_d", h2)
    conv_w8, wout = need("conv_w8", h2), need("wout", h2)
    z_at = _mm_nn(h2, win_a, F32, 1024, 768, "mix_in_attn")
    z_dn = _mm_nn(h2, win_d, F32, 1024, 768, "mix_in_dn")

    conv, qkvn, bg = _dn_pre_fwd(z_dn, conv_w8, sp["alog_row"], sp["dtb_row"], "dn_pre")
    *prep, tinv = _dn_prep_fwd(qkvn, bg, "dn_prep")
    o_dn, states = _dn_scan_fwd(prep, "dn_scan")
    dn_b = _dn_post_fwd(o_dn, z_dn, sp["dn_norm"], "dn_post")
    attn_f, attn_b, lse = _attn_fwd(z_at, "attn_fwd", after=ahead("wg2", dn_b))

    x2, h3 = _mm_nn_resnorm([(attn_b, wout[:AW]), (dn_b, wout[AW:])], x1, 1.0, sp["norm_ffn2"], 512,
                            "mix_out_norm")
    act2, saved2 = _ffn_up(h3, lambda kind, a: need("w" + kind + "2", a),
                           lambda kind, a: ahead("w" + kind + "2", a), "ffn2")

    dx3, dys3, g["norm_final"], loss8 = _down_loss_bwd(x2, act2, saved2[-1], sp["norm_final"], target,
                                                       "ffn2_down_loss")
    dx2, dx2b, g["norm_ffn2"] = _ffn_bwd(
        x2, sp["norm_ffn2"], saved2, dx3, dys3, 1.0,
        lambda kind, dw: emit(kind + "2", {"w" + kind + "2": dw}), "ffn2b")

    zero = emit("wout", {"wout": jnp.concatenate([_mm_tn(attn_b, dx2b, BF, 512, 1024, "mix_out_dw_a"),
                                                  _mm_tn(dn_b, dx2b, BF, 512, 1024, "mix_out_dw_d")], axis=0)})
    dmix = _mm_nt(dx2b, wout, F32, 1024, 1024, "mix_out_dx", after=zero)

    dz_at = jnp.concatenate(_attn_bwd(z_at, dmix, attn_f, lse, "attn_bwd"), axis=1)

    do_dn, dgate, g["dn_norm"] = _dn_post_bwd(o_dn, z_dn, sp["dn_norm"], dmix, "dn_post_b")
    cts = _dn_scan_bwd(prep, states, do_dn, "dn_scan_b")
    dqn, dkn, dvn, dbg = _dn_prep_bwd(qkvn, bg, tinv, cts, "dn_prep_b")
    dconv, dbd, g["alog_row"], g["dtb_row"] = _dn_pre_bwd(
        conv, z_dn, sp["alog_row"], sp["dtb_row"], dqn, dkn, dvn, dbg, "dn_pre_b")
    draw, dconv_w8 = _dn_conv_bwd(dconv, z_dn, conv_w8, "dn_conv_b")
    dz_dn = _dz_dn_assemble(draw, dgate, dbd, "dn_dz")

    zero = emit("win", {"win_a": _mm_tn(h2, dz_at, BF, 512, 768, "mix_in_dw_a"),
                        "win_d": _mm_tn(h2, dz_dn, BF, 512, 768, "mix_in_dw_d"), "conv_w8": dconv_w8})
    dx1, dys1, g["norm_mix"] = _mm_rms_bwd([(dz_at, win_a), (dz_dn, win_d)], NT, x1, sp["norm_mix"], dx2, 0.5,
                                           512, "mix_in_dx_dnorm", after=zero)
    dx0, _, g["norm_ffn1"] = _ffn_bwd(
        x0, sp["norm_ffn1"], saved1, dx1, dys1, 1.0,
        lambda kind, dw: emit(kind + "1", {"w" + kind + "1": dw}), "ffn1b")
    return loss8[0, 0], dx0, g


def _cols_from_shards(gathered):
    n, r, c = gathered.shape
    return jnp.transpose(gathered, (1, 0, 2)).reshape(r, n * c)


def _shards_from_cols(full, dtype):
    r, nc = full.shape
    return jnp.transpose(full.reshape(r, N_DEV, nc // N_DEV), (1, 0, 2)).astype(dtype)


def _lane_row(vec4):
    return jnp.zeros((1, 128), F32).at[:, DN_H:2 * DN_H].set(vec4.astype(F32))


WEIGHT_SOURCES = {"wg1": "gate1", "wu1": "up1", "wd1": "down1", "win_a": "w_in", "win_d": "w_in",
                  "conv_w8": "conv_w", "wout": "w_out", "wg2": "gate2", "wu2": "up2", "wd2": "down2"}
TRANSPOSED = ("gate1", "up1", "gate2", "up2")


def _build_weights(name, gath):
    if name in ("wg1", "wu1", "wd1", "wg2", "wu2", "wd2"):
        return {name: gath[WEIGHT_SOURCES[name]].reshape(F, D)}
    if name in ("win_a", "win_d"):
        w_in = _cols_from_shards(gath["w_in"])
        c0 = 3 * AW + 3 * DNW
        win_d = jnp.concatenate([w_in[:, ZA:c0], w_in[:, c0 + 2 * DN_H:], w_in[:, c0:c0 + 2 * DN_H],
                                 jnp.zeros((D, ZP - IN_COLS), w_in.dtype)], axis=1)
        return {"win_a": w_in[:, :ZA], "win_d": win_d}
    if name == "wout":
        return {name: gath["w_out"].reshape(D, D)}
    conv = _cols_from_shards(gath["conv_w"])
    return {"conv_w8": jnp.concatenate([conv, jnp.zeros((4, 3 * DNW), F32)], axis=0)}


def _small_params(norm_ffn1, norm_mix, norm_ffn2, norm_final, a_log, dt_bias, dn_norm):
    return {"norm_ffn1": norm_ffn1, "norm_mix": norm_mix, "norm_ffn2": norm_ffn2,
            "norm_final": norm_final.reshape(1, D), "alog_row": _lane_row(a_log), "dtb_row": _lane_row(dt_bias),
            "dn_norm": dn_norm}


def _grad_slabs(group, g):
    if group[0] in "gud":
        return {WEIGHT_SOURCES["w" + group]: g["w" + group].reshape(N_DEV, F // N_DEV, D)}
    if group == "wout":
        return {"w_out": g["wout"].reshape(N_DEV, D // N_DEV, D)}
    gp = jnp.concatenate([g["win_a"], g["win_d"]], axis=1)
    c0 = 3 * AW + 3 * DNW
    g_in = jnp.concatenate([gp[:, :c0], gp[:, c0 + DNW:c0 + DNW + 2 * DN_H], gp[:, c0:c0 + DNW]], axis=1)
    return {"w_in": _shards_from_cols(g_in, BF), "conv_w": _shards_from_cols(g["conv_w8"][:4], F32)}


SMALL_ROWS = 40


def _small_pack(norm_ffn1, norm_mix, norm_ffn2, norm_final, dn_norm, alog_row, dtb_row, loss=None):
    rows = [a.reshape(8, 128) for a in (norm_ffn1, norm_mix, norm_ffn2, norm_final)]
    loss_row = jnp.zeros((1, 128), F32) if loss is None else jnp.broadcast_to(loss.reshape(1, 1), (1, 128))
    rows += [dn_norm.reshape(1, 128), alog_row, dtb_row, loss_row, jnp.zeros((SMALL_ROWS - 36, 128), F32)]
    return jnp.concatenate(rows, axis=0)


def _small_unpack(pk):
    pk = pk[0]
    return (pk[0:8].reshape(1, D), pk[8:16].reshape(1, D), pk[16:24].reshape(1, D), pk[24:32].reshape(D),
            pk[32:33], pk[33:34, DN_H:2 * DN_H], pk[34:35, DN_H:2 * DN_H])


ADAM_TILE = {"gate1": 256, "up1": 256, "down1": 176, "gate2": 256, "up2": 256, "down2": 176,
             "w_in": 256, "w_out": 128, "conv_w": 4}
BIG = ("gate1", "up1", "down1", "w_in", "w_out", "gate2", "up2", "down2", "conv_w")


def kernel(x, norm_ffn1, ffn1_gate, ffn1_up, ffn1_down, norm_mix, w_in, conv_w, a_log, dt_bias, dn_norm, w_out, norm_ffn2, ffn2_gate, ffn2_up, ffn2_down, norm_final, loss_target, m_norm_ffn1, m_ffn1_gate, m_ffn1_up, m_ffn1_down, m_norm_mix, m_w_in, m_conv_w, m_a_log, m_dt_bias, m_dn_norm, m_w_out, m_norm_ffn2, m_ffn2_gate, m_ffn2_up, m_ffn2_down, m_norm_final, v_norm_ffn1, v_ffn1_gate, v_ffn1_up, v_ffn1_down, v_norm_mix, v_w_in, v_conv_w, v_a_log, v_dt_bias, v_dn_norm, v_w_out, v_norm_ffn2, v_ffn2_gate, v_ffn2_up, v_ffn2_down, v_norm_final):
    w = {"gate1": ffn1_gate, "up1": ffn1_up, "down1": ffn1_down, "w_in": w_in, "w_out": w_out,
         "gate2": ffn2_gate, "up2": ffn2_up, "down2": ffn2_down, "conv_w": conv_w}
    m = {"gate1": m_ffn1_gate, "up1": m_ffn1_up, "down1": m_ffn1_down, "w_in": m_w_in, "w_out": m_w_out,
         "gate2": m_ffn2_gate, "up2": m_ffn2_up, "down2": m_ffn2_down, "conv_w": m_conv_w}
    v = {"gate1": v_ffn1_gate, "up1": v_ffn1_up, "down1": v_ffn1_down, "w_in": v_w_in, "w_out": v_w_out,
         "gate2": v_ffn2_gate, "up2": v_ffn2_up, "down2": v_ffn2_down, "conv_w": v_conv_w}

    me = 4 * lax.axis_index("x") + 2 * lax.axis_index("y") + lax.axis_index("c")
    own_slot = lambda land, mine: lax.dynamic_update_index_in_dim(land, mine, me, 0)

    ag_order = ("gate1", "up1", "down1", "w_in", "conv_w", "w_out", "gate2", "up2", "down2")
    ag_groups = (("gate1",), ("up1",), ("down1",), ("w_in", "conv_w", "w_out"), ("gate2", "up2", "down2"))
    pos = {n: i for i, n in enumerate(ag_order)}

    def shard(n):
        if n == "conv_w":
            return w[n][0]
        return (w[n][0].T if n in TRANSPOSED else w[n][0]).astype(BF)

    ss, rs, thru, land, zero = _xstart([shard(n) for n in ag_order], False, "weights_start", FIRST_HOP)
    gath, built, on_its_way = {}, {}, {}
    group_of = lambda name: [i for i, grp in enumerate(ag_groups) if WEIGHT_SOURCES[name] in grp][0]

    def ahead(name, after):
        gi = group_of(name)
        if WEIGHT_SOURCES[name] in gath or gi in on_its_way:
            return None
        ids = [pos[n] for n in ag_groups[gi]]
        thrus, lands = _xwait([ss[i] for i in ids], [rs[i] for i in ids], [thru[i] for i in ids],
                              [land[i] for i in ids], False, after, "weights_wait%d" % gi, FIRST_HOP)
        fss, frs, lands, token = _fstart(lands, "weights_forward%d" % gi)
        on_its_way[gi] = (thrus, fss, frs, lands)
        return token

    def need(name, after):
        if name not in built:
            if WEIGHT_SOURCES[name] not in gath:
                gi = group_of(name)
                ahead(name, after)
                thrus, fss, frs, lands = on_its_way.pop(gi)
                lands = _fwait(fss, frs, lands, after, "weights_forward_wait%d" % gi)
                for n, t, l in zip(ag_groups[gi], thrus, lands):
                    gath[n] = own_slot(l, t)
            built.update(_build_weights(name, gath))
        return built[name]

    pending = []

    def emit(group, grads):
        slabs = grads if group == "small" else _grad_slabs(group, grads)
        names = list(slabs)
        started = _xstart([slabs[n] for n in names], True, "grads_start_" + group)
        pending.append((group, names) + started[:4])
        return started[4]

    sp = _small_params(norm_ffn1 + zero, norm_mix, norm_ffn2, norm_final, a_log, dt_bias, dn_norm)
    loss_part, dx, g = _local_step(x[0], loss_target[0], sp, need, ahead, emit)
    small = _small_pack(g["norm_ffn1"], g["norm_mix"], g["norm_ffn2"], g["norm_final"], g["dn_norm"],
                        g["alog_row"], g["dtb_row"], loss_part)
    emit("small", {"small": jnp.broadcast_to(small[None], (N_DEV, SMALL_ROWS, 128))})

    pack = lambda a: _small_pack(*a)[None]
    res, after = {}, dx
    for group, names, gss, grs, gthru, gland in pending:
        thrus, lands = _xwait(gss, grs, gthru, gland, True, after, "grads_wait_" + group)
        for n, t, l in zip(names, thrus, lands):
            recv = own_slot(l, lax.dynamic_index_in_dim(t, me, 0, keepdims=False))
            if n == "small":
                res[n] = _adam(
                    recv,
                    pack((norm_ffn1, norm_mix, norm_ffn2, norm_final, dn_norm, _lane_row(a_log), _lane_row(dt_bias))),
                    pack((m_norm_ffn1, m_norm_mix, m_norm_ffn2, m_norm_final, m_dn_norm, _lane_row(m_a_log),
                          _lane_row(m_dt_bias))),
                    pack((v_norm_ffn1, v_norm_mix, v_norm_ffn2, v_norm_final, v_dn_norm, _lane_row(v_a_log),
                          _lane_row(v_dt_bias))),
                    SMALL_ROWS, "adam_small")
            elif n in TRANSPOSED:
                flip = lambda a: jnp.swapaxes(a, 1, 2)
                res[n] = [flip(o) for o in _adam(recv, flip(w[n]), flip(m[n]), flip(v[n]), F // N_DEV // 2,
                                                 "adam_" + n)]
            else:
                res[n] = _adam(recv, w[n], m[n], v[n], ADAM_TILE[n], "adam_" + n)
            after = res[n][0]
    res_s = res["small"]

    loss = res_s[0][0, 35, 0]
    outs = [loss, dx[None]]
    for k in range(4):
        n1, nm, n2, nf, dn, al, dt = _small_unpack(res_s[k])
        big = {n: res[n][k] for n in BIG}
        outs += [n1, big["gate1"], big["up1"], big["down1"], nm, big["w_in"], big["conv_w"], al, dt, dn,
                 big["w_out"], n2, big["gate2"], big["up2"], big["down2"], nf]
    return tuple(outs)
```

```python
import functools

import numpy as np
import jax
import jax.numpy as jnp
from jax import lax
from jax.experimental import pallas as pl
from jax.experimental.pallas import tpu as pltpu

T = 4096
D = 1024
F = 2816
N_DEV = 8
A_HEADS = 8
A_HD = 64
AW = A_HEADS * A_HD
DN_H = 4
DN_HD = 128
DNW = DN_H * DN_HD
CH = 64
PAIR = 2 * CH
ZA = 3 * AW
ZD = 3 * DNW + DNW + 256
ZP = ZA + ZD
BD_BLK = (3 * DNW + DNW) // 128
IN_COLS = 3592
PATTERNS = ((128, 1), (512, 4), (2048, 16))
NORM_EPS = 1e-6
L2_EPS = 1e-6
ADAM_LR, ADAM_B1, ADAM_B2, ADAM_EPS, ADAM_WD, ADAM_STEP = 0.001, 0.9, 0.999, 1e-08, 0.01, 10
VMEM_LIMIT = 56 * 1024 * 1024
NEG = -1e30

BF = jnp.bfloat16
F32 = jnp.float32
NN = (((1,), (0,)), ((), ()))
NT = (((1,), (1,)), ((), ()))
TN = (((0,), (0,)), ((), ()))
HI = lax.Precision.HIGHEST
MESH_ID = pl.DeviceIdType.MESH
ANY = pl.BlockSpec(memory_space=pl.ANY)


def _cp():
    return pltpu.CompilerParams(vmem_limit_bytes=VMEM_LIMIT)


def _dg(a, b, dims):
    return lax.dot_general(a, b, dims, preferred_element_type=F32)


def _hdot(a, b):
    return lax.dot_general(a, b, NN, precision=HI, preferred_element_type=F32)


def _make_bdot(dims, da_dims, da_swap, db_dims, db_swap):
    @jax.custom_vjp
    def f(a, b):
        return _dg(a.astype(BF), b.astype(BF), dims)

    def fwd(a, b):
        return f(a, b), (a, b)

    def bwd(res, g):
        a, b = res
        gb, ab, bb = g.astype(BF), a.astype(BF), b.astype(BF)
        da = _dg(bb, gb, da_dims) if da_swap else _dg(gb, bb, da_dims)
        db = _dg(gb, ab, db_dims) if db_swap else _dg(ab, gb, db_dims)
        return da, db

    f.defvjp(fwd, bwd)
    return f


_bdot_nn = _make_bdot(NN, NT, False, TN, False)
_bdot_nt = _make_bdot(NT, NN, False, TN, True)
_bdot_tn = _make_bdot(TN, NT, True, NN, False)


def _iota(shape, dim):
    return lax.broadcasted_iota(jnp.int32, shape, dim)


def _col(x, idx):
    return jnp.sum(jnp.where(_iota(x.shape, 1) == idx, x, 0.0), axis=1, keepdims=True)


def _mm_nn(a, b, out_dtype, tm, tn, name):
    m, k = a.shape
    n = b.shape[1]

    def body(a_ref, b_ref, o_ref):
        o_ref[...] = _dg(a_ref[...], b_ref[...], NN).astype(out_dtype)

    return pl.pallas_call(
        body, grid=(m // tm, n // tn),
        in_specs=[pl.BlockSpec((tm, k), lambda i, j: (i, 0)), pl.BlockSpec((k, tn), lambda i, j: (0, j))],
        out_specs=pl.BlockSpec((tm, tn), lambda i, j: (i, j)),
        out_shape=jax.ShapeDtypeStruct((m, n), out_dtype), name=name, compiler_params=_cp())(a, b)


def _tie(body, after):
    if after is None:
        return body, [], []
    return (lambda tok_ref, *refs: body(*refs)), [ANY], [after.reshape(1, 1)]


def _mm_nt(a, b, out_dtype, tm, tb, name, after=None):
    m, c = a.shape
    kb = b.shape[0]

    def body(a_ref, b_ref, o_ref):
        o_ref[...] = _dg(a_ref[...], b_ref[...], NT).astype(out_dtype)

    body, tspec, tok = _tie(body, after)
    return pl.pallas_call(
        body, grid=(m // tm, kb // tb),
        in_specs=tspec + [pl.BlockSpec((tm, c), lambda i, j: (i, 0)), pl.BlockSpec((tb, c), lambda i, j: (j, 0))],
        out_specs=pl.BlockSpec((tm, tb), lambda i, j: (i, j)),
        out_shape=jax.ShapeDtypeStruct((m, kb), out_dtype), name=name, compiler_params=_cp())(*tok, a, b)


def _mm_tn(a, b, out_dtype, ta, tb, name, after=None):
    m, ka = a.shape
    nb = b.shape[1]

    def body(a_ref, b_ref, o_ref):
        o_ref[...] = _dg(a_ref[...], b_ref[...], TN).astype(out_dtype)

    body, tspec, tok = _tie(body, after)
    return pl.pallas_call(
        body, grid=(ka // ta, nb // tb),
        in_specs=tspec + [pl.BlockSpec((m, ta), lambda i, j: (0, i)), pl.BlockSpec((m, tb), lambda i, j: (0, j))],
        out_specs=pl.BlockSpec((ta, tb), lambda i, j: (i, j)),
        out_shape=jax.ShapeDtypeStruct((ka, nb), out_dtype), name=name, compiler_params=_cp())(*tok, a, b)


def _mm_nt_swiglu(h, wu_t, gate, tm, tb, name, after=None):
    m, c = h.shape
    kb = wu_t.shape[0]

    def body(h_ref, w_ref, g_ref, u_ref, a_ref):
        u = _dg(h_ref[...], w_ref[...], NT)
        g = g_ref[...].astype(F32)
        u_ref[...] = u.astype(BF)
        a_ref[...] = (g * jax.nn.sigmoid(g) * u).astype(BF)

    body, tspec, tok = _tie(body, after)
    tile = pl.BlockSpec((tm, tb), lambda i, j: (i, j))
    return pl.pallas_call(
        body, grid=(m // tm, kb // tb),
        in_specs=tspec + [pl.BlockSpec((tm, c), lambda i, j: (i, 0)), pl.BlockSpec((tb, c), lambda i, j: (j, 0)),
                          tile],
        out_specs=[tile, tile], out_shape=[jax.ShapeDtypeStruct((m, kb), BF)] * 2,
        name=name, compiler_params=_cp())(*tok, h, wu_t, gate)


def _mm_nt_dswiglu(dys, wd, gate, up, tm, tb, name, after=None):
    m, c = dys.shape
    kb = wd.shape[0]

    def body(d_ref, w_ref, g_ref, u_ref, dg_ref, du_ref):
        da = _dg(d_ref[...], w_ref[...], NT)
        g = g_ref[...].astype(F32)
        u = u_ref[...].astype(F32)
        s = jax.nn.sigmoid(g)
        dg_ref[...] = (da * u * (s * (1.0 + g * (1.0 - s)))).astype(BF)
        du_ref[...] = (da * (g * s)).astype(BF)

    body, tspec, tok = _tie(body, after)
    tile = pl.BlockSpec((tm, tb), lambda i, j: (i, j))
    return pl.pallas_call(
        body, grid=(m // tm, kb // tb),
        in_specs=tspec + [pl.BlockSpec((tm, c), lambda i, j: (i, 0)), pl.BlockSpec((tb, c), lambda i, j: (j, 0)),
                          tile, tile],
        out_specs=[tile, tile], out_shape=[jax.ShapeDtypeStruct((m, kb), BF)] * 2,
        name=name, compiler_params=_cp())(*tok, dys, wd, gate, up)


def _rms_fwd(x, gain, name):
    tm = 512
    row = pl.BlockSpec((tm, D), lambda i: (i, 0))

    def body(x_ref, g_ref, h_ref):
        xv = x_ref[...]
        r = lax.rsqrt(jnp.mean(xv * xv, axis=-1, keepdims=True) + NORM_EPS)
        h_ref[...] = (xv * r * g_ref[...]).astype(BF)

    return pl.pallas_call(
        body, grid=(T // tm,), in_specs=[row, pl.BlockSpec((1, D), lambda i: (0, 0))], out_specs=row,
        out_shape=jax.ShapeDtypeStruct((T, D), BF), name=name, compiler_params=_cp())(x, gain)


def _mm_rms_bwd(pairs, dims, x, gain, dres, alpha_out, tm, name, after=None):
    n = len(pairs)
    row = pl.BlockSpec((tm, D), lambda i: (i, 0))
    gspec = pl.BlockSpec((1, D), lambda i: (0, 0))

    def body(*refs):
        x_ref, g_ref, dres_ref, dx_ref, dxs_ref, dg_ref = refs[2 * n:]
        i = pl.program_id(0)
        dhv = _dg(refs[0][...], refs[n][...], dims)
        for p in range(1, n):
            dhv = dhv + _dg(refs[p][...], refs[n + p][...], dims)
        xv = x_ref[...]
        r = lax.rsqrt(jnp.mean(xv * xv, axis=-1, keepdims=True) + NORM_EPS)
        xh = xv * r
        part = jnp.sum(dhv * xh, axis=0, keepdims=True)

        @pl.when(i == 0)
        def _():
            dg_ref[...] = part

        @pl.when(i > 0)
        def _():
            dg_ref[...] += part

        dxh = dhv * g_ref[...]
        dx = r * (dxh - xh * jnp.mean(dxh * xh, axis=-1, keepdims=True)) + dres_ref[...]
        dx_ref[...] = dx
        dxs_ref[...] = (alpha_out * dx).astype(BF)

    body, tspec, tok = _tie(body, after)
    return pl.pallas_call(
        body, grid=(T // tm,),
        in_specs=tspec + [pl.BlockSpec((tm, a.shape[1]), lambda i: (i, 0)) for a, _ in pairs]
        + [pl.BlockSpec(b.shape, lambda i: (0, 0)) for _, b in pairs] + [row, gspec, row],
        out_specs=[row, row, gspec],
        out_shape=[jax.ShapeDtypeStruct((T, D), F32), jax.ShapeDtypeStruct((T, D), BF),
                   jax.ShapeDtypeStruct((1, D), F32)],
        name=name, compiler_params=_cp())(*tok, *[a for a, _ in pairs], *[b for _, b in pairs], x, gain, dres)


def _down_loss_bwd(x_prev, act, wd, gain, target, name):
    tm = 512
    row = pl.BlockSpec((tm, D), lambda i: (i, 0))
    gspec = pl.BlockSpec((1, D), lambda i: (0, 0))
    lspec = pl.BlockSpec((8, 128), lambda i: (0, 0))

    def body(x_ref, a_ref, w_ref, g_ref, t_ref, dx_ref, dxs_ref, dg_ref, loss_ref):
        i = pl.program_id(0)
        xv = x_ref[...] + 0.5 * _dg(a_ref[...], w_ref[...], NN)
        r = lax.rsqrt(jnp.mean(xv * xv, axis=-1, keepdims=True) + NORM_EPS)
        xh = xv * r
        diff = xh * g_ref[...] - t_ref[...]
        lpart = 0.5 * jnp.sum(jnp.mean(diff * diff, axis=-1, keepdims=True), axis=0, keepdims=True)
        dy = diff * (1.0 / D)
        part = jnp.sum(dy * xh, axis=0, keepdims=True)

        @pl.when(i == 0)
        def _():
            dg_ref[...] = part
            loss_ref[...] = jnp.broadcast_to(lpart, (8, 128))

        @pl.when(i > 0)
        def _():
            dg_ref[...] += part
            loss_ref[...] += jnp.broadcast_to(lpart, (8, 128))

        dxh = dy * g_ref[...]
        dx = r * (dxh - xh * jnp.mean(dxh * xh, axis=-1, keepdims=True))
        dx_ref[...] = dx
        dxs_ref[...] = (0.5 * dx).astype(BF)

    return pl.pallas_call(
        body, grid=(T // tm,),
        in_specs=[row, pl.BlockSpec((tm, F), lambda i: (i, 0)), pl.BlockSpec((F, D), lambda i: (0, 0)), gspec, row],
        out_specs=[row, row, gspec, lspec],
        out_shape=[jax.ShapeDtypeStruct((T, D), F32), jax.ShapeDtypeStruct((T, D), BF),
                   jax.ShapeDtypeStruct((1, D), F32), jax.ShapeDtypeStruct((8, 128), F32)],
        name=name, compiler_params=_cp())(x_prev, act, wd, gain, target)


def _mm_nn_resnorm(pairs, x_prev, alpha, gain, tm, name, after=None):
    m = x_prev.shape[0]
    n = len(pairs)

    def body(*refs):
        x_ref, g_ref, xo_ref, h_ref = refs[2 * n:]
        y = _dg(refs[0][...], refs[n][...], NN)
        for i in range(1, n):
            y = y + _dg(refs[i][...], refs[n + i][...], NN)
        xv = x_ref[...] + alpha * y
        xo_ref[...] = xv
        r = lax.rsqrt(jnp.mean(xv * xv, axis=-1, keepdims=True) + NORM_EPS)
        h_ref[...] = (xv * r * g_ref[...]).astype(BF)

    body, tspec, tok = _tie(body, after)
    row = pl.BlockSpec((tm, D), lambda i: (i, 0))
    return pl.pallas_call(
        body, grid=(m // tm,),
        in_specs=tspec + [pl.BlockSpec((tm, a.shape[1]), lambda i: (i, 0)) for a, _ in pairs]
        + [pl.BlockSpec(b.shape, lambda i: (0, 0)) for _, b in pairs] + [row, pl.BlockSpec((1, D), lambda i: (0, 0))],
        out_specs=[row, row],
        out_shape=[jax.ShapeDtypeStruct((m, D), F32), jax.ShapeDtypeStruct((m, D), BF)],
        name=name, compiler_params=_cp())(*tok, *[a for a, _ in pairs], *[b for _, b in pairs], x_prev, gain)


def _ffn_up(h, need, ahead, tag):
    wg = need("g", h)
    gate = _mm_nt(h, wg, BF, 1024, 1408, tag + "_gate")
    wu = need("u", gate)
    up, act = _mm_nt_swiglu(h, wu, gate, 1024, 1408, tag + "_up_act", after=ahead("d", wu))
    wd = need("d", up)
    return act, (h, gate, up, act, wg, wu, wd)


def _ffn_bwd(x_in, gain, saved, dxo, dys, alpha_out, emit, tag):
    h, gate, up, act, wg, wu, wd = saved
    sent = emit("d", _mm_tn(act, dys, BF, 256, 1024, tag + "_dwd"))
    dgate, dup = _mm_nt_dswiglu(dys, wd, gate, up, 1024, 1408, tag + "_dact", after=sent)
    sent = emit("g", _mm_tn(dgate, h, BF, 256, 1024, tag + "_dwg"))
    sent = emit("u", _mm_tn(dup, h, BF, 256, 1024, tag + "_dwu", after=sent))
    return _mm_rms_bwd([(dgate, wg), (dup, wu)], NN, x_in, gain, dxo, alpha_out, 512, tag + "_dh_dnorm",
                       after=sent)


SLAB = 2048
N_SLAB = T // SLAB
N_PAIR = A_HEADS // 2


def _pair_masks():
    lane = _iota((128, 128), 1)
    return lane < A_HD, lane >= A_HD


def _slope_table():
    h = 2 * jnp.arange(N_PAIR)[:, None] + jnp.minimum(jnp.arange(8), 1)[None, :]
    return jnp.broadcast_to((2.0 ** (-(h + 1).astype(F32)))[:, :, None], (N_PAIR, 8, 128))


def _rows(ref, start, d):
    if d == 1:
        return ref[pl.ds(start, 128), :]
    return ref[pl.ds(start, 128, stride=d), :]


def _put_rows(ref, start, d, val):
    if d == 1:
        ref[pl.ds(start, 128), :] = val
    else:
        ref[pl.ds(start, 128, stride=d), :] = val


def _units(d):
    return [(r, b, r + 128 * d * b) for r in range(d) for b in range(SLAB // (128 * d))]


def _biases(d, slopes, has_prev):
    qi = _iota((128, 256), 0)
    kj = _iota((128, 256), 1)
    steps = qi + 128 - kj
    in_band = (steps >= 0) & (steps <= 128)
    dist = (steps * d).astype(F32)
    base = [jnp.where(in_band, -(sl * dist), NEG) for sl in slopes]
    edge = [jnp.where(has_prev | (kj >= 128), b, NEG) for b in base]
    return base, edge


def _attn_fwd(z_at, name, after=None):
    def body(sl_ref, q_ref, kc_ref, kp_ref, vc_ref, vp_ref, of_ref, ob_ref, lse_ref, m_s, l_s, a_s):
        n = pl.program_id(1)
        lo, hi = _pair_masks()
        slopes = (sl_ref[0, 0:1, 0:1], sl_ref[0, 1:2, 0:1])

        def unit(d, start, b, first, carry, bias):
            q = _rows(q_ref, start, d).astype(BF)
            kcur, vcur = _rows(kc_ref, start, d).astype(BF), _rows(vc_ref, start, d).astype(BF)
            if b > 0:
                kprev, vprev = carry
            else:
                pstart = start + SLAB - 128 * d
                kprev, vprev = _rows(kp_ref, pstart, d).astype(BF), _rows(vp_ref, pstart, d).astype(BF)
            kcat = jnp.concatenate([kprev, kcur], axis=0)
            vcat = jnp.concatenate([vprev, vcur], axis=0)
            ms, ls, pvs = [], [], []
            for e in range(2):
                qm = jnp.where(lo if e == 0 else hi, q, jnp.zeros_like(q))
                s = _dg(qm, kcat, NT) * (A_HD ** -0.5) + bias[e]
                m = jnp.max(s, axis=1, keepdims=True)
                p = jnp.exp(s - m)
                ms.append(m)
                ls.append(jnp.sum(p, axis=1, keepdims=True))
                pvs.append(_dg(p.astype(BF), vcat, NN))
            m_u = jnp.where(lo, ms[0], ms[1])
            l_u = jnp.where(lo, ls[0], ls[1])
            a_u = jnp.where(lo, pvs[0], pvs[1])
            if first:
                m_n, l_n, a_n = m_u, l_u, a_u
            else:
                m_o = _rows(m_s, start, d)
                m_n = jnp.maximum(m_o, m_u)
                c_o = jnp.exp(m_o - m_n)
                c_u = jnp.exp(m_u - m_n)
                l_n = _rows(l_s, start, d) * c_o + l_u * c_u
                a_n = _rows(a_s, start, d) * c_o + a_u * c_u
            _put_rows(m_s, start, d, m_n)
            _put_rows(l_s, start, d, l_n)
            _put_rows(a_s, start, d, a_n)
            return kcur, vcur

        for pi, (_, d) in enumerate(PATTERNS):
            base, edge = _biases(d, slopes, n > 0)
            carry = None
            for r, b, start in _units(d):
                carry = unit(d, start, b, pi == 0, carry, edge if b == 0 else base)
        l = l_s[...]
        out = a_s[...] / l
        of_ref[...] = out
        ob_ref[...] = out.astype(BF)
        lse_ref[...] = m_s[...] + jnp.log(l)

    body, tspec, tok = _tie(body, after)
    cur = lambda c: pl.BlockSpec((SLAB, 128), lambda j, n: (n, c * N_PAIR + j))
    prv = lambda c: pl.BlockSpec((SLAB, 128), lambda j, n: (jnp.maximum(n - 1, 0), c * N_PAIR + j))
    out = pl.BlockSpec((SLAB, 128), lambda j, n: (n, j))
    return pl.pallas_call(
        body, grid=(N_PAIR, N_SLAB),
        in_specs=tspec + [pl.BlockSpec((1, 8, 128), lambda j, n: (j, 0, 0)), cur(0), cur(1), prv(1), cur(2), prv(2)],
        out_specs=[out, out, out],
        out_shape=[jax.ShapeDtypeStruct((T, AW), F32), jax.ShapeDtypeStruct((T, AW), BF),
                   jax.ShapeDtypeStruct((T, AW), F32)],
        scratch_shapes=[pltpu.VMEM((SLAB, 128), F32)] * 3,
        name=name, compiler_params=_cp())(*tok, _slope_table(), z_at, z_at, z_at, z_at, z_at)


def _attn_bwd(z_at, dout, out, lse, name):
    def body(sl_ref, q_ref, kc_ref, kp_ref, vc_ref, vp_ref, do_ref, o_ref, lse_ref, dq_ref, dk_ref, dv_ref,
             dq_s, dk_s, dv_s, ck_s, cv_s):
        step = pl.program_id(1)
        n = N_SLAB - 1 - step
        lo, hi = _pair_masks()
        slopes = (sl_ref[0, 0:1, 0:1], sl_ref[0, 1:2, 0:1])

        @pl.when(step == 0)
        def _():
            ck_s[...] = jnp.zeros_like(ck_s)
            cv_s[...] = jnp.zeros_like(cv_s)

        dk_s[...] = ck_s[...]
        dv_s[...] = cv_s[...]
        ck_s[...] = jnp.zeros_like(ck_s)
        cv_s[...] = jnp.zeros_like(cv_s)

        def add_rows(ref, start, d, val):
            _put_rows(ref, start, d, _rows(ref, start, d) + val)

        def unit(d, start, b, first, carry, bias):
            q = _rows(q_ref, start, d).astype(BF)
            do_f = _rows(do_ref, start, d)
            do = do_f.astype(BF)
            prod = do_f * _rows(o_ref, start, d)
            lse_u = _rows(lse_ref, start, d)
            kcur, vcur = _rows(kc_ref, start, d).astype(BF), _rows(vc_ref, start, d).astype(BF)
            if b > 0:
                kprev, vprev = carry
            else:
                pstart = start + SLAB - 128 * d
                kprev, vprev = _rows(kp_ref, pstart, d).astype(BF), _rows(vp_ref, pstart, d).astype(BF)
            kcat = jnp.concatenate([kprev, kcur], axis=0)
            vcat = jnp.concatenate([vprev, vcur], axis=0)
            masks = (lo, hi)
            qms = [jnp.where(msk, q, jnp.zeros_like(q)) for msk in masks]
            doms = [jnp.where(msk, do, jnp.zeros_like(do)) for msk in masks]
            deltas = [jnp.sum(jnp.where(msk, prod, 0.0), axis=1, keepdims=True) for msk in masks]
            ss = [_dg(qm, kcat, NT) * (A_HD ** -0.5) + bs for qm, bs in zip(qms, bias)]
            dps = [_dg(dom, vcat, NT) for dom in doms]
            ps = [jnp.exp(s - lse_u[:, 64 * e:64 * e + 1]) for e, s in enumerate(ss)]
            dss = [(p * (dp - delta)).astype(BF) for p, dp, delta in zip(ps, dps, deltas)]
            pbs = [p.astype(BF) for p in ps]
            dqs = [_dg(ds, kcat, NN) for ds in dss]
            dkc = (_dg(dss[0], qms[0], TN) + _dg(dss[1], qms[1], TN)) * (A_HD ** -0.5)
            dvc = _dg(pbs[0], doms[0], TN) + _dg(pbs[1], doms[1], TN)
            dq_u = jnp.where(lo, dqs[0], dqs[1]) * (A_HD ** -0.5)
            if first:
                _put_rows(dq_s, start, d, dq_u)
            else:
                add_rows(dq_s, start, d, dq_u)
            add_rows(dk_s, start, d, dkc[128:])
            add_rows(dv_s, start, d, dvc[128:])
            if b > 0:
                add_rows(dk_s, start - 128 * d, d, dkc[:128])
                add_rows(dv_s, start - 128 * d, d, dvc[:128])
            else:
                pstart = start + SLAB - 128 * d
                add_rows(ck_s, pstart, d, dkc[:128])
                add_rows(cv_s, pstart, d, dvc[:128])
            return kcur, vcur

        for pi, (_, d) in enumerate(PATTERNS):
            base, edge = _biases(d, slopes, n > 0)
            carry = None
            for r, b, start in _units(d):
                carry = unit(d, start, b, pi == 0, carry, edge if b == 0 else base)
        dq_ref[...] = dq_s[...].astype(BF)
        dk_ref[...] = dk_s[...].astype(BF)
        dv_ref[...] = dv_s[...].astype(BF)

    rev = lambda n: N_SLAB - 1 - n
    cur = lambda c: pl.BlockSpec((SLAB, 128), lambda j, n: (rev(n), c * N_PAIR + j))
    prv = lambda c: pl.BlockSpec((SLAB, 128), lambda j, n: (jnp.maximum(rev(n) - 1, 0), c * N_PAIR + j))
    one = pl.BlockSpec((SLAB, 128), lambda j, n: (rev(n), j))
    return pl.pallas_call(
        body, grid=(N_PAIR, N_SLAB),
        in_specs=[pl.BlockSpec((1, 8, 128), lambda j, n: (j, 0, 0)), cur(0), cur(1), prv(1), cur(2), prv(2),
                  one, one, one],
        out_specs=[one, one, one], out_shape=[jax.ShapeDtypeStruct((T, AW), BF)] * 3,
        scratch_shapes=[pltpu.VMEM((SLAB, 128), F32)] * 5,
        name=name, compiler_params=_cp())(_slope_table(), z_at, z_at, z_at, z_at, z_at, dout, out, lse)


def _silu(x):
    return x * jax.nn.sigmoid(x)


def _qk_math(c):
    s = _silu(c)
    return s * lax.rsqrt(jnp.sum(s * s, axis=-1, keepdims=True) + L2_EPS)


def _softplus(x):
    return jnp.maximum(x, 0.0) + jnp.log(1.0 + jnp.exp(-jnp.abs(x)))


def _gate_math(bd, alog_row, dtb_row):
    rows = bd.shape[0]
    lane = _iota(bd.shape, 1)
    beta = jax.nn.sigmoid(bd)
    g = jnp.where((lane >= DN_H) & (lane < 2 * DN_H), -jnp.exp(alog_row) * _softplus(bd + dtb_row), 0.0)
    ri = _iota((rows, rows), 0)
    ci = _iota((rows, rows), 1)
    same = (ri // CH) == (ci // CH)
    li = _iota((128, 128), 0)
    lj = _iota((128, 128), 1)
    to_next_group = jnp.where((lj == li + DN_H) & (li >= DN_H) & (li < 2 * DN_H), 1.0, 0.0)
    gc = _hdot(jnp.where(same & (ci <= ri), 1.0, 0.0), g)
    glast = _hdot(_hdot(jnp.where(same, 1.0, 0.0), g), to_next_group)
    return jnp.where(lane < DN_H, beta, 0.0) + gc + glast


def _shift_down(cur, halo, s):
    if s == 0:
        return cur
    rolled = pltpu.roll(cur, s, 0)
    hr = pltpu.roll(halo, s, 0)
    head = jnp.where(_iota(hr.shape, 0) < s, hr, rolled[:8])
    return jnp.concatenate([head, rolled[8:]], axis=0)


def _shift_up(cur, halo, s):
    if s == 0:
        return cur
    rows = cur.shape[0]
    rolled = pltpu.roll(cur, rows - s, 0)
    hr = pltpu.roll(halo, 8 - s, 0)
    tail = jnp.where(_iota(hr.shape, 0) >= 8 - s, hr, rolled[rows - 8:])
    return jnp.concatenate([rolled[:rows - 8], tail], axis=0)


def _dn_pre_fwd(z_dn, conv_w8, alog_row, dtb_row, name):
    tm = 256
    wq = 3 * DNW

    def body(raw_ref, halo_ref, bd_ref, w_ref, al_ref, dt_ref, conv_ref, qkv_ref, bg_ref):
        i = pl.program_id(0)
        cur = raw_ref[...]
        halo = jnp.where(i > 0, halo_ref[...], 0.0)
        w = w_ref[...]
        conv = jnp.zeros((tm, wq), F32)
        for j in range(4):
            conv = conv + _shift_down(cur, halo, 3 - j) * w[j:j + 1, :]
        conv_ref[...] = conv
        for blk in range(3 * DN_H):
            sl = slice(128 * blk, 128 * blk + 128)
            c = conv[:, sl]
            qkv_ref[:, sl] = _qk_math(c) if blk < 2 * DN_H else _silu(c)
        bg_ref[...] = _gate_math(bd_ref[...], al_ref[...], dt_ref[...])

    one = pl.BlockSpec((1, 128), lambda i: (0, 0))
    return pl.pallas_call(
        body, grid=(T // tm,),
        in_specs=[pl.BlockSpec((tm, wq), lambda i: (i, 0)),
                  pl.BlockSpec((8, wq), lambda i: (jnp.maximum(i * (tm // 8) - 1, 0), 0)),
                  pl.BlockSpec((tm, 128), lambda i: (i, BD_BLK)),
                  pl.BlockSpec((8, wq), lambda i: (0, 0)), one, one],
        out_specs=[pl.BlockSpec((tm, wq), lambda i: (i, 0)), pl.BlockSpec((tm, wq), lambda i: (i, 0)),
                   pl.BlockSpec((tm, 128), lambda i: (i, 0))],
        out_shape=[jax.ShapeDtypeStruct((T, wq), F32), jax.ShapeDtypeStruct((T, wq), F32),
                   jax.ShapeDtypeStruct((T, 128), F32)],
        name=name, compiler_params=_cp())(z_dn, z_dn, z_dn, conv_w8, alog_row, dtb_row)


def _dn_pre_bwd(conv, z_dn, alog_row, dtb_row, dqn, dkn, dvn, dbg, name):
    tm = 256
    wq = 3 * DNW

    def body(conv_ref, bd_ref, al_ref, dt_ref, dq_ref, dk_ref, dv_ref, dbg_ref,
             dconv_ref, dbd_ref, dal_ref, ddt_ref):
        i = pl.program_id(0)
        for blk in range(3 * DN_H):
            sl = slice(128 * blk, 128 * blk + 128)
            src = (dq_ref, dk_ref, dv_ref)[blk // DN_H]
            ct = src[:, 128 * (blk % DN_H):128 * (blk % DN_H) + 128]
            fn = _qk_math if blk < 2 * DN_H else _silu
            _, vjp = jax.vjp(fn, conv_ref[:, sl])
            dconv_ref[:, sl] = vjp(ct)[0]
        _, vjp = jax.vjp(_gate_math, bd_ref[...], al_ref[...], dt_ref[...])
        dbd, dal, ddt = vjp(dbg_ref[...])
        dbd_ref[...] = dbd

        @pl.when(i == 0)
        def _():
            dal_ref[...] = dal
            ddt_ref[...] = ddt

        @pl.when(i > 0)
        def _():
            dal_ref[...] += dal
            ddt_ref[...] += ddt

    one = pl.BlockSpec((1, 128), lambda i: (0, 0))
    row = pl.BlockSpec((tm, wq), lambda i: (i, 0))
    hd = pl.BlockSpec((tm, DNW), lambda i: (i, 0))
    st = pl.BlockSpec((tm, 128), lambda i: (i, 0))
    return pl.pallas_call(
        body, grid=(T // tm,),
        in_specs=[row, pl.BlockSpec((tm, 128), lambda i: (i, BD_BLK)), one, one, hd, hd, hd, st],
        out_specs=[row, st, one, one],
        out_shape=[jax.ShapeDtypeStruct((T, wq), F32), jax.ShapeDtypeStruct((T, 128), F32),
                   jax.ShapeDtypeStruct((1, 128), F32), jax.ShapeDtypeStruct((1, 128), F32)],
        name=name, compiler_params=_cp())(conv, z_dn, alog_row, dtb_row, dqn, dkn, dvn, dbg)


def _dn_conv_bwd(dconv, z_dn, conv_w8, name):
    tm = 256
    wq = 3 * DNW
    last = T // tm - 1

    def body(dc_ref, dcn_ref, raw_ref, halo_ref, w_ref, draw_ref, dw_ref):
        i = pl.program_id(0)
        dc = dc_ref[...]
        nxt = jnp.where(i < last, dcn_ref[...], 0.0)
        cur = raw_ref[...]
        halo = jnp.where(i > 0, halo_ref[...], 0.0)
        w = w_ref[...]
        draw = jnp.zeros((tm, wq), F32)
        rows = []
        for j in range(4):
            draw = draw + _shift_up(dc, nxt, 3 - j) * w[j:j + 1, :]
            rows.append(jnp.sum(dc * _shift_down(cur, halo, 3 - j), axis=0, keepdims=True))
        draw_ref[...] = draw
        part = jnp.concatenate(rows + [jnp.zeros((4, wq), F32)], axis=0)

        @pl.when(i == 0)
        def _():
            dw_ref[...] = part

        @pl.when(i > 0)
        def _():
            dw_ref[...] += part

    row = pl.BlockSpec((tm, wq), lambda i: (i, 0))
    return pl.pallas_call(
        body, grid=(T // tm,),
        in_specs=[row, pl.BlockSpec((8, wq), lambda i: (jnp.minimum((i + 1) * (tm // 8), T // 8 - 1), 0)),
                  row, pl.BlockSpec((8, wq), lambda i: (jnp.maximum(i * (tm // 8) - 1, 0), 0)),
                  pl.BlockSpec((8, wq), lambda i: (0, 0))],
        out_specs=[row, pl.BlockSpec((8, wq), lambda i: (0, 0))],
        out_shape=[jax.ShapeDtypeStruct((T, wq), F32), jax.ShapeDtypeStruct((8, wq), F32)],
        name=name, compiler_params=_cp())(dconv, dconv, z_dn, z_dn, conv_w8)


def _h3(a, b, dims=NN):
    return lax.dot_general(a, b, dims, precision=lax.Precision.HIGH, preferred_element_type=F32)


@jax.custom_vjp
def _inverse_given(a_mat, tinv):
    return tinv


def _inverse_given_fwd(a_mat, tinv):
    return tinv, tinv


def _inverse_given_bwd(tinv, g):
    return -_bdot_tn(tinv, _bdot_nt(g, tinv)), jnp.zeros_like(tinv)


_inverse_given.defvjp(_inverse_given_fwd, _inverse_given_bwd)


@jax.custom_vjp
def _h3_lo(a, b):
    return _h3(a, b)


def _h3_lo_fwd(a, b):
    return _h3(a, b), (a, b)


def _h3_lo_bwd(res, g):
    a, b = res
    gb = g.astype(BF)
    return _dg(gb, b.astype(BF), NT), _dg(a.astype(BF), gb, TN)


_h3_lo.defvjp(_h3_lo_fwd, _h3_lo_bwd)


def _prep_head(q, k, v, bgc, h):
    beta = _col(bgc, h)
    gc = jnp.broadcast_to(_col(bgc, DN_H + h), (PAIR, 128))
    glast = jnp.broadcast_to(_col(bgc, 2 * DN_H + h), (PAIR, 128))
    ri = _iota((PAIR, PAIR), 0)
    ci = _iota((PAIR, PAIR), 1)
    same = (ri // CH) == (ci // CH)
    causal = same & (ci <= ri)
    strict = same & (ci < ri)
    eye = ri == ci
    gc_cols = _hdot(jnp.ones((PAIR, PAIR), F32), jnp.where(eye, gc, 0.0))
    decay = jnp.exp(jnp.where(causal, gc - gc_cols, NEG))
    egc = jnp.exp(gc)
    kb = k * beta
    a_mat = jnp.where(strict, _bdot_nt(kb, k) * decay, 0.0)
    qs = q * (DN_HD ** -0.5)
    attn = jnp.where(causal, _bdot_nt(qs, k) * decay, 0.0)
    return a_mat, (v * beta, kb * egc, qs * egc, k * jnp.exp(glast - gc), attn, jnp.exp(glast))


def _prep_tail(tinv, ctx):
    vb, kbe, qg, kdec, attn, decb = ctx
    return _h3_lo(tinv, vb), _h3_lo(tinv, kbe), qg, kdec, attn, decb


def _inverses(a_mats):
    eye = jnp.where(_iota((PAIR, PAIR), 0) == _iota((PAIR, PAIR), 1), 1.0, 0.0)
    ps = [-a for a in a_mats]
    tinvs = [eye + p for p in ps]
    for _ in range(5):
        ps = [_h3(p, p) for p in ps]
        tinvs = [t + _h3(t, p) for t, p in zip(tinvs, ps)]
    return tinvs


def _dn_prep_fwd(qkv, bg, name):
    rows = 512
    hd = lambda off: pl.BlockSpec((rows, 128), lambda g, h: (g, off + h))
    out = pl.BlockSpec((rows, 128), lambda g, h: (g, h))

    def body(q_ref, k_ref, v_ref, bg_ref, *outs):
        h = pl.program_id(1)
        spans = [slice(PAIR * pr, PAIR * pr + PAIR) for pr in range(rows // PAIR)]
        heads = [_prep_head(q_ref[rs, :], k_ref[rs, :], v_ref[rs, :], bg_ref[rs, :], h) for rs in spans]
        tinvs = _inverses([a for a, _ in heads])
        for rs, tinv, (_, ctx) in zip(spans, tinvs, heads):
            for o_ref, val in zip(outs, _prep_tail(tinv, ctx) + (tinv,)):
                o_ref[rs, :] = val

    return pl.pallas_call(
        body, grid=(T // rows, DN_H),
        in_specs=[hd(0), hd(DN_H), hd(2 * DN_H), pl.BlockSpec((rows, 128), lambda g, h: (g, 0))],
        out_specs=[out] * 7, out_shape=[jax.ShapeDtypeStruct((T, DNW), F32)] * 7,
        name=name, compiler_params=_cp())(qkv, qkv, qkv, bg)


def _dn_prep_bwd(qkv, bg, tinv, cts, name):
    rows = 512
    hd = lambda off: pl.BlockSpec((rows, 128), lambda g, h: (g, off + h))
    out = pl.BlockSpec((rows, 128), lambda g, h: (g, h))
    st = pl.BlockSpec((rows, 128), lambda g, h: (g, 0))

    def body(q_ref, k_ref, v_ref, bg_ref, ti_ref, c0, c1, c2, c3, c4, c5, dq_ref, dk_ref, dv_ref, dbg_ref):
        h = pl.program_id(1)
        spans = [slice(PAIR * pr, PAIR * pr + PAIR) for pr in range(rows // PAIR)]
        tis = [ti_ref[rs, :] for rs in spans]

        def joint(qs, ks, vs, bs):
            heads = [_prep_head(q, k, v, b, h) for q, k, v, b in zip(qs, ks, vs, bs)]
            return [_prep_tail(_inverse_given(a, ti), ctx) for (a, ctx), ti in zip(heads, tis)]

        _, vjp = jax.vjp(joint, *[[r[rs, :] for rs in spans] for r in (q_ref, k_ref, v_ref, bg_ref)])
        dqs, dks, dvs, dbs = vjp([tuple(c[rs, :] for c in (c0, c1, c2, c3, c4, c5)) for rs in spans])
        for rs, dq, dk, dv in zip(spans, dqs, dks, dvs):
            dq_ref[rs, :] = dq
            dk_ref[rs, :] = dk
            dv_ref[rs, :] = dv
        dbg_all = jnp.concatenate(dbs, axis=0)

        @pl.when(h == 0)
        def _():
            dbg_ref[...] = dbg_all

        @pl.when(h > 0)
        def _():
            dbg_ref[...] += dbg_all

    return pl.pallas_call(
        body, grid=(T // rows, DN_H),
        in_specs=[hd(0), hd(DN_H), hd(2 * DN_H), st] + [out] * 7,
        out_specs=[out, out, out, st],
        out_shape=[jax.ShapeDtypeStruct((T, DNW), F32)] * 3 + [jax.ShapeDtypeStruct((T, 128), F32)],
        name=name, compiler_params=_cp())(qkv, qkv, qkv, bg, tinv, *cts)


def _step_math(ss, us, ws, qgs, kdecs, attns, decbs, sub):
    z = jnp.zeros((CH, 128), F32)
    vnews = [u - _bdot_nn(w, s) for u, w, s in zip(us, ws, ss)]
    vfulls = [jnp.concatenate([v, z] if sub == 0 else [z, v], axis=0) for v in vnews]
    os = [_bdot_nn(qg, s) + _bdot_nn(attn, vf) for qg, s, attn, vf in zip(qgs, ss, attns, vfulls)]
    decs = [jnp.sum(decb, axis=0, keepdims=True) * (1.0 / CH) for decb in decbs]
    return [s * dec + _bdot_tn(kdec, v) for s, dec, kdec, v in zip(ss, decs, kdecs, vnews)], os


def _dn_scan_fwd(prep, name):
    npair = T // PAIR
    row = pl.BlockSpec((PAIR, DNW), lambda p: (p, 0))

    def body(u_ref, w_ref, qg_ref, kd_ref, at_ref, db_ref, o_ref, ss_ref, s_ref):
        @pl.when(pl.program_id(0) == 0)
        def _():
            s_ref[...] = jnp.zeros_like(s_ref)

        lanes = [slice(128 * h, 128 * h + 128) for h in range(DN_H)]
        states = [s_ref[h] for h in range(DN_H)]
        for sub in range(2):
            rs = slice(CH * sub, CH * sub + CH)
            for h in range(DN_H):
                ss_ref[sub, h] = states[h]
            states, os = _step_math(states, *[[r[rs, ls] for ls in lanes]
                                              for r in (u_ref, w_ref, qg_ref, kd_ref, at_ref, db_ref)], sub)
            for ls, o in zip(lanes, os):
                o_ref[rs, ls] = o
        for h in range(DN_H):
            s_ref[h] = states[h]

    return pl.pallas_call(
        body, grid=(npair,), in_specs=[row] * 6,
        out_specs=[row, pl.BlockSpec((2, DN_H, 128, 128), lambda p: (p, 0, 0, 0))],
        out_shape=[jax.ShapeDtypeStruct((T, DNW), F32), jax.ShapeDtypeStruct((T // CH, DN_H, 128, 128), F32)],
        scratch_shapes=[pltpu.VMEM((DN_H, 128, 128), F32)],
        name=name, compiler_params=_cp())(*prep)


def _dn_scan_bwd(prep, states, do, name):
    npair = T // PAIR
    row = pl.BlockSpec((PAIR, DNW), lambda p: (npair - 1 - p, 0))

    def body(u_ref, w_ref, qg_ref, kd_ref, at_ref, db_ref, ss_ref, do_ref, *rest):
        outs, ds_ref = rest[:6], rest[6]

        @pl.when(pl.program_id(0) == 0)
        def _():
            ds_ref[...] = jnp.zeros_like(ds_ref)

        lanes = [slice(128 * h, 128 * h + 128) for h in range(DN_H)]
        dss = [ds_ref[h] for h in range(DN_H)]
        for sub in (1, 0):
            rs = slice(CH * sub, CH * sub + CH)
            args = [[ss_ref[sub, h] for h in range(DN_H)]] + [
                [r[rs, ls] for ls in lanes] for r in (u_ref, w_ref, qg_ref, kd_ref, at_ref, db_ref)]
            _, vjp = jax.vjp(functools.partial(_step_math, sub=sub), *args)
            cts = vjp((dss, [do_ref[rs, ls] for ls in lanes]))
            dss = cts[0]
            for o_ref, vals in zip(outs, cts[1:]):
                for ls, val in zip(lanes, vals):
                    o_ref[rs, ls] = val
        for h in range(DN_H):
            ds_ref[h] = dss[h]

    return pl.pallas_call(
        body, grid=(npair,),
        in_specs=[row] * 6 + [pl.BlockSpec((2, DN_H, 128, 128), lambda p: (npair - 1 - p, 0, 0, 0)), row],
        out_specs=[row] * 6, out_shape=[jax.ShapeDtypeStruct((T, DNW), F32)] * 6,
        scratch_shapes=[pltpu.VMEM((DN_H, 128, 128), F32)],
        name=name, compiler_params=_cp())(*prep, states, do)


def _post_math(o, gate, wrow):
    return o * lax.rsqrt(jnp.mean(o * o, axis=-1, keepdims=True) + NORM_EPS) * wrow * _silu(gate)


def _dn_post_fwd(o, z_dn, dn_norm, name):
    tm = 512
    row = pl.BlockSpec((tm, DNW), lambda i: (i, 0))

    def body(o_ref, g_ref, w_ref, y_ref):
        for h in range(DN_H):
            ls = slice(128 * h, 128 * h + 128)
            y_ref[:, ls] = _post_math(o_ref[:, ls], g_ref[:, ls], w_ref[...]).astype(BF)

    return pl.pallas_call(
        body, grid=(T // tm,),
        in_specs=[row, pl.BlockSpec((tm, DNW), lambda i: (i, 3)), pl.BlockSpec((1, 128), lambda i: (0, 0))],
        out_specs=row, out_shape=jax.ShapeDtypeStruct((T, DNW), BF),
        name=name, compiler_params=_cp())(o, z_dn, dn_norm)


def _dn_post_bwd(o, z_dn, dn_norm, dy, name):
    tm = 512
    row = pl.BlockSpec((tm, DNW), lambda i: (i, 0))
    one = pl.BlockSpec((1, 128), lambda i: (0, 0))

    def body(o_ref, g_ref, w_ref, dy_ref, do_ref, dg_ref, dw_ref):
        i = pl.program_id(0)
        dw = jnp.zeros((1, 128), F32)
        for h in range(DN_H):
            ls = slice(128 * h, 128 * h + 128)
            _, vjp = jax.vjp(_post_math, o_ref[:, ls], g_ref[:, ls], w_ref[...])
            do, dg, dwh = vjp(dy_ref[:, ls].astype(F32))
            do_ref[:, ls] = do
            dg_ref[:, ls] = dg
            dw = dw + dwh

        @pl.when(i == 0)
        def _():
            dw_ref[...] = dw

        @pl.when(i > 0)
        def _():
            dw_ref[...] += dw

    return pl.pallas_call(
        body, grid=(T // tm,),
        in_specs=[row, pl.BlockSpec((tm, DNW), lambda i: (i, 3)), one, pl.BlockSpec((tm, DNW), lambda i: (i, 1))],
        out_specs=[row, row, one],
        out_shape=[jax.ShapeDtypeStruct((T, DNW), F32), jax.ShapeDtypeStruct((T, DNW), F32),
                   jax.ShapeDtypeStruct((1, 128), F32)],
        name=name, compiler_params=_cp())(o, z_dn, dn_norm, dy)


def _dz_dn_assemble(draw, dgate, dbd, name):
    tm = 512

    def body(a_ref, b_ref, c_ref, o_ref):
        o_ref[:, :3 * DNW] = a_ref[...].astype(BF)
        o_ref[:, 3 * DNW:4 * DNW] = b_ref[...].astype(BF)
        o_ref[:, 4 * DNW:4 * DNW + 128] = c_ref[...].astype(BF)
        o_ref[:, 4 * DNW + 128:] = jnp.zeros((tm, 128), BF)

    return pl.pallas_call(
        body, grid=(T // tm,),
        in_specs=[pl.BlockSpec((tm, 3 * DNW), lambda i: (i, 0)), pl.BlockSpec((tm, DNW), lambda i: (i, 0)),
                  pl.BlockSpec((tm, 128), lambda i: (i, 0))],
        out_specs=pl.BlockSpec((tm, ZD), lambda i: (i, 0)),
        out_shape=jax.ShapeDtypeStruct((T, ZD), BF), name=name, compiler_params=_cp())(draw, dgate, dbd)


HBM = pl.BlockSpec(memory_space=pltpu.HBM)
SEM = pl.BlockSpec(memory_space=pltpu.SEMAPHORE)
EFFECT = pltpu.SideEffectType.DATAFLOW_SIDE_EFFECTING
N_PEER = N_DEV - 1


ALL_PEERS = (1, 2, 4, 3, 5, 6, 7)
FIRST_HOP = (1, 2, 4, 6)
FORWARDED = (2, 4, 6)


def _peers(x, y, c, ks=ALL_PEERS):
    return [(k, (x ^ (k >> 2), y ^ ((k >> 1) & 1), c ^ (k & 1))) for k in ks]


def _exchange_copy(ins, lands, ssems, rsems, scatter, t, k, pos, me):
    px, py, pc = pos
    src = ins[t].at[4 * px + 2 * py + pc] if scatter else ins[t]
    return pltpu.make_async_remote_copy(
        src_ref=src, dst_ref=lands[t].at[me], send_sem=ssems[t].at[k - 1], recv_sem=rsems[t].at[k - 1],
        device_id=pos, device_id_type=MESH_ID)


def _xstart(bufs, scatter, name, ks=ALL_PEERS):
    nt = len(bufs)
    lands = [lax.empty((N_DEV,) + tuple(b.shape[1:] if scatter else b.shape), b.dtype) for b in bufs]

    def body(*refs):
        ins, lnd = refs[:nt], refs[nt:2 * nt]
        ssems, rsems = refs[2 * nt:3 * nt], refs[3 * nt:4 * nt]
        token = refs[-1]
        x, y, c = lax.axis_index("x"), lax.axis_index("y"), lax.axis_index("c")
        me = 4 * x + 2 * y + c
        for t in range(nt):
            for k, pos in _peers(x, y, c, ks):
                _exchange_copy(ins, lnd, ssems, rsems, scatter, t, k, pos, me).start()
        token[...] = jnp.zeros_like(token)

    both = list(bufs) + lands
    res = pl.pallas_call(
        body, name=name,
        out_shape=[pltpu.SemaphoreType.DMA((N_PEER,))] * (2 * nt)
        + [pltpu.HBM(b.shape, b.dtype) for b in both] + [jax.ShapeDtypeStruct((8, 128), F32)],
        in_specs=[HBM] * (2 * nt),
        out_specs=[SEM] * (2 * nt) + [HBM] * (2 * nt) + [pl.BlockSpec(memory_space=pltpu.VMEM)],
        input_output_aliases={i: 2 * nt + i for i in range(2 * nt)},
        compiler_params=pltpu.CompilerParams(has_side_effects=EFFECT),
    )(*[pltpu.with_memory_space_constraint(b, pltpu.HBM) for b in both])
    return res[:nt], res[nt:2 * nt], res[2 * nt:3 * nt], res[3 * nt:4 * nt], res[-1][0, 0]


def _xwait(ssems, rsems, thrus, lands, scatter, after, name, ks=ALL_PEERS):
    nt = len(lands)

    def body(*refs):
        ins, lnd = refs[:nt], refs[nt:2 * nt]
        ss, rs = refs[2 * nt:3 * nt], refs[3 * nt:4 * nt]
        x, y, c = lax.axis_index("x"), lax.axis_index("y"), lax.axis_index("c")
        me = 4 * x + 2 * y + c
        for t in range(nt):
            for k, pos in _peers(x, y, c, ks):
                cp = _exchange_copy(ins, lnd, ss, rs, scatter, t, k, pos, me)
                cp.wait_send()
                cp.wait_recv()

    both = list(thrus) + list(lands)
    res = pl.pallas_call(
        body, name=name, out_shape=[pltpu.HBM(b.shape, b.dtype) for b in both],
        in_specs=[HBM] * (2 * nt) + [SEM] * (2 * nt) + [ANY], out_specs=[HBM] * (2 * nt),
        input_output_aliases={i: i for i in range(2 * nt)},
        compiler_params=pltpu.CompilerParams(has_side_effects=EFFECT),
    )(*both, *ssems, *rsems, after)
    return res[:nt], res[nt:]


def _forward_copy(lands, ssems, rsems, t, k, pos, sibling):
    px, py, pc = pos
    slot = lands[t].at[4 * px + 2 * py + pc]
    return pltpu.make_async_remote_copy(
        src_ref=slot, dst_ref=slot, send_sem=ssems[t].at[k - 1], recv_sem=rsems[t].at[k - 1],
        device_id=sibling, device_id_type=MESH_ID)


def _fstart(lands, name):
    nt = len(lands)

    def body(*refs):
        lnd = refs[:nt]
        ssems, rsems = refs[nt:2 * nt], refs[2 * nt:3 * nt]
        token = refs[-1]
        x, y, c = lax.axis_index("x"), lax.axis_index("y"), lax.axis_index("c")
        for t in range(nt):
            for k, pos in _peers(x, y, c, FORWARDED):
                _forward_copy(lnd, ssems, rsems, t, k, pos, (x, y, c ^ 1)).start()
        token[...] = jnp.zeros_like(token)

    res = pl.pallas_call(
        body, name=name,
        out_shape=[pltpu.SemaphoreType.DMA((N_PEER,))] * (2 * nt)
        + [pltpu.HBM(b.shape, b.dtype) for b in lands] + [jax.ShapeDtypeStruct((8, 128), F32)],
        in_specs=[HBM] * nt,
        out_specs=[SEM] * (2 * nt) + [HBM] * nt + [pl.BlockSpec(memory_space=pltpu.VMEM)],
        input_output_aliases={i: 2 * nt + i for i in range(nt)},
        compiler_params=pltpu.CompilerParams(has_side_effects=EFFECT),
    )(*[pltpu.with_memory_space_constraint(b, pltpu.HBM) for b in lands])
    return res[:nt], res[nt:2 * nt], res[2 * nt:3 * nt], res[-1][0, 0]


def _fwait(ssems, rsems, lands, after, name):
    nt = len(lands)

    def body(*refs):
        lnd = refs[:nt]
        ss, rs = refs[nt:2 * nt], refs[2 * nt:3 * nt]
        x, y, c = lax.axis_index("x"), lax.axis_index("y"), lax.axis_index("c")
        for t in range(nt):
            for k, pos in _peers(x, y, c, FORWARDED):
                cp = _forward_copy(lnd, ss, rs, t, k, pos, (x, y, c ^ 1))
                cp.wait_send()
                cp.wait_recv()

    return pl.pallas_call(
        body, name=name, out_shape=[pltpu.HBM(b.shape, b.dtype) for b in lands],
        in_specs=[HBM] * nt + [SEM] * (2 * nt) + [ANY], out_specs=[HBM] * nt,
        input_output_aliases={i: i for i in range(nt)},
        compiler_params=pltpu.CompilerParams(has_side_effects=EFFECT),
    )(*lands, *ssems, *rsems, after)


def _adam(recv, w, m, v, tr, name):
    _, r, c = w.shape
    n_part = recv.shape[0]
    c1 = np.float32(1.0 - ADAM_B1 ** ADAM_STEP)
    c2 = np.float32(1.0 - ADAM_B2 ** ADAM_STEP)

    def body(r_ref, w_ref, m_ref, v_ref, g_ref, d_ref, mo_ref, vo_ref):
        g = r_ref[0].astype(F32)
        for s in range(1, n_part):
            g = g + r_ref[s].astype(F32)
        mn = ADAM_B1 * m_ref[0] + (1.0 - ADAM_B1) * g
        vn = ADAM_B2 * v_ref[0] + (1.0 - ADAM_B2) * (g * g)
        g_ref[0] = g
        mo_ref[0] = mn
        vo_ref[0] = vn
        d_ref[0] = -ADAM_LR * ((mn / c1) / (jnp.sqrt(vn / c2) + ADAM_EPS) + ADAM_WD * w_ref[0])

    one = pl.BlockSpec((1, tr, c), lambda i: (0, i, 0))
    return pl.pallas_call(
        body, grid=(r // tr,), in_specs=[pl.BlockSpec((n_part, tr, c), lambda i: (0, i, 0)), one, one, one],
        out_specs=[one] * 4, out_shape=[jax.ShapeDtypeStruct((1, r, c), F32)] * 4,
        name=name, compiler_params=_cp())(recv, w, m, v)


def _local_step(x, target, sp, need, ahead, emit):
    g = {}
    x0, h1 = x, _rms_fwd(x, sp["norm_ffn1"], "ffn1_norm")
    act1, saved1 = _ffn_up(h1, lambda kind, a: need("w" + kind + "1", a),
                           lambda kind, a: ahead("w" + kind + "1", a), "ffn1")
    x1, h2 = _mm_nn_resnorm([(act1, saved1[-1])], x0, 0.5, sp["norm_mix"], 512, "ffn1_down_norm",
                            after=ahead("win_a", act1))
    win_a, win_d = need("win_a", h2), need("win_d", h2)
    conv_w8, wout = need("conv_w8", h2), need("wout", h2)
    z_at = _mm_nn(h2, win_a, F32, 1024, 768, "mix_in_attn")
    z_dn = _mm_nn(h2, win_d, F32, 1024, 768, "mix_in_dn")

    conv, qkvn, bg = _dn_pre_fwd(z_dn, conv_w8, sp["alog_row"], sp["dtb_row"], "dn_pre")
    *prep, tinv = _dn_prep_fwd(qkvn, bg, "dn_prep")
    o_dn, states = _dn_scan_fwd(prep, "dn_scan")
    dn_b = _dn_post_fwd(o_dn, z_dn, sp["dn_norm"], "dn_post")
    attn_f, attn_b, lse = _attn_fwd(z_at, "attn_fwd", after=ahead("wg2", dn_b))

    x2, h3 = _mm_nn_resnorm([(attn_b, wout[:AW]), (dn_b, wout[AW:])], x1, 1.0, sp["norm_ffn2"], 512,
                            "mix_out_norm")
    act2, saved2 = _ffn_up(h3, lambda kind, a: need("w" + kind + "2", a),
                           lambda kind, a: ahead("w" + kind + "2", a), "ffn2")

    dx3, dys3, g["norm_final"], loss8 = _down_loss_bwd(x2, act2, saved2[-1], sp["norm_final"], target,
                                                       "ffn2_down_loss")
    dx2, dx2b, g["norm_ffn2"] = _ffn_bwd(
        x2, sp["norm_ffn2"], saved2, dx3, dys3, 1.0,
        lambda kind, dw: emit(kind + "2", {"w" + kind + "2": dw}), "ffn2b")

    zero = emit("wout", {"wout": jnp.concatenate([_mm_tn(attn_b, dx2b, BF, 512, 1024, "mix_out_dw_a"),
                                                  _mm_tn(dn_b, dx2b, BF, 512, 1024, "mix_out_dw_d")], axis=0)})
    dmix = _mm_nt(dx2b, wout, F32, 1024, 1024, "mix_out_dx", after=zero)

    dz_at = jnp.concatenate(_attn_bwd(z_at, dmix, attn_f, lse, "attn_bwd"), axis=1)

    do_dn, dgate, g["dn_norm"] = _dn_post_bwd(o_dn, z_dn, sp["dn_norm"], dmix, "dn_post_b")
    cts = _dn_scan_bwd(prep, states, do_dn, "dn_scan_b")
    dqn, dkn, dvn, dbg = _dn_prep_bwd(qkvn, bg, tinv, cts, "dn_prep_b")
    dconv, dbd, g["alog_row"], g["dtb_row"] = _dn_pre_bwd(
        conv, z_dn, sp["alog_row"], sp["dtb_row"], dqn, dkn, dvn, dbg, "dn_pre_b")
    draw, dconv_w8 = _dn_conv_bwd(dconv, z_dn, conv_w8, "dn_conv_b")
    dz_dn = _dz_dn_assemble(draw, dgate, dbd, "dn_dz")

    zero = emit("win", {"win_a": _mm_tn(h2, dz_at, BF, 512, 768, "mix_in_dw_a"),
                        "win_d": _mm_tn(h2, dz_dn, BF, 512, 768, "mix_in_dw_d"), "conv_w8": dconv_w8})
    dx1, dys1, g["norm_mix"] = _mm_rms_bwd([(dz_at, win_a), (dz_dn, win_d)], NT, x1, sp["norm_mix"], dx2, 0.5,
                                           512, "mix_in_dx_dnorm", after=zero)
    dx0, _, g["norm_ffn1"] = _ffn_bwd(
        x0, sp["norm_ffn1"], saved1, dx1, dys1, 1.0,
        lambda kind, dw: emit(kind + "1", {"w" + kind + "1": dw}), "ffn1b")
    return loss8[0, 0], dx0, g


def _cols_from_shards(gathered):
    n, r, c = gathered.shape
    return jnp.transpose(gathered, (1, 0, 2)).reshape(r, n * c)


def _shards_from_cols(full, dtype):
    r, nc = full.shape
    return jnp.transpose(full.reshape(r, N_DEV, nc // N_DEV), (1, 0, 2)).astype(dtype)


def _lane_row(vec4):
    return jnp.zeros((1, 128), F32).at[:, DN_H:2 * DN_H].set(vec4.astype(F32))


WEIGHT_SOURCES = {"wg1": "gate1", "wu1": "up1", "wd1": "down1", "win_a": "w_in", "win_d": "w_in",
                  "conv_w8": "conv_w", "wout": "w_out", "wg2": "gate2", "wu2": "up2", "wd2": "down2"}
TRANSPOSED = ("gate1", "up1", "gate2", "up2")


def _build_weights(name, gath):
    if name in ("wg1", "wu1", "wd1", "wg2", "wu2", "wd2"):
        return {name: gath[WEIGHT_SOURCES[name]].reshape(F, D)}
    if name in ("win_a", "win_d"):
        w_in = _cols_from_shards(gath["w_in"])
        c0 = 3 * AW + 3 * DNW
        win_d = jnp.concatenate([w_in[:, ZA:c0], w_in[:, c0 + 2 * DN_H:], w_in[:, c0:c0 + 2 * DN_H],
                                 jnp.zeros((D, ZP - IN_COLS), w_in.dtype)], axis=1)
        return {"win_a": w_in[:, :ZA], "win_d": win_d}
    if name == "wout":
        return {name: gath["w_out"].reshape(D, D)}
    conv = _cols_from_shards(gath["conv_w"])
    return {"conv_w8": jnp.concatenate([conv, jnp.zeros((4, 3 * DNW), F32)], axis=0)}


def _small_params(norm_ffn1, norm_mix, norm_ffn2, norm_final, a_log, dt_bias, dn_norm):
    return {"norm_ffn1": norm_ffn1, "norm_mix": norm_mix, "norm_ffn2": norm_ffn2,
            "norm_final": norm_final.reshape(1, D), "alog_row": _lane_row(a_log), "dtb_row": _lane_row(dt_bias),
            "dn_norm": dn_norm}


def _grad_slabs(group, g):
    if group[0] in "gud":
        return {WEIGHT_SOURCES["w" + group]: g["w" + group].reshape(N_DEV, F // N_DEV, D)}
    if group == "wout":
        return {"w_out": g["wout"].reshape(N_DEV, D // N_DEV, D)}
    gp = jnp.concatenate([g["win_a"], g["win_d"]], axis=1)
    c0 = 3 * AW + 3 * DNW
    g_in = jnp.concatenate([gp[:, :c0], gp[:, c0 + DNW:c0 + DNW + 2 * DN_H], gp[:, c0:c0 + DNW]], axis=1)
    return {"w_in": _shards_from_cols(g_in, BF), "conv_w": _shards_from_cols(g["conv_w8"][:4], F32)}


SMALL_ROWS = 40


def _small_pack(norm_ffn1, norm_mix, norm_ffn2, norm_final, dn_norm, alog_row, dtb_row, loss=None):
    rows = [a.reshape(8, 128) for a in (norm_ffn1, norm_mix, norm_ffn2, norm_final)]
    loss_row = jnp.zeros((1, 128), F32) if loss is None else jnp.broadcast_to(loss.reshape(1, 1), (1, 128))
    rows += [dn_norm.reshape(1, 128), alog_row, dtb_row, loss_row, jnp.zeros((SMALL_ROWS - 36, 128), F32)]
    return jnp.concatenate(rows, axis=0)


def _small_unpack(pk):
    pk = pk[0]
    return (pk[0:8].reshape(1, D), pk[8:16].reshape(1, D), pk[16:24].reshape(1, D), pk[24:32].reshape(D),
            pk[32:33], pk[33:34, DN_H:2 * DN_H], pk[34:35, DN_H:2 * DN_H])


ADAM_TILE = {"gate1": 256, "up1": 256, "down1": 176, "gate2": 256, "up2": 256, "down2": 176,
             "w_in": 256, "w_out": 128, "conv_w": 4}
BIG = ("gate1", "up1", "down1", "w_in", "w_out", "gate2", "up2", "down2", "conv_w")


def kernel(x, norm_ffn1, ffn1_gate, ffn1_up, ffn1_down, norm_mix, w_in, conv_w, a_log, dt_bias, dn_norm, w_out, norm_ffn2, ffn2_gate, ffn2_up, ffn2_down, norm_final, loss_target, m_norm_ffn1, m_ffn1_gate, m_ffn1_up, m_ffn1_down, m_norm_mix, m_w_in, m_conv_w, m_a_log, m_dt_bias, m_dn_norm, m_w_out, m_norm_ffn2, m_ffn2_gate, m_ffn2_up, m_ffn2_down, m_norm_final, v_norm_ffn1, v_ffn1_gate, v_ffn1_up, v_ffn1_down, v_norm_mix, v_w_in, v_conv_w, v_a_log, v_dt_bias, v_dn_norm, v_w_out, v_norm_ffn2, v_ffn2_gate, v_ffn2_up, v_ffn2_down, v_norm_final):
    w = {"gate1": ffn1_gate, "up1": ffn1_up, "down1": ffn1_down, "w_in": w_in, "w_out": w_out,
         "gate2": ffn2_gate, "up2": ffn2_up, "down2": ffn2_down, "conv_w": conv_w}
    m = {"gate1": m_ffn1_gate, "up1": m_ffn1_up, "down1": m_ffn1_down, "w_in": m_w_in, "w_out": m_w_out,
         "gate2": m_ffn2_gate, "up2": m_ffn2_up, "down2": m_ffn2_down, "conv_w": m_conv_w}
    v = {"gate1": v_ffn1_gate, "up1": v_ffn1_up, "down1": v_ffn1_down, "w_in": v_w_in, "w_out": v_w_out,
         "gate2": v_ffn2_gate, "up2": v_ffn2_up, "down2": v_ffn2_down, "conv_w": v_conv_w}

    me = 4 * lax.axis_index("x") + 2 * lax.axis_index("y") + lax.axis_index("c")
    own_slot = lambda land, mine: lax.dynamic_update_index_in_dim(land, mine, me, 0)

    ag_order = ("gate1", "up1", "down1", "w_in", "conv_w", "w_out", "gate2", "up2", "down2")
    ag_groups = (("gate1",), ("up1",), ("down1",), ("w_in", "conv_w", "w_out"), ("gate2", "up2", "down2"))
    pos = {n: i for i, n in enumerate(ag_order)}

    def shard(n):
        if n == "conv_w":
            return w[n][0]
        return (w[n][0].T if n in TRANSPOSED else w[n][0]).astype(BF)

    ss, rs, thru, land, zero = _xstart([shard(n) for n in ag_order], False, "weights_start", FIRST_HOP)
    gath, built, on_its_way = {}, {}, {}
    group_of = lambda name: [i for i, grp in enumerate(ag_groups) if WEIGHT_SOURCES[name] in grp][0]

    def ahead(name, after):
        gi = group_of(name)
        if WEIGHT_SOURCES[name] in gath or gi in on_its_way:
            return None
        ids = [pos[n] for n in ag_groups[gi]]
        thrus, lands = _xwait([ss[i] for i in ids], [rs[i] for i in ids], [thru[i] for i in ids],
                              [land[i] for i in ids], False, after, "weights_wait%d" % gi, FIRST_HOP)
        fss, frs, lands, token = _fstart(lands, "weights_forward%d" % gi)
        on_its_way[gi] = (thrus, fss, frs, lands)
        return token

    def need(name, after):
        if name not in built:
            if WEIGHT_SOURCES[name] not in gath:
                gi = group_of(name)
                ahead(name, after)
                thrus, fss, frs, lands = on_its_way.pop(gi)
                lands = _fwait(fss, frs, lands, after, "weights_forward_wait%d" % gi)
                for n, t, l in zip(ag_groups[gi], thrus, lands):
                    gath[n] = own_slot(l, t)
            built.update(_build_weights(name, gath))
        return built[name]

    pending = []

    def emit(group, grads):
        slabs = grads if group == "small" else _grad_slabs(group, grads)
        names = list(slabs)
        started = _xstart([slabs[n] for n in names], True, "grads_start_" + group)
        pending.append((group, names) + started[:4])
        return started[4]

    sp = _small_params(norm_ffn1 + zero, norm_mix, norm_ffn2, norm_final, a_log, dt_bias, dn_norm)
    loss_part, dx, g = _local_step(x[0], loss_target[0], sp, need, ahead, emit)
    small = _small_pack(g["norm_ffn1"], g["norm_mix"], g["norm_ffn2"], g["norm_final"], g["dn_norm"],
                        g["alog_row"], g["dtb_row"], loss_part)
    emit("small", {"small": jnp.broadcast_to(small[None], (N_DEV, SMALL_ROWS, 128))})

    pack = lambda a: _small_pack(*a)[None]
    res, after = {}, dx
    for group, names, gss, grs, gthru, gland in pending:
        thrus, lands = _xwait(gss, grs, gthru, gland, True, after, "grads_wait_" + group)
        for n, t, l in zip(names, thrus, lands):
            recv = own_slot(l, lax.dynamic_index_in_dim(t, me, 0, keepdims=False))
            if n == "small":
                res[n] = _adam(
                    recv,
                    pack((norm_ffn1, norm_mix, norm_ffn2, norm_final, dn_norm, _lane_row(a_log), _lane_row(dt_bias))),
                    pack((m_norm_ffn1, m_norm_mix, m_norm_ffn2, m_norm_final, m_dn_norm, _lane_row(m_a_log),
                          _lane_row(m_dt_bias))),
                    pack((v_norm_ffn1, v_norm_mix, v_norm_ffn2, v_norm_final, v_dn_norm, _lane_row(v_a_log),
                          _lane_row(v_dt_bias))),
                    SMALL_ROWS, "adam_small")
            elif n in TRANSPOSED:
                flip = lambda a: jnp.swapaxes(a, 1, 2)
                res[n] = [flip(o) for o in _adam(recv, flip(w[n]), flip(m[n]), flip(v[n]), F // N_DEV // 2,
                                                 "adam_" + n)]
            else:
                res[n] = _adam(recv, w[n], m[n], v[n], ADAM_TILE[n], "adam_" + n)
            after = res[n][0]
    res_s = res["small"]

    loss = res_s[0][0, 35, 0]
    outs = [loss, dx[None]]
    for k in range(4):
        n1, nm, n2, nf, dn, al, dt = _small_unpack(res_s[k])
        big = {n: res[n][k] for n in BIG}
        outs += [n1, big["gate1"], big["up1"], big["down1"], nm, big["w_in"], big["conv_w"], al, dt, dn,
                 big["w_out"], n2, big["gate2"], big["up2"], big["down2"], nf]
    return tuple(outs)
```

```python
import functools

import numpy as np
import jax
import jax.numpy as jnp
from jax import lax
from jax.experimental import pallas as pl
from jax.experimental.pallas import tpu as pltpu

T = 4096
D = 1024
F = 2816
N_DEV = 8
A_HEADS = 8
A_HD = 64
AW = A_HEADS * A_HD
DN_H = 4
DN_HD = 128
DNW = DN_H * DN_HD
CH = 64
PAIR = 2 * CH
ZA = 3 * AW
ZD = 3 * DNW + DNW + 256
ZP = ZA + ZD
BD_BLK = (3 * DNW + DNW) // 128
IN_COLS = 3592
PATTERNS = ((128, 1), (512, 4), (2048, 16))
NORM_EPS = 1e-6
L2_EPS = 1e-6
ADAM_LR, ADAM_B1, ADAM_B2, ADAM_EPS, ADAM_WD, ADAM_STEP = 0.001, 0.9, 0.999, 1e-08, 0.01, 10
VMEM_LIMIT = 56 * 1024 * 1024
NEG = -1e30

BF = jnp.bfloat16
F32 = jnp.float32
NN = (((1,), (0,)), ((), ()))
NT = (((1,), (1,)), ((), ()))
TN = (((0,), (0,)), ((), ()))
HI = lax.Precision.HIGHEST
MESH_ID = pl.DeviceIdType.MESH
ANY = pl.BlockSpec(memory_space=pl.ANY)


def _cp():
    return pltpu.CompilerParams(vmem_limit_bytes=VMEM_LIMIT)


def _dg(a, b, dims):
    return lax.dot_general(a, b, dims, preferred_element_type=F32)


def _hdot(a, b):
    return lax.dot_general(a, b, NN, precision=HI, preferred_element_type=F32)


def _make_bdot(dims, da_dims, da_swap, db_dims, db_swap):
    @jax.custom_vjp
    def f(a, b):
        return _dg(a.astype(BF), b.astype(BF), dims)

    def fwd(a, b):
        return f(a, b), (a, b)

    def bwd(res, g):
        a, b = res
        gb, ab, bb = g.astype(BF), a.astype(BF), b.astype(BF)
        da = _dg(bb, gb, da_dims) if da_swap else _dg(gb, bb, da_dims)
        db = _dg(gb, ab, db_dims) if db_swap else _dg(ab, gb, db_dims)
        return da, db

    f.defvjp(fwd, bwd)
    return f


_bdot_nn = _make_bdot(NN, NT, False, TN, False)
_bdot_nt = _make_bdot(NT, NN, False, TN, True)
_bdot_tn = _make_bdot(TN, NT, True, NN, False)


def _iota(shape, dim):
    return lax.broadcasted_iota(jnp.int32, shape, dim)


def _col(x, idx):
    return jnp.sum(jnp.where(_iota(x.shape, 1) == idx, x, 0.0), axis=1, keepdims=True)


def _mm_nn(a, b, out_dtype, tm, tn, name):
    m, k = a.shape
    n = b.shape[1]

    def body(a_ref, b_ref, o_ref):
        o_ref[...] = _dg(a_ref[...], b_ref[...], NN).astype(out_dtype)

    return pl.pallas_call(
        body, grid=(m // tm, n // tn),
        in_specs=[pl.BlockSpec((tm, k), lambda i, j: (i, 0)), pl.BlockSpec((k, tn), lambda i, j: (0, j))],
        out_specs=pl.BlockSpec((tm, tn), lambda i, j: (i, j)),
        out_shape=jax.ShapeDtypeStruct((m, n), out_dtype), name=name, compiler_params=_cp())(a, b)


def _tie(body, after):
    if after is None:
        return body, [], []
    return (lambda tok_ref, *refs: body(*refs)), [ANY], [after.reshape(1, 1)]


def _mm_nt(a, b, out_dtype, tm, tb, name, after=None):
    m, c = a.shape
    kb = b.shape[0]

    def body(a_ref, b_ref, o_ref):
        o_ref[...] = _dg(a_ref[...], b_ref[...], NT).astype(out_dtype)

    body, tspec, tok = _tie(body, after)
    return pl.pallas_call(
        body, grid=(m // tm, kb // tb),
        in_specs=tspec + [pl.BlockSpec((tm, c), lambda i, j: (i, 0)), pl.BlockSpec((tb, c), lambda i, j: (j, 0))],
        out_specs=pl.BlockSpec((tm, tb), lambda i, j: (i, j)),
        out_shape=jax.ShapeDtypeStruct((m, kb), out_dtype), name=name, compiler_params=_cp())(*tok, a, b)


def _mm_tn(a, b, out_dtype, ta, tb, name, after=None):
    m, ka = a.shape
    nb = b.shape[1]

    def body(a_ref, b_ref, o_ref):
        o_ref[...] = _dg(a_ref[...], b_ref[...], TN).astype(out_dtype)

    body, tspec, tok = _tie(body, after)
    return pl.pallas_call(
        body, grid=(ka // ta, nb // tb),
        in_specs=tspec + [pl.BlockSpec((m, ta), lambda i, j: (0, i)), pl.BlockSpec((m, tb), lambda i, j: (0, j))],
        out_specs=pl.BlockSpec((ta, tb), lambda i, j: (i, j)),
        out_shape=jax.ShapeDtypeStruct((ka, nb), out_dtype), name=name, compiler_params=_cp())(*tok, a, b)


def _mm_nt_swiglu(h, wu_t, gate, tm, tb, name, after=None):
    m, c = h.shape
    kb = wu_t.shape[0]

    def body(h_ref, w_ref, g_ref, u_ref, a_ref):
        u = _dg(h_ref[...], w_ref[...], NT)
        g = g_ref[...].astype(F32)
        u_ref[...] = u.astype(BF)
        a_ref[...] = (g * jax.nn.sigmoid(g) * u).astype(BF)

    body, tspec, tok = _tie(body, after)
    tile = pl.BlockSpec((tm, tb), lambda i, j: (i, j))
    return pl.pallas_call(
        body, grid=(m // tm, kb // tb),
        in_specs=tspec + [pl.BlockSpec((tm, c), lambda i, j: (i, 0)), pl.BlockSpec((tb, c), lambda i, j: (j, 0)),
                          tile],
        out_specs=[tile, tile], out_shape=[jax.ShapeDtypeStruct((m, kb), BF)] * 2,
        name=name, compiler_params=_cp())(*tok, h, wu_t, gate)


def _mm_nt_dswiglu(dys, wd, gate, up, tm, tb, name, after=None):
    m, c = dys.shape
    kb = wd.shape[0]

    def body(d_ref, w_ref, g_ref, u_ref, dg_ref, du_ref):
        da = _dg(d_ref[...], w_ref[...], NT)
        g = g_ref[...].astype(F32)
        u = u_ref[...].astype(F32)
        s = jax.nn.sigmoid(g)
        dg_ref[...] = (da * u * (s * (1.0 + g * (1.0 - s)))).astype(BF)
        du_ref[...] = (da * (g * s)).astype(BF)

    body, tspec, tok = _tie(body, after)
    tile = pl.BlockSpec((tm, tb), lambda i, j: (i, j))
    return pl.pallas_call(
        body, grid=(m // tm, kb // tb),
        in_specs=tspec + [pl.BlockSpec((tm, c), lambda i, j: (i, 0)), pl.BlockSpec((tb, c), lambda i, j: (j, 0)),
                          tile, tile],
        out_specs=[tile, tile], out_shape=[jax.ShapeDtypeStruct((m, kb), BF)] * 2,
        name=name, compiler_params=_cp())(*tok, dys, wd, gate, up)


def _rms_fwd(x, gain, name):
    tm = 512
    row = pl.BlockSpec((tm, D), lambda i: (i, 0))

    def body(x_ref, g_ref, h_ref):
        xv = x_ref[...]
        r = lax.rsqrt(jnp.mean(xv * xv, axis=-1, keepdims=True) + NORM_EPS)
        h_ref[...] = (xv * r * g_ref[...]).astype(BF)

    return pl.pallas_call(
        body, grid=(T // tm,), in_specs=[row, pl.BlockSpec((1, D), lambda i: (0, 0))], out_specs=row,
        out_shape=jax.ShapeDtypeStruct((T, D), BF), name=name, compiler_params=_cp())(x, gain)


def _mm_rms_bwd(pairs, dims, x, gain, dres, alpha_out, tm, name, after=None):
    n = len(pairs)
    row = pl.BlockSpec((tm, D), lambda i: (i, 0))
    gspec = pl.BlockSpec((1, D), lambda i: (0, 0))

    def body(*refs):
        x_ref, g_ref, dres_ref, dx_ref, dxs_ref, dg_ref = refs[2 * n:]
        i = pl.program_id(0)
        dhv = _dg(refs[0][...], refs[n][...], dims)
        for p in range(1, n):
            dhv = dhv + _dg(refs[p][...], refs[n + p][...], dims)
        xv = x_ref[...]
        r = lax.rsqrt(jnp.mean(xv * xv, axis=-1, keepdims=True) + NORM_EPS)
        xh = xv * r
        part = jnp.sum(dhv * xh, axis=0, keepdims=True)

        @pl.when(i == 0)
        def _():
            dg_ref[...] = part

        @pl.when(i > 0)
        def _():
            dg_ref[...] += part

        dxh = dhv * g_ref[...]
        dx = r * (dxh - xh * jnp.mean(dxh * xh, axis=-1, keepdims=True)) + dres_ref[...]
        dx_ref[...] = dx
        dxs_ref[...] = (alpha_out * dx).astype(BF)

    body, tspec, tok = _tie(body, after)
    return pl.pallas_call(
        body, grid=(T // tm,),
        in_specs=tspec + [pl.BlockSpec((tm, a.shape[1]), lambda i: (i, 0)) for a, _ in pairs]
        + [pl.BlockSpec(b.shape, lambda i: (0, 0)) for _, b in pairs] + [row, gspec, row],
        out_specs=[row, row, gspec],
        out_shape=[jax.ShapeDtypeStruct((T, D), F32), jax.ShapeDtypeStruct((T, D), BF),
                   jax.ShapeDtypeStruct((1, D), F32)],
        name=name, compiler_params=_cp())(*tok, *[a for a, _ in pairs], *[b for _, b in pairs], x, gain, dres)


def _down_loss_bwd(x_prev, act, wd, gain, target, name):
    tm = 512
    row = pl.BlockSpec((tm, D), lambda i: (i, 0))
    gspec = pl.BlockSpec((1, D), lambda i: (0, 0))
    lspec = pl.BlockSpec((8, 128), lambda i: (0, 0))

    def body(x_ref, a_ref, w_ref, g_ref, t_ref, dx_ref, dxs_ref, dg_ref, loss_ref):
        i = pl.program_id(0)
        xv = x_ref[...] + 0.5 * _dg(a_ref[...], w_ref[...], NN)
        r = lax.rsqrt(jnp.mean(xv * xv, axis=-1, keepdims=True) + NORM_EPS)
        xh = xv * r
        diff = xh * g_ref[...] - t_ref[...]
        lpart = 0.5 * jnp.sum(jnp.mean(diff * diff, axis=-1, keepdims=True), axis=0, keepdims=True)
        dy = diff * (1.0 / D)
        part = jnp.sum(dy * xh, axis=0, keepdims=True)

        @pl.when(i == 0)
        def _():
            dg_ref[...] = part
            loss_ref[...] = jnp.broadcast_to(lpart, (8, 128))

        @pl.when(i > 0)
        def _():
            dg_ref[...] += part
            loss_ref[...] += jnp.broadcast_to(lpart, (8, 128))

        dxh = dy * g_ref[...]
        dx = r * (dxh - xh * jnp.mean(dxh * xh, axis=-1, keepdims=True))
        dx_ref[...] = dx
        dxs_ref[...] = (0.5 * dx).astype(BF)

    return pl.pallas_call(
        body, grid=(T // tm,),
        in_specs=[row, pl.BlockSpec((tm, F), lambda i: (i, 0)), pl.BlockSpec((F, D), lambda i: (0, 0)), gspec, row],
        out_specs=[row, row, gspec, lspec],
        out_shape=[jax.ShapeDtypeStruct((T, D), F32), jax.ShapeDtypeStruct((T, D), BF),
                   jax.ShapeDtypeStruct((1, D), F32), jax.ShapeDtypeStruct((8, 128), F32)],
        name=name, compiler_params=_cp())(x_prev, act, wd, gain, target)


def _mm_nn_resnorm(pairs, x_prev, alpha, gain, tm, name, after=None):
    m = x_prev.shape[0]
    n = len(pairs)

    def body(*refs):
        x_ref, g_ref, xo_ref, h_ref = refs[2 * n:]
        y = _dg(refs[0][...], refs[n][...], NN)
        for i in range(1, n):
            y = y + _dg(refs[i][...], refs[n + i][...], NN)
        xv = x_ref[...] + alpha * y
        xo_ref[...] = xv
        r = lax.rsqrt(jnp.mean(xv * xv, axis=-1, keepdims=True) + NORM_EPS)
        h_ref[...] = (xv * r * g_ref[...]).astype(BF)

    body, tspec, tok = _tie(body, after)
    row = pl.BlockSpec((tm, D), lambda i: (i, 0))
    return pl.pallas_call(
        body, grid=(m // tm,),
        in_specs=tspec + [pl.BlockSpec((tm, a.shape[1]), lambda i: (i, 0)) for a, _ in pairs]
        + [pl.BlockSpec(b.shape, lambda i: (0, 0)) for _, b in pairs] + [row, pl.BlockSpec((1, D), lambda i: (0, 0))],
        out_specs=[row, row],
        out_shape=[jax.ShapeDtypeStruct((m, D), F32), jax.ShapeDtypeStruct((m, D), BF)],
        name=name, compiler_params=_cp())(*tok, *[a for a, _ in pairs], *[b for _, b in pairs], x_prev, gain)


def _ffn_up(h, need, ahead, tag):
    wg = need("g", h)
    gate = _mm_nt(h, wg, BF, 1024, 1408, tag + "_gate")
    wu = need("u", gate)
    up, act = _mm_nt_swiglu(h, wu, gate, 1024, 1408, tag + "_up_act", after=ahead("d", wu))
    wd = need("d", up)
    return act, (h, gate, up, act, wg, wu, wd)


def _ffn_bwd(x_in, gain, saved, dxo, dys, alpha_out, emit, tag):
    h, gate, up, act, wg, wu, wd = saved
    sent = emit("d", _mm_tn(act, dys, BF, 256, 1024, tag + "_dwd"))
    dgate, dup = _mm_nt_dswiglu(dys, wd, gate, up, 1024, 1408, tag + "_dact", after=sent)
    sent = emit("g", _mm_tn(dgate, h, BF, 256, 1024, tag + "_dwg"))
    sent = emit("u", _mm_tn(dup, h, BF, 256, 1024, tag + "_dwu", after=sent))
    return _mm_rms_bwd([(dgate, wg), (dup, wu)], NN, x_in, gain, dxo, alpha_out, 512, tag + "_dh_dnorm",
                       after=sent)


SLAB = 2048
N_SLAB = T // SLAB
N_PAIR = A_HEADS // 2


def _pair_masks():
    lane = _iota((128, 128), 1)
    return lane < A_HD, lane >= A_HD


def _slope_table():
    h = 2 * jnp.arange(N_PAIR)[:, None] + jnp.minimum(jnp.arange(8), 1)[None, :]
    return jnp.broadcast_to((2.0 ** (-(h + 1).astype(F32)))[:, :, None], (N_PAIR, 8, 128))


def _rows(ref, start, d):
    if d == 1:
        return ref[pl.ds(start, 128), :]
    return ref[pl.ds(start, 128, stride=d), :]


def _put_rows(ref, start, d, val):
    if d == 1:
        ref[pl.ds(start, 128), :] = val
    else:
        ref[pl.ds(start, 128, stride=d), :] = val


def _units(d):
    return [(r, b, r + 128 * d * b) for r in range(d) for b in range(SLAB // (128 * d))]


def _biases(d, slopes, has_prev):
    qi = _iota((128, 256), 0)
    kj = _iota((128, 256), 1)
    steps = qi + 128 - kj
    in_band = (steps >= 0) & (steps <= 128)
    dist = (steps * d).astype(F32)
    base = [jnp.where(in_band, -(sl * dist), NEG) for sl in slopes]
    edge = [jnp.where(has_prev | (kj >= 128), b, NEG) for b in base]
    return base, edge


def _attn_fwd(z_at, name, after=None):
    def body(sl_ref, q_ref, kc_ref, kp_ref, vc_ref, vp_ref, of_ref, ob_ref, lse_ref, m_s, l_s, a_s):
        n = pl.program_id(1)
        lo, hi = _pair_masks()
        slopes = (sl_ref[0, 0:1, 0:1], sl_ref[0, 1:2, 0:1])

        def unit(d, start, b, first, carry, bias):
            q = _rows(q_ref, start, d).astype(BF)
            kcur, vcur = _rows(kc_ref, start, d).astype(BF), _rows(vc_ref, start, d).astype(BF)
            if b > 0:
                kprev, vprev = carry
            else:
                pstart = start + SLAB - 128 * d
                kprev, vprev = _rows(kp_ref, pstart, d).astype(BF), _rows(vp_ref, pstart, d).astype(BF)
            kcat = jnp.concatenate([kprev, kcur], axis=0)
            vcat = jnp.concatenate([vprev, vcur], axis=0)
            ms, ls, pvs = [], [], []
            for e in range(2):
                qm = jnp.where(lo if e == 0 else hi, q, jnp.zeros_like(q))
                s = _dg(qm, kcat, NT) * (A_HD ** -0.5) + bias[e]
                m = jnp.max(s, axis=1, keepdims=True)
                p = jnp.exp(s - m)
                ms.append(m)
                ls.append(jnp.sum(p, axis=1, keepdims=True))
                pvs.append(_dg(p.astype(BF), vcat, NN))
            m_u = jnp.where(lo, ms[0], ms[1])
            l_u = jnp.where(lo, ls[0], ls[1])
            a_u = jnp.where(lo, pvs[0], pvs[1])
            if first:
                m_n, l_n, a_n = m_u, l_u, a_u
            else:
                m_o = _rows(m_s, start, d)
                m_n = jnp.maximum(m_o, m_u)
                c_o = jnp.exp(m_o - m_n)
                c_u = jnp.exp(m_u - m_n)
                l_n = _rows(l_s, start, d) * c_o + l_u * c_u
                a_n = _rows(a_s, start, d) * c_o + a_u * c_u
            _put_rows(m_s, start, d, m_n)
            _put_rows(l_s, start, d, l_n)
            _put_rows(a_s, start, d, a_n)
            return kcur, vcur

        for pi, (_, d) in enumerate(PATTERNS):
            base, edge = _biases(d, slopes, n > 0)
            carry = None
            for r, b, start in _units(d):
                carry = unit(d, start, b, pi == 0, carry, edge if b == 0 else base)
        l = l_s[...]
        out = a_s[...] / l
        of_ref[...] = out
        ob_ref[...] = out.astype(BF)
        lse_ref[...] = m_s[...] + jnp.log(l)

    body, tspec, tok = _tie(body, after)
    cur = lambda c: pl.BlockSpec((SLAB, 128), lambda j, n: (n, c * N_PAIR + j))
    prv = lambda c: pl.BlockSpec((SLAB, 128), lambda j, n: (jnp.maximum(n - 1, 0), c * N_PAIR + j))
    out = pl.BlockSpec((SLAB, 128), lambda j, n: (n, j))
    return pl.pallas_call(
        body, grid=(N_PAIR, N_SLAB),
        in_specs=tspec + [pl.BlockSpec((1, 8, 128), lambda j, n: (j, 0, 0)), cur(0), cur(1), prv(1), cur(2), prv(2)],
        out_specs=[out, out, out],
        out_shape=[jax.ShapeDtypeStruct((T, AW), F32), jax.ShapeDtypeStruct((T, AW), BF),
                   jax.ShapeDtypeStruct((T, AW), F32)],
        scratch_shapes=[pltpu.VMEM((SLAB, 128), F32)] * 3,
        name=name, compiler_params=_cp())(*tok, _slope_table(), z_at, z_at, z_at, z_at, z_at)


def _attn_bwd(z_at, dout, out, lse, name):
    def body(sl_ref, q_ref, kc_ref, kp_ref, vc_ref, vp_ref, do_ref, o_ref, lse_ref, dq_ref, dk_ref, dv_ref,
             dq_s, dk_s, dv_s, ck_s, cv_s):
        step = pl.program_id(1)
        n = N_SLAB - 1 - step
        lo, hi = _pair_masks()
        slopes = (sl_ref[0, 0:1, 0:1], sl_ref[0, 1:2, 0:1])

        @pl.when(step == 0)
        def _():
            ck_s[...] = jnp.zeros_like(ck_s)
            cv_s[...] = jnp.zeros_like(cv_s)

        dk_s[...] = ck_s[...]
        dv_s[...] = cv_s[...]
        ck_s[...] = jnp.zeros_like(ck_s)
        cv_s[...] = jnp.zeros_like(cv_s)

        def add_rows(ref, start, d, val):
            _put_rows(ref, start, d, _rows(ref, start, d) + val)

        def unit(d, start, b, first, carry, bias):
            q = _rows(q_ref, start, d).astype(BF)
            do_f = _rows(do_ref, start, d)
            do = do_f.astype(BF)
            prod = do_f * _rows(o_ref, start, d)
            lse_u = _rows(lse_ref, start, d)
            kcur, vcur = _rows(kc_ref, start, d).astype(BF), _rows(vc_ref, start, d).astype(BF)
            if b > 0:
                kprev, vprev = carry
            else:
                pstart = start + SLAB - 128 * d
                kprev, vprev = _rows(kp_ref, pstart, d).astype(BF), _rows(vp_ref, pstart, d).astype(BF)
            kcat = jnp.concatenate([kprev, kcur], axis=0)
            vcat = jnp.concatenate([vprev, vcur], axis=0)
            masks = (lo, hi)
            qms = [jnp.where(msk, q, jnp.zeros_like(q)) for msk in masks]
            doms = [jnp.where(msk, do, jnp.zeros_like(do)) for msk in masks]
            deltas = [jnp.sum(jnp.where(msk, prod, 0.0), axis=1, keepdims=True) for msk in masks]
            ss = [_dg(qm, kcat, NT) * (A_HD ** -0.5) + bs for qm, bs in zip(qms, bias)]
            dps = [_dg(dom, vcat, NT) for dom in doms]
            ps = [jnp.exp(s - lse_u[:, 64 * e:64 * e + 1]) for e, s in enumerate(ss)]
            dss = [(p * (dp - delta)).astype(BF) for p, dp, delta in zip(ps, dps, deltas)]
            pbs = [p.astype(BF) for p in ps]
            dqs = [_dg(ds, kcat, NN) for ds in dss]
            dkc = (_dg(dss[0], qms[0], TN) + _dg(dss[1], qms[1], TN)) * (A_HD ** -0.5)
            dvc = _dg(pbs[0], doms[0], TN) + _dg(pbs[1], doms[1], TN)
            dq_u = jnp.where(lo, dqs[0], dqs[1]) * (A_HD ** -0.5)
            if first:
                _put_rows(dq_s, start, d, dq_u)
            else:
                add_rows(dq_s, start, d, dq_u)
            add_rows(dk_s, start, d, dkc[128:])
            add_rows(dv_s, start, d, dvc[128:])
            if b > 0:
                add_rows(dk_s, start - 128 * d, d, dkc[:128])
                add_rows(dv_s, start - 128 * d, d, dvc[:128])
            else:
                pstart = start + SLAB - 128 * d
                add_rows(ck_s, pstart, d, dkc[:128])
                add_rows(cv_s, pstart, d, dvc[:128])
            return kcur, vcur

        for pi, (_, d) in enumerate(PATTERNS):
            base, edge = _biases(d, slopes, n > 0)
            carry = None
            for r, b, start in _units(d):
                carry = unit(d, start, b, pi == 0, carry, edge if b == 0 else base)
        dq_ref[...] = dq_s[...].astype(BF)
        dk_ref[...] = dk_s[...].astype(BF)
        dv_ref[...] = dv_s[...].astype(BF)

    rev = lambda n: N_SLAB - 1 - n
    cur = lambda c: pl.BlockSpec((SLAB, 128), lambda j, n: (rev(n), c * N_PAIR + j))
    prv = lambda c: pl.BlockSpec((SLAB, 128), lambda j, n: (jnp.maximum(rev(n) - 1, 0), c * N_PAIR + j))
    one = pl.BlockSpec((SLAB, 128), lambda j, n: (rev(n), j))
    return pl.pallas_call(
        body, grid=(N_PAIR, N_SLAB),
        in_specs=[pl.BlockSpec((1, 8, 128), lambda j, n: (j, 0, 0)), cur(0), cur(1), prv(1), cur(2), prv(2),
                  one, one, one],
        out_specs=[one, one, one], out_shape=[jax.ShapeDtypeStruct((T, AW), BF)] * 3,
        scratch_shapes=[pltpu.VMEM((SLAB, 128), F32)] * 5,
        name=name, compiler_params=_cp())(_slope_table(), z_at, z_at, z_at, z_at, z_at, dout, out, lse)


def _silu(x):
    return x * jax.nn.sigmoid(x)


def _qk_math(c):
    s = _silu(c)
    return s * lax.rsqrt(jnp.sum(s * s, axis=-1, keepdims=True) + L2_EPS)


def _softplus(x):
    return jnp.maximum(x, 0.0) + jnp.log(1.0 + jnp.exp(-jnp.abs(x)))


def _gate_math(bd, alog_row, dtb_row):
    rows = bd.shape[0]
    lane = _iota(bd.shape, 1)
    beta = jax.nn.sigmoid(bd)
    g = jnp.where((lane >= DN_H) & (lane < 2 * DN_H), -jnp.exp(alog_row) * _softplus(bd + dtb_row), 0.0)
    ri = _iota((rows, rows), 0)
    ci = _iota((rows, rows), 1)
    same = (ri // CH) == (ci // CH)
    li = _iota((128, 128), 0)
    lj = _iota((128, 128), 1)
    to_next_group = jnp.where((lj == li + DN_H) & (li >= DN_H) & (li < 2 * DN_H), 1.0, 0.0)
    gc = _hdot(jnp.where(same & (ci <= ri), 1.0, 0.0), g)
    glast = _hdot(_hdot(jnp.where(same, 1.0, 0.0), g), to_next_group)
    return jnp.where(lane < DN_H, beta, 0.0) + gc + glast


def _shift_down(cur, halo, s):
    if s == 0:
        return cur
    rolled = pltpu.roll(cur, s, 0)
    hr = pltpu.roll(halo, s, 0)
    head = jnp.where(_iota(hr.shape, 0) < s, hr, rolled[:8])
    return jnp.concatenate([head, rolled[8:]], axis=0)


def _shift_up(cur, halo, s):
    if s == 0:
        return cur
    rows = cur.shape[0]
    rolled = pltpu.roll(cur, rows - s, 0)
    hr = pltpu.roll(halo, 8 - s, 0)
    tail = jnp.where(_iota(hr.shape, 0) >= 8 - s, hr, rolled[rows - 8:])
    return jnp.concatenate([rolled[:rows - 8], tail], axis=0)


def _dn_pre_fwd(z_dn, conv_w8, alog_row, dtb_row, name):
    tm = 256
    wq = 3 * DNW

    def body(raw_ref, halo_ref, bd_ref, w_ref, al_ref, dt_ref, conv_ref, qkv_ref, bg_ref):
        i = pl.program_id(0)
        cur = raw_ref[...]
        halo = jnp.where(i > 0, halo_ref[...], 0.0)
        w = w_ref[...]
        conv = jnp.zeros((tm, wq), F32)
        for j in range(4):
            conv = conv + _shift_down(cur, halo, 3 - j) * w[j:j + 1, :]
        conv_ref[...] = conv
        for blk in range(3 * DN_H):
            sl = slice(128 * blk, 128 * blk + 128)
            c = conv[:, sl]
            qkv_ref[:, sl] = _qk_math(c) if blk < 2 * DN_H else _silu(c)
        for r0 in range(0, tm, PAIR):
            rs = slice(r0, r0 + PAIR)
            bg_ref[rs, :] = _gate_math(bd_ref[rs, :], al_ref[...], dt_ref[...])

    one = pl.BlockSpec((1, 128), lambda i: (0, 0))
    return pl.pallas_call(
        body, grid=(T // tm,),
        in_specs=[pl.BlockSpec((tm, wq), lambda i: (i, 0)),
                  pl.BlockSpec((8, wq), lambda i: (jnp.maximum(i * (tm // 8) - 1, 0), 0)),
                  pl.BlockSpec((tm, 128), lambda i: (i, BD_BLK)),
                  pl.BlockSpec((8, wq), lambda i: (0, 0)), one, one],
        out_specs=[pl.BlockSpec((tm, wq), lambda i: (i, 0)), pl.BlockSpec((tm, wq), lambda i: (i, 0)),
                   pl.BlockSpec((tm, 128), lambda i: (i, 0))],
        out_shape=[jax.ShapeDtypeStruct((T, wq), F32), jax.ShapeDtypeStruct((T, wq), F32),
                   jax.ShapeDtypeStruct((T, 128), F32)],
        name=name, compiler_params=_cp())(z_dn, z_dn, z_dn, conv_w8, alog_row, dtb_row)


def _dn_pre_bwd(conv, z_dn, alog_row, dtb_row, dqn, dkn, dvn, dbg, name):
    tm = 256
    wq = 3 * DNW

    def body(conv_ref, bd_ref, al_ref, dt_ref, dq_ref, dk_ref, dv_ref, dbg_ref,
             dconv_ref, dbd_ref, dal_ref, ddt_ref):
        i = pl.program_id(0)
        for blk in range(3 * DN_H):
            sl = slice(128 * blk, 128 * blk + 128)
            src = (dq_ref, dk_ref, dv_ref)[blk // DN_H]
            ct = src[:, 128 * (blk % DN_H):128 * (blk % DN_H) + 128]
            fn = _qk_math if blk < 2 * DN_H else _silu
            _, vjp = jax.vjp(fn, conv_ref[:, sl])
            dconv_ref[:, sl] = vjp(ct)[0]
        dal = jnp.zeros((1, 128), F32)
        ddt = jnp.zeros((1, 128), F32)
        for r0 in range(0, tm, PAIR):
            rs = slice(r0, r0 + PAIR)
            _, vjp = jax.vjp(_gate_math, bd_ref[rs, :], al_ref[...], dt_ref[...])
            dbd, dal_p, ddt_p = vjp(dbg_ref[rs, :])
            dbd_ref[rs, :] = dbd
            dal, ddt = dal + dal_p, ddt + ddt_p

        @pl.when(i == 0)
        def _():
            dal_ref[...] = dal
            ddt_ref[...] = ddt

        @pl.when(i > 0)
        def _():
            dal_ref[...] += dal
            ddt_ref[...] += ddt

    one = pl.BlockSpec((1, 128), lambda i: (0, 0))
    row = pl.BlockSpec((tm, wq), lambda i: (i, 0))
    hd = pl.BlockSpec((tm, DNW), lambda i: (i, 0))
    st = pl.BlockSpec((tm, 128), lambda i: (i, 0))
    return pl.pallas_call(
        body, grid=(T // tm,),
        in_specs=[row, pl.BlockSpec((tm, 128), lambda i: (i, BD_BLK)), one, one, hd, hd, hd, st],
        out_specs=[row, st, one, one],
        out_shape=[jax.ShapeDtypeStruct((T, wq), F32), jax.ShapeDtypeStruct((T, 128), F32),
                   jax.ShapeDtypeStruct((1, 128), F32), jax.ShapeDtypeStruct((1, 128), F32)],
        name=name, compiler_params=_cp())(conv, z_dn, alog_row, dtb_row, dqn, dkn, dvn, dbg)


def _dn_conv_bwd(dconv, z_dn, conv_w8, name):
    tm = 256
    wq = 3 * DNW
    last = T // tm - 1

    def body(dc_ref, dcn_ref, raw_ref, halo_ref, w_ref, draw_ref, dw_ref):
        i = pl.program_id(0)
        dc = dc_ref[...]
        nxt = jnp.where(i < last, dcn_ref[...], 0.0)
        cur = raw_ref[...]
        halo = jnp.where(i > 0, halo_ref[...], 0.0)
        w = w_ref[...]
        draw = jnp.zeros((tm, wq), F32)
        rows = []
        for j in range(4):
            draw = draw + _shift_up(dc, nxt, 3 - j) * w[j:j + 1, :]
            rows.append(jnp.sum(dc * _shift_down(cur, halo, 3 - j), axis=0, keepdims=True))
        draw_ref[...] = draw
        part = jnp.concatenate(rows + [jnp.zeros((4, wq), F32)], axis=0)

        @pl.when(i == 0)
        def _():
            dw_ref[...] = part

        @pl.when(i > 0)
        def _():
            dw_ref[...] += part

    row = pl.BlockSpec((tm, wq), lambda i: (i, 0))
    return pl.pallas_call(
        body, grid=(T // tm,),
        in_specs=[row, pl.BlockSpec((8, wq), lambda i: (jnp.minimum((i + 1) * (tm // 8), T // 8 - 1), 0)),
                  row, pl.BlockSpec((8, wq), lambda i: (jnp.maximum(i * (tm // 8) - 1, 0), 0)),
                  pl.BlockSpec((8, wq), lambda i: (0, 0))],
        out_specs=[row, pl.BlockSpec((8, wq), lambda i: (0, 0))],
        out_shape=[jax.ShapeDtypeStruct((T, wq), F32), jax.ShapeDtypeStruct((8, wq), F32)],
        name=name, compiler_params=_cp())(dconv, dconv, z_dn, z_dn, conv_w8)


def _h3(a, b, dims=NN):
    return lax.dot_general(a, b, dims, precision=lax.Precision.HIGH, preferred_element_type=F32)


@jax.custom_vjp
def _inverse_given(a_mat, tinv):
    return tinv


def _inverse_given_fwd(a_mat, tinv):
    return tinv, tinv


def _inverse_given_bwd(tinv, g):
    return -_bdot_tn(tinv, _bdot_nt(g, tinv)), jnp.zeros_like(tinv)


_inverse_given.defvjp(_inverse_given_fwd, _inverse_given_bwd)


@jax.custom_vjp
def _h3_lo(a, b):
    return _h3(a, b)


def _h3_lo_fwd(a, b):
    return _h3(a, b), (a, b)


def _h3_lo_bwd(res, g):
    a, b = res
    gb = g.astype(BF)
    return _dg(gb, b.astype(BF), NT), _dg(a.astype(BF), gb, TN)


_h3_lo.defvjp(_h3_lo_fwd, _h3_lo_bwd)


def _prep_head(q, k, v, bgc, h):
    beta = _col(bgc, h)
    gc = jnp.broadcast_to(_col(bgc, DN_H + h), (PAIR, 128))
    glast = jnp.broadcast_to(_col(bgc, 2 * DN_H + h), (PAIR, 128))
    ri = _iota((PAIR, PAIR), 0)
    ci = _iota((PAIR, PAIR), 1)
    same = (ri // CH) == (ci // CH)
    causal = same & (ci <= ri)
    strict = same & (ci < ri)
    eye = ri == ci
    gc_cols = _hdot(jnp.ones((PAIR, PAIR), F32), jnp.where(eye, gc, 0.0))
    decay = jnp.exp(jnp.where(causal, gc - gc_cols, NEG))
    egc = jnp.exp(gc)
    kb = k * beta
    a_mat = jnp.where(strict, _bdot_nt(kb, k) * decay, 0.0)
    qs = q * (DN_HD ** -0.5)
    attn = jnp.where(causal, _bdot_nt(qs, k) * decay, 0.0)
    return a_mat, (v * beta, kb * egc, qs * egc, k * jnp.exp(glast - gc), attn, jnp.exp(glast))


def _prep_tail(tinv, ctx):
    vb, kbe, qg, kdec, attn, decb = ctx
    return _h3_lo(tinv, vb), _h3_lo(tinv, kbe), qg, kdec, attn, decb


def _inverses(a_mats):
    eye = jnp.where(_iota((PAIR, PAIR), 0) == _iota((PAIR, PAIR), 1), 1.0, 0.0)
    ps = [-a for a in a_mats]
    tinvs = [eye + p for p in ps]
    for _ in range(5):
        ps = [_h3(p, p) for p in ps]
        tinvs = [t + _h3(t, p) for t, p in zip(tinvs, ps)]
    return tinvs


def _dn_prep_fwd(qkv, bg, name):
    rows = 1024
    hd = lambda off: pl.BlockSpec((rows, 128), lambda g, h: (g, off + h))
    out = pl.BlockSpec((rows, 128), lambda g, h: (g, h))

    def body(q_ref, k_ref, v_ref, bg_ref, *outs):
        h = pl.program_id(1)
        spans = [slice(PAIR * pr, PAIR * pr + PAIR) for pr in range(rows // PAIR)]
        heads = [_prep_head(q_ref[rs, :], k_ref[rs, :], v_ref[rs, :], bg_ref[rs, :], h) for rs in spans]
        tinvs = _inverses([a for a, _ in heads])
        for rs, tinv, (_, ctx) in zip(spans, tinvs, heads):
            for o_ref, val in zip(outs, _prep_tail(tinv, ctx) + (tinv,)):
                o_ref[rs, :] = val

    return pl.pallas_call(
        body, grid=(T // rows, DN_H),
        in_specs=[hd(0), hd(DN_H), hd(2 * DN_H), pl.BlockSpec((rows, 128), lambda g, h: (g, 0))],
        out_specs=[out] * 7, out_shape=[jax.ShapeDtypeStruct((T, DNW), F32)] * 7,
        name=name, compiler_params=_cp())(qkv, qkv, qkv, bg)


def _dn_prep_bwd(qkv, bg, tinv, cts, name):
    rows = 1024
    hd = lambda off: pl.BlockSpec((rows, 128), lambda g, h: (g, off + h))
    out = pl.BlockSpec((rows, 128), lambda g, h: (g, h))
    st = pl.BlockSpec((rows, 128), lambda g, h: (g, 0))

    def body(q_ref, k_ref, v_ref, bg_ref, ti_ref, c0, c1, c2, c3, c4, c5, dq_ref, dk_ref, dv_ref, dbg_ref):
        h = pl.program_id(1)
        spans = [slice(PAIR * pr, PAIR * pr + PAIR) for pr in range(rows // PAIR)]
        tis = [ti_ref[rs, :] for rs in spans]

        def joint(qs, ks, vs, bs):
            heads = [_prep_head(q, k, v, b, h) for q, k, v, b in zip(qs, ks, vs, bs)]
            return [_prep_tail(_inverse_given(a, ti), ctx) for (a, ctx), ti in zip(heads, tis)]

        _, vjp = jax.vjp(joint, *[[r[rs, :] for rs in spans] for r in (q_ref, k_ref, v_ref, bg_ref)])
        dqs, dks, dvs, dbs = vjp([tuple(c[rs, :] for c in (c0, c1, c2, c3, c4, c5)) for rs in spans])
        for rs, dq, dk, dv in zip(spans, dqs, dks, dvs):
            dq_ref[rs, :] = dq
            dk_ref[rs, :] = dk
            dv_ref[rs, :] = dv
        dbg_all = jnp.concatenate(dbs, axis=0)

        @pl.when(h == 0)
        def _():
            dbg_ref[...] = dbg_all

        @pl.when(h > 0)
        def _():
            dbg_ref[...] += dbg_all

    return pl.pallas_call(
        body, grid=(T // rows, DN_H),
        in_specs=[hd(0), hd(DN_H), hd(2 * DN_H), st] + [out] * 7,
        out_specs=[out, out, out, st],
        out_shape=[jax.ShapeDtypeStruct((T, DNW), F32)] * 3 + [jax.ShapeDtypeStruct((T, 128), F32)],
        name=name, compiler_params=_cp())(qkv, qkv, qkv, bg, tinv, *cts)


def _step_math(ss, us, ws, qgs, kdecs, attns, decbs, sub):
    z = jnp.zeros((CH, 128), F32)
    vnews = [u - _bdot_nn(w, s) for u, w, s in zip(us, ws, ss)]
    vfulls = [jnp.concatenate([v, z] if sub == 0 else [z, v], axis=0) for v in vnews]
    os = [_bdot_nn(qg, s) + _bdot_nn(attn, vf) for qg, s, attn, vf in zip(qgs, ss, attns, vfulls)]
    decs = [jnp.sum(decb, axis=0, keepdims=True) * (1.0 / CH) for decb in decbs]
    return [s * dec + _bdot_tn(kdec, v) for s, dec, kdec, v in zip(ss, decs, kdecs, vnews)], os


SCAN_ROWS = 256


def _dn_scan_fwd(prep, name):
    nstep = T // SCAN_ROWS
    nch = SCAN_ROWS // CH
    row = pl.BlockSpec((SCAN_ROWS, DNW), lambda p: (p, 0))

    def body(u_ref, w_ref, qg_ref, kd_ref, at_ref, db_ref, o_ref, ss_ref, s_ref):
        @pl.when(pl.program_id(0) == 0)
        def _():
            s_ref[...] = jnp.zeros_like(s_ref)

        lanes = [slice(128 * h, 128 * h + 128) for h in range(DN_H)]
        states = [s_ref[h] for h in range(DN_H)]
        for ch in range(nch):
            rs = slice(CH * ch, CH * ch + CH)
            for h in range(DN_H):
                ss_ref[ch, h] = states[h]
            states, os = _step_math(states, *[[r[rs, ls] for ls in lanes]
                                              for r in (u_ref, w_ref, qg_ref, kd_ref, at_ref, db_ref)], ch % 2)
            for ls, o in zip(lanes, os):
                o_ref[rs, ls] = o
        for h in range(DN_H):
            s_ref[h] = states[h]

    return pl.pallas_call(
        body, grid=(nstep,), in_specs=[row] * 6,
        out_specs=[row, pl.BlockSpec((nch, DN_H, 128, 128), lambda p: (p, 0, 0, 0))],
        out_shape=[jax.ShapeDtypeStruct((T, DNW), F32), jax.ShapeDtypeStruct((T // CH, DN_H, 128, 128), F32)],
        scratch_shapes=[pltpu.VMEM((DN_H, 128, 128), F32)],
        name=name, compiler_params=_cp())(*prep)


def _dn_scan_bwd(prep, states, do, name):
    nstep = T // SCAN_ROWS
    nch = SCAN_ROWS // CH
    row = pl.BlockSpec((SCAN_ROWS, DNW), lambda p: (nstep - 1 - p, 0))

    def body(u_ref, w_ref, qg_ref, kd_ref, at_ref, db_ref, ss_ref, do_ref, *rest):
        outs, ds_ref = rest[:6], rest[6]

        @pl.when(pl.program_id(0) == 0)
        def _():
            ds_ref[...] = jnp.zeros_like(ds_ref)

        lanes = [slice(128 * h, 128 * h + 128) for h in range(DN_H)]
        dss = [ds_ref[h] for h in range(DN_H)]
        for ch in reversed(range(nch)):
            rs = slice(CH * ch, CH * ch + CH)
            args = [[ss_ref[ch, h] for h in range(DN_H)]] + [
                [r[rs, ls] for ls in lanes] for r in (u_ref, w_ref, qg_ref, kd_ref, at_ref, db_ref)]
            _, vjp = jax.vjp(functools.partial(_step_math, sub=ch % 2), *args)
            cts = vjp((dss, [do_ref[rs, ls] for ls in lanes]))
            dss = cts[0]
            for o_ref, vals in zip(outs, cts[1:]):
                for ls, val in zip(lanes, vals):
                    o_ref[rs, ls] = val
        for h in range(DN_H):
            ds_ref[h] = dss[h]

    return pl.pallas_call(
        body, grid=(nstep,),
        in_specs=[row] * 6 + [pl.BlockSpec((nch, DN_H, 128, 128), lambda p: (nstep - 1 - p, 0, 0, 0)), row],
        out_specs=[row] * 6, out_shape=[jax.ShapeDtypeStruct((T, DNW), F32)] * 6,
        scratch_shapes=[pltpu.VMEM((DN_H, 128, 128), F32)],
        name=name, compiler_params=_cp())(*prep, states, do)


def _post_math(o, gate, wrow):
    return o * lax.rsqrt(jnp.mean(o * o, axis=-1, keepdims=True) + NORM_EPS) * wrow * _silu(gate)


def _dn_post_fwd(o, z_dn, dn_norm, name):
    tm = 512
    row = pl.BlockSpec((tm, DNW), lambda i: (i, 0))

    def body(o_ref, g_ref, w_ref, y_ref):
        for h in range(DN_H):
            ls = slice(128 * h, 128 * h + 128)
            y_ref[:, ls] = _post_math(o_ref[:, ls], g_ref[:, ls], w_ref[...]).astype(BF)

    return pl.pallas_call(
        body, grid=(T // tm,),
        in_specs=[row, pl.BlockSpec((tm, DNW), lambda i: (i, 3)), pl.BlockSpec((1, 128), lambda i: (0, 0))],
        out_specs=row, out_shape=jax.ShapeDtypeStruct((T, DNW), BF),
        name=name, compiler_params=_cp())(o, z_dn, dn_norm)


def _dn_post_bwd(o, z_dn, dn_norm, dy, name):
    tm = 512
    row = pl.BlockSpec((tm, DNW), lambda i: (i, 0))
    one = pl.BlockSpec((1, 128), lambda i: (0, 0))

    def body(o_ref, g_ref, w_ref, dy_ref, do_ref, dg_ref, dw_ref):
        i = pl.program_id(0)
        dw = jnp.zeros((1, 128), F32)
        for h in range(DN_H):
            ls = slice(128 * h, 128 * h + 128)
            _, vjp = jax.vjp(_post_math, o_ref[:, ls], g_ref[:, ls], w_ref[...])
            do, dg, dwh = vjp(dy_ref[:, ls].astype(F32))
            do_ref[:, ls] = do
            dg_ref[:, ls] = dg
            dw = dw + dwh

        @pl.when(i == 0)
        def _():
            dw_ref[...] = dw

        @pl.when(i > 0)
        def _():
            dw_ref[...] += dw

    return pl.pallas_call(
        body, grid=(T // tm,),
        in_specs=[row, pl.BlockSpec((tm, DNW), lambda i: (i, 3)), one, pl.BlockSpec((tm, DNW), lambda i: (i, 1))],
        out_specs=[row, row, one],
        out_shape=[jax.ShapeDtypeStruct((T, DNW), F32), jax.ShapeDtypeStruct((T, DNW), F32),
                   jax.ShapeDtypeStruct((1, 128), F32)],
        name=name, compiler_params=_cp())(o, z_dn, dn_norm, dy)


def _dz_dn_assemble(draw, dgate, dbd, name):
    tm = 512

    def body(a_ref, b_ref, c_ref, o_ref):
        o_ref[:, :3 * DNW] = a_ref[...].astype(BF)
        o_ref[:, 3 * DNW:4 * DNW] = b_ref[...].astype(BF)
        o_ref[:, 4 * DNW:4 * DNW + 128] = c_ref[...].astype(BF)
        o_ref[:, 4 * DNW + 128:] = jnp.zeros((tm, 128), BF)

    return pl.pallas_call(
        body, grid=(T // tm,),
        in_specs=[pl.BlockSpec((tm, 3 * DNW), lambda i: (i, 0)), pl.BlockSpec((tm, DNW), lambda i: (i, 0)),
                  pl.BlockSpec((tm, 128), lambda i: (i, 0))],
        out_specs=pl.BlockSpec((tm, ZD), lambda i: (i, 0)),
        out_shape=jax.ShapeDtypeStruct((T, ZD), BF), name=name, compiler_params=_cp())(draw, dgate, dbd)


HBM = pl.BlockSpec(memory_space=pltpu.HBM)
SEM = pl.BlockSpec(memory_space=pltpu.SEMAPHORE)
EFFECT = pltpu.SideEffectType.DATAFLOW_SIDE_EFFECTING
N_PEER = N_DEV - 1


ALL_PEERS = (1, 2, 4, 3, 5, 6, 7)
FIRST_HOP = (1, 2, 4, 6)
FORWARDED = (2, 4, 6)


def _peers(x, y, c, ks=ALL_PEERS):
    return [(k, (x ^ (k >> 2), y ^ ((k >> 1) & 1), c ^ (k & 1))) for k in ks]


def _exchange_copy(ins, lands, ssems, rsems, scatter, t, k, pos, me):
    px, py, pc = pos
    src = ins[t].at[4 * px + 2 * py + pc] if scatter else ins[t]
    return pltpu.make_async_remote_copy(
        src_ref=src, dst_ref=lands[t].at[me], send_sem=ssems[t].at[k - 1], recv_sem=rsems[t].at[k - 1],
        device_id=pos, device_id_type=MESH_ID)


def _xstart(bufs, scatter, name, ks=ALL_PEERS):
    nt = len(bufs)
    lands = [lax.empty((N_DEV,) + tuple(b.shape[1:] if scatter else b.shape), b.dtype) for b in bufs]

    def body(*refs):
        ins, lnd = refs[:nt], refs[nt:2 * nt]
        ssems, rsems = refs[2 * nt:3 * nt], refs[3 * nt:4 * nt]
        token = refs[-1]
        x, y, c = lax.axis_index("x"), lax.axis_index("y"), lax.axis_index("c")
        me = 4 * x + 2 * y + c
        for t in range(nt):
            for k, pos in _peers(x, y, c, ks):
                _exchange_copy(ins, lnd, ssems, rsems, scatter, t, k, pos, me).start()
        token[...] = jnp.zeros_like(token)

    both = list(bufs) + lands
    res = pl.pallas_call(
        body, name=name,
        out_shape=[pltpu.SemaphoreType.DMA((N_PEER,))] * (2 * nt)
        + [pltpu.HBM(b.shape, b.dtype) for b in both] + [jax.ShapeDtypeStruct((8, 128), F32)],
        in_specs=[HBM] * (2 * nt),
        out_specs=[SEM] * (2 * nt) + [HBM] * (2 * nt) + [pl.BlockSpec(memory_space=pltpu.VMEM)],
        input_output_aliases={i: 2 * nt + i for i in range(2 * nt)},
        compiler_params=pltpu.CompilerParams(has_side_effects=EFFECT),
    )(*[pltpu.with_memory_space_constraint(b, pltpu.HBM) for b in both])
    return res[:nt], res[nt:2 * nt], res[2 * nt:3 * nt], res[3 * nt:4 * nt], res[-1][0, 0]


def _xwait(ssems, rsems, thrus, lands, scatter, after, name, ks=ALL_PEERS):
    nt = len(lands)

    def body(*refs):
        ins, lnd = refs[:nt], refs[nt:2 * nt]
        ss, rs = refs[2 * nt:3 * nt], refs[3 * nt:4 * nt]
        x, y, c = lax.axis_index("x"), lax.axis_index("y"), lax.axis_index("c")
        me = 4 * x + 2 * y + c
        for t in range(nt):
            for k, pos in _peers(x, y, c, ks):
                cp = _exchange_copy(ins, lnd, ss, rs, scatter, t, k, pos, me)
                cp.wait_send()
                cp.wait_recv()

    both = list(thrus) + list(lands)
    res = pl.pallas_call(
        body, name=name, out_shape=[pltpu.HBM(b.shape, b.dtype) for b in both],
        in_specs=[HBM] * (2 * nt) + [SEM] * (2 * nt) + [ANY], out_specs=[HBM] * (2 * nt),
        input_output_aliases={i: i for i in range(2 * nt)},
        compiler_params=pltpu.CompilerParams(has_side_effects=EFFECT),
    )(*both, *ssems, *rsems, after)
    return res[:nt], res[nt:]


def _forward_copy(lands, ssems, rsems, t, k, pos, sibling):
    px, py, pc = pos
    slot = lands[t].at[4 * px + 2 * py + pc]
    return pltpu.make_async_remote_copy(
        src_ref=slot, dst_ref=slot, send_sem=ssems[t].at[k - 1], recv_sem=rsems[t].at[k - 1],
        device_id=sibling, device_id_type=MESH_ID)


def _fstart(lands, name):
    nt = len(lands)

    def body(*refs):
        lnd = refs[:nt]
        ssems, rsems = refs[nt:2 * nt], refs[2 * nt:3 * nt]
        token = refs[-1]
        x, y, c = lax.axis_index("x"), lax.axis_index("y"), lax.axis_index("c")
        for t in range(nt):
            for k, pos in _peers(x, y, c, FORWARDED):
                _forward_copy(lnd, ssems, rsems, t, k, pos, (x, y, c ^ 1)).start()
        token[...] = jnp.zeros_like(token)

    res = pl.pallas_call(
        body, name=name,
        out_shape=[pltpu.SemaphoreType.DMA((N_PEER,))] * (2 * nt)
        + [pltpu.HBM(b.shape, b.dtype) for b in lands] + [jax.ShapeDtypeStruct((8, 128), F32)],
        in_specs=[HBM] * nt,
        out_specs=[SEM] * (2 * nt) + [HBM] * nt + [pl.BlockSpec(memory_space=pltpu.VMEM)],
        input_output_aliases={i: 2 * nt + i for i in range(nt)},
        compiler_params=pltpu.CompilerParams(has_side_effects=EFFECT),
    )(*[pltpu.with_memory_space_constraint(b, pltpu.HBM) for b in lands])
    return res[:nt], res[nt:2 * nt], res[2 * nt:3 * nt], res[-1][0, 0]


def _fwait(ssems, rsems, lands, after, name):
    nt = len(lands)

    def body(*refs):
        lnd = refs[:nt]
        ss, rs = refs[nt:2 * nt], refs[2 * nt:3 * nt]
        x, y, c = lax.axis_index("x"), lax.axis_index("y"), lax.axis_index("c")
        for t in range(nt):
            for k, pos in _peers(x, y, c, FORWARDED):
                cp = _forward_copy(lnd, ss, rs, t, k, pos, (x, y, c ^ 1))
                cp.wait_send()
                cp.wait_recv()

    return pl.pallas_call(
        body, name=name, out_shape=[pltpu.HBM(b.shape, b.dtype) for b in lands],
        in_specs=[HBM] * nt + [SEM] * (2 * nt) + [ANY], out_specs=[HBM] * nt,
        input_output_aliases={i: i for i in range(nt)},
        compiler_params=pltpu.CompilerParams(has_side_effects=EFFECT),
    )(*lands, *ssems, *rsems, after)


def _adam(recv, w, m, v, tr, name):
    _, r, c = w.shape
    n_part = recv.shape[0]
    c1 = np.float32(1.0 - ADAM_B1 ** ADAM_STEP)
    c2 = np.float32(1.0 - ADAM_B2 ** ADAM_STEP)

    def body(r_ref, w_ref, m_ref, v_ref, g_ref, d_ref, mo_ref, vo_ref):
        g = r_ref[0].astype(F32)
        for s in range(1, n_part):
            g = g + r_ref[s].astype(F32)
        mn = ADAM_B1 * m_ref[0] + (1.0 - ADAM_B1) * g
        vn = ADAM_B2 * v_ref[0] + (1.0 - ADAM_B2) * (g * g)
        g_ref[0] = g
        mo_ref[0] = mn
        vo_ref[0] = vn
        d_ref[0] = -ADAM_LR * ((mn / c1) / (jnp.sqrt(vn / c2) + ADAM_EPS) + ADAM_WD * w_ref[0])

    one = pl.BlockSpec((1, tr, c), lambda i: (0, i, 0))
    return pl.pallas_call(
        body, grid=(r // tr,), in_specs=[pl.BlockSpec((n_part, tr, c), lambda i: (0, i, 0)), one, one, one],
        out_specs=[one] * 4, out_shape=[jax.ShapeDtypeStruct((1, r, c), F32)] * 4,
        name=name, compiler_params=_cp())(recv, w, m, v)


def _local_step(x, target, sp, need, ahead, emit):
    g = {}
    x0, h1 = x, _rms_fwd(x, sp["norm_ffn1"], "ffn1_norm")
    act1, saved1 = _ffn_up(h1, lambda kind, a: need("w" + kind + "1", a),
                           lambda kind, a: ahead("w" + kind + "1", a), "ffn1")
    x1, h2 = _mm_nn_resnorm([(act1, saved1[-1])], x0, 0.5, sp["norm_mix"], 512, "ffn1_down_norm",
                            after=ahead("win_a", act1))
    win_a, win_d = need("win_a", h2), need("win_d", h2)
    conv_w8, wout = need("conv_w8", h2), need("wout", h2)
    z_at = _mm_nn(h2, win_a, F32, 1024, 768, "mix_in_attn")
    z_dn = _mm_nn(h2, win_d, F32, 1024, 768, "mix_in_dn")

    conv, qkvn, bg = _dn_pre_fwd(z_dn, conv_w8, sp["alog_row"], sp["dtb_row"], "dn_pre")
    *prep, tinv = _dn_prep_fwd(qkvn, bg, "dn_prep")
    o_dn, states = _dn_scan_fwd(prep, "dn_scan")
    dn_b = _dn_post_fwd(o_dn, z_dn, sp["dn_norm"], "dn_post")
    attn_f, attn_b, lse = _attn_fwd(z_at, "attn_fwd", after=ahead("wg2", dn_b))

    x2, h3 = _mm_nn_resnorm([(attn_b, wout[:AW]), (dn_b, wout[AW:])], x1, 1.0, sp["norm_ffn2"], 512,
                            "mix_out_norm")
    act2, saved2 = _ffn_up(h3, lambda kind, a: need("w" + kind + "2", a),
                           lambda kind, a: ahead("w" + kind + "2", a), "ffn2")

    dx3, dys3, g["norm_final"], loss8 = _down_loss_bwd(x2, act2, saved2[-1], sp["norm_final"], target,
                                                       "ffn2_down_loss")
    dx2, dx2b, g["norm_ffn2"] = _ffn_bwd(
        x2, sp["norm_ffn2"], saved2, dx3, dys3, 1.0,
        lambda kind, dw: emit(kind + "2", {"w" + kind + "2": dw}), "ffn2b")

    zero = emit("wout", {"wout": jnp.concatenate([_mm_tn(attn_b, dx2b, BF, 512, 1024, "mix_out_dw_a"),
                                                  _mm_tn(dn_b, dx2b, BF, 512, 1024, "mix_out_dw_d")], axis=0)})
    dmix = _mm_nt(dx2b, wout, F32, 1024, 1024, "mix_out_dx", after=zero)

    dz_at = jnp.concatenate(_attn_bwd(z_at, dmix, attn_f, lse, "attn_bwd"), axis=1)

    do_dn, dgate, g["dn_norm"] = _dn_post_bwd(o_dn, z_dn, sp["dn_norm"], dmix, "dn_post_b")
    cts = _dn_scan_bwd(prep, states, do_dn, "dn_scan_b")
    dqn, dkn, dvn, dbg = _dn_prep_bwd(qkvn, bg, tinv, cts, "dn_prep_b")
    dconv, dbd, g["alog_row"], g["dtb_row"] = _dn_pre_bwd(
        conv, z_dn, sp["alog_row"], sp["dtb_row"], dqn, dkn, dvn, dbg, "dn_pre_b")
    draw, dconv_w8 = _dn_conv_bwd(dconv, z_dn, conv_w8, "dn_conv_b")
    dz_dn = _dz_dn_assemble(draw, dgate, dbd, "dn_dz")

    zero = emit("win", {"win_a": _mm_tn(h2, dz_at, BF, 512, 768, "mix_in_dw_a"),
                        "win_d": _mm_tn(h2, dz_dn, BF, 512, 768, "mix_in_dw_d"), "conv_w8": dconv_w8})
    dx1, dys1, g["norm_mix"] = _mm_rms_bwd([(dz_at, win_a), (dz_dn, win_d)], NT, x1, sp["norm_mix"], dx2, 0.5,
                                           512, "mix_in_dx_dnorm", after=zero)
    dx0, _, g["norm_ffn1"] = _ffn_bwd(
        x0, sp["norm_ffn1"], saved1, dx1, dys1, 1.0,
        lambda kind, dw: emit(kind + "1", {"w" + kind + "1": dw}), "ffn1b")
    return loss8[0, 0], dx0, g


def _cols_from_shards(gathered):
    n, r, c = gathered.shape
    return jnp.transpose(gathered, (1, 0, 2)).reshape(r, n * c)


def _shards_from_cols(full, dtype):
    r, nc = full.shape
    return jnp.transpose(full.reshape(r, N_DEV, nc // N_DEV), (1, 0, 2)).astype(dtype)


def _lane_row(vec4):
    return jnp.zeros((1, 128), F32).at[:, DN_H:2 * DN_H].set(vec4.astype(F32))


WEIGHT_SOURCES = {"wg1": "gate1", "wu1": "up1", "wd1": "down1", "win_a": "w_in", "win_d": "w_in",
                  "conv_w8": "conv_w", "wout": "w_out", "wg2": "gate2", "wu2": "up2", "wd2": "down2"}
TRANSPOSED = ("gate1", "up1", "gate2", "up2")


def _build_weights(name, gath):
    if name in ("wg1", "wu1", "wd1", "wg2", "wu2", "wd2"):
        return {name: gath[WEIGHT_SOURCES[name]].reshape(F, D)}
    if name in ("win_a", "win_d"):
        w_in = _cols_from_shards(gath["w_in"])
        c0 = 3 * AW + 3 * DNW
        win_d = jnp.concatenate([w_in[:, ZA:c0], w_in[:, c0 + 2 * DN_H:], w_in[:, c0:c0 + 2 * DN_H],
                                 jnp.zeros((D, ZP - IN_COLS), w_in.dtype)], axis=1)
        return {"win_a": w_in[:, :ZA], "win_d": win_d}
    if name == "wout":
        return {name: gath["w_out"].reshape(D, D)}
    conv = _cols_from_shards(gath["conv_w"])
    return {"conv_w8": jnp.concatenate([conv, jnp.zeros((4, 3 * DNW), F32)], axis=0)}


def _small_params(norm_ffn1, norm_mix, norm_ffn2, norm_final, a_log, dt_bias, dn_norm):
    return {"norm_ffn1": norm_ffn1, "norm_mix": norm_mix, "norm_ffn2": norm_ffn2,
            "norm_final": norm_final.reshape(1, D), "alog_row": _lane_row(a_log), "dtb_row": _lane_row(dt_bias),
            "dn_norm": dn_norm}


def _grad_slabs(group, g):
    if group[0] in "gud":
        return {WEIGHT_SOURCES["w" + group]: g["w" + group].reshape(N_DEV, F // N_DEV, D)}
    if group == "wout":
        return {"w_out": g["wout"].reshape(N_DEV, D // N_DEV, D)}
    gp = jnp.concatenate([g["win_a"], g["win_d"]], axis=1)
    c0 = 3 * AW + 3 * DNW
    g_in = jnp.concatenate([gp[:, :c0], gp[:, c0 + DNW:c0 + DNW + 2 * DN_H], gp[:, c0:c0 + DNW]], axis=1)
    return {"w_in": _shards_from_cols(g_in, BF), "conv_w": _shards_from_cols(g["conv_w8"][:4], F32)}


SMALL_ROWS = 40


def _small_pack(norm_ffn1, norm_mix, norm_ffn2, norm_final, dn_norm, alog_row, dtb_row, loss=None):
    rows = [a.reshape(8, 128) for a in (norm_ffn1, norm_mix, norm_ffn2, norm_final)]
    loss_row = jnp.zeros((1, 128), F32) if loss is None else jnp.broadcast_to(loss.reshape(1, 1), (1, 128))
    rows += [dn_norm.reshape(1, 128), alog_row, dtb_row, loss_row, jnp.zeros((SMALL_ROWS - 36, 128), F32)]
    return jnp.concatenate(rows, axis=0)


def _small_unpack(pk):
    pk = pk[0]
    return (pk[0:8].reshape(1, D), pk[8:16].reshape(1, D), pk[16:24].reshape(1, D), pk[24:32].reshape(D),
            pk[32:33], pk[33:34, DN_H:2 * DN_H], pk[34:35, DN_H:2 * DN_H])


ADAM_TILE = {"gate1": 256, "up1": 256, "down1": 176, "gate2": 256, "up2": 256, "down2": 176,
             "w_in": 256, "w_out": 128, "conv_w": 4}
BIG = ("gate1", "up1", "down1", "w_in", "w_out", "gate2", "up2", "down2", "conv_w")


def kernel(x, norm_ffn1, ffn1_gate, ffn1_up, ffn1_down, norm_mix, w_in, conv_w, a_log, dt_bias, dn_norm, w_out, norm_ffn2, ffn2_gate, ffn2_up, ffn2_down, norm_final, loss_target, m_norm_ffn1, m_ffn1_gate, m_ffn1_up, m_ffn1_down, m_norm_mix, m_w_in, m_conv_w, m_a_log, m_dt_bias, m_dn_norm, m_w_out, m_norm_ffn2, m_ffn2_gate, m_ffn2_up, m_ffn2_down, m_norm_final, v_norm_ffn1, v_ffn1_gate, v_ffn1_up, v_ffn1_down, v_norm_mix, v_w_in, v_conv_w, v_a_log, v_dt_bias, v_dn_norm, v_w_out, v_norm_ffn2, v_ffn2_gate, v_ffn2_up, v_ffn2_down, v_norm_final):
    w = {"gate1": ffn1_gate, "up1": ffn1_up, "down1": ffn1_down, "w_in": w_in, "w_out": w_out,
         "gate2": ffn2_gate, "up2": ffn2_up, "down2": ffn2_down, "conv_w": conv_w}
    m = {"gate1": m_ffn1_gate, "up1": m_ffn1_up, "down1": m_ffn1_down, "w_in": m_w_in, "w_out": m_w_out,
         "gate2": m_ffn2_gate, "up2": m_ffn2_up, "down2": m_ffn2_down, "conv_w": m_conv_w}
    v = {"gate1": v_ffn1_gate, "up1": v_ffn1_up, "down1": v_ffn1_down, "w_in": v_w_in, "w_out": v_w_out,
         "gate2": v_ffn2_gate, "up2": v_ffn2_up, "down2": v_ffn2_down, "conv_w": v_conv_w}

    me = 4 * lax.axis_index("x") + 2 * lax.axis_index("y") + lax.axis_index("c")
    own_slot = lambda land, mine: lax.dynamic_update_index_in_dim(land, mine, me, 0)

    ag_order = ("gate1", "up1", "down1", "w_in", "conv_w", "w_out", "gate2", "up2", "down2")
    ag_groups = (("gate1",), ("up1",), ("down1",), ("w_in", "conv_w", "w_out"), ("gate2", "up2", "down2"))
    pos = {n: i for i, n in enumerate(ag_order)}

    def shard(n):
        if n == "conv_w":
            return w[n][0]
        return (w[n][0].T if n in TRANSPOSED else w[n][0]).astype(BF)

    ss, rs, thru, land, zero = _xstart([shard(n) for n in ag_order], False, "weights_start", FIRST_HOP)
    gath, built, on_its_way = {}, {}, {}
    group_of = lambda name: [i for i, grp in enumerate(ag_groups) if WEIGHT_SOURCES[name] in grp][0]

    def ahead(name, after):
        gi = group_of(name)
        if WEIGHT_SOURCES[name] in gath or gi in on_its_way:
            return None
        ids = [pos[n] for n in ag_groups[gi]]
        thrus, lands = _xwait([ss[i] for i in ids], [rs[i] for i in ids], [thru[i] for i in ids],
                              [land[i] for i in ids], False, after, "weights_wait%d" % gi, FIRST_HOP)
        fss, frs, lands, token = _fstart(lands, "weights_forward%d" % gi)
        on_its_way[gi] = (thrus, fss, frs, lands)
        return token

    def need(name, after):
        if name not in built:
            if WEIGHT_SOURCES[name] not in gath:
                gi = group_of(name)
                ahead(name, after)
                thrus, fss, frs, lands = on_its_way.pop(gi)
                lands = _fwait(fss, frs, lands, after, "weights_forward_wait%d" % gi)
                for n, t, l in zip(ag_groups[gi], thrus, lands):
                    gath[n] = own_slot(l, t)
            built.update(_build_weights(name, gath))
        return built[name]

    pending = []

    def emit(group, grads):
        slabs = grads if group == "small" else _grad_slabs(group, grads)
        names = list(slabs)
        started = _xstart([slabs[n] for n in names], True, "grads_start_" + group)
        pending.append((group, names) + started[:4])
        return started[4]

    sp = _small_params(norm_ffn1 + zero, norm_mix, norm_ffn2, norm_final, a_log, dt_bias, dn_norm)
    loss_part, dx, g = _local_step(x[0], loss_target[0], sp, need, ahead, emit)
    small = _small_pack(g["norm_ffn1"], g["norm_mix"], g["norm_ffn2"], g["norm_final"], g["dn_norm"],
                        g["alog_row"], g["dtb_row"], loss_part)
    emit("small", {"small": jnp.broadcast_to(small[None], (N_DEV, SMALL_ROWS, 128))})

    pack = lambda a: _small_pack(*a)[None]
    res, after = {}, dx
    for group, names, gss, grs, gthru, gland in pending:
        thrus, lands = _xwait(gss, grs, gthru, gland, True, after, "grads_wait_" + group)
        for n, t, l in zip(names, thrus, lands):
            recv = own_slot(l, lax.dynamic_index_in_dim(t, me, 0, keepdims=False))
            if n == "small":
                res[n] = _adam(
                    recv,
                    pack((norm_ffn1, norm_mix, norm_ffn2, norm_final, dn_norm, _lane_row(a_log), _lane_row(dt_bias))),
                    pack((m_norm_ffn1, m_norm_mix, m_norm_ffn2, m_norm_final, m_dn_norm, _lane_row(m_a_log),
                          _lane_row(m_dt_bias))),
                    pack((v_norm_ffn1, v_norm_mix, v_norm_ffn2, v_norm_final, v_dn_norm, _lane_row(v_a_log),
                          _lane_row(v_dt_bias))),
                    SMALL_ROWS, "adam_small")
            elif n in TRANSPOSED:
                flip = lambda a: jnp.swapaxes(a, 1, 2)
                res[n] = [flip(o) for o in _adam(recv, flip(w[n]), flip(m[n]), flip(v[n]), F // N_DEV // 2,
                                                 "adam_" + n)]
            else:
                res[n] = _adam(recv, w[n], m[n], v[n], ADAM_TILE[n], "adam_" + n)
            after = res[n][0]
    res_s = res["small"]

    loss = res_s[0][0, 35, 0]
    outs = [loss, dx[None]]
    for k in range(4):
        n1, nm, n2, nf, dn, al, dt = _small_unpack(res_s[k])
        big = {n: res[n][k] for n in BIG}
        outs += [n1, big["gate1"], big["up1"], big["down1"], nm, big["w_in"], big["conv_w"], al, dt, dn,
                 big["w_out"], n2, big["gate2"], big["up2"], big["down2"], nf]
    return tuple(outs)
```

```python
import functools

import numpy as np
import jax
import jax.numpy as jnp
from jax import lax
from jax.experimental import pallas as pl
from jax.experimental.pallas import tpu as pltpu

T = 4096
D = 1024
F = 2816
N_DEV = 8
A_HEADS = 8
A_HD = 64
AW = A_HEADS * A_HD
DN_H = 4
DN_HD = 128
DNW = DN_H * DN_HD
CH = 64
PAIR = 2 * CH
ZA = 3 * AW
ZD = 3 * DNW + DNW + 256
ZP = ZA + ZD
BD_BLK = (3 * DNW + DNW) // 128
IN_COLS = 3592
PATTERNS = ((128, 1), (512, 4), (2048, 16))
NORM_EPS = 1e-6
L2_EPS = 1e-6
ADAM_LR, ADAM_B1, ADAM_B2, ADAM_EPS, ADAM_WD, ADAM_STEP = 0.001, 0.9, 0.999, 1e-08, 0.01, 10
VMEM_LIMIT = 56 * 1024 * 1024
NEG = -1e30

BF = jnp.bfloat16
F32 = jnp.float32
NN = (((1,), (0,)), ((), ()))
NT = (((1,), (1,)), ((), ()))
TN = (((0,), (0,)), ((), ()))
HI = lax.Precision.HIGHEST
MESH_ID = pl.DeviceIdType.MESH
ANY = pl.BlockSpec(memory_space=pl.ANY)


def _cp():
    return pltpu.CompilerParams(vmem_limit_bytes=VMEM_LIMIT)


def _dg(a, b, dims):
    return lax.dot_general(a, b, dims, preferred_element_type=F32)


def _hdot(a, b):
    return lax.dot_general(a, b, NN, precision=HI, preferred_element_type=F32)


def _make_bdot(dims, da_dims, da_swap, db_dims, db_swap):
    @jax.custom_vjp
    def f(a, b):
        return _dg(a.astype(BF), b.astype(BF), dims)

    def fwd(a, b):
        return f(a, b), (a, b)

    def bwd(res, g):
        a, b = res
        gb, ab, bb = g.astype(BF), a.astype(BF), b.astype(BF)
        da = _dg(bb, gb, da_dims) if da_swap else _dg(gb, bb, da_dims)
        db = _dg(gb, ab, db_dims) if db_swap else _dg(ab, gb, db_dims)
        return da.astype(a.dtype), db.astype(b.dtype)

    f.defvjp(fwd, bwd)
    return f


_bdot_nn = _make_bdot(NN, NT, False, TN, False)
_bdot_nt = _make_bdot(NT, NN, False, TN, True)
_bdot_tn = _make_bdot(TN, NT, True, NN, False)


def _iota(shape, dim):
    return lax.broadcasted_iota(jnp.int32, shape, dim)


def _col(x, idx):
    return jnp.sum(jnp.where(_iota(x.shape, 1) == idx, x, 0.0), axis=1, keepdims=True)


def _mm_nn(a, b, out_dtype, tm, tn, name):
    m, k = a.shape
    n = b.shape[1]

    def body(a_ref, b_ref, o_ref):
        o_ref[...] = _dg(a_ref[...], b_ref[...], NN).astype(out_dtype)

    return pl.pallas_call(
        body, grid=(m // tm, n // tn),
        in_specs=[pl.BlockSpec((tm, k), lambda i, j: (i, 0)), pl.BlockSpec((k, tn), lambda i, j: (0, j))],
        out_specs=pl.BlockSpec((tm, tn), lambda i, j: (i, j)),
        out_shape=jax.ShapeDtypeStruct((m, n), out_dtype), name=name, compiler_params=_cp())(a, b)


def _tie(body, after):
    if after is None:
        return body, [], []
    return (lambda tok_ref, *refs: body(*refs)), [ANY], [after.reshape(1, 1)]


def _mm_nt(a, b, out_dtype, tm, tb, name, after=None):
    m, c = a.shape
    kb = b.shape[0]

    def body(a_ref, b_ref, o_ref):
        o_ref[...] = _dg(a_ref[...], b_ref[...], NT).astype(out_dtype)

    body, tspec, tok = _tie(body, after)
    return pl.pallas_call(
        body, grid=(m // tm, kb // tb),
        in_specs=tspec + [pl.BlockSpec((tm, c), lambda i, j: (i, 0)), pl.BlockSpec((tb, c), lambda i, j: (j, 0))],
        out_specs=pl.BlockSpec((tm, tb), lambda i, j: (i, j)),
        out_shape=jax.ShapeDtypeStruct((m, kb), out_dtype), name=name, compiler_params=_cp())(*tok, a, b)


def _mm_tn(a, b, out_dtype, ta, tb, name, after=None):
    m, ka = a.shape
    nb = b.shape[1]

    def body(a_ref, b_ref, o_ref):
        o_ref[...] = _dg(a_ref[...], b_ref[...], TN).astype(out_dtype)

    body, tspec, tok = _tie(body, after)
    return pl.pallas_call(
        body, grid=(ka // ta, nb // tb),
        in_specs=tspec + [pl.BlockSpec((m, ta), lambda i, j: (0, i)), pl.BlockSpec((m, tb), lambda i, j: (0, j))],
        out_specs=pl.BlockSpec((ta, tb), lambda i, j: (i, j)),
        out_shape=jax.ShapeDtypeStruct((ka, nb), out_dtype), name=name, compiler_params=_cp())(*tok, a, b)


def _mm_nt_swiglu(h, wu_t, gate, tm, tb, name, after=None):
    m, c = h.shape
    kb = wu_t.shape[0]

    def body(h_ref, w_ref, g_ref, u_ref, a_ref):
        u = _dg(h_ref[...], w_ref[...], NT)
        g = g_ref[...].astype(F32)
        u_ref[...] = u.astype(BF)
        a_ref[...] = (g * jax.nn.sigmoid(g) * u).astype(BF)

    body, tspec, tok = _tie(body, after)
    tile = pl.BlockSpec((tm, tb), lambda i, j: (i, j))
    return pl.pallas_call(
        body, grid=(m // tm, kb // tb),
        in_specs=tspec + [pl.BlockSpec((tm, c), lambda i, j: (i, 0)), pl.BlockSpec((tb, c), lambda i, j: (j, 0)),
                          tile],
        out_specs=[tile, tile], out_shape=[jax.ShapeDtypeStruct((m, kb), BF)] * 2,
        name=name, compiler_params=_cp())(*tok, h, wu_t, gate)


def _mm_nt_dswiglu(dys, wd, gate, up, tm, tb, name, after=None):
    m, c = dys.shape
    kb = wd.shape[0]

    def body(d_ref, w_ref, g_ref, u_ref, dg_ref, du_ref):
        da = _dg(d_ref[...], w_ref[...], NT)
        g = g_ref[...].astype(F32)
        u = u_ref[...].astype(F32)
        s = jax.nn.sigmoid(g)
        dg_ref[...] = (da * u * (s * (1.0 + g * (1.0 - s)))).astype(BF)
        du_ref[...] = (da * (g * s)).astype(BF)

    body, tspec, tok = _tie(body, after)
    tile = pl.BlockSpec((tm, tb), lambda i, j: (i, j))
    return pl.pallas_call(
        body, grid=(m // tm, kb // tb),
        in_specs=tspec + [pl.BlockSpec((tm, c), lambda i, j: (i, 0)), pl.BlockSpec((tb, c), lambda i, j: (j, 0)),
                          tile, tile],
        out_specs=[tile, tile], out_shape=[jax.ShapeDtypeStruct((m, kb), BF)] * 2,
        name=name, compiler_params=_cp())(*tok, dys, wd, gate, up)


def _rms_fwd(x, gain, name):
    tm = 512
    row = pl.BlockSpec((tm, D), lambda i: (i, 0))

    def body(x_ref, g_ref, h_ref):
        xv = x_ref[...]
        r = lax.rsqrt(jnp.mean(xv * xv, axis=-1, keepdims=True) + NORM_EPS)
        h_ref[...] = (xv * r * g_ref[...]).astype(BF)

    return pl.pallas_call(
        body, grid=(T // tm,), in_specs=[row, pl.BlockSpec((1, D), lambda i: (0, 0))], out_specs=row,
        out_shape=jax.ShapeDtypeStruct((T, D), BF), name=name, compiler_params=_cp())(x, gain)


def _mm_rms_bwd(pairs, dims, x, gain, dres, alpha_out, tm, name, after=None):
    n = len(pairs)
    row = pl.BlockSpec((tm, D), lambda i: (i, 0))
    gspec = pl.BlockSpec((1, D), lambda i: (0, 0))

    def body(*refs):
        x_ref, g_ref, dres_ref, dx_ref, dxs_ref, dg_ref = refs[2 * n:]
        i = pl.program_id(0)
        dhv = _dg(refs[0][...], refs[n][...], dims)
        for p in range(1, n):
            dhv = dhv + _dg(refs[p][...], refs[n + p][...], dims)
        xv = x_ref[...]
        r = lax.rsqrt(jnp.mean(xv * xv, axis=-1, keepdims=True) + NORM_EPS)
        xh = xv * r
        part = jnp.sum(dhv * xh, axis=0, keepdims=True)

        @pl.when(i == 0)
        def _():
            dg_ref[...] = part

        @pl.when(i > 0)
        def _():
            dg_ref[...] += part

        dxh = dhv * g_ref[...]
        dx = r * (dxh - xh * jnp.mean(dxh * xh, axis=-1, keepdims=True)) + dres_ref[...]
        dx_ref[...] = dx
        dxs_ref[...] = (alpha_out * dx).astype(BF)

    body, tspec, tok = _tie(body, after)
    return pl.pallas_call(
        body, grid=(T // tm,),
        in_specs=tspec + [pl.BlockSpec((tm, a.shape[1]), lambda i: (i, 0)) for a, _ in pairs]
        + [pl.BlockSpec(b.shape, lambda i: (0, 0)) for _, b in pairs] + [row, gspec, row],
        out_specs=[row, row, gspec],
        out_shape=[jax.ShapeDtypeStruct((T, D), F32), jax.ShapeDtypeStruct((T, D), BF),
                   jax.ShapeDtypeStruct((1, D), F32)],
        name=name, compiler_params=_cp())(*tok, *[a for a, _ in pairs], *[b for _, b in pairs], x, gain, dres)


def _down_loss_bwd(x_prev, act, wd, gain, target, name):
    tm = 512
    row = pl.BlockSpec((tm, D), lambda i: (i, 0))
    gspec = pl.BlockSpec((1, D), lambda i: (0, 0))
    lspec = pl.BlockSpec((8, 128), lambda i: (0, 0))

    def body(x_ref, a_ref, w_ref, g_ref, t_ref, dx_ref, dxs_ref, dg_ref, loss_ref):
        i = pl.program_id(0)
        xv = x_ref[...] + 0.5 * _dg(a_ref[...], w_ref[...], NN)
        r = lax.rsqrt(jnp.mean(xv * xv, axis=-1, keepdims=True) + NORM_EPS)
        xh = xv * r
        diff = xh * g_ref[...] - t_ref[...]
        lpart = 0.5 * jnp.sum(jnp.mean(diff * diff, axis=-1, keepdims=True), axis=0, keepdims=True)
        dy = diff * (1.0 / D)
        part = jnp.sum(dy * xh, axis=0, keepdims=True)

        @pl.when(i == 0)
        def _():
            dg_ref[...] = part
            loss_ref[...] = jnp.broadcast_to(lpart, (8, 128))

        @pl.when(i > 0)
        def _():
            dg_ref[...] += part
            loss_ref[...] += jnp.broadcast_to(lpart, (8, 128))

        dxh = dy * g_ref[...]
        dx = r * (dxh - xh * jnp.mean(dxh * xh, axis=-1, keepdims=True))
        dx_ref[...] = dx
        dxs_ref[...] = (0.5 * dx).astype(BF)

    return pl.pallas_call(
        body, grid=(T // tm,),
        in_specs=[row, pl.BlockSpec((tm, F), lambda i: (i, 0)), pl.BlockSpec((F, D), lambda i: (0, 0)), gspec, row],
        out_specs=[row, row, gspec, lspec],
        out_shape=[jax.ShapeDtypeStruct((T, D), F32), jax.ShapeDtypeStruct((T, D), BF),
                   jax.ShapeDtypeStruct((1, D), F32), jax.ShapeDtypeStruct((8, 128), F32)],
        name=name, compiler_params=_cp())(x_prev, act, wd, gain, target)


def _mm_nn_resnorm(pairs, x_prev, alpha, gain, tm, name, after=None):
    m = x_prev.shape[0]
    n = len(pairs)

    def body(*refs):
        x_ref, g_ref, xo_ref, h_ref = refs[2 * n:]
        y = _dg(refs[0][...], refs[n][...], NN)
        for i in range(1, n):
            y = y + _dg(refs[i][...], refs[n + i][...], NN)
        xv = x_ref[...] + alpha * y
        xo_ref[...] = xv
        r = lax.rsqrt(jnp.mean(xv * xv, axis=-1, keepdims=True) + NORM_EPS)
        h_ref[...] = (xv * r * g_ref[...]).astype(BF)

    body, tspec, tok = _tie(body, after)
    row = pl.BlockSpec((tm, D), lambda i: (i, 0))
    return pl.pallas_call(
        body, grid=(m // tm,),
        in_specs=tspec + [pl.BlockSpec((tm, a.shape[1]), lambda i: (i, 0)) for a, _ in pairs]
        + [pl.BlockSpec(b.shape, lambda i: (0, 0)) for _, b in pairs] + [row, pl.BlockSpec((1, D), lambda i: (0, 0))],
        out_specs=[row, row],
        out_shape=[jax.ShapeDtypeStruct((m, D), F32), jax.ShapeDtypeStruct((m, D), BF)],
        name=name, compiler_params=_cp())(*tok, *[a for a, _ in pairs], *[b for _, b in pairs], x_prev, gain)


def _ffn_up(h, need, ahead, tag):
    wg = need("g", h)
    gate = _mm_nt(h, wg, BF, 1024, 1408, tag + "_gate")
    wu = need("u", gate)
    up, act = _mm_nt_swiglu(h, wu, gate, 1024, 1408, tag + "_up_act", after=ahead("d", wu))
    wd = need("d", up)
    return act, (h, gate, up, act, wg, wu, wd)


def _ffn_bwd(x_in, gain, saved, dxo, dys, alpha_out, emit, tag):
    h, gate, up, act, wg, wu, wd = saved
    sent = emit("d", _mm_tn(act, dys, BF, 256, 1024, tag + "_dwd"))
    dgate, dup = _mm_nt_dswiglu(dys, wd, gate, up, 1024, 1408, tag + "_dact", after=sent)
    sent = emit("g", _mm_tn(dgate, h, BF, 256, 1024, tag + "_dwg"))
    sent = emit("u", _mm_tn(dup, h, BF, 256, 1024, tag + "_dwu", after=sent))
    return _mm_rms_bwd([(dgate, wg), (dup, wu)], NN, x_in, gain, dxo, alpha_out, 512, tag + "_dh_dnorm",
                       after=sent)


SLAB = 2048
N_SLAB = T // SLAB
N_PAIR = A_HEADS // 2


def _pair_masks():
    lane = _iota((128, 128), 1)
    return lane < A_HD, lane >= A_HD


def _slope_table():
    h = 2 * jnp.arange(N_PAIR)[:, None] + jnp.minimum(jnp.arange(8), 1)[None, :]
    return jnp.broadcast_to((2.0 ** (-(h + 1).astype(F32)))[:, :, None], (N_PAIR, 8, 128))


def _rows(ref, start, d):
    if d == 1:
        return ref[pl.ds(start, 128), :]
    return ref[pl.ds(start, 128, stride=d), :]


def _put_rows(ref, start, d, val):
    if d == 1:
        ref[pl.ds(start, 128), :] = val
    else:
        ref[pl.ds(start, 128, stride=d), :] = val


def _units(d):
    return [(r, b, r + 128 * d * b) for r in range(d) for b in range(SLAB // (128 * d))]


def _biases(d, slopes, has_prev):
    qi = _iota((128, 256), 0)
    kj = _iota((128, 256), 1)
    steps = qi + 128 - kj
    in_band = (steps >= 0) & (steps <= 128)
    dist = (steps * d).astype(F32)
    base = [jnp.where(in_band, -(sl * dist), NEG) for sl in slopes]
    edge = [jnp.where(has_prev | (kj >= 128), b, NEG) for b in base]
    return base, edge


def _attn_fwd(z_at, name, after=None):
    def body(sl_ref, q_ref, kc_ref, kp_ref, vc_ref, vp_ref, of_ref, ob_ref, lse_ref, m_s, l_s, a_s):
        n = pl.program_id(1)
        lo, hi = _pair_masks()
        slopes = (sl_ref[0, 0:1, 0:1], sl_ref[0, 1:2, 0:1])

        def unit(d, start, b, first, carry, bias):
            q = _rows(q_ref, start, d).astype(BF)
            kcur, vcur = _rows(kc_ref, start, d).astype(BF), _rows(vc_ref, start, d).astype(BF)
            if b > 0:
                kprev, vprev = carry
            else:
                pstart = start + SLAB - 128 * d
                kprev, vprev = _rows(kp_ref, pstart, d).astype(BF), _rows(vp_ref, pstart, d).astype(BF)
            kcat = jnp.concatenate([kprev, kcur], axis=0)
            vcat = jnp.concatenate([vprev, vcur], axis=0)
            ms, ls, pvs = [], [], []
            for e in range(2):
                qm = jnp.where(lo if e == 0 else hi, q, jnp.zeros_like(q))
                s = _dg(qm, kcat, NT) * (A_HD ** -0.5) + bias[e]
                m = jnp.max(s, axis=1, keepdims=True)
                p = jnp.exp(s - m)
                ms.append(m)
                ls.append(jnp.sum(p, axis=1, keepdims=True))
                pvs.append(_dg(p.astype(BF), vcat, NN))
            m_u = jnp.where(lo, ms[0], ms[1])
            l_u = jnp.where(lo, ls[0], ls[1])
            a_u = jnp.where(lo, pvs[0], pvs[1])
            if first:
                m_n, l_n, a_n = m_u, l_u, a_u
            else:
                m_o = _rows(m_s, start, d)
                m_n = jnp.maximum(m_o, m_u)
                c_o = jnp.exp(m_o - m_n)
                c_u = jnp.exp(m_u - m_n)
                l_n = _rows(l_s, start, d) * c_o + l_u * c_u
                a_n = _rows(a_s, start, d) * c_o + a_u * c_u
            _put_rows(m_s, start, d, m_n)
            _put_rows(l_s, start, d, l_n)
            _put_rows(a_s, start, d, a_n)
            return kcur, vcur

        for pi, (_, d) in enumerate(PATTERNS):
            base, edge = _biases(d, slopes, n > 0)
            carry = None
            for r, b, start in _units(d):
                carry = unit(d, start, b, pi == 0, carry, edge if b == 0 else base)
        l = l_s[...]
        out = a_s[...] / l
        of_ref[...] = out
        ob_ref[...] = out.astype(BF)
        lse_ref[...] = m_s[...] + jnp.log(l)

    body, tspec, tok = _tie(body, after)
    cur = lambda c: pl.BlockSpec((SLAB, 128), lambda j, n: (n, c * N_PAIR + j))
    prv = lambda c: pl.BlockSpec((SLAB, 128), lambda j, n: (jnp.maximum(n - 1, 0), c * N_PAIR + j))
    out = pl.BlockSpec((SLAB, 128), lambda j, n: (n, j))
    return pl.pallas_call(
        body, grid=(N_PAIR, N_SLAB),
        in_specs=tspec + [pl.BlockSpec((1, 8, 128), lambda j, n: (j, 0, 0)), cur(0), cur(1), prv(1), cur(2), prv(2)],
        out_specs=[out, out, out],
        out_shape=[jax.ShapeDtypeStruct((T, AW), F32), jax.ShapeDtypeStruct((T, AW), BF),
                   jax.ShapeDtypeStruct((T, AW), F32)],
        scratch_shapes=[pltpu.VMEM((SLAB, 128), F32)] * 3,
        name=name, compiler_params=_cp())(*tok, _slope_table(), z_at, z_at, z_at, z_at, z_at)


def _attn_bwd(z_at, dout, out, lse, name):
    def body(sl_ref, q_ref, kc_ref, kp_ref, vc_ref, vp_ref, do_ref, o_ref, lse_ref, dq_ref, dk_ref, dv_ref,
             dq_s, dk_s, dv_s, ck_s, cv_s):
        step = pl.program_id(1)
        n = N_SLAB - 1 - step
        lo, hi = _pair_masks()
        slopes = (sl_ref[0, 0:1, 0:1], sl_ref[0, 1:2, 0:1])

        @pl.when(step == 0)
        def _():
            ck_s[...] = jnp.zeros_like(ck_s)
            cv_s[...] = jnp.zeros_like(cv_s)

        dk_s[...] = ck_s[...]
        dv_s[...] = cv_s[...]
        ck_s[...] = jnp.zeros_like(ck_s)
        cv_s[...] = jnp.zeros_like(cv_s)

        def add_rows(ref, start, d, val):
            _put_rows(ref, start, d, _rows(ref, start, d) + val)

        def unit(d, start, b, first, carry, bias):
            q = _rows(q_ref, start, d).astype(BF)
            do_f = _rows(do_ref, start, d)
            do = do_f.astype(BF)
            prod = do_f * _rows(o_ref, start, d)
            lse_u = _rows(lse_ref, start, d)
            kcur, vcur = _rows(kc_ref, start, d).astype(BF), _rows(vc_ref, start, d).astype(BF)
            if b > 0:
                kprev, vprev = carry
            else:
                pstart = start + SLAB - 128 * d
                kprev, vprev = _rows(kp_ref, pstart, d).astype(BF), _rows(vp_ref, pstart, d).astype(BF)
            kcat = jnp.concatenate([kprev, kcur], axis=0)
            vcat = jnp.concatenate([vprev, vcur], axis=0)
            masks = (lo, hi)
            qms = [jnp.where(msk, q, jnp.zeros_like(q)) for msk in masks]
            doms = [jnp.where(msk, do, jnp.zeros_like(do)) for msk in masks]
            deltas = [jnp.sum(jnp.where(msk, prod, 0.0), axis=1, keepdims=True) for msk in masks]
            ss = [_dg(qm, kcat, NT) * (A_HD ** -0.5) + bs for qm, bs in zip(qms, bias)]
            dps = [_dg(dom, vcat, NT) for dom in doms]
            ps = [jnp.exp(s - lse_u[:, 64 * e:64 * e + 1]) for e, s in enumerate(ss)]
            dss = [(p * (dp - delta)).astype(BF) for p, dp, delta in zip(ps, dps, deltas)]
            pbs = [p.astype(BF) for p in ps]
            dqs = [_dg(ds, kcat, NN) for ds in dss]
            dkc = (_dg(dss[0], qms[0], TN) + _dg(dss[1], qms[1], TN)) * (A_HD ** -0.5)
            dvc = _dg(pbs[0], doms[0], TN) + _dg(pbs[1], doms[1], TN)
            dq_u = jnp.where(lo, dqs[0], dqs[1]) * (A_HD ** -0.5)
            if first:
                _put_rows(dq_s, start, d, dq_u)
            else:
                add_rows(dq_s, start, d, dq_u)
            add_rows(dk_s, start, d, dkc[128:])
            add_rows(dv_s, start, d, dvc[128:])
            if b > 0:
                add_rows(dk_s, start - 128 * d, d, dkc[:128])
                add_rows(dv_s, start - 128 * d, d, dvc[:128])
            else:
                pstart = start + SLAB - 128 * d
                add_rows(ck_s, pstart, d, dkc[:128])
                add_rows(cv_s, pstart, d, dvc[:128])
            return kcur, vcur

        for pi, (_, d) in enumerate(PATTERNS):
            base, edge = _biases(d, slopes, n > 0)
            carry = None
            for r, b, start in _units(d):
                carry = unit(d, start, b, pi == 0, carry, edge if b == 0 else base)
        dq_ref[...] = dq_s[...].astype(BF)
        dk_ref[...] = dk_s[...].astype(BF)
        dv_ref[...] = dv_s[...].astype(BF)

    rev = lambda n: N_SLAB - 1 - n
    cur = lambda c: pl.BlockSpec((SLAB, 128), lambda j, n: (rev(n), c * N_PAIR + j))
    prv = lambda c: pl.BlockSpec((SLAB, 128), lambda j, n: (jnp.maximum(rev(n) - 1, 0), c * N_PAIR + j))
    one = pl.BlockSpec((SLAB, 128), lambda j, n: (rev(n), j))
    return pl.pallas_call(
        body, grid=(N_PAIR, N_SLAB),
        in_specs=[pl.BlockSpec((1, 8, 128), lambda j, n: (j, 0, 0)), cur(0), cur(1), prv(1), cur(2), prv(2),
                  one, one, one],
        out_specs=[one, one, one], out_shape=[jax.ShapeDtypeStruct((T, AW), BF)] * 3,
        scratch_shapes=[pltpu.VMEM((SLAB, 128), F32)] * 5,
        name=name, compiler_params=_cp())(_slope_table(), z_at, z_at, z_at, z_at, z_at, dout, out, lse)


def _silu(x):
    return x * jax.nn.sigmoid(x)


def _qk_math(c):
    s = _silu(c)
    return s * lax.rsqrt(jnp.sum(s * s, axis=-1, keepdims=True) + L2_EPS)


def _softplus(x):
    return jnp.maximum(x, 0.0) + jnp.log(1.0 + jnp.exp(-jnp.abs(x)))


def _gate_math(bd, alog_row, dtb_row):
    rows = bd.shape[0]
    lane = _iota(bd.shape, 1)
    beta = jax.nn.sigmoid(bd)
    g = jnp.where((lane >= DN_H) & (lane < 2 * DN_H), -jnp.exp(alog_row) * _softplus(bd + dtb_row), 0.0)
    ri = _iota((rows, rows), 0)
    ci = _iota((rows, rows), 1)
    same = (ri // CH) == (ci // CH)
    li = _iota((128, 128), 0)
    lj = _iota((128, 128), 1)
    to_next_group = jnp.where((lj == li + DN_H) & (li >= DN_H) & (li < 2 * DN_H), 1.0, 0.0)
    gc = _hdot(jnp.where(same & (ci <= ri), 1.0, 0.0), g)
    glast = _hdot(_hdot(jnp.where(same, 1.0, 0.0), g), to_next_group)
    return jnp.where(lane < DN_H, beta, 0.0) + gc + glast


def _shift_down(cur, halo, s):
    if s == 0:
        return cur
    rolled = pltpu.roll(cur, s, 0)
    hr = pltpu.roll(halo, s, 0)
    head = jnp.where(_iota(hr.shape, 0) < s, hr, rolled[:8])
    return jnp.concatenate([head, rolled[8:]], axis=0)


def _shift_up(cur, halo, s):
    if s == 0:
        return cur
    rows = cur.shape[0]
    rolled = pltpu.roll(cur, rows - s, 0)
    hr = pltpu.roll(halo, 8 - s, 0)
    tail = jnp.where(_iota(hr.shape, 0) >= 8 - s, hr, rolled[rows - 8:])
    return jnp.concatenate([rolled[:rows - 8], tail], axis=0)


def _dn_pre_fwd(z_dn, conv_w8, alog_row, dtb_row, name):
    tm = 256
    wq = 3 * DNW

    def body(raw_ref, halo_ref, bd_ref, w_ref, al_ref, dt_ref, conv_ref, qkv_ref, bg_ref):
        i = pl.program_id(0)
        cur = raw_ref[...]
        halo = jnp.where(i > 0, halo_ref[...], 0.0)
        w = w_ref[...]
        conv = jnp.zeros((tm, wq), F32)
        for j in range(4):
            conv = conv + _shift_down(cur, halo, 3 - j) * w[j:j + 1, :]
        conv_ref[...] = conv
        for blk in range(3 * DN_H):
            sl = slice(128 * blk, 128 * blk + 128)
            c = conv[:, sl]
            qkv_ref[:, sl] = _qk_math(c) if blk < 2 * DN_H else _silu(c)
        for r0 in range(0, tm, PAIR):
            rs = slice(r0, r0 + PAIR)
            bg_ref[rs, :] = _gate_math(bd_ref[rs, :], al_ref[...], dt_ref[...])

    one = pl.BlockSpec((1, 128), lambda i: (0, 0))
    return pl.pallas_call(
        body, grid=(T // tm,),
        in_specs=[pl.BlockSpec((tm, wq), lambda i: (i, 0)),
                  pl.BlockSpec((8, wq), lambda i: (jnp.maximum(i * (tm // 8) - 1, 0), 0)),
                  pl.BlockSpec((tm, 128), lambda i: (i, BD_BLK)),
                  pl.BlockSpec((8, wq), lambda i: (0, 0)), one, one],
        out_specs=[pl.BlockSpec((tm, wq), lambda i: (i, 0)), pl.BlockSpec((tm, wq), lambda i: (i, 0)),
                   pl.BlockSpec((tm, 128), lambda i: (i, 0))],
        out_shape=[jax.ShapeDtypeStruct((T, wq), F32), jax.ShapeDtypeStruct((T, wq), F32),
                   jax.ShapeDtypeStruct((T, 128), F32)],
        name=name, compiler_params=_cp())(z_dn, z_dn, z_dn, conv_w8, alog_row, dtb_row)


def _dn_pre_bwd(conv, z_dn, alog_row, dtb_row, dqn, dkn, dvn, dbg, name):
    tm = 256
    wq = 3 * DNW

    def body(conv_ref, bd_ref, al_ref, dt_ref, dq_ref, dk_ref, dv_ref, dbg_ref,
             dconv_ref, dbd_ref, dal_ref, ddt_ref):
        i = pl.program_id(0)
        for blk in range(3 * DN_H):
            sl = slice(128 * blk, 128 * blk + 128)
            src = (dq_ref, dk_ref, dv_ref)[blk // DN_H]
            ct = src[:, 128 * (blk % DN_H):128 * (blk % DN_H) + 128]
            fn = _qk_math if blk < 2 * DN_H else _silu
            _, vjp = jax.vjp(fn, conv_ref[:, sl])
            dconv_ref[:, sl] = vjp(ct)[0]
        dal = jnp.zeros((1, 128), F32)
        ddt = jnp.zeros((1, 128), F32)
        for r0 in range(0, tm, PAIR):
            rs = slice(r0, r0 + PAIR)
            _, vjp = jax.vjp(_gate_math, bd_ref[rs, :], al_ref[...], dt_ref[...])
            dbd, dal_p, ddt_p = vjp(dbg_ref[rs, :])
            dbd_ref[rs, :] = dbd
            dal, ddt = dal + dal_p, ddt + ddt_p

        @pl.when(i == 0)
        def _():
            dal_ref[...] = dal
            ddt_ref[...] = ddt

        @pl.when(i > 0)
        def _():
            dal_ref[...] += dal
            ddt_ref[...] += ddt

    one = pl.BlockSpec((1, 128), lambda i: (0, 0))
    row = pl.BlockSpec((tm, wq), lambda i: (i, 0))
    hd = pl.BlockSpec((tm, DNW), lambda i: (i, 0))
    st = pl.BlockSpec((tm, 128), lambda i: (i, 0))
    return pl.pallas_call(
        body, grid=(T // tm,),
        in_specs=[row, pl.BlockSpec((tm, 128), lambda i: (i, BD_BLK)), one, one, hd, hd, hd, st],
        out_specs=[row, st, one, one],
        out_shape=[jax.ShapeDtypeStruct((T, wq), F32), jax.ShapeDtypeStruct((T, 128), F32),
                   jax.ShapeDtypeStruct((1, 128), F32), jax.ShapeDtypeStruct((1, 128), F32)],
        name=name, compiler_params=_cp())(conv, z_dn, alog_row, dtb_row, dqn, dkn, dvn, dbg)


def _dn_conv_bwd(dconv, z_dn, conv_w8, name):
    tm = 256
    wq = 3 * DNW
    last = T // tm - 1

    def body(dc_ref, dcn_ref, raw_ref, halo_ref, w_ref, draw_ref, dw_ref):
        i = pl.program_id(0)
        dc = dc_ref[...]
        nxt = jnp.where(i < last, dcn_ref[...], 0.0)
        cur = raw_ref[...]
        halo = jnp.where(i > 0, halo_ref[...], 0.0)
        w = w_ref[...]
        draw = jnp.zeros((tm, wq), F32)
        rows = []
        for j in range(4):
            draw = draw + _shift_up(dc, nxt, 3 - j) * w[j:j + 1, :]
            rows.append(jnp.sum(dc * _shift_down(cur, halo, 3 - j), axis=0, keepdims=True))
        draw_ref[...] = draw
        part = jnp.concatenate(rows + [jnp.zeros((4, wq), F32)], axis=0)

        @pl.when(i == 0)
        def _():
            dw_ref[...] = part

        @pl.when(i > 0)
        def _():
            dw_ref[...] += part

    row = pl.BlockSpec((tm, wq), lambda i: (i, 0))
    return pl.pallas_call(
        body, grid=(T // tm,),
        in_specs=[row, pl.BlockSpec((8, wq), lambda i: (jnp.minimum((i + 1) * (tm // 8), T // 8 - 1), 0)),
                  row, pl.BlockSpec((8, wq), lambda i: (jnp.maximum(i * (tm // 8) - 1, 0), 0)),
                  pl.BlockSpec((8, wq), lambda i: (0, 0))],
        out_specs=[row, pl.BlockSpec((8, wq), lambda i: (0, 0))],
        out_shape=[jax.ShapeDtypeStruct((T, wq), F32), jax.ShapeDtypeStruct((8, wq), F32)],
        name=name, compiler_params=_cp())(dconv, dconv, z_dn, z_dn, conv_w8)


def _h3(a, b, dims=NN):
    return lax.dot_general(a, b, dims, precision=lax.Precision.HIGH, preferred_element_type=F32)


@jax.custom_vjp
def _inverse_given(a_mat, tinv):
    return tinv


def _inverse_given_fwd(a_mat, tinv):
    return tinv, tinv


def _inverse_given_bwd(tinv, g):
    return -_bdot_tn(tinv, _bdot_nt(g, tinv)), jnp.zeros_like(tinv)


_inverse_given.defvjp(_inverse_given_fwd, _inverse_given_bwd)


@jax.custom_vjp
def _h3_lo(a, b):
    return _h3(a, b)


def _h3_lo_fwd(a, b):
    return _h3(a, b), (a, b)


def _h3_lo_bwd(res, g):
    a, b = res
    gb = g.astype(BF)
    return _dg(gb, b.astype(BF), NT), _dg(a.astype(BF), gb, TN)


_h3_lo.defvjp(_h3_lo_fwd, _h3_lo_bwd)


def _prep_head(q, k, v, bgc, h):
    beta = _col(bgc, h)
    gc = jnp.broadcast_to(_col(bgc, DN_H + h), (PAIR, 128))
    glast = jnp.broadcast_to(_col(bgc, 2 * DN_H + h), (PAIR, 128))
    ri = _iota((PAIR, PAIR), 0)
    ci = _iota((PAIR, PAIR), 1)
    same = (ri // CH) == (ci // CH)
    causal = same & (ci <= ri)
    strict = same & (ci < ri)
    eye = ri == ci
    gc_cols = _hdot(jnp.ones((PAIR, PAIR), F32), jnp.where(eye, gc, 0.0))
    decay = jnp.exp(jnp.where(causal, gc - gc_cols, NEG))
    egc = jnp.exp(gc)
    kb = k * beta
    a_mat = jnp.where(strict, _bdot_nt(kb, k) * decay, 0.0)
    qs = q * (DN_HD ** -0.5)
    attn = jnp.where(causal, _bdot_nt(qs, k) * decay, 0.0)
    return a_mat, (v * beta, kb * egc, qs * egc, k * jnp.exp(glast - gc), attn, jnp.exp(glast))


PREP_DTYPES = (F32, BF, BF, BF, BF, F32)


def _prep_tail(tinv, ctx):
    vb, kbe, qg, kdec, attn, decb = ctx
    outs = (_h3_lo(tinv, vb), _h3_lo(tinv, kbe), qg, kdec, attn, decb)
    return tuple(o.astype(dt) for o, dt in zip(outs, PREP_DTYPES))


def _inverses(a_mats):
    eye = jnp.where(_iota((PAIR, PAIR), 0) == _iota((PAIR, PAIR), 1), 1.0, 0.0)
    ps = [-a for a in a_mats]
    tinvs = [eye + p for p in ps]
    for _ in range(5):
        ps = [_h3(p, p) for p in ps]
        tinvs = [t + _h3(t, p) for t, p in zip(tinvs, ps)]
    return tinvs


def _dn_prep_fwd(qkv, bg, name):
    rows = 1024
    hd = lambda off: pl.BlockSpec((rows, 128), lambda g, h: (g, off + h))
    out = pl.BlockSpec((rows, 128), lambda g, h: (g, h))

    def body(q_ref, k_ref, v_ref, bg_ref, *outs):
        h = pl.program_id(1)
        spans = [slice(PAIR * pr, PAIR * pr + PAIR) for pr in range(rows // PAIR)]
        heads = [_prep_head(q_ref[rs, :], k_ref[rs, :], v_ref[rs, :], bg_ref[rs, :], h) for rs in spans]
        tinvs = _inverses([a for a, _ in heads])
        for rs, tinv, (_, ctx) in zip(spans, tinvs, heads):
            for o_ref, val in zip(outs, _prep_tail(tinv, ctx) + (tinv,)):
                o_ref[rs, :] = val

    return pl.pallas_call(
        body, grid=(T // rows, DN_H),
        in_specs=[hd(0), hd(DN_H), hd(2 * DN_H), pl.BlockSpec((rows, 128), lambda g, h: (g, 0))],
        out_specs=[out] * 7, out_shape=[jax.ShapeDtypeStruct((T, DNW), dt) for dt in PREP_DTYPES + (F32,)],
        name=name, compiler_params=_cp())(qkv, qkv, qkv, bg)


def _dn_prep_bwd(qkv, bg, tinv, cts, name):
    rows = 1024
    hd = lambda off: pl.BlockSpec((rows, 128), lambda g, h: (g, off + h))
    out = pl.BlockSpec((rows, 128), lambda g, h: (g, h))
    st = pl.BlockSpec((rows, 128), lambda g, h: (g, 0))

    def body(q_ref, k_ref, v_ref, bg_ref, ti_ref, c0, c1, c2, c3, c4, c5, dq_ref, dk_ref, dv_ref, dbg_ref):
        h = pl.program_id(1)
        spans = [slice(PAIR * pr, PAIR * pr + PAIR) for pr in range(rows // PAIR)]
        tis = [ti_ref[rs, :] for rs in spans]

        def joint(qs, ks, vs, bs):
            heads = [_prep_head(q, k, v, b, h) for q, k, v, b in zip(qs, ks, vs, bs)]
            return [_prep_tail(_inverse_given(a, ti), ctx) for (a, ctx), ti in zip(heads, tis)]

        _, vjp = jax.vjp(joint, *[[r[rs, :] for rs in spans] for r in (q_ref, k_ref, v_ref, bg_ref)])
        dqs, dks, dvs, dbs = vjp([tuple(c[rs, :] for c in (c0, c1, c2, c3, c4, c5)) for rs in spans])
        for rs, dq, dk, dv in zip(spans, dqs, dks, dvs):
            dq_ref[rs, :] = dq
            dk_ref[rs, :] = dk
            dv_ref[rs, :] = dv
        dbg_all = jnp.concatenate(dbs, axis=0)

        @pl.when(h == 0)
        def _():
            dbg_ref[...] = dbg_all

        @pl.when(h > 0)
        def _():
            dbg_ref[...] += dbg_all

    return pl.pallas_call(
        body, grid=(T // rows, DN_H),
        in_specs=[hd(0), hd(DN_H), hd(2 * DN_H), st] + [out] * 7,
        out_specs=[out, out, out, st],
        out_shape=[jax.ShapeDtypeStruct((T, DNW), F32)] * 3 + [jax.ShapeDtypeStruct((T, 128), F32)],
        name=name, compiler_params=_cp())(qkv, qkv, qkv, bg, tinv, *cts)


def _step_math(ss, us, ws, qgs, kdecs, attns, decbs, sub):
    z = jnp.zeros((CH, 128), F32)
    vnews = [u - _bdot_nn(w, s) for u, w, s in zip(us, ws, ss)]
    vfulls = [jnp.concatenate([v, z] if sub == 0 else [z, v], axis=0) for v in vnews]
    os = [_bdot_nn(qg, s) + _bdot_nn(attn, vf) for qg, s, attn, vf in zip(qgs, ss, attns, vfulls)]
    decs = [jnp.sum(decb, axis=0, keepdims=True) * (1.0 / CH) for decb in decbs]
    return [s * dec + _bdot_tn(kdec, v) for s, dec, kdec, v in zip(ss, decs, kdecs, vnews)], os


SCAN_ROWS = 256


def _dn_scan_fwd(prep, name):
    nstep = T // SCAN_ROWS
    nch = SCAN_ROWS // CH
    row = pl.BlockSpec((SCAN_ROWS, DNW), lambda p: (p, 0))

    def body(u_ref, w_ref, qg_ref, kd_ref, at_ref, db_ref, o_ref, ss_ref, s_ref):
        @pl.when(pl.program_id(0) == 0)
        def _():
            s_ref[...] = jnp.zeros_like(s_ref)

        lanes = [slice(128 * h, 128 * h + 128) for h in range(DN_H)]
        states = [s_ref[h] for h in range(DN_H)]
        for ch in range(nch):
            rs = slice(CH * ch, CH * ch + CH)
            for h in range(DN_H):
                ss_ref[ch, h] = states[h]
            states, os = _step_math(states, *[[r[rs, ls] for ls in lanes]
                                              for r in (u_ref, w_ref, qg_ref, kd_ref, at_ref, db_ref)], ch % 2)
            for ls, o in zip(lanes, os):
                o_ref[rs, ls] = o
        for h in range(DN_H):
            s_ref[h] = states[h]

    return pl.pallas_call(
        body, grid=(nstep,), in_specs=[row] * 6,
        out_specs=[row, pl.BlockSpec((nch, DN_H, 128, 128), lambda p: (p, 0, 0, 0))],
        out_shape=[jax.ShapeDtypeStruct((T, DNW), F32), jax.ShapeDtypeStruct((T // CH, DN_H, 128, 128), F32)],
        scratch_shapes=[pltpu.VMEM((DN_H, 128, 128), F32)],
        name=name, compiler_params=_cp())(*prep)


def _dn_scan_bwd(prep, states, do, name):
    nstep = T // SCAN_ROWS
    nch = SCAN_ROWS // CH
    row = pl.BlockSpec((SCAN_ROWS, DNW), lambda p: (nstep - 1 - p, 0))

    def body(u_ref, w_ref, qg_ref, kd_ref, at_ref, db_ref, ss_ref, do_ref, *rest):
        outs, ds_ref = rest[:6], rest[6]

        @pl.when(pl.program_id(0) == 0)
        def _():
            ds_ref[...] = jnp.zeros_like(ds_ref)

        lanes = [slice(128 * h, 128 * h + 128) for h in range(DN_H)]
        dss = [ds_ref[h] for h in range(DN_H)]
        for ch in reversed(range(nch)):
            rs = slice(CH * ch, CH * ch + CH)
            args = [[ss_ref[ch, h] for h in range(DN_H)]] + [
                [r[rs, ls] for ls in lanes] for r in (u_ref, w_ref, qg_ref, kd_ref, at_ref, db_ref)]
            _, vjp = jax.vjp(functools.partial(_step_math, sub=ch % 2), *args)
            cts = vjp((dss, [do_ref[rs, ls] for ls in lanes]))
            dss = cts[0]
            for o_ref, vals in zip(outs, cts[1:]):
                for ls, val in zip(lanes, vals):
                    o_ref[rs, ls] = val
        for h in range(DN_H):
            ds_ref[h] = dss[h]

    return pl.pallas_call(
        body, grid=(nstep,),
        in_specs=[row] * 6 + [pl.BlockSpec((nch, DN_H, 128, 128), lambda p: (nstep - 1 - p, 0, 0, 0)), row],
        out_specs=[row] * 6, out_shape=[jax.ShapeDtypeStruct((T, DNW), dt) for dt in PREP_DTYPES],
        scratch_shapes=[pltpu.VMEM((DN_H, 128, 128), F32)],
        name=name, compiler_params=_cp())(*prep, states, do)


def _post_math(o, gate, wrow):
    return o * lax.rsqrt(jnp.mean(o * o, axis=-1, keepdims=True) + NORM_EPS) * wrow * _silu(gate)


def _dn_post_fwd(o, z_dn, dn_norm, name):
    tm = 512
    row = pl.BlockSpec((tm, DNW), lambda i: (i, 0))

    def body(o_ref, g_ref, w_ref, y_ref):
        for h in range(DN_H):
            ls = slice(128 * h, 128 * h + 128)
            y_ref[:, ls] = _post_math(o_ref[:, ls], g_ref[:, ls], w_ref[...]).astype(BF)

    return pl.pallas_call(
        body, grid=(T // tm,),
        in_specs=[row, pl.BlockSpec((tm, DNW), lambda i: (i, 3)), pl.BlockSpec((1, 128), lambda i: (0, 0))],
        out_specs=row, out_shape=jax.ShapeDtypeStruct((T, DNW), BF),
        name=name, compiler_params=_cp())(o, z_dn, dn_norm)


def _dn_post_bwd(o, z_dn, dn_norm, dy, name):
    tm = 512
    row = pl.BlockSpec((tm, DNW), lambda i: (i, 0))
    one = pl.BlockSpec((1, 128), lambda i: (0, 0))

    def body(o_ref, g_ref, w_ref, dy_ref, do_ref, dg_ref, dw_ref):
        i = pl.program_id(0)
        dw = jnp.zeros((1, 128), F32)
        for h in range(DN_H):
            ls = slice(128 * h, 128 * h + 128)
            _, vjp = jax.vjp(_post_math, o_ref[:, ls], g_ref[:, ls], w_ref[...])
            do, dg, dwh = vjp(dy_ref[:, ls].astype(F32))
            do_ref[:, ls] = do
            dg_ref[:, ls] = dg
            dw = dw + dwh

        @pl.when(i == 0)
        def _():
            dw_ref[...] = dw

        @pl.when(i > 0)
        def _():
            dw_ref[...] += dw

    return pl.pallas_call(
        body, grid=(T // tm,),
        in_specs=[row, pl.BlockSpec((tm, DNW), lambda i: (i, 3)), one, pl.BlockSpec((tm, DNW), lambda i: (i, 1))],
        out_specs=[row, row, one],
        out_shape=[jax.ShapeDtypeStruct((T, DNW), F32), jax.ShapeDtypeStruct((T, DNW), F32),
                   jax.ShapeDtypeStruct((1, 128), F32)],
        name=name, compiler_params=_cp())(o, z_dn, dn_norm, dy)


def _dz_dn_assemble(draw, dgate, dbd, name):
    tm = 512

    def body(a_ref, b_ref, c_ref, o_ref):
        o_ref[:, :3 * DNW] = a_ref[...].astype(BF)
        o_ref[:, 3 * DNW:4 * DNW] = b_ref[...].astype(BF)
        o_ref[:, 4 * DNW:4 * DNW + 128] = c_ref[...].astype(BF)
        o_ref[:, 4 * DNW + 128:] = jnp.zeros((tm, 128), BF)

    return pl.pallas_call(
        body, grid=(T // tm,),
        in_specs=[pl.BlockSpec((tm, 3 * DNW), lambda i: (i, 0)), pl.BlockSpec((tm, DNW), lambda i: (i, 0)),
                  pl.BlockSpec((tm, 128), lambda i: (i, 0))],
        out_specs=pl.BlockSpec((tm, ZD), lambda i: (i, 0)),
        out_shape=jax.ShapeDtypeStruct((T, ZD), BF), name=name, compiler_params=_cp())(draw, dgate, dbd)


HBM = pl.BlockSpec(memory_space=pltpu.HBM)
SEM = pl.BlockSpec(memory_space=pltpu.SEMAPHORE)
EFFECT = pltpu.SideEffectType.DATAFLOW_SIDE_EFFECTING
N_PEER = N_DEV - 1


ALL_PEERS = (1, 2, 4, 3, 5, 6, 7)
FIRST_HOP = (1, 2, 4, 6)
FORWARDED = (2, 4, 6)


def _peers(x, y, c, ks=ALL_PEERS):
    return [(k, (x ^ (k >> 2), y ^ ((k >> 1) & 1), c ^ (k & 1))) for k in ks]


def _exchange_copy(ins, lands, ssems, rsems, scatter, t, k, pos, me):
    px, py, pc = pos
    src = ins[t].at[4 * px + 2 * py + pc] if scatter else ins[t]
    return pltpu.make_async_remote_copy(
        src_ref=src, dst_ref=lands[t].at[me], send_sem=ssems[t].at[k - 1], recv_sem=rsems[t].at[k - 1],
        device_id=pos, device_id_type=MESH_ID)


def _xstart(bufs, scatter, name, ks=ALL_PEERS):
    nt = len(bufs)
    lands = [lax.empty((N_DEV,) + tuple(b.shape[1:] if scatter else b.shape), b.dtype) for b in bufs]

    def body(*refs):
        ins, lnd = refs[:nt], refs[nt:2 * nt]
        ssems, rsems = refs[2 * nt:3 * nt], refs[3 * nt:4 * nt]
        token = refs[-1]
        x, y, c = lax.axis_index("x"), lax.axis_index("y"), lax.axis_index("c")
        me = 4 * x + 2 * y + c
        for t in range(nt):
            for k, pos in _peers(x, y, c, ks):
                _exchange_copy(ins, lnd, ssems, rsems, scatter, t, k, pos, me).start()
        token[...] = jnp.zeros_like(token)

    both = list(bufs) + lands
    res = pl.pallas_call(
        body, name=name,
        out_shape=[pltpu.SemaphoreType.DMA((N_PEER,))] * (2 * nt)
        + [pltpu.HBM(b.shape, b.dtype) for b in both] + [jax.ShapeDtypeStruct((8, 128), F32)],
        in_specs=[HBM] * (2 * nt),
        out_specs=[SEM] * (2 * nt) + [HBM] * (2 * nt) + [pl.BlockSpec(memory_space=pltpu.VMEM)],
        input_output_aliases={i: 2 * nt + i for i in range(2 * nt)},
        compiler_params=pltpu.CompilerParams(has_side_effects=EFFECT),
    )(*[pltpu.with_memory_space_constraint(b, pltpu.HBM) for b in both])
    return res[:nt], res[nt:2 * nt], res[2 * nt:3 * nt], res[3 * nt:4 * nt], res[-1][0, 0]


def _xwait(ssems, rsems, thrus, lands, scatter, after, name, ks=ALL_PEERS):
    nt = len(lands)

    def body(*refs):
        ins, lnd = refs[:nt], refs[nt:2 * nt]
        ss, rs = refs[2 * nt:3 * nt], refs[3 * nt:4 * nt]
        x, y, c = lax.axis_index("x"), lax.axis_index("y"), lax.axis_index("c")
        me = 4 * x + 2 * y + c
        for t in range(nt):
            for k, pos in _peers(x, y, c, ks):
                cp = _exchange_copy(ins, lnd, ss, rs, scatter, t, k, pos, me)
                cp.wait_send()
                cp.wait_recv()

    both = list(thrus) + list(lands)
    res = pl.pallas_call(
        body, name=name, out_shape=[pltpu.HBM(b.shape, b.dtype) for b in both],
        in_specs=[HBM] * (2 * nt) + [SEM] * (2 * nt) + [ANY], out_specs=[HBM] * (2 * nt),
        input_output_aliases={i: i for i in range(2 * nt)},
        compiler_params=pltpu.CompilerParams(has_side_effects=EFFECT),
    )(*both, *ssems, *rsems, after)
    return res[:nt], res[nt:]


def _forward_copy(lands, ssems, rsems, t, k, pos, sibling):
    px, py, pc = pos
    slot = lands[t].at[4 * px + 2 * py + pc]
    return pltpu.make_async_remote_copy(
        src_ref=slot, dst_ref=slot, send_sem=ssems[t].at[k - 1], recv_sem=rsems[t].at[k - 1],
        device_id=sibling, device_id_type=MESH_ID)


def _fstart(lands, name):
    nt = len(lands)

    def body(*refs):
        lnd = refs[:nt]
        ssems, rsems = refs[nt:2 * nt], refs[2 * nt:3 * nt]
        token = refs[-1]
        x, y, c = lax.axis_index("x"), lax.axis_index("y"), lax.axis_index("c")
        for t in range(nt):
            for k, pos in _peers(x, y, c, FORWARDED):
                _forward_copy(lnd, ssems, rsems, t, k, pos, (x, y, c ^ 1)).start()
        token[...] = jnp.zeros_like(token)

    res = pl.pallas_call(
        body, name=name,
        out_shape=[pltpu.SemaphoreType.DMA((N_PEER,))] * (2 * nt)
        + [pltpu.HBM(b.shape, b.dtype) for b in lands] + [jax.ShapeDtypeStruct((8, 128), F32)],
        in_specs=[HBM] * nt,
        out_specs=[SEM] * (2 * nt) + [HBM] * nt + [pl.BlockSpec(memory_space=pltpu.VMEM)],
        input_output_aliases={i: 2 * nt + i for i in range(nt)},
        compiler_params=pltpu.CompilerParams(has_side_effects=EFFECT),
    )(*[pltpu.with_memory_space_constraint(b, pltpu.HBM) for b in lands])
    return res[:nt], res[nt:2 * nt], res[2 * nt:3 * nt], res[-1][0, 0]


def _fwait(ssems, rsems, lands, after, name):
    nt = len(lands)

    def body(*refs):
        lnd = refs[:nt]
        ss, rs = refs[nt:2 * nt], refs[2 * nt:3 * nt]
        x, y, c = lax.axis_index("x"), lax.axis_index("y"), lax.axis_index("c")
        for t in range(nt):
            for k, pos in _peers(x, y, c, FORWARDED):
                cp = _forward_copy(lnd, ss, rs, t, k, pos, (x, y, c ^ 1))
                cp.wait_send()
                cp.wait_recv()

    return pl.pallas_call(
        body, name=name, out_shape=[pltpu.HBM(b.shape, b.dtype) for b in lands],
        in_specs=[HBM] * nt + [SEM] * (2 * nt) + [ANY], out_specs=[HBM] * nt,
        input_output_aliases={i: i for i in range(nt)},
        compiler_params=pltpu.CompilerParams(has_side_effects=EFFECT),
    )(*lands, *ssems, *rsems, after)


def _adam(recv, w, m, v, tr, name):
    _, r, c = w.shape
    n_part = recv.shape[0]
    c1 = np.float32(1.0 - ADAM_B1 ** ADAM_STEP)
    c2 = np.float32(1.0 - ADAM_B2 ** ADAM_STEP)

    def body(r_ref, w_ref, m_ref, v_ref, g_ref, d_ref, mo_ref, vo_ref):
        g = r_ref[0].astype(F32)
        for s in range(1, n_part):
            g = g + r_ref[s].astype(F32)
        mn = ADAM_B1 * m_ref[0] + (1.0 - ADAM_B1) * g
        vn = ADAM_B2 * v_ref[0] + (1.0 - ADAM_B2) * (g * g)
        g_ref[0] = g
        mo_ref[0] = mn
        vo_ref[0] = vn
        d_ref[0] = -ADAM_LR * ((mn / c1) / (jnp.sqrt(vn / c2) + ADAM_EPS) + ADAM_WD * w_ref[0])

    one = pl.BlockSpec((1, tr, c), lambda i: (0, i, 0))
    return pl.pallas_call(
        body, grid=(r // tr,), in_specs=[pl.BlockSpec((n_part, tr, c), lambda i: (0, i, 0)), one, one, one],
        out_specs=[one] * 4, out_shape=[jax.ShapeDtypeStruct((1, r, c), F32)] * 4,
        name=name, compiler_params=_cp())(recv, w, m, v)


def _local_step(x, target, sp, need, ahead, emit):
    g = {}
    x0, h1 = x, _rms_fwd(x, sp["norm_ffn1"], "ffn1_norm")
    act1, saved1 = _ffn_up(h1, lambda kind, a: need("w" + kind + "1", a),
                           lambda kind, a: ahead("w" + kind + "1", a), "ffn1")
    x1, h2 = _mm_nn_resnorm([(act1, saved1[-1])], x0, 0.5, sp["norm_mix"], 512, "ffn1_down_norm",
                            after=ahead("win_a", act1))
    win_a, win_d = need("win_a", h2), need("win_d", h2)
    conv_w8, wout = need("conv_w8", h2), need("wout", h2)
    z_at = _mm_nn(h2, win_a, F32, 1024, 768, "mix_in_attn")
    z_dn = _mm_nn(h2, win_d, F32, 1024, 768, "mix_in_dn")

    conv, qkvn, bg = _dn_pre_fwd(z_dn, conv_w8, sp["alog_row"], sp["dtb_row"], "dn_pre")
    *prep, tinv = _dn_prep_fwd(qkvn, bg, "dn_prep")
    o_dn, states = _dn_scan_fwd(prep, "dn_scan")
    dn_b = _dn_post_fwd(o_dn, z_dn, sp["dn_norm"], "dn_post")
    attn_f, attn_b, lse = _attn_fwd(z_at, "attn_fwd", after=ahead("wg2", dn_b))

    x2, h3 = _mm_nn_resnorm([(attn_b, wout[:AW]), (dn_b, wout[AW:])], x1, 1.0, sp["norm_ffn2"], 512,
                            "mix_out_norm")
    act2, saved2 = _ffn_up(h3, lambda kind, a: need("w" + kind + "2", a),
                           lambda kind, a: ahead("w" + kind + "2", a), "ffn2")

    dx3, dys3, g["norm_final"], loss8 = _down_loss_bwd(x2, act2, saved2[-1], sp["norm_final"], target,
                                                       "ffn2_down_loss")
    dx2, dx2b, g["norm_ffn2"] = _ffn_bwd(
        x2, sp["norm_ffn2"], saved2, dx3, dys3, 1.0,
        lambda kind, dw: emit(kind + "2", {"w" + kind + "2": dw}), "ffn2b")

    zero = emit("wout", {"wout": jnp.concatenate([_mm_tn(attn_b, dx2b, BF, 512, 1024, "mix_out_dw_a"),
                                                  _mm_tn(dn_b, dx2b, BF, 512, 1024, "mix_out_dw_d")], axis=0)})
    dmix = _mm_nt(dx2b, wout, F32, 1024, 1024, "mix_out_dx", after=zero)

    dz_at = jnp.concatenate(_attn_bwd(z_at, dmix, attn_f, lse, "attn_bwd"), axis=1)

    do_dn, dgate, g["dn_norm"] = _dn_post_bwd(o_dn, z_dn, sp["dn_norm"], dmix, "dn_post_b")
    cts = _dn_scan_bwd(prep, states, do_dn, "dn_scan_b")
    dqn, dkn, dvn, dbg = _dn_prep_bwd(qkvn, bg, tinv, cts, "dn_prep_b")
    dconv, dbd, g["alog_row"], g["dtb_row"] = _dn_pre_bwd(
        conv, z_dn, sp["alog_row"], sp["dtb_row"], dqn, dkn, dvn, dbg, "dn_pre_b")
    draw, dconv_w8 = _dn_conv_bwd(dconv, z_dn, conv_w8, "dn_conv_b")
    dz_dn = _dz_dn_assemble(draw, dgate, dbd, "dn_dz")

    zero = emit("win", {"win_a": _mm_tn(h2, dz_at, BF, 512, 768, "mix_in_dw_a"),
                        "win_d": _mm_tn(h2, dz_dn, BF, 512, 768, "mix_in_dw_d"), "conv_w8": dconv_w8})
    dx1, dys1, g["norm_mix"] = _mm_rms_bwd([(dz_at, win_a), (dz_dn, win_d)], NT, x1, sp["norm_mix"], dx2, 0.5,
                                           512, "mix_in_dx_dnorm", after=zero)
    dx0, _, g["norm_ffn1"] = _ffn_bwd(
        x0, sp["norm_ffn1"], saved1, dx1, dys1, 1.0,
        lambda kind, dw: emit(kind + "1", {"w" + kind + "1": dw}), "ffn1b")
    return loss8[0, 0], dx0, g


def _cols_from_shards(gathered):
    n, r, c = gathered.shape
    return jnp.transpose(gathered, (1, 0, 2)).reshape(r, n * c)


def _shards_from_cols(full, dtype):
    r, nc = full.shape
    return jnp.transpose(full.reshape(r, N_DEV, nc // N_DEV), (1, 0, 2)).astype(dtype)


def _lane_row(vec4):
    return jnp.zeros((1, 128), F32).at[:, DN_H:2 * DN_H].set(vec4.astype(F32))


WEIGHT_SOURCES = {"wg1": "gate1", "wu1": "up1", "wd1": "down1", "win_a": "w_in", "win_d": "w_in",
                  "conv_w8": "conv_w", "wout": "w_out", "wg2": "gate2", "wu2": "up2", "wd2": "down2"}
TRANSPOSED = ("gate1", "up1", "gate2", "up2")


def _build_weights(name, gath):
    if name in ("wg1", "wu1", "wd1", "wg2", "wu2", "wd2"):
        return {name: gath[WEIGHT_SOURCES[name]].reshape(F, D)}
    if name in ("win_a", "win_d"):
        w_in = _cols_from_shards(gath["w_in"])
        c0 = 3 * AW + 3 * DNW
        win_d = jnp.concatenate([w_in[:, ZA:c0], w_in[:, c0 + 2 * DN_H:], w_in[:, c0:c0 + 2 * DN_H],
                                 jnp.zeros((D, ZP - IN_COLS), w_in.dtype)], axis=1)
        return {"win_a": w_in[:, :ZA], "win_d": win_d}
    if name == "wout":
        return {name: gath["w_out"].reshape(D, D)}
    conv = _cols_from_shards(gath["conv_w"])
    return {"conv_w8": jnp.concatenate([conv, jnp.zeros((4, 3 * DNW), F32)], axis=0)}


def _small_params(norm_ffn1, norm_mix, norm_ffn2, norm_final, a_log, dt_bias, dn_norm):
    return {"norm_ffn1": norm_ffn1, "norm_mix": norm_mix, "norm_ffn2": norm_ffn2,
            "norm_final": norm_final.reshape(1, D), "alog_row": _lane_row(a_log), "dtb_row": _lane_row(dt_bias),
            "dn_norm": dn_norm}


def _grad_slabs(group, g):
    if group[0] in "gud":
        return {WEIGHT_SOURCES["w" + group]: g["w" + group].reshape(N_DEV, F // N_DEV, D)}
    if group == "wout":
        return {"w_out": g["wout"].reshape(N_DEV, D // N_DEV, D)}
    gp = jnp.concatenate([g["win_a"], g["win_d"]], axis=1)
    c0 = 3 * AW + 3 * DNW
    g_in = jnp.concatenate([gp[:, :c0], gp[:, c0 + DNW:c0 + DNW + 2 * DN_H], gp[:, c0:c0 + DNW]], axis=1)
    return {"w_in": _shards_from_cols(g_in, BF), "conv_w": _shards_from_cols(g["conv_w8"][:4], F32)}


SMALL_ROWS = 40


def _small_pack(norm_ffn1, norm_mix, norm_ffn2, norm_final, dn_norm, alog_row, dtb_row, loss=None):
    rows = [a.reshape(8, 128) for a in (norm_ffn1, norm_mix, norm_ffn2, norm_final)]
    loss_row = jnp.zeros((1, 128), F32) if loss is None else jnp.broadcast_to(loss.reshape(1, 1), (1, 128))
    rows += [dn_norm.reshape(1, 128), alog_row, dtb_row, loss_row, jnp.zeros((SMALL_ROWS - 36, 128), F32)]
    return jnp.concatenate(rows, axis=0)


def _small_unpack(pk):
    pk = pk[0]
    return (pk[0:8].reshape(1, D), pk[8:16].reshape(1, D), pk[16:24].reshape(1, D), pk[24:32].reshape(D),
            pk[32:33], pk[33:34, DN_H:2 * DN_H], pk[34:35, DN_H:2 * DN_H])


ADAM_TILE = {"gate1": 256, "up1": 256, "down1": 176, "gate2": 256, "up2": 256, "down2": 176,
             "w_in": 256, "w_out": 128, "conv_w": 4}
BIG = ("gate1", "up1", "down1", "w_in", "w_out", "gate2", "up2", "down2", "conv_w")


def kernel(x, norm_ffn1, ffn1_gate, ffn1_up, ffn1_down, norm_mix, w_in, conv_w, a_log, dt_bias, dn_norm, w_out, norm_ffn2, ffn2_gate, ffn2_up, ffn2_down, norm_final, loss_target, m_norm_ffn1, m_ffn1_gate, m_ffn1_up, m_ffn1_down, m_norm_mix, m_w_in, m_conv_w, m_a_log, m_dt_bias, m_dn_norm, m_w_out, m_norm_ffn2, m_ffn2_gate, m_ffn2_up, m_ffn2_down, m_norm_final, v_norm_ffn1, v_ffn1_gate, v_ffn1_up, v_ffn1_down, v_norm_mix, v_w_in, v_conv_w, v_a_log, v_dt_bias, v_dn_norm, v_w_out, v_norm_ffn2, v_ffn2_gate, v_ffn2_up, v_ffn2_down, v_norm_final):
    w = {"gate1": ffn1_gate, "up1": ffn1_up, "down1": ffn1_down, "w_in": w_in, "w_out": w_out,
         "gate2": ffn2_gate, "up2": ffn2_up, "down2": ffn2_down, "conv_w": conv_w}
    m = {"gate1": m_ffn1_gate, "up1": m_ffn1_up, "down1": m_ffn1_down, "w_in": m_w_in, "w_out": m_w_out,
         "gate2": m_ffn2_gate, "up2": m_ffn2_up, "down2": m_ffn2_down, "conv_w": m_conv_w}
    v = {"gate1": v_ffn1_gate, "up1": v_ffn1_up, "down1": v_ffn1_down, "w_in": v_w_in, "w_out": v_w_out,
         "gate2": v_ffn2_gate, "up2": v_ffn2_up, "down2": v_ffn2_down, "conv_w": v_conv_w}

    me = 4 * lax.axis_index("x") + 2 * lax.axis_index("y") + lax.axis_index("c")
    own_slot = lambda land, mine: lax.dynamic_update_index_in_dim(land, mine, me, 0)

    ag_order = ("gate1", "up1", "down1", "w_in", "conv_w", "w_out", "gate2", "up2", "down2")
    ag_groups = (("gate1",), ("up1",), ("down1",), ("w_in", "conv_w", "w_out"), ("gate2", "up2", "down2"))
    pos = {n: i for i, n in enumerate(ag_order)}

    def shard(n):
        if n == "conv_w":
            return w[n][0]
        return (w[n][0].T if n in TRANSPOSED else w[n][0]).astype(BF)

    ss, rs, thru, land, zero = _xstart([shard(n) for n in ag_order], False, "weights_start", FIRST_HOP)
    gath, built, on_its_way = {}, {}, {}
    group_of = lambda name: [i for i, grp in enumerate(ag_groups) if WEIGHT_SOURCES[name] in grp][0]

    def ahead(name, after):
        gi = group_of(name)
        if WEIGHT_SOURCES[name] in gath or gi in on_its_way:
            return None
        ids = [pos[n] for n in ag_groups[gi]]
        thrus, lands = _xwait([ss[i] for i in ids], [rs[i] for i in ids], [thru[i] for i in ids],
                              [land[i] for i in ids], False, after, "weights_wait%d" % gi, FIRST_HOP)
        fss, frs, lands, token = _fstart(lands, "weights_forward%d" % gi)
        on_its_way[gi] = (thrus, fss, frs, lands)
        return token

    def need(name, after):
        if name not in built:
            if WEIGHT_SOURCES[name] not in gath:
                gi = group_of(name)
                ahead(name, after)
                thrus, fss, frs, lands = on_its_way.pop(gi)
                lands = _fwait(fss, frs, lands, after, "weights_forward_wait%d" % gi)
                for n, t, l in zip(ag_groups[gi], thrus, lands):
                    gath[n] = own_slot(l, t)
            built.update(_build_weights(name, gath))
        return built[name]

    pending = []

    def emit(group, grads):
        slabs = grads if group == "small" else _grad_slabs(group, grads)
        names = list(slabs)
        started = _xstart([slabs[n] for n in names], True, "grads_start_" + group)
        pending.append((group, names) + started[:4])
        return started[4]

    sp = _small_params(norm_ffn1 + zero, norm_mix, norm_ffn2, norm_final, a_log, dt_bias, dn_norm)
    loss_part, dx, g = _local_step(x[0], loss_target[0], sp, need, ahead, emit)
    small = _small_pack(g["norm_ffn1"], g["norm_mix"], g["norm_ffn2"], g["norm_final"], g["dn_norm"],
                        g["alog_row"], g["dtb_row"], loss_part)
    emit("small", {"small": jnp.broadcast_to(small[None], (N_DEV, SMALL_ROWS, 128))})

    pack = lambda a: _small_pack(*a)[None]
    res, after = {}, dx
    for group, names, gss, grs, gthru, gland in pending:
        thrus, lands = _xwait(gss, grs, gthru, gland, True, after, "grads_wait_" + group)
        for n, t, l in zip(names, thrus, lands):
            recv = own_slot(l, lax.dynamic_index_in_dim(t, me, 0, keepdims=False))
            if n == "small":
                res[n] = _adam(
                    recv,
                    pack((norm_ffn1, norm_mix, norm_ffn2, norm_final, dn_norm, _lane_row(a_log), _lane_row(dt_bias))),
                    pack((m_norm_ffn1, m_norm_mix, m_norm_ffn2, m_norm_final, m_dn_norm, _lane_row(m_a_log),
                          _lane_row(m_dt_bias))),
                    pack((v_norm_ffn1, v_norm_mix, v_norm_ffn2, v_norm_final, v_dn_norm, _lane_row(v_a_log),
                          _lane_row(v_dt_bias))),
                    SMALL_ROWS, "adam_small")
            elif n in TRANSPOSED:
                flip = lambda a: jnp.swapaxes(a, 1, 2)
                res[n] = [flip(o) for o in _adam(recv, flip(w[n]), flip(m[n]), flip(v[n]), F // N_DEV // 2,
                                                 "adam_" + n)]
            else:
                res[n] = _adam(recv, w[n], m[n], v[n], ADAM_TILE[n], "adam_" + n)
            after = res[n][0]
    res_s = res["small"]

    loss = res_s[0][0, 35, 0]
    outs = [loss, dx[None]]
    for k in range(4):
        n1, nm, n2, nf, dn, al, dt = _small_unpack(res_s[k])
        big = {n: res[n][k] for n in BIG}
        outs += [n1, big["gate1"], big["up1"], big["down1"], nm, big["w_in"], big["conv_w"], al, dt, dn,
                 big["w_out"], n2, big["gate2"], big["up2"], big["down2"], nf]
    return tuple(outs)
```

```python
import functools

import numpy as np
import jax
import jax.numpy as jnp
from jax import lax
from jax.experimental import pallas as pl
from jax.experimental.pallas import tpu as pltpu

T = 4096
D = 1024
F = 2816
N_DEV = 8
A_HEADS = 8
A_HD = 64
AW = A_HEADS * A_HD
DN_H = 4
DN_HD = 128
DNW = DN_H * DN_HD
CH = 64
PAIR = 2 * CH
ZA = 3 * AW
ZD = 3 * DNW + DNW + 256
ZP = ZA + ZD
BD_BLK = (3 * DNW + DNW) // 128
IN_COLS = 3592
PATTERNS = ((128, 1), (512, 4), (2048, 16))
NORM_EPS = 1e-6
L2_EPS = 1e-6
ADAM_LR, ADAM_B1, ADAM_B2, ADAM_EPS, ADAM_WD, ADAM_STEP = 0.001, 0.9, 0.999, 1e-08, 0.01, 10
VMEM_LIMIT = 56 * 1024 * 1024
NEG = -1e30

BF = jnp.bfloat16
F32 = jnp.float32
NN = (((1,), (0,)), ((), ()))
NT = (((1,), (1,)), ((), ()))
TN = (((0,), (0,)), ((), ()))
HI = lax.Precision.HIGHEST
MESH_ID = pl.DeviceIdType.MESH
ANY = pl.BlockSpec(memory_space=pl.ANY)


def _cp():
    return pltpu.CompilerParams(vmem_limit_bytes=VMEM_LIMIT)


def _dg(a, b, dims):
    return lax.dot_general(a, b, dims, preferred_element_type=F32)


def _hdot(a, b):
    return lax.dot_general(a, b, NN, precision=HI, preferred_element_type=F32)


def _make_bdot(dims, da_dims, da_swap, db_dims, db_swap):
    @jax.custom_vjp
    def f(a, b):
        return _dg(a.astype(BF), b.astype(BF), dims)

    def fwd(a, b):
        return f(a, b), (a, b)

    def bwd(res, g):
        a, b = res
        gb, ab, bb = g.astype(BF), a.astype(BF), b.astype(BF)
        da = _dg(bb, gb, da_dims) if da_swap else _dg(gb, bb, da_dims)
        db = _dg(gb, ab, db_dims) if db_swap else _dg(ab, gb, db_dims)
        return da.astype(a.dtype), db.astype(b.dtype)

    f.defvjp(fwd, bwd)
    return f


_bdot_nn = _make_bdot(NN, NT, False, TN, False)
_bdot_nt = _make_bdot(NT, NN, False, TN, True)
_bdot_tn = _make_bdot(TN, NT, True, NN, False)


def _iota(shape, dim):
    return lax.broadcasted_iota(jnp.int32, shape, dim)


def _col(x, idx):
    return jnp.sum(jnp.where(_iota(x.shape, 1) == idx, x, 0.0), axis=1, keepdims=True)


def _mm_nn(a, b, out_dtype, tm, tn, name):
    m, k = a.shape
    n = b.shape[1]

    def body(a_ref, b_ref, o_ref):
        o_ref[...] = _dg(a_ref[...], b_ref[...], NN).astype(out_dtype)

    return pl.pallas_call(
        body, grid=(m // tm, n // tn),
        in_specs=[pl.BlockSpec((tm, k), lambda i, j: (i, 0)), pl.BlockSpec((k, tn), lambda i, j: (0, j))],
        out_specs=pl.BlockSpec((tm, tn), lambda i, j: (i, j)),
        out_shape=jax.ShapeDtypeStruct((m, n), out_dtype), name=name, compiler_params=_cp())(a, b)


def _tie(body, after):
    if after is None:
        return body, [], []
    return (lambda tok_ref, *refs: body(*refs)), [ANY], [after.reshape(1, 1)]


def _mm_nt(a, b, out_dtype, tm, tb, name, after=None):
    m, c = a.shape
    kb = b.shape[0]

    def body(a_ref, b_ref, o_ref):
        o_ref[...] = _dg(a_ref[...], b_ref[...], NT).astype(out_dtype)

    body, tspec, tok = _tie(body, after)
    return pl.pallas_call(
        body, grid=(m // tm, kb // tb),
        in_specs=tspec + [pl.BlockSpec((tm, c), lambda i, j: (i, 0)), pl.BlockSpec((tb, c), lambda i, j: (j, 0))],
        out_specs=pl.BlockSpec((tm, tb), lambda i, j: (i, j)),
        out_shape=jax.ShapeDtypeStruct((m, kb), out_dtype), name=name, compiler_params=_cp())(*tok, a, b)


def _mm_tn(a, b, out_dtype, ta, tb, name, after=None):
    m, ka = a.shape
    nb = b.shape[1]

    def body(a_ref, b_ref, o_ref):
        o_ref[...] = _dg(a_ref[...], b_ref[...], TN).astype(out_dtype)

    body, tspec, tok = _tie(body, after)
    return pl.pallas_call(
        body, grid=(ka // ta, nb // tb),
        in_specs=tspec + [pl.BlockSpec((m, ta), lambda i, j: (0, i)), pl.BlockSpec((m, tb), lambda i, j: (0, j))],
        out_specs=pl.BlockSpec((ta, tb), lambda i, j: (i, j)),
        out_shape=jax.ShapeDtypeStruct((ka, nb), out_dtype), name=name, compiler_params=_cp())(*tok, a, b)


def _mm_nt_swiglu(h, wu_t, gate, tm, tb, name, after=None):
    m, c = h.shape
    kb = wu_t.shape[0]

    def body(h_ref, w_ref, g_ref, u_ref, a_ref):
        u = _dg(h_ref[...], w_ref[...], NT)
        g = g_ref[...].astype(F32)
        u_ref[...] = u.astype(BF)
        a_ref[...] = (g * jax.nn.sigmoid(g) * u).astype(BF)

    body, tspec, tok = _tie(body, after)
    tile = pl.BlockSpec((tm, tb), lambda i, j: (i, j))
    return pl.pallas_call(
        body, grid=(m // tm, kb // tb),
        in_specs=tspec + [pl.BlockSpec((tm, c), lambda i, j: (i, 0)), pl.BlockSpec((tb, c), lambda i, j: (j, 0)),
                          tile],
        out_specs=[tile, tile], out_shape=[jax.ShapeDtypeStruct((m, kb), BF)] * 2,
        name=name, compiler_params=_cp())(*tok, h, wu_t, gate)


def _mm_nt_dswiglu(dys, wd, gate, up, tm, tb, name, after=None):
    m, c = dys.shape
    kb = wd.shape[0]

    def body(d_ref, w_ref, g_ref, u_ref, dg_ref, du_ref):
        da = _dg(d_ref[...], w_ref[...], NT)
        g = g_ref[...].astype(F32)
        u = u_ref[...].astype(F32)
        s = jax.nn.sigmoid(g)
        dg_ref[...] = (da * u * (s * (1.0 + g * (1.0 - s)))).astype(BF)
        du_ref[...] = (da * (g * s)).astype(BF)

    body, tspec, tok = _tie(body, after)
    tile = pl.BlockSpec((tm, tb), lambda i, j: (i, j))
    return pl.pallas_call(
        body, grid=(m // tm, kb // tb),
        in_specs=tspec + [pl.BlockSpec((tm, c), lambda i, j: (i, 0)), pl.BlockSpec((tb, c), lambda i, j: (j, 0)),
                          tile, tile],
        out_specs=[tile, tile], out_shape=[jax.ShapeDtypeStruct((m, kb), BF)] * 2,
        name=name, compiler_params=_cp())(*tok, dys, wd, gate, up)


def _rms_fwd(x, gain, name):
    tm = 512
    row = pl.BlockSpec((tm, D), lambda i: (i, 0))

    def body(x_ref, g_ref, h_ref):
        xv = x_ref[...]
        r = lax.rsqrt(jnp.mean(xv * xv, axis=-1, keepdims=True) + NORM_EPS)
        h_ref[...] = (xv * r * g_ref[...]).astype(BF)

    return pl.pallas_call(
        body, grid=(T // tm,), in_specs=[row, pl.BlockSpec((1, D), lambda i: (0, 0))], out_specs=row,
        out_shape=jax.ShapeDtypeStruct((T, D), BF), name=name, compiler_params=_cp())(x, gain)


def _mm_rms_bwd(pairs, dims, x, gain, dres, alpha_out, tm, name, after=None):
    n = len(pairs)
    row = pl.BlockSpec((tm, D), lambda i: (i, 0))
    gspec = pl.BlockSpec((1, D), lambda i: (0, 0))

    def body(*refs):
        x_ref, g_ref, dres_ref, dx_ref, dxs_ref, dg_ref = refs[2 * n:]
        i = pl.program_id(0)
        dhv = _dg(refs[0][...], refs[n][...], dims)
        for p in range(1, n):
            dhv = dhv + _dg(refs[p][...], refs[n + p][...], dims)
        xv = x_ref[...]
        r = lax.rsqrt(jnp.mean(xv * xv, axis=-1, keepdims=True) + NORM_EPS)
        xh = xv * r
        part = jnp.sum(dhv * xh, axis=0, keepdims=True)

        @pl.when(i == 0)
        def _():
            dg_ref[...] = part

        @pl.when(i > 0)
        def _():
            dg_ref[...] += part

        dxh = dhv * g_ref[...]
        dx = r * (dxh - xh * jnp.mean(dxh * xh, axis=-1, keepdims=True)) + dres_ref[...]
        dx_ref[...] = dx
        dxs_ref[...] = (alpha_out * dx).astype(BF)

    body, tspec, tok = _tie(body, after)

    def bspec(a, b, kblk):
        if kblk is None:
            return pl.BlockSpec(b.shape, lambda i: (0, 0))
        return pl.BlockSpec((b.shape[0], a.shape[1]), lambda i: (0, kblk))

    return pl.pallas_call(
        body, grid=(T // tm,),
        in_specs=tspec + [pl.BlockSpec((tm, p[0].shape[1]), lambda i: (i, 0)) for p in pairs]
        + [bspec(p[0], p[1], p[2] if len(p) > 2 else None) for p in pairs] + [row, gspec, row],
        out_specs=[row, row, gspec],
        out_shape=[jax.ShapeDtypeStruct((T, D), F32), jax.ShapeDtypeStruct((T, D), BF),
                   jax.ShapeDtypeStruct((1, D), F32)],
        name=name, compiler_params=_cp())(*tok, *[p[0] for p in pairs], *[p[1] for p in pairs], x, gain, dres)


def _down_loss_bwd(x_prev, act, wd, gain, target, name):
    tm = 512
    row = pl.BlockSpec((tm, D), lambda i: (i, 0))
    gspec = pl.BlockSpec((1, D), lambda i: (0, 0))
    lspec = pl.BlockSpec((8, 128), lambda i: (0, 0))

    def body(x_ref, a_ref, w_ref, g_ref, t_ref, dx_ref, dxs_ref, dg_ref, loss_ref):
        i = pl.program_id(0)
        xv = x_ref[...] + 0.5 * _dg(a_ref[...], w_ref[...], NN)
        r = lax.rsqrt(jnp.mean(xv * xv, axis=-1, keepdims=True) + NORM_EPS)
        xh = xv * r
        diff = xh * g_ref[...] - t_ref[...]
        lpart = 0.5 * jnp.sum(jnp.mean(diff * diff, axis=-1, keepdims=True), axis=0, keepdims=True)
        dy = diff * (1.0 / D)
        part = jnp.sum(dy * xh, axis=0, keepdims=True)

        @pl.when(i == 0)
        def _():
            dg_ref[...] = part
            loss_ref[...] = jnp.broadcast_to(lpart, (8, 128))

        @pl.when(i > 0)
        def _():
            dg_ref[...] += part
            loss_ref[...] += jnp.broadcast_to(lpart, (8, 128))

        dxh = dy * g_ref[...]
        dx = r * (dxh - xh * jnp.mean(dxh * xh, axis=-1, keepdims=True))
        dx_ref[...] = dx
        dxs_ref[...] = (0.5 * dx).astype(BF)

    return pl.pallas_call(
        body, grid=(T // tm,),
        in_specs=[row, pl.BlockSpec((tm, F), lambda i: (i, 0)), pl.BlockSpec((F, D), lambda i: (0, 0)), gspec, row],
        out_specs=[row, row, gspec, lspec],
        out_shape=[jax.ShapeDtypeStruct((T, D), F32), jax.ShapeDtypeStruct((T, D), BF),
                   jax.ShapeDtypeStruct((1, D), F32), jax.ShapeDtypeStruct((8, 128), F32)],
        name=name, compiler_params=_cp())(x_prev, act, wd, gain, target)


def _mm_nn_resnorm(pairs, x_prev, alpha, gain, tm, name, after=None):
    m = x_prev.shape[0]
    n = len(pairs)

    def body(*refs):
        x_ref, g_ref, xo_ref, h_ref = refs[2 * n:]
        y = _dg(refs[0][...], refs[n][...], NN)
        for i in range(1, n):
            y = y + _dg(refs[i][...], refs[n + i][...], NN)
        xv = x_ref[...] + alpha * y
        xo_ref[...] = xv
        r = lax.rsqrt(jnp.mean(xv * xv, axis=-1, keepdims=True) + NORM_EPS)
        h_ref[...] = (xv * r * g_ref[...]).astype(BF)

    body, tspec, tok = _tie(body, after)
    row = pl.BlockSpec((tm, D), lambda i: (i, 0))
    return pl.pallas_call(
        body, grid=(m // tm,),
        in_specs=tspec + [pl.BlockSpec((tm, a.shape[1]), lambda i: (i, 0)) for a, _ in pairs]
        + [pl.BlockSpec(b.shape, lambda i: (0, 0)) for _, b in pairs] + [row, pl.BlockSpec((1, D), lambda i: (0, 0))],
        out_specs=[row, row],
        out_shape=[jax.ShapeDtypeStruct((m, D), F32), jax.ShapeDtypeStruct((m, D), BF)],
        name=name, compiler_params=_cp())(*tok, *[a for a, _ in pairs], *[b for _, b in pairs], x_prev, gain)


def _ffn_up(h, need, ahead, tag):
    wg = need("g", h)
    gate = _mm_nt(h, wg, BF, 1024, 1408, tag + "_gate")
    wu = need("u", gate)
    up, act = _mm_nt_swiglu(h, wu, gate, 1024, 1408, tag + "_up_act", after=ahead("d", wu))
    wd = need("d", up)
    return act, (h, gate, up, act, wg, wu, wd)


def _ffn_bwd(x_in, gain, saved, dxo, dys, alpha_out, emit, tag):
    h, gate, up, act, wg, wu, wd = saved
    sent = emit("d", _mm_tn(act, dys, BF, 256, 1024, tag + "_dwd"))
    dgate, dup = _mm_nt_dswiglu(dys, wd, gate, up, 1024, 1408, tag + "_dact", after=sent)
    sent = emit("g", _mm_tn(dgate, h, BF, 256, 1024, tag + "_dwg"))
    sent = emit("u", _mm_tn(dup, h, BF, 256, 1024, tag + "_dwu", after=sent))
    return _mm_rms_bwd([(dgate, wg), (dup, wu)], NN, x_in, gain, dxo, alpha_out, 512, tag + "_dh_dnorm",
                       after=sent)


SLAB = 2048
N_SLAB = T // SLAB
N_PAIR = A_HEADS // 2


def _pair_masks():
    lane = _iota((128, 128), 1)
    return lane < A_HD, lane >= A_HD


def _slope_table():
    h = 2 * jnp.arange(N_PAIR)[:, None] + jnp.minimum(jnp.arange(8), 1)[None, :]
    return jnp.broadcast_to((2.0 ** (-(h + 1).astype(F32)))[:, :, None], (N_PAIR, 8, 128))


def _rows(ref, start, d):
    if d == 1:
        return ref[pl.ds(start, 128), :]
    return ref[pl.ds(start, 128, stride=d), :]


def _put_rows(ref, start, d, val):
    if d == 1:
        ref[pl.ds(start, 128), :] = val
    else:
        ref[pl.ds(start, 128, stride=d), :] = val


def _units(d):
    return [(r, b, r + 128 * d * b) for r in range(d) for b in range(SLAB // (128 * d))]


def _biases(d, slopes, has_prev):
    qi = _iota((128, 256), 0)
    kj = _iota((128, 256), 1)
    steps = qi + 128 - kj
    in_band = (steps >= 0) & (steps <= 128)
    dist = (steps * d).astype(F32)
    base = [jnp.where(in_band, -(sl * dist), NEG) for sl in slopes]
    edge = [jnp.where(has_prev | (kj >= 128), b, NEG) for b in base]
    return base, edge


def _attn_fwd(z_at, name, after=None):
    def body(sl_ref, q_ref, kc_ref, kp_ref, vc_ref, vp_ref, of_ref, ob_ref, lse_ref, m_s, l_s, a_s):
        n = pl.program_id(1)
        lo, hi = _pair_masks()
        slopes = (sl_ref[0, 0:1, 0:1], sl_ref[0, 1:2, 0:1])

        def unit(d, start, b, first, carry, bias):
            q = _rows(q_ref, start, d).astype(BF)
            kcur, vcur = _rows(kc_ref, start, d).astype(BF), _rows(vc_ref, start, d).astype(BF)
            if b > 0:
                kprev, vprev = carry
            else:
                pstart = start + SLAB - 128 * d
                kprev, vprev = _rows(kp_ref, pstart, d).astype(BF), _rows(vp_ref, pstart, d).astype(BF)
            kcat = jnp.concatenate([kprev, kcur], axis=0)
            vcat = jnp.concatenate([vprev, vcur], axis=0)
            ms, ls, pvs = [], [], []
            for e in range(2):
                qm = jnp.where(lo if e == 0 else hi, q, jnp.zeros_like(q))
                s = _dg(qm, kcat, NT) * (A_HD ** -0.5) + bias[e]
                m = jnp.max(s, axis=1, keepdims=True)
                p = jnp.exp(s - m)
                ms.append(m)
                ls.append(jnp.sum(p, axis=1, keepdims=True))
                pvs.append(_dg(p.astype(BF), vcat, NN))
            m_u = jnp.where(lo, ms[0], ms[1])
            l_u = jnp.where(lo, ls[0], ls[1])
            a_u = jnp.where(lo, pvs[0], pvs[1])
            if first:
                m_n, l_n, a_n = m_u, l_u, a_u
            else:
                m_o = _rows(m_s, start, d)
                m_n = jnp.maximum(m_o, m_u)
                c_o = jnp.exp(m_o - m_n)
                c_u = jnp.exp(m_u - m_n)
                l_n = _rows(l_s, start, d) * c_o + l_u * c_u
                a_n = _rows(a_s, start, d) * c_o + a_u * c_u
            _put_rows(m_s, start, d, m_n)
            _put_rows(l_s, start, d, l_n)
            _put_rows(a_s, start, d, a_n)
            return kcur, vcur

        for pi, (_, d) in enumerate(PATTERNS):
            base, edge = _biases(d, slopes, n > 0)
            carry = None
            for r, b, start in _units(d):
                carry = unit(d, start, b, pi == 0, carry, edge if b == 0 else base)
        l = l_s[...]
        out = a_s[...] / l
        of_ref[...] = out
        ob_ref[...] = out.astype(BF)
        lse_ref[...] = m_s[...] + jnp.log(l)

    body, tspec, tok = _tie(body, after)
    cur = lambda c: pl.BlockSpec((SLAB, 128), lambda j, n: (n, c * N_PAIR + j))
    prv = lambda c: pl.BlockSpec((SLAB, 128), lambda j, n: (jnp.maximum(n - 1, 0), c * N_PAIR + j))
    out = pl.BlockSpec((SLAB, 128), lambda j, n: (n, j))
    return pl.pallas_call(
        body, grid=(N_PAIR, N_SLAB),
        in_specs=tspec + [pl.BlockSpec((1, 8, 128), lambda j, n: (j, 0, 0)), cur(0), cur(1), prv(1), cur(2), prv(2)],
        out_specs=[out, out, out],
        out_shape=[jax.ShapeDtypeStruct((T, AW), F32), jax.ShapeDtypeStruct((T, AW), BF),
                   jax.ShapeDtypeStruct((T, AW), F32)],
        scratch_shapes=[pltpu.VMEM((SLAB, 128), F32)] * 3,
        name=name, compiler_params=_cp())(*tok, _slope_table(), z_at, z_at, z_at, z_at, z_at)


def _attn_bwd(z_at, dout, out, lse, name):
    def body(sl_ref, q_ref, kc_ref, kp_ref, vc_ref, vp_ref, do_ref, o_ref, lse_ref, dq_ref, dk_ref, dv_ref,
             dq_s, dk_s, dv_s, ck_s, cv_s):
        step = pl.program_id(1)
        n = N_SLAB - 1 - step
        lo, hi = _pair_masks()
        slopes = (sl_ref[0, 0:1, 0:1], sl_ref[0, 1:2, 0:1])

        @pl.when(step == 0)
        def _():
            ck_s[...] = jnp.zeros_like(ck_s)
            cv_s[...] = jnp.zeros_like(cv_s)

        dk_s[...] = ck_s[...]
        dv_s[...] = cv_s[...]
        ck_s[...] = jnp.zeros_like(ck_s)
        cv_s[...] = jnp.zeros_like(cv_s)

        def add_rows(ref, start, d, val):
            _put_rows(ref, start, d, _rows(ref, start, d) + val)

        def unit(d, start, b, first, carry, bias):
            q = _rows(q_ref, start, d).astype(BF)
            do_f = _rows(do_ref, start, d)
            do = do_f.astype(BF)
            prod = do_f * _rows(o_ref, start, d)
            lse_u = _rows(lse_ref, start, d)
            kcur, vcur = _rows(kc_ref, start, d).astype(BF), _rows(vc_ref, start, d).astype(BF)
            if b > 0:
                kprev, vprev = carry
            else:
                pstart = start + SLAB - 128 * d
                kprev, vprev = _rows(kp_ref, pstart, d).astype(BF), _rows(vp_ref, pstart, d).astype(BF)
            kcat = jnp.concatenate([kprev, kcur], axis=0)
            vcat = jnp.concatenate([vprev, vcur], axis=0)
            masks = (lo, hi)
            qms = [jnp.where(msk, q, jnp.zeros_like(q)) for msk in masks]
            doms = [jnp.where(msk, do, jnp.zeros_like(do)) for msk in masks]
            deltas = [jnp.sum(jnp.where(msk, prod, 0.0), axis=1, keepdims=True) for msk in masks]
            ss = [_dg(qm, kcat, NT) * (A_HD ** -0.5) + bs for qm, bs in zip(qms, bias)]
            dps = [_dg(dom, vcat, NT) for dom in doms]
            ps = [jnp.exp(s - lse_u[:, 64 * e:64 * e + 1]) for e, s in enumerate(ss)]
            dss = [(p * (dp - delta)).astype(BF) for p, dp, delta in zip(ps, dps, deltas)]
            pbs = [p.astype(BF) for p in ps]
            dqs = [_dg(ds, kcat, NN) for ds in dss]
            dkc = (_dg(dss[0], qms[0], TN) + _dg(dss[1], qms[1], TN)) * (A_HD ** -0.5)
            dvc = _dg(pbs[0], doms[0], TN) + _dg(pbs[1], doms[1], TN)
            dq_u = jnp.where(lo, dqs[0], dqs[1]) * (A_HD ** -0.5)
            if first:
                _put_rows(dq_s, start, d, dq_u)
            else:
                add_rows(dq_s, start, d, dq_u)
            add_rows(dk_s, start, d, dkc[128:])
            add_rows(dv_s, start, d, dvc[128:])
            if b > 0:
                add_rows(dk_s, start - 128 * d, d, dkc[:128])
                add_rows(dv_s, start - 128 * d, d, dvc[:128])
            else:
                pstart = start + SLAB - 128 * d
                add_rows(ck_s, pstart, d, dkc[:128])
                add_rows(cv_s, pstart, d, dvc[:128])
            return kcur, vcur

        for pi, (_, d) in enumerate(PATTERNS):
            base, edge = _biases(d, slopes, n > 0)
            carry = None
            for r, b, start in _units(d):
                carry = unit(d, start, b, pi == 0, carry, edge if b == 0 else base)
        dq_ref[...] = dq_s[...].astype(BF)
        dk_ref[...] = dk_s[...].astype(BF)
        dv_ref[...] = dv_s[...].astype(BF)

    rev = lambda n: N_SLAB - 1 - n
    cur = lambda c: pl.BlockSpec((SLAB, 128), lambda j, n: (rev(n), c * N_PAIR + j))
    prv = lambda c: pl.BlockSpec((SLAB, 128), lambda j, n: (jnp.maximum(rev(n) - 1, 0), c * N_PAIR + j))
    one = pl.BlockSpec((SLAB, 128), lambda j, n: (rev(n), j))
    return pl.pallas_call(
        body, grid=(N_PAIR, N_SLAB),
        in_specs=[pl.BlockSpec((1, 8, 128), lambda j, n: (j, 0, 0)), cur(0), cur(1), prv(1), cur(2), prv(2),
                  one, one, one],
        out_specs=[one, one, one], out_shape=[jax.ShapeDtypeStruct((T, AW), BF)] * 3,
        scratch_shapes=[pltpu.VMEM((SLAB, 128), F32)] * 5,
        name=name, compiler_params=_cp())(_slope_table(), z_at, z_at, z_at, z_at, z_at, dout, out, lse)


def _silu(x):
    return x * jax.nn.sigmoid(x)


def _qk_math(c):
    s = _silu(c)
    return s * lax.rsqrt(jnp.sum(s * s, axis=-1, keepdims=True) + L2_EPS)


def _softplus(x):
    return jnp.maximum(x, 0.0) + jnp.log(1.0 + jnp.exp(-jnp.abs(x)))


def _gate_math(bd, alog_row, dtb_row):
    rows = bd.shape[0]
    lane = _iota(bd.shape, 1)
    beta = jax.nn.sigmoid(bd)
    g = jnp.where((lane >= DN_H) & (lane < 2 * DN_H), -jnp.exp(alog_row) * _softplus(bd + dtb_row), 0.0)
    ri = _iota((rows, rows), 0)
    ci = _iota((rows, rows), 1)
    same = (ri // CH) == (ci // CH)
    li = _iota((128, 128), 0)
    lj = _iota((128, 128), 1)
    to_next_group = jnp.where((lj == li + DN_H) & (li >= DN_H) & (li < 2 * DN_H), 1.0, 0.0)
    gc = _hdot(jnp.where(same & (ci <= ri), 1.0, 0.0), g)
    glast = _hdot(_hdot(jnp.where(same, 1.0, 0.0), g), to_next_group)
    return jnp.where(lane < DN_H, beta, 0.0) + gc + glast


def _shift_down(cur, halo, s):
    if s == 0:
        return cur
    rolled = pltpu.roll(cur, s, 0)
    hr = pltpu.roll(halo, s, 0)
    head = jnp.where(_iota(hr.shape, 0) < s, hr, rolled[:8])
    return jnp.concatenate([head, rolled[8:]], axis=0)


def _shift_up(cur, halo, s):
    if s == 0:
        return cur
    rows = cur.shape[0]
    rolled = pltpu.roll(cur, rows - s, 0)
    hr = pltpu.roll(halo, 8 - s, 0)
    tail = jnp.where(_iota(hr.shape, 0) >= 8 - s, hr, rolled[rows - 8:])
    return jnp.concatenate([rolled[:rows - 8], tail], axis=0)


def _dn_pre_fwd(z_dn, conv_w8, alog_row, dtb_row, name):
    tm = 256
    wq = 3 * DNW

    def body(raw_ref, halo_ref, bd_ref, w_ref, al_ref, dt_ref, conv_ref, qkv_ref, bg_ref):
        i = pl.program_id(0)
        cur = raw_ref[...]
        halo = jnp.where(i > 0, halo_ref[...], 0.0)
        w = w_ref[...]
        conv = jnp.zeros((tm, wq), F32)
        for j in range(4):
            conv = conv + _shift_down(cur, halo, 3 - j) * w[j:j + 1, :]
        conv_ref[...] = conv
        for blk in range(3 * DN_H):
            sl = slice(128 * blk, 128 * blk + 128)
            c = conv[:, sl]
            qkv_ref[:, sl] = _qk_math(c) if blk < 2 * DN_H else _silu(c)
        for r0 in range(0, tm, PAIR):
            rs = slice(r0, r0 + PAIR)
            bg_ref[rs, :] = _gate_math(bd_ref[rs, :], al_ref[...], dt_ref[...])

    one = pl.BlockSpec((1, 128), lambda i: (0, 0))
    return pl.pallas_call(
        body, grid=(T // tm,),
        in_specs=[pl.BlockSpec((tm, wq), lambda i: (i, 0)),
                  pl.BlockSpec((8, wq), lambda i: (jnp.maximum(i * (tm // 8) - 1, 0), 0)),
                  pl.BlockSpec((tm, 128), lambda i: (i, BD_BLK)),
                  pl.BlockSpec((8, wq), lambda i: (0, 0)), one, one],
        out_specs=[pl.BlockSpec((tm, wq), lambda i: (i, 0)), pl.BlockSpec((tm, wq), lambda i: (i, 0)),
                   pl.BlockSpec((tm, 128), lambda i: (i, 0))],
        out_shape=[jax.ShapeDtypeStruct((T, wq), F32), jax.ShapeDtypeStruct((T, wq), F32),
                   jax.ShapeDtypeStruct((T, 128), F32)],
        name=name, compiler_params=_cp())(z_dn, z_dn, z_dn, conv_w8, alog_row, dtb_row)


def _dn_pre_bwd(conv, z_dn, alog_row, dtb_row, dqn, dkn, dvn, dbg, name):
    tm = 256
    wq = 3 * DNW

    def body(conv_ref, bd_ref, al_ref, dt_ref, dq_ref, dk_ref, dv_ref, dbg_ref,
             dconv_ref, dbd_ref, dal_ref, ddt_ref):
        i = pl.program_id(0)
        for blk in range(3 * DN_H):
            sl = slice(128 * blk, 128 * blk + 128)
            src = (dq_ref, dk_ref, dv_ref)[blk // DN_H]
            ct = src[:, 128 * (blk % DN_H):128 * (blk % DN_H) + 128]
            fn = _qk_math if blk < 2 * DN_H else _silu
            _, vjp = jax.vjp(fn, conv_ref[:, sl])
            dconv_ref[:, sl] = vjp(ct)[0]
        dal = jnp.zeros((1, 128), F32)
        ddt = jnp.zeros((1, 128), F32)
        for r0 in range(0, tm, PAIR):
            rs = slice(r0, r0 + PAIR)
            _, vjp = jax.vjp(_gate_math, bd_ref[rs, :], al_ref[...], dt_ref[...])
            dbd, dal_p, ddt_p = vjp(dbg_ref[rs, :])
            dbd_ref[rs, :] = dbd.astype(BF)
            dal, ddt = dal + dal_p, ddt + ddt_p

        @pl.when(i == 0)
        def _():
            dal_ref[...] = dal
            ddt_ref[...] = ddt

        @pl.when(i > 0)
        def _():
            dal_ref[...] += dal
            ddt_ref[...] += ddt

    one = pl.BlockSpec((1, 128), lambda i: (0, 0))
    row = pl.BlockSpec((tm, wq), lambda i: (i, 0))
    hd = pl.BlockSpec((tm, DNW), lambda i: (i, 0))
    st = pl.BlockSpec((tm, 128), lambda i: (i, 0))
    return pl.pallas_call(
        body, grid=(T // tm,),
        in_specs=[row, pl.BlockSpec((tm, 128), lambda i: (i, BD_BLK)), one, one, hd, hd, hd, st],
        out_specs=[row, st, one, one],
        out_shape=[jax.ShapeDtypeStruct((T, wq), F32), jax.ShapeDtypeStruct((T, 128), BF),
                   jax.ShapeDtypeStruct((1, 128), F32), jax.ShapeDtypeStruct((1, 128), F32)],
        name=name, compiler_params=_cp())(conv, z_dn, alog_row, dtb_row, dqn, dkn, dvn, dbg)


def _dn_conv_bwd(dconv, z_dn, conv_w8, name):
    tm = 256
    wq = 3 * DNW
    last = T // tm - 1

    def body(dc_ref, dcn_ref, raw_ref, halo_ref, w_ref, draw_ref, dw_ref):
        i = pl.program_id(0)
        dc = dc_ref[...]
        nxt = jnp.where(i < last, dcn_ref[...], 0.0)
        cur = raw_ref[...]
        halo = jnp.where(i > 0, halo_ref[...], 0.0)
        w = w_ref[...]
        draw = jnp.zeros((tm, wq), F32)
        rows = []
        for j in range(4):
            draw = draw + _shift_up(dc, nxt, 3 - j) * w[j:j + 1, :]
            rows.append(jnp.sum(dc * _shift_down(cur, halo, 3 - j), axis=0, keepdims=True))
        draw_ref[...] = draw.astype(BF)
        part = jnp.concatenate(rows + [jnp.zeros((4, wq), F32)], axis=0)

        @pl.when(i == 0)
        def _():
            dw_ref[...] = part

        @pl.when(i > 0)
        def _():
            dw_ref[...] += part

    row = pl.BlockSpec((tm, wq), lambda i: (i, 0))
    return pl.pallas_call(
        body, grid=(T // tm,),
        in_specs=[row, pl.BlockSpec((8, wq), lambda i: (jnp.minimum((i + 1) * (tm // 8), T // 8 - 1), 0)),
                  row, pl.BlockSpec((8, wq), lambda i: (jnp.maximum(i * (tm // 8) - 1, 0), 0)),
                  pl.BlockSpec((8, wq), lambda i: (0, 0))],
        out_specs=[row, pl.BlockSpec((8, wq), lambda i: (0, 0))],
        out_shape=[jax.ShapeDtypeStruct((T, wq), BF), jax.ShapeDtypeStruct((8, wq), F32)],
        name=name, compiler_params=_cp())(dconv, dconv, z_dn, z_dn, conv_w8)


def _h3(a, b, dims=NN):
    return lax.dot_general(a, b, dims, precision=lax.Precision.HIGH, preferred_element_type=F32)


@jax.custom_vjp
def _inverse_given(a_mat, tinv):
    return tinv


def _inverse_given_fwd(a_mat, tinv):
    return tinv, tinv


def _inverse_given_bwd(tinv, g):
    return -_bdot_tn(tinv, _bdot_nt(g, tinv)), jnp.zeros_like(tinv)


_inverse_given.defvjp(_inverse_given_fwd, _inverse_given_bwd)


@jax.custom_vjp
def _h3_lo(a, b):
    return _h3(a, b)


def _h3_lo_fwd(a, b):
    return _h3(a, b), (a, b)


def _h3_lo_bwd(res, g):
    a, b = res
    gb = g.astype(BF)
    return _dg(gb, b.astype(BF), NT), _dg(a.astype(BF), gb, TN)


_h3_lo.defvjp(_h3_lo_fwd, _h3_lo_bwd)


def _prep_head(q, k, v, bgc, h):
    beta = _col(bgc, h)
    gc = jnp.broadcast_to(_col(bgc, DN_H + h), (PAIR, 128))
    glast = jnp.broadcast_to(_col(bgc, 2 * DN_H + h), (PAIR, 128))
    ri = _iota((PAIR, PAIR), 0)
    ci = _iota((PAIR, PAIR), 1)
    same = (ri // CH) == (ci // CH)
    causal = same & (ci <= ri)
    strict = same & (ci < ri)
    eye = ri == ci
    gc_cols = _hdot(jnp.ones((PAIR, PAIR), F32), jnp.where(eye, gc, 0.0))
    decay = jnp.exp(jnp.where(causal, gc - gc_cols, NEG))
    egc = jnp.exp(gc)
    kb = k * beta
    a_mat = jnp.where(strict, _bdot_nt(kb, k) * decay, 0.0)
    qs = q * (DN_HD ** -0.5)
    attn = jnp.where(causal, _bdot_nt(qs, k) * decay, 0.0)
    return a_mat, (v * beta, kb * egc, qs * egc, k * jnp.exp(glast - gc), attn, jnp.exp(glast))


PREP_DTYPES = (F32, BF, BF, BF, BF, F32)


def _prep_tail(tinv, ctx):
    vb, kbe, qg, kdec, attn, decb = ctx
    outs = (_h3_lo(tinv, vb), _h3_lo(tinv, kbe), qg, kdec, attn, decb)
    return tuple(o.astype(dt) for o, dt in zip(outs, PREP_DTYPES))


def _inverses(a_mats):
    eye = jnp.where(_iota((PAIR, PAIR), 0) == _iota((PAIR, PAIR), 1), 1.0, 0.0)
    ps = [-a for a in a_mats]
    tinvs = [eye + p for p in ps]
    for _ in range(5):
        ps = [_h3(p, p) for p in ps]
        tinvs = [t + _h3(t, p) for t, p in zip(tinvs, ps)]
    return tinvs


def _dn_prep_fwd(qkv, bg, name):
    rows = 1024
    hd = lambda off: pl.BlockSpec((rows, 128), lambda g, h: (g, off + h))
    out = pl.BlockSpec((rows, 128), lambda g, h: (g, h))

    def body(q_ref, k_ref, v_ref, bg_ref, *outs):
        h = pl.program_id(1)
        spans = [slice(PAIR * pr, PAIR * pr + PAIR) for pr in range(rows // PAIR)]
        heads = [_prep_head(q_ref[rs, :], k_ref[rs, :], v_ref[rs, :], bg_ref[rs, :], h) for rs in spans]
        tinvs = _inverses([a for a, _ in heads])
        for rs, tinv, (_, ctx) in zip(spans, tinvs, heads):
            for o_ref, val in zip(outs, _prep_tail(tinv, ctx) + (tinv,)):
                o_ref[rs, :] = val

    return pl.pallas_call(
        body, grid=(T // rows, DN_H),
        in_specs=[hd(0), hd(DN_H), hd(2 * DN_H), pl.BlockSpec((rows, 128), lambda g, h: (g, 0))],
        out_specs=[out] * 7, out_shape=[jax.ShapeDtypeStruct((T, DNW), dt) for dt in PREP_DTYPES + (F32,)],
        name=name, compiler_params=_cp())(qkv, qkv, qkv, bg)


def _dn_prep_bwd(qkv, bg, tinv, cts, name):
    rows = 1024
    hd = lambda off: pl.BlockSpec((rows, 128), lambda g, h: (g, off + h))
    out = pl.BlockSpec((rows, 128), lambda g, h: (g, h))
    st = pl.BlockSpec((rows, 128), lambda g, h: (g, 0))

    def body(q_ref, k_ref, v_ref, bg_ref, ti_ref, c0, c1, c2, c3, c4, c5, dq_ref, dk_ref, dv_ref, dbg_ref):
        h = pl.program_id(1)
        spans = [slice(PAIR * pr, PAIR * pr + PAIR) for pr in range(rows // PAIR)]
        tis = [ti_ref[rs, :] for rs in spans]

        def joint(qs, ks, vs, bs):
            heads = [_prep_head(q, k, v, b, h) for q, k, v, b in zip(qs, ks, vs, bs)]
            return [_prep_tail(_inverse_given(a, ti), ctx) for (a, ctx), ti in zip(heads, tis)]

        _, vjp = jax.vjp(joint, *[[r[rs, :] for rs in spans] for r in (q_ref, k_ref, v_ref, bg_ref)])
        dqs, dks, dvs, dbs = vjp([tuple(c[rs, :] for c in (c0, c1, c2, c3, c4, c5)) for rs in spans])
        for rs, dq, dk, dv in zip(spans, dqs, dks, dvs):
            dq_ref[rs, :] = dq
            dk_ref[rs, :] = dk
            dv_ref[rs, :] = dv
        dbg_all = jnp.concatenate(dbs, axis=0)

        @pl.when(h == 0)
        def _():
            dbg_ref[...] = dbg_all

        @pl.when(h > 0)
        def _():
            dbg_ref[...] += dbg_all

    return pl.pallas_call(
        body, grid=(T // rows, DN_H),
        in_specs=[hd(0), hd(DN_H), hd(2 * DN_H), st] + [out] * 7,
        out_specs=[out, out, out, st],
        out_shape=[jax.ShapeDtypeStruct((T, DNW), F32)] * 3 + [jax.ShapeDtypeStruct((T, 128), F32)],
        name=name, compiler_params=_cp())(qkv, qkv, qkv, bg, tinv, *cts)


def _step_math(ss, us, ws, qgs, kdecs, attns, decbs, sub):
    z = jnp.zeros((CH, 128), F32)
    vnews = [u - _bdot_nn(w, s) for u, w, s in zip(us, ws, ss)]
    vfulls = [jnp.concatenate([v, z] if sub == 0 else [z, v], axis=0) for v in vnews]
    os = [_bdot_nn(qg, s) + _bdot_nn(attn, vf) for qg, s, attn, vf in zip(qgs, ss, attns, vfulls)]
    decs = [jnp.sum(decb, axis=0, keepdims=True) * (1.0 / CH) for decb in decbs]
    return [s * dec + _bdot_tn(kdec, v) for s, dec, kdec, v in zip(ss, decs, kdecs, vnews)], os


SCAN_ROWS = 256


def _dn_scan_fwd(prep, name):
    nstep = T // SCAN_ROWS
    nch = SCAN_ROWS // CH
    row = pl.BlockSpec((SCAN_ROWS, DNW), lambda p: (p, 0))

    def body(u_ref, w_ref, qg_ref, kd_ref, at_ref, db_ref, o_ref, ss_ref, s_ref):
        @pl.when(pl.program_id(0) == 0)
        def _():
            s_ref[...] = jnp.zeros_like(s_ref)

        lanes = [slice(128 * h, 128 * h + 128) for h in range(DN_H)]
        states = [s_ref[h] for h in range(DN_H)]
        for ch in range(nch):
            rs = slice(CH * ch, CH * ch + CH)
            for h in range(DN_H):
                ss_ref[ch, h] = states[h]
            states, os = _step_math(states, *[[r[rs, ls] for ls in lanes]
                                              for r in (u_ref, w_ref, qg_ref, kd_ref, at_ref, db_ref)], ch % 2)
            for ls, o in zip(lanes, os):
                o_ref[rs, ls] = o
        for h in range(DN_H):
            s_ref[h] = states[h]

    return pl.pallas_call(
        body, grid=(nstep,), in_specs=[row] * 6,
        out_specs=[row, pl.BlockSpec((nch, DN_H, 128, 128), lambda p: (p, 0, 0, 0))],
        out_shape=[jax.ShapeDtypeStruct((T, DNW), F32), jax.ShapeDtypeStruct((T // CH, DN_H, 128, 128), F32)],
        scratch_shapes=[pltpu.VMEM((DN_H, 128, 128), F32)],
        name=name, compiler_params=_cp())(*prep)


def _dn_scan_bwd(prep, states, do, name):
    nstep = T // SCAN_ROWS
    nch = SCAN_ROWS // CH
    row = pl.BlockSpec((SCAN_ROWS, DNW), lambda p: (nstep - 1 - p, 0))

    def body(u_ref, w_ref, qg_ref, kd_ref, at_ref, db_ref, ss_ref, do_ref, *rest):
        outs, ds_ref = rest[:6], rest[6]

        @pl.when(pl.program_id(0) == 0)
        def _():
            ds_ref[...] = jnp.zeros_like(ds_ref)

        lanes = [slice(128 * h, 128 * h + 128) for h in range(DN_H)]
        dss = [ds_ref[h] for h in range(DN_H)]
        for ch in reversed(range(nch)):
            rs = slice(CH * ch, CH * ch + CH)
            args = [[ss_ref[ch, h] for h in range(DN_H)]] + [
                [r[rs, ls] for ls in lanes] for r in (u_ref, w_ref, qg_ref, kd_ref, at_ref, db_ref)]
            _, vjp = jax.vjp(functools.partial(_step_math, sub=ch % 2), *args)
            cts = vjp((dss, [do_ref[rs, ls] for ls in lanes]))
            dss = cts[0]
            for o_ref, vals in zip(outs, cts[1:]):
                for ls, val in zip(lanes, vals):
                    o_ref[rs, ls] = val
        for h in range(DN_H):
            ds_ref[h] = dss[h]

    return pl.pallas_call(
        body, grid=(nstep,),
        in_specs=[row] * 6 + [pl.BlockSpec((nch, DN_H, 128, 128), lambda p: (nstep - 1 - p, 0, 0, 0)), row],
        out_specs=[row] * 6, out_shape=[jax.ShapeDtypeStruct((T, DNW), dt) for dt in PREP_DTYPES],
        scratch_shapes=[pltpu.VMEM((DN_H, 128, 128), F32)],
        name=name, compiler_params=_cp())(*prep, states, do)


def _post_math(o, gate, wrow):
    return o * lax.rsqrt(jnp.mean(o * o, axis=-1, keepdims=True) + NORM_EPS) * wrow * _silu(gate)


def _dn_post_fwd(o, z_dn, dn_norm, name):
    tm = 512
    row = pl.BlockSpec((tm, DNW), lambda i: (i, 0))

    def body(o_ref, g_ref, w_ref, y_ref):
        for h in range(DN_H):
            ls = slice(128 * h, 128 * h + 128)
            y_ref[:, ls] = _post_math(o_ref[:, ls], g_ref[:, ls], w_ref[...]).astype(BF)

    return pl.pallas_call(
        body, grid=(T // tm,),
        in_specs=[row, pl.BlockSpec((tm, DNW), lambda i: (i, 3)), pl.BlockSpec((1, 128), lambda i: (0, 0))],
        out_specs=row, out_shape=jax.ShapeDtypeStruct((T, DNW), BF),
        name=name, compiler_params=_cp())(o, z_dn, dn_norm)


def _dn_post_bwd(o, z_dn, dn_norm, dy, name):
    tm = 512
    row = pl.BlockSpec((tm, DNW), lambda i: (i, 0))
    one = pl.BlockSpec((1, 128), lambda i: (0, 0))

    def body(o_ref, g_ref, w_ref, dy_ref, do_ref, dg_ref, dw_ref):
        i = pl.program_id(0)
        dw = jnp.zeros((1, 128), F32)
        for h in range(DN_H):
            ls = slice(128 * h, 128 * h + 128)
            _, vjp = jax.vjp(_post_math, o_ref[:, ls], g_ref[:, ls], w_ref[...])
            do, dg, dwh = vjp(dy_ref[:, ls].astype(F32))
            do_ref[:, ls] = do
            dg_ref[:, ls] = dg.astype(BF)
            dw = dw + dwh

        @pl.when(i == 0)
        def _():
            dw_ref[...] = dw

        @pl.when(i > 0)
        def _():
            dw_ref[...] += dw

    return pl.pallas_call(
        body, grid=(T // tm,),
        in_specs=[row, pl.BlockSpec((tm, DNW), lambda i: (i, 3)), one, pl.BlockSpec((tm, DNW), lambda i: (i, 1))],
        out_specs=[row, row, one],
        out_shape=[jax.ShapeDtypeStruct((T, DNW), F32), jax.ShapeDtypeStruct((T, DNW), BF),
                   jax.ShapeDtypeStruct((1, 128), F32)],
        name=name, compiler_params=_cp())(o, z_dn, dn_norm, dy)


HBM = pl.BlockSpec(memory_space=pltpu.HBM)
SEM = pl.BlockSpec(memory_space=pltpu.SEMAPHORE)
EFFECT = pltpu.SideEffectType.DATAFLOW_SIDE_EFFECTING
N_PEER = N_DEV - 1


ALL_PEERS = (1, 2, 4, 3, 5, 6, 7)
FIRST_HOP = (1, 2, 4, 6)
FORWARDED = (2, 4, 6)


def _peers(x, y, c, ks=ALL_PEERS):
    return [(k, (x ^ (k >> 2), y ^ ((k >> 1) & 1), c ^ (k & 1))) for k in ks]


def _exchange_copy(ins, lands, ssems, rsems, scatter, t, k, pos, me):
    px, py, pc = pos
    src = ins[t].at[4 * px + 2 * py + pc] if scatter else ins[t]
    return pltpu.make_async_remote_copy(
        src_ref=src, dst_ref=lands[t].at[me], send_sem=ssems[t].at[k - 1], recv_sem=rsems[t].at[k - 1],
        device_id=pos, device_id_type=MESH_ID)


def _xstart(bufs, scatter, name, ks=ALL_PEERS):
    nt = len(bufs)
    lands = [lax.empty((N_DEV,) + tuple(b.shape[1:] if scatter else b.shape), b.dtype) for b in bufs]

    def body(*refs):
        ins, lnd = refs[:nt], refs[nt:2 * nt]
        ssems, rsems = refs[2 * nt:3 * nt], refs[3 * nt:4 * nt]
        token = refs[-1]
        x, y, c = lax.axis_index("x"), lax.axis_index("y"), lax.axis_index("c")
        me = 4 * x + 2 * y + c
        for t in range(nt):
            for k, pos in _peers(x, y, c, ks):
                _exchange_copy(ins, lnd, ssems, rsems, scatter, t, k, pos, me).start()
        token[...] = jnp.zeros_like(token)

    both = list(bufs) + lands
    res = pl.pallas_call(
        body, name=name,
        out_shape=[pltpu.SemaphoreType.DMA((N_PEER,))] * (2 * nt)
        + [pltpu.HBM(b.shape, b.dtype) for b in both] + [jax.ShapeDtypeStruct((8, 128), F32)],
        in_specs=[HBM] * (2 * nt),
        out_specs=[SEM] * (2 * nt) + [HBM] * (2 * nt) + [pl.BlockSpec(memory_space=pltpu.VMEM)],
        input_output_aliases={i: 2 * nt + i for i in range(2 * nt)},
        compiler_params=pltpu.CompilerParams(has_side_effects=EFFECT),
    )(*[pltpu.with_memory_space_constraint(b, pltpu.HBM) for b in both])
    return res[:nt], res[nt:2 * nt], res[2 * nt:3 * nt], res[3 * nt:4 * nt], res[-1][0, 0]


def _xwait(ssems, rsems, thrus, lands, scatter, after, name, ks=ALL_PEERS):
    nt = len(lands)

    def body(*refs):
        ins, lnd = refs[:nt], refs[nt:2 * nt]
        ss, rs = refs[2 * nt:3 * nt], refs[3 * nt:4 * nt]
        x, y, c = lax.axis_index("x"), lax.axis_index("y"), lax.axis_index("c")
        me = 4 * x + 2 * y + c
        for t in range(nt):
            for k, pos in _peers(x, y, c, ks):
                cp = _exchange_copy(ins, lnd, ss, rs, scatter, t, k, pos, me)
                cp.wait_send()
                cp.wait_recv()

    both = list(thrus) + list(lands)
    res = pl.pallas_call(
        body, name=name, out_shape=[pltpu.HBM(b.shape, b.dtype) for b in both],
        in_specs=[HBM] * (2 * nt) + [SEM] * (2 * nt) + [ANY], out_specs=[HBM] * (2 * nt),
        input_output_aliases={i: i for i in range(2 * nt)},
        compiler_params=pltpu.CompilerParams(has_side_effects=EFFECT),
    )(*both, *ssems, *rsems, after)
    return res[:nt], res[nt:]


def _forward_copy(lands, ssems, rsems, t, k, pos, sibling):
    px, py, pc = pos
    slot = lands[t].at[4 * px + 2 * py + pc]
    return pltpu.make_async_remote_copy(
        src_ref=slot, dst_ref=slot, send_sem=ssems[t].at[k - 1], recv_sem=rsems[t].at[k - 1],
        device_id=sibling, device_id_type=MESH_ID)


def _fstart(lands, name):
    nt = len(lands)

    def body(*refs):
        lnd = refs[:nt]
        ssems, rsems = refs[nt:2 * nt], refs[2 * nt:3 * nt]
        token = refs[-1]
        x, y, c = lax.axis_index("x"), lax.axis_index("y"), lax.axis_index("c")
        for t in range(nt):
            for k, pos in _peers(x, y, c, FORWARDED):
                _forward_copy(lnd, ssems, rsems, t, k, pos, (x, y, c ^ 1)).start()
        token[...] = jnp.zeros_like(token)

    res = pl.pallas_call(
        body, name=name,
        out_shape=[pltpu.SemaphoreType.DMA((N_PEER,))] * (2 * nt)
        + [pltpu.HBM(b.shape, b.dtype) for b in lands] + [jax.ShapeDtypeStruct((8, 128), F32)],
        in_specs=[HBM] * nt,
        out_specs=[SEM] * (2 * nt) + [HBM] * nt + [pl.BlockSpec(memory_space=pltpu.VMEM)],
        input_output_aliases={i: 2 * nt + i for i in range(nt)},
        compiler_params=pltpu.CompilerParams(has_side_effects=EFFECT),
    )(*[pltpu.with_memory_space_constraint(b, pltpu.HBM) for b in lands])
    return res[:nt], res[nt:2 * nt], res[2 * nt:3 * nt], res[-1][0, 0]


def _fwait(ssems, rsems, lands, after, name):
    nt = len(lands)

    def body(*refs):
        lnd = refs[:nt]
        ss, rs = refs[nt:2 * nt], refs[2 * nt:3 * nt]
        x, y, c = lax.axis_index("x"), lax.axis_index("y"), lax.axis_index("c")
        for t in range(nt):
            for k, pos in _peers(x, y, c, FORWARDED):
                cp = _forward_copy(lnd, ss, rs, t, k, pos, (x, y, c ^ 1))
                cp.wait_send()
                cp.wait_recv()

    return pl.pallas_call(
        body, name=name, out_shape=[pltpu.HBM(b.shape, b.dtype) for b in lands],
        in_specs=[HBM] * nt + [SEM] * (2 * nt) + [ANY], out_specs=[HBM] * nt,
        input_output_aliases={i: i for i in range(nt)},
        compiler_params=pltpu.CompilerParams(has_side_effects=EFFECT),
    )(*lands, *ssems, *rsems, after)


def _adam(recv, w, m, v, tr, name):
    _, r, c = w.shape
    n_part = recv.shape[0]
    c1 = np.float32(1.0 - ADAM_B1 ** ADAM_STEP)
    c2 = np.float32(1.0 - ADAM_B2 ** ADAM_STEP)

    def body(r_ref, w_ref, m_ref, v_ref, g_ref, d_ref, mo_ref, vo_ref):
        g = r_ref[0].astype(F32)
        for s in range(1, n_part):
            g = g + r_ref[s].astype(F32)
        mn = ADAM_B1 * m_ref[0] + (1.0 - ADAM_B1) * g
        vn = ADAM_B2 * v_ref[0] + (1.0 - ADAM_B2) * (g * g)
        g_ref[0] = g
        mo_ref[0] = mn
        vo_ref[0] = vn
        d_ref[0] = -ADAM_LR * ((mn / c1) / (jnp.sqrt(vn / c2) + ADAM_EPS) + ADAM_WD * w_ref[0])

    one = pl.BlockSpec((1, tr, c), lambda i: (0, i, 0))
    return pl.pallas_call(
        body, grid=(r // tr,), in_specs=[pl.BlockSpec((n_part, tr, c), lambda i: (0, i, 0)), one, one, one],
        out_specs=[one] * 4, out_shape=[jax.ShapeDtypeStruct((1, r, c), F32)] * 4,
        name=name, compiler_params=_cp())(recv, w, m, v)


def _local_step(x, target, sp, need, ahead, emit):
    g = {}
    x0, h1 = x, _rms_fwd(x, sp["norm_ffn1"], "ffn1_norm")
    act1, saved1 = _ffn_up(h1, lambda kind, a: need("w" + kind + "1", a),
                           lambda kind, a: ahead("w" + kind + "1", a), "ffn1")
    x1, h2 = _mm_nn_resnorm([(act1, saved1[-1])], x0, 0.5, sp["norm_mix"], 512, "ffn1_down_norm",
                            after=ahead("win_a", act1))
    win_a, win_d = need("win_a", h2), need("win_d", h2)
    conv_w8, wout = need("conv_w8", h2), need("wout", h2)
    z_at = _mm_nn(h2, win_a, F32, 1024, 768, "mix_in_attn")
    z_dn = _mm_nn(h2, win_d, F32, 1024, 768, "mix_in_dn")

    conv, qkvn, bg = _dn_pre_fwd(z_dn, conv_w8, sp["alog_row"], sp["dtb_row"], "dn_pre")
    *prep, tinv = _dn_prep_fwd(qkvn, bg, "dn_prep")
    o_dn, states = _dn_scan_fwd(prep, "dn_scan")
    dn_b = _dn_post_fwd(o_dn, z_dn, sp["dn_norm"], "dn_post")
    attn_f, attn_b, lse = _attn_fwd(z_at, "attn_fwd", after=ahead("wg2", dn_b))

    x2, h3 = _mm_nn_resnorm([(attn_b, wout[:AW]), (dn_b, wout[AW:])], x1, 1.0, sp["norm_ffn2"], 512,
                            "mix_out_norm")
    act2, saved2 = _ffn_up(h3, lambda kind, a: need("w" + kind + "2", a),
                           lambda kind, a: ahead("w" + kind + "2", a), "ffn2")

    dx3, dys3, g["norm_final"], loss8 = _down_loss_bwd(x2, act2, saved2[-1], sp["norm_final"], target,
                                                       "ffn2_down_loss")
    dx2, dx2b, g["norm_ffn2"] = _ffn_bwd(
        x2, sp["norm_ffn2"], saved2, dx3, dys3, 1.0,
        lambda kind, dw: emit(kind + "2", {"w" + kind + "2": dw}), "ffn2b")

    zero = emit("wout", {"wout": jnp.concatenate([_mm_tn(attn_b, dx2b, BF, 512, 1024, "mix_out_dw_a"),
                                                  _mm_tn(dn_b, dx2b, BF, 512, 1024, "mix_out_dw_d")], axis=0)})
    dmix = _mm_nt(dx2b, wout, F32, 1024, 1024, "mix_out_dx", after=zero)

    dq, dk, dv = _attn_bwd(z_at, dmix, attn_f, lse, "attn_bwd")

    do_dn, dgate, g["dn_norm"] = _dn_post_bwd(o_dn, z_dn, sp["dn_norm"], dmix, "dn_post_b")
    cts = _dn_scan_bwd(prep, states, do_dn, "dn_scan_b")
    dqn, dkn, dvn, dbg = _dn_prep_bwd(qkvn, bg, tinv, cts, "dn_prep_b")
    dconv, dbd, g["alog_row"], g["dtb_row"] = _dn_pre_bwd(
        conv, z_dn, sp["alog_row"], sp["dtb_row"], dqn, dkn, dvn, dbg, "dn_pre_b")
    draw, dconv_w8 = _dn_conv_bwd(dconv, z_dn, conv_w8, "dn_conv_b")

    pieces = ((dq, win_a, 0), (dk, win_a, 1), (dv, win_a, 2),
              (draw, win_d, 0), (dgate, win_d, 3 * DNW // DNW), (dbd, win_d, BD_BLK))
    zero = emit("win", {"w_in_parts": [_mm_tn(h2, p[0], BF, 512, min(p[0].shape[1], 768), "mix_in_dw%d" % i)
                                       for i, p in enumerate(pieces)], "conv_w8": dconv_w8})
    dx1, dys1, g["norm_mix"] = _mm_rms_bwd(pieces, NT, x1, sp["norm_mix"], dx2, 0.5, 512, "mix_in_dx_dnorm",
                                           after=zero)
    dx0, _, g["norm_ffn1"] = _ffn_bwd(
        x0, sp["norm_ffn1"], saved1, dx1, dys1, 1.0,
        lambda kind, dw: emit(kind + "1", {"w" + kind + "1": dw}), "ffn1b")
    return loss8[0, 0], dx0, g


def _cols_from_shards(gathered):
    n, r, c = gathered.shape
    return jnp.transpose(gathered, (1, 0, 2)).reshape(r, n * c)


def _shards_from_cols(full, dtype):
    r, nc = full.shape
    return jnp.transpose(full.reshape(r, N_DEV, nc // N_DEV), (1, 0, 2)).astype(dtype)


def _lane_row(vec4):
    return jnp.zeros((1, 128), F32).at[:, DN_H:2 * DN_H].set(vec4.astype(F32))


WEIGHT_SOURCES = {"wg1": "gate1", "wu1": "up1", "wd1": "down1", "win_a": "w_in", "win_d": "w_in",
                  "conv_w8": "conv_w", "wout": "w_out", "wg2": "gate2", "wu2": "up2", "wd2": "down2"}
TRANSPOSED = ("gate1", "up1", "gate2", "up2")


def _build_weights(name, gath):
    if name in ("wg1", "wu1", "wd1", "wg2", "wu2", "wd2"):
        return {name: gath[WEIGHT_SOURCES[name]].reshape(F, D)}
    if name in ("win_a", "win_d"):
        w_in = _cols_from_shards(gath["w_in"])
        c0 = 3 * AW + 3 * DNW
        win_d = jnp.concatenate([w_in[:, ZA:c0], w_in[:, c0 + 2 * DN_H:], w_in[:, c0:c0 + 2 * DN_H],
                                 jnp.zeros((D, ZP - IN_COLS), w_in.dtype)], axis=1)
        return {"win_a": w_in[:, :ZA], "win_d": win_d}
    if name == "wout":
        return {name: gath["w_out"].reshape(D, D)}
    conv = _cols_from_shards(gath["conv_w"])
    return {"conv_w8": jnp.concatenate([conv, jnp.zeros((4, 3 * DNW), F32)], axis=0)}


def _small_params(norm_ffn1, norm_mix, norm_ffn2, norm_final, a_log, dt_bias, dn_norm):
    return {"norm_ffn1": norm_ffn1, "norm_mix": norm_mix, "norm_ffn2": norm_ffn2,
            "norm_final": norm_final.reshape(1, D), "alog_row": _lane_row(a_log), "dtb_row": _lane_row(dt_bias),
            "dn_norm": dn_norm}


def _grad_slabs(group, g):
    if group[0] in "gud":
        return {WEIGHT_SOURCES["w" + group]: g["w" + group].reshape(N_DEV, F // N_DEV, D)}
    if group == "wout":
        return {"w_out": g["wout"].reshape(N_DEV, D // N_DEV, D)}
    aq, ak, av, dqkv, gate, bd = g["w_in_parts"]
    g_in = jnp.concatenate([aq, ak, av, dqkv, bd[:, :2 * DN_H], gate], axis=1)
    return {"w_in": _shards_from_cols(g_in, BF), "conv_w": _shards_from_cols(g["conv_w8"][:4], F32)}


SMALL_ROWS = 40


def _small_pack(norm_ffn1, norm_mix, norm_ffn2, norm_final, dn_norm, alog_row, dtb_row, loss=None):
    rows = [a.reshape(8, 128) for a in (norm_ffn1, norm_mix, norm_ffn2, norm_final)]
    loss_row = jnp.zeros((1, 128), F32) if loss is None else jnp.broadcast_to(loss.reshape(1, 1), (1, 128))
    rows += [dn_norm.reshape(1, 128), alog_row, dtb_row, loss_row, jnp.zeros((SMALL_ROWS - 36, 128), F32)]
    return jnp.concatenate(rows, axis=0)


def _small_unpack(pk):
    pk = pk[0]
    return (pk[0:8].reshape(1, D), pk[8:16].reshape(1, D), pk[16:24].reshape(1, D), pk[24:32].reshape(D),
            pk[32:33], pk[33:34, DN_H:2 * DN_H], pk[34:35, DN_H:2 * DN_H])


ADAM_TILE = {"gate1": 256, "up1": 256, "down1": 176, "gate2": 256, "up2": 256, "down2": 176,
             "w_in": 256, "w_out": 128, "conv_w": 4}
BIG = ("gate1", "up1", "down1", "w_in", "w_out", "gate2", "up2", "down2", "conv_w")


def kernel(x, norm_ffn1, ffn1_gate, ffn1_up, ffn1_down, norm_mix, w_in, conv_w, a_log, dt_bias, dn_norm, w_out, norm_ffn2, ffn2_gate, ffn2_up, ffn2_down, norm_final, loss_target, m_norm_ffn1, m_ffn1_gate, m_ffn1_up, m_ffn1_down, m_norm_mix, m_w_in, m_conv_w, m_a_log, m_dt_bias, m_dn_norm, m_w_out, m_norm_ffn2, m_ffn2_gate, m_ffn2_up, m_ffn2_down, m_norm_final, v_norm_ffn1, v_ffn1_gate, v_ffn1_up, v_ffn1_down, v_norm_mix, v_w_in, v_conv_w, v_a_log, v_dt_bias, v_dn_norm, v_w_out, v_norm_ffn2, v_ffn2_gate, v_ffn2_up, v_ffn2_down, v_norm_final):
    w = {"gate1": ffn1_gate, "up1": ffn1_up, "down1": ffn1_down, "w_in": w_in, "w_out": w_out,
         "gate2": ffn2_gate, "up2": ffn2_up, "down2": ffn2_down, "conv_w": conv_w}
    m = {"gate1": m_ffn1_gate, "up1": m_ffn1_up, "down1": m_ffn1_down, "w_in": m_w_in, "w_out": m_w_out,
         "gate2": m_ffn2_gate, "up2": m_ffn2_up, "down2": m_ffn2_down, "conv_w": m_conv_w}
    v = {"gate1": v_ffn1_gate, "up1": v_ffn1_up, "down1": v_ffn1_down, "w_in": v_w_in, "w_out": v_w_out,
         "gate2": v_ffn2_gate, "up2": v_ffn2_up, "down2": v_ffn2_down, "conv_w": v_conv_w}

    me = 4 * lax.axis_index("x") + 2 * lax.axis_index("y") + lax.axis_index("c")
    own_slot = lambda land, mine: lax.dynamic_update_index_in_dim(land, mine, me, 0)

    ag_order = ("gate1", "up1", "down1", "w_in", "conv_w", "w_out", "gate2", "up2", "down2")
    ag_groups = (("gate1",), ("up1",), ("down1",), ("w_in", "conv_w", "w_out"), ("gate2", "up2", "down2"))
    pos = {n: i for i, n in enumerate(ag_order)}

    def shard(n):
        if n == "conv_w":
            return w[n][0]
        return (w[n][0].T if n in TRANSPOSED else w[n][0]).astype(BF)

    ss, rs, thru, land, zero = _xstart([shard(n) for n in ag_order], False, "weights_start", FIRST_HOP)
    gath, built, on_its_way = {}, {}, {}
    group_of = lambda name: [i for i, grp in enumerate(ag_groups) if WEIGHT_SOURCES[name] in grp][0]

    def ahead(name, after):
        gi = group_of(name)
        if WEIGHT_SOURCES[name] in gath or gi in on_its_way:
            return None
        ids = [pos[n] for n in ag_groups[gi]]
        thrus, lands = _xwait([ss[i] for i in ids], [rs[i] for i in ids], [thru[i] for i in ids],
                              [land[i] for i in ids], False, after, "weights_wait%d" % gi, FIRST_HOP)
        fss, frs, lands, token = _fstart(lands, "weights_forward%d" % gi)
        on_its_way[gi] = (thrus, fss, frs, lands)
        return token

    def need(name, after):
        if name not in built:
            if WEIGHT_SOURCES[name] not in gath:
                gi = group_of(name)
                ahead(name, after)
                thrus, fss, frs, lands = on_its_way.pop(gi)
                lands = _fwait(fss, frs, lands, after, "weights_forward_wait%d" % gi)
                for n, t, l in zip(ag_groups[gi], thrus, lands):
                    gath[n] = own_slot(l, t)
            built.update(_build_weights(name, gath))
        return built[name]

    pending = []

    def emit(group, grads):
        slabs = grads if group == "small" else _grad_slabs(group, grads)
        names = list(slabs)
        started = _xstart([slabs[n] for n in names], True, "grads_start_" + group)
        pending.append((group, names) + started[:4])
        return started[4]

    sp = _small_params(norm_ffn1 + zero, norm_mix, norm_ffn2, norm_final, a_log, dt_bias, dn_norm)
    loss_part, dx, g = _local_step(x[0], loss_target[0], sp, need, ahead, emit)
    small = _small_pack(g["norm_ffn1"], g["norm_mix"], g["norm_ffn2"], g["norm_final"], g["dn_norm"],
                        g["alog_row"], g["dtb_row"], loss_part)
    emit("small", {"small": jnp.broadcast_to(small[None], (N_DEV, SMALL_ROWS, 128))})

    pack = lambda a: _small_pack(*a)[None]
    res, after = {}, dx
    for group, names, gss, grs, gthru, gland in pending:
        thrus, lands = _xwait(gss, grs, gthru, gland, True, after, "grads_wait_" + group)
        for n, t, l in zip(names, thrus, lands):
            recv = own_slot(l, lax.dynamic_index_in_dim(t, me, 0, keepdims=False))
            if n == "small":
                res[n] = _adam(
                    recv,
                    pack((norm_ffn1, norm_mix, norm_ffn2, norm_final, dn_norm, _lane_row(a_log), _lane_row(dt_bias))),
                    pack((m_norm_ffn1, m_norm_mix, m_norm_ffn2, m_norm_final, m_dn_norm, _lane_row(m_a_log),
                          _lane_row(m_dt_bias))),
                    pack((v_norm_ffn1, v_norm_mix, v_norm_ffn2, v_norm_final, v_dn_norm, _lane_row(v_a_log),
                          _lane_row(v_dt_bias))),
                    SMALL_ROWS, "adam_small")
            elif n in TRANSPOSED:
                flip = lambda a: jnp.swapaxes(a, 1, 2)
                res[n] = [flip(o) for o in _adam(recv, flip(w[n]), flip(m[n]), flip(v[n]), F // N_DEV // 2,
                                                 "adam_" + n)]
            else:
                res[n] = _adam(recv, w[n], m[n], v[n], ADAM_TILE[n], "adam_" + n)
            after = res[n][0]
    res_s = res["small"]

    loss = res_s[0][0, 35, 0]
    outs = [loss, dx[None]]
    for k in range(4):
        n1, nm, n2, nf, dn, al, dt = _small_unpack(res_s[k])
        big = {n: res[n][k] for n in BIG}
        outs += [n1, big["gate1"], big["up1"], big["down1"], nm, big["w_in"], big["conv_w"], al, dt, dn,
                 big["w_out"], n2, big["gate2"], big["up2"], big["down2"], nf]
    return tuple(outs)
```

```python
import functools

import numpy as np
import jax
import jax.numpy as jnp
from jax import lax
from jax.experimental import pallas as pl
from jax.experimental.pallas import tpu as pltpu

T = 4096
D = 1024
F = 2816
N_DEV = 8
A_HEADS = 8
A_HD = 64
AW = A_HEADS * A_HD
DN_H = 4
DN_HD = 128
DNW = DN_H * DN_HD
CH = 64
PAIR = 2 * CH
ZA = 3 * AW
ZD = 3 * DNW + DNW + 256
ZP = ZA + ZD
BD_BLK = (3 * DNW + DNW) // 128
IN_COLS = 3592
PATTERNS = ((128, 1), (512, 4), (2048, 16))
NORM_EPS = 1e-6
L2_EPS = 1e-6
ADAM_LR, ADAM_B1, ADAM_B2, ADAM_EPS, ADAM_WD, ADAM_STEP = 0.001, 0.9, 0.999, 1e-08, 0.01, 10
VMEM_LIMIT = 56 * 1024 * 1024
NEG = -1e30

BF = jnp.bfloat16
F32 = jnp.float32
NN = (((1,), (0,)), ((), ()))
NT = (((1,), (1,)), ((), ()))
TN = (((0,), (0,)), ((), ()))
HI = lax.Precision.HIGHEST
MESH_ID = pl.DeviceIdType.MESH
ANY = pl.BlockSpec(memory_space=pl.ANY)


def _cp():
    return pltpu.CompilerParams(vmem_limit_bytes=VMEM_LIMIT)


def _dg(a, b, dims):
    return lax.dot_general(a, b, dims, preferred_element_type=F32)


def _hdot(a, b):
    return lax.dot_general(a, b, NN, precision=HI, preferred_element_type=F32)


def _make_bdot(dims, da_dims, da_swap, db_dims, db_swap):
    @jax.custom_vjp
    def f(a, b):
        return _dg(a.astype(BF), b.astype(BF), dims)

    def fwd(a, b):
        return f(a, b), (a, b)

    def bwd(res, g):
        a, b = res
        gb, ab, bb = g.astype(BF), a.astype(BF), b.astype(BF)
        da = _dg(bb, gb, da_dims) if da_swap else _dg(gb, bb, da_dims)
        db = _dg(gb, ab, db_dims) if db_swap else _dg(ab, gb, db_dims)
        return da.astype(a.dtype), db.astype(b.dtype)

    f.defvjp(fwd, bwd)
    return f


_bdot_nn = _make_bdot(NN, NT, False, TN, False)
_bdot_nt = _make_bdot(NT, NN, False, TN, True)
_bdot_tn = _make_bdot(TN, NT, True, NN, False)


def _iota(shape, dim):
    return lax.broadcasted_iota(jnp.int32, shape, dim)


def _col(x, idx):
    return jnp.sum(jnp.where(_iota(x.shape, 1) == idx, x, 0.0), axis=1, keepdims=True)


def _mm_nn(a, b, out_dtype, tm, tn, name):
    m, k = a.shape
    n = b.shape[1]

    def body(a_ref, b_ref, o_ref):
        o_ref[...] = _dg(a_ref[...], b_ref[...], NN).astype(out_dtype)

    return pl.pallas_call(
        body, grid=(m // tm, n // tn),
        in_specs=[pl.BlockSpec((tm, k), lambda i, j: (i, 0)), pl.BlockSpec((k, tn), lambda i, j: (0, j))],
        out_specs=pl.BlockSpec((tm, tn), lambda i, j: (i, j)),
        out_shape=jax.ShapeDtypeStruct((m, n), out_dtype), name=name, compiler_params=_cp())(a, b)


def _tie(body, after):
    if after is None:
        return body, [], []
    return (lambda tok_ref, *refs: body(*refs)), [ANY], [after.reshape(1, 1)]


def _mm_nt(a, b, out_dtype, tm, tb, name, after=None):
    m, c = a.shape
    kb = b.shape[0]

    def body(a_ref, b_ref, o_ref):
        o_ref[...] = _dg(a_ref[...], b_ref[...], NT).astype(out_dtype)

    body, tspec, tok = _tie(body, after)
    return pl.pallas_call(
        body, grid=(m // tm, kb // tb),
        in_specs=tspec + [pl.BlockSpec((tm, c), lambda i, j: (i, 0)), pl.BlockSpec((tb, c), lambda i, j: (j, 0))],
        out_specs=pl.BlockSpec((tm, tb), lambda i, j: (i, j)),
        out_shape=jax.ShapeDtypeStruct((m, kb), out_dtype), name=name, compiler_params=_cp())(*tok, a, b)


def _mm_tn(a, b, out_dtype, ta, tb, name, after=None):
    m, ka = a.shape
    nb = b.shape[1]

    def body(a_ref, b_ref, o_ref):
        o_ref[...] = _dg(a_ref[...], b_ref[...], TN).astype(out_dtype)

    body, tspec, tok = _tie(body, after)
    return pl.pallas_call(
        body, grid=(ka // ta, nb // tb),
        in_specs=tspec + [pl.BlockSpec((m, ta), lambda i, j: (0, i)), pl.BlockSpec((m, tb), lambda i, j: (0, j))],
        out_specs=pl.BlockSpec((ta, tb), lambda i, j: (i, j)),
        out_shape=jax.ShapeDtypeStruct((ka, nb), out_dtype), name=name, compiler_params=_cp())(*tok, a, b)


def _mm_nt_swiglu(h, wu_t, gate, tm, tb, name, after=None):
    m, c = h.shape
    kb = wu_t.shape[0]

    def body(h_ref, w_ref, g_ref, u_ref, a_ref):
        u = _dg(h_ref[...], w_ref[...], NT)
        g = g_ref[...].astype(F32)
        u_ref[...] = u.astype(BF)
        a_ref[...] = (g * jax.nn.sigmoid(g) * u).astype(BF)

    body, tspec, tok = _tie(body, after)
    tile = pl.BlockSpec((tm, tb), lambda i, j: (i, j))
    return pl.pallas_call(
        body, grid=(m // tm, kb // tb),
        in_specs=tspec + [pl.BlockSpec((tm, c), lambda i, j: (i, 0)), pl.BlockSpec((tb, c), lambda i, j: (j, 0)),
                          tile],
        out_specs=[tile, tile], out_shape=[jax.ShapeDtypeStruct((m, kb), BF)] * 2,
        name=name, compiler_params=_cp())(*tok, h, wu_t, gate)


def _mm_nt_dswiglu(dys, wd, gate, up, tm, tb, name, after=None):
    m, c = dys.shape
    kb = wd.shape[0]

    def body(d_ref, w_ref, g_ref, u_ref, dg_ref, du_ref):
        da = _dg(d_ref[...], w_ref[...], NT)
        g = g_ref[...].astype(F32)
        u = u_ref[...].astype(F32)
        s = jax.nn.sigmoid(g)
        dg_ref[...] = (da * u * (s * (1.0 + g * (1.0 - s)))).astype(BF)
        du_ref[...] = (da * (g * s)).astype(BF)

    body, tspec, tok = _tie(body, after)
    tile = pl.BlockSpec((tm, tb), lambda i, j: (i, j))
    return pl.pallas_call(
        body, grid=(m // tm, kb // tb),
        in_specs=tspec + [pl.BlockSpec((tm, c), lambda i, j: (i, 0)), pl.BlockSpec((tb, c), lambda i, j: (j, 0)),
                          tile, tile],
        out_specs=[tile, tile], out_shape=[jax.ShapeDtypeStruct((m, kb), BF)] * 2,
        name=name, compiler_params=_cp())(*tok, dys, wd, gate, up)


def _rms_fwd(x, gain, name):
    tm = 512
    row = pl.BlockSpec((tm, D), lambda i: (i, 0))

    def body(x_ref, g_ref, h_ref):
        xv = x_ref[...]
        r = lax.rsqrt(jnp.mean(xv * xv, axis=-1, keepdims=True) + NORM_EPS)
        h_ref[...] = (xv * r * g_ref[...]).astype(BF)

    return pl.pallas_call(
        body, grid=(T // tm,), in_specs=[row, pl.BlockSpec((1, D), lambda i: (0, 0))], out_specs=row,
        out_shape=jax.ShapeDtypeStruct((T, D), BF), name=name, compiler_params=_cp())(x, gain)


def _mm_rms_bwd(pairs, dims, x, gain, dres, alpha_out, tm, name, after=None):
    n = len(pairs)
    row = pl.BlockSpec((tm, D), lambda i: (i, 0))
    gspec = pl.BlockSpec((1, D), lambda i: (0, 0))

    def body(*refs):
        x_ref, g_ref, dres_ref, dx_ref, dxs_ref, dg_ref = refs[2 * n:]
        i = pl.program_id(0)
        dhv = _dg(refs[0][...], refs[n][...], dims)
        for p in range(1, n):
            dhv = dhv + _dg(refs[p][...], refs[n + p][...], dims)
        xv = x_ref[...]
        r = lax.rsqrt(jnp.mean(xv * xv, axis=-1, keepdims=True) + NORM_EPS)
        xh = xv * r
        part = jnp.sum(dhv * xh, axis=0, keepdims=True)

        @pl.when(i == 0)
        def _():
            dg_ref[...] = part

        @pl.when(i > 0)
        def _():
            dg_ref[...] += part

        dxh = dhv * g_ref[...]
        dx = r * (dxh - xh * jnp.mean(dxh * xh, axis=-1, keepdims=True)) + dres_ref[...]
        dx_ref[...] = dx
        dxs_ref[...] = (alpha_out * dx).astype(BF)

    body, tspec, tok = _tie(body, after)

    def bspec(a, b, kblk):
        if kblk is None:
            return pl.BlockSpec(b.shape, lambda i: (0, 0))
        return pl.BlockSpec((b.shape[0], a.shape[1]), lambda i: (0, kblk))

    return pl.pallas_call(
        body, grid=(T // tm,),
        in_specs=tspec + [pl.BlockSpec((tm, p[0].shape[1]), lambda i: (i, 0)) for p in pairs]
        + [bspec(p[0], p[1], p[2] if len(p) > 2 else None) for p in pairs] + [row, gspec, row],
        out_specs=[row, row, gspec],
        out_shape=[jax.ShapeDtypeStruct((T, D), F32), jax.ShapeDtypeStruct((T, D), BF),
                   jax.ShapeDtypeStruct((1, D), F32)],
        name=name, compiler_params=_cp())(*tok, *[p[0] for p in pairs], *[p[1] for p in pairs], x, gain, dres)


def _down_loss_bwd(x_prev, act, wd, gain, target, name):
    tm = 512
    row = pl.BlockSpec((tm, D), lambda i: (i, 0))
    gspec = pl.BlockSpec((1, D), lambda i: (0, 0))
    lspec = pl.BlockSpec((8, 128), lambda i: (0, 0))

    def body(x_ref, a_ref, w_ref, g_ref, t_ref, dx_ref, dxs_ref, dg_ref, loss_ref):
        i = pl.program_id(0)
        xv = x_ref[...] + 0.5 * _dg(a_ref[...], w_ref[...], NN)
        r = lax.rsqrt(jnp.mean(xv * xv, axis=-1, keepdims=True) + NORM_EPS)
        xh = xv * r
        diff = xh * g_ref[...] - t_ref[...]
        lpart = 0.5 * jnp.sum(jnp.mean(diff * diff, axis=-1, keepdims=True), axis=0, keepdims=True)
        dy = diff * (1.0 / D)
        part = jnp.sum(dy * xh, axis=0, keepdims=True)

        @pl.when(i == 0)
        def _():
            dg_ref[...] = part
            loss_ref[...] = jnp.broadcast_to(lpart, (8, 128))

        @pl.when(i > 0)
        def _():
            dg_ref[...] += part
            loss_ref[...] += jnp.broadcast_to(lpart, (8, 128))

        dxh = dy * g_ref[...]
        dx = r * (dxh - xh * jnp.mean(dxh * xh, axis=-1, keepdims=True))
        dx_ref[...] = dx
        dxs_ref[...] = (0.5 * dx).astype(BF)

    return pl.pallas_call(
        body, grid=(T // tm,),
        in_specs=[row, pl.BlockSpec((tm, F), lambda i: (i, 0)), pl.BlockSpec((F, D), lambda i: (0, 0)), gspec, row],
        out_specs=[row, row, gspec, lspec],
        out_shape=[jax.ShapeDtypeStruct((T, D), F32), jax.ShapeDtypeStruct((T, D), BF),
                   jax.ShapeDtypeStruct((1, D), F32), jax.ShapeDtypeStruct((8, 128), F32)],
        name=name, compiler_params=_cp())(x_prev, act, wd, gain, target)


def _mm_nn_resnorm(pairs, x_prev, alpha, gain, tm, name, after=None):
    m = x_prev.shape[0]
    n = len(pairs)

    def body(*refs):
        x_ref, g_ref, xo_ref, h_ref = refs[2 * n:]
        y = _dg(refs[0][...], refs[n][...], NN)
        for i in range(1, n):
            y = y + _dg(refs[i][...], refs[n + i][...], NN)
        xv = x_ref[...] + alpha * y
        xo_ref[...] = xv
        r = lax.rsqrt(jnp.mean(xv * xv, axis=-1, keepdims=True) + NORM_EPS)
        h_ref[...] = (xv * r * g_ref[...]).astype(BF)

    body, tspec, tok = _tie(body, after)
    row = pl.BlockSpec((tm, D), lambda i: (i, 0))
    return pl.pallas_call(
        body, grid=(m // tm,),
        in_specs=tspec + [pl.BlockSpec((tm, a.shape[1]), lambda i: (i, 0)) for a, _ in pairs]
        + [pl.BlockSpec(b.shape, lambda i: (0, 0)) for _, b in pairs] + [row, pl.BlockSpec((1, D), lambda i: (0, 0))],
        out_specs=[row, row],
        out_shape=[jax.ShapeDtypeStruct((m, D), F32), jax.ShapeDtypeStruct((m, D), BF)],
        name=name, compiler_params=_cp())(*tok, *[a for a, _ in pairs], *[b for _, b in pairs], x_prev, gain)


def _ffn_up(h, need, ahead, tag):
    wg = need("g", h)
    gate = _mm_nt(h, wg, BF, 1024, 1408, tag + "_gate")
    wu = need("u", gate)
    up, act = _mm_nt_swiglu(h, wu, gate, 1024, 1408, tag + "_up_act", after=ahead("d", wu))
    wd = need("d", up)
    return act, (h, gate, up, act, wg, wu, wd)


def _ffn_bwd(x_in, gain, saved, dxo, dys, alpha_out, emit, tag):
    h, gate, up, act, wg, wu, wd = saved
    sent = emit("d", _mm_tn(act, dys, BF, 256, 1024, tag + "_dwd"))
    dgate, dup = _mm_nt_dswiglu(dys, wd, gate, up, 1024, 1408, tag + "_dact", after=sent)
    sent = emit("g", _mm_tn(dgate, h, BF, 256, 1024, tag + "_dwg"))
    sent = emit("u", _mm_tn(dup, h, BF, 256, 1024, tag + "_dwu", after=sent))
    return _mm_rms_bwd([(dgate, wg), (dup, wu)], NN, x_in, gain, dxo, alpha_out, 512, tag + "_dh_dnorm",
                       after=sent)


SLAB = 2048
N_SLAB = T // SLAB
N_PAIR = A_HEADS // 2


def _pair_masks():
    lane = _iota((128, 128), 1)
    return lane < A_HD, lane >= A_HD


def _slope_table():
    h = 2 * jnp.arange(N_PAIR)[:, None] + jnp.minimum(jnp.arange(8), 1)[None, :]
    return jnp.broadcast_to((2.0 ** (-(h + 1).astype(F32)))[:, :, None], (N_PAIR, 8, 128))


def _rows(ref, start, d):
    if d == 1:
        return ref[pl.ds(start, 128), :]
    return ref[pl.ds(start, 128, stride=d), :]


def _put_rows(ref, start, d, val):
    if d == 1:
        ref[pl.ds(start, 128), :] = val
    else:
        ref[pl.ds(start, 128, stride=d), :] = val


def _units(d):
    return [(r, b, r + 128 * d * b) for r in range(d) for b in range(SLAB // (128 * d))]


def _biases(d, slopes, has_prev):
    qi = _iota((128, 256), 0)
    kj = _iota((128, 256), 1)
    steps = qi + 128 - kj
    in_band = (steps >= 0) & (steps <= 128)
    dist = (steps * d).astype(F32)
    base = [jnp.where(in_band, -(sl * dist), NEG) for sl in slopes]
    edge = [jnp.where(has_prev | (kj >= 128), b, NEG) for b in base]
    return base, edge


def _attn_fwd(z_at, name, after=None):
    def body(sl_ref, q_ref, kc_ref, kp_ref, vc_ref, vp_ref, of_ref, ob_ref, lse_ref, m_s, l_s, a_s):
        n = pl.program_id(1)
        lo, hi = _pair_masks()
        slopes = (sl_ref[0, 0:1, 0:1], sl_ref[0, 1:2, 0:1])

        def unit(d, start, b, first, carry, bias):
            q = (_rows(q_ref, start, d) * (A_HD ** -0.5)).astype(BF)
            kcur, vcur = _rows(kc_ref, start, d).astype(BF), _rows(vc_ref, start, d).astype(BF)
            if b > 0:
                kprev, vprev = carry
            else:
                pstart = start + SLAB - 128 * d
                kprev, vprev = _rows(kp_ref, pstart, d).astype(BF), _rows(vp_ref, pstart, d).astype(BF)
            kcat = jnp.concatenate([kprev, kcur], axis=0)
            vcat = jnp.concatenate([vprev, vcur], axis=0)
            ms, ls, pvs = [], [], []
            for e in range(2):
                qm = jnp.where(lo if e == 0 else hi, q, jnp.zeros_like(q))
                s = _dg(qm, kcat, NT) + bias[e]
                m = jnp.max(s, axis=1, keepdims=True)
                p = jnp.exp(s - m)
                ms.append(m)
                ls.append(jnp.sum(p, axis=1, keepdims=True))
                pvs.append(_dg(p.astype(BF), vcat, NN))
            m_u = jnp.where(lo, ms[0], ms[1])
            l_u = jnp.where(lo, ls[0], ls[1])
            a_u = jnp.where(lo, pvs[0], pvs[1])
            if first:
                m_n, l_n, a_n = m_u, l_u, a_u
            else:
                m_o = _rows(m_s, start, d)
                m_n = jnp.maximum(m_o, m_u)
                c_o = jnp.exp(m_o - m_n)
                c_u = jnp.exp(m_u - m_n)
                l_n = _rows(l_s, start, d) * c_o + l_u * c_u
                a_n = _rows(a_s, start, d) * c_o + a_u * c_u
            _put_rows(m_s, start, d, m_n)
            _put_rows(l_s, start, d, l_n)
            _put_rows(a_s, start, d, a_n)
            return kcur, vcur

        for pi, (_, d) in enumerate(PATTERNS):
            base, edge = _biases(d, slopes, n > 0)
            carry = None
            for r, b, start in _units(d):
                carry = unit(d, start, b, pi == 0, carry, edge if b == 0 else base)
        l = l_s[...]
        out = a_s[...] / l
        of_ref[...] = out
        ob_ref[...] = out.astype(BF)
        lse_ref[...] = m_s[...] + jnp.log(l)

    body, tspec, tok = _tie(body, after)
    cur = lambda c: pl.BlockSpec((SLAB, 128), lambda j, n: (n, c * N_PAIR + j))
    prv = lambda c: pl.BlockSpec((SLAB, 128), lambda j, n: (jnp.maximum(n - 1, 0), c * N_PAIR + j))
    out = pl.BlockSpec((SLAB, 128), lambda j, n: (n, j))
    return pl.pallas_call(
        body, grid=(N_PAIR, N_SLAB),
        in_specs=tspec + [pl.BlockSpec((1, 8, 128), lambda j, n: (j, 0, 0)), cur(0), cur(1), prv(1), cur(2), prv(2)],
        out_specs=[out, out, out],
        out_shape=[jax.ShapeDtypeStruct((T, AW), F32), jax.ShapeDtypeStruct((T, AW), BF),
                   jax.ShapeDtypeStruct((T, AW), F32)],
        scratch_shapes=[pltpu.VMEM((SLAB, 128), F32)] * 3,
        name=name, compiler_params=_cp())(*tok, _slope_table(), z_at, z_at, z_at, z_at, z_at)


def _attn_bwd(z_at, dout, out, lse, name):
    def body(sl_ref, q_ref, kc_ref, kp_ref, vc_ref, vp_ref, do_ref, o_ref, lse_ref, dq_ref, dk_ref, dv_ref,
             dq_s, dk_s, dv_s, ck_s, cv_s):
        step = pl.program_id(1)
        n = N_SLAB - 1 - step
        lo, hi = _pair_masks()
        slopes = (sl_ref[0, 0:1, 0:1], sl_ref[0, 1:2, 0:1])

        @pl.when(step == 0)
        def _():
            ck_s[...] = jnp.zeros_like(ck_s)
            cv_s[...] = jnp.zeros_like(cv_s)

        dk_s[...] = ck_s[...]
        dv_s[...] = cv_s[...]
        ck_s[...] = jnp.zeros_like(ck_s)
        cv_s[...] = jnp.zeros_like(cv_s)

        def add_rows(ref, start, d, val):
            _put_rows(ref, start, d, _rows(ref, start, d) + val)

        def unit(d, start, b, first, carry, bias):
            q = (_rows(q_ref, start, d) * (A_HD ** -0.5)).astype(BF)
            do_f = _rows(do_ref, start, d)
            do = do_f.astype(BF)
            prod = do_f * _rows(o_ref, start, d)
            lse_u = _rows(lse_ref, start, d)
            kcur, vcur = _rows(kc_ref, start, d).astype(BF), _rows(vc_ref, start, d).astype(BF)
            if b > 0:
                kprev, vprev = carry
            else:
                pstart = start + SLAB - 128 * d
                kprev, vprev = _rows(kp_ref, pstart, d).astype(BF), _rows(vp_ref, pstart, d).astype(BF)
            kcat = jnp.concatenate([kprev, kcur], axis=0)
            vcat = jnp.concatenate([vprev, vcur], axis=0)
            masks = (lo, hi)
            qms = [jnp.where(msk, q, jnp.zeros_like(q)) for msk in masks]
            doms = [jnp.where(msk, do, jnp.zeros_like(do)) for msk in masks]
            deltas = [jnp.sum(jnp.where(msk, prod, 0.0), axis=1, keepdims=True) for msk in masks]
            ss = [_dg(qm, kcat, NT) + bs for qm, bs in zip(qms, bias)]
            dps = [_dg(dom, vcat, NT) for dom in doms]
            ps = [jnp.exp(s - lse_u[:, 64 * e:64 * e + 1]) for e, s in enumerate(ss)]
            dss = [(p * (dp - delta)).astype(BF) for p, dp, delta in zip(ps, dps, deltas)]
            pbs = [p.astype(BF) for p in ps]
            dqs = [_dg(ds, kcat, NN) for ds in dss]
            dkc = _dg(dss[0], qms[0], TN) + _dg(dss[1], qms[1], TN)
            dvc = _dg(pbs[0], doms[0], TN) + _dg(pbs[1], doms[1], TN)
            dq_u = jnp.where(lo, dqs[0], dqs[1]) * (A_HD ** -0.5)
            if first:
                _put_rows(dq_s, start, d, dq_u)
            else:
                add_rows(dq_s, start, d, dq_u)
            add_rows(dk_s, start, d, dkc[128:])
            add_rows(dv_s, start, d, dvc[128:])
            if b > 0:
                add_rows(dk_s, start - 128 * d, d, dkc[:128])
                add_rows(dv_s, start - 128 * d, d, dvc[:128])
            else:
                pstart = start + SLAB - 128 * d
                add_rows(ck_s, pstart, d, dkc[:128])
                add_rows(cv_s, pstart, d, dvc[:128])
            return kcur, vcur

        for pi, (_, d) in enumerate(PATTERNS):
            base, edge = _biases(d, slopes, n > 0)
            carry = None
            for r, b, start in _units(d):
                carry = unit(d, start, b, pi == 0, carry, edge if b == 0 else base)
        dq_ref[...] = dq_s[...].astype(BF)
        dk_ref[...] = dk_s[...].astype(BF)
        dv_ref[...] = dv_s[...].astype(BF)

    rev = lambda n: N_SLAB - 1 - n
    cur = lambda c: pl.BlockSpec((SLAB, 128), lambda j, n: (rev(n), c * N_PAIR + j))
    prv = lambda c: pl.BlockSpec((SLAB, 128), lambda j, n: (jnp.maximum(rev(n) - 1, 0), c * N_PAIR + j))
    one = pl.BlockSpec((SLAB, 128), lambda j, n: (rev(n), j))
    return pl.pallas_call(
        body, grid=(N_PAIR, N_SLAB),
        in_specs=[pl.BlockSpec((1, 8, 128), lambda j, n: (j, 0, 0)), cur(0), cur(1), prv(1), cur(2), prv(2),
                  one, one, one],
        out_specs=[one, one, one], out_shape=[jax.ShapeDtypeStruct((T, AW), BF)] * 3,
        scratch_shapes=[pltpu.VMEM((SLAB, 128), F32)] * 5,
        name=name, compiler_params=_cp())(_slope_table(), z_at, z_at, z_at, z_at, z_at, dout, out, lse)


def _silu(x):
    return x * jax.nn.sigmoid(x)


def _qk_math(c):
    s = _silu(c)
    return s * lax.rsqrt(jnp.sum(s * s, axis=-1, keepdims=True) + L2_EPS)


def _softplus(x):
    return jnp.maximum(x, 0.0) + jnp.log(1.0 + jnp.exp(-jnp.abs(x)))


def _gate_math(bd, alog_row, dtb_row):
    rows = bd.shape[0]
    lane = _iota(bd.shape, 1)
    beta = jax.nn.sigmoid(bd)
    g = jnp.where((lane >= DN_H) & (lane < 2 * DN_H), -jnp.exp(alog_row) * _softplus(bd + dtb_row), 0.0)
    ri = _iota((rows, rows), 0)
    ci = _iota((rows, rows), 1)
    same = (ri // CH) == (ci // CH)
    li = _iota((128, 128), 0)
    lj = _iota((128, 128), 1)
    to_next_group = jnp.where((lj == li + DN_H) & (li >= DN_H) & (li < 2 * DN_H), 1.0, 0.0)
    gc = _hdot(jnp.where(same & (ci <= ri), 1.0, 0.0), g)
    glast = _hdot(_hdot(jnp.where(same, 1.0, 0.0), g), to_next_group)
    return jnp.where(lane < DN_H, beta, 0.0) + gc + glast


def _shift_down(cur, halo, s):
    if s == 0:
        return cur
    rolled = pltpu.roll(cur, s, 0)
    hr = pltpu.roll(halo, s, 0)
    head = jnp.where(_iota(hr.shape, 0) < s, hr, rolled[:8])
    return jnp.concatenate([head, rolled[8:]], axis=0)


def _shift_up(cur, halo, s):
    if s == 0:
        return cur
    rows = cur.shape[0]
    rolled = pltpu.roll(cur, rows - s, 0)
    hr = pltpu.roll(halo, 8 - s, 0)
    tail = jnp.where(_iota(hr.shape, 0) >= 8 - s, hr, rolled[rows - 8:])
    return jnp.concatenate([rolled[:rows - 8], tail], axis=0)


def _dn_pre_fwd(z_dn, conv_w8, alog_row, dtb_row, name):
    tm = 512
    wq = 3 * DNW

    def body(raw_ref, halo_ref, bd_ref, w_ref, al_ref, dt_ref, conv_ref, qkv_ref, bg_ref):
        i = pl.program_id(0)
        cur = raw_ref[...]
        halo = jnp.where(i > 0, halo_ref[...], 0.0)
        w = w_ref[...]
        conv = jnp.zeros((tm, wq), F32)
        for j in range(4):
            conv = conv + _shift_down(cur, halo, 3 - j) * w[j:j + 1, :]
        conv_ref[...] = conv
        for blk in range(3 * DN_H):
            sl = slice(128 * blk, 128 * blk + 128)
            c = conv[:, sl]
            qkv_ref[:, sl] = _qk_math(c) if blk < 2 * DN_H else _silu(c)
        for r0 in range(0, tm, PAIR):
            rs = slice(r0, r0 + PAIR)
            bg_ref[rs, :] = _gate_math(bd_ref[rs, :], al_ref[...], dt_ref[...])

    one = pl.BlockSpec((1, 128), lambda i: (0, 0))
    return pl.pallas_call(
        body, grid=(T // tm,),
        in_specs=[pl.BlockSpec((tm, wq), lambda i: (i, 0)),
                  pl.BlockSpec((8, wq), lambda i: (jnp.maximum(i * (tm // 8) - 1, 0), 0)),
                  pl.BlockSpec((tm, 128), lambda i: (i, BD_BLK)),
                  pl.BlockSpec((8, wq), lambda i: (0, 0)), one, one],
        out_specs=[pl.BlockSpec((tm, wq), lambda i: (i, 0)), pl.BlockSpec((tm, wq), lambda i: (i, 0)),
                   pl.BlockSpec((tm, 128), lambda i: (i, 0))],
        out_shape=[jax.ShapeDtypeStruct((T, wq), F32), jax.ShapeDtypeStruct((T, wq), F32),
                   jax.ShapeDtypeStruct((T, 128), F32)],
        name=name, compiler_params=_cp())(z_dn, z_dn, z_dn, conv_w8, alog_row, dtb_row)


def _dn_pre_bwd(conv, z_dn, alog_row, dtb_row, dqn, dkn, dvn, dbg, name):
    tm = 512
    wq = 3 * DNW

    def body(conv_ref, bd_ref, al_ref, dt_ref, dq_ref, dk_ref, dv_ref, dbg_ref,
             dconv_ref, dbd_ref, dal_ref, ddt_ref):
        i = pl.program_id(0)
        for blk in range(3 * DN_H):
            sl = slice(128 * blk, 128 * blk + 128)
            src = (dq_ref, dk_ref, dv_ref)[blk // DN_H]
            ct = src[:, 128 * (blk % DN_H):128 * (blk % DN_H) + 128]
            fn = _qk_math if blk < 2 * DN_H else _silu
            _, vjp = jax.vjp(fn, conv_ref[:, sl])
            dconv_ref[:, sl] = vjp(ct)[0]
        dal = jnp.zeros((1, 128), F32)
        ddt = jnp.zeros((1, 128), F32)
        for r0 in range(0, tm, PAIR):
            rs = slice(r0, r0 + PAIR)
            _, vjp = jax.vjp(_gate_math, bd_ref[rs, :], al_ref[...], dt_ref[...])
            dbd, dal_p, ddt_p = vjp(dbg_ref[rs, :])
            dbd_ref[rs, :] = dbd.astype(BF)
            dal, ddt = dal + dal_p, ddt + ddt_p

        @pl.when(i == 0)
        def _():
            dal_ref[...] = dal
            ddt_ref[...] = ddt

        @pl.when(i > 0)
        def _():
            dal_ref[...] += dal
            ddt_ref[...] += ddt

    one = pl.BlockSpec((1, 128), lambda i: (0, 0))
    row = pl.BlockSpec((tm, wq), lambda i: (i, 0))
    hd = pl.BlockSpec((tm, DNW), lambda i: (i, 0))
    st = pl.BlockSpec((tm, 128), lambda i: (i, 0))
    return pl.pallas_call(
        body, grid=(T // tm,),
        in_specs=[row, pl.BlockSpec((tm, 128), lambda i: (i, BD_BLK)), one, one, hd, hd, hd, st],
        out_specs=[row, st, one, one],
        out_shape=[jax.ShapeDtypeStruct((T, wq), F32), jax.ShapeDtypeStruct((T, 128), BF),
                   jax.ShapeDtypeStruct((1, 128), F32), jax.ShapeDtypeStruct((1, 128), F32)],
        name=name, compiler_params=_cp())(conv, z_dn, alog_row, dtb_row, dqn, dkn, dvn, dbg)


def _dn_conv_bwd(dconv, z_dn, conv_w8, name):
    tm = 512
    wq = 3 * DNW
    last = T // tm - 1

    def body(dc_ref, dcn_ref, raw_ref, halo_ref, w_ref, draw_ref, dw_ref):
        i = pl.program_id(0)
        dc = dc_ref[...]
        nxt = jnp.where(i < last, dcn_ref[...], 0.0)
        cur = raw_ref[...]
        halo = jnp.where(i > 0, halo_ref[...], 0.0)
        w = w_ref[...]
        draw = jnp.zeros((tm, wq), F32)
        rows = []
        for j in range(4):
            draw = draw + _shift_up(dc, nxt, 3 - j) * w[j:j + 1, :]
            rows.append(jnp.sum(dc * _shift_down(cur, halo, 3 - j), axis=0, keepdims=True))
        draw_ref[...] = draw.astype(BF)
        part = jnp.concatenate(rows + [jnp.zeros((4, wq), F32)], axis=0)

        @pl.when(i == 0)
        def _():
            dw_ref[...] = part

        @pl.when(i > 0)
        def _():
            dw_ref[...] += part

    row = pl.BlockSpec((tm, wq), lambda i: (i, 0))
    return pl.pallas_call(
        body, grid=(T // tm,),
        in_specs=[row, pl.BlockSpec((8, wq), lambda i: (jnp.minimum((i + 1) * (tm // 8), T // 8 - 1), 0)),
                  row, pl.BlockSpec((8, wq), lambda i: (jnp.maximum(i * (tm // 8) - 1, 0), 0)),
                  pl.BlockSpec((8, wq), lambda i: (0, 0))],
        out_specs=[row, pl.BlockSpec((8, wq), lambda i: (0, 0))],
        out_shape=[jax.ShapeDtypeStruct((T, wq), BF), jax.ShapeDtypeStruct((8, wq), F32)],
        name=name, compiler_params=_cp())(dconv, dconv, z_dn, z_dn, conv_w8)


def _h3(a, b, dims=NN):
    return lax.dot_general(a, b, dims, precision=lax.Precision.HIGH, preferred_element_type=F32)


@jax.custom_vjp
def _inverse_given(a_mat, tinv):
    return tinv


def _inverse_given_fwd(a_mat, tinv):
    return tinv, tinv


def _inverse_given_bwd(tinv, g):
    return -_bdot_tn(tinv, _bdot_nt(g, tinv)), jnp.zeros_like(tinv)


_inverse_given.defvjp(_inverse_given_fwd, _inverse_given_bwd)


@jax.custom_vjp
def _h3_lo(a, b):
    return _h3(a, b)


def _h3_lo_fwd(a, b):
    return _h3(a, b), (a, b)


def _h3_lo_bwd(res, g):
    a, b = res
    gb = g.astype(BF)
    return _dg(gb, b.astype(BF), NT), _dg(a.astype(BF), gb, TN)


_h3_lo.defvjp(_h3_lo_fwd, _h3_lo_bwd)


def _prep_head(q, k, v, bgc, h):
    beta = _col(bgc, h)
    gc = jnp.broadcast_to(_col(bgc, DN_H + h), (PAIR, 128))
    glast = jnp.broadcast_to(_col(bgc, 2 * DN_H + h), (PAIR, 128))
    ri = _iota((PAIR, PAIR), 0)
    ci = _iota((PAIR, PAIR), 1)
    same = (ri // CH) == (ci // CH)
    causal = same & (ci <= ri)
    strict = same & (ci < ri)
    eye = ri == ci
    gc_cols = _hdot(jnp.ones((PAIR, PAIR), F32), jnp.where(eye, gc, 0.0))
    decay = jnp.exp(jnp.where(causal, gc - gc_cols, NEG))
    egc = jnp.exp(gc)
    kb = k * beta
    a_mat = jnp.where(strict, _bdot_nt(kb, k) * decay, 0.0)
    qs = q * (DN_HD ** -0.5)
    attn = jnp.where(causal, _bdot_nt(qs, k) * decay, 0.0)
    return a_mat, (v * beta, kb * egc, qs * egc, k * jnp.exp(glast - gc), attn, jnp.exp(glast))


PREP_DTYPES = (F32, BF, BF, BF, BF, F32)


def _prep_tail(tinv, ctx):
    vb, kbe, qg, kdec, attn, decb = ctx
    outs = (_h3_lo(tinv, vb), _h3_lo(tinv, kbe), qg, kdec, attn, decb)
    return tuple(o.astype(dt) for o, dt in zip(outs, PREP_DTYPES))


def _inverses(a_mats):
    eye = jnp.where(_iota((PAIR, PAIR), 0) == _iota((PAIR, PAIR), 1), 1.0, 0.0)
    ps = [-a for a in a_mats]
    tinvs = [eye + p for p in ps]
    for _ in range(5):
        ps = [_h3(p, p) for p in ps]
        tinvs = [t + _h3(t, p) for t, p in zip(tinvs, ps)]
    return tinvs


def _dn_prep_fwd(qkv, bg, name):
    rows = 1024
    hd = lambda off: pl.BlockSpec((rows, 128), lambda g, h: (g, off + h))
    out = pl.BlockSpec((rows, 128), lambda g, h: (g, h))

    def body(q_ref, k_ref, v_ref, bg_ref, *outs):
        h = pl.program_id(1)
        spans = [slice(PAIR * pr, PAIR * pr + PAIR) for pr in range(rows // PAIR)]
        heads = [_prep_head(q_ref[rs, :], k_ref[rs, :], v_ref[rs, :], bg_ref[rs, :], h) for rs in spans]
        tinvs = _inverses([a for a, _ in heads])
        for rs, tinv, (_, ctx) in zip(spans, tinvs, heads):
            for o_ref, val in zip(outs, _prep_tail(tinv, ctx) + (tinv,)):
                o_ref[rs, :] = val

    return pl.pallas_call(
        body, grid=(T // rows, DN_H),
        in_specs=[hd(0), hd(DN_H), hd(2 * DN_H), pl.BlockSpec((rows, 128), lambda g, h: (g, 0))],
        out_specs=[out] * 7, out_shape=[jax.ShapeDtypeStruct((T, DNW), dt) for dt in PREP_DTYPES + (F32,)],
        name=name, compiler_params=_cp())(qkv, qkv, qkv, bg)


def _dn_prep_bwd(qkv, bg, tinv, cts, name):
    rows = 1024
    hd = lambda off: pl.BlockSpec((rows, 128), lambda g, h: (g, off + h))
    out = pl.BlockSpec((rows, 128), lambda g, h: (g, h))
    st = pl.BlockSpec((rows, 128), lambda g, h: (g, 0))

    def body(q_ref, k_ref, v_ref, bg_ref, ti_ref, c0, c1, c2, c3, c4, c5, dq_ref, dk_ref, dv_ref, dbg_ref):
        h = pl.program_id(1)
        spans = [slice(PAIR * pr, PAIR * pr + PAIR) for pr in range(rows // PAIR)]
        tis = [ti_ref[rs, :] for rs in spans]

        def joint(qs, ks, vs, bs):
            heads = [_prep_head(q, k, v, b, h) for q, k, v, b in zip(qs, ks, vs, bs)]
            return [_prep_tail(_inverse_given(a, ti), ctx) for (a, ctx), ti in zip(heads, tis)]

        _, vjp = jax.vjp(joint, *[[r[rs, :] for rs in spans] for r in (q_ref, k_ref, v_ref, bg_ref)])
        dqs, dks, dvs, dbs = vjp([tuple(c[rs, :] for c in (c0, c1, c2, c3, c4, c5)) for rs in spans])
        for rs, dq, dk, dv in zip(spans, dqs, dks, dvs):
            dq_ref[rs, :] = dq
            dk_ref[rs, :] = dk
            dv_ref[rs, :] = dv
        dbg_all = jnp.concatenate(dbs, axis=0)

        @pl.when(h == 0)
        def _():
            dbg_ref[...] = dbg_all

        @pl.when(h > 0)
        def _():
            dbg_ref[...] += dbg_all

    return pl.pallas_call(
        body, grid=(T // rows, DN_H),
        in_specs=[hd(0), hd(DN_H), hd(2 * DN_H), st] + [out] * 7,
        out_specs=[out, out, out, st],
        out_shape=[jax.ShapeDtypeStruct((T, DNW), F32)] * 3 + [jax.ShapeDtypeStruct((T, 128), F32)],
        name=name, compiler_params=_cp())(qkv, qkv, qkv, bg, tinv, *cts)


def _step_math(ss, us, ws, qgs, kdecs, attns, decbs, sub):
    z = jnp.zeros((CH, 128), F32)
    vnews = [u - _bdot_nn(w, s) for u, w, s in zip(us, ws, ss)]
    vfulls = [jnp.concatenate([v, z] if sub == 0 else [z, v], axis=0) for v in vnews]
    os = [_bdot_nn(qg, s) + _bdot_nn(attn, vf) for qg, s, attn, vf in zip(qgs, ss, attns, vfulls)]
    decs = [jnp.sum(decb, axis=0, keepdims=True) * (1.0 / CH) for decb in decbs]
    return [s * dec + _bdot_tn(kdec, v) for s, dec, kdec, v in zip(ss, decs, kdecs, vnews)], os


SCAN_ROWS = 256


def _dn_scan_fwd(prep, name):
    nstep = T // SCAN_ROWS
    nch = SCAN_ROWS // CH
    row = pl.BlockSpec((SCAN_ROWS, DNW), lambda p: (p, 0))

    def body(u_ref, w_ref, qg_ref, kd_ref, at_ref, db_ref, o_ref, ss_ref, s_ref):
        @pl.when(pl.program_id(0) == 0)
        def _():
            s_ref[...] = jnp.zeros_like(s_ref)

        lanes = [slice(128 * h, 128 * h + 128) for h in range(DN_H)]
        states = [s_ref[h] for h in range(DN_H)]
        for ch in range(nch):
            rs = slice(CH * ch, CH * ch + CH)
            for h in range(DN_H):
                ss_ref[ch, h] = states[h]
            states, os = _step_math(states, *[[r[rs, ls] for ls in lanes]
                                              for r in (u_ref, w_ref, qg_ref, kd_ref, at_ref, db_ref)], ch % 2)
            for ls, o in zip(lanes, os):
                o_ref[rs, ls] = o
        for h in range(DN_H):
            s_ref[h] = states[h]

    return pl.pallas_call(
        body, grid=(nstep,), in_specs=[row] * 6,
        out_specs=[row, pl.BlockSpec((nch, DN_H, 128, 128), lambda p: (p, 0, 0, 0))],
        out_shape=[jax.ShapeDtypeStruct((T, DNW), F32), jax.ShapeDtypeStruct((T // CH, DN_H, 128, 128), F32)],
        scratch_shapes=[pltpu.VMEM((DN_H, 128, 128), F32)],
        name=name, compiler_params=_cp())(*prep)


def _dn_scan_bwd(prep, states, do, name):
    nstep = T // SCAN_ROWS
    nch = SCAN_ROWS // CH
    row = pl.BlockSpec((SCAN_ROWS, DNW), lambda p: (nstep - 1 - p, 0))

    def body(u_ref, w_ref, qg_ref, kd_ref, at_ref, db_ref, ss_ref, do_ref, *rest):
        outs, ds_ref = rest[:6], rest[6]

        @pl.when(pl.program_id(0) == 0)
        def _():
            ds_ref[...] = jnp.zeros_like(ds_ref)

        lanes = [slice(128 * h, 128 * h + 128) for h in range(DN_H)]
        dss = [ds_ref[h] for h in range(DN_H)]
        for ch in reversed(range(nch)):
            rs = slice(CH * ch, CH * ch + CH)
            args = [[ss_ref[ch, h] for h in range(DN_H)]] + [
                [r[rs, ls] for ls in lanes] for r in (u_ref, w_ref, qg_ref, kd_ref, at_ref, db_ref)]
            _, vjp = jax.vjp(functools.partial(_step_math, sub=ch % 2), *args)
            cts = vjp((dss, [do_ref[rs, ls] for ls in lanes]))
            dss = cts[0]
            for o_ref, vals in zip(outs, cts[1:]):
                for ls, val in zip(lanes, vals):
                    o_ref[rs, ls] = val
        for h in range(DN_H):
            ds_ref[h] = dss[h]

    return pl.pallas_call(
        body, grid=(nstep,),
        in_specs=[row] * 6 + [pl.BlockSpec((nch, DN_H, 128, 128), lambda p: (nstep - 1 - p, 0, 0, 0)), row],
        out_specs=[row] * 6, out_shape=[jax.ShapeDtypeStruct((T, DNW), dt) for dt in PREP_DTYPES],
        scratch_shapes=[pltpu.VMEM((DN_H, 128, 128), F32)],
        name=name, compiler_params=_cp())(*prep, states, do)


def _post_math(o, gate, wrow):
    return o * lax.rsqrt(jnp.mean(o * o, axis=-1, keepdims=True) + NORM_EPS) * wrow * _silu(gate)


def _dn_post_fwd(o, z_dn, dn_norm, name):
    tm = 512
    row = pl.BlockSpec((tm, DNW), lambda i: (i, 0))

    def body(o_ref, g_ref, w_ref, y_ref):
        for h in range(DN_H):
            ls = slice(128 * h, 128 * h + 128)
            y_ref[:, ls] = _post_math(o_ref[:, ls], g_ref[:, ls], w_ref[...]).astype(BF)

    return pl.pallas_call(
        body, grid=(T // tm,),
        in_specs=[row, pl.BlockSpec((tm, DNW), lambda i: (i, 3)), pl.BlockSpec((1, 128), lambda i: (0, 0))],
        out_specs=row, out_shape=jax.ShapeDtypeStruct((T, DNW), BF),
        name=name, compiler_params=_cp())(o, z_dn, dn_norm)


def _dn_post_bwd(o, z_dn, dn_norm, dy, name):
    tm = 512
    row = pl.BlockSpec((tm, DNW), lambda i: (i, 0))
    one = pl.BlockSpec((1, 128), lambda i: (0, 0))

    def body(o_ref, g_ref, w_ref, dy_ref, do_ref, dg_ref, dw_ref):
        i = pl.program_id(0)
        dw = jnp.zeros((1, 128), F32)
        for h in range(DN_H):
            ls = slice(128 * h, 128 * h + 128)
            _, vjp = jax.vjp(_post_math, o_ref[:, ls], g_ref[:, ls], w_ref[...])
            do, dg, dwh = vjp(dy_ref[:, ls].astype(F32))
            do_ref[:, ls] = do
            dg_ref[:, ls] = dg.astype(BF)
            dw = dw + dwh

        @pl.when(i == 0)
        def _():
            dw_ref[...] = dw

        @pl.when(i > 0)
        def _():
            dw_ref[...] += dw

    return pl.pallas_call(
        body, grid=(T // tm,),
        in_specs=[row, pl.BlockSpec((tm, DNW), lambda i: (i, 3)), one, pl.BlockSpec((tm, DNW), lambda i: (i, 1))],
        out_specs=[row, row, one],
        out_shape=[jax.ShapeDtypeStruct((T, DNW), F32), jax.ShapeDtypeStruct((T, DNW), BF),
                   jax.ShapeDtypeStruct((1, 128), F32)],
        name=name, compiler_params=_cp())(o, z_dn, dn_norm, dy)


HBM = pl.BlockSpec(memory_space=pltpu.HBM)
SEM = pl.BlockSpec(memory_space=pltpu.SEMAPHORE)
EFFECT = pltpu.SideEffectType.DATAFLOW_SIDE_EFFECTING
N_PEER = N_DEV - 1


ALL_PEERS = (1, 2, 4, 3, 5, 6, 7)
FIRST_HOP = (1, 2, 4, 6)
FORWARDED = (2, 4, 6)


def _peers(x, y, c, ks=ALL_PEERS):
    return [(k, (x ^ (k >> 2), y ^ ((k >> 1) & 1), c ^ (k & 1))) for k in ks]


def _exchange_copy(ins, lands, ssems, rsems, scatter, t, k, pos, me):
    px, py, pc = pos
    src = ins[t].at[4 * px + 2 * py + pc] if scatter else ins[t]
    return pltpu.make_async_remote_copy(
        src_ref=src, dst_ref=lands[t].at[me], send_sem=ssems[t].at[k - 1], recv_sem=rsems[t].at[k - 1],
        device_id=pos, device_id_type=MESH_ID)


def _xstart(bufs, scatter, name, ks=ALL_PEERS):
    nt = len(bufs)
    lands = [lax.empty((N_DEV,) + tuple(b.shape[1:] if scatter else b.shape), b.dtype) for b in bufs]

    def body(*refs):
        ins, lnd = refs[:nt], refs[nt:2 * nt]
        ssems, rsems = refs[2 * nt:3 * nt], refs[3 * nt:4 * nt]
        token = refs[-1]
        x, y, c = lax.axis_index("x"), lax.axis_index("y"), lax.axis_index("c")
        me = 4 * x + 2 * y + c
        for t in range(nt):
            for k, pos in _peers(x, y, c, ks):
                _exchange_copy(ins, lnd, ssems, rsems, scatter, t, k, pos, me).start()
        token[...] = jnp.zeros_like(token)

    both = list(bufs) + lands
    res = pl.pallas_call(
        body, name=name,
        out_shape=[pltpu.SemaphoreType.DMA((N_PEER,))] * (2 * nt)
        + [pltpu.HBM(b.shape, b.dtype) for b in both] + [jax.ShapeDtypeStruct((8, 128), F32)],
        in_specs=[HBM] * (2 * nt),
        out_specs=[SEM] * (2 * nt) + [HBM] * (2 * nt) + [pl.BlockSpec(memory_space=pltpu.VMEM)],
        input_output_aliases={i: 2 * nt + i for i in range(2 * nt)},
        compiler_params=pltpu.CompilerParams(has_side_effects=EFFECT),
    )(*[pltpu.with_memory_space_constraint(b, pltpu.HBM) for b in both])
    return res[:nt], res[nt:2 * nt], res[2 * nt:3 * nt], res[3 * nt:4 * nt], res[-1][0, 0]


def _xwait(ssems, rsems, thrus, lands, scatter, after, name, ks=ALL_PEERS):
    nt = len(lands)

    def body(*refs):
        ins, lnd = refs[:nt], refs[nt:2 * nt]
        ss, rs = refs[2 * nt:3 * nt], refs[3 * nt:4 * nt]
        x, y, c = lax.axis_index("x"), lax.axis_index("y"), lax.axis_index("c")
        me = 4 * x + 2 * y + c
        for t in range(nt):
            for k, pos in _peers(x, y, c, ks):
                cp = _exchange_copy(ins, lnd, ss, rs, scatter, t, k, pos, me)
                cp.wait_send()
                cp.wait_recv()

    both = list(thrus) + list(lands)
    res = pl.pallas_call(
        body, name=name, out_shape=[pltpu.HBM(b.shape, b.dtype) for b in both],
        in_specs=[HBM] * (2 * nt) + [SEM] * (2 * nt) + [ANY], out_specs=[HBM] * (2 * nt),
        input_output_aliases={i: i for i in range(2 * nt)},
        compiler_params=pltpu.CompilerParams(has_side_effects=EFFECT),
    )(*both, *ssems, *rsems, after)
    return res[:nt], res[nt:]


def _forward_copy(lands, ssems, rsems, t, k, pos, sibling):
    px, py, pc = pos
    slot = lands[t].at[4 * px + 2 * py + pc]
    return pltpu.make_async_remote_copy(
        src_ref=slot, dst_ref=slot, send_sem=ssems[t].at[k - 1], recv_sem=rsems[t].at[k - 1],
        device_id=sibling, device_id_type=MESH_ID)


def _fstart(lands, name):
    nt = len(lands)

    def body(*refs):
        lnd = refs[:nt]
        ssems, rsems = refs[nt:2 * nt], refs[2 * nt:3 * nt]
        token = refs[-1]
        x, y, c = lax.axis_index("x"), lax.axis_index("y"), lax.axis_index("c")
        for t in range(nt):
            for k, pos in _peers(x, y, c, FORWARDED):
                _forward_copy(lnd, ssems, rsems, t, k, pos, (x, y, c ^ 1)).start()
        token[...] = jnp.zeros_like(token)

    res = pl.pallas_call(
        body, name=name,
        out_shape=[pltpu.SemaphoreType.DMA((N_PEER,))] * (2 * nt)
        + [pltpu.HBM(b.shape, b.dtype) for b in lands] + [jax.ShapeDtypeStruct((8, 128), F32)],
        in_specs=[HBM] * nt,
        out_specs=[SEM] * (2 * nt) + [HBM] * nt + [pl.BlockSpec(memory_space=pltpu.VMEM)],
        input_output_aliases={i: 2 * nt + i for i in range(nt)},
        compiler_params=pltpu.CompilerParams(has_side_effects=EFFECT),
    )(*[pltpu.with_memory_space_constraint(b, pltpu.HBM) for b in lands])
    return res[:nt], res[nt:2 * nt], res[2 * nt:3 * nt], res[-1][0, 0]


def _fwait(ssems, rsems, lands, after, name):
    nt = len(lands)

    def body(*refs):
        lnd = refs[:nt]
        ss, rs = refs[nt:2 * nt], refs[2 * nt:3 * nt]
        x, y, c = lax.axis_index("x"), lax.axis_index("y"), lax.axis_index("c")
        for t in range(nt):
            for k, pos in _peers(x, y, c, FORWARDED):
                cp = _forward_copy(lnd, ss, rs, t, k, pos, (x, y, c ^ 1))
                cp.wait_send()
                cp.wait_recv()

    return pl.pallas_call(
        body, name=name, out_shape=[pltpu.HBM(b.shape, b.dtype) for b in lands],
        in_specs=[HBM] * nt + [SEM] * (2 * nt) + [ANY], out_specs=[HBM] * nt,
        input_output_aliases={i: i for i in range(nt)},
        compiler_params=pltpu.CompilerParams(has_side_effects=EFFECT),
    )(*lands, *ssems, *rsems, after)


def _adam(recv, w, m, v, tr, name):
    _, r, c = w.shape
    n_part = recv.shape[0]
    c1 = np.float32(1.0 - ADAM_B1 ** ADAM_STEP)
    c2 = np.float32(1.0 - ADAM_B2 ** ADAM_STEP)

    def body(r_ref, w_ref, m_ref, v_ref, g_ref, d_ref, mo_ref, vo_ref):
        g = r_ref[0].astype(F32)
        for s in range(1, n_part):
            g = g + r_ref[s].astype(F32)
        mn = ADAM_B1 * m_ref[0] + (1.0 - ADAM_B1) * g
        vn = ADAM_B2 * v_ref[0] + (1.0 - ADAM_B2) * (g * g)
        g_ref[0] = g
        mo_ref[0] = mn
        vo_ref[0] = vn
        d_ref[0] = -ADAM_LR * ((mn / c1) / (jnp.sqrt(vn / c2) + ADAM_EPS) + ADAM_WD * w_ref[0])

    one = pl.BlockSpec((1, tr, c), lambda i: (0, i, 0))
    return pl.pallas_call(
        body, grid=(r // tr,), in_specs=[pl.BlockSpec((n_part, tr, c), lambda i: (0, i, 0)), one, one, one],
        out_specs=[one] * 4, out_shape=[jax.ShapeDtypeStruct((1, r, c), F32)] * 4,
        name=name, compiler_params=_cp())(recv, w, m, v)


def _local_step(x, target, sp, need, ahead, emit):
    g = {}
    x0, h1 = x, _rms_fwd(x, sp["norm_ffn1"], "ffn1_norm")
    act1, saved1 = _ffn_up(h1, lambda kind, a: need("w" + kind + "1", a),
                           lambda kind, a: ahead("w" + kind + "1", a), "ffn1")
    x1, h2 = _mm_nn_resnorm([(act1, saved1[-1])], x0, 0.5, sp["norm_mix"], 512, "ffn1_down_norm",
                            after=ahead("win_a", act1))
    win_a, win_d = need("win_a", h2), need("win_d", h2)
    conv_w8, wout = need("conv_w8", h2), need("wout", h2)
    z_at = _mm_nn(h2, win_a, F32, 1024, 768, "mix_in_attn")
    z_dn = _mm_nn(h2, win_d, F32, 1024, 768, "mix_in_dn")

    conv, qkvn, bg = _dn_pre_fwd(z_dn, conv_w8, sp["alog_row"], sp["dtb_row"], "dn_pre")
    *prep, tinv = _dn_prep_fwd(qkvn, bg, "dn_prep")
    o_dn, states = _dn_scan_fwd(prep, "dn_scan")
    dn_b = _dn_post_fwd(o_dn, z_dn, sp["dn_norm"], "dn_post")
    attn_f, attn_b, lse = _attn_fwd(z_at, "attn_fwd", after=ahead("wg2", dn_b))

    x2, h3 = _mm_nn_resnorm([(attn_b, wout[:AW]), (dn_b, wout[AW:])], x1, 1.0, sp["norm_ffn2"], 512,
                            "mix_out_norm")
    act2, saved2 = _ffn_up(h3, lambda kind, a: need("w" + kind + "2", a),
                           lambda kind, a: ahead("w" + kind + "2", a), "ffn2")

    dx3, dys3, g["norm_final"], loss8 = _down_loss_bwd(x2, act2, saved2[-1], sp["norm_final"], target,
                                                       "ffn2_down_loss")
    dx2, dx2b, g["norm_ffn2"] = _ffn_bwd(
        x2, sp["norm_ffn2"], saved2, dx3, dys3, 1.0,
        lambda kind, dw: emit(kind + "2", {"w" + kind + "2": dw}), "ffn2b")

    zero = emit("wout", {"wout": jnp.concatenate([_mm_tn(attn_b, dx2b, BF, 512, 1024, "mix_out_dw_a"),
                                                  _mm_tn(dn_b, dx2b, BF, 512, 1024, "mix_out_dw_d")], axis=0)})
    dmix = _mm_nt(dx2b, wout, F32, 1024, 1024, "mix_out_dx", after=zero)

    dq, dk, dv = _attn_bwd(z_at, dmix, attn_f, lse, "attn_bwd")

    do_dn, dgate, g["dn_norm"] = _dn_post_bwd(o_dn, z_dn, sp["dn_norm"], dmix, "dn_post_b")
    cts = _dn_scan_bwd(prep, states, do_dn, "dn_scan_b")
    dqn, dkn, dvn, dbg = _dn_prep_bwd(qkvn, bg, tinv, cts, "dn_prep_b")
    dconv, dbd, g["alog_row"], g["dtb_row"] = _dn_pre_bwd(
        conv, z_dn, sp["alog_row"], sp["dtb_row"], dqn, dkn, dvn, dbg, "dn_pre_b")
    draw, dconv_w8 = _dn_conv_bwd(dconv, z_dn, conv_w8, "dn_conv_b")

    pieces = ((dq, win_a, 0), (dk, win_a, 1), (dv, win_a, 2),
              (draw, win_d, 0), (dgate, win_d, 3 * DNW // DNW), (dbd, win_d, BD_BLK))
    zero = emit("win", {"w_in_parts": [_mm_tn(h2, p[0], BF, 512, min(p[0].shape[1], 768), "mix_in_dw%d" % i)
                                       for i, p in enumerate(pieces)], "conv_w8": dconv_w8})
    dx1, dys1, g["norm_mix"] = _mm_rms_bwd(pieces, NT, x1, sp["norm_mix"], dx2, 0.5, 512, "mix_in_dx_dnorm",
                                           after=zero)
    dx0, _, g["norm_ffn1"] = _ffn_bwd(
        x0, sp["norm_ffn1"], saved1, dx1, dys1, 1.0,
        lambda kind, dw: emit(kind + "1", {"w" + kind + "1": dw}), "ffn1b")
    return loss8[0, 0], dx0, g


def _cols_from_shards(gathered):
    n, r, c = gathered.shape
    return jnp.transpose(gathered, (1, 0, 2)).reshape(r, n * c)


def _shards_from_cols(full, dtype):
    r, nc = full.shape
    return jnp.transpose(full.reshape(r, N_DEV, nc // N_DEV), (1, 0, 2)).astype(dtype)


def _lane_row(vec4):
    return jnp.zeros((1, 128), F32).at[:, DN_H:2 * DN_H].set(vec4.astype(F32))


WEIGHT_SOURCES = {"wg1": "gate1", "wu1": "up1", "wd1": "down1", "win_a": "w_in", "win_d": "w_in",
                  "conv_w8": "conv_w", "wout": "w_out", "wg2": "gate2", "wu2": "up2", "wd2": "down2"}
TRANSPOSED = ("gate1", "up1", "gate2", "up2")


def _build_weights(name, gath):
    if name in ("wg1", "wu1", "wd1", "wg2", "wu2", "wd2"):
        return {name: gath[WEIGHT_SOURCES[name]].reshape(F, D)}
    if name in ("win_a", "win_d"):
        w_in = _cols_from_shards(gath["w_in"])
        c0 = 3 * AW + 3 * DNW
        win_d = jnp.concatenate([w_in[:, ZA:c0], w_in[:, c0 + 2 * DN_H:], w_in[:, c0:c0 + 2 * DN_H],
                                 jnp.zeros((D, ZP - IN_COLS), w_in.dtype)], axis=1)
        return {"win_a": w_in[:, :ZA], "win_d": win_d}
    if name == "wout":
        return {name: gath["w_out"].reshape(D, D)}
    conv = _cols_from_shards(gath["conv_w"])
    return {"conv_w8": jnp.concatenate([conv, jnp.zeros((4, 3 * DNW), F32)], axis=0)}


def _small_params(norm_ffn1, norm_mix, norm_ffn2, norm_final, a_log, dt_bias, dn_norm):
    return {"norm_ffn1": norm_ffn1, "norm_mix": norm_mix, "norm_ffn2": norm_ffn2,
            "norm_final": norm_final.reshape(1, D), "alog_row": _lane_row(a_log), "dtb_row": _lane_row(dt_bias),
            "dn_norm": dn_norm}


def _grad_slabs(group, g):
    if group[0] in "gud":
        return {WEIGHT_SOURCES["w" + group]: g["w" + group].reshape(N_DEV, F // N_DEV, D)}
    if group == "wout":
        return {"w_out": g["wout"].reshape(N_DEV, D // N_DEV, D)}
    aq, ak, av, dqkv, gate, bd = g["w_in_parts"]
    g_in = jnp.concatenate([aq, ak, av, dqkv, bd[:, :2 * DN_H], gate], axis=1)
    return {"w_in": _shards_from_cols(g_in, BF), "conv_w": _shards_from_cols(g["conv_w8"][:4], F32)}


SMALL_ROWS = 40


def _small_pack(norm_ffn1, norm_mix, norm_ffn2, norm_final, dn_norm, alog_row, dtb_row, loss=None):
    rows = [a.reshape(8, 128) for a in (norm_ffn1, norm_mix, norm_ffn2, norm_final)]
    loss_row = jnp.zeros((1, 128), F32) if loss is None else jnp.broadcast_to(loss.reshape(1, 1), (1, 128))
    rows += [dn_norm.reshape(1, 128), alog_row, dtb_row, loss_row, jnp.zeros((SMALL_ROWS - 36, 128), F32)]
    return jnp.concatenate(rows, axis=0)


def _small_unpack(pk):
    pk = pk[0]
    return (pk[0:8].reshape(1, D), pk[8:16].reshape(1, D), pk[16:24].reshape(1, D), pk[24:32].reshape(D),
            pk[32:33], pk[33:34, DN_H:2 * DN_H], pk[34:35, DN_H:2 * DN_H])


ADAM_TILE = {"gate1": 176, "up1": 176, "down1": 176, "gate2": 176, "up2": 176, "down2": 176,
             "w_in": 256, "w_out": 128, "conv_w": 4}
BIG = ("gate1", "up1", "down1", "w_in", "w_out", "gate2", "up2", "down2", "conv_w")


def kernel(x, norm_ffn1, ffn1_gate, ffn1_up, ffn1_down, norm_mix, w_in, conv_w, a_log, dt_bias, dn_norm, w_out, norm_ffn2, ffn2_gate, ffn2_up, ffn2_down, norm_final, loss_target, m_norm_ffn1, m_ffn1_gate, m_ffn1_up, m_ffn1_down, m_norm_mix, m_w_in, m_conv_w, m_a_log, m_dt_bias, m_dn_norm, m_w_out, m_norm_ffn2, m_ffn2_gate, m_ffn2_up, m_ffn2_down, m_norm_final, v_norm_ffn1, v_ffn1_gate, v_ffn1_up, v_ffn1_down, v_norm_mix, v_w_in, v_conv_w, v_a_log, v_dt_bias, v_dn_norm, v_w_out, v_norm_ffn2, v_ffn2_gate, v_ffn2_up, v_ffn2_down, v_norm_final):
    w = {"gate1": ffn1_gate, "up1": ffn1_up, "down1": ffn1_down, "w_in": w_in, "w_out": w_out,
         "gate2": ffn2_gate, "up2": ffn2_up, "down2": ffn2_down, "conv_w": conv_w}
    m = {"gate1": m_ffn1_gate, "up1": m_ffn1_up, "down1": m_ffn1_down, "w_in": m_w_in, "w_out": m_w_out,
         "gate2": m_ffn2_gate, "up2": m_ffn2_up, "down2": m_ffn2_down, "conv_w": m_conv_w}
    v = {"gate1": v_ffn1_gate, "up1": v_ffn1_up, "down1": v_ffn1_down, "w_in": v_w_in, "w_out": v_w_out,
         "gate2": v_ffn2_gate, "up2": v_ffn2_up, "down2": v_ffn2_down, "conv_w": v_conv_w}

    me = 4 * lax.axis_index("x") + 2 * lax.axis_index("y") + lax.axis_index("c")
    own_slot = lambda land, mine: lax.dynamic_update_index_in_dim(land, mine, me, 0)

    ag_order = ("gate1", "up1", "down1", "w_in", "conv_w", "w_out", "gate2", "up2", "down2")
    ag_groups = (("gate1",), ("up1",), ("down1",), ("w_in", "conv_w", "w_out"), ("gate2", "up2", "down2"))
    pos = {n: i for i, n in enumerate(ag_order)}

    def shard(n):
        if n == "conv_w":
            return w[n][0]
        return (w[n][0].T if n in TRANSPOSED else w[n][0]).astype(BF)

    ss, rs, thru, land, zero = _xstart([shard(n) for n in ag_order], False, "weights_start", FIRST_HOP)
    gath, built, on_its_way = {}, {}, {}
    group_of = lambda name: [i for i, grp in enumerate(ag_groups) if WEIGHT_SOURCES[name] in grp][0]

    def ahead(name, after):
        gi = group_of(name)
        if WEIGHT_SOURCES[name] in gath or gi in on_its_way:
            return None
        ids = [pos[n] for n in ag_groups[gi]]
        thrus, lands = _xwait([ss[i] for i in ids], [rs[i] for i in ids], [thru[i] for i in ids],
                              [land[i] for i in ids], False, after, "weights_wait%d" % gi, FIRST_HOP)
        fss, frs, lands, token = _fstart(lands, "weights_forward%d" % gi)
        on_its_way[gi] = (thrus, fss, frs, lands)
        return token

    def need(name, after):
        if name not in built:
            if WEIGHT_SOURCES[name] not in gath:
                gi = group_of(name)
                ahead(name, after)
                thrus, fss, frs, lands = on_its_way.pop(gi)
                lands = _fwait(fss, frs, lands, after, "weights_forward_wait%d" % gi)
                for n, t, l in zip(ag_groups[gi], thrus, lands):
                    gath[n] = own_slot(l, t)
            built.update(_build_weights(name, gath))
        return built[name]

    pending = []

    def emit(group, grads):
        slabs = grads if group == "small" else _grad_slabs(group, grads)
        names = list(slabs)
        started = _xstart([slabs[n] for n in names], True, "grads_start_" + group)
        pending.append((group, names) + started[:4])
        return started[4]

    sp = _small_params(norm_ffn1 + zero, norm_mix, norm_ffn2, norm_final, a_log, dt_bias, dn_norm)
    loss_part, dx, g = _local_step(x[0], loss_target[0], sp, need, ahead, emit)
    small = _small_pack(g["norm_ffn1"], g["norm_mix"], g["norm_ffn2"], g["norm_final"], g["dn_norm"],
                        g["alog_row"], g["dtb_row"], loss_part)
    emit("small", {"small": jnp.broadcast_to(small[None], (N_DEV, SMALL_ROWS, 128))})

    pack = lambda a: _small_pack(*a)[None]
    res, after = {}, dx
    for group, names, gss, grs, gthru, gland in pending:
        thrus, lands = _xwait(gss, grs, gthru, gland, True, after, "grads_wait_" + group)
        for n, t, l in zip(names, thrus, lands):
            recv = own_slot(l, lax.dynamic_index_in_dim(t, me, 0, keepdims=False))
            if n == "small":
                res[n] = _adam(
                    recv,
                    pack((norm_ffn1, norm_mix, norm_ffn2, norm_final, dn_norm, _lane_row(a_log), _lane_row(dt_bias))),
                    pack((m_norm_ffn1, m_norm_mix, m_norm_ffn2, m_norm_final, m_dn_norm, _lane_row(m_a_log),
                          _lane_row(m_dt_bias))),
                    pack((v_norm_ffn1, v_norm_mix, v_norm_ffn2, v_norm_final, v_dn_norm, _lane_row(v_a_log),
                          _lane_row(v_dt_bias))),
                    SMALL_ROWS, "adam_small")
            elif n in TRANSPOSED:
                flip = lambda a: jnp.swapaxes(a, 1, 2)
                res[n] = [flip(o) for o in _adam(recv, flip(w[n]), flip(m[n]), flip(v[n]), ADAM_TILE[n],
                                                 "adam_" + n)]
            else:
                res[n] = _adam(recv, w[n], m[n], v[n], ADAM_TILE[n], "adam_" + n)
            after = res[n][0]
    res_s = res["small"]

    loss = res_s[0][0, 35, 0]
    outs = [loss, dx[None]]
    for k in range(4):
        n1, nm, n2, nf, dn, al, dt = _small_unpack(res_s[k])
        big = {n: res[n][k] for n in BIG}
        outs += [n1, big["gate1"], big["up1"], big["down1"], nm, big["w_in"], big["conv_w"], al, dt, dn,
                 big["w_out"], n2, big["gate2"], big["up2"], big["down2"], nf]
    return tuple(outs)
```

```python
import functools

import numpy as np
import jax
import jax.numpy as jnp
from jax import lax
from jax.experimental import pallas as pl
from jax.experimental.pallas import tpu as pltpu

T = 4096
D = 1024
F = 2816
N_DEV = 8
A_HEADS = 8
A_HD = 64
AW = A_HEADS * A_HD
DN_H = 4
DN_HD = 128
DNW = DN_H * DN_HD
CH = 64
PAIR = 2 * CH
ZA = 3 * AW
ZD = 3 * DNW + DNW + 256
ZP = ZA + ZD
BD_BLK = (3 * DNW + DNW) // 128
IN_COLS = 3592
PATTERNS = ((128, 1), (512, 4), (2048, 16))
NORM_EPS = 1e-6
L2_EPS = 1e-6
ADAM_LR, ADAM_B1, ADAM_B2, ADAM_EPS, ADAM_WD, ADAM_STEP = 0.001, 0.9, 0.999, 1e-08, 0.01, 10
VMEM_LIMIT = 56 * 1024 * 1024
NEG = -1e30

BF = jnp.bfloat16
F32 = jnp.float32
NN = (((1,), (0,)), ((), ()))
NT = (((1,), (1,)), ((), ()))
TN = (((0,), (0,)), ((), ()))
HI = lax.Precision.HIGHEST
MESH_ID = pl.DeviceIdType.MESH
ANY = pl.BlockSpec(memory_space=pl.ANY)


def _cp():
    return pltpu.CompilerParams(vmem_limit_bytes=VMEM_LIMIT)


def _dg(a, b, dims):
    return lax.dot_general(a, b, dims, preferred_element_type=F32)


def _hdot(a, b):
    return lax.dot_general(a, b, NN, precision=HI, preferred_element_type=F32)


def _make_bdot(dims, da_dims, da_swap, db_dims, db_swap):
    @jax.custom_vjp
    def f(a, b):
        return _dg(a.astype(BF), b.astype(BF), dims)

    def fwd(a, b):
        return f(a, b), (a, b)

    def bwd(res, g):
        a, b = res
        gb, ab, bb = g.astype(BF), a.astype(BF), b.astype(BF)
        da = _dg(bb, gb, da_dims) if da_swap else _dg(gb, bb, da_dims)
        db = _dg(gb, ab, db_dims) if db_swap else _dg(ab, gb, db_dims)
        return da.astype(a.dtype), db.astype(b.dtype)

    f.defvjp(fwd, bwd)
    return f


_bdot_nn = _make_bdot(NN, NT, False, TN, False)
_bdot_nt = _make_bdot(NT, NN, False, TN, True)
_bdot_tn = _make_bdot(TN, NT, True, NN, False)


def _sigmoid(x):
    return 0.5 * jnp.tanh(0.5 * x) + 0.5


def _iota(shape, dim):
    return lax.broadcasted_iota(jnp.int32, shape, dim)


def _col(x, idx):
    return jnp.sum(jnp.where(_iota(x.shape, 1) == idx, x, 0.0), axis=1, keepdims=True)


def _mm_nn(a, b, out_dtype, tm, tn, name):
    m, k = a.shape
    n = b.shape[1]

    def body(a_ref, b_ref, o_ref):
        o_ref[...] = _dg(a_ref[...], b_ref[...], NN).astype(out_dtype)

    return pl.pallas_call(
        body, grid=(m // tm, n // tn),
        in_specs=[pl.BlockSpec((tm, k), lambda i, j: (i, 0)), pl.BlockSpec((k, tn), lambda i, j: (0, j))],
        out_specs=pl.BlockSpec((tm, tn), lambda i, j: (i, j)),
        out_shape=jax.ShapeDtypeStruct((m, n), out_dtype), name=name, compiler_params=_cp())(a, b)


def _tie(body, after):
    if after is None:
        return body, [], []
    return (lambda tok_ref, *refs: body(*refs)), [ANY], [after.reshape(1, 1)]


def _mm_nt(a, b, out_dtype, tm, tb, name, after=None):
    m, c = a.shape
    kb = b.shape[0]

    def body(a_ref, b_ref, o_ref):
        o_ref[...] = _dg(a_ref[...], b_ref[...], NT).astype(out_dtype)

    body, tspec, tok = _tie(body, after)
    return pl.pallas_call(
        body, grid=(m // tm, kb // tb),
        in_specs=tspec + [pl.BlockSpec((tm, c), lambda i, j: (i, 0)), pl.BlockSpec((tb, c), lambda i, j: (j, 0))],
        out_specs=pl.BlockSpec((tm, tb), lambda i, j: (i, j)),
        out_shape=jax.ShapeDtypeStruct((m, kb), out_dtype), name=name, compiler_params=_cp())(*tok, a, b)


def _mm_tn(a, b, out_dtype, ta, tb, name, after=None):
    m, ka = a.shape
    nb = b.shape[1]

    def body(a_ref, b_ref, o_ref):
        o_ref[...] = _dg(a_ref[...], b_ref[...], TN).astype(out_dtype)

    body, tspec, tok = _tie(body, after)
    return pl.pallas_call(
        body, grid=(ka // ta, nb // tb),
        in_specs=tspec + [pl.BlockSpec((m, ta), lambda i, j: (0, i)), pl.BlockSpec((m, tb), lambda i, j: (0, j))],
        out_specs=pl.BlockSpec((ta, tb), lambda i, j: (i, j)),
        out_shape=jax.ShapeDtypeStruct((ka, nb), out_dtype), name=name, compiler_params=_cp())(*tok, a, b)


def _mm_nt_swiglu(h, wu_t, gate, tm, tb, name, after=None):
    m, c = h.shape
    kb = wu_t.shape[0]

    def body(h_ref, w_ref, g_ref, u_ref, a_ref):
        u = _dg(h_ref[...], w_ref[...], NT)
        g = g_ref[...].astype(F32)
        u_ref[...] = u.astype(BF)
        a_ref[...] = (g * _sigmoid(g) * u).astype(BF)

    body, tspec, tok = _tie(body, after)
    tile = pl.BlockSpec((tm, tb), lambda i, j: (i, j))
    return pl.pallas_call(
        body, grid=(m // tm, kb // tb),
        in_specs=tspec + [pl.BlockSpec((tm, c), lambda i, j: (i, 0)), pl.BlockSpec((tb, c), lambda i, j: (j, 0)),
                          tile],
        out_specs=[tile, tile], out_shape=[jax.ShapeDtypeStruct((m, kb), BF)] * 2,
        name=name, compiler_params=_cp())(*tok, h, wu_t, gate)


def _mm_nt_dswiglu(dys, wd, gate, up, tm, tb, name, after=None):
    m, c = dys.shape
    kb = wd.shape[0]

    def body(d_ref, w_ref, g_ref, u_ref, dg_ref, du_ref):
        da = _dg(d_ref[...], w_ref[...], NT)
        g = g_ref[...].astype(F32)
        u = u_ref[...].astype(F32)
        s = _sigmoid(g)
        dg_ref[...] = (da * u * (s * (1.0 + g * (1.0 - s)))).astype(BF)
        du_ref[...] = (da * (g * s)).astype(BF)

    body, tspec, tok = _tie(body, after)
    tile = pl.BlockSpec((tm, tb), lambda i, j: (i, j))
    return pl.pallas_call(
        body, grid=(m // tm, kb // tb),
        in_specs=tspec + [pl.BlockSpec((tm, c), lambda i, j: (i, 0)), pl.BlockSpec((tb, c), lambda i, j: (j, 0)),
                          tile, tile],
        out_specs=[tile, tile], out_shape=[jax.ShapeDtypeStruct((m, kb), BF)] * 2,
        name=name, compiler_params=_cp())(*tok, dys, wd, gate, up)


def _rms_fwd(x, gain, name):
    tm = 512
    row = pl.BlockSpec((tm, D), lambda i: (i, 0))

    def body(x_ref, g_ref, h_ref):
        xv = x_ref[...]
        r = lax.rsqrt(jnp.mean(xv * xv, axis=-1, keepdims=True) + NORM_EPS)
        h_ref[...] = (xv * r * g_ref[...]).astype(BF)

    return pl.pallas_call(
        body, grid=(T // tm,), in_specs=[row, pl.BlockSpec((1, D), lambda i: (0, 0))], out_specs=row,
        out_shape=jax.ShapeDtypeStruct((T, D), BF), name=name, compiler_params=_cp())(x, gain)


def _mm_rms_bwd(pairs, dims, x, gain, dres, alpha_out, tm, name, after=None):
    n = len(pairs)
    row = pl.BlockSpec((tm, D), lambda i: (i, 0))
    gspec = pl.BlockSpec((1, D), lambda i: (0, 0))

    def body(*refs):
        x_ref, g_ref, dres_ref, dx_ref, dxs_ref, dg_ref = refs[2 * n:]
        i = pl.program_id(0)
        dhv = _dg(refs[0][...], refs[n][...], dims)
        for p in range(1, n):
            dhv = dhv + _dg(refs[p][...], refs[n + p][...], dims)
        xv = x_ref[...]
        r = lax.rsqrt(jnp.mean(xv * xv, axis=-1, keepdims=True) + NORM_EPS)
        xh = xv * r
        part = jnp.sum(dhv * xh, axis=0, keepdims=True)

        @pl.when(i == 0)
        def _():
            dg_ref[...] = part

        @pl.when(i > 0)
        def _():
            dg_ref[...] += part

        dxh = dhv * g_ref[...]
        dx = r * (dxh - xh * jnp.mean(dxh * xh, axis=-1, keepdims=True)) + dres_ref[...]
        dx_ref[...] = dx
        dxs_ref[...] = (alpha_out * dx).astype(BF)

    body, tspec, tok = _tie(body, after)

    def bspec(a, b, kblk):
        if kblk is None:
            return pl.BlockSpec(b.shape, lambda i: (0, 0))
        return pl.BlockSpec((b.shape[0], a.shape[1]), lambda i: (0, kblk))

    return pl.pallas_call(
        body, grid=(T // tm,),
        in_specs=tspec + [pl.BlockSpec((tm, p[0].shape[1]), lambda i: (i, 0)) for p in pairs]
        + [bspec(p[0], p[1], p[2] if len(p) > 2 else None) for p in pairs] + [row, gspec, row],
        out_specs=[row, row, gspec],
        out_shape=[jax.ShapeDtypeStruct((T, D), F32), jax.ShapeDtypeStruct((T, D), BF),
                   jax.ShapeDtypeStruct((1, D), F32)],
        name=name, compiler_params=_cp())(*tok, *[p[0] for p in pairs], *[p[1] for p in pairs], x, gain, dres)


def _down_loss_bwd(x_prev, act, wd, gain, target, name):
    tm = 512
    row = pl.BlockSpec((tm, D), lambda i: (i, 0))
    gspec = pl.BlockSpec((1, D), lambda i: (0, 0))
    lspec = pl.BlockSpec((8, 128), lambda i: (0, 0))

    def body(x_ref, a_ref, w_ref, g_ref, t_ref, dx_ref, dxs_ref, dg_ref, loss_ref):
        i = pl.program_id(0)
        xv = x_ref[...] + 0.5 * _dg(a_ref[...], w_ref[...], NN)
        r = lax.rsqrt(jnp.mean(xv * xv, axis=-1, keepdims=True) + NORM_EPS)
        xh = xv * r
        diff = xh * g_ref[...] - t_ref[...]
        lpart = 0.5 * jnp.sum(jnp.mean(diff * diff, axis=-1, keepdims=True), axis=0, keepdims=True)
        dy = diff * (1.0 / D)
        part = jnp.sum(dy * xh, axis=0, keepdims=True)

        @pl.when(i == 0)
        def _():
            dg_ref[...] = part
            loss_ref[...] = jnp.broadcast_to(lpart, (8, 128))

        @pl.when(i > 0)
        def _():
            dg_ref[...] += part
            loss_ref[...] += jnp.broadcast_to(lpart, (8, 128))

        dxh = dy * g_ref[...]
        dx = r * (dxh - xh * jnp.mean(dxh * xh, axis=-1, keepdims=True))
        dx_ref[...] = dx
        dxs_ref[...] = (0.5 * dx).astype(BF)

    return pl.pallas_call(
        body, grid=(T // tm,),
        in_specs=[row, pl.BlockSpec((tm, F), lambda i: (i, 0)), pl.BlockSpec((F, D), lambda i: (0, 0)), gspec, row],
        out_specs=[row, row, gspec, lspec],
        out_shape=[jax.ShapeDtypeStruct((T, D), F32), jax.ShapeDtypeStruct((T, D), BF),
                   jax.ShapeDtypeStruct((1, D), F32), jax.ShapeDtypeStruct((8, 128), F32)],
        name=name, compiler_params=_cp())(x_prev, act, wd, gain, target)


def _mm_nn_resnorm(pairs, x_prev, alpha, gain, tm, name, after=None):
    m = x_prev.shape[0]
    n = len(pairs)

    def body(*refs):
        x_ref, g_ref, xo_ref, h_ref = refs[2 * n:]
        y = _dg(refs[0][...], refs[n][...], NN)
        for i in range(1, n):
            y = y + _dg(refs[i][...], refs[n + i][...], NN)
        xv = x_ref[...] + alpha * y
        xo_ref[...] = xv
        r = lax.rsqrt(jnp.mean(xv * xv, axis=-1, keepdims=True) + NORM_EPS)
        h_ref[...] = (xv * r * g_ref[...]).astype(BF)

    body, tspec, tok = _tie(body, after)
    row = pl.BlockSpec((tm, D), lambda i: (i, 0))
    return pl.pallas_call(
        body, grid=(m // tm,),
        in_specs=tspec + [pl.BlockSpec((tm, a.shape[1]), lambda i: (i, 0)) for a, _ in pairs]
        + [pl.BlockSpec(b.shape, lambda i: (0, 0)) for _, b in pairs] + [row, pl.BlockSpec((1, D), lambda i: (0, 0))],
        out_specs=[row, row],
        out_shape=[jax.ShapeDtypeStruct((m, D), F32), jax.ShapeDtypeStruct((m, D), BF)],
        name=name, compiler_params=_cp())(*tok, *[a for a, _ in pairs], *[b for _, b in pairs], x_prev, gain)


def _ffn_up(h, need, ahead, tag):
    wg = need("g", h)
    gate = _mm_nt(h, wg, BF, 1024, 1408, tag + "_gate")
    wu = need("u", gate)
    up, act = _mm_nt_swiglu(h, wu, gate, 1024, 1408, tag + "_up_act", after=ahead("d", wu))
    wd = need("d", up)
    return act, (h, gate, up, act, wg, wu, wd)


def _ffn_bwd(x_in, gain, saved, dxo, dys, alpha_out, emit, tag):
    h, gate, up, act, wg, wu, wd = saved
    sent = emit("d", _mm_tn(act, dys, BF, 256, 1024, tag + "_dwd"))
    dgate, dup = _mm_nt_dswiglu(dys, wd, gate, up, 1024, 1408, tag + "_dact", after=sent)
    sent = emit("g", _mm_tn(dgate, h, BF, 256, 1024, tag + "_dwg"))
    sent = emit("u", _mm_tn(dup, h, BF, 256, 1024, tag + "_dwu", after=sent))
    return _mm_rms_bwd([(dgate, wg), (dup, wu)], NN, x_in, gain, dxo, alpha_out, 512, tag + "_dh_dnorm",
                       after=sent)


SLAB = 2048
N_SLAB = T // SLAB
N_PAIR = A_HEADS // 2


def _pair_masks():
    lane = _iota((128, 128), 1)
    return lane < A_HD, lane >= A_HD


def _slope_table():
    h = 2 * jnp.arange(N_PAIR)[:, None] + jnp.minimum(jnp.arange(8), 1)[None, :]
    return jnp.broadcast_to((2.0 ** (-(h + 1).astype(F32)))[:, :, None], (N_PAIR, 8, 128))


def _rows(ref, start, d):
    if d == 1:
        return ref[pl.ds(start, 128), :]
    return ref[pl.ds(start, 128, stride=d), :]


def _put_rows(ref, start, d, val):
    if d == 1:
        ref[pl.ds(start, 128), :] = val
    else:
        ref[pl.ds(start, 128, stride=d), :] = val


def _units(d):
    return [(r, b, r + 128 * d * b) for r in range(d) for b in range(SLAB // (128 * d))]


def _biases(d, slopes, has_prev):
    qi = _iota((128, 256), 0)
    kj = _iota((128, 256), 1)
    steps = qi + 128 - kj
    in_band = (steps >= 0) & (steps <= 128)
    dist = (steps * d).astype(F32)
    base = [jnp.where(in_band, -(sl * dist), NEG) for sl in slopes]
    edge = [jnp.where(has_prev | (kj >= 128), b, NEG) for b in base]
    return base, edge


def _attn_fwd(z_at, name, after=None):
    def body(sl_ref, q_ref, kc_ref, kp_ref, vc_ref, vp_ref, of_ref, ob_ref, lse_ref, m_s, l_s, a_s):
        n = pl.program_id(1)
        lo, hi = _pair_masks()
        slopes = (sl_ref[0, 0:1, 0:1], sl_ref[0, 1:2, 0:1])

        def unit(d, start, b, first, carry, bias):
            q = (_rows(q_ref, start, d) * (A_HD ** -0.5)).astype(BF)
            kcur, vcur = _rows(kc_ref, start, d).astype(BF), _rows(vc_ref, start, d).astype(BF)
            if b > 0:
                kprev, vprev = carry
            else:
                pstart = start + SLAB - 128 * d
                kprev, vprev = _rows(kp_ref, pstart, d).astype(BF), _rows(vp_ref, pstart, d).astype(BF)
            kcat = jnp.concatenate([kprev, kcur], axis=0)
            vcat = jnp.concatenate([vprev, vcur], axis=0)
            ms, ls, pvs = [], [], []
            for e in range(2):
                qm = jnp.where(lo if e == 0 else hi, q, jnp.zeros_like(q))
                s = _dg(qm, kcat, NT) + bias[e]
                m = jnp.max(s, axis=1, keepdims=True)
                p = jnp.exp(s - m)
                ms.append(m)
                ls.append(jnp.sum(p, axis=1, keepdims=True))
                pvs.append(_dg(p.astype(BF), vcat, NN))
            m_u = jnp.where(lo, ms[0], ms[1])
            l_u = jnp.where(lo, ls[0], ls[1])
            a_u = jnp.where(lo, pvs[0], pvs[1])
            if first:
                m_n, l_n, a_n = m_u, l_u, a_u
            else:
                m_o = _rows(m_s, start, d)
                m_n = jnp.maximum(m_o, m_u)
                c_o = jnp.exp(m_o - m_n)
                c_u = jnp.exp(m_u - m_n)
                l_n = _rows(l_s, start, d) * c_o + l_u * c_u
                a_n = _rows(a_s, start, d) * c_o + a_u * c_u
            _put_rows(m_s, start, d, m_n)
            _put_rows(l_s, start, d, l_n)
            _put_rows(a_s, start, d, a_n)
            return kcur, vcur

        for pi, (_, d) in enumerate(PATTERNS):
            base, edge = _biases(d, slopes, n > 0)
            carry = None
            for r, b, start in _units(d):
                carry = unit(d, start, b, pi == 0, carry, edge if b == 0 else base)
        l = l_s[...]
        out = a_s[...] / l
        of_ref[...] = out
        ob_ref[...] = out.astype(BF)
        lse_ref[...] = m_s[...] + jnp.log(l)

    body, tspec, tok = _tie(body, after)
    cur = lambda c: pl.BlockSpec((SLAB, 128), lambda j, n: (n, c * N_PAIR + j))
    prv = lambda c: pl.BlockSpec((SLAB, 128), lambda j, n: (jnp.maximum(n - 1, 0), c * N_PAIR + j))
    out = pl.BlockSpec((SLAB, 128), lambda j, n: (n, j))
    return pl.pallas_call(
        body, grid=(N_PAIR, N_SLAB),
        in_specs=tspec + [pl.BlockSpec((1, 8, 128), lambda j, n: (j, 0, 0)), cur(0), cur(1), prv(1), cur(2), prv(2)],
        out_specs=[out, out, out],
        out_shape=[jax.ShapeDtypeStruct((T, AW), F32), jax.ShapeDtypeStruct((T, AW), BF),
                   jax.ShapeDtypeStruct((T, AW), F32)],
        scratch_shapes=[pltpu.VMEM((SLAB, 128), F32)] * 3,
        name=name, compiler_params=_cp())(*tok, _slope_table(), z_at, z_at, z_at, z_at, z_at)


def _attn_bwd(z_at, dout, out, lse, name):
    def body(sl_ref, q_ref, kc_ref, kp_ref, vc_ref, vp_ref, do_ref, o_ref, lse_ref, dq_ref, dk_ref, dv_ref,
             dq_s, dk_s, dv_s, ck_s, cv_s):
        step = pl.program_id(1)
        n = N_SLAB - 1 - step
        lo, hi = _pair_masks()
        slopes = (sl_ref[0, 0:1, 0:1], sl_ref[0, 1:2, 0:1])

        @pl.when(step == 0)
        def _():
            ck_s[...] = jnp.zeros_like(ck_s)
            cv_s[...] = jnp.zeros_like(cv_s)

        dk_s[...] = ck_s[...]
        dv_s[...] = cv_s[...]
        ck_s[...] = jnp.zeros_like(ck_s)
        cv_s[...] = jnp.zeros_like(cv_s)

        def add_rows(ref, start, d, val):
            _put_rows(ref, start, d, _rows(ref, start, d) + val)

        def unit(d, start, b, first, carry, bias):
            q = (_rows(q_ref, start, d) * (A_HD ** -0.5)).astype(BF)
            do_f = _rows(do_ref, start, d)
            do = do_f.astype(BF)
            prod = do_f * _rows(o_ref, start, d)
            lse_u = _rows(lse_ref, start, d)
            kcur, vcur = _rows(kc_ref, start, d).astype(BF), _rows(vc_ref, start, d).astype(BF)
            if b > 0:
                kprev, vprev = carry
            else:
                pstart = start + SLAB - 128 * d
                kprev, vprev = _rows(kp_ref, pstart, d).astype(BF), _rows(vp_ref, pstart, d).astype(BF)
            kcat = jnp.concatenate([kprev, kcur], axis=0)
            vcat = jnp.concatenate([vprev, vcur], axis=0)
            masks = (lo, hi)
            qms = [jnp.where(msk, q, jnp.zeros_like(q)) for msk in masks]
            doms = [jnp.where(msk, do, jnp.zeros_like(do)) for msk in masks]
            deltas = [jnp.sum(jnp.where(msk, prod, 0.0), axis=1, keepdims=True) for msk in masks]
            ss = [_dg(qm, kcat, NT) + bs for qm, bs in zip(qms, bias)]
            dps = [_dg(dom, vcat, NT) for dom in doms]
            ps = [jnp.exp(s - lse_u[:, 64 * e:64 * e + 1]) for e, s in enumerate(ss)]
            dss = [(p * (dp - delta)).astype(BF) for p, dp, delta in zip(ps, dps, deltas)]
            pbs = [p.astype(BF) for p in ps]
            dqs = [_dg(ds, kcat, NN) for ds in dss]
            dkc = _dg(dss[0], qms[0], TN) + _dg(dss[1], qms[1], TN)
            dvc = _dg(pbs[0], doms[0], TN) + _dg(pbs[1], doms[1], TN)
            dq_u = jnp.where(lo, dqs[0], dqs[1]) * (A_HD ** -0.5)
            if first:
                _put_rows(dq_s, start, d, dq_u)
            else:
                add_rows(dq_s, start, d, dq_u)
            add_rows(dk_s, start, d, dkc[128:])
            add_rows(dv_s, start, d, dvc[128:])
            if b > 0:
                add_rows(dk_s, start - 128 * d, d, dkc[:128])
                add_rows(dv_s, start - 128 * d, d, dvc[:128])
            else:
                pstart = start + SLAB - 128 * d
                add_rows(ck_s, pstart, d, dkc[:128])
                add_rows(cv_s, pstart, d, dvc[:128])
            return kcur, vcur

        for pi, (_, d) in enumerate(PATTERNS):
            base, edge = _biases(d, slopes, n > 0)
            carry = None
            for r, b, start in _units(d):
                carry = unit(d, start, b, pi == 0, carry, edge if b == 0 else base)
        dq_ref[...] = dq_s[...].astype(BF)
        dk_ref[...] = dk_s[...].astype(BF)
        dv_ref[...] = dv_s[...].astype(BF)

    rev = lambda n: N_SLAB - 1 - n
    cur = lambda c: pl.BlockSpec((SLAB, 128), lambda j, n: (rev(n), c * N_PAIR + j))
    prv = lambda c: pl.BlockSpec((SLAB, 128), lambda j, n: (jnp.maximum(rev(n) - 1, 0), c * N_PAIR + j))
    one = pl.BlockSpec((SLAB, 128), lambda j, n: (rev(n), j))
    return pl.pallas_call(
        body, grid=(N_PAIR, N_SLAB),
        in_specs=[pl.BlockSpec((1, 8, 128), lambda j, n: (j, 0, 0)), cur(0), cur(1), prv(1), cur(2), prv(2),
                  one, one, one],
        out_specs=[one, one, one], out_shape=[jax.ShapeDtypeStruct((T, AW), BF)] * 3,
        scratch_shapes=[pltpu.VMEM((SLAB, 128), F32)] * 5,
        name=name, compiler_params=_cp())(_slope_table(), z_at, z_at, z_at, z_at, z_at, dout, out, lse)


def _silu(x):
    return x * jax.nn.sigmoid(x)


def _qk_math(c):
    s = _silu(c)
    return s * lax.rsqrt(jnp.sum(s * s, axis=-1, keepdims=True) + L2_EPS)


def _softplus(x):
    return jnp.maximum(x, 0.0) + jnp.log(1.0 + jnp.exp(-jnp.abs(x)))


def _gate_math(bd, alog_row, dtb_row):
    rows = bd.shape[0]
    lane = _iota(bd.shape, 1)
    beta = jax.nn.sigmoid(bd)
    g = jnp.where((lane >= DN_H) & (lane < 2 * DN_H), -jnp.exp(alog_row) * _softplus(bd + dtb_row), 0.0)
    ri = _iota((rows, rows), 0)
    ci = _iota((rows, rows), 1)
    same = (ri // CH) == (ci // CH)
    li = _iota((128, 128), 0)
    lj = _iota((128, 128), 1)
    to_next_group = jnp.where((lj == li + DN_H) & (li >= DN_H) & (li < 2 * DN_H), 1.0, 0.0)
    gc = _hdot(jnp.where(same & (ci <= ri), 1.0, 0.0), g)
    glast = _hdot(_hdot(jnp.where(same, 1.0, 0.0), g), to_next_group)
    return jnp.where(lane < DN_H, beta, 0.0) + gc + glast


def _shift_down(cur, halo, s):
    if s == 0:
        return cur
    rolled = pltpu.roll(cur, s, 0)
    hr = pltpu.roll(halo, s, 0)
    head = jnp.where(_iota(hr.shape, 0) < s, hr, rolled[:8])
    return jnp.concatenate([head, rolled[8:]], axis=0)


def _shift_up(cur, halo, s):
    if s == 0:
        return cur
    rows = cur.shape[0]
    rolled = pltpu.roll(cur, rows - s, 0)
    hr = pltpu.roll(halo, 8 - s, 0)
    tail = jnp.where(_iota(hr.shape, 0) >= 8 - s, hr, rolled[rows - 8:])
    return jnp.concatenate([rolled[:rows - 8], tail], axis=0)


def _dn_pre_fwd(z_dn, conv_w8, alog_row, dtb_row, name):
    tm = 512
    wq = 3 * DNW

    def body(raw_ref, halo_ref, bd_ref, w_ref, al_ref, dt_ref, conv_ref, qkv_ref, bg_ref):
        i = pl.program_id(0)
        cur = raw_ref[...]
        halo = jnp.where(i > 0, halo_ref[...], 0.0)
        w = w_ref[...]
        conv = jnp.zeros((tm, wq), F32)
        for j in range(4):
            conv = conv + _shift_down(cur, halo, 3 - j) * w[j:j + 1, :]
        conv_ref[...] = conv
        for blk in range(3 * DN_H):
            sl = slice(128 * blk, 128 * blk + 128)
            c = conv[:, sl]
            qkv_ref[:, sl] = _qk_math(c) if blk < 2 * DN_H else _silu(c)
        for r0 in range(0, tm, PAIR):
            rs = slice(r0, r0 + PAIR)
            bg_ref[rs, :] = _gate_math(bd_ref[rs, :], al_ref[...], dt_ref[...])

    one = pl.BlockSpec((1, 128), lambda i: (0, 0))
    return pl.pallas_call(
        body, grid=(T // tm,),
        in_specs=[pl.BlockSpec((tm, wq), lambda i: (i, 0)),
                  pl.BlockSpec((8, wq), lambda i: (jnp.maximum(i * (tm // 8) - 1, 0), 0)),
                  pl.BlockSpec((tm, 128), lambda i: (i, BD_BLK)),
                  pl.BlockSpec((8, wq), lambda i: (0, 0)), one, one],
        out_specs=[pl.BlockSpec((tm, wq), lambda i: (i, 0)), pl.BlockSpec((tm, wq), lambda i: (i, 0)),
                   pl.BlockSpec((tm, 128), lambda i: (i, 0))],
        out_shape=[jax.ShapeDtypeStruct((T, wq), F32), jax.ShapeDtypeStruct((T, wq), F32),
                   jax.ShapeDtypeStruct((T, 128), F32)],
        name=name, compiler_params=_cp())(z_dn, z_dn, z_dn, conv_w8, alog_row, dtb_row)


def _dn_pre_bwd(conv, z_dn, alog_row, dtb_row, dqn, dkn, dvn, dbg, name):
    tm = 512
    wq = 3 * DNW

    def body(conv_ref, bd_ref, al_ref, dt_ref, dq_ref, dk_ref, dv_ref, dbg_ref,
             dconv_ref, dbd_ref, dal_ref, ddt_ref):
        i = pl.program_id(0)
        for blk in range(3 * DN_H):
            sl = slice(128 * blk, 128 * blk + 128)
            src = (dq_ref, dk_ref, dv_ref)[blk // DN_H]
            ct = src[:, 128 * (blk % DN_H):128 * (blk % DN_H) + 128]
            fn = _qk_math if blk < 2 * DN_H else _silu
            _, vjp = jax.vjp(fn, conv_ref[:, sl])
            dconv_ref[:, sl] = vjp(ct)[0]
        dal = jnp.zeros((1, 128), F32)
        ddt = jnp.zeros((1, 128), F32)
        for r0 in range(0, tm, PAIR):
            rs = slice(r0, r0 + PAIR)
            _, vjp = jax.vjp(_gate_math, bd_ref[rs, :], al_ref[...], dt_ref[...])
            dbd, dal_p, ddt_p = vjp(dbg_ref[rs, :])
            dbd_ref[rs, :] = dbd.astype(BF)
            dal, ddt = dal + dal_p, ddt + ddt_p

        @pl.when(i == 0)
        def _():
            dal_ref[...] = dal
            ddt_ref[...] = ddt

        @pl.when(i > 0)
        def _():
            dal_ref[...] += dal
            ddt_ref[...] += ddt

    one = pl.BlockSpec((1, 128), lambda i: (0, 0))
    row = pl.BlockSpec((tm, wq), lambda i: (i, 0))
    hd = pl.BlockSpec((tm, DNW), lambda i: (i, 0))
    st = pl.BlockSpec((tm, 128), lambda i: (i, 0))
    return pl.pallas_call(
        body, grid=(T // tm,),
        in_specs=[row, pl.BlockSpec((tm, 128), lambda i: (i, BD_BLK)), one, one, hd, hd, hd, st],
        out_specs=[row, st, one, one],
        out_shape=[jax.ShapeDtypeStruct((T, wq), F32), jax.ShapeDtypeStruct((T, 128), BF),
                   jax.ShapeDtypeStruct((1, 128), F32), jax.ShapeDtypeStruct((1, 128), F32)],
        name=name, compiler_params=_cp())(conv, z_dn, alog_row, dtb_row, dqn, dkn, dvn, dbg)


def _dn_conv_bwd(dconv, z_dn, conv_w8, name):
    tm = 512
    wq = 3 * DNW
    last = T // tm - 1

    def body(dc_ref, dcn_ref, raw_ref, halo_ref, w_ref, draw_ref, dw_ref):
        i = pl.program_id(0)
        dc = dc_ref[...]
        nxt = jnp.where(i < last, dcn_ref[...], 0.0)
        cur = raw_ref[...]
        halo = jnp.where(i > 0, halo_ref[...], 0.0)
        w = w_ref[...]
        draw = jnp.zeros((tm, wq), F32)
        rows = []
        for j in range(4):
            draw = draw + _shift_up(dc, nxt, 3 - j) * w[j:j + 1, :]
            rows.append(jnp.sum(dc * _shift_down(cur, halo, 3 - j), axis=0, keepdims=True))
        draw_ref[...] = draw.astype(BF)
        part = jnp.concatenate(rows + [jnp.zeros((4, wq), F32)], axis=0)

        @pl.when(i == 0)
        def _():
            dw_ref[...] = part

        @pl.when(i > 0)
        def _():
            dw_ref[...] += part

    row = pl.BlockSpec((tm, wq), lambda i: (i, 0))
    return pl.pallas_call(
        body, grid=(T // tm,),
        in_specs=[row, pl.BlockSpec((8, wq), lambda i: (jnp.minimum((i + 1) * (tm // 8), T // 8 - 1), 0)),
                  row, pl.BlockSpec((8, wq), lambda i: (jnp.maximum(i * (tm // 8) - 1, 0), 0)),
                  pl.BlockSpec((8, wq), lambda i: (0, 0))],
        out_specs=[row, pl.BlockSpec((8, wq), lambda i: (0, 0))],
        out_shape=[jax.ShapeDtypeStruct((T, wq), BF), jax.ShapeDtypeStruct((8, wq), F32)],
        name=name, compiler_params=_cp())(dconv, dconv, z_dn, z_dn, conv_w8)


def _h3(a, b, dims=NN):
    return lax.dot_general(a, b, dims, precision=lax.Precision.HIGH, preferred_element_type=F32)


@jax.custom_vjp
def _inverse_given(a_mat, tinv):
    return tinv


def _inverse_given_fwd(a_mat, tinv):
    return tinv, tinv


def _inverse_given_bwd(tinv, g):
    return -_bdot_tn(tinv, _bdot_nt(g, tinv)), jnp.zeros_like(tinv)


_inverse_given.defvjp(_inverse_given_fwd, _inverse_given_bwd)


@jax.custom_vjp
def _h3_lo(a, b):
    return _h3(a, b)


def _h3_lo_fwd(a, b):
    return _h3(a, b), (a, b)


def _h3_lo_bwd(res, g):
    a, b = res
    gb = g.astype(BF)
    return _dg(gb, b.astype(BF), NT), _dg(a.astype(BF), gb, TN)


_h3_lo.defvjp(_h3_lo_fwd, _h3_lo_bwd)


def _prep_head(q, k, v, bgc, h):
    beta = _col(bgc, h)
    gc = jnp.broadcast_to(_col(bgc, DN_H + h), (PAIR, 128))
    glast = jnp.broadcast_to(_col(bgc, 2 * DN_H + h), (PAIR, 128))
    ri = _iota((PAIR, PAIR), 0)
    ci = _iota((PAIR, PAIR), 1)
    same = (ri // CH) == (ci // CH)
    causal = same & (ci <= ri)
    strict = same & (ci < ri)
    eye = ri == ci
    gc_cols = _hdot(jnp.ones((PAIR, PAIR), F32), jnp.where(eye, gc, 0.0))
    decay = jnp.exp(jnp.where(causal, gc - gc_cols, NEG))
    egc = jnp.exp(gc)
    kb = k * beta
    a_mat = jnp.where(strict, _bdot_nt(kb, k) * decay, 0.0)
    qs = q * (DN_HD ** -0.5)
    attn = jnp.where(causal, _bdot_nt(qs, k) * decay, 0.0)
    return a_mat, (v * beta, kb * egc, qs * egc, k * jnp.exp(glast - gc), attn, jnp.exp(glast))


PREP_DTYPES = (F32, BF, BF, BF, BF, F32)


def _prep_tail(tinv, ctx):
    vb, kbe, qg, kdec, attn, decb = ctx
    outs = (_h3_lo(tinv, vb), _h3_lo(tinv, kbe), qg, kdec, attn, decb)
    return tuple(o.astype(dt) for o, dt in zip(outs, PREP_DTYPES))


def _inverses(a_mats):
    eye = jnp.where(_iota((PAIR, PAIR), 0) == _iota((PAIR, PAIR), 1), 1.0, 0.0)
    ps = [-a for a in a_mats]
    tinvs = [eye + p for p in ps]
    for _ in range(5):
        ps = [_h3(p, p) for p in ps]
        tinvs = [t + _h3(t, p) for t, p in zip(tinvs, ps)]
    return tinvs


def _dn_prep_fwd(qkv, bg, name):
    rows = 1024
    hd = lambda off: pl.BlockSpec((rows, 128), lambda g, h: (g, off + h))
    out = pl.BlockSpec((rows, 128), lambda g, h: (g, h))

    def body(q_ref, k_ref, v_ref, bg_ref, *outs):
        h = pl.program_id(1)
        spans = [slice(PAIR * pr, PAIR * pr + PAIR) for pr in range(rows // PAIR)]
        heads = [_prep_head(q_ref[rs, :], k_ref[rs, :], v_ref[rs, :], bg_ref[rs, :], h) for rs in spans]
        tinvs = _inverses([a for a, _ in heads])
        for rs, tinv, (_, ctx) in zip(spans, tinvs, heads):
            for o_ref, val in zip(outs, _prep_tail(tinv, ctx) + (tinv,)):
                o_ref[rs, :] = val

    return pl.pallas_call(
        body, grid=(T // rows, DN_H),
        in_specs=[hd(0), hd(DN_H), hd(2 * DN_H), pl.BlockSpec((rows, 128), lambda g, h: (g, 0))],
        out_specs=[out] * 7, out_shape=[jax.ShapeDtypeStruct((T, DNW), dt) for dt in PREP_DTYPES + (F32,)],
        name=name, compiler_params=_cp())(qkv, qkv, qkv, bg)


def _dn_prep_bwd(qkv, bg, tinv, cts, name):
    rows = 1024
    hd = lambda off: pl.BlockSpec((rows, 128), lambda g, h: (g, off + h))
    out = pl.BlockSpec((rows, 128), lambda g, h: (g, h))
    st = pl.BlockSpec((rows, 128), lambda g, h: (g, 0))

    def body(q_ref, k_ref, v_ref, bg_ref, ti_ref, c0, c1, c2, c3, c4, c5, dq_ref, dk_ref, dv_ref, dbg_ref):
        h = pl.program_id(1)
        spans = [slice(PAIR * pr, PAIR * pr + PAIR) for pr in range(rows // PAIR)]
        tis = [ti_ref[rs, :] for rs in spans]

        def joint(qs, ks, vs, bs):
            heads = [_prep_head(q, k, v, b, h) for q, k, v, b in zip(qs, ks, vs, bs)]
            return [_prep_tail(_inverse_given(a, ti), ctx) for (a, ctx), ti in zip(heads, tis)]

        _, vjp = jax.vjp(joint, *[[r[rs, :] for rs in spans] for r in (q_ref, k_ref, v_ref, bg_ref)])
        dqs, dks, dvs, dbs = vjp([tuple(c[rs, :] for c in (c0, c1, c2, c3, c4, c5)) for rs in spans])
        for rs, dq, dk, dv in zip(spans, dqs, dks, dvs):
            dq_ref[rs, :] = dq
            dk_ref[rs, :] = dk
            dv_ref[rs, :] = dv
        dbg_all = jnp.concatenate(dbs, axis=0)

        @pl.when(h == 0)
        def _():
            dbg_ref[...] = dbg_all

        @pl.when(h > 0)
        def _():
            dbg_ref[...] += dbg_all

    return pl.pallas_call(
        body, grid=(T // rows, DN_H),
        in_specs=[hd(0), hd(DN_H), hd(2 * DN_H), st] + [out] * 7,
        out_specs=[out, out, out, st],
        out_shape=[jax.ShapeDtypeStruct((T, DNW), F32)] * 3 + [jax.ShapeDtypeStruct((T, 128), F32)],
        name=name, compiler_params=_cp())(qkv, qkv, qkv, bg, tinv, *cts)


def _step_math(ss, us, ws, qgs, kdecs, attns, decbs, sub):
    z = jnp.zeros((CH, 128), F32)
    vnews = [u - _bdot_nn(w, s) for u, w, s in zip(us, ws, ss)]
    vfulls = [jnp.concatenate([v, z] if sub == 0 else [z, v], axis=0) for v in vnews]
    os = [_bdot_nn(qg, s) + _bdot_nn(attn, vf) for qg, s, attn, vf in zip(qgs, ss, attns, vfulls)]
    decs = [jnp.sum(decb, axis=0, keepdims=True) * (1.0 / CH) for decb in decbs]
    return [s * dec + _bdot_tn(kdec, v) for s, dec, kdec, v in zip(ss, decs, kdecs, vnews)], os


SCAN_ROWS = 256


def _dn_scan_fwd(prep, name):
    nstep = T // SCAN_ROWS
    nch = SCAN_ROWS // CH
    row = pl.BlockSpec((SCAN_ROWS, DNW), lambda p: (p, 0))

    def body(u_ref, w_ref, qg_ref, kd_ref, at_ref, db_ref, o_ref, ss_ref, s_ref):
        @pl.when(pl.program_id(0) == 0)
        def _():
            s_ref[...] = jnp.zeros_like(s_ref)

        lanes = [slice(128 * h, 128 * h + 128) for h in range(DN_H)]
        states = [s_ref[h] for h in range(DN_H)]
        for ch in range(nch):
            rs = slice(CH * ch, CH * ch + CH)
            for h in range(DN_H):
                ss_ref[ch, h] = states[h]
            states, os = _step_math(states, *[[r[rs, ls] for ls in lanes]
                                              for r in (u_ref, w_ref, qg_ref, kd_ref, at_ref, db_ref)], ch % 2)
            for ls, o in zip(lanes, os):
                o_ref[rs, ls] = o
        for h in range(DN_H):
            s_ref[h] = states[h]

    return pl.pallas_call(
        body, grid=(nstep,), in_specs=[row] * 6,
        out_specs=[row, pl.BlockSpec((nch, DN_H, 128, 128), lambda p: (p, 0, 0, 0))],
        out_shape=[jax.ShapeDtypeStruct((T, DNW), F32), jax.ShapeDtypeStruct((T // CH, DN_H, 128, 128), F32)],
        scratch_shapes=[pltpu.VMEM((DN_H, 128, 128), F32)],
        name=name, compiler_params=_cp())(*prep)


def _dn_scan_bwd(prep, states, do, name):
    nstep = T // SCAN_ROWS
    nch = SCAN_ROWS // CH
    row = pl.BlockSpec((SCAN_ROWS, DNW), lambda p: (nstep - 1 - p, 0))

    def body(u_ref, w_ref, qg_ref, kd_ref, at_ref, db_ref, ss_ref, do_ref, *rest):
        outs, ds_ref = rest[:6], rest[6]

        @pl.when(pl.program_id(0) == 0)
        def _():
            ds_ref[...] = jnp.zeros_like(ds_ref)

        lanes = [slice(128 * h, 128 * h + 128) for h in range(DN_H)]
        dss = [ds_ref[h] for h in range(DN_H)]
        for ch in reversed(range(nch)):
            rs = slice(CH * ch, CH * ch + CH)
            args = [[ss_ref[ch, h] for h in range(DN_H)]] + [
                [r[rs, ls] for ls in lanes] for r in (u_ref, w_ref, qg_ref, kd_ref, at_ref, db_ref)]
            _, vjp = jax.vjp(functools.partial(_step_math, sub=ch % 2), *args)
            cts = vjp((dss, [do_ref[rs, ls] for ls in lanes]))
            dss = cts[0]
            for o_ref, vals in zip(outs, cts[1:]):
                for ls, val in zip(lanes, vals):
                    o_ref[rs, ls] = val
        for h in range(DN_H):
            ds_ref[h] = dss[h]

    return pl.pallas_call(
        body, grid=(nstep,),
        in_specs=[row] * 6 + [pl.BlockSpec((nch, DN_H, 128, 128), lambda p: (nstep - 1 - p, 0, 0, 0)), row],
        out_specs=[row] * 6, out_shape=[jax.ShapeDtypeStruct((T, DNW), dt) for dt in PREP_DTYPES],
        scratch_shapes=[pltpu.VMEM((DN_H, 128, 128), F32)],
        name=name, compiler_params=_cp())(*prep, states, do)


def _post_math(o, gate, wrow):
    return o * lax.rsqrt(jnp.mean(o * o, axis=-1, keepdims=True) + NORM_EPS) * wrow * _silu(gate)


def _dn_post_fwd(o, z_dn, dn_norm, name):
    tm = 512
    row = pl.BlockSpec((tm, DNW), lambda i: (i, 0))

    def body(o_ref, g_ref, w_ref, y_ref):
        for h in range(DN_H):
            ls = slice(128 * h, 128 * h + 128)
            y_ref[:, ls] = _post_math(o_ref[:, ls], g_ref[:, ls], w_ref[...]).astype(BF)

    return pl.pallas_call(
        body, grid=(T // tm,),
        in_specs=[row, pl.BlockSpec((tm, DNW), lambda i: (i, 3)), pl.BlockSpec((1, 128), lambda i: (0, 0))],
        out_specs=row, out_shape=jax.ShapeDtypeStruct((T, DNW), BF),
        name=name, compiler_params=_cp())(o, z_dn, dn_norm)


def _dn_post_bwd(o, z_dn, dn_norm, dy, name):
    tm = 512
    row = pl.BlockSpec((tm, DNW), lambda i: (i, 0))
    one = pl.BlockSpec((1, 128), lambda i: (0, 0))

    def body(o_ref, g_ref, w_ref, dy_ref, do_ref, dg_ref, dw_ref):
        i = pl.program_id(0)
        dw = jnp.zeros((1, 128), F32)
        for h in range(DN_H):
            ls = slice(128 * h, 128 * h + 128)
            _, vjp = jax.vjp(_post_math, o_ref[:, ls], g_ref[:, ls], w_ref[...])
            do, dg, dwh = vjp(dy_ref[:, ls].astype(F32))
            do_ref[:, ls] = do
            dg_ref[:, ls] = dg.astype(BF)
            dw = dw + dwh

        @pl.when(i == 0)
        def _():
            dw_ref[...] = dw

        @pl.when(i > 0)
        def _():
            dw_ref[...] += dw

    return pl.pallas_call(
        body, grid=(T // tm,),
        in_specs=[row, pl.BlockSpec((tm, DNW), lambda i: (i, 3)), one, pl.BlockSpec((tm, DNW), lambda i: (i, 1))],
        out_specs=[row, row, one],
        out_shape=[jax.ShapeDtypeStruct((T, DNW), F32), jax.ShapeDtypeStruct((T, DNW), BF),
                   jax.ShapeDtypeStruct((1, 128), F32)],
        name=name, compiler_params=_cp())(o, z_dn, dn_norm, dy)


HBM = pl.BlockSpec(memory_space=pltpu.HBM)
SEM = pl.BlockSpec(memory_space=pltpu.SEMAPHORE)
EFFECT = pltpu.SideEffectType.DATAFLOW_SIDE_EFFECTING
N_PEER = N_DEV - 1


ALL_PEERS = (1, 2, 4, 3, 5, 6, 7)
FIRST_HOP = (1, 2, 4, 6)
FORWARDED = (2, 4, 6)


def _peers(x, y, c, ks=ALL_PEERS):
    return [(k, (x ^ (k >> 2), y ^ ((k >> 1) & 1), c ^ (k & 1))) for k in ks]


def _exchange_copy(ins, lands, ssems, rsems, scatter, t, k, pos, me):
    px, py, pc = pos
    src = ins[t].at[4 * px + 2 * py + pc] if scatter else ins[t]
    return pltpu.make_async_remote_copy(
        src_ref=src, dst_ref=lands[t].at[me], send_sem=ssems[t].at[k - 1], recv_sem=rsems[t].at[k - 1],
        device_id=pos, device_id_type=MESH_ID)


def _xstart(bufs, scatter, name, ks=ALL_PEERS):
    nt = len(bufs)
    lands = [lax.empty((N_DEV,) + tuple(b.shape[1:] if scatter else b.shape), b.dtype) for b in bufs]

    def body(*refs):
        ins, lnd = refs[:nt], refs[nt:2 * nt]
        ssems, rsems = refs[2 * nt:3 * nt], refs[3 * nt:4 * nt]
        token = refs[-1]
        x, y, c = lax.axis_index("x"), lax.axis_index("y"), lax.axis_index("c")
        me = 4 * x + 2 * y + c
        for t in range(nt):
            for k, pos in _peers(x, y, c, ks):
                _exchange_copy(ins, lnd, ssems, rsems, scatter, t, k, pos, me).start()
        token[...] = jnp.zeros_like(token)

    both = list(bufs) + lands
    res = pl.pallas_call(
        body, name=name,
        out_shape=[pltpu.SemaphoreType.DMA((N_PEER,))] * (2 * nt)
        + [pltpu.HBM(b.shape, b.dtype) for b in both] + [jax.ShapeDtypeStruct((8, 128), F32)],
        in_specs=[HBM] * (2 * nt),
        out_specs=[SEM] * (2 * nt) + [HBM] * (2 * nt) + [pl.BlockSpec(memory_space=pltpu.VMEM)],
        input_output_aliases={i: 2 * nt + i for i in range(2 * nt)},
        compiler_params=pltpu.CompilerParams(has_side_effects=EFFECT),
    )(*[pltpu.with_memory_space_constraint(b, pltpu.HBM) for b in both])
    return res[:nt], res[nt:2 * nt], res[2 * nt:3 * nt], res[3 * nt:4 * nt], res[-1][0, 0]


def _xwait(ssems, rsems, thrus, lands, scatter, after, name, ks=ALL_PEERS):
    nt = len(lands)

    def body(*refs):
        ins, lnd = refs[:nt], refs[nt:2 * nt]
        ss, rs = refs[2 * nt:3 * nt], refs[3 * nt:4 * nt]
        x, y, c = lax.axis_index("x"), lax.axis_index("y"), lax.axis_index("c")
        me = 4 * x + 2 * y + c
        for t in range(nt):
            for k, pos in _peers(x, y, c, ks):
                cp = _exchange_copy(ins, lnd, ss, rs, scatter, t, k, pos, me)
                cp.wait_send()
                cp.wait_recv()

    both = list(thrus) + list(lands)
    res = pl.pallas_call(
        body, name=name, out_shape=[pltpu.HBM(b.shape, b.dtype) for b in both],
        in_specs=[HBM] * (2 * nt) + [SEM] * (2 * nt) + [ANY], out_specs=[HBM] * (2 * nt),
        input_output_aliases={i: i for i in range(2 * nt)},
        compiler_params=pltpu.CompilerParams(has_side_effects=EFFECT),
    )(*both, *ssems, *rsems, after)
    return res[:nt], res[nt:]


def _forward_copy(lands, ssems, rsems, t, k, pos, sibling):
    px, py, pc = pos
    slot = lands[t].at[4 * px + 2 * py + pc]
    return pltpu.make_async_remote_copy(
        src_ref=slot, dst_ref=slot, send_sem=ssems[t].at[k - 1], recv_sem=rsems[t].at[k - 1],
        device_id=sibling, device_id_type=MESH_ID)


def _fstart(lands, name):
    nt = len(lands)

    def body(*refs):
        lnd = refs[:nt]
        ssems, rsems = refs[nt:2 * nt], refs[2 * nt:3 * nt]
        token = refs[-1]
        x, y, c = lax.axis_index("x"), lax.axis_index("y"), lax.axis_index("c")
        for t in range(nt):
            for k, pos in _peers(x, y, c, FORWARDED):
                _forward_copy(lnd, ssems, rsems, t, k, pos, (x, y, c ^ 1)).start()
        token[...] = jnp.zeros_like(token)

    res = pl.pallas_call(
        body, name=name,
        out_shape=[pltpu.SemaphoreType.DMA((N_PEER,))] * (2 * nt)
        + [pltpu.HBM(b.shape, b.dtype) for b in lands] + [jax.ShapeDtypeStruct((8, 128), F32)],
        in_specs=[HBM] * nt,
        out_specs=[SEM] * (2 * nt) + [HBM] * nt + [pl.BlockSpec(memory_space=pltpu.VMEM)],
        input_output_aliases={i: 2 * nt + i for i in range(nt)},
        compiler_params=pltpu.CompilerParams(has_side_effects=EFFECT),
    )(*[pltpu.with_memory_space_constraint(b, pltpu.HBM) for b in lands])
    return res[:nt], res[nt:2 * nt], res[2 * nt:3 * nt], res[-1][0, 0]


def _fwait(ssems, rsems, lands, after, name):
    nt = len(lands)

    def body(*refs):
        lnd = refs[:nt]
        ss, rs = refs[nt:2 * nt], refs[2 * nt:3 * nt]
        x, y, c = lax.axis_index("x"), lax.axis_index("y"), lax.axis_index("c")
        for t in range(nt):
            for k, pos in _peers(x, y, c, FORWARDED):
                cp = _forward_copy(lnd, ss, rs, t, k, pos, (x, y, c ^ 1))
                cp.wait_send()
                cp.wait_recv()

    return pl.pallas_call(
        body, name=name, out_shape=[pltpu.HBM(b.shape, b.dtype) for b in lands],
        in_specs=[HBM] * nt + [SEM] * (2 * nt) + [ANY], out_specs=[HBM] * nt,
        input_output_aliases={i: i for i in range(nt)},
        compiler_params=pltpu.CompilerParams(has_side_effects=EFFECT),
    )(*lands, *ssems, *rsems, after)


def _adam(recv, w, m, v, tr, name):
    _, r, c = w.shape
    n_part = recv.shape[0]
    c1 = np.float32(1.0 - ADAM_B1 ** ADAM_STEP)
    c2 = np.float32(1.0 - ADAM_B2 ** ADAM_STEP)

    def body(r_ref, w_ref, m_ref, v_ref, g_ref, d_ref, mo_ref, vo_ref):
        g = r_ref[0].astype(F32)
        for s in range(1, n_part):
            g = g + r_ref[s].astype(F32)
        mn = ADAM_B1 * m_ref[0] + (1.0 - ADAM_B1) * g
        vn = ADAM_B2 * v_ref[0] + (1.0 - ADAM_B2) * (g * g)
        g_ref[0] = g
        mo_ref[0] = mn
        vo_ref[0] = vn
        d_ref[0] = -ADAM_LR * ((mn / c1) / (jnp.sqrt(vn / c2) + ADAM_EPS) + ADAM_WD * w_ref[0])

    one = pl.BlockSpec((1, tr, c), lambda i: (0, i, 0))
    return pl.pallas_call(
        body, grid=(r // tr,), in_specs=[pl.BlockSpec((n_part, tr, c), lambda i: (0, i, 0)), one, one, one],
        out_specs=[one] * 4, out_shape=[jax.ShapeDtypeStruct((1, r, c), F32)] * 4,
        name=name, compiler_params=_cp())(recv, w, m, v)


def _local_step(x, target, sp, need, ahead, emit):
    g = {}
    x0, h1 = x, _rms_fwd(x, sp["norm_ffn1"], "ffn1_norm")
    act1, saved1 = _ffn_up(h1, lambda kind, a: need("w" + kind + "1", a),
                           lambda kind, a: ahead("w" + kind + "1", a), "ffn1")
    x1, h2 = _mm_nn_resnorm([(act1, saved1[-1])], x0, 0.5, sp["norm_mix"], 512, "ffn1_down_norm",
                            after=ahead("win_a", act1))
    win_a, win_d = need("win_a", h2), need("win_d", h2)
    conv_w8, wout = need("conv_w8", h2), need("wout", h2)
    z_at = _mm_nn(h2, win_a, F32, 1024, 768, "mix_in_attn")
    z_dn = _mm_nn(h2, win_d, F32, 1024, 768, "mix_in_dn")

    conv, qkvn, bg = _dn_pre_fwd(z_dn, conv_w8, sp["alog_row"], sp["dtb_row"], "dn_pre")
    *prep, tinv = _dn_prep_fwd(qkvn, bg, "dn_prep")
    o_dn, states = _dn_scan_fwd(prep, "dn_scan")
    dn_b = _dn_post_fwd(o_dn, z_dn, sp["dn_norm"], "dn_post")
    attn_f, attn_b, lse = _attn_fwd(z_at, "attn_fwd", after=ahead("wg2", dn_b))

    x2, h3 = _mm_nn_resnorm([(attn_b, wout[:AW]), (dn_b, wout[AW:])], x1, 1.0, sp["norm_ffn2"], 512,
                            "mix_out_norm")
    act2, saved2 = _ffn_up(h3, lambda kind, a: need("w" + kind + "2", a),
                           lambda kind, a: ahead("w" + kind + "2", a), "ffn2")

    dx3, dys3, g["norm_final"], loss8 = _down_loss_bwd(x2, act2, saved2[-1], sp["norm_final"], target,
                                                       "ffn2_down_loss")
    dx2, dx2b, g["norm_ffn2"] = _ffn_bwd(
        x2, sp["norm_ffn2"], saved2, dx3, dys3, 1.0,
        lambda kind, dw: emit(kind + "2", {"w" + kind + "2": dw}), "ffn2b")

    zero = emit("wout", {"wout": jnp.concatenate([_mm_tn(attn_b, dx2b, BF, 512, 1024, "mix_out_dw_a"),
                                                  _mm_tn(dn_b, dx2b, BF, 512, 1024, "mix_out_dw_d")], axis=0)})
    dmix = _mm_nt(dx2b, wout, F32, 1024, 1024, "mix_out_dx", after=zero)

    dq, dk, dv = _attn_bwd(z_at, dmix, attn_f, lse, "attn_bwd")

    do_dn, dgate, g["dn_norm"] = _dn_post_bwd(o_dn, z_dn, sp["dn_norm"], dmix, "dn_post_b")
    cts = _dn_scan_bwd(prep, states, do_dn, "dn_scan_b")
    dqn, dkn, dvn, dbg = _dn_prep_bwd(qkvn, bg, tinv, cts, "dn_prep_b")
    dconv, dbd, g["alog_row"], g["dtb_row"] = _dn_pre_bwd(
        conv, z_dn, sp["alog_row"], sp["dtb_row"], dqn, dkn, dvn, dbg, "dn_pre_b")
    draw, dconv_w8 = _dn_conv_bwd(dconv, z_dn, conv_w8, "dn_conv_b")

    pieces = ((dq, win_a, 0), (dk, win_a, 1), (dv, win_a, 2),
              (draw, win_d, 0), (dgate, win_d, 3 * DNW // DNW), (dbd, win_d, BD_BLK))
    zero = emit("win", {"w_in_parts": [_mm_tn(h2, p[0], BF, 512, min(p[0].shape[1], 768), "mix_in_dw%d" % i)
                                       for i, p in enumerate(pieces)], "conv_w8": dconv_w8})
    dx1, dys1, g["norm_mix"] = _mm_rms_bwd(pieces, NT, x1, sp["norm_mix"], dx2, 0.5, 512, "mix_in_dx_dnorm",
                                           after=zero)
    dx0, _, g["norm_ffn1"] = _ffn_bwd(
        x0, sp["norm_ffn1"], saved1, dx1, dys1, 1.0,
        lambda kind, dw: emit(kind + "1", {"w" + kind + "1": dw}), "ffn1b")
    return loss8[0, 0], dx0, g


def _cols_from_shards(gathered):
    n, r, c = gathered.shape
    return jnp.transpose(gathered, (1, 0, 2)).reshape(r, n * c)


def _shards_from_cols(full, dtype):
    r, nc = full.shape
    return jnp.transpose(full.reshape(r, N_DEV, nc // N_DEV), (1, 0, 2)).astype(dtype)


def _lane_row(vec4):
    return jnp.zeros((1, 128), F32).at[:, DN_H:2 * DN_H].set(vec4.astype(F32))


WEIGHT_SOURCES = {"wg1": "gate1", "wu1": "up1", "wd1": "down1", "win_a": "w_in", "win_d": "w_in",
                  "conv_w8": "conv_w", "wout": "w_out", "wg2": "gate2", "wu2": "up2", "wd2": "down2"}
TRANSPOSED = ("gate1", "up1", "gate2", "up2")


def _build_weights(name, gath):
    if name in ("wg1", "wu1", "wd1", "wg2", "wu2", "wd2"):
        return {name: gath[WEIGHT_SOURCES[name]].reshape(F, D)}
    if name in ("win_a", "win_d"):
        w_in = _cols_from_shards(gath["w_in"])
        c0 = 3 * AW + 3 * DNW
        win_d = jnp.concatenate([w_in[:, ZA:c0], w_in[:, c0 + 2 * DN_H:], w_in[:, c0:c0 + 2 * DN_H],
                                 jnp.zeros((D, ZP - IN_COLS), w_in.dtype)], axis=1)
        return {"win_a": w_in[:, :ZA], "win_d": win_d}
    if name == "wout":
        return {name: gath["w_out"].reshape(D, D)}
    conv = _cols_from_shards(gath["conv_w"])
    return {"conv_w8": jnp.concatenate([conv, jnp.zeros((4, 3 * DNW), F32)], axis=0)}


def _small_params(norm_ffn1, norm_mix, norm_ffn2, norm_final, a_log, dt_bias, dn_norm):
    return {"norm_ffn1": norm_ffn1, "norm_mix": norm_mix, "norm_ffn2": norm_ffn2,
            "norm_final": norm_final.reshape(1, D), "alog_row": _lane_row(a_log), "dtb_row": _lane_row(dt_bias),
            "dn_norm": dn_norm}


def _grad_slabs(group, g):
    if group[0] in "gud":
        return {WEIGHT_SOURCES["w" + group]: g["w" + group].reshape(N_DEV, F // N_DEV, D)}
    if group == "wout":
        return {"w_out": g["wout"].reshape(N_DEV, D // N_DEV, D)}
    aq, ak, av, dqkv, gate, bd = g["w_in_parts"]
    g_in = jnp.concatenate([aq, ak, av, dqkv, bd[:, :2 * DN_H], gate], axis=1)
    return {"w_in": _shards_from_cols(g_in, BF), "conv_w": _shards_from_cols(g["conv_w8"][:4], F32)}


SMALL_ROWS = 40


def _small_pack(norm_ffn1, norm_mix, norm_ffn2, norm_final, dn_norm, alog_row, dtb_row, loss=None):
    rows = [a.reshape(8, 128) for a in (norm_ffn1, norm_mix, norm_ffn2, norm_final)]
    loss_row = jnp.zeros((1, 128), F32) if loss is None else jnp.broadcast_to(loss.reshape(1, 1), (1, 128))
    rows += [dn_norm.reshape(1, 128), alog_row, dtb_row, loss_row, jnp.zeros((SMALL_ROWS - 36, 128), F32)]
    return jnp.concatenate(rows, axis=0)


def _small_unpack(pk):
    pk = pk[0]
    return (pk[0:8].reshape(1, D), pk[8:16].reshape(1, D), pk[16:24].reshape(1, D), pk[24:32].reshape(D),
            pk[32:33], pk[33:34, DN_H:2 * DN_H], pk[34:35, DN_H:2 * DN_H])


ADAM_TILE = {"gate1": 176, "up1": 176, "down1": 176, "gate2": 176, "up2": 176, "down2": 176,
             "w_in": 256, "w_out": 128, "conv_w": 4}
BIG = ("gate1", "up1", "down1", "w_in", "w_out", "gate2", "up2", "down2", "conv_w")


def kernel(x, norm_ffn1, ffn1_gate, ffn1_up, ffn1_down, norm_mix, w_in, conv_w, a_log, dt_bias, dn_norm, w_out, norm_ffn2, ffn2_gate, ffn2_up, ffn2_down, norm_final, loss_target, m_norm_ffn1, m_ffn1_gate, m_ffn1_up, m_ffn1_down, m_norm_mix, m_w_in, m_conv_w, m_a_log, m_dt_bias, m_dn_norm, m_w_out, m_norm_ffn2, m_ffn2_gate, m_ffn2_up, m_ffn2_down, m_norm_final, v_norm_ffn1, v_ffn1_gate, v_ffn1_up, v_ffn1_down, v_norm_mix, v_w_in, v_conv_w, v_a_log, v_dt_bias, v_dn_norm, v_w_out, v_norm_ffn2, v_ffn2_gate, v_ffn2_up, v_ffn2_down, v_norm_final):
    w = {"gate1": ffn1_gate, "up1": ffn1_up, "down1": ffn1_down, "w_in": w_in, "w_out": w_out,
         "gate2": ffn2_gate, "up2": ffn2_up, "down2": ffn2_down, "conv_w": conv_w}
    m = {"gate1": m_ffn1_gate, "up1": m_ffn1_up, "down1": m_ffn1_down, "w_in": m_w_in, "w_out": m_w_out,
         "gate2": m_ffn2_gate, "up2": m_ffn2_up, "down2": m_ffn2_down, "conv_w": m_conv_w}
    v = {"gate1": v_ffn1_gate, "up1": v_ffn1_up, "down1": v_ffn1_down, "w_in": v_w_in, "w_out": v_w_out,
         "gate2": v_ffn2_gate, "up2": v_ffn2_up, "down2": v_ffn2_down, "conv_w": v_conv_w}

    me = 4 * lax.axis_index("x") + 2 * lax.axis_index("y") + lax.axis_index("c")
    own_slot = lambda land, mine: lax.dynamic_update_index_in_dim(land, mine, me, 0)

    ag_order = ("gate1", "up1", "down1", "w_in", "conv_w", "w_out", "gate2", "up2", "down2")
    ag_groups = (("gate1",), ("up1",), ("down1",), ("w_in", "conv_w", "w_out"), ("gate2", "up2", "down2"))
    pos = {n: i for i, n in enumerate(ag_order)}

    def shard(n):
        if n == "conv_w":
            return w[n][0]
        return (w[n][0].T if n in TRANSPOSED else w[n][0]).astype(BF)

    ss, rs, thru, land, zero = _xstart([shard(n) for n in ag_order], False, "weights_start", FIRST_HOP)
    gath, built, on_its_way = {}, {}, {}
    group_of = lambda name: [i for i, grp in enumerate(ag_groups) if WEIGHT_SOURCES[name] in grp][0]

    def ahead(name, after):
        gi = group_of(name)
        if WEIGHT_SOURCES[name] in gath or gi in on_its_way:
            return None
        ids = [pos[n] for n in ag_groups[gi]]
        thrus, lands = _xwait([ss[i] for i in ids], [rs[i] for i in ids], [thru[i] for i in ids],
                              [land[i] for i in ids], False, after, "weights_wait%d" % gi, FIRST_HOP)
        fss, frs, lands, token = _fstart(lands, "weights_forward%d" % gi)
        on_its_way[gi] = (thrus, fss, frs, lands)
        return token

    def need(name, after):
        if name not in built:
            if WEIGHT_SOURCES[name] not in gath:
                gi = group_of(name)
                ahead(name, after)
                thrus, fss, frs, lands = on_its_way.pop(gi)
                lands = _fwait(fss, frs, lands, after, "weights_forward_wait%d" % gi)
                for n, t, l in zip(ag_groups[gi], thrus, lands):
                    gath[n] = own_slot(l, t)
            built.update(_build_weights(name, gath))
        return built[name]

    pending = []

    def emit(group, grads):
        slabs = grads if group == "small" else _grad_slabs(group, grads)
        names = list(slabs)
        started = _xstart([slabs[n] for n in names], True, "grads_start_" + group)
        pending.append((group, names) + started[:4])
        return started[4]

    sp = _small_params(norm_ffn1 + zero, norm_mix, norm_ffn2, norm_final, a_log, dt_bias, dn_norm)
    loss_part, dx, g = _local_step(x[0], loss_target[0], sp, need, ahead, emit)
    small = _small_pack(g["norm_ffn1"], g["norm_mix"], g["norm_ffn2"], g["norm_final"], g["dn_norm"],
                        g["alog_row"], g["dtb_row"], loss_part)
    emit("small", {"small": jnp.broadcast_to(small[None], (N_DEV, SMALL_ROWS, 128))})

    pack = lambda a: _small_pack(*a)[None]
    res, after = {}, dx
    for group, names, gss, grs, gthru, gland in pending:
        thrus, lands = _xwait(gss, grs, gthru, gland, True, after, "grads_wait_" + group)
        for n, t, l in zip(names, thrus, lands):
            recv = own_slot(l, lax.dynamic_index_in_dim(t, me, 0, keepdims=False))
            if n == "small":
                res[n] = _adam(
                    recv,
                    pack((norm_ffn1, norm_mix, norm_ffn2, norm_final, dn_norm, _lane_row(a_log), _lane_row(dt_bias))),
                    pack((m_norm_ffn1, m_norm_mix, m_norm_ffn2, m_norm_final, m_dn_norm, _lane_row(m_a_log),
                          _lane_row(m_dt_bias))),
                    pack((v_norm_ffn1, v_norm_mix, v_norm_ffn2, v_norm_final, v_dn_norm, _lane_row(v_a_log),
                          _lane_row(v_dt_bias))),
                    SMALL_ROWS, "adam_small")
            elif n in TRANSPOSED:
                flip = lambda a: jnp.swapaxes(a, 1, 2)
                res[n] = [flip(o) for o in _adam(recv, flip(w[n]), flip(m[n]), flip(v[n]), ADAM_TILE[n],
                                                 "adam_" + n)]
            else:
                res[n] = _adam(recv, w[n], m[n], v[n], ADAM_TILE[n], "adam_" + n)
            after = res[n][0]
    res_s = res["small"]

    loss = res_s[0][0, 35, 0]
    outs = [loss, dx[None]]
    for k in range(4):
        n1, nm, n2, nf, dn, al, dt = _small_unpack(res_s[k])
        big = {n: res[n][k] for n in BIG}
        outs += [n1, big["gate1"], big["up1"], big["down1"], nm, big["w_in"], big["conv_w"], al, dt, dn,
                 big["w_out"], n2, big["gate2"], big["up2"], big["down2"], nf]
    return tuple(outs)
```

```python
import functools

import numpy as np
import jax
import jax.numpy as jnp
from jax import lax
from jax.experimental import pallas as pl
from jax.experimental.pallas import tpu as pltpu

T = 4096
D = 1024
F = 2816
N_DEV = 8
A_HEADS = 8
A_HD = 64
AW = A_HEADS * A_HD
DN_H = 4
DN_HD = 128
DNW = DN_H * DN_HD
CH = 64
PAIR = 2 * CH
ZA = 3 * AW
ZD = 3 * DNW + DNW + 256
ZP = ZA + ZD
BD_BLK = (3 * DNW + DNW) // 128
IN_COLS = 3592
PATTERNS = ((128, 1), (512, 4), (2048, 16))
NORM_EPS = 1e-6
L2_EPS = 1e-6
ADAM_LR, ADAM_B1, ADAM_B2, ADAM_EPS, ADAM_WD, ADAM_STEP = 0.001, 0.9, 0.999, 1e-08, 0.01, 10
VMEM_LIMIT = 56 * 1024 * 1024
NEG = -1e30

BF = jnp.bfloat16
F32 = jnp.float32
NN = (((1,), (0,)), ((), ()))
NT = (((1,), (1,)), ((), ()))
TN = (((0,), (0,)), ((), ()))
HI = lax.Precision.HIGHEST
MESH_ID = pl.DeviceIdType.MESH
ANY = pl.BlockSpec(memory_space=pl.ANY)


def _cp():
    return pltpu.CompilerParams(vmem_limit_bytes=VMEM_LIMIT)


def _dg(a, b, dims):
    return lax.dot_general(a, b, dims, preferred_element_type=F32)


def _hdot(a, b):
    return lax.dot_general(a, b, NN, precision=HI, preferred_element_type=F32)


def _make_bdot(dims, da_dims, da_swap, db_dims, db_swap):
    @jax.custom_vjp
    def f(a, b):
        return _dg(a.astype(BF), b.astype(BF), dims)

    def fwd(a, b):
        return f(a, b), (a, b)

    def bwd(res, g):
        a, b = res
        gb, ab, bb = g.astype(BF), a.astype(BF), b.astype(BF)
        da = _dg(bb, gb, da_dims) if da_swap else _dg(gb, bb, da_dims)
        db = _dg(gb, ab, db_dims) if db_swap else _dg(ab, gb, db_dims)
        return da.astype(a.dtype), db.astype(b.dtype)

    f.defvjp(fwd, bwd)
    return f


_bdot_nn = _make_bdot(NN, NT, False, TN, False)
_bdot_nt = _make_bdot(NT, NN, False, TN, True)
_bdot_tn = _make_bdot(TN, NT, True, NN, False)


def _iota(shape, dim):
    return lax.broadcasted_iota(jnp.int32, shape, dim)


def _col(x, idx):
    return jnp.sum(jnp.where(_iota(x.shape, 1) == idx, x, 0.0), axis=1, keepdims=True)


def _mm_nn(a, b, out_dtype, tm, tn, name):
    m, k = a.shape
    n = b.shape[1]

    def body(a_ref, b_ref, o_ref):
        o_ref[...] = _dg(a_ref[...], b_ref[...], NN).astype(out_dtype)

    return pl.pallas_call(
        body, grid=(m // tm, n // tn),
        in_specs=[pl.BlockSpec((tm, k), lambda i, j: (i, 0)), pl.BlockSpec((k, tn), lambda i, j: (0, j))],
        out_specs=pl.BlockSpec((tm, tn), lambda i, j: (i, j)),
        out_shape=jax.ShapeDtypeStruct((m, n), out_dtype), name=name, compiler_params=_cp())(a, b)


def _tie(body, after):
    if after is None:
        return body, [], []
    return (lambda tok_ref, *refs: body(*refs)), [ANY], [after.reshape(1, 1)]


def _mm_nt(a, b, out_dtype, tm, tb, name, after=None):
    m, c = a.shape
    kb = b.shape[0]

    def body(a_ref, b_ref, o_ref):
        o_ref[...] = _dg(a_ref[...], b_ref[...], NT).astype(out_dtype)

    body, tspec, tok = _tie(body, after)
    return pl.pallas_call(
        body, grid=(m // tm, kb // tb),
        in_specs=tspec + [pl.BlockSpec((tm, c), lambda i, j: (i, 0)), pl.BlockSpec((tb, c), lambda i, j: (j, 0))],
        out_specs=pl.BlockSpec((tm, tb), lambda i, j: (i, j)),
        out_shape=jax.ShapeDtypeStruct((m, kb), out_dtype), name=name, compiler_params=_cp())(*tok, a, b)


def _mm_tn(a, b, out_dtype, ta, tb, name, after=None):
    m, ka = a.shape
    nb = b.shape[1]

    def body(a_ref, b_ref, o_ref):
        o_ref[...] = _dg(a_ref[...], b_ref[...], TN).astype(out_dtype)

    body, tspec, tok = _tie(body, after)
    return pl.pallas_call(
        body, grid=(ka // ta, nb // tb),
        in_specs=tspec + [pl.BlockSpec((m, ta), lambda i, j: (0, i)), pl.BlockSpec((m, tb), lambda i, j: (0, j))],
        out_specs=pl.BlockSpec((ta, tb), lambda i, j: (i, j)),
        out_shape=jax.ShapeDtypeStruct((ka, nb), out_dtype), name=name, compiler_params=_cp())(*tok, a, b)


def _mm_nt_swiglu(h, wu_t, gate, tm, tb, name, after=None):
    m, c = h.shape
    kb = wu_t.shape[0]

    def body(h_ref, w_ref, g_ref, u_ref, a_ref):
        u = _dg(h_ref[...], w_ref[...], NT)
        g = g_ref[...].astype(F32)
        u_ref[...] = u.astype(BF)
        a_ref[...] = (g * jax.nn.sigmoid(g) * u).astype(BF)

    body, tspec, tok = _tie(body, after)
    tile = pl.BlockSpec((tm, tb), lambda i, j: (i, j))
    return pl.pallas_call(
        body, grid=(m // tm, kb // tb),
        in_specs=tspec + [pl.BlockSpec((tm, c), lambda i, j: (i, 0)), pl.BlockSpec((tb, c), lambda i, j: (j, 0)),
                          tile],
        out_specs=[tile, tile], out_shape=[jax.ShapeDtypeStruct((m, kb), BF)] * 2,
        name=name, compiler_params=_cp())(*tok, h, wu_t, gate)


def _mm_nt_dswiglu(dys, wd, gate, up, tm, tb, name, after=None):
    m, c = dys.shape
    kb = wd.shape[0]

    def body(d_ref, w_ref, g_ref, u_ref, dg_ref, du_ref):
        da = _dg(d_ref[...], w_ref[...], NT)
        g = g_ref[...].astype(F32)
        u = u_ref[...].astype(F32)
        s = jax.nn.sigmoid(g)
        dg_ref[...] = (da * u * (s * (1.0 + g * (1.0 - s)))).astype(BF)
        du_ref[...] = (da * (g * s)).astype(BF)

    body, tspec, tok = _tie(body, after)
    tile = pl.BlockSpec((tm, tb), lambda i, j: (i, j))
    return pl.pallas_call(
        body, grid=(m // tm, kb // tb),
        in_specs=tspec + [pl.BlockSpec((tm, c), lambda i, j: (i, 0)), pl.BlockSpec((tb, c), lambda i, j: (j, 0)),
                          tile, tile],
        out_specs=[tile, tile], out_shape=[jax.ShapeDtypeStruct((m, kb), BF)] * 2,
        name=name, compiler_params=_cp())(*tok, dys, wd, gate, up)


def _rms_fwd(x, gain, name):
    tm = 512
    row = pl.BlockSpec((tm, D), lambda i: (i, 0))

    def body(x_ref, g_ref, h_ref):
        xv = x_ref[...]
        r = lax.rsqrt(jnp.mean(xv * xv, axis=-1, keepdims=True) + NORM_EPS)
        h_ref[...] = (xv * r * g_ref[...]).astype(BF)

    return pl.pallas_call(
        body, grid=(T // tm,), in_specs=[row, pl.BlockSpec((1, D), lambda i: (0, 0))], out_specs=row,
        out_shape=jax.ShapeDtypeStruct((T, D), BF), name=name, compiler_params=_cp())(x, gain)


def _mm_rms_bwd(pairs, dims, x, gain, dres, alpha_out, tm, name, after=None):
    n = len(pairs)
    row = pl.BlockSpec((tm, D), lambda i: (i, 0))
    gspec = pl.BlockSpec((1, D), lambda i: (0, 0))

    def body(*refs):
        x_ref, g_ref, dres_ref, dx_ref, dxs_ref, dg_ref = refs[2 * n:]
        i = pl.program_id(0)
        dhv = _dg(refs[0][...], refs[n][...], dims)
        for p in range(1, n):
            dhv = dhv + _dg(refs[p][...], refs[n + p][...], dims)
        xv = x_ref[...]
        r = lax.rsqrt(jnp.mean(xv * xv, axis=-1, keepdims=True) + NORM_EPS)
        xh = xv * r
        part = jnp.sum(dhv * xh, axis=0, keepdims=True)

        @pl.when(i == 0)
        def _():
            dg_ref[...] = part

        @pl.when(i > 0)
        def _():
            dg_ref[...] += part

        dxh = dhv * g_ref[...]
        dx = r * (dxh - xh * jnp.mean(dxh * xh, axis=-1, keepdims=True)) + dres_ref[...]
        dx_ref[...] = dx
        dxs_ref[...] = (alpha_out * dx).astype(BF)

    body, tspec, tok = _tie(body, after)

    def bspec(a, b, kblk):
        if kblk is None:
            return pl.BlockSpec(b.shape, lambda i: (0, 0))
        return pl.BlockSpec((b.shape[0], a.shape[1]), lambda i: (0, kblk))

    return pl.pallas_call(
        body, grid=(T // tm,),
        in_specs=tspec + [pl.BlockSpec((tm, p[0].shape[1]), lambda i: (i, 0)) for p in pairs]
        + [bspec(p[0], p[1], p[2] if len(p) > 2 else None) for p in pairs] + [row, gspec, row],
        out_specs=[row, row, gspec],
        out_shape=[jax.ShapeDtypeStruct((T, D), F32), jax.ShapeDtypeStruct((T, D), BF),
                   jax.ShapeDtypeStruct((1, D), F32)],
        name=name, compiler_params=_cp())(*tok, *[p[0] for p in pairs], *[p[1] for p in pairs], x, gain, dres)


def _down_loss_bwd(x_prev, act, wd, gain, target, name):
    tm = 512
    row = pl.BlockSpec((tm, D), lambda i: (i, 0))
    gspec = pl.BlockSpec((1, D), lambda i: (0, 0))
    lspec = pl.BlockSpec((8, 128), lambda i: (0, 0))

    def body(x_ref, a_ref, w_ref, g_ref, t_ref, dx_ref, dxs_ref, dg_ref, loss_ref):
        i = pl.program_id(0)
        xv = x_ref[...] + 0.5 * _dg(a_ref[...], w_ref[...], NN)
        r = lax.rsqrt(jnp.mean(xv * xv, axis=-1, keepdims=True) + NORM_EPS)
        xh = xv * r
        diff = xh * g_ref[...] - t_ref[...]
        lpart = 0.5 * jnp.sum(jnp.mean(diff * diff, axis=-1, keepdims=True), axis=0, keepdims=True)
        dy = diff * (1.0 / D)
        part = jnp.sum(dy * xh, axis=0, keepdims=True)

        @pl.when(i == 0)
        def _():
            dg_ref[...] = part
            loss_ref[...] = jnp.broadcast_to(lpart, (8, 128))

        @pl.when(i > 0)
        def _():
            dg_ref[...] += part
            loss_ref[...] += jnp.broadcast_to(lpart, (8, 128))

        dxh = dy * g_ref[...]
        dx = r * (dxh - xh * jnp.mean(dxh * xh, axis=-1, keepdims=True))
        dx_ref[...] = dx
        dxs_ref[...] = (0.5 * dx).astype(BF)

    return pl.pallas_call(
        body, grid=(T // tm,),
        in_specs=[row, pl.BlockSpec((tm, F), lambda i: (i, 0)), pl.BlockSpec((F, D), lambda i: (0, 0)), gspec, row],
        out_specs=[row, row, gspec, lspec],
        out_shape=[jax.ShapeDtypeStruct((T, D), F32), jax.ShapeDtypeStruct((T, D), BF),
                   jax.ShapeDtypeStruct((1, D), F32), jax.ShapeDtypeStruct((8, 128), F32)],
        name=name, compiler_params=_cp())(x_prev, act, wd, gain, target)


def _mm_nn_resnorm(pairs, x_prev, alpha, gain, tm, name, after=None):
    m = x_prev.shape[0]
    n = len(pairs)

    def body(*refs):
        x_ref, g_ref, xo_ref, h_ref = refs[2 * n:]
        y = _dg(refs[0][...], refs[n][...], NN)
        for i in range(1, n):
            y = y + _dg(refs[i][...], refs[n + i][...], NN)
        xv = x_ref[...] + alpha * y
        xo_ref[...] = xv
        r = lax.rsqrt(jnp.mean(xv * xv, axis=-1, keepdims=True) + NORM_EPS)
        h_ref[...] = (xv * r * g_ref[...]).astype(BF)

    body, tspec, tok = _tie(body, after)
    row = pl.BlockSpec((tm, D), lambda i: (i, 0))
    return pl.pallas_call(
        body, grid=(m // tm,),
        in_specs=tspec + [pl.BlockSpec((tm, a.shape[1]), lambda i: (i, 0)) for a, _ in pairs]
        + [pl.BlockSpec(b.shape, lambda i: (0, 0)) for _, b in pairs] + [row, pl.BlockSpec((1, D), lambda i: (0, 0))],
        out_specs=[row, row],
        out_shape=[jax.ShapeDtypeStruct((m, D), F32), jax.ShapeDtypeStruct((m, D), BF)],
        name=name, compiler_params=_cp())(*tok, *[a for a, _ in pairs], *[b for _, b in pairs], x_prev, gain)


def _ffn_up(h, need, ahead, tag):
    wg = need("g", h)
    gate = _mm_nt(h, wg, BF, 1024, 1408, tag + "_gate")
    wu = need("u", gate)
    up, act = _mm_nt_swiglu(h, wu, gate, 1024, 1408, tag + "_up_act", after=ahead("d", wu))
    wd = need("d", up)
    return act, (h, gate, up, act, wg, wu, wd)


def _ffn_bwd(x_in, gain, saved, dxo, dys, alpha_out, emit, tag):
    h, gate, up, act, wg, wu, wd = saved
    sent = emit("d", _mm_tn(act, dys, BF, 256, 1024, tag + "_dwd"))
    dgate, dup = _mm_nt_dswiglu(dys, wd, gate, up, 1024, 1408, tag + "_dact", after=sent)
    sent = emit("g", _mm_tn(dgate, h, BF, 256, 1024, tag + "_dwg"))
    sent = emit("u", _mm_tn(dup, h, BF, 256, 1024, tag + "_dwu", after=sent))
    return _mm_rms_bwd([(dgate, wg), (dup, wu)], NN, x_in, gain, dxo, alpha_out, 512, tag + "_dh_dnorm",
                       after=sent)


SLAB = 2048
N_SLAB = T // SLAB
N_PAIR = A_HEADS // 2


def _pair_masks():
    lane = _iota((128, 128), 1)
    return lane < A_HD, lane >= A_HD


def _slope_table():
    h = 2 * jnp.arange(N_PAIR)[:, None] + jnp.minimum(jnp.arange(8), 1)[None, :]
    return jnp.broadcast_to((2.0 ** (-(h + 1).astype(F32)))[:, :, None], (N_PAIR, 8, 128))


def _rows(ref, start, d):
    if d == 1:
        return ref[pl.ds(start, 128), :]
    return ref[pl.ds(start, 128, stride=d), :]


def _put_rows(ref, start, d, val):
    if d == 1:
        ref[pl.ds(start, 128), :] = val
    else:
        ref[pl.ds(start, 128, stride=d), :] = val


def _units(d):
    return [(r, b, r + 128 * d * b) for r in range(d) for b in range(SLAB // (128 * d))]


def _biases(d, slopes, has_prev):
    qi = _iota((128, 256), 0)
    kj = _iota((128, 256), 1)
    steps = qi + 128 - kj
    in_band = (steps >= 0) & (steps <= 128)
    dist = (steps * d).astype(F32)
    base = [jnp.where(in_band, -(sl * dist), NEG) for sl in slopes]
    edge = [jnp.where(has_prev | (kj >= 128), b, NEG) for b in base]
    return base, edge


def _attn_fwd(z_at, name, after=None):
    def body(sl_ref, q_ref, kc_ref, kp_ref, vc_ref, vp_ref, of_ref, ob_ref, lse_ref, m_s, l_s, a_s):
        n = pl.program_id(1)
        lo, hi = _pair_masks()
        slopes = (sl_ref[0, 0:1, 0:1], sl_ref[0, 1:2, 0:1])

        def unit(d, start, b, first, carry, bias):
            q = (_rows(q_ref, start, d) * (A_HD ** -0.5)).astype(BF)
            kcur, vcur = _rows(kc_ref, start, d).astype(BF), _rows(vc_ref, start, d).astype(BF)
            if b > 0:
                kprev, vprev = carry
            else:
                pstart = start + SLAB - 128 * d
                kprev, vprev = _rows(kp_ref, pstart, d).astype(BF), _rows(vp_ref, pstart, d).astype(BF)
            kcat = jnp.concatenate([kprev, kcur], axis=0)
            vcat = jnp.concatenate([vprev, vcur], axis=0)
            ms, ls, pvs = [], [], []
            for e in range(2):
                qm = jnp.where(lo if e == 0 else hi, q, jnp.zeros_like(q))
                s = _dg(qm, kcat, NT) + bias[e]
                m = jnp.max(s, axis=1, keepdims=True)
                p = jnp.exp(s - m)
                ms.append(m)
                ls.append(jnp.sum(p, axis=1, keepdims=True))
                pvs.append(_dg(p.astype(BF), vcat, NN))
            m_u = jnp.where(lo, ms[0], ms[1])
            l_u = jnp.where(lo, ls[0], ls[1])
            a_u = jnp.where(lo, pvs[0], pvs[1])
            if first:
                m_n, l_n, a_n = m_u, l_u, a_u
            else:
                m_o = _rows(m_s, start, d)
                m_n = jnp.maximum(m_o, m_u)
                c_o = jnp.exp(m_o - m_n)
                c_u = jnp.exp(m_u - m_n)
                l_n = _rows(l_s, start, d) * c_o + l_u * c_u
                a_n = _rows(a_s, start, d) * c_o + a_u * c_u
            _put_rows(m_s, start, d, m_n)
            _put_rows(l_s, start, d, l_n)
            _put_rows(a_s, start, d, a_n)
            return kcur, vcur

        for pi, (_, d) in enumerate(PATTERNS):
            base, edge = _biases(d, slopes, n > 0)
            carry = None
            for r, b, start in _units(d):
                carry = unit(d, start, b, pi == 0, carry, edge if b == 0 else base)
        l = l_s[...]
        out = a_s[...] / l
        of_ref[...] = out
        ob_ref[...] = out.astype(BF)
        lse_ref[...] = m_s[...] + jnp.log(l)

    body, tspec, tok = _tie(body, after)
    cur = lambda c: pl.BlockSpec((SLAB, 128), lambda j, n: (n, c * N_PAIR + j))
    prv = lambda c: pl.BlockSpec((SLAB, 128), lambda j, n: (jnp.maximum(n - 1, 0), c * N_PAIR + j))
    out = pl.BlockSpec((SLAB, 128), lambda j, n: (n, j))
    return pl.pallas_call(
        body, grid=(N_PAIR, N_SLAB),
        in_specs=tspec + [pl.BlockSpec((1, 8, 128), lambda j, n: (j, 0, 0)), cur(0), cur(1), prv(1), cur(2), prv(2)],
        out_specs=[out, out, out],
        out_shape=[jax.ShapeDtypeStruct((T, AW), F32), jax.ShapeDtypeStruct((T, AW), BF),
                   jax.ShapeDtypeStruct((T, AW), F32)],
        scratch_shapes=[pltpu.VMEM((SLAB, 128), F32)] * 3,
        name=name, compiler_params=_cp())(*tok, _slope_table(), z_at, z_at, z_at, z_at, z_at)


def _attn_bwd(z_at, dout, out, lse, name):
    def body(sl_ref, q_ref, kc_ref, kp_ref, vc_ref, vp_ref, do_ref, o_ref, lse_ref, dq_ref, dk_ref, dv_ref,
             dq_s, dk_s, dv_s, ck_s, cv_s):
        step = pl.program_id(1)
        n = N_SLAB - 1 - step
        lo, hi = _pair_masks()
        slopes = (sl_ref[0, 0:1, 0:1], sl_ref[0, 1:2, 0:1])

        @pl.when(step == 0)
        def _():
            ck_s[...] = jnp.zeros_like(ck_s)
            cv_s[...] = jnp.zeros_like(cv_s)

        dk_s[...] = ck_s[...]
        dv_s[...] = cv_s[...]
        ck_s[...] = jnp.zeros_like(ck_s)
        cv_s[...] = jnp.zeros_like(cv_s)

        def add_rows(ref, start, d, val):
            _put_rows(ref, start, d, _rows(ref, start, d) + val)

        def unit(d, start, b, first, carry, bias):
            q = (_rows(q_ref, start, d) * (A_HD ** -0.5)).astype(BF)
            do_f = _rows(do_ref, start, d)
            do = do_f.astype(BF)
            prod = do_f * _rows(o_ref, start, d)
            lse_u = _rows(lse_ref, start, d)
            kcur, vcur = _rows(kc_ref, start, d).astype(BF), _rows(vc_ref, start, d).astype(BF)
            if b > 0:
                kprev, vprev = carry
            else:
                pstart = start + SLAB - 128 * d
                kprev, vprev = _rows(kp_ref, pstart, d).astype(BF), _rows(vp_ref, pstart, d).astype(BF)
            kcat = jnp.concatenate([kprev, kcur], axis=0)
            vcat = jnp.concatenate([vprev, vcur], axis=0)
            masks = (lo, hi)
            qms = [jnp.where(msk, q, jnp.zeros_like(q)) for msk in masks]
            doms = [jnp.where(msk, do, jnp.zeros_like(do)) for msk in masks]
            deltas = [jnp.sum(jnp.where(msk, prod, 0.0), axis=1, keepdims=True) for msk in masks]
            ss = [_dg(qm, kcat, NT) + bs for qm, bs in zip(qms, bias)]
            dps = [_dg(dom, vcat, NT) for dom in doms]
            ps = [jnp.exp(s - lse_u[:, 64 * e:64 * e + 1]) for e, s in enumerate(ss)]
            dss = [(p * (dp - delta)).astype(BF) for p, dp, delta in zip(ps, dps, deltas)]
            pbs = [p.astype(BF) for p in ps]
            dqs = [_dg(ds, kcat, NN) for ds in dss]
            dkc = _dg(dss[0], qms[0], TN) + _dg(dss[1], qms[1], TN)
            dvc = _dg(pbs[0], doms[0], TN) + _dg(pbs[1], doms[1], TN)
            dq_u = jnp.where(lo, dqs[0], dqs[1]) * (A_HD ** -0.5)
            if first:
                _put_rows(dq_s, start, d, dq_u)
            else:
                add_rows(dq_s, start, d, dq_u)
            add_rows(dk_s, start, d, dkc[128:])
            add_rows(dv_s, start, d, dvc[128:])
            if b > 0:
                add_rows(dk_s, start - 128 * d, d, dkc[:128])
                add_rows(dv_s, start - 128 * d, d, dvc[:128])
            else:
                pstart = start + SLAB - 128 * d
                add_rows(ck_s, pstart, d, dkc[:128])
                add_rows(cv_s, pstart, d, dvc[:128])
            return kcur, vcur

        for pi, (_, d) in enumerate(PATTERNS):
            base, edge = _biases(d, slopes, n > 0)
            carry = None
            for r, b, start in _units(d):
                carry = unit(d, start, b, pi == 0, carry, edge if b == 0 else base)
        dq_ref[...] = dq_s[...].astype(BF)
        dk_ref[...] = dk_s[...].astype(BF)
        dv_ref[...] = dv_s[...].astype(BF)

    rev = lambda n: N_SLAB - 1 - n
    cur = lambda c: pl.BlockSpec((SLAB, 128), lambda j, n: (rev(n), c * N_PAIR + j))
    prv = lambda c: pl.BlockSpec((SLAB, 128), lambda j, n: (jnp.maximum(rev(n) - 1, 0), c * N_PAIR + j))
    one = pl.BlockSpec((SLAB, 128), lambda j, n: (rev(n), j))
    return pl.pallas_call(
        body, grid=(N_PAIR, N_SLAB),
        in_specs=[pl.BlockSpec((1, 8, 128), lambda j, n: (j, 0, 0)), cur(0), cur(1), prv(1), cur(2), prv(2),
                  one, one, one],
        out_specs=[one, one, one], out_shape=[jax.ShapeDtypeStruct((T, AW), BF)] * 3,
        scratch_shapes=[pltpu.VMEM((SLAB, 128), F32)] * 5,
        name=name, compiler_params=_cp())(_slope_table(), z_at, z_at, z_at, z_at, z_at, dout, out, lse)


def _silu(x):
    return x * jax.nn.sigmoid(x)


def _qk_math(c):
    s = _silu(c)
    return s * lax.rsqrt(jnp.sum(s * s, axis=-1, keepdims=True) + L2_EPS)


def _softplus(x):
    return jnp.maximum(x, 0.0) + jnp.log(1.0 + jnp.exp(-jnp.abs(x)))


def _gate_math(bd, alog_row, dtb_row):
    rows = bd.shape[0]
    lane = _iota(bd.shape, 1)
    beta = jax.nn.sigmoid(bd)
    g = jnp.where((lane >= DN_H) & (lane < 2 * DN_H), -jnp.exp(alog_row) * _softplus(bd + dtb_row), 0.0)
    ri = _iota((rows, rows), 0)
    ci = _iota((rows, rows), 1)
    same = (ri // CH) == (ci // CH)
    li = _iota((128, 128), 0)
    lj = _iota((128, 128), 1)
    to_next_group = jnp.where((lj == li + DN_H) & (li >= DN_H) & (li < 2 * DN_H), 1.0, 0.0)
    gc = _hdot(jnp.where(same & (ci <= ri), 1.0, 0.0), g)
    glast = _hdot(_hdot(jnp.where(same, 1.0, 0.0), g), to_next_group)
    return jnp.where(lane < DN_H, beta, 0.0) + gc + glast


def _shift_down(cur, halo, s):
    if s == 0:
        return cur
    rolled = pltpu.roll(cur, s, 0)
    hr = pltpu.roll(halo, s, 0)
    head = jnp.where(_iota(hr.shape, 0) < s, hr, rolled[:8])
    return jnp.concatenate([head, rolled[8:]], axis=0)


def _shift_up(cur, halo, s):
    if s == 0:
        return cur
    rows = cur.shape[0]
    rolled = pltpu.roll(cur, rows - s, 0)
    hr = pltpu.roll(halo, 8 - s, 0)
    tail = jnp.where(_iota(hr.shape, 0) >= 8 - s, hr, rolled[rows - 8:])
    return jnp.concatenate([rolled[:rows - 8], tail], axis=0)


def _dn_pre_fwd(z_dn, conv_w8, alog_row, dtb_row, name):
    tm = 512
    wq = 3 * DNW

    def body(raw_ref, halo_ref, bd_ref, w_ref, al_ref, dt_ref, conv_ref, qkv_ref, bg_ref):
        i = pl.program_id(0)
        cur = raw_ref[...]
        halo = jnp.where(i > 0, halo_ref[...], 0.0)
        w = w_ref[...]
        conv = jnp.zeros((tm, wq), F32)
        for j in range(4):
            conv = conv + _shift_down(cur, halo, 3 - j) * w[j:j + 1, :]
        conv_ref[...] = conv
        for blk in range(3 * DN_H):
            sl = slice(128 * blk, 128 * blk + 128)
            c = conv[:, sl]
            qkv_ref[:, sl] = _qk_math(c) if blk < 2 * DN_H else _silu(c)
        for r0 in range(0, tm, PAIR):
            rs = slice(r0, r0 + PAIR)
            bg_ref[rs, :] = _gate_math(bd_ref[rs, :], al_ref[...], dt_ref[...])

    one = pl.BlockSpec((1, 128), lambda i: (0, 0))
    return pl.pallas_call(
        body, grid=(T // tm,),
        in_specs=[pl.BlockSpec((tm, wq), lambda i: (i, 0)),
                  pl.BlockSpec((8, wq), lambda i: (jnp.maximum(i * (tm // 8) - 1, 0), 0)),
                  pl.BlockSpec((tm, 128), lambda i: (i, BD_BLK)),
                  pl.BlockSpec((8, wq), lambda i: (0, 0)), one, one],
        out_specs=[pl.BlockSpec((tm, wq), lambda i: (i, 0)), pl.BlockSpec((tm, wq), lambda i: (i, 0)),
                   pl.BlockSpec((tm, 128), lambda i: (i, 0))],
        out_shape=[jax.ShapeDtypeStruct((T, wq), F32), jax.ShapeDtypeStruct((T, wq), F32),
                   jax.ShapeDtypeStruct((T, 128), F32)],
        name=name, compiler_params=_cp())(z_dn, z_dn, z_dn, conv_w8, alog_row, dtb_row)


def _dn_pre_bwd(conv, z_dn, alog_row, dtb_row, dqn, dkn, dvn, dbg, name):
    tm = 512
    wq = 3 * DNW

    def body(conv_ref, bd_ref, al_ref, dt_ref, dq_ref, dk_ref, dv_ref, dbg_ref,
             dconv_ref, dbd_ref, dal_ref, ddt_ref):
        i = pl.program_id(0)
        for blk in range(3 * DN_H):
            sl = slice(128 * blk, 128 * blk + 128)
            src = (dq_ref, dk_ref, dv_ref)[blk // DN_H]
            ct = src[:, 128 * (blk % DN_H):128 * (blk % DN_H) + 128]
            fn = _qk_math if blk < 2 * DN_H else _silu
            _, vjp = jax.vjp(fn, conv_ref[:, sl])
            dconv_ref[:, sl] = vjp(ct)[0]
        dal = jnp.zeros((1, 128), F32)
        ddt = jnp.zeros((1, 128), F32)
        for r0 in range(0, tm, PAIR):
            rs = slice(r0, r0 + PAIR)
            _, vjp = jax.vjp(_gate_math, bd_ref[rs, :], al_ref[...], dt_ref[...])
            dbd, dal_p, ddt_p = vjp(dbg_ref[rs, :])
            dbd_ref[rs, :] = dbd.astype(BF)
            dal, ddt = dal + dal_p, ddt + ddt_p

        @pl.when(i == 0)
        def _():
            dal_ref[...] = dal
            ddt_ref[...] = ddt

        @pl.when(i > 0)
        def _():
            dal_ref[...] += dal
            ddt_ref[...] += ddt

    one = pl.BlockSpec((1, 128), lambda i: (0, 0))
    row = pl.BlockSpec((tm, wq), lambda i: (i, 0))
    hd = pl.BlockSpec((tm, DNW), lambda i: (i, 0))
    st = pl.BlockSpec((tm, 128), lambda i: (i, 0))
    return pl.pallas_call(
        body, grid=(T // tm,),
        in_specs=[row, pl.BlockSpec((tm, 128), lambda i: (i, BD_BLK)), one, one, hd, hd, hd, st],
        out_specs=[row, st, one, one],
        out_shape=[jax.ShapeDtypeStruct((T, wq), F32), jax.ShapeDtypeStruct((T, 128), BF),
                   jax.ShapeDtypeStruct((1, 128), F32), jax.ShapeDtypeStruct((1, 128), F32)],
        name=name, compiler_params=_cp())(conv, z_dn, alog_row, dtb_row, dqn, dkn, dvn, dbg)


def _dn_conv_bwd(dconv, z_dn, conv_w8, name):
    tm = 512
    wq = 3 * DNW
    last = T // tm - 1

    def body(dc_ref, dcn_ref, raw_ref, halo_ref, w_ref, draw_ref, dw_ref):
        i = pl.program_id(0)
        dc = dc_ref[...]
        nxt = jnp.where(i < last, dcn_ref[...], 0.0)
        cur = raw_ref[...]
        halo = jnp.where(i > 0, halo_ref[...], 0.0)
        w = w_ref[...]
        draw = jnp.zeros((tm, wq), F32)
        rows = []
        for j in range(4):
            draw = draw + _shift_up(dc, nxt, 3 - j) * w[j:j + 1, :]
            rows.append(jnp.sum(dc * _shift_down(cur, halo, 3 - j), axis=0, keepdims=True))
        draw_ref[...] = draw.astype(BF)
        part = jnp.concatenate(rows + [jnp.zeros((4, wq), F32)], axis=0)

        @pl.when(i == 0)
        def _():
            dw_ref[...] = part

        @pl.when(i > 0)
        def _():
            dw_ref[...] += part

    row = pl.BlockSpec((tm, wq), lambda i: (i, 0))
    return pl.pallas_call(
        body, grid=(T // tm,),
        in_specs=[row, pl.BlockSpec((8, wq), lambda i: (jnp.minimum((i + 1) * (tm // 8), T // 8 - 1), 0)),
                  row, pl.BlockSpec((8, wq), lambda i: (jnp.maximum(i * (tm // 8) - 1, 0), 0)),
                  pl.BlockSpec((8, wq), lambda i: (0, 0))],
        out_specs=[row, pl.BlockSpec((8, wq), lambda i: (0, 0))],
        out_shape=[jax.ShapeDtypeStruct((T, wq), BF), jax.ShapeDtypeStruct((8, wq), F32)],
        name=name, compiler_params=_cp())(dconv, dconv, z_dn, z_dn, conv_w8)


def _h3(a, b, dims=NN):
    return lax.dot_general(a, b, dims, precision=lax.Precision.HIGH, preferred_element_type=F32)


@jax.custom_vjp
def _inverse_given(a_mat, tinv):
    return tinv


def _inverse_given_fwd(a_mat, tinv):
    return tinv, tinv


def _inverse_given_bwd(tinv, g):
    return -_bdot_tn(tinv, _bdot_nt(g, tinv)), jnp.zeros_like(tinv)


_inverse_given.defvjp(_inverse_given_fwd, _inverse_given_bwd)


@jax.custom_vjp
def _h3_lo(a, b):
    return _h3(a, b)


def _h3_lo_fwd(a, b):
    return _h3(a, b), (a, b)


def _h3_lo_bwd(res, g):
    a, b = res
    gb = g.astype(BF)
    return _dg(gb, b.astype(BF), NT), _dg(a.astype(BF), gb, TN)


_h3_lo.defvjp(_h3_lo_fwd, _h3_lo_bwd)


def _prep_head(q, k, v, bgc, h):
    beta = _col(bgc, h)
    gc = jnp.broadcast_to(_col(bgc, DN_H + h), (PAIR, 128))
    glast = jnp.broadcast_to(_col(bgc, 2 * DN_H + h), (PAIR, 128))
    ri = _iota((PAIR, PAIR), 0)
    ci = _iota((PAIR, PAIR), 1)
    same = (ri // CH) == (ci // CH)
    causal = same & (ci <= ri)
    strict = same & (ci < ri)
    eye = ri == ci
    gc_cols = _hdot(jnp.ones((PAIR, PAIR), F32), jnp.where(eye, gc, 0.0))
    decay = jnp.exp(jnp.where(causal, gc - gc_cols, NEG))
    egc = jnp.exp(gc)
    kb = k * beta
    a_mat = jnp.where(strict, _bdot_nt(kb, k) * decay, 0.0)
    qs = q * (DN_HD ** -0.5)
    attn = jnp.where(causal, _bdot_nt(qs, k) * decay, 0.0)
    return a_mat, (v * beta, kb * egc, qs * egc, k * jnp.exp(glast - gc), attn, jnp.exp(glast))


PREP_DTYPES = (F32, BF, BF, BF, BF, F32)


def _prep_tail(tinv, ctx):
    vb, kbe, qg, kdec, attn, decb = ctx
    outs = (_h3_lo(tinv, vb), _h3_lo(tinv, kbe), qg, kdec, attn, decb)
    return tuple(o.astype(dt) for o, dt in zip(outs, PREP_DTYPES))


def _inverses(a_mats):
    eye = jnp.where(_iota((PAIR, PAIR), 0) == _iota((PAIR, PAIR), 1), 1.0, 0.0)
    ps = [-a for a in a_mats]
    tinvs = [eye + p for p in ps]
    for _ in range(5):
        ps = [_h3(p, p) for p in ps]
        tinvs = [t + _h3(t, p) for t, p in zip(tinvs, ps)]
    return tinvs


def _dn_prep_fwd(qkv, bg, name):
    rows = 1024
    hd = lambda off: pl.BlockSpec((rows, 128), lambda g, h: (g, off + h))
    out = pl.BlockSpec((rows, 128), lambda g, h: (g, h))

    def body(q_ref, k_ref, v_ref, bg_ref, *outs):
        h = pl.program_id(1)
        spans = [slice(PAIR * pr, PAIR * pr + PAIR) for pr in range(rows // PAIR)]
        heads = [_prep_head(q_ref[rs, :], k_ref[rs, :], v_ref[rs, :], bg_ref[rs, :], h) for rs in spans]
        tinvs = _inverses([a for a, _ in heads])
        for rs, tinv, (_, ctx) in zip(spans, tinvs, heads):
            for o_ref, val in zip(outs, _prep_tail(tinv, ctx) + (tinv,)):
                o_ref[rs, :] = val

    return pl.pallas_call(
        body, grid=(T // rows, DN_H),
        in_specs=[hd(0), hd(DN_H), hd(2 * DN_H), pl.BlockSpec((rows, 128), lambda g, h: (g, 0))],
        out_specs=[out] * 7, out_shape=[jax.ShapeDtypeStruct((T, DNW), dt) for dt in PREP_DTYPES + (F32,)],
        name=name, compiler_params=_cp())(qkv, qkv, qkv, bg)


def _dn_prep_bwd(qkv, bg, tinv, cts, name):
    rows = 1024
    hd = lambda off: pl.BlockSpec((rows, 128), lambda g, h: (g, off + h))
    out = pl.BlockSpec((rows, 128), lambda g, h: (g, h))
    st = pl.BlockSpec((rows, 128), lambda g, h: (g, 0))

    def body(q_ref, k_ref, v_ref, bg_ref, ti_ref, c0, c1, c2, c3, c4, c5, dq_ref, dk_ref, dv_ref, dbg_ref):
        h = pl.program_id(1)
        spans = [slice(PAIR * pr, PAIR * pr + PAIR) for pr in range(rows // PAIR)]
        tis = [ti_ref[rs, :] for rs in spans]

        def joint(qs, ks, vs, bs):
            heads = [_prep_head(q, k, v, b, h) for q, k, v, b in zip(qs, ks, vs, bs)]
            return [_prep_tail(_inverse_given(a, ti), ctx) for (a, ctx), ti in zip(heads, tis)]

        _, vjp = jax.vjp(joint, *[[r[rs, :] for rs in spans] for r in (q_ref, k_ref, v_ref, bg_ref)])
        dqs, dks, dvs, dbs = vjp([tuple(c[rs, :] for c in (c0, c1, c2, c3, c4, c5)) for rs in spans])
        for rs, dq, dk, dv in zip(spans, dqs, dks, dvs):
            dq_ref[rs, :] = dq
            dk_ref[rs, :] = dk
            dv_ref[rs, :] = dv
        dbg_all = jnp.concatenate(dbs, axis=0)

        @pl.when(h == 0)
        def _():
            dbg_ref[...] = dbg_all

        @pl.when(h > 0)
        def _():
            dbg_ref[...] += dbg_all

    return pl.pallas_call(
        body, grid=(T // rows, DN_H),
        in_specs=[hd(0), hd(DN_H), hd(2 * DN_H), st] + [out] * 7,
        out_specs=[out, out, out, st],
        out_shape=[jax.ShapeDtypeStruct((T, DNW), F32)] * 3 + [jax.ShapeDtypeStruct((T, 128), F32)],
        name=name, compiler_params=_cp())(qkv, qkv, qkv, bg, tinv, *cts)


def _step_math(ss, us, ws, qgs, kdecs, attns, decbs, sub):
    z = jnp.zeros((CH, 128), F32)
    vnews = [u - _bdot_nn(w, s) for u, w, s in zip(us, ws, ss)]
    vfulls = [jnp.concatenate([v, z] if sub == 0 else [z, v], axis=0) for v in vnews]
    os = [_bdot_nn(qg, s) + _bdot_nn(attn, vf) for qg, s, attn, vf in zip(qgs, ss, attns, vfulls)]
    decs = [jnp.sum(decb, axis=0, keepdims=True) * (1.0 / CH) for decb in decbs]
    return [s * dec + _bdot_tn(kdec, v) for s, dec, kdec, v in zip(ss, decs, kdecs, vnews)], os


SCAN_ROWS = 256


def _dn_scan_fwd(prep, name):
    nstep = T // SCAN_ROWS
    nch = SCAN_ROWS // CH
    row = pl.BlockSpec((SCAN_ROWS, DNW), lambda p: (p, 0))

    def body(u_ref, w_ref, qg_ref, kd_ref, at_ref, db_ref, o_ref, ss_ref, s_ref):
        @pl.when(pl.program_id(0) == 0)
        def _():
            s_ref[...] = jnp.zeros_like(s_ref)

        lanes = [slice(128 * h, 128 * h + 128) for h in range(DN_H)]
        states = [s_ref[h] for h in range(DN_H)]
        for ch in range(nch):
            rs = slice(CH * ch, CH * ch + CH)
            for h in range(DN_H):
                ss_ref[ch, h] = states[h]
            states, os = _step_math(states, *[[r[rs, ls] for ls in lanes]
                                              for r in (u_ref, w_ref, qg_ref, kd_ref, at_ref, db_ref)], ch % 2)
            for ls, o in zip(lanes, os):
                o_ref[rs, ls] = o
        for h in range(DN_H):
            s_ref[h] = states[h]

    return pl.pallas_call(
        body, grid=(nstep,), in_specs=[row] * 6,
        out_specs=[row, pl.BlockSpec((nch, DN_H, 128, 128), lambda p: (p, 0, 0, 0))],
        out_shape=[jax.ShapeDtypeStruct((T, DNW), F32), jax.ShapeDtypeStruct((T // CH, DN_H, 128, 128), F32)],
        scratch_shapes=[pltpu.VMEM((DN_H, 128, 128), F32)],
        name=name, compiler_params=_cp())(*prep)


def _dn_scan_bwd(prep, states, do, name):
    nstep = T // SCAN_ROWS
    nch = SCAN_ROWS // CH
    row = pl.BlockSpec((SCAN_ROWS, DNW), lambda p: (nstep - 1 - p, 0))

    def body(u_ref, w_ref, qg_ref, kd_ref, at_ref, db_ref, ss_ref, do_ref, *rest):
        outs, ds_ref = rest[:6], rest[6]

        @pl.when(pl.program_id(0) == 0)
        def _():
            ds_ref[...] = jnp.zeros_like(ds_ref)

        lanes = [slice(128 * h, 128 * h + 128) for h in range(DN_H)]
        dss = [ds_ref[h] for h in range(DN_H)]
        for ch in reversed(range(nch)):
            rs = slice(CH * ch, CH * ch + CH)
            args = [[ss_ref[ch, h] for h in range(DN_H)]] + [
                [r[rs, ls] for ls in lanes] for r in (u_ref, w_ref, qg_ref, kd_ref, at_ref, db_ref)]
            _, vjp = jax.vjp(functools.partial(_step_math, sub=ch % 2), *args)
            cts = vjp((dss, [do_ref[rs, ls] for ls in lanes]))
            dss = cts[0]
            for o_ref, vals in zip(outs, cts[1:]):
                for ls, val in zip(lanes, vals):
                    o_ref[rs, ls] = val
        for h in range(DN_H):
            ds_ref[h] = dss[h]

    return pl.pallas_call(
        body, grid=(nstep,),
        in_specs=[row] * 6 + [pl.BlockSpec((nch, DN_H, 128, 128), lambda p: (nstep - 1 - p, 0, 0, 0)), row],
        out_specs=[row] * 6, out_shape=[jax.ShapeDtypeStruct((T, DNW), dt) for dt in PREP_DTYPES],
        scratch_shapes=[pltpu.VMEM((DN_H, 128, 128), F32)],
        name=name, compiler_params=_cp())(*prep, states, do)


def _post_math(o, gate, wrow):
    return o * lax.rsqrt(jnp.mean(o * o, axis=-1, keepdims=True) + NORM_EPS) * wrow * _silu(gate)


def _dn_post_fwd(o, z_dn, dn_norm, name):
    tm = 512
    row = pl.BlockSpec((tm, DNW), lambda i: (i, 0))

    def body(o_ref, g_ref, w_ref, y_ref):
        for h in range(DN_H):
            ls = slice(128 * h, 128 * h + 128)
            y_ref[:, ls] = _post_math(o_ref[:, ls], g_ref[:, ls], w_ref[...]).astype(BF)

    return pl.pallas_call(
        body, grid=(T // tm,),
        in_specs=[row, pl.BlockSpec((tm, DNW), lambda i: (i, 3)), pl.BlockSpec((1, 128), lambda i: (0, 0))],
        out_specs=row, out_shape=jax.ShapeDtypeStruct((T, DNW), BF),
        name=name, compiler_params=_cp())(o, z_dn, dn_norm)


def _dn_post_bwd(o, z_dn, dn_norm, dy, name):
    tm = 512
    row = pl.BlockSpec((tm, DNW), lambda i: (i, 0))
    one = pl.BlockSpec((1, 128), lambda i: (0, 0))

    def body(o_ref, g_ref, w_ref, dy_ref, do_ref, dg_ref, dw_ref):
        i = pl.program_id(0)
        dw = jnp.zeros((1, 128), F32)
        for h in range(DN_H):
            ls = slice(128 * h, 128 * h + 128)
            _, vjp = jax.vjp(_post_math, o_ref[:, ls], g_ref[:, ls], w_ref[...])
            do, dg, dwh = vjp(dy_ref[:, ls].astype(F32))
            do_ref[:, ls] = do
            dg_ref[:, ls] = dg.astype(BF)
            dw = dw + dwh

        @pl.when(i == 0)
        def _():
            dw_ref[...] = dw

        @pl.when(i > 0)
        def _():
            dw_ref[...] += dw

    return pl.pallas_call(
        body, grid=(T // tm,),
        in_specs=[row, pl.BlockSpec((tm, DNW), lambda i: (i, 3)), one, pl.BlockSpec((tm, DNW), lambda i: (i, 1))],
        out_specs=[row, row, one],
        out_shape=[jax.ShapeDtypeStruct((T, DNW), F32), jax.ShapeDtypeStruct((T, DNW), BF),
                   jax.ShapeDtypeStruct((1, 128), F32)],
        name=name, compiler_params=_cp())(o, z_dn, dn_norm, dy)


HBM = pl.BlockSpec(memory_space=pltpu.HBM)
SEM = pl.BlockSpec(memory_space=pltpu.SEMAPHORE)
EFFECT = pltpu.SideEffectType.DATAFLOW_SIDE_EFFECTING
N_PEER = N_DEV - 1


ALL_PEERS = (1, 2, 4, 3, 5, 6, 7)
FIRST_HOP = (1, 2, 4, 6)
FORWARDED = (2, 4, 6)


def _peers(x, y, c, ks=ALL_PEERS):
    return [(k, (x ^ (k >> 2), y ^ ((k >> 1) & 1), c ^ (k & 1))) for k in ks]


def _exchange_copy(ins, lands, ssems, rsems, scatter, t, k, pos, me):
    px, py, pc = pos
    src = ins[t].at[4 * px + 2 * py + pc] if scatter else ins[t]
    return pltpu.make_async_remote_copy(
        src_ref=src, dst_ref=lands[t].at[me], send_sem=ssems[t].at[k - 1], recv_sem=rsems[t].at[k - 1],
        device_id=pos, device_id_type=MESH_ID)


def _xstart(bufs, scatter, name, ks=ALL_PEERS):
    nt = len(bufs)
    lands = [lax.empty((N_DEV,) + tuple(b.shape[1:] if scatter else b.shape), b.dtype) for b in bufs]

    def body(*refs):
        ins, lnd = refs[:nt], refs[nt:2 * nt]
        ssems, rsems = refs[2 * nt:3 * nt], refs[3 * nt:4 * nt]
        token = refs[-1]
        x, y, c = lax.axis_index("x"), lax.axis_index("y"), lax.axis_index("c")
        me = 4 * x + 2 * y + c
        for t in range(nt):
            for k, pos in _peers(x, y, c, ks):
                _exchange_copy(ins, lnd, ssems, rsems, scatter, t, k, pos, me).start()
        token[...] = jnp.zeros_like(token)

    both = list(bufs) + lands
    res = pl.pallas_call(
        body, name=name,
        out_shape=[pltpu.SemaphoreType.DMA((N_PEER,))] * (2 * nt)
        + [pltpu.HBM(b.shape, b.dtype) for b in both] + [jax.ShapeDtypeStruct((8, 128), F32)],
        in_specs=[HBM] * (2 * nt),
        out_specs=[SEM] * (2 * nt) + [HBM] * (2 * nt) + [pl.BlockSpec(memory_space=pltpu.VMEM)],
        input_output_aliases={i: 2 * nt + i for i in range(2 * nt)},
        compiler_params=pltpu.CompilerParams(has_side_effects=EFFECT),
    )(*[pltpu.with_memory_space_constraint(b, pltpu.HBM) for b in both])
    return res[:nt], res[nt:2 * nt], res[2 * nt:3 * nt], res[3 * nt:4 * nt], res[-1][0, 0]


def _xwait(ssems, rsems, thrus, lands, scatter, after, name, ks=ALL_PEERS):
    nt = len(lands)

    def body(*refs):
        ins, lnd = refs[:nt], refs[nt:2 * nt]
        ss, rs = refs[2 * nt:3 * nt], refs[3 * nt:4 * nt]
        x, y, c = lax.axis_index("x"), lax.axis_index("y"), lax.axis_index("c")
        me = 4 * x + 2 * y + c
        for t in range(nt):
            for k, pos in _peers(x, y, c, ks):
                cp = _exchange_copy(ins, lnd, ss, rs, scatter, t, k, pos, me)
                cp.wait_send()
                cp.wait_recv()

    both = list(thrus) + list(lands)
    res = pl.pallas_call(
        body, name=name, out_shape=[pltpu.HBM(b.shape, b.dtype) for b in both],
        in_specs=[HBM] * (2 * nt) + [SEM] * (2 * nt) + [ANY], out_specs=[HBM] * (2 * nt),
        input_output_aliases={i: i for i in range(2 * nt)},
        compiler_params=pltpu.CompilerParams(has_side_effects=EFFECT),
    )(*both, *ssems, *rsems, after)
    return res[:nt], res[nt:]


def _forward_copy(lands, ssems, rsems, t, k, pos, sibling):
    px, py, pc = pos
    slot = lands[t].at[4 * px + 2 * py + pc]
    return pltpu.make_async_remote_copy(
        src_ref=slot, dst_ref=slot, send_sem=ssems[t].at[k - 1], recv_sem=rsems[t].at[k - 1],
        device_id=sibling, device_id_type=MESH_ID)


def _fstart(lands, name):
    nt = len(lands)

    def body(*refs):
        lnd = refs[:nt]
        ssems, rsems = refs[nt:2 * nt], refs[2 * nt:3 * nt]
        token = refs[-1]
        x, y, c = lax.axis_index("x"), lax.axis_index("y"), lax.axis_index("c")
        for t in range(nt):
            for k, pos in _peers(x, y, c, FORWARDED):
                _forward_copy(lnd, ssems, rsems, t, k, pos, (x, y, c ^ 1)).start()
        token[...] = jnp.zeros_like(token)

    res = pl.pallas_call(
        body, name=name,
        out_shape=[pltpu.SemaphoreType.DMA((N_PEER,))] * (2 * nt)
        + [pltpu.HBM(b.shape, b.dtype) for b in lands] + [jax.ShapeDtypeStruct((8, 128), F32)],
        in_specs=[HBM] * nt,
        out_specs=[SEM] * (2 * nt) + [HBM] * nt + [pl.BlockSpec(memory_space=pltpu.VMEM)],
        input_output_aliases={i: 2 * nt + i for i in range(nt)},
        compiler_params=pltpu.CompilerParams(has_side_effects=EFFECT),
    )(*[pltpu.with_memory_space_constraint(b, pltpu.HBM) for b in lands])
    return res[:nt], res[nt:2 * nt], res[2 * nt:3 * nt], res[-1][0, 0]


def _fwait(ssems, rsems, lands, after, name):
    nt = len(lands)

    def body(*refs):
        lnd = refs[:nt]
        ss, rs = refs[nt:2 * nt], refs[2 * nt:3 * nt]
        x, y, c = lax.axis_index("x"), lax.axis_index("y"), lax.axis_index("c")
        for t in range(nt):
            for k, pos in _peers(x, y, c, FORWARDED):
                cp = _forward_copy(lnd, ss, rs, t, k, pos, (x, y, c ^ 1))
                cp.wait_send()
                cp.wait_recv()

    return pl.pallas_call(
        body, name=name, out_shape=[pltpu.HBM(b.shape, b.dtype) for b in lands],
        in_specs=[HBM] * nt + [SEM] * (2 * nt) + [ANY], out_specs=[HBM] * nt,
        input_output_aliases={i: i for i in range(nt)},
        compiler_params=pltpu.CompilerParams(has_side_effects=EFFECT),
    )(*lands, *ssems, *rsems, after)


def _adam(recv, sent, me, w, m, v, tr, name):
    _, r, c = w.shape
    c1 = np.float32(1.0 - ADAM_B1 ** ADAM_STEP)
    c2 = np.float32(1.0 - ADAM_B2 ** ADAM_STEP)

    def body(me_ref, r_ref, s_ref, w_ref, m_ref, v_ref, g_ref, d_ref, mo_ref, vo_ref):
        mine = me_ref[0]
        g = jnp.where(mine == 0, s_ref[0], r_ref[0]).astype(F32)
        for s in range(1, N_DEV):
            g = g + jnp.where(mine == s, s_ref[0], r_ref[s]).astype(F32)
        mn = ADAM_B1 * m_ref[0] + (1.0 - ADAM_B1) * g
        vn = ADAM_B2 * v_ref[0] + (1.0 - ADAM_B2) * (g * g)
        g_ref[0] = g
        mo_ref[0] = mn
        vo_ref[0] = vn
        d_ref[0] = -ADAM_LR * ((mn / c1) / (jnp.sqrt(vn / c2) + ADAM_EPS) + ADAM_WD * w_ref[0])

    one = pl.BlockSpec((1, tr, c), lambda i, me_ref: (0, i, 0))
    return pl.pallas_call(
        body,
        grid_spec=pltpu.PrefetchScalarGridSpec(
            num_scalar_prefetch=1, grid=(r // tr,),
            in_specs=[pl.BlockSpec((N_DEV, tr, c), lambda i, me_ref: (0, i, 0)),
                      pl.BlockSpec((1, tr, c), lambda i, me_ref: (me_ref[0], i, 0)), one, one, one],
            out_specs=[one] * 4),
        out_shape=[jax.ShapeDtypeStruct((1, r, c), F32)] * 4,
        name=name, compiler_params=_cp())(me, recv, sent, w, m, v)


def _local_step(x, target, sp, need, ahead, emit):
    g = {}
    x0, h1 = x, _rms_fwd(x, sp["norm_ffn1"], "ffn1_norm")
    act1, saved1 = _ffn_up(h1, lambda kind, a: need("w" + kind + "1", a),
                           lambda kind, a: ahead("w" + kind + "1", a), "ffn1")
    x1, h2 = _mm_nn_resnorm([(act1, saved1[-1])], x0, 0.5, sp["norm_mix"], 512, "ffn1_down_norm",
                            after=ahead("win_a", act1))
    win_a, win_d = need("win_a", h2), need("win_d", h2)
    conv_w8, wout = need("conv_w8", h2), need("wout", h2)
    z_at = _mm_nn(h2, win_a, F32, 1024, 768, "mix_in_attn")
    z_dn = _mm_nn(h2, win_d, F32, 1024, 768, "mix_in_dn")

    conv, qkvn, bg = _dn_pre_fwd(z_dn, conv_w8, sp["alog_row"], sp["dtb_row"], "dn_pre")
    *prep, tinv = _dn_prep_fwd(qkvn, bg, "dn_prep")
    o_dn, states = _dn_scan_fwd(prep, "dn_scan")
    dn_b = _dn_post_fwd(o_dn, z_dn, sp["dn_norm"], "dn_post")
    attn_f, attn_b, lse = _attn_fwd(z_at, "attn_fwd", after=ahead("wg2", dn_b))

    x2, h3 = _mm_nn_resnorm([(attn_b, wout[:AW]), (dn_b, wout[AW:])], x1, 1.0, sp["norm_ffn2"], 512,
                            "mix_out_norm")
    act2, saved2 = _ffn_up(h3, lambda kind, a: need("w" + kind + "2", a),
                           lambda kind, a: ahead("w" + kind + "2", a), "ffn2")

    dx3, dys3, g["norm_final"], loss8 = _down_loss_bwd(x2, act2, saved2[-1], sp["norm_final"], target,
                                                       "ffn2_down_loss")
    dx2, dx2b, g["norm_ffn2"] = _ffn_bwd(
        x2, sp["norm_ffn2"], saved2, dx3, dys3, 1.0,
        lambda kind, dw: emit(kind + "2", {"w" + kind + "2": dw}), "ffn2b")

    zero = emit("wout", {"wout": jnp.concatenate([_mm_tn(attn_b, dx2b, BF, 512, 1024, "mix_out_dw_a"),
                                                  _mm_tn(dn_b, dx2b, BF, 512, 1024, "mix_out_dw_d")], axis=0)})
    dmix = _mm_nt(dx2b, wout, F32, 1024, 1024, "mix_out_dx", after=zero)

    dq, dk, dv = _attn_bwd(z_at, dmix, attn_f, lse, "attn_bwd")

    do_dn, dgate, g["dn_norm"] = _dn_post_bwd(o_dn, z_dn, sp["dn_norm"], dmix, "dn_post_b")
    cts = _dn_scan_bwd(prep, states, do_dn, "dn_scan_b")
    dqn, dkn, dvn, dbg = _dn_prep_bwd(qkvn, bg, tinv, cts, "dn_prep_b")
    dconv, dbd, g["alog_row"], g["dtb_row"] = _dn_pre_bwd(
        conv, z_dn, sp["alog_row"], sp["dtb_row"], dqn, dkn, dvn, dbg, "dn_pre_b")
    draw, dconv_w8 = _dn_conv_bwd(dconv, z_dn, conv_w8, "dn_conv_b")

    pieces = ((dq, win_a, 0), (dk, win_a, 1), (dv, win_a, 2),
              (draw, win_d, 0), (dgate, win_d, 3 * DNW // DNW), (dbd, win_d, BD_BLK))
    zero = emit("win", {"w_in_parts": [_mm_tn(h2, p[0], BF, 512, min(p[0].shape[1], 768), "mix_in_dw%d" % i)
                                       for i, p in enumerate(pieces)], "conv_w8": dconv_w8})
    dx1, dys1, g["norm_mix"] = _mm_rms_bwd(pieces, NT, x1, sp["norm_mix"], dx2, 0.5, 512, "mix_in_dx_dnorm",
                                           after=zero)
    dx0, _, g["norm_ffn1"] = _ffn_bwd(
        x0, sp["norm_ffn1"], saved1, dx1, dys1, 1.0,
        lambda kind, dw: emit(kind + "1", {"w" + kind + "1": dw}), "ffn1b")
    return loss8[0, 0], dx0, g


def _cols_from_shards(gathered):
    n, r, c = gathered.shape
    return jnp.transpose(gathered, (1, 0, 2)).reshape(r, n * c)


def _shards_from_cols(full, dtype):
    r, nc = full.shape
    return jnp.transpose(full.reshape(r, N_DEV, nc // N_DEV), (1, 0, 2)).astype(dtype)


def _lane_row(vec4):
    return jnp.zeros((1, 128), F32).at[:, DN_H:2 * DN_H].set(vec4.astype(F32))


WEIGHT_SOURCES = {"wg1": "gate1", "wu1": "up1", "wd1": "down1", "win_a": "w_in", "win_d": "w_in",
                  "conv_w8": "conv_w", "wout": "w_out", "wg2": "gate2", "wu2": "up2", "wd2": "down2"}
TRANSPOSED = ("gate1", "up1", "gate2", "up2")


def _build_weights(name, gath):
    if name in ("wg1", "wu1", "wd1", "wg2", "wu2", "wd2"):
        return {name: gath[WEIGHT_SOURCES[name]].reshape(F, D)}
    if name in ("win_a", "win_d"):
        w_in = _cols_from_shards(gath["w_in"])
        c0 = 3 * AW + 3 * DNW
        win_d = jnp.concatenate([w_in[:, ZA:c0], w_in[:, c0 + 2 * DN_H:], w_in[:, c0:c0 + 2 * DN_H],
                                 jnp.zeros((D, ZP - IN_COLS), w_in.dtype)], axis=1)
        return {"win_a": w_in[:, :ZA], "win_d": win_d}
    if name == "wout":
        return {name: gath["w_out"].reshape(D, D)}
    conv = _cols_from_shards(gath["conv_w"])
    return {"conv_w8": jnp.concatenate([conv, jnp.zeros((4, 3 * DNW), F32)], axis=0)}


def _small_params(norm_ffn1, norm_mix, norm_ffn2, norm_final, a_log, dt_bias, dn_norm):
    return {"norm_ffn1": norm_ffn1, "norm_mix": norm_mix, "norm_ffn2": norm_ffn2,
            "norm_final": norm_final.reshape(1, D), "alog_row": _lane_row(a_log), "dtb_row": _lane_row(dt_bias),
            "dn_norm": dn_norm}


def _grad_slabs(group, g):
    if group[0] in "gud":
        return {WEIGHT_SOURCES["w" + group]: g["w" + group].reshape(N_DEV, F // N_DEV, D)}
    if group == "wout":
        return {"w_out": g["wout"].reshape(N_DEV, D // N_DEV, D)}
    aq, ak, av, dqkv, gate, bd = g["w_in_parts"]
    g_in = jnp.concatenate([aq, ak, av, dqkv, bd[:, :2 * DN_H], gate], axis=1)
    return {"w_in": _shards_from_cols(g_in, BF), "conv_w": _shards_from_cols(g["conv_w8"][:4], F32)}


SMALL_ROWS = 40


def _small_pack(norm_ffn1, norm_mix, norm_ffn2, norm_final, dn_norm, alog_row, dtb_row, loss=None):
    rows = [a.reshape(8, 128) for a in (norm_ffn1, norm_mix, norm_ffn2, norm_final)]
    loss_row = jnp.zeros((1, 128), F32) if loss is None else jnp.broadcast_to(loss.reshape(1, 1), (1, 128))
    rows += [dn_norm.reshape(1, 128), alog_row, dtb_row, loss_row, jnp.zeros((SMALL_ROWS - 36, 128), F32)]
    return jnp.concatenate(rows, axis=0)


def _small_unpack(pk):
    pk = pk[0]
    return (pk[0:8].reshape(1, D), pk[8:16].reshape(1, D), pk[16:24].reshape(1, D), pk[24:32].reshape(D),
            pk[32:33], pk[33:34, DN_H:2 * DN_H], pk[34:35, DN_H:2 * DN_H])


ADAM_TILE = {"gate1": 176, "up1": 176, "down1": 176, "gate2": 176, "up2": 176, "down2": 176,
             "w_in": 256, "w_out": 128, "conv_w": 4}
BIG = ("gate1", "up1", "down1", "w_in", "w_out", "gate2", "up2", "down2", "conv_w")


def kernel(x, norm_ffn1, ffn1_gate, ffn1_up, ffn1_down, norm_mix, w_in, conv_w, a_log, dt_bias, dn_norm, w_out, norm_ffn2, ffn2_gate, ffn2_up, ffn2_down, norm_final, loss_target, m_norm_ffn1, m_ffn1_gate, m_ffn1_up, m_ffn1_down, m_norm_mix, m_w_in, m_conv_w, m_a_log, m_dt_bias, m_dn_norm, m_w_out, m_norm_ffn2, m_ffn2_gate, m_ffn2_up, m_ffn2_down, m_norm_final, v_norm_ffn1, v_ffn1_gate, v_ffn1_up, v_ffn1_down, v_norm_mix, v_w_in, v_conv_w, v_a_log, v_dt_bias, v_dn_norm, v_w_out, v_norm_ffn2, v_ffn2_gate, v_ffn2_up, v_ffn2_down, v_norm_final):
    w = {"gate1": ffn1_gate, "up1": ffn1_up, "down1": ffn1_down, "w_in": w_in, "w_out": w_out,
         "gate2": ffn2_gate, "up2": ffn2_up, "down2": ffn2_down, "conv_w": conv_w}
    m = {"gate1": m_ffn1_gate, "up1": m_ffn1_up, "down1": m_ffn1_down, "w_in": m_w_in, "w_out": m_w_out,
         "gate2": m_ffn2_gate, "up2": m_ffn2_up, "down2": m_ffn2_down, "conv_w": m_conv_w}
    v = {"gate1": v_ffn1_gate, "up1": v_ffn1_up, "down1": v_ffn1_down, "w_in": v_w_in, "w_out": v_w_out,
         "gate2": v_ffn2_gate, "up2": v_ffn2_up, "down2": v_ffn2_down, "conv_w": v_conv_w}

    me = 4 * lax.axis_index("x") + 2 * lax.axis_index("y") + lax.axis_index("c")
    own_slot = lambda land, mine: lax.dynamic_update_index_in_dim(land, mine, me, 0)

    ag_order = ("gate1", "up1", "down1", "w_in", "conv_w", "w_out", "gate2", "up2", "down2")
    ag_groups = (("gate1",), ("up1",), ("down1",), ("w_in", "conv_w", "w_out"), ("gate2", "up2", "down2"))
    pos = {n: i for i, n in enumerate(ag_order)}

    def shard(n):
        if n == "conv_w":
            return w[n][0]
        return (w[n][0].T if n in TRANSPOSED else w[n][0]).astype(BF)

    ss, rs, thru, land, zero = _xstart([shard(n) for n in ag_order], False, "weights_start", FIRST_HOP)
    gath, built, on_its_way = {}, {}, {}
    group_of = lambda name: [i for i, grp in enumerate(ag_groups) if WEIGHT_SOURCES[name] in grp][0]

    def ahead(name, after):
        gi = group_of(name)
        if WEIGHT_SOURCES[name] in gath or gi in on_its_way:
            return None
        ids = [pos[n] for n in ag_groups[gi]]
        thrus, lands = _xwait([ss[i] for i in ids], [rs[i] for i in ids], [thru[i] for i in ids],
                              [land[i] for i in ids], False, after, "weights_wait%d" % gi, FIRST_HOP)
        fss, frs, lands, token = _fstart(lands, "weights_forward%d" % gi)
        on_its_way[gi] = (thrus, fss, frs, lands)
        return token

    def need(name, after):
        if name not in built:
            if WEIGHT_SOURCES[name] not in gath:
                gi = group_of(name)
                ahead(name, after)
                thrus, fss, frs, lands = on_its_way.pop(gi)
                lands = _fwait(fss, frs, lands, after, "weights_forward_wait%d" % gi)
                for n, t, l in zip(ag_groups[gi], thrus, lands):
                    gath[n] = own_slot(l, t)
            built.update(_build_weights(name, gath))
        return built[name]

    pending = []

    def emit(group, grads):
        slabs = grads if group == "small" else _grad_slabs(group, grads)
        names = list(slabs)
        started = _xstart([slabs[n] for n in names], True, "grads_start_" + group)
        pending.append((group, names) + started[:4])
        return started[4]

    sp = _small_params(norm_ffn1 + zero, norm_mix, norm_ffn2, norm_final, a_log, dt_bias, dn_norm)
    loss_part, dx, g = _local_step(x[0], loss_target[0], sp, need, ahead, emit)
    small = _small_pack(g["norm_ffn1"], g["norm_mix"], g["norm_ffn2"], g["norm_final"], g["dn_norm"],
                        g["alog_row"], g["dtb_row"], loss_part)
    emit("small", {"small": jnp.broadcast_to(small[None], (N_DEV, SMALL_ROWS, 128))})

    pack = lambda a: _small_pack(*a)[None]
    me1 = me.astype(jnp.int32).reshape(1)
    res, after = {}, dx
    for group, names, gss, grs, gthru, gland in pending:
        thrus, lands = _xwait(gss, grs, gthru, gland, True, after, "grads_wait_" + group)
        for n, sent, recv in zip(names, thrus, lands):
            if n == "small":
                res[n] = _adam(
                    recv, sent, me1,
                    pack((norm_ffn1, norm_mix, norm_ffn2, norm_final, dn_norm, _lane_row(a_log), _lane_row(dt_bias))),
                    pack((m_norm_ffn1, m_norm_mix, m_norm_ffn2, m_norm_final, m_dn_norm, _lane_row(m_a_log),
                          _lane_row(m_dt_bias))),
                    pack((v_norm_ffn1, v_norm_mix, v_norm_ffn2, v_norm_final, v_dn_norm, _lane_row(v_a_log),
                          _lane_row(v_dt_bias))),
                    SMALL_ROWS, "adam_small")
            elif n in TRANSPOSED:
                flip = lambda a: jnp.swapaxes(a, 1, 2)
                res[n] = [flip(o) for o in _adam(recv, sent, me1, flip(w[n]), flip(m[n]), flip(v[n]), ADAM_TILE[n],
                                                 "adam_" + n)]
            else:
                res[n] = _adam(recv, sent, me1, w[n], m[n], v[n], ADAM_TILE[n], "adam_" + n)
            after = res[n][0]
    res_s = res["small"]

    loss = res_s[0][0, 35, 0]
    outs = [loss, dx[None]]
    for k in range(4):
        n1, nm, n2, nf, dn, al, dt = _small_unpack(res_s[k])
        big = {n: res[n][k] for n in BIG}
        outs += [n1, big["gate1"], big["up1"], big["down1"], nm, big["w_in"], big["conv_w"], al, dt, dn,
                 big["w_out"], n2, big["gate2"], big["up2"], big["down2"], nf]
    return tuple(outs)
```

```python
import functools

import numpy as np
import jax
import jax.numpy as jnp
from jax import lax
from jax.experimental import pallas as pl
from jax.experimental.pallas import tpu as pltpu

T = 4096
D = 1024
F = 2816
N_DEV = 8
A_HEADS = 8
A_HD = 64
AW = A_HEADS * A_HD
DN_H = 4
DN_HD = 128
DNW = DN_H * DN_HD
CH = 64
PAIR = 2 * CH
ZA = 3 * AW
ZD = 3 * DNW + DNW + 256
ZP = ZA + ZD
BD_BLK = (3 * DNW + DNW) // 128
IN_COLS = 3592
PATTERNS = ((128, 1), (512, 4), (2048, 16))
NORM_EPS = 1e-6
L2_EPS = 1e-6
ADAM_LR, ADAM_B1, ADAM_B2, ADAM_EPS, ADAM_WD, ADAM_STEP = 0.001, 0.9, 0.999, 1e-08, 0.01, 10
VMEM_LIMIT = 56 * 1024 * 1024
NEG = -1e30

BF = jnp.bfloat16
F32 = jnp.float32
NN = (((1,), (0,)), ((), ()))
NT = (((1,), (1,)), ((), ()))
TN = (((0,), (0,)), ((), ()))
HI = lax.Precision.HIGHEST
MESH_ID = pl.DeviceIdType.MESH
ANY = pl.BlockSpec(memory_space=pl.ANY)


def _cp():
    return pltpu.CompilerParams(vmem_limit_bytes=VMEM_LIMIT)


def _dg(a, b, dims):
    return lax.dot_general(a, b, dims, preferred_element_type=F32)


def _hdot(a, b):
    return lax.dot_general(a, b, NN, precision=HI, preferred_element_type=F32)


def _make_bdot(dims, da_dims, da_swap, db_dims, db_swap):
    @jax.custom_vjp
    def f(a, b):
        return _dg(a.astype(BF), b.astype(BF), dims)

    def fwd(a, b):
        return f(a, b), (a, b)

    def bwd(res, g):
        a, b = res
        gb, ab, bb = g.astype(BF), a.astype(BF), b.astype(BF)
        da = _dg(bb, gb, da_dims) if da_swap else _dg(gb, bb, da_dims)
        db = _dg(gb, ab, db_dims) if db_swap else _dg(ab, gb, db_dims)
        return da.astype(a.dtype), db.astype(b.dtype)

    f.defvjp(fwd, bwd)
    return f


_bdot_nn = _make_bdot(NN, NT, False, TN, False)
_bdot_nt = _make_bdot(NT, NN, False, TN, True)
_bdot_tn = _make_bdot(TN, NT, True, NN, False)


def _iota(shape, dim):
    return lax.broadcasted_iota(jnp.int32, shape, dim)


def _col(x, idx):
    return jnp.sum(jnp.where(_iota(x.shape, 1) == idx, x, 0.0), axis=1, keepdims=True)


def _mm_nn(a, b, out_dtype, tm, tn, name):
    m, k = a.shape
    n = b.shape[1]

    def body(a_ref, b_ref, o_ref):
        o_ref[...] = _dg(a_ref[...], b_ref[...], NN).astype(out_dtype)

    return pl.pallas_call(
        body, grid=(m // tm, n // tn),
        in_specs=[pl.BlockSpec((tm, k), lambda i, j: (i, 0)), pl.BlockSpec((k, tn), lambda i, j: (0, j))],
        out_specs=pl.BlockSpec((tm, tn), lambda i, j: (i, j)),
        out_shape=jax.ShapeDtypeStruct((m, n), out_dtype), name=name, compiler_params=_cp())(a, b)


def _tie(body, after):
    if after is None:
        return body, [], []
    return (lambda tok_ref, *refs: body(*refs)), [ANY], [after.reshape(1, 1)]


def _mm_nt(a, b, out_dtype, tm, tb, name, after=None):
    m, c = a.shape
    kb = b.shape[0]

    def body(a_ref, b_ref, o_ref):
        o_ref[...] = _dg(a_ref[...], b_ref[...], NT).astype(out_dtype)

    body, tspec, tok = _tie(body, after)
    return pl.pallas_call(
        body, grid=(m // tm, kb // tb),
        in_specs=tspec + [pl.BlockSpec((tm, c), lambda i, j: (i, 0)), pl.BlockSpec((tb, c), lambda i, j: (j, 0))],
        out_specs=pl.BlockSpec((tm, tb), lambda i, j: (i, j)),
        out_shape=jax.ShapeDtypeStruct((m, kb), out_dtype), name=name, compiler_params=_cp())(*tok, a, b)


def _mm_tn(a, b, out_dtype, ta, tb, name, after=None):
    m, ka = a.shape
    nb = b.shape[1]

    def body(a_ref, b_ref, o_ref):
        o_ref[...] = _dg(a_ref[...], b_ref[...], TN).astype(out_dtype)

    body, tspec, tok = _tie(body, after)
    return pl.pallas_call(
        body, grid=(ka // ta, nb // tb),
        in_specs=tspec + [pl.BlockSpec((m, ta), lambda i, j: (0, i)), pl.BlockSpec((m, tb), lambda i, j: (0, j))],
        out_specs=pl.BlockSpec((ta, tb), lambda i, j: (i, j)),
        out_shape=jax.ShapeDtypeStruct((ka, nb), out_dtype), name=name, compiler_params=_cp())(*tok, a, b)


def _mm_nt_swiglu(h, wu_t, gate, tm, tb, name, after=None):
    m, c = h.shape
    kb = wu_t.shape[0]

    def body(h_ref, w_ref, g_ref, u_ref, a_ref):
        u = _dg(h_ref[...], w_ref[...], NT)
        g = g_ref[...].astype(F32)
        u_ref[...] = u.astype(BF)
        a_ref[...] = (g * jax.nn.sigmoid(g) * u).astype(BF)

    body, tspec, tok = _tie(body, after)
    tile = pl.BlockSpec((tm, tb), lambda i, j: (i, j))
    return pl.pallas_call(
        body, grid=(m // tm, kb // tb),
        in_specs=tspec + [pl.BlockSpec((tm, c), lambda i, j: (i, 0)), pl.BlockSpec((tb, c), lambda i, j: (j, 0)),
                          tile],
        out_specs=[tile, tile], out_shape=[jax.ShapeDtypeStruct((m, kb), BF)] * 2,
        name=name, compiler_params=_cp())(*tok, h, wu_t, gate)


def _mm_nt_dswiglu(dys, wd, gate, up, tm, tb, name, after=None):
    m, c = dys.shape
    kb = wd.shape[0]

    def body(d_ref, w_ref, g_ref, u_ref, dg_ref, du_ref):
        da = _dg(d_ref[...], w_ref[...], NT)
        g = g_ref[...].astype(F32)
        u = u_ref[...].astype(F32)
        s = jax.nn.sigmoid(g)
        dg_ref[...] = (da * u * (s * (1.0 + g * (1.0 - s)))).astype(BF)
        du_ref[...] = (da * (g * s)).astype(BF)

    body, tspec, tok = _tie(body, after)
    tile = pl.BlockSpec((tm, tb), lambda i, j: (i, j))
    return pl.pallas_call(
        body, grid=(m // tm, kb // tb),
        in_specs=tspec + [pl.BlockSpec((tm, c), lambda i, j: (i, 0)), pl.BlockSpec((tb, c), lambda i, j: (j, 0)),
                          tile, tile],
        out_specs=[tile, tile], out_shape=[jax.ShapeDtypeStruct((m, kb), BF)] * 2,
        name=name, compiler_params=_cp())(*tok, dys, wd, gate, up)


def _rms_fwd(x, gain, name):
    tm = 512
    row = pl.BlockSpec((tm, D), lambda i: (i, 0))

    def body(x_ref, g_ref, h_ref):
        xv = x_ref[...]
        r = lax.rsqrt(jnp.mean(xv * xv, axis=-1, keepdims=True) + NORM_EPS)
        h_ref[...] = (xv * r * g_ref[...]).astype(BF)

    return pl.pallas_call(
        body, grid=(T // tm,), in_specs=[row, pl.BlockSpec((1, D), lambda i: (0, 0))], out_specs=row,
        out_shape=jax.ShapeDtypeStruct((T, D), BF), name=name, compiler_params=_cp())(x, gain)


def _mm_rms_bwd(pairs, dims, x, gain, dres, alpha_out, tm, name, after=None):
    n = len(pairs)
    row = pl.BlockSpec((tm, D), lambda i: (i, 0))
    gspec = pl.BlockSpec((1, D), lambda i: (0, 0))

    def body(*refs):
        x_ref, g_ref, dres_ref, dx_ref, dxs_ref, dg_ref = refs[2 * n:]
        i = pl.program_id(0)
        dhv = _dg(refs[0][...], refs[n][...], dims)
        for p in range(1, n):
            dhv = dhv + _dg(refs[p][...], refs[n + p][...], dims)
        xv = x_ref[...]
        r = lax.rsqrt(jnp.mean(xv * xv, axis=-1, keepdims=True) + NORM_EPS)
        xh = xv * r
        part = jnp.sum(dhv * xh, axis=0, keepdims=True)

        @pl.when(i == 0)
        def _():
            dg_ref[...] = part

        @pl.when(i > 0)
        def _():
            dg_ref[...] += part

        dxh = dhv * g_ref[...]
        dx = r * (dxh - xh * jnp.mean(dxh * xh, axis=-1, keepdims=True)) + dres_ref[...]
        dx_ref[...] = dx
        dxs_ref[...] = (alpha_out * dx).astype(BF)

    body, tspec, tok = _tie(body, after)

    def bspec(a, b, kblk):
        if kblk is None:
            return pl.BlockSpec(b.shape, lambda i: (0, 0))
        return pl.BlockSpec((b.shape[0], a.shape[1]), lambda i: (0, kblk))

    return pl.pallas_call(
        body, grid=(T // tm,),
        in_specs=tspec + [pl.BlockSpec((tm, p[0].shape[1]), lambda i: (i, 0)) for p in pairs]
        + [bspec(p[0], p[1], p[2] if len(p) > 2 else None) for p in pairs] + [row, gspec, row],
        out_specs=[row, row, gspec],
        out_shape=[jax.ShapeDtypeStruct((T, D), F32), jax.ShapeDtypeStruct((T, D), BF),
                   jax.ShapeDtypeStruct((1, D), F32)],
        name=name, compiler_params=_cp())(*tok, *[p[0] for p in pairs], *[p[1] for p in pairs], x, gain, dres)


def _down_loss_bwd(x_prev, act, wd, gain, target, name):
    tm = 512
    row = pl.BlockSpec((tm, D), lambda i: (i, 0))
    gspec = pl.BlockSpec((1, D), lambda i: (0, 0))
    lspec = pl.BlockSpec((8, 128), lambda i: (0, 0))

    def body(x_ref, a_ref, w_ref, g_ref, t_ref, dx_ref, dxs_ref, dg_ref, loss_ref):
        i = pl.program_id(0)
        xv = x_ref[...] + 0.5 * _dg(a_ref[...], w_ref[...], NN)
        r = lax.rsqrt(jnp.mean(xv * xv, axis=-1, keepdims=True) + NORM_EPS)
        xh = xv * r
        diff = xh * g_ref[...] - t_ref[...]
        lpart = 0.5 * jnp.sum(jnp.mean(diff * diff, axis=-1, keepdims=True), axis=0, keepdims=True)
        dy = diff * (1.0 / D)
        part = jnp.sum(dy * xh, axis=0, keepdims=True)

        @pl.when(i == 0)
        def _():
            dg_ref[...] = part
            loss_ref[...] = jnp.broadcast_to(lpart, (8, 128))

        @pl.when(i > 0)
        def _():
            dg_ref[...] += part
            loss_ref[...] += jnp.broadcast_to(lpart, (8, 128))

        dxh = dy * g_ref[...]
        dx = r * (dxh - xh * jnp.mean(dxh * xh, axis=-1, keepdims=True))
        dx_ref[...] = dx
        dxs_ref[...] = (0.5 * dx).astype(BF)

    return pl.pallas_call(
        body, grid=(T // tm,),
        in_specs=[row, pl.BlockSpec((tm, F), lambda i: (i, 0)), pl.BlockSpec((F, D), lambda i: (0, 0)), gspec, row],
        out_specs=[row, row, gspec, lspec],
        out_shape=[jax.ShapeDtypeStruct((T, D), F32), jax.ShapeDtypeStruct((T, D), BF),
                   jax.ShapeDtypeStruct((1, D), F32), jax.ShapeDtypeStruct((8, 128), F32)],
        name=name, compiler_params=_cp())(x_prev, act, wd, gain, target)


def _mm_nn_resnorm(pairs, x_prev, alpha, gain, tm, name, after=None):
    m = x_prev.shape[0]
    n = len(pairs)

    def body(*refs):
        x_ref, g_ref, xo_ref, h_ref = refs[2 * n:]
        y = _dg(refs[0][...], refs[n][...], NN)
        for i in range(1, n):
            y = y + _dg(refs[i][...], refs[n + i][...], NN)
        xv = x_ref[...] + alpha * y
        xo_ref[...] = xv
        r = lax.rsqrt(jnp.mean(xv * xv, axis=-1, keepdims=True) + NORM_EPS)
        h_ref[...] = (xv * r * g_ref[...]).astype(BF)

    body, tspec, tok = _tie(body, after)
    row = pl.BlockSpec((tm, D), lambda i: (i, 0))
    return pl.pallas_call(
        body, grid=(m // tm,),
        in_specs=tspec + [pl.BlockSpec((tm, a.shape[1]), lambda i: (i, 0)) for a, _ in pairs]
        + [pl.BlockSpec(b.shape, lambda i: (0, 0)) for _, b in pairs] + [row, pl.BlockSpec((1, D), lambda i: (0, 0))],
        out_specs=[row, row],
        out_shape=[jax.ShapeDtypeStruct((m, D), F32), jax.ShapeDtypeStruct((m, D), BF)],
        name=name, compiler_params=_cp())(*tok, *[a for a, _ in pairs], *[b for _, b in pairs], x_prev, gain)


def _ffn_up(h, need, ahead, tag):
    wg = need("g", h)
    gate = _mm_nt(h, wg, BF, 1024, 1408, tag + "_gate")
    wu = need("u", gate)
    up, act = _mm_nt_swiglu(h, wu, gate, 1024, 1408, tag + "_up_act", after=ahead("d", wu))
    wd = need("d", up)
    return act, (h, gate, up, act, wg, wu, wd)


def _ffn_bwd(x_in, gain, saved, dxo, dys, alpha_out, emit, tag):
    h, gate, up, act, wg, wu, wd = saved
    sent = emit("d", _mm_tn(act, dys, BF, 256, 1024, tag + "_dwd"))
    dgate, dup = _mm_nt_dswiglu(dys, wd, gate, up, 1024, 1408, tag + "_dact", after=sent)
    sent = emit("g", _mm_tn(dgate, h, BF, 256, 1024, tag + "_dwg"))
    sent = emit("u", _mm_tn(dup, h, BF, 256, 1024, tag + "_dwu", after=sent))
    return _mm_rms_bwd([(dgate, wg), (dup, wu)], NN, x_in, gain, dxo, alpha_out, 512, tag + "_dh_dnorm",
                       after=sent)


SLAB = 2048
N_SLAB = T // SLAB
N_PAIR = A_HEADS // 2


def _pair_masks():
    lane = _iota((128, 128), 1)
    return lane < A_HD, lane >= A_HD


def _slope_table():
    h = 2 * jnp.arange(N_PAIR)[:, None] + jnp.minimum(jnp.arange(8), 1)[None, :]
    return jnp.broadcast_to((2.0 ** (-(h + 1).astype(F32)))[:, :, None], (N_PAIR, 8, 128))


def _rows(ref, start, d):
    if d == 1:
        return ref[pl.ds(start, 128), :]
    return ref[pl.ds(start, 128, stride=d), :]


def _put_rows(ref, start, d, val):
    if d == 1:
        ref[pl.ds(start, 128), :] = val
    else:
        ref[pl.ds(start, 128, stride=d), :] = val


def _units(d):
    return [(r, b, r + 128 * d * b) for r in range(d) for b in range(SLAB // (128 * d))]


def _biases(d, slopes, has_prev):
    qi = _iota((128, 256), 0)
    kj = _iota((128, 256), 1)
    steps = qi + 128 - kj
    in_band = (steps >= 0) & (steps <= 128)
    dist = (steps * d).astype(F32)
    base = [jnp.where(in_band, -(sl * dist), NEG) for sl in slopes]
    edge = [jnp.where(has_prev | (kj >= 128), b, NEG) for b in base]
    return base, edge


def _attn_fwd(z_at, name, after=None):
    def body(sl_ref, q_ref, kc_ref, kp_ref, vc_ref, vp_ref, of_ref, ob_ref, lse_ref, m_s, l_s, a_s):
        n = pl.program_id(1)
        lo, hi = _pair_masks()
        slopes = (sl_ref[0, 0:1, 0:1], sl_ref[0, 1:2, 0:1])

        def unit(d, start, b, first, carry, bias):
            q = (_rows(q_ref, start, d) * (A_HD ** -0.5)).astype(BF)
            kcur, vcur = _rows(kc_ref, start, d).astype(BF), _rows(vc_ref, start, d).astype(BF)
            if b > 0:
                kprev, vprev = carry
            else:
                pstart = start + SLAB - 128 * d
                kprev, vprev = _rows(kp_ref, pstart, d).astype(BF), _rows(vp_ref, pstart, d).astype(BF)
            kcat = jnp.concatenate([kprev, kcur], axis=0)
            vcat = jnp.concatenate([vprev, vcur], axis=0)
            ms, ls, pvs = [], [], []
            for e in range(2):
                qm = jnp.where(lo if e == 0 else hi, q, jnp.zeros_like(q))
                s = _dg(qm, kcat, NT) + bias[e]
                m = jnp.max(s, axis=1, keepdims=True)
                p = jnp.exp(s - m)
                ms.append(m)
                ls.append(jnp.sum(p, axis=1, keepdims=True))
                pvs.append(_dg(p.astype(BF), vcat, NN))
            m_u = jnp.where(lo, ms[0], ms[1])
            l_u = jnp.where(lo, ls[0], ls[1])
            a_u = jnp.where(lo, pvs[0], pvs[1])
            if first:
                m_n, l_n, a_n = m_u, l_u, a_u
            else:
                m_o = _rows(m_s, start, d)
                m_n = jnp.maximum(m_o, m_u)
                c_o = jnp.exp(m_o - m_n)
                c_u = jnp.exp(m_u - m_n)
                l_n = _rows(l_s, start, d) * c_o + l_u * c_u
                a_n = _rows(a_s, start, d) * c_o + a_u * c_u
            _put_rows(m_s, start, d, m_n)
            _put_rows(l_s, start, d, l_n)
            _put_rows(a_s, start, d, a_n)
            return kcur, vcur

        for pi, (_, d) in enumerate(PATTERNS):
            base, edge = _biases(d, slopes, n > 0)
            carry = None
            for r, b, start in _units(d):
                carry = unit(d, start, b, pi == 0, carry, edge if b == 0 else base)
        l = l_s[...]
        out = a_s[...] / l
        of_ref[...] = out
        ob_ref[...] = out.astype(BF)
        lse_ref[...] = m_s[...] + jnp.log(l)

    body, tspec, tok = _tie(body, after)
    cur = lambda c: pl.BlockSpec((SLAB, 128), lambda j, n: (n, c * N_PAIR + j))
    prv = lambda c: pl.BlockSpec((SLAB, 128), lambda j, n: (jnp.maximum(n - 1, 0), c * N_PAIR + j))
    out = pl.BlockSpec((SLAB, 128), lambda j, n: (n, j))
    return pl.pallas_call(
        body, grid=(N_PAIR, N_SLAB),
        in_specs=tspec + [pl.BlockSpec((1, 8, 128), lambda j, n: (j, 0, 0)), cur(0), cur(1), prv(1), cur(2), prv(2)],
        out_specs=[out, out, out],
        out_shape=[jax.ShapeDtypeStruct((T, AW), F32), jax.ShapeDtypeStruct((T, AW), BF),
                   jax.ShapeDtypeStruct((T, AW), F32)],
        scratch_shapes=[pltpu.VMEM((SLAB, 128), F32)] * 3,
        name=name, compiler_params=_cp())(*tok, _slope_table(), z_at, z_at, z_at, z_at, z_at)


def _attn_bwd(z_at, dout, out, lse, name):
    def body(sl_ref, q_ref, kc_ref, kp_ref, vc_ref, vp_ref, do_ref, o_ref, lse_ref, dq_ref, dk_ref, dv_ref,
             dq_s, dk_s, dv_s, ck_s, cv_s):
        step = pl.program_id(1)
        n = N_SLAB - 1 - step
        lo, hi = _pair_masks()
        slopes = (sl_ref[0, 0:1, 0:1], sl_ref[0, 1:2, 0:1])

        @pl.when(step == 0)
        def _():
            ck_s[...] = jnp.zeros_like(ck_s)
            cv_s[...] = jnp.zeros_like(cv_s)

        dk_s[...] = ck_s[...]
        dv_s[...] = cv_s[...]
        ck_s[...] = jnp.zeros_like(ck_s)
        cv_s[...] = jnp.zeros_like(cv_s)

        def add_rows(ref, start, d, val):
            _put_rows(ref, start, d, _rows(ref, start, d) + val)

        def unit(d, start, b, first, carry, bias):
            q = (_rows(q_ref, start, d) * (A_HD ** -0.5)).astype(BF)
            do_f = _rows(do_ref, start, d)
            do = do_f.astype(BF)
            prod = do_f * _rows(o_ref, start, d)
            lse_u = _rows(lse_ref, start, d)
            kcur, vcur = _rows(kc_ref, start, d).astype(BF), _rows(vc_ref, start, d).astype(BF)
            if b > 0:
                kprev, vprev = carry
            else:
                pstart = start + SLAB - 128 * d
                kprev, vprev = _rows(kp_ref, pstart, d).astype(BF), _rows(vp_ref, pstart, d).astype(BF)
            kcat = jnp.concatenate([kprev, kcur], axis=0)
            vcat = jnp.concatenate([vprev, vcur], axis=0)
            masks = (lo, hi)
            qms = [jnp.where(msk, q, jnp.zeros_like(q)) for msk in masks]
            doms = [jnp.where(msk, do, jnp.zeros_like(do)) for msk in masks]
            deltas = [jnp.sum(jnp.where(msk, prod, 0.0), axis=1, keepdims=True) for msk in masks]
            ss = [_dg(qm, kcat, NT) + bs for qm, bs in zip(qms, bias)]
            dps = [_dg(dom, vcat, NT) for dom in doms]
            ps = [jnp.exp(s - lse_u[:, 64 * e:64 * e + 1]) for e, s in enumerate(ss)]
            dss = [(p * (dp - delta)).astype(BF) for p, dp, delta in zip(ps, dps, deltas)]
            pbs = [p.astype(BF) for p in ps]
            dqs = [_dg(ds, kcat, NN) for ds in dss]
            dkc = _dg(dss[0], qms[0], TN) + _dg(dss[1], qms[1], TN)
            dvc = _dg(pbs[0], doms[0], TN) + _dg(pbs[1], doms[1], TN)
            dq_u = jnp.where(lo, dqs[0], dqs[1]) * (A_HD ** -0.5)
            if first:
                _put_rows(dq_s, start, d, dq_u)
            else:
                add_rows(dq_s, start, d, dq_u)
            add_rows(dk_s, start, d, dkc[128:])
            add_rows(dv_s, start, d, dvc[128:])
            if b > 0:
                add_rows(dk_s, start - 128 * d, d, dkc[:128])
                add_rows(dv_s, start - 128 * d, d, dvc[:128])
            else:
                pstart = start + SLAB - 128 * d
                add_rows(ck_s, pstart, d, dkc[:128])
                add_rows(cv_s, pstart, d, dvc[:128])
            return kcur, vcur

        for pi, (_, d) in enumerate(PATTERNS):
            base, edge = _biases(d, slopes, n > 0)
            carry = None
            for r, b, start in _units(d):
                carry = unit(d, start, b, pi == 0, carry, edge if b == 0 else base)
        dq_ref[...] = dq_s[...].astype(BF)
        dk_ref[...] = dk_s[...].astype(BF)
        dv_ref[...] = dv_s[...].astype(BF)

    rev = lambda n: N_SLAB - 1 - n
    cur = lambda c: pl.BlockSpec((SLAB, 128), lambda j, n: (rev(n), c * N_PAIR + j))
    prv = lambda c: pl.BlockSpec((SLAB, 128), lambda j, n: (jnp.maximum(rev(n) - 1, 0), c * N_PAIR + j))
    one = pl.BlockSpec((SLAB, 128), lambda j, n: (rev(n), j))
    return pl.pallas_call(
        body, grid=(N_PAIR, N_SLAB),
        in_specs=[pl.BlockSpec((1, 8, 128), lambda j, n: (j, 0, 0)), cur(0), cur(1), prv(1), cur(2), prv(2),
                  one, one, one],
        out_specs=[one, one, one], out_shape=[jax.ShapeDtypeStruct((T, AW), BF)] * 3,
        scratch_shapes=[pltpu.VMEM((SLAB, 128), F32)] * 5,
        name=name, compiler_params=_cp())(_slope_table(), z_at, z_at, z_at, z_at, z_at, dout, out, lse)


def _silu(x):
    return x * jax.nn.sigmoid(x)


def _qk_math(c):
    s = _silu(c)
    return s * lax.rsqrt(jnp.sum(s * s, axis=-1, keepdims=True) + L2_EPS)


def _softplus(x):
    return jnp.maximum(x, 0.0) + jnp.log(1.0 + jnp.exp(-jnp.abs(x)))


def _gate_math(bd, alog_row, dtb_row):
    rows = bd.shape[0]
    lane = _iota(bd.shape, 1)
    beta = jax.nn.sigmoid(bd)
    g = jnp.where((lane >= DN_H) & (lane < 2 * DN_H), -jnp.exp(alog_row) * _softplus(bd + dtb_row), 0.0)
    ri = _iota((rows, rows), 0)
    ci = _iota((rows, rows), 1)
    same = (ri // CH) == (ci // CH)
    li = _iota((128, 128), 0)
    lj = _iota((128, 128), 1)
    to_next_group = jnp.where((lj == li + DN_H) & (li >= DN_H) & (li < 2 * DN_H), 1.0, 0.0)
    gc = _hdot(jnp.where(same & (ci <= ri), 1.0, 0.0), g)
    glast = _hdot(_hdot(jnp.where(same, 1.0, 0.0), g), to_next_group)
    return jnp.where(lane < DN_H, beta, 0.0) + gc + glast


def _shift_down(cur, halo, s):
    if s == 0:
        return cur
    rolled = pltpu.roll(cur, s, 0)
    hr = pltpu.roll(halo, s, 0)
    head = jnp.where(_iota(hr.shape, 0) < s, hr, rolled[:8])
    return jnp.concatenate([head, rolled[8:]], axis=0)


def _shift_up(cur, halo, s):
    if s == 0:
        return cur
    rows = cur.shape[0]
    rolled = pltpu.roll(cur, rows - s, 0)
    hr = pltpu.roll(halo, 8 - s, 0)
    tail = jnp.where(_iota(hr.shape, 0) >= 8 - s, hr, rolled[rows - 8:])
    return jnp.concatenate([rolled[:rows - 8], tail], axis=0)


def _dn_pre_fwd(z_dn, conv_w8, alog_row, dtb_row, name):
    tm = 512
    wq = 3 * DNW

    def body(raw_ref, halo_ref, bd_ref, w_ref, al_ref, dt_ref, conv_ref, qkv_ref, bg_ref):
        i = pl.program_id(0)
        cur = raw_ref[...]
        halo = jnp.where(i > 0, halo_ref[...], 0.0)
        w = w_ref[...]
        conv = jnp.zeros((tm, wq), F32)
        for j in range(4):
            conv = conv + _shift_down(cur, halo, 3 - j) * w[j:j + 1, :]
        conv_ref[...] = conv
        for blk in range(3 * DN_H):
            sl = slice(128 * blk, 128 * blk + 128)
            c = conv[:, sl]
            qkv_ref[:, sl] = _qk_math(c) if blk < 2 * DN_H else _silu(c)
        for r0 in range(0, tm, PAIR):
            rs = slice(r0, r0 + PAIR)
            bg_ref[rs, :] = _gate_math(bd_ref[rs, :], al_ref[...], dt_ref[...])

    one = pl.BlockSpec((1, 128), lambda i: (0, 0))
    return pl.pallas_call(
        body, grid=(T // tm,),
        in_specs=[pl.BlockSpec((tm, wq), lambda i: (i, 0)),
                  pl.BlockSpec((8, wq), lambda i: (jnp.maximum(i * (tm // 8) - 1, 0), 0)),
                  pl.BlockSpec((tm, 128), lambda i: (i, BD_BLK)),
                  pl.BlockSpec((8, wq), lambda i: (0, 0)), one, one],
        out_specs=[pl.BlockSpec((tm, wq), lambda i: (i, 0)), pl.BlockSpec((tm, wq), lambda i: (i, 0)),
                   pl.BlockSpec((tm, 128), lambda i: (i, 0))],
        out_shape=[jax.ShapeDtypeStruct((T, wq), F32), jax.ShapeDtypeStruct((T, wq), F32),
                   jax.ShapeDtypeStruct((T, 128), F32)],
        name=name, compiler_params=_cp())(z_dn, z_dn, z_dn, conv_w8, alog_row, dtb_row)


def _dn_pre_bwd(conv, z_dn, alog_row, dtb_row, dqn, dkn, dvn, dbg, name):
    tm = 512
    wq = 3 * DNW

    def body(conv_ref, bd_ref, al_ref, dt_ref, dq_ref, dk_ref, dv_ref, dbg_ref,
             dconv_ref, dbd_ref, dal_ref, ddt_ref):
        i = pl.program_id(0)
        for blk in range(3 * DN_H):
            sl = slice(128 * blk, 128 * blk + 128)
            src = (dq_ref, dk_ref, dv_ref)[blk // DN_H]
            ct = src[:, 128 * (blk % DN_H):128 * (blk % DN_H) + 128]
            fn = _qk_math if blk < 2 * DN_H else _silu
            _, vjp = jax.vjp(fn, conv_ref[:, sl])
            dconv_ref[:, sl] = vjp(ct)[0]
        dal = jnp.zeros((1, 128), F32)
        ddt = jnp.zeros((1, 128), F32)
        for r0 in range(0, tm, PAIR):
            rs = slice(r0, r0 + PAIR)
            _, vjp = jax.vjp(_gate_math, bd_ref[rs, :], al_ref[...], dt_ref[...])
            dbd, dal_p, ddt_p = vjp(dbg_ref[rs, :])
            dbd_ref[rs, :] = dbd.astype(BF)
            dal, ddt = dal + dal_p, ddt + ddt_p

        @pl.when(i == 0)
        def _():
            dal_ref[...] = dal
            ddt_ref[...] = ddt

        @pl.when(i > 0)
        def _():
            dal_ref[...] += dal
            ddt_ref[...] += ddt

    one = pl.BlockSpec((1, 128), lambda i: (0, 0))
    row = pl.BlockSpec((tm, wq), lambda i: (i, 0))
    hd = pl.BlockSpec((tm, DNW), lambda i: (i, 0))
    st = pl.BlockSpec((tm, 128), lambda i: (i, 0))
    return pl.pallas_call(
        body, grid=(T // tm,),
        in_specs=[row, pl.BlockSpec((tm, 128), lambda i: (i, BD_BLK)), one, one, hd, hd, hd, st],
        out_specs=[row, st, one, one],
        out_shape=[jax.ShapeDtypeStruct((T, wq), F32), jax.ShapeDtypeStruct((T, 128), BF),
                   jax.ShapeDtypeStruct((1, 128), F32), jax.ShapeDtypeStruct((1, 128), F32)],
        name=name, compiler_params=_cp())(conv, z_dn, alog_row, dtb_row, dqn, dkn, dvn, dbg)


def _dn_conv_bwd(dconv, z_dn, conv_w8, name):
    tm = 512
    wq = 3 * DNW
    last = T // tm - 1

    def body(dc_ref, dcn_ref, raw_ref, halo_ref, w_ref, draw_ref, dw_ref):
        i = pl.program_id(0)
        dc = dc_ref[...]
        nxt = jnp.where(i < last, dcn_ref[...], 0.0)
        cur = raw_ref[...]
        halo = jnp.where(i > 0, halo_ref[...], 0.0)
        w = w_ref[...]
        draw = jnp.zeros((tm, wq), F32)
        rows = []
        for j in range(4):
            draw = draw + _shift_up(dc, nxt, 3 - j) * w[j:j + 1, :]
            rows.append(jnp.sum(dc * _shift_down(cur, halo, 3 - j), axis=0, keepdims=True))
        draw_ref[...] = draw.astype(BF)
        part = jnp.concatenate(rows + [jnp.zeros((4, wq), F32)], axis=0)

        @pl.when(i == 0)
        def _():
            dw_ref[...] = part

        @pl.when(i > 0)
        def _():
            dw_ref[...] += part

    row = pl.BlockSpec((tm, wq), lambda i: (i, 0))
    return pl.pallas_call(
        body, grid=(T // tm,),
        in_specs=[row, pl.BlockSpec((8, wq), lambda i: (jnp.minimum((i + 1) * (tm // 8), T // 8 - 1), 0)),
                  row, pl.BlockSpec((8, wq), lambda i: (jnp.maximum(i * (tm // 8) - 1, 0), 0)),
                  pl.BlockSpec((8, wq), lambda i: (0, 0))],
        out_specs=[row, pl.BlockSpec((8, wq), lambda i: (0, 0))],
        out_shape=[jax.ShapeDtypeStruct((T, wq), BF), jax.ShapeDtypeStruct((8, wq), F32)],
        name=name, compiler_params=_cp())(dconv, dconv, z_dn, z_dn, conv_w8)


def _h3(a, b, dims=NN):
    return lax.dot_general(a, b, dims, precision=lax.Precision.HIGH, preferred_element_type=F32)


@jax.custom_vjp
def _inverse_given(a_mat, tinv):
    return tinv


def _inverse_given_fwd(a_mat, tinv):
    return tinv, tinv


def _inverse_given_bwd(tinv, g):
    return -_bdot_tn(tinv, _bdot_nt(g, tinv)), jnp.zeros_like(tinv)


_inverse_given.defvjp(_inverse_given_fwd, _inverse_given_bwd)


@jax.custom_vjp
def _h3_lo(a, b):
    return _h3(a, b)


def _h3_lo_fwd(a, b):
    return _h3(a, b), (a, b)


def _h3_lo_bwd(res, g):
    a, b = res
    gb = g.astype(BF)
    return _dg(gb, b.astype(BF), NT), _dg(a.astype(BF), gb, TN)


_h3_lo.defvjp(_h3_lo_fwd, _h3_lo_bwd)


def _prep_head(q, k, v, bgc, h):
    beta = _col(bgc, h)
    gc = jnp.broadcast_to(_col(bgc, DN_H + h), (PAIR, 128))
    glast = jnp.broadcast_to(_col(bgc, 2 * DN_H + h), (PAIR, 128))
    ri = _iota((PAIR, PAIR), 0)
    ci = _iota((PAIR, PAIR), 1)
    same = (ri // CH) == (ci // CH)
    causal = same & (ci <= ri)
    strict = same & (ci < ri)
    eye = ri == ci
    gc_cols = _hdot(jnp.ones((PAIR, PAIR), F32), jnp.where(eye, gc, 0.0))
    decay = jnp.exp(jnp.where(causal, gc - gc_cols, NEG))
    egc = jnp.exp(gc)
    kb = k * beta
    a_mat = jnp.where(strict, _bdot_nt(kb, k) * decay, 0.0)
    qs = q * (DN_HD ** -0.5)
    attn = jnp.where(causal, _bdot_nt(qs, k) * decay, 0.0)
    return a_mat, (v * beta, kb * egc, qs * egc, k * jnp.exp(glast - gc), attn, jnp.exp(glast))


PREP_DTYPES = (F32, BF, BF, BF, BF, F32)


def _prep_tail(tinv, ctx):
    vb, kbe, qg, kdec, attn, decb = ctx
    outs = (_h3_lo(tinv, vb), _h3_lo(tinv, kbe), qg, kdec, attn, decb)
    return tuple(o.astype(dt) for o, dt in zip(outs, PREP_DTYPES))


def _inverses(a_mats):
    eye = jnp.where(_iota((PAIR, PAIR), 0) == _iota((PAIR, PAIR), 1), 1.0, 0.0)
    ps = [-a for a in a_mats]
    tinvs = [eye + p for p in ps]
    for _ in range(5):
        ps = [_h3(p, p) for p in ps]
        tinvs = [t + _h3(t, p) for t, p in zip(tinvs, ps)]
    return tinvs


def _dn_prep_fwd(qkv, bg, name):
    rows = 1024
    hd = lambda off: pl.BlockSpec((rows, 128), lambda g, h: (g, off + h))
    out = pl.BlockSpec((rows, 128), lambda g, h: (g, h))

    def body(q_ref, k_ref, v_ref, bg_ref, *outs):
        h = pl.program_id(1)
        spans = [slice(PAIR * pr, PAIR * pr + PAIR) for pr in range(rows // PAIR)]
        heads = [_prep_head(q_ref[rs, :], k_ref[rs, :], v_ref[rs, :], bg_ref[rs, :], h) for rs in spans]
        tinvs = _inverses([a for a, _ in heads])
        for rs, tinv, (_, ctx) in zip(spans, tinvs, heads):
            for o_ref, val in zip(outs, _prep_tail(tinv, ctx) + (tinv,)):
                o_ref[rs, :] = val

    return pl.pallas_call(
        body, grid=(T // rows, DN_H),
        in_specs=[hd(0), hd(DN_H), hd(2 * DN_H), pl.BlockSpec((rows, 128), lambda g, h: (g, 0))],
        out_specs=[out] * 7, out_shape=[jax.ShapeDtypeStruct((T, DNW), dt) for dt in PREP_DTYPES + (F32,)],
        name=name, compiler_params=_cp())(qkv, qkv, qkv, bg)


def _dn_prep_bwd(qkv, bg, tinv, cts, name):
    rows = 1024
    hd = lambda off: pl.BlockSpec((rows, 128), lambda g, h: (g, off + h))
    out = pl.BlockSpec((rows, 128), lambda g, h: (g, h))
    st = pl.BlockSpec((rows, 128), lambda g, h: (g, 0))

    def body(q_ref, k_ref, v_ref, bg_ref, ti_ref, c0, c1, c2, c3, c4, c5, dq_ref, dk_ref, dv_ref, dbg_ref):
        h = pl.program_id(1)
        spans = [slice(PAIR * pr, PAIR * pr + PAIR) for pr in range(rows // PAIR)]
        tis = [ti_ref[rs, :] for rs in spans]

        def joint(qs, ks, vs, bs):
            heads = [_prep_head(q, k, v, b, h) for q, k, v, b in zip(qs, ks, vs, bs)]
            return [_prep_tail(_inverse_given(a, ti), ctx) for (a, ctx), ti in zip(heads, tis)]

        _, vjp = jax.vjp(joint, *[[r[rs, :] for rs in spans] for r in (q_ref, k_ref, v_ref, bg_ref)])
        dqs, dks, dvs, dbs = vjp([tuple(c[rs, :] for c in (c0, c1, c2, c3, c4, c5)) for rs in spans])
        for rs, dq, dk, dv in zip(spans, dqs, dks, dvs):
            dq_ref[rs, :] = dq
            dk_ref[rs, :] = dk
            dv_ref[rs, :] = dv
        dbg_all = jnp.concatenate(dbs, axis=0)

        @pl.when(h == 0)
        def _():
            dbg_ref[...] = dbg_all

        @pl.when(h > 0)
        def _():
            dbg_ref[...] += dbg_all

    return pl.pallas_call(
        body, grid=(T // rows, DN_H),
        in_specs=[hd(0), hd(DN_H), hd(2 * DN_H), st] + [out] * 7,
        out_specs=[out, out, out, st],
        out_shape=[jax.ShapeDtypeStruct((T, DNW), F32)] * 3 + [jax.ShapeDtypeStruct((T, 128), F32)],
        name=name, compiler_params=_cp())(qkv, qkv, qkv, bg, tinv, *cts)


def _step_math(ss, us, ws, qgs, kdecs, attns, decbs, sub):
    z = jnp.zeros((CH, 128), F32)
    vnews = [u - _bdot_nn(w, s) for u, w, s in zip(us, ws, ss)]
    vfulls = [jnp.concatenate([v, z] if sub == 0 else [z, v], axis=0) for v in vnews]
    os = [_bdot_nn(qg, s) + _bdot_nn(attn, vf) for qg, s, attn, vf in zip(qgs, ss, attns, vfulls)]
    decs = [jnp.sum(decb, axis=0, keepdims=True) * (1.0 / CH) for decb in decbs]
    return [s * dec + _bdot_tn(kdec, v) for s, dec, kdec, v in zip(ss, decs, kdecs, vnews)], os


SCAN_ROWS = 512


def _dn_scan_fwd(prep, name):
    nstep = T // SCAN_ROWS
    nch = SCAN_ROWS // CH
    row = pl.BlockSpec((SCAN_ROWS, DNW), lambda p: (p, 0))

    def body(u_ref, w_ref, qg_ref, kd_ref, at_ref, db_ref, o_ref, ss_ref, s_ref):
        @pl.when(pl.program_id(0) == 0)
        def _():
            s_ref[...] = jnp.zeros_like(s_ref)

        lanes = [slice(128 * h, 128 * h + 128) for h in range(DN_H)]
        states = [s_ref[h] for h in range(DN_H)]
        for ch in range(nch):
            rs = slice(CH * ch, CH * ch + CH)
            for h in range(DN_H):
                ss_ref[ch, h] = states[h]
            states, os = _step_math(states, *[[r[rs, ls] for ls in lanes]
                                              for r in (u_ref, w_ref, qg_ref, kd_ref, at_ref, db_ref)], ch % 2)
            for ls, o in zip(lanes, os):
                o_ref[rs, ls] = o
        for h in range(DN_H):
            s_ref[h] = states[h]

    return pl.pallas_call(
        body, grid=(nstep,), in_specs=[row] * 6,
        out_specs=[row, pl.BlockSpec((nch, DN_H, 128, 128), lambda p: (p, 0, 0, 0))],
        out_shape=[jax.ShapeDtypeStruct((T, DNW), F32), jax.ShapeDtypeStruct((T // CH, DN_H, 128, 128), F32)],
        scratch_shapes=[pltpu.VMEM((DN_H, 128, 128), F32)],
        name=name, compiler_params=_cp())(*prep)


def _dn_scan_bwd(prep, states, do, name):
    nstep = T // SCAN_ROWS
    nch = SCAN_ROWS // CH
    row = pl.BlockSpec((SCAN_ROWS, DNW), lambda p: (nstep - 1 - p, 0))

    def body(u_ref, w_ref, qg_ref, kd_ref, at_ref, db_ref, ss_ref, do_ref, *rest):
        outs, ds_ref = rest[:6], rest[6]

        @pl.when(pl.program_id(0) == 0)
        def _():
            ds_ref[...] = jnp.zeros_like(ds_ref)

        lanes = [slice(128 * h, 128 * h + 128) for h in range(DN_H)]
        dss = [ds_ref[h] for h in range(DN_H)]
        for ch in reversed(range(nch)):
            rs = slice(CH * ch, CH * ch + CH)
            args = [[ss_ref[ch, h] for h in range(DN_H)]] + [
                [r[rs, ls] for ls in lanes] for r in (u_ref, w_ref, qg_ref, kd_ref, at_ref, db_ref)]
            _, vjp = jax.vjp(functools.partial(_step_math, sub=ch % 2), *args)
            cts = vjp((dss, [do_ref[rs, ls] for ls in lanes]))
            dss = cts[0]
            for o_ref, vals in zip(outs, cts[1:]):
                for ls, val in zip(lanes, vals):
                    o_ref[rs, ls] = val
        for h in range(DN_H):
            ds_ref[h] = dss[h]

    return pl.pallas_call(
        body, grid=(nstep,),
        in_specs=[row] * 6 + [pl.BlockSpec((nch, DN_H, 128, 128), lambda p: (nstep - 1 - p, 0, 0, 0)), row],
        out_specs=[row] * 6, out_shape=[jax.ShapeDtypeStruct((T, DNW), dt) for dt in PREP_DTYPES],
        scratch_shapes=[pltpu.VMEM((DN_H, 128, 128), F32)],
        name=name, compiler_params=_cp())(*prep, states, do)


def _post_math(o, gate, wrow):
    return o * lax.rsqrt(jnp.mean(o * o, axis=-1, keepdims=True) + NORM_EPS) * wrow * _silu(gate)


def _dn_post_fwd(o, z_dn, dn_norm, name):
    tm = 512
    row = pl.BlockSpec((tm, DNW), lambda i: (i, 0))

    def body(o_ref, g_ref, w_ref, y_ref):
        for h in range(DN_H):
            ls = slice(128 * h, 128 * h + 128)
            y_ref[:, ls] = _post_math(o_ref[:, ls], g_ref[:, ls], w_ref[...]).astype(BF)

    return pl.pallas_call(
        body, grid=(T // tm,),
        in_specs=[row, pl.BlockSpec((tm, DNW), lambda i: (i, 3)), pl.BlockSpec((1, 128), lambda i: (0, 0))],
        out_specs=row, out_shape=jax.ShapeDtypeStruct((T, DNW), BF),
        name=name, compiler_params=_cp())(o, z_dn, dn_norm)


def _dn_post_bwd(o, z_dn, dn_norm, dy, name):
    tm = 512
    row = pl.BlockSpec((tm, DNW), lambda i: (i, 0))
    one = pl.BlockSpec((1, 128), lambda i: (0, 0))

    def body(o_ref, g_ref, w_ref, dy_ref, do_ref, dg_ref, dw_ref):
        i = pl.program_id(0)
        dw = jnp.zeros((1, 128), F32)
        for h in range(DN_H):
            ls = slice(128 * h, 128 * h + 128)
            _, vjp = jax.vjp(_post_math, o_ref[:, ls], g_ref[:, ls], w_ref[...])
            do, dg, dwh = vjp(dy_ref[:, ls].astype(F32))
            do_ref[:, ls] = do
            dg_ref[:, ls] = dg.astype(BF)
            dw = dw + dwh

        @pl.when(i == 0)
        def _():
            dw_ref[...] = dw

        @pl.when(i > 0)
        def _():
            dw_ref[...] += dw

    return pl.pallas_call(
        body, grid=(T // tm,),
        in_specs=[row, pl.BlockSpec((tm, DNW), lambda i: (i, 3)), one, pl.BlockSpec((tm, DNW), lambda i: (i, 1))],
        out_specs=[row, row, one],
        out_shape=[jax.ShapeDtypeStruct((T, DNW), F32), jax.ShapeDtypeStruct((T, DNW), BF),
                   jax.ShapeDtypeStruct((1, 128), F32)],
        name=name, compiler_params=_cp())(o, z_dn, dn_norm, dy)


HBM = pl.BlockSpec(memory_space=pltpu.HBM)
SEM = pl.BlockSpec(memory_space=pltpu.SEMAPHORE)
EFFECT = pltpu.SideEffectType.DATAFLOW_SIDE_EFFECTING
N_PEER = N_DEV - 1


ALL_PEERS = (1, 2, 4, 3, 5, 6, 7)
FIRST_HOP = (1, 2, 4, 6)
FORWARDED = (2, 4, 6)


def _peers(x, y, c, ks=ALL_PEERS):
    return [(k, (x ^ (k >> 2), y ^ ((k >> 1) & 1), c ^ (k & 1))) for k in ks]


def _exchange_copy(ins, lands, ssems, rsems, scatter, t, k, pos, me):
    px, py, pc = pos
    src = ins[t].at[4 * px + 2 * py + pc] if scatter else ins[t]
    return pltpu.make_async_remote_copy(
        src_ref=src, dst_ref=lands[t].at[me], send_sem=ssems[t].at[k - 1], recv_sem=rsems[t].at[k - 1],
        device_id=pos, device_id_type=MESH_ID)


def _xstart(bufs, scatter, name, ks=ALL_PEERS):
    nt = len(bufs)
    lands = [lax.empty((N_DEV,) + tuple(b.shape[1:] if scatter else b.shape), b.dtype) for b in bufs]

    def body(*refs):
        ins, lnd = refs[:nt], refs[nt:2 * nt]
        ssems, rsems = refs[2 * nt:3 * nt], refs[3 * nt:4 * nt]
        token = refs[-1]
        x, y, c = lax.axis_index("x"), lax.axis_index("y"), lax.axis_index("c")
        me = 4 * x + 2 * y + c
        for t in range(nt):
            for k, pos in _peers(x, y, c, ks):
                _exchange_copy(ins, lnd, ssems, rsems, scatter, t, k, pos, me).start()
        token[...] = jnp.zeros_like(token)

    both = list(bufs) + lands
    res = pl.pallas_call(
        body, name=name,
        out_shape=[pltpu.SemaphoreType.DMA((N_PEER,))] * (2 * nt)
        + [pltpu.HBM(b.shape, b.dtype) for b in both] + [jax.ShapeDtypeStruct((8, 128), F32)],
        in_specs=[HBM] * (2 * nt),
        out_specs=[SEM] * (2 * nt) + [HBM] * (2 * nt) + [pl.BlockSpec(memory_space=pltpu.VMEM)],
        input_output_aliases={i: 2 * nt + i for i in range(2 * nt)},
        compiler_params=pltpu.CompilerParams(has_side_effects=EFFECT),
    )(*[pltpu.with_memory_space_constraint(b, pltpu.HBM) for b in both])
    return res[:nt], res[nt:2 * nt], res[2 * nt:3 * nt], res[3 * nt:4 * nt], res[-1][0, 0]


def _xwait(ssems, rsems, thrus, lands, scatter, after, name, ks=ALL_PEERS):
    nt = len(lands)

    def body(*refs):
        ins, lnd = refs[:nt], refs[nt:2 * nt]
        ss, rs = refs[2 * nt:3 * nt], refs[3 * nt:4 * nt]
        x, y, c = lax.axis_index("x"), lax.axis_index("y"), lax.axis_index("c")
        me = 4 * x + 2 * y + c
        for t in range(nt):
            for k, pos in _peers(x, y, c, ks):
                cp = _exchange_copy(ins, lnd, ss, rs, scatter, t, k, pos, me)
                cp.wait_send()
                cp.wait_recv()

    both = list(thrus) + list(lands)
    res = pl.pallas_call(
        body, name=name, out_shape=[pltpu.HBM(b.shape, b.dtype) for b in both],
        in_specs=[HBM] * (2 * nt) + [SEM] * (2 * nt) + [ANY], out_specs=[HBM] * (2 * nt),
        input_output_aliases={i: i for i in range(2 * nt)},
        compiler_params=pltpu.CompilerParams(has_side_effects=EFFECT),
    )(*both, *ssems, *rsems, after)
    return res[:nt], res[nt:]


def _forward_copy(lands, ssems, rsems, t, k, pos, sibling):
    px, py, pc = pos
    slot = lands[t].at[4 * px + 2 * py + pc]
    return pltpu.make_async_remote_copy(
        src_ref=slot, dst_ref=slot, send_sem=ssems[t].at[k - 1], recv_sem=rsems[t].at[k - 1],
        device_id=sibling, device_id_type=MESH_ID)


def _fstart(lands, name):
    nt = len(lands)

    def body(*refs):
        lnd = refs[:nt]
        ssems, rsems = refs[nt:2 * nt], refs[2 * nt:3 * nt]
        token = refs[-1]
        x, y, c = lax.axis_index("x"), lax.axis_index("y"), lax.axis_index("c")
        for t in range(nt):
            for k, pos in _peers(x, y, c, FORWARDED):
                _forward_copy(lnd, ssems, rsems, t, k, pos, (x, y, c ^ 1)).start()
        token[...] = jnp.zeros_like(token)

    res = pl.pallas_call(
        body, name=name,
        out_shape=[pltpu.SemaphoreType.DMA((N_PEER,))] * (2 * nt)
        + [pltpu.HBM(b.shape, b.dtype) for b in lands] + [jax.ShapeDtypeStruct((8, 128), F32)],
        in_specs=[HBM] * nt,
        out_specs=[SEM] * (2 * nt) + [HBM] * nt + [pl.BlockSpec(memory_space=pltpu.VMEM)],
        input_output_aliases={i: 2 * nt + i for i in range(nt)},
        compiler_params=pltpu.CompilerParams(has_side_effects=EFFECT),
    )(*[pltpu.with_memory_space_constraint(b, pltpu.HBM) for b in lands])
    return res[:nt], res[nt:2 * nt], res[2 * nt:3 * nt], res[-1][0, 0]


def _fwait(ssems, rsems, lands, after, name):
    nt = len(lands)

    def body(*refs):
        lnd = refs[:nt]
        ss, rs = refs[nt:2 * nt], refs[2 * nt:3 * nt]
        x, y, c = lax.axis_index("x"), lax.axis_index("y"), lax.axis_index("c")
        for t in range(nt):
            for k, pos in _peers(x, y, c, FORWARDED):
                cp = _forward_copy(lnd, ss, rs, t, k, pos, (x, y, c ^ 1))
                cp.wait_send()
                cp.wait_recv()

    return pl.pallas_call(
        body, name=name, out_shape=[pltpu.HBM(b.shape, b.dtype) for b in lands],
        in_specs=[HBM] * nt + [SEM] * (2 * nt) + [ANY], out_specs=[HBM] * nt,
        input_output_aliases={i: i for i in range(nt)},
        compiler_params=pltpu.CompilerParams(has_side_effects=EFFECT),
    )(*lands, *ssems, *rsems, after)


def _adam(recv, sent, me, w, m, v, tr, name):
    _, r, c = w.shape
    c1 = np.float32(1.0 - ADAM_B1 ** ADAM_STEP)
    c2 = np.float32(1.0 - ADAM_B2 ** ADAM_STEP)

    def body(me_ref, r_ref, s_ref, w_ref, m_ref, v_ref, g_ref, d_ref, mo_ref, vo_ref):
        mine = me_ref[0]
        g = jnp.where(mine == 0, s_ref[0], r_ref[0]).astype(F32)
        for s in range(1, N_DEV):
            g = g + jnp.where(mine == s, s_ref[0], r_ref[s]).astype(F32)
        mn = ADAM_B1 * m_ref[0] + (1.0 - ADAM_B1) * g
        vn = ADAM_B2 * v_ref[0] + (1.0 - ADAM_B2) * (g * g)
        g_ref[0] = g
        mo_ref[0] = mn
        vo_ref[0] = vn
        d_ref[0] = -ADAM_LR * ((mn / c1) / (jnp.sqrt(vn / c2) + ADAM_EPS) + ADAM_WD * w_ref[0])

    one = pl.BlockSpec((1, tr, c), lambda i, me_ref: (0, i, 0))
    return pl.pallas_call(
        body,
        grid_spec=pltpu.PrefetchScalarGridSpec(
            num_scalar_prefetch=1, grid=(r // tr,),
            in_specs=[pl.BlockSpec((N_DEV, tr, c), lambda i, me_ref: (0, i, 0)),
                      pl.BlockSpec((1, tr, c), lambda i, me_ref: (me_ref[0], i, 0)), one, one, one],
            out_specs=[one] * 4),
        out_shape=[jax.ShapeDtypeStruct((1, r, c), F32)] * 4,
        name=name, compiler_params=_cp())(me, recv, sent, w, m, v)


def _local_step(x, target, sp, need, ahead, emit):
    g = {}
    x0, h1 = x, _rms_fwd(x, sp["norm_ffn1"], "ffn1_norm")
    act1, saved1 = _ffn_up(h1, lambda kind, a: need("w" + kind + "1", a),
                           lambda kind, a: ahead("w" + kind + "1", a), "ffn1")
    x1, h2 = _mm_nn_resnorm([(act1, saved1[-1])], x0, 0.5, sp["norm_mix"], 512, "ffn1_down_norm",
                            after=ahead("win_a", act1))
    win_a, win_d = need("win_a", h2), need("win_d", h2)
    conv_w8, wout = need("conv_w8", h2), need("wout", h2)
    z_at = _mm_nn(h2, win_a, F32, 1024, 768, "mix_in_attn")
    z_dn = _mm_nn(h2, win_d, F32, 1024, 768, "mix_in_dn")

    conv, qkvn, bg = _dn_pre_fwd(z_dn, conv_w8, sp["alog_row"], sp["dtb_row"], "dn_pre")
    *prep, tinv = _dn_prep_fwd(qkvn, bg, "dn_prep")
    o_dn, states = _dn_scan_fwd(prep, "dn_scan")
    dn_b = _dn_post_fwd(o_dn, z_dn, sp["dn_norm"], "dn_post")
    attn_f, attn_b, lse = _attn_fwd(z_at, "attn_fwd", after=ahead("wg2", dn_b))

    x2, h3 = _mm_nn_resnorm([(attn_b, wout[:AW]), (dn_b, wout[AW:])], x1, 1.0, sp["norm_ffn2"], 512,
                            "mix_out_norm")
    act2, saved2 = _ffn_up(h3, lambda kind, a: need("w" + kind + "2", a),
                           lambda kind, a: ahead("w" + kind + "2", a), "ffn2")

    dx3, dys3, g["norm_final"], loss8 = _down_loss_bwd(x2, act2, saved2[-1], sp["norm_final"], target,
                                                       "ffn2_down_loss")
    dx2, dx2b, g["norm_ffn2"] = _ffn_bwd(
        x2, sp["norm_ffn2"], saved2, dx3, dys3, 1.0,
        lambda kind, dw: emit(kind + "2", {"w" + kind + "2": dw}), "ffn2b")

    zero = emit("wout", {"wout": jnp.concatenate([_mm_tn(attn_b, dx2b, BF, 512, 1024, "mix_out_dw_a"),
                                                  _mm_tn(dn_b, dx2b, BF, 512, 1024, "mix_out_dw_d")], axis=0)})
    dmix = _mm_nt(dx2b, wout, F32, 1024, 1024, "mix_out_dx", after=zero)

    dq, dk, dv = _attn_bwd(z_at, dmix, attn_f, lse, "attn_bwd")

    do_dn, dgate, g["dn_norm"] = _dn_post_bwd(o_dn, z_dn, sp["dn_norm"], dmix, "dn_post_b")
    cts = _dn_scan_bwd(prep, states, do_dn, "dn_scan_b")
    dqn, dkn, dvn, dbg = _dn_prep_bwd(qkvn, bg, tinv, cts, "dn_prep_b")
    dconv, dbd, g["alog_row"], g["dtb_row"] = _dn_pre_bwd(
        conv, z_dn, sp["alog_row"], sp["dtb_row"], dqn, dkn, dvn, dbg, "dn_pre_b")
    draw, dconv_w8 = _dn_conv_bwd(dconv, z_dn, conv_w8, "dn_conv_b")

    pieces = ((dq, win_a, 0), (dk, win_a, 1), (dv, win_a, 2),
              (draw, win_d, 0), (dgate, win_d, 3 * DNW // DNW), (dbd, win_d, BD_BLK))
    zero = emit("win", {"w_in_parts": [_mm_tn(h2, p[0], BF, 512, min(p[0].shape[1], 768), "mix_in_dw%d" % i)
                                       for i, p in enumerate(pieces)], "conv_w8": dconv_w8})
    dx1, dys1, g["norm_mix"] = _mm_rms_bwd(pieces, NT, x1, sp["norm_mix"], dx2, 0.5, 512, "mix_in_dx_dnorm",
                                           after=zero)
    dx0, _, g["norm_ffn1"] = _ffn_bwd(
        x0, sp["norm_ffn1"], saved1, dx1, dys1, 1.0,
        lambda kind, dw: emit(kind + "1", {"w" + kind + "1": dw}), "ffn1b")
    return loss8[0, 0], dx0, g


def _cols_from_shards(gathered):
    n, r, c = gathered.shape
    return jnp.transpose(gathered, (1, 0, 2)).reshape(r, n * c)


def _shards_from_cols(full, dtype):
    r, nc = full.shape
    return jnp.transpose(full.reshape(r, N_DEV, nc // N_DEV), (1, 0, 2)).astype(dtype)


def _lane_row(vec4):
    return jnp.zeros((1, 128), F32).at[:, DN_H:2 * DN_H].set(vec4.astype(F32))


WEIGHT_SOURCES = {"wg1": "gate1", "wu1": "up1", "wd1": "down1", "win_a": "w_in", "win_d": "w_in",
                  "conv_w8": "conv_w", "wout": "w_out", "wg2": "gate2", "wu2": "up2", "wd2": "down2"}
TRANSPOSED = ("gate1", "up1", "gate2", "up2")


def _build_weights(name, gath):
    if name in ("wg1", "wu1", "wd1", "wg2", "wu2", "wd2"):
        return {name: gath[WEIGHT_SOURCES[name]].reshape(F, D)}
    if name in ("win_a", "win_d"):
        w_in = _cols_from_shards(gath["w_in"])
        c0 = 3 * AW + 3 * DNW
        win_d = jnp.concatenate([w_in[:, ZA:c0], w_in[:, c0 + 2 * DN_H:], w_in[:, c0:c0 + 2 * DN_H],
                                 jnp.zeros((D, ZP - IN_COLS), w_in.dtype)], axis=1)
        return {"win_a": w_in[:, :ZA], "win_d": win_d}
    if name == "wout":
        return {name: gath["w_out"].reshape(D, D)}
    conv = _cols_from_shards(gath["conv_w"])
    return {"conv_w8": jnp.concatenate([conv, jnp.zeros((4, 3 * DNW), F32)], axis=0)}


def _small_params(norm_ffn1, norm_mix, norm_ffn2, norm_final, a_log, dt_bias, dn_norm):
    return {"norm_ffn1": norm_ffn1, "norm_mix": norm_mix, "norm_ffn2": norm_ffn2,
            "norm_final": norm_final.reshape(1, D), "alog_row": _lane_row(a_log), "dtb_row": _lane_row(dt_bias),
            "dn_norm": dn_norm}


def _grad_slabs(group, g):
    if group[0] in "gud":
        return {WEIGHT_SOURCES["w" + group]: g["w" + group].reshape(N_DEV, F // N_DEV, D)}
    if group == "wout":
        return {"w_out": g["wout"].reshape(N_DEV, D // N_DEV, D)}
    aq, ak, av, dqkv, gate, bd = g["w_in_parts"]
    g_in = jnp.concatenate([aq, ak, av, dqkv, bd[:, :2 * DN_H], gate], axis=1)
    return {"w_in": _shards_from_cols(g_in, BF), "conv_w": _shards_from_cols(g["conv_w8"][:4], F32)}


SMALL_ROWS = 40


def _small_pack(norm_ffn1, norm_mix, norm_ffn2, norm_final, dn_norm, alog_row, dtb_row, loss=None):
    rows = [a.reshape(8, 128) for a in (norm_ffn1, norm_mix, norm_ffn2, norm_final)]
    loss_row = jnp.zeros((1, 128), F32) if loss is None else jnp.broadcast_to(loss.reshape(1, 1), (1, 128))
    rows += [dn_norm.reshape(1, 128), alog_row, dtb_row, loss_row, jnp.zeros((SMALL_ROWS - 36, 128), F32)]
    return jnp.concatenate(rows, axis=0)


def _small_unpack(pk):
    pk = pk[0]
    return (pk[0:8].reshape(1, D), pk[8:16].reshape(1, D), pk[16:24].reshape(1, D), pk[24:32].reshape(D),
            pk[32:33], pk[33:34, DN_H:2 * DN_H], pk[34:35, DN_H:2 * DN_H])


ADAM_TILE = {"gate1": 176, "up1": 176, "down1": 176, "gate2": 176, "up2": 176, "down2": 176,
             "w_in": 256, "w_out": 128, "conv_w": 4}
BIG = ("gate1", "up1", "down1", "w_in", "w_out", "gate2", "up2", "down2", "conv_w")


def kernel(x, norm_ffn1, ffn1_gate, ffn1_up, ffn1_down, norm_mix, w_in, conv_w, a_log, dt_bias, dn_norm, w_out, norm_ffn2, ffn2_gate, ffn2_up, ffn2_down, norm_final, loss_target, m_norm_ffn1, m_ffn1_gate, m_ffn1_up, m_ffn1_down, m_norm_mix, m_w_in, m_conv_w, m_a_log, m_dt_bias, m_dn_norm, m_w_out, m_norm_ffn2, m_ffn2_gate, m_ffn2_up, m_ffn2_down, m_norm_final, v_norm_ffn1, v_ffn1_gate, v_ffn1_up, v_ffn1_down, v_norm_mix, v_w_in, v_conv_w, v_a_log, v_dt_bias, v_dn_norm, v_w_out, v_norm_ffn2, v_ffn2_gate, v_ffn2_up, v_ffn2_down, v_norm_final):
    w = {"gate1": ffn1_gate, "up1": ffn1_up, "down1": ffn1_down, "w_in": w_in, "w_out": w_out,
         "gate2": ffn2_gate, "up2": ffn2_up, "down2": ffn2_down, "conv_w": conv_w}
    m = {"gate1": m_ffn1_gate, "up1": m_ffn1_up, "down1": m_ffn1_down, "w_in": m_w_in, "w_out": m_w_out,
         "gate2": m_ffn2_gate, "up2": m_ffn2_up, "down2": m_ffn2_down, "conv_w": m_conv_w}
    v = {"gate1": v_ffn1_gate, "up1": v_ffn1_up, "down1": v_ffn1_down, "w_in": v_w_in, "w_out": v_w_out,
         "gate2": v_ffn2_gate, "up2": v_ffn2_up, "down2": v_ffn2_down, "conv_w": v_conv_w}

    me = 4 * lax.axis_index("x") + 2 * lax.axis_index("y") + lax.axis_index("c")
    own_slot = lambda land, mine: lax.dynamic_update_index_in_dim(land, mine, me, 0)

    ag_order = ("gate1", "up1", "down1", "w_in", "conv_w", "w_out", "gate2", "up2", "down2")
    ag_groups = (("gate1",), ("up1",), ("down1",), ("w_in", "conv_w", "w_out"), ("gate2", "up2", "down2"))
    pos = {n: i for i, n in enumerate(ag_order)}

    def shard(n):
        if n == "conv_w":
            return w[n][0]
        return (w[n][0].T if n in TRANSPOSED else w[n][0]).astype(BF)

    ss, rs, thru, land, zero = _xstart([shard(n) for n in ag_order], False, "weights_start", FIRST_HOP)
    gath, built, on_its_way = {}, {}, {}
    group_of = lambda name: [i for i, grp in enumerate(ag_groups) if WEIGHT_SOURCES[name] in grp][0]

    def ahead(name, after):
        gi = group_of(name)
        if WEIGHT_SOURCES[name] in gath or gi in on_its_way:
            return None
        ids = [pos[n] for n in ag_groups[gi]]
        thrus, lands = _xwait([ss[i] for i in ids], [rs[i] for i in ids], [thru[i] for i in ids],
                              [land[i] for i in ids], False, after, "weights_wait%d" % gi, FIRST_HOP)
        fss, frs, lands, token = _fstart(lands, "weights_forward%d" % gi)
        on_its_way[gi] = (thrus, fss, frs, lands)
        return token

    def need(name, after):
        if name not in built:
            if WEIGHT_SOURCES[name] not in gath:
                gi = group_of(name)
                ahead(name, after)
                thrus, fss, frs, lands = on_its_way.pop(gi)
                lands = _fwait(fss, frs, lands, after, "weights_forward_wait%d" % gi)
                for n, t, l in zip(ag_groups[gi], thrus, lands):
                    gath[n] = own_slot(l, t)
            built.update(_build_weights(name, gath))
        return built[name]

    pending = []

    def emit(group, grads):
        slabs = grads if group == "small" else _grad_slabs(group, grads)
        names = list(slabs)
        started = _xstart([slabs[n] for n in names], True, "grads_start_" + group)
        pending.append((group, names) + started[:4])
        return started[4]

    sp = _small_params(norm_ffn1 + zero, norm_mix, norm_ffn2, norm_final, a_log, dt_bias, dn_norm)
    loss_part, dx, g = _local_step(x[0], loss_target[0], sp, need, ahead, emit)
    small = _small_pack(g["norm_ffn1"], g["norm_mix"], g["norm_ffn2"], g["norm_final"], g["dn_norm"],
                        g["alog_row"], g["dtb_row"], loss_part)
    emit("small", {"small": jnp.broadcast_to(small[None], (N_DEV, SMALL_ROWS, 128))})

    pack = lambda a: _small_pack(*a)[None]
    me1 = me.astype(jnp.int32).reshape(1)
    res, after = {}, dx
    for group, names, gss, grs, gthru, gland in pending:
        thrus, lands = _xwait(gss, grs, gthru, gland, True, after, "grads_wait_" + group)
        for n, sent, recv in zip(names, thrus, lands):
            if n == "small":
                res[n] = _adam(
                    recv, sent, me1,
                    pack((norm_ffn1, norm_mix, norm_ffn2, norm_final, dn_norm, _lane_row(a_log), _lane_row(dt_bias))),
                    pack((m_norm_ffn1, m_norm_mix, m_norm_ffn2, m_norm_final, m_dn_norm, _lane_row(m_a_log),
                          _lane_row(m_dt_bias))),
                    pack((v_norm_ffn1, v_norm_mix, v_norm_ffn2, v_norm_final, v_dn_norm, _lane_row(v_a_log),
                          _lane_row(v_dt_bias))),
                    SMALL_ROWS, "adam_small")
            elif n in TRANSPOSED:
                flip = lambda a: jnp.swapaxes(a, 1, 2)
                res[n] = [flip(o) for o in _adam(recv, sent, me1, flip(w[n]), flip(m[n]), flip(v[n]), ADAM_TILE[n],
                                                 "adam_" + n)]
            else:
                res[n] = _adam(recv, sent, me1, w[n], m[n], v[n], ADAM_TILE[n], "adam_" + n)
            after = res[n][0]
    res_s = res["small"]

    loss = res_s[0][0, 35, 0]
    outs = [loss, dx[None]]
    for k in range(4):
        n1, nm, n2, nf, dn, al, dt = _small_unpack(res_s[k])
        big = {n: res[n][k] for n in BIG}
        outs += [n1, big["gate1"], big["up1"], big["down1"], nm, big["w_in"], big["conv_w"], al, dt, dn,
                 big["w_out"], n2, big["gate2"], big["up2"], big["down2"], nf]
    return tuple(outs)
```

```python
import functools

import numpy as np
import jax
import jax.numpy as jnp
from jax import lax
from jax.experimental import pallas as pl
from jax.experimental.pallas import tpu as pltpu

T = 4096
D = 1024
F = 2816
N_DEV = 8
A_HEADS = 8
A_HD = 64
AW = A_HEADS * A_HD
DN_H = 4
DN_HD = 128
DNW = DN_H * DN_HD
CH = 64
PAIR = 2 * CH
ZA = 3 * AW
ZD = 3 * DNW + DNW + 256
ZP = ZA + ZD
BD_BLK = (3 * DNW + DNW) // 128
IN_COLS = 3592
PATTERNS = ((128, 1), (512, 4), (2048, 16))
NORM_EPS = 1e-6
L2_EPS = 1e-6
ADAM_LR, ADAM_B1, ADAM_B2, ADAM_EPS, ADAM_WD, ADAM_STEP = 0.001, 0.9, 0.999, 1e-08, 0.01, 10
VMEM_LIMIT = 56 * 1024 * 1024
NEG = -1e30

BF = jnp.bfloat16
F32 = jnp.float32
NN = (((1,), (0,)), ((), ()))
NT = (((1,), (1,)), ((), ()))
TN = (((0,), (0,)), ((), ()))
HI = lax.Precision.HIGHEST
MESH_ID = pl.DeviceIdType.MESH
ANY = pl.BlockSpec(memory_space=pl.ANY)


def _cp():
    return pltpu.CompilerParams(vmem_limit_bytes=VMEM_LIMIT)


def _dg(a, b, dims):
    return lax.dot_general(a, b, dims, preferred_element_type=F32)


def _hdot(a, b):
    return lax.dot_general(a, b, NN, precision=HI, preferred_element_type=F32)


def _make_bdot(dims, da_dims, da_swap, db_dims, db_swap):
    @jax.custom_vjp
    def f(a, b):
        return _dg(a.astype(BF), b.astype(BF), dims)

    def fwd(a, b):
        return f(a, b), (a, b)

    def bwd(res, g):
        a, b = res
        gb, ab, bb = g.astype(BF), a.astype(BF), b.astype(BF)
        da = _dg(bb, gb, da_dims) if da_swap else _dg(gb, bb, da_dims)
        db = _dg(gb, ab, db_dims) if db_swap else _dg(ab, gb, db_dims)
        return da.astype(a.dtype), db.astype(b.dtype)

    f.defvjp(fwd, bwd)
    return f


_bdot_nn = _make_bdot(NN, NT, False, TN, False)
_bdot_nt = _make_bdot(NT, NN, False, TN, True)
_bdot_tn = _make_bdot(TN, NT, True, NN, False)


def _iota(shape, dim):
    return lax.broadcasted_iota(jnp.int32, shape, dim)


def _col(x, idx):
    return jnp.sum(jnp.where(_iota(x.shape, 1) == idx, x, 0.0), axis=1, keepdims=True)


def _mm_nn(a, b, out_dtype, tm, tn, name):
    m, k = a.shape
    n = b.shape[1]

    def body(a_ref, b_ref, o_ref):
        o_ref[...] = _dg(a_ref[...], b_ref[...], NN).astype(out_dtype)

    return pl.pallas_call(
        body, grid=(m // tm, n // tn),
        in_specs=[pl.BlockSpec((tm, k), lambda i, j: (i, 0)), pl.BlockSpec((k, tn), lambda i, j: (0, j))],
        out_specs=pl.BlockSpec((tm, tn), lambda i, j: (i, j)),
        out_shape=jax.ShapeDtypeStruct((m, n), out_dtype), name=name, compiler_params=_cp())(a, b)


def _tie(body, after):
    if after is None:
        return body, [], []
    return (lambda tok_ref, *refs: body(*refs)), [ANY], [after.reshape(1, 1)]


def _mm_nt(a, b, out_dtype, tm, tb, name, after=None):
    m, c = a.shape
    kb = b.shape[0]

    def body(a_ref, b_ref, o_ref):
        o_ref[...] = _dg(a_ref[...], b_ref[...], NT).astype(out_dtype)

    body, tspec, tok = _tie(body, after)
    return pl.pallas_call(
        body, grid=(m // tm, kb // tb),
        in_specs=tspec + [pl.BlockSpec((tm, c), lambda i, j: (i, 0)), pl.BlockSpec((tb, c), lambda i, j: (j, 0))],
        out_specs=pl.BlockSpec((tm, tb), lambda i, j: (i, j)),
        out_shape=jax.ShapeDtypeStruct((m, kb), out_dtype), name=name, compiler_params=_cp())(*tok, a, b)


def _mm_tn(a, b, out_dtype, ta, tb, name, after=None):
    m, ka = a.shape
    nb = b.shape[1]

    def body(a_ref, b_ref, o_ref):
        o_ref[...] = _dg(a_ref[...], b_ref[...], TN).astype(out_dtype)

    body, tspec, tok = _tie(body, after)
    return pl.pallas_call(
        body, grid=(ka // ta, nb // tb),
        in_specs=tspec + [pl.BlockSpec((m, ta), lambda i, j: (0, i)), pl.BlockSpec((m, tb), lambda i, j: (0, j))],
        out_specs=pl.BlockSpec((ta, tb), lambda i, j: (i, j)),
        out_shape=jax.ShapeDtypeStruct((ka, nb), out_dtype), name=name, compiler_params=_cp())(*tok, a, b)


def _mm_nt_swiglu(h, wu_t, gate, tm, tb, name, after=None):
    m, c = h.shape
    kb = wu_t.shape[0]

    def body(h_ref, w_ref, g_ref, u_ref, a_ref):
        u = _dg(h_ref[...], w_ref[...], NT)
        g = g_ref[...].astype(F32)
        u_ref[...] = u.astype(BF)
        a_ref[...] = (g * jax.nn.sigmoid(g) * u).astype(BF)

    body, tspec, tok = _tie(body, after)
    tile = pl.BlockSpec((tm, tb), lambda i, j: (i, j))
    return pl.pallas_call(
        body, grid=(m // tm, kb // tb),
        in_specs=tspec + [pl.BlockSpec((tm, c), lambda i, j: (i, 0)), pl.BlockSpec((tb, c), lambda i, j: (j, 0)),
                          tile],
        out_specs=[tile, tile], out_shape=[jax.ShapeDtypeStruct((m, kb), BF)] * 2,
        name=name, compiler_params=_cp())(*tok, h, wu_t, gate)


def _mm_nt_gate_up_act(h, wg_t, wu_t, tm, tb, name):
    m, c = h.shape
    kb = wg_t.shape[0]

    def body(h_ref, wg_ref, wu_ref, g_ref, u_ref, a_ref):
        hv = h_ref[...]
        g = _dg(hv, wg_ref[...], NT)
        u = _dg(hv, wu_ref[...], NT)
        g_ref[...] = g.astype(BF)
        u_ref[...] = u.astype(BF)
        gb = g.astype(BF).astype(F32)
        a_ref[...] = (gb * jax.nn.sigmoid(gb) * u).astype(BF)

    tile = pl.BlockSpec((tm, tb), lambda i, j: (i, j))
    wspec = pl.BlockSpec((tb, c), lambda i, j: (j, 0))
    return pl.pallas_call(
        body, grid=(m // tm, kb // tb),
        in_specs=[pl.BlockSpec((tm, c), lambda i, j: (i, 0)), wspec, wspec],
        out_specs=[tile, tile, tile], out_shape=[jax.ShapeDtypeStruct((m, kb), BF)] * 3,
        name=name, compiler_params=_cp())(h, wg_t, wu_t)


def _mm_nt_dswiglu(dys, wd, gate, up, tm, tb, name, after=None):
    m, c = dys.shape
    kb = wd.shape[0]

    def body(d_ref, w_ref, g_ref, u_ref, dg_ref, du_ref):
        da = _dg(d_ref[...], w_ref[...], NT)
        g = g_ref[...].astype(F32)
        u = u_ref[...].astype(F32)
        s = jax.nn.sigmoid(g)
        dg_ref[...] = (da * u * (s * (1.0 + g * (1.0 - s)))).astype(BF)
        du_ref[...] = (da * (g * s)).astype(BF)

    body, tspec, tok = _tie(body, after)
    tile = pl.BlockSpec((tm, tb), lambda i, j: (i, j))
    return pl.pallas_call(
        body, grid=(m // tm, kb // tb),
        in_specs=tspec + [pl.BlockSpec((tm, c), lambda i, j: (i, 0)), pl.BlockSpec((tb, c), lambda i, j: (j, 0)),
                          tile, tile],
        out_specs=[tile, tile], out_shape=[jax.ShapeDtypeStruct((m, kb), BF)] * 2,
        name=name, compiler_params=_cp())(*tok, dys, wd, gate, up)


def _rms_fwd(x, gain, name):
    tm = 512
    row = pl.BlockSpec((tm, D), lambda i: (i, 0))

    def body(x_ref, g_ref, h_ref):
        xv = x_ref[...]
        r = lax.rsqrt(jnp.mean(xv * xv, axis=-1, keepdims=True) + NORM_EPS)
        h_ref[...] = (xv * r * g_ref[...]).astype(BF)

    return pl.pallas_call(
        body, grid=(T // tm,), in_specs=[row, pl.BlockSpec((1, D), lambda i: (0, 0))], out_specs=row,
        out_shape=jax.ShapeDtypeStruct((T, D), BF), name=name, compiler_params=_cp())(x, gain)


def _mm_rms_bwd(pairs, dims, x, gain, dres, alpha_out, tm, name, after=None):
    n = len(pairs)
    row = pl.BlockSpec((tm, D), lambda i: (i, 0))
    gspec = pl.BlockSpec((1, D), lambda i: (0, 0))

    def body(*refs):
        x_ref, g_ref, dres_ref, dx_ref, dxs_ref, dg_ref = refs[2 * n:]
        i = pl.program_id(0)
        dhv = _dg(refs[0][...], refs[n][...], dims)
        for p in range(1, n):
            dhv = dhv + _dg(refs[p][...], refs[n + p][...], dims)
        xv = x_ref[...]
        r = lax.rsqrt(jnp.mean(xv * xv, axis=-1, keepdims=True) + NORM_EPS)
        xh = xv * r
        part = jnp.sum(dhv * xh, axis=0, keepdims=True)

        @pl.when(i == 0)
        def _():
            dg_ref[...] = part

        @pl.when(i > 0)
        def _():
            dg_ref[...] += part

        dxh = dhv * g_ref[...]
        dx = r * (dxh - xh * jnp.mean(dxh * xh, axis=-1, keepdims=True)) + dres_ref[...]
        dx_ref[...] = dx
        dxs_ref[...] = (alpha_out * dx).astype(BF)

    body, tspec, tok = _tie(body, after)

    def bspec(a, b, kblk):
        if kblk is None:
            return pl.BlockSpec(b.shape, lambda i: (0, 0))
        return pl.BlockSpec((b.shape[0], a.shape[1]), lambda i: (0, kblk))

    return pl.pallas_call(
        body, grid=(T // tm,),
        in_specs=tspec + [pl.BlockSpec((tm, p[0].shape[1]), lambda i: (i, 0)) for p in pairs]
        + [bspec(p[0], p[1], p[2] if len(p) > 2 else None) for p in pairs] + [row, gspec, row],
        out_specs=[row, row, gspec],
        out_shape=[jax.ShapeDtypeStruct((T, D), F32), jax.ShapeDtypeStruct((T, D), BF),
                   jax.ShapeDtypeStruct((1, D), F32)],
        name=name, compiler_params=_cp())(*tok, *[p[0] for p in pairs], *[p[1] for p in pairs], x, gain, dres)


def _down_loss_bwd(x_prev, act, wd, gain, target, name):
    tm = 512
    row = pl.BlockSpec((tm, D), lambda i: (i, 0))
    gspec = pl.BlockSpec((1, D), lambda i: (0, 0))
    lspec = pl.BlockSpec((8, 128), lambda i: (0, 0))

    def body(x_ref, a_ref, w_ref, g_ref, t_ref, dx_ref, dxs_ref, dg_ref, loss_ref):
        i = pl.program_id(0)
        xv = x_ref[...] + 0.5 * _dg(a_ref[...], w_ref[...], NN)
        r = lax.rsqrt(jnp.mean(xv * xv, axis=-1, keepdims=True) + NORM_EPS)
        xh = xv * r
        diff = xh * g_ref[...] - t_ref[...]
        lpart = 0.5 * jnp.sum(jnp.mean(diff * diff, axis=-1, keepdims=True), axis=0, keepdims=True)
        dy = diff * (1.0 / D)
        part = jnp.sum(dy * xh, axis=0, keepdims=True)

        @pl.when(i == 0)
        def _():
            dg_ref[...] = part
            loss_ref[...] = jnp.broadcast_to(lpart, (8, 128))

        @pl.when(i > 0)
        def _():
            dg_ref[...] += part
            loss_ref[...] += jnp.broadcast_to(lpart, (8, 128))

        dxh = dy * g_ref[...]
        dx = r * (dxh - xh * jnp.mean(dxh * xh, axis=-1, keepdims=True))
        dx_ref[...] = dx
        dxs_ref[...] = (0.5 * dx).astype(BF)

    return pl.pallas_call(
        body, grid=(T // tm,),
        in_specs=[row, pl.BlockSpec((tm, F), lambda i: (i, 0)), pl.BlockSpec((F, D), lambda i: (0, 0)), gspec, row],
        out_specs=[row, row, gspec, lspec],
        out_shape=[jax.ShapeDtypeStruct((T, D), F32), jax.ShapeDtypeStruct((T, D), BF),
                   jax.ShapeDtypeStruct((1, D), F32), jax.ShapeDtypeStruct((8, 128), F32)],
        name=name, compiler_params=_cp())(x_prev, act, wd, gain, target)


def _mm_nn_resnorm(pairs, x_prev, alpha, gain, tm, name, after=None):
    m = x_prev.shape[0]
    n = len(pairs)

    def body(*refs):
        x_ref, g_ref, xo_ref, h_ref = refs[2 * n:]
        y = _dg(refs[0][...], refs[n][...], NN)
        for i in range(1, n):
            y = y + _dg(refs[i][...], refs[n + i][...], NN)
        xv = x_ref[...] + alpha * y
        xo_ref[...] = xv
        r = lax.rsqrt(jnp.mean(xv * xv, axis=-1, keepdims=True) + NORM_EPS)
        h_ref[...] = (xv * r * g_ref[...]).astype(BF)

    body, tspec, tok = _tie(body, after)
    row = pl.BlockSpec((tm, D), lambda i: (i, 0))
    return pl.pallas_call(
        body, grid=(m // tm,),
        in_specs=tspec + [pl.BlockSpec((tm, a.shape[1]), lambda i: (i, 0)) for a, _ in pairs]
        + [pl.BlockSpec(b.shape, lambda i: (0, 0)) for _, b in pairs] + [row, pl.BlockSpec((1, D), lambda i: (0, 0))],
        out_specs=[row, row],
        out_shape=[jax.ShapeDtypeStruct((m, D), F32), jax.ShapeDtypeStruct((m, D), BF)],
        name=name, compiler_params=_cp())(*tok, *[a for a, _ in pairs], *[b for _, b in pairs], x_prev, gain)


def _ffn_up(h, need, ahead, tag, one_call=False):
    wg = need("g", h)
    if one_call:
        wu, wd = need("u", h), need("d", h)
        gate, up, act = _mm_nt_gate_up_act(h, wg, wu, 1024, 1408, tag + "_gate_up_act")
        return act, (h, gate, up, act, wg, wu, wd)
    gate = _mm_nt(h, wg, BF, 1024, 1408, tag + "_gate")
    wu = need("u", gate)
    up, act = _mm_nt_swiglu(h, wu, gate, 1024, 1408, tag + "_up_act", after=ahead("d", wu))
    wd = need("d", up)
    return act, (h, gate, up, act, wg, wu, wd)


def _ffn_bwd(x_in, gain, saved, dxo, dys, alpha_out, emit, tag):
    h, gate, up, act, wg, wu, wd = saved
    sent = emit("d", _mm_tn(act, dys, BF, 256, 1024, tag + "_dwd"))
    dgate, dup = _mm_nt_dswiglu(dys, wd, gate, up, 1024, 1408, tag + "_dact", after=sent)
    sent = emit("g", _mm_tn(dgate, h, BF, 256, 1024, tag + "_dwg"))
    sent = emit("u", _mm_tn(dup, h, BF, 256, 1024, tag + "_dwu", after=sent))
    return _mm_rms_bwd([(dgate, wg), (dup, wu)], NN, x_in, gain, dxo, alpha_out, 512, tag + "_dh_dnorm",
                       after=sent)


SLAB = 2048
N_SLAB = T // SLAB
N_PAIR = A_HEADS // 2


def _pair_masks():
    lane = _iota((128, 128), 1)
    return lane < A_HD, lane >= A_HD


def _slope_table():
    h = 2 * jnp.arange(N_PAIR)[:, None] + jnp.minimum(jnp.arange(8), 1)[None, :]
    return jnp.broadcast_to((2.0 ** (-(h + 1).astype(F32)))[:, :, None], (N_PAIR, 8, 128))


def _rows(ref, start, d):
    if d == 1:
        return ref[pl.ds(start, 128), :]
    return ref[pl.ds(start, 128, stride=d), :]


def _put_rows(ref, start, d, val):
    if d == 1:
        ref[pl.ds(start, 128), :] = val
    else:
        ref[pl.ds(start, 128, stride=d), :] = val


def _units(d):
    return [(r, b, r + 128 * d * b) for r in range(d) for b in range(SLAB // (128 * d))]


def _biases(d, slopes, has_prev):
    qi = _iota((128, 256), 0)
    kj = _iota((128, 256), 1)
    steps = qi + 128 - kj
    in_band = (steps >= 0) & (steps <= 128)
    dist = (steps * d).astype(F32)
    base = [jnp.where(in_band, -(sl * dist), NEG) for sl in slopes]
    edge = [jnp.where(has_prev | (kj >= 128), b, NEG) for b in base]
    return base, edge


def _attn_fwd(z_at, name, after=None):
    def body(sl_ref, q_ref, kc_ref, kp_ref, vc_ref, vp_ref, of_ref, ob_ref, lse_ref, m_s, l_s, a_s):
        n = pl.program_id(1)
        lo, hi = _pair_masks()
        slopes = (sl_ref[0, 0:1, 0:1], sl_ref[0, 1:2, 0:1])

        def unit(d, start, b, first, carry, bias):
            q = (_rows(q_ref, start, d) * (A_HD ** -0.5)).astype(BF)
            kcur, vcur = _rows(kc_ref, start, d).astype(BF), _rows(vc_ref, start, d).astype(BF)
            if b > 0:
                kprev, vprev = carry
            else:
                pstart = start + SLAB - 128 * d
                kprev, vprev = _rows(kp_ref, pstart, d).astype(BF), _rows(vp_ref, pstart, d).astype(BF)
            kcat = jnp.concatenate([kprev, kcur], axis=0)
            vcat = jnp.concatenate([vprev, vcur], axis=0)
            ms, ls, pvs = [], [], []
            for e in range(2):
                qm = jnp.where(lo if e == 0 else hi, q, jnp.zeros_like(q))
                s = _dg(qm, kcat, NT) + bias[e]
                m = jnp.max(s, axis=1, keepdims=True)
                p = jnp.exp(s - m)
                ms.append(m)
                ls.append(jnp.sum(p, axis=1, keepdims=True))
                pvs.append(_dg(p.astype(BF), vcat, NN))
            m_u = jnp.where(lo, ms[0], ms[1])
            l_u = jnp.where(lo, ls[0], ls[1])
            a_u = jnp.where(lo, pvs[0], pvs[1])
            if first:
                m_n, l_n, a_n = m_u, l_u, a_u
            else:
                m_o = _rows(m_s, start, d)
                m_n = jnp.maximum(m_o, m_u)
                c_o = jnp.exp(m_o - m_n)
                c_u = jnp.exp(m_u - m_n)
                l_n = _rows(l_s, start, d) * c_o + l_u * c_u
                a_n = _rows(a_s, start, d) * c_o + a_u * c_u
            _put_rows(m_s, start, d, m_n)
            _put_rows(l_s, start, d, l_n)
            _put_rows(a_s, start, d, a_n)
            return kcur, vcur

        for pi, (_, d) in enumerate(PATTERNS):
            base, edge = _biases(d, slopes, n > 0)
            carry = None
            for r, b, start in _units(d):
                carry = unit(d, start, b, pi == 0, carry, edge if b == 0 else base)
        l = l_s[...]
        out = a_s[...] / l
        of_ref[...] = out
        ob_ref[...] = out.astype(BF)
        lse_ref[...] = m_s[...] + jnp.log(l)

    body, tspec, tok = _tie(body, after)
    cur = lambda c: pl.BlockSpec((SLAB, 128), lambda j, n: (n, c * N_PAIR + j))
    prv = lambda c: pl.BlockSpec((SLAB, 128), lambda j, n: (jnp.maximum(n - 1, 0), c * N_PAIR + j))
    out = pl.BlockSpec((SLAB, 128), lambda j, n: (n, j))
    return pl.pallas_call(
        body, grid=(N_PAIR, N_SLAB),
        in_specs=tspec + [pl.BlockSpec((1, 8, 128), lambda j, n: (j, 0, 0)), cur(0), cur(1), prv(1), cur(2), prv(2)],
        out_specs=[out, out, out],
        out_shape=[jax.ShapeDtypeStruct((T, AW), F32), jax.ShapeDtypeStruct((T, AW), BF),
                   jax.ShapeDtypeStruct((T, AW), F32)],
        scratch_shapes=[pltpu.VMEM((SLAB, 128), F32)] * 3,
        name=name, compiler_params=_cp())(*tok, _slope_table(), z_at, z_at, z_at, z_at, z_at)


def _attn_bwd(z_at, dout, out, lse, name):
    def body(sl_ref, q_ref, kc_ref, kp_ref, vc_ref, vp_ref, do_ref, o_ref, lse_ref, dq_ref, dk_ref, dv_ref,
             dq_s, dk_s, dv_s, ck_s, cv_s):
        step = pl.program_id(1)
        n = N_SLAB - 1 - step
        lo, hi = _pair_masks()
        slopes = (sl_ref[0, 0:1, 0:1], sl_ref[0, 1:2, 0:1])

        @pl.when(step == 0)
        def _():
            ck_s[...] = jnp.zeros_like(ck_s)
            cv_s[...] = jnp.zeros_like(cv_s)

        dk_s[...] = ck_s[...]
        dv_s[...] = cv_s[...]
        ck_s[...] = jnp.zeros_like(ck_s)
        cv_s[...] = jnp.zeros_like(cv_s)

        def add_rows(ref, start, d, val):
            _put_rows(ref, start, d, _rows(ref, start, d) + val)

        def unit(d, start, b, first, carry, bias):
            q = (_rows(q_ref, start, d) * (A_HD ** -0.5)).astype(BF)
            do_f = _rows(do_ref, start, d)
            do = do_f.astype(BF)
            prod = do_f * _rows(o_ref, start, d)
            lse_u = _rows(lse_ref, start, d)
            kcur, vcur = _rows(kc_ref, start, d).astype(BF), _rows(vc_ref, start, d).astype(BF)
            if b > 0:
                kprev, vprev = carry
            else:
                pstart = start + SLAB - 128 * d
                kprev, vprev = _rows(kp_ref, pstart, d).astype(BF), _rows(vp_ref, pstart, d).astype(BF)
            kcat = jnp.concatenate([kprev, kcur], axis=0)
            vcat = jnp.concatenate([vprev, vcur], axis=0)
            masks = (lo, hi)
            qms = [jnp.where(msk, q, jnp.zeros_like(q)) for msk in masks]
            doms = [jnp.where(msk, do, jnp.zeros_like(do)) for msk in masks]
            deltas = [jnp.sum(jnp.where(msk, prod, 0.0), axis=1, keepdims=True) for msk in masks]
            ss = [_dg(qm, kcat, NT) + bs for qm, bs in zip(qms, bias)]
            dps = [_dg(dom, vcat, NT) for dom in doms]
            ps = [jnp.exp(s - lse_u[:, 64 * e:64 * e + 1]) for e, s in enumerate(ss)]
            dss = [(p * (dp - delta)).astype(BF) for p, dp, delta in zip(ps, dps, deltas)]
            pbs = [p.astype(BF) for p in ps]
            dqs = [_dg(ds, kcat, NN) for ds in dss]
            dkc = _dg(dss[0], qms[0], TN) + _dg(dss[1], qms[1], TN)
            dvc = _dg(pbs[0], doms[0], TN) + _dg(pbs[1], doms[1], TN)
            dq_u = jnp.where(lo, dqs[0], dqs[1]) * (A_HD ** -0.5)
            if first:
                _put_rows(dq_s, start, d, dq_u)
            else:
                add_rows(dq_s, start, d, dq_u)
            add_rows(dk_s, start, d, dkc[128:])
            add_rows(dv_s, start, d, dvc[128:])
            if b > 0:
                add_rows(dk_s, start - 128 * d, d, dkc[:128])
                add_rows(dv_s, start - 128 * d, d, dvc[:128])
            else:
                pstart = start + SLAB - 128 * d
                add_rows(ck_s, pstart, d, dkc[:128])
                add_rows(cv_s, pstart, d, dvc[:128])
            return kcur, vcur

        for pi, (_, d) in enumerate(PATTERNS):
            base, edge = _biases(d, slopes, n > 0)
            carry = None
            for r, b, start in _units(d):
                carry = unit(d, start, b, pi == 0, carry, edge if b == 0 else base)
        dq_ref[...] = dq_s[...].astype(BF)
        dk_ref[...] = dk_s[...].astype(BF)
        dv_ref[...] = dv_s[...].astype(BF)

    rev = lambda n: N_SLAB - 1 - n
    cur = lambda c: pl.BlockSpec((SLAB, 128), lambda j, n: (rev(n), c * N_PAIR + j))
    prv = lambda c: pl.BlockSpec((SLAB, 128), lambda j, n: (jnp.maximum(rev(n) - 1, 0), c * N_PAIR + j))
    one = pl.BlockSpec((SLAB, 128), lambda j, n: (rev(n), j))
    return pl.pallas_call(
        body, grid=(N_PAIR, N_SLAB),
        in_specs=[pl.BlockSpec((1, 8, 128), lambda j, n: (j, 0, 0)), cur(0), cur(1), prv(1), cur(2), prv(2),
                  one, one, one],
        out_specs=[one, one, one], out_shape=[jax.ShapeDtypeStruct((T, AW), BF)] * 3,
        scratch_shapes=[pltpu.VMEM((SLAB, 128), F32)] * 5,
        name=name, compiler_params=_cp())(_slope_table(), z_at, z_at, z_at, z_at, z_at, dout, out, lse)


def _silu(x):
    return x * jax.nn.sigmoid(x)


def _qk_math(c):
    s = _silu(c)
    return s * lax.rsqrt(jnp.sum(s * s, axis=-1, keepdims=True) + L2_EPS)


def _softplus(x):
    return jnp.maximum(x, 0.0) + jnp.log(1.0 + jnp.exp(-jnp.abs(x)))


def _gate_math(bd, alog_row, dtb_row):
    rows = bd.shape[0]
    lane = _iota(bd.shape, 1)
    beta = jax.nn.sigmoid(bd)
    g = jnp.where((lane >= DN_H) & (lane < 2 * DN_H), -jnp.exp(alog_row) * _softplus(bd + dtb_row), 0.0)
    ri = _iota((rows, rows), 0)
    ci = _iota((rows, rows), 1)
    same = (ri // CH) == (ci // CH)
    li = _iota((128, 128), 0)
    lj = _iota((128, 128), 1)
    to_next_group = jnp.where((lj == li + DN_H) & (li >= DN_H) & (li < 2 * DN_H), 1.0, 0.0)
    gc = _hdot(jnp.where(same & (ci <= ri), 1.0, 0.0), g)
    glast = _hdot(_hdot(jnp.where(same, 1.0, 0.0), g), to_next_group)
    return jnp.where(lane < DN_H, beta, 0.0) + gc + glast


def _shift_down(cur, halo, s):
    if s == 0:
        return cur
    rolled = pltpu.roll(cur, s, 0)
    hr = pltpu.roll(halo, s, 0)
    head = jnp.where(_iota(hr.shape, 0) < s, hr, rolled[:8])
    return jnp.concatenate([head, rolled[8:]], axis=0)


def _shift_up(cur, halo, s):
    if s == 0:
        return cur
    rows = cur.shape[0]
    rolled = pltpu.roll(cur, rows - s, 0)
    hr = pltpu.roll(halo, 8 - s, 0)
    tail = jnp.where(_iota(hr.shape, 0) >= 8 - s, hr, rolled[rows - 8:])
    return jnp.concatenate([rolled[:rows - 8], tail], axis=0)


def _dn_pre_fwd(z_dn, conv_w8, alog_row, dtb_row, name):
    tm = 512
    wq = 3 * DNW

    def body(raw_ref, halo_ref, bd_ref, w_ref, al_ref, dt_ref, conv_ref, qkv_ref, bg_ref):
        i = pl.program_id(0)
        cur = raw_ref[...]
        halo = jnp.where(i > 0, halo_ref[...], 0.0)
        w = w_ref[...]
        conv = jnp.zeros((tm, wq), F32)
        for j in range(4):
            conv = conv + _shift_down(cur, halo, 3 - j) * w[j:j + 1, :]
        conv_ref[...] = conv
        for blk in range(3 * DN_H):
            sl = slice(128 * blk, 128 * blk + 128)
            c = conv[:, sl]
            qkv_ref[:, sl] = _qk_math(c) if blk < 2 * DN_H else _silu(c)
        for r0 in range(0, tm, PAIR):
            rs = slice(r0, r0 + PAIR)
            bg_ref[rs, :] = _gate_math(bd_ref[rs, :], al_ref[...], dt_ref[...])

    one = pl.BlockSpec((1, 128), lambda i: (0, 0))
    return pl.pallas_call(
        body, grid=(T // tm,),
        in_specs=[pl.BlockSpec((tm, wq), lambda i: (i, 0)),
                  pl.BlockSpec((8, wq), lambda i: (jnp.maximum(i * (tm // 8) - 1, 0), 0)),
                  pl.BlockSpec((tm, 128), lambda i: (i, BD_BLK)),
                  pl.BlockSpec((8, wq), lambda i: (0, 0)), one, one],
        out_specs=[pl.BlockSpec((tm, wq), lambda i: (i, 0)), pl.BlockSpec((tm, wq), lambda i: (i, 0)),
                   pl.BlockSpec((tm, 128), lambda i: (i, 0))],
        out_shape=[jax.ShapeDtypeStruct((T, wq), F32), jax.ShapeDtypeStruct((T, wq), F32),
                   jax.ShapeDtypeStruct((T, 128), F32)],
        name=name, compiler_params=_cp())(z_dn, z_dn, z_dn, conv_w8, alog_row, dtb_row)


def _dn_pre_bwd(conv, z_dn, alog_row, dtb_row, dqn, dkn, dvn, dbg, name):
    tm = 512
    wq = 3 * DNW

    def body(conv_ref, bd_ref, al_ref, dt_ref, dq_ref, dk_ref, dv_ref, dbg_ref,
             dconv_ref, dbd_ref, dal_ref, ddt_ref):
        i = pl.program_id(0)
        for blk in range(3 * DN_H):
            sl = slice(128 * blk, 128 * blk + 128)
            src = (dq_ref, dk_ref, dv_ref)[blk // DN_H]
            ct = src[:, 128 * (blk % DN_H):128 * (blk % DN_H) + 128]
            fn = _qk_math if blk < 2 * DN_H else _silu
            _, vjp = jax.vjp(fn, conv_ref[:, sl])
            dconv_ref[:, sl] = vjp(ct)[0]
        dal = jnp.zeros((1, 128), F32)
        ddt = jnp.zeros((1, 128), F32)
        for r0 in range(0, tm, PAIR):
            rs = slice(r0, r0 + PAIR)
            _, vjp = jax.vjp(_gate_math, bd_ref[rs, :], al_ref[...], dt_ref[...])
            dbd, dal_p, ddt_p = vjp(dbg_ref[rs, :])
            dbd_ref[rs, :] = dbd.astype(BF)
            dal, ddt = dal + dal_p, ddt + ddt_p

        @pl.when(i == 0)
        def _():
            dal_ref[...] = dal
            ddt_ref[...] = ddt

        @pl.when(i > 0)
        def _():
            dal_ref[...] += dal
            ddt_ref[...] += ddt

    one = pl.BlockSpec((1, 128), lambda i: (0, 0))
    row = pl.BlockSpec((tm, wq), lambda i: (i, 0))
    hd = pl.BlockSpec((tm, DNW), lambda i: (i, 0))
    st = pl.BlockSpec((tm, 128), lambda i: (i, 0))
    return pl.pallas_call(
        body, grid=(T // tm,),
        in_specs=[row, pl.BlockSpec((tm, 128), lambda i: (i, BD_BLK)), one, one, hd, hd, hd, st],
        out_specs=[row, st, one, one],
        out_shape=[jax.ShapeDtypeStruct((T, wq), F32), jax.ShapeDtypeStruct((T, 128), BF),
                   jax.ShapeDtypeStruct((1, 128), F32), jax.ShapeDtypeStruct((1, 128), F32)],
        name=name, compiler_params=_cp())(conv, z_dn, alog_row, dtb_row, dqn, dkn, dvn, dbg)


def _dn_conv_bwd(dconv, z_dn, conv_w8, name):
    tm = 512
    wq = 3 * DNW
    last = T // tm - 1

    def body(dc_ref, dcn_ref, raw_ref, halo_ref, w_ref, draw_ref, dw_ref):
        i = pl.program_id(0)
        dc = dc_ref[...]
        nxt = jnp.where(i < last, dcn_ref[...], 0.0)
        cur = raw_ref[...]
        halo = jnp.where(i > 0, halo_ref[...], 0.0)
        w = w_ref[...]
        draw = jnp.zeros((tm, wq), F32)
        rows = []
        for j in range(4):
            draw = draw + _shift_up(dc, nxt, 3 - j) * w[j:j + 1, :]
            rows.append(jnp.sum(dc * _shift_down(cur, halo, 3 - j), axis=0, keepdims=True))
        draw_ref[...] = draw.astype(BF)
        part = jnp.concatenate(rows + [jnp.zeros((4, wq), F32)], axis=0)

        @pl.when(i == 0)
        def _():
            dw_ref[...] = part

        @pl.when(i > 0)
        def _():
            dw_ref[...] += part

    row = pl.BlockSpec((tm, wq), lambda i: (i, 0))
    return pl.pallas_call(
        body, grid=(T // tm,),
        in_specs=[row, pl.BlockSpec((8, wq), lambda i: (jnp.minimum((i + 1) * (tm // 8), T // 8 - 1), 0)),
                  row, pl.BlockSpec((8, wq), lambda i: (jnp.maximum(i * (tm // 8) - 1, 0), 0)),
                  pl.BlockSpec((8, wq), lambda i: (0, 0))],
        out_specs=[row, pl.BlockSpec((8, wq), lambda i: (0, 0))],
        out_shape=[jax.ShapeDtypeStruct((T, wq), BF), jax.ShapeDtypeStruct((8, wq), F32)],
        name=name, compiler_params=_cp())(dconv, dconv, z_dn, z_dn, conv_w8)


def _h3(a, b, dims=NN):
    return lax.dot_general(a, b, dims, precision=lax.Precision.HIGH, preferred_element_type=F32)


@jax.custom_vjp
def _inverse_given(a_mat, tinv):
    return tinv


def _inverse_given_fwd(a_mat, tinv):
    return tinv, tinv


def _inverse_given_bwd(tinv, g):
    return -_bdot_tn(tinv, _bdot_nt(g, tinv)), jnp.zeros_like(tinv)


_inverse_given.defvjp(_inverse_given_fwd, _inverse_given_bwd)


@jax.custom_vjp
def _h3_lo(a, b):
    return _h3(a, b)


def _h3_lo_fwd(a, b):
    return _h3(a, b), (a, b)


def _h3_lo_bwd(res, g):
    a, b = res
    gb = g.astype(BF)
    return _dg(gb, b.astype(BF), NT), _dg(a.astype(BF), gb, TN)


_h3_lo.defvjp(_h3_lo_fwd, _h3_lo_bwd)


def _prep_head(q, k, v, bgc, h):
    beta = _col(bgc, h)
    gc = jnp.broadcast_to(_col(bgc, DN_H + h), (PAIR, 128))
    glast = jnp.broadcast_to(_col(bgc, 2 * DN_H + h), (PAIR, 128))
    ri = _iota((PAIR, PAIR), 0)
    ci = _iota((PAIR, PAIR), 1)
    same = (ri // CH) == (ci // CH)
    causal = same & (ci <= ri)
    strict = same & (ci < ri)
    eye = ri == ci
    gc_cols = _hdot(jnp.ones((PAIR, PAIR), F32), jnp.where(eye, gc, 0.0))
    decay = jnp.exp(jnp.where(causal, gc - gc_cols, NEG))
    egc = jnp.exp(gc)
    kb = k * beta
    a_mat = jnp.where(strict, _bdot_nt(kb, k) * decay, 0.0)
    qs = q * (DN_HD ** -0.5)
    attn = jnp.where(causal, _bdot_nt(qs, k) * decay, 0.0)
    return a_mat, (v * beta, kb * egc, qs * egc, k * jnp.exp(glast - gc), attn, jnp.exp(glast))


PREP_DTYPES = (F32, BF, BF, BF, BF, F32)


def _prep_tail(tinv, ctx):
    vb, kbe, qg, kdec, attn, decb = ctx
    outs = (_h3_lo(tinv, vb), _h3_lo(tinv, kbe), qg, kdec, attn, decb)
    return tuple(o.astype(dt) for o, dt in zip(outs, PREP_DTYPES))


def _inverses(a_mats):
    eye = jnp.where(_iota((PAIR, PAIR), 0) == _iota((PAIR, PAIR), 1), 1.0, 0.0)
    ps = [-a for a in a_mats]
    tinvs = [eye + p for p in ps]
    for _ in range(5):
        ps = [_h3(p, p) for p in ps]
        tinvs = [t + _h3(t, p) for t, p in zip(tinvs, ps)]
    return tinvs


def _dn_prep_fwd(qkv, bg, name):
    rows = 1024
    hd = lambda off: pl.BlockSpec((rows, 128), lambda g, h: (g, off + h))
    out = pl.BlockSpec((rows, 128), lambda g, h: (g, h))

    def body(q_ref, k_ref, v_ref, bg_ref, *outs):
        h = pl.program_id(1)
        spans = [slice(PAIR * pr, PAIR * pr + PAIR) for pr in range(rows // PAIR)]
        heads = [_prep_head(q_ref[rs, :], k_ref[rs, :], v_ref[rs, :], bg_ref[rs, :], h) for rs in spans]
        tinvs = _inverses([a for a, _ in heads])
        for rs, tinv, (_, ctx) in zip(spans, tinvs, heads):
            for o_ref, val in zip(outs, _prep_tail(tinv, ctx) + (tinv,)):
                o_ref[rs, :] = val

    return pl.pallas_call(
        body, grid=(T // rows, DN_H),
        in_specs=[hd(0), hd(DN_H), hd(2 * DN_H), pl.BlockSpec((rows, 128), lambda g, h: (g, 0))],
        out_specs=[out] * 7, out_shape=[jax.ShapeDtypeStruct((T, DNW), dt) for dt in PREP_DTYPES + (F32,)],
        name=name, compiler_params=_cp())(qkv, qkv, qkv, bg)


def _dn_prep_bwd(qkv, bg, tinv, cts, name):
    rows = 1024
    hd = lambda off: pl.BlockSpec((rows, 128), lambda g, h: (g, off + h))
    out = pl.BlockSpec((rows, 128), lambda g, h: (g, h))
    st = pl.BlockSpec((rows, 128), lambda g, h: (g, 0))

    def body(q_ref, k_ref, v_ref, bg_ref, ti_ref, c0, c1, c2, c3, c4, c5, dq_ref, dk_ref, dv_ref, dbg_ref):
        h = pl.program_id(1)
        spans = [slice(PAIR * pr, PAIR * pr + PAIR) for pr in range(rows // PAIR)]
        tis = [ti_ref[rs, :] for rs in spans]

        def joint(qs, ks, vs, bs):
            heads = [_prep_head(q, k, v, b, h) for q, k, v, b in zip(qs, ks, vs, bs)]
            return [_prep_tail(_inverse_given(a, ti), ctx) for (a, ctx), ti in zip(heads, tis)]

        _, vjp = jax.vjp(joint, *[[r[rs, :] for rs in spans] for r in (q_ref, k_ref, v_ref, bg_ref)])
        dqs, dks, dvs, dbs = vjp([tuple(c[rs, :] for c in (c0, c1, c2, c3, c4, c5)) for rs in spans])
        for rs, dq, dk, dv in zip(spans, dqs, dks, dvs):
            dq_ref[rs, :] = dq
            dk_ref[rs, :] = dk
            dv_ref[rs, :] = dv
        dbg_all = jnp.concatenate(dbs, axis=0)

        @pl.when(h == 0)
        def _():
            dbg_ref[...] = dbg_all

        @pl.when(h > 0)
        def _():
            dbg_ref[...] += dbg_all

    return pl.pallas_call(
        body, grid=(T // rows, DN_H),
        in_specs=[hd(0), hd(DN_H), hd(2 * DN_H), st] + [out] * 7,
        out_specs=[out, out, out, st],
        out_shape=[jax.ShapeDtypeStruct((T, DNW), F32)] * 3 + [jax.ShapeDtypeStruct((T, 128), F32)],
        name=name, compiler_params=_cp())(qkv, qkv, qkv, bg, tinv, *cts)


def _step_math(ss, us, ws, qgs, kdecs, attns, decbs, sub):
    z = jnp.zeros((CH, 128), F32)
    vnews = [u - _bdot_nn(w, s) for u, w, s in zip(us, ws, ss)]
    vfulls = [jnp.concatenate([v, z] if sub == 0 else [z, v], axis=0) for v in vnews]
    os = [_bdot_nn(qg, s) + _bdot_nn(attn, vf) for qg, s, attn, vf in zip(qgs, ss, attns, vfulls)]
    decs = [jnp.sum(decb, axis=0, keepdims=True) * (1.0 / CH) for decb in decbs]
    return [s * dec + _bdot_tn(kdec, v) for s, dec, kdec, v in zip(ss, decs, kdecs, vnews)], os


SCAN_ROWS = 256


def _dn_scan_fwd(prep, name):
    nstep = T // SCAN_ROWS
    nch = SCAN_ROWS // CH
    row = pl.BlockSpec((SCAN_ROWS, DNW), lambda p: (p, 0))

    def body(u_ref, w_ref, qg_ref, kd_ref, at_ref, db_ref, o_ref, ss_ref, s_ref):
        @pl.when(pl.program_id(0) == 0)
        def _():
            s_ref[...] = jnp.zeros_like(s_ref)

        lanes = [slice(128 * h, 128 * h + 128) for h in range(DN_H)]
        states = [s_ref[h] for h in range(DN_H)]
        for ch in range(nch):
            rs = slice(CH * ch, CH * ch + CH)
            for h in range(DN_H):
                ss_ref[ch, h] = states[h]
            states, os = _step_math(states, *[[r[rs, ls] for ls in lanes]
                                              for r in (u_ref, w_ref, qg_ref, kd_ref, at_ref, db_ref)], ch % 2)
            for ls, o in zip(lanes, os):
                o_ref[rs, ls] = o
        for h in range(DN_H):
            s_ref[h] = states[h]

    return pl.pallas_call(
        body, grid=(nstep,), in_specs=[row] * 6,
        out_specs=[row, pl.BlockSpec((nch, DN_H, 128, 128), lambda p: (p, 0, 0, 0))],
        out_shape=[jax.ShapeDtypeStruct((T, DNW), F32), jax.ShapeDtypeStruct((T // CH, DN_H, 128, 128), F32)],
        scratch_shapes=[pltpu.VMEM((DN_H, 128, 128), F32)],
        name=name, compiler_params=_cp())(*prep)


def _dn_scan_bwd(prep, states, do, name):
    nstep = T // SCAN_ROWS
    nch = SCAN_ROWS // CH
    row = pl.BlockSpec((SCAN_ROWS, DNW), lambda p: (nstep - 1 - p, 0))

    def body(u_ref, w_ref, qg_ref, kd_ref, at_ref, db_ref, ss_ref, do_ref, *rest):
        outs, ds_ref = rest[:6], rest[6]

        @pl.when(pl.program_id(0) == 0)
        def _():
            ds_ref[...] = jnp.zeros_like(ds_ref)

        lanes = [slice(128 * h, 128 * h + 128) for h in range(DN_H)]
        dss = [ds_ref[h] for h in range(DN_H)]
        for ch in reversed(range(nch)):
            rs = slice(CH * ch, CH * ch + CH)
            args = [[ss_ref[ch, h] for h in range(DN_H)]] + [
                [r[rs, ls] for ls in lanes] for r in (u_ref, w_ref, qg_ref, kd_ref, at_ref, db_ref)]
            _, vjp = jax.vjp(functools.partial(_step_math, sub=ch % 2), *args)
            cts = vjp((dss, [do_ref[rs, ls] for ls in lanes]))
            dss = cts[0]
            for o_ref, vals in zip(outs, cts[1:]):
                for ls, val in zip(lanes, vals):
                    o_ref[rs, ls] = val
        for h in range(DN_H):
            ds_ref[h] = dss[h]

    return pl.pallas_call(
        body, grid=(nstep,),
        in_specs=[row] * 6 + [pl.BlockSpec((nch, DN_H, 128, 128), lambda p: (nstep - 1 - p, 0, 0, 0)), row],
        out_specs=[row] * 6, out_shape=[jax.ShapeDtypeStruct((T, DNW), dt) for dt in PREP_DTYPES],
        scratch_shapes=[pltpu.VMEM((DN_H, 128, 128), F32)],
        name=name, compiler_params=_cp())(*prep, states, do)


def _post_math(o, gate, wrow):
    return o * lax.rsqrt(jnp.mean(o * o, axis=-1, keepdims=True) + NORM_EPS) * wrow * _silu(gate)


def _dn_post_fwd(o, z_dn, dn_norm, name):
    tm = 512
    row = pl.BlockSpec((tm, DNW), lambda i: (i, 0))

    def body(o_ref, g_ref, w_ref, y_ref):
        for h in range(DN_H):
            ls = slice(128 * h, 128 * h + 128)
            y_ref[:, ls] = _post_math(o_ref[:, ls], g_ref[:, ls], w_ref[...]).astype(BF)

    return pl.pallas_call(
        body, grid=(T // tm,),
        in_specs=[row, pl.BlockSpec((tm, DNW), lambda i: (i, 3)), pl.BlockSpec((1, 128), lambda i: (0, 0))],
        out_specs=row, out_shape=jax.ShapeDtypeStruct((T, DNW), BF),
        name=name, compiler_params=_cp())(o, z_dn, dn_norm)


def _dn_post_bwd(o, z_dn, dn_norm, dy, name):
    tm = 512
    row = pl.BlockSpec((tm, DNW), lambda i: (i, 0))
    one = pl.BlockSpec((1, 128), lambda i: (0, 0))

    def body(o_ref, g_ref, w_ref, dy_ref, do_ref, dg_ref, dw_ref):
        i = pl.program_id(0)
        dw = jnp.zeros((1, 128), F32)
        for h in range(DN_H):
            ls = slice(128 * h, 128 * h + 128)
            _, vjp = jax.vjp(_post_math, o_ref[:, ls], g_ref[:, ls], w_ref[...])
            do, dg, dwh = vjp(dy_ref[:, ls].astype(F32))
            do_ref[:, ls] = do
            dg_ref[:, ls] = dg.astype(BF)
            dw = dw + dwh

        @pl.when(i == 0)
        def _():
            dw_ref[...] = dw

        @pl.when(i > 0)
        def _():
            dw_ref[...] += dw

    return pl.pallas_call(
        body, grid=(T // tm,),
        in_specs=[row, pl.BlockSpec((tm, DNW), lambda i: (i, 3)), one, pl.BlockSpec((tm, DNW), lambda i: (i, 1))],
        out_specs=[row, row, one],
        out_shape=[jax.ShapeDtypeStruct((T, DNW), F32), jax.ShapeDtypeStruct((T, DNW), BF),
                   jax.ShapeDtypeStruct((1, 128), F32)],
        name=name, compiler_params=_cp())(o, z_dn, dn_norm, dy)


HBM = pl.BlockSpec(memory_space=pltpu.HBM)
SEM = pl.BlockSpec(memory_space=pltpu.SEMAPHORE)
EFFECT = pltpu.SideEffectType.DATAFLOW_SIDE_EFFECTING
N_PEER = N_DEV - 1


ALL_PEERS = (1, 2, 4, 3, 5, 6, 7)
FIRST_HOP = (1, 2, 4, 6)
FORWARDED = (2, 4, 6)


def _peers(x, y, c, ks=ALL_PEERS):
    return [(k, (x ^ (k >> 2), y ^ ((k >> 1) & 1), c ^ (k & 1))) for k in ks]


def _exchange_copy(ins, lands, ssems, rsems, scatter, t, k, pos, me):
    px, py, pc = pos
    src = ins[t].at[4 * px + 2 * py + pc] if scatter else ins[t]
    return pltpu.make_async_remote_copy(
        src_ref=src, dst_ref=lands[t].at[me], send_sem=ssems[t].at[k - 1], recv_sem=rsems[t].at[k - 1],
        device_id=pos, device_id_type=MESH_ID)


def _xstart(bufs, scatter, name, ks=ALL_PEERS):
    nt = len(bufs)
    lands = [lax.empty((N_DEV,) + tuple(b.shape[1:] if scatter else b.shape), b.dtype) for b in bufs]

    def body(*refs):
        ins, lnd = refs[:nt], refs[nt:2 * nt]
        ssems, rsems = refs[2 * nt:3 * nt], refs[3 * nt:4 * nt]
        token = refs[-1]
        x, y, c = lax.axis_index("x"), lax.axis_index("y"), lax.axis_index("c")
        me = 4 * x + 2 * y + c
        for t in range(nt):
            for k, pos in _peers(x, y, c, ks):
                _exchange_copy(ins, lnd, ssems, rsems, scatter, t, k, pos, me).start()
        token[...] = jnp.zeros_like(token)

    both = list(bufs) + lands
    res = pl.pallas_call(
        body, name=name,
        out_shape=[pltpu.SemaphoreType.DMA((N_PEER,))] * (2 * nt)
        + [pltpu.HBM(b.shape, b.dtype) for b in both] + [jax.ShapeDtypeStruct((8, 128), F32)],
        in_specs=[HBM] * (2 * nt),
        out_specs=[SEM] * (2 * nt) + [HBM] * (2 * nt) + [pl.BlockSpec(memory_space=pltpu.VMEM)],
        input_output_aliases={i: 2 * nt + i for i in range(2 * nt)},
        compiler_params=pltpu.CompilerParams(has_side_effects=EFFECT),
    )(*[pltpu.with_memory_space_constraint(b, pltpu.HBM) for b in both])
    return res[:nt], res[nt:2 * nt], res[2 * nt:3 * nt], res[3 * nt:4 * nt], res[-1][0, 0]


def _xwait(ssems, rsems, thrus, lands, scatter, after, name, ks=ALL_PEERS):
    nt = len(lands)

    def body(*refs):
        ins, lnd = refs[:nt], refs[nt:2 * nt]
        ss, rs = refs[2 * nt:3 * nt], refs[3 * nt:4 * nt]
        x, y, c = lax.axis_index("x"), lax.axis_index("y"), lax.axis_index("c")
        me = 4 * x + 2 * y + c
        for t in range(nt):
            for k, pos in _peers(x, y, c, ks):
                cp = _exchange_copy(ins, lnd, ss, rs, scatter, t, k, pos, me)
                cp.wait_send()
                cp.wait_recv()

    both = list(thrus) + list(lands)
    res = pl.pallas_call(
        body, name=name, out_shape=[pltpu.HBM(b.shape, b.dtype) for b in both],
        in_specs=[HBM] * (2 * nt) + [SEM] * (2 * nt) + [ANY], out_specs=[HBM] * (2 * nt),
        input_output_aliases={i: i for i in range(2 * nt)},
        compiler_params=pltpu.CompilerParams(has_side_effects=EFFECT),
    )(*both, *ssems, *rsems, after)
    return res[:nt], res[nt:]


def _forward_copy(lands, ssems, rsems, t, k, pos, sibling):
    px, py, pc = pos
    slot = lands[t].at[4 * px + 2 * py + pc]
    return pltpu.make_async_remote_copy(
        src_ref=slot, dst_ref=slot, send_sem=ssems[t].at[k - 1], recv_sem=rsems[t].at[k - 1],
        device_id=sibling, device_id_type=MESH_ID)


def _fstart(lands, name):
    nt = len(lands)

    def body(*refs):
        lnd = refs[:nt]
        ssems, rsems = refs[nt:2 * nt], refs[2 * nt:3 * nt]
        token = refs[-1]
        x, y, c = lax.axis_index("x"), lax.axis_index("y"), lax.axis_index("c")
        for t in range(nt):
            for k, pos in _peers(x, y, c, FORWARDED):
                _forward_copy(lnd, ssems, rsems, t, k, pos, (x, y, c ^ 1)).start()
        token[...] = jnp.zeros_like(token)

    res = pl.pallas_call(
        body, name=name,
        out_shape=[pltpu.SemaphoreType.DMA((N_PEER,))] * (2 * nt)
        + [pltpu.HBM(b.shape, b.dtype) for b in lands] + [jax.ShapeDtypeStruct((8, 128), F32)],
        in_specs=[HBM] * nt,
        out_specs=[SEM] * (2 * nt) + [HBM] * nt + [pl.BlockSpec(memory_space=pltpu.VMEM)],
        input_output_aliases={i: 2 * nt + i for i in range(nt)},
        compiler_params=pltpu.CompilerParams(has_side_effects=EFFECT),
    )(*[pltpu.with_memory_space_constraint(b, pltpu.HBM) for b in lands])
    return res[:nt], res[nt:2 * nt], res[2 * nt:3 * nt], res[-1][0, 0]


def _fwait(ssems, rsems, lands, after, name):
    nt = len(lands)

    def body(*refs):
        lnd = refs[:nt]
        ss, rs = refs[nt:2 * nt], refs[2 * nt:3 * nt]
        x, y, c = lax.axis_index("x"), lax.axis_index("y"), lax.axis_index("c")
        for t in range(nt):
            for k, pos in _peers(x, y, c, FORWARDED):
                cp = _forward_copy(lnd, ss, rs, t, k, pos, (x, y, c ^ 1))
                cp.wait_send()
                cp.wait_recv()

    return pl.pallas_call(
        body, name=name, out_shape=[pltpu.HBM(b.shape, b.dtype) for b in lands],
        in_specs=[HBM] * nt + [SEM] * (2 * nt) + [ANY], out_specs=[HBM] * nt,
        input_output_aliases={i: i for i in range(nt)},
        compiler_params=pltpu.CompilerParams(has_side_effects=EFFECT),
    )(*lands, *ssems, *rsems, after)


def _adam(recv, sent, me, w, m, v, tr, name):
    _, r, c = w.shape
    c1 = np.float32(1.0 - ADAM_B1 ** ADAM_STEP)
    c2 = np.float32(1.0 - ADAM_B2 ** ADAM_STEP)

    def body(me_ref, r_ref, s_ref, w_ref, m_ref, v_ref, g_ref, d_ref, mo_ref, vo_ref):
        mine = me_ref[0]
        g = jnp.where(mine == 0, s_ref[0], r_ref[0]).astype(F32)
        for s in range(1, N_DEV):
            g = g + jnp.where(mine == s, s_ref[0], r_ref[s]).astype(F32)
        mn = ADAM_B1 * m_ref[0] + (1.0 - ADAM_B1) * g
        vn = ADAM_B2 * v_ref[0] + (1.0 - ADAM_B2) * (g * g)
        g_ref[0] = g
        mo_ref[0] = mn
        vo_ref[0] = vn
        d_ref[0] = -ADAM_LR * ((mn / c1) / (jnp.sqrt(vn / c2) + ADAM_EPS) + ADAM_WD * w_ref[0])

    one = pl.BlockSpec((1, tr, c), lambda i, me_ref: (0, i, 0))
    return pl.pallas_call(
        body,
        grid_spec=pltpu.PrefetchScalarGridSpec(
            num_scalar_prefetch=1, grid=(r // tr,),
            in_specs=[pl.BlockSpec((N_DEV, tr, c), lambda i, me_ref: (0, i, 0)),
                      pl.BlockSpec((1, tr, c), lambda i, me_ref: (me_ref[0], i, 0)), one, one, one],
            out_specs=[one] * 4),
        out_shape=[jax.ShapeDtypeStruct((1, r, c), F32)] * 4,
        name=name, compiler_params=_cp())(me, recv, sent, w, m, v)


def _local_step(x, target, sp, need, ahead, emit):
    g = {}
    x0, h1 = x, _rms_fwd(x, sp["norm_ffn1"], "ffn1_norm")
    act1, saved1 = _ffn_up(h1, lambda kind, a: need("w" + kind + "1", a),
                           lambda kind, a: ahead("w" + kind + "1", a), "ffn1")
    x1, h2 = _mm_nn_resnorm([(act1, saved1[-1])], x0, 0.5, sp["norm_mix"], 512, "ffn1_down_norm",
                            after=ahead("win_a", act1))
    win_a, win_d = need("win_a", h2), need("win_d", h2)
    conv_w8, wout = need("conv_w8", h2), need("wout", h2)
    z_at = _mm_nn(h2, win_a, F32, 1024, 768, "mix_in_attn")
    z_dn = _mm_nn(h2, win_d, F32, 1024, 768, "mix_in_dn")

    conv, qkvn, bg = _dn_pre_fwd(z_dn, conv_w8, sp["alog_row"], sp["dtb_row"], "dn_pre")
    *prep, tinv = _dn_prep_fwd(qkvn, bg, "dn_prep")
    o_dn, states = _dn_scan_fwd(prep, "dn_scan")
    dn_b = _dn_post_fwd(o_dn, z_dn, sp["dn_norm"], "dn_post")
    attn_f, attn_b, lse = _attn_fwd(z_at, "attn_fwd", after=ahead("wg2", dn_b))

    x2, h3 = _mm_nn_resnorm([(attn_b, wout[:AW]), (dn_b, wout[AW:])], x1, 1.0, sp["norm_ffn2"], 512,
                            "mix_out_norm")
    act2, saved2 = _ffn_up(h3, lambda kind, a: need("w" + kind + "2", a),
                           lambda kind, a: ahead("w" + kind + "2", a), "ffn2", one_call=True)

    dx3, dys3, g["norm_final"], loss8 = _down_loss_bwd(x2, act2, saved2[-1], sp["norm_final"], target,
                                                       "ffn2_down_loss")
    dx2, dx2b, g["norm_ffn2"] = _ffn_bwd(
        x2, sp["norm_ffn2"], saved2, dx3, dys3, 1.0,
        lambda kind, dw: emit(kind + "2", {"w" + kind + "2": dw}), "ffn2b")

    zero = emit("wout", {"wout": jnp.concatenate([_mm_tn(attn_b, dx2b, BF, 512, 1024, "mix_out_dw_a"),
                                                  _mm_tn(dn_b, dx2b, BF, 512, 1024, "mix_out_dw_d")], axis=0)})
    dmix = _mm_nt(dx2b, wout, F32, 1024, 1024, "mix_out_dx", after=zero)

    dq, dk, dv = _attn_bwd(z_at, dmix, attn_f, lse, "attn_bwd")

    do_dn, dgate, g["dn_norm"] = _dn_post_bwd(o_dn, z_dn, sp["dn_norm"], dmix, "dn_post_b")
    cts = _dn_scan_bwd(prep, states, do_dn, "dn_scan_b")
    dqn, dkn, dvn, dbg = _dn_prep_bwd(qkvn, bg, tinv, cts, "dn_prep_b")
    dconv, dbd, g["alog_row"], g["dtb_row"] = _dn_pre_bwd(
        conv, z_dn, sp["alog_row"], sp["dtb_row"], dqn, dkn, dvn, dbg, "dn_pre_b")
    draw, dconv_w8 = _dn_conv_bwd(dconv, z_dn, conv_w8, "dn_conv_b")

    pieces = ((dq, win_a, 0), (dk, win_a, 1), (dv, win_a, 2),
              (draw, win_d, 0), (dgate, win_d, 3 * DNW // DNW), (dbd, win_d, BD_BLK))
    zero = emit("win", {"w_in_parts": [_mm_tn(h2, p[0], BF, 512, min(p[0].shape[1], 768), "mix_in_dw%d" % i)
                                       for i, p in enumerate(pieces)], "conv_w8": dconv_w8})
    dx1, dys1, g["norm_mix"] = _mm_rms_bwd(pieces, NT, x1, sp["norm_mix"], dx2, 0.5, 512, "mix_in_dx_dnorm",
                                           after=zero)
    dx0, _, g["norm_ffn1"] = _ffn_bwd(
        x0, sp["norm_ffn1"], saved1, dx1, dys1, 1.0,
        lambda kind, dw: emit(kind + "1", {"w" + kind + "1": dw}), "ffn1b")
    return loss8[0, 0], dx0, g


def _cols_from_shards(gathered):
    n, r, c = gathered.shape
    return jnp.transpose(gathered, (1, 0, 2)).reshape(r, n * c)


def _shards_from_cols(full, dtype):
    r, nc = full.shape
    return jnp.transpose(full.reshape(r, N_DEV, nc // N_DEV), (1, 0, 2)).astype(dtype)


def _lane_row(vec4):
    return jnp.zeros((1, 128), F32).at[:, DN_H:2 * DN_H].set(vec4.astype(F32))


WEIGHT_SOURCES = {"wg1": "gate1", "wu1": "up1", "wd1": "down1", "win_a": "w_in", "win_d": "w_in",
                  "conv_w8": "conv_w", "wout": "w_out", "wg2": "gate2", "wu2": "up2", "wd2": "down2"}
TRANSPOSED = ("gate1", "up1", "gate2", "up2")


def _build_weights(name, gath):
    if name in ("wg1", "wu1", "wd1", "wg2", "wu2", "wd2"):
        return {name: gath[WEIGHT_SOURCES[name]].reshape(F, D)}
    if name in ("win_a", "win_d"):
        w_in = _cols_from_shards(gath["w_in"])
        c0 = 3 * AW + 3 * DNW
        win_d = jnp.concatenate([w_in[:, ZA:c0], w_in[:, c0 + 2 * DN_H:], w_in[:, c0:c0 + 2 * DN_H],
                                 jnp.zeros((D, ZP - IN_COLS), w_in.dtype)], axis=1)
        return {"win_a": w_in[:, :ZA], "win_d": win_d}
    if name == "wout":
        return {name: gath["w_out"].reshape(D, D)}
    conv = _cols_from_shards(gath["conv_w"])
    return {"conv_w8": jnp.concatenate([conv, jnp.zeros((4, 3 * DNW), F32)], axis=0)}


def _small_params(norm_ffn1, norm_mix, norm_ffn2, norm_final, a_log, dt_bias, dn_norm):
    return {"norm_ffn1": norm_ffn1, "norm_mix": norm_mix, "norm_ffn2": norm_ffn2,
            "norm_final": norm_final.reshape(1, D), "alog_row": _lane_row(a_log), "dtb_row": _lane_row(dt_bias),
            "dn_norm": dn_norm}


def _grad_slabs(group, g):
    if group[0] in "gud":
        return {WEIGHT_SOURCES["w" + group]: g["w" + group].reshape(N_DEV, F // N_DEV, D)}
    if group == "wout":
        return {"w_out": g["wout"].reshape(N_DEV, D // N_DEV, D)}
    aq, ak, av, dqkv, gate, bd = g["w_in_parts"]
    g_in = jnp.concatenate([aq, ak, av, dqkv, bd[:, :2 * DN_H], gate], axis=1)
    return {"w_in": _shards_from_cols(g_in, BF), "conv_w": _shards_from_cols(g["conv_w8"][:4], F32)}


SMALL_ROWS = 40


def _small_pack(norm_ffn1, norm_mix, norm_ffn2, norm_final, dn_norm, alog_row, dtb_row, loss=None):
    rows = [a.reshape(8, 128) for a in (norm_ffn1, norm_mix, norm_ffn2, norm_final)]
    loss_row = jnp.zeros((1, 128), F32) if loss is None else jnp.broadcast_to(loss.reshape(1, 1), (1, 128))
    rows += [dn_norm.reshape(1, 128), alog_row, dtb_row, loss_row, jnp.zeros((SMALL_ROWS - 36, 128), F32)]
    return jnp.concatenate(rows, axis=0)


def _small_unpack(pk):
    pk = pk[0]
    return (pk[0:8].reshape(1, D), pk[8:16].reshape(1, D), pk[16:24].reshape(1, D), pk[24:32].reshape(D),
            pk[32:33], pk[33:34, DN_H:2 * DN_H], pk[34:35, DN_H:2 * DN_H])


ADAM_TILE = {"gate1": 176, "up1": 176, "down1": 176, "gate2": 176, "up2": 176, "down2": 176,
             "w_in": 256, "w_out": 128, "conv_w": 4}
BIG = ("gate1", "up1", "down1", "w_in", "w_out", "gate2", "up2", "down2", "conv_w")


def kernel(x, norm_ffn1, ffn1_gate, ffn1_up, ffn1_down, norm_mix, w_in, conv_w, a_log, dt_bias, dn_norm, w_out, norm_ffn2, ffn2_gate, ffn2_up, ffn2_down, norm_final, loss_target, m_norm_ffn1, m_ffn1_gate, m_ffn1_up, m_ffn1_down, m_norm_mix, m_w_in, m_conv_w, m_a_log, m_dt_bias, m_dn_norm, m_w_out, m_norm_ffn2, m_ffn2_gate, m_ffn2_up, m_ffn2_down, m_norm_final, v_norm_ffn1, v_ffn1_gate, v_ffn1_up, v_ffn1_down, v_norm_mix, v_w_in, v_conv_w, v_a_log, v_dt_bias, v_dn_norm, v_w_out, v_norm_ffn2, v_ffn2_gate, v_ffn2_up, v_ffn2_down, v_norm_final):
    w = {"gate1": ffn1_gate, "up1": ffn1_up, "down1": ffn1_down, "w_in": w_in, "w_out": w_out,
         "gate2": ffn2_gate, "up2": ffn2_up, "down2": ffn2_down, "conv_w": conv_w}
    m = {"gate1": m_ffn1_gate, "up1": m_ffn1_up, "down1": m_ffn1_down, "w_in": m_w_in, "w_out": m_w_out,
         "gate2": m_ffn2_gate, "up2": m_ffn2_up, "down2": m_ffn2_down, "conv_w": m_conv_w}
    v = {"gate1": v_ffn1_gate, "up1": v_ffn1_up, "down1": v_ffn1_down, "w_in": v_w_in, "w_out": v_w_out,
         "gate2": v_ffn2_gate, "up2": v_ffn2_up, "down2": v_ffn2_down, "conv_w": v_conv_w}

    me = 4 * lax.axis_index("x") + 2 * lax.axis_index("y") + lax.axis_index("c")
    own_slot = lambda land, mine: lax.dynamic_update_index_in_dim(land, mine, me, 0)

    ag_order = ("gate1", "up1", "down1", "w_in", "conv_w", "w_out", "gate2", "up2", "down2")
    ag_groups = (("gate1",), ("up1",), ("down1",), ("w_in", "conv_w", "w_out"), ("gate2", "up2", "down2"))
    pos = {n: i for i, n in enumerate(ag_order)}

    def shard(n):
        if n == "conv_w":
            return w[n][0]
        return (w[n][0].T if n in TRANSPOSED else w[n][0]).astype(BF)

    ss, rs, thru, land, zero = _xstart([shard(n) for n in ag_order], False, "weights_start", FIRST_HOP)
    gath, built, on_its_way = {}, {}, {}
    group_of = lambda name: [i for i, grp in enumerate(ag_groups) if WEIGHT_SOURCES[name] in grp][0]

    def ahead(name, after):
        gi = group_of(name)
        if WEIGHT_SOURCES[name] in gath or gi in on_its_way:
            return None
        ids = [pos[n] for n in ag_groups[gi]]
        thrus, lands = _xwait([ss[i] for i in ids], [rs[i] for i in ids], [thru[i] for i in ids],
                              [land[i] for i in ids], False, after, "weights_wait%d" % gi, FIRST_HOP)
        fss, frs, lands, token = _fstart(lands, "weights_forward%d" % gi)
        on_its_way[gi] = (thrus, fss, frs, lands)
        return token

    def need(name, after):
        if name not in built:
            if WEIGHT_SOURCES[name] not in gath:
                gi = group_of(name)
                ahead(name, after)
                thrus, fss, frs, lands = on_its_way.pop(gi)
                lands = _fwait(fss, frs, lands, after, "weights_forward_wait%d" % gi)
                for n, t, l in zip(ag_groups[gi], thrus, lands):
                    gath[n] = own_slot(l, t)
            built.update(_build_weights(name, gath))
        return built[name]

    pending = []

    def emit(group, grads):
        slabs = grads if group == "small" else _grad_slabs(group, grads)
        names = list(slabs)
        started = _xstart([slabs[n] for n in names], True, "grads_start_" + group)
        pending.append((group, names) + started[:4])
        return started[4]

    sp = _small_params(norm_ffn1 + zero, norm_mix, norm_ffn2, norm_final, a_log, dt_bias, dn_norm)
    loss_part, dx, g = _local_step(x[0], loss_target[0], sp, need, ahead, emit)
    small = _small_pack(g["norm_ffn1"], g["norm_mix"], g["norm_ffn2"], g["norm_final"], g["dn_norm"],
                        g["alog_row"], g["dtb_row"], loss_part)
    emit("small", {"small": jnp.broadcast_to(small[None], (N_DEV, SMALL_ROWS, 128))})

    pack = lambda a: _small_pack(*a)[None]
    me1 = me.astype(jnp.int32).reshape(1)
    res, after = {}, dx
    for group, names, gss, grs, gthru, gland in pending:
        thrus, lands = _xwait(gss, grs, gthru, gland, True, after, "grads_wait_" + group)
        for n, sent, recv in zip(names, thrus, lands):
            if n == "small":
                res[n] = _adam(
                    recv, sent, me1,
                    pack((norm_ffn1, norm_mix, norm_ffn2, norm_final, dn_norm, _lane_row(a_log), _lane_row(dt_bias))),
                    pack((m_norm_ffn1, m_norm_mix, m_norm_ffn2, m_norm_final, m_dn_norm, _lane_row(m_a_log),
                          _lane_row(m_dt_bias))),
                    pack((v_norm_ffn1, v_norm_mix, v_norm_ffn2, v_norm_final, v_dn_norm, _lane_row(v_a_log),
                          _lane_row(v_dt_bias))),
                    SMALL_ROWS, "adam_small")
            elif n in TRANSPOSED:
                flip = lambda a: jnp.swapaxes(a, 1, 2)
                res[n] = [flip(o) for o in _adam(recv, sent, me1, flip(w[n]), flip(m[n]), flip(v[n]), ADAM_TILE[n],
                                                 "adam_" + n)]
            else:
                res[n] = _adam(recv, sent, me1, w[n], m[n], v[n], ADAM_TILE[n], "adam_" + n)
            after = res[n][0]
    res_s = res["small"]

    loss = res_s[0][0, 35, 0]
    outs = [loss, dx[None]]
    for k in range(4):
        n1, nm, n2, nf, dn, al, dt = _small_unpack(res_s[k])
        big = {n: res[n][k] for n in BIG}
        outs += [n1, big["gate1"], big["up1"], big["down1"], nm, big["w_in"], big["conv_w"], al, dt, dn,
                 big["w_out"], n2, big["gate2"], big["up2"], big["down2"], nf]
    return tuple(outs)
```

```python
import functools

import numpy as np
import jax
import jax.numpy as jnp
from jax import lax
from jax.experimental import pallas as pl
from jax.experimental.pallas import tpu as pltpu

T = 4096
D = 1024
F = 2816
N_DEV = 8
A_HEADS = 8
A_HD = 64
AW = A_HEADS * A_HD
DN_H = 4
DN_HD = 128
DNW = DN_H * DN_HD
CH = 64
PAIR = 2 * CH
ZA = 3 * AW
ZD = 3 * DNW + DNW + 256
ZP = ZA + ZD
BD_BLK = (3 * DNW + DNW) // 128
IN_COLS = 3592
PATTERNS = ((128, 1), (512, 4), (2048, 16))
NORM_EPS = 1e-6
L2_EPS = 1e-6
ADAM_LR, ADAM_B1, ADAM_B2, ADAM_EPS, ADAM_WD, ADAM_STEP = 0.001, 0.9, 0.999, 1e-08, 0.01, 10
VMEM_LIMIT = 56 * 1024 * 1024
NEG = -1e30

BF = jnp.bfloat16
F32 = jnp.float32
NN = (((1,), (0,)), ((), ()))
NT = (((1,), (1,)), ((), ()))
TN = (((0,), (0,)), ((), ()))
HI = lax.Precision.HIGHEST
MESH_ID = pl.DeviceIdType.MESH
ANY = pl.BlockSpec(memory_space=pl.ANY)


def _cp():
    return pltpu.CompilerParams(vmem_limit_bytes=VMEM_LIMIT)


def _dg(a, b, dims):
    return lax.dot_general(a, b, dims, preferred_element_type=F32)


def _hdot(a, b):
    return lax.dot_general(a, b, NN, precision=HI, preferred_element_type=F32)


def _make_bdot(dims, da_dims, da_swap, db_dims, db_swap):
    @jax.custom_vjp
    def f(a, b):
        return _dg(a.astype(BF), b.astype(BF), dims)

    def fwd(a, b):
        return f(a, b), (a, b)

    def bwd(res, g):
        a, b = res
        gb, ab, bb = g.astype(BF), a.astype(BF), b.astype(BF)
        da = _dg(bb, gb, da_dims) if da_swap else _dg(gb, bb, da_dims)
        db = _dg(gb, ab, db_dims) if db_swap else _dg(ab, gb, db_dims)
        return da.astype(a.dtype), db.astype(b.dtype)

    f.defvjp(fwd, bwd)
    return f


_bdot_nn = _make_bdot(NN, NT, False, TN, False)
_bdot_nt = _make_bdot(NT, NN, False, TN, True)
_bdot_tn = _make_bdot(TN, NT, True, NN, False)


def _iota(shape, dim):
    return lax.broadcasted_iota(jnp.int32, shape, dim)


def _col(x, idx):
    return jnp.sum(jnp.where(_iota(x.shape, 1) == idx, x, 0.0), axis=1, keepdims=True)


def _mm_nn(a, b, out_dtype, tm, tn, name):
    m, k = a.shape
    n = b.shape[1]

    def body(a_ref, b_ref, o_ref):
        o_ref[...] = _dg(a_ref[...], b_ref[...], NN).astype(out_dtype)

    return pl.pallas_call(
        body, grid=(m // tm, n // tn),
        in_specs=[pl.BlockSpec((tm, k), lambda i, j: (i, 0)), pl.BlockSpec((k, tn), lambda i, j: (0, j))],
        out_specs=pl.BlockSpec((tm, tn), lambda i, j: (i, j)),
        out_shape=jax.ShapeDtypeStruct((m, n), out_dtype), name=name, compiler_params=_cp())(a, b)


def _tie(body, after):
    if after is None:
        return body, [], []
    return (lambda tok_ref, *refs: body(*refs)), [ANY], [after.reshape(1, 1)]


def _mm_nt(a, b, out_dtype, tm, tb, name, after=None):
    m, c = a.shape
    kb = b.shape[0]

    def body(a_ref, b_ref, o_ref):
        o_ref[...] = _dg(a_ref[...], b_ref[...], NT).astype(out_dtype)

    body, tspec, tok = _tie(body, after)
    return pl.pallas_call(
        body, grid=(m // tm, kb // tb),
        in_specs=tspec + [pl.BlockSpec((tm, c), lambda i, j: (i, 0)), pl.BlockSpec((tb, c), lambda i, j: (j, 0))],
        out_specs=pl.BlockSpec((tm, tb), lambda i, j: (i, j)),
        out_shape=jax.ShapeDtypeStruct((m, kb), out_dtype), name=name, compiler_params=_cp())(*tok, a, b)


def _mm_tn(a, b, out_dtype, ta, tb, name, after=None):
    m, ka = a.shape
    nb = b.shape[1]

    def body(a_ref, b_ref, o_ref):
        o_ref[...] = _dg(a_ref[...], b_ref[...], TN).astype(out_dtype)

    body, tspec, tok = _tie(body, after)
    return pl.pallas_call(
        body, grid=(ka // ta, nb // tb),
        in_specs=tspec + [pl.BlockSpec((m, ta), lambda i, j: (0, i)), pl.BlockSpec((m, tb), lambda i, j: (0, j))],
        out_specs=pl.BlockSpec((ta, tb), lambda i, j: (i, j)),
        out_shape=jax.ShapeDtypeStruct((ka, nb), out_dtype), name=name, compiler_params=_cp())(*tok, a, b)


def _mm_nt_swiglu(h, wu_t, gate, tm, tb, name, after=None):
    m, c = h.shape
    kb = wu_t.shape[0]

    def body(h_ref, w_ref, g_ref, u_ref, a_ref):
        u = _dg(h_ref[...], w_ref[...], NT)
        g = g_ref[...].astype(F32)
        u_ref[...] = u.astype(BF)
        a_ref[...] = (g * jax.nn.sigmoid(g) * u).astype(BF)

    body, tspec, tok = _tie(body, after)
    tile = pl.BlockSpec((tm, tb), lambda i, j: (i, j))
    return pl.pallas_call(
        body, grid=(m // tm, kb // tb),
        in_specs=tspec + [pl.BlockSpec((tm, c), lambda i, j: (i, 0)), pl.BlockSpec((tb, c), lambda i, j: (j, 0)),
                          tile],
        out_specs=[tile, tile], out_shape=[jax.ShapeDtypeStruct((m, kb), BF)] * 2,
        name=name, compiler_params=_cp())(*tok, h, wu_t, gate)


def _mm_nt_gate_up_act(h, wg_t, wu_t, tm, tb, name, after=None):
    m, c = h.shape
    kb = wg_t.shape[0]

    def body(h_ref, wg_ref, wu_ref, g_ref, u_ref, a_ref):
        hv = h_ref[...]
        g = _dg(hv, wg_ref[...], NT)
        u = _dg(hv, wu_ref[...], NT)
        g_ref[...] = g.astype(BF)
        u_ref[...] = u.astype(BF)
        gb = g.astype(BF).astype(F32)
        a_ref[...] = (gb * jax.nn.sigmoid(gb) * u).astype(BF)

    body, tspec, tok = _tie(body, after)
    tile = pl.BlockSpec((tm, tb), lambda i, j: (i, j))
    wspec = pl.BlockSpec((tb, c), lambda i, j: (j, 0))
    return pl.pallas_call(
        body, grid=(m // tm, kb // tb),
        in_specs=tspec + [pl.BlockSpec((tm, c), lambda i, j: (i, 0)), wspec, wspec],
        out_specs=[tile, tile, tile], out_shape=[jax.ShapeDtypeStruct((m, kb), BF)] * 3,
        name=name, compiler_params=_cp())(*tok, h, wg_t, wu_t)


def _mm_nt_dswiglu(dys, wd, gate, up, tm, tb, name, after=None):
    m, c = dys.shape
    kb = wd.shape[0]

    def body(d_ref, w_ref, g_ref, u_ref, dg_ref, du_ref):
        da = _dg(d_ref[...], w_ref[...], NT)
        g = g_ref[...].astype(F32)
        u = u_ref[...].astype(F32)
        s = jax.nn.sigmoid(g)
        dg_ref[...] = (da * u * (s * (1.0 + g * (1.0 - s)))).astype(BF)
        du_ref[...] = (da * (g * s)).astype(BF)

    body, tspec, tok = _tie(body, after)
    tile = pl.BlockSpec((tm, tb), lambda i, j: (i, j))
    return pl.pallas_call(
        body, grid=(m // tm, kb // tb),
        in_specs=tspec + [pl.BlockSpec((tm, c), lambda i, j: (i, 0)), pl.BlockSpec((tb, c), lambda i, j: (j, 0)),
                          tile, tile],
        out_specs=[tile, tile], out_shape=[jax.ShapeDtypeStruct((m, kb), BF)] * 2,
        name=name, compiler_params=_cp())(*tok, dys, wd, gate, up)


def _rms_fwd(x, gain, name):
    tm = 512
    row = pl.BlockSpec((tm, D), lambda i: (i, 0))

    def body(x_ref, g_ref, h_ref):
        xv = x_ref[...]
        r = lax.rsqrt(jnp.mean(xv * xv, axis=-1, keepdims=True) + NORM_EPS)
        h_ref[...] = (xv * r * g_ref[...]).astype(BF)

    return pl.pallas_call(
        body, grid=(T // tm,), in_specs=[row, pl.BlockSpec((1, D), lambda i: (0, 0))], out_specs=row,
        out_shape=jax.ShapeDtypeStruct((T, D), BF), name=name, compiler_params=_cp())(x, gain)


def _mm_rms_bwd(pairs, dims, x, gain, dres, alpha_out, tm, name, after=None):
    n = len(pairs)
    row = pl.BlockSpec((tm, D), lambda i: (i, 0))
    gspec = pl.BlockSpec((1, D), lambda i: (0, 0))

    def body(*refs):
        x_ref, g_ref, dres_ref, dx_ref, dxs_ref, dg_ref = refs[2 * n:]
        i = pl.program_id(0)
        dhv = _dg(refs[0][...], refs[n][...], dims)
        for p in range(1, n):
            dhv = dhv + _dg(refs[p][...], refs[n + p][...], dims)
        xv = x_ref[...]
        r = lax.rsqrt(jnp.mean(xv * xv, axis=-1, keepdims=True) + NORM_EPS)
        xh = xv * r
        part = jnp.sum(dhv * xh, axis=0, keepdims=True)

        @pl.when(i == 0)
        def _():
            dg_ref[...] = part

        @pl.when(i > 0)
        def _():
            dg_ref[...] += part

        dxh = dhv * g_ref[...]
        dx = r * (dxh - xh * jnp.mean(dxh * xh, axis=-1, keepdims=True)) + dres_ref[...]
        dx_ref[...] = dx
        dxs_ref[...] = (alpha_out * dx).astype(BF)

    body, tspec, tok = _tie(body, after)

    def bspec(a, b, kblk):
        if kblk is None:
            return pl.BlockSpec(b.shape, lambda i: (0, 0))
        return pl.BlockSpec((b.shape[0], a.shape[1]), lambda i: (0, kblk))

    return pl.pallas_call(
        body, grid=(T // tm,),
        in_specs=tspec + [pl.BlockSpec((tm, p[0].shape[1]), lambda i: (i, 0)) for p in pairs]
        + [bspec(p[0], p[1], p[2] if len(p) > 2 else None) for p in pairs] + [row, gspec, row],
        out_specs=[row, row, gspec],
        out_shape=[jax.ShapeDtypeStruct((T, D), F32), jax.ShapeDtypeStruct((T, D), BF),
                   jax.ShapeDtypeStruct((1, D), F32)],
        name=name, compiler_params=_cp())(*tok, *[p[0] for p in pairs], *[p[1] for p in pairs], x, gain, dres)


def _down_loss_bwd(x_prev, act, wd, gain, target, name):
    tm = 512
    row = pl.BlockSpec((tm, D), lambda i: (i, 0))
    gspec = pl.BlockSpec((1, D), lambda i: (0, 0))
    lspec = pl.BlockSpec((8, 128), lambda i: (0, 0))

    def body(x_ref, a_ref, w_ref, g_ref, t_ref, dx_ref, dxs_ref, dg_ref, loss_ref):
        i = pl.program_id(0)
        xv = x_ref[...] + 0.5 * _dg(a_ref[...], w_ref[...], NN)
        r = lax.rsqrt(jnp.mean(xv * xv, axis=-1, keepdims=True) + NORM_EPS)
        xh = xv * r
        diff = xh * g_ref[...] - t_ref[...]
        lpart = 0.5 * jnp.sum(jnp.mean(diff * diff, axis=-1, keepdims=True), axis=0, keepdims=True)
        dy = diff * (1.0 / D)
        part = jnp.sum(dy * xh, axis=0, keepdims=True)

        @pl.when(i == 0)
        def _():
            dg_ref[...] = part
            loss_ref[...] = jnp.broadcast_to(lpart, (8, 128))

        @pl.when(i > 0)
        def _():
            dg_ref[...] += part
            loss_ref[...] += jnp.broadcast_to(lpart, (8, 128))

        dxh = dy * g_ref[...]
        dx = r * (dxh - xh * jnp.mean(dxh * xh, axis=-1, keepdims=True))
        dx_ref[...] = dx
        dxs_ref[...] = (0.5 * dx).astype(BF)

    return pl.pallas_call(
        body, grid=(T // tm,),
        in_specs=[row, pl.BlockSpec((tm, F), lambda i: (i, 0)), pl.BlockSpec((F, D), lambda i: (0, 0)), gspec, row],
        out_specs=[row, row, gspec, lspec],
        out_shape=[jax.ShapeDtypeStruct((T, D), F32), jax.ShapeDtypeStruct((T, D), BF),
                   jax.ShapeDtypeStruct((1, D), F32), jax.ShapeDtypeStruct((8, 128), F32)],
        name=name, compiler_params=_cp())(x_prev, act, wd, gain, target)


def _mm_nn_resnorm(pairs, x_prev, alpha, gain, tm, name, after=None):
    m = x_prev.shape[0]
    n = len(pairs)

    def body(*refs):
        x_ref, g_ref, xo_ref, h_ref = refs[2 * n:]
        y = _dg(refs[0][...], refs[n][...], NN)
        for i in range(1, n):
            y = y + _dg(refs[i][...], refs[n + i][...], NN)
        xv = x_ref[...] + alpha * y
        xo_ref[...] = xv
        r = lax.rsqrt(jnp.mean(xv * xv, axis=-1, keepdims=True) + NORM_EPS)
        h_ref[...] = (xv * r * g_ref[...]).astype(BF)

    body, tspec, tok = _tie(body, after)
    row = pl.BlockSpec((tm, D), lambda i: (i, 0))
    return pl.pallas_call(
        body, grid=(m // tm,),
        in_specs=tspec + [pl.BlockSpec((tm, a.shape[1]), lambda i: (i, 0)) for a, _ in pairs]
        + [pl.BlockSpec(b.shape, lambda i: (0, 0)) for _, b in pairs] + [row, pl.BlockSpec((1, D), lambda i: (0, 0))],
        out_specs=[row, row],
        out_shape=[jax.ShapeDtypeStruct((m, D), F32), jax.ShapeDtypeStruct((m, D), BF)],
        name=name, compiler_params=_cp())(*tok, *[a for a, _ in pairs], *[b for _, b in pairs], x_prev, gain)


def _ffn_up(h, need, ahead, tag, one_call=False):
    wg = need("g", h)
    if one_call:
        wu = need("u", h)
        gate, up, act = _mm_nt_gate_up_act(h, wg, wu, 1024, 1408, tag + "_gate_up_act", after=ahead("d", wu))
        wd = need("d", up)
        return act, (h, gate, up, act, wg, wu, wd)
    gate = _mm_nt(h, wg, BF, 1024, 1408, tag + "_gate")
    wu = need("u", gate)
    up, act = _mm_nt_swiglu(h, wu, gate, 1024, 1408, tag + "_up_act", after=ahead("d", wu))
    wd = need("d", up)
    return act, (h, gate, up, act, wg, wu, wd)


def _ffn_bwd(x_in, gain, saved, dxo, dys, alpha_out, emit, tag):
    h, gate, up, act, wg, wu, wd = saved
    sent = emit("d", _mm_tn(act, dys, BF, 256, 1024, tag + "_dwd"))
    dgate, dup = _mm_nt_dswiglu(dys, wd, gate, up, 1024, 1408, tag + "_dact", after=sent)
    sent = emit("g", _mm_tn(dgate, h, BF, 256, 1024, tag + "_dwg"))
    sent = emit("u", _mm_tn(dup, h, BF, 256, 1024, tag + "_dwu", after=sent))
    return _mm_rms_bwd([(dgate, wg), (dup, wu)], NN, x_in, gain, dxo, alpha_out, 512, tag + "_dh_dnorm",
                       after=sent)


SLAB = 2048
N_SLAB = T // SLAB
N_PAIR = A_HEADS // 2


def _pair_masks():
    lane = _iota((128, 128), 1)
    return lane < A_HD, lane >= A_HD


def _slope_table():
    h = 2 * jnp.arange(N_PAIR)[:, None] + jnp.minimum(jnp.arange(8), 1)[None, :]
    return jnp.broadcast_to((2.0 ** (-(h + 1).astype(F32)))[:, :, None], (N_PAIR, 8, 128))


def _rows(ref, start, d):
    if d == 1:
        return ref[pl.ds(start, 128), :]
    return ref[pl.ds(start, 128, stride=d), :]


def _put_rows(ref, start, d, val):
    if d == 1:
        ref[pl.ds(start, 128), :] = val
    else:
        ref[pl.ds(start, 128, stride=d), :] = val


def _units(d):
    return [(r, b, r + 128 * d * b) for r in range(d) for b in range(SLAB // (128 * d))]


def _biases(d, slopes, has_prev):
    qi = _iota((128, 256), 0)
    kj = _iota((128, 256), 1)
    steps = qi + 128 - kj
    in_band = (steps >= 0) & (steps <= 128)
    dist = (steps * d).astype(F32)
    base = [jnp.where(in_band, -(sl * dist), NEG) for sl in slopes]
    edge = [jnp.where(has_prev | (kj >= 128), b, NEG) for b in base]
    return base, edge


def _attn_fwd(z_at, name, after=None):
    def body(sl_ref, q_ref, kc_ref, kp_ref, vc_ref, vp_ref, of_ref, ob_ref, lse_ref, m_s, l_s, a_s):
        n = pl.program_id(1)
        lo, hi = _pair_masks()
        slopes = (sl_ref[0, 0:1, 0:1], sl_ref[0, 1:2, 0:1])

        def unit(d, start, b, first, carry, bias):
            q = (_rows(q_ref, start, d) * (A_HD ** -0.5)).astype(BF)
            kcur, vcur = _rows(kc_ref, start, d).astype(BF), _rows(vc_ref, start, d).astype(BF)
            if b > 0:
                kprev, vprev = carry
            else:
                pstart = start + SLAB - 128 * d
                kprev, vprev = _rows(kp_ref, pstart, d).astype(BF), _rows(vp_ref, pstart, d).astype(BF)
            kcat = jnp.concatenate([kprev, kcur], axis=0)
            vcat = jnp.concatenate([vprev, vcur], axis=0)
            ms, ls, pvs = [], [], []
            for e in range(2):
                qm = jnp.where(lo if e == 0 else hi, q, jnp.zeros_like(q))
                s = _dg(qm, kcat, NT) + bias[e]
                m = jnp.max(s, axis=1, keepdims=True)
                p = jnp.exp(s - m)
                ms.append(m)
                ls.append(jnp.sum(p, axis=1, keepdims=True))
                pvs.append(_dg(p.astype(BF), vcat, NN))
            m_u = jnp.where(lo, ms[0], ms[1])
            l_u = jnp.where(lo, ls[0], ls[1])
            a_u = jnp.where(lo, pvs[0], pvs[1])
            if first:
                m_n, l_n, a_n = m_u, l_u, a_u
            else:
                m_o = _rows(m_s, start, d)
                m_n = jnp.maximum(m_o, m_u)
                c_o = jnp.exp(m_o - m_n)
                c_u = jnp.exp(m_u - m_n)
                l_n = _rows(l_s, start, d) * c_o + l_u * c_u
                a_n = _rows(a_s, start, d) * c_o + a_u * c_u
            _put_rows(m_s, start, d, m_n)
            _put_rows(l_s, start, d, l_n)
            _put_rows(a_s, start, d, a_n)
            return kcur, vcur

        for pi, (_, d) in enumerate(PATTERNS):
            base, edge = _biases(d, slopes, n > 0)
            carry = None
            for r, b, start in _units(d):
                carry = unit(d, start, b, pi == 0, carry, edge if b == 0 else base)
        l = l_s[...]
        out = a_s[...] / l
        of_ref[...] = out
        ob_ref[...] = out.astype(BF)
        lse_ref[...] = m_s[...] + jnp.log(l)

    body, tspec, tok = _tie(body, after)
    cur = lambda c: pl.BlockSpec((SLAB, 128), lambda j, n: (n, c * N_PAIR + j))
    prv = lambda c: pl.BlockSpec((SLAB, 128), lambda j, n: (jnp.maximum(n - 1, 0), c * N_PAIR + j))
    out = pl.BlockSpec((SLAB, 128), lambda j, n: (n, j))
    return pl.pallas_call(
        body, grid=(N_PAIR, N_SLAB),
        in_specs=tspec + [pl.BlockSpec((1, 8, 128), lambda j, n: (j, 0, 0)), cur(0), cur(1), prv(1), cur(2), prv(2)],
        out_specs=[out, out, out],
        out_shape=[jax.ShapeDtypeStruct((T, AW), F32), jax.ShapeDtypeStruct((T, AW), BF),
                   jax.ShapeDtypeStruct((T, AW), F32)],
        scratch_shapes=[pltpu.VMEM((SLAB, 128), F32)] * 3,
        name=name, compiler_params=_cp())(*tok, _slope_table(), z_at, z_at, z_at, z_at, z_at)


def _attn_bwd(z_at, dout, out, lse, name):
    def body(sl_ref, q_ref, kc_ref, kp_ref, vc_ref, vp_ref, do_ref, o_ref, lse_ref, dq_ref, dk_ref, dv_ref,
             dq_s, dk_s, dv_s, ck_s, cv_s):
        step = pl.program_id(1)
        n = N_SLAB - 1 - step
        lo, hi = _pair_masks()
        slopes = (sl_ref[0, 0:1, 0:1], sl_ref[0, 1:2, 0:1])

        @pl.when(step == 0)
        def _():
            ck_s[...] = jnp.zeros_like(ck_s)
            cv_s[...] = jnp.zeros_like(cv_s)

        dk_s[...] = ck_s[...]
        dv_s[...] = cv_s[...]
        ck_s[...] = jnp.zeros_like(ck_s)
        cv_s[...] = jnp.zeros_like(cv_s)

        def add_rows(ref, start, d, val):
            _put_rows(ref, start, d, _rows(ref, start, d) + val)

        def unit(d, start, b, first, carry, bias):
            q = (_rows(q_ref, start, d) * (A_HD ** -0.5)).astype(BF)
            do_f = _rows(do_ref, start, d)
            do = do_f.astype(BF)
            prod = do_f * _rows(o_ref, start, d)
            lse_u = _rows(lse_ref, start, d)
            kcur, vcur = _rows(kc_ref, start, d).astype(BF), _rows(vc_ref, start, d).astype(BF)
            if b > 0:
                kprev, vprev = carry
            else:
                pstart = start + SLAB - 128 * d
                kprev, vprev = _rows(kp_ref, pstart, d).astype(BF), _rows(vp_ref, pstart, d).astype(BF)
            kcat = jnp.concatenate([kprev, kcur], axis=0)
            vcat = jnp.concatenate([vprev, vcur], axis=0)
            masks = (lo, hi)
            qms = [jnp.where(msk, q, jnp.zeros_like(q)) for msk in masks]
            doms = [jnp.where(msk, do, jnp.zeros_like(do)) for msk in masks]
            deltas = [jnp.sum(jnp.where(msk, prod, 0.0), axis=1, keepdims=True) for msk in masks]
            ss = [_dg(qm, kcat, NT) + bs for qm, bs in zip(qms, bias)]
            dps = [_dg(dom, vcat, NT) for dom in doms]
            ps = [jnp.exp(s - lse_u[:, 64 * e:64 * e + 1]) for e, s in enumerate(ss)]
            dss = [(p * (dp - delta)).astype(BF) for p, dp, delta in zip(ps, dps, deltas)]
            pbs = [p.astype(BF) for p in ps]
            dqs = [_dg(ds, kcat, NN) for ds in dss]
            dkc = _dg(dss[0], qms[0], TN) + _dg(dss[1], qms[1], TN)
            dvc = _dg(pbs[0], doms[0], TN) + _dg(pbs[1], doms[1], TN)
            dq_u = jnp.where(lo, dqs[0], dqs[1]) * (A_HD ** -0.5)
            if first:
                _put_rows(dq_s, start, d, dq_u)
            else:
                add_rows(dq_s, start, d, dq_u)
            add_rows(dk_s, start, d, dkc[128:])
            add_rows(dv_s, start, d, dvc[128:])
            if b > 0:
                add_rows(dk_s, start - 128 * d, d, dkc[:128])
                add_rows(dv_s, start - 128 * d, d, dvc[:128])
            else:
                pstart = start + SLAB - 128 * d
                add_rows(ck_s, pstart, d, dkc[:128])
                add_rows(cv_s, pstart, d, dvc[:128])
            return kcur, vcur

        for pi, (_, d) in enumerate(PATTERNS):
            base, edge = _biases(d, slopes, n > 0)
            carry = None
            for r, b, start in _units(d):
                carry = unit(d, start, b, pi == 0, carry, edge if b == 0 else base)
        dq_ref[...] = dq_s[...].astype(BF)
        dk_ref[...] = dk_s[...].astype(BF)
        dv_ref[...] = dv_s[...].astype(BF)

    rev = lambda n: N_SLAB - 1 - n
    cur = lambda c: pl.BlockSpec((SLAB, 128), lambda j, n: (rev(n), c * N_PAIR + j))
    prv = lambda c: pl.BlockSpec((SLAB, 128), lambda j, n: (jnp.maximum(rev(n) - 1, 0), c * N_PAIR + j))
    one = pl.BlockSpec((SLAB, 128), lambda j, n: (rev(n), j))
    return pl.pallas_call(
        body, grid=(N_PAIR, N_SLAB),
        in_specs=[pl.BlockSpec((1, 8, 128), lambda j, n: (j, 0, 0)), cur(0), cur(1), prv(1), cur(2), prv(2),
                  one, one, one],
        out_specs=[one, one, one], out_shape=[jax.ShapeDtypeStruct((T, AW), BF)] * 3,
        scratch_shapes=[pltpu.VMEM((SLAB, 128), F32)] * 5,
        name=name, compiler_params=_cp())(_slope_table(), z_at, z_at, z_at, z_at, z_at, dout, out, lse)


def _silu(x):
    return x * jax.nn.sigmoid(x)


def _qk_math(c):
    s = _silu(c)
    return s * lax.rsqrt(jnp.sum(s * s, axis=-1, keepdims=True) + L2_EPS)


def _softplus(x):
    return jnp.maximum(x, 0.0) + jnp.log(1.0 + jnp.exp(-jnp.abs(x)))


def _gate_math(bd, alog_row, dtb_row):
    rows = bd.shape[0]
    lane = _iota(bd.shape, 1)
    beta = jax.nn.sigmoid(bd)
    g = jnp.where((lane >= DN_H) & (lane < 2 * DN_H), -jnp.exp(alog_row) * _softplus(bd + dtb_row), 0.0)
    ri = _iota((rows, rows), 0)
    ci = _iota((rows, rows), 1)
    same = (ri // CH) == (ci // CH)
    li = _iota((128, 128), 0)
    lj = _iota((128, 128), 1)
    to_next_group = jnp.where((lj == li + DN_H) & (li >= DN_H) & (li < 2 * DN_H), 1.0, 0.0)
    gc = _hdot(jnp.where(same & (ci <= ri), 1.0, 0.0), g)
    glast = _hdot(_hdot(jnp.where(same, 1.0, 0.0), g), to_next_group)
    return jnp.where(lane < DN_H, beta, 0.0) + gc + glast


def _shift_down(cur, halo, s):
    if s == 0:
        return cur
    rolled = pltpu.roll(cur, s, 0)
    hr = pltpu.roll(halo, s, 0)
    head = jnp.where(_iota(hr.shape, 0) < s, hr, rolled[:8])
    return jnp.concatenate([head, rolled[8:]], axis=0)


def _shift_up(cur, halo, s):
    if s == 0:
        return cur
    rows = cur.shape[0]
    rolled = pltpu.roll(cur, rows - s, 0)
    hr = pltpu.roll(halo, 8 - s, 0)
    tail = jnp.where(_iota(hr.shape, 0) >= 8 - s, hr, rolled[rows - 8:])
    return jnp.concatenate([rolled[:rows - 8], tail], axis=0)


def _dn_pre_fwd(z_dn, conv_w8, alog_row, dtb_row, name):
    tm = 512
    wq = 3 * DNW

    def body(raw_ref, halo_ref, bd_ref, w_ref, al_ref, dt_ref, conv_ref, qkv_ref, bg_ref):
        i = pl.program_id(0)
        cur = raw_ref[...]
        halo = jnp.where(i > 0, halo_ref[...], 0.0)
        w = w_ref[...]
        conv = jnp.zeros((tm, wq), F32)
        for j in range(4):
            conv = conv + _shift_down(cur, halo, 3 - j) * w[j:j + 1, :]
        conv_ref[...] = conv
        for blk in range(3 * DN_H):
            sl = slice(128 * blk, 128 * blk + 128)
            c = conv[:, sl]
            qkv_ref[:, sl] = _qk_math(c) if blk < 2 * DN_H else _silu(c)
        for r0 in range(0, tm, PAIR):
            rs = slice(r0, r0 + PAIR)
            bg_ref[rs, :] = _gate_math(bd_ref[rs, :], al_ref[...], dt_ref[...])

    one = pl.BlockSpec((1, 128), lambda i: (0, 0))
    return pl.pallas_call(
        body, grid=(T // tm,),
        in_specs=[pl.BlockSpec((tm, wq), lambda i: (i, 0)),
                  pl.BlockSpec((8, wq), lambda i: (jnp.maximum(i * (tm // 8) - 1, 0), 0)),
                  pl.BlockSpec((tm, 128), lambda i: (i, BD_BLK)),
                  pl.BlockSpec((8, wq), lambda i: (0, 0)), one, one],
        out_specs=[pl.BlockSpec((tm, wq), lambda i: (i, 0)), pl.BlockSpec((tm, wq), lambda i: (i, 0)),
                   pl.BlockSpec((tm, 128), lambda i: (i, 0))],
        out_shape=[jax.ShapeDtypeStruct((T, wq), F32), jax.ShapeDtypeStruct((T, wq), F32),
                   jax.ShapeDtypeStruct((T, 128), F32)],
        name=name, compiler_params=_cp())(z_dn, z_dn, z_dn, conv_w8, alog_row, dtb_row)


def _dn_pre_bwd(conv, z_dn, alog_row, dtb_row, dqn, dkn, dvn, dbg, name):
    tm = 512
    wq = 3 * DNW

    def body(conv_ref, bd_ref, al_ref, dt_ref, dq_ref, dk_ref, dv_ref, dbg_ref,
             dconv_ref, dbd_ref, dal_ref, ddt_ref):
        i = pl.program_id(0)
        for blk in range(3 * DN_H):
            sl = slice(128 * blk, 128 * blk + 128)
            src = (dq_ref, dk_ref, dv_ref)[blk // DN_H]
            ct = src[:, 128 * (blk % DN_H):128 * (blk % DN_H) + 128]
            fn = _qk_math if blk < 2 * DN_H else _silu
            _, vjp = jax.vjp(fn, conv_ref[:, sl])
            dconv_ref[:, sl] = vjp(ct)[0]
        dal = jnp.zeros((1, 128), F32)
        ddt = jnp.zeros((1, 128), F32)
        for r0 in range(0, tm, PAIR):
            rs = slice(r0, r0 + PAIR)
            _, vjp = jax.vjp(_gate_math, bd_ref[rs, :], al_ref[...], dt_ref[...])
            dbd, dal_p, ddt_p = vjp(dbg_ref[rs, :])
            dbd_ref[rs, :] = dbd.astype(BF)
            dal, ddt = dal + dal_p, ddt + ddt_p

        @pl.when(i == 0)
        def _():
            dal_ref[...] = dal
            ddt_ref[...] = ddt

        @pl.when(i > 0)
        def _():
            dal_ref[...] += dal
            ddt_ref[...] += ddt

    one = pl.BlockSpec((1, 128), lambda i: (0, 0))
    row = pl.BlockSpec((tm, wq), lambda i: (i, 0))
    hd = pl.BlockSpec((tm, DNW), lambda i: (i, 0))
    st = pl.BlockSpec((tm, 128), lambda i: (i, 0))
    return pl.pallas_call(
        body, grid=(T // tm,),
        in_specs=[row, pl.BlockSpec((tm, 128), lambda i: (i, BD_BLK)), one, one, hd, hd, hd, st],
        out_specs=[row, st, one, one],
        out_shape=[jax.ShapeDtypeStruct((T, wq), F32), jax.ShapeDtypeStruct((T, 128), BF),
                   jax.ShapeDtypeStruct((1, 128), F32), jax.ShapeDtypeStruct((1, 128), F32)],
        name=name, compiler_params=_cp())(conv, z_dn, alog_row, dtb_row, dqn, dkn, dvn, dbg)


def _dn_conv_bwd(dconv, z_dn, conv_w8, name):
    tm = 512
    wq = 3 * DNW
    last = T // tm - 1

    def body(dc_ref, dcn_ref, raw_ref, halo_ref, w_ref, draw_ref, dw_ref):
        i = pl.program_id(0)
        dc = dc_ref[...]
        nxt = jnp.where(i < last, dcn_ref[...], 0.0)
        cur = raw_ref[...]
        halo = jnp.where(i > 0, halo_ref[...], 0.0)
        w = w_ref[...]
        draw = jnp.zeros((tm, wq), F32)
        rows = []
        for j in range(4):
            draw = draw + _shift_up(dc, nxt, 3 - j) * w[j:j + 1, :]
            rows.append(jnp.sum(dc * _shift_down(cur, halo, 3 - j), axis=0, keepdims=True))
        draw_ref[...] = draw.astype(BF)
        part = jnp.concatenate(rows + [jnp.zeros((4, wq), F32)], axis=0)

        @pl.when(i == 0)
        def _():
            dw_ref[...] = part

        @pl.when(i > 0)
        def _():
            dw_ref[...] += part

    row = pl.BlockSpec((tm, wq), lambda i: (i, 0))
    return pl.pallas_call(
        body, grid=(T // tm,),
        in_specs=[row, pl.BlockSpec((8, wq), lambda i: (jnp.minimum((i + 1) * (tm // 8), T // 8 - 1), 0)),
                  row, pl.BlockSpec((8, wq), lambda i: (jnp.maximum(i * (tm // 8) - 1, 0), 0)),
                  pl.BlockSpec((8, wq), lambda i: (0, 0))],
        out_specs=[row, pl.BlockSpec((8, wq), lambda i: (0, 0))],
        out_shape=[jax.ShapeDtypeStruct((T, wq), BF), jax.ShapeDtypeStruct((8, wq), F32)],
        name=name, compiler_params=_cp())(dconv, dconv, z_dn, z_dn, conv_w8)


def _h3(a, b, dims=NN):
    return lax.dot_general(a, b, dims, precision=lax.Precision.HIGH, preferred_element_type=F32)


@jax.custom_vjp
def _inverse_given(a_mat, tinv):
    return tinv


def _inverse_given_fwd(a_mat, tinv):
    return tinv, tinv


def _inverse_given_bwd(tinv, g):
    return -_bdot_tn(tinv, _bdot_nt(g, tinv)), jnp.zeros_like(tinv)


_inverse_given.defvjp(_inverse_given_fwd, _inverse_given_bwd)


@jax.custom_vjp
def _h3_lo(a, b):
    return _h3(a, b)


def _h3_lo_fwd(a, b):
    return _h3(a, b), (a, b)


def _h3_lo_bwd(res, g):
    a, b = res
    gb = g.astype(BF)
    return _dg(gb, b.astype(BF), NT), _dg(a.astype(BF), gb, TN)


_h3_lo.defvjp(_h3_lo_fwd, _h3_lo_bwd)


def _prep_head(q, k, v, bgc, h):
    beta = _col(bgc, h)
    gc = jnp.broadcast_to(_col(bgc, DN_H + h), (PAIR, 128))
    glast = jnp.broadcast_to(_col(bgc, 2 * DN_H + h), (PAIR, 128))
    ri = _iota((PAIR, PAIR), 0)
    ci = _iota((PAIR, PAIR), 1)
    same = (ri // CH) == (ci // CH)
    causal = same & (ci <= ri)
    strict = same & (ci < ri)
    eye = ri == ci
    gc_cols = _hdot(jnp.ones((PAIR, PAIR), F32), jnp.where(eye, gc, 0.0))
    decay = jnp.exp(jnp.where(causal, gc - gc_cols, NEG))
    egc = jnp.exp(gc)
    kb = k * beta
    a_mat = jnp.where(strict, _bdot_nt(kb, k) * decay, 0.0)
    qs = q * (DN_HD ** -0.5)
    attn = jnp.where(causal, _bdot_nt(qs, k) * decay, 0.0)
    return a_mat, (v * beta, kb * egc, qs * egc, k * jnp.exp(glast - gc), attn, jnp.exp(glast))


PREP_DTYPES = (F32, BF, BF, BF, BF, F32)


def _prep_tail(tinv, ctx):
    vb, kbe, qg, kdec, attn, decb = ctx
    outs = (_h3_lo(tinv, vb), _h3_lo(tinv, kbe), qg, kdec, attn, decb)
    return tuple(o.astype(dt) for o, dt in zip(outs, PREP_DTYPES))


def _inverses(a_mats):
    eye = jnp.where(_iota((PAIR, PAIR), 0) == _iota((PAIR, PAIR), 1), 1.0, 0.0)
    ps = [-a for a in a_mats]
    tinvs = [eye + p for p in ps]
    for _ in range(5):
        ps = [_h3(p, p) for p in ps]
        tinvs = [t + _h3(t, p) for t, p in zip(tinvs, ps)]
    return tinvs


def _dn_prep_fwd(qkv, bg, name):
    rows = 1024
    hd = lambda off: pl.BlockSpec((rows, 128), lambda g, h: (g, off + h))
    out = pl.BlockSpec((rows, 128), lambda g, h: (g, h))

    def body(q_ref, k_ref, v_ref, bg_ref, *outs):
        h = pl.program_id(1)
        spans = [slice(PAIR * pr, PAIR * pr + PAIR) for pr in range(rows // PAIR)]
        heads = [_prep_head(q_ref[rs, :], k_ref[rs, :], v_ref[rs, :], bg_ref[rs, :], h) for rs in spans]
        tinvs = _inverses([a for a, _ in heads])
        for rs, tinv, (_, ctx) in zip(spans, tinvs, heads):
            for o_ref, val in zip(outs, _prep_tail(tinv, ctx) + (tinv,)):
                o_ref[rs, :] = val

    return pl.pallas_call(
        body, grid=(T // rows, DN_H),
        in_specs=[hd(0), hd(DN_H), hd(2 * DN_H), pl.BlockSpec((rows, 128), lambda g, h: (g, 0))],
        out_specs=[out] * 7, out_shape=[jax.ShapeDtypeStruct((T, DNW), dt) for dt in PREP_DTYPES + (F32,)],
        name=name, compiler_params=_cp())(qkv, qkv, qkv, bg)


def _dn_prep_bwd(qkv, bg, tinv, cts, name):
    rows = 1024
    hd = lambda off: pl.BlockSpec((rows, 128), lambda g, h: (g, off + h))
    out = pl.BlockSpec((rows, 128), lambda g, h: (g, h))
    st = pl.BlockSpec((rows, 128), lambda g, h: (g, 0))

    def body(q_ref, k_ref, v_ref, bg_ref, ti_ref, c0, c1, c2, c3, c4, c5, dq_ref, dk_ref, dv_ref, dbg_ref):
        h = pl.program_id(1)
        spans = [slice(PAIR * pr, PAIR * pr + PAIR) for pr in range(rows // PAIR)]
        tis = [ti_ref[rs, :] for rs in spans]

        def joint(qs, ks, vs, bs):
            heads = [_prep_head(q, k, v, b, h) for q, k, v, b in zip(qs, ks, vs, bs)]
            return [_prep_tail(_inverse_given(a, ti), ctx) for (a, ctx), ti in zip(heads, tis)]

        _, vjp = jax.vjp(joint, *[[r[rs, :] for rs in spans] for r in (q_ref, k_ref, v_ref, bg_ref)])
        dqs, dks, dvs, dbs = vjp([tuple(c[rs, :] for c in (c0, c1, c2, c3, c4, c5)) for rs in spans])
        for rs, dq, dk, dv in zip(spans, dqs, dks, dvs):
            dq_ref[rs, :] = dq
            dk_ref[rs, :] = dk
            dv_ref[rs, :] = dv
        dbg_all = jnp.concatenate(dbs, axis=0)

        @pl.when(h == 0)
        def _():
            dbg_ref[...] = dbg_all

        @pl.when(h > 0)
        def _():
            dbg_ref[...] += dbg_all

    return pl.pallas_call(
        body, grid=(T // rows, DN_H),
        in_specs=[hd(0), hd(DN_H), hd(2 * DN_H), st] + [out] * 7,
        out_specs=[out, out, out, st],
        out_shape=[jax.ShapeDtypeStruct((T, DNW), F32)] * 3 + [jax.ShapeDtypeStruct((T, 128), F32)],
        name=name, compiler_params=_cp())(qkv, qkv, qkv, bg, tinv, *cts)


def _step_math(ss, us, ws, qgs, kdecs, attns, decbs, sub):
    z = jnp.zeros((CH, 128), F32)
    vnews = [u - _bdot_nn(w, s) for u, w, s in zip(us, ws, ss)]
    vfulls = [jnp.concatenate([v, z] if sub == 0 else [z, v], axis=0) for v in vnews]
    os = [_bdot_nn(qg, s) + _bdot_nn(attn, vf) for qg, s, attn, vf in zip(qgs, ss, attns, vfulls)]
    decs = [jnp.sum(decb, axis=0, keepdims=True) * (1.0 / CH) for decb in decbs]
    return [s * dec + _bdot_tn(kdec, v) for s, dec, kdec, v in zip(ss, decs, kdecs, vnews)], os


SCAN_ROWS = 256


def _dn_scan_fwd(prep, name):
    nstep = T // SCAN_ROWS
    nch = SCAN_ROWS // CH
    row = pl.BlockSpec((SCAN_ROWS, DNW), lambda p: (p, 0))

    def body(u_ref, w_ref, qg_ref, kd_ref, at_ref, db_ref, o_ref, ss_ref, s_ref):
        @pl.when(pl.program_id(0) == 0)
        def _():
            s_ref[...] = jnp.zeros_like(s_ref)

        lanes = [slice(128 * h, 128 * h + 128) for h in range(DN_H)]
        states = [s_ref[h] for h in range(DN_H)]
        for ch in range(nch):
            rs = slice(CH * ch, CH * ch + CH)
            for h in range(DN_H):
                ss_ref[ch, h] = states[h]
            states, os = _step_math(states, *[[r[rs, ls] for ls in lanes]
                                              for r in (u_ref, w_ref, qg_ref, kd_ref, at_ref, db_ref)], ch % 2)
            for ls, o in zip(lanes, os):
                o_ref[rs, ls] = o
        for h in range(DN_H):
            s_ref[h] = states[h]

    return pl.pallas_call(
        body, grid=(nstep,), in_specs=[row] * 6,
        out_specs=[row, pl.BlockSpec((nch, DN_H, 128, 128), lambda p: (p, 0, 0, 0))],
        out_shape=[jax.ShapeDtypeStruct((T, DNW), F32), jax.ShapeDtypeStruct((T // CH, DN_H, 128, 128), F32)],
        scratch_shapes=[pltpu.VMEM((DN_H, 128, 128), F32)],
        name=name, compiler_params=_cp())(*prep)


def _dn_scan_bwd(prep, states, do, name):
    nstep = T // SCAN_ROWS
    nch = SCAN_ROWS // CH
    row = pl.BlockSpec((SCAN_ROWS, DNW), lambda p: (nstep - 1 - p, 0))

    def body(u_ref, w_ref, qg_ref, kd_ref, at_ref, db_ref, ss_ref, do_ref, *rest):
        outs, ds_ref = rest[:6], rest[6]

        @pl.when(pl.program_id(0) == 0)
        def _():
            ds_ref[...] = jnp.zeros_like(ds_ref)

        lanes = [slice(128 * h, 128 * h + 128) for h in range(DN_H)]
        dss = [ds_ref[h] for h in range(DN_H)]
        for ch in reversed(range(nch)):
            rs = slice(CH * ch, CH * ch + CH)
            args = [[ss_ref[ch, h] for h in range(DN_H)]] + [
                [r[rs, ls] for ls in lanes] for r in (u_ref, w_ref, qg_ref, kd_ref, at_ref, db_ref)]
            _, vjp = jax.vjp(functools.partial(_step_math, sub=ch % 2), *args)
            cts = vjp((dss, [do_ref[rs, ls] for ls in lanes]))
            dss = cts[0]
            for o_ref, vals in zip(outs, cts[1:]):
                for ls, val in zip(lanes, vals):
                    o_ref[rs, ls] = val
        for h in range(DN_H):
            ds_ref[h] = dss[h]

    return pl.pallas_call(
        body, grid=(nstep,),
        in_specs=[row] * 6 + [pl.BlockSpec((nch, DN_H, 128, 128), lambda p: (nstep - 1 - p, 0, 0, 0)), row],
        out_specs=[row] * 6, out_shape=[jax.ShapeDtypeStruct((T, DNW), dt) for dt in PREP_DTYPES],
        scratch_shapes=[pltpu.VMEM((DN_H, 128, 128), F32)],
        name=name, compiler_params=_cp())(*prep, states, do)


def _post_math(o, gate, wrow):
    return o * lax.rsqrt(jnp.mean(o * o, axis=-1, keepdims=True) + NORM_EPS) * wrow * _silu(gate)


def _dn_post_fwd(o, z_dn, dn_norm, name):
    tm = 512
    row = pl.BlockSpec((tm, DNW), lambda i: (i, 0))

    def body(o_ref, g_ref, w_ref, y_ref):
        for h in range(DN_H):
            ls = slice(128 * h, 128 * h + 128)
            y_ref[:, ls] = _post_math(o_ref[:, ls], g_ref[:, ls], w_ref[...]).astype(BF)

    return pl.pallas_call(
        body, grid=(T // tm,),
        in_specs=[row, pl.BlockSpec((tm, DNW), lambda i: (i, 3)), pl.BlockSpec((1, 128), lambda i: (0, 0))],
        out_specs=row, out_shape=jax.ShapeDtypeStruct((T, DNW), BF),
        name=name, compiler_params=_cp())(o, z_dn, dn_norm)


def _dn_post_bwd(o, z_dn, dn_norm, dy, name):
    tm = 512
    row = pl.BlockSpec((tm, DNW), lambda i: (i, 0))
    one = pl.BlockSpec((1, 128), lambda i: (0, 0))

    def body(o_ref, g_ref, w_ref, dy_ref, do_ref, dg_ref, dw_ref):
        i = pl.program_id(0)
        dw = jnp.zeros((1, 128), F32)
        for h in range(DN_H):
            ls = slice(128 * h, 128 * h + 128)
            _, vjp = jax.vjp(_post_math, o_ref[:, ls], g_ref[:, ls], w_ref[...])
            do, dg, dwh = vjp(dy_ref[:, ls].astype(F32))
            do_ref[:, ls] = do
            dg_ref[:, ls] = dg.astype(BF)
            dw = dw + dwh

        @pl.when(i == 0)
        def _():
            dw_ref[...] = dw

        @pl.when(i > 0)
        def _():
            dw_ref[...] += dw

    return pl.pallas_call(
        body, grid=(T // tm,),
        in_specs=[row, pl.BlockSpec((tm, DNW), lambda i: (i, 3)), one, pl.BlockSpec((tm, DNW), lambda i: (i, 1))],
        out_specs=[row, row, one],
        out_shape=[jax.ShapeDtypeStruct((T, DNW), F32), jax.ShapeDtypeStruct((T, DNW), BF),
                   jax.ShapeDtypeStruct((1, 128), F32)],
        name=name, compiler_params=_cp())(o, z_dn, dn_norm, dy)


HBM = pl.BlockSpec(memory_space=pltpu.HBM)
SEM = pl.BlockSpec(memory_space=pltpu.SEMAPHORE)
EFFECT = pltpu.SideEffectType.DATAFLOW_SIDE_EFFECTING
N_PEER = N_DEV - 1


ALL_PEERS = (1, 2, 4, 3, 5, 6, 7)
FIRST_HOP = (1, 2, 4, 6)
FORWARDED = (2, 4, 6)


def _peers(x, y, c, ks=ALL_PEERS):
    return [(k, (x ^ (k >> 2), y ^ ((k >> 1) & 1), c ^ (k & 1))) for k in ks]


def _exchange_copy(ins, lands, ssems, rsems, scatter, t, k, pos, me):
    px, py, pc = pos
    src = ins[t].at[4 * px + 2 * py + pc] if scatter else ins[t]
    return pltpu.make_async_remote_copy(
        src_ref=src, dst_ref=lands[t].at[me], send_sem=ssems[t].at[k - 1], recv_sem=rsems[t].at[k - 1],
        device_id=pos, device_id_type=MESH_ID)


def _xstart(bufs, scatter, name, ks=ALL_PEERS):
    nt = len(bufs)
    lands = [lax.empty((N_DEV,) + tuple(b.shape[1:] if scatter else b.shape), b.dtype) for b in bufs]

    def body(*refs):
        ins, lnd = refs[:nt], refs[nt:2 * nt]
        ssems, rsems = refs[2 * nt:3 * nt], refs[3 * nt:4 * nt]
        token = refs[-1]
        x, y, c = lax.axis_index("x"), lax.axis_index("y"), lax.axis_index("c")
        me = 4 * x + 2 * y + c
        for t in range(nt):
            for k, pos in _peers(x, y, c, ks):
                _exchange_copy(ins, lnd, ssems, rsems, scatter, t, k, pos, me).start()
        token[...] = jnp.zeros_like(token)

    both = list(bufs) + lands
    res = pl.pallas_call(
        body, name=name,
        out_shape=[pltpu.SemaphoreType.DMA((N_PEER,))] * (2 * nt)
        + [pltpu.HBM(b.shape, b.dtype) for b in both] + [jax.ShapeDtypeStruct((8, 128), F32)],
        in_specs=[HBM] * (2 * nt),
        out_specs=[SEM] * (2 * nt) + [HBM] * (2 * nt) + [pl.BlockSpec(memory_space=pltpu.VMEM)],
        input_output_aliases={i: 2 * nt + i for i in range(2 * nt)},
        compiler_params=pltpu.CompilerParams(has_side_effects=EFFECT),
    )(*[pltpu.with_memory_space_constraint(b, pltpu.HBM) for b in both])
    return res[:nt], res[nt:2 * nt], res[2 * nt:3 * nt], res[3 * nt:4 * nt], res[-1][0, 0]


def _xwait(ssems, rsems, thrus, lands, scatter, after, name, ks=ALL_PEERS):
    nt = len(lands)

    def body(*refs):
        ins, lnd = refs[:nt], refs[nt:2 * nt]
        ss, rs = refs[2 * nt:3 * nt], refs[3 * nt:4 * nt]
        x, y, c = lax.axis_index("x"), lax.axis_index("y"), lax.axis_index("c")
        me = 4 * x + 2 * y + c
        for t in range(nt):
            for k, pos in _peers(x, y, c, ks):
                cp = _exchange_copy(ins, lnd, ss, rs, scatter, t, k, pos, me)
                cp.wait_send()
                cp.wait_recv()

    both = list(thrus) + list(lands)
    res = pl.pallas_call(
        body, name=name, out_shape=[pltpu.HBM(b.shape, b.dtype) for b in both],
        in_specs=[HBM] * (2 * nt) + [SEM] * (2 * nt) + [ANY], out_specs=[HBM] * (2 * nt),
        input_output_aliases={i: i for i in range(2 * nt)},
        compiler_params=pltpu.CompilerParams(has_side_effects=EFFECT),
    )(*both, *ssems, *rsems, after)
    return res[:nt], res[nt:]


def _forward_copy(lands, ssems, rsems, t, k, pos, sibling):
    px, py, pc = pos
    slot = lands[t].at[4 * px + 2 * py + pc]
    return pltpu.make_async_remote_copy(
        src_ref=slot, dst_ref=slot, send_sem=ssems[t].at[k - 1], recv_sem=rsems[t].at[k - 1],
        device_id=sibling, device_id_type=MESH_ID)


def _fstart(lands, name):
    nt = len(lands)

    def body(*refs):
        lnd = refs[:nt]
        ssems, rsems = refs[nt:2 * nt], refs[2 * nt:3 * nt]
        token = refs[-1]
        x, y, c = lax.axis_index("x"), lax.axis_index("y"), lax.axis_index("c")
        for t in range(nt):
            for k, pos in _peers(x, y, c, FORWARDED):
                _forward_copy(lnd, ssems, rsems, t, k, pos, (x, y, c ^ 1)).start()
        token[...] = jnp.zeros_like(token)

    res = pl.pallas_call(
        body, name=name,
        out_shape=[pltpu.SemaphoreType.DMA((N_PEER,))] * (2 * nt)
        + [pltpu.HBM(b.shape, b.dtype) for b in lands] + [jax.ShapeDtypeStruct((8, 128), F32)],
        in_specs=[HBM] * nt,
        out_specs=[SEM] * (2 * nt) + [HBM] * nt + [pl.BlockSpec(memory_space=pltpu.VMEM)],
        input_output_aliases={i: 2 * nt + i for i in range(nt)},
        compiler_params=pltpu.CompilerParams(has_side_effects=EFFECT),
    )(*[pltpu.with_memory_space_constraint(b, pltpu.HBM) for b in lands])
    return res[:nt], res[nt:2 * nt], res[2 * nt:3 * nt], res[-1][0, 0]


def _fwait(ssems, rsems, lands, after, name):
    nt = len(lands)

    def body(*refs):
        lnd = refs[:nt]
        ss, rs = refs[nt:2 * nt], refs[2 * nt:3 * nt]
        x, y, c = lax.axis_index("x"), lax.axis_index("y"), lax.axis_index("c")
        for t in range(nt):
            for k, pos in _peers(x, y, c, FORWARDED):
                cp = _forward_copy(lnd, ss, rs, t, k, pos, (x, y, c ^ 1))
                cp.wait_send()
                cp.wait_recv()

    return pl.pallas_call(
        body, name=name, out_shape=[pltpu.HBM(b.shape, b.dtype) for b in lands],
        in_specs=[HBM] * nt + [SEM] * (2 * nt) + [ANY], out_specs=[HBM] * nt,
        input_output_aliases={i: i for i in range(nt)},
        compiler_params=pltpu.CompilerParams(has_side_effects=EFFECT),
    )(*lands, *ssems, *rsems, after)


def _adam(recv, sent, me, w, m, v, tr, name):
    _, r, c = w.shape
    c1 = np.float32(1.0 - ADAM_B1 ** ADAM_STEP)
    c2 = np.float32(1.0 - ADAM_B2 ** ADAM_STEP)

    def body(me_ref, r_ref, s_ref, w_ref, m_ref, v_ref, g_ref, d_ref, mo_ref, vo_ref):
        mine = me_ref[0]
        g = jnp.where(mine == 0, s_ref[0], r_ref[0]).astype(F32)
        for s in range(1, N_DEV):
            g = g + jnp.where(mine == s, s_ref[0], r_ref[s]).astype(F32)
        mn = ADAM_B1 * m_ref[0] + (1.0 - ADAM_B1) * g
        vn = ADAM_B2 * v_ref[0] + (1.0 - ADAM_B2) * (g * g)
        g_ref[0] = g
        mo_ref[0] = mn
        vo_ref[0] = vn
        d_ref[0] = -ADAM_LR * ((mn / c1) / (jnp.sqrt(vn / c2) + ADAM_EPS) + ADAM_WD * w_ref[0])

    one = pl.BlockSpec((1, tr, c), lambda i, me_ref: (0, i, 0))
    return pl.pallas_call(
        body,
        grid_spec=pltpu.PrefetchScalarGridSpec(
            num_scalar_prefetch=1, grid=(r // tr,),
            in_specs=[pl.BlockSpec((N_DEV, tr, c), lambda i, me_ref: (0, i, 0)),
                      pl.BlockSpec((1, tr, c), lambda i, me_ref: (me_ref[0], i, 0)), one, one, one],
            out_specs=[one] * 4),
        out_shape=[jax.ShapeDtypeStruct((1, r, c), F32)] * 4,
        name=name, compiler_params=_cp())(me, recv, sent, w, m, v)


def _local_step(x, target, sp, need, ahead, emit):
    g = {}
    x0, h1 = x, _rms_fwd(x, sp["norm_ffn1"], "ffn1_norm")
    act1, saved1 = _ffn_up(h1, lambda kind, a: need("w" + kind + "1", a),
                           lambda kind, a: ahead("w" + kind + "1", a), "ffn1", one_call=True)
    x1, h2 = _mm_nn_resnorm([(act1, saved1[-1])], x0, 0.5, sp["norm_mix"], 512, "ffn1_down_norm",
                            after=ahead("win_a", act1))
    win_a, win_d = need("win_a", h2), need("win_d", h2)
    conv_w8, wout = need("conv_w8", h2), need("wout", h2)
    z_at = _mm_nn(h2, win_a, F32, 1024, 768, "mix_in_attn")
    z_dn = _mm_nn(h2, win_d, F32, 1024, 768, "mix_in_dn")

    conv, qkvn, bg = _dn_pre_fwd(z_dn, conv_w8, sp["alog_row"], sp["dtb_row"], "dn_pre")
    *prep, tinv = _dn_prep_fwd(qkvn, bg, "dn_prep")
    o_dn, states = _dn_scan_fwd(prep, "dn_scan")
    dn_b = _dn_post_fwd(o_dn, z_dn, sp["dn_norm"], "dn_post")
    attn_f, attn_b, lse = _attn_fwd(z_at, "attn_fwd", after=ahead("wg2", dn_b))

    x2, h3 = _mm_nn_resnorm([(attn_b, wout[:AW]), (dn_b, wout[AW:])], x1, 1.0, sp["norm_ffn2"], 512,
                            "mix_out_norm")
    act2, saved2 = _ffn_up(h3, lambda kind, a: need("w" + kind + "2", a),
                           lambda kind, a: ahead("w" + kind + "2", a), "ffn2", one_call=True)

    dx3, dys3, g["norm_final"], loss8 = _down_loss_bwd(x2, act2, saved2[-1], sp["norm_final"], target,
                                                       "ffn2_down_loss")
    dx2, dx2b, g["norm_ffn2"] = _ffn_bwd(
        x2, sp["norm_ffn2"], saved2, dx3, dys3, 1.0,
        lambda kind, dw: emit(kind + "2", {"w" + kind + "2": dw}), "ffn2b")

    zero = emit("wout", {"wout": jnp.concatenate([_mm_tn(attn_b, dx2b, BF, 512, 1024, "mix_out_dw_a"),
                                                  _mm_tn(dn_b, dx2b, BF, 512, 1024, "mix_out_dw_d")], axis=0)})
    dmix = _mm_nt(dx2b, wout, F32, 1024, 1024, "mix_out_dx", after=zero)

    dq, dk, dv = _attn_bwd(z_at, dmix, attn_f, lse, "attn_bwd")

    do_dn, dgate, g["dn_norm"] = _dn_post_bwd(o_dn, z_dn, sp["dn_norm"], dmix, "dn_post_b")
    cts = _dn_scan_bwd(prep, states, do_dn, "dn_scan_b")
    dqn, dkn, dvn, dbg = _dn_prep_bwd(qkvn, bg, tinv, cts, "dn_prep_b")
    dconv, dbd, g["alog_row"], g["dtb_row"] = _dn_pre_bwd(
        conv, z_dn, sp["alog_row"], sp["dtb_row"], dqn, dkn, dvn, dbg, "dn_pre_b")
    draw, dconv_w8 = _dn_conv_bwd(dconv, z_dn, conv_w8, "dn_conv_b")

    pieces = ((dq, win_a, 0), (dk, win_a, 1), (dv, win_a, 2),
              (draw, win_d, 0), (dgate, win_d, 3 * DNW // DNW), (dbd, win_d, BD_BLK))
    zero = emit("win", {"w_in_parts": [_mm_tn(h2, p[0], BF, 512, min(p[0].shape[1], 768), "mix_in_dw%d" % i)
                                       for i, p in enumerate(pieces)], "conv_w8": dconv_w8})
    dx1, dys1, g["norm_mix"] = _mm_rms_bwd(pieces, NT, x1, sp["norm_mix"], dx2, 0.5, 512, "mix_in_dx_dnorm",
                                           after=zero)
    dx0, _, g["norm_ffn1"] = _ffn_bwd(
        x0, sp["norm_ffn1"], saved1, dx1, dys1, 1.0,
        lambda kind, dw: emit(kind + "1", {"w" + kind + "1": dw}), "ffn1b")
    return loss8[0, 0], dx0, g


def _cols_from_shards(gathered):
    n, r, c = gathered.shape
    return jnp.transpose(gathered, (1, 0, 2)).reshape(r, n * c)


def _shards_from_cols(full, dtype):
    r, nc = full.shape
    return jnp.transpose(full.reshape(r, N_DEV, nc // N_DEV), (1, 0, 2)).astype(dtype)


def _lane_row(vec4):
    return jnp.zeros((1, 128), F32).at[:, DN_H:2 * DN_H].set(vec4.astype(F32))


WEIGHT_SOURCES = {"wg1": "gate1", "wu1": "up1", "wd1": "down1", "win_a": "w_in", "win_d": "w_in",
                  "conv_w8": "conv_w", "wout": "w_out", "wg2": "gate2", "wu2": "up2", "wd2": "down2"}
TRANSPOSED = ("gate1", "up1", "gate2", "up2")


def _build_weights(name, gath):
    if name in ("wg1", "wu1", "wd1", "wg2", "wu2", "wd2"):
        return {name: gath[WEIGHT_SOURCES[name]].reshape(F, D)}
    if name in ("win_a", "win_d"):
        w_in = _cols_from_shards(gath["w_in"])
        c0 = 3 * AW + 3 * DNW
        win_d = jnp.concatenate([w_in[:, ZA:c0], w_in[:, c0 + 2 * DN_H:], w_in[:, c0:c0 + 2 * DN_H],
                                 jnp.zeros((D, ZP - IN_COLS), w_in.dtype)], axis=1)
        return {"win_a": w_in[:, :ZA], "win_d": win_d}
    if name == "wout":
        return {name: gath["w_out"].reshape(D, D)}
    conv = _cols_from_shards(gath["conv_w"])
    return {"conv_w8": jnp.concatenate([conv, jnp.zeros((4, 3 * DNW), F32)], axis=0)}


def _small_params(norm_ffn1, norm_mix, norm_ffn2, norm_final, a_log, dt_bias, dn_norm):
    return {"norm_ffn1": norm_ffn1, "norm_mix": norm_mix, "norm_ffn2": norm_ffn2,
            "norm_final": norm_final.reshape(1, D), "alog_row": _lane_row(a_log), "dtb_row": _lane_row(dt_bias),
            "dn_norm": dn_norm}


def _grad_slabs(group, g):
    if group[0] in "gud":
        return {WEIGHT_SOURCES["w" + group]: g["w" + group].reshape(N_DEV, F // N_DEV, D)}
    if group == "wout":
        return {"w_out": g["wout"].reshape(N_DEV, D // N_DEV, D)}
    aq, ak, av, dqkv, gate, bd = g["w_in_parts"]
    g_in = jnp.concatenate([aq, ak, av, dqkv, bd[:, :2 * DN_H], gate], axis=1)
    return {"w_in": _shards_from_cols(g_in, BF), "conv_w": _shards_from_cols(g["conv_w8"][:4], F32)}


SMALL_ROWS = 40


def _small_pack(norm_ffn1, norm_mix, norm_ffn2, norm_final, dn_norm, alog_row, dtb_row, loss=None):
    rows = [a.reshape(8, 128) for a in (norm_ffn1, norm_mix, norm_ffn2, norm_final)]
    loss_row = jnp.zeros((1, 128), F32) if loss is None else jnp.broadcast_to(loss.reshape(1, 1), (1, 128))
    rows += [dn_norm.reshape(1, 128), alog_row, dtb_row, loss_row, jnp.zeros((SMALL_ROWS - 36, 128), F32)]
    return jnp.concatenate(rows, axis=0)


def _small_unpack(pk):
    pk = pk[0]
    return (pk[0:8].reshape(1, D), pk[8:16].reshape(1, D), pk[16:24].reshape(1, D), pk[24:32].reshape(D),
            pk[32:33], pk[33:34, DN_H:2 * DN_H], pk[34:35, DN_H:2 * DN_H])


ADAM_TILE = {"gate1": 176, "up1": 176, "down1": 176, "gate2": 176, "up2": 176, "down2": 176,
             "w_in": 256, "w_out": 128, "conv_w": 4}
BIG = ("gate1", "up1", "down1", "w_in", "w_out", "gate2", "up2", "down2", "conv_w")


def kernel(x, norm_ffn1, ffn1_gate, ffn1_up, ffn1_down, norm_mix, w_in, conv_w, a_log, dt_bias, dn_norm, w_out, norm_ffn2, ffn2_gate, ffn2_up, ffn2_down, norm_final, loss_target, m_norm_ffn1, m_ffn1_gate, m_ffn1_up, m_ffn1_down, m_norm_mix, m_w_in, m_conv_w, m_a_log, m_dt_bias, m_dn_norm, m_w_out, m_norm_ffn2, m_ffn2_gate, m_ffn2_up, m_ffn2_down, m_norm_final, v_norm_ffn1, v_ffn1_gate, v_ffn1_up, v_ffn1_down, v_norm_mix, v_w_in, v_conv_w, v_a_log, v_dt_bias, v_dn_norm, v_w_out, v_norm_ffn2, v_ffn2_gate, v_ffn2_up, v_ffn2_down, v_norm_final):
    w = {"gate1": ffn1_gate, "up1": ffn1_up, "down1": ffn1_down, "w_in": w_in, "w_out": w_out,
         "gate2": ffn2_gate, "up2": ffn2_up, "down2": ffn2_down, "conv_w": conv_w}
    m = {"gate1": m_ffn1_gate, "up1": m_ffn1_up, "down1": m_ffn1_down, "w_in": m_w_in, "w_out": m_w_out,
         "gate2": m_ffn2_gate, "up2": m_ffn2_up, "down2": m_ffn2_down, "conv_w": m_conv_w}
    v = {"gate1": v_ffn1_gate, "up1": v_ffn1_up, "down1": v_ffn1_down, "w_in": v_w_in, "w_out": v_w_out,
         "gate2": v_ffn2_gate, "up2": v_ffn2_up, "down2": v_ffn2_down, "conv_w": v_conv_w}

    me = 4 * lax.axis_index("x") + 2 * lax.axis_index("y") + lax.axis_index("c")
    own_slot = lambda land, mine: lax.dynamic_update_index_in_dim(land, mine, me, 0)

    ag_order = ("gate1", "up1", "down1", "w_in", "conv_w", "w_out", "gate2", "up2", "down2")
    ag_groups = (("gate1",), ("up1",), ("down1",), ("w_in", "conv_w", "w_out"), ("gate2", "up2", "down2"))
    pos = {n: i for i, n in enumerate(ag_order)}

    def shard(n):
        if n == "conv_w":
            return w[n][0]
        return (w[n][0].T if n in TRANSPOSED else w[n][0]).astype(BF)

    ss, rs, thru, land, zero = _xstart([shard(n) for n in ag_order], False, "weights_start", FIRST_HOP)
    gath, built, on_its_way = {}, {}, {}
    group_of = lambda name: [i for i, grp in enumerate(ag_groups) if WEIGHT_SOURCES[name] in grp][0]

    def ahead(name, after):
        gi = group_of(name)
        if WEIGHT_SOURCES[name] in gath or gi in on_its_way:
            return None
        ids = [pos[n] for n in ag_groups[gi]]
        thrus, lands = _xwait([ss[i] for i in ids], [rs[i] for i in ids], [thru[i] for i in ids],
                              [land[i] for i in ids], False, after, "weights_wait%d" % gi, FIRST_HOP)
        fss, frs, lands, token = _fstart(lands, "weights_forward%d" % gi)
        on_its_way[gi] = (thrus, fss, frs, lands)
        return token

    def need(name, after):
        if name not in built:
            if WEIGHT_SOURCES[name] not in gath:
                gi = group_of(name)
                ahead(name, after)
                thrus, fss, frs, lands = on_its_way.pop(gi)
                lands = _fwait(fss, frs, lands, after, "weights_forward_wait%d" % gi)
                for n, t, l in zip(ag_groups[gi], thrus, lands):
                    gath[n] = own_slot(l, t)
            built.update(_build_weights(name, gath))
        return built[name]

    pending = []

    def emit(group, grads):
        slabs = grads if group == "small" else _grad_slabs(group, grads)
        names = list(slabs)
        started = _xstart([slabs[n] for n in names], True, "grads_start_" + group)
        pending.append((group, names) + started[:4])
        return started[4]

    sp = _small_params(norm_ffn1 + zero, norm_mix, norm_ffn2, norm_final, a_log, dt_bias, dn_norm)
    loss_part, dx, g = _local_step(x[0], loss_target[0], sp, need, ahead, emit)
    small = _small_pack(g["norm_ffn1"], g["norm_mix"], g["norm_ffn2"], g["norm_final"], g["dn_norm"],
                        g["alog_row"], g["dtb_row"], loss_part)
    emit("small", {"small": jnp.broadcast_to(small[None], (N_DEV, SMALL_ROWS, 128))})

    pack = lambda a: _small_pack(*a)[None]
    me1 = me.astype(jnp.int32).reshape(1)
    res, after = {}, dx
    for group, names, gss, grs, gthru, gland in pending:
        thrus, lands = _xwait(gss, grs, gthru, gland, True, after, "grads_wait_" + group)
        for n, sent, recv in zip(names, thrus, lands):
            if n == "small":
                res[n] = _adam(
                    recv, sent, me1,
                    pack((norm_ffn1, norm_mix, norm_ffn2, norm_final, dn_norm, _lane_row(a_log), _lane_row(dt_bias))),
                    pack((m_norm_ffn1, m_norm_mix, m_norm_ffn2, m_norm_final, m_dn_norm, _lane_row(m_a_log),
                          _lane_row(m_dt_bias))),
                    pack((v_norm_ffn1, v_norm_mix, v_norm_ffn2, v_norm_final, v_dn_norm, _lane_row(v_a_log),
                          _lane_row(v_dt_bias))),
                    SMALL_ROWS, "adam_small")
            elif n in TRANSPOSED:
                flip = lambda a: jnp.swapaxes(a, 1, 2)
                res[n] = [flip(o) for o in _adam(recv, sent, me1, flip(w[n]), flip(m[n]), flip(v[n]), ADAM_TILE[n],
                                                 "adam_" + n)]
            else:
                res[n] = _adam(recv, sent, me1, w[n], m[n], v[n], ADAM_TILE[n], "adam_" + n)
            after = res[n][0]
    res_s = res["small"]

    loss = res_s[0][0, 35, 0]
    outs = [loss, dx[None]]
    for k in range(4):
        n1, nm, n2, nf, dn, al, dt = _small_unpack(res_s[k])
        big = {n: res[n][k] for n in BIG}
        outs += [n1, big["gate1"], big["up1"], big["down1"], nm, big["w_in"], big["conv_w"], al, dt, dn,
                 big["w_out"], n2, big["gate2"], big["up2"], big["down2"], nf]
    return tuple(outs)
```
